```python
import math
import jax, jax.numpy as jnp
from jax import lax
import numpy as np

D_MODEL = 1024
BATCH = 8
SEQ = 8192
DEPTH = 1

HEAD_DIM = 64
N_ATTN_HEADS = 8
N_DELTA_HEADS = 8
ATTN_WIDTH = N_ATTN_HEADS * HEAD_DIM
DELTA_WIDTH = N_DELTA_HEADS * HEAD_DIM
MIX_WIDTH = ATTN_WIDTH + DELTA_WIDTH
DILATED_BRANCHES = ((128, 1), (512, 4), (2048, 16))
PAD_UNIT = 2048
N_BUCKETS = 32
MAX_DISTANCE = 2048
CONV_WIDTH = 4
CHUNK = 64
D_FF = (8 * D_MODEL + 3 * 256 - 1) // (3 * 256) * 256
IN_WIDTH = 3 * ATTN_WIDTH + 4 * DELTA_WIDTH + 2 * N_DELTA_HEADS
EPS = 1e-6
NEG_INF = -1e30

kernel_name = 'hybrid_dilated_attn_gated_deltanet_block'


def _rmsnorm(x, g):
    x32 = x.astype(jnp.float32)
    y = x32 * lax.rsqrt(jnp.mean(x32 * x32, axis=-1, keepdims=True) + EPS)
    return (y * g.astype(jnp.float32)).astype(x.dtype)


def _l2norm(x):
    return x * lax.rsqrt(jnp.sum(x * x, axis=-1, keepdims=True) + EPS)


def _t5_bucket(distance):
    max_exact = N_BUCKETS // 2
    dist_f = jnp.maximum(distance, 1).astype(jnp.float32)
    large = max_exact + (jnp.log(dist_f / max_exact) / math.log(MAX_DISTANCE / max_exact)
                         * (N_BUCKETS - max_exact)).astype(jnp.int32)
    return jnp.where(distance < max_exact, distance, jnp.minimum(large, N_BUCKETS - 1))


def _dilated_branch(q, k, v, rel_bias, window, dilation):
    b, h, p, dh = q.shape
    band = window // dilation
    length = p // dilation
    n_blocks = length // band

    def to_blocks(t):
        t = t.reshape(b, h, length, dilation, dh).transpose(0, 1, 3, 2, 4)
        return t.reshape(b, h, dilation, n_blocks, band, dh)

    qb, kb, vb = to_blocks(q), to_blocks(k), to_blocks(v)

    def with_prev(t):
        prev = jnp.pad(t, ((0, 0), (0, 0), (0, 0), (1, 0), (0, 0), (0, 0)))[:, :, :, :-1]
        return jnp.concatenate([prev, t], axis=-2)

    kw, vw = with_prev(kb), with_prev(vb)
    qi = jnp.arange(band)[:, None]
    kj = jnp.arange(2 * band)[None, :]
    steps = qi + band - kj
    in_window = (steps >= 0) & (steps <= band)
    not_before_start = (jnp.arange(n_blocks)[:, None, None] > 0) | (kj >= band)[None]
    valid = in_window[None] & not_before_start
    bias = rel_bias.astype(jnp.float32)[_t5_bucket(jnp.maximum(steps, 0) * dilation)]
    bias = bias.transpose(2, 0, 1)
    s = jnp.einsum('bhrnqd,bhrnkd->bhrnqk', qb * dh ** -0.5, kw) + bias[None, :, None, None]
    s = jnp.where(valid, s, NEG_INF)
    m = jnp.max(s, axis=-1, keepdims=True)
    e = jnp.exp(s - m)
    denom = jnp.sum(e, axis=-1, keepdims=True)
    o = jnp.einsum('bhrnqk,bhrnkd->bhrnqd', e, vw) / denom
    lse = (m + jnp.log(denom))[..., 0]
    o = o.reshape(b, h, dilation, length, dh).transpose(0, 1, 3, 2, 4).reshape(b, h, p, dh)
    lse = lse.reshape(b, h, dilation, length).transpose(0, 1, 3, 2).reshape(b, h, p)
    return o, lse


def _dilated_attention(q, k, v, rel_bias):
    b, s, h, dh = q.shape
    p = (s + PAD_UNIT - 1) // PAD_UNIT * PAD_UNIT

    def to_bhpd(t):
        t = jnp.pad(t.astype(jnp.float32), ((0, 0), (0, p - s), (0, 0), (0, 0)))
        return t.transpose(0, 2, 1, 3)

    q, k, v = to_bhpd(q), to_bhpd(k), to_bhpd(v)
    outs, lses = [], []
    for window, dilation in DILATED_BRANCHES:
        o_i, lse_i = _dilated_branch(q, k, v, rel_bias, window, dilation)
        outs.append(o_i)
        lses.append(lse_i)
    w = jax.nn.softmax(jnp.stack(lses), axis=0)
    o = jnp.sum(w[..., None] * jnp.stack(outs), axis=0)
    return o[:, :, :s].transpose(0, 2, 1, 3).reshape(b, s, h * dh)


def _causal_conv(x, w):
    return lax.conv_general_dilated(x, w[:, None, :], window_strides=(1,),
                                    padding=((CONV_WIDTH - 1, 0),),
                                    dimension_numbers=('NWC', 'WIO', 'NWC'),
                                    feature_group_count=x.shape[-1])


def _chunk_gated_delta_rule(q, k, v, g, beta):
    b, s, h, dk = q.shape
    dv = v.shape[-1]
    nc = s // CHUNK

    def chunks(t):
        return t.reshape(b, nc, CHUNK, h, t.shape[-1]).transpose(1, 0, 3, 2, 4)

    qc, kc, vc = chunks(q), chunks(k), chunks(v)
    gcum = jnp.cumsum(g.reshape(b, nc, CHUNK, h).transpose(1, 0, 3, 2), axis=-1)
    bc = beta.reshape(b, nc, CHUNK, h).transpose(1, 0, 3, 2)
    causal = jnp.tril(jnp.ones((CHUNK, CHUNK), dtype=bool))
    strict = jnp.tril(jnp.ones((CHUNK, CHUNK), dtype=bool), k=-1)
    diff = gcum[..., :, None] - gcum[..., None, :]
    decay = jnp.where(causal, jnp.exp(jnp.where(causal, diff, 0.0)), 0.0)
    k_beta = kc * bc[..., None]
    a_mat = jnp.where(strict, jnp.einsum('nbhcd,nbhed->nbhce', k_beta, kc) * decay, 0.0)
    rhs = jnp.concatenate([vc * bc[..., None], k_beta * jnp.exp(gcum)[..., None]], axis=-1)
    sol = lax.linalg.triangular_solve(a_mat + jnp.eye(CHUNK, dtype=a_mat.dtype), rhs,
                                      left_side=True, lower=True, unit_diagonal=True)
    u, w = sol[..., :dv], sol[..., dv:]
    qk = jnp.where(causal, jnp.einsum('nbhcd,nbhed->nbhce', qc, kc) * decay, 0.0)

    def step(state, xs):
        q_i, k_i, u_i, w_i, g_i, qk_i = xs
        v_new = u_i - jnp.einsum('bhck,bhkv->bhcv', w_i, state)
        o_i = (jnp.einsum('bhck,bhkv->bhcv', q_i * jnp.exp(g_i)[..., None], state)
               + jnp.einsum('bhce,bhev->bhcv', qk_i, v_new))
        g_last = g_i[..., -1]
        k_dec = k_i * jnp.exp(g_last[..., None] - g_i)[..., None]
        state = state * jnp.exp(g_last)[..., None, None] + jnp.einsum('bhck,bhcv->bhkv', k_dec, v_new)
        return state, o_i

    state0 = jnp.zeros((b, h, dk, dv), jnp.float32)
    _, o = lax.scan(step, state0, (qc, kc, u, w, gcum, qk))
    return o.transpose(1, 0, 3, 2, 4).reshape(b, s, h, dv)


def _gated_deltanet(q, k, v, z, b_logit, a_logit, conv_w, a_log, dt_bias, norm_g):
    bsz, s, _ = q.shape
    qkv = jax.nn.silu(_causal_conv(jnp.concatenate([q, k, v], axis=-1), conv_w))
    q, k, v = jnp.split(qkv.astype(jnp.float32), 3, axis=-1)
    shp = (bsz, s, N_DELTA_HEADS, HEAD_DIM)
    q = _l2norm(q.reshape(shp)) * HEAD_DIM ** -0.5
    k = _l2norm(k.reshape(shp))
    v = v.reshape(shp)
    beta = jax.nn.sigmoid(b_logit.astype(jnp.float32))
    g = -jnp.exp(a_log.astype(jnp.float32)) * jax.nn.softplus(
        a_logit.astype(jnp.float32) + dt_bias.astype(jnp.float32))
    o = _chunk_gated_delta_rule(q, k, v, g, beta)
    o = _rmsnorm(o, norm_g) * jax.nn.silu(z.astype(jnp.float32).reshape(shp))
    return o.reshape(bsz, s, DELTA_WIDTH).astype(z.dtype)


def _fwd_setup_inputs(seed: int = 0) -> dict:
    key = jax.random.key(seed)
    ks = jax.random.split(key, 20)
    f32 = jnp.float32

    def nrm(k, shape, scale):
        return jax.random.normal(k, shape, f32) * scale

    dt = jnp.exp(jax.random.uniform(ks[9], (DEPTH, N_DELTA_HEADS), f32,
                                    minval=math.log(1e-3), maxval=math.log(1e-1)))
    return {
        'x': nrm(ks[0], (BATCH, SEQ, D_MODEL), 1.0),
        'c': nrm(ks[1], (BATCH, D_MODEL), 1.0),
        'w_ada': nrm(ks[2], (DEPTH, D_MODEL, 6 * D_MODEL), 0.5 * D_MODEL ** -0.5),
        'b_ada': nrm(ks[3], (DEPTH, 6 * D_MODEL), 0.02),
        'norm_attn_g': 1.0 + nrm(ks[4], (DEPTH, D_MODEL), 0.05),
        'w_in': nrm(ks[5], (DEPTH, D_MODEL, IN_WIDTH), D_MODEL ** -0.5),
        'rel_bias': nrm(ks[6], (N_BUCKETS, N_ATTN_HEADS), 0.5),
        'conv_w': nrm(ks[7], (DEPTH, CONV_WIDTH, 3 * DELTA_WIDTH), CONV_WIDTH ** -0.5),
        'a_log': jnp.log(jax.random.uniform(ks[8], (DEPTH, N_DELTA_HEADS), f32, minval=1.0, maxval=16.0)),
        'dt_bias': dt + jnp.log(-jnp.expm1(-dt)),
        'delta_norm_g': 1.0 + nrm(ks[10], (DEPTH, HEAD_DIM), 0.05),
        'w_out': nrm(ks[11], (DEPTH, MIX_WIDTH, D_MODEL), MIX_WIDTH ** -0.5),
        'norm_ffn_g': 1.0 + nrm(ks[12], (DEPTH, D_MODEL), 0.05),
        'w_gate': nrm(ks[13], (DEPTH, D_MODEL, D_FF), D_MODEL ** -0.5),
        'w_up': nrm(ks[14], (DEPTH, D_MODEL, D_FF), D_MODEL ** -0.5),
        'w_down': nrm(ks[15], (DEPTH, D_FF, D_MODEL), D_FF ** -0.5),
        'final_norm_g': 1.0 + nrm(ks[16], (D_MODEL,), 0.05),
    }


def _fwd_reference(x, c, w_ada, b_ada, norm_attn_g, w_in, rel_bias, conv_w, a_log, dt_bias,
              delta_norm_g, w_out, norm_ffn_g, w_gate, w_up, w_down, final_norm_g):
    bsz, s, _ = x.shape
    split_points = np.cumsum([ATTN_WIDTH] * 3 + [DELTA_WIDTH] * 4 + [N_DELTA_HEADS])
    c_act = jax.nn.silu(c)
    for l in range(DEPTH):
        mod = c_act @ w_ada[l] + b_ada[l]
        sh1, sc1, g1, sh2, sc2, g2 = [m[:, None, :] for m in jnp.split(mod, 6, axis=-1)]
        h = _rmsnorm(x, norm_attn_g[l]) * (1.0 + sc1) + sh1
        proj = h @ w_in[l]
        q_a, k_a, v_a, q_d, k_d, v_d, z_d, b_d, a_d = jnp.split(proj, split_points, axis=-1)
        hs = (bsz, s, N_ATTN_HEADS, HEAD_DIM)
        y_attn = _dilated_attention(q_a.reshape(hs), k_a.reshape(hs), v_a.reshape(hs), rel_bias).astype(x.dtype)
        y_delta = _gated_deltanet(q_d, k_d, v_d, z_d, b_d, a_d, conv_w[l], a_log[l], dt_bias[l], delta_norm_g[l])
        y = jnp.concatenate([y_attn, y_delta], axis=-1) @ w_out[l]
        x = x + g1 * y
        h = _rmsnorm(x, norm_ffn_g[l]) * (1.0 + sc2) + sh2
        y = (jax.nn.silu(h @ w_gate[l]) * (h @ w_up[l])) @ w_down[l]
        x = x + g2 * y
    return _rmsnorm(x, final_norm_g)


import jax as _jax
import jax.numpy as _jnp

TWIN_FORMAT = 'train_step'
FWD_PARAMS = ['x', 'c', 'w_ada', 'b_ada', 'norm_attn_g', 'w_in', 'rel_bias', 'conv_w', 'a_log', 'dt_bias', 'delta_norm_g', 'w_out', 'norm_ffn_g', 'w_gate', 'w_up', 'w_down', 'final_norm_g']
TWIN_WEIGHTS = ['w_ada', 'b_ada', 'norm_attn_g', 'w_in', 'rel_bias', 'conv_w', 'a_log', 'dt_bias', 'delta_norm_g', 'w_out', 'norm_ffn_g', 'w_gate', 'w_up', 'w_down', 'final_norm_g']
TWIN_DIFF_INPUT = 'x'
TWIN_INPUTS = ['x', 'c', 'w_ada', 'b_ada', 'norm_attn_g', 'w_in', 'rel_bias', 'conv_w', 'a_log', 'dt_bias', 'delta_norm_g', 'w_out', 'norm_ffn_g', 'w_gate', 'w_up', 'w_down', 'final_norm_g', 'loss_target', 'm_w_ada', 'm_b_ada', 'm_norm_attn_g', 'm_w_in', 'm_rel_bias', 'm_conv_w', 'm_a_log', 'm_dt_bias', 'm_delta_norm_g', 'm_w_out', 'm_norm_ffn_g', 'm_w_gate', 'm_w_up', 'm_w_down', 'm_final_norm_g', 'v_w_ada', 'v_b_ada', 'v_norm_attn_g', 'v_w_in', 'v_rel_bias', 'v_conv_w', 'v_a_log', 'v_dt_bias', 'v_delta_norm_g', 'v_w_out', 'v_norm_ffn_g', 'v_w_gate', 'v_w_up', 'v_w_down', 'v_final_norm_g']
TWIN_OUTPUTS = ['loss', 'grad_x', 'grad_w_ada', 'grad_b_ada', 'grad_norm_attn_g', 'grad_w_in', 'grad_rel_bias', 'grad_conv_w', 'grad_a_log', 'grad_dt_bias', 'grad_delta_norm_g', 'grad_w_out', 'grad_norm_ffn_g', 'grad_w_gate', 'grad_w_up', 'grad_w_down', 'grad_final_norm_g', 'delta_w_ada', 'delta_b_ada', 'delta_norm_attn_g', 'delta_w_in', 'delta_rel_bias', 'delta_conv_w', 'delta_a_log', 'delta_dt_bias', 'delta_delta_norm_g', 'delta_w_out', 'delta_norm_ffn_g', 'delta_w_gate', 'delta_w_up', 'delta_w_down', 'delta_final_norm_g', 'new_m_w_ada', 'new_m_b_ada', 'new_m_norm_attn_g', 'new_m_w_in', 'new_m_rel_bias', 'new_m_conv_w', 'new_m_a_log', 'new_m_dt_bias', 'new_m_delta_norm_g', 'new_m_w_out', 'new_m_norm_ffn_g', 'new_m_w_gate', 'new_m_w_up', 'new_m_w_down', 'new_m_final_norm_g', 'new_v_w_ada', 'new_v_b_ada', 'new_v_norm_attn_g', 'new_v_w_in', 'new_v_rel_bias', 'new_v_conv_w', 'new_v_a_log', 'new_v_dt_bias', 'new_v_delta_norm_g', 'new_v_w_out', 'new_v_norm_ffn_g', 'new_v_w_gate', 'new_v_w_up', 'new_v_w_down', 'new_v_final_norm_g']
TWIN_LEAF_KINDS = {'loss': 'loss', 'grad_x': 'grad_x', 'grad_w_ada': 'grad_w', 'grad_b_ada': 'grad_w', 'grad_norm_attn_g': 'grad_w', 'grad_w_in': 'grad_w', 'grad_rel_bias': 'grad_w', 'grad_conv_w': 'grad_w', 'grad_a_log': 'grad_w', 'grad_dt_bias': 'grad_w', 'grad_delta_norm_g': 'grad_w', 'grad_w_out': 'grad_w', 'grad_norm_ffn_g': 'grad_w', 'grad_w_gate': 'grad_w', 'grad_w_up': 'grad_w', 'grad_w_down': 'grad_w', 'grad_final_norm_g': 'grad_w', 'delta_w_ada': 'delta_w', 'delta_b_ada': 'delta_w', 'delta_norm_attn_g': 'delta_w', 'delta_w_in': 'delta_w', 'delta_rel_bias': 'delta_w', 'delta_conv_w': 'delta_w', 'delta_a_log': 'delta_w', 'delta_dt_bias': 'delta_w', 'delta_delta_norm_g': 'delta_w', 'delta_w_out': 'delta_w', 'delta_norm_ffn_g': 'delta_w', 'delta_w_gate': 'delta_w', 'delta_w_up': 'delta_w', 'delta_w_down': 'delta_w', 'delta_final_norm_g': 'delta_w', 'new_m_w_ada': 'new_m', 'new_m_b_ada': 'new_m', 'new_m_norm_attn_g': 'new_m', 'new_m_w_in': 'new_m', 'new_m_rel_bias': 'new_m', 'new_m_conv_w': 'new_m', 'new_m_a_log': 'new_m', 'new_m_dt_bias': 'new_m', 'new_m_delta_norm_g': 'new_m', 'new_m_w_out': 'new_m', 'new_m_norm_ffn_g': 'new_m', 'new_m_w_gate': 'new_m', 'new_m_w_up': 'new_m', 'new_m_w_down': 'new_m', 'new_m_final_norm_g': 'new_m', 'new_v_w_ada': 'new_v', 'new_v_b_ada': 'new_v', 'new_v_norm_attn_g': 'new_v', 'new_v_w_in': 'new_v', 'new_v_rel_bias': 'new_v', 'new_v_conv_w': 'new_v', 'new_v_a_log': 'new_v', 'new_v_dt_bias': 'new_v', 'new_v_delta_norm_g': 'new_v', 'new_v_w_out': 'new_v', 'new_v_norm_ffn_g': 'new_v', 'new_v_w_gate': 'new_v', 'new_v_w_up': 'new_v', 'new_v_w_down': 'new_v', 'new_v_final_norm_g': 'new_v'}


def _forward(args):
    return _fwd_reference(*[args[k] for k in FWD_PARAMS])


def _output_shape():
    out = _jax.eval_shape(lambda: _forward(_fwd_setup_inputs(0)))
    return out.shape, out.dtype

N_MICROBATCH = 1
ADAM_LR = 0.001
ADAM_B1 = 0.9
ADAM_B2 = 0.999
ADAM_EPS = 1e-08
ADAM_WD = 0.01
ADAM_STEP = 10
PER_EXAMPLE_BATCH_AXIS = {'x': 0, 'c': 0, 'loss_target': 0}
SHARED_INPUTS = []
_WEIGHT_DTYPES = {'w_ada': _jnp.float32, 'b_ada': _jnp.float32, 'norm_attn_g': _jnp.float32, 'w_in': _jnp.float32, 'rel_bias': _jnp.float32, 'conv_w': _jnp.float32, 'a_log': _jnp.float32, 'dt_bias': _jnp.float32, 'delta_norm_g': _jnp.float32, 'w_out': _jnp.float32, 'norm_ffn_g': _jnp.float32, 'w_gate': _jnp.float32, 'w_up': _jnp.float32, 'w_down': _jnp.float32, 'final_norm_g': _jnp.float32}
MOMENT_SCALE = {'w_ada': 8.706285e-02, 'b_ada': 1.539242e-01, 'norm_attn_g': 7.766429e-02, 'w_in': 4.065255e-02, 'rel_bias': 2.354831e-02, 'conv_w': 4.814686e-02, 'a_log': 9.361540e-01, 'dt_bias': 8.533934e-01, 'delta_norm_g': 1.966711e-01, 'w_out': 4.328599e-02, 'norm_ffn_g': 7.578762e-02, 'w_gate': 3.404707e-02, 'w_up': 3.302567e-02, 'w_down': 5.457125e-02, 'final_norm_g': 6.413584e+01}


def _to_microbatches(a, axis):
    t = _jnp.moveaxis(a, axis, 0)
    t = t.reshape((N_MICROBATCH, t.shape[0] // N_MICROBATCH) + t.shape[1:])
    return _jnp.moveaxis(t, 1, axis + 1)


def setup_inputs(seed: int = 0) -> dict:
    inp = _fwd_setup_inputs(seed)
    key = _jax.random.fold_in(_jax.random.key(seed), 7919)
    shape, _ = _output_shape()
    out = dict(inp)
    out["loss_target"] = _jax.random.normal(_jax.random.fold_in(key, 0), shape, _jnp.float32)
    for i, name in enumerate(TWIN_WEIGHTS):
        w = inp[name].astype(_jnp.float32)
        if MOMENT_SCALE is None:
            s = _jnp.sqrt(_jnp.mean(_jnp.square(w)) + 1e-30)
        else:
            s = MOMENT_SCALE[name]
        km, kv = _jax.random.split(_jax.random.fold_in(key, i + 1))
        out[name] = w
        out["m_" + name] = s * _jax.random.normal(km, w.shape, _jnp.float32)
        out["v_" + name] = (s * s) * _jax.random.uniform(kv, w.shape, _jnp.float32, 0.5, 1.5)
    if N_MICROBATCH > 1:
        for name, axis in PER_EXAMPLE_BATCH_AXIS.items():
            out[name] = _to_microbatches(out[name], axis)
    return {'x': out['x'], 'c': out['c'], 'w_ada': out['w_ada'], 'b_ada': out['b_ada'], 'norm_attn_g': out['norm_attn_g'], 'w_in': out['w_in'], 'rel_bias': out['rel_bias'], 'conv_w': out['conv_w'], 'a_log': out['a_log'], 'dt_bias': out['dt_bias'], 'delta_norm_g': out['delta_norm_g'], 'w_out': out['w_out'], 'norm_ffn_g': out['norm_ffn_g'], 'w_gate': out['w_gate'], 'w_up': out['w_up'], 'w_down': out['w_down'], 'final_norm_g': out['final_norm_g'], 'loss_target': out['loss_target'], 'm_w_ada': out['m_w_ada'], 'm_b_ada': out['m_b_ada'], 'm_norm_attn_g': out['m_norm_attn_g'], 'm_w_in': out['m_w_in'], 'm_rel_bias': out['m_rel_bias'], 'm_conv_w': out['m_conv_w'], 'm_a_log': out['m_a_log'], 'm_dt_bias': out['m_dt_bias'], 'm_delta_norm_g': out['m_delta_norm_g'], 'm_w_out': out['m_w_out'], 'm_norm_ffn_g': out['m_norm_ffn_g'], 'm_w_gate': out['m_w_gate'], 'm_w_up': out['m_w_up'], 'm_w_down': out['m_w_down'], 'm_final_norm_g': out['m_final_norm_g'], 'v_w_ada': out['v_w_ada'], 'v_b_ada': out['v_b_ada'], 'v_norm_attn_g': out['v_norm_attn_g'], 'v_w_in': out['v_w_in'], 'v_rel_bias': out['v_rel_bias'], 'v_conv_w': out['v_conv_w'], 'v_a_log': out['v_a_log'], 'v_dt_bias': out['v_dt_bias'], 'v_delta_norm_g': out['v_delta_norm_g'], 'v_w_out': out['v_w_out'], 'v_norm_ffn_g': out['v_norm_ffn_g'], 'v_w_gate': out['v_w_gate'], 'v_w_up': out['v_w_up'], 'v_w_down': out['v_w_down'], 'v_final_norm_g': out['v_final_norm_g']}


def _loss(weights, diff, rest, loss_target):
    with _jax.named_scope("forward"):
        args = {**rest, TWIN_DIFF_INPUT: diff, **{k: w.astype(_WEIGHT_DTYPES[k]) for k, w in weights.items()}}
        y = _forward(args)
    with _jax.named_scope("loss_head"):
        err = _jnp.square(y.astype(_jnp.float32) - loss_target)
        return 0.5 * _jnp.sum(_jnp.mean(err, axis=-1)) if err.ndim else 0.5 * err


def _adamw(w, g, m, v):
    m = ADAM_B1 * m + (1.0 - ADAM_B1) * g
    v = ADAM_B2 * v + (1.0 - ADAM_B2) * _jnp.square(g)
    m_hat = m / (1.0 - ADAM_B1 ** ADAM_STEP)
    v_hat = v / (1.0 - ADAM_B2 ** ADAM_STEP)
    delta = -ADAM_LR * (m_hat / (_jnp.sqrt(v_hat) + ADAM_EPS) + ADAM_WD * w)
    return delta, m, v


def reference(x, c, w_ada, b_ada, norm_attn_g, w_in, rel_bias, conv_w, a_log, dt_bias, delta_norm_g, w_out, norm_ffn_g, w_gate, w_up, w_down, final_norm_g, loss_target, m_w_ada, m_b_ada, m_norm_attn_g, m_w_in, m_rel_bias, m_conv_w, m_a_log, m_dt_bias, m_delta_norm_g, m_w_out, m_norm_ffn_g, m_w_gate, m_w_up, m_w_down, m_final_norm_g, v_w_ada, v_b_ada, v_norm_attn_g, v_w_in, v_rel_bias, v_conv_w, v_a_log, v_dt_bias, v_delta_norm_g, v_w_out, v_norm_ffn_g, v_w_gate, v_w_up, v_w_down, v_final_norm_g):
    given = dict(x=x, c=c, w_ada=w_ada, b_ada=b_ada, norm_attn_g=norm_attn_g, w_in=w_in, rel_bias=rel_bias, conv_w=conv_w, a_log=a_log, dt_bias=dt_bias, delta_norm_g=delta_norm_g, w_out=w_out, norm_ffn_g=norm_ffn_g, w_gate=w_gate, w_up=w_up, w_down=w_down, final_norm_g=final_norm_g, loss_target=loss_target, m_w_ada=m_w_ada, m_b_ada=m_b_ada, m_norm_attn_g=m_norm_attn_g, m_w_in=m_w_in, m_rel_bias=m_rel_bias, m_conv_w=m_conv_w, m_a_log=m_a_log, m_dt_bias=m_dt_bias, m_delta_norm_g=m_delta_norm_g, m_w_out=m_w_out, m_norm_ffn_g=m_norm_ffn_g, m_w_gate=m_w_gate, m_w_up=m_w_up, m_w_down=m_w_down, m_final_norm_g=m_final_norm_g, v_w_ada=v_w_ada, v_b_ada=v_b_ada, v_norm_attn_g=v_norm_attn_g, v_w_in=v_w_in, v_rel_bias=v_rel_bias, v_conv_w=v_conv_w, v_a_log=v_a_log, v_dt_bias=v_dt_bias, v_delta_norm_g=v_delta_norm_g, v_w_out=v_w_out, v_norm_ffn_g=v_norm_ffn_g, v_w_gate=v_w_gate, v_w_up=v_w_up, v_w_down=v_w_down, v_final_norm_g=v_final_norm_g)
    weights = {n: given[n] for n in TWIN_WEIGHTS}
    shared = {n: given[n] for n in SHARED_INPUTS}
    per_example = {n: given[n] for n in ['x', 'c']}
    grad_fn = _jax.value_and_grad(_loss, argnums=(0, 1))

    def one_microbatch(ex, loss_target):
        ex = dict(ex)
        diff = ex.pop(TWIN_DIFF_INPUT)
        return grad_fn(weights, diff, {**shared, **ex}, loss_target)

    if N_MICROBATCH == 1:
        loss, (grad_w, grad_x) = one_microbatch(per_example, given["loss_target"])
    else:
        def body(carry, xs):
            loss_sum, grad_sum = carry
            l_k, (gw_k, gx_k) = one_microbatch(xs[0], xs[1])
            with _jax.named_scope("update"):
                return (loss_sum + l_k, _jax.tree.map(_jnp.add, grad_sum, gw_k)), gx_k

        init = (_jnp.zeros((), _jnp.float32), _jax.tree.map(_jnp.zeros_like, weights))
        (loss, grad_w), grad_x = _jax.lax.scan(body, init, (per_example, given["loss_target"]))
    with _jax.named_scope("update"):
        delta_w, new_m, new_v = {}, {}, {}
        for n in TWIN_WEIGHTS:
            delta_w[n], new_m[n], new_v[n] = _adamw(weights[n], grad_w[n], given["m_" + n], given["v_" + n])
    return (loss, grad_x, *[grad_w[n] for n in TWIN_WEIGHTS], *[delta_w[n] for n in TWIN_WEIGHTS],
            *[new_m[n] for n in TWIN_WEIGHTS], *[new_v[n] for n in TWIN_WEIGHTS])
```

```python
import functools
import math

import jax
import jax.numpy as jnp
from jax import lax
from jax.experimental import pallas as pl
from jax.experimental.pallas import tpu as pltpu

f32 = jnp.float32
bf16 = jnp.bfloat16

D_MODEL = 1024
HEAD_DIM = 64
N_HEADS = 8
GROUP_W = 512
IN_WIDTH = 3600
IN_PAD = 3840
D_FF = 2816
EPS = 1e-6
NEG_INF = -1e30
BAND = 128
PAD_UNIT = 2048
DILATIONS = (1, 4, 16)
N_BUCKETS = 32
MAX_DISTANCE = 2048
CONV_WIDTH = 4
CHUNK = 64
N_DEV = 8
VMEM_LIMIT = 56 * 1024 * 1024

ADAM_LR, ADAM_B1, ADAM_B2, ADAM_EPS, ADAM_WD, ADAM_STEP = 0.001, 0.9, 0.999, 1e-08, 0.01, 10


def _cparams(sem):
    return pltpu.CompilerParams(dimension_semantics=sem, vmem_limit_bytes=VMEM_LIMIT)


def _mm(a, b, mode, out_dtype, tm, tn, tk, name):
    if mode == "nn":
        (m, k), (_, n) = a.shape, b.shape
        a_spec = pl.BlockSpec((tm, tk), lambda j, i, kk: (i, kk))
        b_spec = pl.BlockSpec((tk, tn), lambda j, i, kk: (kk, j))
        dims = (((1,), (0,)), ((), ()))
    elif mode == "nt":
        (m, k), (n, _) = a.shape, b.shape
        a_spec = pl.BlockSpec((tm, tk), lambda j, i, kk: (i, kk))
        b_spec = pl.BlockSpec((tn, tk), lambda j, i, kk: (j, kk))
        dims = (((1,), (1,)), ((), ()))
    else:
        (k, m), (_, n) = a.shape, b.shape
        a_spec = pl.BlockSpec((tk, tm), lambda j, i, kk: (kk, i))
        b_spec = pl.BlockSpec((tk, tn), lambda j, i, kk: (kk, j))
        dims = (((0,), (0,)), ((), ()))
    assert m % tm == 0 and n % tn == 0 and k % tk == 0, (name, m, n, k, tm, tn, tk)
    nk = k // tk

    def body(a_ref, b_ref, o_ref, acc_ref):
        kk = pl.program_id(2)

        @pl.when(kk == 0)
        def _():
            acc_ref[...] = jnp.zeros_like(acc_ref)

        acc_ref[...] += lax.dot_general(a_ref[...].astype(bf16), b_ref[...].astype(bf16), dims,
                                        preferred_element_type=f32)

        @pl.when(kk == nk - 1)
        def _():
            o_ref[...] = acc_ref[...].astype(o_ref.dtype)

    return pl.pallas_call(
        body, name=name,
        grid=(n // tn, m // tm, nk),
        in_specs=[a_spec, b_spec],
        out_specs=pl.BlockSpec((tm, tn), lambda j, i, kk: (i, j)),
        out_shape=jax.ShapeDtypeStruct((m, n), out_dtype),
        scratch_shapes=[pltpu.VMEM((tm, tn), f32)],
        compiler_params=_cparams(("parallel", "parallel", "arbitrary")),
    )(a, b)


TOK_TILE = 512


def _row_spec(width, tile=TOK_TILE):
    return pl.BlockSpec((tile, width), lambda i: (i, 0))


def _vec_spec(width, rows=1):
    return pl.BlockSpec((rows, width), lambda i: (0, 0))


def _ln_mod_fwd(x, gain, sc, sh, name):
    s, d = x.shape

    def body(x_ref, g_ref, sc_ref, sh_ref, h_ref):
        xv = x_ref[...]
        rstd = lax.rsqrt(jnp.mean(xv * xv, axis=-1, keepdims=True) + EPS)
        h = (xv * rstd) * g_ref[...] * (1.0 + sc_ref[...]) + sh_ref[...]
        h_ref[...] = h.astype(bf16)

    return pl.pallas_call(
        body, name=name, grid=(s // TOK_TILE,),
        in_specs=[_row_spec(d), _vec_spec(d), _vec_spec(d), _vec_spec(d)],
        out_specs=_row_spec(d),
        out_shape=jax.ShapeDtypeStruct((s, d), bf16),
        compiler_params=_cparams(("parallel",)),
    )(x, gain, sc, sh)


def _resid_ln_mod_fwd(x, y, gate, gain, sc, sh, name):
    s, d = x.shape

    def body(x_ref, y_ref, gt_ref, g_ref, sc_ref, sh_ref, x1_ref, h_ref):
        x1 = x_ref[...] + gt_ref[...] * y_ref[...]
        x1_ref[...] = x1
        rstd = lax.rsqrt(jnp.mean(x1 * x1, axis=-1, keepdims=True) + EPS)
        h = (x1 * rstd) * g_ref[...] * (1.0 + sc_ref[...]) + sh_ref[...]
        h_ref[...] = h.astype(bf16)

    return pl.pallas_call(
        body, name=name, grid=(s // TOK_TILE,),
        in_specs=[_row_spec(d), _row_spec(d)] + [_vec_spec(d)] * 4,
        out_specs=[_row_spec(d), _row_spec(d)],
        out_shape=[jax.ShapeDtypeStruct((s, d), f32), jax.ShapeDtypeStruct((s, d), bf16)],
        compiler_params=_cparams(("parallel",)),
    )(x, y, gate, gain, sc, sh)


def _swiglu_fwd(gu, name):
    s = gu.shape[0]

    def body(g_ref, u_ref, a_ref):
        g = g_ref[...]
        a_ref[...] = (g * jax.nn.sigmoid(g) * u_ref[...]).astype(bf16)

    tile = 256
    return pl.pallas_call(
        body, name=name, grid=(s // tile,),
        in_specs=[pl.BlockSpec((tile, D_FF), lambda i: (i, 0)), pl.BlockSpec((tile, D_FF), lambda i: (i, 1))],
        out_specs=pl.BlockSpec((tile, D_FF), lambda i: (i, 0)),
        out_shape=jax.ShapeDtypeStruct((s, D_FF), bf16),
        compiler_params=_cparams(("parallel",)),
    )(gu, gu)


def _swiglu_bwd(gu, dact, name):
    s = gu.shape[0]

    def body(g_ref, u_ref, da_ref, o_ref):
        g = g_ref[...]
        sg = jax.nn.sigmoid(g)
        da = da_ref[...]
        o_ref[:, D_FF:] = (da * g * sg).astype(bf16)
        o_ref[:, :D_FF] = (da * u_ref[...] * sg * (1.0 + g * (1.0 - sg))).astype(bf16)

    tile = 256
    return pl.pallas_call(
        body, name=name, grid=(s // tile,),
        in_specs=[pl.BlockSpec((tile, D_FF), lambda i: (i, 0)), pl.BlockSpec((tile, D_FF), lambda i: (i, 1)),
                  pl.BlockSpec((tile, D_FF), lambda i: (i, 0))],
        out_specs=pl.BlockSpec((tile, 2 * D_FF), lambda i: (i, 0)),
        out_shape=jax.ShapeDtypeStruct((s, 2 * D_FF), bf16),
        compiler_params=_cparams(("parallel",)),
    )(gu, gu, dact)


def _acc_spec(width):
    return pl.BlockSpec((1, width), lambda i: (0, 0))


def _final_loss_bwd(x1, y2, gate2, final_g, target, name):
    s, d = x1.shape

    def body(x1_ref, y2_ref, gt_ref, fg_ref, tg_ref, dx2_ref, dy2_ref, loss_ref, dfg_ref, dgt_ref):
        @pl.when(pl.program_id(0) == 0)
        def _():
            loss_ref[...] = jnp.zeros_like(loss_ref)
            dfg_ref[...] = jnp.zeros_like(dfg_ref)
            dgt_ref[...] = jnp.zeros_like(dgt_ref)

        y2 = y2_ref[...]
        gt = gt_ref[...]
        fg = fg_ref[...]
        x2 = x1_ref[...] + gt * y2
        rstd = lax.rsqrt(jnp.mean(x2 * x2, axis=-1, keepdims=True) + EPS)
        xn = x2 * rstd
        err = xn * fg - tg_ref[...]
        row = jnp.sum(err * err, axis=-1, keepdims=True) * (0.5 / d)
        loss_ref[...] += jnp.sum(row, axis=0, keepdims=True) + jnp.zeros_like(loss_ref)
        dout = err * (1.0 / d)
        dfg_ref[...] += jnp.sum(dout * xn, axis=0, keepdims=True)
        dxn = dout * fg
        dx2 = rstd * (dxn - xn * jnp.mean(dxn * xn, axis=-1, keepdims=True))
        dx2_ref[...] = dx2
        dgt_ref[...] += jnp.sum(dx2 * y2, axis=0, keepdims=True)
        dy2_ref[...] = (gt * dx2).astype(bf16)

    return pl.pallas_call(
        body, name=name, grid=(s // TOK_TILE,),
        in_specs=[_row_spec(d), _row_spec(d), _vec_spec(d), _vec_spec(d), _row_spec(d)],
        out_specs=[_row_spec(d), _row_spec(d), _acc_spec(128), _acc_spec(d), _acc_spec(d)],
        out_shape=[jax.ShapeDtypeStruct((s, d), f32), jax.ShapeDtypeStruct((s, d), bf16),
                   jax.ShapeDtypeStruct((1, 128), f32), jax.ShapeDtypeStruct((1, d), f32),
                   jax.ShapeDtypeStruct((1, d), f32)],
        compiler_params=_cparams(("arbitrary",)),
    )(x1, y2, gate2, final_g, target)


def _ln_mod_bwd(xin, gain, sc, dh, dres, name, gate=None, y=None):
    s, d = xin.shape
    with_gate = gate is not None

    def body(*refs):
        if with_gate:
            (x_ref, g_ref, sc_ref, dh_ref, dr_ref, gt_ref, y_ref,
             dx_ref, dsh_ref, dsc_ref, dg_ref, dy_ref, dgt_ref) = refs
        else:
            x_ref, g_ref, sc_ref, dh_ref, dr_ref, dx_ref, dsh_ref, dsc_ref, dg_ref = refs

        @pl.when(pl.program_id(0) == 0)
        def _():
            dsh_ref[...] = jnp.zeros_like(dsh_ref)
            dsc_ref[...] = jnp.zeros_like(dsc_ref)
            dg_ref[...] = jnp.zeros_like(dg_ref)
            if with_gate:
                dgt_ref[...] = jnp.zeros_like(dgt_ref)

        xv = x_ref[...]
        g = g_ref[...]
        sc1 = 1.0 + sc_ref[...]
        dh = dh_ref[...]
        rstd = lax.rsqrt(jnp.mean(xv * xv, axis=-1, keepdims=True) + EPS)
        xn = xv * rstd
        dsh_ref[...] += jnp.sum(dh, axis=0, keepdims=True)
        dsc_ref[...] += jnp.sum(dh * (xn * g), axis=0, keepdims=True)
        dg_ref[...] += jnp.sum(dh * sc1 * xn, axis=0, keepdims=True)
        dxn = dh * sc1 * g
        dx = dr_ref[...] + rstd * (dxn - xn * jnp.mean(dxn * xn, axis=-1, keepdims=True))
        dx_ref[...] = dx
        if with_gate:
            dgt_ref[...] += jnp.sum(dx * y_ref[...], axis=0, keepdims=True)
            dy_ref[...] = (gt_ref[...] * dx).astype(bf16)

    in_specs = [_row_spec(d), _vec_spec(d), _vec_spec(d), _row_spec(d), _row_spec(d)]
    out_specs = [_row_spec(d), _acc_spec(d), _acc_spec(d), _acc_spec(d)]
    out_shape = [jax.ShapeDtypeStruct((s, d), f32)] + [jax.ShapeDtypeStruct((1, d), f32)] * 3
    args = [xin, gain, sc, dh, dres]
    if with_gate:
        in_specs += [_vec_spec(d), _row_spec(d)]
        out_specs += [_row_spec(d), _acc_spec(d)]
        out_shape += [jax.ShapeDtypeStruct((s, d), bf16), jax.ShapeDtypeStruct((1, d), f32)]
        args += [gate, y]
    return pl.pallas_call(
        body, name=name, grid=(s // TOK_TILE,),
        in_specs=in_specs, out_specs=out_specs, out_shape=out_shape,
        compiler_params=_cparams(("arbitrary",)),
    )(*args)


def _bucket_tables():
    import numpy as np
    qi = np.arange(BAND)[:, None]
    kj = np.arange(2 * BAND)[None, :]
    steps = qi + BAND - kj
    max_exact = N_BUCKETS // 2
    out = []
    for d in DILATIONS:
        dist = np.maximum(steps, 0) * d
        dist_f = np.maximum(dist, 1).astype(np.float32)
        large = max_exact + (np.log(dist_f / np.float32(max_exact)) / np.float32(math.log(MAX_DISTANCE / max_exact))
                             * np.float32(N_BUCKETS - max_exact)).astype(np.int32)
        out.append(np.where(dist < max_exact, dist, np.minimum(large, N_BUCKETS - 1)))
    return jnp.asarray(np.stack(out).astype(np.int32))


def _bias_tables(rel_bias, idx):
    def body(idx_ref, rb_ref, o_ref):
        h = pl.program_id(1)
        idxv = idx_ref[0]
        acc = jnp.zeros((BAND, 2 * BAND), f32)
        for b in range(N_BUCKETS):
            acc = jnp.where(idxv == b, rb_ref[b, h], acc)
        o_ref[0, 0] = acc

    return pl.pallas_call(
        body, name="attn_bias_tables", grid=(3, N_HEADS),
        in_specs=[pl.BlockSpec((1, BAND, 2 * BAND), lambda br, h: (br, 0, 0)),
                  pl.BlockSpec(memory_space=pltpu.SMEM)],
        out_specs=pl.BlockSpec((1, 1, BAND, 2 * BAND), lambda br, h: (br, h, 0, 0)),
        out_shape=jax.ShapeDtypeStruct((3, N_HEADS, BAND, 2 * BAND), f32),
        compiler_params=_cparams(("parallel", "parallel")),
    )(idx, rel_bias)


def _bias_grad(dbias, idx):
    def body(idx_ref, db_ref, o_ref):
        br = pl.program_id(1)

        @pl.when(br == 0)
        def _():
            o_ref[...] = jnp.zeros_like(o_ref)

        idxv = idx_ref[0]
        dbv = db_ref[0, 0]
        row = lax.broadcasted_iota(jnp.int32, (N_BUCKETS, 128), 0)
        acc = jnp.zeros((N_BUCKETS, 128), f32)
        for b in range(N_BUCKETS):
            sb = jnp.sum(jnp.sum(jnp.where(idxv == b, dbv, 0.0), axis=1, keepdims=True), axis=0, keepdims=True)
            acc = acc + jnp.where(row == b, sb, 0.0)
        o_ref[0] += acc

    return pl.pallas_call(
        body, name="attn_bias_grad", grid=(N_HEADS, 3),
        in_specs=[pl.BlockSpec((1, BAND, 2 * BAND), lambda h, br: (br, 0, 0)),
                  pl.BlockSpec((1, 1, BAND, 2 * BAND), lambda h, br: (br, h, 0, 0))],
        out_specs=pl.BlockSpec((1, N_BUCKETS, 128), lambda h, br: (h, 0, 0)),
        out_shape=jax.ShapeDtypeStruct((N_HEADS, N_BUCKETS, 128), f32),
        compiler_params=_cparams(("parallel", "arbitrary")),
    )(idx, dbias)


def _attn_masks():
    lane = lax.broadcasted_iota(jnp.int32, (BAND, 128), 1)
    m0 = lane < HEAD_DIM
    qi = lax.broadcasted_iota(jnp.int32, (BAND, 2 * BAND), 0)
    kj = lax.broadcasted_iota(jnp.int32, (BAND, 2 * BAND), 1)
    steps = qi + BAND - kj
    in_window = (steps >= 0) & (steps <= BAND)
    return m0, in_window, kj >= BAND


_NT = (((1,), (1,)), ((), ()))
_TN = (((0,), (0,)), ((), ()))
ATTN_ITEMS = PAD_UNIT // BAND
Q_COL, K_COL, V_COL = 0, 4, 8


def _attn_item_rows(j, d, c, cbase):
    r = lax.rem(j, d)
    b = lax.div(j, d)
    loc = b * (d * BAND) + r
    first = jnp.logical_and(c == 0, b == 0)
    start = cbase + loc
    pstart = jnp.where(first, start, start - d * BAND)
    return loc, start, pstart, first


def _attn_fwd(proj, bias):
    s = proj.shape[0]

    def body(q_ref, k_ref, v_ref, b_ref, y_ref, lse_ref, o_s, l_s):
        c = pl.program_id(1)
        cbase = pl.multiple_of(c * PAD_UNIT, PAD_UNIT)
        m0, in_window, cur_half = _attn_masks()
        for bi, d in enumerate(DILATIONS):
            def item(j, carry, bi=bi, d=d):
                loc, start, pstart, first = _attn_item_rows(j, d, c, cbase)
                q = q_ref[pl.ds(loc, BAND, stride=d), :]
                kk = jnp.concatenate([k_ref[pl.ds(pstart, BAND, stride=d), :],
                                      k_ref[pl.ds(start, BAND, stride=d), :]], axis=0).astype(bf16)
                vv = jnp.concatenate([v_ref[pl.ds(pstart, BAND, stride=d), :],
                                      v_ref[pl.ds(start, BAND, stride=d), :]], axis=0).astype(bf16)
                valid = in_window & jnp.logical_or(cur_half, jnp.logical_not(first))
                outs, lses = [], []
                for hh in range(2):
                    mh = m0 if hh == 0 else jnp.logical_not(m0)
                    qh = (jnp.where(mh, q, 0.0) * 0.125).astype(bf16)
                    sc = lax.dot_general(qh, kk, _NT, preferred_element_type=f32) + b_ref[bi, hh]
                    sc = jnp.where(valid, sc, NEG_INF)
                    mx = jnp.max(sc, axis=-1, keepdims=True)
                    e = jnp.exp(sc - mx)
                    l = jnp.sum(e, axis=-1, keepdims=True)
                    pv = jnp.dot(e.astype(bf16), vv, preferred_element_type=f32)
                    outs.append(pv / l)
                    lses.append(mx + jnp.log(l))
                o_s[bi, pl.ds(loc, BAND, stride=d), :] = jnp.where(m0, outs[0], outs[1])
                l_s[bi, pl.ds(loc, BAND, stride=d), :] = jnp.where(m0, lses[0], lses[1])
                return carry

            lax.fori_loop(0, ATTN_ITEMS, item, 0)

        def merge(t, carry):
            rows = pl.ds(pl.multiple_of(t * 256, 256), 256)
            ls = [l_s[i, rows, :] for i in range(3)]
            mx = jnp.maximum(jnp.maximum(ls[0], ls[1]), ls[2])
            ws = [jnp.exp(l - mx) for l in ls]
            tot = ws[0] + ws[1] + ws[2]
            y = (ws[0] * o_s[0, rows, :] + ws[1] * o_s[1, rows, :] + ws[2] * o_s[2, rows, :]) / tot
            y_ref[rows, :] = y
            lse_ref[rows, :] = mx + jnp.log(tot)
            return carry

        lax.fori_loop(0, PAD_UNIT // 256, merge, 0)

    chunk = lambda col: pl.BlockSpec((PAD_UNIT, 128), lambda p, c: (c, col + p))
    full = lambda col: pl.BlockSpec((s, 128), lambda p, c: (0, col + p))
    return pl.pallas_call(
        body, name="attn_fwd", grid=(N_HEADS // 2, s // PAD_UNIT),
        in_specs=[chunk(Q_COL), full(K_COL), full(V_COL),
                  pl.BlockSpec((3, 2, BAND, 2 * BAND), lambda p, c: (0, p, 0, 0))],
        out_specs=[chunk(0), chunk(0)],
        out_shape=[jax.ShapeDtypeStruct((s, GROUP_W), f32)] * 2,
        scratch_shapes=[pltpu.VMEM((3, PAD_UNIT, 128), f32)] * 2,
        compiler_params=_cparams(("parallel", "arbitrary")),
    )(proj, proj, proj, bias)


def _attn_bwd(proj, bias, y, lse, dycat):
    s = proj.shape[0]

    def body(q_ref, k_ref, v_ref, b_ref, y_ref, lse_ref, dy_ref, dq_ref, dk_ref, dv_ref, db_ref, dd_s):
        c = pl.program_id(1)
        cbase = pl.multiple_of(c * PAD_UNIT, PAD_UNIT)
        m0, in_window, cur_half = _attn_masks()

        @pl.when(c == 0)
        def _():
            dk_ref[...] = jnp.zeros_like(dk_ref)
            dv_ref[...] = jnp.zeros_like(dv_ref)
            db_ref[...] = jnp.zeros_like(db_ref)

        dq_ref[...] = jnp.zeros_like(dq_ref)

        def rowdot(t, carry):
            rows = pl.ds(pl.multiple_of(t * 256, 256), 256)
            prod = dy_ref[rows, :] * y_ref[rows, :]
            lane = lax.broadcasted_iota(jnp.int32, prod.shape, 1)
            h0 = lane < HEAD_DIM
            d0 = jnp.sum(jnp.where(h0, prod, 0.0), axis=-1, keepdims=True)
            d1 = jnp.sum(jnp.where(h0, 0.0, prod), axis=-1, keepdims=True)
            dd_s[rows, :] = jnp.where(h0, d0, d1)
            return carry

        lax.fori_loop(0, PAD_UNIT // 256, rowdot, 0)

        for bi, d in enumerate(DILATIONS):
            def item(j, carry, bi=bi, d=d):
                loc, start, pstart, first = _attn_item_rows(j, d, c, cbase)
                qrows = pl.ds(loc, BAND, stride=d)
                rows = pl.ds(start, BAND, stride=d)
                prows = pl.ds(pstart, BAND, stride=d)
                q = q_ref[qrows, :]
                do = dy_ref[qrows, :]
                lq = lse_ref[qrows, :]
                dq_ = dd_s[qrows, :]
                kk = jnp.concatenate([k_ref[prows, :], k_ref[rows, :]], axis=0).astype(bf16)
                vv = jnp.concatenate([v_ref[prows, :], v_ref[rows, :]], axis=0).astype(bf16)
                valid = in_window & jnp.logical_or(cur_half, jnp.logical_not(first))
                dqs = []
                dk = jnp.zeros((2 * BAND, 128), f32)
                dv = jnp.zeros((2 * BAND, 128), f32)
                for hh in range(2):
                    mh = m0 if hh == 0 else jnp.logical_not(m0)
                    col = slice(hh * HEAD_DIM, hh * HEAD_DIM + 1)
                    qh = (jnp.where(mh, q, 0.0) * 0.125).astype(bf16)
                    doh = jnp.where(mh, do, 0.0).astype(bf16)
                    sc = lax.dot_general(qh, kk, _NT, preferred_element_type=f32) + b_ref[bi, hh]
                    sc = jnp.where(valid, sc, NEG_INF)
                    p = jnp.exp(sc - lq[:, col])
                    dp = lax.dot_general(doh, vv, _NT, preferred_element_type=f32)
                    ds = p * (dp - dq_[:, col])
                    db_ref[bi, hh] += ds
                    dsb = ds.astype(bf16)
                    dqs.append(jnp.dot(dsb, kk, preferred_element_type=f32) * 0.125)
                    dk = dk + lax.dot_general(dsb, qh, _TN, preferred_element_type=f32)
                    dv = dv + lax.dot_general(p.astype(bf16), doh, _TN, preferred_element_type=f32)
                dq_ref[qrows, :] += jnp.where(m0, dqs[0], dqs[1])
                dk_ref[prows, :] += dk[:BAND]
                dk_ref[rows, :] += dk[BAND:]
                dv_ref[prows, :] += dv[:BAND]
                dv_ref[rows, :] += dv[BAND:]
                return carry

            lax.fori_loop(0, ATTN_ITEMS, item, 0)

    chunk = lambda col: pl.BlockSpec((PAD_UNIT, 128), lambda p, c: (c, col + p))
    full = lambda col: pl.BlockSpec((s, 128), lambda p, c: (0, col + p))
    bias_spec = pl.BlockSpec((3, 2, BAND, 2 * BAND), lambda p, c: (0, p, 0, 0))
    return pl.pallas_call(
        body, name="attn_bwd", grid=(N_HEADS // 2, s // PAD_UNIT),
        in_specs=[chunk(Q_COL), full(K_COL), full(V_COL), bias_spec, chunk(0), chunk(0), chunk(0)],
        out_specs=[chunk(0), full(0), full(0), bias_spec],
        out_shape=[jax.ShapeDtypeStruct((s, GROUP_W), f32)] * 3
        + [jax.ShapeDtypeStruct((3, N_HEADS, BAND, 2 * BAND), f32)],
        scratch_shapes=[pltpu.VMEM((PAD_UNIT, 128), f32)],
        compiler_params=_cparams(("parallel", "arbitrary")),
    )(proj, proj, proj, bias, y, lse, dycat)


_HI = lax.Precision.HIGHEST
DELTA_COL = 1536
Z_COL = 3072
BA_BLOCK = 28
DELTA_ROWS = 512


def _hdot(a, b):
    return jnp.dot(a, b, precision=_HI, preferred_element_type=f32)


def _pair_iota():
    row = lax.broadcasted_iota(jnp.int32, (CHUNK, 128), 0)
    lane = lax.broadcasted_iota(jnp.int32, (CHUNK, 128), 1)
    return row, lane, lane & (CHUNK - 1)


def _bd(x):
    _, lane, _ = _pair_iota()
    m0 = lane < CHUNK
    return jnp.concatenate([jnp.where(m0, x, 0.0), jnp.where(m0, 0.0, x)], axis=0)


def _pmm(a, b):
    return _hdot(a, _bd(b))


def _ntp(x, y):
    return lax.dot_general(x, _bd(y), _NT, precision=_HI, preferred_element_type=f32)


def _tnp(x, y):
    full = lax.dot_general(x, y, _TN, precision=_HI, preferred_element_type=f32)
    r = lax.broadcasted_iota(jnp.int32, (128, 128), 0)
    c = lax.broadcasted_iota(jnp.int32, (128, 128), 1)
    diag = jnp.where((r < CHUNK) == (c < CHUNK), full, 0.0)
    row, lane, jj = _pair_iota()
    fold = jnp.where(jj == row, 1.0, 0.0).astype(f32)
    return _hdot(fold, diag)


def _tri_inv(a):
    row, lane, jj = _pair_iota()
    eye = jnp.where(row == jj, 1.0, 0.0).astype(f32)

    def same_block(log2b):
        return (row >> log2b) == (jj >> log2b)

    dgl = jnp.where(same_block(3), a, 0.0)
    d2 = _pmm(dgl, dgl)
    d4 = _pmm(d2, d2)
    t = _pmm(_pmm(eye - dgl, eye + d2), eye + d4)
    for lb in (3, 4, 5):
        off = jnp.where(same_block(lb + 1) & jnp.logical_not(same_block(lb)), a, 0.0)
        t = t - _pmm(_pmm(t, off), t)
    return t


@jax.custom_vjp
def _solve2(a, xv, xk):
    t = _tri_inv(a)
    return _pmm(t, xv), _pmm(t, xk)


def _solve2_fwd(a, xv, xk):
    t = _tri_inv(a)
    u, w = _pmm(t, xv), _pmm(t, xk)
    return (u, w), (t, u, w)


def _solve2_bwd(res, cts):
    t, u, w = res
    du, dw = cts
    dxv = _tnp(t, du)
    dxk = _tnp(t, dw)
    return -(_ntp(dxv, u) + _ntp(dxk, w)), dxv, dxk


_solve2.defvjp(_solve2_fwd, _solve2_bwd)


def _chunk_pre(qp, kp, vp, bp, gp):
    row, lane, jj = _pair_iota()
    causal = row >= jj
    strict = row > jj
    tril = jnp.where(lax.broadcasted_iota(jnp.int32, (CHUNK, CHUNK), 0)
                     >= lax.broadcasted_iota(jnp.int32, (CHUNK, CHUNK), 1), 1.0, 0.0).astype(f32)
    gcum = _hdot(tril, gp)
    rsel = _hdot(jnp.ones((CHUNK, CHUNK), f32), jnp.where(row == jj, gcum, 0.0))
    decay = jnp.where(causal, jnp.exp(jnp.where(causal, gcum - rsel, 0.0)), 0.0)
    kb = kp * bp
    kd = _bd(kp)
    a = jnp.where(strict, lax.dot_general(kb, kd, _NT, precision=_HI, preferred_element_type=f32) * decay, 0.0)
    eg = jnp.exp(gcum)
    u, w = _solve2(a, vp * bp, kb * eg)
    qk = jnp.where(causal, lax.dot_general(qp, kd, _NT, precision=_HI, preferred_element_type=f32) * decay, 0.0)
    glast = jnp.sum(jnp.where(row == CHUNK - 1, gcum, 0.0), axis=0, keepdims=True)
    return u, w, qp * eg, kp * jnp.exp(glast - gcum), qk, jnp.exp(glast)


def _chunk_post(u, w, qt, kh, qk, gam, sp):
    sd = _bd(sp)
    vnew = u - _hdot(w, sd)
    o = _hdot(qt, sd) + _pmm(qk, vnew)
    return o, gam * sp + _tnp(kh, vnew)


def _pair_spec(rows=DELTA_ROWS):
    return pl.BlockSpec((rows, 128), lambda i, p: (i, p))


def _delta_chunk_pre(qn, kn, sv, beta, g):
    s = qn.shape[0]
    nck = DELTA_ROWS // CHUNK

    def body(q_ref, k_ref, v_ref, b_ref, g_ref, u_ref, w_ref, qt_ref, kh_ref, qk_ref, gm_ref):
        def chunk(ci, carry):
            rows = pl.ds(pl.multiple_of(ci * CHUNK, CHUNK), CHUNK)
            u, w, qt, kh, qk, gam = _chunk_pre(q_ref[rows, :], k_ref[rows, :], v_ref[rows, :],
                                                b_ref[rows, :], g_ref[rows, :])
            u_ref[rows, :] = u
            w_ref[rows, :] = w
            qt_ref[rows, :] = qt
            kh_ref[rows, :] = kh
            qk_ref[rows, :] = qk
            gm_ref[pl.ds(pl.multiple_of(ci * 8, 8), 8), :] = jnp.broadcast_to(gam, (8, 128))
            return carry

        lax.fori_loop(0, nck, chunk, 0)

    v_spec = pl.BlockSpec((DELTA_ROWS, 128), lambda i, p: (i, 8 + p))
    return pl.pallas_call(
        body, name="delta_chunk_pre", grid=(s // DELTA_ROWS, 4),
        in_specs=[_pair_spec(), _pair_spec(), v_spec, _pair_spec(), _pair_spec()],
        out_specs=[_pair_spec()] * 5 + [_pair_spec(nck * 8)],
        out_shape=[jax.ShapeDtypeStruct((s, GROUP_W), f32)] * 5 + [jax.ShapeDtypeStruct((s // 8, GROUP_W), f32)],
        compiler_params=_cparams(("parallel", "parallel")),
    )(qn, kn, sv, beta, g)


def _delta_scan_fwd(u, w, qt, kh, qk, gm):
    s = u.shape[0]
    nck = DELTA_ROWS // CHUNK

    def body(u_ref, w_ref, qt_ref, kh_ref, qk_ref, gm_ref, o_ref, ss_ref, st):
        @pl.when(pl.program_id(0) == 0)
        def _():
            st[...] = jnp.zeros_like(st)

        def chunk(ci, carry):
            rows = pl.ds(pl.multiple_of(ci * CHUNK, CHUNK), CHUNK)
            grow = pl.ds(pl.multiple_of(ci * 8, 8), 1)
            for p in range(4):
                lanes = slice(p * 128, (p + 1) * 128)
                sp = st[p]
                ss_ref[rows, lanes] = sp
                o, s2 = _chunk_post(u_ref[rows, lanes], w_ref[rows, lanes], qt_ref[rows, lanes],
                                    kh_ref[rows, lanes], qk_ref[rows, lanes], gm_ref[grow, lanes], sp)
                o_ref[rows, lanes] = o
                st[p] = s2
            return carry

        lax.fori_loop(0, nck, chunk, 0)

    spec = pl.BlockSpec((DELTA_ROWS, GROUP_W), lambda i: (i, 0))
    gspec = pl.BlockSpec((nck * 8, GROUP_W), lambda i: (i, 0))
    return pl.pallas_call(
        body, name="delta_scan_fwd", grid=(s // DELTA_ROWS,),
        in_specs=[spec] * 5 + [gspec],
        out_specs=[spec, spec],
        out_shape=[jax.ShapeDtypeStruct((s, GROUP_W), f32)] * 2,
        scratch_shapes=[pltpu.VMEM((4, CHUNK, 128), f32)],
        compiler_params=_cparams(("arbitrary",)),
    )(u, w, qt, kh, qk, gm)


def _delta_scan_bwd(w, qt, kh, qk, gm, do):
    s = w.shape[0]
    nck = DELTA_ROWS // CHUNK
    nb = s // DELTA_ROWS

    def body(w_ref, qt_ref, kh_ref, qk_ref, gm_ref, do_ref, dso_ref, dst):
        @pl.when(pl.program_id(0) == 0)
        def _():
            dst[...] = jnp.zeros_like(dst)

        def chunk(t, carry):
            ci = nck - 1 - t
            rows = pl.ds(pl.multiple_of(ci * CHUNK, CHUNK), CHUNK)
            grow = pl.ds(pl.multiple_of(ci * 8, 8), 1)
            for p in range(4):
                lanes = slice(p * 128, (p + 1) * 128)
                ds = dst[p]
                dso_ref[rows, lanes] = ds
                do = do_ref[rows, lanes]
                dvn = _tnp(qk_ref[rows, lanes], do) + _pmm(kh_ref[rows, lanes], ds)
                dst[p] = _tnp(qt_ref[rows, lanes], do) + gm_ref[grow, lanes] * ds - _tnp(w_ref[rows, lanes], dvn)
            return carry

        lax.fori_loop(0, nck, chunk, 0)

    spec = pl.BlockSpec((DELTA_ROWS, GROUP_W), lambda i: (nb - 1 - i, 0))
    gspec = pl.BlockSpec((nck * 8, GROUP_W), lambda i: (nb - 1 - i, 0))
    return pl.pallas_call(
        body, name="delta_scan_bwd", grid=(nb,),
        in_specs=[spec] * 4 + [gspec, spec],
        out_specs=spec,
        out_shape=jax.ShapeDtypeStruct((s, GROUP_W), f32),
        scratch_shapes=[pltpu.VMEM((4, CHUNK, 128), f32)],
        compiler_params=_cparams(("arbitrary",)),
    )(w, qt, kh, qk, gm, do)


def _delta_chunk_bwd(qn, kn, sv, beta, g, ss, dso, do):
    s = qn.shape[0]
    nck = DELTA_ROWS // CHUNK

    def body(q_ref, k_ref, v_ref, b_ref, g_ref, ss_ref, dso_ref, do_ref, dq_ref, dk_ref, dv_ref, db_ref, dg_ref):
        def chunk(ci, carry):
            rows = pl.ds(pl.multiple_of(ci * CHUNK, CHUNK), CHUNK)
            sp = ss_ref[rows, :]

            def fn(q, k, v, b, gg):
                return _chunk_post(*_chunk_pre(q, k, v, b, gg), sp)

            _, vjp = jax.vjp(fn, q_ref[rows, :], k_ref[rows, :], v_ref[rows, :], b_ref[rows, :], g_ref[rows, :])
            dq, dk, dv, db, dg = vjp((do_ref[rows, :], dso_ref[rows, :]))
            dq_ref[rows, :] = dq
            dk_ref[rows, :] = dk
            dv_ref[rows, :] = dv
            db_ref[rows, :] = db
            dg_ref[rows, :] = dg
            return carry

        lax.fori_loop(0, nck, chunk, 0)

    v_spec = pl.BlockSpec((DELTA_ROWS, 128), lambda i, p: (i, 8 + p))
    return pl.pallas_call(
        body, name="delta_chunk_bwd", grid=(s // DELTA_ROWS, 4),
        in_specs=[_pair_spec(), _pair_spec(), v_spec] + [_pair_spec()] * 5,
        out_specs=[_pair_spec()] * 5,
        out_shape=[jax.ShapeDtypeStruct((s, GROUP_W), f32)] * 5,
        compiler_params=_cparams(("parallel", "parallel")),
    )(qn, kn, sv, beta, g, ss, dso, do)


def _head_sum_matrix():
    r = lax.broadcasted_iota(jnp.int32, (GROUP_W, GROUP_W), 0)
    c = lax.broadcasted_iota(jnp.int32, (GROUP_W, GROUP_W), 1)
    return jnp.where((r >> 6) == (c >> 6), 1.0, 0.0).astype(f32)


def _softplus(x):
    return jnp.maximum(x, 0.0) + jnp.log(1.0 + jnp.exp(-jnp.abs(x)))


def _prep_fn(sq, sk, ba, alog_e, dt_e):
    hs = _head_sum_matrix()
    qn = sq * lax.rsqrt(_hdot(sq * sq, hs) + EPS) * (HEAD_DIM ** -0.5)
    kn = sk * lax.rsqrt(_hdot(sk * sk, hs) + EPS)
    r = lax.broadcasted_iota(jnp.int32, (128, GROUP_W), 0)
    c = lax.broadcasted_iota(jnp.int32, (128, GROUP_W), 1) >> 6
    bl = _hdot(ba, jnp.where(r == c, 1.0, 0.0).astype(f32))
    al = _hdot(ba, jnp.where(r == c + N_HEADS, 1.0, 0.0).astype(f32))
    beta = jax.nn.sigmoid(bl)
    g = -jnp.exp(alog_e) * _softplus(al + dt_e)
    return qn, kn, beta, g


def _gnorm_fn(o, z, ng_e):
    ms = _hdot(o * o, _head_sum_matrix()) * (1.0 / HEAD_DIM)
    return o * lax.rsqrt(ms + EPS) * ng_e * (z * jax.nn.sigmoid(z))


def _tok_spec(width, col):
    return pl.BlockSpec((TOK_TILE, width), lambda i: (i, col))


def _conv_taps(xs_ref, w_ref, base, n):
    acc = w_ref[CONV_WIDTH - 1:CONV_WIDTH, :] * xs_ref[pl.ds(base, n), :]
    for j in range(CONV_WIDTH - 1):
        acc = acc + w_ref[j:j + 1, :] * xs_ref[pl.ds(base - (CONV_WIDTH - 1) + j, n), :]
    return acc


def _conv_silu_fwd(proj, conv_w):
    s = proj.shape[0]
    wd = 3 * GROUP_W
    hb = TOK_TILE // 8

    def body(x_ref, halo_ref, w_ref, o_ref, xs):
        xs[0:8, :] = jnp.where(pl.program_id(0) > 0, halo_ref[...], 0.0)
        xs[8:, :] = x_ref[...]
        y = _conv_taps(xs, w_ref, 8, TOK_TILE)
        o_ref[...] = y * jax.nn.sigmoid(y)

    return pl.pallas_call(
        body, name="delta_conv_fwd", grid=(s // TOK_TILE,),
        in_specs=[_tok_spec(wd, 1), pl.BlockSpec((8, wd), lambda i: (jnp.maximum(i * hb - 1, 0), 1)),
                  pl.BlockSpec((CONV_WIDTH, wd), lambda i: (0, 0))],
        out_specs=_tok_spec(wd, 0),
        out_shape=jax.ShapeDtypeStruct((s, wd), f32),
        scratch_shapes=[pltpu.VMEM((TOK_TILE + 8, wd), f32)],
        compiler_params=_cparams(("parallel",)),
    )(proj, proj, conv_w)


def _conv_silu_bwd(proj, conv_w, ds):
    s = proj.shape[0]
    wd = 3 * GROUP_W
    hb = TOK_TILE // 8
    nt = s // TOK_TILE

    def body(x_ref, hp_ref, hn_ref, ds_ref, dsn_ref, w_ref, dx_ref, dw_ref, xs, dys):
        i = pl.program_id(0)

        @pl.when(i == 0)
        def _():
            dw_ref[...] = jnp.zeros_like(dw_ref)

        last = i == nt - 1
        xs[0:8, :] = jnp.where(i > 0, hp_ref[...], 0.0)
        xs[8:8 + TOK_TILE, :] = x_ref[...]
        xs[8 + TOK_TILE:, :] = jnp.where(last, 0.0, hn_ref[...])
        y = _conv_taps(xs, w_ref, 8, TOK_TILE)
        sg = jax.nn.sigmoid(y)
        dys[0:TOK_TILE, :] = ds_ref[...] * sg * (1.0 + y * (1.0 - sg))
        yn = _conv_taps(xs, w_ref, 8 + TOK_TILE, 8)
        sgn = jax.nn.sigmoid(yn)
        dys[TOK_TILE:, :] = jnp.where(last, 0.0, dsn_ref[...]) * sgn * (1.0 + yn * (1.0 - sgn))
        dy0 = dys[0:TOK_TILE, :]
        dx = w_ref[CONV_WIDTH - 1:CONV_WIDTH, :] * dy0
        for j in range(CONV_WIDTH - 1):
            dx = dx + w_ref[j:j + 1, :] * dys[pl.ds(CONV_WIDTH - 1 - j, TOK_TILE), :]
        dx_ref[...] = dx
        for j in range(CONV_WIDTH):
            dw_ref[j:j + 1, :] += jnp.sum(dy0 * xs[pl.ds(8 - (CONV_WIDTH - 1) + j, TOK_TILE), :],
                                          axis=0, keepdims=True)

    prev8 = lambda col: pl.BlockSpec((8, wd), lambda i: (jnp.maximum(i * hb - 1, 0), col))
    next8 = lambda col: pl.BlockSpec((8, wd), lambda i: (jnp.minimum((i + 1) * hb, s // 8 - 1), col))
    return pl.pallas_call(
        body, name="delta_conv_bwd", grid=(nt,),
        in_specs=[_tok_spec(wd, 1), prev8(1), next8(1), _tok_spec(wd, 0), next8(0),
                  pl.BlockSpec((CONV_WIDTH, wd), lambda i: (0, 0))],
        out_specs=[_tok_spec(wd, 0), pl.BlockSpec((CONV_WIDTH, wd), lambda i: (0, 0))],
        out_shape=[jax.ShapeDtypeStruct((s, wd), f32), jax.ShapeDtypeStruct((CONV_WIDTH, wd), f32)],
        scratch_shapes=[pltpu.VMEM((TOK_TILE + 16, wd), f32), pltpu.VMEM((TOK_TILE + 8, wd), f32)],
        compiler_params=_cparams(("arbitrary",)),
    )(proj, proj, proj, ds, ds, conv_w)


def _delta_prep_fwd(sconv, proj, alog_e, dt_e):
    s = sconv.shape[0]

    def body(sq_ref, sk_ref, ba_ref, al_ref, dt_ref, q_ref, k_ref, b_ref, g_ref):
        qn, kn, beta, g = _prep_fn(sq_ref[...], sk_ref[...], ba_ref[...], al_ref[...], dt_ref[...])
        q_ref[...] = qn
        k_ref[...] = kn
        b_ref[...] = beta
        g_ref[...] = g

    return pl.pallas_call(
        body, name="delta_prep_fwd", grid=(s // TOK_TILE,),
        in_specs=[_tok_spec(GROUP_W, 0), _tok_spec(GROUP_W, 1), _tok_spec(128, BA_BLOCK),
                  _vec_spec(GROUP_W), _vec_spec(GROUP_W)],
        out_specs=[_tok_spec(GROUP_W, 0)] * 4,
        out_shape=[jax.ShapeDtypeStruct((s, GROUP_W), f32)] * 4,
        compiler_params=_cparams(("parallel",)),
    )(sconv, sconv, proj, alog_e, dt_e)


def _delta_prep_bwd(sconv, proj, alog_e, dt_e, dqn, dkn, dbeta, dg):
    s = sconv.shape[0]

    def body(sq_ref, sk_ref, ba_ref, al_ref, dt_ref, dq_ref, dk_ref, db_ref, dg_ref,
             dsq_ref, dsk_ref, dba_ref, dal_ref, ddt_ref):
        @pl.when(pl.program_id(0) == 0)
        def _():
            dal_ref[...] = jnp.zeros_like(dal_ref)
            ddt_ref[...] = jnp.zeros_like(ddt_ref)

        _, vjp = jax.vjp(_prep_fn, sq_ref[...], sk_ref[...], ba_ref[...], al_ref[...], dt_ref[...])
        dsq, dsk, dba, dal, ddt = vjp((dq_ref[...], dk_ref[...], db_ref[...], dg_ref[...]))
        dsq_ref[...] = dsq
        dsk_ref[...] = dsk
        dba_ref[...] = dba
        dal_ref[...] += dal
        ddt_ref[...] += ddt

    return pl.pallas_call(
        body, name="delta_prep_bwd", grid=(s // TOK_TILE,),
        in_specs=[_tok_spec(GROUP_W, 0), _tok_spec(GROUP_W, 1), _tok_spec(128, BA_BLOCK),
                  _vec_spec(GROUP_W), _vec_spec(GROUP_W)] + [_tok_spec(GROUP_W, 0)] * 4,
        out_specs=[_tok_spec(GROUP_W, 0), _tok_spec(GROUP_W, 0), _tok_spec(128, 0),
                   _acc_spec(GROUP_W), _acc_spec(GROUP_W)],
        out_shape=[jax.ShapeDtypeStruct((s, GROUP_W), f32)] * 2 + [jax.ShapeDtypeStruct((s, 128), f32)]
        + [jax.ShapeDtypeStruct((1, GROUP_W), f32)] * 2,
        compiler_params=_cparams(("arbitrary",)),
    )(sconv, sconv, proj, alog_e, dt_e, dqn, dkn, dbeta, dg)


def _gnorm_fwd(o, proj, ng_e):
    s = o.shape[0]

    def body(o_ref, z_ref, g_ref, y_ref):
        y_ref[...] = _gnorm_fn(o_ref[...], z_ref[...], g_ref[...])

    return pl.pallas_call(
        body, name="delta_gnorm_fwd", grid=(s // TOK_TILE,),
        in_specs=[_tok_spec(GROUP_W, 0), _tok_spec(GROUP_W, Z_COL // GROUP_W), _vec_spec(GROUP_W)],
        out_specs=_tok_spec(GROUP_W, 0),
        out_shape=jax.ShapeDtypeStruct((s, GROUP_W), f32),
        compiler_params=_cparams(("parallel",)),
    )(o, proj, ng_e)


def _gnorm_bwd(o, proj, ng_e, dycat):
    s = o.shape[0]

    def body(o_ref, z_ref, g_ref, dy_ref, do_ref, dz_ref, dg_ref):
        @pl.when(pl.program_id(0) == 0)
        def _():
            dg_ref[...] = jnp.zeros_like(dg_ref)

        _, vjp = jax.vjp(_gnorm_fn, o_ref[...], z_ref[...], g_ref[...])
        do, dz, dg = vjp(dy_ref[...])
        do_ref[...] = do
        dz_ref[...] = dz
        dg_ref[...] += dg

    return pl.pallas_call(
        body, name="delta_gnorm_bwd", grid=(s // TOK_TILE,),
        in_specs=[_tok_spec(GROUP_W, 0), _tok_spec(GROUP_W, Z_COL // GROUP_W), _vec_spec(GROUP_W),
                  _tok_spec(GROUP_W, 1)],
        out_specs=[_tok_spec(GROUP_W, 0), _tok_spec(GROUP_W, 0), _acc_spec(GROUP_W)],
        out_shape=[jax.ShapeDtypeStruct((s, GROUP_W), f32)] * 2 + [jax.ShapeDtypeStruct((1, GROUP_W), f32)],
        compiler_params=_cparams(("arbitrary",)),
    )(o, proj, ng_e, dycat)


_MESH = pl.DeviceIdType.MESH
_ANY = pl.BlockSpec(memory_space=pl.ANY)
_VMEM = pl.BlockSpec(memory_space=pltpu.VMEM)


def _my_place():
    x, y, c = lax.axis_index("x"), lax.axis_index("y"), lax.axis_index("c")
    return x, y, c, 4 * x + 2 * y + c


def _peer(k, x, y, c):
    px = 1 - x if k & 4 else x
    py = 1 - y if k & 2 else y
    pc = 1 - c if k & 1 else c
    return (px, py, pc), 4 * px + 2 * py + pc


def _exchange_all(src_of_peer, dst_ref, send_sems, recv_sems, x, y, c, me):
    sent = []
    for k in range(1, N_DEV):
        dev, pidx = _peer(k, x, y, c)
        cp = pltpu.make_async_remote_copy(src_ref=src_of_peer(pidx), dst_ref=dst_ref.at[me],
                                          send_sem=send_sems.at[k - 1], recv_sem=recv_sems.at[k - 1],
                                          device_id=dev, device_id_type=_MESH)
        cp.start()
        sent.append(cp)
    for k in range(1, N_DEV):
        dev, pidx = _peer(k, x, y, c)
        pltpu.make_async_remote_copy(src_ref=src_of_peer(pidx), dst_ref=dst_ref.at[pidx],
                                     send_sem=send_sems.at[k - 1], recv_sem=recv_sems.at[k - 1],
                                     device_id=dev, device_id_type=_MESH).wait_recv()
    for cp in sent:
        cp.wait_send()


def _ada_exchange(cv8, w_ada, b_ada8):
    def body(cv_ref, w_ref, b_ref, call_ref, modp_ref, part_s, s1, r1, s2, r2):
        x, y, c, me = _my_place()
        call_ref[me] = cv_ref[...]
        _exchange_all(lambda pidx: cv_ref, call_ref, s1, r1, x, y, c, me)
        bias = b_ref[me]
        for j in range(N_DEV):
            cj = call_ref[j][:, :D_MODEL]
            part_s[j] = _hdot(cj * jax.nn.sigmoid(cj), w_ref[...]) + bias
        modp_ref[me] = part_s[me]
        _exchange_all(lambda pidx: part_s.at[pidx], modp_ref, s2, r2, x, y, c, me)

    nsh = w_ada.shape[1]
    return pl.pallas_call(
        body, name="ada_exchange",
        in_specs=[_VMEM, _VMEM, _VMEM], out_specs=[_VMEM, _VMEM],
        out_shape=[jax.ShapeDtypeStruct((N_DEV, 8, cv8.shape[1]), f32), jax.ShapeDtypeStruct((N_DEV, 8, nsh), f32)],
        scratch_shapes=[pltpu.VMEM((N_DEV, 8, nsh), f32)] + [pltpu.SemaphoreType.DMA((N_DEV - 1,))] * 4,
        compiler_params=pltpu.CompilerParams(vmem_limit_bytes=VMEM_LIMIT),
    )(cv8, w_ada, b_ada8)


def _all_to_all(arrs, name):
    n = len(arrs)

    def body(*refs):
        srcs, dsts = refs[:n], refs[n:2 * n]
        send_sems, recv_sems, local_sems = refs[2 * n:]
        x, y, c, me = _my_place()
        local = []
        for a in range(n):
            cp = pltpu.make_async_copy(srcs[a].at[me], dsts[a].at[me], local_sems.at[a])
            cp.start()
            local.append(cp)
        sent = []
        for a in range(n):
            for k in range(1, N_DEV):
                dev, pidx = _peer(k, x, y, c)
                cp = pltpu.make_async_remote_copy(src_ref=srcs[a].at[pidx], dst_ref=dsts[a].at[me],
                                                  send_sem=send_sems.at[a, k - 1], recv_sem=recv_sems.at[a, k - 1],
                                                  device_id=dev, device_id_type=_MESH)
                cp.start()
                sent.append(cp)
        for a in range(n):
            for k in range(1, N_DEV):
                dev, pidx = _peer(k, x, y, c)
                pltpu.make_async_remote_copy(src_ref=srcs[a].at[pidx], dst_ref=dsts[a].at[pidx],
                                             send_sem=send_sems.at[a, k - 1], recv_sem=recv_sems.at[a, k - 1],
                                             device_id=dev, device_id_type=_MESH).wait_recv()
        for cp in sent:
            cp.wait_send()
        for cp in local:
            cp.wait()

    return pl.pallas_call(
        body, name=name,
        in_specs=[_ANY] * n, out_specs=[_ANY] * n,
        out_shape=[jax.ShapeDtypeStruct(a.shape, a.dtype) for a in arrs],
        scratch_shapes=[pltpu.SemaphoreType.DMA((n, N_DEV - 1)), pltpu.SemaphoreType.DMA((n, N_DEV - 1)),
                        pltpu.SemaphoreType.DMA((n,))],
    )(*arrs)


def _all_gather_weights(shards):
    n = len(shards)

    def body(*refs):
        srcs, outs = refs[:n], refs[n:2 * n]
        send_sems, recv_sems, local_sems = refs[2 * n:]
        x, y, c, me = _my_place()
        sib = (x, y, 1 - c)
        chips = [(1 - x, y), (x, 1 - y), (1 - x, 1 - y)]

        def idx(px, py, pc):
            return 4 * px + 2 * py + pc

        def copy(a, k, block, to, src=None):
            rows = outs[a].at[idx(*block)]
            return pltpu.make_async_remote_copy(src_ref=rows if src is None else src, dst_ref=rows,
                                                send_sem=send_sems.at[a, k], recv_sem=recv_sems.at[a, k],
                                                device_id=to, device_id_type=_MESH)

        mine, first, passed = [], [], []
        for a in range(n):
            cp = pltpu.make_async_copy(srcs[a], outs[a].at[me], local_sems.at[a])
            cp.start()
            mine.append(cp)
            fa = [copy(a, 0, (x, y, c), sib, src=srcs[a])]
            fa += [copy(a, 1 + j, (x, y, c), (*chip, c), src=srcs[a]) for j, chip in enumerate(chips)]
            for cp in fa:
                cp.start()
            first += fa
        for a in range(n):
            for j, chip in enumerate(chips):
                copy(a, 1 + j, (*chip, c), (x, y, c)).wait_recv()
                cp = copy(a, 4 + j, (*chip, c), sib)
                cp.start()
                passed.append(cp)
        for a in range(n):
            copy(a, 0, (x, y, 1 - c), (x, y, c)).wait_recv()
            for j, chip in enumerate(chips):
                copy(a, 4 + j, (*chip, 1 - c), (x, y, c)).wait_recv()
        for cp in first + passed:
            cp.wait_send()
        for cp in mine:
            cp.wait()

    return pl.pallas_call(
        body, name="gather_weights",
        in_specs=[_ANY] * n, out_specs=[_ANY] * n,
        out_shape=[jax.ShapeDtypeStruct((N_DEV,) + a.shape, a.dtype) for a in shards],
        scratch_shapes=[pltpu.SemaphoreType.DMA((n, N_DEV - 1)), pltpu.SemaphoreType.DMA((n, N_DEV - 1)),
                        pltpu.SemaphoreType.DMA((n,))],
    )(*shards)


def _adamw_math(w, g, m, v):
    m2 = ADAM_B1 * m + (1.0 - ADAM_B1) * g
    v2 = ADAM_B2 * v + (1.0 - ADAM_B2) * (g * g)
    m_hat = m2 / (1.0 - ADAM_B1 ** ADAM_STEP)
    v_hat = v2 / (1.0 - ADAM_B2 ** ADAM_STEP)
    delta = -ADAM_LR * (m_hat / (jnp.sqrt(v_hat) + ADAM_EPS) + ADAM_WD * w)
    return delta, m2, v2


def _row_tile(rows):
    for t in (256, 128, 64, 32, 16, 8):
        if rows % t == 0:
            return t
    return rows


def _reduce_adamw(parts, w, m, v, name):
    _, r, cdim = parts.shape
    tr = _row_tile(r)

    def body(p_ref, w_ref, m_ref, v_ref, g_ref, d_ref, m2_ref, v2_ref):
        g = p_ref[0]
        for j in range(1, N_DEV):
            g = g + p_ref[j]
        delta, m2, v2 = _adamw_math(w_ref[...], g, m_ref[...], v_ref[...])
        g_ref[...] = g
        d_ref[...] = delta
        m2_ref[...] = m2
        v2_ref[...] = v2

    spec = pl.BlockSpec((tr, cdim), lambda i: (i, 0))
    return pl.pallas_call(
        body, name=name, grid=(r // tr,),
        in_specs=[pl.BlockSpec((N_DEV, tr, cdim), lambda i: (0, i, 0)), spec, spec, spec],
        out_specs=[spec] * 4,
        out_shape=[jax.ShapeDtypeStruct((r, cdim), f32)] * 4,
        compiler_params=_cparams(("parallel",)),
    )(parts, w, m, v)


def _adamw(w, g, m, v, name):
    r, cdim = w.shape
    tr = _row_tile(r)

    def body(w_ref, g_ref, m_ref, v_ref, d_ref, m2_ref, v2_ref):
        delta, m2, v2 = _adamw_math(w_ref[...], g_ref[...], m_ref[...], v_ref[...])
        d_ref[...] = delta
        m2_ref[...] = m2
        v2_ref[...] = v2

    spec = pl.BlockSpec((tr, cdim), lambda i: (i, 0))
    return pl.pallas_call(
        body, name=name, grid=(r // tr,),
        in_specs=[spec] * 4, out_specs=[spec] * 3,
        out_shape=[jax.ShapeDtypeStruct((r, cdim), f32)] * 3,
        compiler_params=_cparams(("parallel",)),
    )(w, g, m, v)


def _sum_devices(parts, name):
    _, r, cdim = parts.shape

    def body(p_ref, o_ref):
        g = p_ref[0]
        for j in range(1, N_DEV):
            g = g + p_ref[j]
        o_ref[...] = g

    return pl.pallas_call(
        body, name=name, out_shape=jax.ShapeDtypeStruct((r, cdim), f32),
        in_specs=[_VMEM], out_specs=_VMEM,
    )(parts)


def _ada_wgrad(c_all8, dmod_cols):
    nsh = dmod_cols.shape[1]

    def body(c_ref, d_ref, o_ref):
        cv = c_ref[...]
        o_ref[...] = lax.dot_general(cv * jax.nn.sigmoid(cv), d_ref[...], _TN, precision=_HI,
                                     preferred_element_type=f32)

    return pl.pallas_call(
        body, name="ada_wgrad", out_shape=jax.ShapeDtypeStruct((D_MODEL, nsh), f32),
        in_specs=[_VMEM, _VMEM], out_specs=_VMEM,
        compiler_params=pltpu.CompilerParams(vmem_limit_bytes=VMEM_LIMIT),
    )(c_all8, dmod_cols)


def _local_step(x, tgt, mod, norm_attn_g, w_in_p, rel_bias, conv_full, a_log, dt_bias, delta_norm_g,
                w_out_b, norm_ffn_g, w_gu_b, w_down_b, final_norm_g):
    s = x.shape[0]
    sh1, sc1, g1, sh2, sc2, g2 = [mod[:, i * D_MODEL:(i + 1) * D_MODEL] for i in range(6)]
    nag = norm_attn_g.reshape(1, D_MODEL)
    nfg = norm_ffn_g.reshape(1, D_MODEL)
    fg = final_norm_g.reshape(1, D_MODEL)
    idx = _bucket_tables()
    bias = _bias_tables(rel_bias, idx)
    alog_e = jnp.repeat(a_log.reshape(N_HEADS), HEAD_DIM)[None]
    dt_e = jnp.repeat(dt_bias.reshape(N_HEADS), HEAD_DIM)[None]
    ng_e = jnp.tile(delta_norm_g.reshape(HEAD_DIM), N_HEADS)[None]

    h1 = _ln_mod_fwd(x, nag, sc1, sh1, "ln1_fwd")
    proj = _mm(h1, w_in_p, "nn", f32, 512, 1280, 1024, "in_proj")
    y_attn, lse = _attn_fwd(proj, bias)
    sconv = _conv_silu_fwd(proj, conv_full)
    qn, kn, beta, g = _delta_prep_fwd(sconv, proj, alog_e, dt_e)
    u, w, qt, kh, qk, gm = _delta_chunk_pre(qn, kn, sconv, beta, g)
    o, ss = _delta_scan_fwd(u, w, qt, kh, qk, gm)
    y_delta = _gnorm_fwd(o, proj, ng_e)
    ycat = jnp.concatenate([y_attn, y_delta], axis=1).astype(bf16)
    y = _mm(ycat, w_out_b, "nn", f32, 512, 1024, 1024, "out_proj")
    x1, h2 = _resid_ln_mod_fwd(x, y, g1, nfg, sc2, sh2, "ln2_fwd")
    gu = _mm(h2, w_gu_b, "nn", f32, 512, 1408, 1024, "ffn_up")
    act = _swiglu_fwd(gu, "swiglu_fwd")
    y2 = _mm(act, w_down_b, "nn", f32, 512, 1024, D_FF, "ffn_down")
    dx2, dy2, loss, dfg, dg2 = _final_loss_bwd(x1, y2, g2, fg, tgt, "final_loss")

    dact = _mm(dy2, w_down_b, "nt", f32, 512, 1408, 1024, "ffn_down_dx")
    g_down = _mm(act, dy2, "tn", f32, 1408, 1024, 512, "ffn_down_dw")
    dgu = _swiglu_bwd(gu, dact, "swiglu_bwd")
    dh2 = _mm(dgu, w_gu_b, "nt", f32, 512, 1024, 1408, "ffn_up_dx")
    g_gu = _mm(h2, dgu, "tn", f32, 1024, 1408, 512, "ffn_up_dw")
    dx1, dsh2, dsc2, dnfg, dy, dg1 = _ln_mod_bwd(x1, nfg, sc2, dh2, dx2, "ln2_bwd", gate=g1, y=y)
    dycat = _mm(dy, w_out_b, "nt", f32, 512, 1024, 1024, "out_proj_dx")
    g_out = _mm(ycat, dy, "tn", f32, 1024, 1024, 512, "out_proj_dw")
    dq, dk, dv, dbias = _attn_bwd(proj, bias, y_attn, lse, dycat)
    g_rb = _bias_grad(dbias, idx)[:, :, 0].T
    do, dz, dng = _gnorm_bwd(o, proj, ng_e, dycat)
    dso = _delta_scan_bwd(w, qt, kh, qk, gm, do)
    dqn, dkn, dvd, dbeta, dgd = _delta_chunk_bwd(qn, kn, sconv, beta, g, ss, dso, do)
    dsq, dsk, dba, dal, ddt = _delta_prep_bwd(sconv, proj, alog_e, dt_e, dqn, dkn, dbeta, dgd)
    dxc, g_conv = _conv_silu_bwd(proj, conv_full, jnp.concatenate([dsq, dsk, dvd], axis=1))
    dproj = jnp.concatenate([dq, dk, dv, dxc, dz, dba, jnp.zeros((s, IN_PAD - BA_BLOCK * 128 - 128), f32)],
                            axis=1).astype(bf16)
    dh1 = _mm(dproj, w_in_p, "nt", f32, 512, 1024, 1280, "in_proj_dx")
    g_in = _mm(h1, dproj, "tn", f32, 1024, 1280, 512, "in_proj_dw")
    gx, dsh1, dsc1, dnag = _ln_mod_bwd(x, nag, sc1, dh1, dx1, "ln1_bwd")
    grads = dict(
        x=gx, mod=jnp.concatenate([dsh1, dsc1, dg1, dsh2, dsc2, dg2], axis=1),
        norm_attn_g=dnag, norm_ffn_g=dnfg, final_norm_g=dfg, rel_bias=g_rb, conv_w=g_conv,
        a_log=dal.reshape(N_HEADS, HEAD_DIM).sum(-1), dt_bias=ddt.reshape(N_HEADS, HEAD_DIM).sum(-1),
        delta_norm_g=dng.reshape(N_HEADS, HEAD_DIM).sum(0),
        w_in=g_in, w_out=g_out, w_gu=g_gu, w_down=g_down)
    return loss[0, 0], grads


MISC_OFF = dict(rel_bias=0, a_log=256, dt_bias=264, delta_norm_g=272)


def _misc_row(rel_bias, a_log, dt_bias, delta_norm_g):
    flat = jnp.concatenate([rel_bias.reshape(-1), a_log.reshape(-1), dt_bias.reshape(-1), delta_norm_g.reshape(-1)])
    return jnp.pad(flat, (0, D_MODEL - flat.shape[0]))[None]


def _pack_small(b_ada, nag, nfg, fng, rel_bias, a_log, dt_bias, dng, conv_shard):
    rows = [b_ada.reshape(6, D_MODEL), nag.reshape(1, D_MODEL), nfg.reshape(1, D_MODEL), fng.reshape(1, D_MODEL),
            _misc_row(rel_bias, a_log, dt_bias, dng),
            jnp.pad(conv_shard.reshape(-1), (0, D_MODEL - conv_shard.size))[None],
            jnp.zeros((5, D_MODEL), f32)]
    return jnp.concatenate(rows, axis=0)


def _unpack_small(p, conv_shape):
    misc = p[9]
    return dict(
        b_ada=p[0:6].reshape(1, 6 * D_MODEL), norm_attn_g=p[6:7], norm_ffn_g=p[7:8], final_norm_g=p[8],
        rel_bias=misc[0:256].reshape(N_BUCKETS, N_HEADS), a_log=misc[256:264].reshape(1, N_HEADS),
        dt_bias=misc[264:272].reshape(1, N_HEADS), delta_norm_g=misc[272:336].reshape(1, HEAD_DIM),
        conv_w=p[10, :conv_shape[1] * conv_shape[2]].reshape(conv_shape))


def kernel(x, c, w_ada, b_ada, norm_attn_g, w_in, rel_bias, conv_w, a_log, dt_bias, delta_norm_g, w_out, norm_ffn_g, w_gate, w_up, w_down, final_norm_g, loss_target, m_w_ada, m_b_ada, m_norm_attn_g, m_w_in, m_rel_bias, m_conv_w, m_a_log, m_dt_bias, m_delta_norm_g, m_w_out, m_norm_ffn_g, m_w_gate, m_w_up, m_w_down, m_final_norm_g, v_w_ada, v_b_ada, v_norm_attn_g, v_w_in, v_rel_bias, v_conv_w, v_a_log, v_dt_bias, v_delta_norm_g, v_w_out, v_norm_ffn_g, v_w_gate, v_w_up, v_w_down, v_final_norm_g):
    me = 4 * lax.axis_index("x") + 2 * lax.axis_index("y") + lax.axis_index("c")
    ada_sh = w_ada.shape[2]
    conv_sh = conv_w.shape[2]

    cv = jnp.concatenate([c[0], conv_w[0].reshape(-1)])
    cv8 = jnp.zeros((8, 2 * D_MODEL), f32).at[0, :cv.shape[0]].set(cv)
    b8 = jnp.broadcast_to(b_ada.reshape(N_DEV, 1, ada_sh), (N_DEV, 8, ada_sh))
    call, modp = _ada_exchange(cv8, w_ada[0], b8)
    mod = modp[:, 0, :].reshape(1, 6 * D_MODEL)
    c_all = call[:, 0, :D_MODEL]
    conv_full = call[:, 0, D_MODEL:D_MODEL + CONV_WIDTH * conv_sh].reshape(N_DEV, CONV_WIDTH, conv_sh)
    conv_full = conv_full.transpose(1, 0, 2).reshape(CONV_WIDTH, N_DEV * conv_sh)

    gw = _all_gather_weights([w_in[0].astype(bf16), w_out[0].astype(bf16), w_gate[0].astype(bf16),
                              w_up[0].astype(bf16), w_down[0].astype(bf16)])
    cols = lambda t: t.transpose(1, 0, 2).reshape(t.shape[1], N_DEV * t.shape[2])
    w_in_p = jnp.pad(cols(gw[0]), ((0, 0), (0, IN_PAD - IN_WIDTH)))
    w_out_b = gw[1].reshape(2 * GROUP_W, D_MODEL)
    w_gu_b = jnp.concatenate([cols(gw[2]), cols(gw[3])], axis=1)
    w_down_b = gw[4].reshape(D_FF, D_MODEL)

    loss_local, gr = _local_step(x[0], loss_target[0], mod, norm_attn_g, w_in_p, rel_bias, conv_full, a_log,
                                 dt_bias, delta_norm_g, w_out_b, norm_ffn_g, w_gu_b, w_down_b, final_norm_g)
    loss = lax.psum(loss_local, ("x", "y", "c"))

    small = jnp.concatenate([
        gr["mod"].reshape(6, D_MODEL), gr["norm_attn_g"], gr["norm_ffn_g"], gr["final_norm_g"],
        gr["conv_w"].reshape(6, D_MODEL),
        _misc_row(gr["rel_bias"], gr["a_log"], gr["dt_bias"], gr["delta_norm_g"])], axis=0)
    parts = _all_to_all([jnp.broadcast_to(small[None], (N_DEV,) + small.shape)], "small_gather")[0]
    tot = _sum_devices(parts, "small_sum")
    g_conv_full = tot[9:15].reshape(CONV_WIDTH, N_DEV * conv_sh)
    g_conv = lax.dynamic_slice(g_conv_full, (0, me * conv_sh), (CONV_WIDTH, conv_sh))
    misc = tot[15]
    g_small = _pack_small(tot[0:6], tot[6], tot[7], tot[8], misc[0:256], misc[256:264], misc[264:272],
                          misc[272:336], g_conv)
    pk = lambda pre: _pack_small(pre[0], pre[1], pre[2], pre[3], pre[4], pre[5], pre[6], pre[7], pre[8])
    w_small = pk((b_ada, norm_attn_g, norm_ffn_g, final_norm_g, rel_bias, a_log, dt_bias, delta_norm_g, conv_w))
    m_small = pk((m_b_ada, m_norm_attn_g, m_norm_ffn_g, m_final_norm_g, m_rel_bias, m_a_log, m_dt_bias,
                  m_delta_norm_g, m_conv_w))
    v_small = pk((v_b_ada, v_norm_attn_g, v_norm_ffn_g, v_final_norm_g, v_rel_bias, v_a_log, v_dt_bias,
                  v_delta_norm_g, v_conv_w))
    d_small, m2_small, v2_small = _adamw(w_small, g_small, m_small, v_small, "adamw_small")
    cshape = conv_w.shape
    G, Dl, M2, V2 = (_unpack_small(t, cshape) for t in (g_small, d_small, m2_small, v2_small))

    dmod_all = parts[:, 0:6, :].reshape(N_DEV, 6 * D_MODEL)
    dmod_cols = lax.dynamic_slice(dmod_all, (0, me * ada_sh), (N_DEV, ada_sh))
    g_ada = _ada_wgrad(c_all, dmod_cols)
    d_ada, m2_ada, v2_ada = _adamw(w_ada[0], g_ada, m_w_ada[0], v_w_ada[0], "adamw_w_ada")

    colblk = lambda t, n: t.reshape(t.shape[0], N_DEV, n).transpose(1, 0, 2)
    rowblk = lambda t: t.reshape(N_DEV, t.shape[0] // N_DEV, t.shape[1])
    n_in, n_ff = w_in.shape[2], w_gate.shape[2]
    sends = [colblk(gr["w_in"][:, :IN_WIDTH], n_in), rowblk(gr["w_out"]), colblk(gr["w_gu"][:, :D_FF], n_ff),
             colblk(gr["w_gu"][:, D_FF:], n_ff), rowblk(gr["w_down"])]
    recv = _all_to_all(sends, "grad_exchange")
    big = {}
    for name, p, w_, m_, v_ in (("w_in", recv[0], w_in, m_w_in, v_w_in), ("w_out", recv[1], w_out, m_w_out, v_w_out),
                                ("w_gate", recv[2], w_gate, m_w_gate, v_w_gate), ("w_up", recv[3], w_up, m_w_up, v_w_up),
                                ("w_down", recv[4], w_down, m_w_down, v_w_down)):
        big[name] = [t[None] for t in _reduce_adamw(p, w_[0], m_[0], v_[0], "reduce_adamw_" + name)]

    def leaf(i, name):
        if name == "w_ada":
            return (g_ada, d_ada, m2_ada, v2_ada)[i][None]
        if name in big:
            return big[name][i]
        return (G, Dl, M2, V2)[i][name]

    order = ["w_ada", "b_ada", "norm_attn_g", "w_in", "rel_bias", "conv_w", "a_log", "dt_bias", "delta_norm_g",
             "w_out", "norm_ffn_g", "w_gate", "w_up", "w_down", "final_norm_g"]
    outs = [loss, gr["x"][None]]
    for i in range(4):
        outs += [leaf(i, n) for n in order]
    return tuple(outs)
```

```python
import functools
import math

import jax
import jax.numpy as jnp
from jax import lax
from jax.experimental import pallas as pl
from jax.experimental.pallas import tpu as pltpu

f32 = jnp.float32
bf16 = jnp.bfloat16

D_MODEL = 1024
HEAD_DIM = 64
N_HEADS = 8
GROUP_W = 512
IN_WIDTH = 3600
IN_PAD = 3840
D_FF = 2816
EPS = 1e-6
NEG_INF = -1e30
BAND = 128
PAD_UNIT = 2048
DILATIONS = (1, 4, 16)
N_BUCKETS = 32
MAX_DISTANCE = 2048
CONV_WIDTH = 4
CHUNK = 64
N_DEV = 8
VMEM_LIMIT = 56 * 1024 * 1024

ADAM_LR, ADAM_B1, ADAM_B2, ADAM_EPS, ADAM_WD, ADAM_STEP = 0.001, 0.9, 0.999, 1e-08, 0.01, 10


def _cparams(sem):
    return pltpu.CompilerParams(dimension_semantics=sem, vmem_limit_bytes=VMEM_LIMIT)


def _mm(a, b, mode, out_dtype, tm, tn, tk, name):
    if mode == "nn":
        (m, k), (_, n) = a.shape, b.shape
        a_spec = pl.BlockSpec((tm, tk), lambda j, i, kk: (i, kk))
        b_spec = pl.BlockSpec((tk, tn), lambda j, i, kk: (kk, j))
        dims = (((1,), (0,)), ((), ()))
    elif mode == "nt":
        (m, k), (n, _) = a.shape, b.shape
        a_spec = pl.BlockSpec((tm, tk), lambda j, i, kk: (i, kk))
        b_spec = pl.BlockSpec((tn, tk), lambda j, i, kk: (j, kk))
        dims = (((1,), (1,)), ((), ()))
    else:
        (k, m), (_, n) = a.shape, b.shape
        a_spec = pl.BlockSpec((tk, tm), lambda j, i, kk: (kk, i))
        b_spec = pl.BlockSpec((tk, tn), lambda j, i, kk: (kk, j))
        dims = (((0,), (0,)), ((), ()))
    assert m % tm == 0 and n % tn == 0 and k % tk == 0, (name, m, n, k, tm, tn, tk)
    nk = k // tk

    def body(a_ref, b_ref, o_ref, acc_ref):
        kk = pl.program_id(2)

        @pl.when(kk == 0)
        def _():
            acc_ref[...] = jnp.zeros_like(acc_ref)

        acc_ref[...] += lax.dot_general(a_ref[...].astype(bf16), b_ref[...].astype(bf16), dims,
                                        preferred_element_type=f32)

        @pl.when(kk == nk - 1)
        def _():
            o_ref[...] = acc_ref[...].astype(o_ref.dtype)

    return pl.pallas_call(
        body, name=name,
        grid=(n // tn, m // tm, nk),
        in_specs=[a_spec, b_spec],
        out_specs=pl.BlockSpec((tm, tn), lambda j, i, kk: (i, j)),
        out_shape=jax.ShapeDtypeStruct((m, n), out_dtype),
        scratch_shapes=[pltpu.VMEM((tm, tn), f32)],
        compiler_params=_cparams(("parallel", "parallel", "arbitrary")),
    )(a, b)


TOK_TILE = 512


def _row_spec(width, tile=TOK_TILE):
    return pl.BlockSpec((tile, width), lambda i: (i, 0))


def _vec_spec(width, rows=1):
    return pl.BlockSpec((rows, width), lambda i: (0, 0))


def _ln_mod_fwd(x, gain, sc, sh, name):
    s, d = x.shape

    def body(x_ref, g_ref, sc_ref, sh_ref, h_ref):
        xv = x_ref[...]
        rstd = lax.rsqrt(jnp.mean(xv * xv, axis=-1, keepdims=True) + EPS)
        h = (xv * rstd) * g_ref[...] * (1.0 + sc_ref[...]) + sh_ref[...]
        h_ref[...] = h.astype(bf16)

    return pl.pallas_call(
        body, name=name, grid=(s // TOK_TILE,),
        in_specs=[_row_spec(d), _vec_spec(d), _vec_spec(d), _vec_spec(d)],
        out_specs=_row_spec(d),
        out_shape=jax.ShapeDtypeStruct((s, d), bf16),
        compiler_params=_cparams(("parallel",)),
    )(x, gain, sc, sh)


def _resid_ln_mod_fwd(x, y, gate, gain, sc, sh, name):
    s, d = x.shape

    def body(x_ref, y_ref, gt_ref, g_ref, sc_ref, sh_ref, x1_ref, h_ref):
        x1 = x_ref[...] + gt_ref[...] * y_ref[...]
        x1_ref[...] = x1
        rstd = lax.rsqrt(jnp.mean(x1 * x1, axis=-1, keepdims=True) + EPS)
        h = (x1 * rstd) * g_ref[...] * (1.0 + sc_ref[...]) + sh_ref[...]
        h_ref[...] = h.astype(bf16)

    return pl.pallas_call(
        body, name=name, grid=(s // TOK_TILE,),
        in_specs=[_row_spec(d), _row_spec(d)] + [_vec_spec(d)] * 4,
        out_specs=[_row_spec(d), _row_spec(d)],
        out_shape=[jax.ShapeDtypeStruct((s, d), f32), jax.ShapeDtypeStruct((s, d), bf16)],
        compiler_params=_cparams(("parallel",)),
    )(x, y, gate, gain, sc, sh)


def _swiglu_fwd(gu, name):
    s = gu.shape[0]

    def body(g_ref, u_ref, a_ref):
        g = g_ref[...]
        a_ref[...] = (g * jax.nn.sigmoid(g) * u_ref[...]).astype(bf16)

    tile = 256
    return pl.pallas_call(
        body, name=name, grid=(s // tile,),
        in_specs=[pl.BlockSpec((tile, D_FF), lambda i: (i, 0)), pl.BlockSpec((tile, D_FF), lambda i: (i, 1))],
        out_specs=pl.BlockSpec((tile, D_FF), lambda i: (i, 0)),
        out_shape=jax.ShapeDtypeStruct((s, D_FF), bf16),
        compiler_params=_cparams(("parallel",)),
    )(gu, gu)


def _swiglu_bwd(gu, dact, name):
    s = gu.shape[0]

    def body(g_ref, u_ref, da_ref, o_ref):
        g = g_ref[...]
        sg = jax.nn.sigmoid(g)
        da = da_ref[...]
        o_ref[:, D_FF:] = (da * g * sg).astype(bf16)
        o_ref[:, :D_FF] = (da * u_ref[...] * sg * (1.0 + g * (1.0 - sg))).astype(bf16)

    tile = 256
    return pl.pallas_call(
        body, name=name, grid=(s // tile,),
        in_specs=[pl.BlockSpec((tile, D_FF), lambda i: (i, 0)), pl.BlockSpec((tile, D_FF), lambda i: (i, 1)),
                  pl.BlockSpec((tile, D_FF), lambda i: (i, 0))],
        out_specs=pl.BlockSpec((tile, 2 * D_FF), lambda i: (i, 0)),
        out_shape=jax.ShapeDtypeStruct((s, 2 * D_FF), bf16),
        compiler_params=_cparams(("parallel",)),
    )(gu, gu, dact)


def _acc_spec(width):
    return pl.BlockSpec((1, width), lambda i: (0, 0))


def _final_loss_bwd(x1, y2, gate2, final_g, target, name):
    s, d = x1.shape

    def body(x1_ref, y2_ref, gt_ref, fg_ref, tg_ref, dx2_ref, dy2_ref, loss_ref, dfg_ref, dgt_ref):
        @pl.when(pl.program_id(0) == 0)
        def _():
            loss_ref[...] = jnp.zeros_like(loss_ref)
            dfg_ref[...] = jnp.zeros_like(dfg_ref)
            dgt_ref[...] = jnp.zeros_like(dgt_ref)

        y2 = y2_ref[...]
        gt = gt_ref[...]
        fg = fg_ref[...]
        x2 = x1_ref[...] + gt * y2
        rstd = lax.rsqrt(jnp.mean(x2 * x2, axis=-1, keepdims=True) + EPS)
        xn = x2 * rstd
        err = xn * fg - tg_ref[...]
        row = jnp.sum(err * err, axis=-1, keepdims=True) * (0.5 / d)
        loss_ref[...] += jnp.sum(row, axis=0, keepdims=True) + jnp.zeros_like(loss_ref)
        dout = err * (1.0 / d)
        dfg_ref[...] += jnp.sum(dout * xn, axis=0, keepdims=True)
        dxn = dout * fg
        dx2 = rstd * (dxn - xn * jnp.mean(dxn * xn, axis=-1, keepdims=True))
        dx2_ref[...] = dx2
        dgt_ref[...] += jnp.sum(dx2 * y2, axis=0, keepdims=True)
        dy2_ref[...] = (gt * dx2).astype(bf16)

    return pl.pallas_call(
        body, name=name, grid=(s // TOK_TILE,),
        in_specs=[_row_spec(d), _row_spec(d), _vec_spec(d), _vec_spec(d), _row_spec(d)],
        out_specs=[_row_spec(d), _row_spec(d), _acc_spec(128), _acc_spec(d), _acc_spec(d)],
        out_shape=[jax.ShapeDtypeStruct((s, d), f32), jax.ShapeDtypeStruct((s, d), bf16),
                   jax.ShapeDtypeStruct((1, 128), f32), jax.ShapeDtypeStruct((1, d), f32),
                   jax.ShapeDtypeStruct((1, d), f32)],
        compiler_params=_cparams(("arbitrary",)),
    )(x1, y2, gate2, final_g, target)


def _ln_mod_bwd(xin, gain, sc, dh, dres, name, gate=None, y=None):
    s, d = xin.shape
    with_gate = gate is not None

    def body(*refs):
        if with_gate:
            (x_ref, g_ref, sc_ref, dh_ref, dr_ref, gt_ref, y_ref,
             dx_ref, dsh_ref, dsc_ref, dg_ref, dy_ref, dgt_ref) = refs
        else:
            x_ref, g_ref, sc_ref, dh_ref, dr_ref, dx_ref, dsh_ref, dsc_ref, dg_ref = refs

        @pl.when(pl.program_id(0) == 0)
        def _():
            dsh_ref[...] = jnp.zeros_like(dsh_ref)
            dsc_ref[...] = jnp.zeros_like(dsc_ref)
            dg_ref[...] = jnp.zeros_like(dg_ref)
            if with_gate:
                dgt_ref[...] = jnp.zeros_like(dgt_ref)

        xv = x_ref[...]
        g = g_ref[...]
        sc1 = 1.0 + sc_ref[...]
        dh = dh_ref[...]
        rstd = lax.rsqrt(jnp.mean(xv * xv, axis=-1, keepdims=True) + EPS)
        xn = xv * rstd
        dsh_ref[...] += jnp.sum(dh, axis=0, keepdims=True)
        dsc_ref[...] += jnp.sum(dh * (xn * g), axis=0, keepdims=True)
        dg_ref[...] += jnp.sum(dh * sc1 * xn, axis=0, keepdims=True)
        dxn = dh * sc1 * g
        dx = dr_ref[...] + rstd * (dxn - xn * jnp.mean(dxn * xn, axis=-1, keepdims=True))
        dx_ref[...] = dx
        if with_gate:
            dgt_ref[...] += jnp.sum(dx * y_ref[...], axis=0, keepdims=True)
            dy_ref[...] = (gt_ref[...] * dx).astype(bf16)

    in_specs = [_row_spec(d), _vec_spec(d), _vec_spec(d), _row_spec(d), _row_spec(d)]
    out_specs = [_row_spec(d), _acc_spec(d), _acc_spec(d), _acc_spec(d)]
    out_shape = [jax.ShapeDtypeStruct((s, d), f32)] + [jax.ShapeDtypeStruct((1, d), f32)] * 3
    args = [xin, gain, sc, dh, dres]
    if with_gate:
        in_specs += [_vec_spec(d), _row_spec(d)]
        out_specs += [_row_spec(d), _acc_spec(d)]
        out_shape += [jax.ShapeDtypeStruct((s, d), bf16), jax.ShapeDtypeStruct((1, d), f32)]
        args += [gate, y]
    return pl.pallas_call(
        body, name=name, grid=(s // TOK_TILE,),
        in_specs=in_specs, out_specs=out_specs, out_shape=out_shape,
        compiler_params=_cparams(("arbitrary",)),
    )(*args)


def _bucket_tables():
    import numpy as np
    qi = np.arange(BAND)[:, None]
    kj = np.arange(2 * BAND)[None, :]
    steps = qi + BAND - kj
    max_exact = N_BUCKETS // 2
    out = []
    for d in DILATIONS:
        dist = np.maximum(steps, 0) * d
        dist_f = np.maximum(dist, 1).astype(np.float32)
        large = max_exact + (np.log(dist_f / np.float32(max_exact)) / np.float32(math.log(MAX_DISTANCE / max_exact))
                             * np.float32(N_BUCKETS - max_exact)).astype(np.int32)
        out.append(np.where(dist < max_exact, dist, np.minimum(large, N_BUCKETS - 1)))
    return jnp.asarray(np.stack(out).astype(np.int32))


def _bias_tables(rel_bias, idx):
    def body(idx_ref, rb_ref, o_ref):
        h = pl.program_id(1)
        idxv = idx_ref[0]
        acc = jnp.zeros((BAND, 2 * BAND), f32)
        for b in range(N_BUCKETS):
            acc = jnp.where(idxv == b, rb_ref[b, h], acc)
        o_ref[0, 0] = acc

    return pl.pallas_call(
        body, name="attn_bias_tables", grid=(3, N_HEADS),
        in_specs=[pl.BlockSpec((1, BAND, 2 * BAND), lambda br, h: (br, 0, 0)),
                  pl.BlockSpec(memory_space=pltpu.SMEM)],
        out_specs=pl.BlockSpec((1, 1, BAND, 2 * BAND), lambda br, h: (br, h, 0, 0)),
        out_shape=jax.ShapeDtypeStruct((3, N_HEADS, BAND, 2 * BAND), f32),
        compiler_params=_cparams(("parallel", "parallel")),
    )(idx, rel_bias)


def _bias_grad(dbias, idx):
    def body(idx_ref, db_ref, o_ref):
        br = pl.program_id(1)

        @pl.when(br == 0)
        def _():
            o_ref[...] = jnp.zeros_like(o_ref)

        idxv = idx_ref[0]
        dbv = db_ref[0, 0]
        row = lax.broadcasted_iota(jnp.int32, (N_BUCKETS, 128), 0)
        acc = jnp.zeros((N_BUCKETS, 128), f32)
        for b in range(N_BUCKETS):
            sb = jnp.sum(jnp.sum(jnp.where(idxv == b, dbv, 0.0), axis=1, keepdims=True), axis=0, keepdims=True)
            acc = acc + jnp.where(row == b, sb, 0.0)
        o_ref[0] += acc

    return pl.pallas_call(
        body, name="attn_bias_grad", grid=(N_HEADS, 3),
        in_specs=[pl.BlockSpec((1, BAND, 2 * BAND), lambda h, br: (br, 0, 0)),
                  pl.BlockSpec((1, 1, BAND, 2 * BAND), lambda h, br: (br, h, 0, 0))],
        out_specs=pl.BlockSpec((1, N_BUCKETS, 128), lambda h, br: (h, 0, 0)),
        out_shape=jax.ShapeDtypeStruct((N_HEADS, N_BUCKETS, 128), f32),
        compiler_params=_cparams(("parallel", "arbitrary")),
    )(idx, dbias)


def _attn_masks():
    lane = lax.broadcasted_iota(jnp.int32, (BAND, 128), 1)
    m0 = lane < HEAD_DIM
    qi = lax.broadcasted_iota(jnp.int32, (BAND, 2 * BAND), 0)
    kj = lax.broadcasted_iota(jnp.int32, (BAND, 2 * BAND), 1)
    steps = qi + BAND - kj
    in_window = (steps >= 0) & (steps <= BAND)
    return m0, in_window, kj >= BAND


_NT = (((1,), (1,)), ((), ()))
_TN = (((0,), (0,)), ((), ()))
ATTN_ITEMS = PAD_UNIT // BAND
Q_COL, K_COL, V_COL = 0, 4, 8


def _attn_item_rows(j, d, c, cbase):
    r = lax.rem(j, d)
    b = lax.div(j, d)
    loc = b * (d * BAND) + r
    first = jnp.logical_and(c == 0, b == 0)
    start = cbase + loc
    pstart = jnp.where(first, start, start - d * BAND)
    return loc, start, pstart, first


def _attn_fwd(proj, bias):
    s = proj.shape[0]

    def body(q_ref, k_ref, v_ref, b_ref, y_ref, lse_ref, o_s, l_s):
        c = pl.program_id(1)
        cbase = pl.multiple_of(c * PAD_UNIT, PAD_UNIT)
        m0, in_window, cur_half = _attn_masks()
        for bi, d in enumerate(DILATIONS):
            def item(j, carry, bi=bi, d=d):
                loc, start, pstart, first = _attn_item_rows(j, d, c, cbase)
                q = q_ref[pl.ds(loc, BAND, stride=d), :]
                kk = jnp.concatenate([k_ref[pl.ds(pstart, BAND, stride=d), :],
                                      k_ref[pl.ds(start, BAND, stride=d), :]], axis=0).astype(bf16)
                vv = jnp.concatenate([v_ref[pl.ds(pstart, BAND, stride=d), :],
                                      v_ref[pl.ds(start, BAND, stride=d), :]], axis=0).astype(bf16)
                valid = in_window & jnp.logical_or(cur_half, jnp.logical_not(first))
                outs, lses = [], []
                for hh in range(2):
                    mh = m0 if hh == 0 else jnp.logical_not(m0)
                    qh = (jnp.where(mh, q, 0.0) * 0.125).astype(bf16)
                    sc = lax.dot_general(qh, kk, _NT, preferred_element_type=f32) + b_ref[bi, hh]
                    sc = jnp.where(valid, sc, NEG_INF)
                    mx = jnp.max(sc, axis=-1, keepdims=True)
                    e = jnp.exp(sc - mx)
                    l = jnp.sum(e, axis=-1, keepdims=True)
                    pv = jnp.dot(e.astype(bf16), vv, preferred_element_type=f32)
                    outs.append(pv / l)
                    lses.append(mx + jnp.log(l))
                o_s[bi, pl.ds(loc, BAND, stride=d), :] = jnp.where(m0, outs[0], outs[1])
                l_s[bi, pl.ds(loc, BAND, stride=d), :] = jnp.where(m0, lses[0], lses[1])
                return carry

            lax.fori_loop(0, ATTN_ITEMS, item, 0)

        def merge(t, carry):
            rows = pl.ds(pl.multiple_of(t * 256, 256), 256)
            ls = [l_s[i, rows, :] for i in range(3)]
            mx = jnp.maximum(jnp.maximum(ls[0], ls[1]), ls[2])
            ws = [jnp.exp(l - mx) for l in ls]
            tot = ws[0] + ws[1] + ws[2]
            y = (ws[0] * o_s[0, rows, :] + ws[1] * o_s[1, rows, :] + ws[2] * o_s[2, rows, :]) / tot
            y_ref[rows, :] = y
            lse_ref[rows, :] = mx + jnp.log(tot)
            return carry

        lax.fori_loop(0, PAD_UNIT // 256, merge, 0)

    chunk = lambda col: pl.BlockSpec((PAD_UNIT, 128), lambda p, c: (c, col + p))
    full = lambda col: pl.BlockSpec((s, 128), lambda p, c: (0, col + p))
    return pl.pallas_call(
        body, name="attn_fwd", grid=(N_HEADS // 2, s // PAD_UNIT),
        in_specs=[chunk(Q_COL), full(K_COL), full(V_COL),
                  pl.BlockSpec((3, 2, BAND, 2 * BAND), lambda p, c: (0, p, 0, 0))],
        out_specs=[chunk(0), chunk(0)],
        out_shape=[jax.ShapeDtypeStruct((s, GROUP_W), f32)] * 2,
        scratch_shapes=[pltpu.VMEM((3, PAD_UNIT, 128), f32)] * 2,
        compiler_params=_cparams(("parallel", "arbitrary")),
    )(proj, proj, proj, bias)


def _attn_bwd(proj, bias, y, lse, dycat):
    s = proj.shape[0]

    def body(q_ref, k_ref, v_ref, b_ref, y_ref, lse_ref, dy_ref, dq_ref, dk_ref, dv_ref, db_ref, dd_s):
        c = pl.program_id(1)
        cbase = pl.multiple_of(c * PAD_UNIT, PAD_UNIT)
        m0, in_window, cur_half = _attn_masks()

        @pl.when(c == 0)
        def _():
            dk_ref[...] = jnp.zeros_like(dk_ref)
            dv_ref[...] = jnp.zeros_like(dv_ref)
            db_ref[...] = jnp.zeros_like(db_ref)

        dq_ref[...] = jnp.zeros_like(dq_ref)

        def rowdot(t, carry):
            rows = pl.ds(pl.multiple_of(t * 256, 256), 256)
            prod = dy_ref[rows, :] * y_ref[rows, :]
            lane = lax.broadcasted_iota(jnp.int32, prod.shape, 1)
            h0 = lane < HEAD_DIM
            d0 = jnp.sum(jnp.where(h0, prod, 0.0), axis=-1, keepdims=True)
            d1 = jnp.sum(jnp.where(h0, 0.0, prod), axis=-1, keepdims=True)
            dd_s[rows, :] = jnp.where(h0, d0, d1)
            return carry

        lax.fori_loop(0, PAD_UNIT // 256, rowdot, 0)

        for bi, d in enumerate(DILATIONS):
            def item(j, carry, bi=bi, d=d):
                loc, start, pstart, first = _attn_item_rows(j, d, c, cbase)
                qrows = pl.ds(loc, BAND, stride=d)
                rows = pl.ds(start, BAND, stride=d)
                prows = pl.ds(pstart, BAND, stride=d)
                q = q_ref[qrows, :]
                do = dy_ref[qrows, :]
                lq = lse_ref[qrows, :]
                dq_ = dd_s[qrows, :]
                kk = jnp.concatenate([k_ref[prows, :], k_ref[rows, :]], axis=0).astype(bf16)
                vv = jnp.concatenate([v_ref[prows, :], v_ref[rows, :]], axis=0).astype(bf16)
                valid = in_window & jnp.logical_or(cur_half, jnp.logical_not(first))
                dqs = []
                dk = jnp.zeros((2 * BAND, 128), f32)
                dv = jnp.zeros((2 * BAND, 128), f32)
                for hh in range(2):
                    mh = m0 if hh == 0 else jnp.logical_not(m0)
                    col = slice(hh * HEAD_DIM, hh * HEAD_DIM + 1)
                    qh = (jnp.where(mh, q, 0.0) * 0.125).astype(bf16)
                    doh = jnp.where(mh, do, 0.0).astype(bf16)
                    sc = lax.dot_general(qh, kk, _NT, preferred_element_type=f32) + b_ref[bi, hh]
                    sc = jnp.where(valid, sc, NEG_INF)
                    p = jnp.exp(sc - lq[:, col])
                    dp = lax.dot_general(doh, vv, _NT, preferred_element_type=f32)
                    ds = p * (dp - dq_[:, col])
                    db_ref[bi, hh] += ds
                    dsb = ds.astype(bf16)
                    dqs.append(jnp.dot(dsb, kk, preferred_element_type=f32) * 0.125)
                    dk = dk + lax.dot_general(dsb, qh, _TN, preferred_element_type=f32)
                    dv = dv + lax.dot_general(p.astype(bf16), doh, _TN, preferred_element_type=f32)
                dq_ref[qrows, :] += jnp.where(m0, dqs[0], dqs[1])
                dk_ref[prows, :] += dk[:BAND]
                dk_ref[rows, :] += dk[BAND:]
                dv_ref[prows, :] += dv[:BAND]
                dv_ref[rows, :] += dv[BAND:]
                return carry

            lax.fori_loop(0, ATTN_ITEMS, item, 0)

    chunk = lambda col: pl.BlockSpec((PAD_UNIT, 128), lambda p, c: (c, col + p))
    full = lambda col: pl.BlockSpec((s, 128), lambda p, c: (0, col + p))
    bias_spec = pl.BlockSpec((3, 2, BAND, 2 * BAND), lambda p, c: (0, p, 0, 0))
    return pl.pallas_call(
        body, name="attn_bwd", grid=(N_HEADS // 2, s // PAD_UNIT),
        in_specs=[chunk(Q_COL), full(K_COL), full(V_COL), bias_spec, chunk(0), chunk(0), chunk(0)],
        out_specs=[chunk(0), full(0), full(0), bias_spec],
        out_shape=[jax.ShapeDtypeStruct((s, GROUP_W), f32)] * 3
        + [jax.ShapeDtypeStruct((3, N_HEADS, BAND, 2 * BAND), f32)],
        scratch_shapes=[pltpu.VMEM((PAD_UNIT, 128), f32)],
        compiler_params=_cparams(("parallel", "arbitrary")),
    )(proj, proj, proj, bias, y, lse, dycat)


_HI = lax.Precision.HIGHEST
DELTA_COL = 1536
Z_COL = 3072
BA_BLOCK = 28
DELTA_ROWS = 512


def _hdot(a, b):
    return jnp.dot(a, b, precision=_HI, preferred_element_type=f32)


_NN = (((1,), (0,)), ((), ()))
_DIMS = dict(nn=_NN, nt=_NT, tn=_TN)


@functools.partial(jax.custom_vjp, nondiff_argnums=(2,))
def _mmx(a, b, mode):
    return lax.dot_general(a.astype(bf16), b.astype(bf16), _DIMS[mode], preferred_element_type=f32)


def _mmx_fwd(a, b, mode):
    return _mmx(a, b, mode), (a, b)


def _mmx_bwd(mode, res, g):
    a, b = res
    if mode == "nn":
        return _mmx(g, b, "nt"), _mmx(a, g, "tn")
    if mode == "nt":
        return _mmx(g, b, "nn"), _mmx(g, a, "tn")
    return _mmx(b, g, "nt"), _mmx(a, g, "nn")


_mmx.defvjp(_mmx_fwd, _mmx_bwd)


def _pair_iota():
    row = lax.broadcasted_iota(jnp.int32, (CHUNK, 128), 0)
    lane = lax.broadcasted_iota(jnp.int32, (CHUNK, 128), 1)
    return row, lane, lane & (CHUNK - 1)


def _bd(x):
    _, lane, _ = _pair_iota()
    m0 = lane < CHUNK
    return jnp.concatenate([jnp.where(m0, x, 0.0), jnp.where(m0, 0.0, x)], axis=0)


def _pmm(a, b):
    return _mmx(a, _bd(b), "nn")


def _ntp(x, y):
    return _mmx(x, _bd(y), "nt")


def _tnp(x, y):
    full = _mmx(x, y, "tn")
    _, lane, _ = _pair_iota()
    return jnp.where(lane < CHUNK, full[:CHUNK], full[CHUNK:])


def _tri_inv(a):
    row, lane, jj = _pair_iota()
    eye = jnp.where(row == jj, 1.0, 0.0).astype(f32)

    def same_block(log2b):
        return (row >> log2b) == (jj >> log2b)

    dgl = jnp.where(same_block(3), a, 0.0)
    d2 = _pmm(dgl, dgl)
    d4 = _pmm(d2, d2)
    t = _pmm(_pmm(eye - dgl, eye + d2), eye + d4)
    for lb in (3, 4, 5):
        off = jnp.where(same_block(lb + 1) & jnp.logical_not(same_block(lb)), a, 0.0)
        t = t - _pmm(_pmm(t, off), t)
    return t


@jax.custom_vjp
def _solve2(a, xv, xk):
    t = _tri_inv(a)
    return _pmm(t, xv), _pmm(t, xk)


def _solve2_fwd(a, xv, xk):
    t = _tri_inv(a)
    u, w = _pmm(t, xv), _pmm(t, xk)
    return (u, w), (t, u, w)


def _solve2_bwd(res, cts):
    t, u, w = res
    du, dw = cts
    dxv = _tnp(t, du)
    dxk = _tnp(t, dw)
    return -(_ntp(dxv, u) + _ntp(dxk, w)), dxv, dxk


_solve2.defvjp(_solve2_fwd, _solve2_bwd)


def _chunk_pre(qp, kp, vp, bp, gcum):
    row, lane, jj = _pair_iota()
    causal = row >= jj
    strict = row > jj
    rsel = jnp.sum(jnp.where(row == jj, gcum, 0.0), axis=0, keepdims=True)
    decay = jnp.where(causal, jnp.exp(jnp.where(causal, gcum - rsel, 0.0)), 0.0)
    kb = kp * bp
    kd = _bd(kp)
    a = jnp.where(strict, _mmx(kb, kd, "nt") * decay, 0.0)
    eg = jnp.exp(gcum)
    u, w = _solve2(a, vp * bp, kb * eg)
    qk = jnp.where(causal, _mmx(qp, kd, "nt") * decay, 0.0)
    glast = jnp.sum(jnp.where(row == CHUNK - 1, gcum, 0.0), axis=0, keepdims=True)
    return u, w, qp * eg, kp * jnp.exp(glast - gcum), qk, jnp.exp(glast)


def _chunk_post(u, w, qt, kh, qk, gam, sp):
    sd = _bd(sp)
    vnew = u - _mmx(w, sd, "nn")
    o = _mmx(qt, sd, "nn") + _pmm(qk, vnew)
    return o, gam * sp + _tnp(kh, vnew)


def _pair_spec(rows=DELTA_ROWS):
    return pl.BlockSpec((rows, 128), lambda i, p: (i, p))


def _delta_chunk_pre(qn, kn, sv, beta, g):
    s = qn.shape[0]
    nck = DELTA_ROWS // CHUNK

    def body(q_ref, k_ref, v_ref, b_ref, g_ref, u_ref, w_ref, qt_ref, kh_ref, qk_ref, gm_ref):
        def chunk(ci, carry):
            rows = pl.ds(pl.multiple_of(ci * CHUNK, CHUNK), CHUNK)
            u, w, qt, kh, qk, gam = _chunk_pre(q_ref[rows, :], k_ref[rows, :], v_ref[rows, :],
                                                b_ref[rows, :], g_ref[rows, :])
            u_ref[rows, :] = u
            w_ref[rows, :] = w
            qt_ref[rows, :] = qt
            kh_ref[rows, :] = kh
            qk_ref[rows, :] = qk
            gm_ref[pl.ds(pl.multiple_of(ci * 8, 8), 8), :] = jnp.broadcast_to(gam, (8, 128))
            return carry

        lax.fori_loop(0, nck, chunk, 0, unroll=2)

    v_spec = pl.BlockSpec((DELTA_ROWS, 128), lambda i, p: (i, 8 + p))
    return pl.pallas_call(
        body, name="delta_chunk_pre", grid=(s // DELTA_ROWS, 4),
        in_specs=[_pair_spec(), _pair_spec(), v_spec, _pair_spec(), _pair_spec()],
        out_specs=[_pair_spec()] * 5 + [_pair_spec(nck * 8)],
        out_shape=[jax.ShapeDtypeStruct((s, GROUP_W), f32)] * 5 + [jax.ShapeDtypeStruct((s // 8, GROUP_W), f32)],
        compiler_params=_cparams(("parallel", "parallel")),
    )(qn, kn, sv, beta, g)


def _delta_scan_fwd(u, w, qt, kh, qk, gm):
    s = u.shape[0]
    nck = DELTA_ROWS // CHUNK

    def body(u_ref, w_ref, qt_ref, kh_ref, qk_ref, gm_ref, o_ref, ss_ref, st):
        @pl.when(pl.program_id(0) == 0)
        def _():
            st[...] = jnp.zeros_like(st)

        def chunk(ci, carry):
            rows = pl.ds(pl.multiple_of(ci * CHUNK, CHUNK), CHUNK)
            grow = pl.ds(pl.multiple_of(ci * 8, 8), 1)
            for p in range(4):
                lanes = slice(p * 128, (p + 1) * 128)
                sp = st[p]
                ss_ref[rows, lanes] = sp
                o, s2 = _chunk_post(u_ref[rows, lanes], w_ref[rows, lanes], qt_ref[rows, lanes],
                                    kh_ref[rows, lanes], qk_ref[rows, lanes], gm_ref[grow, lanes], sp)
                o_ref[rows, lanes] = o
                st[p] = s2
            return carry

        lax.fori_loop(0, nck, chunk, 0)

    spec = pl.BlockSpec((DELTA_ROWS, GROUP_W), lambda i: (i, 0))
    gspec = pl.BlockSpec((nck * 8, GROUP_W), lambda i: (i, 0))
    return pl.pallas_call(
        body, name="delta_scan_fwd", grid=(s // DELTA_ROWS,),
        in_specs=[spec] * 5 + [gspec],
        out_specs=[spec, spec],
        out_shape=[jax.ShapeDtypeStruct((s, GROUP_W), f32)] * 2,
        scratch_shapes=[pltpu.VMEM((4, CHUNK, 128), f32)],
        compiler_params=_cparams(("arbitrary",)),
    )(u, w, qt, kh, qk, gm)


def _delta_scan_bwd(w, qt, kh, qk, gm, do):
    s = w.shape[0]
    nck = DELTA_ROWS // CHUNK
    nb = s // DELTA_ROWS

    def body(w_ref, qt_ref, kh_ref, qk_ref, gm_ref, do_ref, dso_ref, dst):
        @pl.when(pl.program_id(0) == 0)
        def _():
            dst[...] = jnp.zeros_like(dst)

        def chunk(t, carry):
            ci = nck - 1 - t
            rows = pl.ds(pl.multiple_of(ci * CHUNK, CHUNK), CHUNK)
            grow = pl.ds(pl.multiple_of(ci * 8, 8), 1)
            for p in range(4):
                lanes = slice(p * 128, (p + 1) * 128)
                ds = dst[p]
                dso_ref[rows, lanes] = ds
                do = do_ref[rows, lanes]
                dvn = _tnp(qk_ref[rows, lanes], do) + _pmm(kh_ref[rows, lanes], ds)
                dst[p] = _tnp(qt_ref[rows, lanes], do) + gm_ref[grow, lanes] * ds - _tnp(w_ref[rows, lanes], dvn)
            return carry

        lax.fori_loop(0, nck, chunk, 0)

    spec = pl.BlockSpec((DELTA_ROWS, GROUP_W), lambda i: (nb - 1 - i, 0))
    gspec = pl.BlockSpec((nck * 8, GROUP_W), lambda i: (nb - 1 - i, 0))
    return pl.pallas_call(
        body, name="delta_scan_bwd", grid=(nb,),
        in_specs=[spec] * 4 + [gspec, spec],
        out_specs=spec,
        out_shape=jax.ShapeDtypeStruct((s, GROUP_W), f32),
        scratch_shapes=[pltpu.VMEM((4, CHUNK, 128), f32)],
        compiler_params=_cparams(("arbitrary",)),
    )(w, qt, kh, qk, gm, do)


def _delta_chunk_bwd(qn, kn, sv, beta, g, ss, dso, do):
    s = qn.shape[0]
    nck = DELTA_ROWS // CHUNK

    def body(q_ref, k_ref, v_ref, b_ref, g_ref, ss_ref, dso_ref, do_ref, dq_ref, dk_ref, dv_ref, db_ref, dg_ref):
        def chunk(ci, carry):
            rows = pl.ds(pl.multiple_of(ci * CHUNK, CHUNK), CHUNK)
            sp = ss_ref[rows, :]

            def fn(q, k, v, b, gg):
                return _chunk_post(*_chunk_pre(q, k, v, b, gg), sp)

            _, vjp = jax.vjp(fn, q_ref[rows, :], k_ref[rows, :], v_ref[rows, :], b_ref[rows, :], g_ref[rows, :])
            dq, dk, dv, db, dg = vjp((do_ref[rows, :], dso_ref[rows, :]))
            dq_ref[rows, :] = dq
            dk_ref[rows, :] = dk
            dv_ref[rows, :] = dv
            db_ref[rows, :] = db
            dg_ref[rows, :] = dg
            return carry

        lax.fori_loop(0, nck, chunk, 0, unroll=2)

    v_spec = pl.BlockSpec((DELTA_ROWS, 128), lambda i, p: (i, 8 + p))
    return pl.pallas_call(
        body, name="delta_chunk_bwd", grid=(s // DELTA_ROWS, 4),
        in_specs=[_pair_spec(), _pair_spec(), v_spec] + [_pair_spec()] * 5,
        out_specs=[_pair_spec()] * 5,
        out_shape=[jax.ShapeDtypeStruct((s, GROUP_W), f32)] * 5,
        compiler_params=_cparams(("parallel", "parallel")),
    )(qn, kn, sv, beta, g, ss, dso, do)


def _head_sum_matrix():
    r = lax.broadcasted_iota(jnp.int32, (GROUP_W, GROUP_W), 0)
    c = lax.broadcasted_iota(jnp.int32, (GROUP_W, GROUP_W), 1)
    return jnp.where((r >> 6) == (c >> 6), 1.0, 0.0).astype(f32)


def _softplus(x):
    return jnp.maximum(x, 0.0) + jnp.log(1.0 + jnp.exp(-jnp.abs(x)))


def _prep_fn(sq, sk, ba, alog_e, dt_e):
    hs = _head_sum_matrix()
    qn = sq * lax.rsqrt(_hdot(sq * sq, hs) + EPS) * (HEAD_DIM ** -0.5)
    kn = sk * lax.rsqrt(_hdot(sk * sk, hs) + EPS)
    r = lax.broadcasted_iota(jnp.int32, (128, GROUP_W), 0)
    c = lax.broadcasted_iota(jnp.int32, (128, GROUP_W), 1) >> 6
    bl = _hdot(ba, jnp.where(r == c, 1.0, 0.0).astype(f32))
    al = _hdot(ba, jnp.where(r == c + N_HEADS, 1.0, 0.0).astype(f32))
    beta = jax.nn.sigmoid(bl)
    g = -jnp.exp(alog_e) * _softplus(al + dt_e)
    ri = lax.broadcasted_iota(jnp.int32, (TOK_TILE, TOK_TILE), 0)
    ci = lax.broadcasted_iota(jnp.int32, (TOK_TILE, TOK_TILE), 1)
    within = jnp.where(((ri >> 6) == (ci >> 6)) & (ri >= ci), 1.0, 0.0).astype(f32)
    return qn, kn, beta, _hdot(within, g)


def _gnorm_fn(o, z, ng_e):
    ms = _hdot(o * o, _head_sum_matrix()) * (1.0 / HEAD_DIM)
    return o * lax.rsqrt(ms + EPS) * ng_e * (z * jax.nn.sigmoid(z))


def _tok_spec(width, col):
    return pl.BlockSpec((TOK_TILE, width), lambda i: (i, col))


def _conv_taps(xs_ref, w_ref, base, n):
    acc = w_ref[CONV_WIDTH - 1:CONV_WIDTH, :] * xs_ref[pl.ds(base, n), :]
    for j in range(CONV_WIDTH - 1):
        acc = acc + w_ref[j:j + 1, :] * xs_ref[pl.ds(base - (CONV_WIDTH - 1) + j, n), :]
    return acc


def _conv_silu_fwd(proj, conv_w):
    s = proj.shape[0]
    wd = 3 * GROUP_W
    hb = TOK_TILE // 8

    def body(x_ref, halo_ref, w_ref, o_ref, xs):
        xs[0:8, :] = jnp.where(pl.program_id(0) > 0, halo_ref[...], 0.0)
        xs[8:, :] = x_ref[...]
        y = _conv_taps(xs, w_ref, 8, TOK_TILE)
        o_ref[...] = y * jax.nn.sigmoid(y)

    return pl.pallas_call(
        body, name="delta_conv_fwd", grid=(s // TOK_TILE,),
        in_specs=[_tok_spec(wd, 1), pl.BlockSpec((8, wd), lambda i: (jnp.maximum(i * hb - 1, 0), 1)),
                  pl.BlockSpec((CONV_WIDTH, wd), lambda i: (0, 0))],
        out_specs=_tok_spec(wd, 0),
        out_shape=jax.ShapeDtypeStruct((s, wd), f32),
        scratch_shapes=[pltpu.VMEM((TOK_TILE + 8, wd), f32)],
        compiler_params=_cparams(("parallel",)),
    )(proj, proj, conv_w)


def _conv_silu_bwd(proj, conv_w, ds):
    s = proj.shape[0]
    wd = 3 * GROUP_W
    hb = TOK_TILE // 8
    nt = s // TOK_TILE

    def body(x_ref, hp_ref, hn_ref, ds_ref, dsn_ref, w_ref, dx_ref, dw_ref, xs, dys):
        i = pl.program_id(0)

        @pl.when(i == 0)
        def _():
            dw_ref[...] = jnp.zeros_like(dw_ref)

        last = i == nt - 1
        xs[0:8, :] = jnp.where(i > 0, hp_ref[...], 0.0)
        xs[8:8 + TOK_TILE, :] = x_ref[...]
        xs[8 + TOK_TILE:, :] = jnp.where(last, 0.0, hn_ref[...])
        y = _conv_taps(xs, w_ref, 8, TOK_TILE)
        sg = jax.nn.sigmoid(y)
        dys[0:TOK_TILE, :] = ds_ref[...] * sg * (1.0 + y * (1.0 - sg))
        yn = _conv_taps(xs, w_ref, 8 + TOK_TILE, 8)
        sgn = jax.nn.sigmoid(yn)
        dys[TOK_TILE:, :] = jnp.where(last, 0.0, dsn_ref[...]) * sgn * (1.0 + yn * (1.0 - sgn))
        dy0 = dys[0:TOK_TILE, :]
        dx = w_ref[CONV_WIDTH - 1:CONV_WIDTH, :] * dy0
        for j in range(CONV_WIDTH - 1):
            dx = dx + w_ref[j:j + 1, :] * dys[pl.ds(CONV_WIDTH - 1 - j, TOK_TILE), :]
        dx_ref[...] = dx
        for j in range(CONV_WIDTH):
            dw_ref[j:j + 1, :] += jnp.sum(dy0 * xs[pl.ds(8 - (CONV_WIDTH - 1) + j, TOK_TILE), :],
                                          axis=0, keepdims=True)

    prev8 = lambda col: pl.BlockSpec((8, wd), lambda i: (jnp.maximum(i * hb - 1, 0), col))
    next8 = lambda col: pl.BlockSpec((8, wd), lambda i: (jnp.minimum((i + 1) * hb, s // 8 - 1), col))
    return pl.pallas_call(
        body, name="delta_conv_bwd", grid=(nt,),
        in_specs=[_tok_spec(wd, 1), prev8(1), next8(1), _tok_spec(wd, 0), next8(0),
                  pl.BlockSpec((CONV_WIDTH, wd), lambda i: (0, 0))],
        out_specs=[_tok_spec(wd, 0), pl.BlockSpec((CONV_WIDTH, wd), lambda i: (0, 0))],
        out_shape=[jax.ShapeDtypeStruct((s, wd), f32), jax.ShapeDtypeStruct((CONV_WIDTH, wd), f32)],
        scratch_shapes=[pltpu.VMEM((TOK_TILE + 16, wd), f32), pltpu.VMEM((TOK_TILE + 8, wd), f32)],
        compiler_params=_cparams(("arbitrary",)),
    )(proj, proj, proj, ds, ds, conv_w)


def _delta_prep_fwd(sconv, proj, alog_e, dt_e):
    s = sconv.shape[0]

    def body(sq_ref, sk_ref, ba_ref, al_ref, dt_ref, q_ref, k_ref, b_ref, g_ref):
        qn, kn, beta, g = _prep_fn(sq_ref[...], sk_ref[...], ba_ref[...], al_ref[...], dt_ref[...])
        q_ref[...] = qn
        k_ref[...] = kn
        b_ref[...] = beta
        g_ref[...] = g

    return pl.pallas_call(
        body, name="delta_prep_fwd", grid=(s // TOK_TILE,),
        in_specs=[_tok_spec(GROUP_W, 0), _tok_spec(GROUP_W, 1), _tok_spec(128, BA_BLOCK),
                  _vec_spec(GROUP_W), _vec_spec(GROUP_W)],
        out_specs=[_tok_spec(GROUP_W, 0)] * 4,
        out_shape=[jax.ShapeDtypeStruct((s, GROUP_W), f32)] * 4,
        compiler_params=_cparams(("parallel",)),
    )(sconv, sconv, proj, alog_e, dt_e)


def _delta_prep_bwd(sconv, proj, alog_e, dt_e, dqn, dkn, dbeta, dg):
    s = sconv.shape[0]

    def body(sq_ref, sk_ref, ba_ref, al_ref, dt_ref, dq_ref, dk_ref, db_ref, dg_ref,
             dsq_ref, dsk_ref, dba_ref, dal_ref, ddt_ref):
        @pl.when(pl.program_id(0) == 0)
        def _():
            dal_ref[...] = jnp.zeros_like(dal_ref)
            ddt_ref[...] = jnp.zeros_like(ddt_ref)

        _, vjp = jax.vjp(_prep_fn, sq_ref[...], sk_ref[...], ba_ref[...], al_ref[...], dt_ref[...])
        dsq, dsk, dba, dal, ddt = vjp((dq_ref[...], dk_ref[...], db_ref[...], dg_ref[...]))
        dsq_ref[...] = dsq
        dsk_ref[...] = dsk
        dba_ref[...] = dba
        dal_ref[...] += dal
        ddt_ref[...] += ddt

    return pl.pallas_call(
        body, name="delta_prep_bwd", grid=(s // TOK_TILE,),
        in_specs=[_tok_spec(GROUP_W, 0), _tok_spec(GROUP_W, 1), _tok_spec(128, BA_BLOCK),
                  _vec_spec(GROUP_W), _vec_spec(GROUP_W)] + [_tok_spec(GROUP_W, 0)] * 4,
        out_specs=[_tok_spec(GROUP_W, 0), _tok_spec(GROUP_W, 0), _tok_spec(128, 0),
                   _acc_spec(GROUP_W), _acc_spec(GROUP_W)],
        out_shape=[jax.ShapeDtypeStruct((s, GROUP_W), f32)] * 2 + [jax.ShapeDtypeStruct((s, 128), f32)]
        + [jax.ShapeDtypeStruct((1, GROUP_W), f32)] * 2,
        compiler_params=_cparams(("arbitrary",)),
    )(sconv, sconv, proj, alog_e, dt_e, dqn, dkn, dbeta, dg)


def _gnorm_fwd(o, proj, ng_e):
    s = o.shape[0]

    def body(o_ref, z_ref, g_ref, y_ref):
        y_ref[...] = _gnorm_fn(o_ref[...], z_ref[...], g_ref[...])

    return pl.pallas_call(
        body, name="delta_gnorm_fwd", grid=(s // TOK_TILE,),
        in_specs=[_tok_spec(GROUP_W, 0), _tok_spec(GROUP_W, Z_COL // GROUP_W), _vec_spec(GROUP_W)],
        out_specs=_tok_spec(GROUP_W, 0),
        out_shape=jax.ShapeDtypeStruct((s, GROUP_W), f32),
        compiler_params=_cparams(("parallel",)),
    )(o, proj, ng_e)


def _gnorm_bwd(o, proj, ng_e, dycat):
    s = o.shape[0]

    def body(o_ref, z_ref, g_ref, dy_ref, do_ref, dz_ref, dg_ref):
        @pl.when(pl.program_id(0) == 0)
        def _():
            dg_ref[...] = jnp.zeros_like(dg_ref)

        _, vjp = jax.vjp(_gnorm_fn, o_ref[...], z_ref[...], g_ref[...])
        do, dz, dg = vjp(dy_ref[...])
        do_ref[...] = do
        dz_ref[...] = dz
        dg_ref[...] += dg

    return pl.pallas_call(
        body, name="delta_gnorm_bwd", grid=(s // TOK_TILE,),
        in_specs=[_tok_spec(GROUP_W, 0), _tok_spec(GROUP_W, Z_COL // GROUP_W), _vec_spec(GROUP_W),
                  _tok_spec(GROUP_W, 1)],
        out_specs=[_tok_spec(GROUP_W, 0), _tok_spec(GROUP_W, 0), _acc_spec(GROUP_W)],
        out_shape=[jax.ShapeDtypeStruct((s, GROUP_W), f32)] * 2 + [jax.ShapeDtypeStruct((1, GROUP_W), f32)],
        compiler_params=_cparams(("arbitrary",)),
    )(o, proj, ng_e, dycat)


_MESH = pl.DeviceIdType.MESH
_ANY = pl.BlockSpec(memory_space=pl.ANY)
_VMEM = pl.BlockSpec(memory_space=pltpu.VMEM)


def _my_place():
    x, y, c = lax.axis_index("x"), lax.axis_index("y"), lax.axis_index("c")
    return x, y, c, 4 * x + 2 * y + c


def _peer(k, x, y, c):
    px = 1 - x if k & 4 else x
    py = 1 - y if k & 2 else y
    pc = 1 - c if k & 1 else c
    return (px, py, pc), 4 * px + 2 * py + pc


def _exchange_all(src_of_peer, dst_ref, send_sems, recv_sems, x, y, c, me):
    sent = []
    for k in range(1, N_DEV):
        dev, pidx = _peer(k, x, y, c)
        cp = pltpu.make_async_remote_copy(src_ref=src_of_peer(pidx), dst_ref=dst_ref.at[me],
                                          send_sem=send_sems.at[k - 1], recv_sem=recv_sems.at[k - 1],
                                          device_id=dev, device_id_type=_MESH)
        cp.start()
        sent.append(cp)
    for k in range(1, N_DEV):
        dev, pidx = _peer(k, x, y, c)
        pltpu.make_async_remote_copy(src_ref=src_of_peer(pidx), dst_ref=dst_ref.at[pidx],
                                     send_sem=send_sems.at[k - 1], recv_sem=recv_sems.at[k - 1],
                                     device_id=dev, device_id_type=_MESH).wait_recv()
    for cp in sent:
        cp.wait_send()


def _ada_exchange(cv8, w_ada, b_ada8):
    def body(cv_ref, w_ref, b_ref, call_ref, modp_ref, part_s, s1, r1, s2, r2):
        x, y, c, me = _my_place()
        call_ref[me] = cv_ref[...]
        _exchange_all(lambda pidx: cv_ref, call_ref, s1, r1, x, y, c, me)
        bias = b_ref[me]
        for j in range(N_DEV):
            cj = call_ref[j][:, :D_MODEL]
            part_s[j] = _hdot(cj * jax.nn.sigmoid(cj), w_ref[...]) + bias
        modp_ref[me] = part_s[me]
        _exchange_all(lambda pidx: part_s.at[pidx], modp_ref, s2, r2, x, y, c, me)

    nsh = w_ada.shape[1]
    return pl.pallas_call(
        body, name="ada_exchange",
        in_specs=[_VMEM, _VMEM, _VMEM], out_specs=[_VMEM, _VMEM],
        out_shape=[jax.ShapeDtypeStruct((N_DEV, 8, cv8.shape[1]), f32), jax.ShapeDtypeStruct((N_DEV, 8, nsh), f32)],
        scratch_shapes=[pltpu.VMEM((N_DEV, 8, nsh), f32)] + [pltpu.SemaphoreType.DMA((N_DEV - 1,))] * 4,
        compiler_params=pltpu.CompilerParams(vmem_limit_bytes=VMEM_LIMIT),
    )(cv8, w_ada, b_ada8)


def _all_to_all(arrs, name):
    n = len(arrs)

    def body(*refs):
        srcs, dsts = refs[:n], refs[n:2 * n]
        send_sems, recv_sems, local_sems = refs[2 * n:]
        x, y, c, me = _my_place()
        local = []
        for a in range(n):
            cp = pltpu.make_async_copy(srcs[a].at[me], dsts[a].at[me], local_sems.at[a])
            cp.start()
            local.append(cp)
        sent = []
        for a in range(n):
            for k in range(1, N_DEV):
                dev, pidx = _peer(k, x, y, c)
                cp = pltpu.make_async_remote_copy(src_ref=srcs[a].at[pidx], dst_ref=dsts[a].at[me],
                                                  send_sem=send_sems.at[a, k - 1], recv_sem=recv_sems.at[a, k - 1],
                                                  device_id=dev, device_id_type=_MESH)
                cp.start()
                sent.append(cp)
        for a in range(n):
            for k in range(1, N_DEV):
                dev, pidx = _peer(k, x, y, c)
                pltpu.make_async_remote_copy(src_ref=srcs[a].at[pidx], dst_ref=dsts[a].at[pidx],
                                             send_sem=send_sems.at[a, k - 1], recv_sem=recv_sems.at[a, k - 1],
                                             device_id=dev, device_id_type=_MESH).wait_recv()
        for cp in sent:
            cp.wait_send()
        for cp in local:
            cp.wait()

    return pl.pallas_call(
        body, name=name,
        in_specs=[_ANY] * n, out_specs=[_ANY] * n,
        out_shape=[jax.ShapeDtypeStruct(a.shape, a.dtype) for a in arrs],
        scratch_shapes=[pltpu.SemaphoreType.DMA((n, N_DEV - 1)), pltpu.SemaphoreType.DMA((n, N_DEV - 1)),
                        pltpu.SemaphoreType.DMA((n,))],
    )(*arrs)


def _all_gather_weights(shards):
    n = len(shards)

    def body(*refs):
        srcs, outs = refs[:n], refs[n:2 * n]
        send_sems, recv_sems, local_sems = refs[2 * n:]
        x, y, c, me = _my_place()
        sib = (x, y, 1 - c)
        chips = [(1 - x, y), (x, 1 - y), (1 - x, 1 - y)]

        def idx(px, py, pc):
            return 4 * px + 2 * py + pc

        def copy(a, k, block, to, src=None):
            rows = outs[a].at[idx(*block)]
            return pltpu.make_async_remote_copy(src_ref=rows if src is None else src, dst_ref=rows,
                                                send_sem=send_sems.at[a, k], recv_sem=recv_sems.at[a, k],
                                                device_id=to, device_id_type=_MESH)

        mine, first, passed = [], [], []
        for a in range(n):
            cp = pltpu.make_async_copy(srcs[a], outs[a].at[me], local_sems.at[a])
            cp.start()
            mine.append(cp)
            fa = [copy(a, 0, (x, y, c), sib, src=srcs[a])]
            fa += [copy(a, 1 + j, (x, y, c), (*chip, c), src=srcs[a]) for j, chip in enumerate(chips)]
            for cp in fa:
                cp.start()
            first += fa
        for a in range(n):
            for j, chip in enumerate(chips):
                copy(a, 1 + j, (*chip, c), (x, y, c)).wait_recv()
                cp = copy(a, 4 + j, (*chip, c), sib)
                cp.start()
                passed.append(cp)
        for a in range(n):
            copy(a, 0, (x, y, 1 - c), (x, y, c)).wait_recv()
            for j, chip in enumerate(chips):
                copy(a, 4 + j, (*chip, 1 - c), (x, y, c)).wait_recv()
        for cp in first + passed:
            cp.wait_send()
        for cp in mine:
            cp.wait()

    return pl.pallas_call(
        body, name="gather_weights",
        in_specs=[_ANY] * n, out_specs=[_ANY] * n,
        out_shape=[jax.ShapeDtypeStruct((N_DEV,) + a.shape, a.dtype) for a in shards],
        scratch_shapes=[pltpu.SemaphoreType.DMA((n, N_DEV - 1)), pltpu.SemaphoreType.DMA((n, N_DEV - 1)),
                        pltpu.SemaphoreType.DMA((n,))],
    )(*shards)


def _adamw_math(w, g, m, v):
    m2 = ADAM_B1 * m + (1.0 - ADAM_B1) * g
    v2 = ADAM_B2 * v + (1.0 - ADAM_B2) * (g * g)
    m_hat = m2 / (1.0 - ADAM_B1 ** ADAM_STEP)
    v_hat = v2 / (1.0 - ADAM_B2 ** ADAM_STEP)
    delta = -ADAM_LR * (m_hat / (jnp.sqrt(v_hat) + ADAM_EPS) + ADAM_WD * w)
    return delta, m2, v2


def _row_tile(rows):
    for t in (256, 128, 64, 32, 16, 8):
        if rows % t == 0:
            return t
    return rows


def _reduce_adamw(parts, w, m, v, name):
    _, r, cdim = parts.shape
    tr = _row_tile(r)

    def body(p_ref, w_ref, m_ref, v_ref, g_ref, d_ref, m2_ref, v2_ref):
        g = p_ref[0].astype(f32)
        for j in range(1, N_DEV):
            g = g + p_ref[j].astype(f32)
        delta, m2, v2 = _adamw_math(w_ref[...], g, m_ref[...], v_ref[...])
        g_ref[...] = g
        d_ref[...] = delta
        m2_ref[...] = m2
        v2_ref[...] = v2

    spec = pl.BlockSpec((tr, cdim), lambda i: (i, 0))
    return pl.pallas_call(
        body, name=name, grid=(r // tr,),
        in_specs=[pl.BlockSpec((N_DEV, tr, cdim), lambda i: (0, i, 0)), spec, spec, spec],
        out_specs=[spec] * 4,
        out_shape=[jax.ShapeDtypeStruct((r, cdim), f32)] * 4,
        compiler_params=_cparams(("parallel",)),
    )(parts, w, m, v)


def _adamw(w, g, m, v, name):
    r, cdim = w.shape
    tr = _row_tile(r)

    def body(w_ref, g_ref, m_ref, v_ref, d_ref, m2_ref, v2_ref):
        delta, m2, v2 = _adamw_math(w_ref[...], g_ref[...], m_ref[...], v_ref[...])
        d_ref[...] = delta
        m2_ref[...] = m2
        v2_ref[...] = v2

    spec = pl.BlockSpec((tr, cdim), lambda i: (i, 0))
    return pl.pallas_call(
        body, name=name, grid=(r // tr,),
        in_specs=[spec] * 4, out_specs=[spec] * 3,
        out_shape=[jax.ShapeDtypeStruct((r, cdim), f32)] * 3,
        compiler_params=_cparams(("parallel",)),
    )(w, g, m, v)


def _sum_devices(parts, name):
    _, r, cdim = parts.shape

    def body(p_ref, o_ref):
        g = p_ref[0]
        for j in range(1, N_DEV):
            g = g + p_ref[j]
        o_ref[...] = g

    return pl.pallas_call(
        body, name=name, out_shape=jax.ShapeDtypeStruct((r, cdim), f32),
        in_specs=[_VMEM], out_specs=_VMEM,
    )(parts)


def _ada_wgrad(c_all8, dmod_cols):
    nsh = dmod_cols.shape[1]

    def body(c_ref, d_ref, o_ref):
        cv = c_ref[...]
        o_ref[...] = lax.dot_general(cv * jax.nn.sigmoid(cv), d_ref[...], _TN, precision=_HI,
                                     preferred_element_type=f32)

    return pl.pallas_call(
        body, name="ada_wgrad", out_shape=jax.ShapeDtypeStruct((D_MODEL, nsh), f32),
        in_specs=[_VMEM, _VMEM], out_specs=_VMEM,
        compiler_params=pltpu.CompilerParams(vmem_limit_bytes=VMEM_LIMIT),
    )(c_all8, dmod_cols)


def _local_step(x, tgt, mod, norm_attn_g, w_in_p, rel_bias, conv_full, a_log, dt_bias, delta_norm_g,
                w_out_b, norm_ffn_g, w_gu_b, w_down_b, final_norm_g):
    s = x.shape[0]
    sh1, sc1, g1, sh2, sc2, g2 = [mod[:, i * D_MODEL:(i + 1) * D_MODEL] for i in range(6)]
    nag = norm_attn_g.reshape(1, D_MODEL)
    nfg = norm_ffn_g.reshape(1, D_MODEL)
    fg = final_norm_g.reshape(1, D_MODEL)
    idx = _bucket_tables()
    bias = _bias_tables(rel_bias, idx)
    alog_e = jnp.repeat(a_log.reshape(N_HEADS), HEAD_DIM)[None]
    dt_e = jnp.repeat(dt_bias.reshape(N_HEADS), HEAD_DIM)[None]
    ng_e = jnp.tile(delta_norm_g.reshape(HEAD_DIM), N_HEADS)[None]

    h1 = _ln_mod_fwd(x, nag, sc1, sh1, "ln1_fwd")
    proj = _mm(h1, w_in_p, "nn", f32, 512, 1280, 1024, "in_proj")
    y_attn, lse = _attn_fwd(proj, bias)
    sconv = _conv_silu_fwd(proj, conv_full)
    qn, kn, beta, g = _delta_prep_fwd(sconv, proj, alog_e, dt_e)
    u, w, qt, kh, qk, gm = _delta_chunk_pre(qn, kn, sconv, beta, g)
    o, ss = _delta_scan_fwd(u, w, qt, kh, qk, gm)
    y_delta = _gnorm_fwd(o, proj, ng_e)
    ycat = jnp.concatenate([y_attn, y_delta], axis=1).astype(bf16)
    y = _mm(ycat, w_out_b, "nn", f32, 512, 1024, 1024, "out_proj")
    x1, h2 = _resid_ln_mod_fwd(x, y, g1, nfg, sc2, sh2, "ln2_fwd")
    gu = _mm(h2, w_gu_b, "nn", f32, 512, 1408, 1024, "ffn_up")
    act = _swiglu_fwd(gu, "swiglu_fwd")
    y2 = _mm(act, w_down_b, "nn", f32, 512, 1024, D_FF, "ffn_down")
    dx2, dy2, loss, dfg, dg2 = _final_loss_bwd(x1, y2, g2, fg, tgt, "final_loss")

    dact = _mm(dy2, w_down_b, "nt", f32, 512, 1408, 1024, "ffn_down_dx")
    g_down = _mm(act, dy2, "tn", f32, 1408, 1024, 512, "ffn_down_dw")
    dgu = _swiglu_bwd(gu, dact, "swiglu_bwd")
    dh2 = _mm(dgu, w_gu_b, "nt", f32, 512, 1024, 1408, "ffn_up_dx")
    g_gu = _mm(h2, dgu, "tn", f32, 1024, 1408, 512, "ffn_up_dw")
    dx1, dsh2, dsc2, dnfg, dy, dg1 = _ln_mod_bwd(x1, nfg, sc2, dh2, dx2, "ln2_bwd", gate=g1, y=y)
    dycat = _mm(dy, w_out_b, "nt", f32, 512, 1024, 1024, "out_proj_dx")
    g_out = _mm(ycat, dy, "tn", f32, 1024, 1024, 512, "out_proj_dw")
    dq, dk, dv, dbias = _attn_bwd(proj, bias, y_attn, lse, dycat)
    g_rb = _bias_grad(dbias, idx)[:, :, 0].T
    do, dz, dng = _gnorm_bwd(o, proj, ng_e, dycat)
    dso = _delta_scan_bwd(w, qt, kh, qk, gm, do)
    dqn, dkn, dvd, dbeta, dgd = _delta_chunk_bwd(qn, kn, sconv, beta, g, ss, dso, do)
    dsq, dsk, dba, dal, ddt = _delta_prep_bwd(sconv, proj, alog_e, dt_e, dqn, dkn, dbeta, dgd)
    dxc, g_conv = _conv_silu_bwd(proj, conv_full, jnp.concatenate([dsq, dsk, dvd], axis=1))
    dproj = jnp.concatenate([dq, dk, dv, dxc, dz, dba, jnp.zeros((s, IN_PAD - BA_BLOCK * 128 - 128), f32)],
                            axis=1).astype(bf16)
    dh1 = _mm(dproj, w_in_p, "nt", f32, 512, 1024, 1280, "in_proj_dx")
    g_in = _mm(h1, dproj, "tn", f32, 1024, 1280, 512, "in_proj_dw")
    gx, dsh1, dsc1, dnag = _ln_mod_bwd(x, nag, sc1, dh1, dx1, "ln1_bwd")
    grads = dict(
        x=gx, mod=jnp.concatenate([dsh1, dsc1, dg1, dsh2, dsc2, dg2], axis=1),
        norm_attn_g=dnag, norm_ffn_g=dnfg, final_norm_g=dfg, rel_bias=g_rb, conv_w=g_conv,
        a_log=dal.reshape(N_HEADS, HEAD_DIM).sum(-1), dt_bias=ddt.reshape(N_HEADS, HEAD_DIM).sum(-1),
        delta_norm_g=dng.reshape(N_HEADS, HEAD_DIM).sum(0),
        w_in=g_in, w_out=g_out, w_gu=g_gu, w_down=g_down)
    return loss[0, 0], grads


MISC_OFF = dict(rel_bias=0, a_log=256, dt_bias=264, delta_norm_g=272)


def _misc_row(rel_bias, a_log, dt_bias, delta_norm_g):
    flat = jnp.concatenate([rel_bias.reshape(-1), a_log.reshape(-1), dt_bias.reshape(-1), delta_norm_g.reshape(-1)])
    return jnp.pad(flat, (0, D_MODEL - flat.shape[0]))[None]


def _pack_small(b_ada, nag, nfg, fng, rel_bias, a_log, dt_bias, dng, conv_shard):
    rows = [b_ada.reshape(6, D_MODEL), nag.reshape(1, D_MODEL), nfg.reshape(1, D_MODEL), fng.reshape(1, D_MODEL),
            _misc_row(rel_bias, a_log, dt_bias, dng),
            jnp.pad(conv_shard.reshape(-1), (0, D_MODEL - conv_shard.size))[None],
            jnp.zeros((5, D_MODEL), f32)]
    return jnp.concatenate(rows, axis=0)


def _unpack_small(p, conv_shape):
    misc = p[9]
    return dict(
        b_ada=p[0:6].reshape(1, 6 * D_MODEL), norm_attn_g=p[6:7], norm_ffn_g=p[7:8], final_norm_g=p[8],
        rel_bias=misc[0:256].reshape(N_BUCKETS, N_HEADS), a_log=misc[256:264].reshape(1, N_HEADS),
        dt_bias=misc[264:272].reshape(1, N_HEADS), delta_norm_g=misc[272:336].reshape(1, HEAD_DIM),
        conv_w=p[10, :conv_shape[1] * conv_shape[2]].reshape(conv_shape))


def kernel(x, c, w_ada, b_ada, norm_attn_g, w_in, rel_bias, conv_w, a_log, dt_bias, delta_norm_g, w_out, norm_ffn_g, w_gate, w_up, w_down, final_norm_g, loss_target, m_w_ada, m_b_ada, m_norm_attn_g, m_w_in, m_rel_bias, m_conv_w, m_a_log, m_dt_bias, m_delta_norm_g, m_w_out, m_norm_ffn_g, m_w_gate, m_w_up, m_w_down, m_final_norm_g, v_w_ada, v_b_ada, v_norm_attn_g, v_w_in, v_rel_bias, v_conv_w, v_a_log, v_dt_bias, v_delta_norm_g, v_w_out, v_norm_ffn_g, v_w_gate, v_w_up, v_w_down, v_final_norm_g):
    me = 4 * lax.axis_index("x") + 2 * lax.axis_index("y") + lax.axis_index("c")
    ada_sh = w_ada.shape[2]
    conv_sh = conv_w.shape[2]

    cv = jnp.concatenate([c[0], conv_w[0].reshape(-1)])
    cv8 = jnp.zeros((8, 2 * D_MODEL), f32).at[0, :cv.shape[0]].set(cv)
    b8 = jnp.broadcast_to(b_ada.reshape(N_DEV, 1, ada_sh), (N_DEV, 8, ada_sh))
    call, modp = _ada_exchange(cv8, w_ada[0], b8)
    mod = modp[:, 0, :].reshape(1, 6 * D_MODEL)
    c_all = call[:, 0, :D_MODEL]
    conv_full = call[:, 0, D_MODEL:D_MODEL + CONV_WIDTH * conv_sh].reshape(N_DEV, CONV_WIDTH, conv_sh)
    conv_full = conv_full.transpose(1, 0, 2).reshape(CONV_WIDTH, N_DEV * conv_sh)

    gw = _all_gather_weights([w_in[0].astype(bf16), w_out[0].astype(bf16), w_gate[0].astype(bf16),
                              w_up[0].astype(bf16), w_down[0].astype(bf16)])
    cols = lambda t: t.transpose(1, 0, 2).reshape(t.shape[1], N_DEV * t.shape[2])
    w_in_p = jnp.pad(cols(gw[0]), ((0, 0), (0, IN_PAD - IN_WIDTH)))
    w_out_b = gw[1].reshape(2 * GROUP_W, D_MODEL)
    w_gu_b = jnp.concatenate([cols(gw[2]), cols(gw[3])], axis=1)
    w_down_b = gw[4].reshape(D_FF, D_MODEL)

    loss_local, gr = _local_step(x[0], loss_target[0], mod, norm_attn_g, w_in_p, rel_bias, conv_full, a_log,
                                 dt_bias, delta_norm_g, w_out_b, norm_ffn_g, w_gu_b, w_down_b, final_norm_g)
    loss = lax.psum(loss_local, ("x", "y", "c"))

    small = jnp.concatenate([
        gr["mod"].reshape(6, D_MODEL), gr["norm_attn_g"], gr["norm_ffn_g"], gr["final_norm_g"],
        gr["conv_w"].reshape(6, D_MODEL),
        _misc_row(gr["rel_bias"], gr["a_log"], gr["dt_bias"], gr["delta_norm_g"])], axis=0)
    parts = _all_to_all([jnp.broadcast_to(small[None], (N_DEV,) + small.shape)], "small_gather")[0]
    tot = _sum_devices(parts, "small_sum")
    g_conv_full = tot[9:15].reshape(CONV_WIDTH, N_DEV * conv_sh)
    g_conv = lax.dynamic_slice(g_conv_full, (0, me * conv_sh), (CONV_WIDTH, conv_sh))
    misc = tot[15]
    g_small = _pack_small(tot[0:6], tot[6], tot[7], tot[8], misc[0:256], misc[256:264], misc[264:272],
                          misc[272:336], g_conv)
    pk = lambda pre: _pack_small(pre[0], pre[1], pre[2], pre[3], pre[4], pre[5], pre[6], pre[7], pre[8])
    w_small = pk((b_ada, norm_attn_g, norm_ffn_g, final_norm_g, rel_bias, a_log, dt_bias, delta_norm_g, conv_w))
    m_small = pk((m_b_ada, m_norm_attn_g, m_norm_ffn_g, m_final_norm_g, m_rel_bias, m_a_log, m_dt_bias,
                  m_delta_norm_g, m_conv_w))
    v_small = pk((v_b_ada, v_norm_attn_g, v_norm_ffn_g, v_final_norm_g, v_rel_bias, v_a_log, v_dt_bias,
                  v_delta_norm_g, v_conv_w))
    d_small, m2_small, v2_small = _adamw(w_small, g_small, m_small, v_small, "adamw_small")
    cshape = conv_w.shape
    G, Dl, M2, V2 = (_unpack_small(t, cshape) for t in (g_small, d_small, m2_small, v2_small))

    dmod_all = parts[:, 0:6, :].reshape(N_DEV, 6 * D_MODEL)
    dmod_cols = lax.dynamic_slice(dmod_all, (0, me * ada_sh), (N_DEV, ada_sh))
    g_ada = _ada_wgrad(c_all, dmod_cols)
    d_ada, m2_ada, v2_ada = _adamw(w_ada[0], g_ada, m_w_ada[0], v_w_ada[0], "adamw_w_ada")

    colblk = lambda t, n: t.reshape(t.shape[0], N_DEV, n).transpose(1, 0, 2)
    rowblk = lambda t: t.reshape(N_DEV, t.shape[0] // N_DEV, t.shape[1])
    n_in, n_ff = w_in.shape[2], w_gate.shape[2]
    sends = [colblk(gr["w_in"][:, :IN_WIDTH], n_in), rowblk(gr["w_out"]), colblk(gr["w_gu"][:, :D_FF], n_ff),
             colblk(gr["w_gu"][:, D_FF:], n_ff), rowblk(gr["w_down"])]
    recv = _all_to_all([t.astype(bf16) for t in sends], "grad_exchange")
    big = {}
    for name, p, w_, m_, v_ in (("w_in", recv[0], w_in, m_w_in, v_w_in), ("w_out", recv[1], w_out, m_w_out, v_w_out),
                                ("w_gate", recv[2], w_gate, m_w_gate, v_w_gate), ("w_up", recv[3], w_up, m_w_up, v_w_up),
                                ("w_down", recv[4], w_down, m_w_down, v_w_down)):
        big[name] = [t[None] for t in _reduce_adamw(p, w_[0], m_[0], v_[0], "reduce_adamw_" + name)]

    def leaf(i, name):
        if name == "w_ada":
            return (g_ada, d_ada, m2_ada, v2_ada)[i][None]
        if name in big:
            return big[name][i]
        return (G, Dl, M2, V2)[i][name]

    order = ["w_ada", "b_ada", "norm_attn_g", "w_in", "rel_bias", "conv_w", "a_log", "dt_bias", "delta_norm_g",
             "w_out", "norm_ffn_g", "w_gate", "w_up", "w_down", "final_norm_g"]
    outs = [loss, gr["x"][None]]
    for i in range(4):
        outs += [leaf(i, n) for n in order]
    return tuple(outs)
```

```python
import functools
import math

import jax
import jax.numpy as jnp
from jax import lax
from jax.experimental import pallas as pl
from jax.experimental.pallas import tpu as pltpu

f32 = jnp.float32
bf16 = jnp.bfloat16

D_MODEL = 1024
HEAD_DIM = 64
N_HEADS = 8
GROUP_W = 512
IN_WIDTH = 3600
IN_PAD = 3840
D_FF = 2816
EPS = 1e-6
NEG_INF = -1e30
BAND = 128
PAD_UNIT = 2048
DILATIONS = (1, 4, 16)
N_BUCKETS = 32
MAX_DISTANCE = 2048
CONV_WIDTH = 4
CHUNK = 64
N_DEV = 8
VMEM_LIMIT = 56 * 1024 * 1024

ADAM_LR, ADAM_B1, ADAM_B2, ADAM_EPS, ADAM_WD, ADAM_STEP = 0.001, 0.9, 0.999, 1e-08, 0.01, 10


def _cparams(sem):
    return pltpu.CompilerParams(dimension_semantics=sem, vmem_limit_bytes=VMEM_LIMIT)


def _mm(a, b, mode, out_dtype, tm, tn, tk, name):
    if mode == "nn":
        (m, k), (_, n) = a.shape, b.shape
        a_spec = pl.BlockSpec((tm, tk), lambda j, i, kk: (i, kk))
        b_spec = pl.BlockSpec((tk, tn), lambda j, i, kk: (kk, j))
        dims = (((1,), (0,)), ((), ()))
    elif mode == "nt":
        (m, k), (n, _) = a.shape, b.shape
        a_spec = pl.BlockSpec((tm, tk), lambda j, i, kk: (i, kk))
        b_spec = pl.BlockSpec((tn, tk), lambda j, i, kk: (j, kk))
        dims = (((1,), (1,)), ((), ()))
    else:
        (k, m), (_, n) = a.shape, b.shape
        a_spec = pl.BlockSpec((tk, tm), lambda j, i, kk: (kk, i))
        b_spec = pl.BlockSpec((tk, tn), lambda j, i, kk: (kk, j))
        dims = (((0,), (0,)), ((), ()))
    assert m % tm == 0 and n % tn == 0 and k % tk == 0, (name, m, n, k, tm, tn, tk)
    nk = k // tk

    def body(a_ref, b_ref, o_ref, acc_ref):
        kk = pl.program_id(2)

        @pl.when(kk == 0)
        def _():
            acc_ref[...] = jnp.zeros_like(acc_ref)

        acc_ref[...] += lax.dot_general(a_ref[...].astype(bf16), b_ref[...].astype(bf16), dims,
                                        preferred_element_type=f32)

        @pl.when(kk == nk - 1)
        def _():
            o_ref[...] = acc_ref[...].astype(o_ref.dtype)

    return pl.pallas_call(
        body, name=name,
        grid=(n // tn, m // tm, nk),
        in_specs=[a_spec, b_spec],
        out_specs=pl.BlockSpec((tm, tn), lambda j, i, kk: (i, j)),
        out_shape=jax.ShapeDtypeStruct((m, n), out_dtype),
        scratch_shapes=[pltpu.VMEM((tm, tn), f32)],
        compiler_params=_cparams(("parallel", "parallel", "arbitrary")),
    )(a, b)


TOK_TILE = 512


def _row_spec(width, tile=TOK_TILE):
    return pl.BlockSpec((tile, width), lambda i: (i, 0))


def _vec_spec(width, rows=1):
    return pl.BlockSpec((rows, width), lambda i: (0, 0))


def _ln_mod_fwd(x, gain, sc, sh, name):
    s, d = x.shape

    def body(x_ref, g_ref, sc_ref, sh_ref, h_ref):
        xv = x_ref[...]
        rstd = lax.rsqrt(jnp.mean(xv * xv, axis=-1, keepdims=True) + EPS)
        h = (xv * rstd) * g_ref[...] * (1.0 + sc_ref[...]) + sh_ref[...]
        h_ref[...] = h.astype(bf16)

    return pl.pallas_call(
        body, name=name, grid=(s // TOK_TILE,),
        in_specs=[_row_spec(d), _vec_spec(d), _vec_spec(d), _vec_spec(d)],
        out_specs=_row_spec(d),
        out_shape=jax.ShapeDtypeStruct((s, d), bf16),
        compiler_params=_cparams(("parallel",)),
    )(x, gain, sc, sh)


def _resid_ln_mod_fwd(x, y, gate, gain, sc, sh, name):
    s, d = x.shape

    def body(x_ref, y_ref, gt_ref, g_ref, sc_ref, sh_ref, x1_ref, h_ref):
        x1 = x_ref[...] + gt_ref[...] * y_ref[...]
        x1_ref[...] = x1
        rstd = lax.rsqrt(jnp.mean(x1 * x1, axis=-1, keepdims=True) + EPS)
        h = (x1 * rstd) * g_ref[...] * (1.0 + sc_ref[...]) + sh_ref[...]
        h_ref[...] = h.astype(bf16)

    return pl.pallas_call(
        body, name=name, grid=(s // TOK_TILE,),
        in_specs=[_row_spec(d), _row_spec(d)] + [_vec_spec(d)] * 4,
        out_specs=[_row_spec(d), _row_spec(d)],
        out_shape=[jax.ShapeDtypeStruct((s, d), f32), jax.ShapeDtypeStruct((s, d), bf16)],
        compiler_params=_cparams(("parallel",)),
    )(x, y, gate, gain, sc, sh)


def _swiglu_fwd(gu, name):
    s = gu.shape[0]

    def body(g_ref, u_ref, a_ref):
        g = g_ref[...]
        a_ref[...] = (g * jax.nn.sigmoid(g) * u_ref[...]).astype(bf16)

    tile = 256
    return pl.pallas_call(
        body, name=name, grid=(s // tile,),
        in_specs=[pl.BlockSpec((tile, D_FF), lambda i: (i, 0)), pl.BlockSpec((tile, D_FF), lambda i: (i, 1))],
        out_specs=pl.BlockSpec((tile, D_FF), lambda i: (i, 0)),
        out_shape=jax.ShapeDtypeStruct((s, D_FF), bf16),
        compiler_params=_cparams(("parallel",)),
    )(gu, gu)


def _swiglu_bwd(gu, dact, name):
    s = gu.shape[0]

    def body(g_ref, u_ref, da_ref, o_ref):
        g = g_ref[...]
        sg = jax.nn.sigmoid(g)
        da = da_ref[...]
        o_ref[:, D_FF:] = (da * g * sg).astype(bf16)
        o_ref[:, :D_FF] = (da * u_ref[...] * sg * (1.0 + g * (1.0 - sg))).astype(bf16)

    tile = 256
    return pl.pallas_call(
        body, name=name, grid=(s // tile,),
        in_specs=[pl.BlockSpec((tile, D_FF), lambda i: (i, 0)), pl.BlockSpec((tile, D_FF), lambda i: (i, 1)),
                  pl.BlockSpec((tile, D_FF), lambda i: (i, 0))],
        out_specs=pl.BlockSpec((tile, 2 * D_FF), lambda i: (i, 0)),
        out_shape=jax.ShapeDtypeStruct((s, 2 * D_FF), bf16),
        compiler_params=_cparams(("parallel",)),
    )(gu, gu, dact)


def _acc_spec(width):
    return pl.BlockSpec((1, width), lambda i: (0, 0))


def _final_loss_bwd(x1, y2, gate2, final_g, target, name):
    s, d = x1.shape

    def body(x1_ref, y2_ref, gt_ref, fg_ref, tg_ref, dx2_ref, dy2_ref, loss_ref, dfg_ref, dgt_ref):
        @pl.when(pl.program_id(0) == 0)
        def _():
            loss_ref[...] = jnp.zeros_like(loss_ref)
            dfg_ref[...] = jnp.zeros_like(dfg_ref)
            dgt_ref[...] = jnp.zeros_like(dgt_ref)

        y2 = y2_ref[...]
        gt = gt_ref[...]
        fg = fg_ref[...]
        x2 = x1_ref[...] + gt * y2
        rstd = lax.rsqrt(jnp.mean(x2 * x2, axis=-1, keepdims=True) + EPS)
        xn = x2 * rstd
        err = xn * fg - tg_ref[...]
        row = jnp.sum(err * err, axis=-1, keepdims=True) * (0.5 / d)
        loss_ref[...] += jnp.sum(row, axis=0, keepdims=True) + jnp.zeros_like(loss_ref)
        dout = err * (1.0 / d)
        dfg_ref[...] += jnp.sum(dout * xn, axis=0, keepdims=True)
        dxn = dout * fg
        dx2 = rstd * (dxn - xn * jnp.mean(dxn * xn, axis=-1, keepdims=True))
        dx2_ref[...] = dx2
        dgt_ref[...] += jnp.sum(dx2 * y2, axis=0, keepdims=True)
        dy2_ref[...] = (gt * dx2).astype(bf16)

    return pl.pallas_call(
        body, name=name, grid=(s // TOK_TILE,),
        in_specs=[_row_spec(d), _row_spec(d), _vec_spec(d), _vec_spec(d), _row_spec(d)],
        out_specs=[_row_spec(d), _row_spec(d), _acc_spec(128), _acc_spec(d), _acc_spec(d)],
        out_shape=[jax.ShapeDtypeStruct((s, d), f32), jax.ShapeDtypeStruct((s, d), bf16),
                   jax.ShapeDtypeStruct((1, 128), f32), jax.ShapeDtypeStruct((1, d), f32),
                   jax.ShapeDtypeStruct((1, d), f32)],
        compiler_params=_cparams(("arbitrary",)),
    )(x1, y2, gate2, final_g, target)


def _ln_mod_bwd(xin, gain, sc, dh, dres, name, gate=None, y=None):
    s, d = xin.shape
    with_gate = gate is not None

    def body(*refs):
        if with_gate:
            (x_ref, g_ref, sc_ref, dh_ref, dr_ref, gt_ref, y_ref,
             dx_ref, dsh_ref, dsc_ref, dg_ref, dy_ref, dgt_ref) = refs
        else:
            x_ref, g_ref, sc_ref, dh_ref, dr_ref, dx_ref, dsh_ref, dsc_ref, dg_ref = refs

        @pl.when(pl.program_id(0) == 0)
        def _():
            dsh_ref[...] = jnp.zeros_like(dsh_ref)
            dsc_ref[...] = jnp.zeros_like(dsc_ref)
            dg_ref[...] = jnp.zeros_like(dg_ref)
            if with_gate:
                dgt_ref[...] = jnp.zeros_like(dgt_ref)

        xv = x_ref[...]
        g = g_ref[...]
        sc1 = 1.0 + sc_ref[...]
        dh = dh_ref[...]
        rstd = lax.rsqrt(jnp.mean(xv * xv, axis=-1, keepdims=True) + EPS)
        xn = xv * rstd
        dsh_ref[...] += jnp.sum(dh, axis=0, keepdims=True)
        dsc_ref[...] += jnp.sum(dh * (xn * g), axis=0, keepdims=True)
        dg_ref[...] += jnp.sum(dh * sc1 * xn, axis=0, keepdims=True)
        dxn = dh * sc1 * g
        dx = dr_ref[...] + rstd * (dxn - xn * jnp.mean(dxn * xn, axis=-1, keepdims=True))
        dx_ref[...] = dx
        if with_gate:
            dgt_ref[...] += jnp.sum(dx * y_ref[...], axis=0, keepdims=True)
            dy_ref[...] = (gt_ref[...] * dx).astype(bf16)

    in_specs = [_row_spec(d), _vec_spec(d), _vec_spec(d), _row_spec(d), _row_spec(d)]
    out_specs = [_row_spec(d), _acc_spec(d), _acc_spec(d), _acc_spec(d)]
    out_shape = [jax.ShapeDtypeStruct((s, d), f32)] + [jax.ShapeDtypeStruct((1, d), f32)] * 3
    args = [xin, gain, sc, dh, dres]
    if with_gate:
        in_specs += [_vec_spec(d), _row_spec(d)]
        out_specs += [_row_spec(d), _acc_spec(d)]
        out_shape += [jax.ShapeDtypeStruct((s, d), bf16), jax.ShapeDtypeStruct((1, d), f32)]
        args += [gate, y]
    return pl.pallas_call(
        body, name=name, grid=(s // TOK_TILE,),
        in_specs=in_specs, out_specs=out_specs, out_shape=out_shape,
        compiler_params=_cparams(("arbitrary",)),
    )(*args)


def _bucket_tables():
    import numpy as np
    qi = np.arange(BAND)[:, None]
    kj = np.arange(2 * BAND)[None, :]
    steps = qi + BAND - kj
    max_exact = N_BUCKETS // 2
    out = []
    for d in DILATIONS:
        dist = np.maximum(steps, 0) * d
        dist_f = np.maximum(dist, 1).astype(np.float32)
        large = max_exact + (np.log(dist_f / np.float32(max_exact)) / np.float32(math.log(MAX_DISTANCE / max_exact))
                             * np.float32(N_BUCKETS - max_exact)).astype(np.int32)
        out.append(np.where(dist < max_exact, dist, np.minimum(large, N_BUCKETS - 1)))
    return jnp.asarray(np.stack(out).astype(np.int32))


def _bias_tables(rel_bias, idx):
    def body(idx_ref, rb_ref, o_ref):
        h = pl.program_id(1)
        idxv = idx_ref[0]
        acc = jnp.zeros((BAND, 2 * BAND), f32)
        for b in range(N_BUCKETS):
            acc = jnp.where(idxv == b, rb_ref[b, h], acc)
        o_ref[0, 0] = acc

    return pl.pallas_call(
        body, name="attn_bias_tables", grid=(3, N_HEADS),
        in_specs=[pl.BlockSpec((1, BAND, 2 * BAND), lambda br, h: (br, 0, 0)),
                  pl.BlockSpec(memory_space=pltpu.SMEM)],
        out_specs=pl.BlockSpec((1, 1, BAND, 2 * BAND), lambda br, h: (br, h, 0, 0)),
        out_shape=jax.ShapeDtypeStruct((3, N_HEADS, BAND, 2 * BAND), f32),
        compiler_params=_cparams(("parallel", "parallel")),
    )(idx, rel_bias)


def _bias_grad(dbias, idx):
    def body(idx_ref, db_ref, o_ref):
        br = pl.program_id(1)

        @pl.when(br == 0)
        def _():
            o_ref[...] = jnp.zeros_like(o_ref)

        idxv = idx_ref[0]
        dbv = db_ref[0, 0]
        row = lax.broadcasted_iota(jnp.int32, (N_BUCKETS, 128), 0)
        acc = jnp.zeros((N_BUCKETS, 128), f32)
        for b in range(N_BUCKETS):
            sb = jnp.sum(jnp.sum(jnp.where(idxv == b, dbv, 0.0), axis=1, keepdims=True), axis=0, keepdims=True)
            acc = acc + jnp.where(row == b, sb, 0.0)
        o_ref[0] += acc

    return pl.pallas_call(
        body, name="attn_bias_grad", grid=(N_HEADS, 3),
        in_specs=[pl.BlockSpec((1, BAND, 2 * BAND), lambda h, br: (br, 0, 0)),
                  pl.BlockSpec((1, 1, BAND, 2 * BAND), lambda h, br: (br, h, 0, 0))],
        out_specs=pl.BlockSpec((1, N_BUCKETS, 128), lambda h, br: (h, 0, 0)),
        out_shape=jax.ShapeDtypeStruct((N_HEADS, N_BUCKETS, 128), f32),
        compiler_params=_cparams(("parallel", "arbitrary")),
    )(idx, dbias)


def _attn_masks():
    lane = lax.broadcasted_iota(jnp.int32, (BAND, 128), 1)
    m0 = lane < HEAD_DIM
    qi = lax.broadcasted_iota(jnp.int32, (BAND, 2 * BAND), 0)
    kj = lax.broadcasted_iota(jnp.int32, (BAND, 2 * BAND), 1)
    steps = qi + BAND - kj
    in_window = (steps >= 0) & (steps <= BAND)
    return m0, in_window, kj >= BAND


_NT = (((1,), (1,)), ((), ()))
_TN = (((0,), (0,)), ((), ()))
_BNN = (((2,), (1,)), ((0,), (0,)))
_BNT = (((2,), (2,)), ((0,), (0,)))
_BTN = (((1,), (1,)), ((0,), (0,)))
ATTN_GROUP = 4
ATTN_ITEMS = PAD_UNIT // BAND
Q_COL, K_COL, V_COL = 0, 4, 8


def _attn_item_rows(j, d, c, cbase):
    r = lax.rem(j, d)
    b = lax.div(j, d)
    loc = b * (d * BAND) + r
    first = jnp.logical_and(c == 0, b == 0)
    start = cbase + loc
    pstart = jnp.where(first, start, start - d * BAND)
    return loc, start, pstart, first


def _attn_fwd(proj, bias):
    s = proj.shape[0]

    def body(q_ref, k_ref, v_ref, b_ref, y_ref, lse_ref, o_s, l_s):
        c = pl.program_id(1)
        cbase = pl.multiple_of(c * PAD_UNIT, PAD_UNIT)
        m0, in_window, cur_half = _attn_masks()
        for bi, d in enumerate(DILATIONS):
            def group(jg, carry, bi=bi, d=d):
                locs, qs, ks, vs, pens = [], [], [], [], []
                for t in range(ATTN_GROUP):
                    loc, start, pstart, first = _attn_item_rows(jg * ATTN_GROUP + t, d, c, cbase)
                    locs.append(loc)
                    qs.append(q_ref[pl.ds(loc, BAND, stride=d), :])
                    ks.append(jnp.concatenate([k_ref[pl.ds(pstart, BAND, stride=d), :],
                                               k_ref[pl.ds(start, BAND, stride=d), :]], axis=0))
                    vs.append(jnp.concatenate([v_ref[pl.ds(pstart, BAND, stride=d), :],
                                               v_ref[pl.ds(start, BAND, stride=d), :]], axis=0))
                    pens.append(jnp.where(cur_half, 0.0, jnp.where(first, NEG_INF, 0.0)))
                q = jnp.stack(qs)
                kk = jnp.stack(ks + ks).astype(bf16)
                vv = jnp.stack(vs + vs).astype(bf16)
                pen = jnp.stack(pens + pens)
                qh = (jnp.concatenate([jnp.where(m0, q, 0.0), jnp.where(m0, 0.0, q)], axis=0) * 0.125).astype(bf16)
                sc = lax.dot_general(qh, kk, _BNT, preferred_element_type=f32)
                sc = (sc.reshape(2, ATTN_GROUP, BAND, 2 * BAND) + b_ref[bi][:, None]).reshape(sc.shape) + pen
                sc = jnp.where(in_window, sc, NEG_INF)
                mx = jnp.max(sc, axis=-1, keepdims=True)
                e = jnp.exp(sc - mx)
                l = jnp.sum(e, axis=-1, keepdims=True)
                o = lax.dot_general(e.astype(bf16), vv, _BNN, preferred_element_type=f32) / l
                ls = mx + jnp.log(l)
                for t in range(ATTN_GROUP):
                    rows = pl.ds(locs[t], BAND, stride=d)
                    o_s[bi, rows, :] = jnp.where(m0, o[t], o[ATTN_GROUP + t])
                    l_s[bi, rows, :] = jnp.where(m0, ls[t], ls[ATTN_GROUP + t])
                return carry

            lax.fori_loop(0, ATTN_ITEMS // ATTN_GROUP, group, 0)

        def merge(t, carry):
            rows = pl.ds(pl.multiple_of(t * 256, 256), 256)
            ls = [l_s[i, rows, :] for i in range(3)]
            mx = jnp.maximum(jnp.maximum(ls[0], ls[1]), ls[2])
            ws = [jnp.exp(l - mx) for l in ls]
            tot = ws[0] + ws[1] + ws[2]
            y = (ws[0] * o_s[0, rows, :] + ws[1] * o_s[1, rows, :] + ws[2] * o_s[2, rows, :]) / tot
            y_ref[rows, :] = y
            lse_ref[rows, :] = mx + jnp.log(tot)
            return carry

        lax.fori_loop(0, PAD_UNIT // 256, merge, 0)

    chunk = lambda col: pl.BlockSpec((PAD_UNIT, 128), lambda p, c: (c, col + p))
    full = lambda col: pl.BlockSpec((s, 128), lambda p, c: (0, col + p))
    return pl.pallas_call(
        body, name="attn_fwd", grid=(N_HEADS // 2, s // PAD_UNIT),
        in_specs=[chunk(Q_COL), full(K_COL), full(V_COL),
                  pl.BlockSpec((3, 2, BAND, 2 * BAND), lambda p, c: (0, p, 0, 0))],
        out_specs=[chunk(0), chunk(0)],
        out_shape=[jax.ShapeDtypeStruct((s, GROUP_W), f32)] * 2,
        scratch_shapes=[pltpu.VMEM((3, PAD_UNIT, 128), f32)] * 2,
        compiler_params=_cparams(("parallel", "arbitrary")),
    )(proj, proj, proj, bias)


def _attn_bwd(proj, bias, y, lse, dycat):
    s = proj.shape[0]

    def body(q_ref, k_ref, v_ref, b_ref, y_ref, lse_ref, dy_ref, dq_ref, dk_ref, dv_ref, db_ref, dd_s):
        c = pl.program_id(1)
        cbase = pl.multiple_of(c * PAD_UNIT, PAD_UNIT)
        m0, in_window, cur_half = _attn_masks()

        @pl.when(c == 0)
        def _():
            dk_ref[...] = jnp.zeros_like(dk_ref)
            dv_ref[...] = jnp.zeros_like(dv_ref)
            db_ref[...] = jnp.zeros_like(db_ref)

        dq_ref[...] = jnp.zeros_like(dq_ref)

        def rowdot(t, carry):
            rows = pl.ds(pl.multiple_of(t * 256, 256), 256)
            prod = dy_ref[rows, :] * y_ref[rows, :]
            lane = lax.broadcasted_iota(jnp.int32, prod.shape, 1)
            h0 = lane < HEAD_DIM
            d0 = jnp.sum(jnp.where(h0, prod, 0.0), axis=-1, keepdims=True)
            d1 = jnp.sum(jnp.where(h0, 0.0, prod), axis=-1, keepdims=True)
            dd_s[rows, :] = jnp.where(h0, d0, d1)
            return carry

        lax.fori_loop(0, PAD_UNIT // 256, rowdot, 0)

        for bi, d in enumerate(DILATIONS):
            def group(jg, carry, bi=bi, d=d):
                ng = ATTN_GROUP
                meta, qs, dos, lqs, dds, ks, vs, pens = [], [], [], [], [], [], [], []
                for t in range(ng):
                    loc, start, pstart, first = _attn_item_rows(jg * ng + t, d, c, cbase)
                    qrows = pl.ds(loc, BAND, stride=d)
                    rows = pl.ds(start, BAND, stride=d)
                    prows = pl.ds(pstart, BAND, stride=d)
                    meta.append((qrows, rows, prows))
                    qs.append(q_ref[qrows, :])
                    dos.append(dy_ref[qrows, :])
                    lqs.append(lse_ref[qrows, :])
                    dds.append(dd_s[qrows, :])
                    ks.append(jnp.concatenate([k_ref[prows, :], k_ref[rows, :]], axis=0))
                    vs.append(jnp.concatenate([v_ref[prows, :], v_ref[rows, :]], axis=0))
                    pens.append(jnp.where(cur_half, 0.0, jnp.where(first, NEG_INF, 0.0)))

                def heads(t):
                    return jnp.concatenate([jnp.where(m0, t, 0.0), jnp.where(m0, 0.0, t)], axis=0)

                def head_col(t):
                    return jnp.concatenate([t[:, :, 0:1], t[:, :, HEAD_DIM:HEAD_DIM + 1]], axis=0)

                qh = (heads(jnp.stack(qs)) * 0.125).astype(bf16)
                doh = heads(jnp.stack(dos)).astype(bf16)
                kk = jnp.stack(ks + ks).astype(bf16)
                vv = jnp.stack(vs + vs).astype(bf16)
                sc = lax.dot_general(qh, kk, _BNT, preferred_element_type=f32)
                sc = (sc.reshape(2, ng, BAND, 2 * BAND) + b_ref[bi][:, None]).reshape(sc.shape) + jnp.stack(pens + pens)
                sc = jnp.where(in_window, sc, NEG_INF)
                p = jnp.exp(sc - head_col(jnp.stack(lqs)))
                dp = lax.dot_general(doh, vv, _BNT, preferred_element_type=f32)
                ds = p * (dp - head_col(jnp.stack(dds)))
                db_ref[bi] += jnp.sum(ds.reshape(2, ng, BAND, 2 * BAND), axis=1)
                dsb = ds.astype(bf16)
                dq = lax.dot_general(dsb, kk, _BNN, preferred_element_type=f32) * 0.125
                dk = lax.dot_general(dsb, qh, _BTN, preferred_element_type=f32)
                dv = lax.dot_general(p.astype(bf16), doh, _BTN, preferred_element_type=f32)
                for t in range(ng):
                    qrows, rows, prows = meta[t]
                    dq_ref[qrows, :] += jnp.where(m0, dq[t], dq[ng + t])
                    dkt = dk[t] + dk[ng + t]
                    dvt = dv[t] + dv[ng + t]
                    dk_ref[prows, :] += dkt[:BAND]
                    dk_ref[rows, :] += dkt[BAND:]
                    dv_ref[prows, :] += dvt[:BAND]
                    dv_ref[rows, :] += dvt[BAND:]
                return carry

            lax.fori_loop(0, ATTN_ITEMS // ATTN_GROUP, group, 0)

    chunk = lambda col: pl.BlockSpec((PAD_UNIT, 128), lambda p, c: (c, col + p))
    full = lambda col: pl.BlockSpec((s, 128), lambda p, c: (0, col + p))
    bias_spec = pl.BlockSpec((3, 2, BAND, 2 * BAND), lambda p, c: (0, p, 0, 0))
    return pl.pallas_call(
        body, name="attn_bwd", grid=(N_HEADS // 2, s // PAD_UNIT),
        in_specs=[chunk(Q_COL), full(K_COL), full(V_COL), bias_spec, chunk(0), chunk(0), chunk(0)],
        out_specs=[chunk(0), full(0), full(0), bias_spec],
        out_shape=[jax.ShapeDtypeStruct((s, GROUP_W), f32)] * 3
        + [jax.ShapeDtypeStruct((3, N_HEADS, BAND, 2 * BAND), f32)],
        scratch_shapes=[pltpu.VMEM((PAD_UNIT, 128), f32)],
        compiler_params=_cparams(("parallel", "arbitrary")),
    )(proj, proj, proj, bias, y, lse, dycat)


_HI = lax.Precision.HIGHEST
DELTA_COL = 1536
Z_COL = 3072
BA_BLOCK = 28
DELTA_ROWS = 512


def _hdot(a, b):
    return jnp.dot(a, b, precision=_HI, preferred_element_type=f32)


_DIMS = dict(nn=(((2,), (1,)), ((0,), (0,))), nt=(((2,), (2,)), ((0,), (0,))), tn=(((1,), (1,)), ((0,), (0,))))


@functools.partial(jax.custom_vjp, nondiff_argnums=(2,))
def _mmx(a, b, mode):
    return lax.dot_general(a.astype(bf16), b.astype(bf16), _DIMS[mode], preferred_element_type=f32)


def _mmx_fwd(a, b, mode):
    return _mmx(a, b, mode), (a, b)


def _mmx_bwd(mode, res, g):
    a, b = res
    if mode == "nn":
        return _mmx(g, b, "nt"), _mmx(a, g, "tn")
    if mode == "nt":
        return _mmx(g, b, "nn"), _mmx(g, a, "tn")
    return _mmx(b, g, "nt"), _mmx(a, g, "nn")


_mmx.defvjp(_mmx_fwd, _mmx_bwd)


def _pair_iota():
    row = lax.broadcasted_iota(jnp.int32, (CHUNK, 128), 0)
    lane = lax.broadcasted_iota(jnp.int32, (CHUNK, 128), 1)
    return row, lane, lane & (CHUNK - 1)


def _bd(x):
    _, lane, _ = _pair_iota()
    m0 = lane < CHUNK
    return jnp.concatenate([jnp.where(m0, x, 0.0), jnp.where(m0, 0.0, x)], axis=1)


def _pmm(a, b):
    return _mmx(a, _bd(b), "nn")


def _ntp(x, y):
    return _mmx(x, _bd(y), "nt")


def _tnp(x, y):
    full = _mmx(x, y, "tn")
    _, lane, _ = _pair_iota()
    return jnp.where(lane < CHUNK, full[:, :CHUNK], full[:, CHUNK:])


def _tri_inv(a):
    row, lane, jj = _pair_iota()
    eye = jnp.where(row == jj, 1.0, 0.0).astype(f32)

    def same_block(log2b):
        return (row >> log2b) == (jj >> log2b)

    dgl = jnp.where(same_block(3), a, 0.0)
    d2 = _pmm(dgl, dgl)
    d4 = _pmm(d2, d2)
    t = _pmm(_pmm(eye - dgl, eye + d2), eye + d4)
    for lb in (3, 4, 5):
        off = jnp.where(same_block(lb + 1) & jnp.logical_not(same_block(lb)), a, 0.0)
        t = t - _pmm(_pmm(t, off), t)
    return t


@jax.custom_vjp
def _solve2(a, xv, xk):
    t = _tri_inv(a)
    return _pmm(t, xv), _pmm(t, xk)


def _solve2_fwd(a, xv, xk):
    t = _tri_inv(a)
    u, w = _pmm(t, xv), _pmm(t, xk)
    return (u, w), (t, u, w)


def _solve2_bwd(res, cts):
    t, u, w = res
    du, dw = cts
    dxv = _tnp(t, du)
    dxk = _tnp(t, dw)
    return -(_ntp(dxv, u) + _ntp(dxk, w)), dxv, dxk


_solve2.defvjp(_solve2_fwd, _solve2_bwd)


def _chunk_pre(qp, kp, vp, bp, gcum):
    row, lane, jj = _pair_iota()
    causal = row >= jj
    strict = row > jj
    rsel = jnp.sum(jnp.where(row == jj, gcum, 0.0), axis=1, keepdims=True)
    decay = jnp.where(causal, jnp.exp(jnp.where(causal, gcum - rsel, 0.0)), 0.0)
    kb = kp * bp
    kd = _bd(kp)
    a = jnp.where(strict, _mmx(kb, kd, "nt") * decay, 0.0)
    eg = jnp.exp(gcum)
    u, w = _solve2(a, vp * bp, kb * eg)
    qk = jnp.where(causal, _mmx(qp, kd, "nt") * decay, 0.0)
    glast = jnp.sum(jnp.where(row == CHUNK - 1, gcum, 0.0), axis=1, keepdims=True)
    return u, w, qp * eg, kp * jnp.exp(glast - gcum), qk, jnp.exp(glast)


def _chunk_post(u, w, qt, kh, qk, gam, sp):
    sd = _bd(sp)
    vnew = u - _mmx(w, sd, "nn")
    o = _mmx(qt, sd, "nn") + _pmm(qk, vnew)
    return o, gam * sp + _tnp(kh, vnew)


def _pair_spec(rows=DELTA_ROWS):
    return pl.BlockSpec((rows, 128), lambda i, p: (i, p))


DELTA_NB = DELTA_ROWS // CHUNK


def _chunks(ref):
    return ref[...].reshape(DELTA_NB, CHUNK, 128)


def _pairs(ref, rows):
    return jnp.stack([ref[rows, p * 128:(p + 1) * 128] for p in range(4)], axis=0)


def _delta_chunk_pre(qn, kn, sv, beta, g):
    s = qn.shape[0]

    def body(q_ref, k_ref, v_ref, b_ref, g_ref, u_ref, w_ref, qt_ref, kh_ref, qk_ref, gm_ref):
        outs = _chunk_pre(_chunks(q_ref), _chunks(k_ref), _chunks(v_ref), _chunks(b_ref), _chunks(g_ref))
        for ref, val in zip((u_ref, w_ref, qt_ref, kh_ref, qk_ref), outs[:5]):
            ref[...] = val.reshape(DELTA_ROWS, 128)
        gm_ref[...] = jnp.broadcast_to(outs[5], (DELTA_NB, 8, 128)).reshape(DELTA_NB * 8, 128)

    v_spec = pl.BlockSpec((DELTA_ROWS, 128), lambda i, p: (i, 8 + p))
    return pl.pallas_call(
        body, name="delta_chunk_pre", grid=(s // DELTA_ROWS, 4),
        in_specs=[_pair_spec(), _pair_spec(), v_spec, _pair_spec(), _pair_spec()],
        out_specs=[_pair_spec()] * 5 + [_pair_spec(DELTA_NB * 8)],
        out_shape=[jax.ShapeDtypeStruct((s, GROUP_W), f32)] * 5 + [jax.ShapeDtypeStruct((s // 8, GROUP_W), f32)],
        compiler_params=_cparams(("parallel", "parallel")),
    )(qn, kn, sv, beta, g)


def _delta_scan_fwd(u, w, qt, kh, qk, gm):
    s = u.shape[0]

    def body(u_ref, w_ref, qt_ref, kh_ref, qk_ref, gm_ref, o_ref, ss_ref, st):
        @pl.when(pl.program_id(0) == 0)
        def _():
            st[...] = jnp.zeros_like(st)

        def chunk(ci, carry):
            rows = pl.ds(pl.multiple_of(ci * CHUNK, CHUNK), CHUNK)
            grow = pl.ds(pl.multiple_of(ci * 8, 8), 1)
            sp = st[...]
            o, s2 = _chunk_post(_pairs(u_ref, rows), _pairs(w_ref, rows), _pairs(qt_ref, rows),
                                _pairs(kh_ref, rows), _pairs(qk_ref, rows), _pairs(gm_ref, grow), sp)
            for p in range(4):
                ss_ref[rows, p * 128:(p + 1) * 128] = sp[p]
                o_ref[rows, p * 128:(p + 1) * 128] = o[p]
            st[...] = s2
            return carry

        lax.fori_loop(0, DELTA_NB, chunk, 0)

    spec = pl.BlockSpec((DELTA_ROWS, GROUP_W), lambda i: (i, 0))
    gspec = pl.BlockSpec((DELTA_NB * 8, GROUP_W), lambda i: (i, 0))
    return pl.pallas_call(
        body, name="delta_scan_fwd", grid=(s // DELTA_ROWS,),
        in_specs=[spec] * 5 + [gspec],
        out_specs=[spec, spec],
        out_shape=[jax.ShapeDtypeStruct((s, GROUP_W), f32)] * 2,
        scratch_shapes=[pltpu.VMEM((4, CHUNK, 128), f32)],
        compiler_params=_cparams(("arbitrary",)),
    )(u, w, qt, kh, qk, gm)


def _delta_scan_bwd(w, qt, kh, qk, gm, do):
    s = w.shape[0]
    nb = s // DELTA_ROWS

    def body(w_ref, qt_ref, kh_ref, qk_ref, gm_ref, do_ref, dso_ref, dst):
        @pl.when(pl.program_id(0) == 0)
        def _():
            dst[...] = jnp.zeros_like(dst)

        def chunk(t, carry):
            ci = DELTA_NB - 1 - t
            rows = pl.ds(pl.multiple_of(ci * CHUNK, CHUNK), CHUNK)
            grow = pl.ds(pl.multiple_of(ci * 8, 8), 1)
            ds = dst[...]
            for p in range(4):
                dso_ref[rows, p * 128:(p + 1) * 128] = ds[p]
            do = _pairs(do_ref, rows)
            dvn = _tnp(_pairs(qk_ref, rows), do) + _pmm(_pairs(kh_ref, rows), ds)
            dst[...] = _tnp(_pairs(qt_ref, rows), do) + _pairs(gm_ref, grow) * ds - _tnp(_pairs(w_ref, rows), dvn)
            return carry

        lax.fori_loop(0, DELTA_NB, chunk, 0)

    spec = pl.BlockSpec((DELTA_ROWS, GROUP_W), lambda i: (nb - 1 - i, 0))
    gspec = pl.BlockSpec((DELTA_NB * 8, GROUP_W), lambda i: (nb - 1 - i, 0))
    return pl.pallas_call(
        body, name="delta_scan_bwd", grid=(nb,),
        in_specs=[spec] * 4 + [gspec, spec],
        out_specs=spec,
        out_shape=jax.ShapeDtypeStruct((s, GROUP_W), f32),
        scratch_shapes=[pltpu.VMEM((4, CHUNK, 128), f32)],
        compiler_params=_cparams(("arbitrary",)),
    )(w, qt, kh, qk, gm, do)


def _delta_chunk_bwd(qn, kn, sv, beta, g, ss, dso, do):
    s = qn.shape[0]

    def body(q_ref, k_ref, v_ref, b_ref, g_ref, ss_ref, dso_ref, do_ref, dq_ref, dk_ref, dv_ref, db_ref, dg_ref):
        sp = _chunks(ss_ref)

        def fn(q, k, v, b, gg):
            return _chunk_post(*_chunk_pre(q, k, v, b, gg), sp)

        _, vjp = jax.vjp(fn, _chunks(q_ref), _chunks(k_ref), _chunks(v_ref), _chunks(b_ref), _chunks(g_ref))
        grads = vjp((_chunks(do_ref), _chunks(dso_ref)))
        for ref, val in zip((dq_ref, dk_ref, dv_ref, db_ref, dg_ref), grads):
            ref[...] = val.reshape(DELTA_ROWS, 128)

    v_spec = pl.BlockSpec((DELTA_ROWS, 128), lambda i, p: (i, 8 + p))
    return pl.pallas_call(
        body, name="delta_chunk_bwd", grid=(s // DELTA_ROWS, 4),
        in_specs=[_pair_spec(), _pair_spec(), v_spec] + [_pair_spec()] * 5,
        out_specs=[_pair_spec()] * 5,
        out_shape=[jax.ShapeDtypeStruct((s, GROUP_W), f32)] * 5,
        compiler_params=_cparams(("parallel", "parallel")),
    )(qn, kn, sv, beta, g, ss, dso, do)


def _head_sum_matrix():
    r = lax.broadcasted_iota(jnp.int32, (GROUP_W, GROUP_W), 0)
    c = lax.broadcasted_iota(jnp.int32, (GROUP_W, GROUP_W), 1)
    return jnp.where((r >> 6) == (c >> 6), 1.0, 0.0).astype(f32)


def _head_sums(x):
    return _mmx(x[None], _head_sum_matrix()[None], "nn")[0]


def _sel_dot(a, b):
    return jnp.dot(a, b, precision=lax.Precision.HIGH, preferred_element_type=f32)


def _softplus(x):
    return jnp.maximum(x, 0.0) + jnp.log(1.0 + jnp.exp(-jnp.abs(x)))


def _prep_fn(sq, sk, ba, alog_e, dt_e):
    qn = sq * lax.rsqrt(_head_sums(sq * sq) + EPS) * (HEAD_DIM ** -0.5)
    kn = sk * lax.rsqrt(_head_sums(sk * sk) + EPS)
    r = lax.broadcasted_iota(jnp.int32, (128, GROUP_W), 0)
    c = lax.broadcasted_iota(jnp.int32, (128, GROUP_W), 1) >> 6
    bl = _sel_dot(ba, jnp.where(r == c, 1.0, 0.0).astype(f32))
    al = _sel_dot(ba, jnp.where(r == c + N_HEADS, 1.0, 0.0).astype(f32))
    beta = jax.nn.sigmoid(bl)
    g = -jnp.exp(alog_e) * _softplus(al + dt_e)
    ri = lax.broadcasted_iota(jnp.int32, (TOK_TILE, TOK_TILE), 0)
    ci = lax.broadcasted_iota(jnp.int32, (TOK_TILE, TOK_TILE), 1)
    within = jnp.where(((ri >> 6) == (ci >> 6)) & (ri >= ci), 1.0, 0.0).astype(f32)
    return qn, kn, beta, _sel_dot(within, g)


def _gnorm_fn(o, z, ng_e):
    ms = _head_sums(o * o) * (1.0 / HEAD_DIM)
    return o * lax.rsqrt(ms + EPS) * ng_e * (z * jax.nn.sigmoid(z))


def _tok_spec(width, col):
    return pl.BlockSpec((TOK_TILE, width), lambda i: (i, col))


def _conv_taps(xs_ref, w_ref, base, n):
    acc = w_ref[CONV_WIDTH - 1:CONV_WIDTH, :] * xs_ref[pl.ds(base, n), :]
    for j in range(CONV_WIDTH - 1):
        acc = acc + w_ref[j:j + 1, :] * xs_ref[pl.ds(base - (CONV_WIDTH - 1) + j, n), :]
    return acc


def _conv_silu_fwd(proj, conv_w):
    s = proj.shape[0]
    wd = 3 * GROUP_W
    hb = TOK_TILE // 8

    def body(x_ref, halo_ref, w_ref, o_ref, xs):
        xs[0:8, :] = jnp.where(pl.program_id(0) > 0, halo_ref[...], 0.0)
        xs[8:, :] = x_ref[...]
        y = _conv_taps(xs, w_ref, 8, TOK_TILE)
        o_ref[...] = y * jax.nn.sigmoid(y)

    return pl.pallas_call(
        body, name="delta_conv_fwd", grid=(s // TOK_TILE,),
        in_specs=[_tok_spec(wd, 1), pl.BlockSpec((8, wd), lambda i: (jnp.maximum(i * hb - 1, 0), 1)),
                  pl.BlockSpec((CONV_WIDTH, wd), lambda i: (0, 0))],
        out_specs=_tok_spec(wd, 0),
        out_shape=jax.ShapeDtypeStruct((s, wd), f32),
        scratch_shapes=[pltpu.VMEM((TOK_TILE + 8, wd), f32)],
        compiler_params=_cparams(("parallel",)),
    )(proj, proj, conv_w)


def _conv_silu_bwd(proj, conv_w, ds):
    s = proj.shape[0]
    wd = 3 * GROUP_W
    hb = TOK_TILE // 8
    nt = s // TOK_TILE

    def body(x_ref, hp_ref, hn_ref, ds_ref, dsn_ref, w_ref, dx_ref, dw_ref, xs, dys):
        i = pl.program_id(0)

        @pl.when(i == 0)
        def _():
            dw_ref[...] = jnp.zeros_like(dw_ref)

        last = i == nt - 1
        xs[0:8, :] = jnp.where(i > 0, hp_ref[...], 0.0)
        xs[8:8 + TOK_TILE, :] = x_ref[...]
        xs[8 + TOK_TILE:, :] = jnp.where(last, 0.0, hn_ref[...])
        y = _conv_taps(xs, w_ref, 8, TOK_TILE)
        sg = jax.nn.sigmoid(y)
        dys[0:TOK_TILE, :] = ds_ref[...] * sg * (1.0 + y * (1.0 - sg))
        yn = _conv_taps(xs, w_ref, 8 + TOK_TILE, 8)
        sgn = jax.nn.sigmoid(yn)
        dys[TOK_TILE:, :] = jnp.where(last, 0.0, dsn_ref[...]) * sgn * (1.0 + yn * (1.0 - sgn))
        dy0 = dys[0:TOK_TILE, :]
        dx = w_ref[CONV_WIDTH - 1:CONV_WIDTH, :] * dy0
        for j in range(CONV_WIDTH - 1):
            dx = dx + w_ref[j:j + 1, :] * dys[pl.ds(CONV_WIDTH - 1 - j, TOK_TILE), :]
        dx_ref[...] = dx
        for j in range(CONV_WIDTH):
            dw_ref[j:j + 1, :] += jnp.sum(dy0 * xs[pl.ds(8 - (CONV_WIDTH - 1) + j, TOK_TILE), :],
                                          axis=0, keepdims=True)

    prev8 = lambda col: pl.BlockSpec((8, wd), lambda i: (jnp.maximum(i * hb - 1, 0), col))
    next8 = lambda col: pl.BlockSpec((8, wd), lambda i: (jnp.minimum((i + 1) * hb, s // 8 - 1), col))
    return pl.pallas_call(
        body, name="delta_conv_bwd", grid=(nt,),
        in_specs=[_tok_spec(wd, 1), prev8(1), next8(1), _tok_spec(wd, 0), next8(0),
                  pl.BlockSpec((CONV_WIDTH, wd), lambda i: (0, 0))],
        out_specs=[_tok_spec(wd, 0), pl.BlockSpec((CONV_WIDTH, wd), lambda i: (0, 0))],
        out_shape=[jax.ShapeDtypeStruct((s, wd), f32), jax.ShapeDtypeStruct((CONV_WIDTH, wd), f32)],
        scratch_shapes=[pltpu.VMEM((TOK_TILE + 16, wd), f32), pltpu.VMEM((TOK_TILE + 8, wd), f32)],
        compiler_params=_cparams(("arbitrary",)),
    )(proj, proj, proj, ds, ds, conv_w)


def _delta_prep_fwd(sconv, proj, alog_e, dt_e):
    s = sconv.shape[0]

    def body(sq_ref, sk_ref, ba_ref, al_ref, dt_ref, q_ref, k_ref, b_ref, g_ref):
        qn, kn, beta, g = _prep_fn(sq_ref[...], sk_ref[...], ba_ref[...], al_ref[...], dt_ref[...])
        q_ref[...] = qn
        k_ref[...] = kn
        b_ref[...] = beta
        g_ref[...] = g

    return pl.pallas_call(
        body, name="delta_prep_fwd", grid=(s // TOK_TILE,),
        in_specs=[_tok_spec(GROUP_W, 0), _tok_spec(GROUP_W, 1), _tok_spec(128, BA_BLOCK),
                  _vec_spec(GROUP_W), _vec_spec(GROUP_W)],
        out_specs=[_tok_spec(GROUP_W, 0)] * 4,
        out_shape=[jax.ShapeDtypeStruct((s, GROUP_W), f32)] * 4,
        compiler_params=_cparams(("parallel",)),
    )(sconv, sconv, proj, alog_e, dt_e)


def _delta_prep_bwd(sconv, proj, alog_e, dt_e, dqn, dkn, dbeta, dg):
    s = sconv.shape[0]

    def body(sq_ref, sk_ref, ba_ref, al_ref, dt_ref, dq_ref, dk_ref, db_ref, dg_ref,
             dsq_ref, dsk_ref, dba_ref, dal_ref, ddt_ref):
        @pl.when(pl.program_id(0) == 0)
        def _():
            dal_ref[...] = jnp.zeros_like(dal_ref)
            ddt_ref[...] = jnp.zeros_like(ddt_ref)

        _, vjp = jax.vjp(_prep_fn, sq_ref[...], sk_ref[...], ba_ref[...], al_ref[...], dt_ref[...])
        dsq, dsk, dba, dal, ddt = vjp((dq_ref[...], dk_ref[...], db_ref[...], dg_ref[...]))
        dsq_ref[...] = dsq
        dsk_ref[...] = dsk
        dba_ref[...] = dba
        dal_ref[...] += dal
        ddt_ref[...] += ddt

    return pl.pallas_call(
        body, name="delta_prep_bwd", grid=(s // TOK_TILE,),
        in_specs=[_tok_spec(GROUP_W, 0), _tok_spec(GROUP_W, 1), _tok_spec(128, BA_BLOCK),
                  _vec_spec(GROUP_W), _vec_spec(GROUP_W)] + [_tok_spec(GROUP_W, 0)] * 4,
        out_specs=[_tok_spec(GROUP_W, 0), _tok_spec(GROUP_W, 0), _tok_spec(128, 0),
                   _acc_spec(GROUP_W), _acc_spec(GROUP_W)],
        out_shape=[jax.ShapeDtypeStruct((s, GROUP_W), f32)] * 2 + [jax.ShapeDtypeStruct((s, 128), f32)]
        + [jax.ShapeDtypeStruct((1, GROUP_W), f32)] * 2,
        compiler_params=_cparams(("arbitrary",)),
    )(sconv, sconv, proj, alog_e, dt_e, dqn, dkn, dbeta, dg)


def _gnorm_fwd(o, proj, ng_e):
    s = o.shape[0]

    def body(o_ref, z_ref, g_ref, y_ref):
        y_ref[...] = _gnorm_fn(o_ref[...], z_ref[...], g_ref[...])

    return pl.pallas_call(
        body, name="delta_gnorm_fwd", grid=(s // TOK_TILE,),
        in_specs=[_tok_spec(GROUP_W, 0), _tok_spec(GROUP_W, Z_COL // GROUP_W), _vec_spec(GROUP_W)],
        out_specs=_tok_spec(GROUP_W, 0),
        out_shape=jax.ShapeDtypeStruct((s, GROUP_W), f32),
        compiler_params=_cparams(("parallel",)),
    )(o, proj, ng_e)


def _gnorm_bwd(o, proj, ng_e, dycat):
    s = o.shape[0]

    def body(o_ref, z_ref, g_ref, dy_ref, do_ref, dz_ref, dg_ref):
        @pl.when(pl.program_id(0) == 0)
        def _():
            dg_ref[...] = jnp.zeros_like(dg_ref)

        _, vjp = jax.vjp(_gnorm_fn, o_ref[...], z_ref[...], g_ref[...])
        do, dz, dg = vjp(dy_ref[...])
        do_ref[...] = do
        dz_ref[...] = dz
        dg_ref[...] += dg

    return pl.pallas_call(
        body, name="delta_gnorm_bwd", grid=(s // TOK_TILE,),
        in_specs=[_tok_spec(GROUP_W, 0), _tok_spec(GROUP_W, Z_COL // GROUP_W), _vec_spec(GROUP_W),
                  _tok_spec(GROUP_W, 1)],
        out_specs=[_tok_spec(GROUP_W, 0), _tok_spec(GROUP_W, 0), _acc_spec(GROUP_W)],
        out_shape=[jax.ShapeDtypeStruct((s, GROUP_W), f32)] * 2 + [jax.ShapeDtypeStruct((1, GROUP_W), f32)],
        compiler_params=_cparams(("arbitrary",)),
    )(o, proj, ng_e, dycat)


_MESH = pl.DeviceIdType.MESH
_ANY = pl.BlockSpec(memory_space=pl.ANY)
_VMEM = pl.BlockSpec(memory_space=pltpu.VMEM)


def _my_place():
    x, y, c = lax.axis_index("x"), lax.axis_index("y"), lax.axis_index("c")
    return x, y, c, 4 * x + 2 * y + c


def _peer(k, x, y, c):
    px = 1 - x if k & 4 else x
    py = 1 - y if k & 2 else y
    pc = 1 - c if k & 1 else c
    return (px, py, pc), 4 * px + 2 * py + pc


def _exchange_all(src_of_peer, dst_ref, send_sems, recv_sems, x, y, c, me):
    sent = []
    for k in range(1, N_DEV):
        dev, pidx = _peer(k, x, y, c)
        cp = pltpu.make_async_remote_copy(src_ref=src_of_peer(pidx), dst_ref=dst_ref.at[me],
                                          send_sem=send_sems.at[k - 1], recv_sem=recv_sems.at[k - 1],
                                          device_id=dev, device_id_type=_MESH)
        cp.start()
        sent.append(cp)
    for k in range(1, N_DEV):
        dev, pidx = _peer(k, x, y, c)
        pltpu.make_async_remote_copy(src_ref=src_of_peer(pidx), dst_ref=dst_ref.at[pidx],
                                     send_sem=send_sems.at[k - 1], recv_sem=recv_sems.at[k - 1],
                                     device_id=dev, device_id_type=_MESH).wait_recv()
    for cp in sent:
        cp.wait_send()


def _ada_exchange(cv8, w_ada, b_ada8):
    def body(cv_ref, w_ref, b_ref, call_ref, modp_ref, part_s, s1, r1, s2, r2):
        x, y, c, me = _my_place()
        call_ref[me] = cv_ref[...]
        _exchange_all(lambda pidx: cv_ref, call_ref, s1, r1, x, y, c, me)
        bias = b_ref[me]
        for j in range(N_DEV):
            cj = call_ref[j][:, :D_MODEL]
            part_s[j] = _hdot(cj * jax.nn.sigmoid(cj), w_ref[...]) + bias
        modp_ref[me] = part_s[me]
        _exchange_all(lambda pidx: part_s.at[pidx], modp_ref, s2, r2, x, y, c, me)

    nsh = w_ada.shape[1]
    return pl.pallas_call(
        body, name="ada_exchange",
        in_specs=[_VMEM, _VMEM, _VMEM], out_specs=[_VMEM, _VMEM],
        out_shape=[jax.ShapeDtypeStruct((N_DEV, 8, cv8.shape[1]), f32), jax.ShapeDtypeStruct((N_DEV, 8, nsh), f32)],
        scratch_shapes=[pltpu.VMEM((N_DEV, 8, nsh), f32)] + [pltpu.SemaphoreType.DMA((N_DEV - 1,))] * 4,
        compiler_params=pltpu.CompilerParams(vmem_limit_bytes=VMEM_LIMIT),
    )(cv8, w_ada, b_ada8)


def _all_to_all(arrs, name):
    n = len(arrs)

    def body(*refs):
        srcs, dsts = refs[:n], refs[n:2 * n]
        send_sems, recv_sems, local_sems = refs[2 * n:]
        x, y, c, me = _my_place()
        local = []
        for a in range(n):
            cp = pltpu.make_async_copy(srcs[a].at[me], dsts[a].at[me], local_sems.at[a])
            cp.start()
            local.append(cp)
        sent = []
        for a in range(n):
            for k in range(1, N_DEV):
                dev, pidx = _peer(k, x, y, c)
                cp = pltpu.make_async_remote_copy(src_ref=srcs[a].at[pidx], dst_ref=dsts[a].at[me],
                                                  send_sem=send_sems.at[a, k - 1], recv_sem=recv_sems.at[a, k - 1],
                                                  device_id=dev, device_id_type=_MESH)
                cp.start()
                sent.append(cp)
        for a in range(n):
            for k in range(1, N_DEV):
                dev, pidx = _peer(k, x, y, c)
                pltpu.make_async_remote_copy(src_ref=srcs[a].at[pidx], dst_ref=dsts[a].at[pidx],
                                             send_sem=send_sems.at[a, k - 1], recv_sem=recv_sems.at[a, k - 1],
                                             device_id=dev, device_id_type=_MESH).wait_recv()
        for cp in sent:
            cp.wait_send()
        for cp in local:
            cp.wait()

    return pl.pallas_call(
        body, name=name,
        in_specs=[_ANY] * n, out_specs=[_ANY] * n,
        out_shape=[jax.ShapeDtypeStruct(a.shape, a.dtype) for a in arrs],
        scratch_shapes=[pltpu.SemaphoreType.DMA((n, N_DEV - 1)), pltpu.SemaphoreType.DMA((n, N_DEV - 1)),
                        pltpu.SemaphoreType.DMA((n,))],
    )(*arrs)


def _all_gather_weights(shards):
    n = len(shards)

    def body(*refs):
        srcs, outs = refs[:n], refs[n:2 * n]
        send_sems, recv_sems, local_sems = refs[2 * n:]
        x, y, c, me = _my_place()
        sib = (x, y, 1 - c)
        chips = [(1 - x, y), (x, 1 - y), (1 - x, 1 - y)]

        def idx(px, py, pc):
            return 4 * px + 2 * py + pc

        def copy(a, k, block, to, src=None):
            rows = outs[a].at[idx(*block)]
            return pltpu.make_async_remote_copy(src_ref=rows if src is None else src, dst_ref=rows,
                                                send_sem=send_sems.at[a, k], recv_sem=recv_sems.at[a, k],
                                                device_id=to, device_id_type=_MESH)

        mine, first, passed = [], [], []
        for a in range(n):
            cp = pltpu.make_async_copy(srcs[a], outs[a].at[me], local_sems.at[a])
            cp.start()
            mine.append(cp)
            fa = [copy(a, 0, (x, y, c), sib, src=srcs[a])]
            fa += [copy(a, 1 + j, (x, y, c), (*chip, c), src=srcs[a]) for j, chip in enumerate(chips)]
            for cp in fa:
                cp.start()
            first += fa
        for a in range(n):
            for j, chip in enumerate(chips):
                copy(a, 1 + j, (*chip, c), (x, y, c)).wait_recv()
                cp = copy(a, 4 + j, (*chip, c), sib)
                cp.start()
                passed.append(cp)
        for a in range(n):
            copy(a, 0, (x, y, 1 - c), (x, y, c)).wait_recv()
            for j, chip in enumerate(chips):
                copy(a, 4 + j, (*chip, 1 - c), (x, y, c)).wait_recv()
        for cp in first + passed:
            cp.wait_send()
        for cp in mine:
            cp.wait()

    return pl.pallas_call(
        body, name="gather_weights",
        in_specs=[_ANY] * n, out_specs=[_ANY] * n,
        out_shape=[jax.ShapeDtypeStruct((N_DEV,) + a.shape, a.dtype) for a in shards],
        scratch_shapes=[pltpu.SemaphoreType.DMA((n, N_DEV - 1)), pltpu.SemaphoreType.DMA((n, N_DEV - 1)),
                        pltpu.SemaphoreType.DMA((n,))],
    )(*shards)


def _adamw_math(w, g, m, v):
    m2 = ADAM_B1 * m + (1.0 - ADAM_B1) * g
    v2 = ADAM_B2 * v + (1.0 - ADAM_B2) * (g * g)
    m_hat = m2 / (1.0 - ADAM_B1 ** ADAM_STEP)
    v_hat = v2 / (1.0 - ADAM_B2 ** ADAM_STEP)
    delta = -ADAM_LR * (m_hat / (jnp.sqrt(v_hat) + ADAM_EPS) + ADAM_WD * w)
    return delta, m2, v2


def _row_tile(rows):
    for t in (256, 128, 64, 32, 16, 8):
        if rows % t == 0:
            return t
    return rows


def _reduce_adamw(parts, w, m, v, name):
    _, r, cdim = parts.shape
    tr = _row_tile(r)

    def body(p_ref, w_ref, m_ref, v_ref, g_ref, d_ref, m2_ref, v2_ref):
        g = p_ref[0].astype(f32)
        for j in range(1, N_DEV):
            g = g + p_ref[j].astype(f32)
        delta, m2, v2 = _adamw_math(w_ref[...], g, m_ref[...], v_ref[...])
        g_ref[...] = g
        d_ref[...] = delta
        m2_ref[...] = m2
        v2_ref[...] = v2

    spec = pl.BlockSpec((tr, cdim), lambda i: (i, 0))
    return pl.pallas_call(
        body, name=name, grid=(r // tr,),
        in_specs=[pl.BlockSpec((N_DEV, tr, cdim), lambda i: (0, i, 0)), spec, spec, spec],
        out_specs=[spec] * 4,
        out_shape=[jax.ShapeDtypeStruct((r, cdim), f32)] * 4,
        compiler_params=_cparams(("parallel",)),
    )(parts, w, m, v)


def _adamw(w, g, m, v, name):
    r, cdim = w.shape
    tr = _row_tile(r)

    def body(w_ref, g_ref, m_ref, v_ref, d_ref, m2_ref, v2_ref):
        delta, m2, v2 = _adamw_math(w_ref[...], g_ref[...], m_ref[...], v_ref[...])
        d_ref[...] = delta
        m2_ref[...] = m2
        v2_ref[...] = v2

    spec = pl.BlockSpec((tr, cdim), lambda i: (i, 0))
    return pl.pallas_call(
        body, name=name, grid=(r // tr,),
        in_specs=[spec] * 4, out_specs=[spec] * 3,
        out_shape=[jax.ShapeDtypeStruct((r, cdim), f32)] * 3,
        compiler_params=_cparams(("parallel",)),
    )(w, g, m, v)


def _sum_devices(parts, name):
    _, r, cdim = parts.shape

    def body(p_ref, o_ref):
        g = p_ref[0]
        for j in range(1, N_DEV):
            g = g + p_ref[j]
        o_ref[...] = g

    return pl.pallas_call(
        body, name=name, out_shape=jax.ShapeDtypeStruct((r, cdim), f32),
        in_specs=[_VMEM], out_specs=_VMEM,
    )(parts)


def _ada_wgrad(c_all8, dmod_cols):
    nsh = dmod_cols.shape[1]

    def body(c_ref, d_ref, o_ref):
        cv = c_ref[...]
        o_ref[...] = lax.dot_general(cv * jax.nn.sigmoid(cv), d_ref[...], _TN, precision=_HI,
                                     preferred_element_type=f32)

    return pl.pallas_call(
        body, name="ada_wgrad", out_shape=jax.ShapeDtypeStruct((D_MODEL, nsh), f32),
        in_specs=[_VMEM, _VMEM], out_specs=_VMEM,
        compiler_params=pltpu.CompilerParams(vmem_limit_bytes=VMEM_LIMIT),
    )(c_all8, dmod_cols)


def _local_step(x, tgt, mod, norm_attn_g, w_in_p, rel_bias, conv_full, a_log, dt_bias, delta_norm_g,
                w_out_b, norm_ffn_g, w_gu_b, w_down_b, final_norm_g):
    s = x.shape[0]
    sh1, sc1, g1, sh2, sc2, g2 = [mod[:, i * D_MODEL:(i + 1) * D_MODEL] for i in range(6)]
    nag = norm_attn_g.reshape(1, D_MODEL)
    nfg = norm_ffn_g.reshape(1, D_MODEL)
    fg = final_norm_g.reshape(1, D_MODEL)
    idx = _bucket_tables()
    bias = _bias_tables(rel_bias, idx)
    alog_e = jnp.repeat(a_log.reshape(N_HEADS), HEAD_DIM)[None]
    dt_e = jnp.repeat(dt_bias.reshape(N_HEADS), HEAD_DIM)[None]
    ng_e = jnp.tile(delta_norm_g.reshape(HEAD_DIM), N_HEADS)[None]

    h1 = _ln_mod_fwd(x, nag, sc1, sh1, "ln1_fwd")
    proj = _mm(h1, w_in_p, "nn", f32, 512, 1280, 1024, "in_proj")
    y_attn, lse = _attn_fwd(proj, bias)
    sconv = _conv_silu_fwd(proj, conv_full)
    qn, kn, beta, g = _delta_prep_fwd(sconv, proj, alog_e, dt_e)
    u, w, qt, kh, qk, gm = _delta_chunk_pre(qn, kn, sconv, beta, g)
    o, ss = _delta_scan_fwd(u, w, qt, kh, qk, gm)
    y_delta = _gnorm_fwd(o, proj, ng_e)
    ycat = jnp.concatenate([y_attn, y_delta], axis=1).astype(bf16)
    y = _mm(ycat, w_out_b, "nn", f32, 512, 1024, 1024, "out_proj")
    x1, h2 = _resid_ln_mod_fwd(x, y, g1, nfg, sc2, sh2, "ln2_fwd")
    gu = _mm(h2, w_gu_b, "nn", f32, 512, 1408, 1024, "ffn_up")
    act = _swiglu_fwd(gu, "swiglu_fwd")
    y2 = _mm(act, w_down_b, "nn", f32, 512, 1024, D_FF, "ffn_down")
    dx2, dy2, loss, dfg, dg2 = _final_loss_bwd(x1, y2, g2, fg, tgt, "final_loss")

    dact = _mm(dy2, w_down_b, "nt", f32, 512, 1408, 1024, "ffn_down_dx")
    g_down = _mm(act, dy2, "tn", f32, 1408, 1024, 512, "ffn_down_dw")
    dgu = _swiglu_bwd(gu, dact, "swiglu_bwd")
    dh2 = _mm(dgu, w_gu_b, "nt", f32, 512, 1024, 1408, "ffn_up_dx")
    g_gu = _mm(h2, dgu, "tn", f32, 1024, 1408, 512, "ffn_up_dw")
    dx1, dsh2, dsc2, dnfg, dy, dg1 = _ln_mod_bwd(x1, nfg, sc2, dh2, dx2, "ln2_bwd", gate=g1, y=y)
    dycat = _mm(dy, w_out_b, "nt", f32, 512, 1024, 1024, "out_proj_dx")
    g_out = _mm(ycat, dy, "tn", f32, 1024, 1024, 512, "out_proj_dw")
    dq, dk, dv, dbias = _attn_bwd(proj, bias, y_attn, lse, dycat)
    g_rb = _bias_grad(dbias, idx)[:, :, 0].T
    do, dz, dng = _gnorm_bwd(o, proj, ng_e, dycat)
    dso = _delta_scan_bwd(w, qt, kh, qk, gm, do)
    dqn, dkn, dvd, dbeta, dgd = _delta_chunk_bwd(qn, kn, sconv, beta, g, ss, dso, do)
    dsq, dsk, dba, dal, ddt = _delta_prep_bwd(sconv, proj, alog_e, dt_e, dqn, dkn, dbeta, dgd)
    dxc, g_conv = _conv_silu_bwd(proj, conv_full, jnp.concatenate([dsq, dsk, dvd], axis=1))
    dproj = jnp.concatenate([dq, dk, dv, dxc, dz, dba, jnp.zeros((s, IN_PAD - BA_BLOCK * 128 - 128), f32)],
                            axis=1).astype(bf16)
    dh1 = _mm(dproj, w_in_p, "nt", f32, 512, 1024, 1280, "in_proj_dx")
    g_in = _mm(h1, dproj, "tn", f32, 1024, 1280, 512, "in_proj_dw")
    gx, dsh1, dsc1, dnag = _ln_mod_bwd(x, nag, sc1, dh1, dx1, "ln1_bwd")
    grads = dict(
        x=gx, mod=jnp.concatenate([dsh1, dsc1, dg1, dsh2, dsc2, dg2], axis=1),
        norm_attn_g=dnag, norm_ffn_g=dnfg, final_norm_g=dfg, rel_bias=g_rb, conv_w=g_conv,
        a_log=dal.reshape(N_HEADS, HEAD_DIM).sum(-1), dt_bias=ddt.reshape(N_HEADS, HEAD_DIM).sum(-1),
        delta_norm_g=dng.reshape(N_HEADS, HEAD_DIM).sum(0),
        w_in=g_in, w_out=g_out, w_gu=g_gu, w_down=g_down)
    return loss[0, 0], grads


MISC_OFF = dict(rel_bias=0, a_log=256, dt_bias=264, delta_norm_g=272)


def _misc_row(rel_bias, a_log, dt_bias, delta_norm_g):
    flat = jnp.concatenate([rel_bias.reshape(-1), a_log.reshape(-1), dt_bias.reshape(-1), delta_norm_g.reshape(-1)])
    return jnp.pad(flat, (0, D_MODEL - flat.shape[0]))[None]


def _pack_small(b_ada, nag, nfg, fng, rel_bias, a_log, dt_bias, dng, conv_shard):
    rows = [b_ada.reshape(6, D_MODEL), nag.reshape(1, D_MODEL), nfg.reshape(1, D_MODEL), fng.reshape(1, D_MODEL),
            _misc_row(rel_bias, a_log, dt_bias, dng),
            jnp.pad(conv_shard.reshape(-1), (0, D_MODEL - conv_shard.size))[None],
            jnp.zeros((5, D_MODEL), f32)]
    return jnp.concatenate(rows, axis=0)


def _unpack_small(p, conv_shape):
    misc = p[9]
    return dict(
        b_ada=p[0:6].reshape(1, 6 * D_MODEL), norm_attn_g=p[6:7], norm_ffn_g=p[7:8], final_norm_g=p[8],
        rel_bias=misc[0:256].reshape(N_BUCKETS, N_HEADS), a_log=misc[256:264].reshape(1, N_HEADS),
        dt_bias=misc[264:272].reshape(1, N_HEADS), delta_norm_g=misc[272:336].reshape(1, HEAD_DIM),
        conv_w=p[10, :conv_shape[1] * conv_shape[2]].reshape(conv_shape))


def kernel(x, c, w_ada, b_ada, norm_attn_g, w_in, rel_bias, conv_w, a_log, dt_bias, delta_norm_g, w_out, norm_ffn_g, w_gate, w_up, w_down, final_norm_g, loss_target, m_w_ada, m_b_ada, m_norm_attn_g, m_w_in, m_rel_bias, m_conv_w, m_a_log, m_dt_bias, m_delta_norm_g, m_w_out, m_norm_ffn_g, m_w_gate, m_w_up, m_w_down, m_final_norm_g, v_w_ada, v_b_ada, v_norm_attn_g, v_w_in, v_rel_bias, v_conv_w, v_a_log, v_dt_bias, v_delta_norm_g, v_w_out, v_norm_ffn_g, v_w_gate, v_w_up, v_w_down, v_final_norm_g):
    me = 4 * lax.axis_index("x") + 2 * lax.axis_index("y") + lax.axis_index("c")
    ada_sh = w_ada.shape[2]
    conv_sh = conv_w.shape[2]

    cv = jnp.concatenate([c[0], conv_w[0].reshape(-1)])
    cv8 = jnp.zeros((8, 2 * D_MODEL), f32).at[0, :cv.shape[0]].set(cv)
    b8 = jnp.broadcast_to(b_ada.reshape(N_DEV, 1, ada_sh), (N_DEV, 8, ada_sh))
    call, modp = _ada_exchange(cv8, w_ada[0], b8)
    mod = modp[:, 0, :].reshape(1, 6 * D_MODEL)
    c_all = call[:, 0, :D_MODEL]
    conv_full = call[:, 0, D_MODEL:D_MODEL + CONV_WIDTH * conv_sh].reshape(N_DEV, CONV_WIDTH, conv_sh)
    conv_full = conv_full.transpose(1, 0, 2).reshape(CONV_WIDTH, N_DEV * conv_sh)

    gw = _all_gather_weights([w_in[0].astype(bf16), w_out[0].astype(bf16), w_gate[0].astype(bf16),
                              w_up[0].astype(bf16), w_down[0].astype(bf16)])
    cols = lambda t: t.transpose(1, 0, 2).reshape(t.shape[1], N_DEV * t.shape[2])
    w_in_p = jnp.pad(cols(gw[0]), ((0, 0), (0, IN_PAD - IN_WIDTH)))
    w_out_b = gw[1].reshape(2 * GROUP_W, D_MODEL)
    w_gu_b = jnp.concatenate([cols(gw[2]), cols(gw[3])], axis=1)
    w_down_b = gw[4].reshape(D_FF, D_MODEL)

    loss_local, gr = _local_step(x[0], loss_target[0], mod, norm_attn_g, w_in_p, rel_bias, conv_full, a_log,
                                 dt_bias, delta_norm_g, w_out_b, norm_ffn_g, w_gu_b, w_down_b, final_norm_g)
    loss = lax.psum(loss_local, ("x", "y", "c"))

    small = jnp.concatenate([
        gr["mod"].reshape(6, D_MODEL), gr["norm_attn_g"], gr["norm_ffn_g"], gr["final_norm_g"],
        gr["conv_w"].reshape(6, D_MODEL),
        _misc_row(gr["rel_bias"], gr["a_log"], gr["dt_bias"], gr["delta_norm_g"])], axis=0)
    parts = _all_to_all([jnp.broadcast_to(small[None], (N_DEV,) + small.shape)], "small_gather")[0]
    tot = _sum_devices(parts, "small_sum")
    g_conv_full = tot[9:15].reshape(CONV_WIDTH, N_DEV * conv_sh)
    g_conv = lax.dynamic_slice(g_conv_full, (0, me * conv_sh), (CONV_WIDTH, conv_sh))
    misc = tot[15]
    g_small = _pack_small(tot[0:6], tot[6], tot[7], tot[8], misc[0:256], misc[256:264], misc[264:272],
                          misc[272:336], g_conv)
    pk = lambda pre: _pack_small(pre[0], pre[1], pre[2], pre[3], pre[4], pre[5], pre[6], pre[7], pre[8])
    w_small = pk((b_ada, norm_attn_g, norm_ffn_g, final_norm_g, rel_bias, a_log, dt_bias, delta_norm_g, conv_w))
    m_small = pk((m_b_ada, m_norm_attn_g, m_norm_ffn_g, m_final_norm_g, m_rel_bias, m_a_log, m_dt_bias,
                  m_delta_norm_g, m_conv_w))
    v_small = pk((v_b_ada, v_norm_attn_g, v_norm_ffn_g, v_final_norm_g, v_rel_bias, v_a_log, v_dt_bias,
                  v_delta_norm_g, v_conv_w))
    d_small, m2_small, v2_small = _adamw(w_small, g_small, m_small, v_small, "adamw_small")
    cshape = conv_w.shape
    G, Dl, M2, V2 = (_unpack_small(t, cshape) for t in (g_small, d_small, m2_small, v2_small))

    dmod_all = parts[:, 0:6, :].reshape(N_DEV, 6 * D_MODEL)
    dmod_cols = lax.dynamic_slice(dmod_all, (0, me * ada_sh), (N_DEV, ada_sh))
    g_ada = _ada_wgrad(c_all, dmod_cols)
    d_ada, m2_ada, v2_ada = _adamw(w_ada[0], g_ada, m_w_ada[0], v_w_ada[0], "adamw_w_ada")

    colblk = lambda t, n: t.reshape(t.shape[0], N_DEV, n).transpose(1, 0, 2)
    rowblk = lambda t: t.reshape(N_DEV, t.shape[0] // N_DEV, t.shape[1])
    n_in, n_ff = w_in.shape[2], w_gate.shape[2]
    sends = [colblk(gr["w_in"][:, :IN_WIDTH], n_in), rowblk(gr["w_out"]), colblk(gr["w_gu"][:, :D_FF], n_ff),
             colblk(gr["w_gu"][:, D_FF:], n_ff), rowblk(gr["w_down"])]
    recv = _all_to_all([t.astype(bf16) for t in sends], "grad_exchange")
    big = {}
    for name, p, w_, m_, v_ in (("w_in", recv[0], w_in, m_w_in, v_w_in), ("w_out", recv[1], w_out, m_w_out, v_w_out),
                                ("w_gate", recv[2], w_gate, m_w_gate, v_w_gate), ("w_up", recv[3], w_up, m_w_up, v_w_up),
                                ("w_down", recv[4], w_down, m_w_down, v_w_down)):
        big[name] = [t[None] for t in _reduce_adamw(p, w_[0], m_[0], v_[0], "reduce_adamw_" + name)]

    def leaf(i, name):
        if name == "w_ada":
            return (g_ada, d_ada, m2_ada, v2_ada)[i][None]
        if name in big:
            return big[name][i]
        return (G, Dl, M2, V2)[i][name]

    order = ["w_ada", "b_ada", "norm_attn_g", "w_in", "rel_bias", "conv_w", "a_log", "dt_bias", "delta_norm_g",
             "w_out", "norm_ffn_g", "w_gate", "w_up", "w_down", "final_norm_g"]
    outs = [loss, gr["x"][None]]
    for i in range(4):
        outs += [leaf(i, n) for n in order]
    return tuple(outs)
```

```python
import functools
import math

import jax
import jax.numpy as jnp
from jax import lax
from jax.experimental import pallas as pl
from jax.experimental.pallas import tpu as pltpu

f32 = jnp.float32
bf16 = jnp.bfloat16

D_MODEL = 1024
HEAD_DIM = 64
N_HEADS = 8
GROUP_W = 512
IN_WIDTH = 3600
IN_PAD = 3840
D_FF = 2816
EPS = 1e-6
NEG_INF = -1e30
BAND = 128
PAD_UNIT = 2048
DILATIONS = (1, 4, 16)
N_BUCKETS = 32
MAX_DISTANCE = 2048
CONV_WIDTH = 4
CHUNK = 64
N_DEV = 8
VMEM_LIMIT = 56 * 1024 * 1024

ADAM_LR, ADAM_B1, ADAM_B2, ADAM_EPS, ADAM_WD, ADAM_STEP = 0.001, 0.9, 0.999, 1e-08, 0.01, 10


def _cparams(sem):
    return pltpu.CompilerParams(dimension_semantics=sem, vmem_limit_bytes=VMEM_LIMIT)


def _mm(a, b, mode, out_dtype, tm, tn, tk, name, xchg=None):
    if mode == "nn":
        (m, k), (_, n) = a.shape, b.shape
        a_spec = pl.BlockSpec((tm, tk), lambda j, i, kk: (i, kk))
        b_spec = pl.BlockSpec((tk, tn), lambda j, i, kk: (kk, j))
        dims = (((1,), (0,)), ((), ()))
    elif mode == "nt":
        (m, k), (n, _) = a.shape, b.shape
        a_spec = pl.BlockSpec((tm, tk), lambda j, i, kk: (i, kk))
        b_spec = pl.BlockSpec((tn, tk), lambda j, i, kk: (j, kk))
        dims = (((1,), (1,)), ((), ()))
    else:
        (k, m), (_, n) = a.shape, b.shape
        a_spec = pl.BlockSpec((tk, tm), lambda j, i, kk: (kk, i))
        b_spec = pl.BlockSpec((tk, tn), lambda j, i, kk: (kk, j))
        dims = (((0,), (0,)), ((), ()))
    assert m % tm == 0 and n % tn == 0 and k % tk == 0, (name, m, n, k, tm, tn, tk)
    nk = k // tk
    grid = (n // tn, m // tm, nk)
    nx = xchg.n if xchg is not None else 0

    def body(*refs):
        a_ref, b_ref = refs[:2]
        o_ref = refs[2 + nx]
        scratch = refs[3 + 2 * nx:]
        if nx:
            xrefs = (refs[2:2 + nx], refs[3 + nx:3 + 2 * nx], scratch[-3:])
            xchg.start_at_first_step(grid, *xrefs)
        prod = lax.dot_general(a_ref[...].astype(bf16), b_ref[...].astype(bf16), dims, preferred_element_type=f32)
        if nk == 1:
            o_ref[...] = prod.astype(o_ref.dtype)
        else:
            acc_ref = scratch[0]
            kk = pl.program_id(2)

            @pl.when(kk == 0)
            def _():
                acc_ref[...] = prod

            @pl.when(kk > 0)
            def _():
                acc_ref[...] += prod

            @pl.when(kk == nk - 1)
            def _():
                o_ref[...] = acc_ref[...].astype(o_ref.dtype)
        if nx:
            xchg.wait_at_last_step(grid, *xrefs)

    out = pl.pallas_call(
        body, name=name, grid=grid,
        in_specs=[a_spec, b_spec] + ([_ANY] * nx),
        out_specs=[pl.BlockSpec((tm, tn), lambda j, i, kk: (i, j))] + ([_ANY] * nx),
        out_shape=[jax.ShapeDtypeStruct((m, n), out_dtype)] + (xchg.out_shape() if nx else []),
        scratch_shapes=([pltpu.VMEM((tm, tn), f32)] if nk > 1 else []) + (xchg.scratch() if nx else []),
        compiler_params=_cparams(("arbitrary",) * 3 if nx else ("parallel", "parallel", "arbitrary")),
    )(a, b, *(xchg.arrs if nx else []))
    return (out[0], out[1:]) if nx else out[0]


TOK_TILE = 512


def _row_spec(width, tile=TOK_TILE):
    return pl.BlockSpec((tile, width), lambda i: (i, 0))


def _vec_spec(width, rows=1):
    return pl.BlockSpec((rows, width), lambda i: (0, 0))


def _ln_mod_fwd(x, gain, sc, sh, name):
    s, d = x.shape

    def body(x_ref, g_ref, sc_ref, sh_ref, h_ref):
        xv = x_ref[...]
        rstd = lax.rsqrt(jnp.mean(xv * xv, axis=-1, keepdims=True) + EPS)
        h = (xv * rstd) * g_ref[...] * (1.0 + sc_ref[...]) + sh_ref[...]
        h_ref[...] = h.astype(bf16)

    return pl.pallas_call(
        body, name=name, grid=(s // TOK_TILE,),
        in_specs=[_row_spec(d), _vec_spec(d), _vec_spec(d), _vec_spec(d)],
        out_specs=_row_spec(d),
        out_shape=jax.ShapeDtypeStruct((s, d), bf16),
        compiler_params=_cparams(("parallel",)),
    )(x, gain, sc, sh)


def _resid_ln_mod_fwd(x, y, gate, gain, sc, sh, name):
    s, d = x.shape

    def body(x_ref, y_ref, gt_ref, g_ref, sc_ref, sh_ref, x1_ref, h_ref):
        x1 = x_ref[...] + gt_ref[...] * y_ref[...]
        x1_ref[...] = x1
        rstd = lax.rsqrt(jnp.mean(x1 * x1, axis=-1, keepdims=True) + EPS)
        h = (x1 * rstd) * g_ref[...] * (1.0 + sc_ref[...]) + sh_ref[...]
        h_ref[...] = h.astype(bf16)

    return pl.pallas_call(
        body, name=name, grid=(s // TOK_TILE,),
        in_specs=[_row_spec(d), _row_spec(d)] + [_vec_spec(d)] * 4,
        out_specs=[_row_spec(d), _row_spec(d)],
        out_shape=[jax.ShapeDtypeStruct((s, d), f32), jax.ShapeDtypeStruct((s, d), bf16)],
        compiler_params=_cparams(("parallel",)),
    )(x, y, gate, gain, sc, sh)


def _swiglu_fwd(gu, name):
    s = gu.shape[0]

    def body(g_ref, u_ref, a_ref):
        g = g_ref[...]
        a_ref[...] = (g * jax.nn.sigmoid(g) * u_ref[...]).astype(bf16)

    tile = 256
    return pl.pallas_call(
        body, name=name, grid=(s // tile,),
        in_specs=[pl.BlockSpec((tile, D_FF), lambda i: (i, 0)), pl.BlockSpec((tile, D_FF), lambda i: (i, 1))],
        out_specs=pl.BlockSpec((tile, D_FF), lambda i: (i, 0)),
        out_shape=jax.ShapeDtypeStruct((s, D_FF), bf16),
        compiler_params=_cparams(("parallel",)),
    )(gu, gu)


def _swiglu_bwd(gu, dact, name):
    s = gu.shape[0]

    def body(g_ref, u_ref, da_ref, o_ref):
        g = g_ref[...]
        sg = jax.nn.sigmoid(g)
        da = da_ref[...]
        o_ref[:, D_FF:] = (da * g * sg).astype(bf16)
        o_ref[:, :D_FF] = (da * u_ref[...] * sg * (1.0 + g * (1.0 - sg))).astype(bf16)

    tile = 256
    return pl.pallas_call(
        body, name=name, grid=(s // tile,),
        in_specs=[pl.BlockSpec((tile, D_FF), lambda i: (i, 0)), pl.BlockSpec((tile, D_FF), lambda i: (i, 1)),
                  pl.BlockSpec((tile, D_FF), lambda i: (i, 0))],
        out_specs=pl.BlockSpec((tile, 2 * D_FF), lambda i: (i, 0)),
        out_shape=jax.ShapeDtypeStruct((s, 2 * D_FF), bf16),
        compiler_params=_cparams(("parallel",)),
    )(gu, gu, dact)


def _acc_spec(width):
    return pl.BlockSpec((1, width), lambda i: (0, 0))


def _final_loss_bwd(x1, y2, gate2, final_g, target, name):
    s, d = x1.shape

    def body(x1_ref, y2_ref, gt_ref, fg_ref, tg_ref, dx2_ref, dy2_ref, loss_ref, dfg_ref, dgt_ref):
        @pl.when(pl.program_id(0) == 0)
        def _():
            loss_ref[...] = jnp.zeros_like(loss_ref)
            dfg_ref[...] = jnp.zeros_like(dfg_ref)
            dgt_ref[...] = jnp.zeros_like(dgt_ref)

        y2 = y2_ref[...]
        gt = gt_ref[...]
        fg = fg_ref[...]
        x2 = x1_ref[...] + gt * y2
        rstd = lax.rsqrt(jnp.mean(x2 * x2, axis=-1, keepdims=True) + EPS)
        xn = x2 * rstd
        err = xn * fg - tg_ref[...]
        row = jnp.sum(err * err, axis=-1, keepdims=True) * (0.5 / d)
        loss_ref[...] += jnp.sum(row, axis=0, keepdims=True) + jnp.zeros_like(loss_ref)
        dout = err * (1.0 / d)
        dfg_ref[...] += jnp.sum(dout * xn, axis=0, keepdims=True)
        dxn = dout * fg
        dx2 = rstd * (dxn - xn * jnp.mean(dxn * xn, axis=-1, keepdims=True))
        dx2_ref[...] = dx2
        dgt_ref[...] += jnp.sum(dx2 * y2, axis=0, keepdims=True)
        dy2_ref[...] = (gt * dx2).astype(bf16)

    return pl.pallas_call(
        body, name=name, grid=(s // TOK_TILE,),
        in_specs=[_row_spec(d), _row_spec(d), _vec_spec(d), _vec_spec(d), _row_spec(d)],
        out_specs=[_row_spec(d), _row_spec(d), _acc_spec(128), _acc_spec(d), _acc_spec(d)],
        out_shape=[jax.ShapeDtypeStruct((s, d), f32), jax.ShapeDtypeStruct((s, d), bf16),
                   jax.ShapeDtypeStruct((1, 128), f32), jax.ShapeDtypeStruct((1, d), f32),
                   jax.ShapeDtypeStruct((1, d), f32)],
        compiler_params=_cparams(("arbitrary",)),
    )(x1, y2, gate2, final_g, target)


def _ln_mod_bwd(xin, gain, sc, dh, dres, name, gate=None, y=None):
    s, d = xin.shape
    with_gate = gate is not None

    def body(*refs):
        if with_gate:
            (x_ref, g_ref, sc_ref, dh_ref, dr_ref, gt_ref, y_ref,
             dx_ref, dsh_ref, dsc_ref, dg_ref, dy_ref, dgt_ref) = refs
        else:
            x_ref, g_ref, sc_ref, dh_ref, dr_ref, dx_ref, dsh_ref, dsc_ref, dg_ref = refs

        @pl.when(pl.program_id(0) == 0)
        def _():
            dsh_ref[...] = jnp.zeros_like(dsh_ref)
            dsc_ref[...] = jnp.zeros_like(dsc_ref)
            dg_ref[...] = jnp.zeros_like(dg_ref)
            if with_gate:
                dgt_ref[...] = jnp.zeros_like(dgt_ref)

        xv = x_ref[...]
        g = g_ref[...]
        sc1 = 1.0 + sc_ref[...]
        dh = dh_ref[...]
        rstd = lax.rsqrt(jnp.mean(xv * xv, axis=-1, keepdims=True) + EPS)
        xn = xv * rstd
        dsh_ref[...] += jnp.sum(dh, axis=0, keepdims=True)
        dsc_ref[...] += jnp.sum(dh * (xn * g), axis=0, keepdims=True)
        dg_ref[...] += jnp.sum(dh * sc1 * xn, axis=0, keepdims=True)
        dxn = dh * sc1 * g
        dx = dr_ref[...] + rstd * (dxn - xn * jnp.mean(dxn * xn, axis=-1, keepdims=True))
        dx_ref[...] = dx
        if with_gate:
            dgt_ref[...] += jnp.sum(dx * y_ref[...], axis=0, keepdims=True)
            dy_ref[...] = (gt_ref[...] * dx).astype(bf16)

    in_specs = [_row_spec(d), _vec_spec(d), _vec_spec(d), _row_spec(d), _row_spec(d)]
    out_specs = [_row_spec(d), _acc_spec(d), _acc_spec(d), _acc_spec(d)]
    out_shape = [jax.ShapeDtypeStruct((s, d), f32)] + [jax.ShapeDtypeStruct((1, d), f32)] * 3
    args = [xin, gain, sc, dh, dres]
    if with_gate:
        in_specs += [_vec_spec(d), _row_spec(d)]
        out_specs += [_row_spec(d), _acc_spec(d)]
        out_shape += [jax.ShapeDtypeStruct((s, d), bf16), jax.ShapeDtypeStruct((1, d), f32)]
        args += [gate, y]
    return pl.pallas_call(
        body, name=name, grid=(s // TOK_TILE,),
        in_specs=in_specs, out_specs=out_specs, out_shape=out_shape,
        compiler_params=_cparams(("arbitrary",)),
    )(*args)


def _bucket_tables():
    import numpy as np
    qi = np.arange(BAND)[:, None]
    kj = np.arange(2 * BAND)[None, :]
    steps = qi + BAND - kj
    max_exact = N_BUCKETS // 2
    out = []
    for d in DILATIONS:
        dist = np.maximum(steps, 0) * d
        dist_f = np.maximum(dist, 1).astype(np.float32)
        large = max_exact + (np.log(dist_f / np.float32(max_exact)) / np.float32(math.log(MAX_DISTANCE / max_exact))
                             * np.float32(N_BUCKETS - max_exact)).astype(np.int32)
        out.append(np.where(dist < max_exact, dist, np.minimum(large, N_BUCKETS - 1)))
    return jnp.asarray(np.stack(out).astype(np.int32))


def _bias_tables(rel_bias, idx):
    def body(idx_ref, rb_ref, o_ref):
        h = pl.program_id(1)
        idxv = idx_ref[0]
        acc = jnp.zeros((BAND, 2 * BAND), f32)
        for b in range(N_BUCKETS):
            acc = jnp.where(idxv == b, rb_ref[b, h], acc)
        o_ref[0, 0] = acc

    return pl.pallas_call(
        body, name="attn_bias_tables", grid=(3, N_HEADS),
        in_specs=[pl.BlockSpec((1, BAND, 2 * BAND), lambda br, h: (br, 0, 0)),
                  pl.BlockSpec(memory_space=pltpu.SMEM)],
        out_specs=pl.BlockSpec((1, 1, BAND, 2 * BAND), lambda br, h: (br, h, 0, 0)),
        out_shape=jax.ShapeDtypeStruct((3, N_HEADS, BAND, 2 * BAND), f32),
        compiler_params=_cparams(("parallel", "parallel")),
    )(idx, rel_bias)


def _bias_grad(dbias, idx):
    def body(idx_ref, db_ref, o_ref):
        br = pl.program_id(1)

        @pl.when(br == 0)
        def _():
            o_ref[...] = jnp.zeros_like(o_ref)

        idxv = idx_ref[0]
        dbv = db_ref[0, 0]
        row = lax.broadcasted_iota(jnp.int32, (N_BUCKETS, 128), 0)
        acc = jnp.zeros((N_BUCKETS, 128), f32)
        for b in range(N_BUCKETS):
            sb = jnp.sum(jnp.sum(jnp.where(idxv == b, dbv, 0.0), axis=1, keepdims=True), axis=0, keepdims=True)
            acc = acc + jnp.where(row == b, sb, 0.0)
        o_ref[0] += acc

    return pl.pallas_call(
        body, name="attn_bias_grad", grid=(N_HEADS, 3),
        in_specs=[pl.BlockSpec((1, BAND, 2 * BAND), lambda h, br: (br, 0, 0)),
                  pl.BlockSpec((1, 1, BAND, 2 * BAND), lambda h, br: (br, h, 0, 0))],
        out_specs=pl.BlockSpec((1, N_BUCKETS, 128), lambda h, br: (h, 0, 0)),
        out_shape=jax.ShapeDtypeStruct((N_HEADS, N_BUCKETS, 128), f32),
        compiler_params=_cparams(("parallel", "arbitrary")),
    )(idx, dbias)


def _attn_masks():
    lane = lax.broadcasted_iota(jnp.int32, (BAND, 128), 1)
    m0 = lane < HEAD_DIM
    qi = lax.broadcasted_iota(jnp.int32, (BAND, 2 * BAND), 0)
    kj = lax.broadcasted_iota(jnp.int32, (BAND, 2 * BAND), 1)
    steps = qi + BAND - kj
    in_window = (steps >= 0) & (steps <= BAND)
    return m0, in_window, kj >= BAND


_NT = (((1,), (1,)), ((), ()))
_TN = (((0,), (0,)), ((), ()))
_BNN = (((2,), (1,)), ((0,), (0,)))
_BNT = (((2,), (2,)), ((0,), (0,)))
_BTN = (((1,), (1,)), ((0,), (0,)))
ATTN_GROUP = 4
ATTN_ITEMS = PAD_UNIT // BAND
Q_COL, K_COL, V_COL = 0, 4, 8


def _attn_item_rows(j, d, c, cbase):
    r = lax.rem(j, d)
    b = lax.div(j, d)
    loc = b * (d * BAND) + r
    first = jnp.logical_and(c == 0, b == 0)
    start = cbase + loc
    pstart = jnp.where(first, start, start - d * BAND)
    return loc, start, pstart, first


def _attn_fwd(proj, bias, xchg):
    s = proj.shape[0]

    def body(q_ref, k_ref, v_ref, b_ref, y_ref, lse_ref, o_s, l_s):
        c = pl.program_id(1)
        cbase = pl.multiple_of(c * PAD_UNIT, PAD_UNIT)
        m0, in_window, cur_half = _attn_masks()
        for bi, d in enumerate(DILATIONS):
            def group(jg, carry, bi=bi, d=d):
                locs, qs, ks, vs, pens = [], [], [], [], []
                for t in range(ATTN_GROUP):
                    loc, start, pstart, first = _attn_item_rows(jg * ATTN_GROUP + t, d, c, cbase)
                    locs.append(loc)
                    qs.append(q_ref[pl.ds(loc, BAND, stride=d), :])
                    ks.append(jnp.concatenate([k_ref[pl.ds(pstart, BAND, stride=d), :],
                                               k_ref[pl.ds(start, BAND, stride=d), :]], axis=0))
                    vs.append(jnp.concatenate([v_ref[pl.ds(pstart, BAND, stride=d), :],
                                               v_ref[pl.ds(start, BAND, stride=d), :]], axis=0))
                    pens.append(jnp.where(cur_half, 0.0, jnp.where(first, NEG_INF, 0.0)))
                q = jnp.stack(qs)
                kk = jnp.stack(ks + ks).astype(bf16)
                vv = jnp.stack(vs + vs).astype(bf16)
                pen = jnp.stack(pens + pens)
                qh = (jnp.concatenate([jnp.where(m0, q, 0.0), jnp.where(m0, 0.0, q)], axis=0) * 0.125).astype(bf16)
                sc = lax.dot_general(qh, kk, _BNT, preferred_element_type=f32)
                sc = (sc.reshape(2, ATTN_GROUP, BAND, 2 * BAND) + b_ref[bi][:, None]).reshape(sc.shape) + pen
                sc = jnp.where(in_window, sc, NEG_INF)
                mx = jnp.max(sc, axis=-1, keepdims=True)
                e = jnp.exp(sc - mx)
                l = jnp.sum(e, axis=-1, keepdims=True)
                o = lax.dot_general(e.astype(bf16), vv, _BNN, preferred_element_type=f32) / l
                ls = mx + jnp.log(l)
                for t in range(ATTN_GROUP):
                    rows = pl.ds(locs[t], BAND, stride=d)
                    o_s[bi, rows, :] = jnp.where(m0, o[t], o[ATTN_GROUP + t])
                    l_s[bi, rows, :] = jnp.where(m0, ls[t], ls[ATTN_GROUP + t])
                return carry

            lax.fori_loop(0, ATTN_ITEMS // ATTN_GROUP, group, 0)

        def merge(t, carry):
            rows = pl.ds(pl.multiple_of(t * 256, 256), 256)
            ls = [l_s[i, rows, :] for i in range(3)]
            mx = jnp.maximum(jnp.maximum(ls[0], ls[1]), ls[2])
            ws = [jnp.exp(l - mx) for l in ls]
            tot = ws[0] + ws[1] + ws[2]
            y = (ws[0] * o_s[0, rows, :] + ws[1] * o_s[1, rows, :] + ws[2] * o_s[2, rows, :]) / tot
            y_ref[rows, :] = y
            lse_ref[rows, :] = mx + jnp.log(tot)
            return carry

        lax.fori_loop(0, PAD_UNIT // 256, merge, 0)

    chunk = lambda col: pl.BlockSpec((PAD_UNIT, 128), lambda p, c: (c, col + p))
    full = lambda col: pl.BlockSpec((s, 128), lambda p, c: (0, col + p))
    grid = (N_HEADS // 2, s // PAD_UNIT)
    out = pl.pallas_call(
        _ride(body, 4, 2, xchg, grid), name="attn_fwd", grid=grid,
        in_specs=[chunk(Q_COL), full(K_COL), full(V_COL),
                  pl.BlockSpec((3, 2, BAND, 2 * BAND), lambda p, c: (0, p, 0, 0))] + [_ANY] * xchg.n,
        out_specs=[chunk(0), chunk(0)] + [_ANY] * xchg.n,
        out_shape=[jax.ShapeDtypeStruct((s, GROUP_W), f32)] * 2 + xchg.out_shape(),
        scratch_shapes=[pltpu.VMEM((3, PAD_UNIT, 128), f32)] * 2 + xchg.scratch(),
        compiler_params=_cparams(("arbitrary", "arbitrary")),
    )(proj, proj, proj, bias, *xchg.arrs)
    return out[:2], out[2:]


def _attn_bwd(proj, bias, y, lse, dycat):
    s = proj.shape[0]

    def body(q_ref, k_ref, v_ref, b_ref, y_ref, lse_ref, dy_ref, dq_ref, dk_ref, dv_ref, db_ref, dd_s):
        c = pl.program_id(1)
        cbase = pl.multiple_of(c * PAD_UNIT, PAD_UNIT)
        m0, in_window, cur_half = _attn_masks()

        @pl.when(c == 0)
        def _():
            dk_ref[...] = jnp.zeros_like(dk_ref)
            dv_ref[...] = jnp.zeros_like(dv_ref)
            db_ref[...] = jnp.zeros_like(db_ref)

        dq_ref[...] = jnp.zeros_like(dq_ref)

        def rowdot(t, carry):
            rows = pl.ds(pl.multiple_of(t * 256, 256), 256)
            prod = dy_ref[rows, :] * y_ref[rows, :]
            lane = lax.broadcasted_iota(jnp.int32, prod.shape, 1)
            h0 = lane < HEAD_DIM
            d0 = jnp.sum(jnp.where(h0, prod, 0.0), axis=-1, keepdims=True)
            d1 = jnp.sum(jnp.where(h0, 0.0, prod), axis=-1, keepdims=True)
            dd_s[rows, :] = jnp.where(h0, d0, d1)
            return carry

        lax.fori_loop(0, PAD_UNIT // 256, rowdot, 0)

        for bi, d in enumerate(DILATIONS):
            def group(jg, carry, bi=bi, d=d):
                ng = ATTN_GROUP
                meta, qs, dos, lqs, dds, ks, vs, pens = [], [], [], [], [], [], [], []
                for t in range(ng):
                    loc, start, pstart, first = _attn_item_rows(jg * ng + t, d, c, cbase)
                    qrows = pl.ds(loc, BAND, stride=d)
                    rows = pl.ds(start, BAND, stride=d)
                    prows = pl.ds(pstart, BAND, stride=d)
                    meta.append((qrows, rows, prows))
                    qs.append(q_ref[qrows, :])
                    dos.append(dy_ref[qrows, :])
                    lqs.append(lse_ref[qrows, :])
                    dds.append(dd_s[qrows, :])
                    ks.append(jnp.concatenate([k_ref[prows, :], k_ref[rows, :]], axis=0))
                    vs.append(jnp.concatenate([v_ref[prows, :], v_ref[rows, :]], axis=0))
                    pens.append(jnp.where(cur_half, 0.0, jnp.where(first, NEG_INF, 0.0)))

                def heads(t):
                    return jnp.concatenate([jnp.where(m0, t, 0.0), jnp.where(m0, 0.0, t)], axis=0)

                def head_col(t):
                    return jnp.concatenate([t[:, :, 0:1], t[:, :, HEAD_DIM:HEAD_DIM + 1]], axis=0)

                qh = (heads(jnp.stack(qs)) * 0.125).astype(bf16)
                doh = heads(jnp.stack(dos)).astype(bf16)
                kk = jnp.stack(ks + ks).astype(bf16)
                vv = jnp.stack(vs + vs).astype(bf16)
                sc = lax.dot_general(qh, kk, _BNT, preferred_element_type=f32)
                sc = (sc.reshape(2, ng, BAND, 2 * BAND) + b_ref[bi][:, None]).reshape(sc.shape) + jnp.stack(pens + pens)
                sc = jnp.where(in_window, sc, NEG_INF)
                p = jnp.exp(sc - head_col(jnp.stack(lqs)))
                dp = lax.dot_general(doh, vv, _BNT, preferred_element_type=f32)
                ds = p * (dp - head_col(jnp.stack(dds)))
                db_ref[bi] += jnp.sum(ds.reshape(2, ng, BAND, 2 * BAND), axis=1)
                dsb = ds.astype(bf16)
                dq = lax.dot_general(dsb, kk, _BNN, preferred_element_type=f32) * 0.125
                dk = lax.dot_general(dsb, qh, _BTN, preferred_element_type=f32)
                dv = lax.dot_general(p.astype(bf16), doh, _BTN, preferred_element_type=f32)
                for t in range(ng):
                    qrows, rows, prows = meta[t]
                    dq_ref[qrows, :] += jnp.where(m0, dq[t], dq[ng + t])
                    dkt = dk[t] + dk[ng + t]
                    dvt = dv[t] + dv[ng + t]
                    dk_ref[prows, :] += dkt[:BAND]
                    dk_ref[rows, :] += dkt[BAND:]
                    dv_ref[prows, :] += dvt[:BAND]
                    dv_ref[rows, :] += dvt[BAND:]
                return carry

            lax.fori_loop(0, ATTN_ITEMS // ATTN_GROUP, group, 0)

    chunk = lambda col: pl.BlockSpec((PAD_UNIT, 128), lambda p, c: (c, col + p))
    full = lambda col: pl.BlockSpec((s, 128), lambda p, c: (0, col + p))
    bias_spec = pl.BlockSpec((3, 2, BAND, 2 * BAND), lambda p, c: (0, p, 0, 0))
    return pl.pallas_call(
        body, name="attn_bwd", grid=(N_HEADS // 2, s // PAD_UNIT),
        in_specs=[chunk(Q_COL), full(K_COL), full(V_COL), bias_spec, chunk(0), chunk(0), chunk(0)],
        out_specs=[chunk(0), full(0), full(0), bias_spec],
        out_shape=[jax.ShapeDtypeStruct((s, GROUP_W), f32)] * 3
        + [jax.ShapeDtypeStruct((3, N_HEADS, BAND, 2 * BAND), f32)],
        scratch_shapes=[pltpu.VMEM((PAD_UNIT, 128), f32)],
        compiler_params=_cparams(("parallel", "arbitrary")),
    )(proj, proj, proj, bias, y, lse, dycat)


_HI = lax.Precision.HIGHEST
DELTA_COL = 1536
Z_COL = 3072
BA_BLOCK = 28
DELTA_ROWS = 512


def _hdot(a, b):
    return jnp.dot(a, b, precision=_HI, preferred_element_type=f32)


_DIMS = dict(nn=(((2,), (1,)), ((0,), (0,))), nt=(((2,), (2,)), ((0,), (0,))), tn=(((1,), (1,)), ((0,), (0,))))


@functools.partial(jax.custom_vjp, nondiff_argnums=(2,))
def _mmx(a, b, mode):
    return lax.dot_general(a.astype(bf16), b.astype(bf16), _DIMS[mode], preferred_element_type=f32)


def _mmx_fwd(a, b, mode):
    return _mmx(a, b, mode), (a, b)


def _mmx_bwd(mode, res, g):
    a, b = res
    if mode == "nn":
        return _mmx(g, b, "nt"), _mmx(a, g, "tn")
    if mode == "nt":
        return _mmx(g, b, "nn"), _mmx(g, a, "tn")
    return _mmx(b, g, "nt"), _mmx(a, g, "nn")


_mmx.defvjp(_mmx_fwd, _mmx_bwd)


def _pair_iota():
    row = lax.broadcasted_iota(jnp.int32, (CHUNK, 128), 0)
    lane = lax.broadcasted_iota(jnp.int32, (CHUNK, 128), 1)
    return row, lane, lane & (CHUNK - 1)


def _bd(x):
    _, lane, _ = _pair_iota()
    m0 = lane < CHUNK
    return jnp.concatenate([jnp.where(m0, x, 0.0), jnp.where(m0, 0.0, x)], axis=1)


def _pmm(a, b):
    return _mmx(a, _bd(b), "nn")


def _ntp(x, y):
    return _mmx(x, _bd(y), "nt")


def _tnp(x, y):
    full = _mmx(x, y, "tn")
    _, lane, _ = _pair_iota()
    return jnp.where(lane < CHUNK, full[:, :CHUNK], full[:, CHUNK:])


def _tri_inv(a):
    row, lane, jj = _pair_iota()
    eye = jnp.where(row == jj, 1.0, 0.0).astype(f32)

    def same_block(log2b):
        return (row >> log2b) == (jj >> log2b)

    dgl = jnp.where(same_block(3), a, 0.0)
    d2 = _pmm(dgl, dgl)
    d4 = _pmm(d2, d2)
    t = _pmm(_pmm(eye - dgl, eye + d2), eye + d4)
    for lb in (3, 4, 5):
        off = jnp.where(same_block(lb + 1) & jnp.logical_not(same_block(lb)), a, 0.0)
        t = t - _pmm(_pmm(t, off), t)
    return t


@jax.custom_vjp
def _solve2(a, xv, xk):
    t = _tri_inv(a)
    return _pmm(t, xv), _pmm(t, xk)


def _solve2_fwd(a, xv, xk):
    t = _tri_inv(a)
    u, w = _pmm(t, xv), _pmm(t, xk)
    return (u, w), (t, u, w)


def _solve2_bwd(res, cts):
    t, u, w = res
    du, dw = cts
    dxv = _tnp(t, du)
    dxk = _tnp(t, dw)
    return -(_ntp(dxv, u) + _ntp(dxk, w)), dxv, dxk


_solve2.defvjp(_solve2_fwd, _solve2_bwd)


def _chunk_pre(qp, kp, vp, bp, gcum):
    row, lane, jj = _pair_iota()
    causal = row >= jj
    strict = row > jj
    rsel = jnp.sum(jnp.where(row == jj, gcum, 0.0), axis=1, keepdims=True)
    decay = jnp.where(causal, jnp.exp(jnp.where(causal, gcum - rsel, 0.0)), 0.0)
    kb = kp * bp
    kd = _bd(kp)
    a = jnp.where(strict, _mmx(kb, kd, "nt") * decay, 0.0)
    eg = jnp.exp(gcum)
    u, w = _solve2(a, vp * bp, kb * eg)
    qk = jnp.where(causal, _mmx(qp, kd, "nt") * decay, 0.0)
    glast = jnp.sum(jnp.where(row == CHUNK - 1, gcum, 0.0), axis=1, keepdims=True)
    return u, w, qp * eg, kp * jnp.exp(glast - gcum), qk, jnp.exp(glast)


def _chunk_post(u, w, qt, kh, qk, gam, sp):
    sd = _bd(sp)
    vnew = u - _mmx(w, sd, "nn")
    o = _mmx(qt, sd, "nn") + _pmm(qk, vnew)
    return o, gam * sp + _tnp(kh, vnew)


def _pair_spec(rows=DELTA_ROWS):
    return pl.BlockSpec((rows, 128), lambda i, p: (i, p))


DELTA_NB = DELTA_ROWS // CHUNK


def _chunks(ref):
    return ref[...].reshape(DELTA_NB, CHUNK, 128)


def _pairs(ref, rows):
    return jnp.stack([ref[rows, p * 128:(p + 1) * 128] for p in range(4)], axis=0)


def _delta_chunk_pre(qn, kn, sv, beta, g):
    s = qn.shape[0]

    def body(q_ref, k_ref, v_ref, b_ref, g_ref, u_ref, w_ref, qt_ref, kh_ref, qk_ref, gm_ref):
        outs = _chunk_pre(_chunks(q_ref), _chunks(k_ref), _chunks(v_ref), _chunks(b_ref), _chunks(g_ref))
        for ref, val in zip((u_ref, w_ref, qt_ref, kh_ref, qk_ref), outs[:5]):
            ref[...] = val.reshape(DELTA_ROWS, 128)
        gm_ref[...] = jnp.broadcast_to(outs[5], (DELTA_NB, 8, 128)).reshape(DELTA_NB * 8, 128)

    v_spec = pl.BlockSpec((DELTA_ROWS, 128), lambda i, p: (i, 8 + p))
    return pl.pallas_call(
        body, name="delta_chunk_pre", grid=(s // DELTA_ROWS, 4),
        in_specs=[_pair_spec(), _pair_spec(), v_spec, _pair_spec(), _pair_spec()],
        out_specs=[_pair_spec()] * 5 + [_pair_spec(DELTA_NB * 8)],
        out_shape=[jax.ShapeDtypeStruct((s, GROUP_W), f32)] * 5 + [jax.ShapeDtypeStruct((s // 8, GROUP_W), f32)],
        compiler_params=_cparams(("parallel", "parallel")),
    )(qn, kn, sv, beta, g)


def _delta_scan_fwd(u, w, qt, kh, qk, gm):
    s = u.shape[0]

    def body(u_ref, w_ref, qt_ref, kh_ref, qk_ref, gm_ref, o_ref, ss_ref, st):
        @pl.when(pl.program_id(0) == 0)
        def _():
            st[...] = jnp.zeros_like(st)

        def chunk(ci, carry):
            rows = pl.ds(pl.multiple_of(ci * CHUNK, CHUNK), CHUNK)
            grow = pl.ds(pl.multiple_of(ci * 8, 8), 1)
            sp = st[...]
            o, s2 = _chunk_post(_pairs(u_ref, rows), _pairs(w_ref, rows), _pairs(qt_ref, rows),
                                _pairs(kh_ref, rows), _pairs(qk_ref, rows), _pairs(gm_ref, grow), sp)
            for p in range(4):
                ss_ref[rows, p * 128:(p + 1) * 128] = sp[p]
                o_ref[rows, p * 128:(p + 1) * 128] = o[p]
            st[...] = s2
            return carry

        lax.fori_loop(0, DELTA_NB, chunk, 0)

    spec = pl.BlockSpec((DELTA_ROWS, GROUP_W), lambda i: (i, 0))
    gspec = pl.BlockSpec((DELTA_NB * 8, GROUP_W), lambda i: (i, 0))
    return pl.pallas_call(
        body, name="delta_scan_fwd", grid=(s // DELTA_ROWS,),
        in_specs=[spec] * 5 + [gspec],
        out_specs=[spec, spec],
        out_shape=[jax.ShapeDtypeStruct((s, GROUP_W), f32)] * 2,
        scratch_shapes=[pltpu.VMEM((4, CHUNK, 128), f32)],
        compiler_params=_cparams(("arbitrary",)),
    )(u, w, qt, kh, qk, gm)


def _delta_scan_bwd(w, qt, kh, qk, gm, do):
    s = w.shape[0]
    nb = s // DELTA_ROWS

    def body(w_ref, qt_ref, kh_ref, qk_ref, gm_ref, do_ref, dso_ref, dst):
        @pl.when(pl.program_id(0) == 0)
        def _():
            dst[...] = jnp.zeros_like(dst)

        def chunk(t, carry):
            ci = DELTA_NB - 1 - t
            rows = pl.ds(pl.multiple_of(ci * CHUNK, CHUNK), CHUNK)
            grow = pl.ds(pl.multiple_of(ci * 8, 8), 1)
            ds = dst[...]
            for p in range(4):
                dso_ref[rows, p * 128:(p + 1) * 128] = ds[p]
            do = _pairs(do_ref, rows)
            dvn = _tnp(_pairs(qk_ref, rows), do) + _pmm(_pairs(kh_ref, rows), ds)
            dst[...] = _tnp(_pairs(qt_ref, rows), do) + _pairs(gm_ref, grow) * ds - _tnp(_pairs(w_ref, rows), dvn)
            return carry

        lax.fori_loop(0, DELTA_NB, chunk, 0)

    spec = pl.BlockSpec((DELTA_ROWS, GROUP_W), lambda i: (nb - 1 - i, 0))
    gspec = pl.BlockSpec((DELTA_NB * 8, GROUP_W), lambda i: (nb - 1 - i, 0))
    return pl.pallas_call(
        body, name="delta_scan_bwd", grid=(nb,),
        in_specs=[spec] * 4 + [gspec, spec],
        out_specs=spec,
        out_shape=jax.ShapeDtypeStruct((s, GROUP_W), f32),
        scratch_shapes=[pltpu.VMEM((4, CHUNK, 128), f32)],
        compiler_params=_cparams(("arbitrary",)),
    )(w, qt, kh, qk, gm, do)


def _delta_chunk_bwd(qn, kn, sv, beta, g, ss, dso, do):
    s = qn.shape[0]

    def body(q_ref, k_ref, v_ref, b_ref, g_ref, ss_ref, dso_ref, do_ref, dq_ref, dk_ref, dv_ref, db_ref, dg_ref):
        sp = _chunks(ss_ref)

        def fn(q, k, v, b, gg):
            return _chunk_post(*_chunk_pre(q, k, v, b, gg), sp)

        _, vjp = jax.vjp(fn, _chunks(q_ref), _chunks(k_ref), _chunks(v_ref), _chunks(b_ref), _chunks(g_ref))
        grads = vjp((_chunks(do_ref), _chunks(dso_ref)))
        for ref, val in zip((dq_ref, dk_ref, dv_ref, db_ref, dg_ref), grads):
            ref[...] = val.reshape(DELTA_ROWS, 128)

    v_spec = pl.BlockSpec((DELTA_ROWS, 128), lambda i, p: (i, 8 + p))
    return pl.pallas_call(
        body, name="delta_chunk_bwd", grid=(s // DELTA_ROWS, 4),
        in_specs=[_pair_spec(), _pair_spec(), v_spec] + [_pair_spec()] * 5,
        out_specs=[_pair_spec()] * 5,
        out_shape=[jax.ShapeDtypeStruct((s, GROUP_W), f32)] * 5,
        compiler_params=_cparams(("parallel", "parallel")),
    )(qn, kn, sv, beta, g, ss, dso, do)


def _head_sum_matrix():
    r = lax.broadcasted_iota(jnp.int32, (GROUP_W, GROUP_W), 0)
    c = lax.broadcasted_iota(jnp.int32, (GROUP_W, GROUP_W), 1)
    return jnp.where((r >> 6) == (c >> 6), 1.0, 0.0).astype(f32)


def _head_sums(x):
    return _mmx(x[None], _head_sum_matrix()[None], "nn")[0]


def _sel_dot(a, b):
    return jnp.dot(a, b, precision=lax.Precision.HIGH, preferred_element_type=f32)


def _softplus(x):
    return jnp.maximum(x, 0.0) + jnp.log(1.0 + jnp.exp(-jnp.abs(x)))


def _prep_fn(sq, sk, ba, alog_e, dt_e):
    qn = sq * lax.rsqrt(_head_sums(sq * sq) + EPS) * (HEAD_DIM ** -0.5)
    kn = sk * lax.rsqrt(_head_sums(sk * sk) + EPS)
    r = lax.broadcasted_iota(jnp.int32, (128, GROUP_W), 0)
    c = lax.broadcasted_iota(jnp.int32, (128, GROUP_W), 1) >> 6
    bl = _sel_dot(ba, jnp.where(r == c, 1.0, 0.0).astype(f32))
    al = _sel_dot(ba, jnp.where(r == c + N_HEADS, 1.0, 0.0).astype(f32))
    beta = jax.nn.sigmoid(bl)
    g = -jnp.exp(alog_e) * _softplus(al + dt_e)
    ri = lax.broadcasted_iota(jnp.int32, (TOK_TILE, TOK_TILE), 0)
    ci = lax.broadcasted_iota(jnp.int32, (TOK_TILE, TOK_TILE), 1)
    within = jnp.where(((ri >> 6) == (ci >> 6)) & (ri >= ci), 1.0, 0.0).astype(f32)
    return qn, kn, beta, _sel_dot(within, g)


def _gnorm_fn(o, z, ng_e):
    ms = _head_sums(o * o) * (1.0 / HEAD_DIM)
    return o * lax.rsqrt(ms + EPS) * ng_e * (z * jax.nn.sigmoid(z))


def _tok_spec(width, col):
    return pl.BlockSpec((TOK_TILE, width), lambda i: (i, col))


def _conv_taps(xs_ref, w_ref, base, n):
    acc = w_ref[CONV_WIDTH - 1:CONV_WIDTH, :] * xs_ref[pl.ds(base, n), :]
    for j in range(CONV_WIDTH - 1):
        acc = acc + w_ref[j:j + 1, :] * xs_ref[pl.ds(base - (CONV_WIDTH - 1) + j, n), :]
    return acc


def _conv_silu_fwd(proj, conv_w):
    s = proj.shape[0]
    wd = 3 * GROUP_W
    hb = TOK_TILE // 8

    def body(x_ref, halo_ref, w_ref, o_ref, xs):
        xs[0:8, :] = jnp.where(pl.program_id(0) > 0, halo_ref[...], 0.0)
        xs[8:, :] = x_ref[...]
        y = _conv_taps(xs, w_ref, 8, TOK_TILE)
        o_ref[...] = y * jax.nn.sigmoid(y)

    return pl.pallas_call(
        body, name="delta_conv_fwd", grid=(s // TOK_TILE,),
        in_specs=[_tok_spec(wd, 1), pl.BlockSpec((8, wd), lambda i: (jnp.maximum(i * hb - 1, 0), 1)),
                  pl.BlockSpec((CONV_WIDTH, wd), lambda i: (0, 0))],
        out_specs=_tok_spec(wd, 0),
        out_shape=jax.ShapeDtypeStruct((s, wd), f32),
        scratch_shapes=[pltpu.VMEM((TOK_TILE + 8, wd), f32)],
        compiler_params=_cparams(("parallel",)),
    )(proj, proj, conv_w)


def _conv_silu_bwd(proj, conv_w, ds, xchg):
    s = proj.shape[0]
    wd = 3 * GROUP_W
    hb = TOK_TILE // 8
    nt = s // TOK_TILE

    def body(x_ref, hp_ref, hn_ref, ds_ref, dsn_ref, w_ref, dx_ref, dw_ref, xs, dys):
        i = pl.program_id(0)

        @pl.when(i == 0)
        def _():
            dw_ref[...] = jnp.zeros_like(dw_ref)

        last = i == nt - 1
        xs[0:8, :] = jnp.where(i > 0, hp_ref[...], 0.0)
        xs[8:8 + TOK_TILE, :] = x_ref[...]
        xs[8 + TOK_TILE:, :] = jnp.where(last, 0.0, hn_ref[...])
        y = _conv_taps(xs, w_ref, 8, TOK_TILE)
        sg = jax.nn.sigmoid(y)
        dys[0:TOK_TILE, :] = ds_ref[...] * sg * (1.0 + y * (1.0 - sg))
        yn = _conv_taps(xs, w_ref, 8 + TOK_TILE, 8)
        sgn = jax.nn.sigmoid(yn)
        dys[TOK_TILE:, :] = jnp.where(last, 0.0, dsn_ref[...]) * sgn * (1.0 + yn * (1.0 - sgn))
        dy0 = dys[0:TOK_TILE, :]
        dx = w_ref[CONV_WIDTH - 1:CONV_WIDTH, :] * dy0
        for j in range(CONV_WIDTH - 1):
            dx = dx + w_ref[j:j + 1, :] * dys[pl.ds(CONV_WIDTH - 1 - j, TOK_TILE), :]
        dx_ref[...] = dx
        for j in range(CONV_WIDTH):
            dw_ref[j:j + 1, :] += jnp.sum(dy0 * xs[pl.ds(8 - (CONV_WIDTH - 1) + j, TOK_TILE), :],
                                          axis=0, keepdims=True)

    prev8 = lambda col: pl.BlockSpec((8, wd), lambda i: (jnp.maximum(i * hb - 1, 0), col))
    next8 = lambda col: pl.BlockSpec((8, wd), lambda i: (jnp.minimum((i + 1) * hb, s // 8 - 1), col))
    out = pl.pallas_call(
        _ride(body, 6, 2, xchg, (nt,)), name="delta_conv_bwd", grid=(nt,),
        in_specs=[_tok_spec(wd, 1), prev8(1), next8(1), _tok_spec(wd, 0), next8(0),
                  pl.BlockSpec((CONV_WIDTH, wd), lambda i: (0, 0))] + [_ANY] * xchg.n,
        out_specs=[_tok_spec(wd, 0), pl.BlockSpec((CONV_WIDTH, wd), lambda i: (0, 0))] + [_ANY] * xchg.n,
        out_shape=[jax.ShapeDtypeStruct((s, wd), f32), jax.ShapeDtypeStruct((CONV_WIDTH, wd), f32)] + xchg.out_shape(),
        scratch_shapes=[pltpu.VMEM((TOK_TILE + 16, wd), f32), pltpu.VMEM((TOK_TILE + 8, wd), f32)] + xchg.scratch(),
        compiler_params=_cparams(("arbitrary",)),
    )(proj, proj, proj, ds, ds, conv_w, *xchg.arrs)
    return out[:2], out[2:]


def _delta_prep_fwd(sconv, proj, alog_e, dt_e):
    s = sconv.shape[0]

    def body(sq_ref, sk_ref, ba_ref, al_ref, dt_ref, q_ref, k_ref, b_ref, g_ref):
        qn, kn, beta, g = _prep_fn(sq_ref[...], sk_ref[...], ba_ref[...], al_ref[...], dt_ref[...])
        q_ref[...] = qn
        k_ref[...] = kn
        b_ref[...] = beta
        g_ref[...] = g

    return pl.pallas_call(
        body, name="delta_prep_fwd", grid=(s // TOK_TILE,),
        in_specs=[_tok_spec(GROUP_W, 0), _tok_spec(GROUP_W, 1), _tok_spec(128, BA_BLOCK),
                  _vec_spec(GROUP_W), _vec_spec(GROUP_W)],
        out_specs=[_tok_spec(GROUP_W, 0)] * 4,
        out_shape=[jax.ShapeDtypeStruct((s, GROUP_W), f32)] * 4,
        compiler_params=_cparams(("parallel",)),
    )(sconv, sconv, proj, alog_e, dt_e)


def _delta_prep_bwd(sconv, proj, alog_e, dt_e, dqn, dkn, dbeta, dg):
    s = sconv.shape[0]

    def body(sq_ref, sk_ref, ba_ref, al_ref, dt_ref, dq_ref, dk_ref, db_ref, dg_ref,
             dsq_ref, dsk_ref, dba_ref, dal_ref, ddt_ref):
        @pl.when(pl.program_id(0) == 0)
        def _():
            dal_ref[...] = jnp.zeros_like(dal_ref)
            ddt_ref[...] = jnp.zeros_like(ddt_ref)

        _, vjp = jax.vjp(_prep_fn, sq_ref[...], sk_ref[...], ba_ref[...], al_ref[...], dt_ref[...])
        dsq, dsk, dba, dal, ddt = vjp((dq_ref[...], dk_ref[...], db_ref[...], dg_ref[...]))
        dsq_ref[...] = dsq
        dsk_ref[...] = dsk
        dba_ref[...] = dba
        dal_ref[...] += dal
        ddt_ref[...] += ddt

    return pl.pallas_call(
        body, name="delta_prep_bwd", grid=(s // TOK_TILE,),
        in_specs=[_tok_spec(GROUP_W, 0), _tok_spec(GROUP_W, 1), _tok_spec(128, BA_BLOCK),
                  _vec_spec(GROUP_W), _vec_spec(GROUP_W)] + [_tok_spec(GROUP_W, 0)] * 4,
        out_specs=[_tok_spec(GROUP_W, 0), _tok_spec(GROUP_W, 0), _tok_spec(128, 0),
                   _acc_spec(GROUP_W), _acc_spec(GROUP_W)],
        out_shape=[jax.ShapeDtypeStruct((s, GROUP_W), f32)] * 2 + [jax.ShapeDtypeStruct((s, 128), f32)]
        + [jax.ShapeDtypeStruct((1, GROUP_W), f32)] * 2,
        compiler_params=_cparams(("arbitrary",)),
    )(sconv, sconv, proj, alog_e, dt_e, dqn, dkn, dbeta, dg)


def _gnorm_fwd(o, proj, ng_e):
    s = o.shape[0]

    def body(o_ref, z_ref, g_ref, y_ref):
        y_ref[...] = _gnorm_fn(o_ref[...], z_ref[...], g_ref[...])

    return pl.pallas_call(
        body, name="delta_gnorm_fwd", grid=(s // TOK_TILE,),
        in_specs=[_tok_spec(GROUP_W, 0), _tok_spec(GROUP_W, Z_COL // GROUP_W), _vec_spec(GROUP_W)],
        out_specs=_tok_spec(GROUP_W, 0),
        out_shape=jax.ShapeDtypeStruct((s, GROUP_W), f32),
        compiler_params=_cparams(("parallel",)),
    )(o, proj, ng_e)


def _gnorm_bwd(o, proj, ng_e, dycat):
    s = o.shape[0]

    def body(o_ref, z_ref, g_ref, dy_ref, do_ref, dz_ref, dg_ref):
        @pl.when(pl.program_id(0) == 0)
        def _():
            dg_ref[...] = jnp.zeros_like(dg_ref)

        _, vjp = jax.vjp(_gnorm_fn, o_ref[...], z_ref[...], g_ref[...])
        do, dz, dg = vjp(dy_ref[...])
        do_ref[...] = do
        dz_ref[...] = dz
        dg_ref[...] += dg

    return pl.pallas_call(
        body, name="delta_gnorm_bwd", grid=(s // TOK_TILE,),
        in_specs=[_tok_spec(GROUP_W, 0), _tok_spec(GROUP_W, Z_COL // GROUP_W), _vec_spec(GROUP_W),
                  _tok_spec(GROUP_W, 1)],
        out_specs=[_tok_spec(GROUP_W, 0), _tok_spec(GROUP_W, 0), _acc_spec(GROUP_W)],
        out_shape=[jax.ShapeDtypeStruct((s, GROUP_W), f32)] * 2 + [jax.ShapeDtypeStruct((1, GROUP_W), f32)],
        compiler_params=_cparams(("arbitrary",)),
    )(o, proj, ng_e, dycat)


_MESH = pl.DeviceIdType.MESH
_ANY = pl.BlockSpec(memory_space=pl.ANY)
_VMEM = pl.BlockSpec(memory_space=pltpu.VMEM)


def _my_place():
    x, y, c = lax.axis_index("x"), lax.axis_index("y"), lax.axis_index("c")
    return x, y, c, 4 * x + 2 * y + c


def _peer(k, x, y, c):
    px = 1 - x if k & 4 else x
    py = 1 - y if k & 2 else y
    pc = 1 - c if k & 1 else c
    return (px, py, pc), 4 * px + 2 * py + pc


def _exchange_all(src_of_peer, dst_ref, send_sems, recv_sems, x, y, c, me):
    sent = []
    for k in range(1, N_DEV):
        dev, pidx = _peer(k, x, y, c)
        cp = pltpu.make_async_remote_copy(src_ref=src_of_peer(pidx), dst_ref=dst_ref.at[me],
                                          send_sem=send_sems.at[k - 1], recv_sem=recv_sems.at[k - 1],
                                          device_id=dev, device_id_type=_MESH)
        cp.start()
        sent.append(cp)
    for k in range(1, N_DEV):
        dev, pidx = _peer(k, x, y, c)
        pltpu.make_async_remote_copy(src_ref=src_of_peer(pidx), dst_ref=dst_ref.at[pidx],
                                     send_sem=send_sems.at[k - 1], recv_sem=recv_sems.at[k - 1],
                                     device_id=dev, device_id_type=_MESH).wait_recv()
    for cp in sent:
        cp.wait_send()


def _ada_exchange(cv8, w_ada, b_ada8):
    def body(cv_ref, w_ref, b_ref, call_ref, modp_ref, part_s, s1, r1, s2, r2):
        x, y, c, me = _my_place()
        call_ref[me] = cv_ref[...]
        _exchange_all(lambda pidx: cv_ref, call_ref, s1, r1, x, y, c, me)
        bias = b_ref[me]
        for j in range(N_DEV):
            cj = call_ref[j][:, :D_MODEL]
            part_s[j] = _hdot(cj * jax.nn.sigmoid(cj), w_ref[...]) + bias
        modp_ref[me] = part_s[me]
        _exchange_all(lambda pidx: part_s.at[pidx], modp_ref, s2, r2, x, y, c, me)

    nsh = w_ada.shape[1]
    return pl.pallas_call(
        body, name="ada_exchange",
        in_specs=[_VMEM, _VMEM, _VMEM], out_specs=[_VMEM, _VMEM],
        out_shape=[jax.ShapeDtypeStruct((N_DEV, 8, cv8.shape[1]), f32), jax.ShapeDtypeStruct((N_DEV, 8, nsh), f32)],
        scratch_shapes=[pltpu.VMEM((N_DEV, 8, nsh), f32)] + [pltpu.SemaphoreType.DMA((N_DEV - 1,))] * 4,
        compiler_params=pltpu.CompilerParams(vmem_limit_bytes=VMEM_LIMIT),
    )(cv8, w_ada, b_ada8)


def _all_to_all(arrs, name):
    ex = _Exchange(arrs, gather=False)

    def body(*refs):
        srcs, dsts, sems = refs[:ex.n], refs[ex.n:2 * ex.n], refs[2 * ex.n:]
        ex.start(srcs, dsts, sems)
        ex.wait(srcs, dsts, sems)

    return pl.pallas_call(
        body, name=name,
        in_specs=[_ANY] * ex.n, out_specs=[_ANY] * ex.n,
        out_shape=ex.out_shape(), scratch_shapes=ex.scratch(),
    )(*arrs)


class _Exchange:
    def __init__(self, arrs, gather):
        self.arrs, self.gather, self.n = list(arrs), gather, len(arrs)

    def out_shape(self):
        return [jax.ShapeDtypeStruct(((N_DEV,) + a.shape) if self.gather else a.shape, a.dtype) for a in self.arrs]

    def scratch(self):
        if self.n == 0:
            return []
        return [pltpu.SemaphoreType.DMA((self.n, N_DEV - 1)), pltpu.SemaphoreType.DMA((self.n, N_DEV - 1)),
                pltpu.SemaphoreType.DMA((self.n,))]

    def _src(self, srcs, a, idx):
        return srcs[a] if self.gather else srcs[a].at[idx]

    def _copies(self, srcs, dsts, sems, incoming):
        send_sems, recv_sems, _ = sems
        x, y, c, me = _my_place()
        out = []
        for a in range(self.n):
            for k in range(1, N_DEV):
                dev, pidx = _peer(k, x, y, c)
                out.append(pltpu.make_async_remote_copy(
                    src_ref=self._src(srcs, a, pidx), dst_ref=dsts[a].at[pidx if incoming else me],
                    send_sem=send_sems.at[a, k - 1], recv_sem=recv_sems.at[a, k - 1],
                    device_id=dev, device_id_type=_MESH))
        return out

    def _local(self, srcs, dsts, sems):
        me = _my_place()[3]
        return [pltpu.make_async_copy(self._src(srcs, a, me), dsts[a].at[me], sems[2].at[a]) for a in range(self.n)]

    def start(self, srcs, dsts, sems):
        for cp in self._local(srcs, dsts, sems) + self._copies(srcs, dsts, sems, incoming=False):
            cp.start()

    def wait(self, srcs, dsts, sems):
        for cp in self._copies(srcs, dsts, sems, incoming=True):
            cp.wait_recv()
        for cp in self._copies(srcs, dsts, sems, incoming=False):
            cp.wait_send()
        for cp in self._local(srcs, dsts, sems):
            cp.wait()

    def start_at_first_step(self, grid, srcs, dsts, sems):
        first = functools.reduce(jnp.logical_and, [pl.program_id(i) == 0 for i in range(len(grid))])
        pl.when(first)(lambda: self.start(srcs, dsts, sems))

    def wait_at_last_step(self, grid, srcs, dsts, sems):
        last = functools.reduce(jnp.logical_and, [pl.program_id(i) == g - 1 for i, g in enumerate(grid)])
        pl.when(last)(lambda: self.wait(srcs, dsts, sems))


def _ride(body, n_in, n_out, xchg, grid):
    nx = xchg.n
    if nx == 0:
        return body

    def wrapped(*refs):
        ins, xs = refs[:n_in], refs[n_in:n_in + nx]
        outs, xd = refs[n_in + nx:n_in + nx + n_out], refs[n_in + nx + n_out:n_in + 2 * nx + n_out]
        scratch = refs[n_in + 2 * nx + n_out:]
        xchg.start_at_first_step(grid, xs, xd, scratch[-3:])
        body(*ins, *outs, *scratch[:-3])
        xchg.wait_at_last_step(grid, xs, xd, scratch[-3:])

    return wrapped


def _all_gather_weights(shards):
    n = len(shards)

    def body(*refs):
        srcs, outs = refs[:n], refs[n:2 * n]
        send_sems, recv_sems, local_sems = refs[2 * n:]
        x, y, c, me = _my_place()
        sib = (x, y, 1 - c)
        chips = [(1 - x, y), (x, 1 - y), (1 - x, 1 - y)]

        def idx(px, py, pc):
            return 4 * px + 2 * py + pc

        def copy(a, k, block, to, src=None):
            rows = outs[a].at[idx(*block)]
            return pltpu.make_async_remote_copy(src_ref=rows if src is None else src, dst_ref=rows,
                                                send_sem=send_sems.at[a, k], recv_sem=recv_sems.at[a, k],
                                                device_id=to, device_id_type=_MESH)

        mine, first, passed = [], [], []
        for a in range(n):
            cp = pltpu.make_async_copy(srcs[a], outs[a].at[me], local_sems.at[a])
            cp.start()
            mine.append(cp)
            fa = [copy(a, 0, (x, y, c), sib, src=srcs[a])]
            fa += [copy(a, 1 + j, (x, y, c), (*chip, c), src=srcs[a]) for j, chip in enumerate(chips)]
            for cp in fa:
                cp.start()
            first += fa
        for a in range(n):
            for j, chip in enumerate(chips):
                copy(a, 1 + j, (*chip, c), (x, y, c)).wait_recv()
                cp = copy(a, 4 + j, (*chip, c), sib)
                cp.start()
                passed.append(cp)
        for a in range(n):
            copy(a, 0, (x, y, 1 - c), (x, y, c)).wait_recv()
            for j, chip in enumerate(chips):
                copy(a, 4 + j, (*chip, 1 - c), (x, y, c)).wait_recv()
        for cp in first + passed:
            cp.wait_send()
        for cp in mine:
            cp.wait()

    return pl.pallas_call(
        body, name="gather_weights",
        in_specs=[_ANY] * n, out_specs=[_ANY] * n,
        out_shape=[jax.ShapeDtypeStruct((N_DEV,) + a.shape, a.dtype) for a in shards],
        scratch_shapes=[pltpu.SemaphoreType.DMA((n, N_DEV - 1)), pltpu.SemaphoreType.DMA((n, N_DEV - 1)),
                        pltpu.SemaphoreType.DMA((n,))],
    )(*shards)


def _adamw_math(w, g, m, v):
    m2 = ADAM_B1 * m + (1.0 - ADAM_B1) * g
    v2 = ADAM_B2 * v + (1.0 - ADAM_B2) * (g * g)
    m_hat = m2 / (1.0 - ADAM_B1 ** ADAM_STEP)
    v_hat = v2 / (1.0 - ADAM_B2 ** ADAM_STEP)
    delta = -ADAM_LR * (m_hat / (jnp.sqrt(v_hat) + ADAM_EPS) + ADAM_WD * w)
    return delta, m2, v2


def _row_tile(rows):
    for t in (256, 128, 64, 32, 16, 8):
        if rows % t == 0:
            return t
    return rows


def _reduce_adamw(parts, w, m, v, name):
    _, r, cdim = parts.shape
    tr = _row_tile(r)

    def body(p_ref, w_ref, m_ref, v_ref, g_ref, d_ref, m2_ref, v2_ref):
        g = p_ref[0].astype(f32)
        for j in range(1, N_DEV):
            g = g + p_ref[j].astype(f32)
        delta, m2, v2 = _adamw_math(w_ref[...], g, m_ref[...], v_ref[...])
        g_ref[...] = g
        d_ref[...] = delta
        m2_ref[...] = m2
        v2_ref[...] = v2

    spec = pl.BlockSpec((tr, cdim), lambda i: (i, 0))
    return pl.pallas_call(
        body, name=name, grid=(r // tr,),
        in_specs=[pl.BlockSpec((N_DEV, tr, cdim), lambda i: (0, i, 0)), spec, spec, spec],
        out_specs=[spec] * 4,
        out_shape=[jax.ShapeDtypeStruct((r, cdim), f32)] * 4,
        compiler_params=_cparams(("parallel",)),
    )(parts, w, m, v)


def _adamw(w, g, m, v, name):
    r, cdim = w.shape
    tr = _row_tile(r)

    def body(w_ref, g_ref, m_ref, v_ref, d_ref, m2_ref, v2_ref):
        delta, m2, v2 = _adamw_math(w_ref[...], g_ref[...], m_ref[...], v_ref[...])
        d_ref[...] = delta
        m2_ref[...] = m2
        v2_ref[...] = v2

    spec = pl.BlockSpec((tr, cdim), lambda i: (i, 0))
    return pl.pallas_call(
        body, name=name, grid=(r // tr,),
        in_specs=[spec] * 4, out_specs=[spec] * 3,
        out_shape=[jax.ShapeDtypeStruct((r, cdim), f32)] * 3,
        compiler_params=_cparams(("parallel",)),
    )(w, g, m, v)


def _sum_devices(parts, name):
    _, r, cdim = parts.shape

    def body(p_ref, o_ref):
        g = p_ref[0]
        for j in range(1, N_DEV):
            g = g + p_ref[j]
        o_ref[...] = g

    return pl.pallas_call(
        body, name=name, out_shape=jax.ShapeDtypeStruct((r, cdim), f32),
        in_specs=[_VMEM], out_specs=_VMEM,
    )(parts)


def _ada_wgrad(c_all8, dmod_cols):
    nsh = dmod_cols.shape[1]

    def body(c_ref, d_ref, o_ref):
        cv = c_ref[...]
        o_ref[...] = lax.dot_general(cv * jax.nn.sigmoid(cv), d_ref[...], _TN, precision=_HI,
                                     preferred_element_type=f32)

    return pl.pallas_call(
        body, name="ada_wgrad", out_shape=jax.ShapeDtypeStruct((D_MODEL, nsh), f32),
        in_specs=[_VMEM, _VMEM], out_specs=_VMEM,
        compiler_params=pltpu.CompilerParams(vmem_limit_bytes=VMEM_LIMIT),
    )(c_all8, dmod_cols)


def _cols(t):
    return t.transpose(1, 0, 2).reshape(t.shape[1], N_DEV * t.shape[2])


def _col_blocks(t, n):
    return t.reshape(t.shape[0], N_DEV, n).transpose(1, 0, 2).astype(bf16)


def _row_blocks(t):
    return t.reshape(N_DEV, t.shape[0] // N_DEV, t.shape[1]).astype(bf16)


def _local_step(x, tgt, mod, norm_attn_g, w_in_p, rel_bias, conv_full, a_log, dt_bias, delta_norm_g,
                norm_ffn_g, final_norm_g, w_out_sh, w_gate_sh, w_up_sh, w_down_sh):
    s = x.shape[0]
    sh1, sc1, g1, sh2, sc2, g2 = [mod[:, i * D_MODEL:(i + 1) * D_MODEL] for i in range(6)]
    nag = norm_attn_g.reshape(1, D_MODEL)
    nfg = norm_ffn_g.reshape(1, D_MODEL)
    fg = final_norm_g.reshape(1, D_MODEL)
    idx = _bucket_tables()
    bias = _bias_tables(rel_bias, idx)
    alog_e = jnp.repeat(a_log.reshape(N_HEADS), HEAD_DIM)[None]
    dt_e = jnp.repeat(dt_bias.reshape(N_HEADS), HEAD_DIM)[None]
    ng_e = jnp.tile(delta_norm_g.reshape(HEAD_DIM), N_HEADS)[None]

    h1 = _ln_mod_fwd(x, nag, sc1, sh1, "ln1_fwd")
    proj, (w_out_g, w_gate_g) = _mm(h1, w_in_p, "nn", f32, 512, 1280, 1024, "in_proj",
                                    xchg=_Exchange([w_out_sh, w_gate_sh], gather=True))
    (y_attn, lse), (w_up_g, w_down_g) = _attn_fwd(proj, bias, _Exchange([w_up_sh, w_down_sh], gather=True))
    w_out_b = w_out_g.reshape(2 * GROUP_W, D_MODEL)
    w_gu_b = jnp.concatenate([_cols(w_gate_g), _cols(w_up_g)], axis=1)
    w_down_b = w_down_g.reshape(D_FF, D_MODEL)
    n_ff = w_gate_sh.shape[1]
    sconv = _conv_silu_fwd(proj, conv_full)
    qn, kn, beta, g = _delta_prep_fwd(sconv, proj, alog_e, dt_e)
    u, w, qt, kh, qk, gm = _delta_chunk_pre(qn, kn, sconv, beta, g)
    o, ss = _delta_scan_fwd(u, w, qt, kh, qk, gm)
    y_delta = _gnorm_fwd(o, proj, ng_e)
    ycat = jnp.concatenate([y_attn, y_delta], axis=1).astype(bf16)
    y = _mm(ycat, w_out_b, "nn", f32, 512, 1024, 1024, "out_proj")
    x1, h2 = _resid_ln_mod_fwd(x, y, g1, nfg, sc2, sh2, "ln2_fwd")
    gu = _mm(h2, w_gu_b, "nn", f32, 512, 1408, 1024, "ffn_up")
    act = _swiglu_fwd(gu, "swiglu_fwd")
    y2 = _mm(act, w_down_b, "nn", f32, 512, 1024, D_FF, "ffn_down")
    dx2, dy2, loss, dfg, dg2 = _final_loss_bwd(x1, y2, g2, fg, tgt, "final_loss")

    dact = _mm(dy2, w_down_b, "nt", f32, 512, 1408, 1024, "ffn_down_dx")
    g_down = _mm(act, dy2, "tn", f32, 1408, 1024, 512, "ffn_down_dw")
    dgu = _swiglu_bwd(gu, dact, "swiglu_bwd")
    dh2, (r_down,) = _mm(dgu, w_gu_b, "nt", f32, 512, 1024, 1408, "ffn_up_dx",
                         xchg=_Exchange([_row_blocks(g_down)], gather=False))
    g_gu = _mm(h2, dgu, "tn", f32, 1024, 1408, 512, "ffn_up_dw")
    dx1, dsh2, dsc2, dnfg, dy, dg1 = _ln_mod_bwd(x1, nfg, sc2, dh2, dx2, "ln2_bwd", gate=g1, y=y)
    dycat = _mm(dy, w_out_b, "nt", f32, 512, 1024, 1024, "out_proj_dx")
    g_out = _mm(ycat, dy, "tn", f32, 1024, 1024, 512, "out_proj_dw")
    dq, dk, dv, dbias = _attn_bwd(proj, bias, y_attn, lse, dycat)
    g_rb = _bias_grad(dbias, idx)[:, :, 0].T
    do, dz, dng = _gnorm_bwd(o, proj, ng_e, dycat)
    dso = _delta_scan_bwd(w, qt, kh, qk, gm, do)
    dqn, dkn, dvd, dbeta, dgd = _delta_chunk_bwd(qn, kn, sconv, beta, g, ss, dso, do)
    dsq, dsk, dba, dal, ddt = _delta_prep_bwd(sconv, proj, alog_e, dt_e, dqn, dkn, dbeta, dgd)
    (dxc, g_conv), (r_gate, r_up, r_out) = _conv_silu_bwd(
        proj, conv_full, jnp.concatenate([dsq, dsk, dvd], axis=1),
        _Exchange([_col_blocks(g_gu[:, :D_FF], n_ff), _col_blocks(g_gu[:, D_FF:], n_ff), _row_blocks(g_out)],
                  gather=False))
    dproj = jnp.concatenate([dq, dk, dv, dxc, dz, dba, jnp.zeros((s, IN_PAD - BA_BLOCK * 128 - 128), f32)],
                            axis=1).astype(bf16)
    g_in = _mm(h1, dproj, "tn", f32, 1024, 1280, 512, "in_proj_dw")
    dh1, (r_in,) = _mm(dproj, w_in_p, "nt", f32, 512, 1024, 1280, "in_proj_dx",
                       xchg=_Exchange([_col_blocks(g_in[:, :IN_WIDTH], IN_WIDTH // N_DEV)], gather=False))
    gx, dsh1, dsc1, dnag = _ln_mod_bwd(x, nag, sc1, dh1, dx1, "ln1_bwd")
    grads = dict(
        x=gx, mod=jnp.concatenate([dsh1, dsc1, dg1, dsh2, dsc2, dg2], axis=1),
        norm_attn_g=dnag, norm_ffn_g=dnfg, final_norm_g=dfg, rel_bias=g_rb, conv_w=g_conv,
        a_log=dal.reshape(N_HEADS, HEAD_DIM).sum(-1), dt_bias=ddt.reshape(N_HEADS, HEAD_DIM).sum(-1),
        delta_norm_g=dng.reshape(N_HEADS, HEAD_DIM).sum(0),
        w_in=r_in, w_out=r_out, w_gate=r_gate, w_up=r_up, w_down=r_down)
    return loss[0, 0], grads


MISC_OFF = dict(rel_bias=0, a_log=256, dt_bias=264, delta_norm_g=272)


def _misc_row(rel_bias, a_log, dt_bias, delta_norm_g):
    flat = jnp.concatenate([rel_bias.reshape(-1), a_log.reshape(-1), dt_bias.reshape(-1), delta_norm_g.reshape(-1)])
    return jnp.pad(flat, (0, D_MODEL - flat.shape[0]))[None]


def _pack_small(b_ada, nag, nfg, fng, rel_bias, a_log, dt_bias, dng, conv_shard):
    rows = [b_ada.reshape(6, D_MODEL), nag.reshape(1, D_MODEL), nfg.reshape(1, D_MODEL), fng.reshape(1, D_MODEL),
            _misc_row(rel_bias, a_log, dt_bias, dng),
            jnp.pad(conv_shard.reshape(-1), (0, D_MODEL - conv_shard.size))[None],
            jnp.zeros((5, D_MODEL), f32)]
    return jnp.concatenate(rows, axis=0)


def _unpack_small(p, conv_shape):
    misc = p[9]
    return dict(
        b_ada=p[0:6].reshape(1, 6 * D_MODEL), norm_attn_g=p[6:7], norm_ffn_g=p[7:8], final_norm_g=p[8],
        rel_bias=misc[0:256].reshape(N_BUCKETS, N_HEADS), a_log=misc[256:264].reshape(1, N_HEADS),
        dt_bias=misc[264:272].reshape(1, N_HEADS), delta_norm_g=misc[272:336].reshape(1, HEAD_DIM),
        conv_w=p[10, :conv_shape[1] * conv_shape[2]].reshape(conv_shape))


def kernel(x, c, w_ada, b_ada, norm_attn_g, w_in, rel_bias, conv_w, a_log, dt_bias, delta_norm_g, w_out, norm_ffn_g, w_gate, w_up, w_down, final_norm_g, loss_target, m_w_ada, m_b_ada, m_norm_attn_g, m_w_in, m_rel_bias, m_conv_w, m_a_log, m_dt_bias, m_delta_norm_g, m_w_out, m_norm_ffn_g, m_w_gate, m_w_up, m_w_down, m_final_norm_g, v_w_ada, v_b_ada, v_norm_attn_g, v_w_in, v_rel_bias, v_conv_w, v_a_log, v_dt_bias, v_delta_norm_g, v_w_out, v_norm_ffn_g, v_w_gate, v_w_up, v_w_down, v_final_norm_g):
    me = 4 * lax.axis_index("x") + 2 * lax.axis_index("y") + lax.axis_index("c")
    ada_sh = w_ada.shape[2]
    conv_sh = conv_w.shape[2]

    cv = jnp.concatenate([c[0], conv_w[0].reshape(-1)])
    cv8 = jnp.zeros((8, 2 * D_MODEL), f32).at[0, :cv.shape[0]].set(cv)
    b8 = jnp.broadcast_to(b_ada.reshape(N_DEV, 1, ada_sh), (N_DEV, 8, ada_sh))
    call, modp = _ada_exchange(cv8, w_ada[0], b8)
    mod = modp[:, 0, :].reshape(1, 6 * D_MODEL)
    c_all = call[:, 0, :D_MODEL]
    conv_full = call[:, 0, D_MODEL:D_MODEL + CONV_WIDTH * conv_sh].reshape(N_DEV, CONV_WIDTH, conv_sh)
    conv_full = conv_full.transpose(1, 0, 2).reshape(CONV_WIDTH, N_DEV * conv_sh)

    w_in_p = jnp.pad(_cols(_all_gather_weights([w_in[0].astype(bf16)])[0]), ((0, 0), (0, IN_PAD - IN_WIDTH)))
    loss_local, gr = _local_step(x[0], loss_target[0], mod, norm_attn_g, w_in_p, rel_bias, conv_full, a_log,
                                 dt_bias, delta_norm_g, norm_ffn_g, final_norm_g, w_out[0].astype(bf16),
                                 w_gate[0].astype(bf16), w_up[0].astype(bf16), w_down[0].astype(bf16))
    loss = lax.psum(loss_local, ("x", "y", "c"))

    small = jnp.concatenate([
        gr["mod"].reshape(6, D_MODEL), gr["norm_attn_g"], gr["norm_ffn_g"], gr["final_norm_g"],
        gr["conv_w"].reshape(6, D_MODEL),
        _misc_row(gr["rel_bias"], gr["a_log"], gr["dt_bias"], gr["delta_norm_g"])], axis=0)
    parts = _all_to_all([jnp.broadcast_to(small[None], (N_DEV,) + small.shape)], "small_gather")[0]
    tot = _sum_devices(parts, "small_sum")
    g_conv_full = tot[9:15].reshape(CONV_WIDTH, N_DEV * conv_sh)
    g_conv = lax.dynamic_slice(g_conv_full, (0, me * conv_sh), (CONV_WIDTH, conv_sh))
    misc = tot[15]
    g_small = _pack_small(tot[0:6], tot[6], tot[7], tot[8], misc[0:256], misc[256:264], misc[264:272],
                          misc[272:336], g_conv)
    pk = lambda pre: _pack_small(pre[0], pre[1], pre[2], pre[3], pre[4], pre[5], pre[6], pre[7], pre[8])
    w_small = pk((b_ada, norm_attn_g, norm_ffn_g, final_norm_g, rel_bias, a_log, dt_bias, delta_norm_g, conv_w))
    m_small = pk((m_b_ada, m_norm_attn_g, m_norm_ffn_g, m_final_norm_g, m_rel_bias, m_a_log, m_dt_bias,
                  m_delta_norm_g, m_conv_w))
    v_small = pk((v_b_ada, v_norm_attn_g, v_norm_ffn_g, v_final_norm_g, v_rel_bias, v_a_log, v_dt_bias,
                  v_delta_norm_g, v_conv_w))
    d_small, m2_small, v2_small = _adamw(w_small, g_small, m_small, v_small, "adamw_small")
    cshape = conv_w.shape
    G, Dl, M2, V2 = (_unpack_small(t, cshape) for t in (g_small, d_small, m2_small, v2_small))

    dmod_all = parts[:, 0:6, :].reshape(N_DEV, 6 * D_MODEL)
    dmod_cols = lax.dynamic_slice(dmod_all, (0, me * ada_sh), (N_DEV, ada_sh))
    g_ada = _ada_wgrad(c_all, dmod_cols)
    d_ada, m2_ada, v2_ada = _adamw(w_ada[0], g_ada, m_w_ada[0], v_w_ada[0], "adamw_w_ada")

    big = {}
    for name, w_, m_, v_ in (("w_in", w_in, m_w_in, v_w_in), ("w_out", w_out, m_w_out, v_w_out),
                             ("w_gate", w_gate, m_w_gate, v_w_gate), ("w_up", w_up, m_w_up, v_w_up),
                             ("w_down", w_down, m_w_down, v_w_down)):
        big[name] = [t[None] for t in _reduce_adamw(gr[name], w_[0], m_[0], v_[0], "reduce_adamw_" + name)]

    def leaf(i, name):
        if name == "w_ada":
            return (g_ada, d_ada, m2_ada, v2_ada)[i][None]
        if name in big:
            return big[name][i]
        return (G, Dl, M2, V2)[i][name]

    order = ["w_ada", "b_ada", "norm_attn_g", "w_in", "rel_bias", "conv_w", "a_log", "dt_bias", "delta_norm_g",
             "w_out", "norm_ffn_g", "w_gate", "w_up", "w_down", "final_norm_g"]
    outs = [loss, gr["x"][None]]
    for i in range(4):
        outs += [leaf(i, n) for n in order]
    return tuple(outs)
```

```python
import functools
import math

import jax
import jax.numpy as jnp
from jax import lax
from jax.experimental import pallas as pl
from jax.experimental.pallas import tpu as pltpu

f32 = jnp.float32
bf16 = jnp.bfloat16

D_MODEL = 1024
HEAD_DIM = 64
N_HEADS = 8
GROUP_W = 512
IN_WIDTH = 3600
IN_PAD = 3840
D_FF = 2816
EPS = 1e-6
NEG_INF = -1e30
BAND = 128
PAD_UNIT = 2048
DILATIONS = (1, 4, 16)
N_BUCKETS = 32
MAX_DISTANCE = 2048
CONV_WIDTH = 4
CHUNK = 64
N_DEV = 8
VMEM_LIMIT = 56 * 1024 * 1024

ADAM_LR, ADAM_B1, ADAM_B2, ADAM_EPS, ADAM_WD, ADAM_STEP = 0.001, 0.9, 0.999, 1e-08, 0.01, 10


def _cparams(sem):
    return pltpu.CompilerParams(dimension_semantics=sem, vmem_limit_bytes=VMEM_LIMIT)


def _mm(a, b, mode, out_dtype, tm, tn, tk, name, xchg=None):
    if mode == "nn":
        (m, k), (_, n) = a.shape, b.shape
        a_spec = pl.BlockSpec((tm, tk), lambda j, i, kk: (i, kk))
        b_spec = pl.BlockSpec((tk, tn), lambda j, i, kk: (kk, j))
        dims = (((1,), (0,)), ((), ()))
    elif mode == "nt":
        (m, k), (n, _) = a.shape, b.shape
        a_spec = pl.BlockSpec((tm, tk), lambda j, i, kk: (i, kk))
        b_spec = pl.BlockSpec((tn, tk), lambda j, i, kk: (j, kk))
        dims = (((1,), (1,)), ((), ()))
    else:
        (k, m), (_, n) = a.shape, b.shape
        a_spec = pl.BlockSpec((tk, tm), lambda j, i, kk: (kk, i))
        b_spec = pl.BlockSpec((tk, tn), lambda j, i, kk: (kk, j))
        dims = (((0,), (0,)), ((), ()))
    assert m % tm == 0 and n % tn == 0 and k % tk == 0, (name, m, n, k, tm, tn, tk)
    nk = k // tk
    grid = (n // tn, m // tm, nk)
    nx = xchg.n if xchg is not None else 0

    def body(*refs):
        a_ref, b_ref = refs[:2]
        o_ref = refs[2 + nx]
        scratch = refs[3 + 2 * nx:]
        if nx:
            xrefs = (refs[2:2 + nx], refs[3 + nx:3 + 2 * nx], scratch[-3:])
            xchg.start_at_first_step(grid, *xrefs)
        if nk == 1:
            o_ref[...] = lax.dot_general(a_ref[...].astype(bf16), b_ref[...].astype(bf16), dims,
                                         preferred_element_type=f32).astype(o_ref.dtype)
        else:
            acc_ref = scratch[0]
            kk = pl.program_id(2)

            @pl.when(kk == 0)
            def _():
                acc_ref[...] = jnp.zeros_like(acc_ref)

            acc_ref[...] += lax.dot_general(a_ref[...].astype(bf16), b_ref[...].astype(bf16), dims,
                                            preferred_element_type=f32)

            @pl.when(kk == nk - 1)
            def _():
                o_ref[...] = acc_ref[...].astype(o_ref.dtype)
        if nx:
            xchg.wait_at_last_step(grid, *xrefs)

    out = pl.pallas_call(
        body, name=name, grid=grid,
        in_specs=[a_spec, b_spec] + ([_ANY] * nx),
        out_specs=[pl.BlockSpec((tm, tn), lambda j, i, kk: (i, j))] + ([_ANY] * nx),
        out_shape=[jax.ShapeDtypeStruct((m, n), out_dtype)] + (xchg.out_shape() if nx else []),
        scratch_shapes=([pltpu.VMEM((tm, tn), f32)] if nk > 1 else []) + (xchg.scratch() if nx else []),
        compiler_params=_cparams(("arbitrary",) * 3 if nx else ("parallel", "parallel", "arbitrary")),
    )(a, b, *(xchg.arrs if nx else []))
    return (out[0], out[1:]) if nx else out[0]


def _mm_nt2(a1, b1, a2, b2, tm, tn, tk, name, xchg):
    (m, k), (n, _) = a1.shape, b1.shape
    assert a2.shape == a1.shape and b2.shape == b1.shape and m % tm == 0 and n % tn == 0 and k % tk == 0
    nk = k // tk
    grid = (n // tn, m // tm, 2 * nk)
    one = lambda kk: jnp.minimum(kk, nk - 1)
    two = lambda kk: jnp.maximum(kk - nk, 0)

    def body(a1_ref, b1_ref, a2_ref, b2_ref, o_ref, acc_ref):
        kk = pl.program_id(2)

        @pl.when(kk == 0)
        def _():
            acc_ref[...] = jnp.zeros_like(acc_ref)

        @pl.when(kk < nk)
        def _():
            acc_ref[...] += lax.dot_general(a1_ref[...], b1_ref[...], _NT, preferred_element_type=f32)

        @pl.when(kk >= nk)
        def _():
            acc_ref[...] += lax.dot_general(a2_ref[...], b2_ref[...], _NT, preferred_element_type=f32)

        @pl.when(kk == 2 * nk - 1)
        def _():
            o_ref[...] = acc_ref[...]

    out = pl.pallas_call(
        _ride(body, 4, 1, xchg, grid), name=name, grid=grid,
        in_specs=[pl.BlockSpec((tm, tk), lambda j, i, kk: (i, one(kk))),
                  pl.BlockSpec((tn, tk), lambda j, i, kk: (j, one(kk))),
                  pl.BlockSpec((tm, tk), lambda j, i, kk: (i, two(kk))),
                  pl.BlockSpec((tn, tk), lambda j, i, kk: (j, two(kk)))] + [_ANY] * xchg.n,
        out_specs=[pl.BlockSpec((tm, tn), lambda j, i, kk: (i, j))] + [_ANY] * xchg.n,
        out_shape=[jax.ShapeDtypeStruct((m, n), f32)] + xchg.out_shape(),
        scratch_shapes=[pltpu.VMEM((tm, tn), f32)] + xchg.scratch(),
        compiler_params=_cparams(("arbitrary",) * 3),
    )(a1, b1, a2, b2, *xchg.arrs)
    return out[0], out[1:]


TOK_TILE = 512


def _row_spec(width, tile=TOK_TILE):
    return pl.BlockSpec((tile, width), lambda i: (i, 0))


def _vec_spec(width, rows=1):
    return pl.BlockSpec((rows, width), lambda i: (0, 0))


def _ln_mod_fwd(x, gain, sc, sh, name):
    s, d = x.shape

    def body(x_ref, g_ref, sc_ref, sh_ref, h_ref):
        xv = x_ref[...]
        rstd = lax.rsqrt(jnp.mean(xv * xv, axis=-1, keepdims=True) + EPS)
        h = (xv * rstd) * g_ref[...] * (1.0 + sc_ref[...]) + sh_ref[...]
        h_ref[...] = h.astype(bf16)

    return pl.pallas_call(
        body, name=name, grid=(s // TOK_TILE,),
        in_specs=[_row_spec(d), _vec_spec(d), _vec_spec(d), _vec_spec(d)],
        out_specs=_row_spec(d),
        out_shape=jax.ShapeDtypeStruct((s, d), bf16),
        compiler_params=_cparams(("parallel",)),
    )(x, gain, sc, sh)


def _resid_ln_mod_fwd(x, y, gate, gain, sc, sh, name):
    s, d = x.shape

    def body(x_ref, y_ref, gt_ref, g_ref, sc_ref, sh_ref, x1_ref, h_ref):
        x1 = x_ref[...] + gt_ref[...] * y_ref[...]
        x1_ref[...] = x1
        rstd = lax.rsqrt(jnp.mean(x1 * x1, axis=-1, keepdims=True) + EPS)
        h = (x1 * rstd) * g_ref[...] * (1.0 + sc_ref[...]) + sh_ref[...]
        h_ref[...] = h.astype(bf16)

    return pl.pallas_call(
        body, name=name, grid=(s // TOK_TILE,),
        in_specs=[_row_spec(d), _row_spec(d)] + [_vec_spec(d)] * 4,
        out_specs=[_row_spec(d), _row_spec(d)],
        out_shape=[jax.ShapeDtypeStruct((s, d), f32), jax.ShapeDtypeStruct((s, d), bf16)],
        compiler_params=_cparams(("parallel",)),
    )(x, y, gate, gain, sc, sh)


FFN_TN = 1408


def _ffn_up(h2, w_gate, w_up, name):
    s, d = h2.shape
    tm = TOK_TILE

    def body(h_ref, wg_ref, wu_ref, a_ref, g_ref, u_ref):
        h = h_ref[...]
        g = jnp.dot(h, wg_ref[...], preferred_element_type=f32)
        u = jnp.dot(h, wu_ref[...], preferred_element_type=f32)
        a_ref[...] = (g * jax.nn.sigmoid(g) * u).astype(bf16)
        g_ref[...] = g.astype(bf16)
        u_ref[...] = u.astype(bf16)

    w_spec = pl.BlockSpec((d, FFN_TN), lambda j, i: (0, j))
    o_spec = pl.BlockSpec((tm, FFN_TN), lambda j, i: (i, j))
    return pl.pallas_call(
        body, name=name, grid=(D_FF // FFN_TN, s // tm),
        in_specs=[pl.BlockSpec((tm, d), lambda j, i: (i, 0)), w_spec, w_spec],
        out_specs=[o_spec] * 3,
        out_shape=[jax.ShapeDtypeStruct((s, D_FF), bf16)] * 3,
        compiler_params=_cparams(("parallel", "parallel")),
    )(h2, w_gate, w_up)


def _ffn_down_dx(dy2, w_down, gate, up, name):
    s, d = dy2.shape
    tm = TOK_TILE

    def body(dy_ref, w_ref, g_ref, u_ref, dg_ref, du_ref):
        da = lax.dot_general(dy_ref[...], w_ref[...], _NT, preferred_element_type=f32)
        g = g_ref[...].astype(f32)
        sg = jax.nn.sigmoid(g)
        du_ref[...] = (da * g * sg).astype(bf16)
        dg_ref[...] = (da * u_ref[...].astype(f32) * sg * (1.0 + g * (1.0 - sg))).astype(bf16)

    t_spec = pl.BlockSpec((tm, FFN_TN), lambda j, i: (i, j))
    return pl.pallas_call(
        body, name=name, grid=(D_FF // FFN_TN, s // tm),
        in_specs=[pl.BlockSpec((tm, d), lambda j, i: (i, 0)), pl.BlockSpec((FFN_TN, d), lambda j, i: (j, 0)),
                  t_spec, t_spec],
        out_specs=[t_spec, t_spec],
        out_shape=[jax.ShapeDtypeStruct((s, D_FF), bf16)] * 2,
        compiler_params=_cparams(("parallel", "parallel")),
    )(dy2, w_down, gate, up)


def _acc_spec(width):
    return pl.BlockSpec((1, width), lambda i: (0, 0))


def _final_loss_bwd(x1, y2, gate2, final_g, target, name):
    s, d = x1.shape

    def body(x1_ref, y2_ref, gt_ref, fg_ref, tg_ref, dx2_ref, dy2_ref, loss_ref, dfg_ref, dgt_ref):
        @pl.when(pl.program_id(0) == 0)
        def _():
            loss_ref[...] = jnp.zeros_like(loss_ref)
            dfg_ref[...] = jnp.zeros_like(dfg_ref)
            dgt_ref[...] = jnp.zeros_like(dgt_ref)

        y2 = y2_ref[...]
        gt = gt_ref[...]
        fg = fg_ref[...]
        x2 = x1_ref[...] + gt * y2
        rstd = lax.rsqrt(jnp.mean(x2 * x2, axis=-1, keepdims=True) + EPS)
        xn = x2 * rstd
        err = xn * fg - tg_ref[...]
        row = jnp.sum(err * err, axis=-1, keepdims=True) * (0.5 / d)
        loss_ref[...] += jnp.sum(row, axis=0, keepdims=True) + jnp.zeros_like(loss_ref)
        dout = err * (1.0 / d)
        dfg_ref[...] += jnp.sum(dout * xn, axis=0, keepdims=True)
        dxn = dout * fg
        dx2 = rstd * (dxn - xn * jnp.mean(dxn * xn, axis=-1, keepdims=True))
        dx2_ref[...] = dx2
        dgt_ref[...] += jnp.sum(dx2 * y2, axis=0, keepdims=True)
        dy2_ref[...] = (gt * dx2).astype(bf16)

    return pl.pallas_call(
        body, name=name, grid=(s // TOK_TILE,),
        in_specs=[_row_spec(d), _row_spec(d), _vec_spec(d), _vec_spec(d), _row_spec(d)],
        out_specs=[_row_spec(d), _row_spec(d), _acc_spec(128), _acc_spec(d), _acc_spec(d)],
        out_shape=[jax.ShapeDtypeStruct((s, d), f32), jax.ShapeDtypeStruct((s, d), bf16),
                   jax.ShapeDtypeStruct((1, 128), f32), jax.ShapeDtypeStruct((1, d), f32),
                   jax.ShapeDtypeStruct((1, d), f32)],
        compiler_params=_cparams(("arbitrary",)),
    )(x1, y2, gate2, final_g, target)


def _ln_mod_bwd(xin, gain, sc, dh, dres, name, gate=None, y=None):
    s, d = xin.shape
    with_gate = gate is not None

    def body(*refs):
        if with_gate:
            (x_ref, g_ref, sc_ref, dh_ref, dr_ref, gt_ref, y_ref,
             dx_ref, dsh_ref, dsc_ref, dg_ref, dy_ref, dgt_ref) = refs
        else:
            x_ref, g_ref, sc_ref, dh_ref, dr_ref, dx_ref, dsh_ref, dsc_ref, dg_ref = refs

        @pl.when(pl.program_id(0) == 0)
        def _():
            dsh_ref[...] = jnp.zeros_like(dsh_ref)
            dsc_ref[...] = jnp.zeros_like(dsc_ref)
            dg_ref[...] = jnp.zeros_like(dg_ref)
            if with_gate:
                dgt_ref[...] = jnp.zeros_like(dgt_ref)

        xv = x_ref[...]
        g = g_ref[...]
        sc1 = 1.0 + sc_ref[...]
        dh = dh_ref[...]
        rstd = lax.rsqrt(jnp.mean(xv * xv, axis=-1, keepdims=True) + EPS)
        xn = xv * rstd
        dsh_ref[...] += jnp.sum(dh, axis=0, keepdims=True)
        dsc_ref[...] += jnp.sum(dh * (xn * g), axis=0, keepdims=True)
        dg_ref[...] += jnp.sum(dh * sc1 * xn, axis=0, keepdims=True)
        dxn = dh * sc1 * g
        dx = dr_ref[...] + rstd * (dxn - xn * jnp.mean(dxn * xn, axis=-1, keepdims=True))
        dx_ref[...] = dx
        if with_gate:
            dgt_ref[...] += jnp.sum(dx * y_ref[...], axis=0, keepdims=True)
            dy_ref[...] = (gt_ref[...] * dx).astype(bf16)

    in_specs = [_row_spec(d), _vec_spec(d), _vec_spec(d), _row_spec(d), _row_spec(d)]
    out_specs = [_row_spec(d), _acc_spec(d), _acc_spec(d), _acc_spec(d)]
    out_shape = [jax.ShapeDtypeStruct((s, d), f32)] + [jax.ShapeDtypeStruct((1, d), f32)] * 3
    args = [xin, gain, sc, dh, dres]
    if with_gate:
        in_specs += [_vec_spec(d), _row_spec(d)]
        out_specs += [_row_spec(d), _acc_spec(d)]
        out_shape += [jax.ShapeDtypeStruct((s, d), bf16), jax.ShapeDtypeStruct((1, d), f32)]
        args += [gate, y]
    return pl.pallas_call(
        body, name=name, grid=(s // TOK_TILE,),
        in_specs=in_specs, out_specs=out_specs, out_shape=out_shape,
        compiler_params=_cparams(("arbitrary",)),
    )(*args)


def _bucket_tables():
    import numpy as np
    qi = np.arange(BAND)[:, None]
    kj = np.arange(2 * BAND)[None, :]
    steps = qi + BAND - kj
    max_exact = N_BUCKETS // 2
    out = []
    for d in DILATIONS:
        dist = np.maximum(steps, 0) * d
        dist_f = np.maximum(dist, 1).astype(np.float32)
        large = max_exact + (np.log(dist_f / np.float32(max_exact)) / np.float32(math.log(MAX_DISTANCE / max_exact))
                             * np.float32(N_BUCKETS - max_exact)).astype(np.int32)
        out.append(np.where(dist < max_exact, dist, np.minimum(large, N_BUCKETS - 1)))
    return jnp.asarray(np.stack(out).astype(np.int32))


def _bias_tables(rel_bias, idx):
    def body(idx_ref, rb_ref, o_ref):
        h = pl.program_id(1)
        idxv = idx_ref[0]
        acc = jnp.zeros((BAND, 2 * BAND), f32)
        for b in range(N_BUCKETS):
            acc = jnp.where(idxv == b, rb_ref[b, h], acc)
        o_ref[0, 0] = acc

    return pl.pallas_call(
        body, name="attn_bias_tables", grid=(3, N_HEADS),
        in_specs=[pl.BlockSpec((1, BAND, 2 * BAND), lambda br, h: (br, 0, 0)),
                  pl.BlockSpec(memory_space=pltpu.SMEM)],
        out_specs=pl.BlockSpec((1, 1, BAND, 2 * BAND), lambda br, h: (br, h, 0, 0)),
        out_shape=jax.ShapeDtypeStruct((3, N_HEADS, BAND, 2 * BAND), f32),
        compiler_params=_cparams(("parallel", "parallel")),
    )(idx, rel_bias)


def _bias_grad(dbias, idx):
    def body(idx_ref, db_ref, o_ref):
        br = pl.program_id(1)

        @pl.when(br == 0)
        def _():
            o_ref[...] = jnp.zeros_like(o_ref)

        idxv = idx_ref[0]
        dbv = db_ref[0, 0]
        row = lax.broadcasted_iota(jnp.int32, (N_BUCKETS, 128), 0)
        acc = jnp.zeros((N_BUCKETS, 128), f32)
        for b in range(N_BUCKETS):
            sb = jnp.sum(jnp.sum(jnp.where(idxv == b, dbv, 0.0), axis=1, keepdims=True), axis=0, keepdims=True)
            acc = acc + jnp.where(row == b, sb, 0.0)
        o_ref[0] += acc

    return pl.pallas_call(
        body, name="attn_bias_grad", grid=(N_HEADS, 3),
        in_specs=[pl.BlockSpec((1, BAND, 2 * BAND), lambda h, br: (br, 0, 0)),
                  pl.BlockSpec((1, 1, BAND, 2 * BAND), lambda h, br: (br, h, 0, 0))],
        out_specs=pl.BlockSpec((1, N_BUCKETS, 128), lambda h, br: (h, 0, 0)),
        out_shape=jax.ShapeDtypeStruct((N_HEADS, N_BUCKETS, 128), f32),
        compiler_params=_cparams(("parallel", "arbitrary")),
    )(idx, dbias)


def _attn_masks():
    lane = lax.broadcasted_iota(jnp.int32, (BAND, 128), 1)
    m0 = lane < HEAD_DIM
    qi = lax.broadcasted_iota(jnp.int32, (BAND, 2 * BAND), 0)
    kj = lax.broadcasted_iota(jnp.int32, (BAND, 2 * BAND), 1)
    steps = qi + BAND - kj
    in_window = (steps >= 0) & (steps <= BAND)
    return m0, in_window, kj >= BAND


_NT = (((1,), (1,)), ((), ()))
_TN = (((0,), (0,)), ((), ()))
_BNN = (((2,), (1,)), ((0,), (0,)))
_BNT = (((2,), (2,)), ((0,), (0,)))
_BTN = (((1,), (1,)), ((0,), (0,)))
ATTN_GROUP = 4
ATTN_ITEMS = PAD_UNIT // BAND
Q_COL, K_COL, V_COL = 0, 4, 8


def _attn_item_rows(j, d, c, cbase):
    r = lax.rem(j, d)
    b = lax.div(j, d)
    loc = b * (d * BAND) + r
    first = jnp.logical_and(c == 0, b == 0)
    start = cbase + loc
    pstart = jnp.where(first, start, start - d * BAND)
    return loc, start, pstart, first


def _attn_fwd(proj, bias, xchg):
    s = proj.shape[0]

    def body(q_ref, k_ref, v_ref, b_ref, y_ref, lse_ref, o_s, l_s):
        c = pl.program_id(1)
        cbase = pl.multiple_of(c * PAD_UNIT, PAD_UNIT)
        m0, in_window, cur_half = _attn_masks()
        for bi, d in enumerate(DILATIONS):
            def group(jg, carry, bi=bi, d=d):
                locs, qs, ks, vs, pens = [], [], [], [], []
                for t in range(ATTN_GROUP):
                    loc, start, pstart, first = _attn_item_rows(jg * ATTN_GROUP + t, d, c, cbase)
                    locs.append(loc)
                    qs.append(q_ref[pl.ds(loc, BAND, stride=d), :])
                    ks.append(jnp.concatenate([k_ref[pl.ds(pstart, BAND, stride=d), :],
                                               k_ref[pl.ds(start, BAND, stride=d), :]], axis=0))
                    vs.append(jnp.concatenate([v_ref[pl.ds(pstart, BAND, stride=d), :],
                                               v_ref[pl.ds(start, BAND, stride=d), :]], axis=0))
                    pens.append(jnp.where(cur_half, 0.0, jnp.where(first, NEG_INF, 0.0)))
                q = jnp.stack(qs)
                kk = jnp.stack(ks + ks).astype(bf16)
                vv = jnp.stack(vs + vs).astype(bf16)
                pen = jnp.stack(pens + pens)
                qh = (jnp.concatenate([jnp.where(m0, q, 0.0), jnp.where(m0, 0.0, q)], axis=0) * 0.125).astype(bf16)
                sc = lax.dot_general(qh, kk, _BNT, preferred_element_type=f32)
                sc = (sc.reshape(2, ATTN_GROUP, BAND, 2 * BAND) + b_ref[bi][:, None]).reshape(sc.shape) + pen
                sc = jnp.where(in_window, sc, NEG_INF)
                mx = jnp.max(sc, axis=-1, keepdims=True)
                e = jnp.exp(sc - mx)
                l = jnp.sum(e, axis=-1, keepdims=True)
                o = lax.dot_general(e.astype(bf16), vv, _BNN, preferred_element_type=f32) / l
                ls = mx + jnp.log(l)
                for t in range(ATTN_GROUP):
                    rows = pl.ds(locs[t], BAND, stride=d)
                    o_s[bi, rows, :] = jnp.where(m0, o[t], o[ATTN_GROUP + t])
                    l_s[bi, rows, :] = jnp.where(m0, ls[t], ls[ATTN_GROUP + t])
                return carry

            lax.fori_loop(0, ATTN_ITEMS // ATTN_GROUP, group, 0)

        def merge(t, carry):
            rows = pl.ds(pl.multiple_of(t * 256, 256), 256)
            ls = [l_s[i, rows, :] for i in range(3)]
            mx = jnp.maximum(jnp.maximum(ls[0], ls[1]), ls[2])
            ws = [jnp.exp(l - mx) for l in ls]
            tot = ws[0] + ws[1] + ws[2]
            y = (ws[0] * o_s[0, rows, :] + ws[1] * o_s[1, rows, :] + ws[2] * o_s[2, rows, :]) / tot
            y_ref[rows, :] = y
            lse_ref[rows, :] = mx + jnp.log(tot)
            return carry

        lax.fori_loop(0, PAD_UNIT // 256, merge, 0)

    chunk = lambda col: pl.BlockSpec((PAD_UNIT, 128), lambda p, c: (c, col + p))
    full = lambda col: pl.BlockSpec((s, 128), lambda p, c: (0, col + p))
    grid = (N_HEADS // 2, s // PAD_UNIT)
    out = pl.pallas_call(
        _ride(body, 4, 2, xchg, grid), name="attn_fwd", grid=grid,
        in_specs=[chunk(Q_COL), full(K_COL), full(V_COL),
                  pl.BlockSpec((3, 2, BAND, 2 * BAND), lambda p, c: (0, p, 0, 0))] + [_ANY] * xchg.n,
        out_specs=[chunk(0), chunk(0)] + [_ANY] * xchg.n,
        out_shape=[jax.ShapeDtypeStruct((s, GROUP_W), f32)] * 2 + xchg.out_shape(),
        scratch_shapes=[pltpu.VMEM((3, PAD_UNIT, 128), f32)] * 2 + xchg.scratch(),
        compiler_params=_cparams(("arbitrary", "arbitrary")),
    )(proj, proj, proj, bias, *xchg.arrs)
    return out[:2], out[2:]


def _attn_bwd(proj, bias, y, lse, dycat):
    s = proj.shape[0]

    def body(q_ref, k_ref, v_ref, b_ref, y_ref, lse_ref, dy_ref, dq_ref, dk_ref, dv_ref, db_ref, dd_s):
        c = pl.program_id(1)
        cbase = pl.multiple_of(c * PAD_UNIT, PAD_UNIT)
        m0, in_window, cur_half = _attn_masks()

        @pl.when(c == 0)
        def _():
            dk_ref[...] = jnp.zeros_like(dk_ref)
            dv_ref[...] = jnp.zeros_like(dv_ref)
            db_ref[...] = jnp.zeros_like(db_ref)

        dq_ref[...] = jnp.zeros_like(dq_ref)

        def rowdot(t, carry):
            rows = pl.ds(pl.multiple_of(t * 256, 256), 256)
            prod = dy_ref[rows, :] * y_ref[rows, :]
            lane = lax.broadcasted_iota(jnp.int32, prod.shape, 1)
            h0 = lane < HEAD_DIM
            d0 = jnp.sum(jnp.where(h0, prod, 0.0), axis=-1, keepdims=True)
            d1 = jnp.sum(jnp.where(h0, 0.0, prod), axis=-1, keepdims=True)
            dd_s[rows, :] = jnp.where(h0, d0, d1)
            return carry

        lax.fori_loop(0, PAD_UNIT // 256, rowdot, 0)

        for bi, d in enumerate(DILATIONS):
            def group(jg, carry, bi=bi, d=d):
                ng = ATTN_GROUP
                meta, qs, dos, lqs, dds, ks, vs, pens = [], [], [], [], [], [], [], []
                for t in range(ng):
                    loc, start, pstart, first = _attn_item_rows(jg * ng + t, d, c, cbase)
                    qrows = pl.ds(loc, BAND, stride=d)
                    rows = pl.ds(start, BAND, stride=d)
                    prows = pl.ds(pstart, BAND, stride=d)
                    meta.append((qrows, rows, prows))
                    qs.append(q_ref[qrows, :])
                    dos.append(dy_ref[qrows, :])
                    lqs.append(lse_ref[qrows, :])
                    dds.append(dd_s[qrows, :])
                    ks.append(jnp.concatenate([k_ref[prows, :], k_ref[rows, :]], axis=0))
                    vs.append(jnp.concatenate([v_ref[prows, :], v_ref[rows, :]], axis=0))
                    pens.append(jnp.where(cur_half, 0.0, jnp.where(first, NEG_INF, 0.0)))

                def heads(t):
                    return jnp.concatenate([jnp.where(m0, t, 0.0), jnp.where(m0, 0.0, t)], axis=0)

                def head_col(t):
                    return jnp.concatenate([t[:, :, 0:1], t[:, :, HEAD_DIM:HEAD_DIM + 1]], axis=0)

                qh = (heads(jnp.stack(qs)) * 0.125).astype(bf16)
                doh = heads(jnp.stack(dos)).astype(bf16)
                kk = jnp.stack(ks + ks).astype(bf16)
                vv = jnp.stack(vs + vs).astype(bf16)
                sc = lax.dot_general(qh, kk, _BNT, preferred_element_type=f32)
                sc = (sc.reshape(2, ng, BAND, 2 * BAND) + b_ref[bi][:, None]).reshape(sc.shape) + jnp.stack(pens + pens)
                sc = jnp.where(in_window, sc, NEG_INF)
                p = jnp.exp(sc - head_col(jnp.stack(lqs)))
                dp = lax.dot_general(doh, vv, _BNT, preferred_element_type=f32)
                ds = p * (dp - head_col(jnp.stack(dds)))
                db_ref[bi] += jnp.sum(ds.reshape(2, ng, BAND, 2 * BAND), axis=1)
                dsb = ds.astype(bf16)
                dq = lax.dot_general(dsb, kk, _BNN, preferred_element_type=f32) * 0.125
                dk = lax.dot_general(dsb, qh, _BTN, preferred_element_type=f32)
                dv = lax.dot_general(p.astype(bf16), doh, _BTN, preferred_element_type=f32)
                for t in range(ng):
                    qrows, rows, prows = meta[t]
                    dq_ref[qrows, :] += jnp.where(m0, dq[t], dq[ng + t])
                    dkt = dk[t] + dk[ng + t]
                    dvt = dv[t] + dv[ng + t]
                    dk_ref[prows, :] += dkt[:BAND]
                    dk_ref[rows, :] += dkt[BAND:]
                    dv_ref[prows, :] += dvt[:BAND]
                    dv_ref[rows, :] += dvt[BAND:]
                return carry

            lax.fori_loop(0, ATTN_ITEMS // ATTN_GROUP, group, 0)

    chunk = lambda col: pl.BlockSpec((PAD_UNIT, 128), lambda p, c: (c, col + p))
    full = lambda col: pl.BlockSpec((s, 128), lambda p, c: (0, col + p))
    bias_spec = pl.BlockSpec((3, 2, BAND, 2 * BAND), lambda p, c: (0, p, 0, 0))
    return pl.pallas_call(
        body, name="attn_bwd", grid=(N_HEADS // 2, s // PAD_UNIT),
        in_specs=[chunk(Q_COL), full(K_COL), full(V_COL), bias_spec, chunk(0), chunk(0), chunk(0)],
        out_specs=[chunk(0), full(0), full(0), bias_spec],
        out_shape=[jax.ShapeDtypeStruct((s, GROUP_W), f32)] * 3
        + [jax.ShapeDtypeStruct((3, N_HEADS, BAND, 2 * BAND), f32)],
        scratch_shapes=[pltpu.VMEM((PAD_UNIT, 128), f32)],
        compiler_params=_cparams(("parallel", "arbitrary")),
    )(proj, proj, proj, bias, y, lse, dycat)


_HI = lax.Precision.HIGHEST
DELTA_COL = 1536
Z_COL = 3072
BA_BLOCK = 28
DELTA_ROWS = 512


def _hdot(a, b):
    return jnp.dot(a, b, precision=_HI, preferred_element_type=f32)


_DIMS = dict(nn=(((2,), (1,)), ((0,), (0,))), nt=(((2,), (2,)), ((0,), (0,))), tn=(((1,), (1,)), ((0,), (0,))))


@functools.partial(jax.custom_vjp, nondiff_argnums=(2,))
def _mmx(a, b, mode):
    return lax.dot_general(a.astype(bf16), b.astype(bf16), _DIMS[mode], preferred_element_type=f32)


def _mmx_fwd(a, b, mode):
    return _mmx(a, b, mode), (a, b)


def _mmx_bwd(mode, res, g):
    a, b = res
    if mode == "nn":
        return _mmx(g, b, "nt"), _mmx(a, g, "tn")
    if mode == "nt":
        return _mmx(g, b, "nn"), _mmx(g, a, "tn")
    return _mmx(b, g, "nt"), _mmx(a, g, "nn")


_mmx.defvjp(_mmx_fwd, _mmx_bwd)


def _pair_iota():
    row = lax.broadcasted_iota(jnp.int32, (CHUNK, 128), 0)
    lane = lax.broadcasted_iota(jnp.int32, (CHUNK, 128), 1)
    return row, lane, lane & (CHUNK - 1)


def _bd(x):
    _, lane, _ = _pair_iota()
    m0 = lane < CHUNK
    return jnp.concatenate([jnp.where(m0, x, 0.0), jnp.where(m0, 0.0, x)], axis=1)


def _pmm(a, b):
    return _mmx(a, _bd(b), "nn")


def _ntp(x, y):
    return _mmx(x, _bd(y), "nt")


def _tnp(x, y):
    full = _mmx(x, y, "tn")
    _, lane, _ = _pair_iota()
    return jnp.where(lane < CHUNK, full[:, :CHUNK], full[:, CHUNK:])


def _tri_inv(a):
    row, lane, jj = _pair_iota()
    eye = jnp.where(row == jj, 1.0, 0.0).astype(f32)

    def same_block(log2b):
        return (row >> log2b) == (jj >> log2b)

    dgl = jnp.where(same_block(3), a, 0.0)
    d2 = _pmm(dgl, dgl)
    d4 = _pmm(d2, d2)
    t = _pmm(_pmm(eye - dgl, eye + d2), eye + d4)
    for lb in (3, 4, 5):
        off = jnp.where(same_block(lb + 1) & jnp.logical_not(same_block(lb)), a, 0.0)
        t = t - _pmm(_pmm(t, off), t)
    return t


@jax.custom_vjp
def _solve2(a, xv, xk):
    t = _tri_inv(a)
    return _pmm(t, xv), _pmm(t, xk)


def _solve2_fwd(a, xv, xk):
    t = _tri_inv(a)
    u, w = _pmm(t, xv), _pmm(t, xk)
    return (u, w), (t, u, w)


def _solve2_bwd(res, cts):
    t, u, w = res
    du, dw = cts
    dxv = _tnp(t, du)
    dxk = _tnp(t, dw)
    return -(_ntp(dxv, u) + _ntp(dxk, w)), dxv, dxk


_solve2.defvjp(_solve2_fwd, _solve2_bwd)


def _chunk_pre(qp, kp, vp, bp, gcum):
    row, lane, jj = _pair_iota()
    causal = row >= jj
    strict = row > jj
    rsel = jnp.sum(jnp.where(row == jj, gcum, 0.0), axis=1, keepdims=True)
    decay = jnp.where(causal, jnp.exp(jnp.where(causal, gcum - rsel, 0.0)), 0.0)
    kb = kp * bp
    kd = _bd(kp)
    a = jnp.where(strict, _mmx(kb, kd, "nt") * decay, 0.0)
    eg = jnp.exp(gcum)
    u, w = _solve2(a, vp * bp, kb * eg)
    qk = jnp.where(causal, _mmx(qp, kd, "nt") * decay, 0.0)
    glast = jnp.sum(jnp.where(row == CHUNK - 1, gcum, 0.0), axis=1, keepdims=True)
    return u, w, qp * eg, kp * jnp.exp(glast - gcum), qk, jnp.exp(glast)


def _chunk_post(u, w, qt, kh, qk, gam, sp):
    sd = _bd(sp)
    vnew = u - _mmx(w, sd, "nn")
    o = _mmx(qt, sd, "nn") + _pmm(qk, vnew)
    return o, gam * sp + _tnp(kh, vnew)


def _pair_spec(rows=DELTA_ROWS):
    return pl.BlockSpec((rows, 128), lambda i, p: (i, p))


DELTA_NB = DELTA_ROWS // CHUNK


def _chunks(ref):
    return ref[...].reshape(DELTA_NB, CHUNK, 128)


def _pairs(ref, rows):
    return jnp.stack([ref[rows, p * 128:(p + 1) * 128] for p in range(4)], axis=0)


def _delta_chunk_pre(qn, kn, sv, beta, g):
    s = qn.shape[0]

    def body(q_ref, k_ref, v_ref, b_ref, g_ref, u_ref, w_ref, qt_ref, kh_ref, qk_ref, gm_ref):
        outs = _chunk_pre(_chunks(q_ref), _chunks(k_ref), _chunks(v_ref), _chunks(b_ref), _chunks(g_ref))
        for ref, val in zip((u_ref, w_ref, qt_ref, kh_ref, qk_ref), outs[:5]):
            ref[...] = val.reshape(DELTA_ROWS, 128)
        gm_ref[...] = jnp.broadcast_to(outs[5], (DELTA_NB, 8, 128)).reshape(DELTA_NB * 8, 128)

    v_spec = pl.BlockSpec((DELTA_ROWS, 128), lambda i, p: (i, 8 + p))
    return pl.pallas_call(
        body, name="delta_chunk_pre", grid=(s // DELTA_ROWS, 4),
        in_specs=[_pair_spec(), _pair_spec(), v_spec, _pair_spec(), _pair_spec()],
        out_specs=[_pair_spec()] * 5 + [_pair_spec(DELTA_NB * 8)],
        out_shape=[jax.ShapeDtypeStruct((s, GROUP_W), f32)] * 5 + [jax.ShapeDtypeStruct((s // 8, GROUP_W), f32)],
        compiler_params=_cparams(("parallel", "parallel")),
    )(qn, kn, sv, beta, g)


def _delta_scan_fwd(u, w, qt, kh, qk, gm):
    s = u.shape[0]

    def body(u_ref, w_ref, qt_ref, kh_ref, qk_ref, gm_ref, o_ref, ss_ref, st):
        @pl.when(pl.program_id(0) == 0)
        def _():
            st[...] = jnp.zeros_like(st)

        def chunk(ci, carry):
            rows = pl.ds(pl.multiple_of(ci * CHUNK, CHUNK), CHUNK)
            grow = pl.ds(pl.multiple_of(ci * 8, 8), 1)
            sp = st[...]
            o, s2 = _chunk_post(_pairs(u_ref, rows), _pairs(w_ref, rows), _pairs(qt_ref, rows),
                                _pairs(kh_ref, rows), _pairs(qk_ref, rows), _pairs(gm_ref, grow), sp)
            for p in range(4):
                ss_ref[rows, p * 128:(p + 1) * 128] = sp[p]
                o_ref[rows, p * 128:(p + 1) * 128] = o[p]
            st[...] = s2
            return carry

        lax.fori_loop(0, DELTA_NB, chunk, 0)

    spec = pl.BlockSpec((DELTA_ROWS, GROUP_W), lambda i: (i, 0))
    gspec = pl.BlockSpec((DELTA_NB * 8, GROUP_W), lambda i: (i, 0))
    return pl.pallas_call(
        body, name="delta_scan_fwd", grid=(s // DELTA_ROWS,),
        in_specs=[spec] * 5 + [gspec],
        out_specs=[spec, spec],
        out_shape=[jax.ShapeDtypeStruct((s, GROUP_W), f32)] * 2,
        scratch_shapes=[pltpu.VMEM((4, CHUNK, 128), f32)],
        compiler_params=_cparams(("arbitrary",)),
    )(u, w, qt, kh, qk, gm)


def _delta_scan_bwd(w, qt, kh, qk, gm, do):
    s = w.shape[0]
    nb = s // DELTA_ROWS

    def body(w_ref, qt_ref, kh_ref, qk_ref, gm_ref, do_ref, dso_ref, dst):
        @pl.when(pl.program_id(0) == 0)
        def _():
            dst[...] = jnp.zeros_like(dst)

        def chunk(t, carry):
            ci = DELTA_NB - 1 - t
            rows = pl.ds(pl.multiple_of(ci * CHUNK, CHUNK), CHUNK)
            grow = pl.ds(pl.multiple_of(ci * 8, 8), 1)
            ds = dst[...]
            for p in range(4):
                dso_ref[rows, p * 128:(p + 1) * 128] = ds[p]
            do = _pairs(do_ref, rows)
            dvn = _tnp(_pairs(qk_ref, rows), do) + _pmm(_pairs(kh_ref, rows), ds)
            dst[...] = _tnp(_pairs(qt_ref, rows), do) + _pairs(gm_ref, grow) * ds - _tnp(_pairs(w_ref, rows), dvn)
            return carry

        lax.fori_loop(0, DELTA_NB, chunk, 0)

    spec = pl.BlockSpec((DELTA_ROWS, GROUP_W), lambda i: (nb - 1 - i, 0))
    gspec = pl.BlockSpec((DELTA_NB * 8, GROUP_W), lambda i: (nb - 1 - i, 0))
    return pl.pallas_call(
        body, name="delta_scan_bwd", grid=(nb,),
        in_specs=[spec] * 4 + [gspec, spec],
        out_specs=spec,
        out_shape=jax.ShapeDtypeStruct((s, GROUP_W), f32),
        scratch_shapes=[pltpu.VMEM((4, CHUNK, 128), f32)],
        compiler_params=_cparams(("arbitrary",)),
    )(w, qt, kh, qk, gm, do)


def _delta_chunk_bwd(qn, kn, sv, beta, g, ss, dso, do):
    s = qn.shape[0]

    def body(q_ref, k_ref, v_ref, b_ref, g_ref, ss_ref, dso_ref, do_ref, dq_ref, dk_ref, dv_ref, db_ref, dg_ref):
        sp = _chunks(ss_ref)

        def fn(q, k, v, b, gg):
            return _chunk_post(*_chunk_pre(q, k, v, b, gg), sp)

        _, vjp = jax.vjp(fn, _chunks(q_ref), _chunks(k_ref), _chunks(v_ref), _chunks(b_ref), _chunks(g_ref))
        grads = vjp((_chunks(do_ref), _chunks(dso_ref)))
        for ref, val in zip((dq_ref, dk_ref, dv_ref, db_ref, dg_ref), grads):
            ref[...] = val.reshape(DELTA_ROWS, 128)

    v_spec = pl.BlockSpec((DELTA_ROWS, 128), lambda i, p: (i, 8 + p))
    return pl.pallas_call(
        body, name="delta_chunk_bwd", grid=(s // DELTA_ROWS, 4),
        in_specs=[_pair_spec(), _pair_spec(), v_spec] + [_pair_spec()] * 5,
        out_specs=[_pair_spec()] * 5,
        out_shape=[jax.ShapeDtypeStruct((s, GROUP_W), f32)] * 5,
        compiler_params=_cparams(("parallel", "parallel")),
    )(qn, kn, sv, beta, g, ss, dso, do)


def _head_sum_matrix():
    r = lax.broadcasted_iota(jnp.int32, (GROUP_W, GROUP_W), 0)
    c = lax.broadcasted_iota(jnp.int32, (GROUP_W, GROUP_W), 1)
    return jnp.where((r >> 6) == (c >> 6), 1.0, 0.0).astype(f32)


def _head_sums(x):
    return _mmx(x[None], _head_sum_matrix()[None], "nn")[0]


def _sel_dot(a, b):
    return jnp.dot(a, b, precision=lax.Precision.HIGH, preferred_element_type=f32)


def _softplus(x):
    return jnp.maximum(x, 0.0) + jnp.log(1.0 + jnp.exp(-jnp.abs(x)))


def _prep_fn(sq, sk, ba, alog_e, dt_e):
    qn = sq * lax.rsqrt(_head_sums(sq * sq) + EPS) * (HEAD_DIM ** -0.5)
    kn = sk * lax.rsqrt(_head_sums(sk * sk) + EPS)
    r = lax.broadcasted_iota(jnp.int32, (128, GROUP_W), 0)
    c = lax.broadcasted_iota(jnp.int32, (128, GROUP_W), 1) >> 6
    bl = _sel_dot(ba, jnp.where(r == c, 1.0, 0.0).astype(f32))
    al = _sel_dot(ba, jnp.where(r == c + N_HEADS, 1.0, 0.0).astype(f32))
    beta = jax.nn.sigmoid(bl)
    g = -jnp.exp(alog_e) * _softplus(al + dt_e)
    ri = lax.broadcasted_iota(jnp.int32, (TOK_TILE, TOK_TILE), 0)
    ci = lax.broadcasted_iota(jnp.int32, (TOK_TILE, TOK_TILE), 1)
    within = jnp.where(((ri >> 6) == (ci >> 6)) & (ri >= ci), 1.0, 0.0).astype(f32)
    return qn, kn, beta, _sel_dot(within, g)


def _gnorm_fn(o, z, ng_e):
    ms = _head_sums(o * o) * (1.0 / HEAD_DIM)
    return o * lax.rsqrt(ms + EPS) * ng_e * (z * jax.nn.sigmoid(z))


def _tok_spec(width, col):
    return pl.BlockSpec((TOK_TILE, width), lambda i: (i, col))


def _conv_taps(xs_ref, w_ref, base, n):
    acc = w_ref[CONV_WIDTH - 1:CONV_WIDTH, :] * xs_ref[pl.ds(base, n), :]
    for j in range(CONV_WIDTH - 1):
        acc = acc + w_ref[j:j + 1, :] * xs_ref[pl.ds(base - (CONV_WIDTH - 1) + j, n), :]
    return acc


def _conv_silu_fwd(proj, conv_w):
    s = proj.shape[0]
    wd = 3 * GROUP_W
    hb = TOK_TILE // 8

    def body(x_ref, halo_ref, w_ref, o_ref, xs):
        xs[0:8, :] = jnp.where(pl.program_id(0) > 0, halo_ref[...], 0.0)
        xs[8:, :] = x_ref[...]
        y = _conv_taps(xs, w_ref, 8, TOK_TILE)
        o_ref[...] = y * jax.nn.sigmoid(y)

    return pl.pallas_call(
        body, name="delta_conv_fwd", grid=(s // TOK_TILE,),
        in_specs=[_tok_spec(wd, 1), pl.BlockSpec((8, wd), lambda i: (jnp.maximum(i * hb - 1, 0), 1)),
                  pl.BlockSpec((CONV_WIDTH, wd), lambda i: (0, 0))],
        out_specs=_tok_spec(wd, 0),
        out_shape=jax.ShapeDtypeStruct((s, wd), f32),
        scratch_shapes=[pltpu.VMEM((TOK_TILE + 8, wd), f32)],
        compiler_params=_cparams(("parallel",)),
    )(proj, proj, conv_w)


def _conv_silu_bwd(proj, conv_w, ds, xchg):
    s = proj.shape[0]
    wd = 3 * GROUP_W
    hb = TOK_TILE // 8
    nt = s // TOK_TILE

    def body(x_ref, hp_ref, hn_ref, ds_ref, dsn_ref, w_ref, dx_ref, dw_ref, xs, dys):
        i = pl.program_id(0)

        @pl.when(i == 0)
        def _():
            dw_ref[...] = jnp.zeros_like(dw_ref)

        last = i == nt - 1
        xs[0:8, :] = jnp.where(i > 0, hp_ref[...], 0.0)
        xs[8:8 + TOK_TILE, :] = x_ref[...]
        xs[8 + TOK_TILE:, :] = jnp.where(last, 0.0, hn_ref[...])
        y = _conv_taps(xs, w_ref, 8, TOK_TILE)
        sg = jax.nn.sigmoid(y)
        dys[0:TOK_TILE, :] = ds_ref[...] * sg * (1.0 + y * (1.0 - sg))
        yn = _conv_taps(xs, w_ref, 8 + TOK_TILE, 8)
        sgn = jax.nn.sigmoid(yn)
        dys[TOK_TILE:, :] = jnp.where(last, 0.0, dsn_ref[...]) * sgn * (1.0 + yn * (1.0 - sgn))
        dy0 = dys[0:TOK_TILE, :]
        dx = w_ref[CONV_WIDTH - 1:CONV_WIDTH, :] * dy0
        for j in range(CONV_WIDTH - 1):
            dx = dx + w_ref[j:j + 1, :] * dys[pl.ds(CONV_WIDTH - 1 - j, TOK_TILE), :]
        dx_ref[...] = dx
        for j in range(CONV_WIDTH):
            dw_ref[j:j + 1, :] += jnp.sum(dy0 * xs[pl.ds(8 - (CONV_WIDTH - 1) + j, TOK_TILE), :],
                                          axis=0, keepdims=True)

    prev8 = lambda col: pl.BlockSpec((8, wd), lambda i: (jnp.maximum(i * hb - 1, 0), col))
    next8 = lambda col: pl.BlockSpec((8, wd), lambda i: (jnp.minimum((i + 1) * hb, s // 8 - 1), col))
    out = pl.pallas_call(
        _ride(body, 6, 2, xchg, (nt,)), name="delta_conv_bwd", grid=(nt,),
        in_specs=[_tok_spec(wd, 1), prev8(1), next8(1), _tok_spec(wd, 0), next8(0),
                  pl.BlockSpec((CONV_WIDTH, wd), lambda i: (0, 0))] + [_ANY] * xchg.n,
        out_specs=[_tok_spec(wd, 0), pl.BlockSpec((CONV_WIDTH, wd), lambda i: (0, 0))] + [_ANY] * xchg.n,
        out_shape=[jax.ShapeDtypeStruct((s, wd), f32), jax.ShapeDtypeStruct((CONV_WIDTH, wd), f32)] + xchg.out_shape(),
        scratch_shapes=[pltpu.VMEM((TOK_TILE + 16, wd), f32), pltpu.VMEM((TOK_TILE + 8, wd), f32)] + xchg.scratch(),
        compiler_params=_cparams(("arbitrary",)),
    )(proj, proj, proj, ds, ds, conv_w, *xchg.arrs)
    return out[:2], out[2:]


def _delta_prep_fwd(sconv, proj, alog_e, dt_e):
    s = sconv.shape[0]

    def body(sq_ref, sk_ref, ba_ref, al_ref, dt_ref, q_ref, k_ref, b_ref, g_ref):
        qn, kn, beta, g = _prep_fn(sq_ref[...], sk_ref[...], ba_ref[...], al_ref[...], dt_ref[...])
        q_ref[...] = qn
        k_ref[...] = kn
        b_ref[...] = beta
        g_ref[...] = g

    return pl.pallas_call(
        body, name="delta_prep_fwd", grid=(s // TOK_TILE,),
        in_specs=[_tok_spec(GROUP_W, 0), _tok_spec(GROUP_W, 1), _tok_spec(128, BA_BLOCK),
                  _vec_spec(GROUP_W), _vec_spec(GROUP_W)],
        out_specs=[_tok_spec(GROUP_W, 0)] * 4,
        out_shape=[jax.ShapeDtypeStruct((s, GROUP_W), f32)] * 4,
        compiler_params=_cparams(("parallel",)),
    )(sconv, sconv, proj, alog_e, dt_e)


def _delta_prep_bwd(sconv, proj, alog_e, dt_e, dqn, dkn, dbeta, dg):
    s = sconv.shape[0]

    def body(sq_ref, sk_ref, ba_ref, al_ref, dt_ref, dq_ref, dk_ref, db_ref, dg_ref,
             dsq_ref, dsk_ref, dba_ref, dal_ref, ddt_ref):
        @pl.when(pl.program_id(0) == 0)
        def _():
            dal_ref[...] = jnp.zeros_like(dal_ref)
            ddt_ref[...] = jnp.zeros_like(ddt_ref)

        _, vjp = jax.vjp(_prep_fn, sq_ref[...], sk_ref[...], ba_ref[...], al_ref[...], dt_ref[...])
        dsq, dsk, dba, dal, ddt = vjp((dq_ref[...], dk_ref[...], db_ref[...], dg_ref[...]))
        dsq_ref[...] = dsq
        dsk_ref[...] = dsk
        dba_ref[...] = dba
        dal_ref[...] += dal
        ddt_ref[...] += ddt

    return pl.pallas_call(
        body, name="delta_prep_bwd", grid=(s // TOK_TILE,),
        in_specs=[_tok_spec(GROUP_W, 0), _tok_spec(GROUP_W, 1), _tok_spec(128, BA_BLOCK),
                  _vec_spec(GROUP_W), _vec_spec(GROUP_W)] + [_tok_spec(GROUP_W, 0)] * 4,
        out_specs=[_tok_spec(GROUP_W, 0), _tok_spec(GROUP_W, 0), _tok_spec(128, 0),
                   _acc_spec(GROUP_W), _acc_spec(GROUP_W)],
        out_shape=[jax.ShapeDtypeStruct((s, GROUP_W), f32)] * 2 + [jax.ShapeDtypeStruct((s, 128), f32)]
        + [jax.ShapeDtypeStruct((1, GROUP_W), f32)] * 2,
        compiler_params=_cparams(("arbitrary",)),
    )(sconv, sconv, proj, alog_e, dt_e, dqn, dkn, dbeta, dg)


def _gnorm_fwd(o, proj, ng_e):
    s = o.shape[0]

    def body(o_ref, z_ref, g_ref, y_ref):
        y_ref[...] = _gnorm_fn(o_ref[...], z_ref[...], g_ref[...])

    return pl.pallas_call(
        body, name="delta_gnorm_fwd", grid=(s // TOK_TILE,),
        in_specs=[_tok_spec(GROUP_W, 0), _tok_spec(GROUP_W, Z_COL // GROUP_W), _vec_spec(GROUP_W)],
        out_specs=_tok_spec(GROUP_W, 0),
        out_shape=jax.ShapeDtypeStruct((s, GROUP_W), f32),
        compiler_params=_cparams(("parallel",)),
    )(o, proj, ng_e)


def _gnorm_bwd(o, proj, ng_e, dycat):
    s = o.shape[0]

    def body(o_ref, z_ref, g_ref, dy_ref, do_ref, dz_ref, dg_ref):
        @pl.when(pl.program_id(0) == 0)
        def _():
            dg_ref[...] = jnp.zeros_like(dg_ref)

        _, vjp = jax.vjp(_gnorm_fn, o_ref[...], z_ref[...], g_ref[...])
        do, dz, dg = vjp(dy_ref[...])
        do_ref[...] = do
        dz_ref[...] = dz
        dg_ref[...] += dg

    return pl.pallas_call(
        body, name="delta_gnorm_bwd", grid=(s // TOK_TILE,),
        in_specs=[_tok_spec(GROUP_W, 0), _tok_spec(GROUP_W, Z_COL // GROUP_W), _vec_spec(GROUP_W),
                  _tok_spec(GROUP_W, 1)],
        out_specs=[_tok_spec(GROUP_W, 0), _tok_spec(GROUP_W, 0), _acc_spec(GROUP_W)],
        out_shape=[jax.ShapeDtypeStruct((s, GROUP_W), f32)] * 2 + [jax.ShapeDtypeStruct((1, GROUP_W), f32)],
        compiler_params=_cparams(("arbitrary",)),
    )(o, proj, ng_e, dycat)


_MESH = pl.DeviceIdType.MESH
_ANY = pl.BlockSpec(memory_space=pl.ANY)
_VMEM = pl.BlockSpec(memory_space=pltpu.VMEM)


def _my_place():
    x, y, c = lax.axis_index("x"), lax.axis_index("y"), lax.axis_index("c")
    return x, y, c, 4 * x + 2 * y + c


def _peer(k, x, y, c):
    px = 1 - x if k & 4 else x
    py = 1 - y if k & 2 else y
    pc = 1 - c if k & 1 else c
    return (px, py, pc), 4 * px + 2 * py + pc


def _exchange_all(src_of_peer, dst_ref, send_sems, recv_sems, x, y, c, me):
    sent = []
    for k in range(1, N_DEV):
        dev, pidx = _peer(k, x, y, c)
        cp = pltpu.make_async_remote_copy(src_ref=src_of_peer(pidx), dst_ref=dst_ref.at[me],
                                          send_sem=send_sems.at[k - 1], recv_sem=recv_sems.at[k - 1],
                                          device_id=dev, device_id_type=_MESH)
        cp.start()
        sent.append(cp)
    for k in range(1, N_DEV):
        dev, pidx = _peer(k, x, y, c)
        pltpu.make_async_remote_copy(src_ref=src_of_peer(pidx), dst_ref=dst_ref.at[pidx],
                                     send_sem=send_sems.at[k - 1], recv_sem=recv_sems.at[k - 1],
                                     device_id=dev, device_id_type=_MESH).wait_recv()
    for cp in sent:
        cp.wait_send()


def _ada_exchange(cv8, w_ada, b_ada8, w_in_sh):
    ride = _Exchange([w_in_sh], gather=True)

    def body(cv_ref, w_ref, b_ref, wi_ref, call_ref, modp_ref, wig_ref, part_s, s1, r1, s2, r2, *ride_sems):
        ride.start([wi_ref], [wig_ref], ride_sems)
        x, y, c, me = _my_place()
        call_ref[me] = cv_ref[...]
        _exchange_all(lambda pidx: cv_ref, call_ref, s1, r1, x, y, c, me)
        bias = b_ref[me]
        for j in range(N_DEV):
            cj = call_ref[j][:, :D_MODEL]
            part_s[j] = _hdot(cj * jax.nn.sigmoid(cj), w_ref[...]) + bias
        modp_ref[me] = part_s[me]
        _exchange_all(lambda pidx: part_s.at[pidx], modp_ref, s2, r2, x, y, c, me)
        ride.wait([wi_ref], [wig_ref], ride_sems)

    nsh = w_ada.shape[1]
    return pl.pallas_call(
        body, name="ada_exchange",
        in_specs=[_VMEM, _VMEM, _VMEM, _ANY], out_specs=[_VMEM, _VMEM, _ANY],
        out_shape=[jax.ShapeDtypeStruct((N_DEV, 8, cv8.shape[1]), f32), jax.ShapeDtypeStruct((N_DEV, 8, nsh), f32)]
        + ride.out_shape(),
        scratch_shapes=[pltpu.VMEM((N_DEV, 8, nsh), f32)] + [pltpu.SemaphoreType.DMA((N_DEV - 1,))] * 4
        + ride.scratch(),
        compiler_params=pltpu.CompilerParams(vmem_limit_bytes=VMEM_LIMIT),
    )(cv8, w_ada, b_ada8, w_in_sh)


def _all_to_all(arrs, name):
    ex = _Exchange(arrs, gather=False)

    def body(*refs):
        srcs, dsts, sems = refs[:ex.n], refs[ex.n:2 * ex.n], refs[2 * ex.n:]
        ex.start(srcs, dsts, sems)
        ex.wait(srcs, dsts, sems)

    return pl.pallas_call(
        body, name=name,
        in_specs=[_ANY] * ex.n, out_specs=[_ANY] * ex.n,
        out_shape=ex.out_shape(), scratch_shapes=ex.scratch(),
    )(*arrs)


class _Exchange:
    def __init__(self, arrs, gather):
        self.arrs, self.gather, self.n = list(arrs), gather, len(arrs)

    def out_shape(self):
        return [jax.ShapeDtypeStruct(((N_DEV,) + a.shape) if self.gather else a.shape, a.dtype) for a in self.arrs]

    def scratch(self):
        if self.n == 0:
            return []
        return [pltpu.SemaphoreType.DMA((self.n, N_DEV - 1)), pltpu.SemaphoreType.DMA((self.n, N_DEV - 1)),
                pltpu.SemaphoreType.DMA((self.n,))]

    def _src(self, srcs, a, idx):
        return srcs[a] if self.gather else srcs[a].at[idx]

    def _copies(self, srcs, dsts, sems, incoming):
        send_sems, recv_sems, _ = sems
        x, y, c, me = _my_place()
        out = []
        for a in range(self.n):
            for k in range(1, N_DEV):
                dev, pidx = _peer(k, x, y, c)
                out.append(pltpu.make_async_remote_copy(
                    src_ref=self._src(srcs, a, pidx), dst_ref=dsts[a].at[pidx if incoming else me],
                    send_sem=send_sems.at[a, k - 1], recv_sem=recv_sems.at[a, k - 1],
                    device_id=dev, device_id_type=_MESH))
        return out

    def _local(self, srcs, dsts, sems):
        me = _my_place()[3]
        return [pltpu.make_async_copy(self._src(srcs, a, me), dsts[a].at[me], sems[2].at[a]) for a in range(self.n)]

    def start(self, srcs, dsts, sems):
        for cp in self._local(srcs, dsts, sems) + self._copies(srcs, dsts, sems, incoming=False):
            cp.start()

    def wait(self, srcs, dsts, sems):
        for cp in self._copies(srcs, dsts, sems, incoming=True):
            cp.wait_recv()
        for cp in self._copies(srcs, dsts, sems, incoming=False):
            cp.wait_send()
        for cp in self._local(srcs, dsts, sems):
            cp.wait()

    def start_at_first_step(self, grid, srcs, dsts, sems):
        first = functools.reduce(jnp.logical_and, [pl.program_id(i) == 0 for i in range(len(grid))])
        pl.when(first)(lambda: self.start(srcs, dsts, sems))

    def wait_at_last_step(self, grid, srcs, dsts, sems):
        last = functools.reduce(jnp.logical_and, [pl.program_id(i) == g - 1 for i, g in enumerate(grid)])
        pl.when(last)(lambda: self.wait(srcs, dsts, sems))


def _ride(body, n_in, n_out, xchg, grid):
    nx = xchg.n
    if nx == 0:
        return body

    def wrapped(*refs):
        ins, xs = refs[:n_in], refs[n_in:n_in + nx]
        outs, xd = refs[n_in + nx:n_in + nx + n_out], refs[n_in + nx + n_out:n_in + 2 * nx + n_out]
        scratch = refs[n_in + 2 * nx + n_out:]
        xchg.start_at_first_step(grid, xs, xd, scratch[-3:])
        body(*ins, *outs, *scratch[:-3])
        xchg.wait_at_last_step(grid, xs, xd, scratch[-3:])

    return wrapped


def _adamw_math(w, g, m, v):
    m2 = ADAM_B1 * m + (1.0 - ADAM_B1) * g
    v2 = ADAM_B2 * v + (1.0 - ADAM_B2) * (g * g)
    m_hat = m2 / (1.0 - ADAM_B1 ** ADAM_STEP)
    v_hat = v2 / (1.0 - ADAM_B2 ** ADAM_STEP)
    delta = -ADAM_LR * (m_hat / (jnp.sqrt(v_hat) + ADAM_EPS) + ADAM_WD * w)
    return delta, m2, v2


def _row_tile(rows):
    for t in (256, 128, 64, 32, 16, 8):
        if rows % t == 0:
            return t
    return rows


def _reduce_adamw(parts, w, m, v, name):
    _, r, cdim = parts.shape
    tr = _row_tile(r)

    def body(p_ref, w_ref, m_ref, v_ref, g_ref, d_ref, m2_ref, v2_ref):
        g = p_ref[0].astype(f32)
        for j in range(1, N_DEV):
            g = g + p_ref[j].astype(f32)
        delta, m2, v2 = _adamw_math(w_ref[...], g, m_ref[...], v_ref[...])
        g_ref[...] = g
        d_ref[...] = delta
        m2_ref[...] = m2
        v2_ref[...] = v2

    spec = pl.BlockSpec((tr, cdim), lambda i: (i, 0))
    return pl.pallas_call(
        body, name=name, grid=(r // tr,),
        in_specs=[pl.BlockSpec((N_DEV, tr, cdim), lambda i: (0, i, 0)), spec, spec, spec],
        out_specs=[spec] * 4,
        out_shape=[jax.ShapeDtypeStruct((r, cdim), f32)] * 4,
        compiler_params=_cparams(("parallel",)),
    )(parts, w, m, v)


def _adamw(w, g, m, v, name):
    r, cdim = w.shape
    tr = _row_tile(r)

    def body(w_ref, g_ref, m_ref, v_ref, d_ref, m2_ref, v2_ref):
        delta, m2, v2 = _adamw_math(w_ref[...], g_ref[...], m_ref[...], v_ref[...])
        d_ref[...] = delta
        m2_ref[...] = m2
        v2_ref[...] = v2

    spec = pl.BlockSpec((tr, cdim), lambda i: (i, 0))
    return pl.pallas_call(
        body, name=name, grid=(r // tr,),
        in_specs=[spec] * 4, out_specs=[spec] * 3,
        out_shape=[jax.ShapeDtypeStruct((r, cdim), f32)] * 3,
        compiler_params=_cparams(("parallel",)),
    )(w, g, m, v)


def _sum_devices(parts, name):
    _, r, cdim = parts.shape

    def body(p_ref, o_ref):
        g = p_ref[0]
        for j in range(1, N_DEV):
            g = g + p_ref[j]
        o_ref[...] = g

    return pl.pallas_call(
        body, name=name, out_shape=jax.ShapeDtypeStruct((r, cdim), f32),
        in_specs=[_VMEM], out_specs=_VMEM,
    )(parts)


def _ada_wgrad(c_all8, dmod_cols):
    nsh = dmod_cols.shape[1]

    def body(c_ref, d_ref, o_ref):
        cv = c_ref[...]
        o_ref[...] = lax.dot_general(cv * jax.nn.sigmoid(cv), d_ref[...], _TN, precision=_HI,
                                     preferred_element_type=f32)

    return pl.pallas_call(
        body, name="ada_wgrad", out_shape=jax.ShapeDtypeStruct((D_MODEL, nsh), f32),
        in_specs=[_VMEM, _VMEM], out_specs=_VMEM,
        compiler_params=pltpu.CompilerParams(vmem_limit_bytes=VMEM_LIMIT),
    )(c_all8, dmod_cols)


def _cols(t):
    return t.transpose(1, 0, 2).reshape(t.shape[1], N_DEV * t.shape[2])


def _col_blocks(t, n):
    return t.reshape(t.shape[0], N_DEV, n).transpose(1, 0, 2).astype(bf16)


def _row_blocks(t):
    return t.reshape(N_DEV, t.shape[0] // N_DEV, t.shape[1]).astype(bf16)


def _local_step(x, tgt, mod, norm_attn_g, w_in_p, rel_bias, conv_full, a_log, dt_bias, delta_norm_g,
                norm_ffn_g, final_norm_g, w_out_sh, w_gate_sh, w_up_sh, w_down_sh):
    s = x.shape[0]
    sh1, sc1, g1, sh2, sc2, g2 = [mod[:, i * D_MODEL:(i + 1) * D_MODEL] for i in range(6)]
    nag = norm_attn_g.reshape(1, D_MODEL)
    nfg = norm_ffn_g.reshape(1, D_MODEL)
    fg = final_norm_g.reshape(1, D_MODEL)
    idx = _bucket_tables()
    bias = _bias_tables(rel_bias, idx)
    alog_e = jnp.repeat(a_log.reshape(N_HEADS), HEAD_DIM)[None]
    dt_e = jnp.repeat(dt_bias.reshape(N_HEADS), HEAD_DIM)[None]
    ng_e = jnp.tile(delta_norm_g.reshape(HEAD_DIM), N_HEADS)[None]

    h1 = _ln_mod_fwd(x, nag, sc1, sh1, "ln1_fwd")
    proj, (w_out_g, w_gate_g) = _mm(h1, w_in_p, "nn", f32, 512, 1280, 1024, "in_proj",
                                    xchg=_Exchange([w_out_sh, w_gate_sh], gather=True))
    (y_attn, lse), (w_up_g, w_down_g) = _attn_fwd(proj, bias, _Exchange([w_up_sh, w_down_sh], gather=True))
    w_out_b = w_out_g.reshape(2 * GROUP_W, D_MODEL)
    w_gate_b, w_up_b = _cols(w_gate_g), _cols(w_up_g)
    w_down_b = w_down_g.reshape(D_FF, D_MODEL)
    n_ff = w_gate_sh.shape[1]
    sconv = _conv_silu_fwd(proj, conv_full)
    qn, kn, beta, g = _delta_prep_fwd(sconv, proj, alog_e, dt_e)
    u, w, qt, kh, qk, gm = _delta_chunk_pre(qn, kn, sconv, beta, g)
    o, ss = _delta_scan_fwd(u, w, qt, kh, qk, gm)
    y_delta = _gnorm_fwd(o, proj, ng_e)
    ycat = jnp.concatenate([y_attn, y_delta], axis=1).astype(bf16)
    y = _mm(ycat, w_out_b, "nn", f32, 512, 1024, 1024, "out_proj")
    x1, h2 = _resid_ln_mod_fwd(x, y, g1, nfg, sc2, sh2, "ln2_fwd")
    act, gate, up = _ffn_up(h2, w_gate_b, w_up_b, "ffn_up")
    y2 = _mm(act, w_down_b, "nn", f32, 512, 1024, D_FF, "ffn_down")
    dx2, dy2, loss, dfg, dg2 = _final_loss_bwd(x1, y2, g2, fg, tgt, "final_loss")

    dgate, dup = _ffn_down_dx(dy2, w_down_b, gate, up, "ffn_down_dx")
    g_down = _mm(act, dy2, "tn", f32, 1408, 1024, 512, "ffn_down_dw")
    dh2, (r_down,) = _mm_nt2(dgate, w_gate_b, dup, w_up_b, 512, 1024, 1408, "ffn_up_dx",
                             _Exchange([_row_blocks(g_down)], gather=False))
    g_gate = _mm(h2, dgate, "tn", f32, 1024, 1408, 512, "ffn_gate_dw")
    g_up = _mm(h2, dup, "tn", f32, 1024, 1408, 512, "ffn_up_dw")
    dx1, dsh2, dsc2, dnfg, dy, dg1 = _ln_mod_bwd(x1, nfg, sc2, dh2, dx2, "ln2_bwd", gate=g1, y=y)
    dycat = _mm(dy, w_out_b, "nt", f32, 512, 1024, 1024, "out_proj_dx")
    g_out = _mm(ycat, dy, "tn", f32, 1024, 1024, 512, "out_proj_dw")
    dq, dk, dv, dbias = _attn_bwd(proj, bias, y_attn, lse, dycat)
    g_rb = _bias_grad(dbias, idx)[:, :, 0].T
    do, dz, dng = _gnorm_bwd(o, proj, ng_e, dycat)
    dso = _delta_scan_bwd(w, qt, kh, qk, gm, do)
    dqn, dkn, dvd, dbeta, dgd = _delta_chunk_bwd(qn, kn, sconv, beta, g, ss, dso, do)
    dsq, dsk, dba, dal, ddt = _delta_prep_bwd(sconv, proj, alog_e, dt_e, dqn, dkn, dbeta, dgd)
    (dxc, g_conv), (r_gate, r_up, r_out) = _conv_silu_bwd(
        proj, conv_full, jnp.concatenate([dsq, dsk, dvd], axis=1),
        _Exchange([_col_blocks(g_gate, n_ff), _col_blocks(g_up, n_ff), _row_blocks(g_out)],
                  gather=False))
    dproj = jnp.concatenate([dq, dk, dv, dxc, dz, dba, jnp.zeros((s, IN_PAD - BA_BLOCK * 128 - 128), f32)],
                            axis=1).astype(bf16)
    g_in = _mm(h1, dproj, "tn", f32, 1024, 1280, 512, "in_proj_dw")
    dh1, (r_in,) = _mm(dproj, w_in_p, "nt", f32, 512, 1024, 1280, "in_proj_dx",
                       xchg=_Exchange([_col_blocks(g_in[:, :IN_WIDTH], IN_WIDTH // N_DEV)], gather=False))
    gx, dsh1, dsc1, dnag = _ln_mod_bwd(x, nag, sc1, dh1, dx1, "ln1_bwd")
    grads = dict(
        x=gx, mod=jnp.concatenate([dsh1, dsc1, dg1, dsh2, dsc2, dg2], axis=1),
        norm_attn_g=dnag, norm_ffn_g=dnfg, final_norm_g=dfg, rel_bias=g_rb, conv_w=g_conv,
        a_log=dal.reshape(N_HEADS, HEAD_DIM).sum(-1), dt_bias=ddt.reshape(N_HEADS, HEAD_DIM).sum(-1),
        delta_norm_g=dng.reshape(N_HEADS, HEAD_DIM).sum(0),
        w_in=r_in, w_out=r_out, w_gate=r_gate, w_up=r_up, w_down=r_down)
    return loss[0, 0], grads


MISC_OFF = dict(rel_bias=0, a_log=256, dt_bias=264, delta_norm_g=272)


def _misc_row(rel_bias, a_log, dt_bias, delta_norm_g):
    flat = jnp.concatenate([rel_bias.reshape(-1), a_log.reshape(-1), dt_bias.reshape(-1), delta_norm_g.reshape(-1)])
    return jnp.pad(flat, (0, D_MODEL - flat.shape[0]))[None]


def _pack_small(b_ada, nag, nfg, fng, rel_bias, a_log, dt_bias, dng, conv_shard):
    rows = [b_ada.reshape(6, D_MODEL), nag.reshape(1, D_MODEL), nfg.reshape(1, D_MODEL), fng.reshape(1, D_MODEL),
            _misc_row(rel_bias, a_log, dt_bias, dng),
            jnp.pad(conv_shard.reshape(-1), (0, D_MODEL - conv_shard.size))[None],
            jnp.zeros((5, D_MODEL), f32)]
    return jnp.concatenate(rows, axis=0)


def _unpack_small(p, conv_shape):
    misc = p[9]
    return dict(
        b_ada=p[0:6].reshape(1, 6 * D_MODEL), norm_attn_g=p[6:7], norm_ffn_g=p[7:8], final_norm_g=p[8],
        rel_bias=misc[0:256].reshape(N_BUCKETS, N_HEADS), a_log=misc[256:264].reshape(1, N_HEADS),
        dt_bias=misc[264:272].reshape(1, N_HEADS), delta_norm_g=misc[272:336].reshape(1, HEAD_DIM),
        conv_w=p[10, :conv_shape[1] * conv_shape[2]].reshape(conv_shape))


def kernel(x, c, w_ada, b_ada, norm_attn_g, w_in, rel_bias, conv_w, a_log, dt_bias, delta_norm_g, w_out, norm_ffn_g, w_gate, w_up, w_down, final_norm_g, loss_target, m_w_ada, m_b_ada, m_norm_attn_g, m_w_in, m_rel_bias, m_conv_w, m_a_log, m_dt_bias, m_delta_norm_g, m_w_out, m_norm_ffn_g, m_w_gate, m_w_up, m_w_down, m_final_norm_g, v_w_ada, v_b_ada, v_norm_attn_g, v_w_in, v_rel_bias, v_conv_w, v_a_log, v_dt_bias, v_delta_norm_g, v_w_out, v_norm_ffn_g, v_w_gate, v_w_up, v_w_down, v_final_norm_g):
    me = 4 * lax.axis_index("x") + 2 * lax.axis_index("y") + lax.axis_index("c")
    ada_sh = w_ada.shape[2]
    conv_sh = conv_w.shape[2]

    cv = jnp.concatenate([c[0], conv_w[0].reshape(-1)])
    cv8 = jnp.zeros((8, 2 * D_MODEL), f32).at[0, :cv.shape[0]].set(cv)
    b8 = jnp.broadcast_to(b_ada.reshape(N_DEV, 1, ada_sh), (N_DEV, 8, ada_sh))
    call, modp, w_in_g = _ada_exchange(cv8, w_ada[0], b8, w_in[0].astype(bf16))
    mod = modp[:, 0, :].reshape(1, 6 * D_MODEL)
    c_all = call[:, 0, :D_MODEL]
    conv_full = call[:, 0, D_MODEL:D_MODEL + CONV_WIDTH * conv_sh].reshape(N_DEV, CONV_WIDTH, conv_sh)
    conv_full = conv_full.transpose(1, 0, 2).reshape(CONV_WIDTH, N_DEV * conv_sh)

    w_in_p = jnp.pad(_cols(w_in_g), ((0, 0), (0, IN_PAD - IN_WIDTH)))
    loss_local, gr = _local_step(x[0], loss_target[0], mod, norm_attn_g, w_in_p, rel_bias, conv_full, a_log,
                                 dt_bias, delta_norm_g, norm_ffn_g, final_norm_g, w_out[0].astype(bf16),
                                 w_gate[0].astype(bf16), w_up[0].astype(bf16), w_down[0].astype(bf16))
    loss = lax.psum(loss_local, ("x", "y", "c"))

    small = jnp.concatenate([
        gr["mod"].reshape(6, D_MODEL), gr["norm_attn_g"], gr["norm_ffn_g"], gr["final_norm_g"],
        gr["conv_w"].reshape(6, D_MODEL),
        _misc_row(gr["rel_bias"], gr["a_log"], gr["dt_bias"], gr["delta_norm_g"])], axis=0)
    parts = _all_to_all([jnp.broadcast_to(small[None], (N_DEV,) + small.shape)], "small_gather")[0]
    tot = _sum_devices(parts, "small_sum")
    g_conv_full = tot[9:15].reshape(CONV_WIDTH, N_DEV * conv_sh)
    g_conv = lax.dynamic_slice(g_conv_full, (0, me * conv_sh), (CONV_WIDTH, conv_sh))
    misc = tot[15]
    g_small = _pack_small(tot[0:6], tot[6], tot[7], tot[8], misc[0:256], misc[256:264], misc[264:272],
                          misc[272:336], g_conv)
    pk = lambda pre: _pack_small(pre[0], pre[1], pre[2], pre[3], pre[4], pre[5], pre[6], pre[7], pre[8])
    w_small = pk((b_ada, norm_attn_g, norm_ffn_g, final_norm_g, rel_bias, a_log, dt_bias, delta_norm_g, conv_w))
    m_small = pk((m_b_ada, m_norm_attn_g, m_norm_ffn_g, m_final_norm_g, m_rel_bias, m_a_log, m_dt_bias,
                  m_delta_norm_g, m_conv_w))
    v_small = pk((v_b_ada, v_norm_attn_g, v_norm_ffn_g, v_final_norm_g, v_rel_bias, v_a_log, v_dt_bias,
                  v_delta_norm_g, v_conv_w))
    d_small, m2_small, v2_small = _adamw(w_small, g_small, m_small, v_small, "adamw_small")
    cshape = conv_w.shape
    G, Dl, M2, V2 = (_unpack_small(t, cshape) for t in (g_small, d_small, m2_small, v2_small))

    dmod_all = parts[:, 0:6, :].reshape(N_DEV, 6 * D_MODEL)
    dmod_cols = lax.dynamic_slice(dmod_all, (0, me * ada_sh), (N_DEV, ada_sh))
    g_ada = _ada_wgrad(c_all, dmod_cols)
    d_ada, m2_ada, v2_ada = _adamw(w_ada[0], g_ada, m_w_ada[0], v_w_ada[0], "adamw_w_ada")

    big = {}
    for name, w_, m_, v_ in (("w_in", w_in, m_w_in, v_w_in), ("w_out", w_out, m_w_out, v_w_out),
                             ("w_gate", w_gate, m_w_gate, v_w_gate), ("w_up", w_up, m_w_up, v_w_up),
                             ("w_down", w_down, m_w_down, v_w_down)):
        big[name] = [t[None] for t in _reduce_adamw(gr[name], w_[0], m_[0], v_[0], "reduce_adamw_" + name)]

    def leaf(i, name):
        if name == "w_ada":
            return (g_ada, d_ada, m2_ada, v2_ada)[i][None]
        if name in big:
            return big[name][i]
        return (G, Dl, M2, V2)[i][name]

    order = ["w_ada", "b_ada", "norm_attn_g", "w_in", "rel_bias", "conv_w", "a_log", "dt_bias", "delta_norm_g",
             "w_out", "norm_ffn_g", "w_gate", "w_up", "w_down", "final_norm_g"]
    outs = [loss, gr["x"][None]]
    for i in range(4):
        outs += [leaf(i, n) for n in order]
    return tuple(outs)
```

```python
import functools
import math

import jax
import jax.numpy as jnp
from jax import lax
from jax.experimental import pallas as pl
from jax.experimental.pallas import tpu as pltpu

f32 = jnp.float32
bf16 = jnp.bfloat16

D_MODEL = 1024
HEAD_DIM = 64
N_HEADS = 8
GROUP_W = 512
IN_WIDTH = 3600
IN_PAD = 3840
D_FF = 2816
EPS = 1e-6
NEG_INF = -1e30
BAND = 128
PAD_UNIT = 2048
DILATIONS = (1, 4, 16)
N_BUCKETS = 32
MAX_DISTANCE = 2048
CONV_WIDTH = 4
CHUNK = 64
N_DEV = 8
VMEM_LIMIT = 56 * 1024 * 1024

ADAM_LR, ADAM_B1, ADAM_B2, ADAM_EPS, ADAM_WD, ADAM_STEP = 0.001, 0.9, 0.999, 1e-08, 0.01, 10


def _cparams(sem):
    return pltpu.CompilerParams(dimension_semantics=sem, vmem_limit_bytes=VMEM_LIMIT)


def _mm(a, b, mode, out_dtype, tm, tn, tk, name, xchg=None):
    if mode == "nn":
        (m, k), (_, n) = a.shape, b.shape
        a_spec = pl.BlockSpec((tm, tk), lambda j, i, kk: (i, kk))
        b_spec = pl.BlockSpec((tk, tn), lambda j, i, kk: (kk, j))
        dims = (((1,), (0,)), ((), ()))
    elif mode == "nt":
        (m, k), (n, _) = a.shape, b.shape
        a_spec = pl.BlockSpec((tm, tk), lambda j, i, kk: (i, kk))
        b_spec = pl.BlockSpec((tn, tk), lambda j, i, kk: (j, kk))
        dims = (((1,), (1,)), ((), ()))
    else:
        (k, m), (_, n) = a.shape, b.shape
        a_spec = pl.BlockSpec((tk, tm), lambda j, i, kk: (kk, i))
        b_spec = pl.BlockSpec((tk, tn), lambda j, i, kk: (kk, j))
        dims = (((0,), (0,)), ((), ()))
    assert m % tm == 0 and n % tn == 0 and k % tk == 0, (name, m, n, k, tm, tn, tk)
    nk = k // tk
    grid = (n // tn, m // tm, nk)
    nx = xchg.n if xchg is not None else 0

    def body(*refs):
        a_ref, b_ref = refs[:2]
        o_ref = refs[2 + nx]
        scratch = refs[3 + 2 * nx:]
        if nx:
            xrefs = (refs[2:2 + nx], refs[3 + nx:3 + 2 * nx], scratch[-3:])
            xchg.start_at_first_step(grid, *xrefs)
        if nk == 1:
            o_ref[...] = lax.dot_general(a_ref[...].astype(bf16), b_ref[...].astype(bf16), dims,
                                         preferred_element_type=f32).astype(o_ref.dtype)
        else:
            acc_ref = scratch[0]
            kk = pl.program_id(2)

            @pl.when(kk == 0)
            def _():
                acc_ref[...] = jnp.zeros_like(acc_ref)

            acc_ref[...] += lax.dot_general(a_ref[...].astype(bf16), b_ref[...].astype(bf16), dims,
                                            preferred_element_type=f32)

            @pl.when(kk == nk - 1)
            def _():
                o_ref[...] = acc_ref[...].astype(o_ref.dtype)
        if nx:
            xchg.wait_at_last_step(grid, *xrefs)

    out = pl.pallas_call(
        body, name=name, grid=grid,
        in_specs=[a_spec, b_spec] + ([_ANY] * nx),
        out_specs=[pl.BlockSpec((tm, tn), lambda j, i, kk: (i, j))] + ([_ANY] * nx),
        out_shape=[jax.ShapeDtypeStruct((m, n), out_dtype)] + (xchg.out_shape() if nx else []),
        scratch_shapes=([pltpu.VMEM((tm, tn), f32)] if nk > 1 else []) + (xchg.scratch() if nx else []),
        compiler_params=_cparams(("arbitrary",) * 3 if nx else ("parallel", "parallel", "arbitrary")),
    )(a, b, *(xchg.arrs if nx else []))
    return (out[0], out[1:]) if nx else out[0]


def _mm_nt2(a1, b1, a2, b2, tm, tn, name, xchg):
    (m, k), (n, _) = a1.shape, b1.shape
    assert a2.shape == a1.shape and b2.shape == b1.shape and m % tm == 0 and n % tn == 0
    grid = (n // tn, m // tm)

    def body(a1_ref, b1_ref, a2_ref, b2_ref, o_ref):
        o_ref[...] = (lax.dot_general(a1_ref[...], b1_ref[...], _NT, preferred_element_type=f32)
                      + lax.dot_general(a2_ref[...], b2_ref[...], _NT, preferred_element_type=f32))

    a_spec = pl.BlockSpec((tm, k), lambda j, i: (i, 0))
    b_spec = pl.BlockSpec((tn, k), lambda j, i: (j, 0))
    out = pl.pallas_call(
        _ride(body, 4, 1, xchg, grid), name=name, grid=grid,
        in_specs=[a_spec, b_spec, a_spec, b_spec] + [_ANY] * xchg.n,
        out_specs=[pl.BlockSpec((tm, tn), lambda j, i: (i, j))] + [_ANY] * xchg.n,
        out_shape=[jax.ShapeDtypeStruct((m, n), f32)] + xchg.out_shape(),
        scratch_shapes=xchg.scratch(),
        compiler_params=_cparams(("arbitrary",) * 2),
    )(a1, b1, a2, b2, *xchg.arrs)
    return out[0], out[1:]


TOK_TILE = 512


def _row_spec(width, tile=TOK_TILE):
    return pl.BlockSpec((tile, width), lambda i: (i, 0))


def _vec_spec(width, rows=1):
    return pl.BlockSpec((rows, width), lambda i: (0, 0))


def _ln_mod_fwd(x, gain, sc, sh, name):
    s, d = x.shape

    def body(x_ref, g_ref, sc_ref, sh_ref, h_ref):
        xv = x_ref[...]
        rstd = lax.rsqrt(jnp.mean(xv * xv, axis=-1, keepdims=True) + EPS)
        h = (xv * rstd) * g_ref[...] * (1.0 + sc_ref[...]) + sh_ref[...]
        h_ref[...] = h.astype(bf16)

    return pl.pallas_call(
        body, name=name, grid=(s // TOK_TILE,),
        in_specs=[_row_spec(d), _vec_spec(d), _vec_spec(d), _vec_spec(d)],
        out_specs=_row_spec(d),
        out_shape=jax.ShapeDtypeStruct((s, d), bf16),
        compiler_params=_cparams(("parallel",)),
    )(x, gain, sc, sh)


def _resid_ln_mod_fwd(x, y, gate, gain, sc, sh, name):
    s, d = x.shape

    def body(x_ref, y_ref, gt_ref, g_ref, sc_ref, sh_ref, x1_ref, h_ref):
        x1 = x_ref[...] + gt_ref[...] * y_ref[...]
        x1_ref[...] = x1
        rstd = lax.rsqrt(jnp.mean(x1 * x1, axis=-1, keepdims=True) + EPS)
        h = (x1 * rstd) * g_ref[...] * (1.0 + sc_ref[...]) + sh_ref[...]
        h_ref[...] = h.astype(bf16)

    return pl.pallas_call(
        body, name=name, grid=(s // TOK_TILE,),
        in_specs=[_row_spec(d), _row_spec(d)] + [_vec_spec(d)] * 4,
        out_specs=[_row_spec(d), _row_spec(d)],
        out_shape=[jax.ShapeDtypeStruct((s, d), f32), jax.ShapeDtypeStruct((s, d), bf16)],
        compiler_params=_cparams(("parallel",)),
    )(x, y, gate, gain, sc, sh)


FFN_TN = 1408


def _ffn_up(h2, w_gate, w_up, name):
    s, d = h2.shape
    tm = TOK_TILE

    def body(h_ref, wg_ref, wu_ref, a_ref, g_ref, u_ref):
        h = h_ref[...]
        g = jnp.dot(h, wg_ref[...], preferred_element_type=f32)
        u = jnp.dot(h, wu_ref[...], preferred_element_type=f32)
        a_ref[...] = (g * jax.nn.sigmoid(g) * u).astype(bf16)
        g_ref[...] = g.astype(bf16)
        u_ref[...] = u.astype(bf16)

    w_spec = pl.BlockSpec((d, FFN_TN), lambda j, i: (0, j))
    o_spec = pl.BlockSpec((tm, FFN_TN), lambda j, i: (i, j))
    return pl.pallas_call(
        body, name=name, grid=(D_FF // FFN_TN, s // tm),
        in_specs=[pl.BlockSpec((tm, d), lambda j, i: (i, 0)), w_spec, w_spec],
        out_specs=[o_spec] * 3,
        out_shape=[jax.ShapeDtypeStruct((s, D_FF), bf16)] * 3,
        compiler_params=_cparams(("parallel", "parallel")),
    )(h2, w_gate, w_up)


def _ffn_down_dx(dy2, w_down, gate, up, name):
    s, d = dy2.shape
    tm = TOK_TILE

    def body(dy_ref, w_ref, g_ref, u_ref, dg_ref, du_ref):
        da = lax.dot_general(dy_ref[...], w_ref[...], _NT, preferred_element_type=f32)
        g = g_ref[...].astype(f32)
        sg = jax.nn.sigmoid(g)
        du_ref[...] = (da * g * sg).astype(bf16)
        dg_ref[...] = (da * u_ref[...].astype(f32) * sg * (1.0 + g * (1.0 - sg))).astype(bf16)

    t_spec = pl.BlockSpec((tm, FFN_TN), lambda j, i: (i, j))
    return pl.pallas_call(
        body, name=name, grid=(D_FF // FFN_TN, s // tm),
        in_specs=[pl.BlockSpec((tm, d), lambda j, i: (i, 0)), pl.BlockSpec((FFN_TN, d), lambda j, i: (j, 0)),
                  t_spec, t_spec],
        out_specs=[t_spec, t_spec],
        out_shape=[jax.ShapeDtypeStruct((s, D_FF), bf16)] * 2,
        compiler_params=_cparams(("parallel", "parallel")),
    )(dy2, w_down, gate, up)


def _acc_spec(width):
    return pl.BlockSpec((1, width), lambda i: (0, 0))


def _final_loss_bwd(x1, y2, gate2, final_g, target, name):
    s, d = x1.shape

    def body(x1_ref, y2_ref, gt_ref, fg_ref, tg_ref, dx2_ref, dy2_ref, loss_ref, dfg_ref, dgt_ref):
        @pl.when(pl.program_id(0) == 0)
        def _():
            loss_ref[...] = jnp.zeros_like(loss_ref)
            dfg_ref[...] = jnp.zeros_like(dfg_ref)
            dgt_ref[...] = jnp.zeros_like(dgt_ref)

        y2 = y2_ref[...]
        gt = gt_ref[...]
        fg = fg_ref[...]
        x2 = x1_ref[...] + gt * y2
        rstd = lax.rsqrt(jnp.mean(x2 * x2, axis=-1, keepdims=True) + EPS)
        xn = x2 * rstd
        err = xn * fg - tg_ref[...]
        row = jnp.sum(err * err, axis=-1, keepdims=True) * (0.5 / d)
        loss_ref[...] += jnp.sum(row, axis=0, keepdims=True) + jnp.zeros_like(loss_ref)
        dout = err * (1.0 / d)
        dfg_ref[...] += jnp.sum(dout * xn, axis=0, keepdims=True)
        dxn = dout * fg
        dx2 = rstd * (dxn - xn * jnp.mean(dxn * xn, axis=-1, keepdims=True))
        dx2_ref[...] = dx2
        dgt_ref[...] += jnp.sum(dx2 * y2, axis=0, keepdims=True)
        dy2_ref[...] = (gt * dx2).astype(bf16)

    return pl.pallas_call(
        body, name=name, grid=(s // TOK_TILE,),
        in_specs=[_row_spec(d), _row_spec(d), _vec_spec(d), _vec_spec(d), _row_spec(d)],
        out_specs=[_row_spec(d), _row_spec(d), _acc_spec(128), _acc_spec(d), _acc_spec(d)],
        out_shape=[jax.ShapeDtypeStruct((s, d), f32), jax.ShapeDtypeStruct((s, d), bf16),
                   jax.ShapeDtypeStruct((1, 128), f32), jax.ShapeDtypeStruct((1, d), f32),
                   jax.ShapeDtypeStruct((1, d), f32)],
        compiler_params=_cparams(("arbitrary",)),
    )(x1, y2, gate2, final_g, target)


def _ln_mod_bwd(xin, gain, sc, dh, dres, name, gate=None, y=None):
    s, d = xin.shape
    with_gate = gate is not None

    def body(*refs):
        if with_gate:
            (x_ref, g_ref, sc_ref, dh_ref, dr_ref, gt_ref, y_ref,
             dx_ref, dsh_ref, dsc_ref, dg_ref, dy_ref, dgt_ref) = refs
        else:
            x_ref, g_ref, sc_ref, dh_ref, dr_ref, dx_ref, dsh_ref, dsc_ref, dg_ref = refs

        @pl.when(pl.program_id(0) == 0)
        def _():
            dsh_ref[...] = jnp.zeros_like(dsh_ref)
            dsc_ref[...] = jnp.zeros_like(dsc_ref)
            dg_ref[...] = jnp.zeros_like(dg_ref)
            if with_gate:
                dgt_ref[...] = jnp.zeros_like(dgt_ref)

        xv = x_ref[...]
        g = g_ref[...]
        sc1 = 1.0 + sc_ref[...]
        dh = dh_ref[...]
        rstd = lax.rsqrt(jnp.mean(xv * xv, axis=-1, keepdims=True) + EPS)
        xn = xv * rstd
        dsh_ref[...] += jnp.sum(dh, axis=0, keepdims=True)
        dsc_ref[...] += jnp.sum(dh * (xn * g), axis=0, keepdims=True)
        dg_ref[...] += jnp.sum(dh * sc1 * xn, axis=0, keepdims=True)
        dxn = dh * sc1 * g
        dx = dr_ref[...] + rstd * (dxn - xn * jnp.mean(dxn * xn, axis=-1, keepdims=True))
        dx_ref[...] = dx
        if with_gate:
            dgt_ref[...] += jnp.sum(dx * y_ref[...], axis=0, keepdims=True)
            dy_ref[...] = (gt_ref[...] * dx).astype(bf16)

    in_specs = [_row_spec(d), _vec_spec(d), _vec_spec(d), _row_spec(d), _row_spec(d)]
    out_specs = [_row_spec(d), _acc_spec(d), _acc_spec(d), _acc_spec(d)]
    out_shape = [jax.ShapeDtypeStruct((s, d), f32)] + [jax.ShapeDtypeStruct((1, d), f32)] * 3
    args = [xin, gain, sc, dh, dres]
    if with_gate:
        in_specs += [_vec_spec(d), _row_spec(d)]
        out_specs += [_row_spec(d), _acc_spec(d)]
        out_shape += [jax.ShapeDtypeStruct((s, d), bf16), jax.ShapeDtypeStruct((1, d), f32)]
        args += [gate, y]
    return pl.pallas_call(
        body, name=name, grid=(s // TOK_TILE,),
        in_specs=in_specs, out_specs=out_specs, out_shape=out_shape,
        compiler_params=_cparams(("arbitrary",)),
    )(*args)


def _bucket_tables():
    import numpy as np
    qi = np.arange(BAND)[:, None]
    kj = np.arange(2 * BAND)[None, :]
    steps = qi + BAND - kj
    max_exact = N_BUCKETS // 2
    out = []
    for d in DILATIONS:
        dist = np.maximum(steps, 0) * d
        dist_f = np.maximum(dist, 1).astype(np.float32)
        large = max_exact + (np.log(dist_f / np.float32(max_exact)) / np.float32(math.log(MAX_DISTANCE / max_exact))
                             * np.float32(N_BUCKETS - max_exact)).astype(np.int32)
        out.append(np.where(dist < max_exact, dist, np.minimum(large, N_BUCKETS - 1)))
    return jnp.asarray(np.stack(out).astype(np.int32))


def _bias_tables(rel_bias, idx):
    def body(idx_ref, rb_ref, o_ref):
        h = pl.program_id(1)
        idxv = idx_ref[0]
        acc = jnp.zeros((BAND, 2 * BAND), f32)
        for b in range(N_BUCKETS):
            acc = jnp.where(idxv == b, rb_ref[b, h], acc)
        o_ref[0, 0] = acc

    return pl.pallas_call(
        body, name="attn_bias_tables", grid=(3, N_HEADS),
        in_specs=[pl.BlockSpec((1, BAND, 2 * BAND), lambda br, h: (br, 0, 0)),
                  pl.BlockSpec(memory_space=pltpu.SMEM)],
        out_specs=pl.BlockSpec((1, 1, BAND, 2 * BAND), lambda br, h: (br, h, 0, 0)),
        out_shape=jax.ShapeDtypeStruct((3, N_HEADS, BAND, 2 * BAND), f32),
        compiler_params=_cparams(("parallel", "parallel")),
    )(idx, rel_bias)


def _bias_grad(dbias, idx):
    def body(idx_ref, db_ref, o_ref):
        br = pl.program_id(1)

        @pl.when(br == 0)
        def _():
            o_ref[...] = jnp.zeros_like(o_ref)

        idxv = idx_ref[0]
        dbv = db_ref[0, 0]
        row = lax.broadcasted_iota(jnp.int32, (N_BUCKETS, 128), 0)
        acc = jnp.zeros((N_BUCKETS, 128), f32)
        for b in range(N_BUCKETS):
            sb = jnp.sum(jnp.sum(jnp.where(idxv == b, dbv, 0.0), axis=1, keepdims=True), axis=0, keepdims=True)
            acc = acc + jnp.where(row == b, sb, 0.0)
        o_ref[0] += acc

    return pl.pallas_call(
        body, name="attn_bias_grad", grid=(N_HEADS, 3),
        in_specs=[pl.BlockSpec((1, BAND, 2 * BAND), lambda h, br: (br, 0, 0)),
                  pl.BlockSpec((1, 1, BAND, 2 * BAND), lambda h, br: (br, h, 0, 0))],
        out_specs=pl.BlockSpec((1, N_BUCKETS, 128), lambda h, br: (h, 0, 0)),
        out_shape=jax.ShapeDtypeStruct((N_HEADS, N_BUCKETS, 128), f32),
        compiler_params=_cparams(("parallel", "arbitrary")),
    )(idx, dbias)


def _attn_masks():
    lane = lax.broadcasted_iota(jnp.int32, (BAND, 128), 1)
    m0 = lane < HEAD_DIM
    qi = lax.broadcasted_iota(jnp.int32, (BAND, 2 * BAND), 0)
    kj = lax.broadcasted_iota(jnp.int32, (BAND, 2 * BAND), 1)
    steps = qi + BAND - kj
    in_window = (steps >= 0) & (steps <= BAND)
    return m0, in_window, kj >= BAND


_NT = (((1,), (1,)), ((), ()))
_TN = (((0,), (0,)), ((), ()))
_BNN = (((2,), (1,)), ((0,), (0,)))
_BNT = (((2,), (2,)), ((0,), (0,)))
_BTN = (((1,), (1,)), ((0,), (0,)))
ATTN_GROUP = 4
ATTN_ITEMS = PAD_UNIT // BAND
Q_COL, K_COL, V_COL = 0, 4, 8


def _attn_item_rows(j, d, c, cbase):
    r = lax.rem(j, d)
    b = lax.div(j, d)
    loc = b * (d * BAND) + r
    first = jnp.logical_and(c == 0, b == 0)
    start = cbase + loc
    pstart = jnp.where(first, start, start - d * BAND)
    return loc, start, pstart, first


def _attn_fwd(proj, bias, xchg):
    s = proj.shape[0]

    def body(q_ref, k_ref, v_ref, b_ref, y_ref, lse_ref, o_s, l_s):
        c = pl.program_id(1)
        cbase = pl.multiple_of(c * PAD_UNIT, PAD_UNIT)
        m0, in_window, cur_half = _attn_masks()
        for bi, d in enumerate(DILATIONS):
            def group(jg, carry, bi=bi, d=d):
                locs, qs, ks, vs, pens = [], [], [], [], []
                for t in range(ATTN_GROUP):
                    loc, start, pstart, first = _attn_item_rows(jg * ATTN_GROUP + t, d, c, cbase)
                    locs.append(loc)
                    qs.append(q_ref[pl.ds(loc, BAND, stride=d), :])
                    ks.append(jnp.concatenate([k_ref[pl.ds(pstart, BAND, stride=d), :],
                                               k_ref[pl.ds(start, BAND, stride=d), :]], axis=0))
                    vs.append(jnp.concatenate([v_ref[pl.ds(pstart, BAND, stride=d), :],
                                               v_ref[pl.ds(start, BAND, stride=d), :]], axis=0))
                    pens.append(jnp.where(cur_half, 0.0, jnp.where(first, NEG_INF, 0.0)))
                q = jnp.stack(qs)
                kk = jnp.stack(ks + ks).astype(bf16)
                vv = jnp.stack(vs + vs).astype(bf16)
                pen = jnp.stack(pens + pens)
                qh = (jnp.concatenate([jnp.where(m0, q, 0.0), jnp.where(m0, 0.0, q)], axis=0) * 0.125).astype(bf16)
                sc = lax.dot_general(qh, kk, _BNT, preferred_element_type=f32)
                sc = (sc.reshape(2, ATTN_GROUP, BAND, 2 * BAND) + b_ref[bi][:, None]).reshape(sc.shape) + pen
                sc = jnp.where(in_window, sc, NEG_INF)
                mx = jnp.max(sc, axis=-1, keepdims=True)
                e = jnp.exp(sc - mx)
                l = jnp.sum(e, axis=-1, keepdims=True)
                o = lax.dot_general(e.astype(bf16), vv, _BNN, preferred_element_type=f32) / l
                ls = mx + jnp.log(l)
                for t in range(ATTN_GROUP):
                    rows = pl.ds(locs[t], BAND, stride=d)
                    o_s[bi, rows, :] = jnp.where(m0, o[t], o[ATTN_GROUP + t])
                    l_s[bi, rows, :] = jnp.where(m0, ls[t], ls[ATTN_GROUP + t])
                return carry

            lax.fori_loop(0, ATTN_ITEMS // ATTN_GROUP, group, 0)

        def merge(t, carry):
            rows = pl.ds(pl.multiple_of(t * 256, 256), 256)
            ls = [l_s[i, rows, :] for i in range(3)]
            mx = jnp.maximum(jnp.maximum(ls[0], ls[1]), ls[2])
            ws = [jnp.exp(l - mx) for l in ls]
            tot = ws[0] + ws[1] + ws[2]
            y = (ws[0] * o_s[0, rows, :] + ws[1] * o_s[1, rows, :] + ws[2] * o_s[2, rows, :]) / tot
            y_ref[rows, :] = y
            lse_ref[rows, :] = mx + jnp.log(tot)
            return carry

        lax.fori_loop(0, PAD_UNIT // 256, merge, 0)

    chunk = lambda col: pl.BlockSpec((PAD_UNIT, 128), lambda p, c: (c, col + p))
    full = lambda col: pl.BlockSpec((s, 128), lambda p, c: (0, col + p))
    grid = (N_HEADS // 2, s // PAD_UNIT)
    out = pl.pallas_call(
        _ride(body, 4, 2, xchg, grid), name="attn_fwd", grid=grid,
        in_specs=[chunk(Q_COL), full(K_COL), full(V_COL),
                  pl.BlockSpec((3, 2, BAND, 2 * BAND), lambda p, c: (0, p, 0, 0))] + [_ANY] * xchg.n,
        out_specs=[chunk(0), chunk(0)] + [_ANY] * xchg.n,
        out_shape=[jax.ShapeDtypeStruct((s, GROUP_W), f32)] * 2 + xchg.out_shape(),
        scratch_shapes=[pltpu.VMEM((3, PAD_UNIT, 128), f32)] * 2 + xchg.scratch(),
        compiler_params=_cparams(("arbitrary", "arbitrary")),
    )(proj, proj, proj, bias, *xchg.arrs)
    return out[:2], out[2:]


def _attn_bwd(proj, bias, y, lse, dycat):
    s = proj.shape[0]

    def body(q_ref, k_ref, v_ref, b_ref, y_ref, lse_ref, dy_ref, dq_ref, dk_ref, dv_ref, db_ref, dd_s):
        c = pl.program_id(1)
        cbase = pl.multiple_of(c * PAD_UNIT, PAD_UNIT)
        m0, in_window, cur_half = _attn_masks()

        @pl.when(c == 0)
        def _():
            dk_ref[...] = jnp.zeros_like(dk_ref)
            dv_ref[...] = jnp.zeros_like(dv_ref)
            db_ref[...] = jnp.zeros_like(db_ref)

        dq_ref[...] = jnp.zeros_like(dq_ref)

        def rowdot(t, carry):
            rows = pl.ds(pl.multiple_of(t * 256, 256), 256)
            prod = dy_ref[rows, :] * y_ref[rows, :]
            lane = lax.broadcasted_iota(jnp.int32, prod.shape, 1)
            h0 = lane < HEAD_DIM
            d0 = jnp.sum(jnp.where(h0, prod, 0.0), axis=-1, keepdims=True)
            d1 = jnp.sum(jnp.where(h0, 0.0, prod), axis=-1, keepdims=True)
            dd_s[rows, :] = jnp.where(h0, d0, d1)
            return carry

        lax.fori_loop(0, PAD_UNIT // 256, rowdot, 0)

        for bi, d in enumerate(DILATIONS):
            def group(jg, carry, bi=bi, d=d):
                ng = ATTN_GROUP
                meta, qs, dos, lqs, dds, ks, vs, pens = [], [], [], [], [], [], [], []
                for t in range(ng):
                    loc, start, pstart, first = _attn_item_rows(jg * ng + t, d, c, cbase)
                    qrows = pl.ds(loc, BAND, stride=d)
                    rows = pl.ds(start, BAND, stride=d)
                    prows = pl.ds(pstart, BAND, stride=d)
                    meta.append((qrows, rows, prows))
                    qs.append(q_ref[qrows, :])
                    dos.append(dy_ref[qrows, :])
                    lqs.append(lse_ref[qrows, :])
                    dds.append(dd_s[qrows, :])
                    ks.append(jnp.concatenate([k_ref[prows, :], k_ref[rows, :]], axis=0))
                    vs.append(jnp.concatenate([v_ref[prows, :], v_ref[rows, :]], axis=0))
                    pens.append(jnp.where(cur_half, 0.0, jnp.where(first, NEG_INF, 0.0)))

                def heads(t):
                    return jnp.concatenate([jnp.where(m0, t, 0.0), jnp.where(m0, 0.0, t)], axis=0)

                def head_col(t):
                    return jnp.concatenate([t[:, :, 0:1], t[:, :, HEAD_DIM:HEAD_DIM + 1]], axis=0)

                qh = (heads(jnp.stack(qs)) * 0.125).astype(bf16)
                doh = heads(jnp.stack(dos)).astype(bf16)
                kk = jnp.stack(ks + ks).astype(bf16)
                vv = jnp.stack(vs + vs).astype(bf16)
                sc = lax.dot_general(qh, kk, _BNT, preferred_element_type=f32)
                sc = (sc.reshape(2, ng, BAND, 2 * BAND) + b_ref[bi][:, None]).reshape(sc.shape) + jnp.stack(pens + pens)
                sc = jnp.where(in_window, sc, NEG_INF)
                p = jnp.exp(sc - head_col(jnp.stack(lqs)))
                dp = lax.dot_general(doh, vv, _BNT, preferred_element_type=f32)
                ds = p * (dp - head_col(jnp.stack(dds)))
                db_ref[bi] += jnp.sum(ds.reshape(2, ng, BAND, 2 * BAND), axis=1)
                dsb = ds.astype(bf16)
                dq = lax.dot_general(dsb, kk, _BNN, preferred_element_type=f32) * 0.125
                dk = lax.dot_general(dsb, qh, _BTN, preferred_element_type=f32)
                dv = lax.dot_general(p.astype(bf16), doh, _BTN, preferred_element_type=f32)
                for t in range(ng):
                    qrows, rows, prows = meta[t]
                    dq_ref[qrows, :] += jnp.where(m0, dq[t], dq[ng + t])
                    dkt = dk[t] + dk[ng + t]
                    dvt = dv[t] + dv[ng + t]
                    dk_ref[prows, :] += dkt[:BAND]
                    dk_ref[rows, :] += dkt[BAND:]
                    dv_ref[prows, :] += dvt[:BAND]
                    dv_ref[rows, :] += dvt[BAND:]
                return carry

            lax.fori_loop(0, ATTN_ITEMS // ATTN_GROUP, group, 0)

    chunk = lambda col: pl.BlockSpec((PAD_UNIT, 128), lambda p, c: (c, col + p))
    full = lambda col: pl.BlockSpec((s, 128), lambda p, c: (0, col + p))
    bias_spec = pl.BlockSpec((3, 2, BAND, 2 * BAND), lambda p, c: (0, p, 0, 0))
    return pl.pallas_call(
        body, name="attn_bwd", grid=(N_HEADS // 2, s // PAD_UNIT),
        in_specs=[chunk(Q_COL), full(K_COL), full(V_COL), bias_spec, chunk(0), chunk(0), chunk(0)],
        out_specs=[chunk(0), full(0), full(0), bias_spec],
        out_shape=[jax.ShapeDtypeStruct((s, GROUP_W), f32)] * 3
        + [jax.ShapeDtypeStruct((3, N_HEADS, BAND, 2 * BAND), f32)],
        scratch_shapes=[pltpu.VMEM((PAD_UNIT, 128), f32)],
        compiler_params=_cparams(("parallel", "arbitrary")),
    )(proj, proj, proj, bias, y, lse, dycat)


_HI = lax.Precision.HIGHEST
DELTA_COL = 1536
Z_COL = 3072
BA_BLOCK = 28
DELTA_ROWS = 512


def _hdot(a, b):
    return jnp.dot(a, b, precision=_HI, preferred_element_type=f32)


_DIMS = dict(nn=(((2,), (1,)), ((0,), (0,))), nt=(((2,), (2,)), ((0,), (0,))), tn=(((1,), (1,)), ((0,), (0,))))


@functools.partial(jax.custom_vjp, nondiff_argnums=(2,))
def _mmx(a, b, mode):
    return lax.dot_general(a.astype(bf16), b.astype(bf16), _DIMS[mode], preferred_element_type=f32)


def _mmx_fwd(a, b, mode):
    return _mmx(a, b, mode), (a, b)


def _mmx_bwd(mode, res, g):
    a, b = res
    if mode == "nn":
        return _mmx(g, b, "nt"), _mmx(a, g, "tn")
    if mode == "nt":
        return _mmx(g, b, "nn"), _mmx(g, a, "tn")
    return _mmx(b, g, "nt"), _mmx(a, g, "nn")


_mmx.defvjp(_mmx_fwd, _mmx_bwd)


def _pair_iota():
    row = lax.broadcasted_iota(jnp.int32, (CHUNK, 128), 0)
    lane = lax.broadcasted_iota(jnp.int32, (CHUNK, 128), 1)
    return row, lane, lane & (CHUNK - 1)


def _bd(x):
    _, lane, _ = _pair_iota()
    m0 = lane < CHUNK
    return jnp.concatenate([jnp.where(m0, x, 0.0), jnp.where(m0, 0.0, x)], axis=1)


def _pmm(a, b):
    return _mmx(a, _bd(b), "nn")


def _ntp(x, y):
    return _mmx(x, _bd(y), "nt")


def _tnp(x, y):
    full = _mmx(x, y, "tn")
    _, lane, _ = _pair_iota()
    return jnp.where(lane < CHUNK, full[:, :CHUNK], full[:, CHUNK:])


def _tri_inv(a):
    row, lane, jj = _pair_iota()
    eye = jnp.where(row == jj, 1.0, 0.0).astype(f32)

    def same_block(log2b):
        return (row >> log2b) == (jj >> log2b)

    dgl = jnp.where(same_block(3), a, 0.0)
    d2 = _pmm(dgl, dgl)
    d4 = _pmm(d2, d2)
    t = _pmm(_pmm(eye - dgl, eye + d2), eye + d4)
    for lb in (3, 4, 5):
        off = jnp.where(same_block(lb + 1) & jnp.logical_not(same_block(lb)), a, 0.0)
        t = t - _pmm(_pmm(t, off), t)
    return t


@jax.custom_vjp
def _solve2(a, xv, xk):
    t = _tri_inv(a)
    return _pmm(t, xv), _pmm(t, xk)


def _solve2_fwd(a, xv, xk):
    t = _tri_inv(a)
    u, w = _pmm(t, xv), _pmm(t, xk)
    return (u, w), (t, u, w)


def _solve2_bwd(res, cts):
    t, u, w = res
    du, dw = cts
    dxv = _tnp(t, du)
    dxk = _tnp(t, dw)
    return -(_ntp(dxv, u) + _ntp(dxk, w)), dxv, dxk


_solve2.defvjp(_solve2_fwd, _solve2_bwd)


def _chunk_pre(qp, kp, vp, bp, gcum):
    row, lane, jj = _pair_iota()
    causal = row >= jj
    strict = row > jj
    rsel = jnp.sum(jnp.where(row == jj, gcum, 0.0), axis=1, keepdims=True)
    decay = jnp.where(causal, jnp.exp(jnp.where(causal, gcum - rsel, 0.0)), 0.0)
    kb = kp * bp
    kd = _bd(kp)
    a = jnp.where(strict, _mmx(kb, kd, "nt") * decay, 0.0)
    eg = jnp.exp(gcum)
    u, w = _solve2(a, vp * bp, kb * eg)
    qk = jnp.where(causal, _mmx(qp, kd, "nt") * decay, 0.0)
    glast = jnp.sum(jnp.where(row == CHUNK - 1, gcum, 0.0), axis=1, keepdims=True)
    return u, w, qp * eg, kp * jnp.exp(glast - gcum), qk, jnp.exp(glast)


def _chunk_post(u, w, qt, kh, qk, gam, sp):
    sd = _bd(sp)
    vnew = u - _mmx(w, sd, "nn")
    o = _mmx(qt, sd, "nn") + _pmm(qk, vnew)
    return o, gam * sp + _tnp(kh, vnew)


def _pair_spec(rows=DELTA_ROWS):
    return pl.BlockSpec((rows, 128), lambda i, p: (i, p))


DELTA_NB = DELTA_ROWS // CHUNK


def _chunks(ref):
    return ref[...].reshape(DELTA_NB, CHUNK, 128)


def _pairs(ref, rows):
    return jnp.stack([ref[rows, p * 128:(p + 1) * 128] for p in range(4)], axis=0)


def _delta_chunk_pre(qn, kn, sv, beta, g):
    s = qn.shape[0]

    def body(q_ref, k_ref, v_ref, b_ref, g_ref, u_ref, w_ref, qt_ref, kh_ref, qk_ref, gm_ref):
        outs = _chunk_pre(_chunks(q_ref), _chunks(k_ref), _chunks(v_ref), _chunks(b_ref), _chunks(g_ref))
        for ref, val in zip((u_ref, w_ref, qt_ref, kh_ref, qk_ref), outs[:5]):
            ref[...] = val.reshape(DELTA_ROWS, 128).astype(ref.dtype)
        gm_ref[...] = jnp.broadcast_to(outs[5], (DELTA_NB, 8, 128)).reshape(DELTA_NB * 8, 128)

    v_spec = pl.BlockSpec((DELTA_ROWS, 128), lambda i, p: (i, 8 + p))
    return pl.pallas_call(
        body, name="delta_chunk_pre", grid=(s // DELTA_ROWS, 4),
        in_specs=[_pair_spec(), _pair_spec(), v_spec, _pair_spec(), _pair_spec()],
        out_specs=[_pair_spec()] * 5 + [_pair_spec(DELTA_NB * 8)],
        out_shape=[jax.ShapeDtypeStruct((s, GROUP_W), f32)] + [jax.ShapeDtypeStruct((s, GROUP_W), bf16)] * 4
        + [jax.ShapeDtypeStruct((s // 8, GROUP_W), f32)],
        compiler_params=_cparams(("parallel", "parallel")),
    )(qn, kn, sv, beta, g)


def _delta_scan_fwd(u, w, qt, kh, qk, gm):
    s = u.shape[0]

    def body(u_ref, w_ref, qt_ref, kh_ref, qk_ref, gm_ref, o_ref, ss_ref, st):
        @pl.when(pl.program_id(0) == 0)
        def _():
            st[...] = jnp.zeros_like(st)

        def chunk(ci, carry):
            rows = pl.ds(pl.multiple_of(ci * CHUNK, CHUNK), CHUNK)
            grow = pl.ds(pl.multiple_of(ci * 8, 8), 1)
            sp = st[...]
            o, s2 = _chunk_post(_pairs(u_ref, rows), _pairs(w_ref, rows), _pairs(qt_ref, rows),
                                _pairs(kh_ref, rows), _pairs(qk_ref, rows), _pairs(gm_ref, grow), sp)
            for p in range(4):
                ss_ref[rows, p * 128:(p + 1) * 128] = sp[p]
                o_ref[rows, p * 128:(p + 1) * 128] = o[p]
            st[...] = s2
            return carry

        lax.fori_loop(0, DELTA_NB, chunk, 0)

    spec = pl.BlockSpec((DELTA_ROWS, GROUP_W), lambda i: (i, 0))
    gspec = pl.BlockSpec((DELTA_NB * 8, GROUP_W), lambda i: (i, 0))
    return pl.pallas_call(
        body, name="delta_scan_fwd", grid=(s // DELTA_ROWS,),
        in_specs=[spec] * 5 + [gspec],
        out_specs=[spec, spec],
        out_shape=[jax.ShapeDtypeStruct((s, GROUP_W), f32)] * 2,
        scratch_shapes=[pltpu.VMEM((4, CHUNK, 128), f32)],
        compiler_params=_cparams(("arbitrary",)),
    )(u, w, qt, kh, qk, gm)


def _delta_scan_bwd(w, qt, kh, qk, gm, do):
    s = w.shape[0]
    nb = s // DELTA_ROWS

    def body(w_ref, qt_ref, kh_ref, qk_ref, gm_ref, do_ref, dso_ref, dst):
        @pl.when(pl.program_id(0) == 0)
        def _():
            dst[...] = jnp.zeros_like(dst)

        def chunk(t, carry):
            ci = DELTA_NB - 1 - t
            rows = pl.ds(pl.multiple_of(ci * CHUNK, CHUNK), CHUNK)
            grow = pl.ds(pl.multiple_of(ci * 8, 8), 1)
            ds = dst[...]
            for p in range(4):
                dso_ref[rows, p * 128:(p + 1) * 128] = ds[p]
            do = _pairs(do_ref, rows)
            dvn = _tnp(_pairs(qk_ref, rows), do) + _pmm(_pairs(kh_ref, rows), ds)
            dst[...] = _tnp(_pairs(qt_ref, rows), do) + _pairs(gm_ref, grow) * ds - _tnp(_pairs(w_ref, rows), dvn)
            return carry

        lax.fori_loop(0, DELTA_NB, chunk, 0)

    spec = pl.BlockSpec((DELTA_ROWS, GROUP_W), lambda i: (nb - 1 - i, 0))
    gspec = pl.BlockSpec((DELTA_NB * 8, GROUP_W), lambda i: (nb - 1 - i, 0))
    return pl.pallas_call(
        body, name="delta_scan_bwd", grid=(nb,),
        in_specs=[spec] * 4 + [gspec, spec],
        out_specs=spec,
        out_shape=jax.ShapeDtypeStruct((s, GROUP_W), f32),
        scratch_shapes=[pltpu.VMEM((4, CHUNK, 128), f32)],
        compiler_params=_cparams(("arbitrary",)),
    )(w, qt, kh, qk, gm, do)


def _delta_chunk_bwd(qn, kn, sv, beta, g, ss, dso, do):
    s = qn.shape[0]

    def body(q_ref, k_ref, v_ref, b_ref, g_ref, ss_ref, dso_ref, do_ref, dq_ref, dk_ref, dv_ref, db_ref, dg_ref):
        sp = _chunks(ss_ref)

        def fn(q, k, v, b, gg):
            return _chunk_post(*_chunk_pre(q, k, v, b, gg), sp)

        _, vjp = jax.vjp(fn, _chunks(q_ref), _chunks(k_ref), _chunks(v_ref), _chunks(b_ref), _chunks(g_ref))
        grads = vjp((_chunks(do_ref), _chunks(dso_ref)))
        for ref, val in zip((dq_ref, dk_ref, dv_ref, db_ref, dg_ref), grads):
            ref[...] = val.reshape(DELTA_ROWS, 128)

    v_spec = pl.BlockSpec((DELTA_ROWS, 128), lambda i, p: (i, 8 + p))
    return pl.pallas_call(
        body, name="delta_chunk_bwd", grid=(s // DELTA_ROWS, 4),
        in_specs=[_pair_spec(), _pair_spec(), v_spec] + [_pair_spec()] * 5,
        out_specs=[_pair_spec()] * 5,
        out_shape=[jax.ShapeDtypeStruct((s, GROUP_W), f32)] * 5,
        compiler_params=_cparams(("parallel", "parallel")),
    )(qn, kn, sv, beta, g, ss, dso, do)


def _head_sum_matrix():
    r = lax.broadcasted_iota(jnp.int32, (GROUP_W, GROUP_W), 0)
    c = lax.broadcasted_iota(jnp.int32, (GROUP_W, GROUP_W), 1)
    return jnp.where((r >> 6) == (c >> 6), 1.0, 0.0).astype(f32)


def _head_sums(x):
    return _mmx(x[None], _head_sum_matrix()[None], "nn")[0]


def _sel_dot(a, b):
    return jnp.dot(a, b, precision=lax.Precision.HIGH, preferred_element_type=f32)


def _softplus(x):
    return jnp.maximum(x, 0.0) + jnp.log(1.0 + jnp.exp(-jnp.abs(x)))


def _prep_fn(sq, sk, ba, alog_e, dt_e):
    qn = sq * lax.rsqrt(_head_sums(sq * sq) + EPS) * (HEAD_DIM ** -0.5)
    kn = sk * lax.rsqrt(_head_sums(sk * sk) + EPS)
    r = lax.broadcasted_iota(jnp.int32, (128, GROUP_W), 0)
    c = lax.broadcasted_iota(jnp.int32, (128, GROUP_W), 1) >> 6
    bl = _sel_dot(ba, jnp.where(r == c, 1.0, 0.0).astype(f32))
    al = _sel_dot(ba, jnp.where(r == c + N_HEADS, 1.0, 0.0).astype(f32))
    beta = jax.nn.sigmoid(bl)
    g = -jnp.exp(alog_e) * _softplus(al + dt_e)
    ri = lax.broadcasted_iota(jnp.int32, (TOK_TILE, TOK_TILE), 0)
    ci = lax.broadcasted_iota(jnp.int32, (TOK_TILE, TOK_TILE), 1)
    within = jnp.where(((ri >> 6) == (ci >> 6)) & (ri >= ci), 1.0, 0.0).astype(f32)
    return qn, kn, beta, _sel_dot(within, g)


def _gnorm_fn(o, z, ng_e):
    ms = _head_sums(o * o) * (1.0 / HEAD_DIM)
    return o * lax.rsqrt(ms + EPS) * ng_e * (z * jax.nn.sigmoid(z))


def _tok_spec(width, col):
    return pl.BlockSpec((TOK_TILE, width), lambda i: (i, col))


def _conv_taps(xs_ref, w_ref, base, n):
    acc = w_ref[CONV_WIDTH - 1:CONV_WIDTH, :] * xs_ref[pl.ds(base, n), :]
    for j in range(CONV_WIDTH - 1):
        acc = acc + w_ref[j:j + 1, :] * xs_ref[pl.ds(base - (CONV_WIDTH - 1) + j, n), :]
    return acc


def _conv_silu_fwd(proj, conv_w):
    s = proj.shape[0]
    wd = 3 * GROUP_W
    hb = TOK_TILE // 8

    def body(x_ref, halo_ref, w_ref, o_ref, xs):
        xs[0:8, :] = jnp.where(pl.program_id(0) > 0, halo_ref[...], 0.0)
        xs[8:, :] = x_ref[...]
        y = _conv_taps(xs, w_ref, 8, TOK_TILE)
        o_ref[...] = y * jax.nn.sigmoid(y)

    return pl.pallas_call(
        body, name="delta_conv_fwd", grid=(s // TOK_TILE,),
        in_specs=[_tok_spec(wd, 1), pl.BlockSpec((8, wd), lambda i: (jnp.maximum(i * hb - 1, 0), 1)),
                  pl.BlockSpec((CONV_WIDTH, wd), lambda i: (0, 0))],
        out_specs=_tok_spec(wd, 0),
        out_shape=jax.ShapeDtypeStruct((s, wd), f32),
        scratch_shapes=[pltpu.VMEM((TOK_TILE + 8, wd), f32)],
        compiler_params=_cparams(("parallel",)),
    )(proj, proj, conv_w)


def _conv_silu_bwd(proj, conv_w, ds, xchg):
    s = proj.shape[0]
    wd = 3 * GROUP_W
    hb = TOK_TILE // 8
    nt = s // TOK_TILE

    def body(x_ref, hp_ref, hn_ref, ds_ref, dsn_ref, w_ref, dx_ref, dw_ref, xs, dys):
        i = pl.program_id(0)

        @pl.when(i == 0)
        def _():
            dw_ref[...] = jnp.zeros_like(dw_ref)

        last = i == nt - 1
        xs[0:8, :] = jnp.where(i > 0, hp_ref[...], 0.0)
        xs[8:8 + TOK_TILE, :] = x_ref[...]
        xs[8 + TOK_TILE:, :] = jnp.where(last, 0.0, hn_ref[...])
        y = _conv_taps(xs, w_ref, 8, TOK_TILE)
        sg = jax.nn.sigmoid(y)
        dys[0:TOK_TILE, :] = ds_ref[...] * sg * (1.0 + y * (1.0 - sg))
        yn = _conv_taps(xs, w_ref, 8 + TOK_TILE, 8)
        sgn = jax.nn.sigmoid(yn)
        dys[TOK_TILE:, :] = jnp.where(last, 0.0, dsn_ref[...]) * sgn * (1.0 + yn * (1.0 - sgn))
        dy0 = dys[0:TOK_TILE, :]
        dx = w_ref[CONV_WIDTH - 1:CONV_WIDTH, :] * dy0
        for j in range(CONV_WIDTH - 1):
            dx = dx + w_ref[j:j + 1, :] * dys[pl.ds(CONV_WIDTH - 1 - j, TOK_TILE), :]
        dx_ref[...] = dx
        for j in range(CONV_WIDTH):
            dw_ref[j:j + 1, :] += jnp.sum(dy0 * xs[pl.ds(8 - (CONV_WIDTH - 1) + j, TOK_TILE), :],
                                          axis=0, keepdims=True)

    prev8 = lambda col: pl.BlockSpec((8, wd), lambda i: (jnp.maximum(i * hb - 1, 0), col))
    next8 = lambda col: pl.BlockSpec((8, wd), lambda i: (jnp.minimum((i + 1) * hb, s // 8 - 1), col))
    out = pl.pallas_call(
        _ride(body, 6, 2, xchg, (nt,)), name="delta_conv_bwd", grid=(nt,),
        in_specs=[_tok_spec(wd, 1), prev8(1), next8(1), _tok_spec(wd, 0), next8(0),
                  pl.BlockSpec((CONV_WIDTH, wd), lambda i: (0, 0))] + [_ANY] * xchg.n,
        out_specs=[_tok_spec(wd, 0), pl.BlockSpec((CONV_WIDTH, wd), lambda i: (0, 0))] + [_ANY] * xchg.n,
        out_shape=[jax.ShapeDtypeStruct((s, wd), f32), jax.ShapeDtypeStruct((CONV_WIDTH, wd), f32)] + xchg.out_shape(),
        scratch_shapes=[pltpu.VMEM((TOK_TILE + 16, wd), f32), pltpu.VMEM((TOK_TILE + 8, wd), f32)] + xchg.scratch(),
        compiler_params=_cparams(("arbitrary",)),
    )(proj, proj, proj, ds, ds, conv_w, *xchg.arrs)
    return out[:2], out[2:]


def _delta_prep_fwd(sconv, proj, alog_e, dt_e):
    s = sconv.shape[0]

    def body(sq_ref, sk_ref, ba_ref, al_ref, dt_ref, q_ref, k_ref, b_ref, g_ref):
        qn, kn, beta, g = _prep_fn(sq_ref[...], sk_ref[...], ba_ref[...], al_ref[...], dt_ref[...])
        q_ref[...] = qn
        k_ref[...] = kn
        b_ref[...] = beta
        g_ref[...] = g

    return pl.pallas_call(
        body, name="delta_prep_fwd", grid=(s // TOK_TILE,),
        in_specs=[_tok_spec(GROUP_W, 0), _tok_spec(GROUP_W, 1), _tok_spec(128, BA_BLOCK),
                  _vec_spec(GROUP_W), _vec_spec(GROUP_W)],
        out_specs=[_tok_spec(GROUP_W, 0)] * 4,
        out_shape=[jax.ShapeDtypeStruct((s, GROUP_W), f32)] * 4,
        compiler_params=_cparams(("parallel",)),
    )(sconv, sconv, proj, alog_e, dt_e)


def _delta_prep_bwd(sconv, proj, alog_e, dt_e, dqn, dkn, dbeta, dg):
    s = sconv.shape[0]

    def body(sq_ref, sk_ref, ba_ref, al_ref, dt_ref, dq_ref, dk_ref, db_ref, dg_ref,
             dsq_ref, dsk_ref, dba_ref, dal_ref, ddt_ref):
        @pl.when(pl.program_id(0) == 0)
        def _():
            dal_ref[...] = jnp.zeros_like(dal_ref)
            ddt_ref[...] = jnp.zeros_like(ddt_ref)

        _, vjp = jax.vjp(_prep_fn, sq_ref[...], sk_ref[...], ba_ref[...], al_ref[...], dt_ref[...])
        dsq, dsk, dba, dal, ddt = vjp((dq_ref[...], dk_ref[...], db_ref[...], dg_ref[...]))
        dsq_ref[...] = dsq
        dsk_ref[...] = dsk
        dba_ref[...] = dba
        dal_ref[...] += dal
        ddt_ref[...] += ddt

    return pl.pallas_call(
        body, name="delta_prep_bwd", grid=(s // TOK_TILE,),
        in_specs=[_tok_spec(GROUP_W, 0), _tok_spec(GROUP_W, 1), _tok_spec(128, BA_BLOCK),
                  _vec_spec(GROUP_W), _vec_spec(GROUP_W)] + [_tok_spec(GROUP_W, 0)] * 4,
        out_specs=[_tok_spec(GROUP_W, 0), _tok_spec(GROUP_W, 0), _tok_spec(128, 0),
                   _acc_spec(GROUP_W), _acc_spec(GROUP_W)],
        out_shape=[jax.ShapeDtypeStruct((s, GROUP_W), f32)] * 2 + [jax.ShapeDtypeStruct((s, 128), f32)]
        + [jax.ShapeDtypeStruct((1, GROUP_W), f32)] * 2,
        compiler_params=_cparams(("arbitrary",)),
    )(sconv, sconv, proj, alog_e, dt_e, dqn, dkn, dbeta, dg)


def _gnorm_fwd(o, proj, ng_e):
    s = o.shape[0]

    def body(o_ref, z_ref, g_ref, y_ref):
        y_ref[...] = _gnorm_fn(o_ref[...], z_ref[...], g_ref[...])

    return pl.pallas_call(
        body, name="delta_gnorm_fwd", grid=(s // TOK_TILE,),
        in_specs=[_tok_spec(GROUP_W, 0), _tok_spec(GROUP_W, Z_COL // GROUP_W), _vec_spec(GROUP_W)],
        out_specs=_tok_spec(GROUP_W, 0),
        out_shape=jax.ShapeDtypeStruct((s, GROUP_W), f32),
        compiler_params=_cparams(("parallel",)),
    )(o, proj, ng_e)


def _gnorm_bwd(o, proj, ng_e, dycat):
    s = o.shape[0]

    def body(o_ref, z_ref, g_ref, dy_ref, do_ref, dz_ref, dg_ref):
        @pl.when(pl.program_id(0) == 0)
        def _():
            dg_ref[...] = jnp.zeros_like(dg_ref)

        _, vjp = jax.vjp(_gnorm_fn, o_ref[...], z_ref[...], g_ref[...])
        do, dz, dg = vjp(dy_ref[...])
        do_ref[...] = do
        dz_ref[...] = dz
        dg_ref[...] += dg

    return pl.pallas_call(
        body, name="delta_gnorm_bwd", grid=(s // TOK_TILE,),
        in_specs=[_tok_spec(GROUP_W, 0), _tok_spec(GROUP_W, Z_COL // GROUP_W), _vec_spec(GROUP_W),
                  _tok_spec(GROUP_W, 1)],
        out_specs=[_tok_spec(GROUP_W, 0), _tok_spec(GROUP_W, 0), _acc_spec(GROUP_W)],
        out_shape=[jax.ShapeDtypeStruct((s, GROUP_W), f32)] * 2 + [jax.ShapeDtypeStruct((1, GROUP_W), f32)],
        compiler_params=_cparams(("arbitrary",)),
    )(o, proj, ng_e, dycat)


_MESH = pl.DeviceIdType.MESH
_ANY = pl.BlockSpec(memory_space=pl.ANY)
_VMEM = pl.BlockSpec(memory_space=pltpu.VMEM)


def _my_place():
    x, y, c = lax.axis_index("x"), lax.axis_index("y"), lax.axis_index("c")
    return x, y, c, 4 * x + 2 * y + c


def _peer(k, x, y, c):
    px = 1 - x if k & 4 else x
    py = 1 - y if k & 2 else y
    pc = 1 - c if k & 1 else c
    return (px, py, pc), 4 * px + 2 * py + pc


def _exchange_all(src_of_peer, dst_ref, send_sems, recv_sems, x, y, c, me):
    sent = []
    for k in range(1, N_DEV):
        dev, pidx = _peer(k, x, y, c)
        cp = pltpu.make_async_remote_copy(src_ref=src_of_peer(pidx), dst_ref=dst_ref.at[me],
                                          send_sem=send_sems.at[k - 1], recv_sem=recv_sems.at[k - 1],
                                          device_id=dev, device_id_type=_MESH)
        cp.start()
        sent.append(cp)
    for k in range(1, N_DEV):
        dev, pidx = _peer(k, x, y, c)
        pltpu.make_async_remote_copy(src_ref=src_of_peer(pidx), dst_ref=dst_ref.at[pidx],
                                     send_sem=send_sems.at[k - 1], recv_sem=recv_sems.at[k - 1],
                                     device_id=dev, device_id_type=_MESH).wait_recv()
    for cp in sent:
        cp.wait_send()


def _ada_exchange(cv8, w_ada, b_ada8, w_in_sh):
    ride = _ChipGather(w_in_sh)

    def body(cv_ref, w_ref, b_ref, wi_ref, call_ref, modp_ref, wig_ref, part_s, s1, r1, s2, r2, *ride_sems):
        ride.start(wi_ref, wig_ref, ride_sems)
        x, y, c, me = _my_place()
        call_ref[me] = cv_ref[...]
        _exchange_all(lambda pidx: cv_ref, call_ref, s1, r1, x, y, c, me)
        bias = b_ref[me]
        for j in range(N_DEV):
            cj = call_ref[j][:, :D_MODEL]
            part_s[j] = _hdot(cj * jax.nn.sigmoid(cj), w_ref[...]) + bias
        modp_ref[me] = part_s[me]
        ride.forward(wi_ref, wig_ref, ride_sems)
        _exchange_all(lambda pidx: part_s.at[pidx], modp_ref, s2, r2, x, y, c, me)
        ride.finish(wi_ref, wig_ref, ride_sems)

    nsh = w_ada.shape[1]
    return pl.pallas_call(
        body, name="ada_exchange",
        in_specs=[_VMEM, _VMEM, _VMEM, _ANY], out_specs=[_VMEM, _VMEM, _ANY],
        out_shape=[jax.ShapeDtypeStruct((N_DEV, 8, cv8.shape[1]), f32), jax.ShapeDtypeStruct((N_DEV, 8, nsh), f32)]
        + [ride.out_shape()],
        scratch_shapes=[pltpu.VMEM((N_DEV, 8, nsh), f32)] + [pltpu.SemaphoreType.DMA((N_DEV - 1,))] * 4
        + ride.scratch(),
        compiler_params=pltpu.CompilerParams(vmem_limit_bytes=VMEM_LIMIT),
    )(cv8, w_ada, b_ada8, w_in_sh)


def _all_to_all(arrs, name):
    ex = _Exchange(arrs, gather=False)

    def body(*refs):
        srcs, dsts, sems = refs[:ex.n], refs[ex.n:2 * ex.n], refs[2 * ex.n:]
        ex.start(srcs, dsts, sems)
        ex.wait(srcs, dsts, sems)

    return pl.pallas_call(
        body, name=name,
        in_specs=[_ANY] * ex.n, out_specs=[_ANY] * ex.n,
        out_shape=ex.out_shape(), scratch_shapes=ex.scratch(),
    )(*arrs)


class _Exchange:
    def __init__(self, arrs, gather):
        self.arrs, self.gather, self.n = list(arrs), gather, len(arrs)

    def out_shape(self):
        return [jax.ShapeDtypeStruct(((N_DEV,) + a.shape) if self.gather else a.shape, a.dtype) for a in self.arrs]

    def scratch(self):
        if self.n == 0:
            return []
        return [pltpu.SemaphoreType.DMA((self.n, N_DEV - 1)), pltpu.SemaphoreType.DMA((self.n, N_DEV - 1)),
                pltpu.SemaphoreType.DMA((self.n,))]

    def _src(self, srcs, a, idx):
        return srcs[a] if self.gather else srcs[a].at[idx]

    def _copies(self, srcs, dsts, sems, incoming):
        send_sems, recv_sems, _ = sems
        x, y, c, me = _my_place()
        out = []
        for a in range(self.n):
            for k in range(1, N_DEV):
                dev, pidx = _peer(k, x, y, c)
                out.append(pltpu.make_async_remote_copy(
                    src_ref=self._src(srcs, a, pidx), dst_ref=dsts[a].at[pidx if incoming else me],
                    send_sem=send_sems.at[a, k - 1], recv_sem=recv_sems.at[a, k - 1],
                    device_id=dev, device_id_type=_MESH))
        return out

    def _local(self, srcs, dsts, sems):
        me = _my_place()[3]
        return [pltpu.make_async_copy(self._src(srcs, a, me), dsts[a].at[me], sems[2].at[a]) for a in range(self.n)]

    def start(self, srcs, dsts, sems):
        for cp in self._local(srcs, dsts, sems) + self._copies(srcs, dsts, sems, incoming=False):
            cp.start()

    def wait(self, srcs, dsts, sems):
        for cp in self._copies(srcs, dsts, sems, incoming=True):
            cp.wait_recv()
        for cp in self._copies(srcs, dsts, sems, incoming=False):
            cp.wait_send()
        for cp in self._local(srcs, dsts, sems):
            cp.wait()

    def start_at_first_step(self, grid, srcs, dsts, sems):
        first = functools.reduce(jnp.logical_and, [pl.program_id(i) == 0 for i in range(len(grid))])
        pl.when(first)(lambda: self.start(srcs, dsts, sems))

    def wait_at_last_step(self, grid, srcs, dsts, sems):
        last = functools.reduce(jnp.logical_and, [pl.program_id(i) == g - 1 for i, g in enumerate(grid)])
        pl.when(last)(lambda: self.wait(srcs, dsts, sems))


class _ChipGather:
    def __init__(self, shard):
        self.shard = shard

    def out_shape(self):
        return jax.ShapeDtypeStruct((N_DEV,) + self.shard.shape, self.shard.dtype)

    def scratch(self):
        return [pltpu.SemaphoreType.DMA((N_DEV - 1,)), pltpu.SemaphoreType.DMA((N_DEV - 1,)),
                pltpu.SemaphoreType.DMA(())]

    def _place(self):
        x, y, c, me = _my_place()
        return x, y, c, me, (x, y, 1 - c), [(1 - x, y), (x, 1 - y), (1 - x, 1 - y)]

    def _copy(self, out, sems, k, block, to, src=None):
        rows = out.at[4 * block[0] + 2 * block[1] + block[2]]
        return pltpu.make_async_remote_copy(src_ref=rows if src is None else src, dst_ref=rows,
                                            send_sem=sems[0].at[k], recv_sem=sems[1].at[k],
                                            device_id=to, device_id_type=_MESH)

    def start(self, src, out, sems):
        x, y, c, me, sib, chips = self._place()
        pltpu.make_async_copy(src, out.at[me], sems[2]).start()
        self._copy(out, sems, 0, (x, y, c), sib, src=src).start()
        for j, chip in enumerate(chips):
            self._copy(out, sems, 1 + j, (x, y, c), (*chip, c), src=src).start()

    def forward(self, src, out, sems):
        x, y, c, me, sib, chips = self._place()
        for j, chip in enumerate(chips):
            self._copy(out, sems, 1 + j, (*chip, c), (x, y, c)).wait_recv()
            self._copy(out, sems, 4 + j, (*chip, c), sib).start()

    def finish(self, src, out, sems):
        x, y, c, me, sib, chips = self._place()
        self._copy(out, sems, 0, (x, y, 1 - c), (x, y, c)).wait_recv()
        for j, chip in enumerate(chips):
            self._copy(out, sems, 4 + j, (*chip, 1 - c), (x, y, c)).wait_recv()
        self._copy(out, sems, 0, (x, y, c), sib, src=src).wait_send()
        for j, chip in enumerate(chips):
            self._copy(out, sems, 1 + j, (x, y, c), (*chip, c), src=src).wait_send()
            self._copy(out, sems, 4 + j, (*chip, c), sib).wait_send()
        pltpu.make_async_copy(src, out.at[me], sems[2]).wait()


def _ride(body, n_in, n_out, xchg, grid):
    nx = xchg.n
    if nx == 0:
        return body

    def wrapped(*refs):
        ins, xs = refs[:n_in], refs[n_in:n_in + nx]
        outs, xd = refs[n_in + nx:n_in + nx + n_out], refs[n_in + nx + n_out:n_in + 2 * nx + n_out]
        scratch = refs[n_in + 2 * nx + n_out:]
        xchg.start_at_first_step(grid, xs, xd, scratch[-3:])
        body(*ins, *outs, *scratch[:-3])
        xchg.wait_at_last_step(grid, xs, xd, scratch[-3:])

    return wrapped


def _adamw_math(w, g, m, v):
    m2 = ADAM_B1 * m + (1.0 - ADAM_B1) * g
    v2 = ADAM_B2 * v + (1.0 - ADAM_B2) * (g * g)
    m_hat = m2 / (1.0 - ADAM_B1 ** ADAM_STEP)
    v_hat = v2 / (1.0 - ADAM_B2 ** ADAM_STEP)
    delta = -ADAM_LR * (m_hat / (jnp.sqrt(v_hat) + ADAM_EPS) + ADAM_WD * w)
    return delta, m2, v2


def _row_tile(rows):
    for t in (256, 128, 64, 32, 16, 8):
        if rows % t == 0:
            return t
    return rows


def _reduce_adamw(parts, w, m, v, name):
    _, r, cdim = parts.shape
    tr = _row_tile(r)

    def body(p_ref, w_ref, m_ref, v_ref, g_ref, d_ref, m2_ref, v2_ref):
        g = p_ref[0].astype(f32)
        for j in range(1, N_DEV):
            g = g + p_ref[j].astype(f32)
        delta, m2, v2 = _adamw_math(w_ref[...], g, m_ref[...], v_ref[...])
        g_ref[...] = g
        d_ref[...] = delta
        m2_ref[...] = m2
        v2_ref[...] = v2

    spec = pl.BlockSpec((tr, cdim), lambda i: (i, 0))
    return pl.pallas_call(
        body, name=name, grid=(r // tr,),
        in_specs=[pl.BlockSpec((N_DEV, tr, cdim), lambda i: (0, i, 0)), spec, spec, spec],
        out_specs=[spec] * 4,
        out_shape=[jax.ShapeDtypeStruct((r, cdim), f32)] * 4,
        compiler_params=_cparams(("parallel",)),
    )(parts, w, m, v)


def _adamw(w, g, m, v, name):
    r, cdim = w.shape
    tr = _row_tile(r)

    def body(w_ref, g_ref, m_ref, v_ref, d_ref, m2_ref, v2_ref):
        delta, m2, v2 = _adamw_math(w_ref[...], g_ref[...], m_ref[...], v_ref[...])
        d_ref[...] = delta
        m2_ref[...] = m2
        v2_ref[...] = v2

    spec = pl.BlockSpec((tr, cdim), lambda i: (i, 0))
    return pl.pallas_call(
        body, name=name, grid=(r // tr,),
        in_specs=[spec] * 4, out_specs=[spec] * 3,
        out_shape=[jax.ShapeDtypeStruct((r, cdim), f32)] * 3,
        compiler_params=_cparams(("parallel",)),
    )(w, g, m, v)


def _sum_devices(parts, name):
    _, r, cdim = parts.shape

    def body(p_ref, o_ref):
        g = p_ref[0]
        for j in range(1, N_DEV):
            g = g + p_ref[j]
        o_ref[...] = g

    return pl.pallas_call(
        body, name=name, out_shape=jax.ShapeDtypeStruct((r, cdim), f32),
        in_specs=[_VMEM], out_specs=_VMEM,
    )(parts)


def _ada_wgrad(c_all8, dmod_cols):
    nsh = dmod_cols.shape[1]

    def body(c_ref, d_ref, o_ref):
        cv = c_ref[...]
        o_ref[...] = lax.dot_general(cv * jax.nn.sigmoid(cv), d_ref[...], _TN, precision=_HI,
                                     preferred_element_type=f32)

    return pl.pallas_call(
        body, name="ada_wgrad", out_shape=jax.ShapeDtypeStruct((D_MODEL, nsh), f32),
        in_specs=[_VMEM, _VMEM], out_specs=_VMEM,
        compiler_params=pltpu.CompilerParams(vmem_limit_bytes=VMEM_LIMIT),
    )(c_all8, dmod_cols)


def _cols(t):
    return t.transpose(1, 0, 2).reshape(t.shape[1], N_DEV * t.shape[2])


def _col_blocks(t, n):
    return t.reshape(t.shape[0], N_DEV, n).transpose(1, 0, 2).astype(bf16)


def _row_blocks(t):
    return t.reshape(N_DEV, t.shape[0] // N_DEV, t.shape[1]).astype(bf16)


def _local_step(x, tgt, mod, norm_attn_g, w_in_p, rel_bias, conv_full, a_log, dt_bias, delta_norm_g,
                norm_ffn_g, final_norm_g, w_out_sh, w_gate_sh, w_up_sh, w_down_sh):
    s = x.shape[0]
    sh1, sc1, g1, sh2, sc2, g2 = [mod[:, i * D_MODEL:(i + 1) * D_MODEL] for i in range(6)]
    nag = norm_attn_g.reshape(1, D_MODEL)
    nfg = norm_ffn_g.reshape(1, D_MODEL)
    fg = final_norm_g.reshape(1, D_MODEL)
    idx = _bucket_tables()
    bias = _bias_tables(rel_bias, idx)
    alog_e = jnp.repeat(a_log.reshape(N_HEADS), HEAD_DIM)[None]
    dt_e = jnp.repeat(dt_bias.reshape(N_HEADS), HEAD_DIM)[None]
    ng_e = jnp.tile(delta_norm_g.reshape(HEAD_DIM), N_HEADS)[None]

    h1 = _ln_mod_fwd(x, nag, sc1, sh1, "ln1_fwd")
    proj, (w_out_g, w_gate_g) = _mm(h1, w_in_p, "nn", f32, 512, 1280, 1024, "in_proj",
                                    xchg=_Exchange([w_out_sh, w_gate_sh], gather=True))
    (y_attn, lse), (w_up_g, w_down_g) = _attn_fwd(proj, bias, _Exchange([w_up_sh, w_down_sh], gather=True))
    w_out_b = w_out_g.reshape(2 * GROUP_W, D_MODEL)
    w_gate_b, w_up_b = _cols(w_gate_g), _cols(w_up_g)
    w_down_b = w_down_g.reshape(D_FF, D_MODEL)
    n_ff = w_gate_sh.shape[1]
    sconv = _conv_silu_fwd(proj, conv_full)
    qn, kn, beta, g = _delta_prep_fwd(sconv, proj, alog_e, dt_e)
    u, w, qt, kh, qk, gm = _delta_chunk_pre(qn, kn, sconv, beta, g)
    o, ss = _delta_scan_fwd(u, w, qt, kh, qk, gm)
    y_delta = _gnorm_fwd(o, proj, ng_e)
    ycat = jnp.concatenate([y_attn, y_delta], axis=1).astype(bf16)
    y = _mm(ycat, w_out_b, "nn", f32, 512, 1024, 1024, "out_proj")
    x1, h2 = _resid_ln_mod_fwd(x, y, g1, nfg, sc2, sh2, "ln2_fwd")
    act, gate, up = _ffn_up(h2, w_gate_b, w_up_b, "ffn_up")
    y2 = _mm(act, w_down_b, "nn", f32, 512, 1024, D_FF, "ffn_down")
    dx2, dy2, loss, dfg, dg2 = _final_loss_bwd(x1, y2, g2, fg, tgt, "final_loss")

    dgate, dup = _ffn_down_dx(dy2, w_down_b, gate, up, "ffn_down_dx")
    g_down = _mm(act, dy2, "tn", f32, 1408, 1024, 1024, "ffn_down_dw")
    dh2, (r_down,) = _mm_nt2(dgate, w_gate_b, dup, w_up_b, 512, 1024, "ffn_up_dx",
                             _Exchange([_row_blocks(g_down)], gather=False))
    g_gate = _mm(h2, dgate, "tn", f32, 1024, 1408, 1024, "ffn_gate_dw")
    g_up = _mm(h2, dup, "tn", f32, 1024, 1408, 1024, "ffn_up_dw")
    dx1, dsh2, dsc2, dnfg, dy, dg1 = _ln_mod_bwd(x1, nfg, sc2, dh2, dx2, "ln2_bwd", gate=g1, y=y)
    dycat = _mm(dy, w_out_b, "nt", f32, 512, 1024, 1024, "out_proj_dx")
    g_out = _mm(ycat, dy, "tn", f32, 1024, 1024, 1024, "out_proj_dw")
    dq, dk, dv, dbias = _attn_bwd(proj, bias, y_attn, lse, dycat)
    g_rb = _bias_grad(dbias, idx)[:, :, 0].T
    do, dz, dng = _gnorm_bwd(o, proj, ng_e, dycat)
    dso = _delta_scan_bwd(w, qt, kh, qk, gm, do)
    dqn, dkn, dvd, dbeta, dgd = _delta_chunk_bwd(qn, kn, sconv, beta, g, ss, dso, do)
    dsq, dsk, dba, dal, ddt = _delta_prep_bwd(sconv, proj, alog_e, dt_e, dqn, dkn, dbeta, dgd)
    (dxc, g_conv), (r_gate, r_up, r_out) = _conv_silu_bwd(
        proj, conv_full, jnp.concatenate([dsq, dsk, dvd], axis=1),
        _Exchange([_col_blocks(g_gate, n_ff), _col_blocks(g_up, n_ff), _row_blocks(g_out)],
                  gather=False))
    dproj = jnp.concatenate([dq, dk, dv, dxc, dz, dba, jnp.zeros((s, IN_PAD - BA_BLOCK * 128 - 128), f32)],
                            axis=1).astype(bf16)
    g_in = _mm(h1, dproj, "tn", f32, 1024, 1280, 1024, "in_proj_dw")
    dh1, (r_in,) = _mm(dproj, w_in_p, "nt", f32, 512, 1024, IN_PAD, "in_proj_dx",
                       xchg=_Exchange([_col_blocks(g_in[:, :IN_WIDTH], IN_WIDTH // N_DEV)], gather=False))
    gx, dsh1, dsc1, dnag = _ln_mod_bwd(x, nag, sc1, dh1, dx1, "ln1_bwd")
    grads = dict(
        x=gx, mod=jnp.concatenate([dsh1, dsc1, dg1, dsh2, dsc2, dg2], axis=1),
        norm_attn_g=dnag, norm_ffn_g=dnfg, final_norm_g=dfg, rel_bias=g_rb, conv_w=g_conv,
        a_log=dal.reshape(N_HEADS, HEAD_DIM).sum(-1), dt_bias=ddt.reshape(N_HEADS, HEAD_DIM).sum(-1),
        delta_norm_g=dng.reshape(N_HEADS, HEAD_DIM).sum(0),
        w_in=r_in, w_out=r_out, w_gate=r_gate, w_up=r_up, w_down=r_down)
    return loss[0, 0], grads


MISC_OFF = dict(rel_bias=0, a_log=256, dt_bias=264, delta_norm_g=272)


def _misc_row(rel_bias, a_log, dt_bias, delta_norm_g):
    flat = jnp.concatenate([rel_bias.reshape(-1), a_log.reshape(-1), dt_bias.reshape(-1), delta_norm_g.reshape(-1)])
    return jnp.pad(flat, (0, D_MODEL - flat.shape[0]))[None]


def _pack_small(b_ada, nag, nfg, fng, rel_bias, a_log, dt_bias, dng, conv_shard):
    rows = [b_ada.reshape(6, D_MODEL), nag.reshape(1, D_MODEL), nfg.reshape(1, D_MODEL), fng.reshape(1, D_MODEL),
            _misc_row(rel_bias, a_log, dt_bias, dng),
            jnp.pad(conv_shard.reshape(-1), (0, D_MODEL - conv_shard.size))[None],
            jnp.zeros((5, D_MODEL), f32)]
    return jnp.concatenate(rows, axis=0)


def _unpack_small(p, conv_shape):
    misc = p[9]
    return dict(
        b_ada=p[0:6].reshape(1, 6 * D_MODEL), norm_attn_g=p[6:7], norm_ffn_g=p[7:8], final_norm_g=p[8],
        rel_bias=misc[0:256].reshape(N_BUCKETS, N_HEADS), a_log=misc[256:264].reshape(1, N_HEADS),
        dt_bias=misc[264:272].reshape(1, N_HEADS), delta_norm_g=misc[272:336].reshape(1, HEAD_DIM),
        conv_w=p[10, :conv_shape[1] * conv_shape[2]].reshape(conv_shape))


def kernel(x, c, w_ada, b_ada, norm_attn_g, w_in, rel_bias, conv_w, a_log, dt_bias, delta_norm_g, w_out, norm_ffn_g, w_gate, w_up, w_down, final_norm_g, loss_target, m_w_ada, m_b_ada, m_norm_attn_g, m_w_in, m_rel_bias, m_conv_w, m_a_log, m_dt_bias, m_delta_norm_g, m_w_out, m_norm_ffn_g, m_w_gate, m_w_up, m_w_down, m_final_norm_g, v_w_ada, v_b_ada, v_norm_attn_g, v_w_in, v_rel_bias, v_conv_w, v_a_log, v_dt_bias, v_delta_norm_g, v_w_out, v_norm_ffn_g, v_w_gate, v_w_up, v_w_down, v_final_norm_g):
    me = 4 * lax.axis_index("x") + 2 * lax.axis_index("y") + lax.axis_index("c")
    ada_sh = w_ada.shape[2]
    conv_sh = conv_w.shape[2]

    cv = jnp.concatenate([c[0], conv_w[0].reshape(-1)])
    cv8 = jnp.zeros((8, 2 * D_MODEL), f32).at[0, :cv.shape[0]].set(cv)
    b8 = jnp.broadcast_to(b_ada.reshape(N_DEV, 1, ada_sh), (N_DEV, 8, ada_sh))
    call, modp, w_in_g = _ada_exchange(cv8, w_ada[0], b8, w_in[0].astype(bf16))
    mod = modp[:, 0, :].reshape(1, 6 * D_MODEL)
    c_all = call[:, 0, :D_MODEL]
    conv_full = call[:, 0, D_MODEL:D_MODEL + CONV_WIDTH * conv_sh].reshape(N_DEV, CONV_WIDTH, conv_sh)
    conv_full = conv_full.transpose(1, 0, 2).reshape(CONV_WIDTH, N_DEV * conv_sh)

    w_in_p = jnp.pad(_cols(w_in_g), ((0, 0), (0, IN_PAD - IN_WIDTH)))
    loss_local, gr = _local_step(x[0], loss_target[0], mod, norm_attn_g, w_in_p, rel_bias, conv_full, a_log,
                                 dt_bias, delta_norm_g, norm_ffn_g, final_norm_g, w_out[0].astype(bf16),
                                 w_gate[0].astype(bf16), w_up[0].astype(bf16), w_down[0].astype(bf16))
    loss = lax.psum(loss_local, ("x", "y", "c"))

    small = jnp.concatenate([
        gr["mod"].reshape(6, D_MODEL), gr["norm_attn_g"], gr["norm_ffn_g"], gr["final_norm_g"],
        gr["conv_w"].reshape(6, D_MODEL),
        _misc_row(gr["rel_bias"], gr["a_log"], gr["dt_bias"], gr["delta_norm_g"])], axis=0)
    parts = _all_to_all([jnp.broadcast_to(small[None], (N_DEV,) + small.shape)], "small_gather")[0]
    tot = _sum_devices(parts, "small_sum")
    g_conv_full = tot[9:15].reshape(CONV_WIDTH, N_DEV * conv_sh)
    g_conv = lax.dynamic_slice(g_conv_full, (0, me * conv_sh), (CONV_WIDTH, conv_sh))
    misc = tot[15]
    g_small = _pack_small(tot[0:6], tot[6], tot[7], tot[8], misc[0:256], misc[256:264], misc[264:272],
                          misc[272:336], g_conv)
    pk = lambda pre: _pack_small(pre[0], pre[1], pre[2], pre[3], pre[4], pre[5], pre[6], pre[7], pre[8])
    w_small = pk((b_ada, norm_attn_g, norm_ffn_g, final_norm_g, rel_bias, a_log, dt_bias, delta_norm_g, conv_w))
    m_small = pk((m_b_ada, m_norm_attn_g, m_norm_ffn_g, m_final_norm_g, m_rel_bias, m_a_log, m_dt_bias,
                  m_delta_norm_g, m_conv_w))
    v_small = pk((v_b_ada, v_norm_attn_g, v_norm_ffn_g, v_final_norm_g, v_rel_bias, v_a_log, v_dt_bias,
                  v_delta_norm_g, v_conv_w))
    d_small, m2_small, v2_small = _adamw(w_small, g_small, m_small, v_small, "adamw_small")
    cshape = conv_w.shape
    G, Dl, M2, V2 = (_unpack_small(t, cshape) for t in (g_small, d_small, m2_small, v2_small))

    dmod_all = parts[:, 0:6, :].reshape(N_DEV, 6 * D_MODEL)
    dmod_cols = lax.dynamic_slice(dmod_all, (0, me * ada_sh), (N_DEV, ada_sh))
    g_ada = _ada_wgrad(c_all, dmod_cols)
    d_ada, m2_ada, v2_ada = _adamw(w_ada[0], g_ada, m_w_ada[0], v_w_ada[0], "adamw_w_ada")

    big = {}
    for name, w_, m_, v_ in (("w_in", w_in, m_w_in, v_w_in), ("w_out", w_out, m_w_out, v_w_out),
                             ("w_gate", w_gate, m_w_gate, v_w_gate), ("w_up", w_up, m_w_up, v_w_up),
                             ("w_down", w_down, m_w_down, v_w_down)):
        big[name] = [t[None] for t in _reduce_adamw(gr[name], w_[0], m_[0], v_[0], "reduce_adamw_" + name)]

    def leaf(i, name):
        if name == "w_ada":
            return (g_ada, d_ada, m2_ada, v2_ada)[i][None]
        if name in big:
            return big[name][i]
        return (G, Dl, M2, V2)[i][name]

    order = ["w_ada", "b_ada", "norm_attn_g", "w_in", "rel_bias", "conv_w", "a_log", "dt_bias", "delta_norm_g",
             "w_out", "norm_ffn_g", "w_gate", "w_up", "w_down", "final_norm_g"]
    outs = [loss, gr["x"][None]]
    for i in range(4):
        outs += [leaf(i, n) for n in order]
    return tuple(outs)
```

```python
import functools
import math

import jax
import jax.numpy as jnp
from jax import lax
from jax.experimental import pallas as pl
from jax.experimental.pallas import tpu as pltpu

f32 = jnp.float32
bf16 = jnp.bfloat16

D_MODEL = 1024
HEAD_DIM = 64
N_HEADS = 8
GROUP_W = 512
IN_WIDTH = 3600
IN_PAD = 3840
D_FF = 2816
EPS = 1e-6
NEG_INF = -1e30
BAND = 128
PAD_UNIT = 2048
DILATIONS = (1, 4, 16)
N_BUCKETS = 32
MAX_DISTANCE = 2048
CONV_WIDTH = 4
CHUNK = 64
N_DEV = 8
VMEM_LIMIT = 56 * 1024 * 1024

ADAM_LR, ADAM_B1, ADAM_B2, ADAM_EPS, ADAM_WD, ADAM_STEP = 0.001, 0.9, 0.999, 1e-08, 0.01, 10


def _cparams(sem):
    return pltpu.CompilerParams(dimension_semantics=sem, vmem_limit_bytes=VMEM_LIMIT)


def _mm(a, b, mode, out_dtype, tm, tn, tk, name, xchg=None):
    if mode == "nn":
        (m, k), (_, n) = a.shape, b.shape
        a_spec = pl.BlockSpec((tm, tk), lambda j, i, kk: (i, kk))
        b_spec = pl.BlockSpec((tk, tn), lambda j, i, kk: (kk, j))
        dims = (((1,), (0,)), ((), ()))
    elif mode == "nt":
        (m, k), (n, _) = a.shape, b.shape
        a_spec = pl.BlockSpec((tm, tk), lambda j, i, kk: (i, kk))
        b_spec = pl.BlockSpec((tn, tk), lambda j, i, kk: (j, kk))
        dims = (((1,), (1,)), ((), ()))
    else:
        (k, m), (_, n) = a.shape, b.shape
        a_spec = pl.BlockSpec((tk, tm), lambda j, i, kk: (kk, i))
        b_spec = pl.BlockSpec((tk, tn), lambda j, i, kk: (kk, j))
        dims = (((0,), (0,)), ((), ()))
    assert m % tm == 0 and n % tn == 0 and k % tk == 0, (name, m, n, k, tm, tn, tk)
    nk = k // tk
    grid = (n // tn, m // tm, nk)
    nx = xchg.n if xchg is not None else 0

    def body(*refs):
        a_ref, b_ref = refs[:2]
        o_ref = refs[2 + nx]
        scratch = refs[3 + 2 * nx:]
        if nx:
            xrefs = (refs[2:2 + nx], refs[3 + nx:3 + 2 * nx], scratch[-3:])
            xchg.start_at_first_step(grid, *xrefs)
        if nk == 1:
            o_ref[...] = lax.dot_general(a_ref[...].astype(bf16), b_ref[...].astype(bf16), dims,
                                         preferred_element_type=f32).astype(o_ref.dtype)
        else:
            acc_ref = scratch[0]
            kk = pl.program_id(2)

            @pl.when(kk == 0)
            def _():
                acc_ref[...] = jnp.zeros_like(acc_ref)

            acc_ref[...] += lax.dot_general(a_ref[...].astype(bf16), b_ref[...].astype(bf16), dims,
                                            preferred_element_type=f32)

            @pl.when(kk == nk - 1)
            def _():
                o_ref[...] = acc_ref[...].astype(o_ref.dtype)
        if nx:
            xchg.wait_at_last_step(grid, *xrefs)

    out = pl.pallas_call(
        body, name=name, grid=grid,
        in_specs=[a_spec, b_spec] + ([_ANY] * nx),
        out_specs=[pl.BlockSpec((tm, tn), lambda j, i, kk: (i, j))] + ([_ANY] * nx),
        out_shape=[jax.ShapeDtypeStruct((m, n), out_dtype)] + (xchg.out_shape() if nx else []),
        scratch_shapes=([pltpu.VMEM((tm, tn), f32)] if nk > 1 else []) + (xchg.scratch() if nx else []),
        compiler_params=_cparams(("arbitrary",) * 3 if nx else ("parallel", "parallel", "arbitrary")),
    )(a, b, *(xchg.arrs if nx else []))
    return (out[0], out[1:]) if nx else out[0]


TOK_TILE = 512


def _row_spec(width, tile=TOK_TILE):
    return pl.BlockSpec((tile, width), lambda i: (i, 0))


def _vec_spec(width, rows=1):
    return pl.BlockSpec((rows, width), lambda i: (0, 0))


def _ln_mod_fwd(x, gain, sc, sh, name):
    s, d = x.shape

    def body(x_ref, g_ref, sc_ref, sh_ref, h_ref):
        xv = x_ref[...]
        rstd = lax.rsqrt(jnp.mean(xv * xv, axis=-1, keepdims=True) + EPS)
        h = (xv * rstd) * g_ref[...] * (1.0 + sc_ref[...]) + sh_ref[...]
        h_ref[...] = h.astype(bf16)

    return pl.pallas_call(
        body, name=name, grid=(s // TOK_TILE,),
        in_specs=[_row_spec(d), _vec_spec(d), _vec_spec(d), _vec_spec(d)],
        out_specs=_row_spec(d),
        out_shape=jax.ShapeDtypeStruct((s, d), bf16),
        compiler_params=_cparams(("parallel",)),
    )(x, gain, sc, sh)


def _proj_resid_ln_mod_fwd(a, w, x, gate, gain, sc, sh, name):
    s, d = x.shape
    k = a.shape[1]

    def body(a_ref, w_ref, x_ref, gt_ref, g_ref, sc_ref, sh_ref, y_ref, x1_ref, h_ref):
        y = jnp.dot(a_ref[...], w_ref[...], preferred_element_type=f32)
        y_ref[...] = y
        x1 = x_ref[...] + gt_ref[...] * y
        x1_ref[...] = x1
        rstd = lax.rsqrt(jnp.mean(x1 * x1, axis=-1, keepdims=True) + EPS)
        h = (x1 * rstd) * g_ref[...] * (1.0 + sc_ref[...]) + sh_ref[...]
        h_ref[...] = h.astype(bf16)

    return pl.pallas_call(
        body, name=name, grid=(s // TOK_TILE,),
        in_specs=[_row_spec(k), pl.BlockSpec((k, d), lambda i: (0, 0)), _row_spec(d)] + [_vec_spec(d)] * 4,
        out_specs=[_row_spec(d)] * 3,
        out_shape=[jax.ShapeDtypeStruct((s, d), f32)] * 2 + [jax.ShapeDtypeStruct((s, d), bf16)],
        compiler_params=_cparams(("parallel",)),
    )(a, w, x, gate, gain, sc, sh)


FFN_TN = 1408


def _ffn_up(h2, w_gate, w_up, name):
    s, d = h2.shape
    tm = TOK_TILE

    def body(h_ref, wg_ref, wu_ref, a_ref, g_ref, u_ref):
        h = h_ref[...]
        g = jnp.dot(h, wg_ref[...], preferred_element_type=f32)
        u = jnp.dot(h, wu_ref[...], preferred_element_type=f32)
        a_ref[...] = (g * jax.nn.sigmoid(g) * u).astype(bf16)
        g_ref[...] = g.astype(bf16)
        u_ref[...] = u.astype(bf16)

    w_spec = pl.BlockSpec((d, FFN_TN), lambda j, i: (0, j))
    o_spec = pl.BlockSpec((tm, FFN_TN), lambda j, i: (i, j))
    return pl.pallas_call(
        body, name=name, grid=(D_FF // FFN_TN, s // tm),
        in_specs=[pl.BlockSpec((tm, d), lambda j, i: (i, 0)), w_spec, w_spec],
        out_specs=[o_spec] * 3,
        out_shape=[jax.ShapeDtypeStruct((s, D_FF), bf16)] * 3,
        compiler_params=_cparams(("parallel", "parallel")),
    )(h2, w_gate, w_up)


def _ffn_down_dx(dy2, w_down, gate, up, name):
    s, d = dy2.shape
    tm = TOK_TILE

    def body(dy_ref, w_ref, g_ref, u_ref, dg_ref, du_ref):
        da = lax.dot_general(dy_ref[...], w_ref[...], _NT, preferred_element_type=f32)
        g = g_ref[...].astype(f32)
        sg = jax.nn.sigmoid(g)
        du_ref[...] = (da * g * sg).astype(bf16)
        dg_ref[...] = (da * u_ref[...].astype(f32) * sg * (1.0 + g * (1.0 - sg))).astype(bf16)

    t_spec = pl.BlockSpec((tm, FFN_TN), lambda j, i: (i, j))
    return pl.pallas_call(
        body, name=name, grid=(D_FF // FFN_TN, s // tm),
        in_specs=[pl.BlockSpec((tm, d), lambda j, i: (i, 0)), pl.BlockSpec((FFN_TN, d), lambda j, i: (j, 0)),
                  t_spec, t_spec],
        out_specs=[t_spec, t_spec],
        out_shape=[jax.ShapeDtypeStruct((s, D_FF), bf16)] * 2,
        compiler_params=_cparams(("parallel", "parallel")),
    )(dy2, w_down, gate, up)


def _acc_spec(width):
    return pl.BlockSpec((1, width), lambda i: (0, 0))


def _proj_final_loss_bwd(a, w, x1, gate2, final_g, target, name):
    s, d = x1.shape
    k = a.shape[1]

    def body(a_ref, w_ref, x1_ref, gt_ref, fg_ref, tg_ref, dx2_ref, dy2_ref, loss_ref, dfg_ref, dgt_ref):
        @pl.when(pl.program_id(0) == 0)
        def _():
            loss_ref[...] = jnp.zeros_like(loss_ref)
            dfg_ref[...] = jnp.zeros_like(dfg_ref)
            dgt_ref[...] = jnp.zeros_like(dgt_ref)

        y2 = jnp.dot(a_ref[...], w_ref[...], preferred_element_type=f32)
        gt = gt_ref[...]
        fg = fg_ref[...]
        x2 = x1_ref[...] + gt * y2
        rstd = lax.rsqrt(jnp.mean(x2 * x2, axis=-1, keepdims=True) + EPS)
        xn = x2 * rstd
        err = xn * fg - tg_ref[...]
        row = jnp.sum(err * err, axis=-1, keepdims=True) * (0.5 / d)
        loss_ref[...] += jnp.sum(row, axis=0, keepdims=True) + jnp.zeros_like(loss_ref)
        dout = err * (1.0 / d)
        dfg_ref[...] += jnp.sum(dout * xn, axis=0, keepdims=True)
        dxn = dout * fg
        dx2 = rstd * (dxn - xn * jnp.mean(dxn * xn, axis=-1, keepdims=True))
        dx2_ref[...] = dx2
        dgt_ref[...] += jnp.sum(dx2 * y2, axis=0, keepdims=True)
        dy2_ref[...] = (gt * dx2).astype(bf16)

    return pl.pallas_call(
        body, name=name, grid=(s // TOK_TILE,),
        in_specs=[_row_spec(k), pl.BlockSpec((k, d), lambda i: (0, 0)), _row_spec(d), _vec_spec(d), _vec_spec(d),
                  _row_spec(d)],
        out_specs=[_row_spec(d), _row_spec(d), _acc_spec(128), _acc_spec(d), _acc_spec(d)],
        out_shape=[jax.ShapeDtypeStruct((s, d), f32), jax.ShapeDtypeStruct((s, d), bf16),
                   jax.ShapeDtypeStruct((1, 128), f32), jax.ShapeDtypeStruct((1, d), f32),
                   jax.ShapeDtypeStruct((1, d), f32)],
        compiler_params=_cparams(("arbitrary",)),
    )(a, w, x1, gate2, final_g, target)


def _proj_ln_mod_bwd(pairs, xin, gain, sc, dres, tm, name, xchg, gate=None, y=None):
    s, d = xin.shape
    with_gate = gate is not None
    npair = len(pairs)
    n_in = 2 * npair + (7 if with_gate else 5) - 1
    n_out = 6 if with_gate else 4

    def body(*refs):
        ab = refs[:2 * npair]
        if with_gate:
            (x_ref, g_ref, sc_ref, dr_ref, gt_ref, y_ref,
             dx_ref, dsh_ref, dsc_ref, dg_ref, dy_ref, dgt_ref) = refs[2 * npair:]
        else:
            x_ref, g_ref, sc_ref, dr_ref, dx_ref, dsh_ref, dsc_ref, dg_ref = refs[2 * npair:]

        @pl.when(pl.program_id(0) == 0)
        def _():
            dsh_ref[...] = jnp.zeros_like(dsh_ref)
            dsc_ref[...] = jnp.zeros_like(dsc_ref)
            dg_ref[...] = jnp.zeros_like(dg_ref)
            if with_gate:
                dgt_ref[...] = jnp.zeros_like(dgt_ref)

        dh = lax.dot_general(ab[0][...], ab[1][...], _NT, preferred_element_type=f32)
        for t in range(1, npair):
            dh = dh + lax.dot_general(ab[2 * t][...], ab[2 * t + 1][...], _NT, preferred_element_type=f32)
        xv = x_ref[...]
        g = g_ref[...]
        sc1 = 1.0 + sc_ref[...]
        rstd = lax.rsqrt(jnp.mean(xv * xv, axis=-1, keepdims=True) + EPS)
        xn = xv * rstd
        dsh_ref[...] += jnp.sum(dh, axis=0, keepdims=True)
        dsc_ref[...] += jnp.sum(dh * (xn * g), axis=0, keepdims=True)
        dg_ref[...] += jnp.sum(dh * sc1 * xn, axis=0, keepdims=True)
        dxn = dh * sc1 * g
        dx = dr_ref[...] + rstd * (dxn - xn * jnp.mean(dxn * xn, axis=-1, keepdims=True))
        dx_ref[...] = dx
        if with_gate:
            dgt_ref[...] += jnp.sum(dx * y_ref[...], axis=0, keepdims=True)
            dy_ref[...] = (gt_ref[...] * dx).astype(bf16)

    row = lambda width: pl.BlockSpec((tm, width), lambda i: (i, 0))
    in_specs, args = [], []
    for a, b in pairs:
        in_specs += [row(a.shape[1]), pl.BlockSpec(b.shape, lambda i: (0, 0))]
        args += [a, b]
    in_specs += [row(d), _vec_spec(d), _vec_spec(d), row(d)]
    args += [xin, gain, sc, dres]
    out_specs = [row(d), _acc_spec(d), _acc_spec(d), _acc_spec(d)]
    out_shape = [jax.ShapeDtypeStruct((s, d), f32)] + [jax.ShapeDtypeStruct((1, d), f32)] * 3
    if with_gate:
        in_specs += [_vec_spec(d), row(d)]
        out_specs += [row(d), _acc_spec(d)]
        out_shape += [jax.ShapeDtypeStruct((s, d), bf16), jax.ShapeDtypeStruct((1, d), f32)]
        args += [gate, y]
    grid = (s // tm,)
    out = pl.pallas_call(
        _ride(body, n_in, n_out, xchg, grid), name=name, grid=grid,
        in_specs=in_specs + [_ANY] * xchg.n, out_specs=out_specs + [_ANY] * xchg.n,
        out_shape=out_shape + xchg.out_shape(), scratch_shapes=xchg.scratch(),
        compiler_params=_cparams(("arbitrary",)),
    )(*args, *xchg.arrs)
    return out[:n_out], out[n_out:]


def _bucket_tables():
    import numpy as np
    qi = np.arange(BAND)[:, None]
    kj = np.arange(2 * BAND)[None, :]
    steps = qi + BAND - kj
    max_exact = N_BUCKETS // 2
    out = []
    for d in DILATIONS:
        dist = np.maximum(steps, 0) * d
        dist_f = np.maximum(dist, 1).astype(np.float32)
        large = max_exact + (np.log(dist_f / np.float32(max_exact)) / np.float32(math.log(MAX_DISTANCE / max_exact))
                             * np.float32(N_BUCKETS - max_exact)).astype(np.int32)
        out.append(np.where(dist < max_exact, dist, np.minimum(large, N_BUCKETS - 1)))
    return jnp.asarray(np.stack(out).astype(np.int32))


def _bias_tables(rel_bias, idx):
    def body(idx_ref, rb_ref, o_ref):
        h = pl.program_id(1)
        idxv = idx_ref[0]
        acc = jnp.zeros((BAND, 2 * BAND), f32)
        for b in range(N_BUCKETS):
            acc = jnp.where(idxv == b, rb_ref[b, h], acc)
        o_ref[0, 0] = acc

    return pl.pallas_call(
        body, name="attn_bias_tables", grid=(3, N_HEADS),
        in_specs=[pl.BlockSpec((1, BAND, 2 * BAND), lambda br, h: (br, 0, 0)),
                  pl.BlockSpec(memory_space=pltpu.SMEM)],
        out_specs=pl.BlockSpec((1, 1, BAND, 2 * BAND), lambda br, h: (br, h, 0, 0)),
        out_shape=jax.ShapeDtypeStruct((3, N_HEADS, BAND, 2 * BAND), f32),
        compiler_params=_cparams(("parallel", "parallel")),
    )(idx, rel_bias)


def _bias_grad(dbias, idx):
    def body(idx_ref, db_ref, o_ref):
        br = pl.program_id(1)

        @pl.when(br == 0)
        def _():
            o_ref[...] = jnp.zeros_like(o_ref)

        idxv = idx_ref[0]
        dbv = db_ref[0, 0]
        row = lax.broadcasted_iota(jnp.int32, (N_BUCKETS, 128), 0)
        acc = jnp.zeros((N_BUCKETS, 128), f32)
        for b in range(N_BUCKETS):
            sb = jnp.sum(jnp.sum(jnp.where(idxv == b, dbv, 0.0), axis=1, keepdims=True), axis=0, keepdims=True)
            acc = acc + jnp.where(row == b, sb, 0.0)
        o_ref[0] += acc

    return pl.pallas_call(
        body, name="attn_bias_grad", grid=(N_HEADS, 3),
        in_specs=[pl.BlockSpec((1, BAND, 2 * BAND), lambda h, br: (br, 0, 0)),
                  pl.BlockSpec((1, 1, BAND, 2 * BAND), lambda h, br: (br, h, 0, 0))],
        out_specs=pl.BlockSpec((1, N_BUCKETS, 128), lambda h, br: (h, 0, 0)),
        out_shape=jax.ShapeDtypeStruct((N_HEADS, N_BUCKETS, 128), f32),
        compiler_params=_cparams(("parallel", "arbitrary")),
    )(idx, dbias)


def _attn_masks():
    lane = lax.broadcasted_iota(jnp.int32, (BAND, 128), 1)
    m0 = lane < HEAD_DIM
    qi = lax.broadcasted_iota(jnp.int32, (BAND, 2 * BAND), 0)
    kj = lax.broadcasted_iota(jnp.int32, (BAND, 2 * BAND), 1)
    steps = qi + BAND - kj
    in_window = (steps >= 0) & (steps <= BAND)
    return m0, in_window, kj >= BAND


_NT = (((1,), (1,)), ((), ()))
_TN = (((0,), (0,)), ((), ()))
_BNN = (((2,), (1,)), ((0,), (0,)))
_BNT = (((2,), (2,)), ((0,), (0,)))
_BTN = (((1,), (1,)), ((0,), (0,)))
ATTN_GROUP = 4
ATTN_ITEMS = PAD_UNIT // BAND
Q_COL, K_COL, V_COL = 0, 4, 8


def _attn_item_rows(j, d, c, cbase):
    r = lax.rem(j, d)
    b = lax.div(j, d)
    loc = b * (d * BAND) + r
    first = jnp.logical_and(c == 0, b == 0)
    start = cbase + loc
    pstart = jnp.where(first, start, start - d * BAND)
    return loc, start, pstart, first


def _attn_fwd(proj, bias, xchg):
    s = proj.shape[0]

    def body(q_ref, k_ref, v_ref, b_ref, y_ref, lse_ref, o_s, l_s):
        c = pl.program_id(1)
        cbase = pl.multiple_of(c * PAD_UNIT, PAD_UNIT)
        m0, in_window, cur_half = _attn_masks()
        for bi, d in enumerate(DILATIONS):
            def group(jg, carry, bi=bi, d=d):
                locs, qs, ks, vs, pens = [], [], [], [], []
                for t in range(ATTN_GROUP):
                    loc, start, pstart, first = _attn_item_rows(jg * ATTN_GROUP + t, d, c, cbase)
                    locs.append(loc)
                    qs.append(q_ref[pl.ds(loc, BAND, stride=d), :])
                    ks.append(jnp.concatenate([k_ref[pl.ds(pstart, BAND, stride=d), :],
                                               k_ref[pl.ds(start, BAND, stride=d), :]], axis=0))
                    vs.append(jnp.concatenate([v_ref[pl.ds(pstart, BAND, stride=d), :],
                                               v_ref[pl.ds(start, BAND, stride=d), :]], axis=0))
                    pens.append(jnp.where(cur_half, 0.0, jnp.where(first, NEG_INF, 0.0)))
                q = jnp.stack(qs)
                kk = jnp.stack(ks + ks).astype(bf16)
                vv = jnp.stack(vs + vs).astype(bf16)
                pen = jnp.stack(pens + pens)
                qh = (jnp.concatenate([jnp.where(m0, q, 0.0), jnp.where(m0, 0.0, q)], axis=0) * 0.125).astype(bf16)
                sc = lax.dot_general(qh, kk, _BNT, preferred_element_type=f32)
                sc = (sc.reshape(2, ATTN_GROUP, BAND, 2 * BAND) + b_ref[bi][:, None]).reshape(sc.shape) + pen
                sc = jnp.where(in_window, sc, NEG_INF)
                mx = jnp.max(sc, axis=-1, keepdims=True)
                e = jnp.exp(sc - mx)
                l = jnp.sum(e, axis=-1, keepdims=True)
                o = lax.dot_general(e.astype(bf16), vv, _BNN, preferred_element_type=f32) / l
                ls = mx + jnp.log(l)
                for t in range(ATTN_GROUP):
                    rows = pl.ds(locs[t], BAND, stride=d)
                    o_s[bi, rows, :] = jnp.where(m0, o[t], o[ATTN_GROUP + t])
                    l_s[bi, rows, :] = jnp.where(m0, ls[t], ls[ATTN_GROUP + t])
                return carry

            lax.fori_loop(0, ATTN_ITEMS // ATTN_GROUP, group, 0)

        def merge(t, carry):
            rows = pl.ds(pl.multiple_of(t * 256, 256), 256)
            ls = [l_s[i, rows, :] for i in range(3)]
            mx = jnp.maximum(jnp.maximum(ls[0], ls[1]), ls[2])
            ws = [jnp.exp(l - mx) for l in ls]
            tot = ws[0] + ws[1] + ws[2]
            y = (ws[0] * o_s[0, rows, :] + ws[1] * o_s[1, rows, :] + ws[2] * o_s[2, rows, :]) / tot
            y_ref[rows, :] = y
            lse_ref[rows, :] = mx + jnp.log(tot)
            return carry

        lax.fori_loop(0, PAD_UNIT // 256, merge, 0)

    chunk = lambda col: pl.BlockSpec((PAD_UNIT, 128), lambda p, c: (c, col + p))
    full = lambda col: pl.BlockSpec((s, 128), lambda p, c: (0, col + p))
    grid = (N_HEADS // 2, s // PAD_UNIT)
    out = pl.pallas_call(
        _ride(body, 4, 2, xchg, grid), name="attn_fwd", grid=grid,
        in_specs=[chunk(Q_COL), full(K_COL), full(V_COL),
                  pl.BlockSpec((3, 2, BAND, 2 * BAND), lambda p, c: (0, p, 0, 0))] + [_ANY] * xchg.n,
        out_specs=[chunk(0), chunk(0)] + [_ANY] * xchg.n,
        out_shape=[jax.ShapeDtypeStruct((s, GROUP_W), f32)] * 2 + xchg.out_shape(),
        scratch_shapes=[pltpu.VMEM((3, PAD_UNIT, 128), f32)] * 2 + xchg.scratch(),
        compiler_params=_cparams(("arbitrary", "arbitrary")),
    )(proj, proj, proj, bias, *xchg.arrs)
    return out[:2], out[2:]


def _attn_bwd(proj, bias, y, lse, dycat):
    s = proj.shape[0]

    def body(q_ref, k_ref, v_ref, b_ref, y_ref, lse_ref, dy_ref, dq_ref, dk_ref, dv_ref, db_ref, dd_s):
        c = pl.program_id(1)
        cbase = pl.multiple_of(c * PAD_UNIT, PAD_UNIT)
        m0, in_window, cur_half = _attn_masks()

        @pl.when(c == 0)
        def _():
            dk_ref[...] = jnp.zeros_like(dk_ref)
            dv_ref[...] = jnp.zeros_like(dv_ref)
            db_ref[...] = jnp.zeros_like(db_ref)

        dq_ref[...] = jnp.zeros_like(dq_ref)

        def rowdot(t, carry):
            rows = pl.ds(pl.multiple_of(t * 256, 256), 256)
            prod = dy_ref[rows, :] * y_ref[rows, :]
            lane = lax.broadcasted_iota(jnp.int32, prod.shape, 1)
            h0 = lane < HEAD_DIM
            d0 = jnp.sum(jnp.where(h0, prod, 0.0), axis=-1, keepdims=True)
            d1 = jnp.sum(jnp.where(h0, 0.0, prod), axis=-1, keepdims=True)
            dd_s[rows, :] = jnp.where(h0, d0, d1)
            return carry

        lax.fori_loop(0, PAD_UNIT // 256, rowdot, 0)

        for bi, d in enumerate(DILATIONS):
            def group(jg, carry, bi=bi, d=d):
                ng = ATTN_GROUP
                meta, qs, dos, lqs, dds, ks, vs, pens = [], [], [], [], [], [], [], []
                for t in range(ng):
                    loc, start, pstart, first = _attn_item_rows(jg * ng + t, d, c, cbase)
                    qrows = pl.ds(loc, BAND, stride=d)
                    rows = pl.ds(start, BAND, stride=d)
                    prows = pl.ds(pstart, BAND, stride=d)
                    meta.append((qrows, rows, prows))
                    qs.append(q_ref[qrows, :])
                    dos.append(dy_ref[qrows, :])
                    lqs.append(lse_ref[qrows, :])
                    dds.append(dd_s[qrows, :])
                    ks.append(jnp.concatenate([k_ref[prows, :], k_ref[rows, :]], axis=0))
                    vs.append(jnp.concatenate([v_ref[prows, :], v_ref[rows, :]], axis=0))
                    pens.append(jnp.where(cur_half, 0.0, jnp.where(first, NEG_INF, 0.0)))

                def heads(t):
                    return jnp.concatenate([jnp.where(m0, t, 0.0), jnp.where(m0, 0.0, t)], axis=0)

                def head_col(t):
                    return jnp.concatenate([t[:, :, 0:1], t[:, :, HEAD_DIM:HEAD_DIM + 1]], axis=0)

                qh = (heads(jnp.stack(qs)) * 0.125).astype(bf16)
                doh = heads(jnp.stack(dos)).astype(bf16)
                kk = jnp.stack(ks + ks).astype(bf16)
                vv = jnp.stack(vs + vs).astype(bf16)
                sc = lax.dot_general(qh, kk, _BNT, preferred_element_type=f32)
                sc = (sc.reshape(2, ng, BAND, 2 * BAND) + b_ref[bi][:, None]).reshape(sc.shape) + jnp.stack(pens + pens)
                sc = jnp.where(in_window, sc, NEG_INF)
                p = jnp.exp(sc - head_col(jnp.stack(lqs)))
                dp = lax.dot_general(doh, vv, _BNT, preferred_element_type=f32)
                ds = p * (dp - head_col(jnp.stack(dds)))
                db_ref[bi] += jnp.sum(ds.reshape(2, ng, BAND, 2 * BAND), axis=1)
                dsb = ds.astype(bf16)
                dq = lax.dot_general(dsb, kk, _BNN, preferred_element_type=f32) * 0.125
                dk = lax.dot_general(dsb, qh, _BTN, preferred_element_type=f32)
                dv = lax.dot_general(p.astype(bf16), doh, _BTN, preferred_element_type=f32)
                for t in range(ng):
                    qrows, rows, prows = meta[t]
                    dq_ref[qrows, :] += jnp.where(m0, dq[t], dq[ng + t])
                    dkt = dk[t] + dk[ng + t]
                    dvt = dv[t] + dv[ng + t]
                    dk_ref[prows, :] += dkt[:BAND]
                    dk_ref[rows, :] += dkt[BAND:]
                    dv_ref[prows, :] += dvt[:BAND]
                    dv_ref[rows, :] += dvt[BAND:]
                return carry

            lax.fori_loop(0, ATTN_ITEMS // ATTN_GROUP, group, 0)

    chunk = lambda col: pl.BlockSpec((PAD_UNIT, 128), lambda p, c: (c, col + p))
    full = lambda col: pl.BlockSpec((s, 128), lambda p, c: (0, col + p))
    bias_spec = pl.BlockSpec((3, 2, BAND, 2 * BAND), lambda p, c: (0, p, 0, 0))
    return pl.pallas_call(
        body, name="attn_bwd", grid=(N_HEADS // 2, s // PAD_UNIT),
        in_specs=[chunk(Q_COL), full(K_COL), full(V_COL), bias_spec, chunk(0), chunk(0), chunk(0)],
        out_specs=[chunk(0), full(0), full(0), bias_spec],
        out_shape=[jax.ShapeDtypeStruct((s, GROUP_W), f32)] * 3
        + [jax.ShapeDtypeStruct((3, N_HEADS, BAND, 2 * BAND), f32)],
        scratch_shapes=[pltpu.VMEM((PAD_UNIT, 128), f32)],
        compiler_params=_cparams(("parallel", "arbitrary")),
    )(proj, proj, proj, bias, y, lse, dycat)


_HI = lax.Precision.HIGHEST
DELTA_COL = 1536
Z_COL = 3072
BA_BLOCK = 28
DELTA_ROWS = 512


def _hdot(a, b):
    return jnp.dot(a, b, precision=_HI, preferred_element_type=f32)


_DIMS = dict(nn=(((2,), (1,)), ((0,), (0,))), nt=(((2,), (2,)), ((0,), (0,))), tn=(((1,), (1,)), ((0,), (0,))))


@functools.partial(jax.custom_vjp, nondiff_argnums=(2,))
def _mmx(a, b, mode):
    return lax.dot_general(a.astype(bf16), b.astype(bf16), _DIMS[mode], preferred_element_type=f32)


def _mmx_fwd(a, b, mode):
    return _mmx(a, b, mode), (a, b)


def _mmx_bwd(mode, res, g):
    a, b = res
    if mode == "nn":
        return _mmx(g, b, "nt"), _mmx(a, g, "tn")
    if mode == "nt":
        return _mmx(g, b, "nn"), _mmx(g, a, "tn")
    return _mmx(b, g, "nt"), _mmx(a, g, "nn")


_mmx.defvjp(_mmx_fwd, _mmx_bwd)


def _pair_iota():
    row = lax.broadcasted_iota(jnp.int32, (CHUNK, 128), 0)
    lane = lax.broadcasted_iota(jnp.int32, (CHUNK, 128), 1)
    return row, lane, lane & (CHUNK - 1)


def _bd(x):
    _, lane, _ = _pair_iota()
    m0 = lane < CHUNK
    return jnp.concatenate([jnp.where(m0, x, 0.0), jnp.where(m0, 0.0, x)], axis=1)


def _pmm(a, b):
    return _mmx(a, _bd(b), "nn")


def _ntp(x, y):
    return _mmx(x, _bd(y), "nt")


def _tnp(x, y):
    full = _mmx(x, y, "tn")
    _, lane, _ = _pair_iota()
    return jnp.where(lane < CHUNK, full[:, :CHUNK], full[:, CHUNK:])


def _tri_inv(a):
    row, lane, jj = _pair_iota()
    eye = jnp.where(row == jj, 1.0, 0.0).astype(f32)

    def same_block(log2b):
        return (row >> log2b) == (jj >> log2b)

    dgl = jnp.where(same_block(3), a, 0.0)
    d2 = _pmm(dgl, dgl)
    d4 = _pmm(d2, d2)
    t = _pmm(_pmm(eye - dgl, eye + d2), eye + d4)
    for lb in (3, 4, 5):
        off = jnp.where(same_block(lb + 1) & jnp.logical_not(same_block(lb)), a, 0.0)
        t = t - _pmm(_pmm(t, off), t)
    return t


@jax.custom_vjp
def _solve2(a, xv, xk):
    t = _tri_inv(a)
    return _pmm(t, xv), _pmm(t, xk)


def _solve2_fwd(a, xv, xk):
    t = _tri_inv(a)
    u, w = _pmm(t, xv), _pmm(t, xk)
    return (u, w), (t, u, w)


def _solve2_bwd(res, cts):
    t, u, w = res
    du, dw = cts
    dxv = _tnp(t, du)
    dxk = _tnp(t, dw)
    return -(_ntp(dxv, u) + _ntp(dxk, w)), dxv, dxk


_solve2.defvjp(_solve2_fwd, _solve2_bwd)


def _chunk_pre(qp, kp, vp, bp, gcum):
    row, lane, jj = _pair_iota()
    causal = row >= jj
    strict = row > jj
    rsel = jnp.sum(jnp.where(row == jj, gcum, 0.0), axis=1, keepdims=True)
    decay = jnp.where(causal, jnp.exp(jnp.where(causal, gcum - rsel, 0.0)), 0.0)
    kb = kp * bp
    kd = _bd(kp)
    a = jnp.where(strict, _mmx(kb, kd, "nt") * decay, 0.0)
    eg = jnp.exp(gcum)
    u, w = _solve2(a, vp * bp, kb * eg)
    qk = jnp.where(causal, _mmx(qp, kd, "nt") * decay, 0.0)
    glast = jnp.sum(jnp.where(row == CHUNK - 1, gcum, 0.0), axis=1, keepdims=True)
    return u, w, qp * eg, kp * jnp.exp(glast - gcum), qk, jnp.exp(glast)


def _chunk_post(u, w, qt, kh, qk, gam, sp):
    sd = _bd(sp)
    vnew = u - _mmx(w, sd, "nn")
    o = _mmx(qt, sd, "nn") + _pmm(qk, vnew)
    return o, gam * sp + _tnp(kh, vnew)


def _pair_spec(rows=DELTA_ROWS):
    return pl.BlockSpec((rows, 128), lambda i, p: (i, p))


DELTA_NB = DELTA_ROWS // CHUNK


def _chunks(ref):
    return ref[...].reshape(DELTA_NB, CHUNK, 128)


def _pairs(ref, rows):
    return jnp.stack([ref[rows, p * 128:(p + 1) * 128] for p in range(4)], axis=0)


def _delta_chunk_pre(qn, kn, sv, beta, g):
    s = qn.shape[0]

    def body(q_ref, k_ref, v_ref, b_ref, g_ref, u_ref, w_ref, qt_ref, kh_ref, qk_ref, gm_ref):
        outs = _chunk_pre(_chunks(q_ref), _chunks(k_ref), _chunks(v_ref), _chunks(b_ref), _chunks(g_ref))
        for ref, val in zip((u_ref, w_ref, qt_ref, kh_ref, qk_ref), outs[:5]):
            ref[...] = val.reshape(DELTA_ROWS, 128).astype(ref.dtype)
        gm_ref[...] = jnp.broadcast_to(outs[5], (DELTA_NB, 8, 128)).reshape(DELTA_NB * 8, 128)

    v_spec = pl.BlockSpec((DELTA_ROWS, 128), lambda i, p: (i, 8 + p))
    return pl.pallas_call(
        body, name="delta_chunk_pre", grid=(s // DELTA_ROWS, 4),
        in_specs=[_pair_spec(), _pair_spec(), v_spec, _pair_spec(), _pair_spec()],
        out_specs=[_pair_spec()] * 5 + [_pair_spec(DELTA_NB * 8)],
        out_shape=[jax.ShapeDtypeStruct((s, GROUP_W), f32)] + [jax.ShapeDtypeStruct((s, GROUP_W), bf16)] * 4
        + [jax.ShapeDtypeStruct((s // 8, GROUP_W), f32)],
        compiler_params=_cparams(("parallel", "parallel")),
    )(qn, kn, sv, beta, g)


def _delta_scan_fwd(u, w, qt, kh, qk, gm):
    s = u.shape[0]

    def body(u_ref, w_ref, qt_ref, kh_ref, qk_ref, gm_ref, o_ref, ss_ref, st):
        @pl.when(pl.program_id(0) == 0)
        def _():
            st[...] = jnp.zeros_like(st)

        def chunk(ci, carry):
            rows = pl.ds(pl.multiple_of(ci * CHUNK, CHUNK), CHUNK)
            grow = pl.ds(pl.multiple_of(ci * 8, 8), 1)
            sp = st[...]
            o, s2 = _chunk_post(_pairs(u_ref, rows), _pairs(w_ref, rows), _pairs(qt_ref, rows),
                                _pairs(kh_ref, rows), _pairs(qk_ref, rows), _pairs(gm_ref, grow), sp)
            for p in range(4):
                ss_ref[rows, p * 128:(p + 1) * 128] = sp[p]
                o_ref[rows, p * 128:(p + 1) * 128] = o[p]
            st[...] = s2
            return carry

        lax.fori_loop(0, DELTA_NB, chunk, 0)

    spec = pl.BlockSpec((DELTA_ROWS, GROUP_W), lambda i: (i, 0))
    gspec = pl.BlockSpec((DELTA_NB * 8, GROUP_W), lambda i: (i, 0))
    return pl.pallas_call(
        body, name="delta_scan_fwd", grid=(s // DELTA_ROWS,),
        in_specs=[spec] * 5 + [gspec],
        out_specs=[spec, spec],
        out_shape=[jax.ShapeDtypeStruct((s, GROUP_W), f32)] * 2,
        scratch_shapes=[pltpu.VMEM((4, CHUNK, 128), f32)],
        compiler_params=_cparams(("arbitrary",)),
    )(u, w, qt, kh, qk, gm)


def _delta_scan_bwd(w, qt, kh, qk, gm, do):
    s = w.shape[0]
    nb = s // DELTA_ROWS

    def body(w_ref, qt_ref, kh_ref, qk_ref, gm_ref, do_ref, dso_ref, dst):
        @pl.when(pl.program_id(0) == 0)
        def _():
            dst[...] = jnp.zeros_like(dst)

        def chunk(t, carry):
            ci = DELTA_NB - 1 - t
            rows = pl.ds(pl.multiple_of(ci * CHUNK, CHUNK), CHUNK)
            grow = pl.ds(pl.multiple_of(ci * 8, 8), 1)
            ds = dst[...]
            for p in range(4):
                dso_ref[rows, p * 128:(p + 1) * 128] = ds[p]
            do = _pairs(do_ref, rows)
            dvn = _tnp(_pairs(qk_ref, rows), do) + _pmm(_pairs(kh_ref, rows), ds)
            dst[...] = _tnp(_pairs(qt_ref, rows), do) + _pairs(gm_ref, grow) * ds - _tnp(_pairs(w_ref, rows), dvn)
            return carry

        lax.fori_loop(0, DELTA_NB, chunk, 0)

    spec = pl.BlockSpec((DELTA_ROWS, GROUP_W), lambda i: (nb - 1 - i, 0))
    gspec = pl.BlockSpec((DELTA_NB * 8, GROUP_W), lambda i: (nb - 1 - i, 0))
    return pl.pallas_call(
        body, name="delta_scan_bwd", grid=(nb,),
        in_specs=[spec] * 4 + [gspec, spec],
        out_specs=spec,
        out_shape=jax.ShapeDtypeStruct((s, GROUP_W), f32),
        scratch_shapes=[pltpu.VMEM((4, CHUNK, 128), f32)],
        compiler_params=_cparams(("arbitrary",)),
    )(w, qt, kh, qk, gm, do)


def _delta_chunk_bwd(qn, kn, sv, beta, g, ss, dso, do):
    s = qn.shape[0]

    def body(q_ref, k_ref, v_ref, b_ref, g_ref, ss_ref, dso_ref, do_ref, dq_ref, dk_ref, dv_ref, db_ref, dg_ref):
        sp = _chunks(ss_ref)

        def fn(q, k, v, b, gg):
            return _chunk_post(*_chunk_pre(q, k, v, b, gg), sp)

        _, vjp = jax.vjp(fn, _chunks(q_ref), _chunks(k_ref), _chunks(v_ref), _chunks(b_ref), _chunks(g_ref))
        grads = vjp((_chunks(do_ref), _chunks(dso_ref)))
        for ref, val in zip((dq_ref, dk_ref, dv_ref, db_ref, dg_ref), grads):
            ref[...] = val.reshape(DELTA_ROWS, 128)

    v_spec = pl.BlockSpec((DELTA_ROWS, 128), lambda i, p: (i, 8 + p))
    return pl.pallas_call(
        body, name="delta_chunk_bwd", grid=(s // DELTA_ROWS, 4),
        in_specs=[_pair_spec(), _pair_spec(), v_spec] + [_pair_spec()] * 5,
        out_specs=[_pair_spec()] * 5,
        out_shape=[jax.ShapeDtypeStruct((s, GROUP_W), f32)] * 5,
        compiler_params=_cparams(("parallel", "parallel")),
    )(qn, kn, sv, beta, g, ss, dso, do)


def _head_sum_matrix():
    r = lax.broadcasted_iota(jnp.int32, (GROUP_W, GROUP_W), 0)
    c = lax.broadcasted_iota(jnp.int32, (GROUP_W, GROUP_W), 1)
    return jnp.where((r >> 6) == (c >> 6), 1.0, 0.0).astype(f32)


def _head_sums(x):
    return _mmx(x[None], _head_sum_matrix()[None], "nn")[0]


def _sel_dot(a, b):
    return jnp.dot(a, b, precision=lax.Precision.HIGH, preferred_element_type=f32)


def _softplus(x):
    return jnp.maximum(x, 0.0) + jnp.log(1.0 + jnp.exp(-jnp.abs(x)))


def _prep_fn(sq, sk, ba, alog_e, dt_e):
    qn = sq * lax.rsqrt(_head_sums(sq * sq) + EPS) * (HEAD_DIM ** -0.5)
    kn = sk * lax.rsqrt(_head_sums(sk * sk) + EPS)
    r = lax.broadcasted_iota(jnp.int32, (128, GROUP_W), 0)
    c = lax.broadcasted_iota(jnp.int32, (128, GROUP_W), 1) >> 6
    bl = _sel_dot(ba, jnp.where(r == c, 1.0, 0.0).astype(f32))
    al = _sel_dot(ba, jnp.where(r == c + N_HEADS, 1.0, 0.0).astype(f32))
    beta = jax.nn.sigmoid(bl)
    g = -jnp.exp(alog_e) * _softplus(al + dt_e)
    ri = lax.broadcasted_iota(jnp.int32, (TOK_TILE, TOK_TILE), 0)
    ci = lax.broadcasted_iota(jnp.int32, (TOK_TILE, TOK_TILE), 1)
    within = jnp.where(((ri >> 6) == (ci >> 6)) & (ri >= ci), 1.0, 0.0).astype(f32)
    return qn, kn, beta, _sel_dot(within, g)


def _gnorm_fn(o, z, ng_e):
    ms = _head_sums(o * o) * (1.0 / HEAD_DIM)
    return o * lax.rsqrt(ms + EPS) * ng_e * (z * jax.nn.sigmoid(z))


def _tok_spec(width, col):
    return pl.BlockSpec((TOK_TILE, width), lambda i: (i, col))


def _conv_taps(xs_ref, w_ref, base, n):
    acc = w_ref[CONV_WIDTH - 1:CONV_WIDTH, :] * xs_ref[pl.ds(base, n), :]
    for j in range(CONV_WIDTH - 1):
        acc = acc + w_ref[j:j + 1, :] * xs_ref[pl.ds(base - (CONV_WIDTH - 1) + j, n), :]
    return acc


def _conv_silu_fwd(proj, conv_w):
    s = proj.shape[0]
    wd = 3 * GROUP_W
    hb = TOK_TILE // 8

    def body(x_ref, halo_ref, w_ref, o_ref, xs):
        xs[0:8, :] = jnp.where(pl.program_id(0) > 0, halo_ref[...], 0.0)
        xs[8:, :] = x_ref[...]
        y = _conv_taps(xs, w_ref, 8, TOK_TILE)
        o_ref[...] = y * jax.nn.sigmoid(y)

    return pl.pallas_call(
        body, name="delta_conv_fwd", grid=(s // TOK_TILE,),
        in_specs=[_tok_spec(wd, 1), pl.BlockSpec((8, wd), lambda i: (jnp.maximum(i * hb - 1, 0), 1)),
                  pl.BlockSpec((CONV_WIDTH, wd), lambda i: (0, 0))],
        out_specs=_tok_spec(wd, 0),
        out_shape=jax.ShapeDtypeStruct((s, wd), f32),
        scratch_shapes=[pltpu.VMEM((TOK_TILE + 8, wd), f32)],
        compiler_params=_cparams(("parallel",)),
    )(proj, proj, conv_w)


def _conv_silu_bwd(proj, conv_w, ds3, xchg):
    s = proj.shape[0]
    wd = 3 * GROUP_W
    hb = TOK_TILE // 8
    nt = s // TOK_TILE

    def body(x_ref, hp_ref, hn_ref, dq_ref, dk_ref, dv_ref, dqn_ref, dkn_ref, dvn_ref, w_ref, dx_ref, dw_ref, xs, dys):
        i = pl.program_id(0)

        @pl.when(i == 0)
        def _():
            dw_ref[...] = jnp.zeros_like(dw_ref)

        last = i == nt - 1
        xs[0:8, :] = jnp.where(i > 0, hp_ref[...], 0.0)
        xs[8:8 + TOK_TILE, :] = x_ref[...]
        xs[8 + TOK_TILE:, :] = jnp.where(last, 0.0, hn_ref[...])
        y = _conv_taps(xs, w_ref, 8, TOK_TILE)
        sg = jax.nn.sigmoid(y)
        dsilu = sg * (1.0 + y * (1.0 - sg))
        yn = _conv_taps(xs, w_ref, 8 + TOK_TILE, 8)
        sgn = jax.nn.sigmoid(yn)
        dsilu_n = sgn * (1.0 + yn * (1.0 - sgn))
        for t, (cur, nxt) in enumerate(((dq_ref, dqn_ref), (dk_ref, dkn_ref), (dv_ref, dvn_ref))):
            cols = slice(t * GROUP_W, (t + 1) * GROUP_W)
            dys[0:TOK_TILE, cols] = cur[...] * dsilu[:, cols]
            dys[TOK_TILE:, cols] = jnp.where(last, 0.0, nxt[...]) * dsilu_n[:, cols]
        dy0 = dys[0:TOK_TILE, :]
        dx = w_ref[CONV_WIDTH - 1:CONV_WIDTH, :] * dy0
        for j in range(CONV_WIDTH - 1):
            dx = dx + w_ref[j:j + 1, :] * dys[pl.ds(CONV_WIDTH - 1 - j, TOK_TILE), :]
        dx_ref[...] = dx
        for j in range(CONV_WIDTH):
            dw_ref[j:j + 1, :] += jnp.sum(dy0 * xs[pl.ds(8 - (CONV_WIDTH - 1) + j, TOK_TILE), :],
                                          axis=0, keepdims=True)

    prev8 = lambda col: pl.BlockSpec((8, wd), lambda i: (jnp.maximum(i * hb - 1, 0), col))
    next8 = lambda col: pl.BlockSpec((8, wd), lambda i: (jnp.minimum((i + 1) * hb, s // 8 - 1), col))
    next8_third = pl.BlockSpec((8, GROUP_W), lambda i: (jnp.minimum((i + 1) * hb, s // 8 - 1), 0))
    out = pl.pallas_call(
        _ride(body, 10, 2, xchg, (nt,)), name="delta_conv_bwd", grid=(nt,),
        in_specs=[_tok_spec(wd, 1), prev8(1), next8(1)] + [_tok_spec(GROUP_W, 0)] * 3 + [next8_third] * 3
        + [pl.BlockSpec((CONV_WIDTH, wd), lambda i: (0, 0))] + [_ANY] * xchg.n,
        out_specs=[_tok_spec(wd, 0), pl.BlockSpec((CONV_WIDTH, wd), lambda i: (0, 0))] + [_ANY] * xchg.n,
        out_shape=[jax.ShapeDtypeStruct((s, wd), f32), jax.ShapeDtypeStruct((CONV_WIDTH, wd), f32)] + xchg.out_shape(),
        scratch_shapes=[pltpu.VMEM((TOK_TILE + 16, wd), f32), pltpu.VMEM((TOK_TILE + 8, wd), f32)] + xchg.scratch(),
        compiler_params=_cparams(("arbitrary",)),
    )(proj, proj, proj, *ds3, *ds3, conv_w, *xchg.arrs)
    return out[:2], out[2:]


def _delta_prep_fwd(sconv, proj, alog_e, dt_e):
    s = sconv.shape[0]

    def body(sq_ref, sk_ref, ba_ref, al_ref, dt_ref, q_ref, k_ref, b_ref, g_ref):
        qn, kn, beta, g = _prep_fn(sq_ref[...], sk_ref[...], ba_ref[...], al_ref[...], dt_ref[...])
        q_ref[...] = qn
        k_ref[...] = kn
        b_ref[...] = beta
        g_ref[...] = g

    return pl.pallas_call(
        body, name="delta_prep_fwd", grid=(s // TOK_TILE,),
        in_specs=[_tok_spec(GROUP_W, 0), _tok_spec(GROUP_W, 1), _tok_spec(128, BA_BLOCK),
                  _vec_spec(GROUP_W), _vec_spec(GROUP_W)],
        out_specs=[_tok_spec(GROUP_W, 0)] * 4,
        out_shape=[jax.ShapeDtypeStruct((s, GROUP_W), f32)] * 4,
        compiler_params=_cparams(("parallel",)),
    )(sconv, sconv, proj, alog_e, dt_e)


def _delta_prep_bwd(sconv, proj, alog_e, dt_e, dqn, dkn, dbeta, dg):
    s = sconv.shape[0]

    def body(sq_ref, sk_ref, ba_ref, al_ref, dt_ref, dq_ref, dk_ref, db_ref, dg_ref,
             dsq_ref, dsk_ref, dba_ref, dal_ref, ddt_ref):
        @pl.when(pl.program_id(0) == 0)
        def _():
            dal_ref[...] = jnp.zeros_like(dal_ref)
            ddt_ref[...] = jnp.zeros_like(ddt_ref)

        _, vjp = jax.vjp(_prep_fn, sq_ref[...], sk_ref[...], ba_ref[...], al_ref[...], dt_ref[...])
        dsq, dsk, dba, dal, ddt = vjp((dq_ref[...], dk_ref[...], db_ref[...], dg_ref[...]))
        dsq_ref[...] = dsq
        dsk_ref[...] = dsk
        dba_ref[...] = dba
        dal_ref[...] += dal
        ddt_ref[...] += ddt

    return pl.pallas_call(
        body, name="delta_prep_bwd", grid=(s // TOK_TILE,),
        in_specs=[_tok_spec(GROUP_W, 0), _tok_spec(GROUP_W, 1), _tok_spec(128, BA_BLOCK),
                  _vec_spec(GROUP_W), _vec_spec(GROUP_W)] + [_tok_spec(GROUP_W, 0)] * 4,
        out_specs=[_tok_spec(GROUP_W, 0), _tok_spec(GROUP_W, 0), _tok_spec(128, 0),
                   _acc_spec(GROUP_W), _acc_spec(GROUP_W)],
        out_shape=[jax.ShapeDtypeStruct((s, GROUP_W), f32)] * 2 + [jax.ShapeDtypeStruct((s, 128), f32)]
        + [jax.ShapeDtypeStruct((1, GROUP_W), f32)] * 2,
        compiler_params=_cparams(("arbitrary",)),
    )(sconv, sconv, proj, alog_e, dt_e, dqn, dkn, dbeta, dg)


def _gnorm_fwd(o, proj, ng_e):
    s = o.shape[0]

    def body(o_ref, z_ref, g_ref, y_ref):
        y_ref[...] = _gnorm_fn(o_ref[...], z_ref[...], g_ref[...])

    return pl.pallas_call(
        body, name="delta_gnorm_fwd", grid=(s // TOK_TILE,),
        in_specs=[_tok_spec(GROUP_W, 0), _tok_spec(GROUP_W, Z_COL // GROUP_W), _vec_spec(GROUP_W)],
        out_specs=_tok_spec(GROUP_W, 0),
        out_shape=jax.ShapeDtypeStruct((s, GROUP_W), f32),
        compiler_params=_cparams(("parallel",)),
    )(o, proj, ng_e)


def _gnorm_bwd(o, proj, ng_e, dycat):
    s = o.shape[0]

    def body(o_ref, z_ref, g_ref, dy_ref, do_ref, dz_ref, dg_ref):
        @pl.when(pl.program_id(0) == 0)
        def _():
            dg_ref[...] = jnp.zeros_like(dg_ref)

        _, vjp = jax.vjp(_gnorm_fn, o_ref[...], z_ref[...], g_ref[...])
        do, dz, dg = vjp(dy_ref[...])
        do_ref[...] = do
        dz_ref[...] = dz
        dg_ref[...] += dg

    return pl.pallas_call(
        body, name="delta_gnorm_bwd", grid=(s // TOK_TILE,),
        in_specs=[_tok_spec(GROUP_W, 0), _tok_spec(GROUP_W, Z_COL // GROUP_W), _vec_spec(GROUP_W),
                  _tok_spec(GROUP_W, 1)],
        out_specs=[_tok_spec(GROUP_W, 0), _tok_spec(GROUP_W, 0), _acc_spec(GROUP_W)],
        out_shape=[jax.ShapeDtypeStruct((s, GROUP_W), f32)] * 2 + [jax.ShapeDtypeStruct((1, GROUP_W), f32)],
        compiler_params=_cparams(("arbitrary",)),
    )(o, proj, ng_e, dycat)


_MESH = pl.DeviceIdType.MESH
_ANY = pl.BlockSpec(memory_space=pl.ANY)
_VMEM = pl.BlockSpec(memory_space=pltpu.VMEM)


def _my_place():
    x, y, c = lax.axis_index("x"), lax.axis_index("y"), lax.axis_index("c")
    return x, y, c, 4 * x + 2 * y + c


def _peer(k, x, y, c):
    px = 1 - x if k & 4 else x
    py = 1 - y if k & 2 else y
    pc = 1 - c if k & 1 else c
    return (px, py, pc), 4 * px + 2 * py + pc


def _exchange_all(src_of_peer, dst_ref, send_sems, recv_sems, x, y, c, me):
    sent = []
    for k in range(1, N_DEV):
        dev, pidx = _peer(k, x, y, c)
        cp = pltpu.make_async_remote_copy(src_ref=src_of_peer(pidx), dst_ref=dst_ref.at[me],
                                          send_sem=send_sems.at[k - 1], recv_sem=recv_sems.at[k - 1],
                                          device_id=dev, device_id_type=_MESH)
        cp.start()
        sent.append(cp)
    for k in range(1, N_DEV):
        dev, pidx = _peer(k, x, y, c)
        pltpu.make_async_remote_copy(src_ref=src_of_peer(pidx), dst_ref=dst_ref.at[pidx],
                                     send_sem=send_sems.at[k - 1], recv_sem=recv_sems.at[k - 1],
                                     device_id=dev, device_id_type=_MESH).wait_recv()
    for cp in sent:
        cp.wait_send()


def _ada_exchange(cv8, w_ada, b_ada8, w_in_sh):
    ride = _ChipGather(w_in_sh)

    def body(cv_ref, w_ref, b_ref, wi_ref, call_ref, modp_ref, wig_ref, part_s, s1, r1, s2, r2, *ride_sems):
        ride.start(wi_ref, wig_ref, ride_sems)
        x, y, c, me = _my_place()
        call_ref[me] = cv_ref[...]
        _exchange_all(lambda pidx: cv_ref, call_ref, s1, r1, x, y, c, me)
        bias = b_ref[me]
        for j in range(N_DEV):
            cj = call_ref[j][:, :D_MODEL]
            part_s[j] = _hdot(cj * jax.nn.sigmoid(cj), w_ref[...]) + bias
        modp_ref[me] = part_s[me]
        ride.forward(wi_ref, wig_ref, ride_sems)
        _exchange_all(lambda pidx: part_s.at[pidx], modp_ref, s2, r2, x, y, c, me)
        ride.finish(wi_ref, wig_ref, ride_sems)

    nsh = w_ada.shape[1]
    return pl.pallas_call(
        body, name="ada_exchange",
        in_specs=[_VMEM, _VMEM, _VMEM, _ANY], out_specs=[_VMEM, _VMEM, _ANY],
        out_shape=[jax.ShapeDtypeStruct((N_DEV, 8, cv8.shape[1]), f32), jax.ShapeDtypeStruct((N_DEV, 8, nsh), f32)]
        + [ride.out_shape()],
        scratch_shapes=[pltpu.VMEM((N_DEV, 8, nsh), f32)] + [pltpu.SemaphoreType.DMA((N_DEV - 1,))] * 4
        + ride.scratch(),
        compiler_params=pltpu.CompilerParams(vmem_limit_bytes=VMEM_LIMIT),
    )(cv8, w_ada, b_ada8, w_in_sh)


def _all_to_all(arrs, name):
    ex = _Exchange(arrs, gather=False)

    def body(*refs):
        srcs, dsts, sems = refs[:ex.n], refs[ex.n:2 * ex.n], refs[2 * ex.n:]
        ex.start(srcs, dsts, sems)
        ex.wait(srcs, dsts, sems)

    return pl.pallas_call(
        body, name=name,
        in_specs=[_ANY] * ex.n, out_specs=[_ANY] * ex.n,
        out_shape=ex.out_shape(), scratch_shapes=ex.scratch(),
    )(*arrs)


class _Exchange:
    def __init__(self, arrs, gather):
        self.arrs, self.gather, self.n = list(arrs), gather, len(arrs)

    def out_shape(self):
        return [jax.ShapeDtypeStruct(((N_DEV,) + a.shape) if self.gather else a.shape, a.dtype) for a in self.arrs]

    def scratch(self):
        if self.n == 0:
            return []
        return [pltpu.SemaphoreType.DMA((self.n, N_DEV - 1)), pltpu.SemaphoreType.DMA((self.n, N_DEV - 1)),
                pltpu.SemaphoreType.DMA((self.n,))]

    def _src(self, srcs, a, idx):
        return srcs[a] if self.gather else srcs[a].at[idx]

    def _copies(self, srcs, dsts, sems, incoming):
        send_sems, recv_sems, _ = sems
        x, y, c, me = _my_place()
        out = []
        for a in range(self.n):
            for k in range(1, N_DEV):
                dev, pidx = _peer(k, x, y, c)
                out.append(pltpu.make_async_remote_copy(
                    src_ref=self._src(srcs, a, pidx), dst_ref=dsts[a].at[pidx if incoming else me],
                    send_sem=send_sems.at[a, k - 1], recv_sem=recv_sems.at[a, k - 1],
                    device_id=dev, device_id_type=_MESH))
        return out

    def _local(self, srcs, dsts, sems):
        me = _my_place()[3]
        return [pltpu.make_async_copy(self._src(srcs, a, me), dsts[a].at[me], sems[2].at[a]) for a in range(self.n)]

    def start(self, srcs, dsts, sems):
        for cp in self._local(srcs, dsts, sems) + self._copies(srcs, dsts, sems, incoming=False):
            cp.start()

    def wait(self, srcs, dsts, sems):
        for cp in self._copies(srcs, dsts, sems, incoming=True):
            cp.wait_recv()
        for cp in self._copies(srcs, dsts, sems, incoming=False):
            cp.wait_send()
        for cp in self._local(srcs, dsts, sems):
            cp.wait()

    def start_at_first_step(self, grid, srcs, dsts, sems):
        first = functools.reduce(jnp.logical_and, [pl.program_id(i) == 0 for i in range(len(grid))])
        pl.when(first)(lambda: self.start(srcs, dsts, sems))

    def wait_at_last_step(self, grid, srcs, dsts, sems):
        last = functools.reduce(jnp.logical_and, [pl.program_id(i) == g - 1 for i, g in enumerate(grid)])
        pl.when(last)(lambda: self.wait(srcs, dsts, sems))


class _ChipGather:
    def __init__(self, shard):
        self.shard = shard

    def out_shape(self):
        return jax.ShapeDtypeStruct((N_DEV,) + self.shard.shape, self.shard.dtype)

    def scratch(self):
        return [pltpu.SemaphoreType.DMA((N_DEV - 1,)), pltpu.SemaphoreType.DMA((N_DEV - 1,)),
                pltpu.SemaphoreType.DMA(())]

    def _place(self):
        x, y, c, me = _my_place()
        return x, y, c, me, (x, y, 1 - c), [(1 - x, y), (x, 1 - y), (1 - x, 1 - y)]

    def _copy(self, out, sems, k, block, to, src=None):
        rows = out.at[4 * block[0] + 2 * block[1] + block[2]]
        return pltpu.make_async_remote_copy(src_ref=rows if src is None else src, dst_ref=rows,
                                            send_sem=sems[0].at[k], recv_sem=sems[1].at[k],
                                            device_id=to, device_id_type=_MESH)

    def start(self, src, out, sems):
        x, y, c, me, sib, chips = self._place()
        pltpu.make_async_copy(src, out.at[me], sems[2]).start()
        self._copy(out, sems, 0, (x, y, c), sib, src=src).start()
        for j, chip in enumerate(chips):
            self._copy(out, sems, 1 + j, (x, y, c), (*chip, c), src=src).start()

    def forward(self, src, out, sems):
        x, y, c, me, sib, chips = self._place()
        for j, chip in enumerate(chips):
            self._copy(out, sems, 1 + j, (*chip, c), (x, y, c)).wait_recv()
            self._copy(out, sems, 4 + j, (*chip, c), sib).start()

    def finish(self, src, out, sems):
        x, y, c, me, sib, chips = self._place()
        self._copy(out, sems, 0, (x, y, 1 - c), (x, y, c)).wait_recv()
        for j, chip in enumerate(chips):
            self._copy(out, sems, 4 + j, (*chip, 1 - c), (x, y, c)).wait_recv()
        self._copy(out, sems, 0, (x, y, c), sib, src=src).wait_send()
        for j, chip in enumerate(chips):
            self._copy(out, sems, 1 + j, (x, y, c), (*chip, c), src=src).wait_send()
            self._copy(out, sems, 4 + j, (*chip, c), sib).wait_send()
        pltpu.make_async_copy(src, out.at[me], sems[2]).wait()


def _ride(body, n_in, n_out, xchg, grid):
    nx = xchg.n
    if nx == 0:
        return body

    def wrapped(*refs):
        ins, xs = refs[:n_in], refs[n_in:n_in + nx]
        outs, xd = refs[n_in + nx:n_in + nx + n_out], refs[n_in + nx + n_out:n_in + 2 * nx + n_out]
        scratch = refs[n_in + 2 * nx + n_out:]
        xchg.start_at_first_step(grid, xs, xd, scratch[-3:])
        body(*ins, *outs, *scratch[:-3])
        xchg.wait_at_last_step(grid, xs, xd, scratch[-3:])

    return wrapped


def _adamw_math(w, g, m, v):
    m2 = ADAM_B1 * m + (1.0 - ADAM_B1) * g
    v2 = ADAM_B2 * v + (1.0 - ADAM_B2) * (g * g)
    m_hat = m2 / (1.0 - ADAM_B1 ** ADAM_STEP)
    v_hat = v2 / (1.0 - ADAM_B2 ** ADAM_STEP)
    delta = -ADAM_LR * (m_hat / (jnp.sqrt(v_hat) + ADAM_EPS) + ADAM_WD * w)
    return delta, m2, v2


def _row_tile(rows):
    for t in (256, 128, 64, 32, 16, 8):
        if rows % t == 0:
            return t
    return rows


def _reduce_adamw(parts, w, m, v, name):
    _, r, cdim = parts.shape
    tr = _row_tile(r)

    def body(p_ref, w_ref, m_ref, v_ref, g_ref, d_ref, m2_ref, v2_ref):
        g = p_ref[0].astype(f32)
        for j in range(1, N_DEV):
            g = g + p_ref[j].astype(f32)
        delta, m2, v2 = _adamw_math(w_ref[...], g, m_ref[...], v_ref[...])
        g_ref[...] = g
        d_ref[...] = delta
        m2_ref[...] = m2
        v2_ref[...] = v2

    spec = pl.BlockSpec((tr, cdim), lambda i: (i, 0))
    return pl.pallas_call(
        body, name=name, grid=(r // tr,),
        in_specs=[pl.BlockSpec((N_DEV, tr, cdim), lambda i: (0, i, 0)), spec, spec, spec],
        out_specs=[spec] * 4,
        out_shape=[jax.ShapeDtypeStruct((r, cdim), f32)] * 4,
        compiler_params=_cparams(("parallel",)),
    )(parts, w, m, v)


def _adamw(w, g, m, v, name):
    r, cdim = w.shape
    tr = _row_tile(r)

    def body(w_ref, g_ref, m_ref, v_ref, d_ref, m2_ref, v2_ref):
        delta, m2, v2 = _adamw_math(w_ref[...], g_ref[...], m_ref[...], v_ref[...])
        d_ref[...] = delta
        m2_ref[...] = m2
        v2_ref[...] = v2

    spec = pl.BlockSpec((tr, cdim), lambda i: (i, 0))
    return pl.pallas_call(
        body, name=name, grid=(r // tr,),
        in_specs=[spec] * 4, out_specs=[spec] * 3,
        out_shape=[jax.ShapeDtypeStruct((r, cdim), f32)] * 3,
        compiler_params=_cparams(("parallel",)),
    )(w, g, m, v)


def _sum_devices(parts, name):
    _, r, cdim = parts.shape

    def body(p_ref, o_ref):
        g = p_ref[0]
        for j in range(1, N_DEV):
            g = g + p_ref[j]
        o_ref[...] = g

    return pl.pallas_call(
        body, name=name, out_shape=jax.ShapeDtypeStruct((r, cdim), f32),
        in_specs=[_VMEM], out_specs=_VMEM,
    )(parts)


def _ada_wgrad(c_all8, dmod_cols):
    nsh = dmod_cols.shape[1]

    def body(c_ref, d_ref, o_ref):
        cv = c_ref[...]
        o_ref[...] = lax.dot_general(cv * jax.nn.sigmoid(cv), d_ref[...], _TN, precision=_HI,
                                     preferred_element_type=f32)

    return pl.pallas_call(
        body, name="ada_wgrad", out_shape=jax.ShapeDtypeStruct((D_MODEL, nsh), f32),
        in_specs=[_VMEM, _VMEM], out_specs=_VMEM,
        compiler_params=pltpu.CompilerParams(vmem_limit_bytes=VMEM_LIMIT),
    )(c_all8, dmod_cols)


def _cols(t):
    return t.transpose(1, 0, 2).reshape(t.shape[1], N_DEV * t.shape[2])


def _col_blocks(t, n):
    return t.reshape(t.shape[0], N_DEV, n).transpose(1, 0, 2).astype(bf16)


def _row_blocks(t):
    return t.reshape(N_DEV, t.shape[0] // N_DEV, t.shape[1]).astype(bf16)


def _local_step(x, tgt, mod, norm_attn_g, w_in_p, rel_bias, conv_full, a_log, dt_bias, delta_norm_g,
                norm_ffn_g, final_norm_g, w_out_sh, w_gate_sh, w_up_sh, w_down_sh):
    s = x.shape[0]
    sh1, sc1, g1, sh2, sc2, g2 = [mod[:, i * D_MODEL:(i + 1) * D_MODEL] for i in range(6)]
    nag = norm_attn_g.reshape(1, D_MODEL)
    nfg = norm_ffn_g.reshape(1, D_MODEL)
    fg = final_norm_g.reshape(1, D_MODEL)
    idx = _bucket_tables()
    bias = _bias_tables(rel_bias, idx)
    alog_e = jnp.repeat(a_log.reshape(N_HEADS), HEAD_DIM)[None]
    dt_e = jnp.repeat(dt_bias.reshape(N_HEADS), HEAD_DIM)[None]
    ng_e = jnp.tile(delta_norm_g.reshape(HEAD_DIM), N_HEADS)[None]

    h1 = _ln_mod_fwd(x, nag, sc1, sh1, "ln1_fwd")
    proj, (w_out_g, w_gate_g) = _mm(h1, w_in_p, "nn", f32, 512, 1280, 1024, "in_proj",
                                    xchg=_Exchange([w_out_sh, w_gate_sh], gather=True))
    (y_attn, lse), (w_up_g, w_down_g) = _attn_fwd(proj, bias, _Exchange([w_up_sh, w_down_sh], gather=True))
    w_out_b = w_out_g.reshape(2 * GROUP_W, D_MODEL)
    w_gate_b, w_up_b = _cols(w_gate_g), _cols(w_up_g)
    w_down_b = w_down_g.reshape(D_FF, D_MODEL)
    n_ff = w_gate_sh.shape[1]
    sconv = _conv_silu_fwd(proj, conv_full)
    qn, kn, beta, g = _delta_prep_fwd(sconv, proj, alog_e, dt_e)
    u, w, qt, kh, qk, gm = _delta_chunk_pre(qn, kn, sconv, beta, g)
    o, ss = _delta_scan_fwd(u, w, qt, kh, qk, gm)
    y_delta = _gnorm_fwd(o, proj, ng_e)
    ycat = jnp.concatenate([y_attn, y_delta], axis=1).astype(bf16)
    y, x1, h2 = _proj_resid_ln_mod_fwd(ycat, w_out_b, x, g1, nfg, sc2, sh2, "out_proj_ln2")
    act, gate, up = _ffn_up(h2, w_gate_b, w_up_b, "ffn_up")
    dx2, dy2, loss, dfg, dg2 = _proj_final_loss_bwd(act, w_down_b, x1, g2, fg, tgt, "ffn_down_loss")

    dgate, dup = _ffn_down_dx(dy2, w_down_b, gate, up, "ffn_down_dx")
    g_down = _mm(act, dy2, "tn", f32, 1408, 1024, 1024, "ffn_down_dw")
    (dx1, dsh2, dsc2, dnfg, dy, dg1), (r_down,) = _proj_ln_mod_bwd(
        [(dgate, w_gate_b), (dup, w_up_b)], x1, nfg, sc2, dx2, 256, "ffn_up_dx_ln2",
        _Exchange([_row_blocks(g_down)], gather=False), gate=g1, y=y)
    g_gate = _mm(h2, dgate, "tn", f32, 1024, 1408, 1024, "ffn_gate_dw")
    g_up = _mm(h2, dup, "tn", f32, 1024, 1408, 1024, "ffn_up_dw")
    dycat = _mm(dy, w_out_b, "nt", f32, 512, 1024, 1024, "out_proj_dx")
    g_out = _mm(ycat, dy, "tn", f32, 1024, 1024, 1024, "out_proj_dw")
    dq, dk, dv, dbias = _attn_bwd(proj, bias, y_attn, lse, dycat)
    g_rb = _bias_grad(dbias, idx)[:, :, 0].T
    do, dz, dng = _gnorm_bwd(o, proj, ng_e, dycat)
    dso = _delta_scan_bwd(w, qt, kh, qk, gm, do)
    dqn, dkn, dvd, dbeta, dgd = _delta_chunk_bwd(qn, kn, sconv, beta, g, ss, dso, do)
    dsq, dsk, dba, dal, ddt = _delta_prep_bwd(sconv, proj, alog_e, dt_e, dqn, dkn, dbeta, dgd)
    (dxc, g_conv), (r_gate, r_up, r_out) = _conv_silu_bwd(
        proj, conv_full, (dsq, dsk, dvd),
        _Exchange([_col_blocks(g_gate, n_ff), _col_blocks(g_up, n_ff), _row_blocks(g_out)],
                  gather=False))
    dproj = jnp.concatenate([dq, dk, dv, dxc, dz, dba, jnp.zeros((s, IN_PAD - BA_BLOCK * 128 - 128), f32)],
                            axis=1).astype(bf16)
    g_in = _mm(h1, dproj, "tn", f32, 1024, 1280, 1024, "in_proj_dw")
    (gx, dsh1, dsc1, dnag), (r_in,) = _proj_ln_mod_bwd(
        [(dproj, w_in_p)], x, nag, sc1, dx1, TOK_TILE, "in_proj_dx_ln1",
        _Exchange([_col_blocks(g_in[:, :IN_WIDTH], IN_WIDTH // N_DEV)], gather=False))
    grads = dict(
        x=gx, mod=jnp.concatenate([dsh1, dsc1, dg1, dsh2, dsc2, dg2], axis=1),
        norm_attn_g=dnag, norm_ffn_g=dnfg, final_norm_g=dfg, rel_bias=g_rb, conv_w=g_conv,
        a_log=dal.reshape(N_HEADS, HEAD_DIM).sum(-1), dt_bias=ddt.reshape(N_HEADS, HEAD_DIM).sum(-1),
        delta_norm_g=dng.reshape(N_HEADS, HEAD_DIM).sum(0),
        w_in=r_in, w_out=r_out, w_gate=r_gate, w_up=r_up, w_down=r_down)
    return loss[0, 0], grads


MISC_OFF = dict(rel_bias=0, a_log=256, dt_bias=264, delta_norm_g=272)


def _misc_row(rel_bias, a_log, dt_bias, delta_norm_g):
    flat = jnp.concatenate([rel_bias.reshape(-1), a_log.reshape(-1), dt_bias.reshape(-1), delta_norm_g.reshape(-1)])
    return jnp.pad(flat, (0, D_MODEL - flat.shape[0]))[None]


def _pack_small(b_ada, nag, nfg, fng, rel_bias, a_log, dt_bias, dng, conv_shard):
    rows = [b_ada.reshape(6, D_MODEL), nag.reshape(1, D_MODEL), nfg.reshape(1, D_MODEL), fng.reshape(1, D_MODEL),
            _misc_row(rel_bias, a_log, dt_bias, dng),
            jnp.pad(conv_shard.reshape(-1), (0, D_MODEL - conv_shard.size))[None],
            jnp.zeros((5, D_MODEL), f32)]
    return jnp.concatenate(rows, axis=0)


def _unpack_small(p, conv_shape):
    misc = p[9]
    return dict(
        b_ada=p[0:6].reshape(1, 6 * D_MODEL), norm_attn_g=p[6:7], norm_ffn_g=p[7:8], final_norm_g=p[8],
        rel_bias=misc[0:256].reshape(N_BUCKETS, N_HEADS), a_log=misc[256:264].reshape(1, N_HEADS),
        dt_bias=misc[264:272].reshape(1, N_HEADS), delta_norm_g=misc[272:336].reshape(1, HEAD_DIM),
        conv_w=p[10, :conv_shape[1] * conv_shape[2]].reshape(conv_shape))


def kernel(x, c, w_ada, b_ada, norm_attn_g, w_in, rel_bias, conv_w, a_log, dt_bias, delta_norm_g, w_out, norm_ffn_g, w_gate, w_up, w_down, final_norm_g, loss_target, m_w_ada, m_b_ada, m_norm_attn_g, m_w_in, m_rel_bias, m_conv_w, m_a_log, m_dt_bias, m_delta_norm_g, m_w_out, m_norm_ffn_g, m_w_gate, m_w_up, m_w_down, m_final_norm_g, v_w_ada, v_b_ada, v_norm_attn_g, v_w_in, v_rel_bias, v_conv_w, v_a_log, v_dt_bias, v_delta_norm_g, v_w_out, v_norm_ffn_g, v_w_gate, v_w_up, v_w_down, v_final_norm_g):
    me = 4 * lax.axis_index("x") + 2 * lax.axis_index("y") + lax.axis_index("c")
    ada_sh = w_ada.shape[2]
    conv_sh = conv_w.shape[2]

    cv = jnp.concatenate([c[0], conv_w[0].reshape(-1)])
    cv8 = jnp.zeros((8, 2 * D_MODEL), f32).at[0, :cv.shape[0]].set(cv)
    b8 = jnp.broadcast_to(b_ada.reshape(N_DEV, 1, ada_sh), (N_DEV, 8, ada_sh))
    call, modp, w_in_g = _ada_exchange(cv8, w_ada[0], b8, w_in[0].astype(bf16))
    mod = modp[:, 0, :].reshape(1, 6 * D_MODEL)
    c_all = call[:, 0, :D_MODEL]
    conv_full = call[:, 0, D_MODEL:D_MODEL + CONV_WIDTH * conv_sh].reshape(N_DEV, CONV_WIDTH, conv_sh)
    conv_full = conv_full.transpose(1, 0, 2).reshape(CONV_WIDTH, N_DEV * conv_sh)

    w_in_p = jnp.pad(_cols(w_in_g), ((0, 0), (0, IN_PAD - IN_WIDTH)))
    loss_local, gr = _local_step(x[0], loss_target[0], mod, norm_attn_g, w_in_p, rel_bias, conv_full, a_log,
                                 dt_bias, delta_norm_g, norm_ffn_g, final_norm_g, w_out[0].astype(bf16),
                                 w_gate[0].astype(bf16), w_up[0].astype(bf16), w_down[0].astype(bf16))
    loss = lax.psum(loss_local, ("x", "y", "c"))

    small = jnp.concatenate([
        gr["mod"].reshape(6, D_MODEL), gr["norm_attn_g"], gr["norm_ffn_g"], gr["final_norm_g"],
        gr["conv_w"].reshape(6, D_MODEL),
        _misc_row(gr["rel_bias"], gr["a_log"], gr["dt_bias"], gr["delta_norm_g"])], axis=0)
    parts = _all_to_all([jnp.broadcast_to(small[None], (N_DEV,) + small.shape)], "small_gather")[0]
    tot = _sum_devices(parts, "small_sum")
    g_conv_full = tot[9:15].reshape(CONV_WIDTH, N_DEV * conv_sh)
    g_conv = lax.dynamic_slice(g_conv_full, (0, me * conv_sh), (CONV_WIDTH, conv_sh))
    misc = tot[15]
    g_small = _pack_small(tot[0:6], tot[6], tot[7], tot[8], misc[0:256], misc[256:264], misc[264:272],
                          misc[272:336], g_conv)
    pk = lambda pre: _pack_small(pre[0], pre[1], pre[2], pre[3], pre[4], pre[5], pre[6], pre[7], pre[8])
    w_small = pk((b_ada, norm_attn_g, norm_ffn_g, final_norm_g, rel_bias, a_log, dt_bias, delta_norm_g, conv_w))
    m_small = pk((m_b_ada, m_norm_attn_g, m_norm_ffn_g, m_final_norm_g, m_rel_bias, m_a_log, m_dt_bias,
                  m_delta_norm_g, m_conv_w))
    v_small = pk((v_b_ada, v_norm_attn_g, v_norm_ffn_g, v_final_norm_g, v_rel_bias, v_a_log, v_dt_bias,
                  v_delta_norm_g, v_conv_w))
    d_small, m2_small, v2_small = _adamw(w_small, g_small, m_small, v_small, "adamw_small")
    cshape = conv_w.shape
    G, Dl, M2, V2 = (_unpack_small(t, cshape) for t in (g_small, d_small, m2_small, v2_small))

    dmod_all = parts[:, 0:6, :].reshape(N_DEV, 6 * D_MODEL)
    dmod_cols = lax.dynamic_slice(dmod_all, (0, me * ada_sh), (N_DEV, ada_sh))
    g_ada = _ada_wgrad(c_all, dmod_cols)
    d_ada, m2_ada, v2_ada = _adamw(w_ada[0], g_ada, m_w_ada[0], v_w_ada[0], "adamw_w_ada")

    big = {}
    for name, w_, m_, v_ in (("w_in", w_in, m_w_in, v_w_in), ("w_out", w_out, m_w_out, v_w_out),
                             ("w_gate", w_gate, m_w_gate, v_w_gate), ("w_up", w_up, m_w_up, v_w_up),
                             ("w_down", w_down, m_w_down, v_w_down)):
        big[name] = [t[None] for t in _reduce_adamw(gr[name], w_[0], m_[0], v_[0], "reduce_adamw_" + name)]

    def leaf(i, name):
        if name == "w_ada":
            return (g_ada, d_ada, m2_ada, v2_ada)[i][None]
        if name in big:
            return big[name][i]
        return (G, Dl, M2, V2)[i][name]

    order = ["w_ada", "b_ada", "norm_attn_g", "w_in", "rel_bias", "conv_w", "a_log", "dt_bias", "delta_norm_g",
             "w_out", "norm_ffn_g", "w_gate", "w_up", "w_down", "final_norm_g"]
    outs = [loss, gr["x"][None]]
    for i in range(4):
        outs += [leaf(i, n) for n in order]
    return tuple(outs)
```

```python
import functools
import math

import jax
import jax.numpy as jnp
from jax import lax
from jax.experimental import pallas as pl
from jax.experimental.pallas import tpu as pltpu

f32 = jnp.float32
bf16 = jnp.bfloat16

D_MODEL = 1024
HEAD_DIM = 64
N_HEADS = 8
GROUP_W = 512
IN_WIDTH = 3600
IN_PAD = 3840
D_FF = 2816
EPS = 1e-6
NEG_INF = -1e30
BAND = 128
PAD_UNIT = 2048
DILATIONS = (1, 4, 16)
N_BUCKETS = 32
MAX_DISTANCE = 2048
CONV_WIDTH = 4
CHUNK = 64
N_DEV = 8
VMEM_LIMIT = 56 * 1024 * 1024

ADAM_LR, ADAM_B1, ADAM_B2, ADAM_EPS, ADAM_WD, ADAM_STEP = 0.001, 0.9, 0.999, 1e-08, 0.01, 10


def _cparams(sem):
    return pltpu.CompilerParams(dimension_semantics=sem, vmem_limit_bytes=VMEM_LIMIT)


def _mm(a, b, mode, out_dtype, tm, tn, tk, name, xchg=None):
    if mode == "nn":
        (m, k), (_, n) = a.shape, b.shape
        a_spec = pl.BlockSpec((tm, tk), lambda j, i, kk: (i, kk))
        b_spec = pl.BlockSpec((tk, tn), lambda j, i, kk: (kk, j))
        dims = (((1,), (0,)), ((), ()))
    elif mode == "nt":
        (m, k), (n, _) = a.shape, b.shape
        a_spec = pl.BlockSpec((tm, tk), lambda j, i, kk: (i, kk))
        b_spec = pl.BlockSpec((tn, tk), lambda j, i, kk: (j, kk))
        dims = (((1,), (1,)), ((), ()))
    else:
        (k, m), (_, n) = a.shape, b.shape
        a_spec = pl.BlockSpec((tk, tm), lambda j, i, kk: (kk, i))
        b_spec = pl.BlockSpec((tk, tn), lambda j, i, kk: (kk, j))
        dims = (((0,), (0,)), ((), ()))
    assert m % tm == 0 and n % tn == 0 and k % tk == 0, (name, m, n, k, tm, tn, tk)
    nk = k // tk
    grid = (n // tn, m // tm, nk)
    nx = xchg.n if xchg is not None else 0

    def body(*refs):
        a_ref, b_ref = refs[:2]
        o_ref = refs[2 + nx]
        scratch = refs[3 + 2 * nx:]
        if nx:
            xrefs = (refs[2:2 + nx], refs[3 + nx:3 + 2 * nx], scratch[-3:])
            xchg.start_at_first_step(grid, *xrefs)
        if nk == 1:
            o_ref[...] = lax.dot_general(a_ref[...].astype(bf16), b_ref[...].astype(bf16), dims,
                                         preferred_element_type=f32).astype(o_ref.dtype)
        else:
            acc_ref = scratch[0]
            kk = pl.program_id(2)

            @pl.when(kk == 0)
            def _():
                acc_ref[...] = jnp.zeros_like(acc_ref)

            acc_ref[...] += lax.dot_general(a_ref[...].astype(bf16), b_ref[...].astype(bf16), dims,
                                            preferred_element_type=f32)

            @pl.when(kk == nk - 1)
            def _():
                o_ref[...] = acc_ref[...].astype(o_ref.dtype)
        if nx:
            xchg.wait_at_last_step(grid, *xrefs)

    out = pl.pallas_call(
        body, name=name, grid=grid,
        in_specs=[a_spec, b_spec] + ([_ANY] * nx),
        out_specs=[pl.BlockSpec((tm, tn), lambda j, i, kk: (i, j))] + ([_ANY] * nx),
        out_shape=[jax.ShapeDtypeStruct((m, n), out_dtype)] + (xchg.out_shape() if nx else []),
        scratch_shapes=([pltpu.VMEM((tm, tn), f32)] if nk > 1 else []) + (xchg.scratch() if nx else []),
        compiler_params=_cparams(("arbitrary",) * 3 if nx else ("parallel", "parallel", "arbitrary")),
    )(a, b, *(xchg.arrs if nx else []))
    return (out[0], out[1:]) if nx else out[0]


TOK_TILE = 512


def _row_spec(width, tile=TOK_TILE):
    return pl.BlockSpec((tile, width), lambda i: (i, 0))


def _vec_spec(width, rows=1):
    return pl.BlockSpec((rows, width), lambda i: (0, 0))


def _ln_mod_fwd(x, gain, sc, sh, name):
    s, d = x.shape

    def body(x_ref, g_ref, sc_ref, sh_ref, h_ref):
        xv = x_ref[...]
        rstd = lax.rsqrt(jnp.mean(xv * xv, axis=-1, keepdims=True) + EPS)
        h = (xv * rstd) * g_ref[...] * (1.0 + sc_ref[...]) + sh_ref[...]
        h_ref[...] = h.astype(bf16)

    return pl.pallas_call(
        body, name=name, grid=(s // TOK_TILE,),
        in_specs=[_row_spec(d), _vec_spec(d), _vec_spec(d), _vec_spec(d)],
        out_specs=_row_spec(d),
        out_shape=jax.ShapeDtypeStruct((s, d), bf16),
        compiler_params=_cparams(("parallel",)),
    )(x, gain, sc, sh)


def _proj_resid_ln_mod_fwd(a, w, x, gate, gain, sc, sh, name):
    s, d = x.shape
    k = a.shape[1]

    def body(a_ref, w_ref, x_ref, gt_ref, g_ref, sc_ref, sh_ref, y_ref, x1_ref, h_ref):
        y = jnp.dot(a_ref[...], w_ref[...], preferred_element_type=f32)
        y_ref[...] = y
        x1 = x_ref[...] + gt_ref[...] * y
        x1_ref[...] = x1
        rstd = lax.rsqrt(jnp.mean(x1 * x1, axis=-1, keepdims=True) + EPS)
        h = (x1 * rstd) * g_ref[...] * (1.0 + sc_ref[...]) + sh_ref[...]
        h_ref[...] = h.astype(bf16)

    return pl.pallas_call(
        body, name=name, grid=(s // TOK_TILE,),
        in_specs=[_row_spec(k), pl.BlockSpec((k, d), lambda i: (0, 0)), _row_spec(d)] + [_vec_spec(d)] * 4,
        out_specs=[_row_spec(d)] * 3,
        out_shape=[jax.ShapeDtypeStruct((s, d), f32)] * 2 + [jax.ShapeDtypeStruct((s, d), bf16)],
        compiler_params=_cparams(("parallel",)),
    )(a, w, x, gate, gain, sc, sh)


FFN_TN = 1408


def _ffn_up(h2, w_gate, w_up, name):
    s, d = h2.shape
    tm = TOK_TILE

    def body(h_ref, wg_ref, wu_ref, a_ref, g_ref, u_ref):
        h = h_ref[...]
        g = jnp.dot(h, wg_ref[...], preferred_element_type=f32)
        u = jnp.dot(h, wu_ref[...], preferred_element_type=f32)
        a_ref[...] = (g * jax.nn.sigmoid(g) * u).astype(bf16)
        g_ref[...] = g.astype(bf16)
        u_ref[...] = u.astype(bf16)

    w_spec = pl.BlockSpec((d, FFN_TN), lambda j, i: (0, j))
    o_spec = pl.BlockSpec((tm, FFN_TN), lambda j, i: (i, j))
    return pl.pallas_call(
        body, name=name, grid=(D_FF // FFN_TN, s // tm),
        in_specs=[pl.BlockSpec((tm, d), lambda j, i: (i, 0)), w_spec, w_spec],
        out_specs=[o_spec] * 3,
        out_shape=[jax.ShapeDtypeStruct((s, D_FF), bf16)] * 3,
        compiler_params=_cparams(("parallel", "parallel")),
    )(h2, w_gate, w_up)


def _ffn_down_dx(dy2, w_down, gate, up, name):
    s, d = dy2.shape
    tm = TOK_TILE

    def body(dy_ref, w_ref, g_ref, u_ref, dg_ref, du_ref):
        da = lax.dot_general(dy_ref[...], w_ref[...], _NT, preferred_element_type=f32)
        g = g_ref[...].astype(f32)
        sg = jax.nn.sigmoid(g)
        du_ref[...] = (da * g * sg).astype(bf16)
        dg_ref[...] = (da * u_ref[...].astype(f32) * sg * (1.0 + g * (1.0 - sg))).astype(bf16)

    t_spec = pl.BlockSpec((tm, FFN_TN), lambda j, i: (i, j))
    return pl.pallas_call(
        body, name=name, grid=(D_FF // FFN_TN, s // tm),
        in_specs=[pl.BlockSpec((tm, d), lambda j, i: (i, 0)), pl.BlockSpec((FFN_TN, d), lambda j, i: (j, 0)),
                  t_spec, t_spec],
        out_specs=[t_spec, t_spec],
        out_shape=[jax.ShapeDtypeStruct((s, D_FF), bf16)] * 2,
        compiler_params=_cparams(("parallel", "parallel")),
    )(dy2, w_down, gate, up)


def _acc_spec(width):
    return pl.BlockSpec((1, width), lambda i: (0, 0))


def _proj_final_loss_bwd(a, w, x1, gate2, final_g, target, name):
    s, d = x1.shape
    k = a.shape[1]

    def body(a_ref, w_ref, x1_ref, gt_ref, fg_ref, tg_ref, dx2_ref, dy2_ref, loss_ref, dfg_ref, dgt_ref):
        @pl.when(pl.program_id(0) == 0)
        def _():
            loss_ref[...] = jnp.zeros_like(loss_ref)
            dfg_ref[...] = jnp.zeros_like(dfg_ref)
            dgt_ref[...] = jnp.zeros_like(dgt_ref)

        y2 = jnp.dot(a_ref[...], w_ref[...], preferred_element_type=f32)
        gt = gt_ref[...]
        fg = fg_ref[...]
        x2 = x1_ref[...] + gt * y2
        rstd = lax.rsqrt(jnp.mean(x2 * x2, axis=-1, keepdims=True) + EPS)
        xn = x2 * rstd
        err = xn * fg - tg_ref[...]
        row = jnp.sum(err * err, axis=-1, keepdims=True) * (0.5 / d)
        loss_ref[...] += jnp.sum(row, axis=0, keepdims=True) + jnp.zeros_like(loss_ref)
        dout = err * (1.0 / d)
        dfg_ref[...] += jnp.sum(dout * xn, axis=0, keepdims=True)
        dxn = dout * fg
        dx2 = rstd * (dxn - xn * jnp.mean(dxn * xn, axis=-1, keepdims=True))
        dx2_ref[...] = dx2
        dgt_ref[...] += jnp.sum(dx2 * y2, axis=0, keepdims=True)
        dy2_ref[...] = (gt * dx2).astype(bf16)

    return pl.pallas_call(
        body, name=name, grid=(s // TOK_TILE,),
        in_specs=[_row_spec(k), pl.BlockSpec((k, d), lambda i: (0, 0)), _row_spec(d), _vec_spec(d), _vec_spec(d),
                  _row_spec(d)],
        out_specs=[_row_spec(d), _row_spec(d), _acc_spec(128), _acc_spec(d), _acc_spec(d)],
        out_shape=[jax.ShapeDtypeStruct((s, d), f32), jax.ShapeDtypeStruct((s, d), bf16),
                   jax.ShapeDtypeStruct((1, 128), f32), jax.ShapeDtypeStruct((1, d), f32),
                   jax.ShapeDtypeStruct((1, d), f32)],
        compiler_params=_cparams(("arbitrary",)),
    )(a, w, x1, gate2, final_g, target)


def _proj_ln_mod_bwd(pairs, xin, gain, sc, dres, tm, name, xchg, gate=None, y=None):
    s, d = xin.shape
    with_gate = gate is not None
    npair = len(pairs)
    n_in = 2 * npair + (7 if with_gate else 5) - 1
    n_out = 6 if with_gate else 4

    def body(*refs):
        ab = refs[:2 * npair]
        if with_gate:
            (x_ref, g_ref, sc_ref, dr_ref, gt_ref, y_ref,
             dx_ref, dsh_ref, dsc_ref, dg_ref, dy_ref, dgt_ref) = refs[2 * npair:]
        else:
            x_ref, g_ref, sc_ref, dr_ref, dx_ref, dsh_ref, dsc_ref, dg_ref = refs[2 * npair:]

        @pl.when(pl.program_id(0) == 0)
        def _():
            dsh_ref[...] = jnp.zeros_like(dsh_ref)
            dsc_ref[...] = jnp.zeros_like(dsc_ref)
            dg_ref[...] = jnp.zeros_like(dg_ref)
            if with_gate:
                dgt_ref[...] = jnp.zeros_like(dgt_ref)

        dh = lax.dot_general(ab[0][...].astype(bf16), ab[1][...], _NT, preferred_element_type=f32)
        for t in range(1, npair):
            dh = dh + lax.dot_general(ab[2 * t][...].astype(bf16), ab[2 * t + 1][...], _NT,
                                      preferred_element_type=f32)
        xv = x_ref[...]
        g = g_ref[...]
        sc1 = 1.0 + sc_ref[...]
        rstd = lax.rsqrt(jnp.mean(xv * xv, axis=-1, keepdims=True) + EPS)
        xn = xv * rstd
        dsh_ref[...] += jnp.sum(dh, axis=0, keepdims=True)
        dsc_ref[...] += jnp.sum(dh * (xn * g), axis=0, keepdims=True)
        dg_ref[...] += jnp.sum(dh * sc1 * xn, axis=0, keepdims=True)
        dxn = dh * sc1 * g
        dx = dr_ref[...] + rstd * (dxn - xn * jnp.mean(dxn * xn, axis=-1, keepdims=True))
        dx_ref[...] = dx
        if with_gate:
            dgt_ref[...] += jnp.sum(dx * y_ref[...], axis=0, keepdims=True)
            dy_ref[...] = (gt_ref[...] * dx).astype(bf16)

    row = lambda width: pl.BlockSpec((tm, width), lambda i: (i, 0))
    in_specs, args = [], []
    for a, b in pairs:
        in_specs += [row(a.shape[1]), pl.BlockSpec(b.shape, lambda i: (0, 0))]
        args += [a, b]
    in_specs += [row(d), _vec_spec(d), _vec_spec(d), row(d)]
    args += [xin, gain, sc, dres]
    out_specs = [row(d), _acc_spec(d), _acc_spec(d), _acc_spec(d)]
    out_shape = [jax.ShapeDtypeStruct((s, d), f32)] + [jax.ShapeDtypeStruct((1, d), f32)] * 3
    if with_gate:
        in_specs += [_vec_spec(d), row(d)]
        out_specs += [row(d), _acc_spec(d)]
        out_shape += [jax.ShapeDtypeStruct((s, d), bf16), jax.ShapeDtypeStruct((1, d), f32)]
        args += [gate, y]
    grid = (s // tm,)
    out = pl.pallas_call(
        _ride(body, n_in, n_out, xchg, grid), name=name, grid=grid,
        in_specs=in_specs + [_ANY] * xchg.n, out_specs=out_specs + [_ANY] * xchg.n,
        out_shape=out_shape + xchg.out_shape(), scratch_shapes=xchg.scratch(),
        compiler_params=_cparams(("arbitrary",)),
    )(*args, *xchg.arrs)
    return out[:n_out], out[n_out:]


def _bucket_tables():
    import numpy as np
    qi = np.arange(BAND)[:, None]
    kj = np.arange(2 * BAND)[None, :]
    steps = qi + BAND - kj
    max_exact = N_BUCKETS // 2
    out = []
    for d in DILATIONS:
        dist = np.maximum(steps, 0) * d
        dist_f = np.maximum(dist, 1).astype(np.float32)
        large = max_exact + (np.log(dist_f / np.float32(max_exact)) / np.float32(math.log(MAX_DISTANCE / max_exact))
                             * np.float32(N_BUCKETS - max_exact)).astype(np.int32)
        out.append(np.where(dist < max_exact, dist, np.minimum(large, N_BUCKETS - 1)))
    return jnp.asarray(np.stack(out).astype(np.int32))


def _bias_tables(rel_bias, idx):
    def body(idx_ref, rb_ref, o_ref):
        h = pl.program_id(1)
        idxv = idx_ref[0]
        acc = jnp.zeros((BAND, 2 * BAND), f32)
        for b in range(N_BUCKETS):
            acc = jnp.where(idxv == b, rb_ref[b, h], acc)
        o_ref[0, 0] = acc

    return pl.pallas_call(
        body, name="attn_bias_tables", grid=(3, N_HEADS),
        in_specs=[pl.BlockSpec((1, BAND, 2 * BAND), lambda br, h: (br, 0, 0)),
                  pl.BlockSpec(memory_space=pltpu.SMEM)],
        out_specs=pl.BlockSpec((1, 1, BAND, 2 * BAND), lambda br, h: (br, h, 0, 0)),
        out_shape=jax.ShapeDtypeStruct((3, N_HEADS, BAND, 2 * BAND), f32),
        compiler_params=_cparams(("parallel", "parallel")),
    )(idx, rel_bias)


def _bias_grad(dbias, idx):
    def body(idx_ref, db_ref, o_ref):
        br = pl.program_id(1)

        @pl.when(br == 0)
        def _():
            o_ref[...] = jnp.zeros_like(o_ref)

        idxv = idx_ref[0]
        dbv = db_ref[0, 0]
        row = lax.broadcasted_iota(jnp.int32, (N_BUCKETS, 128), 0)
        acc = jnp.zeros((N_BUCKETS, 128), f32)
        for b in range(N_BUCKETS):
            sb = jnp.sum(jnp.sum(jnp.where(idxv == b, dbv, 0.0), axis=1, keepdims=True), axis=0, keepdims=True)
            acc = acc + jnp.where(row == b, sb, 0.0)
        o_ref[0] += acc

    return pl.pallas_call(
        body, name="attn_bias_grad", grid=(N_HEADS, 3),
        in_specs=[pl.BlockSpec((1, BAND, 2 * BAND), lambda h, br: (br, 0, 0)),
                  pl.BlockSpec((1, 1, BAND, 2 * BAND), lambda h, br: (br, h, 0, 0))],
        out_specs=pl.BlockSpec((1, N_BUCKETS, 128), lambda h, br: (h, 0, 0)),
        out_shape=jax.ShapeDtypeStruct((N_HEADS, N_BUCKETS, 128), f32),
        compiler_params=_cparams(("parallel", "arbitrary")),
    )(idx, dbias)


def _attn_masks():
    lane = lax.broadcasted_iota(jnp.int32, (BAND, 128), 1)
    m0 = lane < HEAD_DIM
    qi = lax.broadcasted_iota(jnp.int32, (BAND, 2 * BAND), 0)
    kj = lax.broadcasted_iota(jnp.int32, (BAND, 2 * BAND), 1)
    steps = qi + BAND - kj
    in_window = (steps >= 0) & (steps <= BAND)
    return m0, in_window, kj >= BAND


_NT = (((1,), (1,)), ((), ()))
_TN = (((0,), (0,)), ((), ()))
_BNN = (((2,), (1,)), ((0,), (0,)))
_BNT = (((2,), (2,)), ((0,), (0,)))
_BTN = (((1,), (1,)), ((0,), (0,)))
ATTN_GROUP = 4
ATTN_ITEMS = PAD_UNIT // BAND
Q_COL, K_COL, V_COL = 0, 4, 8


def _attn_item_rows(j, d, c, cbase):
    r = lax.rem(j, d)
    b = lax.div(j, d)
    loc = b * (d * BAND) + r
    first = jnp.logical_and(c == 0, b == 0)
    start = cbase + loc
    pstart = jnp.where(first, start, start - d * BAND)
    return loc, start, pstart, first


def _attn_fwd(proj, bias, xchg):
    s = proj.shape[0]

    def body(q_ref, k_ref, v_ref, b_ref, y_ref, lse_ref, o_s, l_s):
        c = pl.program_id(1)
        cbase = pl.multiple_of(c * PAD_UNIT, PAD_UNIT)
        m0, in_window, cur_half = _attn_masks()
        for bi, d in enumerate(DILATIONS):
            def group(jg, carry, bi=bi, d=d):
                locs, qs, ks, vs, pens = [], [], [], [], []
                for t in range(ATTN_GROUP):
                    loc, start, pstart, first = _attn_item_rows(jg * ATTN_GROUP + t, d, c, cbase)
                    locs.append(loc)
                    qs.append(q_ref[pl.ds(loc, BAND, stride=d), :])
                    ks.append(jnp.concatenate([k_ref[pl.ds(pstart, BAND, stride=d), :],
                                               k_ref[pl.ds(start, BAND, stride=d), :]], axis=0))
                    vs.append(jnp.concatenate([v_ref[pl.ds(pstart, BAND, stride=d), :],
                                               v_ref[pl.ds(start, BAND, stride=d), :]], axis=0))
                    pens.append(jnp.where(cur_half, 0.0, jnp.where(first, NEG_INF, 0.0)))
                q = jnp.stack(qs)
                kk = jnp.stack(ks + ks).astype(bf16)
                vv = jnp.stack(vs + vs).astype(bf16)
                pen = jnp.stack(pens + pens)
                qh = (jnp.concatenate([jnp.where(m0, q, 0.0), jnp.where(m0, 0.0, q)], axis=0) * 0.125).astype(bf16)
                sc = lax.dot_general(qh, kk, _BNT, preferred_element_type=f32)
                sc = (sc.reshape(2, ATTN_GROUP, BAND, 2 * BAND) + b_ref[bi][:, None]).reshape(sc.shape) + pen
                sc = jnp.where(in_window, sc, NEG_INF)
                mx = jnp.max(sc, axis=-1, keepdims=True)
                e = jnp.exp(sc - mx)
                l = jnp.sum(e, axis=-1, keepdims=True)
                o = lax.dot_general(e.astype(bf16), vv, _BNN, preferred_element_type=f32) / l
                ls = mx + jnp.log(l)
                for t in range(ATTN_GROUP):
                    rows = pl.ds(locs[t], BAND, stride=d)
                    o_s[bi, rows, :] = jnp.where(m0, o[t], o[ATTN_GROUP + t])
                    l_s[bi, rows, :] = jnp.where(m0, ls[t], ls[ATTN_GROUP + t])
                return carry

            lax.fori_loop(0, ATTN_ITEMS // ATTN_GROUP, group, 0)

        def merge(t, carry):
            rows = pl.ds(pl.multiple_of(t * 256, 256), 256)
            ls = [l_s[i, rows, :] for i in range(3)]
            mx = jnp.maximum(jnp.maximum(ls[0], ls[1]), ls[2])
            ws = [jnp.exp(l - mx) for l in ls]
            tot = ws[0] + ws[1] + ws[2]
            y = (ws[0] * o_s[0, rows, :] + ws[1] * o_s[1, rows, :] + ws[2] * o_s[2, rows, :]) / tot
            y_ref[rows, :] = y
            lse_ref[rows, :] = mx + jnp.log(tot)
            return carry

        lax.fori_loop(0, PAD_UNIT // 256, merge, 0)

    chunk = lambda col: pl.BlockSpec((PAD_UNIT, 128), lambda p, c: (c, col + p))
    full = lambda col: pl.BlockSpec((s, 128), lambda p, c: (0, col + p))
    grid = (N_HEADS // 2, s // PAD_UNIT)
    out = pl.pallas_call(
        _ride(body, 4, 2, xchg, grid), name="attn_fwd", grid=grid,
        in_specs=[chunk(Q_COL), full(K_COL), full(V_COL),
                  pl.BlockSpec((3, 2, BAND, 2 * BAND), lambda p, c: (0, p, 0, 0))] + [_ANY] * xchg.n,
        out_specs=[chunk(0), chunk(0)] + [_ANY] * xchg.n,
        out_shape=[jax.ShapeDtypeStruct((s, GROUP_W), f32)] * 2 + xchg.out_shape(),
        scratch_shapes=[pltpu.VMEM((3, PAD_UNIT, 128), f32)] * 2 + xchg.scratch(),
        compiler_params=_cparams(("arbitrary", "arbitrary")),
    )(proj, proj, proj, bias, *xchg.arrs)
    return out[:2], out[2:]


def _attn_bwd(proj, bias, y, lse, dycat):
    s = proj.shape[0]

    def body(q_ref, k_ref, v_ref, b_ref, y_ref, lse_ref, dy_ref, dq_ref, dk_ref, dv_ref, db_ref, dd_s):
        c = pl.program_id(1)
        cbase = pl.multiple_of(c * PAD_UNIT, PAD_UNIT)
        m0, in_window, cur_half = _attn_masks()

        @pl.when(c == 0)
        def _():
            dk_ref[...] = jnp.zeros_like(dk_ref)
            dv_ref[...] = jnp.zeros_like(dv_ref)
            db_ref[...] = jnp.zeros_like(db_ref)

        dq_ref[...] = jnp.zeros_like(dq_ref)

        def rowdot(t, carry):
            rows = pl.ds(pl.multiple_of(t * 256, 256), 256)
            prod = dy_ref[rows, :] * y_ref[rows, :]
            lane = lax.broadcasted_iota(jnp.int32, prod.shape, 1)
            h0 = lane < HEAD_DIM
            d0 = jnp.sum(jnp.where(h0, prod, 0.0), axis=-1, keepdims=True)
            d1 = jnp.sum(jnp.where(h0, 0.0, prod), axis=-1, keepdims=True)
            dd_s[rows, :] = jnp.where(h0, d0, d1)
            return carry

        lax.fori_loop(0, PAD_UNIT // 256, rowdot, 0)

        for bi, d in enumerate(DILATIONS):
            def group(jg, carry, bi=bi, d=d):
                ng = ATTN_GROUP
                meta, qs, dos, lqs, dds, ks, vs, pens = [], [], [], [], [], [], [], []
                for t in range(ng):
                    loc, start, pstart, first = _attn_item_rows(jg * ng + t, d, c, cbase)
                    qrows = pl.ds(loc, BAND, stride=d)
                    rows = pl.ds(start, BAND, stride=d)
                    prows = pl.ds(pstart, BAND, stride=d)
                    meta.append((qrows, rows, prows))
                    qs.append(q_ref[qrows, :])
                    dos.append(dy_ref[qrows, :])
                    lqs.append(lse_ref[qrows, :])
                    dds.append(dd_s[qrows, :])
                    ks.append(jnp.concatenate([k_ref[prows, :], k_ref[rows, :]], axis=0))
                    vs.append(jnp.concatenate([v_ref[prows, :], v_ref[rows, :]], axis=0))
                    pens.append(jnp.where(cur_half, 0.0, jnp.where(first, NEG_INF, 0.0)))

                def heads(t):
                    return jnp.concatenate([jnp.where(m0, t, 0.0), jnp.where(m0, 0.0, t)], axis=0)

                def head_col(t):
                    return jnp.concatenate([t[:, :, 0:1], t[:, :, HEAD_DIM:HEAD_DIM + 1]], axis=0)

                qh = (heads(jnp.stack(qs)) * 0.125).astype(bf16)
                doh = heads(jnp.stack(dos)).astype(bf16)
                kk = jnp.stack(ks + ks).astype(bf16)
                vv = jnp.stack(vs + vs).astype(bf16)
                sc = lax.dot_general(qh, kk, _BNT, preferred_element_type=f32)
                sc = (sc.reshape(2, ng, BAND, 2 * BAND) + b_ref[bi][:, None]).reshape(sc.shape) + jnp.stack(pens + pens)
                sc = jnp.where(in_window, sc, NEG_INF)
                p = jnp.exp(sc - head_col(jnp.stack(lqs)))
                dp = lax.dot_general(doh, vv, _BNT, preferred_element_type=f32)
                ds = p * (dp - head_col(jnp.stack(dds)))
                db_ref[bi] += jnp.sum(ds.reshape(2, ng, BAND, 2 * BAND), axis=1)
                dsb = ds.astype(bf16)
                dq = lax.dot_general(dsb, kk, _BNN, preferred_element_type=f32) * 0.125
                dk = lax.dot_general(dsb, qh, _BTN, preferred_element_type=f32)
                dv = lax.dot_general(p.astype(bf16), doh, _BTN, preferred_element_type=f32)
                for t in range(ng):
                    qrows, rows, prows = meta[t]
                    dq_ref[qrows, :] += jnp.where(m0, dq[t], dq[ng + t])
                    dkt = dk[t] + dk[ng + t]
                    dvt = dv[t] + dv[ng + t]
                    dk_ref[prows, :] += dkt[:BAND]
                    dk_ref[rows, :] += dkt[BAND:]
                    dv_ref[prows, :] += dvt[:BAND]
                    dv_ref[rows, :] += dvt[BAND:]
                return carry

            lax.fori_loop(0, ATTN_ITEMS // ATTN_GROUP, group, 0)

    chunk = lambda col: pl.BlockSpec((PAD_UNIT, 128), lambda p, c: (c, col + p))
    full = lambda col: pl.BlockSpec((s, 128), lambda p, c: (0, col + p))
    bias_spec = pl.BlockSpec((3, 2, BAND, 2 * BAND), lambda p, c: (0, p, 0, 0))
    return pl.pallas_call(
        body, name="attn_bwd", grid=(N_HEADS // 2, s // PAD_UNIT),
        in_specs=[chunk(Q_COL), full(K_COL), full(V_COL), bias_spec, chunk(0), chunk(0), chunk(0)],
        out_specs=[chunk(0), full(0), full(0), bias_spec],
        out_shape=[jax.ShapeDtypeStruct((s, GROUP_W), f32)] * 3
        + [jax.ShapeDtypeStruct((3, N_HEADS, BAND, 2 * BAND), f32)],
        scratch_shapes=[pltpu.VMEM((PAD_UNIT, 128), f32)],
        compiler_params=_cparams(("parallel", "arbitrary")),
    )(proj, proj, proj, bias, y, lse, dycat)


_HI = lax.Precision.HIGHEST
DELTA_COL = 1536
Z_COL = 3072
BA_BLOCK = 28
DELTA_ROWS = 512


def _hdot(a, b):
    return jnp.dot(a, b, precision=_HI, preferred_element_type=f32)


_DIMS = dict(nn=(((2,), (1,)), ((0,), (0,))), nt=(((2,), (2,)), ((0,), (0,))), tn=(((1,), (1,)), ((0,), (0,))))


@functools.partial(jax.custom_vjp, nondiff_argnums=(2,))
def _mmx(a, b, mode):
    return lax.dot_general(a.astype(bf16), b.astype(bf16), _DIMS[mode], preferred_element_type=f32)


def _mmx_fwd(a, b, mode):
    return _mmx(a, b, mode), (a, b)


def _mmx_bwd(mode, res, g):
    a, b = res
    if mode == "nn":
        return _mmx(g, b, "nt"), _mmx(a, g, "tn")
    if mode == "nt":
        return _mmx(g, b, "nn"), _mmx(g, a, "tn")
    return _mmx(b, g, "nt"), _mmx(a, g, "nn")


_mmx.defvjp(_mmx_fwd, _mmx_bwd)


def _pair_iota():
    row = lax.broadcasted_iota(jnp.int32, (CHUNK, 128), 0)
    lane = lax.broadcasted_iota(jnp.int32, (CHUNK, 128), 1)
    return row, lane, lane & (CHUNK - 1)


def _bd(x):
    _, lane, _ = _pair_iota()
    m0 = lane < CHUNK
    return jnp.concatenate([jnp.where(m0, x, 0.0), jnp.where(m0, 0.0, x)], axis=1)


def _pmm(a, b):
    return _mmx(a, _bd(b), "nn")


def _ntp(x, y):
    return _mmx(x, _bd(y), "nt")


def _tnp(x, y):
    full = _mmx(x, y, "tn")
    _, lane, _ = _pair_iota()
    return jnp.where(lane < CHUNK, full[:, :CHUNK], full[:, CHUNK:])


def _tri_inv(a):
    row, lane, jj = _pair_iota()
    eye = jnp.where(row == jj, 1.0, 0.0).astype(f32)

    def same_block(log2b):
        return (row >> log2b) == (jj >> log2b)

    dgl = jnp.where(same_block(3), a, 0.0)
    d2 = _pmm(dgl, dgl)
    d4 = _pmm(d2, d2)
    t = _pmm(_pmm(eye - dgl, eye + d2), eye + d4)
    for lb in (3, 4, 5):
        off = jnp.where(same_block(lb + 1) & jnp.logical_not(same_block(lb)), a, 0.0)
        t = t - _pmm(_pmm(t, off), t)
    return t


@jax.custom_vjp
def _solve2(a, xv, xk, t):
    return _pmm(t, xv), _pmm(t, xk)


def _solve2_fwd(a, xv, xk, t):
    u, w = _pmm(t, xv), _pmm(t, xk)
    return (u, w), (t, u, w)


def _solve2_bwd(res, cts):
    t, u, w = res
    du, dw = cts
    dxv = _tnp(t, du)
    dxk = _tnp(t, dw)
    return -(_ntp(dxv, u) + _ntp(dxk, w)), dxv, dxk, jnp.zeros_like(t)


_solve2.defvjp(_solve2_fwd, _solve2_bwd)


def _chunk_pre(qp, kp, vp, bp, gcum, t=None):
    row, lane, jj = _pair_iota()
    causal = row >= jj
    strict = row > jj
    rsel = jnp.sum(jnp.where(row == jj, gcum, 0.0), axis=1, keepdims=True)
    decay = jnp.where(causal, jnp.exp(jnp.where(causal, gcum - rsel, 0.0)), 0.0)
    kb = kp * bp
    kd = _bd(kp)
    a = jnp.where(strict, _mmx(kb, kd, "nt") * decay, 0.0)
    eg = jnp.exp(gcum)
    if t is None:
        t = _tri_inv(a)
    u, w = _solve2(a, vp * bp, kb * eg, t)
    qk = jnp.where(causal, _mmx(qp, kd, "nt") * decay, 0.0)
    glast = jnp.sum(jnp.where(row == CHUNK - 1, gcum, 0.0), axis=1, keepdims=True)
    return u, w, qp * eg, kp * jnp.exp(glast - gcum), qk, jnp.exp(glast), t


def _chunk_post(u, w, qt, kh, qk, gam, sp):
    sd = _bd(sp)
    vnew = u - _mmx(w, sd, "nn")
    o = _mmx(qt, sd, "nn") + _pmm(qk, vnew)
    return o, gam * sp + _tnp(kh, vnew)


def _pair_spec(rows=DELTA_ROWS):
    return pl.BlockSpec((rows, 128), lambda i, p: (i, p))


DELTA_NB = DELTA_ROWS // CHUNK


def _chunks(ref):
    return ref[...].reshape(DELTA_NB, CHUNK, 128)


def _pairs(ref, rows):
    return jnp.stack([ref[rows, p * 128:(p + 1) * 128] for p in range(4)], axis=0)


def _delta_chunk_pre(qn, kn, sv, beta, g):
    s = qn.shape[0]

    def body(q_ref, k_ref, v_ref, b_ref, g_ref, u_ref, w_ref, qt_ref, kh_ref, qk_ref, t_ref, gm_ref):
        outs = _chunk_pre(_chunks(q_ref), _chunks(k_ref), _chunks(v_ref), _chunks(b_ref), _chunks(g_ref))
        for ref, val in zip((u_ref, w_ref, qt_ref, kh_ref, qk_ref, t_ref), outs[:5] + outs[6:]):
            ref[...] = val.reshape(DELTA_ROWS, 128).astype(ref.dtype)
        gm_ref[...] = jnp.broadcast_to(outs[5], (DELTA_NB, 8, 128)).reshape(DELTA_NB * 8, 128)

    v_spec = pl.BlockSpec((DELTA_ROWS, 128), lambda i, p: (i, 8 + p))
    return pl.pallas_call(
        body, name="delta_chunk_pre", grid=(s // DELTA_ROWS, 4),
        in_specs=[_pair_spec(), _pair_spec(), v_spec, _pair_spec(), _pair_spec()],
        out_specs=[_pair_spec()] * 6 + [_pair_spec(DELTA_NB * 8)],
        out_shape=[jax.ShapeDtypeStruct((s, GROUP_W), f32)] + [jax.ShapeDtypeStruct((s, GROUP_W), bf16)] * 5
        + [jax.ShapeDtypeStruct((s // 8, GROUP_W), f32)],
        compiler_params=_cparams(("parallel", "parallel")),
    )(qn, kn, sv, beta, g)


def _delta_scan_fwd(u, w, qt, kh, qk, gm):
    s = u.shape[0]

    def body(u_ref, w_ref, qt_ref, kh_ref, qk_ref, gm_ref, o_ref, ss_ref, st):
        @pl.when(pl.program_id(0) == 0)
        def _():
            st[...] = jnp.zeros_like(st)

        def chunk(ci, carry):
            rows = pl.ds(pl.multiple_of(ci * CHUNK, CHUNK), CHUNK)
            grow = pl.ds(pl.multiple_of(ci * 8, 8), 1)
            sp = st[...]
            o, s2 = _chunk_post(_pairs(u_ref, rows), _pairs(w_ref, rows), _pairs(qt_ref, rows),
                                _pairs(kh_ref, rows), _pairs(qk_ref, rows), _pairs(gm_ref, grow), sp)
            for p in range(4):
                ss_ref[rows, p * 128:(p + 1) * 128] = sp[p]
                o_ref[rows, p * 128:(p + 1) * 128] = o[p]
            st[...] = s2
            return carry

        lax.fori_loop(0, DELTA_NB, chunk, 0)

    spec = pl.BlockSpec((DELTA_ROWS, GROUP_W), lambda i: (i, 0))
    gspec = pl.BlockSpec((DELTA_NB * 8, GROUP_W), lambda i: (i, 0))
    return pl.pallas_call(
        body, name="delta_scan_fwd", grid=(s // DELTA_ROWS,),
        in_specs=[spec] * 5 + [gspec],
        out_specs=[spec, spec],
        out_shape=[jax.ShapeDtypeStruct((s, GROUP_W), f32)] * 2,
        scratch_shapes=[pltpu.VMEM((4, CHUNK, 128), f32)],
        compiler_params=_cparams(("arbitrary",)),
    )(u, w, qt, kh, qk, gm)


def _delta_scan_bwd(w, qt, kh, qk, gm, do):
    s = w.shape[0]
    nb = s // DELTA_ROWS

    def body(w_ref, qt_ref, kh_ref, qk_ref, gm_ref, do_ref, dso_ref, dst):
        @pl.when(pl.program_id(0) == 0)
        def _():
            dst[...] = jnp.zeros_like(dst)

        def chunk(t, carry):
            ci = DELTA_NB - 1 - t
            rows = pl.ds(pl.multiple_of(ci * CHUNK, CHUNK), CHUNK)
            grow = pl.ds(pl.multiple_of(ci * 8, 8), 1)
            ds = dst[...]
            for p in range(4):
                dso_ref[rows, p * 128:(p + 1) * 128] = ds[p]
            do = _pairs(do_ref, rows)
            dvn = _tnp(_pairs(qk_ref, rows), do) + _pmm(_pairs(kh_ref, rows), ds)
            dst[...] = _tnp(_pairs(qt_ref, rows), do) + _pairs(gm_ref, grow) * ds - _tnp(_pairs(w_ref, rows), dvn)
            return carry

        lax.fori_loop(0, DELTA_NB, chunk, 0)

    spec = pl.BlockSpec((DELTA_ROWS, GROUP_W), lambda i: (nb - 1 - i, 0))
    gspec = pl.BlockSpec((DELTA_NB * 8, GROUP_W), lambda i: (nb - 1 - i, 0))
    return pl.pallas_call(
        body, name="delta_scan_bwd", grid=(nb,),
        in_specs=[spec] * 4 + [gspec, spec],
        out_specs=spec,
        out_shape=jax.ShapeDtypeStruct((s, GROUP_W), f32),
        scratch_shapes=[pltpu.VMEM((4, CHUNK, 128), f32)],
        compiler_params=_cparams(("arbitrary",)),
    )(w, qt, kh, qk, gm, do)


def _delta_chunk_bwd(qn, kn, sv, beta, g, tinv, ss, dso, do):
    s = qn.shape[0]

    def body(q_ref, k_ref, v_ref, b_ref, g_ref, t_ref, ss_ref, dso_ref, do_ref,
             dq_ref, dk_ref, dv_ref, db_ref, dg_ref):
        sp = _chunks(ss_ref)
        t = _chunks(t_ref)

        def fn(q, k, v, b, gg):
            return _chunk_post(*_chunk_pre(q, k, v, b, gg, t)[:6], sp)

        _, vjp = jax.vjp(fn, _chunks(q_ref), _chunks(k_ref), _chunks(v_ref), _chunks(b_ref), _chunks(g_ref))
        grads = vjp((_chunks(do_ref), _chunks(dso_ref)))
        for ref, val in zip((dq_ref, dk_ref, dv_ref, db_ref, dg_ref), grads):
            ref[...] = val.reshape(DELTA_ROWS, 128)

    v_spec = pl.BlockSpec((DELTA_ROWS, 128), lambda i, p: (i, 8 + p))
    return pl.pallas_call(
        body, name="delta_chunk_bwd", grid=(s // DELTA_ROWS, 4),
        in_specs=[_pair_spec(), _pair_spec(), v_spec] + [_pair_spec()] * 6,
        out_specs=[_pair_spec()] * 5,
        out_shape=[jax.ShapeDtypeStruct((s, GROUP_W), f32)] * 5,
        compiler_params=_cparams(("parallel", "parallel")),
    )(qn, kn, sv, beta, g, tinv, ss, dso, do)


def _head_sum_matrix():
    r = lax.broadcasted_iota(jnp.int32, (GROUP_W, GROUP_W), 0)
    c = lax.broadcasted_iota(jnp.int32, (GROUP_W, GROUP_W), 1)
    return jnp.where((r >> 6) == (c >> 6), 1.0, 0.0).astype(f32)


def _head_sums(x):
    return _mmx(x[None], _head_sum_matrix()[None], "nn")[0]


def _sel_dot(a, b):
    return jnp.dot(a, b, precision=lax.Precision.HIGH, preferred_element_type=f32)


def _softplus(x):
    return jnp.maximum(x, 0.0) + jnp.log(1.0 + jnp.exp(-jnp.abs(x)))


def _prep_fn(sq, sk, ba, alog_e, dt_e):
    qn = sq * lax.rsqrt(_head_sums(sq * sq) + EPS) * (HEAD_DIM ** -0.5)
    kn = sk * lax.rsqrt(_head_sums(sk * sk) + EPS)
    r = lax.broadcasted_iota(jnp.int32, (128, GROUP_W), 0)
    c = lax.broadcasted_iota(jnp.int32, (128, GROUP_W), 1) >> 6
    bl = _sel_dot(ba, jnp.where(r == c, 1.0, 0.0).astype(f32))
    al = _sel_dot(ba, jnp.where(r == c + N_HEADS, 1.0, 0.0).astype(f32))
    beta = jax.nn.sigmoid(bl)
    g = -jnp.exp(alog_e) * _softplus(al + dt_e)
    ri = lax.broadcasted_iota(jnp.int32, (TOK_TILE, TOK_TILE), 0)
    ci = lax.broadcasted_iota(jnp.int32, (TOK_TILE, TOK_TILE), 1)
    within = jnp.where(((ri >> 6) == (ci >> 6)) & (ri >= ci), 1.0, 0.0).astype(f32)
    return qn, kn, beta, _sel_dot(within, g)


def _gnorm_fn(o, z, ng_e):
    ms = _head_sums(o * o) * (1.0 / HEAD_DIM)
    return o * lax.rsqrt(ms + EPS) * ng_e * (z * jax.nn.sigmoid(z))


def _tok_spec(width, col):
    return pl.BlockSpec((TOK_TILE, width), lambda i: (i, col))


def _conv_taps(xs_ref, w_ref, base, n):
    acc = w_ref[CONV_WIDTH - 1:CONV_WIDTH, :] * xs_ref[pl.ds(base, n), :]
    for j in range(CONV_WIDTH - 1):
        acc = acc + w_ref[j:j + 1, :] * xs_ref[pl.ds(base - (CONV_WIDTH - 1) + j, n), :]
    return acc


def _conv_silu_fwd(proj, conv_w):
    s = proj.shape[0]
    wd = 3 * GROUP_W
    hb = TOK_TILE // 8

    def body(x_ref, halo_ref, w_ref, o_ref, xs):
        xs[0:8, :] = jnp.where(pl.program_id(0) > 0, halo_ref[...], 0.0)
        xs[8:, :] = x_ref[...]
        y = _conv_taps(xs, w_ref, 8, TOK_TILE)
        o_ref[...] = y * jax.nn.sigmoid(y)

    return pl.pallas_call(
        body, name="delta_conv_fwd", grid=(s // TOK_TILE,),
        in_specs=[_tok_spec(wd, 1), pl.BlockSpec((8, wd), lambda i: (jnp.maximum(i * hb - 1, 0), 1)),
                  pl.BlockSpec((CONV_WIDTH, wd), lambda i: (0, 0))],
        out_specs=_tok_spec(wd, 0),
        out_shape=jax.ShapeDtypeStruct((s, wd), f32),
        scratch_shapes=[pltpu.VMEM((TOK_TILE + 8, wd), f32)],
        compiler_params=_cparams(("parallel",)),
    )(proj, proj, conv_w)


def _conv_silu_bwd(proj, conv_w, ds3, xchg):
    s = proj.shape[0]
    wd = 3 * GROUP_W
    hb = TOK_TILE // 8
    nt = s // TOK_TILE

    def body(x_ref, hp_ref, hn_ref, dq_ref, dk_ref, dv_ref, dqn_ref, dkn_ref, dvn_ref, w_ref, dx_ref, dw_ref, xs, dys):
        i = pl.program_id(0)

        @pl.when(i == 0)
        def _():
            dw_ref[...] = jnp.zeros_like(dw_ref)

        last = i == nt - 1
        xs[0:8, :] = jnp.where(i > 0, hp_ref[...], 0.0)
        xs[8:8 + TOK_TILE, :] = x_ref[...]
        xs[8 + TOK_TILE:, :] = jnp.where(last, 0.0, hn_ref[...])
        y = _conv_taps(xs, w_ref, 8, TOK_TILE)
        sg = jax.nn.sigmoid(y)
        dsilu = sg * (1.0 + y * (1.0 - sg))
        yn = _conv_taps(xs, w_ref, 8 + TOK_TILE, 8)
        sgn = jax.nn.sigmoid(yn)
        dsilu_n = sgn * (1.0 + yn * (1.0 - sgn))
        for t, (cur, nxt) in enumerate(((dq_ref, dqn_ref), (dk_ref, dkn_ref), (dv_ref, dvn_ref))):
            cols = slice(t * GROUP_W, (t + 1) * GROUP_W)
            dys[0:TOK_TILE, cols] = cur[...] * dsilu[:, cols]
            dys[TOK_TILE:, cols] = jnp.where(last, 0.0, nxt[...]) * dsilu_n[:, cols]
        dy0 = dys[0:TOK_TILE, :]
        dx = w_ref[CONV_WIDTH - 1:CONV_WIDTH, :] * dy0
        for j in range(CONV_WIDTH - 1):
            dx = dx + w_ref[j:j + 1, :] * dys[pl.ds(CONV_WIDTH - 1 - j, TOK_TILE), :]
        dx_ref[...] = dx.astype(dx_ref.dtype)
        for j in range(CONV_WIDTH):
            dw_ref[j:j + 1, :] += jnp.sum(dy0 * xs[pl.ds(8 - (CONV_WIDTH - 1) + j, TOK_TILE), :],
                                          axis=0, keepdims=True)

    prev8 = lambda col: pl.BlockSpec((8, wd), lambda i: (jnp.maximum(i * hb - 1, 0), col))
    next8 = lambda col: pl.BlockSpec((8, wd), lambda i: (jnp.minimum((i + 1) * hb, s // 8 - 1), col))
    next8_third = pl.BlockSpec((8, GROUP_W), lambda i: (jnp.minimum((i + 1) * hb, s // 8 - 1), 0))
    out = pl.pallas_call(
        _ride(body, 10, 2, xchg, (nt,)), name="delta_conv_bwd", grid=(nt,),
        in_specs=[_tok_spec(wd, 1), prev8(1), next8(1)] + [_tok_spec(GROUP_W, 0)] * 3 + [next8_third] * 3
        + [pl.BlockSpec((CONV_WIDTH, wd), lambda i: (0, 0))] + [_ANY] * xchg.n,
        out_specs=[_tok_spec(wd, 0), pl.BlockSpec((CONV_WIDTH, wd), lambda i: (0, 0))] + [_ANY] * xchg.n,
        out_shape=[jax.ShapeDtypeStruct((s, wd), bf16), jax.ShapeDtypeStruct((CONV_WIDTH, wd), f32)] + xchg.out_shape(),
        scratch_shapes=[pltpu.VMEM((TOK_TILE + 16, wd), f32), pltpu.VMEM((TOK_TILE + 8, wd), f32)] + xchg.scratch(),
        compiler_params=_cparams(("arbitrary",)),
    )(proj, proj, proj, *ds3, *ds3, conv_w, *xchg.arrs)
    return out[:2], out[2:]


def _delta_prep_fwd(sconv, proj, alog_e, dt_e):
    s = sconv.shape[0]

    def body(sq_ref, sk_ref, ba_ref, al_ref, dt_ref, q_ref, k_ref, b_ref, g_ref):
        qn, kn, beta, g = _prep_fn(sq_ref[...], sk_ref[...], ba_ref[...], al_ref[...], dt_ref[...])
        q_ref[...] = qn
        k_ref[...] = kn
        b_ref[...] = beta
        g_ref[...] = g

    return pl.pallas_call(
        body, name="delta_prep_fwd", grid=(s // TOK_TILE,),
        in_specs=[_tok_spec(GROUP_W, 0), _tok_spec(GROUP_W, 1), _tok_spec(128, BA_BLOCK),
                  _vec_spec(GROUP_W), _vec_spec(GROUP_W)],
        out_specs=[_tok_spec(GROUP_W, 0)] * 4,
        out_shape=[jax.ShapeDtypeStruct((s, GROUP_W), f32)] * 4,
        compiler_params=_cparams(("parallel",)),
    )(sconv, sconv, proj, alog_e, dt_e)


def _delta_prep_bwd(sconv, proj, alog_e, dt_e, dqn, dkn, dbeta, dg):
    s = sconv.shape[0]

    def body(sq_ref, sk_ref, ba_ref, al_ref, dt_ref, dq_ref, dk_ref, db_ref, dg_ref,
             dsq_ref, dsk_ref, dba_ref, dal_ref, ddt_ref):
        @pl.when(pl.program_id(0) == 0)
        def _():
            dal_ref[...] = jnp.zeros_like(dal_ref)
            ddt_ref[...] = jnp.zeros_like(ddt_ref)

        _, vjp = jax.vjp(_prep_fn, sq_ref[...], sk_ref[...], ba_ref[...], al_ref[...], dt_ref[...])
        dsq, dsk, dba, dal, ddt = vjp((dq_ref[...], dk_ref[...], db_ref[...], dg_ref[...]))
        dsq_ref[...] = dsq
        dsk_ref[...] = dsk
        dba_ref[...] = dba.astype(bf16)
        dal_ref[...] += dal
        ddt_ref[...] += ddt

    return pl.pallas_call(
        body, name="delta_prep_bwd", grid=(s // TOK_TILE,),
        in_specs=[_tok_spec(GROUP_W, 0), _tok_spec(GROUP_W, 1), _tok_spec(128, BA_BLOCK),
                  _vec_spec(GROUP_W), _vec_spec(GROUP_W)] + [_tok_spec(GROUP_W, 0)] * 4,
        out_specs=[_tok_spec(GROUP_W, 0), _tok_spec(GROUP_W, 0), _tok_spec(128, 0),
                   _acc_spec(GROUP_W), _acc_spec(GROUP_W)],
        out_shape=[jax.ShapeDtypeStruct((s, GROUP_W), f32)] * 2 + [jax.ShapeDtypeStruct((s, 128), bf16)]
        + [jax.ShapeDtypeStruct((1, GROUP_W), f32)] * 2,
        compiler_params=_cparams(("arbitrary",)),
    )(sconv, sconv, proj, alog_e, dt_e, dqn, dkn, dbeta, dg)


def _gnorm_fwd(o, proj, ng_e):
    s = o.shape[0]

    def body(o_ref, z_ref, g_ref, y_ref):
        y_ref[...] = _gnorm_fn(o_ref[...], z_ref[...], g_ref[...])

    return pl.pallas_call(
        body, name="delta_gnorm_fwd", grid=(s // TOK_TILE,),
        in_specs=[_tok_spec(GROUP_W, 0), _tok_spec(GROUP_W, Z_COL // GROUP_W), _vec_spec(GROUP_W)],
        out_specs=_tok_spec(GROUP_W, 0),
        out_shape=jax.ShapeDtypeStruct((s, GROUP_W), f32),
        compiler_params=_cparams(("parallel",)),
    )(o, proj, ng_e)


def _gnorm_bwd(o, proj, ng_e, dycat):
    s = o.shape[0]

    def body(o_ref, z_ref, g_ref, dy_ref, do_ref, dz_ref, dg_ref):
        @pl.when(pl.program_id(0) == 0)
        def _():
            dg_ref[...] = jnp.zeros_like(dg_ref)

        _, vjp = jax.vjp(_gnorm_fn, o_ref[...], z_ref[...], g_ref[...])
        do, dz, dg = vjp(dy_ref[...])
        do_ref[...] = do
        dz_ref[...] = dz.astype(bf16)
        dg_ref[...] += dg

    return pl.pallas_call(
        body, name="delta_gnorm_bwd", grid=(s // TOK_TILE,),
        in_specs=[_tok_spec(GROUP_W, 0), _tok_spec(GROUP_W, Z_COL // GROUP_W), _vec_spec(GROUP_W),
                  _tok_spec(GROUP_W, 1)],
        out_specs=[_tok_spec(GROUP_W, 0), _tok_spec(GROUP_W, 0), _acc_spec(GROUP_W)],
        out_shape=[jax.ShapeDtypeStruct((s, GROUP_W), f32), jax.ShapeDtypeStruct((s, GROUP_W), bf16),
                   jax.ShapeDtypeStruct((1, GROUP_W), f32)],
        compiler_params=_cparams(("arbitrary",)),
    )(o, proj, ng_e, dycat)


_MESH = pl.DeviceIdType.MESH
_ANY = pl.BlockSpec(memory_space=pl.ANY)
_VMEM = pl.BlockSpec(memory_space=pltpu.VMEM)


def _my_place():
    x, y, c = lax.axis_index("x"), lax.axis_index("y"), lax.axis_index("c")
    return x, y, c, 4 * x + 2 * y + c


def _peer(k, x, y, c):
    px = 1 - x if k & 4 else x
    py = 1 - y if k & 2 else y
    pc = 1 - c if k & 1 else c
    return (px, py, pc), 4 * px + 2 * py + pc


def _exchange_all(src_of_peer, dst_ref, send_sems, recv_sems, x, y, c, me):
    sent = []
    for k in range(1, N_DEV):
        dev, pidx = _peer(k, x, y, c)
        cp = pltpu.make_async_remote_copy(src_ref=src_of_peer(pidx), dst_ref=dst_ref.at[me],
                                          send_sem=send_sems.at[k - 1], recv_sem=recv_sems.at[k - 1],
                                          device_id=dev, device_id_type=_MESH)
        cp.start()
        sent.append(cp)
    for k in range(1, N_DEV):
        dev, pidx = _peer(k, x, y, c)
        pltpu.make_async_remote_copy(src_ref=src_of_peer(pidx), dst_ref=dst_ref.at[pidx],
                                     send_sem=send_sems.at[k - 1], recv_sem=recv_sems.at[k - 1],
                                     device_id=dev, device_id_type=_MESH).wait_recv()
    for cp in sent:
        cp.wait_send()


def _ada_exchange(cv8, w_ada, b_ada8, w_in_sh):
    ride = _ChipGather(w_in_sh)

    def body(cv_ref, w_ref, b_ref, wi_ref, call_ref, modp_ref, wig_ref, part_s, s1, r1, s2, r2, *ride_sems):
        ride.start(wi_ref, wig_ref, ride_sems)
        x, y, c, me = _my_place()
        call_ref[me] = cv_ref[...]
        _exchange_all(lambda pidx: cv_ref, call_ref, s1, r1, x, y, c, me)
        bias = b_ref[me]
        for j in range(N_DEV):
            cj = call_ref[j][:, :D_MODEL]
            part_s[j] = _hdot(cj * jax.nn.sigmoid(cj), w_ref[...]) + bias
        modp_ref[me] = part_s[me]
        ride.forward(wi_ref, wig_ref, ride_sems)
        _exchange_all(lambda pidx: part_s.at[pidx], modp_ref, s2, r2, x, y, c, me)
        ride.finish(wi_ref, wig_ref, ride_sems)

    nsh = w_ada.shape[1]
    return pl.pallas_call(
        body, name="ada_exchange",
        in_specs=[_VMEM, _VMEM, _VMEM, _ANY], out_specs=[_VMEM, _VMEM, _ANY],
        out_shape=[jax.ShapeDtypeStruct((N_DEV, 8, cv8.shape[1]), f32), jax.ShapeDtypeStruct((N_DEV, 8, nsh), f32)]
        + [ride.out_shape()],
        scratch_shapes=[pltpu.VMEM((N_DEV, 8, nsh), f32)] + [pltpu.SemaphoreType.DMA((N_DEV - 1,))] * 4
        + ride.scratch(),
        compiler_params=pltpu.CompilerParams(vmem_limit_bytes=VMEM_LIMIT),
    )(cv8, w_ada, b_ada8, w_in_sh)


def _all_to_all(arrs, name):
    ex = _Exchange(arrs, gather=False)

    def body(*refs):
        srcs, dsts, sems = refs[:ex.n], refs[ex.n:2 * ex.n], refs[2 * ex.n:]
        ex.start(srcs, dsts, sems)
        ex.wait(srcs, dsts, sems)

    return pl.pallas_call(
        body, name=name,
        in_specs=[_ANY] * ex.n, out_specs=[_ANY] * ex.n,
        out_shape=ex.out_shape(), scratch_shapes=ex.scratch(),
    )(*arrs)


class _Exchange:
    def __init__(self, arrs, gather):
        self.arrs, self.gather, self.n = list(arrs), gather, len(arrs)

    def out_shape(self):
        return [jax.ShapeDtypeStruct(((N_DEV,) + a.shape) if self.gather else a.shape, a.dtype) for a in self.arrs]

    def scratch(self):
        if self.n == 0:
            return []
        return [pltpu.SemaphoreType.DMA((self.n, N_DEV - 1)), pltpu.SemaphoreType.DMA((self.n, N_DEV - 1)),
                pltpu.SemaphoreType.DMA((self.n,))]

    def _src(self, srcs, a, idx):
        return srcs[a] if self.gather else srcs[a].at[idx]

    def _copies(self, srcs, dsts, sems, incoming):
        send_sems, recv_sems, _ = sems
        x, y, c, me = _my_place()
        out = []
        for a in range(self.n):
            for k in range(1, N_DEV):
                dev, pidx = _peer(k, x, y, c)
                out.append(pltpu.make_async_remote_copy(
                    src_ref=self._src(srcs, a, pidx), dst_ref=dsts[a].at[pidx if incoming else me],
                    send_sem=send_sems.at[a, k - 1], recv_sem=recv_sems.at[a, k - 1],
                    device_id=dev, device_id_type=_MESH))
        return out

    def _local(self, srcs, dsts, sems):
        me = _my_place()[3]
        return [pltpu.make_async_copy(self._src(srcs, a, me), dsts[a].at[me], sems[2].at[a]) for a in range(self.n)]

    def start(self, srcs, dsts, sems):
        for cp in self._local(srcs, dsts, sems) + self._copies(srcs, dsts, sems, incoming=False):
            cp.start()

    def wait(self, srcs, dsts, sems):
        for cp in self._copies(srcs, dsts, sems, incoming=True):
            cp.wait_recv()
        for cp in self._copies(srcs, dsts, sems, incoming=False):
            cp.wait_send()
        for cp in self._local(srcs, dsts, sems):
            cp.wait()

    def start_at_first_step(self, grid, srcs, dsts, sems):
        first = functools.reduce(jnp.logical_and, [pl.program_id(i) == 0 for i in range(len(grid))])
        pl.when(first)(lambda: self.start(srcs, dsts, sems))

    def wait_at_last_step(self, grid, srcs, dsts, sems):
        last = functools.reduce(jnp.logical_and, [pl.program_id(i) == g - 1 for i, g in enumerate(grid)])
        pl.when(last)(lambda: self.wait(srcs, dsts, sems))


class _ChipGather:
    def __init__(self, shard):
        self.shard = shard

    def out_shape(self):
        return jax.ShapeDtypeStruct((N_DEV,) + self.shard.shape, self.shard.dtype)

    def scratch(self):
        return [pltpu.SemaphoreType.DMA((N_DEV - 1,)), pltpu.SemaphoreType.DMA((N_DEV - 1,)),
                pltpu.SemaphoreType.DMA(())]

    def _place(self):
        x, y, c, me = _my_place()
        return x, y, c, me, (x, y, 1 - c), [(1 - x, y), (x, 1 - y), (1 - x, 1 - y)]

    def _copy(self, out, sems, k, block, to, src=None):
        rows = out.at[4 * block[0] + 2 * block[1] + block[2]]
        return pltpu.make_async_remote_copy(src_ref=rows if src is None else src, dst_ref=rows,
                                            send_sem=sems[0].at[k], recv_sem=sems[1].at[k],
                                            device_id=to, device_id_type=_MESH)

    def start(self, src, out, sems):
        x, y, c, me, sib, chips = self._place()
        pltpu.make_async_copy(src, out.at[me], sems[2]).start()
        self._copy(out, sems, 0, (x, y, c), sib, src=src).start()
        for j, chip in enumerate(chips):
            self._copy(out, sems, 1 + j, (x, y, c), (*chip, c), src=src).start()

    def forward(self, src, out, sems):
        x, y, c, me, sib, chips = self._place()
        for j, chip in enumerate(chips):
            self._copy(out, sems, 1 + j, (*chip, c), (x, y, c)).wait_recv()
            self._copy(out, sems, 4 + j, (*chip, c), sib).start()

    def finish(self, src, out, sems):
        x, y, c, me, sib, chips = self._place()
        self._copy(out, sems, 0, (x, y, 1 - c), (x, y, c)).wait_recv()
        for j, chip in enumerate(chips):
            self._copy(out, sems, 4 + j, (*chip, 1 - c), (x, y, c)).wait_recv()
        self._copy(out, sems, 0, (x, y, c), sib, src=src).wait_send()
        for j, chip in enumerate(chips):
            self._copy(out, sems, 1 + j, (x, y, c), (*chip, c), src=src).wait_send()
            self._copy(out, sems, 4 + j, (*chip, c), sib).wait_send()
        pltpu.make_async_copy(src, out.at[me], sems[2]).wait()


def _ride(body, n_in, n_out, xchg, grid):
    nx = xchg.n
    if nx == 0:
        return body

    def wrapped(*refs):
        ins, xs = refs[:n_in], refs[n_in:n_in + nx]
        outs, xd = refs[n_in + nx:n_in + nx + n_out], refs[n_in + nx + n_out:n_in + 2 * nx + n_out]
        scratch = refs[n_in + 2 * nx + n_out:]
        xchg.start_at_first_step(grid, xs, xd, scratch[-3:])
        body(*ins, *outs, *scratch[:-3])
        xchg.wait_at_last_step(grid, xs, xd, scratch[-3:])

    return wrapped


def _adamw_math(w, g, m, v):
    m2 = ADAM_B1 * m + (1.0 - ADAM_B1) * g
    v2 = ADAM_B2 * v + (1.0 - ADAM_B2) * (g * g)
    m_hat = m2 / (1.0 - ADAM_B1 ** ADAM_STEP)
    v_hat = v2 / (1.0 - ADAM_B2 ** ADAM_STEP)
    delta = -ADAM_LR * (m_hat / (jnp.sqrt(v_hat) + ADAM_EPS) + ADAM_WD * w)
    return delta, m2, v2


def _row_tile(rows):
    for t in (256, 128, 64, 32, 16, 8):
        if rows % t == 0:
            return t
    return rows


def _reduce_adamw(parts, w, m, v, name):
    _, r, cdim = parts.shape
    tr = _row_tile(r)

    def body(p_ref, w_ref, m_ref, v_ref, g_ref, d_ref, m2_ref, v2_ref):
        g = p_ref[0].astype(f32)
        for j in range(1, N_DEV):
            g = g + p_ref[j].astype(f32)
        delta, m2, v2 = _adamw_math(w_ref[...], g, m_ref[...], v_ref[...])
        g_ref[...] = g
        d_ref[...] = delta
        m2_ref[...] = m2
        v2_ref[...] = v2

    spec = pl.BlockSpec((tr, cdim), lambda i: (i, 0))
    return pl.pallas_call(
        body, name=name, grid=(r // tr,),
        in_specs=[pl.BlockSpec((N_DEV, tr, cdim), lambda i: (0, i, 0)), spec, spec, spec],
        out_specs=[spec] * 4,
        out_shape=[jax.ShapeDtypeStruct((r, cdim), f32)] * 4,
        compiler_params=_cparams(("parallel",)),
    )(parts, w, m, v)


def _adamw(w, g, m, v, name):
    r, cdim = w.shape
    tr = _row_tile(r)

    def body(w_ref, g_ref, m_ref, v_ref, d_ref, m2_ref, v2_ref):
        delta, m2, v2 = _adamw_math(w_ref[...], g_ref[...], m_ref[...], v_ref[...])
        d_ref[...] = delta
        m2_ref[...] = m2
        v2_ref[...] = v2

    spec = pl.BlockSpec((tr, cdim), lambda i: (i, 0))
    return pl.pallas_call(
        body, name=name, grid=(r // tr,),
        in_specs=[spec] * 4, out_specs=[spec] * 3,
        out_shape=[jax.ShapeDtypeStruct((r, cdim), f32)] * 3,
        compiler_params=_cparams(("parallel",)),
    )(w, g, m, v)


def _sum_devices(parts, name):
    _, r, cdim = parts.shape

    def body(p_ref, o_ref):
        g = p_ref[0]
        for j in range(1, N_DEV):
            g = g + p_ref[j]
        o_ref[...] = g

    return pl.pallas_call(
        body, name=name, out_shape=jax.ShapeDtypeStruct((r, cdim), f32),
        in_specs=[_VMEM], out_specs=_VMEM,
    )(parts)


def _ada_wgrad(c_all8, dmod_cols):
    nsh = dmod_cols.shape[1]

    def body(c_ref, d_ref, o_ref):
        cv = c_ref[...]
        o_ref[...] = lax.dot_general(cv * jax.nn.sigmoid(cv), d_ref[...], _TN, precision=_HI,
                                     preferred_element_type=f32)

    return pl.pallas_call(
        body, name="ada_wgrad", out_shape=jax.ShapeDtypeStruct((D_MODEL, nsh), f32),
        in_specs=[_VMEM, _VMEM], out_specs=_VMEM,
        compiler_params=pltpu.CompilerParams(vmem_limit_bytes=VMEM_LIMIT),
    )(c_all8, dmod_cols)


def _cols(t):
    return t.transpose(1, 0, 2).reshape(t.shape[1], N_DEV * t.shape[2])


def _col_blocks(t, n):
    return t.reshape(t.shape[0], N_DEV, n).transpose(1, 0, 2).astype(bf16)


def _row_blocks(t):
    return t.reshape(N_DEV, t.shape[0] // N_DEV, t.shape[1]).astype(bf16)


def _local_step(x, tgt, mod, norm_attn_g, w_in_p, rel_bias, conv_full, a_log, dt_bias, delta_norm_g,
                norm_ffn_g, final_norm_g, w_out_sh, w_gate_sh, w_up_sh, w_down_sh):
    s = x.shape[0]
    sh1, sc1, g1, sh2, sc2, g2 = [mod[:, i * D_MODEL:(i + 1) * D_MODEL] for i in range(6)]
    nag = norm_attn_g.reshape(1, D_MODEL)
    nfg = norm_ffn_g.reshape(1, D_MODEL)
    fg = final_norm_g.reshape(1, D_MODEL)
    idx = _bucket_tables()
    bias = _bias_tables(rel_bias, idx)
    alog_e = jnp.repeat(a_log.reshape(N_HEADS), HEAD_DIM)[None]
    dt_e = jnp.repeat(dt_bias.reshape(N_HEADS), HEAD_DIM)[None]
    ng_e = jnp.tile(delta_norm_g.reshape(HEAD_DIM), N_HEADS)[None]

    h1 = _ln_mod_fwd(x, nag, sc1, sh1, "ln1_fwd")
    proj, (w_out_g, w_gate_g) = _mm(h1, w_in_p, "nn", f32, 512, 1280, 1024, "in_proj",
                                    xchg=_Exchange([w_out_sh, w_gate_sh], gather=True))
    (y_attn, lse), (w_up_g, w_down_g) = _attn_fwd(proj, bias, _Exchange([w_up_sh, w_down_sh], gather=True))
    w_out_b = w_out_g.reshape(2 * GROUP_W, D_MODEL)
    w_gate_b, w_up_b = _cols(w_gate_g), _cols(w_up_g)
    w_down_b = w_down_g.reshape(D_FF, D_MODEL)
    n_ff = w_gate_sh.shape[1]
    sconv = _conv_silu_fwd(proj, conv_full)
    qn, kn, beta, g = _delta_prep_fwd(sconv, proj, alog_e, dt_e)
    u, w, qt, kh, qk, tinv, gm = _delta_chunk_pre(qn, kn, sconv, beta, g)
    o, ss = _delta_scan_fwd(u, w, qt, kh, qk, gm)
    y_delta = _gnorm_fwd(o, proj, ng_e)
    ycat = jnp.concatenate([y_attn, y_delta], axis=1).astype(bf16)
    y, x1, h2 = _proj_resid_ln_mod_fwd(ycat, w_out_b, x, g1, nfg, sc2, sh2, "out_proj_ln2")
    act, gate, up = _ffn_up(h2, w_gate_b, w_up_b, "ffn_up")
    dx2, dy2, loss, dfg, dg2 = _proj_final_loss_bwd(act, w_down_b, x1, g2, fg, tgt, "ffn_down_loss")

    dgate, dup = _ffn_down_dx(dy2, w_down_b, gate, up, "ffn_down_dx")
    g_down = _mm(act, dy2, "tn", f32, 1408, 1024, 1024, "ffn_down_dw")
    (dx1, dsh2, dsc2, dnfg, dy, dg1), (r_down,) = _proj_ln_mod_bwd(
        [(dgate, w_gate_b), (dup, w_up_b)], x1, nfg, sc2, dx2, 256, "ffn_up_dx_ln2",
        _Exchange([_row_blocks(g_down)], gather=False), gate=g1, y=y)
    g_gate = _mm(h2, dgate, "tn", f32, 1024, 1408, 1024, "ffn_gate_dw")
    g_up = _mm(h2, dup, "tn", f32, 1024, 1408, 1024, "ffn_up_dw")
    dycat = _mm(dy, w_out_b, "nt", f32, 512, 1024, 1024, "out_proj_dx")
    g_out = _mm(ycat, dy, "tn", f32, 1024, 1024, 1024, "out_proj_dw")
    dq, dk, dv, dbias = _attn_bwd(proj, bias, y_attn, lse, dycat)
    g_rb = _bias_grad(dbias, idx)[:, :, 0].T
    do, dz, dng = _gnorm_bwd(o, proj, ng_e, dycat)
    dso = _delta_scan_bwd(w, qt, kh, qk, gm, do)
    dqn, dkn, dvd, dbeta, dgd = _delta_chunk_bwd(qn, kn, sconv, beta, g, tinv, ss, dso, do)
    dsq, dsk, dba, dal, ddt = _delta_prep_bwd(sconv, proj, alog_e, dt_e, dqn, dkn, dbeta, dgd)
    (dxc, g_conv), (r_gate, r_up, r_out) = _conv_silu_bwd(
        proj, conv_full, (dsq, dsk, dvd),
        _Exchange([_col_blocks(g_gate, n_ff), _col_blocks(g_up, n_ff), _row_blocks(g_out)],
                  gather=False))
    pieces = ((dq, 0), (dk, GROUP_W), (dv, 2 * GROUP_W), (dxc, DELTA_COL), (dz, Z_COL), (dba, BA_BLOCK * 128))
    g_in = jnp.concatenate(
        [_mm(h1, p, "tn", f32, 1024, min(p.shape[1], 768), 1024, "in_proj_dw_%d" % c) for p, c in pieces], axis=1)
    (gx, dsh1, dsc1, dnag), (r_in,) = _proj_ln_mod_bwd(
        [(p, w_in_p[:, c:c + p.shape[1]]) for p, c in pieces], x, nag, sc1, dx1, TOK_TILE, "in_proj_dx_ln1",
        _Exchange([_col_blocks(g_in[:, :IN_WIDTH], IN_WIDTH // N_DEV)], gather=False))
    grads = dict(
        x=gx, mod=jnp.concatenate([dsh1, dsc1, dg1, dsh2, dsc2, dg2], axis=1),
        norm_attn_g=dnag, norm_ffn_g=dnfg, final_norm_g=dfg, rel_bias=g_rb, conv_w=g_conv,
        a_log=dal.reshape(N_HEADS, HEAD_DIM).sum(-1), dt_bias=ddt.reshape(N_HEADS, HEAD_DIM).sum(-1),
        delta_norm_g=dng.reshape(N_HEADS, HEAD_DIM).sum(0),
        w_in=r_in, w_out=r_out, w_gate=r_gate, w_up=r_up, w_down=r_down)
    return loss[0, 0], grads


MISC_OFF = dict(rel_bias=0, a_log=256, dt_bias=264, delta_norm_g=272)


def _misc_row(rel_bias, a_log, dt_bias, delta_norm_g):
    flat = jnp.concatenate([rel_bias.reshape(-1), a_log.reshape(-1), dt_bias.reshape(-1), delta_norm_g.reshape(-1)])
    return jnp.pad(flat, (0, D_MODEL - flat.shape[0]))[None]


def _pack_small(b_ada, nag, nfg, fng, rel_bias, a_log, dt_bias, dng, conv_shard):
    rows = [b_ada.reshape(6, D_MODEL), nag.reshape(1, D_MODEL), nfg.reshape(1, D_MODEL), fng.reshape(1, D_MODEL),
            _misc_row(rel_bias, a_log, dt_bias, dng),
            jnp.pad(conv_shard.reshape(-1), (0, D_MODEL - conv_shard.size))[None],
            jnp.zeros((5, D_MODEL), f32)]
    return jnp.concatenate(rows, axis=0)


def _unpack_small(p, conv_shape):
    misc = p[9]
    return dict(
        b_ada=p[0:6].reshape(1, 6 * D_MODEL), norm_attn_g=p[6:7], norm_ffn_g=p[7:8], final_norm_g=p[8],
        rel_bias=misc[0:256].reshape(N_BUCKETS, N_HEADS), a_log=misc[256:264].reshape(1, N_HEADS),
        dt_bias=misc[264:272].reshape(1, N_HEADS), delta_norm_g=misc[272:336].reshape(1, HEAD_DIM),
        conv_w=p[10, :conv_shape[1] * conv_shape[2]].reshape(conv_shape))


def kernel(x, c, w_ada, b_ada, norm_attn_g, w_in, rel_bias, conv_w, a_log, dt_bias, delta_norm_g, w_out, norm_ffn_g, w_gate, w_up, w_down, final_norm_g, loss_target, m_w_ada, m_b_ada, m_norm_attn_g, m_w_in, m_rel_bias, m_conv_w, m_a_log, m_dt_bias, m_delta_norm_g, m_w_out, m_norm_ffn_g, m_w_gate, m_w_up, m_w_down, m_final_norm_g, v_w_ada, v_b_ada, v_norm_attn_g, v_w_in, v_rel_bias, v_conv_w, v_a_log, v_dt_bias, v_delta_norm_g, v_w_out, v_norm_ffn_g, v_w_gate, v_w_up, v_w_down, v_final_norm_g):
    me = 4 * lax.axis_index("x") + 2 * lax.axis_index("y") + lax.axis_index("c")
    ada_sh = w_ada.shape[2]
    conv_sh = conv_w.shape[2]

    cv = jnp.concatenate([c[0], conv_w[0].reshape(-1)])
    cv8 = jnp.zeros((8, 2 * D_MODEL), f32).at[0, :cv.shape[0]].set(cv)
    b8 = jnp.broadcast_to(b_ada.reshape(N_DEV, 1, ada_sh), (N_DEV, 8, ada_sh))
    call, modp, w_in_g = _ada_exchange(cv8, w_ada[0], b8, w_in[0].astype(bf16))
    mod = modp[:, 0, :].reshape(1, 6 * D_MODEL)
    c_all = call[:, 0, :D_MODEL]
    conv_full = call[:, 0, D_MODEL:D_MODEL + CONV_WIDTH * conv_sh].reshape(N_DEV, CONV_WIDTH, conv_sh)
    conv_full = conv_full.transpose(1, 0, 2).reshape(CONV_WIDTH, N_DEV * conv_sh)

    w_in_p = jnp.pad(_cols(w_in_g), ((0, 0), (0, IN_PAD - IN_WIDTH)))
    loss_local, gr = _local_step(x[0], loss_target[0], mod, norm_attn_g, w_in_p, rel_bias, conv_full, a_log,
                                 dt_bias, delta_norm_g, norm_ffn_g, final_norm_g, w_out[0].astype(bf16),
                                 w_gate[0].astype(bf16), w_up[0].astype(bf16), w_down[0].astype(bf16))
    loss = lax.psum(loss_local, ("x", "y", "c"))

    small = jnp.concatenate([
        gr["mod"].reshape(6, D_MODEL), gr["norm_attn_g"], gr["norm_ffn_g"], gr["final_norm_g"],
        gr["conv_w"].reshape(6, D_MODEL),
        _misc_row(gr["rel_bias"], gr["a_log"], gr["dt_bias"], gr["delta_norm_g"])], axis=0)
    parts = _all_to_all([jnp.broadcast_to(small[None], (N_DEV,) + small.shape)], "small_gather")[0]
    tot = _sum_devices(parts, "small_sum")
    g_conv_full = tot[9:15].reshape(CONV_WIDTH, N_DEV * conv_sh)
    g_conv = lax.dynamic_slice(g_conv_full, (0, me * conv_sh), (CONV_WIDTH, conv_sh))
    misc = tot[15]
    g_small = _pack_small(tot[0:6], tot[6], tot[7], tot[8], misc[0:256], misc[256:264], misc[264:272],
                          misc[272:336], g_conv)
    pk = lambda pre: _pack_small(pre[0], pre[1], pre[2], pre[3], pre[4], pre[5], pre[6], pre[7], pre[8])
    w_small = pk((b_ada, norm_attn_g, norm_ffn_g, final_norm_g, rel_bias, a_log, dt_bias, delta_norm_g, conv_w))
    m_small = pk((m_b_ada, m_norm_attn_g, m_norm_ffn_g, m_final_norm_g, m_rel_bias, m_a_log, m_dt_bias,
                  m_delta_norm_g, m_conv_w))
    v_small = pk((v_b_ada, v_norm_attn_g, v_norm_ffn_g, v_final_norm_g, v_rel_bias, v_a_log, v_dt_bias,
                  v_delta_norm_g, v_conv_w))
    d_small, m2_small, v2_small = _adamw(w_small, g_small, m_small, v_small, "adamw_small")
    cshape = conv_w.shape
    G, Dl, M2, V2 = (_unpack_small(t, cshape) for t in (g_small, d_small, m2_small, v2_small))

    dmod_all = parts[:, 0:6, :].reshape(N_DEV, 6 * D_MODEL)
    dmod_cols = lax.dynamic_slice(dmod_all, (0, me * ada_sh), (N_DEV, ada_sh))
    g_ada = _ada_wgrad(c_all, dmod_cols)
    d_ada, m2_ada, v2_ada = _adamw(w_ada[0], g_ada, m_w_ada[0], v_w_ada[0], "adamw_w_ada")

    big = {}
    for name, w_, m_, v_ in (("w_in", w_in, m_w_in, v_w_in), ("w_out", w_out, m_w_out, v_w_out),
                             ("w_gate", w_gate, m_w_gate, v_w_gate), ("w_up", w_up, m_w_up, v_w_up),
                             ("w_down", w_down, m_w_down, v_w_down)):
        big[name] = [t[None] for t in _reduce_adamw(gr[name], w_[0], m_[0], v_[0], "reduce_adamw_" + name)]

    def leaf(i, name):
        if name == "w_ada":
            return (g_ada, d_ada, m2_ada, v2_ada)[i][None]
        if name in big:
            return big[name][i]
        return (G, Dl, M2, V2)[i][name]

    order = ["w_ada", "b_ada", "norm_attn_g", "w_in", "rel_bias", "conv_w", "a_log", "dt_bias", "delta_norm_g",
             "w_out", "norm_ffn_g", "w_gate", "w_up", "w_down", "final_norm_g"]
    outs = [loss, gr["x"][None]]
    for i in range(4):
        outs += [leaf(i, n) for n in order]
    return tuple(outs)
```

```python
import functools
import math

import jax
import jax.numpy as jnp
from jax import lax
from jax.experimental import pallas as pl
from jax.experimental.pallas import tpu as pltpu

f32 = jnp.float32
bf16 = jnp.bfloat16

D_MODEL = 1024
HEAD_DIM = 64
N_HEADS = 8
GROUP_W = 512
IN_WIDTH = 3600
IN_PAD = 3840
D_FF = 2816
EPS = 1e-6
NEG_INF = -1e30
BAND = 128
PAD_UNIT = 2048
DILATIONS = (1, 4, 16)
N_BUCKETS = 32
MAX_DISTANCE = 2048
CONV_WIDTH = 4
CHUNK = 64
N_DEV = 8
VMEM_LIMIT = 56 * 1024 * 1024

ADAM_LR, ADAM_B1, ADAM_B2, ADAM_EPS, ADAM_WD, ADAM_STEP = 0.001, 0.9, 0.999, 1e-08, 0.01, 10


def _cparams(sem):
    return pltpu.CompilerParams(dimension_semantics=sem, vmem_limit_bytes=VMEM_LIMIT)


def _mm(a, b, mode, out_dtype, tm, tn, tk, name, xchg=None):
    if mode == "nn":
        (m, k), (_, n) = a.shape, b.shape
        a_spec = pl.BlockSpec((tm, tk), lambda j, i, kk: (i, kk))
        b_spec = pl.BlockSpec((tk, tn), lambda j, i, kk: (kk, j))
        dims = (((1,), (0,)), ((), ()))
    elif mode == "nt":
        (m, k), (n, _) = a.shape, b.shape
        a_spec = pl.BlockSpec((tm, tk), lambda j, i, kk: (i, kk))
        b_spec = pl.BlockSpec((tn, tk), lambda j, i, kk: (j, kk))
        dims = (((1,), (1,)), ((), ()))
    else:
        (k, m), (_, n) = a.shape, b.shape
        a_spec = pl.BlockSpec((tk, tm), lambda j, i, kk: (kk, i))
        b_spec = pl.BlockSpec((tk, tn), lambda j, i, kk: (kk, j))
        dims = (((0,), (0,)), ((), ()))
    assert m % tm == 0 and n % tn == 0 and k % tk == 0, (name, m, n, k, tm, tn, tk)
    nk = k // tk
    grid = (n // tn, m // tm, nk)
    nx = xchg.n if xchg is not None else 0

    def body(*refs):
        a_ref, b_ref = refs[:2]
        o_ref = refs[2 + nx]
        scratch = refs[3 + 2 * nx:]
        if nx:
            xrefs = (refs[2:2 + nx], refs[3 + nx:3 + 2 * nx], scratch[-3:])
            xchg.start_at_first_step(grid, *xrefs)
        if nk == 1:
            o_ref[...] = lax.dot_general(a_ref[...].astype(bf16), b_ref[...].astype(bf16), dims,
                                         preferred_element_type=f32).astype(o_ref.dtype)
        else:
            acc_ref = scratch[0]
            kk = pl.program_id(2)

            @pl.when(kk == 0)
            def _():
                acc_ref[...] = jnp.zeros_like(acc_ref)

            acc_ref[...] += lax.dot_general(a_ref[...].astype(bf16), b_ref[...].astype(bf16), dims,
                                            preferred_element_type=f32)

            @pl.when(kk == nk - 1)
            def _():
                o_ref[...] = acc_ref[...].astype(o_ref.dtype)
        if nx:
            xchg.wait_at_last_step(grid, *xrefs)

    out = pl.pallas_call(
        body, name=name, grid=grid,
        in_specs=[a_spec, b_spec] + ([_ANY] * nx),
        out_specs=[pl.BlockSpec((tm, tn), lambda j, i, kk: (i, j))] + ([_ANY] * nx),
        out_shape=[jax.ShapeDtypeStruct((m, n), out_dtype)] + (xchg.out_shape() if nx else []),
        scratch_shapes=([pltpu.VMEM((tm, tn), f32)] if nk > 1 else []) + (xchg.scratch() if nx else []),
        compiler_params=_cparams(("arbitrary",) * 3 if nx else ("parallel", "parallel", "arbitrary")),
    )(a, b, *(xchg.arrs if nx else []))
    return (out[0], out[1:]) if nx else out[0]


TOK_TILE = 512
SUB_COLS = 384


def _row_spec(width, tile=TOK_TILE):
    return pl.BlockSpec((tile, width), lambda i: (i, 0))


def _vec_spec(width, rows=1):
    return pl.BlockSpec((rows, width), lambda i: (0, 0))


def _ln_mod_fwd(x, gain, sc, sh, name):
    s, d = x.shape

    def body(x_ref, g_ref, sc_ref, sh_ref, h_ref):
        xv = x_ref[...]
        rstd = lax.rsqrt(jnp.mean(xv * xv, axis=-1, keepdims=True) + EPS)
        h = (xv * rstd) * g_ref[...] * (1.0 + sc_ref[...]) + sh_ref[...]
        h_ref[...] = h.astype(bf16)

    return pl.pallas_call(
        body, name=name, grid=(s // TOK_TILE,),
        in_specs=[_row_spec(d), _vec_spec(d), _vec_spec(d), _vec_spec(d)],
        out_specs=_row_spec(d),
        out_shape=jax.ShapeDtypeStruct((s, d), bf16),
        compiler_params=_cparams(("parallel",)),
    )(x, gain, sc, sh)


def _proj_resid_ln_mod_fwd(pairs, x, gate, gain, sc, sh, name):
    s, d = x.shape
    npair = len(pairs)

    def body(*refs):
        aw = refs[:2 * npair]
        x_ref, gt_ref, g_ref, sc_ref, sh_ref, y_ref, x1_ref, h_ref = refs[2 * npair:]
        y = jnp.dot(aw[0][...].astype(bf16), aw[1][...], preferred_element_type=f32)
        for t in range(1, npair):
            y = y + jnp.dot(aw[2 * t][...].astype(bf16), aw[2 * t + 1][...], preferred_element_type=f32)
        y_ref[...] = y
        x1 = x_ref[...] + gt_ref[...] * y
        x1_ref[...] = x1
        rstd = lax.rsqrt(jnp.mean(x1 * x1, axis=-1, keepdims=True) + EPS)
        h = (x1 * rstd) * g_ref[...] * (1.0 + sc_ref[...]) + sh_ref[...]
        h_ref[...] = h.astype(bf16)

    aw_specs, aw = [], []
    for a, w in pairs:
        aw_specs += [_row_spec(a.shape[1]), pl.BlockSpec(w.shape, lambda i: (0, 0))]
        aw += [a, w]
    return pl.pallas_call(
        body, name=name, grid=(s // TOK_TILE,),
        in_specs=aw_specs + [_row_spec(d)] + [_vec_spec(d)] * 4,
        out_specs=[_row_spec(d)] * 3,
        out_shape=[jax.ShapeDtypeStruct((s, d), f32)] * 2 + [jax.ShapeDtypeStruct((s, d), bf16)],
        compiler_params=_cparams(("parallel",)),
    )(*aw, x, gate, gain, sc, sh)


FFN_TN = 1408


def _ffn_up(h2, w_gate, w_up, name):
    s, d = h2.shape
    tm = TOK_TILE

    def body(h_ref, wg_ref, wu_ref, a_ref, g_ref, u_ref):
        h = h_ref[...]
        g = jnp.dot(h, wg_ref[...], preferred_element_type=f32)
        u = jnp.dot(h, wu_ref[...], preferred_element_type=f32)
        a_ref[...] = (g * jax.nn.sigmoid(g) * u).astype(bf16)
        g_ref[...] = g.astype(bf16)
        u_ref[...] = u.astype(bf16)

    w_spec = pl.BlockSpec((d, FFN_TN), lambda j, i: (0, j))
    o_spec = pl.BlockSpec((tm, FFN_TN), lambda j, i: (i, j))
    return pl.pallas_call(
        body, name=name, grid=(D_FF // FFN_TN, s // tm),
        in_specs=[pl.BlockSpec((tm, d), lambda j, i: (i, 0)), w_spec, w_spec],
        out_specs=[o_spec] * 3,
        out_shape=[jax.ShapeDtypeStruct((s, D_FF), bf16)] * 3,
        compiler_params=_cparams(("parallel", "parallel")),
    )(h2, w_gate, w_up)


def _ffn_down_dx(dy2, w_down, gate, up, name):
    s, d = dy2.shape
    tm = TOK_TILE

    def body(dy_ref, w_ref, g_ref, u_ref, dg_ref, du_ref):
        dy = dy_ref[...]
        for c0 in range(0, FFN_TN, SUB_COLS):
            cols = slice(c0, min(c0 + SUB_COLS, FFN_TN))
            da = lax.dot_general(dy, w_ref[cols, :], _NT, preferred_element_type=f32)
            g = g_ref[:, cols].astype(f32)
            sg = jax.nn.sigmoid(g)
            du_ref[:, cols] = (da * g * sg).astype(bf16)
            dg_ref[:, cols] = (da * u_ref[:, cols].astype(f32) * sg * (1.0 + g * (1.0 - sg))).astype(bf16)

    t_spec = pl.BlockSpec((tm, FFN_TN), lambda j, i: (i, j))
    return pl.pallas_call(
        body, name=name, grid=(D_FF // FFN_TN, s // tm),
        in_specs=[pl.BlockSpec((tm, d), lambda j, i: (i, 0)), pl.BlockSpec((FFN_TN, d), lambda j, i: (j, 0)),
                  t_spec, t_spec],
        out_specs=[t_spec, t_spec],
        out_shape=[jax.ShapeDtypeStruct((s, D_FF), bf16)] * 2,
        compiler_params=_cparams(("parallel", "parallel")),
    )(dy2, w_down, gate, up)


def _acc_spec(width):
    return pl.BlockSpec((1, width), lambda i: (0, 0))


def _proj_final_loss_bwd(a, w, x1, gate2, final_g, target, name):
    s, d = x1.shape
    k = a.shape[1]

    def body(a_ref, w_ref, x1_ref, gt_ref, fg_ref, tg_ref, dx2_ref, dy2_ref, loss_ref, dfg_ref, dgt_ref):
        @pl.when(pl.program_id(0) == 0)
        def _():
            loss_ref[...] = jnp.zeros_like(loss_ref)
            dfg_ref[...] = jnp.zeros_like(dfg_ref)
            dgt_ref[...] = jnp.zeros_like(dgt_ref)

        y2 = jnp.dot(a_ref[...], w_ref[...], preferred_element_type=f32)
        gt = gt_ref[...]
        fg = fg_ref[...]
        x2 = x1_ref[...] + gt * y2
        rstd = lax.rsqrt(jnp.mean(x2 * x2, axis=-1, keepdims=True) + EPS)
        xn = x2 * rstd
        err = xn * fg - tg_ref[...]
        row = jnp.sum(err * err, axis=-1, keepdims=True) * (0.5 / d)
        loss_ref[...] += jnp.sum(row, axis=0, keepdims=True) + jnp.zeros_like(loss_ref)
        dout = err * (1.0 / d)
        dfg_ref[...] += jnp.sum(dout * xn, axis=0, keepdims=True)
        dxn = dout * fg
        dx2 = rstd * (dxn - xn * jnp.mean(dxn * xn, axis=-1, keepdims=True))
        dx2_ref[...] = dx2
        dgt_ref[...] += jnp.sum(dx2 * y2, axis=0, keepdims=True)
        dy2_ref[...] = (gt * dx2).astype(bf16)

    return pl.pallas_call(
        body, name=name, grid=(s // TOK_TILE,),
        in_specs=[_row_spec(k), pl.BlockSpec((k, d), lambda i: (0, 0)), _row_spec(d), _vec_spec(d), _vec_spec(d),
                  _row_spec(d)],
        out_specs=[_row_spec(d), _row_spec(d), _acc_spec(128), _acc_spec(d), _acc_spec(d)],
        out_shape=[jax.ShapeDtypeStruct((s, d), f32), jax.ShapeDtypeStruct((s, d), bf16),
                   jax.ShapeDtypeStruct((1, 128), f32), jax.ShapeDtypeStruct((1, d), f32),
                   jax.ShapeDtypeStruct((1, d), f32)],
        compiler_params=_cparams(("arbitrary",)),
    )(a, w, x1, gate2, final_g, target)


def _proj_ln_mod_bwd(pairs, xin, gain, sc, dres, tm, name, xchg, gate=None, y=None):
    s, d = xin.shape
    with_gate = gate is not None
    npair = len(pairs)
    n_in = 2 * npair + (7 if with_gate else 5) - 1
    n_out = 6 if with_gate else 4

    def body(*refs):
        ab = refs[:2 * npair]
        if with_gate:
            (x_ref, g_ref, sc_ref, dr_ref, gt_ref, y_ref,
             dx_ref, dsh_ref, dsc_ref, dg_ref, dy_ref, dgt_ref) = refs[2 * npair:]
        else:
            x_ref, g_ref, sc_ref, dr_ref, dx_ref, dsh_ref, dsc_ref, dg_ref = refs[2 * npair:]

        @pl.when(pl.program_id(0) == 0)
        def _():
            dsh_ref[...] = jnp.zeros_like(dsh_ref)
            dsc_ref[...] = jnp.zeros_like(dsc_ref)
            dg_ref[...] = jnp.zeros_like(dg_ref)
            if with_gate:
                dgt_ref[...] = jnp.zeros_like(dgt_ref)

        dh = lax.dot_general(ab[0][...].astype(bf16), ab[1][...], _NT, preferred_element_type=f32)
        for t in range(1, npair):
            dh = dh + lax.dot_general(ab[2 * t][...].astype(bf16), ab[2 * t + 1][...], _NT,
                                      preferred_element_type=f32)
        xv = x_ref[...]
        g = g_ref[...]
        sc1 = 1.0 + sc_ref[...]
        rstd = lax.rsqrt(jnp.mean(xv * xv, axis=-1, keepdims=True) + EPS)
        xn = xv * rstd
        dsh_ref[...] += jnp.sum(dh, axis=0, keepdims=True)
        dsc_ref[...] += jnp.sum(dh * (xn * g), axis=0, keepdims=True)
        dg_ref[...] += jnp.sum(dh * sc1 * xn, axis=0, keepdims=True)
        dxn = dh * sc1 * g
        dx = dr_ref[...] + rstd * (dxn - xn * jnp.mean(dxn * xn, axis=-1, keepdims=True))
        dx_ref[...] = dx
        if with_gate:
            dgt_ref[...] += jnp.sum(dx * y_ref[...], axis=0, keepdims=True)
            dy_ref[...] = (gt_ref[...] * dx).astype(bf16)

    row = lambda width: pl.BlockSpec((tm, width), lambda i: (i, 0))
    in_specs, args = [], []
    for a, b in pairs:
        in_specs += [row(a.shape[1]), pl.BlockSpec(b.shape, lambda i: (0, 0))]
        args += [a, b]
    in_specs += [row(d), _vec_spec(d), _vec_spec(d), row(d)]
    args += [xin, gain, sc, dres]
    out_specs = [row(d), _acc_spec(d), _acc_spec(d), _acc_spec(d)]
    out_shape = [jax.ShapeDtypeStruct((s, d), f32)] + [jax.ShapeDtypeStruct((1, d), f32)] * 3
    if with_gate:
        in_specs += [_vec_spec(d), row(d)]
        out_specs += [row(d), _acc_spec(d)]
        out_shape += [jax.ShapeDtypeStruct((s, d), bf16), jax.ShapeDtypeStruct((1, d), f32)]
        args += [gate, y]
    grid = (s // tm,)
    out = pl.pallas_call(
        _ride(body, n_in, n_out, xchg, grid), name=name, grid=grid,
        in_specs=in_specs + [_ANY] * xchg.n, out_specs=out_specs + [_ANY] * xchg.n,
        out_shape=out_shape + xchg.out_shape(), scratch_shapes=xchg.scratch(),
        compiler_params=_cparams(("arbitrary",)),
    )(*args, *xchg.arrs)
    return out[:n_out], out[n_out:]


def _bucket_tables():
    import numpy as np
    qi = np.arange(BAND)[:, None]
    kj = np.arange(2 * BAND)[None, :]
    steps = qi + BAND - kj
    max_exact = N_BUCKETS // 2
    out = []
    for d in DILATIONS:
        dist = np.maximum(steps, 0) * d
        dist_f = np.maximum(dist, 1).astype(np.float32)
        large = max_exact + (np.log(dist_f / np.float32(max_exact)) / np.float32(math.log(MAX_DISTANCE / max_exact))
                             * np.float32(N_BUCKETS - max_exact)).astype(np.int32)
        out.append(np.where(dist < max_exact, dist, np.minimum(large, N_BUCKETS - 1)))
    return jnp.asarray(np.stack(out).astype(np.int32))


def _bias_tables(rel_bias, idx):
    def body(idx_ref, rb_ref, o_ref):
        h = pl.program_id(1)
        idxv = idx_ref[0]
        acc = jnp.zeros((BAND, 2 * BAND), f32)
        for b in range(N_BUCKETS):
            acc = jnp.where(idxv == b, rb_ref[b, h], acc)
        o_ref[0, 0] = jnp.where(_attn_masks()[1], acc, NEG_INF)

    return pl.pallas_call(
        body, name="attn_bias_tables", grid=(3, N_HEADS),
        in_specs=[pl.BlockSpec((1, BAND, 2 * BAND), lambda br, h: (br, 0, 0)),
                  pl.BlockSpec(memory_space=pltpu.SMEM)],
        out_specs=pl.BlockSpec((1, 1, BAND, 2 * BAND), lambda br, h: (br, h, 0, 0)),
        out_shape=jax.ShapeDtypeStruct((3, N_HEADS, BAND, 2 * BAND), f32),
        compiler_params=_cparams(("parallel", "parallel")),
    )(idx, rel_bias)


def _bias_grad(dbias, idx):
    def body(idx_ref, db_ref, o_ref):
        br = pl.program_id(1)

        @pl.when(br == 0)
        def _():
            o_ref[...] = jnp.zeros_like(o_ref)

        idxv = idx_ref[0]
        dbv = db_ref[0, 0]
        row = lax.broadcasted_iota(jnp.int32, (N_BUCKETS, 128), 0)
        acc = jnp.zeros((N_BUCKETS, 128), f32)
        for b in range(N_BUCKETS):
            sb = jnp.sum(jnp.sum(jnp.where(idxv == b, dbv, 0.0), axis=1, keepdims=True), axis=0, keepdims=True)
            acc = acc + jnp.where(row == b, sb, 0.0)
        o_ref[0] += acc

    return pl.pallas_call(
        body, name="attn_bias_grad", grid=(N_HEADS, 3),
        in_specs=[pl.BlockSpec((1, BAND, 2 * BAND), lambda h, br: (br, 0, 0)),
                  pl.BlockSpec((1, 1, BAND, 2 * BAND), lambda h, br: (br, h, 0, 0))],
        out_specs=pl.BlockSpec((1, N_BUCKETS, 128), lambda h, br: (h, 0, 0)),
        out_shape=jax.ShapeDtypeStruct((N_HEADS, N_BUCKETS, 128), f32),
        compiler_params=_cparams(("parallel", "arbitrary")),
    )(idx, dbias)


def _attn_masks():
    lane = lax.broadcasted_iota(jnp.int32, (BAND, 128), 1)
    m0 = lane < HEAD_DIM
    qi = lax.broadcasted_iota(jnp.int32, (BAND, 2 * BAND), 0)
    kj = lax.broadcasted_iota(jnp.int32, (BAND, 2 * BAND), 1)
    steps = qi + BAND - kj
    in_window = (steps >= 0) & (steps <= BAND)
    return m0, in_window, kj >= BAND


_NT = (((1,), (1,)), ((), ()))
_TN = (((0,), (0,)), ((), ()))
_BNN = (((2,), (1,)), ((0,), (0,)))
_BNT = (((2,), (2,)), ((0,), (0,)))
_BTN = (((1,), (1,)), ((0,), (0,)))
ATTN_GROUP = 4
ATTN_ITEMS = PAD_UNIT // BAND
Q_COL, K_COL, V_COL = 0, 4, 8


def _attn_item_rows(j, d, c, cbase):
    r = lax.rem(j, d)
    b = lax.div(j, d)
    loc = b * (d * BAND) + r
    first = jnp.logical_and(c == 0, b == 0)
    start = cbase + loc
    pstart = jnp.where(first, start, start - d * BAND)
    return loc, start, pstart, first


def _attn_fwd(proj, bias, xchg):
    s = proj.shape[0]

    def body(q_ref, k_ref, v_ref, b_ref, y_ref, lse_ref, o_s, l_s):
        c = pl.program_id(1)
        cbase = pl.multiple_of(c * PAD_UNIT, PAD_UNIT)
        m0, in_window, cur_half = _attn_masks()
        for bi, d in enumerate(DILATIONS):
            def group(jg, carry, bi=bi, d=d):
                locs, qs, ks, vs, pens = [], [], [], [], []
                for t in range(ATTN_GROUP):
                    loc, start, pstart, first = _attn_item_rows(jg * ATTN_GROUP + t, d, c, cbase)
                    locs.append(loc)
                    qs.append(q_ref[pl.ds(loc, BAND, stride=d), :])
                    ks.append(jnp.concatenate([k_ref[pl.ds(pstart, BAND, stride=d), :],
                                               k_ref[pl.ds(start, BAND, stride=d), :]], axis=0))
                    vs.append(jnp.concatenate([v_ref[pl.ds(pstart, BAND, stride=d), :],
                                               v_ref[pl.ds(start, BAND, stride=d), :]], axis=0))
                    pens.append(jnp.where(cur_half, 0.0, jnp.where(first, NEG_INF, 0.0)))
                q = jnp.stack(qs)
                kk = jnp.stack(ks + ks).astype(bf16)
                vv = jnp.stack(vs + vs).astype(bf16)
                pen = jnp.stack(pens + pens)
                qh = (jnp.concatenate([jnp.where(m0, q, 0.0), jnp.where(m0, 0.0, q)], axis=0) * 0.125).astype(bf16)
                sc = lax.dot_general(qh, kk, _BNT, preferred_element_type=f32)
                sc = (sc.reshape(2, ATTN_GROUP, BAND, 2 * BAND) + b_ref[bi][:, None]).reshape(sc.shape) + pen
                mx = jnp.max(sc, axis=-1, keepdims=True)
                e = jnp.exp(sc - mx)
                l = jnp.sum(e, axis=-1, keepdims=True)
                o = lax.dot_general(e.astype(bf16), vv, _BNN, preferred_element_type=f32) * (1.0 / l)
                ls = mx + jnp.log(l)
                for t in range(ATTN_GROUP):
                    rows = pl.ds(locs[t], BAND, stride=d)
                    o_s[bi, rows, :] = jnp.where(m0, o[t], o[ATTN_GROUP + t])
                    l_s[bi, rows, :] = jnp.where(m0, ls[t], ls[ATTN_GROUP + t])
                return carry

            lax.fori_loop(0, ATTN_ITEMS // ATTN_GROUP, group, 0)

        def merge(t, carry):
            rows = pl.ds(pl.multiple_of(t * 256, 256), 256)
            ls = [l_s[i, rows, :] for i in range(3)]
            mx = jnp.maximum(jnp.maximum(ls[0], ls[1]), ls[2])
            ws = [jnp.exp(l - mx) for l in ls]
            tot = ws[0] + ws[1] + ws[2]
            y = (ws[0] * o_s[0, rows, :] + ws[1] * o_s[1, rows, :] + ws[2] * o_s[2, rows, :]) / tot
            y_ref[rows, :] = y
            lse_ref[rows, :] = mx + jnp.log(tot)
            return carry

        lax.fori_loop(0, PAD_UNIT // 256, merge, 0)

    chunk = lambda col: pl.BlockSpec((PAD_UNIT, 128), lambda p, c: (c, col + p))
    full = lambda col: pl.BlockSpec((s, 128), lambda p, c: (0, col + p))
    grid = (N_HEADS // 2, s // PAD_UNIT)
    out = pl.pallas_call(
        _ride(body, 4, 2, xchg, grid), name="attn_fwd", grid=grid,
        in_specs=[chunk(Q_COL), full(K_COL), full(V_COL),
                  pl.BlockSpec((3, 2, BAND, 2 * BAND), lambda p, c: (0, p, 0, 0))] + [_ANY] * xchg.n,
        out_specs=[chunk(0), chunk(0)] + [_ANY] * xchg.n,
        out_shape=[jax.ShapeDtypeStruct((s, GROUP_W), f32)] * 2 + xchg.out_shape(),
        scratch_shapes=[pltpu.VMEM((3, PAD_UNIT, 128), f32)] * 2 + xchg.scratch(),
        compiler_params=_cparams(("arbitrary", "arbitrary")),
    )(proj, proj, proj, bias, *xchg.arrs)
    return out[:2], out[2:]


def _attn_bwd(proj, bias, y, lse, dycat):
    s = proj.shape[0]

    def body(q_ref, k_ref, v_ref, b_ref, y_ref, lse_ref, dy_ref, dq_ref, dk_ref, dv_ref, db_ref, dd_s):
        c = pl.program_id(1)
        cbase = pl.multiple_of(c * PAD_UNIT, PAD_UNIT)
        m0, in_window, cur_half = _attn_masks()

        @pl.when(c == 0)
        def _():
            dk_ref[...] = jnp.zeros_like(dk_ref)
            dv_ref[...] = jnp.zeros_like(dv_ref)
            db_ref[...] = jnp.zeros_like(db_ref)

        dq_ref[...] = jnp.zeros_like(dq_ref)

        def rowdot(t, carry):
            rows = pl.ds(pl.multiple_of(t * 256, 256), 256)
            prod = dy_ref[rows, :] * y_ref[rows, :]
            lane = lax.broadcasted_iota(jnp.int32, prod.shape, 1)
            h0 = lane < HEAD_DIM
            d0 = jnp.sum(jnp.where(h0, prod, 0.0), axis=-1, keepdims=True)
            d1 = jnp.sum(jnp.where(h0, 0.0, prod), axis=-1, keepdims=True)
            dd_s[rows, :] = jnp.where(h0, d0, d1)
            return carry

        lax.fori_loop(0, PAD_UNIT // 256, rowdot, 0)

        for bi, d in enumerate(DILATIONS):
            def group(jg, carry, bi=bi, d=d):
                ng = ATTN_GROUP
                meta, qs, dos, lqs, dds, ks, vs, pens = [], [], [], [], [], [], [], []
                for t in range(ng):
                    loc, start, pstart, first = _attn_item_rows(jg * ng + t, d, c, cbase)
                    qrows = pl.ds(loc, BAND, stride=d)
                    rows = pl.ds(start, BAND, stride=d)
                    prows = pl.ds(pstart, BAND, stride=d)
                    meta.append((qrows, rows, prows))
                    qs.append(q_ref[qrows, :])
                    dos.append(dy_ref[qrows, :])
                    lqs.append(lse_ref[qrows, :])
                    dds.append(dd_s[qrows, :])
                    ks.append(jnp.concatenate([k_ref[prows, :], k_ref[rows, :]], axis=0))
                    vs.append(jnp.concatenate([v_ref[prows, :], v_ref[rows, :]], axis=0))
                    pens.append(jnp.where(cur_half, 0.0, jnp.where(first, NEG_INF, 0.0)))

                def heads(t):
                    return jnp.concatenate([jnp.where(m0, t, 0.0), jnp.where(m0, 0.0, t)], axis=0)

                def head_col(t):
                    return jnp.concatenate([t[:, :, 0:1], t[:, :, HEAD_DIM:HEAD_DIM + 1]], axis=0)

                qh = (heads(jnp.stack(qs)) * 0.125).astype(bf16)
                doh = heads(jnp.stack(dos)).astype(bf16)
                kk = jnp.stack(ks + ks).astype(bf16)
                vv = jnp.stack(vs + vs).astype(bf16)
                sc = lax.dot_general(qh, kk, _BNT, preferred_element_type=f32)
                sc = (sc.reshape(2, ng, BAND, 2 * BAND) + b_ref[bi][:, None]).reshape(sc.shape) + jnp.stack(pens + pens)
                p = jnp.exp(sc - head_col(jnp.stack(lqs)))
                dp = lax.dot_general(doh, vv, _BNT, preferred_element_type=f32)
                ds = p * (dp - head_col(jnp.stack(dds)))
                db_ref[bi] += jnp.sum(ds.reshape(2, ng, BAND, 2 * BAND), axis=1)
                dsb = ds.astype(bf16)
                dq = lax.dot_general(dsb, kk, _BNN, preferred_element_type=f32) * 0.125
                dk = lax.dot_general(dsb, qh, _BTN, preferred_element_type=f32)
                dv = lax.dot_general(p.astype(bf16), doh, _BTN, preferred_element_type=f32)
                for t in range(ng):
                    qrows, rows, prows = meta[t]
                    dq_ref[qrows, :] += jnp.where(m0, dq[t], dq[ng + t])
                    dkt = dk[t] + dk[ng + t]
                    dvt = dv[t] + dv[ng + t]
                    dk_ref[prows, :] += dkt[:BAND]
                    dk_ref[rows, :] += dkt[BAND:]
                    dv_ref[prows, :] += dvt[:BAND]
                    dv_ref[rows, :] += dvt[BAND:]
                return carry

            lax.fori_loop(0, ATTN_ITEMS // ATTN_GROUP, group, 0)

    chunk = lambda col: pl.BlockSpec((PAD_UNIT, 128), lambda p, c: (c, col + p))
    full = lambda col: pl.BlockSpec((s, 128), lambda p, c: (0, col + p))
    bias_spec = pl.BlockSpec((3, 2, BAND, 2 * BAND), lambda p, c: (0, p, 0, 0))
    return pl.pallas_call(
        body, name="attn_bwd", grid=(N_HEADS // 2, s // PAD_UNIT),
        in_specs=[chunk(Q_COL), full(K_COL), full(V_COL), bias_spec, chunk(0), chunk(0), chunk(0)],
        out_specs=[chunk(0), full(0), full(0), bias_spec],
        out_shape=[jax.ShapeDtypeStruct((s, GROUP_W), f32)] * 3
        + [jax.ShapeDtypeStruct((3, N_HEADS, BAND, 2 * BAND), f32)],
        scratch_shapes=[pltpu.VMEM((PAD_UNIT, 128), f32)],
        compiler_params=_cparams(("parallel", "arbitrary")),
    )(proj, proj, proj, bias, y, lse, dycat)


_HI = lax.Precision.HIGHEST
DELTA_COL = 1536
Z_COL = 3072
BA_BLOCK = 28
DELTA_ROWS = 512


def _hdot(a, b):
    return jnp.dot(a, b, precision=_HI, preferred_element_type=f32)


_DIMS = dict(nn=(((2,), (1,)), ((0,), (0,))), nt=(((2,), (2,)), ((0,), (0,))), tn=(((1,), (1,)), ((0,), (0,))))


@functools.partial(jax.custom_vjp, nondiff_argnums=(2,))
def _mmx(a, b, mode):
    return lax.dot_general(a.astype(bf16), b.astype(bf16), _DIMS[mode], preferred_element_type=f32)


def _mmx_fwd(a, b, mode):
    return _mmx(a, b, mode), (a, b)


def _mmx_bwd(mode, res, g):
    a, b = res
    if mode == "nn":
        return _mmx(g, b, "nt"), _mmx(a, g, "tn")
    if mode == "nt":
        return _mmx(g, b, "nn"), _mmx(g, a, "tn")
    return _mmx(b, g, "nt"), _mmx(a, g, "nn")


_mmx.defvjp(_mmx_fwd, _mmx_bwd)


def _pair_iota():
    row = lax.broadcasted_iota(jnp.int32, (CHUNK, 128), 0)
    lane = lax.broadcasted_iota(jnp.int32, (CHUNK, 128), 1)
    return row, lane, lane & (CHUNK - 1)


def _bd(x):
    _, lane, _ = _pair_iota()
    m0 = lane < CHUNK
    return jnp.concatenate([jnp.where(m0, x, 0.0), jnp.where(m0, 0.0, x)], axis=1)


def _pmm(a, b):
    return _mmx(a, _bd(b), "nn")


def _ntp(x, y):
    return _mmx(x, _bd(y), "nt")


def _tnp(x, y):
    full = _mmx(x, y, "tn")
    _, lane, _ = _pair_iota()
    return jnp.where(lane < CHUNK, full[:, :CHUNK], full[:, CHUNK:])


def _tri_inv(a):
    row, lane, jj = _pair_iota()
    eye = jnp.where(row == jj, 1.0, 0.0).astype(f32)

    def same_block(log2b):
        return (row >> log2b) == (jj >> log2b)

    dgl = jnp.where(same_block(3), a, 0.0)
    d2 = _pmm(dgl, dgl)
    d4 = _pmm(d2, d2)
    t = _pmm(_pmm(eye - dgl, eye + d2), eye + d4)
    for lb in (3, 4, 5):
        off = jnp.where(same_block(lb + 1) & jnp.logical_not(same_block(lb)), a, 0.0)
        t = t - _pmm(_pmm(t, off), t)
    return t


@jax.custom_vjp
def _solve2(a, xv, xk, t):
    return _pmm(t, xv), _pmm(t, xk)


def _solve2_fwd(a, xv, xk, t):
    u, w = _pmm(t, xv), _pmm(t, xk)
    return (u, w), (t, u, w)


def _solve2_bwd(res, cts):
    t, u, w = res
    du, dw = cts
    dxv = _tnp(t, du)
    dxk = _tnp(t, dw)
    return -(_ntp(dxv, u) + _ntp(dxk, w)), dxv, dxk, jnp.zeros_like(t)


_solve2.defvjp(_solve2_fwd, _solve2_bwd)


def _chunk_pre(qp, kp, vp, bp, gcum, t=None):
    row, lane, jj = _pair_iota()
    causal = row >= jj
    strict = row > jj
    rsel = jnp.sum(jnp.where(row == jj, gcum, 0.0), axis=1, keepdims=True)
    decay = jnp.where(causal, jnp.exp(jnp.where(causal, gcum - rsel, 0.0)), 0.0)
    kb = kp * bp
    kd = _bd(kp)
    a = jnp.where(strict, _mmx(kb, kd, "nt") * decay, 0.0)
    eg = jnp.exp(gcum)
    if t is None:
        t = _tri_inv(a)
    u, w = _solve2(a, vp * bp, kb * eg, t)
    qk = jnp.where(causal, _mmx(qp, kd, "nt") * decay, 0.0)
    glast = jnp.sum(jnp.where(row == CHUNK - 1, gcum, 0.0), axis=1, keepdims=True)
    return u, w, qp * eg, kp * jnp.exp(glast - gcum), qk, jnp.exp(glast), t


def _chunk_post(u, w, qt, kh, qk, gam, sp):
    sd = _bd(sp)
    vnew = u - _mmx(w, sd, "nn")
    o = _mmx(qt, sd, "nn") + _pmm(qk, vnew)
    return o, gam * sp + _tnp(kh, vnew)


def _pair_spec(rows=DELTA_ROWS):
    return pl.BlockSpec((rows, 128), lambda i, p: (i, p))


DELTA_NB = DELTA_ROWS // CHUNK


def _chunks(ref):
    return ref[...].reshape(DELTA_NB, CHUNK, 128)


def _pairs(ref, rows):
    return jnp.stack([ref[rows, p * 128:(p + 1) * 128] for p in range(4)], axis=0)


def _delta_chunk_pre(qn, kn, sv, beta, g):
    s = qn.shape[0]

    def body(q_ref, k_ref, v_ref, b_ref, g_ref, u_ref, w_ref, qt_ref, kh_ref, qk_ref, t_ref, gm_ref):
        outs = _chunk_pre(_chunks(q_ref), _chunks(k_ref), _chunks(v_ref), _chunks(b_ref), _chunks(g_ref))
        for ref, val in zip((u_ref, w_ref, qt_ref, kh_ref, qk_ref, t_ref), outs[:5] + outs[6:]):
            ref[...] = val.reshape(DELTA_ROWS, 128).astype(ref.dtype)
        gm_ref[...] = jnp.broadcast_to(outs[5], (DELTA_NB, 8, 128)).reshape(DELTA_NB * 8, 128)

    v_spec = pl.BlockSpec((DELTA_ROWS, 128), lambda i, p: (i, 8 + p))
    return pl.pallas_call(
        body, name="delta_chunk_pre", grid=(s // DELTA_ROWS, 4),
        in_specs=[_pair_spec(), _pair_spec(), v_spec, _pair_spec(), _pair_spec()],
        out_specs=[_pair_spec()] * 6 + [_pair_spec(DELTA_NB * 8)],
        out_shape=[jax.ShapeDtypeStruct((s, GROUP_W), f32)] + [jax.ShapeDtypeStruct((s, GROUP_W), bf16)] * 5
        + [jax.ShapeDtypeStruct((s // 8, GROUP_W), f32)],
        compiler_params=_cparams(("parallel", "parallel")),
    )(qn, kn, sv, beta, g)


def _delta_scan_fwd(u, w, qt, kh, qk, gm):
    s = u.shape[0]

    def body(u_ref, w_ref, qt_ref, kh_ref, qk_ref, gm_ref, o_ref, ss_ref, st):
        @pl.when(pl.program_id(0) == 0)
        def _():
            st[...] = jnp.zeros_like(st)

        def chunk(ci, carry):
            rows = pl.ds(pl.multiple_of(ci * CHUNK, CHUNK), CHUNK)
            grow = pl.ds(pl.multiple_of(ci * 8, 8), 1)
            sp = st[...]
            o, s2 = _chunk_post(_pairs(u_ref, rows), _pairs(w_ref, rows), _pairs(qt_ref, rows),
                                _pairs(kh_ref, rows), _pairs(qk_ref, rows), _pairs(gm_ref, grow), sp)
            for p in range(4):
                ss_ref[rows, p * 128:(p + 1) * 128] = sp[p]
                o_ref[rows, p * 128:(p + 1) * 128] = o[p]
            st[...] = s2
            return carry

        lax.fori_loop(0, DELTA_NB, chunk, 0)

    spec = pl.BlockSpec((DELTA_ROWS, GROUP_W), lambda i: (i, 0))
    gspec = pl.BlockSpec((DELTA_NB * 8, GROUP_W), lambda i: (i, 0))
    return pl.pallas_call(
        body, name="delta_scan_fwd", grid=(s // DELTA_ROWS,),
        in_specs=[spec] * 5 + [gspec],
        out_specs=[spec, spec],
        out_shape=[jax.ShapeDtypeStruct((s, GROUP_W), f32)] * 2,
        scratch_shapes=[pltpu.VMEM((4, CHUNK, 128), f32)],
        compiler_params=_cparams(("arbitrary",)),
    )(u, w, qt, kh, qk, gm)


def _delta_scan_bwd(w, qt, kh, qk, gm, do):
    s = w.shape[0]
    nb = s // DELTA_ROWS

    def body(w_ref, qt_ref, kh_ref, qk_ref, gm_ref, do_ref, dso_ref, dst):
        @pl.when(pl.program_id(0) == 0)
        def _():
            dst[...] = jnp.zeros_like(dst)

        def chunk(t, carry):
            ci = DELTA_NB - 1 - t
            rows = pl.ds(pl.multiple_of(ci * CHUNK, CHUNK), CHUNK)
            grow = pl.ds(pl.multiple_of(ci * 8, 8), 1)
            ds = dst[...]
            for p in range(4):
                dso_ref[rows, p * 128:(p + 1) * 128] = ds[p]
            do = _pairs(do_ref, rows)
            dvn = _tnp(_pairs(qk_ref, rows), do) + _pmm(_pairs(kh_ref, rows), ds)
            dst[...] = _tnp(_pairs(qt_ref, rows), do) + _pairs(gm_ref, grow) * ds - _tnp(_pairs(w_ref, rows), dvn)
            return carry

        lax.fori_loop(0, DELTA_NB, chunk, 0)

    spec = pl.BlockSpec((DELTA_ROWS, GROUP_W), lambda i: (nb - 1 - i, 0))
    gspec = pl.BlockSpec((DELTA_NB * 8, GROUP_W), lambda i: (nb - 1 - i, 0))
    return pl.pallas_call(
        body, name="delta_scan_bwd", grid=(nb,),
        in_specs=[spec] * 4 + [gspec, spec],
        out_specs=spec,
        out_shape=jax.ShapeDtypeStruct((s, GROUP_W), f32),
        scratch_shapes=[pltpu.VMEM((4, CHUNK, 128), f32)],
        compiler_params=_cparams(("arbitrary",)),
    )(w, qt, kh, qk, gm, do)


def _delta_chunk_bwd(qn, kn, sv, beta, g, tinv, ss, dso, do):
    s = qn.shape[0]

    def body(q_ref, k_ref, v_ref, b_ref, g_ref, t_ref, ss_ref, dso_ref, do_ref,
             dq_ref, dk_ref, dv_ref, db_ref, dg_ref):
        sp = _chunks(ss_ref)
        t = _chunks(t_ref)

        def fn(q, k, v, b, gg):
            return _chunk_post(*_chunk_pre(q, k, v, b, gg, t)[:6], sp)

        _, vjp = jax.vjp(fn, _chunks(q_ref), _chunks(k_ref), _chunks(v_ref), _chunks(b_ref), _chunks(g_ref))
        grads = vjp((_chunks(do_ref), _chunks(dso_ref)))
        for ref, val in zip((dq_ref, dk_ref, dv_ref, db_ref, dg_ref), grads):
            ref[...] = val.reshape(DELTA_ROWS, 128)

    v_spec = pl.BlockSpec((DELTA_ROWS, 128), lambda i, p: (i, 8 + p))
    return pl.pallas_call(
        body, name="delta_chunk_bwd", grid=(s // DELTA_ROWS, 4),
        in_specs=[_pair_spec(), _pair_spec(), v_spec] + [_pair_spec()] * 6,
        out_specs=[_pair_spec()] * 5,
        out_shape=[jax.ShapeDtypeStruct((s, GROUP_W), f32)] * 5,
        compiler_params=_cparams(("parallel", "parallel")),
    )(qn, kn, sv, beta, g, tinv, ss, dso, do)


def _head_sum_matrix():
    r = lax.broadcasted_iota(jnp.int32, (GROUP_W, GROUP_W), 0)
    c = lax.broadcasted_iota(jnp.int32, (GROUP_W, GROUP_W), 1)
    return jnp.where((r >> 6) == (c >> 6), 1.0, 0.0).astype(f32)


def _head_sums(x):
    return _mmx(x[None], _head_sum_matrix()[None], "nn")[0]


def _sel_dot(a, b):
    return jnp.dot(a, b, precision=lax.Precision.HIGH, preferred_element_type=f32)


def _softplus(x):
    return jnp.maximum(x, 0.0) + jnp.log(1.0 + jnp.exp(-jnp.abs(x)))


def _prep_fn(sq, sk, ba, alog_e, dt_e):
    qn = sq * lax.rsqrt(_head_sums(sq * sq) + EPS) * (HEAD_DIM ** -0.5)
    kn = sk * lax.rsqrt(_head_sums(sk * sk) + EPS)
    r = lax.broadcasted_iota(jnp.int32, (128, GROUP_W), 0)
    c = lax.broadcasted_iota(jnp.int32, (128, GROUP_W), 1) >> 6
    bl = _sel_dot(ba, jnp.where(r == c, 1.0, 0.0).astype(f32))
    al = _sel_dot(ba, jnp.where(r == c + N_HEADS, 1.0, 0.0).astype(f32))
    beta = jax.nn.sigmoid(bl)
    g = -jnp.exp(alog_e) * _softplus(al + dt_e)
    ri = lax.broadcasted_iota(jnp.int32, (TOK_TILE, TOK_TILE), 0)
    ci = lax.broadcasted_iota(jnp.int32, (TOK_TILE, TOK_TILE), 1)
    within = jnp.where(((ri >> 6) == (ci >> 6)) & (ri >= ci), 1.0, 0.0).astype(f32)
    return qn, kn, beta, _sel_dot(within, g)


def _gnorm_fn(o, z, ng_e):
    ms = _head_sums(o * o) * (1.0 / HEAD_DIM)
    return o * lax.rsqrt(ms + EPS) * ng_e * (z * jax.nn.sigmoid(z))


def _tok_spec(width, col):
    return pl.BlockSpec((TOK_TILE, width), lambda i: (i, col))


def _conv_taps(xs_ref, w_ref, base, n):
    acc = w_ref[CONV_WIDTH - 1:CONV_WIDTH, :] * xs_ref[pl.ds(base, n), :]
    for j in range(CONV_WIDTH - 1):
        acc = acc + w_ref[j:j + 1, :] * xs_ref[pl.ds(base - (CONV_WIDTH - 1) + j, n), :]
    return acc


def _conv_silu_fwd(proj, conv_w):
    s = proj.shape[0]
    wd = 3 * GROUP_W
    hb = TOK_TILE // 8

    def body(x_ref, halo_ref, w_ref, o_ref, xs):
        xs[0:8, :] = jnp.where(pl.program_id(0) > 0, halo_ref[...], 0.0)
        xs[8:, :] = x_ref[...]
        y = _conv_taps(xs, w_ref, 8, TOK_TILE)
        o_ref[...] = y * jax.nn.sigmoid(y)

    return pl.pallas_call(
        body, name="delta_conv_fwd", grid=(s // TOK_TILE,),
        in_specs=[_tok_spec(wd, 1), pl.BlockSpec((8, wd), lambda i: (jnp.maximum(i * hb - 1, 0), 1)),
                  pl.BlockSpec((CONV_WIDTH, wd), lambda i: (0, 0))],
        out_specs=_tok_spec(wd, 0),
        out_shape=jax.ShapeDtypeStruct((s, wd), f32),
        scratch_shapes=[pltpu.VMEM((TOK_TILE + 8, wd), f32)],
        compiler_params=_cparams(("parallel",)),
    )(proj, proj, conv_w)


def _conv_silu_bwd(proj, conv_w, ds3, xchg):
    s = proj.shape[0]
    wd = 3 * GROUP_W
    hb = TOK_TILE // 8
    nt = s // TOK_TILE

    def body(x_ref, hp_ref, hn_ref, dq_ref, dk_ref, dv_ref, dqn_ref, dkn_ref, dvn_ref, w_ref, dx_ref, dw_ref, xs, dys):
        i = pl.program_id(0)

        @pl.when(i == 0)
        def _():
            dw_ref[...] = jnp.zeros_like(dw_ref)

        last = i == nt - 1
        xs[0:8, :] = jnp.where(i > 0, hp_ref[...], 0.0)
        xs[8:8 + TOK_TILE, :] = x_ref[...]
        xs[8 + TOK_TILE:, :] = jnp.where(last, 0.0, hn_ref[...])
        y = _conv_taps(xs, w_ref, 8, TOK_TILE)
        sg = jax.nn.sigmoid(y)
        dsilu = sg * (1.0 + y * (1.0 - sg))
        yn = _conv_taps(xs, w_ref, 8 + TOK_TILE, 8)
        sgn = jax.nn.sigmoid(yn)
        dsilu_n = sgn * (1.0 + yn * (1.0 - sgn))
        for t, (cur, nxt) in enumerate(((dq_ref, dqn_ref), (dk_ref, dkn_ref), (dv_ref, dvn_ref))):
            cols = slice(t * GROUP_W, (t + 1) * GROUP_W)
            dys[0:TOK_TILE, cols] = cur[...] * dsilu[:, cols]
            dys[TOK_TILE:, cols] = jnp.where(last, 0.0, nxt[...]) * dsilu_n[:, cols]
        dy0 = dys[0:TOK_TILE, :]
        dx = w_ref[CONV_WIDTH - 1:CONV_WIDTH, :] * dy0
        for j in range(CONV_WIDTH - 1):
            dx = dx + w_ref[j:j + 1, :] * dys[pl.ds(CONV_WIDTH - 1 - j, TOK_TILE), :]
        dx_ref[...] = dx.astype(dx_ref.dtype)
        for j in range(CONV_WIDTH):
            dw_ref[j:j + 1, :] += jnp.sum(dy0 * xs[pl.ds(8 - (CONV_WIDTH - 1) + j, TOK_TILE), :],
                                          axis=0, keepdims=True)

    prev8 = lambda col: pl.BlockSpec((8, wd), lambda i: (jnp.maximum(i * hb - 1, 0), col))
    next8 = lambda col: pl.BlockSpec((8, wd), lambda i: (jnp.minimum((i + 1) * hb, s // 8 - 1), col))
    next8_third = pl.BlockSpec((8, GROUP_W), lambda i: (jnp.minimum((i + 1) * hb, s // 8 - 1), 0))
    out = pl.pallas_call(
        _ride(body, 10, 2, xchg, (nt,)), name="delta_conv_bwd", grid=(nt,),
        in_specs=[_tok_spec(wd, 1), prev8(1), next8(1)] + [_tok_spec(GROUP_W, 0)] * 3 + [next8_third] * 3
        + [pl.BlockSpec((CONV_WIDTH, wd), lambda i: (0, 0))] + [_ANY] * xchg.n,
        out_specs=[_tok_spec(wd, 0), pl.BlockSpec((CONV_WIDTH, wd), lambda i: (0, 0))] + [_ANY] * xchg.n,
        out_shape=[jax.ShapeDtypeStruct((s, wd), bf16), jax.ShapeDtypeStruct((CONV_WIDTH, wd), f32)] + xchg.out_shape(),
        scratch_shapes=[pltpu.VMEM((TOK_TILE + 16, wd), f32), pltpu.VMEM((TOK_TILE + 8, wd), f32)] + xchg.scratch(),
        compiler_params=_cparams(("arbitrary",)),
    )(proj, proj, proj, *ds3, *ds3, conv_w, *xchg.arrs)
    return out[:2], out[2:]


def _delta_prep_fwd(sconv, proj, alog_e, dt_e):
    s = sconv.shape[0]

    def body(sq_ref, sk_ref, ba_ref, al_ref, dt_ref, q_ref, k_ref, b_ref, g_ref):
        qn, kn, beta, g = _prep_fn(sq_ref[...], sk_ref[...], ba_ref[...], al_ref[...], dt_ref[...])
        q_ref[...] = qn
        k_ref[...] = kn
        b_ref[...] = beta
        g_ref[...] = g

    return pl.pallas_call(
        body, name="delta_prep_fwd", grid=(s // TOK_TILE,),
        in_specs=[_tok_spec(GROUP_W, 0), _tok_spec(GROUP_W, 1), _tok_spec(128, BA_BLOCK),
                  _vec_spec(GROUP_W), _vec_spec(GROUP_W)],
        out_specs=[_tok_spec(GROUP_W, 0)] * 4,
        out_shape=[jax.ShapeDtypeStruct((s, GROUP_W), f32)] * 4,
        compiler_params=_cparams(("parallel",)),
    )(sconv, sconv, proj, alog_e, dt_e)


def _delta_prep_bwd(sconv, proj, alog_e, dt_e, dqn, dkn, dbeta, dg):
    s = sconv.shape[0]

    def body(sq_ref, sk_ref, ba_ref, al_ref, dt_ref, dq_ref, dk_ref, db_ref, dg_ref,
             dsq_ref, dsk_ref, dba_ref, dal_ref, ddt_ref):
        @pl.when(pl.program_id(0) == 0)
        def _():
            dal_ref[...] = jnp.zeros_like(dal_ref)
            ddt_ref[...] = jnp.zeros_like(ddt_ref)

        _, vjp = jax.vjp(_prep_fn, sq_ref[...], sk_ref[...], ba_ref[...], al_ref[...], dt_ref[...])
        dsq, dsk, dba, dal, ddt = vjp((dq_ref[...], dk_ref[...], db_ref[...], dg_ref[...]))
        dsq_ref[...] = dsq
        dsk_ref[...] = dsk
        dba_ref[...] = dba.astype(bf16)
        dal_ref[...] += dal
        ddt_ref[...] += ddt

    return pl.pallas_call(
        body, name="delta_prep_bwd", grid=(s // TOK_TILE,),
        in_specs=[_tok_spec(GROUP_W, 0), _tok_spec(GROUP_W, 1), _tok_spec(128, BA_BLOCK),
                  _vec_spec(GROUP_W), _vec_spec(GROUP_W)] + [_tok_spec(GROUP_W, 0)] * 4,
        out_specs=[_tok_spec(GROUP_W, 0), _tok_spec(GROUP_W, 0), _tok_spec(128, 0),
                   _acc_spec(GROUP_W), _acc_spec(GROUP_W)],
        out_shape=[jax.ShapeDtypeStruct((s, GROUP_W), f32)] * 2 + [jax.ShapeDtypeStruct((s, 128), bf16)]
        + [jax.ShapeDtypeStruct((1, GROUP_W), f32)] * 2,
        compiler_params=_cparams(("arbitrary",)),
    )(sconv, sconv, proj, alog_e, dt_e, dqn, dkn, dbeta, dg)


def _gnorm_fwd(o, proj, ng_e):
    s = o.shape[0]

    def body(o_ref, z_ref, g_ref, y_ref):
        y_ref[...] = _gnorm_fn(o_ref[...], z_ref[...], g_ref[...])

    return pl.pallas_call(
        body, name="delta_gnorm_fwd", grid=(s // TOK_TILE,),
        in_specs=[_tok_spec(GROUP_W, 0), _tok_spec(GROUP_W, Z_COL // GROUP_W), _vec_spec(GROUP_W)],
        out_specs=_tok_spec(GROUP_W, 0),
        out_shape=jax.ShapeDtypeStruct((s, GROUP_W), f32),
        compiler_params=_cparams(("parallel",)),
    )(o, proj, ng_e)


def _gnorm_bwd(o, proj, ng_e, dycat):
    s = o.shape[0]

    def body(o_ref, z_ref, g_ref, dy_ref, do_ref, dz_ref, dg_ref):
        @pl.when(pl.program_id(0) == 0)
        def _():
            dg_ref[...] = jnp.zeros_like(dg_ref)

        _, vjp = jax.vjp(_gnorm_fn, o_ref[...], z_ref[...], g_ref[...])
        do, dz, dg = vjp(dy_ref[...])
        do_ref[...] = do
        dz_ref[...] = dz.astype(bf16)
        dg_ref[...] += dg

    return pl.pallas_call(
        body, name="delta_gnorm_bwd", grid=(s // TOK_TILE,),
        in_specs=[_tok_spec(GROUP_W, 0), _tok_spec(GROUP_W, Z_COL // GROUP_W), _vec_spec(GROUP_W),
                  _tok_spec(GROUP_W, 1)],
        out_specs=[_tok_spec(GROUP_W, 0), _tok_spec(GROUP_W, 0), _acc_spec(GROUP_W)],
        out_shape=[jax.ShapeDtypeStruct((s, GROUP_W), f32), jax.ShapeDtypeStruct((s, GROUP_W), bf16),
                   jax.ShapeDtypeStruct((1, GROUP_W), f32)],
        compiler_params=_cparams(("arbitrary",)),
    )(o, proj, ng_e, dycat)


_MESH = pl.DeviceIdType.MESH
_ANY = pl.BlockSpec(memory_space=pl.ANY)
_VMEM = pl.BlockSpec(memory_space=pltpu.VMEM)


def _my_place():
    x, y, c = lax.axis_index("x"), lax.axis_index("y"), lax.axis_index("c")
    return x, y, c, 4 * x + 2 * y + c


def _peer(k, x, y, c):
    px = 1 - x if k & 4 else x
    py = 1 - y if k & 2 else y
    pc = 1 - c if k & 1 else c
    return (px, py, pc), 4 * px + 2 * py + pc


def _exchange_all(src_of_peer, dst_ref, send_sems, recv_sems, x, y, c, me):
    sent = []
    for k in range(1, N_DEV):
        dev, pidx = _peer(k, x, y, c)
        cp = pltpu.make_async_remote_copy(src_ref=src_of_peer(pidx), dst_ref=dst_ref.at[me],
                                          send_sem=send_sems.at[k - 1], recv_sem=recv_sems.at[k - 1],
                                          device_id=dev, device_id_type=_MESH)
        cp.start()
        sent.append(cp)
    for k in range(1, N_DEV):
        dev, pidx = _peer(k, x, y, c)
        pltpu.make_async_remote_copy(src_ref=src_of_peer(pidx), dst_ref=dst_ref.at[pidx],
                                     send_sem=send_sems.at[k - 1], recv_sem=recv_sems.at[k - 1],
                                     device_id=dev, device_id_type=_MESH).wait_recv()
    for cp in sent:
        cp.wait_send()


def _ada_exchange(cv8, w_ada, b_ada8, w_in_sh):
    ride = _ChipGather(w_in_sh)

    def body(cv_ref, w_ref, b_ref, wi_ref, call_ref, modp_ref, wig_ref, part_s, s1, r1, s2, r2, *ride_sems):
        ride.start(wi_ref, wig_ref, ride_sems)
        x, y, c, me = _my_place()
        call_ref[me] = cv_ref[...]
        _exchange_all(lambda pidx: cv_ref, call_ref, s1, r1, x, y, c, me)
        bias = b_ref[me]
        for j in range(N_DEV):
            cj = call_ref[j][:, :D_MODEL]
            part_s[j] = _hdot(cj * jax.nn.sigmoid(cj), w_ref[...]) + bias
        modp_ref[me] = part_s[me]
        ride.forward(wi_ref, wig_ref, ride_sems)
        _exchange_all(lambda pidx: part_s.at[pidx], modp_ref, s2, r2, x, y, c, me)
        ride.finish(wi_ref, wig_ref, ride_sems)

    nsh = w_ada.shape[1]
    return pl.pallas_call(
        body, name="ada_exchange",
        in_specs=[_VMEM, _VMEM, _VMEM, _ANY], out_specs=[_VMEM, _VMEM, _ANY],
        out_shape=[jax.ShapeDtypeStruct((N_DEV, 8, cv8.shape[1]), f32), jax.ShapeDtypeStruct((N_DEV, 8, nsh), f32)]
        + [ride.out_shape()],
        scratch_shapes=[pltpu.VMEM((N_DEV, 8, nsh), f32)] + [pltpu.SemaphoreType.DMA((N_DEV - 1,))] * 4
        + ride.scratch(),
        compiler_params=pltpu.CompilerParams(vmem_limit_bytes=VMEM_LIMIT),
    )(cv8, w_ada, b_ada8, w_in_sh)


def _all_to_all(arrs, name):
    ex = _Exchange(arrs, gather=False)

    def body(*refs):
        srcs, dsts, sems = refs[:ex.n], refs[ex.n:2 * ex.n], refs[2 * ex.n:]
        ex.start(srcs, dsts, sems)
        ex.wait(srcs, dsts, sems)

    return pl.pallas_call(
        body, name=name,
        in_specs=[_ANY] * ex.n, out_specs=[_ANY] * ex.n,
        out_shape=ex.out_shape(), scratch_shapes=ex.scratch(),
    )(*arrs)


class _Exchange:
    def __init__(self, arrs, gather):
        self.arrs, self.gather, self.n = list(arrs), gather, len(arrs)

    def out_shape(self):
        return [jax.ShapeDtypeStruct(((N_DEV,) + a.shape) if self.gather else a.shape, a.dtype) for a in self.arrs]

    def scratch(self):
        if self.n == 0:
            return []
        return [pltpu.SemaphoreType.DMA((self.n, N_DEV - 1)), pltpu.SemaphoreType.DMA((self.n, N_DEV - 1)),
                pltpu.SemaphoreType.DMA((self.n,))]

    def _src(self, srcs, a, idx):
        return srcs[a] if self.gather else srcs[a].at[idx]

    def _copies(self, srcs, dsts, sems, incoming):
        send_sems, recv_sems, _ = sems
        x, y, c, me = _my_place()
        out = []
        for a in range(self.n):
            for k in range(1, N_DEV):
                dev, pidx = _peer(k, x, y, c)
                out.append(pltpu.make_async_remote_copy(
                    src_ref=self._src(srcs, a, pidx), dst_ref=dsts[a].at[pidx if incoming else me],
                    send_sem=send_sems.at[a, k - 1], recv_sem=recv_sems.at[a, k - 1],
                    device_id=dev, device_id_type=_MESH))
        return out

    def _local(self, srcs, dsts, sems):
        me = _my_place()[3]
        return [pltpu.make_async_copy(self._src(srcs, a, me), dsts[a].at[me], sems[2].at[a]) for a in range(self.n)]

    def start(self, srcs, dsts, sems):
        for cp in self._local(srcs, dsts, sems) + self._copies(srcs, dsts, sems, incoming=False):
            cp.start()

    def wait(self, srcs, dsts, sems):
        for cp in self._copies(srcs, dsts, sems, incoming=True):
            cp.wait_recv()
        for cp in self._copies(srcs, dsts, sems, incoming=False):
            cp.wait_send()
        for cp in self._local(srcs, dsts, sems):
            cp.wait()

    def start_at_first_step(self, grid, srcs, dsts, sems):
        first = functools.reduce(jnp.logical_and, [pl.program_id(i) == 0 for i in range(len(grid))])
        pl.when(first)(lambda: self.start(srcs, dsts, sems))

    def wait_at_last_step(self, grid, srcs, dsts, sems):
        last = functools.reduce(jnp.logical_and, [pl.program_id(i) == g - 1 for i, g in enumerate(grid)])
        pl.when(last)(lambda: self.wait(srcs, dsts, sems))


class _ChipGather:
    def __init__(self, shard):
        self.shard = shard

    def out_shape(self):
        return jax.ShapeDtypeStruct((N_DEV,) + self.shard.shape, self.shard.dtype)

    def scratch(self):
        return [pltpu.SemaphoreType.DMA((N_DEV - 1,)), pltpu.SemaphoreType.DMA((N_DEV - 1,)),
                pltpu.SemaphoreType.DMA(())]

    def _place(self):
        x, y, c, me = _my_place()
        return x, y, c, me, (x, y, 1 - c), [(1 - x, y), (x, 1 - y), (1 - x, 1 - y)]

    def _copy(self, out, sems, k, block, to, src=None):
        rows = out.at[4 * block[0] + 2 * block[1] + block[2]]
        return pltpu.make_async_remote_copy(src_ref=rows if src is None else src, dst_ref=rows,
                                            send_sem=sems[0].at[k], recv_sem=sems[1].at[k],
                                            device_id=to, device_id_type=_MESH)

    def start(self, src, out, sems):
        x, y, c, me, sib, chips = self._place()
        pltpu.make_async_copy(src, out.at[me], sems[2]).start()
        self._copy(out, sems, 0, (x, y, c), sib, src=src).start()
        for j, chip in enumerate(chips):
            self._copy(out, sems, 1 + j, (x, y, c), (*chip, c), src=src).start()

    def forward(self, src, out, sems):
        x, y, c, me, sib, chips = self._place()
        for j, chip in enumerate(chips):
            self._copy(out, sems, 1 + j, (*chip, c), (x, y, c)).wait_recv()
            self._copy(out, sems, 4 + j, (*chip, c), sib).start()

    def finish(self, src, out, sems):
        x, y, c, me, sib, chips = self._place()
        self._copy(out, sems, 0, (x, y, 1 - c), (x, y, c)).wait_recv()
        for j, chip in enumerate(chips):
            self._copy(out, sems, 4 + j, (*chip, 1 - c), (x, y, c)).wait_recv()
        self._copy(out, sems, 0, (x, y, c), sib, src=src).wait_send()
        for j, chip in enumerate(chips):
            self._copy(out, sems, 1 + j, (x, y, c), (*chip, c), src=src).wait_send()
            self._copy(out, sems, 4 + j, (*chip, c), sib).wait_send()
        pltpu.make_async_copy(src, out.at[me], sems[2]).wait()


def _ride(body, n_in, n_out, xchg, grid):
    nx = xchg.n
    if nx == 0:
        return body

    def wrapped(*refs):
        ins, xs = refs[:n_in], refs[n_in:n_in + nx]
        outs, xd = refs[n_in + nx:n_in + nx + n_out], refs[n_in + nx + n_out:n_in + 2 * nx + n_out]
        scratch = refs[n_in + 2 * nx + n_out:]
        xchg.start_at_first_step(grid, xs, xd, scratch[-3:])
        body(*ins, *outs, *scratch[:-3])
        xchg.wait_at_last_step(grid, xs, xd, scratch[-3:])

    return wrapped


def _adamw_math(w, g, m, v):
    m2 = ADAM_B1 * m + (1.0 - ADAM_B1) * g
    v2 = ADAM_B2 * v + (1.0 - ADAM_B2) * (g * g)
    m_hat = m2 / (1.0 - ADAM_B1 ** ADAM_STEP)
    v_hat = v2 / (1.0 - ADAM_B2 ** ADAM_STEP)
    delta = -ADAM_LR * (m_hat / (jnp.sqrt(v_hat) + ADAM_EPS) + ADAM_WD * w)
    return delta, m2, v2


def _row_tile(rows):
    for t in (256, 128, 64, 32, 16, 8):
        if rows % t == 0:
            return t
    return rows


def _reduce_adamw(parts, w, m, v, name):
    _, r, cdim = parts.shape
    tr = _row_tile(r)

    def body(p_ref, w_ref, m_ref, v_ref, g_ref, d_ref, m2_ref, v2_ref):
        g = p_ref[0].astype(f32)
        for j in range(1, N_DEV):
            g = g + p_ref[j].astype(f32)
        delta, m2, v2 = _adamw_math(w_ref[...], g, m_ref[...], v_ref[...])
        g_ref[...] = g
        d_ref[...] = delta
        m2_ref[...] = m2
        v2_ref[...] = v2

    spec = pl.BlockSpec((tr, cdim), lambda i: (i, 0))
    return pl.pallas_call(
        body, name=name, grid=(r // tr,),
        in_specs=[pl.BlockSpec((N_DEV, tr, cdim), lambda i: (0, i, 0)), spec, spec, spec],
        out_specs=[spec] * 4,
        out_shape=[jax.ShapeDtypeStruct((r, cdim), f32)] * 4,
        compiler_params=_cparams(("parallel",)),
    )(parts, w, m, v)


def _adamw(w, g, m, v, name):
    r, cdim = w.shape
    tr = _row_tile(r)

    def body(w_ref, g_ref, m_ref, v_ref, d_ref, m2_ref, v2_ref):
        delta, m2, v2 = _adamw_math(w_ref[...], g_ref[...], m_ref[...], v_ref[...])
        d_ref[...] = delta
        m2_ref[...] = m2
        v2_ref[...] = v2

    spec = pl.BlockSpec((tr, cdim), lambda i: (i, 0))
    return pl.pallas_call(
        body, name=name, grid=(r // tr,),
        in_specs=[spec] * 4, out_specs=[spec] * 3,
        out_shape=[jax.ShapeDtypeStruct((r, cdim), f32)] * 3,
        compiler_params=_cparams(("parallel",)),
    )(w, g, m, v)


def _sum_devices(parts, name):
    _, r, cdim = parts.shape

    def body(p_ref, o_ref):
        g = p_ref[0]
        for j in range(1, N_DEV):
            g = g + p_ref[j]
        o_ref[...] = g

    return pl.pallas_call(
        body, name=name, out_shape=jax.ShapeDtypeStruct((r, cdim), f32),
        in_specs=[_VMEM], out_specs=_VMEM,
    )(parts)


def _ada_wgrad(c_all8, dmod_cols):
    nsh = dmod_cols.shape[1]

    def body(c_ref, d_ref, o_ref):
        cv = c_ref[...]
        o_ref[...] = lax.dot_general(cv * jax.nn.sigmoid(cv), d_ref[...], _TN, precision=_HI,
                                     preferred_element_type=f32)

    return pl.pallas_call(
        body, name="ada_wgrad", out_shape=jax.ShapeDtypeStruct((D_MODEL, nsh), f32),
        in_specs=[_VMEM, _VMEM], out_specs=_VMEM,
        compiler_params=pltpu.CompilerParams(vmem_limit_bytes=VMEM_LIMIT),
    )(c_all8, dmod_cols)


def _cols(t):
    return t.transpose(1, 0, 2).reshape(t.shape[1], N_DEV * t.shape[2])


def _col_blocks(t, n):
    return t.reshape(t.shape[0], N_DEV, n).transpose(1, 0, 2).astype(bf16)


def _row_blocks(t):
    return t.reshape(N_DEV, t.shape[0] // N_DEV, t.shape[1]).astype(bf16)


def _local_step(x, tgt, mod, norm_attn_g, w_in_p, rel_bias, conv_full, a_log, dt_bias, delta_norm_g,
                norm_ffn_g, final_norm_g, w_out_sh, w_gate_sh, w_up_sh, w_down_sh):
    s = x.shape[0]
    sh1, sc1, g1, sh2, sc2, g2 = [mod[:, i * D_MODEL:(i + 1) * D_MODEL] for i in range(6)]
    nag = norm_attn_g.reshape(1, D_MODEL)
    nfg = norm_ffn_g.reshape(1, D_MODEL)
    fg = final_norm_g.reshape(1, D_MODEL)
    idx = _bucket_tables()
    bias = _bias_tables(rel_bias, idx)
    alog_e = jnp.repeat(a_log.reshape(N_HEADS), HEAD_DIM)[None]
    dt_e = jnp.repeat(dt_bias.reshape(N_HEADS), HEAD_DIM)[None]
    ng_e = jnp.tile(delta_norm_g.reshape(HEAD_DIM), N_HEADS)[None]

    h1 = _ln_mod_fwd(x, nag, sc1, sh1, "ln1_fwd")
    proj, (w_out_g, w_gate_g) = _mm(h1, w_in_p, "nn", f32, 512, IN_PAD, 1024, "in_proj",
                                    xchg=_Exchange([w_out_sh, w_gate_sh], gather=True))
    (y_attn, lse), (w_up_g, w_down_g) = _attn_fwd(proj, bias, _Exchange([w_up_sh, w_down_sh], gather=True))
    w_out_b = w_out_g.reshape(2 * GROUP_W, D_MODEL)
    w_gate_b, w_up_b = _cols(w_gate_g), _cols(w_up_g)
    w_down_b = w_down_g.reshape(D_FF, D_MODEL)
    n_ff = w_gate_sh.shape[1]
    sconv = _conv_silu_fwd(proj, conv_full)
    qn, kn, beta, g = _delta_prep_fwd(sconv, proj, alog_e, dt_e)
    u, w, qt, kh, qk, tinv, gm = _delta_chunk_pre(qn, kn, sconv, beta, g)
    o, ss = _delta_scan_fwd(u, w, qt, kh, qk, gm)
    y_delta = _gnorm_fwd(o, proj, ng_e)
    y, x1, h2 = _proj_resid_ln_mod_fwd([(y_attn, w_out_b[:GROUP_W]), (y_delta, w_out_b[GROUP_W:])],
                                       x, g1, nfg, sc2, sh2, "out_proj_ln2")
    act, gate, up = _ffn_up(h2, w_gate_b, w_up_b, "ffn_up")
    dx2, dy2, loss, dfg, dg2 = _proj_final_loss_bwd(act, w_down_b, x1, g2, fg, tgt, "ffn_down_loss")

    dgate, dup = _ffn_down_dx(dy2, w_down_b, gate, up, "ffn_down_dx")
    g_down = _mm(act, dy2, "tn", f32, 1408, 1024, 1024, "ffn_down_dw")
    (dx1, dsh2, dsc2, dnfg, dy, dg1), (r_down,) = _proj_ln_mod_bwd(
        [(dgate, w_gate_b), (dup, w_up_b)], x1, nfg, sc2, dx2, 256, "ffn_up_dx_ln2",
        _Exchange([_row_blocks(g_down)], gather=False), gate=g1, y=y)
    g_gate = _mm(h2, dgate, "tn", f32, 1024, 1408, 1024, "ffn_gate_dw")
    g_up = _mm(h2, dup, "tn", f32, 1024, 1408, 1024, "ffn_up_dw")
    dycat = _mm(dy, w_out_b, "nt", f32, 512, 1024, 1024, "out_proj_dx")
    g_out = jnp.concatenate([_mm(y_attn, dy, "tn", f32, GROUP_W, 1024, 1024, "out_proj_dw_attn"),
                             _mm(y_delta, dy, "tn", f32, GROUP_W, 1024, 1024, "out_proj_dw_delta")], axis=0)
    dq, dk, dv, dbias = _attn_bwd(proj, bias, y_attn, lse, dycat)
    g_rb = _bias_grad(dbias, idx)[:, :, 0].T
    do, dz, dng = _gnorm_bwd(o, proj, ng_e, dycat)
    dso = _delta_scan_bwd(w, qt, kh, qk, gm, do)
    dqn, dkn, dvd, dbeta, dgd = _delta_chunk_bwd(qn, kn, sconv, beta, g, tinv, ss, dso, do)
    dsq, dsk, dba, dal, ddt = _delta_prep_bwd(sconv, proj, alog_e, dt_e, dqn, dkn, dbeta, dgd)
    (dxc, g_conv), (r_gate, r_up, r_out) = _conv_silu_bwd(
        proj, conv_full, (dsq, dsk, dvd),
        _Exchange([_col_blocks(g_gate, n_ff), _col_blocks(g_up, n_ff), _row_blocks(g_out)],
                  gather=False))
    pieces = ((dq, 0), (dk, GROUP_W), (dv, 2 * GROUP_W), (dxc, DELTA_COL), (dz, Z_COL), (dba, BA_BLOCK * 128))
    g_in = jnp.concatenate(
        [_mm(h1, p, "tn", f32, 1024, min(p.shape[1], 768), 1024, "in_proj_dw_%d" % c) for p, c in pieces], axis=1)
    (gx, dsh1, dsc1, dnag), (r_in,) = _proj_ln_mod_bwd(
        [(p, w_in_p[:, c:c + p.shape[1]]) for p, c in pieces], x, nag, sc1, dx1, TOK_TILE, "in_proj_dx_ln1",
        _Exchange([_col_blocks(g_in[:, :IN_WIDTH], IN_WIDTH // N_DEV)], gather=False))
    grads = dict(
        x=gx, mod=jnp.concatenate([dsh1, dsc1, dg1, dsh2, dsc2, dg2], axis=1),
        norm_attn_g=dnag, norm_ffn_g=dnfg, final_norm_g=dfg, rel_bias=g_rb, conv_w=g_conv,
        a_log=dal.reshape(N_HEADS, HEAD_DIM).sum(-1), dt_bias=ddt.reshape(N_HEADS, HEAD_DIM).sum(-1),
        delta_norm_g=dng.reshape(N_HEADS, HEAD_DIM).sum(0),
        w_in=r_in, w_out=r_out, w_gate=r_gate, w_up=r_up, w_down=r_down)
    return loss[0, 0], grads


MISC_OFF = dict(rel_bias=0, a_log=256, dt_bias=264, delta_norm_g=272)


def _misc_row(rel_bias, a_log, dt_bias, delta_norm_g):
    flat = jnp.concatenate([rel_bias.reshape(-1), a_log.reshape(-1), dt_bias.reshape(-1), delta_norm_g.reshape(-1)])
    return jnp.pad(flat, (0, D_MODEL - flat.shape[0]))[None]


def _pack_small(b_ada, nag, nfg, fng, rel_bias, a_log, dt_bias, dng, conv_shard):
    rows = [b_ada.reshape(6, D_MODEL), nag.reshape(1, D_MODEL), nfg.reshape(1, D_MODEL), fng.reshape(1, D_MODEL),
            _misc_row(rel_bias, a_log, dt_bias, dng),
            jnp.pad(conv_shard.reshape(-1), (0, D_MODEL - conv_shard.size))[None],
            jnp.zeros((5, D_MODEL), f32)]
    return jnp.concatenate(rows, axis=0)


def _unpack_small(p, conv_shape):
    misc = p[9]
    return dict(
        b_ada=p[0:6].reshape(1, 6 * D_MODEL), norm_attn_g=p[6:7], norm_ffn_g=p[7:8], final_norm_g=p[8],
        rel_bias=misc[0:256].reshape(N_BUCKETS, N_HEADS), a_log=misc[256:264].reshape(1, N_HEADS),
        dt_bias=misc[264:272].reshape(1, N_HEADS), delta_norm_g=misc[272:336].reshape(1, HEAD_DIM),
        conv_w=p[10, :conv_shape[1] * conv_shape[2]].reshape(conv_shape))


def kernel(x, c, w_ada, b_ada, norm_attn_g, w_in, rel_bias, conv_w, a_log, dt_bias, delta_norm_g, w_out, norm_ffn_g, w_gate, w_up, w_down, final_norm_g, loss_target, m_w_ada, m_b_ada, m_norm_attn_g, m_w_in, m_rel_bias, m_conv_w, m_a_log, m_dt_bias, m_delta_norm_g, m_w_out, m_norm_ffn_g, m_w_gate, m_w_up, m_w_down, m_final_norm_g, v_w_ada, v_b_ada, v_norm_attn_g, v_w_in, v_rel_bias, v_conv_w, v_a_log, v_dt_bias, v_delta_norm_g, v_w_out, v_norm_ffn_g, v_w_gate, v_w_up, v_w_down, v_final_norm_g):
    me = 4 * lax.axis_index("x") + 2 * lax.axis_index("y") + lax.axis_index("c")
    ada_sh = w_ada.shape[2]
    conv_sh = conv_w.shape[2]

    cv = jnp.concatenate([c[0], conv_w[0].reshape(-1)])
    cv8 = jnp.zeros((8, 2 * D_MODEL), f32).at[0, :cv.shape[0]].set(cv)
    b8 = jnp.broadcast_to(b_ada.reshape(N_DEV, 1, ada_sh), (N_DEV, 8, ada_sh))
    call, modp, w_in_g = _ada_exchange(cv8, w_ada[0], b8, w_in[0].astype(bf16))
    mod = modp[:, 0, :].reshape(1, 6 * D_MODEL)
    c_all = call[:, 0, :D_MODEL]
    conv_full = call[:, 0, D_MODEL:D_MODEL + CONV_WIDTH * conv_sh].reshape(N_DEV, CONV_WIDTH, conv_sh)
    conv_full = conv_full.transpose(1, 0, 2).reshape(CONV_WIDTH, N_DEV * conv_sh)

    w_in_p = jnp.pad(_cols(w_in_g), ((0, 0), (0, IN_PAD - IN_WIDTH)))
    loss_local, gr = _local_step(x[0], loss_target[0], mod, norm_attn_g, w_in_p, rel_bias, conv_full, a_log,
                                 dt_bias, delta_norm_g, norm_ffn_g, final_norm_g, w_out[0].astype(bf16),
                                 w_gate[0].astype(bf16), w_up[0].astype(bf16), w_down[0].astype(bf16))
    loss = lax.psum(loss_local, ("x", "y", "c"))

    small = jnp.concatenate([
        gr["mod"].reshape(6, D_MODEL), gr["norm_attn_g"], gr["norm_ffn_g"], gr["final_norm_g"],
        gr["conv_w"].reshape(6, D_MODEL),
        _misc_row(gr["rel_bias"], gr["a_log"], gr["dt_bias"], gr["delta_norm_g"])], axis=0)
    parts = _all_to_all([jnp.broadcast_to(small[None], (N_DEV,) + small.shape)], "small_gather")[0]
    tot = _sum_devices(parts, "small_sum")
    g_conv_full = tot[9:15].reshape(CONV_WIDTH, N_DEV * conv_sh)
    g_conv = lax.dynamic_slice(g_conv_full, (0, me * conv_sh), (CONV_WIDTH, conv_sh))
    misc = tot[15]
    g_small = _pack_small(tot[0:6], tot[6], tot[7], tot[8], misc[0:256], misc[256:264], misc[264:272],
                          misc[272:336], g_conv)
    pk = lambda pre: _pack_small(pre[0], pre[1], pre[2], pre[3], pre[4], pre[5], pre[6], pre[7], pre[8])
    w_small = pk((b_ada, norm_attn_g, norm_ffn_g, final_norm_g, rel_bias, a_log, dt_bias, delta_norm_g, conv_w))
    m_small = pk((m_b_ada, m_norm_attn_g, m_norm_ffn_g, m_final_norm_g, m_rel_bias, m_a_log, m_dt_bias,
                  m_delta_norm_g, m_conv_w))
    v_small = pk((v_b_ada, v_norm_attn_g, v_norm_ffn_g, v_final_norm_g, v_rel_bias, v_a_log, v_dt_bias,
                  v_delta_norm_g, v_conv_w))
    d_small, m2_small, v2_small = _adamw(w_small, g_small, m_small, v_small, "adamw_small")
    cshape = conv_w.shape
    G, Dl, M2, V2 = (_unpack_small(t, cshape) for t in (g_small, d_small, m2_small, v2_small))

    dmod_all = parts[:, 0:6, :].reshape(N_DEV, 6 * D_MODEL)
    dmod_cols = lax.dynamic_slice(dmod_all, (0, me * ada_sh), (N_DEV, ada_sh))
    g_ada = _ada_wgrad(c_all, dmod_cols)
    d_ada, m2_ada, v2_ada = _adamw(w_ada[0], g_ada, m_w_ada[0], v_w_ada[0], "adamw_w_ada")

    big = {}
    for name, w_, m_, v_ in (("w_in", w_in, m_w_in, v_w_in), ("w_out", w_out, m_w_out, v_w_out),
                             ("w_gate", w_gate, m_w_gate, v_w_gate), ("w_up", w_up, m_w_up, v_w_up),
                             ("w_down", w_down, m_w_down, v_w_down)):
        big[name] = [t[None] for t in _reduce_adamw(gr[name], w_[0], m_[0], v_[0], "reduce_adamw_" + name)]

    def leaf(i, name):
        if name == "w_ada":
            return (g_ada, d_ada, m2_ada, v2_ada)[i][None]
        if name in big:
            return big[name][i]
        return (G, Dl, M2, V2)[i][name]

    order = ["w_ada", "b_ada", "norm_attn_g", "w_in", "rel_bias", "conv_w", "a_log", "dt_bias", "delta_norm_g",
             "w_out", "norm_ffn_g", "w_gate", "w_up", "w_down", "final_norm_g"]
    outs = [loss, gr["x"][None]]
    for i in range(4):
        outs += [leaf(i, n) for n in order]
    return tuple(outs)
```

```python
import functools
import math

import jax
import jax.numpy as jnp
from jax import lax
from jax.experimental import pallas as pl
from jax.experimental.pallas import tpu as pltpu

f32 = jnp.float32
bf16 = jnp.bfloat16

D_MODEL = 1024
HEAD_DIM = 64
N_HEADS = 8
GROUP_W = 512
IN_WIDTH = 3600
IN_PAD = 3840
D_FF = 2816
EPS = 1e-6
NEG_INF = -1e30
BAND = 128
PAD_UNIT = 2048
DILATIONS = (1, 4, 16)
N_BUCKETS = 32
MAX_DISTANCE = 2048
CONV_WIDTH = 4
CHUNK = 64
N_DEV = 8
VMEM_LIMIT = 56 * 1024 * 1024

ADAM_LR, ADAM_B1, ADAM_B2, ADAM_EPS, ADAM_WD, ADAM_STEP = 0.001, 0.9, 0.999, 1e-08, 0.01, 10


def _cparams(sem):
    return pltpu.CompilerParams(dimension_semantics=sem, vmem_limit_bytes=VMEM_LIMIT)


def _mm(a, b, mode, out_dtype, tm, tn, tk, name, xchg=None):
    if mode == "nn":
        (m, k), (_, n) = a.shape, b.shape
        a_spec = pl.BlockSpec((tm, tk), lambda j, i, kk: (i, kk))
        b_spec = pl.BlockSpec((tk, tn), lambda j, i, kk: (kk, j))
        dims = (((1,), (0,)), ((), ()))
    elif mode == "nt":
        (m, k), (n, _) = a.shape, b.shape
        a_spec = pl.BlockSpec((tm, tk), lambda j, i, kk: (i, kk))
        b_spec = pl.BlockSpec((tn, tk), lambda j, i, kk: (j, kk))
        dims = (((1,), (1,)), ((), ()))
    else:
        (k, m), (_, n) = a.shape, b.shape
        a_spec = pl.BlockSpec((tk, tm), lambda j, i, kk: (kk, i))
        b_spec = pl.BlockSpec((tk, tn), lambda j, i, kk: (kk, j))
        dims = (((0,), (0,)), ((), ()))
    assert m % tm == 0 and n % tn == 0 and k % tk == 0, (name, m, n, k, tm, tn, tk)
    nk = k // tk
    grid = (n // tn, m // tm, nk)
    nx = xchg.n if xchg is not None else 0

    def body(*refs):
        a_ref, b_ref = refs[:2]
        o_ref = refs[2 + nx]
        scratch = refs[3 + 2 * nx:]
        if nx:
            xrefs = (refs[2:2 + nx], refs[3 + nx:3 + 2 * nx], scratch[-3:])
            xchg.start_at_first_step(grid, *xrefs)
        if nk == 1:
            o_ref[...] = lax.dot_general(a_ref[...].astype(bf16), b_ref[...].astype(bf16), dims,
                                         preferred_element_type=f32).astype(o_ref.dtype)
        else:
            acc_ref = scratch[0]
            kk = pl.program_id(2)

            @pl.when(kk == 0)
            def _():
                acc_ref[...] = jnp.zeros_like(acc_ref)

            acc_ref[...] += lax.dot_general(a_ref[...].astype(bf16), b_ref[...].astype(bf16), dims,
                                            preferred_element_type=f32)

            @pl.when(kk == nk - 1)
            def _():
                o_ref[...] = acc_ref[...].astype(o_ref.dtype)
        if nx:
            xchg.wait_at_last_step(grid, *xrefs)

    out = pl.pallas_call(
        body, name=name, grid=grid,
        in_specs=[a_spec, b_spec] + ([_ANY] * nx),
        out_specs=[pl.BlockSpec((tm, tn), lambda j, i, kk: (i, j))] + ([_ANY] * nx),
        out_shape=[jax.ShapeDtypeStruct((m, n), out_dtype)] + (xchg.out_shape() if nx else []),
        scratch_shapes=([pltpu.VMEM((tm, tn), f32)] if nk > 1 else []) + (xchg.scratch() if nx else []),
        compiler_params=_cparams(("arbitrary",) * 3 if nx else ("parallel", "parallel", "arbitrary")),
    )(a, b, *(xchg.arrs if nx else []))
    return (out[0], out[1:]) if nx else out[0]


TOK_TILE = 512
SUB_COLS = 384


def _row_spec(width, tile=TOK_TILE):
    return pl.BlockSpec((tile, width), lambda i: (i, 0))


def _vec_spec(width, rows=1):
    return pl.BlockSpec((rows, width), lambda i: (0, 0))


def _ln_mod_fwd(x, gain, sc, sh, shard, name):
    s, d = x.shape
    nt = s // TOK_TILE
    ride = _ChipGather(shard)

    def body(x_ref, g_ref, sc_ref, sh_ref, sh_in, h_ref, sh_out, *sems):
        i = pl.program_id(0)
        pl.when(i == 0)(lambda: ride.start(sh_in, sh_out, sems))
        xv = x_ref[...]
        rstd = lax.rsqrt(jnp.mean(xv * xv, axis=-1, keepdims=True) + EPS)
        h = (xv * rstd) * g_ref[...] * (1.0 + sc_ref[...]) + sh_ref[...]
        h_ref[...] = h.astype(bf16)
        pl.when(i == nt // 2)(lambda: ride.forward(sh_in, sh_out, sems))
        pl.when(i == nt - 1)(lambda: ride.finish(sh_in, sh_out, sems))

    return pl.pallas_call(
        body, name=name, grid=(nt,),
        in_specs=[_row_spec(d), _vec_spec(d), _vec_spec(d), _vec_spec(d), _ANY],
        out_specs=[_row_spec(d), _ANY],
        out_shape=[jax.ShapeDtypeStruct((s, d), bf16), ride.out_shape()],
        scratch_shapes=ride.scratch(),
        compiler_params=_cparams(("arbitrary",)),
    )(x, gain, sc, sh, shard)


def _proj_resid_ln_mod_fwd(pairs, x, gate, gain, sc, sh, name):
    s, d = x.shape
    npair = len(pairs)

    def body(*refs):
        aw = refs[:2 * npair]
        x_ref, gt_ref, g_ref, sc_ref, sh_ref, y_ref, x1_ref, h_ref = refs[2 * npair:]
        y = jnp.dot(aw[0][...].astype(bf16), aw[1][...], preferred_element_type=f32)
        for t in range(1, npair):
            y = y + jnp.dot(aw[2 * t][...].astype(bf16), aw[2 * t + 1][...], preferred_element_type=f32)
        y_ref[...] = y
        x1 = x_ref[...] + gt_ref[...] * y
        x1_ref[...] = x1
        rstd = lax.rsqrt(jnp.mean(x1 * x1, axis=-1, keepdims=True) + EPS)
        h = (x1 * rstd) * g_ref[...] * (1.0 + sc_ref[...]) + sh_ref[...]
        h_ref[...] = h.astype(bf16)

    aw_specs, aw = [], []
    for a, w in pairs:
        aw_specs += [_row_spec(a.shape[1]), pl.BlockSpec(w.shape, lambda i: (0, 0))]
        aw += [a, w]
    return pl.pallas_call(
        body, name=name, grid=(s // TOK_TILE,),
        in_specs=aw_specs + [_row_spec(d)] + [_vec_spec(d)] * 4,
        out_specs=[_row_spec(d)] * 3,
        out_shape=[jax.ShapeDtypeStruct((s, d), f32)] * 2 + [jax.ShapeDtypeStruct((s, d), bf16)],
        compiler_params=_cparams(("parallel",)),
    )(*aw, x, gate, gain, sc, sh)


FFN_TN = 1408


def _ffn_up(h2, w_gate, w_up, name):
    s, d = h2.shape
    tm = TOK_TILE

    def body(h_ref, wg_ref, wu_ref, a_ref, g_ref, u_ref):
        h = h_ref[...]
        g = jnp.dot(h, wg_ref[...], preferred_element_type=f32)
        u = jnp.dot(h, wu_ref[...], preferred_element_type=f32)
        a_ref[...] = (g * jax.nn.sigmoid(g) * u).astype(bf16)
        g_ref[...] = g.astype(bf16)
        u_ref[...] = u.astype(bf16)

    w_spec = pl.BlockSpec((d, FFN_TN), lambda j, i: (0, j))
    o_spec = pl.BlockSpec((tm, FFN_TN), lambda j, i: (i, j))
    return pl.pallas_call(
        body, name=name, grid=(D_FF // FFN_TN, s // tm),
        in_specs=[pl.BlockSpec((tm, d), lambda j, i: (i, 0)), w_spec, w_spec],
        out_specs=[o_spec] * 3,
        out_shape=[jax.ShapeDtypeStruct((s, D_FF), bf16)] * 3,
        compiler_params=_cparams(("parallel", "parallel")),
    )(h2, w_gate, w_up)


def _ffn_down_dx(dy2, w_down, gate, up, name):
    s, d = dy2.shape
    tm = TOK_TILE

    def body(dy_ref, w_ref, g_ref, u_ref, dg_ref, du_ref):
        dy = dy_ref[...]
        for c0 in range(0, FFN_TN, SUB_COLS):
            cols = slice(c0, min(c0 + SUB_COLS, FFN_TN))
            da = lax.dot_general(dy, w_ref[cols, :], _NT, preferred_element_type=f32)
            g = g_ref[:, cols].astype(f32)
            sg = jax.nn.sigmoid(g)
            du_ref[:, cols] = (da * g * sg).astype(bf16)
            dg_ref[:, cols] = (da * u_ref[:, cols].astype(f32) * sg * (1.0 + g * (1.0 - sg))).astype(bf16)

    t_spec = pl.BlockSpec((tm, FFN_TN), lambda j, i: (i, j))
    return pl.pallas_call(
        body, name=name, grid=(D_FF // FFN_TN, s // tm),
        in_specs=[pl.BlockSpec((tm, d), lambda j, i: (i, 0)), pl.BlockSpec((FFN_TN, d), lambda j, i: (j, 0)),
                  t_spec, t_spec],
        out_specs=[t_spec, t_spec],
        out_shape=[jax.ShapeDtypeStruct((s, D_FF), bf16)] * 2,
        compiler_params=_cparams(("parallel", "parallel")),
    )(dy2, w_down, gate, up)


def _acc_spec(width):
    return pl.BlockSpec((1, width), lambda i: (0, 0))


def _proj_final_loss_bwd(a, w, x1, gate2, final_g, target, name):
    s, d = x1.shape
    k = a.shape[1]

    def body(a_ref, w_ref, x1_ref, gt_ref, fg_ref, tg_ref, dx2_ref, dy2_ref, loss_ref, dfg_ref, dgt_ref):
        @pl.when(pl.program_id(0) == 0)
        def _():
            loss_ref[...] = jnp.zeros_like(loss_ref)
            dfg_ref[...] = jnp.zeros_like(dfg_ref)
            dgt_ref[...] = jnp.zeros_like(dgt_ref)

        y2 = jnp.dot(a_ref[...], w_ref[...], preferred_element_type=f32)
        gt = gt_ref[...]
        fg = fg_ref[...]
        x2 = x1_ref[...] + gt * y2
        rstd = lax.rsqrt(jnp.mean(x2 * x2, axis=-1, keepdims=True) + EPS)
        xn = x2 * rstd
        err = xn * fg - tg_ref[...]
        row = jnp.sum(err * err, axis=-1, keepdims=True) * (0.5 / d)
        loss_ref[...] += jnp.sum(row, axis=0, keepdims=True) + jnp.zeros_like(loss_ref)
        dout = err * (1.0 / d)
        dfg_ref[...] += jnp.sum(dout * xn, axis=0, keepdims=True)
        dxn = dout * fg
        dx2 = rstd * (dxn - xn * jnp.mean(dxn * xn, axis=-1, keepdims=True))
        dx2_ref[...] = dx2
        dgt_ref[...] += jnp.sum(dx2 * y2, axis=0, keepdims=True)
        dy2_ref[...] = (gt * dx2).astype(bf16)

    return pl.pallas_call(
        body, name=name, grid=(s // TOK_TILE,),
        in_specs=[_row_spec(k), pl.BlockSpec((k, d), lambda i: (0, 0)), _row_spec(d), _vec_spec(d), _vec_spec(d),
                  _row_spec(d)],
        out_specs=[_row_spec(d), _row_spec(d), _acc_spec(128), _acc_spec(d), _acc_spec(d)],
        out_shape=[jax.ShapeDtypeStruct((s, d), f32), jax.ShapeDtypeStruct((s, d), bf16),
                   jax.ShapeDtypeStruct((1, 128), f32), jax.ShapeDtypeStruct((1, d), f32),
                   jax.ShapeDtypeStruct((1, d), f32)],
        compiler_params=_cparams(("arbitrary",)),
    )(a, w, x1, gate2, final_g, target)


def _proj_ln_mod_bwd(pairs, xin, gain, sc, dres, tm, name, xchg, gate=None, y=None):
    s, d = xin.shape
    with_gate = gate is not None
    npair = len(pairs)
    n_in = 2 * npair + (7 if with_gate else 5) - 1
    n_out = 6 if with_gate else 4

    def body(*refs):
        ab = refs[:2 * npair]
        if with_gate:
            (x_ref, g_ref, sc_ref, dr_ref, gt_ref, y_ref,
             dx_ref, dsh_ref, dsc_ref, dg_ref, dy_ref, dgt_ref) = refs[2 * npair:]
        else:
            x_ref, g_ref, sc_ref, dr_ref, dx_ref, dsh_ref, dsc_ref, dg_ref = refs[2 * npair:]

        @pl.when(pl.program_id(0) == 0)
        def _():
            dsh_ref[...] = jnp.zeros_like(dsh_ref)
            dsc_ref[...] = jnp.zeros_like(dsc_ref)
            dg_ref[...] = jnp.zeros_like(dg_ref)
            if with_gate:
                dgt_ref[...] = jnp.zeros_like(dgt_ref)

        dh = lax.dot_general(ab[0][...].astype(bf16), ab[1][...], _NT, preferred_element_type=f32)
        for t in range(1, npair):
            dh = dh + lax.dot_general(ab[2 * t][...].astype(bf16), ab[2 * t + 1][...], _NT,
                                      preferred_element_type=f32)
        xv = x_ref[...]
        g = g_ref[...]
        sc1 = 1.0 + sc_ref[...]
        rstd = lax.rsqrt(jnp.mean(xv * xv, axis=-1, keepdims=True) + EPS)
        xn = xv * rstd
        dsh_ref[...] += jnp.sum(dh, axis=0, keepdims=True)
        dsc_ref[...] += jnp.sum(dh * (xn * g), axis=0, keepdims=True)
        dg_ref[...] += jnp.sum(dh * sc1 * xn, axis=0, keepdims=True)
        dxn = dh * sc1 * g
        dx = dr_ref[...] + rstd * (dxn - xn * jnp.mean(dxn * xn, axis=-1, keepdims=True))
        dx_ref[...] = dx
        if with_gate:
            dgt_ref[...] += jnp.sum(dx * y_ref[...], axis=0, keepdims=True)
            dy_ref[...] = (gt_ref[...] * dx).astype(bf16)

    row = lambda width: pl.BlockSpec((tm, width), lambda i: (i, 0))
    in_specs, args = [], []
    for a, b in pairs:
        in_specs += [row(a.shape[1]), pl.BlockSpec(b.shape, lambda i: (0, 0))]
        args += [a, b]
    in_specs += [row(d), _vec_spec(d), _vec_spec(d), row(d)]
    args += [xin, gain, sc, dres]
    out_specs = [row(d), _acc_spec(d), _acc_spec(d), _acc_spec(d)]
    out_shape = [jax.ShapeDtypeStruct((s, d), f32)] + [jax.ShapeDtypeStruct((1, d), f32)] * 3
    if with_gate:
        in_specs += [_vec_spec(d), row(d)]
        out_specs += [row(d), _acc_spec(d)]
        out_shape += [jax.ShapeDtypeStruct((s, d), bf16), jax.ShapeDtypeStruct((1, d), f32)]
        args += [gate, y]
    grid = (s // tm,)
    out = pl.pallas_call(
        _ride(body, n_in, n_out, xchg, grid), name=name, grid=grid,
        in_specs=in_specs + [_ANY] * xchg.n, out_specs=out_specs + [_ANY] * xchg.n,
        out_shape=out_shape + xchg.out_shape(), scratch_shapes=xchg.scratch(),
        compiler_params=_cparams(("arbitrary",)),
    )(*args, *xchg.arrs)
    return out[:n_out], out[n_out:]


def _bucket_tables():
    import numpy as np
    qi = np.arange(BAND)[:, None]
    kj = np.arange(2 * BAND)[None, :]
    steps = qi + BAND - kj
    max_exact = N_BUCKETS // 2
    out = []
    for d in DILATIONS:
        dist = np.maximum(steps, 0) * d
        dist_f = np.maximum(dist, 1).astype(np.float32)
        large = max_exact + (np.log(dist_f / np.float32(max_exact)) / np.float32(math.log(MAX_DISTANCE / max_exact))
                             * np.float32(N_BUCKETS - max_exact)).astype(np.int32)
        out.append(np.where(dist < max_exact, dist, np.minimum(large, N_BUCKETS - 1)))
    return jnp.asarray(np.stack(out).astype(np.int32))


def _bias_tables(rel_bias, idx):
    def body(idx_ref, rb_ref, o_ref):
        h = pl.program_id(1)
        idxv = idx_ref[0]
        acc = jnp.zeros((BAND, 2 * BAND), f32)
        for b in range(N_BUCKETS):
            acc = jnp.where(idxv == b, rb_ref[b, h], acc)
        o_ref[0, 0] = jnp.where(_attn_masks()[1], acc, NEG_INF)

    return pl.pallas_call(
        body, name="attn_bias_tables", grid=(3, N_HEADS),
        in_specs=[pl.BlockSpec((1, BAND, 2 * BAND), lambda br, h: (br, 0, 0)),
                  pl.BlockSpec(memory_space=pltpu.SMEM)],
        out_specs=pl.BlockSpec((1, 1, BAND, 2 * BAND), lambda br, h: (br, h, 0, 0)),
        out_shape=jax.ShapeDtypeStruct((3, N_HEADS, BAND, 2 * BAND), f32),
        compiler_params=_cparams(("parallel", "parallel")),
    )(idx, rel_bias)


def _bias_grad(dbias, idx):
    def body(idx_ref, db_ref, o_ref):
        br = pl.program_id(1)

        @pl.when(br == 0)
        def _():
            o_ref[...] = jnp.zeros_like(o_ref)

        idxv = idx_ref[0]
        dbv = db_ref[0, 0]
        row = lax.broadcasted_iota(jnp.int32, (N_BUCKETS, 128), 0)
        acc = jnp.zeros((N_BUCKETS, 128), f32)
        for b in range(N_BUCKETS):
            sb = jnp.sum(jnp.sum(jnp.where(idxv == b, dbv, 0.0), axis=1, keepdims=True), axis=0, keepdims=True)
            acc = acc + jnp.where(row == b, sb, 0.0)
        o_ref[0] += acc

    return pl.pallas_call(
        body, name="attn_bias_grad", grid=(N_HEADS, 3),
        in_specs=[pl.BlockSpec((1, BAND, 2 * BAND), lambda h, br: (br, 0, 0)),
                  pl.BlockSpec((1, 1, BAND, 2 * BAND), lambda h, br: (br, h, 0, 0))],
        out_specs=pl.BlockSpec((1, N_BUCKETS, 128), lambda h, br: (h, 0, 0)),
        out_shape=jax.ShapeDtypeStruct((N_HEADS, N_BUCKETS, 128), f32),
        compiler_params=_cparams(("parallel", "arbitrary")),
    )(idx, dbias)


def _attn_masks():
    lane = lax.broadcasted_iota(jnp.int32, (BAND, 128), 1)
    m0 = lane < HEAD_DIM
    qi = lax.broadcasted_iota(jnp.int32, (BAND, 2 * BAND), 0)
    kj = lax.broadcasted_iota(jnp.int32, (BAND, 2 * BAND), 1)
    steps = qi + BAND - kj
    in_window = (steps >= 0) & (steps <= BAND)
    return m0, in_window, kj >= BAND


_NT = (((1,), (1,)), ((), ()))
_TN = (((0,), (0,)), ((), ()))
_BNN = (((2,), (1,)), ((0,), (0,)))
_BNT = (((2,), (2,)), ((0,), (0,)))
_BTN = (((1,), (1,)), ((0,), (0,)))
ATTN_GROUP = 4
ATTN_ITEMS = PAD_UNIT // BAND
Q_COL, K_COL, V_COL = 0, 4, 8


def _attn_item_rows(j, d, c, cbase):
    r = lax.rem(j, d)
    b = lax.div(j, d)
    loc = b * (d * BAND) + r
    first = jnp.logical_and(c == 0, b == 0)
    start = cbase + loc
    pstart = jnp.where(first, start, start - d * BAND)
    return loc, start, pstart, first


def _attn_fwd(proj, bias, xchg):
    s = proj.shape[0]

    def body(q_ref, k_ref, v_ref, b_ref, y_ref, lse_ref, o_s, l_s):
        c = pl.program_id(1)
        cbase = pl.multiple_of(c * PAD_UNIT, PAD_UNIT)
        m0, in_window, cur_half = _attn_masks()
        for bi, d in enumerate(DILATIONS):
            def group(jg, carry, bi=bi, d=d):
                locs, qs, ks, vs, pens = [], [], [], [], []
                for t in range(ATTN_GROUP):
                    loc, start, pstart, first = _attn_item_rows(jg * ATTN_GROUP + t, d, c, cbase)
                    locs.append(loc)
                    qs.append(q_ref[pl.ds(loc, BAND, stride=d), :])
                    ks.append(jnp.concatenate([k_ref[pl.ds(pstart, BAND, stride=d), :],
                                               k_ref[pl.ds(start, BAND, stride=d), :]], axis=0))
                    vs.append(jnp.concatenate([v_ref[pl.ds(pstart, BAND, stride=d), :],
                                               v_ref[pl.ds(start, BAND, stride=d), :]], axis=0))
                    pens.append(jnp.where(cur_half, 0.0, jnp.where(first, NEG_INF, 0.0)))
                q = jnp.stack(qs)
                kk = jnp.stack(ks + ks).astype(bf16)
                vv = jnp.stack(vs + vs).astype(bf16)
                pen = jnp.stack(pens + pens)
                qh = (jnp.concatenate([jnp.where(m0, q, 0.0), jnp.where(m0, 0.0, q)], axis=0) * 0.125).astype(bf16)
                sc = lax.dot_general(qh, kk, _BNT, preferred_element_type=f32)
                sc = (sc.reshape(2, ATTN_GROUP, BAND, 2 * BAND) + b_ref[bi][:, None]).reshape(sc.shape) + pen
                mx = jnp.max(sc, axis=-1, keepdims=True)
                e = jnp.exp(sc - mx)
                l = jnp.sum(e, axis=-1, keepdims=True)
                o = lax.dot_general(e.astype(bf16), vv, _BNN, preferred_element_type=f32) * (1.0 / l)
                ls = mx + jnp.log(l)
                for t in range(ATTN_GROUP):
                    rows = pl.ds(locs[t], BAND, stride=d)
                    o_s[bi, rows, :] = jnp.where(m0, o[t], o[ATTN_GROUP + t])
                    l_s[bi, rows, :] = jnp.where(m0, ls[t], ls[ATTN_GROUP + t])
                return carry

            lax.fori_loop(0, ATTN_ITEMS // ATTN_GROUP, group, 0)

        def merge(t, carry):
            rows = pl.ds(pl.multiple_of(t * 256, 256), 256)
            ls = [l_s[i, rows, :] for i in range(3)]
            mx = jnp.maximum(jnp.maximum(ls[0], ls[1]), ls[2])
            ws = [jnp.exp(l - mx) for l in ls]
            tot = ws[0] + ws[1] + ws[2]
            y = (ws[0] * o_s[0, rows, :] + ws[1] * o_s[1, rows, :] + ws[2] * o_s[2, rows, :]) / tot
            y_ref[rows, :] = y
            lse_ref[rows, :] = mx + jnp.log(tot)
            return carry

        lax.fori_loop(0, PAD_UNIT // 256, merge, 0)

    chunk = lambda col: pl.BlockSpec((PAD_UNIT, 128), lambda p, c: (c, col + p))
    full = lambda col: pl.BlockSpec((s, 128), lambda p, c: (0, col + p))
    grid = (N_HEADS // 2, s // PAD_UNIT)
    out = pl.pallas_call(
        _ride(body, 4, 2, xchg, grid), name="attn_fwd", grid=grid,
        in_specs=[chunk(Q_COL), full(K_COL), full(V_COL),
                  pl.BlockSpec((3, 2, BAND, 2 * BAND), lambda p, c: (0, p, 0, 0))] + [_ANY] * xchg.n,
        out_specs=[chunk(0), chunk(0)] + [_ANY] * xchg.n,
        out_shape=[jax.ShapeDtypeStruct((s, GROUP_W), f32)] * 2 + xchg.out_shape(),
        scratch_shapes=[pltpu.VMEM((3, PAD_UNIT, 128), f32)] * 2 + xchg.scratch(),
        compiler_params=_cparams(("arbitrary", "arbitrary")),
    )(proj, proj, proj, bias, *xchg.arrs)
    return out[:2], out[2:]


def _attn_bwd(proj, bias, y, lse, dycat):
    s = proj.shape[0]

    def body(q_ref, k_ref, v_ref, b_ref, y_ref, lse_ref, dy_ref, dq_ref, dk_ref, dv_ref, db_ref, dd_s):
        c = pl.program_id(1)
        cbase = pl.multiple_of(c * PAD_UNIT, PAD_UNIT)
        m0, in_window, cur_half = _attn_masks()

        @pl.when(c == 0)
        def _():
            dk_ref[...] = jnp.zeros_like(dk_ref)
            dv_ref[...] = jnp.zeros_like(dv_ref)
            db_ref[...] = jnp.zeros_like(db_ref)

        dq_ref[...] = jnp.zeros_like(dq_ref)

        def rowdot(t, carry):
            rows = pl.ds(pl.multiple_of(t * 256, 256), 256)
            prod = dy_ref[rows, :] * y_ref[rows, :]
            lane = lax.broadcasted_iota(jnp.int32, prod.shape, 1)
            h0 = lane < HEAD_DIM
            d0 = jnp.sum(jnp.where(h0, prod, 0.0), axis=-1, keepdims=True)
            d1 = jnp.sum(jnp.where(h0, 0.0, prod), axis=-1, keepdims=True)
            dd_s[rows, :] = jnp.where(h0, d0, d1)
            return carry

        lax.fori_loop(0, PAD_UNIT // 256, rowdot, 0)

        for bi, d in enumerate(DILATIONS):
            def group(jg, carry, bi=bi, d=d):
                ng = ATTN_GROUP
                meta, qs, dos, lqs, dds, ks, vs, pens = [], [], [], [], [], [], [], []
                for t in range(ng):
                    loc, start, pstart, first = _attn_item_rows(jg * ng + t, d, c, cbase)
                    qrows = pl.ds(loc, BAND, stride=d)
                    rows = pl.ds(start, BAND, stride=d)
                    prows = pl.ds(pstart, BAND, stride=d)
                    meta.append((qrows, rows, prows))
                    qs.append(q_ref[qrows, :])
                    dos.append(dy_ref[qrows, :])
                    lqs.append(lse_ref[qrows, :])
                    dds.append(dd_s[qrows, :])
                    ks.append(jnp.concatenate([k_ref[prows, :], k_ref[rows, :]], axis=0))
                    vs.append(jnp.concatenate([v_ref[prows, :], v_ref[rows, :]], axis=0))
                    pens.append(jnp.where(cur_half, 0.0, jnp.where(first, NEG_INF, 0.0)))

                def heads(t):
                    return jnp.concatenate([jnp.where(m0, t, 0.0), jnp.where(m0, 0.0, t)], axis=0)

                def head_col(t):
                    return jnp.concatenate([t[:, :, 0:1], t[:, :, HEAD_DIM:HEAD_DIM + 1]], axis=0)

                qh = (heads(jnp.stack(qs)) * 0.125).astype(bf16)
                doh = heads(jnp.stack(dos)).astype(bf16)
                kk = jnp.stack(ks + ks).astype(bf16)
                vv = jnp.stack(vs + vs).astype(bf16)
                sc = lax.dot_general(qh, kk, _BNT, preferred_element_type=f32)
                sc = (sc.reshape(2, ng, BAND, 2 * BAND) + b_ref[bi][:, None]).reshape(sc.shape) + jnp.stack(pens + pens)
                p = jnp.exp(sc - head_col(jnp.stack(lqs)))
                dp = lax.dot_general(doh, vv, _BNT, preferred_element_type=f32)
                ds = p * (dp - head_col(jnp.stack(dds)))
                db_ref[bi] += jnp.sum(ds.reshape(2, ng, BAND, 2 * BAND), axis=1)
                dsb = ds.astype(bf16)
                dq = lax.dot_general(dsb, kk, _BNN, preferred_element_type=f32) * 0.125
                dk = lax.dot_general(dsb, qh, _BTN, preferred_element_type=f32)
                dv = lax.dot_general(p.astype(bf16), doh, _BTN, preferred_element_type=f32)
                for t in range(ng):
                    qrows, rows, prows = meta[t]
                    dq_ref[qrows, :] += jnp.where(m0, dq[t], dq[ng + t])
                    dkt = dk[t] + dk[ng + t]
                    dvt = dv[t] + dv[ng + t]
                    dk_ref[prows, :] += dkt[:BAND]
                    dk_ref[rows, :] += dkt[BAND:]
                    dv_ref[prows, :] += dvt[:BAND]
                    dv_ref[rows, :] += dvt[BAND:]
                return carry

            lax.fori_loop(0, ATTN_ITEMS // ATTN_GROUP, group, 0)

    chunk = lambda col: pl.BlockSpec((PAD_UNIT, 128), lambda p, c: (c, col + p))
    full = lambda col: pl.BlockSpec((s, 128), lambda p, c: (0, col + p))
    bias_spec = pl.BlockSpec((3, 2, BAND, 2 * BAND), lambda p, c: (0, p, 0, 0))
    return pl.pallas_call(
        body, name="attn_bwd", grid=(N_HEADS // 2, s // PAD_UNIT),
        in_specs=[chunk(Q_COL), full(K_COL), full(V_COL), bias_spec, chunk(0), chunk(0), chunk(0)],
        out_specs=[chunk(0), full(0), full(0), bias_spec],
        out_shape=[jax.ShapeDtypeStruct((s, GROUP_W), f32)] * 3
        + [jax.ShapeDtypeStruct((3, N_HEADS, BAND, 2 * BAND), f32)],
        scratch_shapes=[pltpu.VMEM((PAD_UNIT, 128), f32)],
        compiler_params=_cparams(("parallel", "arbitrary")),
    )(proj, proj, proj, bias, y, lse, dycat)


_HI = lax.Precision.HIGHEST
DELTA_COL = 1536
Z_COL = 3072
BA_BLOCK = 28
DELTA_ROWS = 512


def _hdot(a, b):
    return jnp.dot(a, b, precision=_HI, preferred_element_type=f32)


_DIMS = dict(nn=(((2,), (1,)), ((0,), (0,))), nt=(((2,), (2,)), ((0,), (0,))), tn=(((1,), (1,)), ((0,), (0,))))


@functools.partial(jax.custom_vjp, nondiff_argnums=(2,))
def _mmx(a, b, mode):
    return lax.dot_general(a.astype(bf16), b.astype(bf16), _DIMS[mode], preferred_element_type=f32)


def _mmx_fwd(a, b, mode):
    return _mmx(a, b, mode), (a, b)


def _mmx_bwd(mode, res, g):
    a, b = res
    if mode == "nn":
        return _mmx(g, b, "nt"), _mmx(a, g, "tn")
    if mode == "nt":
        return _mmx(g, b, "nn"), _mmx(g, a, "tn")
    return _mmx(b, g, "nt"), _mmx(a, g, "nn")


_mmx.defvjp(_mmx_fwd, _mmx_bwd)


def _pair_iota():
    row = lax.broadcasted_iota(jnp.int32, (CHUNK, 128), 0)
    lane = lax.broadcasted_iota(jnp.int32, (CHUNK, 128), 1)
    return row, lane, lane & (CHUNK - 1)


def _bd(x):
    _, lane, _ = _pair_iota()
    m0 = lane < CHUNK
    return jnp.concatenate([jnp.where(m0, x, 0.0), jnp.where(m0, 0.0, x)], axis=1)


def _pmm(a, b):
    return _mmx(a, _bd(b), "nn")


def _ntp(x, y):
    return _mmx(x, _bd(y), "nt")


def _tnp(x, y):
    full = _mmx(x, y, "tn")
    _, lane, _ = _pair_iota()
    return jnp.where(lane < CHUNK, full[:, :CHUNK], full[:, CHUNK:])


def _tri_inv(a):
    row, lane, jj = _pair_iota()
    eye = jnp.where(row == jj, 1.0, 0.0).astype(f32)

    def same_block(log2b):
        return (row >> log2b) == (jj >> log2b)

    dgl = jnp.where(same_block(3), a, 0.0)
    d2 = _pmm(dgl, dgl)
    d4 = _pmm(d2, d2)
    t = _pmm(_pmm(eye - dgl, eye + d2), eye + d4)
    for lb in (3, 4, 5):
        off = jnp.where(same_block(lb + 1) & jnp.logical_not(same_block(lb)), a, 0.0)
        t = t - _pmm(_pmm(t, off), t)
    return t


@jax.custom_vjp
def _solve2(a, xv, xk, t):
    return _pmm(t, xv), _pmm(t, xk)


def _solve2_fwd(a, xv, xk, t):
    u, w = _pmm(t, xv), _pmm(t, xk)
    return (u, w), (t, u, w)


def _solve2_bwd(res, cts):
    t, u, w = res
    du, dw = cts
    dxv = _tnp(t, du)
    dxk = _tnp(t, dw)
    return -(_ntp(dxv, u) + _ntp(dxk, w)), dxv, dxk, jnp.zeros_like(t)


_solve2.defvjp(_solve2_fwd, _solve2_bwd)


def _chunk_pre(qp, kp, vp, bp, gcum, t=None):
    row, lane, jj = _pair_iota()
    causal = row >= jj
    strict = row > jj
    rsel = jnp.sum(jnp.where(row == jj, gcum, 0.0), axis=1, keepdims=True)
    decay = jnp.where(causal, jnp.exp(jnp.where(causal, gcum - rsel, 0.0)), 0.0)
    kb = kp * bp
    kd = _bd(kp)
    a = jnp.where(strict, _mmx(kb, kd, "nt") * decay, 0.0)
    eg = jnp.exp(gcum)
    if t is None:
        t = _tri_inv(a)
    u, w = _solve2(a, vp * bp, kb * eg, t)
    qk = jnp.where(causal, _mmx(qp, kd, "nt") * decay, 0.0)
    glast = jnp.sum(jnp.where(row == CHUNK - 1, gcum, 0.0), axis=1, keepdims=True)
    return u, w, qp * eg, kp * jnp.exp(glast - gcum), qk, jnp.exp(glast), t


def _chunk_post(u, w, qt, kh, qk, gam, sp):
    sd = _bd(sp)
    vnew = u - _mmx(w, sd, "nn")
    o = _mmx(qt, sd, "nn") + _pmm(qk, vnew)
    return o, gam * sp + _tnp(kh, vnew)


def _pair_spec(rows=DELTA_ROWS):
    return pl.BlockSpec((rows, 128), lambda i, p: (i, p))


DELTA_NB = DELTA_ROWS // CHUNK


def _chunks(ref):
    return ref[...].reshape(DELTA_NB, CHUNK, 128)


def _pairs(ref, rows):
    return jnp.stack([ref[rows, p * 128:(p + 1) * 128] for p in range(4)], axis=0)


def _delta_chunk_pre(qn, kn, sv, beta, g):
    s = qn.shape[0]

    def body(q_ref, k_ref, v_ref, b_ref, g_ref, u_ref, w_ref, qt_ref, kh_ref, qk_ref, t_ref, gm_ref):
        outs = _chunk_pre(_chunks(q_ref), _chunks(k_ref), _chunks(v_ref), _chunks(b_ref), _chunks(g_ref))
        for ref, val in zip((u_ref, w_ref, qt_ref, kh_ref, qk_ref, t_ref), outs[:5] + outs[6:]):
            ref[...] = val.reshape(DELTA_ROWS, 128).astype(ref.dtype)
        gm_ref[...] = jnp.broadcast_to(outs[5], (DELTA_NB, 8, 128)).reshape(DELTA_NB * 8, 128)

    v_spec = pl.BlockSpec((DELTA_ROWS, 128), lambda i, p: (i, 8 + p))
    return pl.pallas_call(
        body, name="delta_chunk_pre", grid=(s // DELTA_ROWS, 4),
        in_specs=[_pair_spec(), _pair_spec(), v_spec, _pair_spec(), _pair_spec()],
        out_specs=[_pair_spec()] * 6 + [_pair_spec(DELTA_NB * 8)],
        out_shape=[jax.ShapeDtypeStruct((s, GROUP_W), f32)] + [jax.ShapeDtypeStruct((s, GROUP_W), bf16)] * 5
        + [jax.ShapeDtypeStruct((s // 8, GROUP_W), f32)],
        compiler_params=_cparams(("parallel", "parallel")),
    )(qn, kn, sv, beta, g)


def _delta_scan_fwd(u, w, qt, kh, qk, gm):
    s = u.shape[0]

    def body(u_ref, w_ref, qt_ref, kh_ref, qk_ref, gm_ref, o_ref, ss_ref, st):
        @pl.when(pl.program_id(0) == 0)
        def _():
            st[...] = jnp.zeros_like(st)

        def chunk(ci, carry):
            rows = pl.ds(pl.multiple_of(ci * CHUNK, CHUNK), CHUNK)
            grow = pl.ds(pl.multiple_of(ci * 8, 8), 1)
            sp = st[...]
            o, s2 = _chunk_post(_pairs(u_ref, rows), _pairs(w_ref, rows), _pairs(qt_ref, rows),
                                _pairs(kh_ref, rows), _pairs(qk_ref, rows), _pairs(gm_ref, grow), sp)
            for p in range(4):
                ss_ref[rows, p * 128:(p + 1) * 128] = sp[p]
                o_ref[rows, p * 128:(p + 1) * 128] = o[p]
            st[...] = s2
            return carry

        lax.fori_loop(0, DELTA_NB, chunk, 0)

    spec = pl.BlockSpec((DELTA_ROWS, GROUP_W), lambda i: (i, 0))
    gspec = pl.BlockSpec((DELTA_NB * 8, GROUP_W), lambda i: (i, 0))
    return pl.pallas_call(
        body, name="delta_scan_fwd", grid=(s // DELTA_ROWS,),
        in_specs=[spec] * 5 + [gspec],
        out_specs=[spec, spec],
        out_shape=[jax.ShapeDtypeStruct((s, GROUP_W), f32)] * 2,
        scratch_shapes=[pltpu.VMEM((4, CHUNK, 128), f32)],
        compiler_params=_cparams(("arbitrary",)),
    )(u, w, qt, kh, qk, gm)


def _delta_scan_bwd(w, qt, kh, qk, gm, do):
    s = w.shape[0]
    nb = s // DELTA_ROWS

    def body(w_ref, qt_ref, kh_ref, qk_ref, gm_ref, do_ref, dso_ref, dst):
        @pl.when(pl.program_id(0) == 0)
        def _():
            dst[...] = jnp.zeros_like(dst)

        def chunk(t, carry):
            ci = DELTA_NB - 1 - t
            rows = pl.ds(pl.multiple_of(ci * CHUNK, CHUNK), CHUNK)
            grow = pl.ds(pl.multiple_of(ci * 8, 8), 1)
            ds = dst[...]
            for p in range(4):
                dso_ref[rows, p * 128:(p + 1) * 128] = ds[p]
            do = _pairs(do_ref, rows)
            dvn = _tnp(_pairs(qk_ref, rows), do) + _pmm(_pairs(kh_ref, rows), ds)
            dst[...] = _tnp(_pairs(qt_ref, rows), do) + _pairs(gm_ref, grow) * ds - _tnp(_pairs(w_ref, rows), dvn)
            return carry

        lax.fori_loop(0, DELTA_NB, chunk, 0)

    spec = pl.BlockSpec((DELTA_ROWS, GROUP_W), lambda i: (nb - 1 - i, 0))
    gspec = pl.BlockSpec((DELTA_NB * 8, GROUP_W), lambda i: (nb - 1 - i, 0))
    return pl.pallas_call(
        body, name="delta_scan_bwd", grid=(nb,),
        in_specs=[spec] * 4 + [gspec, spec],
        out_specs=spec,
        out_shape=jax.ShapeDtypeStruct((s, GROUP_W), f32),
        scratch_shapes=[pltpu.VMEM((4, CHUNK, 128), f32)],
        compiler_params=_cparams(("arbitrary",)),
    )(w, qt, kh, qk, gm, do)


def _delta_chunk_bwd(qn, kn, sv, beta, g, tinv, ss, dso, do):
    s = qn.shape[0]

    def body(q_ref, k_ref, v_ref, b_ref, g_ref, t_ref, ss_ref, dso_ref, do_ref,
             dq_ref, dk_ref, dv_ref, db_ref, dg_ref):
        sp = _chunks(ss_ref)
        t = _chunks(t_ref)

        def fn(q, k, v, b, gg):
            return _chunk_post(*_chunk_pre(q, k, v, b, gg, t)[:6], sp)

        _, vjp = jax.vjp(fn, _chunks(q_ref), _chunks(k_ref), _chunks(v_ref), _chunks(b_ref), _chunks(g_ref))
        grads = vjp((_chunks(do_ref), _chunks(dso_ref)))
        for ref, val in zip((dq_ref, dk_ref, dv_ref, db_ref, dg_ref), grads):
            ref[...] = val.reshape(DELTA_ROWS, 128)

    v_spec = pl.BlockSpec((DELTA_ROWS, 128), lambda i, p: (i, 8 + p))
    return pl.pallas_call(
        body, name="delta_chunk_bwd", grid=(s // DELTA_ROWS, 4),
        in_specs=[_pair_spec(), _pair_spec(), v_spec] + [_pair_spec()] * 6,
        out_specs=[_pair_spec()] * 5,
        out_shape=[jax.ShapeDtypeStruct((s, GROUP_W), f32)] * 5,
        compiler_params=_cparams(("parallel", "parallel")),
    )(qn, kn, sv, beta, g, tinv, ss, dso, do)


def _head_sums(x):
    r = lax.broadcasted_iota(jnp.int32, (128, 128), 0)
    c = lax.broadcasted_iota(jnp.int32, (128, 128), 1)
    pair = jnp.where((r >> 6) == (c >> 6), 1.0, 0.0).astype(f32)
    npair = x.shape[1] // 128
    xb = jnp.concatenate([x[None, :, p * 128:(p + 1) * 128] for p in range(npair)], axis=0)
    sums = _mmx(xb, jnp.broadcast_to(pair, (npair, 128, 128)), "nn")
    return jnp.concatenate([sums[p] for p in range(npair)], axis=1)


def _sel_dot(a, b):
    return jnp.dot(a, b, precision=lax.Precision.HIGH, preferred_element_type=f32)


def _expand_matrix(first):
    r = lax.broadcasted_iota(jnp.int32, (128, GROUP_W), 0)
    c = lax.broadcasted_iota(jnp.int32, (128, GROUP_W), 1) >> 6
    return jnp.where(r == c + first, 1.0, 0.0).astype(f32)


@functools.partial(jax.custom_vjp, nondiff_argnums=(1,))
def _expand_heads(ba, first):
    return _sel_dot(ba, _expand_matrix(first))


def _expand_heads_fwd(ba, first):
    return _expand_heads(ba, first), None


def _expand_heads_bwd(first, _, g):
    return (_mmx(g[None], _expand_matrix(first)[None], "nt")[0],)


_expand_heads.defvjp(_expand_heads_fwd, _expand_heads_bwd)


def _softplus(x):
    return jnp.maximum(x, 0.0) + jnp.log(1.0 + jnp.exp(-jnp.abs(x)))


def _prep_fn(sq, sk, ba, alog_e, dt_e):
    qn = sq * lax.rsqrt(_head_sums(sq * sq) + EPS) * (HEAD_DIM ** -0.5)
    kn = sk * lax.rsqrt(_head_sums(sk * sk) + EPS)
    bl = _expand_heads(ba, 0)
    al = _expand_heads(ba, N_HEADS)
    beta = jax.nn.sigmoid(bl)
    g = -jnp.exp(alog_e) * _softplus(al + dt_e)
    nchunk = g.shape[0] // CHUNK
    ri = lax.broadcasted_iota(jnp.int32, (nchunk, CHUNK, CHUNK), 1)
    ci = lax.broadcasted_iota(jnp.int32, (nchunk, CHUNK, CHUNK), 2)
    tril = jnp.where(ri >= ci, 1.0, 0.0).astype(f32)
    gcum = lax.dot_general(tril, g.reshape(nchunk, CHUNK, g.shape[1]), _BNN, precision=lax.Precision.HIGH,
                           preferred_element_type=f32)
    return qn, kn, beta, gcum.reshape(g.shape)


def _gnorm_fn(o, z, ng_e):
    ms = _head_sums(o * o) * (1.0 / HEAD_DIM)
    return o * lax.rsqrt(ms + EPS) * ng_e * (z * jax.nn.sigmoid(z))


def _tok_spec(width, col):
    return pl.BlockSpec((TOK_TILE, width), lambda i: (i, col))


def _conv_taps(xs_ref, w_ref, base, n, cols):
    acc = w_ref[CONV_WIDTH - 1:CONV_WIDTH, cols] * xs_ref[pl.ds(base, n), cols]
    for j in range(CONV_WIDTH - 1):
        acc = acc + w_ref[j:j + 1, cols] * xs_ref[pl.ds(base - (CONV_WIDTH - 1) + j, n), cols]
    return acc


def _conv_silu_fwd(proj, conv_w):
    s = proj.shape[0]
    wd = 3 * GROUP_W
    hb = TOK_TILE // 8

    def body(x_ref, halo_ref, w_ref, o_ref, xs):
        inner = pl.program_id(0) > 0

        def lane_block(cb, carry):
            cols = pl.ds(pl.multiple_of(cb * 128, 128), 128)
            xs[0:8, cols] = jnp.where(inner, halo_ref[:, cols], 0.0)
            xs[8:, cols] = x_ref[:, cols]
            y = _conv_taps(xs, w_ref, 8, TOK_TILE, cols)
            o_ref[:, cols] = y * jax.nn.sigmoid(y)
            return carry

        lax.fori_loop(0, wd // 128, lane_block, 0)

    return pl.pallas_call(
        body, name="delta_conv_fwd", grid=(s // TOK_TILE,),
        in_specs=[_tok_spec(wd, 1), pl.BlockSpec((8, wd), lambda i: (jnp.maximum(i * hb - 1, 0), 1)),
                  pl.BlockSpec((CONV_WIDTH, wd), lambda i: (0, 0))],
        out_specs=_tok_spec(wd, 0),
        out_shape=jax.ShapeDtypeStruct((s, wd), f32),
        scratch_shapes=[pltpu.VMEM((TOK_TILE + 8, wd), f32)],
        compiler_params=_cparams(("parallel",)),
    )(proj, proj, conv_w)


def _conv_silu_bwd(proj, conv_w, ds3, xchg):
    s = proj.shape[0]
    wd = 3 * GROUP_W
    hb = TOK_TILE // 8
    nt = s // TOK_TILE

    def body(x_ref, hp_ref, hn_ref, dq_ref, dk_ref, dv_ref, dqn_ref, dkn_ref, dvn_ref, w_ref, dx_ref, dw_ref, xs, dys):
        i = pl.program_id(0)

        @pl.when(i == 0)
        def _():
            dw_ref[...] = jnp.zeros_like(dw_ref)

        last = i == nt - 1
        def lane_block(lb, carry, third, cur, nxt):
            tcols = pl.ds(pl.multiple_of(lb * 128, 128), 128)
            cols = pl.ds(pl.multiple_of(third * GROUP_W + lb * 128, 128), 128)
            xs[0:8, cols] = jnp.where(i > 0, hp_ref[:, cols], 0.0)
            xs[8:8 + TOK_TILE, cols] = x_ref[:, cols]
            xs[8 + TOK_TILE:, cols] = jnp.where(last, 0.0, hn_ref[:, cols])
            y = _conv_taps(xs, w_ref, 8, TOK_TILE, cols)
            sg = jax.nn.sigmoid(y)
            dy0 = cur[:, tcols] * (sg * (1.0 + y * (1.0 - sg)))
            dys[0:TOK_TILE, cols] = dy0
            yn = _conv_taps(xs, w_ref, 8 + TOK_TILE, 8, cols)
            sgn = jax.nn.sigmoid(yn)
            dys[TOK_TILE:, cols] = jnp.where(last, 0.0, nxt[:, tcols]) * (sgn * (1.0 + yn * (1.0 - sgn)))
            dx = w_ref[CONV_WIDTH - 1:CONV_WIDTH, cols] * dy0
            for j in range(CONV_WIDTH - 1):
                dx = dx + w_ref[j:j + 1, cols] * dys[pl.ds(CONV_WIDTH - 1 - j, TOK_TILE), cols]
            dx_ref[:, cols] = dx.astype(dx_ref.dtype)
            for j in range(CONV_WIDTH):
                dw_ref[j:j + 1, cols] += jnp.sum(dy0 * xs[pl.ds(8 - (CONV_WIDTH - 1) + j, TOK_TILE), cols],
                                                 axis=0, keepdims=True)
            return carry

        for third, (cur, nxt) in enumerate(((dq_ref, dqn_ref), (dk_ref, dkn_ref), (dv_ref, dvn_ref))):
            lax.fori_loop(0, GROUP_W // 128, functools.partial(lane_block, third=third, cur=cur, nxt=nxt), 0)

    prev8 = lambda col: pl.BlockSpec((8, wd), lambda i: (jnp.maximum(i * hb - 1, 0), col))
    next8 = lambda col: pl.BlockSpec((8, wd), lambda i: (jnp.minimum((i + 1) * hb, s // 8 - 1), col))
    next8_third = pl.BlockSpec((8, GROUP_W), lambda i: (jnp.minimum((i + 1) * hb, s // 8 - 1), 0))
    out = pl.pallas_call(
        _ride(body, 10, 2, xchg, (nt,)), name="delta_conv_bwd", grid=(nt,),
        in_specs=[_tok_spec(wd, 1), prev8(1), next8(1)] + [_tok_spec(GROUP_W, 0)] * 3 + [next8_third] * 3
        + [pl.BlockSpec((CONV_WIDTH, wd), lambda i: (0, 0))] + [_ANY] * xchg.n,
        out_specs=[_tok_spec(wd, 0), pl.BlockSpec((CONV_WIDTH, wd), lambda i: (0, 0))] + [_ANY] * xchg.n,
        out_shape=[jax.ShapeDtypeStruct((s, wd), bf16), jax.ShapeDtypeStruct((CONV_WIDTH, wd), f32)] + xchg.out_shape(),
        scratch_shapes=[pltpu.VMEM((TOK_TILE + 16, wd), f32), pltpu.VMEM((TOK_TILE + 8, wd), f32)] + xchg.scratch(),
        compiler_params=_cparams(("arbitrary",)),
    )(proj, proj, proj, *ds3, *ds3, conv_w, *xchg.arrs)
    return out[:2], out[2:]


def _delta_prep_fwd(sconv, proj, alog_e, dt_e):
    s = sconv.shape[0]

    def body(sq_ref, sk_ref, ba_ref, al_ref, dt_ref, q_ref, k_ref, b_ref, g_ref):
        qn, kn, beta, g = _prep_fn(sq_ref[...], sk_ref[...], ba_ref[...], al_ref[...], dt_ref[...])
        q_ref[...] = qn
        k_ref[...] = kn
        b_ref[...] = beta
        g_ref[...] = g

    return pl.pallas_call(
        body, name="delta_prep_fwd", grid=(s // TOK_TILE,),
        in_specs=[_tok_spec(GROUP_W, 0), _tok_spec(GROUP_W, 1), _tok_spec(128, BA_BLOCK),
                  _vec_spec(GROUP_W), _vec_spec(GROUP_W)],
        out_specs=[_tok_spec(GROUP_W, 0)] * 4,
        out_shape=[jax.ShapeDtypeStruct((s, GROUP_W), f32)] * 4,
        compiler_params=_cparams(("parallel",)),
    )(sconv, sconv, proj, alog_e, dt_e)


def _delta_prep_bwd(sconv, proj, alog_e, dt_e, dqn, dkn, dbeta, dg):
    s = sconv.shape[0]

    def body(sq_ref, sk_ref, ba_ref, al_ref, dt_ref, dq_ref, dk_ref, db_ref, dg_ref,
             dsq_ref, dsk_ref, dba_ref, dal_ref, ddt_ref):
        @pl.when(pl.program_id(0) == 0)
        def _():
            dal_ref[...] = jnp.zeros_like(dal_ref)
            ddt_ref[...] = jnp.zeros_like(ddt_ref)

        _, vjp = jax.vjp(_prep_fn, sq_ref[...], sk_ref[...], ba_ref[...], al_ref[...], dt_ref[...])
        dsq, dsk, dba, dal, ddt = vjp((dq_ref[...], dk_ref[...], db_ref[...], dg_ref[...]))
        dsq_ref[...] = dsq
        dsk_ref[...] = dsk
        dba_ref[...] = dba.astype(bf16)
        dal_ref[...] += dal
        ddt_ref[...] += ddt

    return pl.pallas_call(
        body, name="delta_prep_bwd", grid=(s // TOK_TILE,),
        in_specs=[_tok_spec(GROUP_W, 0), _tok_spec(GROUP_W, 1), _tok_spec(128, BA_BLOCK),
                  _vec_spec(GROUP_W), _vec_spec(GROUP_W)] + [_tok_spec(GROUP_W, 0)] * 4,
        out_specs=[_tok_spec(GROUP_W, 0), _tok_spec(GROUP_W, 0), _tok_spec(128, 0),
                   _acc_spec(GROUP_W), _acc_spec(GROUP_W)],
        out_shape=[jax.ShapeDtypeStruct((s, GROUP_W), f32)] * 2 + [jax.ShapeDtypeStruct((s, 128), bf16)]
        + [jax.ShapeDtypeStruct((1, GROUP_W), f32)] * 2,
        compiler_params=_cparams(("arbitrary",)),
    )(sconv, sconv, proj, alog_e, dt_e, dqn, dkn, dbeta, dg)


def _gnorm_fwd(o, proj, ng_e):
    s = o.shape[0]

    def body(o_ref, z_ref, g_ref, y_ref):
        y_ref[...] = _gnorm_fn(o_ref[...], z_ref[...], g_ref[...])

    return pl.pallas_call(
        body, name="delta_gnorm_fwd", grid=(s // TOK_TILE,),
        in_specs=[_tok_spec(GROUP_W, 0), _tok_spec(GROUP_W, Z_COL // GROUP_W), _vec_spec(GROUP_W)],
        out_specs=_tok_spec(GROUP_W, 0),
        out_shape=jax.ShapeDtypeStruct((s, GROUP_W), f32),
        compiler_params=_cparams(("parallel",)),
    )(o, proj, ng_e)


def _gnorm_bwd(o, proj, ng_e, dycat):
    s = o.shape[0]

    def body(o_ref, z_ref, g_ref, dy_ref, do_ref, dz_ref, dg_ref):
        @pl.when(pl.program_id(0) == 0)
        def _():
            dg_ref[...] = jnp.zeros_like(dg_ref)

        _, vjp = jax.vjp(_gnorm_fn, o_ref[...], z_ref[...], g_ref[...])
        do, dz, dg = vjp(dy_ref[...])
        do_ref[...] = do
        dz_ref[...] = dz.astype(bf16)
        dg_ref[...] += dg

    return pl.pallas_call(
        body, name="delta_gnorm_bwd", grid=(s // TOK_TILE,),
        in_specs=[_tok_spec(GROUP_W, 0), _tok_spec(GROUP_W, Z_COL // GROUP_W), _vec_spec(GROUP_W),
                  _tok_spec(GROUP_W, 1)],
        out_specs=[_tok_spec(GROUP_W, 0), _tok_spec(GROUP_W, 0), _acc_spec(GROUP_W)],
        out_shape=[jax.ShapeDtypeStruct((s, GROUP_W), f32), jax.ShapeDtypeStruct((s, GROUP_W), bf16),
                   jax.ShapeDtypeStruct((1, GROUP_W), f32)],
        compiler_params=_cparams(("arbitrary",)),
    )(o, proj, ng_e, dycat)


_MESH = pl.DeviceIdType.MESH
_ANY = pl.BlockSpec(memory_space=pl.ANY)
_VMEM = pl.BlockSpec(memory_space=pltpu.VMEM)


def _my_place():
    x, y, c = lax.axis_index("x"), lax.axis_index("y"), lax.axis_index("c")
    return x, y, c, 4 * x + 2 * y + c


def _peer(k, x, y, c):
    px = 1 - x if k & 4 else x
    py = 1 - y if k & 2 else y
    pc = 1 - c if k & 1 else c
    return (px, py, pc), 4 * px + 2 * py + pc


def _exchange_all(src_of_peer, dst_ref, send_sems, recv_sems, x, y, c, me):
    sent = []
    for k in range(1, N_DEV):
        dev, pidx = _peer(k, x, y, c)
        cp = pltpu.make_async_remote_copy(src_ref=src_of_peer(pidx), dst_ref=dst_ref.at[me],
                                          send_sem=send_sems.at[k - 1], recv_sem=recv_sems.at[k - 1],
                                          device_id=dev, device_id_type=_MESH)
        cp.start()
        sent.append(cp)
    for k in range(1, N_DEV):
        dev, pidx = _peer(k, x, y, c)
        pltpu.make_async_remote_copy(src_ref=src_of_peer(pidx), dst_ref=dst_ref.at[pidx],
                                     send_sem=send_sems.at[k - 1], recv_sem=recv_sems.at[k - 1],
                                     device_id=dev, device_id_type=_MESH).wait_recv()
    for cp in sent:
        cp.wait_send()


def _ada_exchange(cv8, w_ada, b_ada8):
    def body(cv_ref, w_ref, b_ref, call_ref, modp_ref, part_s, s1, r1, s2, r2):
        x, y, c, me = _my_place()
        call_ref[me] = cv_ref[...]
        _exchange_all(lambda pidx: cv_ref, call_ref, s1, r1, x, y, c, me)
        bias = b_ref[me]
        for j in range(N_DEV):
            cj = call_ref[j][:, :D_MODEL]
            part_s[j] = _hdot(cj * jax.nn.sigmoid(cj), w_ref[...]) + bias
        modp_ref[me] = part_s[me]
        _exchange_all(lambda pidx: part_s.at[pidx], modp_ref, s2, r2, x, y, c, me)

    nsh = w_ada.shape[1]
    return pl.pallas_call(
        body, name="ada_exchange",
        in_specs=[_VMEM, _VMEM, _VMEM], out_specs=[_VMEM, _VMEM],
        out_shape=[jax.ShapeDtypeStruct((N_DEV, 8, cv8.shape[1]), f32), jax.ShapeDtypeStruct((N_DEV, 8, nsh), f32)],
        scratch_shapes=[pltpu.VMEM((N_DEV, 8, nsh), f32)] + [pltpu.SemaphoreType.DMA((N_DEV - 1,))] * 4,
        compiler_params=pltpu.CompilerParams(vmem_limit_bytes=VMEM_LIMIT),
    )(cv8, w_ada, b_ada8)


def _all_to_all(arrs, name):
    ex = _Exchange(arrs, gather=False)

    def body(*refs):
        srcs, dsts, sems = refs[:ex.n], refs[ex.n:2 * ex.n], refs[2 * ex.n:]
        ex.start(srcs, dsts, sems)
        ex.wait(srcs, dsts, sems)

    return pl.pallas_call(
        body, name=name,
        in_specs=[_ANY] * ex.n, out_specs=[_ANY] * ex.n,
        out_shape=ex.out_shape(), scratch_shapes=ex.scratch(),
    )(*arrs)


class _Exchange:
    def __init__(self, arrs, gather):
        self.arrs, self.gather, self.n = list(arrs), gather, len(arrs)

    def out_shape(self):
        return [jax.ShapeDtypeStruct(((N_DEV,) + a.shape) if self.gather else a.shape, a.dtype) for a in self.arrs]

    def scratch(self):
        if self.n == 0:
            return []
        return [pltpu.SemaphoreType.DMA((self.n, N_DEV - 1)), pltpu.SemaphoreType.DMA((self.n, N_DEV - 1)),
                pltpu.SemaphoreType.DMA((self.n,))]

    def _src(self, srcs, a, idx):
        return srcs[a] if self.gather else srcs[a].at[idx]

    def _copies(self, srcs, dsts, sems, incoming):
        send_sems, recv_sems, _ = sems
        x, y, c, me = _my_place()
        out = []
        for a in range(self.n):
            for k in range(1, N_DEV):
                dev, pidx = _peer(k, x, y, c)
                out.append(pltpu.make_async_remote_copy(
                    src_ref=self._src(srcs, a, pidx), dst_ref=dsts[a].at[pidx if incoming else me],
                    send_sem=send_sems.at[a, k - 1], recv_sem=recv_sems.at[a, k - 1],
                    device_id=dev, device_id_type=_MESH))
        return out

    def _local(self, srcs, dsts, sems):
        me = _my_place()[3]
        return [pltpu.make_async_copy(self._src(srcs, a, me), dsts[a].at[me], sems[2].at[a]) for a in range(self.n)]

    def start(self, srcs, dsts, sems):
        for cp in self._local(srcs, dsts, sems) + self._copies(srcs, dsts, sems, incoming=False):
            cp.start()

    def wait(self, srcs, dsts, sems):
        for cp in self._copies(srcs, dsts, sems, incoming=True):
            cp.wait_recv()
        for cp in self._copies(srcs, dsts, sems, incoming=False):
            cp.wait_send()
        for cp in self._local(srcs, dsts, sems):
            cp.wait()

    def start_at_first_step(self, grid, srcs, dsts, sems):
        first = functools.reduce(jnp.logical_and, [pl.program_id(i) == 0 for i in range(len(grid))])
        pl.when(first)(lambda: self.start(srcs, dsts, sems))

    def wait_at_last_step(self, grid, srcs, dsts, sems):
        last = functools.reduce(jnp.logical_and, [pl.program_id(i) == g - 1 for i, g in enumerate(grid)])
        pl.when(last)(lambda: self.wait(srcs, dsts, sems))


class _ChipGather:
    def __init__(self, shard):
        self.shard = shard

    def out_shape(self):
        return jax.ShapeDtypeStruct((N_DEV,) + self.shard.shape, self.shard.dtype)

    def scratch(self):
        return [pltpu.SemaphoreType.DMA((N_DEV - 1,)), pltpu.SemaphoreType.DMA((N_DEV - 1,)),
                pltpu.SemaphoreType.DMA(())]

    def _place(self):
        x, y, c, me = _my_place()
        return x, y, c, me, (x, y, 1 - c), [(1 - x, y), (x, 1 - y), (1 - x, 1 - y)]

    def _copy(self, out, sems, k, block, to, src=None):
        rows = out.at[4 * block[0] + 2 * block[1] + block[2]]
        return pltpu.make_async_remote_copy(src_ref=rows if src is None else src, dst_ref=rows,
                                            send_sem=sems[0].at[k], recv_sem=sems[1].at[k],
                                            device_id=to, device_id_type=_MESH)

    def start(self, src, out, sems):
        x, y, c, me, sib, chips = self._place()
        pltpu.make_async_copy(src, out.at[me], sems[2]).start()
        self._copy(out, sems, 0, (x, y, c), sib, src=src).start()
        for j, chip in enumerate(chips):
            self._copy(out, sems, 1 + j, (x, y, c), (*chip, c), src=src).start()

    def forward(self, src, out, sems):
        x, y, c, me, sib, chips = self._place()
        for j, chip in enumerate(chips):
            self._copy(out, sems, 1 + j, (*chip, c), (x, y, c)).wait_recv()
            self._copy(out, sems, 4 + j, (*chip, c), sib).start()

    def finish(self, src, out, sems):
        x, y, c, me, sib, chips = self._place()
        self._copy(out, sems, 0, (x, y, 1 - c), (x, y, c)).wait_recv()
        for j, chip in enumerate(chips):
            self._copy(out, sems, 4 + j, (*chip, 1 - c), (x, y, c)).wait_recv()
        self._copy(out, sems, 0, (x, y, c), sib, src=src).wait_send()
        for j, chip in enumerate(chips):
            self._copy(out, sems, 1 + j, (x, y, c), (*chip, c), src=src).wait_send()
            self._copy(out, sems, 4 + j, (*chip, c), sib).wait_send()
        pltpu.make_async_copy(src, out.at[me], sems[2]).wait()


def _ride(body, n_in, n_out, xchg, grid):
    nx = xchg.n
    if nx == 0:
        return body

    def wrapped(*refs):
        ins, xs = refs[:n_in], refs[n_in:n_in + nx]
        outs, xd = refs[n_in + nx:n_in + nx + n_out], refs[n_in + nx + n_out:n_in + 2 * nx + n_out]
        scratch = refs[n_in + 2 * nx + n_out:]
        xchg.start_at_first_step(grid, xs, xd, scratch[-3:])
        body(*ins, *outs, *scratch[:-3])
        xchg.wait_at_last_step(grid, xs, xd, scratch[-3:])

    return wrapped


def _adamw_math(w, g, m, v):
    m2 = ADAM_B1 * m + (1.0 - ADAM_B1) * g
    v2 = ADAM_B2 * v + (1.0 - ADAM_B2) * (g * g)
    m_hat = m2 / (1.0 - ADAM_B1 ** ADAM_STEP)
    v_hat = v2 / (1.0 - ADAM_B2 ** ADAM_STEP)
    delta = -ADAM_LR * (m_hat / (jnp.sqrt(v_hat) + ADAM_EPS) + ADAM_WD * w)
    return delta, m2, v2


def _row_tile(rows):
    for t in (256, 128, 64, 32, 16, 8):
        if rows % t == 0:
            return t
    return rows


def _reduce_adamw(parts, w, m, v, name):
    _, r, cdim = parts.shape
    tr = _row_tile(r)

    def body(p_ref, w_ref, m_ref, v_ref, g_ref, d_ref, m2_ref, v2_ref):
        g = p_ref[0].astype(f32)
        for j in range(1, N_DEV):
            g = g + p_ref[j].astype(f32)
        delta, m2, v2 = _adamw_math(w_ref[...], g, m_ref[...], v_ref[...])
        g_ref[...] = g
        d_ref[...] = delta
        m2_ref[...] = m2
        v2_ref[...] = v2

    spec = pl.BlockSpec((tr, cdim), lambda i: (i, 0))
    return pl.pallas_call(
        body, name=name, grid=(r // tr,),
        in_specs=[pl.BlockSpec((N_DEV, tr, cdim), lambda i: (0, i, 0)), spec, spec, spec],
        out_specs=[spec] * 4,
        out_shape=[jax.ShapeDtypeStruct((r, cdim), f32)] * 4,
        compiler_params=_cparams(("parallel",)),
    )(parts, w, m, v)


def _adamw(w, g, m, v, name):
    r, cdim = w.shape
    tr = _row_tile(r)

    def body(w_ref, g_ref, m_ref, v_ref, d_ref, m2_ref, v2_ref):
        delta, m2, v2 = _adamw_math(w_ref[...], g_ref[...], m_ref[...], v_ref[...])
        d_ref[...] = delta
        m2_ref[...] = m2
        v2_ref[...] = v2

    spec = pl.BlockSpec((tr, cdim), lambda i: (i, 0))
    return pl.pallas_call(
        body, name=name, grid=(r // tr,),
        in_specs=[spec] * 4, out_specs=[spec] * 3,
        out_shape=[jax.ShapeDtypeStruct((r, cdim), f32)] * 3,
        compiler_params=_cparams(("parallel",)),
    )(w, g, m, v)


def _sum_devices(parts, name):
    _, r, cdim = parts.shape

    def body(p_ref, o_ref):
        g = p_ref[0]
        for j in range(1, N_DEV):
            g = g + p_ref[j]
        o_ref[...] = g

    return pl.pallas_call(
        body, name=name, out_shape=jax.ShapeDtypeStruct((r, cdim), f32),
        in_specs=[_VMEM], out_specs=_VMEM,
    )(parts)


def _ada_wgrad(c_all8, dmod_cols):
    nsh = dmod_cols.shape[1]

    def body(c_ref, d_ref, o_ref):
        cv = c_ref[...]
        o_ref[...] = lax.dot_general(cv * jax.nn.sigmoid(cv), d_ref[...], _TN, precision=_HI,
                                     preferred_element_type=f32)

    return pl.pallas_call(
        body, name="ada_wgrad", out_shape=jax.ShapeDtypeStruct((D_MODEL, nsh), f32),
        in_specs=[_VMEM, _VMEM], out_specs=_VMEM,
        compiler_params=pltpu.CompilerParams(vmem_limit_bytes=VMEM_LIMIT),
    )(c_all8, dmod_cols)


def _cols(t):
    return t.transpose(1, 0, 2).reshape(t.shape[1], N_DEV * t.shape[2])


def _col_blocks(t, n):
    return t.reshape(t.shape[0], N_DEV, n).transpose(1, 0, 2).astype(bf16)


def _row_blocks(t):
    return t.reshape(N_DEV, t.shape[0] // N_DEV, t.shape[1]).astype(bf16)


def _local_step(x, tgt, mod, norm_attn_g, w_in_sh, rel_bias, conv_full, a_log, dt_bias, delta_norm_g,
                norm_ffn_g, final_norm_g, w_out_sh, w_gate_sh, w_up_sh, w_down_sh):
    s = x.shape[0]
    sh1, sc1, g1, sh2, sc2, g2 = [mod[:, i * D_MODEL:(i + 1) * D_MODEL] for i in range(6)]
    nag = norm_attn_g.reshape(1, D_MODEL)
    nfg = norm_ffn_g.reshape(1, D_MODEL)
    fg = final_norm_g.reshape(1, D_MODEL)
    idx = _bucket_tables()
    bias = _bias_tables(rel_bias, idx)
    alog_e = jnp.repeat(a_log.reshape(N_HEADS), HEAD_DIM)[None]
    dt_e = jnp.repeat(dt_bias.reshape(N_HEADS), HEAD_DIM)[None]
    ng_e = jnp.tile(delta_norm_g.reshape(HEAD_DIM), N_HEADS)[None]

    h1, w_in_g = _ln_mod_fwd(x, nag, sc1, sh1, w_in_sh, "ln1_fwd")
    w_in_p = jnp.pad(_cols(w_in_g), ((0, 0), (0, IN_PAD - IN_WIDTH)))
    proj, (w_out_g, w_gate_g) = _mm(h1, w_in_p, "nn", f32, 512, IN_PAD, 1024, "in_proj",
                                    xchg=_Exchange([w_out_sh, w_gate_sh], gather=True))
    (y_attn, lse), (w_up_g, w_down_g) = _attn_fwd(proj, bias, _Exchange([w_up_sh, w_down_sh], gather=True))
    w_out_b = w_out_g.reshape(2 * GROUP_W, D_MODEL)
    w_gate_b, w_up_b = _cols(w_gate_g), _cols(w_up_g)
    w_down_b = w_down_g.reshape(D_FF, D_MODEL)
    n_ff = w_gate_sh.shape[1]
    sconv = _conv_silu_fwd(proj, conv_full)
    qn, kn, beta, g = _delta_prep_fwd(sconv, proj, alog_e, dt_e)
    u, w, qt, kh, qk, tinv, gm = _delta_chunk_pre(qn, kn, sconv, beta, g)
    o, ss = _delta_scan_fwd(u, w, qt, kh, qk, gm)
    y_delta = _gnorm_fwd(o, proj, ng_e)
    y, x1, h2 = _proj_resid_ln_mod_fwd([(y_attn, w_out_b[:GROUP_W]), (y_delta, w_out_b[GROUP_W:])],
                                       x, g1, nfg, sc2, sh2, "out_proj_ln2")
    act, gate, up = _ffn_up(h2, w_gate_b, w_up_b, "ffn_up")
    dx2, dy2, loss, dfg, dg2 = _proj_final_loss_bwd(act, w_down_b, x1, g2, fg, tgt, "ffn_down_loss")

    dgate, dup = _ffn_down_dx(dy2, w_down_b, gate, up, "ffn_down_dx")
    g_down = _mm(act, dy2, "tn", f32, 1408, 1024, 1024, "ffn_down_dw")
    (dx1, dsh2, dsc2, dnfg, dy, dg1), (r_down,) = _proj_ln_mod_bwd(
        [(dgate, w_gate_b), (dup, w_up_b)], x1, nfg, sc2, dx2, 256, "ffn_up_dx_ln2",
        _Exchange([_row_blocks(g_down)], gather=False), gate=g1, y=y)
    g_gate = _mm(h2, dgate, "tn", f32, 1024, 1408, 1024, "ffn_gate_dw")
    g_up = _mm(h2, dup, "tn", f32, 1024, 1408, 1024, "ffn_up_dw")
    dycat = _mm(dy, w_out_b, "nt", f32, 512, 1024, 1024, "out_proj_dx")
    g_out = jnp.concatenate([_mm(y_attn, dy, "tn", f32, GROUP_W, 1024, 1024, "out_proj_dw_attn"),
                             _mm(y_delta, dy, "tn", f32, GROUP_W, 1024, 1024, "out_proj_dw_delta")], axis=0)
    dq, dk, dv, dbias = _attn_bwd(proj, bias, y_attn, lse, dycat)
    g_rb = _bias_grad(dbias, idx)[:, :, 0].T
    do, dz, dng = _gnorm_bwd(o, proj, ng_e, dycat)
    dso = _delta_scan_bwd(w, qt, kh, qk, gm, do)
    dqn, dkn, dvd, dbeta, dgd = _delta_chunk_bwd(qn, kn, sconv, beta, g, tinv, ss, dso, do)
    dsq, dsk, dba, dal, ddt = _delta_prep_bwd(sconv, proj, alog_e, dt_e, dqn, dkn, dbeta, dgd)
    (dxc, g_conv), (r_gate, r_up, r_out) = _conv_silu_bwd(
        proj, conv_full, (dsq, dsk, dvd),
        _Exchange([_col_blocks(g_gate, n_ff), _col_blocks(g_up, n_ff), _row_blocks(g_out)],
                  gather=False))
    pieces = ((dq, 0), (dk, GROUP_W), (dv, 2 * GROUP_W), (dxc, DELTA_COL), (dz, Z_COL), (dba, BA_BLOCK * 128))
    g_in = jnp.concatenate(
        [_mm(h1, p, "tn", f32, 1024, min(p.shape[1], 768), 1024, "in_proj_dw_%d" % c) for p, c in pieces], axis=1)
    (gx, dsh1, dsc1, dnag), (r_in,) = _proj_ln_mod_bwd(
        [(p, w_in_p[:, c:c + p.shape[1]]) for p, c in pieces], x, nag, sc1, dx1, TOK_TILE, "in_proj_dx_ln1",
        _Exchange([_col_blocks(g_in[:, :IN_WIDTH], IN_WIDTH // N_DEV)], gather=False))
    grads = dict(
        x=gx, mod=jnp.concatenate([dsh1, dsc1, dg1, dsh2, dsc2, dg2], axis=1),
        norm_attn_g=dnag, norm_ffn_g=dnfg, final_norm_g=dfg, rel_bias=g_rb, conv_w=g_conv,
        a_log=dal.reshape(N_HEADS, HEAD_DIM).sum(-1), dt_bias=ddt.reshape(N_HEADS, HEAD_DIM).sum(-1),
        delta_norm_g=dng.reshape(N_HEADS, HEAD_DIM).sum(0),
        w_in=r_in, w_out=r_out, w_gate=r_gate, w_up=r_up, w_down=r_down)
    return loss[0, 0], grads


MISC_OFF = dict(rel_bias=0, a_log=256, dt_bias=264, delta_norm_g=272)


def _misc_row(rel_bias, a_log, dt_bias, delta_norm_g):
    flat = jnp.concatenate([rel_bias.reshape(-1), a_log.reshape(-1), dt_bias.reshape(-1), delta_norm_g.reshape(-1)])
    return jnp.pad(flat, (0, D_MODEL - flat.shape[0]))[None]


def _pack_small(b_ada, nag, nfg, fng, rel_bias, a_log, dt_bias, dng, conv_shard):
    rows = [b_ada.reshape(6, D_MODEL), nag.reshape(1, D_MODEL), nfg.reshape(1, D_MODEL), fng.reshape(1, D_MODEL),
            _misc_row(rel_bias, a_log, dt_bias, dng),
            jnp.pad(conv_shard.reshape(-1), (0, D_MODEL - conv_shard.size))[None],
            jnp.zeros((5, D_MODEL), f32)]
    return jnp.concatenate(rows, axis=0)


def _unpack_small(p, conv_shape):
    misc = p[9]
    return dict(
        b_ada=p[0:6].reshape(1, 6 * D_MODEL), norm_attn_g=p[6:7], norm_ffn_g=p[7:8], final_norm_g=p[8],
        rel_bias=misc[0:256].reshape(N_BUCKETS, N_HEADS), a_log=misc[256:264].reshape(1, N_HEADS),
        dt_bias=misc[264:272].reshape(1, N_HEADS), delta_norm_g=misc[272:336].reshape(1, HEAD_DIM),
        conv_w=p[10, :conv_shape[1] * conv_shape[2]].reshape(conv_shape))


def kernel(x, c, w_ada, b_ada, norm_attn_g, w_in, rel_bias, conv_w, a_log, dt_bias, delta_norm_g, w_out, norm_ffn_g, w_gate, w_up, w_down, final_norm_g, loss_target, m_w_ada, m_b_ada, m_norm_attn_g, m_w_in, m_rel_bias, m_conv_w, m_a_log, m_dt_bias, m_delta_norm_g, m_w_out, m_norm_ffn_g, m_w_gate, m_w_up, m_w_down, m_final_norm_g, v_w_ada, v_b_ada, v_norm_attn_g, v_w_in, v_rel_bias, v_conv_w, v_a_log, v_dt_bias, v_delta_norm_g, v_w_out, v_norm_ffn_g, v_w_gate, v_w_up, v_w_down, v_final_norm_g):
    me = 4 * lax.axis_index("x") + 2 * lax.axis_index("y") + lax.axis_index("c")
    ada_sh = w_ada.shape[2]
    conv_sh = conv_w.shape[2]

    cv = jnp.concatenate([c[0], conv_w[0].reshape(-1)])
    cv8 = jnp.zeros((8, 2 * D_MODEL), f32).at[0, :cv.shape[0]].set(cv)
    b8 = jnp.broadcast_to(b_ada.reshape(N_DEV, 1, ada_sh), (N_DEV, 8, ada_sh))
    call, modp = _ada_exchange(cv8, w_ada[0], b8)
    mod = modp[:, 0, :].reshape(1, 6 * D_MODEL)
    c_all = call[:, 0, :D_MODEL]
    conv_full = call[:, 0, D_MODEL:D_MODEL + CONV_WIDTH * conv_sh].reshape(N_DEV, CONV_WIDTH, conv_sh)
    conv_full = conv_full.transpose(1, 0, 2).reshape(CONV_WIDTH, N_DEV * conv_sh)

    loss_local, gr = _local_step(x[0], loss_target[0], mod, norm_attn_g, w_in[0].astype(bf16), rel_bias, conv_full, a_log,
                                 dt_bias, delta_norm_g, norm_ffn_g, final_norm_g, w_out[0].astype(bf16),
                                 w_gate[0].astype(bf16), w_up[0].astype(bf16), w_down[0].astype(bf16))
    loss = lax.psum(loss_local, ("x", "y", "c"))

    small = jnp.concatenate([
        gr["mod"].reshape(6, D_MODEL), gr["norm_attn_g"], gr["norm_ffn_g"], gr["final_norm_g"],
        gr["conv_w"].reshape(6, D_MODEL),
        _misc_row(gr["rel_bias"], gr["a_log"], gr["dt_bias"], gr["delta_norm_g"])], axis=0)
    parts = _all_to_all([jnp.broadcast_to(small[None], (N_DEV,) + small.shape)], "small_gather")[0]
    tot = _sum_devices(parts, "small_sum")
    g_conv_full = tot[9:15].reshape(CONV_WIDTH, N_DEV * conv_sh)
    g_conv = lax.dynamic_slice(g_conv_full, (0, me * conv_sh), (CONV_WIDTH, conv_sh))
    misc = tot[15]
    g_small = _pack_small(tot[0:6], tot[6], tot[7], tot[8], misc[0:256], misc[256:264], misc[264:272],
                          misc[272:336], g_conv)
    pk = lambda pre: _pack_small(pre[0], pre[1], pre[2], pre[3], pre[4], pre[5], pre[6], pre[7], pre[8])
    w_small = pk((b_ada, norm_attn_g, norm_ffn_g, final_norm_g, rel_bias, a_log, dt_bias, delta_norm_g, conv_w))
    m_small = pk((m_b_ada, m_norm_attn_g, m_norm_ffn_g, m_final_norm_g, m_rel_bias, m_a_log, m_dt_bias,
                  m_delta_norm_g, m_conv_w))
    v_small = pk((v_b_ada, v_norm_attn_g, v_norm_ffn_g, v_final_norm_g, v_rel_bias, v_a_log, v_dt_bias,
                  v_delta_norm_g, v_conv_w))
    d_small, m2_small, v2_small = _adamw(w_small, g_small, m_small, v_small, "adamw_small")
    cshape = conv_w.shape
    G, Dl, M2, V2 = (_unpack_small(t, cshape) for t in (g_small, d_small, m2_small, v2_small))

    dmod_all = parts[:, 0:6, :].reshape(N_DEV, 6 * D_MODEL)
    dmod_cols = lax.dynamic_slice(dmod_all, (0, me * ada_sh), (N_DEV, ada_sh))
    g_ada = _ada_wgrad(c_all, dmod_cols)
    d_ada, m2_ada, v2_ada = _adamw(w_ada[0], g_ada, m_w_ada[0], v_w_ada[0], "adamw_w_ada")

    big = {}
    for name, w_, m_, v_ in (("w_in", w_in, m_w_in, v_w_in), ("w_out", w_out, m_w_out, v_w_out),
                             ("w_gate", w_gate, m_w_gate, v_w_gate), ("w_up", w_up, m_w_up, v_w_up),
                             ("w_down", w_down, m_w_down, v_w_down)):
        big[name] = [t[None] for t in _reduce_adamw(gr[name], w_[0], m_[0], v_[0], "reduce_adamw_" + name)]

    def leaf(i, name):
        if name == "w_ada":
            return (g_ada, d_ada, m2_ada, v2_ada)[i][None]
        if name in big:
            return big[name][i]
        return (G, Dl, M2, V2)[i][name]

    order = ["w_ada", "b_ada", "norm_attn_g", "w_in", "rel_bias", "conv_w", "a_log", "dt_bias", "delta_norm_g",
             "w_out", "norm_ffn_g", "w_gate", "w_up", "w_down", "final_norm_g"]
    outs = [loss, gr["x"][None]]
    for i in range(4):
        outs += [leaf(i, n) for n in order]
    return tuple(outs)
```

```python
import functools
import math

import jax
import jax.numpy as jnp
from jax import lax
from jax.experimental import pallas as pl
from jax.experimental.pallas import tpu as pltpu

f32 = jnp.float32
bf16 = jnp.bfloat16

D_MODEL = 1024
HEAD_DIM = 64
N_HEADS = 8
GROUP_W = 512
IN_WIDTH = 3600
IN_PAD = 3840
D_FF = 2816
EPS = 1e-6
NEG_INF = -1e30
BAND = 128
PAD_UNIT = 2048
DILATIONS = (1, 4, 16)
N_BUCKETS = 32
MAX_DISTANCE = 2048
CONV_WIDTH = 4
CHUNK = 64
N_DEV = 8
VMEM_LIMIT = 56 * 1024 * 1024

ADAM_LR, ADAM_B1, ADAM_B2, ADAM_EPS, ADAM_WD, ADAM_STEP = 0.001, 0.9, 0.999, 1e-08, 0.01, 10


def _cparams(sem):
    return pltpu.CompilerParams(dimension_semantics=sem, vmem_limit_bytes=VMEM_LIMIT)


def _mm(a, b, mode, out_dtype, tm, tn, tk, name, xchg=None):
    if mode == "nn":
        (m, k), (_, n) = a.shape, b.shape
        a_spec = pl.BlockSpec((tm, tk), lambda j, i, kk: (i, kk))
        b_spec = pl.BlockSpec((tk, tn), lambda j, i, kk: (kk, j))
        dims = (((1,), (0,)), ((), ()))
    elif mode == "nt":
        (m, k), (n, _) = a.shape, b.shape
        a_spec = pl.BlockSpec((tm, tk), lambda j, i, kk: (i, kk))
        b_spec = pl.BlockSpec((tn, tk), lambda j, i, kk: (j, kk))
        dims = (((1,), (1,)), ((), ()))
    else:
        (k, m), (_, n) = a.shape, b.shape
        a_spec = pl.BlockSpec((tk, tm), lambda j, i, kk: (kk, i))
        b_spec = pl.BlockSpec((tk, tn), lambda j, i, kk: (kk, j))
        dims = (((0,), (0,)), ((), ()))
    assert m % tm == 0 and n % tn == 0 and k % tk == 0, (name, m, n, k, tm, tn, tk)
    nk = k // tk
    grid = (n // tn, m // tm, nk)
    nx = xchg.n if xchg is not None else 0

    def body(*refs):
        a_ref, b_ref = refs[:2]
        o_ref = refs[2 + nx]
        scratch = refs[3 + 2 * nx:]
        if nx:
            xrefs = (refs[2:2 + nx], refs[3 + nx:3 + 2 * nx], scratch[-3:])
            xchg.start_at_first_step(grid, *xrefs)
        if nk == 1:
            o_ref[...] = lax.dot_general(a_ref[...].astype(bf16), b_ref[...].astype(bf16), dims,
                                         preferred_element_type=f32).astype(o_ref.dtype)
        else:
            acc_ref = scratch[0]
            kk = pl.program_id(2)

            @pl.when(kk == 0)
            def _():
                acc_ref[...] = jnp.zeros_like(acc_ref)

            acc_ref[...] += lax.dot_general(a_ref[...].astype(bf16), b_ref[...].astype(bf16), dims,
                                            preferred_element_type=f32)

            @pl.when(kk == nk - 1)
            def _():
                o_ref[...] = acc_ref[...].astype(o_ref.dtype)
        if nx:
            xchg.wait_at_last_step(grid, *xrefs)

    out = pl.pallas_call(
        body, name=name, grid=grid,
        in_specs=[a_spec, b_spec] + ([_ANY] * nx),
        out_specs=[pl.BlockSpec((tm, tn), lambda j, i, kk: (i, j))] + ([_ANY] * nx),
        out_shape=[jax.ShapeDtypeStruct((m, n), out_dtype)] + (xchg.out_shape() if nx else []),
        scratch_shapes=([pltpu.VMEM((tm, tn), f32)] if nk > 1 else []) + (xchg.scratch() if nx else []),
        compiler_params=_cparams(("arbitrary",) * 3 if nx else ("parallel", "parallel", "arbitrary")),
    )(a, b, *(xchg.arrs if nx else []))
    return (out[0], out[1:]) if nx else out[0]


TOK_TILE = 512
SUB_COLS = 384


def _row_spec(width, tile=TOK_TILE):
    return pl.BlockSpec((tile, width), lambda i: (i, 0))


def _vec_spec(width, rows=1):
    return pl.BlockSpec((rows, width), lambda i: (0, 0))


def _ln_mod_fwd(x, gain, sc, sh, shard, name):
    s, d = x.shape
    nt = s // TOK_TILE
    ride = _ChipGather(shard)

    def body(x_ref, g_ref, sc_ref, sh_ref, sh_in, h_ref, sh_out, *sems):
        i = pl.program_id(0)
        pl.when(i == 0)(lambda: ride.start(sh_in, sh_out, sems))
        xv = x_ref[...]
        rstd = lax.rsqrt(jnp.mean(xv * xv, axis=-1, keepdims=True) + EPS)
        h = (xv * rstd) * g_ref[...] * (1.0 + sc_ref[...]) + sh_ref[...]
        h_ref[...] = h.astype(bf16)
        pl.when(i == nt // 2)(lambda: ride.forward(sh_in, sh_out, sems))
        pl.when(i == nt - 1)(lambda: ride.finish(sh_in, sh_out, sems))

    return pl.pallas_call(
        body, name=name, grid=(nt,),
        in_specs=[_row_spec(d), _vec_spec(d), _vec_spec(d), _vec_spec(d), _ANY],
        out_specs=[_row_spec(d), _ANY],
        out_shape=[jax.ShapeDtypeStruct((s, d), bf16), ride.out_shape()],
        scratch_shapes=ride.scratch(),
        compiler_params=_cparams(("arbitrary",)),
    )(x, gain, sc, sh, shard)


def _proj_resid_ln_mod_fwd(pairs, x, gate, gain, sc, sh, name):
    s, d = x.shape
    npair = len(pairs)

    def body(*refs):
        aw = refs[:2 * npair]
        x_ref, gt_ref, g_ref, sc_ref, sh_ref, y_ref, x1_ref, h_ref = refs[2 * npair:]
        y = jnp.dot(aw[0][...].astype(bf16), aw[1][...], preferred_element_type=f32)
        for t in range(1, npair):
            y = y + jnp.dot(aw[2 * t][...].astype(bf16), aw[2 * t + 1][...], preferred_element_type=f32)
        y_ref[...] = y
        x1 = x_ref[...] + gt_ref[...] * y
        x1_ref[...] = x1
        rstd = lax.rsqrt(jnp.mean(x1 * x1, axis=-1, keepdims=True) + EPS)
        h = (x1 * rstd) * g_ref[...] * (1.0 + sc_ref[...]) + sh_ref[...]
        h_ref[...] = h.astype(bf16)

    aw_specs, aw = [], []
    for a, w in pairs:
        aw_specs += [_row_spec(a.shape[1]), pl.BlockSpec(w.shape, lambda i: (0, 0))]
        aw += [a, w]
    return pl.pallas_call(
        body, name=name, grid=(s // TOK_TILE,),
        in_specs=aw_specs + [_row_spec(d)] + [_vec_spec(d)] * 4,
        out_specs=[_row_spec(d)] * 3,
        out_shape=[jax.ShapeDtypeStruct((s, d), f32)] * 2 + [jax.ShapeDtypeStruct((s, d), bf16)],
        compiler_params=_cparams(("parallel",)),
    )(*aw, x, gate, gain, sc, sh)


FFN_TN = 1408


def _ffn_up(h2, w_gate, w_up, name):
    s, d = h2.shape
    tm = TOK_TILE

    def body(h_ref, wg_ref, wu_ref, a_ref, g_ref, u_ref):
        h = h_ref[...]
        g = jnp.dot(h, wg_ref[...], preferred_element_type=f32)
        u = jnp.dot(h, wu_ref[...], preferred_element_type=f32)
        a_ref[...] = (g * jax.nn.sigmoid(g) * u).astype(bf16)
        g_ref[...] = g.astype(bf16)
        u_ref[...] = u.astype(bf16)

    w_spec = pl.BlockSpec((d, FFN_TN), lambda j, i: (0, j))
    o_spec = pl.BlockSpec((tm, FFN_TN), lambda j, i: (i, j))
    return pl.pallas_call(
        body, name=name, grid=(D_FF // FFN_TN, s // tm),
        in_specs=[pl.BlockSpec((tm, d), lambda j, i: (i, 0)), w_spec, w_spec],
        out_specs=[o_spec] * 3,
        out_shape=[jax.ShapeDtypeStruct((s, D_FF), bf16)] * 3,
        compiler_params=_cparams(("parallel", "parallel")),
    )(h2, w_gate, w_up)


def _ffn_down_dx(dy2, w_down, gate, up, name):
    s, d = dy2.shape
    tm = TOK_TILE

    def body(dy_ref, w_ref, g_ref, u_ref, dg_ref, du_ref):
        dy = dy_ref[...]
        for c0 in range(0, FFN_TN, SUB_COLS):
            cols = slice(c0, min(c0 + SUB_COLS, FFN_TN))
            da = lax.dot_general(dy, w_ref[cols, :], _NT, preferred_element_type=f32)
            g = g_ref[:, cols].astype(f32)
            sg = jax.nn.sigmoid(g)
            du_ref[:, cols] = (da * g * sg).astype(bf16)
            dg_ref[:, cols] = (da * u_ref[:, cols].astype(f32) * sg * (1.0 + g * (1.0 - sg))).astype(bf16)

    t_spec = pl.BlockSpec((tm, FFN_TN), lambda j, i: (i, j))
    return pl.pallas_call(
        body, name=name, grid=(D_FF // FFN_TN, s // tm),
        in_specs=[pl.BlockSpec((tm, d), lambda j, i: (i, 0)), pl.BlockSpec((FFN_TN, d), lambda j, i: (j, 0)),
                  t_spec, t_spec],
        out_specs=[t_spec, t_spec],
        out_shape=[jax.ShapeDtypeStruct((s, D_FF), bf16)] * 2,
        compiler_params=_cparams(("parallel", "parallel")),
    )(dy2, w_down, gate, up)


def _acc_spec(width):
    return pl.BlockSpec((1, width), lambda i: (0, 0))


def _proj_final_loss_bwd(a, w, x1, gate2, final_g, target, name):
    s, d = x1.shape
    k = a.shape[1]

    def body(a_ref, w_ref, x1_ref, gt_ref, fg_ref, tg_ref, dx2_ref, dy2_ref, loss_ref, dfg_ref, dgt_ref):
        @pl.when(pl.program_id(0) == 0)
        def _():
            loss_ref[...] = jnp.zeros_like(loss_ref)
            dfg_ref[...] = jnp.zeros_like(dfg_ref)
            dgt_ref[...] = jnp.zeros_like(dgt_ref)

        y2 = jnp.dot(a_ref[...], w_ref[...], preferred_element_type=f32)
        gt = gt_ref[...]
        fg = fg_ref[...]
        x2 = x1_ref[...] + gt * y2
        rstd = lax.rsqrt(jnp.mean(x2 * x2, axis=-1, keepdims=True) + EPS)
        xn = x2 * rstd
        err = xn * fg - tg_ref[...]
        row = jnp.sum(err * err, axis=-1, keepdims=True) * (0.5 / d)
        loss_ref[...] += jnp.sum(row, axis=0, keepdims=True) + jnp.zeros_like(loss_ref)
        dout = err * (1.0 / d)
        dfg_ref[...] += jnp.sum(dout * xn, axis=0, keepdims=True)
        dxn = dout * fg
        dx2 = rstd * (dxn - xn * jnp.mean(dxn * xn, axis=-1, keepdims=True))
        dx2_ref[...] = dx2
        dgt_ref[...] += jnp.sum(dx2 * y2, axis=0, keepdims=True)
        dy2_ref[...] = (gt * dx2).astype(bf16)

    return pl.pallas_call(
        body, name=name, grid=(s // TOK_TILE,),
        in_specs=[_row_spec(k), pl.BlockSpec((k, d), lambda i: (0, 0)), _row_spec(d), _vec_spec(d), _vec_spec(d),
                  _row_spec(d)],
        out_specs=[_row_spec(d), _row_spec(d), _acc_spec(128), _acc_spec(d), _acc_spec(d)],
        out_shape=[jax.ShapeDtypeStruct((s, d), f32), jax.ShapeDtypeStruct((s, d), bf16),
                   jax.ShapeDtypeStruct((1, 128), f32), jax.ShapeDtypeStruct((1, d), f32),
                   jax.ShapeDtypeStruct((1, d), f32)],
        compiler_params=_cparams(("arbitrary",)),
    )(a, w, x1, gate2, final_g, target)


def _proj_ln_mod_bwd(pairs, xin, gain, sc, dres, tm, name, xchg, gate=None, y=None):
    s, d = xin.shape
    with_gate = gate is not None
    npair = len(pairs)
    n_in = 2 * npair + (7 if with_gate else 5) - 1
    n_out = 6 if with_gate else 4

    def body(*refs):
        ab = refs[:2 * npair]
        if with_gate:
            (x_ref, g_ref, sc_ref, dr_ref, gt_ref, y_ref,
             dx_ref, dsh_ref, dsc_ref, dg_ref, dy_ref, dgt_ref) = refs[2 * npair:]
        else:
            x_ref, g_ref, sc_ref, dr_ref, dx_ref, dsh_ref, dsc_ref, dg_ref = refs[2 * npair:]

        @pl.when(pl.program_id(0) == 0)
        def _():
            dsh_ref[...] = jnp.zeros_like(dsh_ref)
            dsc_ref[...] = jnp.zeros_like(dsc_ref)
            dg_ref[...] = jnp.zeros_like(dg_ref)
            if with_gate:
                dgt_ref[...] = jnp.zeros_like(dgt_ref)

        dh = lax.dot_general(ab[0][...].astype(bf16), ab[1][...], _NT, preferred_element_type=f32)
        for t in range(1, npair):
            dh = dh + lax.dot_general(ab[2 * t][...].astype(bf16), ab[2 * t + 1][...], _NT,
                                      preferred_element_type=f32)
        xv = x_ref[...]
        g = g_ref[...]
        sc1 = 1.0 + sc_ref[...]
        rstd = lax.rsqrt(jnp.mean(xv * xv, axis=-1, keepdims=True) + EPS)
        xn = xv * rstd
        dsh_ref[...] += jnp.sum(dh, axis=0, keepdims=True)
        dsc_ref[...] += jnp.sum(dh * (xn * g), axis=0, keepdims=True)
        dg_ref[...] += jnp.sum(dh * sc1 * xn, axis=0, keepdims=True)
        dxn = dh * sc1 * g
        dx = dr_ref[...] + rstd * (dxn - xn * jnp.mean(dxn * xn, axis=-1, keepdims=True))
        dx_ref[...] = dx
        if with_gate:
            dgt_ref[...] += jnp.sum(dx * y_ref[...], axis=0, keepdims=True)
            dy_ref[...] = (gt_ref[...] * dx).astype(bf16)

    row = lambda width: pl.BlockSpec((tm, width), lambda i: (i, 0))
    in_specs, args = [], []
    for a, b in pairs:
        in_specs += [row(a.shape[1]), pl.BlockSpec(b.shape, lambda i: (0, 0))]
        args += [a, b]
    in_specs += [row(d), _vec_spec(d), _vec_spec(d), row(d)]
    args += [xin, gain, sc, dres]
    out_specs = [row(d), _acc_spec(d), _acc_spec(d), _acc_spec(d)]
    out_shape = [jax.ShapeDtypeStruct((s, d), f32)] + [jax.ShapeDtypeStruct((1, d), f32)] * 3
    if with_gate:
        in_specs += [_vec_spec(d), row(d)]
        out_specs += [row(d), _acc_spec(d)]
        out_shape += [jax.ShapeDtypeStruct((s, d), bf16), jax.ShapeDtypeStruct((1, d), f32)]
        args += [gate, y]
    grid = (s // tm,)
    out = pl.pallas_call(
        _ride(body, n_in, n_out, xchg, grid), name=name, grid=grid,
        in_specs=in_specs + [_ANY] * xchg.n, out_specs=out_specs + [_ANY] * xchg.n,
        out_shape=out_shape + xchg.out_shape(), scratch_shapes=xchg.scratch(),
        compiler_params=_cparams(("arbitrary",)),
    )(*args, *xchg.arrs)
    return out[:n_out], out[n_out:]


def _bucket_tables():
    import numpy as np
    qi = np.arange(BAND)[:, None]
    kj = np.arange(2 * BAND)[None, :]
    steps = qi + BAND - kj
    max_exact = N_BUCKETS // 2
    out = []
    for d in DILATIONS:
        dist = np.maximum(steps, 0) * d
        dist_f = np.maximum(dist, 1).astype(np.float32)
        large = max_exact + (np.log(dist_f / np.float32(max_exact)) / np.float32(math.log(MAX_DISTANCE / max_exact))
                             * np.float32(N_BUCKETS - max_exact)).astype(np.int32)
        out.append(np.where(dist < max_exact, dist, np.minimum(large, N_BUCKETS - 1)))
    return jnp.asarray(np.stack(out).astype(np.int32))


def _bias_tables(rel_bias, idx):
    def body(idx_ref, rb_ref, o_ref):
        h = pl.program_id(1)
        idxv = idx_ref[0]
        acc = jnp.zeros((BAND, 2 * BAND), f32)
        for b in range(N_BUCKETS):
            acc = jnp.where(idxv == b, rb_ref[b, h], acc)
        o_ref[0, 0] = jnp.where(_attn_masks()[1], acc, NEG_INF)

    return pl.pallas_call(
        body, name="attn_bias_tables", grid=(3, N_HEADS),
        in_specs=[pl.BlockSpec((1, BAND, 2 * BAND), lambda br, h: (br, 0, 0)),
                  pl.BlockSpec(memory_space=pltpu.SMEM)],
        out_specs=pl.BlockSpec((1, 1, BAND, 2 * BAND), lambda br, h: (br, h, 0, 0)),
        out_shape=jax.ShapeDtypeStruct((3, N_HEADS, BAND, 2 * BAND), f32),
        compiler_params=_cparams(("parallel", "parallel")),
    )(idx, rel_bias)


def _bias_grad(dbias, idx):
    def body(idx_ref, db_ref, o_ref):
        br = pl.program_id(1)

        @pl.when(br == 0)
        def _():
            o_ref[...] = jnp.zeros_like(o_ref)

        idxv = idx_ref[0]
        dbv = db_ref[0, 0]
        row = lax.broadcasted_iota(jnp.int32, (N_BUCKETS, 128), 0)
        acc = jnp.zeros((N_BUCKETS, 128), f32)
        for b in range(N_BUCKETS):
            sb = jnp.sum(jnp.sum(jnp.where(idxv == b, dbv, 0.0), axis=1, keepdims=True), axis=0, keepdims=True)
            acc = acc + jnp.where(row == b, sb, 0.0)
        o_ref[0] += acc

    return pl.pallas_call(
        body, name="attn_bias_grad", grid=(N_HEADS, 3),
        in_specs=[pl.BlockSpec((1, BAND, 2 * BAND), lambda h, br: (br, 0, 0)),
                  pl.BlockSpec((1, 1, BAND, 2 * BAND), lambda h, br: (br, h, 0, 0))],
        out_specs=pl.BlockSpec((1, N_BUCKETS, 128), lambda h, br: (h, 0, 0)),
        out_shape=jax.ShapeDtypeStruct((N_HEADS, N_BUCKETS, 128), f32),
        compiler_params=_cparams(("parallel", "arbitrary")),
    )(idx, dbias)


def _attn_masks():
    lane = lax.broadcasted_iota(jnp.int32, (BAND, 128), 1)
    m0 = lane < HEAD_DIM
    qi = lax.broadcasted_iota(jnp.int32, (BAND, 2 * BAND), 0)
    kj = lax.broadcasted_iota(jnp.int32, (BAND, 2 * BAND), 1)
    steps = qi + BAND - kj
    in_window = (steps >= 0) & (steps <= BAND)
    return m0, in_window, kj >= BAND


_NT = (((1,), (1,)), ((), ()))
_TN = (((0,), (0,)), ((), ()))
_BNN = (((2,), (1,)), ((0,), (0,)))
_BNT = (((2,), (2,)), ((0,), (0,)))
_BTN = (((1,), (1,)), ((0,), (0,)))
ATTN_GROUP = 4
ATTN_ITEMS = PAD_UNIT // BAND
Q_COL, K_COL, V_COL = 0, 4, 8


def _attn_item_rows(j, d, c, cbase):
    r = lax.rem(j, d)
    b = lax.div(j, d)
    loc = b * (d * BAND) + r
    first = jnp.logical_and(c == 0, b == 0)
    start = cbase + loc
    pstart = jnp.where(first, start, start - d * BAND)
    return loc, start, pstart, first


def _attn_fwd(proj, bias, xchg):
    s = proj.shape[0]

    def body(q_ref, k_ref, v_ref, b_ref, y_ref, lse_ref, o_s, l_s):
        c = pl.program_id(1)
        cbase = pl.multiple_of(c * PAD_UNIT, PAD_UNIT)
        m0, in_window, cur_half = _attn_masks()
        for bi, d in enumerate(DILATIONS):
            def group(jg, carry, bi=bi, d=d):
                locs, qs, ks, vs, pens = [], [], [], [], []
                for t in range(ATTN_GROUP):
                    loc, start, pstart, first = _attn_item_rows(jg * ATTN_GROUP + t, d, c, cbase)
                    locs.append(loc)
                    qs.append(q_ref[pl.ds(loc, BAND, stride=d), :])
                    ks.append(jnp.concatenate([k_ref[pl.ds(pstart, BAND, stride=d), :],
                                               k_ref[pl.ds(start, BAND, stride=d), :]], axis=0))
                    vs.append(jnp.concatenate([v_ref[pl.ds(pstart, BAND, stride=d), :],
                                               v_ref[pl.ds(start, BAND, stride=d), :]], axis=0))
                    pens.append(jnp.where(cur_half, 0.0, jnp.where(first, NEG_INF, 0.0)))
                q = jnp.stack(qs)
                kk = jnp.stack(ks + ks).astype(bf16)
                vv = jnp.stack(vs + vs).astype(bf16)
                pen = jnp.stack(pens + pens)
                qh = (jnp.concatenate([jnp.where(m0, q, 0.0), jnp.where(m0, 0.0, q)], axis=0) * 0.125).astype(bf16)
                sc = lax.dot_general(qh, kk, _BNT, preferred_element_type=f32)
                sc = (sc.reshape(2, ATTN_GROUP, BAND, 2 * BAND) + b_ref[bi][:, None]).reshape(sc.shape) + pen
                mx = jnp.max(sc, axis=-1, keepdims=True)
                e = jnp.exp(sc - mx)
                l = jnp.sum(e, axis=-1, keepdims=True)
                o = lax.dot_general(e.astype(bf16), vv, _BNN, preferred_element_type=f32) * (1.0 / l)
                ls = mx + jnp.log(l)
                for t in range(ATTN_GROUP):
                    rows = pl.ds(locs[t], BAND, stride=d)
                    o_s[bi, rows, :] = jnp.where(m0, o[t], o[ATTN_GROUP + t])
                    l_s[bi, rows, :] = jnp.where(m0, ls[t], ls[ATTN_GROUP + t])
                return carry

            lax.fori_loop(0, ATTN_ITEMS // ATTN_GROUP, group, 0)

        def merge(t, carry):
            rows = pl.ds(pl.multiple_of(t * 256, 256), 256)
            ls = [l_s[i, rows, :] for i in range(3)]
            mx = jnp.maximum(jnp.maximum(ls[0], ls[1]), ls[2])
            ws = [jnp.exp(l - mx) for l in ls]
            tot = ws[0] + ws[1] + ws[2]
            y = (ws[0] * o_s[0, rows, :] + ws[1] * o_s[1, rows, :] + ws[2] * o_s[2, rows, :]) / tot
            y_ref[rows, :] = y
            lse_ref[rows, :] = mx + jnp.log(tot)
            return carry

        lax.fori_loop(0, PAD_UNIT // 256, merge, 0)

    chunk = lambda col: pl.BlockSpec((PAD_UNIT, 128), lambda p, c: (c, col + p))
    full = lambda col: pl.BlockSpec((s, 128), lambda p, c: (0, col + p))
    grid = (N_HEADS // 2, s // PAD_UNIT)
    out = pl.pallas_call(
        _ride(body, 4, 2, xchg, grid), name="attn_fwd", grid=grid,
        in_specs=[chunk(Q_COL), full(K_COL), full(V_COL),
                  pl.BlockSpec((3, 2, BAND, 2 * BAND), lambda p, c: (0, p, 0, 0))] + [_ANY] * xchg.n,
        out_specs=[chunk(0), chunk(0)] + [_ANY] * xchg.n,
        out_shape=[jax.ShapeDtypeStruct((s, GROUP_W), f32)] * 2 + xchg.out_shape(),
        scratch_shapes=[pltpu.VMEM((3, PAD_UNIT, 128), f32)] * 2 + xchg.scratch(),
        compiler_params=_cparams(("arbitrary", "arbitrary")),
    )(proj, proj, proj, bias, *xchg.arrs)
    return out[:2], out[2:]


def _attn_bwd(proj, bias, y, lse, dycat):
    s = proj.shape[0]

    def body(q_ref, k_ref, v_ref, b_ref, y_ref, lse_ref, dy_ref, dq_ref, dk_ref, dv_ref, db_ref, dd_s):
        c = pl.program_id(1)
        cbase = pl.multiple_of(c * PAD_UNIT, PAD_UNIT)
        m0, in_window, cur_half = _attn_masks()

        @pl.when(c == 0)
        def _():
            dk_ref[...] = jnp.zeros_like(dk_ref)
            dv_ref[...] = jnp.zeros_like(dv_ref)
            db_ref[...] = jnp.zeros_like(db_ref)

        dq_ref[...] = jnp.zeros_like(dq_ref)

        def rowdot(t, carry):
            rows = pl.ds(pl.multiple_of(t * 256, 256), 256)
            prod = dy_ref[rows, :] * y_ref[rows, :]
            lane = lax.broadcasted_iota(jnp.int32, prod.shape, 1)
            h0 = lane < HEAD_DIM
            d0 = jnp.sum(jnp.where(h0, prod, 0.0), axis=-1, keepdims=True)
            d1 = jnp.sum(jnp.where(h0, 0.0, prod), axis=-1, keepdims=True)
            dd_s[rows, :] = jnp.where(h0, d0, d1)
            return carry

        lax.fori_loop(0, PAD_UNIT // 256, rowdot, 0)

        for bi, d in enumerate(DILATIONS):
            def group(jg, carry, bi=bi, d=d):
                ng = ATTN_GROUP
                meta, qs, dos, lqs, dds, ks, vs, pens = [], [], [], [], [], [], [], []
                for t in range(ng):
                    loc, start, pstart, first = _attn_item_rows(jg * ng + t, d, c, cbase)
                    qrows = pl.ds(loc, BAND, stride=d)
                    rows = pl.ds(start, BAND, stride=d)
                    prows = pl.ds(pstart, BAND, stride=d)
                    meta.append((qrows, rows, prows))
                    qs.append(q_ref[qrows, :])
                    dos.append(dy_ref[qrows, :])
                    lqs.append(lse_ref[qrows, :])
                    dds.append(dd_s[qrows, :])
                    ks.append(jnp.concatenate([k_ref[prows, :], k_ref[rows, :]], axis=0))
                    vs.append(jnp.concatenate([v_ref[prows, :], v_ref[rows, :]], axis=0))
                    pens.append(jnp.where(cur_half, 0.0, jnp.where(first, NEG_INF, 0.0)))

                def heads(t):
                    return jnp.concatenate([jnp.where(m0, t, 0.0), jnp.where(m0, 0.0, t)], axis=0)

                def head_col(t):
                    return jnp.concatenate([t[:, :, 0:1], t[:, :, HEAD_DIM:HEAD_DIM + 1]], axis=0)

                qh = (heads(jnp.stack(qs)) * 0.125).astype(bf16)
                doh = heads(jnp.stack(dos)).astype(bf16)
                kk = jnp.stack(ks + ks).astype(bf16)
                vv = jnp.stack(vs + vs).astype(bf16)
                sc = lax.dot_general(qh, kk, _BNT, preferred_element_type=f32)
                sc = (sc.reshape(2, ng, BAND, 2 * BAND) + b_ref[bi][:, None]).reshape(sc.shape) + jnp.stack(pens + pens)
                p = jnp.exp(sc - head_col(jnp.stack(lqs)))
                dp = lax.dot_general(doh, vv, _BNT, preferred_element_type=f32)
                ds = p * (dp - head_col(jnp.stack(dds)))
                db_ref[bi] += jnp.sum(ds.reshape(2, ng, BAND, 2 * BAND), axis=1)
                dsb = ds.astype(bf16)
                dq = lax.dot_general(dsb, kk, _BNN, preferred_element_type=f32) * 0.125
                dk = lax.dot_general(dsb, qh, _BTN, preferred_element_type=f32)
                dv = lax.dot_general(p.astype(bf16), doh, _BTN, preferred_element_type=f32)
                for t in range(ng):
                    qrows, rows, prows = meta[t]
                    dq_ref[qrows, :] += jnp.where(m0, dq[t], dq[ng + t])
                    dkt = dk[t] + dk[ng + t]
                    dvt = dv[t] + dv[ng + t]
                    dk_ref[prows, :] += dkt[:BAND]
                    dk_ref[rows, :] += dkt[BAND:]
                    dv_ref[prows, :] += dvt[:BAND]
                    dv_ref[rows, :] += dvt[BAND:]
                return carry

            lax.fori_loop(0, ATTN_ITEMS // ATTN_GROUP, group, 0)

    chunk = lambda col: pl.BlockSpec((PAD_UNIT, 128), lambda p, c: (c, col + p))
    full = lambda col: pl.BlockSpec((s, 128), lambda p, c: (0, col + p))
    bias_spec = pl.BlockSpec((3, 2, BAND, 2 * BAND), lambda p, c: (0, p, 0, 0))
    return pl.pallas_call(
        body, name="attn_bwd", grid=(N_HEADS // 2, s // PAD_UNIT),
        in_specs=[chunk(Q_COL), full(K_COL), full(V_COL), bias_spec, chunk(0), chunk(0), chunk(0)],
        out_specs=[chunk(0), full(0), full(0), bias_spec],
        out_shape=[jax.ShapeDtypeStruct((s, GROUP_W), f32)] * 3
        + [jax.ShapeDtypeStruct((3, N_HEADS, BAND, 2 * BAND), f32)],
        scratch_shapes=[pltpu.VMEM((PAD_UNIT, 128), f32)],
        compiler_params=_cparams(("parallel", "arbitrary")),
    )(proj, proj, proj, bias, y, lse, dycat)


_HI = lax.Precision.HIGHEST
DELTA_COL = 1536
Z_COL = 3072
BA_BLOCK = 28
DELTA_ROWS = 1024


def _hdot(a, b):
    return jnp.dot(a, b, precision=_HI, preferred_element_type=f32)


_DIMS = dict(nn=(((2,), (1,)), ((0,), (0,))), nt=(((2,), (2,)), ((0,), (0,))), tn=(((1,), (1,)), ((0,), (0,))))


@functools.partial(jax.custom_vjp, nondiff_argnums=(2,))
def _mmx(a, b, mode):
    return lax.dot_general(a.astype(bf16), b.astype(bf16), _DIMS[mode], preferred_element_type=f32)


def _mmx_fwd(a, b, mode):
    return _mmx(a, b, mode), (a, b)


def _mmx_bwd(mode, res, g):
    a, b = res
    if mode == "nn":
        return _mmx(g, b, "nt"), _mmx(a, g, "tn")
    if mode == "nt":
        return _mmx(g, b, "nn"), _mmx(g, a, "tn")
    return _mmx(b, g, "nt"), _mmx(a, g, "nn")


_mmx.defvjp(_mmx_fwd, _mmx_bwd)


def _pair_iota():
    row = lax.broadcasted_iota(jnp.int32, (CHUNK, 128), 0)
    lane = lax.broadcasted_iota(jnp.int32, (CHUNK, 128), 1)
    return row, lane, lane & (CHUNK - 1)


def _bd(x):
    _, lane, _ = _pair_iota()
    m0 = lane < CHUNK
    return jnp.concatenate([jnp.where(m0, x, 0.0), jnp.where(m0, 0.0, x)], axis=1)


def _pmm(a, b):
    return _mmx(a, _bd(b), "nn")


def _ntp(x, y):
    return _mmx(x, _bd(y), "nt")


def _tnp(x, y):
    full = _mmx(x, y, "tn")
    _, lane, _ = _pair_iota()
    return jnp.where(lane < CHUNK, full[:, :CHUNK], full[:, CHUNK:])


def _tri_inv(a):
    row, lane, jj = _pair_iota()
    eye = jnp.where(row == jj, 1.0, 0.0).astype(f32)

    def same_block(log2b):
        return (row >> log2b) == (jj >> log2b)

    dgl = jnp.where(same_block(3), a, 0.0)
    d2 = _pmm(dgl, dgl)
    d4 = _pmm(d2, d2)
    t = _pmm(_pmm(eye - dgl, eye + d2), eye + d4)
    for lb in (3, 4, 5):
        off = jnp.where(same_block(lb + 1) & jnp.logical_not(same_block(lb)), a, 0.0)
        t = t - _pmm(_pmm(t, off), t)
    return t


@jax.custom_vjp
def _solve2(a, xv, xk, t):
    return _pmm(t, xv), _pmm(t, xk)


def _solve2_fwd(a, xv, xk, t):
    u, w = _pmm(t, xv), _pmm(t, xk)
    return (u, w), (t, u, w)


def _solve2_bwd(res, cts):
    t, u, w = res
    du, dw = cts
    dxv = _tnp(t, du)
    dxk = _tnp(t, dw)
    return -(_ntp(dxv, u) + _ntp(dxk, w)), dxv, dxk, jnp.zeros_like(t)


_solve2.defvjp(_solve2_fwd, _solve2_bwd)


def _chunk_pre(qp, kp, vp, bp, gcum, t=None):
    row, lane, jj = _pair_iota()
    causal = row >= jj
    strict = row > jj
    rsel = jnp.sum(jnp.where(row == jj, gcum, 0.0), axis=1, keepdims=True)
    decay = jnp.where(causal, jnp.exp(jnp.where(causal, gcum - rsel, 0.0)), 0.0)
    kb = kp * bp
    kd = _bd(kp)
    a = jnp.where(strict, _mmx(kb, kd, "nt") * decay, 0.0)
    eg = jnp.exp(gcum)
    if t is None:
        t = _tri_inv(a)
    u, w = _solve2(a, vp * bp, kb * eg, t)
    qk = jnp.where(causal, _mmx(qp, kd, "nt") * decay, 0.0)
    glast = jnp.sum(jnp.where(row == CHUNK - 1, gcum, 0.0), axis=1, keepdims=True)
    return u, w, qp * eg, kp * jnp.exp(glast - gcum), qk, jnp.exp(glast), t


def _chunk_post(u, w, qt, kh, qk, gam, sp):
    sd = _bd(sp)
    vnew = u - _mmx(w, sd, "nn")
    o = _mmx(qt, sd, "nn") + _pmm(qk, vnew)
    return o, gam * sp + _tnp(kh, vnew)


def _pair_spec(rows=DELTA_ROWS):
    return pl.BlockSpec((rows, 128), lambda i, p: (i, p))


DELTA_NB = DELTA_ROWS // CHUNK


def _chunks(ref):
    return ref[...].reshape(DELTA_NB, CHUNK, 128)


def _pairs(ref, rows):
    return jnp.stack([ref[rows, p * 128:(p + 1) * 128] for p in range(4)], axis=0)


def _delta_chunk_pre(qn, kn, sv, beta, g):
    s = qn.shape[0]

    def body(q_ref, k_ref, v_ref, b_ref, g_ref, u_ref, w_ref, qt_ref, kh_ref, qk_ref, t_ref, gm_ref):
        outs = _chunk_pre(_chunks(q_ref), _chunks(k_ref), _chunks(v_ref), _chunks(b_ref), _chunks(g_ref))
        for ref, val in zip((u_ref, w_ref, qt_ref, kh_ref, qk_ref, t_ref), outs[:5] + outs[6:]):
            ref[...] = val.reshape(DELTA_ROWS, 128).astype(ref.dtype)
        gm_ref[...] = jnp.broadcast_to(outs[5], (DELTA_NB, 8, 128)).reshape(DELTA_NB * 8, 128)

    v_spec = pl.BlockSpec((DELTA_ROWS, 128), lambda i, p: (i, 8 + p))
    return pl.pallas_call(
        body, name="delta_chunk_pre", grid=(s // DELTA_ROWS, 4),
        in_specs=[_pair_spec(), _pair_spec(), v_spec, _pair_spec(), _pair_spec()],
        out_specs=[_pair_spec()] * 6 + [_pair_spec(DELTA_NB * 8)],
        out_shape=[jax.ShapeDtypeStruct((s, GROUP_W), f32)] + [jax.ShapeDtypeStruct((s, GROUP_W), bf16)] * 5
        + [jax.ShapeDtypeStruct((s // 8, GROUP_W), f32)],
        compiler_params=_cparams(("parallel", "parallel")),
    )(qn, kn, sv, beta, g)


def _delta_scan_fwd(u, w, qt, kh, qk, gm):
    s = u.shape[0]

    def body(u_ref, w_ref, qt_ref, kh_ref, qk_ref, gm_ref, o_ref, ss_ref, st):
        @pl.when(pl.program_id(0) == 0)
        def _():
            st[...] = jnp.zeros_like(st)

        def chunk(ci, carry):
            rows = pl.ds(pl.multiple_of(ci * CHUNK, CHUNK), CHUNK)
            grow = pl.ds(pl.multiple_of(ci * 8, 8), 1)
            sp = st[...]
            o, s2 = _chunk_post(_pairs(u_ref, rows), _pairs(w_ref, rows), _pairs(qt_ref, rows),
                                _pairs(kh_ref, rows), _pairs(qk_ref, rows), _pairs(gm_ref, grow), sp)
            for p in range(4):
                ss_ref[rows, p * 128:(p + 1) * 128] = sp[p]
                o_ref[rows, p * 128:(p + 1) * 128] = o[p]
            st[...] = s2
            return carry

        lax.fori_loop(0, DELTA_NB, chunk, 0)

    spec = pl.BlockSpec((DELTA_ROWS, GROUP_W), lambda i: (i, 0))
    gspec = pl.BlockSpec((DELTA_NB * 8, GROUP_W), lambda i: (i, 0))
    return pl.pallas_call(
        body, name="delta_scan_fwd", grid=(s // DELTA_ROWS,),
        in_specs=[spec] * 5 + [gspec],
        out_specs=[spec, spec],
        out_shape=[jax.ShapeDtypeStruct((s, GROUP_W), f32)] * 2,
        scratch_shapes=[pltpu.VMEM((4, CHUNK, 128), f32)],
        compiler_params=_cparams(("arbitrary",)),
    )(u, w, qt, kh, qk, gm)


def _delta_scan_bwd(w, qt, kh, qk, gm, do):
    s = w.shape[0]
    nb = s // DELTA_ROWS

    def body(w_ref, qt_ref, kh_ref, qk_ref, gm_ref, do_ref, dso_ref, dst):
        @pl.when(pl.program_id(0) == 0)
        def _():
            dst[...] = jnp.zeros_like(dst)

        def chunk(t, carry):
            ci = DELTA_NB - 1 - t
            rows = pl.ds(pl.multiple_of(ci * CHUNK, CHUNK), CHUNK)
            grow = pl.ds(pl.multiple_of(ci * 8, 8), 1)
            ds = dst[...]
            for p in range(4):
                dso_ref[rows, p * 128:(p + 1) * 128] = ds[p]
            do = _pairs(do_ref, rows)
            dvn = _tnp(_pairs(qk_ref, rows), do) + _pmm(_pairs(kh_ref, rows), ds)
            dst[...] = _tnp(_pairs(qt_ref, rows), do) + _pairs(gm_ref, grow) * ds - _tnp(_pairs(w_ref, rows), dvn)
            return carry

        lax.fori_loop(0, DELTA_NB, chunk, 0)

    spec = pl.BlockSpec((DELTA_ROWS, GROUP_W), lambda i: (nb - 1 - i, 0))
    gspec = pl.BlockSpec((DELTA_NB * 8, GROUP_W), lambda i: (nb - 1 - i, 0))
    return pl.pallas_call(
        body, name="delta_scan_bwd", grid=(nb,),
        in_specs=[spec] * 4 + [gspec, spec],
        out_specs=spec,
        out_shape=jax.ShapeDtypeStruct((s, GROUP_W), f32),
        scratch_shapes=[pltpu.VMEM((4, CHUNK, 128), f32)],
        compiler_params=_cparams(("arbitrary",)),
    )(w, qt, kh, qk, gm, do)


def _delta_chunk_bwd(qn, kn, sv, beta, g, tinv, ss, dso, do):
    s = qn.shape[0]

    def body(q_ref, k_ref, v_ref, b_ref, g_ref, t_ref, ss_ref, dso_ref, do_ref,
             dq_ref, dk_ref, dv_ref, db_ref, dg_ref):
        sp = _chunks(ss_ref)
        t = _chunks(t_ref)

        def fn(q, k, v, b, gg):
            return _chunk_post(*_chunk_pre(q, k, v, b, gg, t)[:6], sp)

        _, vjp = jax.vjp(fn, _chunks(q_ref), _chunks(k_ref), _chunks(v_ref), _chunks(b_ref), _chunks(g_ref))
        grads = vjp((_chunks(do_ref), _chunks(dso_ref)))
        for ref, val in zip((dq_ref, dk_ref, dv_ref, db_ref, dg_ref), grads):
            ref[...] = val.reshape(DELTA_ROWS, 128)

    v_spec = pl.BlockSpec((DELTA_ROWS, 128), lambda i, p: (i, 8 + p))
    return pl.pallas_call(
        body, name="delta_chunk_bwd", grid=(s // DELTA_ROWS, 4),
        in_specs=[_pair_spec(), _pair_spec(), v_spec] + [_pair_spec()] * 6,
        out_specs=[_pair_spec()] * 5,
        out_shape=[jax.ShapeDtypeStruct((s, GROUP_W), f32)] * 5,
        compiler_params=_cparams(("parallel", "parallel")),
    )(qn, kn, sv, beta, g, tinv, ss, dso, do)


def _head_sums(x):
    r = lax.broadcasted_iota(jnp.int32, (128, 128), 0)
    c = lax.broadcasted_iota(jnp.int32, (128, 128), 1)
    pair = jnp.where((r >> 6) == (c >> 6), 1.0, 0.0).astype(f32)
    npair = x.shape[1] // 128
    xb = jnp.concatenate([x[None, :, p * 128:(p + 1) * 128] for p in range(npair)], axis=0)
    sums = _mmx(xb, jnp.broadcast_to(pair, (npair, 128, 128)), "nn")
    return jnp.concatenate([sums[p] for p in range(npair)], axis=1)


def _sel_dot(a, b):
    return jnp.dot(a, b, precision=lax.Precision.HIGH, preferred_element_type=f32)


def _expand_matrix(first):
    r = lax.broadcasted_iota(jnp.int32, (128, GROUP_W), 0)
    c = lax.broadcasted_iota(jnp.int32, (128, GROUP_W), 1) >> 6
    return jnp.where(r == c + first, 1.0, 0.0).astype(f32)


@functools.partial(jax.custom_vjp, nondiff_argnums=(1,))
def _expand_heads(ba, first):
    return _sel_dot(ba, _expand_matrix(first))


def _expand_heads_fwd(ba, first):
    return _expand_heads(ba, first), None


def _expand_heads_bwd(first, _, g):
    return (_mmx(g[None], _expand_matrix(first)[None], "nt")[0],)


_expand_heads.defvjp(_expand_heads_fwd, _expand_heads_bwd)


def _softplus(x):
    return jnp.maximum(x, 0.0) + jnp.log(1.0 + jnp.exp(-jnp.abs(x)))


def _prep_fn(sq, sk, ba, alog_e, dt_e):
    qn = sq * lax.rsqrt(_head_sums(sq * sq) + EPS) * (HEAD_DIM ** -0.5)
    kn = sk * lax.rsqrt(_head_sums(sk * sk) + EPS)
    bl = _expand_heads(ba, 0)
    al = _expand_heads(ba, N_HEADS)
    beta = jax.nn.sigmoid(bl)
    g = -jnp.exp(alog_e) * _softplus(al + dt_e)
    nchunk = g.shape[0] // CHUNK
    ri = lax.broadcasted_iota(jnp.int32, (nchunk, CHUNK, CHUNK), 1)
    ci = lax.broadcasted_iota(jnp.int32, (nchunk, CHUNK, CHUNK), 2)
    tril = jnp.where(ri >= ci, 1.0, 0.0).astype(f32)
    gcum = lax.dot_general(tril, g.reshape(nchunk, CHUNK, g.shape[1]), _BNN, precision=lax.Precision.HIGH,
                           preferred_element_type=f32)
    return qn, kn, beta, gcum.reshape(g.shape)


def _gnorm_fn(o, z, ng_e):
    ms = _head_sums(o * o) * (1.0 / HEAD_DIM)
    return o * lax.rsqrt(ms + EPS) * ng_e * (z * jax.nn.sigmoid(z))


def _tok_spec(width, col):
    return pl.BlockSpec((TOK_TILE, width), lambda i: (i, col))


def _conv_taps(xs_ref, w_ref, base, n, cols):
    acc = w_ref[CONV_WIDTH - 1:CONV_WIDTH, cols] * xs_ref[pl.ds(base, n), cols]
    for j in range(CONV_WIDTH - 1):
        acc = acc + w_ref[j:j + 1, cols] * xs_ref[pl.ds(base - (CONV_WIDTH - 1) + j, n), cols]
    return acc


def _conv_silu_fwd(proj, conv_w):
    s = proj.shape[0]
    wd = 3 * GROUP_W
    hb = TOK_TILE // 8

    def body(x_ref, halo_ref, w_ref, o_ref, xs):
        inner = pl.program_id(0) > 0

        def lane_block(cb, carry):
            cols = pl.ds(pl.multiple_of(cb * 128, 128), 128)
            xs[0:8, cols] = jnp.where(inner, halo_ref[:, cols], 0.0)
            xs[8:, cols] = x_ref[:, cols]
            y = _conv_taps(xs, w_ref, 8, TOK_TILE, cols)
            o_ref[:, cols] = y * jax.nn.sigmoid(y)
            return carry

        lax.fori_loop(0, wd // 128, lane_block, 0)

    return pl.pallas_call(
        body, name="delta_conv_fwd", grid=(s // TOK_TILE,),
        in_specs=[_tok_spec(wd, 1), pl.BlockSpec((8, wd), lambda i: (jnp.maximum(i * hb - 1, 0), 1)),
                  pl.BlockSpec((CONV_WIDTH, wd), lambda i: (0, 0))],
        out_specs=_tok_spec(wd, 0),
        out_shape=jax.ShapeDtypeStruct((s, wd), f32),
        scratch_shapes=[pltpu.VMEM((TOK_TILE + 8, wd), f32)],
        compiler_params=_cparams(("parallel",)),
    )(proj, proj, conv_w)


def _conv_silu_bwd(proj, conv_w, ds3, xchg):
    s = proj.shape[0]
    wd = 3 * GROUP_W
    hb = TOK_TILE // 8
    nt = s // TOK_TILE

    def body(x_ref, hp_ref, hn_ref, dq_ref, dk_ref, dv_ref, dqn_ref, dkn_ref, dvn_ref, w_ref, dx_ref, dw_ref, xs, dys):
        i = pl.program_id(0)

        @pl.when(i == 0)
        def _():
            dw_ref[...] = jnp.zeros_like(dw_ref)

        last = i == nt - 1
        def lane_block(lb, carry, third, cur, nxt):
            tcols = pl.ds(pl.multiple_of(lb * 128, 128), 128)
            cols = pl.ds(pl.multiple_of(third * GROUP_W + lb * 128, 128), 128)
            xs[0:8, cols] = jnp.where(i > 0, hp_ref[:, cols], 0.0)
            xs[8:8 + TOK_TILE, cols] = x_ref[:, cols]
            xs[8 + TOK_TILE:, cols] = jnp.where(last, 0.0, hn_ref[:, cols])
            y = _conv_taps(xs, w_ref, 8, TOK_TILE, cols)
            sg = jax.nn.sigmoid(y)
            dy0 = cur[:, tcols] * (sg * (1.0 + y * (1.0 - sg)))
            dys[0:TOK_TILE, cols] = dy0
            yn = _conv_taps(xs, w_ref, 8 + TOK_TILE, 8, cols)
            sgn = jax.nn.sigmoid(yn)
            dys[TOK_TILE:, cols] = jnp.where(last, 0.0, nxt[:, tcols]) * (sgn * (1.0 + yn * (1.0 - sgn)))
            dx = w_ref[CONV_WIDTH - 1:CONV_WIDTH, cols] * dy0
            for j in range(CONV_WIDTH - 1):
                dx = dx + w_ref[j:j + 1, cols] * dys[pl.ds(CONV_WIDTH - 1 - j, TOK_TILE), cols]
            dx_ref[:, cols] = dx.astype(dx_ref.dtype)
            for j in range(CONV_WIDTH):
                dw_ref[j:j + 1, cols] += jnp.sum(dy0 * xs[pl.ds(8 - (CONV_WIDTH - 1) + j, TOK_TILE), cols],
                                                 axis=0, keepdims=True)
            return carry

        for third, (cur, nxt) in enumerate(((dq_ref, dqn_ref), (dk_ref, dkn_ref), (dv_ref, dvn_ref))):
            lax.fori_loop(0, GROUP_W // 128, functools.partial(lane_block, third=third, cur=cur, nxt=nxt), 0)

    prev8 = lambda col: pl.BlockSpec((8, wd), lambda i: (jnp.maximum(i * hb - 1, 0), col))
    next8 = lambda col: pl.BlockSpec((8, wd), lambda i: (jnp.minimum((i + 1) * hb, s // 8 - 1), col))
    next8_third = pl.BlockSpec((8, GROUP_W), lambda i: (jnp.minimum((i + 1) * hb, s // 8 - 1), 0))
    out = pl.pallas_call(
        _ride(body, 10, 2, xchg, (nt,)), name="delta_conv_bwd", grid=(nt,),
        in_specs=[_tok_spec(wd, 1), prev8(1), next8(1)] + [_tok_spec(GROUP_W, 0)] * 3 + [next8_third] * 3
        + [pl.BlockSpec((CONV_WIDTH, wd), lambda i: (0, 0))] + [_ANY] * xchg.n,
        out_specs=[_tok_spec(wd, 0), pl.BlockSpec((CONV_WIDTH, wd), lambda i: (0, 0))] + [_ANY] * xchg.n,
        out_shape=[jax.ShapeDtypeStruct((s, wd), bf16), jax.ShapeDtypeStruct((CONV_WIDTH, wd), f32)] + xchg.out_shape(),
        scratch_shapes=[pltpu.VMEM((TOK_TILE + 16, wd), f32), pltpu.VMEM((TOK_TILE + 8, wd), f32)] + xchg.scratch(),
        compiler_params=_cparams(("arbitrary",)),
    )(proj, proj, proj, *ds3, *ds3, conv_w, *xchg.arrs)
    return out[:2], out[2:]


def _delta_prep_fwd(sconv, proj, alog_e, dt_e):
    s = sconv.shape[0]

    def body(sq_ref, sk_ref, ba_ref, al_ref, dt_ref, q_ref, k_ref, b_ref, g_ref):
        qn, kn, beta, g = _prep_fn(sq_ref[...], sk_ref[...], ba_ref[...], al_ref[...], dt_ref[...])
        q_ref[...] = qn
        k_ref[...] = kn
        b_ref[...] = beta
        g_ref[...] = g

    return pl.pallas_call(
        body, name="delta_prep_fwd", grid=(s // TOK_TILE,),
        in_specs=[_tok_spec(GROUP_W, 0), _tok_spec(GROUP_W, 1), _tok_spec(128, BA_BLOCK),
                  _vec_spec(GROUP_W), _vec_spec(GROUP_W)],
        out_specs=[_tok_spec(GROUP_W, 0)] * 4,
        out_shape=[jax.ShapeDtypeStruct((s, GROUP_W), f32)] * 4,
        compiler_params=_cparams(("parallel",)),
    )(sconv, sconv, proj, alog_e, dt_e)


def _delta_prep_bwd(sconv, proj, alog_e, dt_e, dqn, dkn, dbeta, dg):
    s = sconv.shape[0]

    def body(sq_ref, sk_ref, ba_ref, al_ref, dt_ref, dq_ref, dk_ref, db_ref, dg_ref,
             dsq_ref, dsk_ref, dba_ref, dal_ref, ddt_ref):
        @pl.when(pl.program_id(0) == 0)
        def _():
            dal_ref[...] = jnp.zeros_like(dal_ref)
            ddt_ref[...] = jnp.zeros_like(ddt_ref)

        _, vjp = jax.vjp(_prep_fn, sq_ref[...], sk_ref[...], ba_ref[...], al_ref[...], dt_ref[...])
        dsq, dsk, dba, dal, ddt = vjp((dq_ref[...], dk_ref[...], db_ref[...], dg_ref[...]))
        dsq_ref[...] = dsq
        dsk_ref[...] = dsk
        dba_ref[...] = dba.astype(bf16)
        dal_ref[...] += dal
        ddt_ref[...] += ddt

    return pl.pallas_call(
        body, name="delta_prep_bwd", grid=(s // TOK_TILE,),
        in_specs=[_tok_spec(GROUP_W, 0), _tok_spec(GROUP_W, 1), _tok_spec(128, BA_BLOCK),
                  _vec_spec(GROUP_W), _vec_spec(GROUP_W)] + [_tok_spec(GROUP_W, 0)] * 4,
        out_specs=[_tok_spec(GROUP_W, 0), _tok_spec(GROUP_W, 0), _tok_spec(128, 0),
                   _acc_spec(GROUP_W), _acc_spec(GROUP_W)],
        out_shape=[jax.ShapeDtypeStruct((s, GROUP_W), f32)] * 2 + [jax.ShapeDtypeStruct((s, 128), bf16)]
        + [jax.ShapeDtypeStruct((1, GROUP_W), f32)] * 2,
        compiler_params=_cparams(("arbitrary",)),
    )(sconv, sconv, proj, alog_e, dt_e, dqn, dkn, dbeta, dg)


def _gnorm_fwd(o, proj, ng_e):
    s = o.shape[0]

    def body(o_ref, z_ref, g_ref, y_ref):
        y_ref[...] = _gnorm_fn(o_ref[...], z_ref[...], g_ref[...])

    return pl.pallas_call(
        body, name="delta_gnorm_fwd", grid=(s // TOK_TILE,),
        in_specs=[_tok_spec(GROUP_W, 0), _tok_spec(GROUP_W, Z_COL // GROUP_W), _vec_spec(GROUP_W)],
        out_specs=_tok_spec(GROUP_W, 0),
        out_shape=jax.ShapeDtypeStruct((s, GROUP_W), f32),
        compiler_params=_cparams(("parallel",)),
    )(o, proj, ng_e)


def _gnorm_bwd(o, proj, ng_e, dycat):
    s = o.shape[0]

    def body(o_ref, z_ref, g_ref, dy_ref, do_ref, dz_ref, dg_ref):
        @pl.when(pl.program_id(0) == 0)
        def _():
            dg_ref[...] = jnp.zeros_like(dg_ref)

        _, vjp = jax.vjp(_gnorm_fn, o_ref[...], z_ref[...], g_ref[...])
        do, dz, dg = vjp(dy_ref[...])
        do_ref[...] = do
        dz_ref[...] = dz.astype(bf16)
        dg_ref[...] += dg

    return pl.pallas_call(
        body, name="delta_gnorm_bwd", grid=(s // TOK_TILE,),
        in_specs=[_tok_spec(GROUP_W, 0), _tok_spec(GROUP_W, Z_COL // GROUP_W), _vec_spec(GROUP_W),
                  _tok_spec(GROUP_W, 1)],
        out_specs=[_tok_spec(GROUP_W, 0), _tok_spec(GROUP_W, 0), _acc_spec(GROUP_W)],
        out_shape=[jax.ShapeDtypeStruct((s, GROUP_W), f32), jax.ShapeDtypeStruct((s, GROUP_W), bf16),
                   jax.ShapeDtypeStruct((1, GROUP_W), f32)],
        compiler_params=_cparams(("arbitrary",)),
    )(o, proj, ng_e, dycat)


_MESH = pl.DeviceIdType.MESH
_ANY = pl.BlockSpec(memory_space=pl.ANY)
_VMEM = pl.BlockSpec(memory_space=pltpu.VMEM)


def _my_place():
    x, y, c = lax.axis_index("x"), lax.axis_index("y"), lax.axis_index("c")
    return x, y, c, 4 * x + 2 * y + c


def _peer(k, x, y, c):
    px = 1 - x if k & 4 else x
    py = 1 - y if k & 2 else y
    pc = 1 - c if k & 1 else c
    return (px, py, pc), 4 * px + 2 * py + pc


def _exchange_all(src_of_peer, dst_ref, send_sems, recv_sems, x, y, c, me):
    sent = []
    for k in range(1, N_DEV):
        dev, pidx = _peer(k, x, y, c)
        cp = pltpu.make_async_remote_copy(src_ref=src_of_peer(pidx), dst_ref=dst_ref.at[me],
                                          send_sem=send_sems.at[k - 1], recv_sem=recv_sems.at[k - 1],
                                          device_id=dev, device_id_type=_MESH)
        cp.start()
        sent.append(cp)
    for k in range(1, N_DEV):
        dev, pidx = _peer(k, x, y, c)
        pltpu.make_async_remote_copy(src_ref=src_of_peer(pidx), dst_ref=dst_ref.at[pidx],
                                     send_sem=send_sems.at[k - 1], recv_sem=recv_sems.at[k - 1],
                                     device_id=dev, device_id_type=_MESH).wait_recv()
    for cp in sent:
        cp.wait_send()


def _ada_exchange(cv8, w_ada, b_ada8):
    def body(cv_ref, w_ref, b_ref, call_ref, modp_ref, part_s, s1, r1, s2, r2):
        x, y, c, me = _my_place()
        call_ref[me] = cv_ref[...]
        _exchange_all(lambda pidx: cv_ref, call_ref, s1, r1, x, y, c, me)
        bias = b_ref[me]
        for j in range(N_DEV):
            cj = call_ref[j][:, :D_MODEL]
            part_s[j] = _hdot(cj * jax.nn.sigmoid(cj), w_ref[...]) + bias
        modp_ref[me] = part_s[me]
        _exchange_all(lambda pidx: part_s.at[pidx], modp_ref, s2, r2, x, y, c, me)

    nsh = w_ada.shape[1]
    return pl.pallas_call(
        body, name="ada_exchange",
        in_specs=[_VMEM, _VMEM, _VMEM], out_specs=[_VMEM, _VMEM],
        out_shape=[jax.ShapeDtypeStruct((N_DEV, 8, cv8.shape[1]), f32), jax.ShapeDtypeStruct((N_DEV, 8, nsh), f32)],
        scratch_shapes=[pltpu.VMEM((N_DEV, 8, nsh), f32)] + [pltpu.SemaphoreType.DMA((N_DEV - 1,))] * 4,
        compiler_params=pltpu.CompilerParams(vmem_limit_bytes=VMEM_LIMIT),
    )(cv8, w_ada, b_ada8)


def _all_to_all(arrs, name):
    ex = _Exchange(arrs, gather=False)

    def body(*refs):
        srcs, dsts, sems = refs[:ex.n], refs[ex.n:2 * ex.n], refs[2 * ex.n:]
        ex.start(srcs, dsts, sems)
        ex.wait(srcs, dsts, sems)

    return pl.pallas_call(
        body, name=name,
        in_specs=[_ANY] * ex.n, out_specs=[_ANY] * ex.n,
        out_shape=ex.out_shape(), scratch_shapes=ex.scratch(),
    )(*arrs)


class _Exchange:
    def __init__(self, arrs, gather):
        self.arrs, self.gather, self.n = list(arrs), gather, len(arrs)

    def out_shape(self):
        return [jax.ShapeDtypeStruct(((N_DEV,) + a.shape) if self.gather else a.shape, a.dtype) for a in self.arrs]

    def scratch(self):
        if self.n == 0:
            return []
        return [pltpu.SemaphoreType.DMA((self.n, N_DEV - 1)), pltpu.SemaphoreType.DMA((self.n, N_DEV - 1)),
                pltpu.SemaphoreType.DMA((self.n,))]

    def _src(self, srcs, a, idx):
        return srcs[a] if self.gather else srcs[a].at[idx]

    def _copies(self, srcs, dsts, sems, incoming):
        send_sems, recv_sems, _ = sems
        x, y, c, me = _my_place()
        out = []
        for a in range(self.n):
            for k in range(1, N_DEV):
                dev, pidx = _peer(k, x, y, c)
                out.append(pltpu.make_async_remote_copy(
                    src_ref=self._src(srcs, a, pidx), dst_ref=dsts[a].at[pidx if incoming else me],
                    send_sem=send_sems.at[a, k - 1], recv_sem=recv_sems.at[a, k - 1],
                    device_id=dev, device_id_type=_MESH))
        return out

    def _local(self, srcs, dsts, sems):
        me = _my_place()[3]
        return [pltpu.make_async_copy(self._src(srcs, a, me), dsts[a].at[me], sems[2].at[a]) for a in range(self.n)]

    def start(self, srcs, dsts, sems):
        for cp in self._local(srcs, dsts, sems) + self._copies(srcs, dsts, sems, incoming=False):
            cp.start()

    def wait(self, srcs, dsts, sems):
        for cp in self._copies(srcs, dsts, sems, incoming=True):
            cp.wait_recv()
        for cp in self._copies(srcs, dsts, sems, incoming=False):
            cp.wait_send()
        for cp in self._local(srcs, dsts, sems):
            cp.wait()

    def start_at_first_step(self, grid, srcs, dsts, sems):
        first = functools.reduce(jnp.logical_and, [pl.program_id(i) == 0 for i in range(len(grid))])
        pl.when(first)(lambda: self.start(srcs, dsts, sems))

    def wait_at_last_step(self, grid, srcs, dsts, sems):
        last = functools.reduce(jnp.logical_and, [pl.program_id(i) == g - 1 for i, g in enumerate(grid)])
        pl.when(last)(lambda: self.wait(srcs, dsts, sems))


class _ChipGather:
    def __init__(self, shard):
        self.shard = shard

    def out_shape(self):
        return jax.ShapeDtypeStruct((N_DEV,) + self.shard.shape, self.shard.dtype)

    def scratch(self):
        return [pltpu.SemaphoreType.DMA((N_DEV - 1,)), pltpu.SemaphoreType.DMA((N_DEV - 1,)),
                pltpu.SemaphoreType.DMA(())]

    def _place(self):
        x, y, c, me = _my_place()
        return x, y, c, me, (x, y, 1 - c), [(1 - x, y), (x, 1 - y), (1 - x, 1 - y)]

    def _copy(self, out, sems, k, block, to, src=None):
        rows = out.at[4 * block[0] + 2 * block[1] + block[2]]
        return pltpu.make_async_remote_copy(src_ref=rows if src is None else src, dst_ref=rows,
                                            send_sem=sems[0].at[k], recv_sem=sems[1].at[k],
                                            device_id=to, device_id_type=_MESH)

    def start(self, src, out, sems):
        x, y, c, me, sib, chips = self._place()
        pltpu.make_async_copy(src, out.at[me], sems[2]).start()
        self._copy(out, sems, 0, (x, y, c), sib, src=src).start()
        for j, chip in enumerate(chips):
            self._copy(out, sems, 1 + j, (x, y, c), (*chip, c), src=src).start()

    def forward(self, src, out, sems):
        x, y, c, me, sib, chips = self._place()
        for j, chip in enumerate(chips):
            self._copy(out, sems, 1 + j, (*chip, c), (x, y, c)).wait_recv()
            self._copy(out, sems, 4 + j, (*chip, c), sib).start()

    def finish(self, src, out, sems):
        x, y, c, me, sib, chips = self._place()
        self._copy(out, sems, 0, (x, y, 1 - c), (x, y, c)).wait_recv()
        for j, chip in enumerate(chips):
            self._copy(out, sems, 4 + j, (*chip, 1 - c), (x, y, c)).wait_recv()
        self._copy(out, sems, 0, (x, y, c), sib, src=src).wait_send()
        for j, chip in enumerate(chips):
            self._copy(out, sems, 1 + j, (x, y, c), (*chip, c), src=src).wait_send()
            self._copy(out, sems, 4 + j, (*chip, c), sib).wait_send()
        pltpu.make_async_copy(src, out.at[me], sems[2]).wait()


def _ride(body, n_in, n_out, xchg, grid):
    nx = xchg.n
    if nx == 0:
        return body

    def wrapped(*refs):
        ins, xs = refs[:n_in], refs[n_in:n_in + nx]
        outs, xd = refs[n_in + nx:n_in + nx + n_out], refs[n_in + nx + n_out:n_in + 2 * nx + n_out]
        scratch = refs[n_in + 2 * nx + n_out:]
        xchg.start_at_first_step(grid, xs, xd, scratch[-3:])
        body(*ins, *outs, *scratch[:-3])
        xchg.wait_at_last_step(grid, xs, xd, scratch[-3:])

    return wrapped


def _adamw_math(w, g, m, v):
    m2 = ADAM_B1 * m + (1.0 - ADAM_B1) * g
    v2 = ADAM_B2 * v + (1.0 - ADAM_B2) * (g * g)
    m_hat = m2 / (1.0 - ADAM_B1 ** ADAM_STEP)
    v_hat = v2 / (1.0 - ADAM_B2 ** ADAM_STEP)
    delta = -ADAM_LR * (m_hat / (jnp.sqrt(v_hat) + ADAM_EPS) + ADAM_WD * w)
    return delta, m2, v2


def _row_tile(rows):
    for t in (256, 128, 64, 32, 16, 8):
        if rows % t == 0:
            return t
    return rows


def _reduce_adamw(parts, w, m, v, name):
    _, r, cdim = parts.shape
    tr = _row_tile(r)

    def body(p_ref, w_ref, m_ref, v_ref, g_ref, d_ref, m2_ref, v2_ref):
        g = p_ref[0].astype(f32)
        for j in range(1, N_DEV):
            g = g + p_ref[j].astype(f32)
        delta, m2, v2 = _adamw_math(w_ref[...], g, m_ref[...], v_ref[...])
        g_ref[...] = g
        d_ref[...] = delta
        m2_ref[...] = m2
        v2_ref[...] = v2

    spec = pl.BlockSpec((tr, cdim), lambda i: (i, 0))
    return pl.pallas_call(
        body, name=name, grid=(r // tr,),
        in_specs=[pl.BlockSpec((N_DEV, tr, cdim), lambda i: (0, i, 0)), spec, spec, spec],
        out_specs=[spec] * 4,
        out_shape=[jax.ShapeDtypeStruct((r, cdim), f32)] * 4,
        compiler_params=_cparams(("parallel",)),
    )(parts, w, m, v)


def _adamw(w, g, m, v, name):
    r, cdim = w.shape
    tr = _row_tile(r)

    def body(w_ref, g_ref, m_ref, v_ref, d_ref, m2_ref, v2_ref):
        delta, m2, v2 = _adamw_math(w_ref[...], g_ref[...], m_ref[...], v_ref[...])
        d_ref[...] = delta
        m2_ref[...] = m2
        v2_ref[...] = v2

    spec = pl.BlockSpec((tr, cdim), lambda i: (i, 0))
    return pl.pallas_call(
        body, name=name, grid=(r // tr,),
        in_specs=[spec] * 4, out_specs=[spec] * 3,
        out_shape=[jax.ShapeDtypeStruct((r, cdim), f32)] * 3,
        compiler_params=_cparams(("parallel",)),
    )(w, g, m, v)


def _sum_devices(parts, name):
    _, r, cdim = parts.shape

    def body(p_ref, o_ref):
        g = p_ref[0]
        for j in range(1, N_DEV):
            g = g + p_ref[j]
        o_ref[...] = g

    return pl.pallas_call(
        body, name=name, out_shape=jax.ShapeDtypeStruct((r, cdim), f32),
        in_specs=[_VMEM], out_specs=_VMEM,
    )(parts)


def _ada_wgrad(c_all8, dmod_cols):
    nsh = dmod_cols.shape[1]

    def body(c_ref, d_ref, o_ref):
        cv = c_ref[...]
        o_ref[...] = lax.dot_general(cv * jax.nn.sigmoid(cv), d_ref[...], _TN, precision=_HI,
                                     preferred_element_type=f32)

    return pl.pallas_call(
        body, name="ada_wgrad", out_shape=jax.ShapeDtypeStruct((D_MODEL, nsh), f32),
        in_specs=[_VMEM, _VMEM], out_specs=_VMEM,
        compiler_params=pltpu.CompilerParams(vmem_limit_bytes=VMEM_LIMIT),
    )(c_all8, dmod_cols)


def _cols(t):
    return t.transpose(1, 0, 2).reshape(t.shape[1], N_DEV * t.shape[2])


def _col_blocks(t, n):
    return t.reshape(t.shape[0], N_DEV, n).transpose(1, 0, 2).astype(bf16)


def _row_blocks(t):
    return t.reshape(N_DEV, t.shape[0] // N_DEV, t.shape[1]).astype(bf16)


def _local_step(x, tgt, mod, norm_attn_g, w_in_sh, rel_bias, conv_full, a_log, dt_bias, delta_norm_g,
                norm_ffn_g, final_norm_g, w_out_sh, w_gate_sh, w_up_sh, w_down_sh):
    s = x.shape[0]
    sh1, sc1, g1, sh2, sc2, g2 = [mod[:, i * D_MODEL:(i + 1) * D_MODEL] for i in range(6)]
    nag = norm_attn_g.reshape(1, D_MODEL)
    nfg = norm_ffn_g.reshape(1, D_MODEL)
    fg = final_norm_g.reshape(1, D_MODEL)
    idx = _bucket_tables()
    bias = _bias_tables(rel_bias, idx)
    alog_e = jnp.repeat(a_log.reshape(N_HEADS), HEAD_DIM)[None]
    dt_e = jnp.repeat(dt_bias.reshape(N_HEADS), HEAD_DIM)[None]
    ng_e = jnp.tile(delta_norm_g.reshape(HEAD_DIM), N_HEADS)[None]

    h1, w_in_g = _ln_mod_fwd(x, nag, sc1, sh1, w_in_sh, "ln1_fwd")
    w_in_p = jnp.pad(_cols(w_in_g), ((0, 0), (0, IN_PAD - IN_WIDTH)))
    proj, (w_out_g, w_gate_g) = _mm(h1, w_in_p, "nn", f32, 512, IN_PAD, 1024, "in_proj",
                                    xchg=_Exchange([w_out_sh, w_gate_sh], gather=True))
    (y_attn, lse), (w_up_g, w_down_g) = _attn_fwd(proj, bias, _Exchange([w_up_sh, w_down_sh], gather=True))
    w_out_b = w_out_g.reshape(2 * GROUP_W, D_MODEL)
    w_gate_b, w_up_b = _cols(w_gate_g), _cols(w_up_g)
    w_down_b = w_down_g.reshape(D_FF, D_MODEL)
    n_ff = w_gate_sh.shape[1]
    sconv = _conv_silu_fwd(proj, conv_full)
    qn, kn, beta, g = _delta_prep_fwd(sconv, proj, alog_e, dt_e)
    u, w, qt, kh, qk, tinv, gm = _delta_chunk_pre(qn, kn, sconv, beta, g)
    o, ss = _delta_scan_fwd(u, w, qt, kh, qk, gm)
    y_delta = _gnorm_fwd(o, proj, ng_e)
    y, x1, h2 = _proj_resid_ln_mod_fwd([(y_attn, w_out_b[:GROUP_W]), (y_delta, w_out_b[GROUP_W:])],
                                       x, g1, nfg, sc2, sh2, "out_proj_ln2")
    act, gate, up = _ffn_up(h2, w_gate_b, w_up_b, "ffn_up")
    dx2, dy2, loss, dfg, dg2 = _proj_final_loss_bwd(act, w_down_b, x1, g2, fg, tgt, "ffn_down_loss")

    dgate, dup = _ffn_down_dx(dy2, w_down_b, gate, up, "ffn_down_dx")
    g_down = _mm(act, dy2, "tn", f32, 1408, 1024, 1024, "ffn_down_dw")
    (dx1, dsh2, dsc2, dnfg, dy, dg1), (r_down,) = _proj_ln_mod_bwd(
        [(dgate, w_gate_b), (dup, w_up_b)], x1, nfg, sc2, dx2, 256, "ffn_up_dx_ln2",
        _Exchange([_row_blocks(g_down)], gather=False), gate=g1, y=y)
    g_gate = _mm(h2, dgate, "tn", f32, 1024, 1408, 1024, "ffn_gate_dw")
    g_up = _mm(h2, dup, "tn", f32, 1024, 1408, 1024, "ffn_up_dw")
    dycat = _mm(dy, w_out_b, "nt", f32, 512, 1024, 1024, "out_proj_dx")
    g_out = jnp.concatenate([_mm(y_attn, dy, "tn", f32, GROUP_W, 1024, 1024, "out_proj_dw_attn"),
                             _mm(y_delta, dy, "tn", f32, GROUP_W, 1024, 1024, "out_proj_dw_delta")], axis=0)
    dq, dk, dv, dbias = _attn_bwd(proj, bias, y_attn, lse, dycat)
    g_rb = _bias_grad(dbias, idx)[:, :, 0].T
    do, dz, dng = _gnorm_bwd(o, proj, ng_e, dycat)
    dso = _delta_scan_bwd(w, qt, kh, qk, gm, do)
    dqn, dkn, dvd, dbeta, dgd = _delta_chunk_bwd(qn, kn, sconv, beta, g, tinv, ss, dso, do)
    dsq, dsk, dba, dal, ddt = _delta_prep_bwd(sconv, proj, alog_e, dt_e, dqn, dkn, dbeta, dgd)
    (dxc, g_conv), (r_gate, r_up, r_out) = _conv_silu_bwd(
        proj, conv_full, (dsq, dsk, dvd),
        _Exchange([_col_blocks(g_gate, n_ff), _col_blocks(g_up, n_ff), _row_blocks(g_out)],
                  gather=False))
    pieces = ((dq, 0), (dk, GROUP_W), (dv, 2 * GROUP_W), (dxc, DELTA_COL), (dz, Z_COL), (dba, BA_BLOCK * 128))
    g_in = jnp.concatenate(
        [_mm(h1, p, "tn", f32, 1024, min(p.shape[1], 768), 1024, "in_proj_dw_%d" % c) for p, c in pieces], axis=1)
    (gx, dsh1, dsc1, dnag), (r_in,) = _proj_ln_mod_bwd(
        [(p, w_in_p[:, c:c + p.shape[1]]) for p, c in pieces], x, nag, sc1, dx1, TOK_TILE, "in_proj_dx_ln1",
        _Exchange([_col_blocks(g_in[:, :IN_WIDTH], IN_WIDTH // N_DEV)], gather=False))
    grads = dict(
        x=gx, mod=jnp.concatenate([dsh1, dsc1, dg1, dsh2, dsc2, dg2], axis=1),
        norm_attn_g=dnag, norm_ffn_g=dnfg, final_norm_g=dfg, rel_bias=g_rb, conv_w=g_conv,
        a_log=dal.reshape(N_HEADS, HEAD_DIM).sum(-1), dt_bias=ddt.reshape(N_HEADS, HEAD_DIM).sum(-1),
        delta_norm_g=dng.reshape(N_HEADS, HEAD_DIM).sum(0),
        w_in=r_in, w_out=r_out, w_gate=r_gate, w_up=r_up, w_down=r_down)
    return loss[0, 0], grads


def _misc_row(rel_bias, a_log, dt_bias, delta_norm_g):
    flat = jnp.concatenate([rel_bias.reshape(-1), a_log.reshape(-1), dt_bias.reshape(-1), delta_norm_g.reshape(-1)])
    return jnp.pad(flat, (0, D_MODEL - flat.shape[0]))[None]


def _pack_small(b_ada, nag, nfg, fng, rel_bias, a_log, dt_bias, dng, conv_shard):
    rows = [b_ada.reshape(6, D_MODEL), nag.reshape(1, D_MODEL), nfg.reshape(1, D_MODEL), fng.reshape(1, D_MODEL),
            _misc_row(rel_bias, a_log, dt_bias, dng),
            jnp.pad(conv_shard.reshape(-1), (0, D_MODEL - conv_shard.size))[None],
            jnp.zeros((5, D_MODEL), f32)]
    return jnp.concatenate(rows, axis=0)


def _unpack_small(p, conv_shape):
    misc = p[9]
    return dict(
        b_ada=p[0:6].reshape(1, 6 * D_MODEL), norm_attn_g=p[6:7], norm_ffn_g=p[7:8], final_norm_g=p[8],
        rel_bias=misc[0:256].reshape(N_BUCKETS, N_HEADS), a_log=misc[256:264].reshape(1, N_HEADS),
        dt_bias=misc[264:272].reshape(1, N_HEADS), delta_norm_g=misc[272:336].reshape(1, HEAD_DIM),
        conv_w=p[10, :conv_shape[1] * conv_shape[2]].reshape(conv_shape))


def kernel(x, c, w_ada, b_ada, norm_attn_g, w_in, rel_bias, conv_w, a_log, dt_bias, delta_norm_g, w_out, norm_ffn_g, w_gate, w_up, w_down, final_norm_g, loss_target, m_w_ada, m_b_ada, m_norm_attn_g, m_w_in, m_rel_bias, m_conv_w, m_a_log, m_dt_bias, m_delta_norm_g, m_w_out, m_norm_ffn_g, m_w_gate, m_w_up, m_w_down, m_final_norm_g, v_w_ada, v_b_ada, v_norm_attn_g, v_w_in, v_rel_bias, v_conv_w, v_a_log, v_dt_bias, v_delta_norm_g, v_w_out, v_norm_ffn_g, v_w_gate, v_w_up, v_w_down, v_final_norm_g):
    me = 4 * lax.axis_index("x") + 2 * lax.axis_index("y") + lax.axis_index("c")
    ada_sh = w_ada.shape[2]
    conv_sh = conv_w.shape[2]

    cv = jnp.concatenate([c[0], conv_w[0].reshape(-1)])
    cv8 = jnp.zeros((8, 2 * D_MODEL), f32).at[0, :cv.shape[0]].set(cv)
    b8 = jnp.broadcast_to(b_ada.reshape(N_DEV, 1, ada_sh), (N_DEV, 8, ada_sh))
    call, modp = _ada_exchange(cv8, w_ada[0], b8)
    mod = modp[:, 0, :].reshape(1, 6 * D_MODEL)
    c_all = call[:, 0, :D_MODEL]
    conv_full = call[:, 0, D_MODEL:D_MODEL + CONV_WIDTH * conv_sh].reshape(N_DEV, CONV_WIDTH, conv_sh)
    conv_full = conv_full.transpose(1, 0, 2).reshape(CONV_WIDTH, N_DEV * conv_sh)

    loss_local, gr = _local_step(x[0], loss_target[0], mod, norm_attn_g, w_in[0].astype(bf16), rel_bias, conv_full, a_log,
                                 dt_bias, delta_norm_g, norm_ffn_g, final_norm_g, w_out[0].astype(bf16),
                                 w_gate[0].astype(bf16), w_up[0].astype(bf16), w_down[0].astype(bf16))
    loss = lax.psum(loss_local, ("x", "y", "c"))

    small = jnp.concatenate([
        gr["mod"].reshape(6, D_MODEL), gr["norm_attn_g"], gr["norm_ffn_g"], gr["final_norm_g"],
        gr["conv_w"].reshape(6, D_MODEL),
        _misc_row(gr["rel_bias"], gr["a_log"], gr["dt_bias"], gr["delta_norm_g"])], axis=0)
    parts = _all_to_all([jnp.broadcast_to(small[None], (N_DEV,) + small.shape)], "small_gather")[0]
    tot = _sum_devices(parts, "small_sum")
    g_conv_full = tot[9:15].reshape(CONV_WIDTH, N_DEV * conv_sh)
    g_conv = lax.dynamic_slice(g_conv_full, (0, me * conv_sh), (CONV_WIDTH, conv_sh))
    misc = tot[15]
    g_small = _pack_small(tot[0:6], tot[6], tot[7], tot[8], misc[0:256], misc[256:264], misc[264:272],
                          misc[272:336], g_conv)
    pk = lambda pre: _pack_small(pre[0], pre[1], pre[2], pre[3], pre[4], pre[5], pre[6], pre[7], pre[8])
    w_small = pk((b_ada, norm_attn_g, norm_ffn_g, final_norm_g, rel_bias, a_log, dt_bias, delta_norm_g, conv_w))
    m_small = pk((m_b_ada, m_norm_attn_g, m_norm_ffn_g, m_final_norm_g, m_rel_bias, m_a_log, m_dt_bias,
                  m_delta_norm_g, m_conv_w))
    v_small = pk((v_b_ada, v_norm_attn_g, v_norm_ffn_g, v_final_norm_g, v_rel_bias, v_a_log, v_dt_bias,
                  v_delta_norm_g, v_conv_w))
    d_small, m2_small, v2_small = _adamw(w_small, g_small, m_small, v_small, "adamw_small")
    cshape = conv_w.shape
    G, Dl, M2, V2 = (_unpack_small(t, cshape) for t in (g_small, d_small, m2_small, v2_small))

    dmod_all = parts[:, 0:6, :].reshape(N_DEV, 6 * D_MODEL)
    dmod_cols = lax.dynamic_slice(dmod_all, (0, me * ada_sh), (N_DEV, ada_sh))
    g_ada = _ada_wgrad(c_all, dmod_cols)
    d_ada, m2_ada, v2_ada = _adamw(w_ada[0], g_ada, m_w_ada[0], v_w_ada[0], "adamw_w_ada")

    big = {}
    for name, w_, m_, v_ in (("w_in", w_in, m_w_in, v_w_in), ("w_out", w_out, m_w_out, v_w_out),
                             ("w_gate", w_gate, m_w_gate, v_w_gate), ("w_up", w_up, m_w_up, v_w_up),
                             ("w_down", w_down, m_w_down, v_w_down)):
        big[name] = [t[None] for t in _reduce_adamw(gr[name], w_[0], m_[0], v_[0], "reduce_adamw_" + name)]

    def leaf(i, name):
        if name == "w_ada":
            return (g_ada, d_ada, m2_ada, v2_ada)[i][None]
        if name in big:
            return big[name][i]
        return (G, Dl, M2, V2)[i][name]

    order = ["w_ada", "b_ada", "norm_attn_g", "w_in", "rel_bias", "conv_w", "a_log", "dt_bias", "delta_norm_g",
             "w_out", "norm_ffn_g", "w_gate", "w_up", "w_down", "final_norm_g"]
    outs = [loss, gr["x"][None]]
    for i in range(4):
        outs += [leaf(i, n) for n in order]
    return tuple(outs)
```

```python
import functools
import math

import jax
import jax.numpy as jnp
from jax import lax
from jax.experimental import pallas as pl
from jax.experimental.pallas import tpu as pltpu

f32 = jnp.float32
bf16 = jnp.bfloat16

D_MODEL = 1024
HEAD_DIM = 64
N_HEADS = 8
GROUP_W = 512
IN_WIDTH = 3600
IN_PAD = 3840
D_FF = 2816
EPS = 1e-6
NEG_INF = -1e30
BAND = 128
PAD_UNIT = 2048
DILATIONS = (1, 4, 16)
N_BUCKETS = 32
MAX_DISTANCE = 2048
CONV_WIDTH = 4
CHUNK = 64
N_DEV = 8
VMEM_LIMIT = 56 * 1024 * 1024

ADAM_LR, ADAM_B1, ADAM_B2, ADAM_EPS, ADAM_WD, ADAM_STEP = 0.001, 0.9, 0.999, 1e-08, 0.01, 10


def _cparams(sem):
    return pltpu.CompilerParams(dimension_semantics=sem, vmem_limit_bytes=VMEM_LIMIT)


def _mm(a, b, mode, out_dtype, tm, tn, tk, name, xchg=None):
    if mode == "nn":
        (m, k), (_, n) = a.shape, b.shape
        a_spec = pl.BlockSpec((tm, tk), lambda j, i, kk: (i, kk))
        b_spec = pl.BlockSpec((tk, tn), lambda j, i, kk: (kk, j))
        dims = (((1,), (0,)), ((), ()))
    elif mode == "nt":
        (m, k), (n, _) = a.shape, b.shape
        a_spec = pl.BlockSpec((tm, tk), lambda j, i, kk: (i, kk))
        b_spec = pl.BlockSpec((tn, tk), lambda j, i, kk: (j, kk))
        dims = (((1,), (1,)), ((), ()))
    else:
        (k, m), (_, n) = a.shape, b.shape
        a_spec = pl.BlockSpec((tk, tm), lambda j, i, kk: (kk, i))
        b_spec = pl.BlockSpec((tk, tn), lambda j, i, kk: (kk, j))
        dims = (((0,), (0,)), ((), ()))
    assert m % tm == 0 and n % tn == 0 and k % tk == 0, (name, m, n, k, tm, tn, tk)
    nk = k // tk
    grid = (n // tn, m // tm, nk)
    nx = xchg.n if xchg is not None else 0

    def body(*refs):
        a_ref, b_ref = refs[:2]
        o_ref = refs[2 + nx]
        scratch = refs[3 + 2 * nx:]
        if nx:
            xrefs = (refs[2:2 + nx], refs[3 + nx:3 + 2 * nx], scratch[-3:])
            xchg.start_at_first_step(grid, *xrefs)
        if nk == 1:
            o_ref[...] = lax.dot_general(a_ref[...].astype(bf16), b_ref[...].astype(bf16), dims,
                                         preferred_element_type=f32).astype(o_ref.dtype)
        else:
            acc_ref = scratch[0]
            kk = pl.program_id(2)

            @pl.when(kk == 0)
            def _():
                acc_ref[...] = jnp.zeros_like(acc_ref)

            acc_ref[...] += lax.dot_general(a_ref[...].astype(bf16), b_ref[...].astype(bf16), dims,
                                            preferred_element_type=f32)

            @pl.when(kk == nk - 1)
            def _():
                o_ref[...] = acc_ref[...].astype(o_ref.dtype)
        if nx:
            xchg.wait_at_last_step(grid, *xrefs)

    out = pl.pallas_call(
        body, name=name, grid=grid,
        in_specs=[a_spec, b_spec] + ([_ANY] * nx),
        out_specs=[pl.BlockSpec((tm, tn), lambda j, i, kk: (i, j))] + ([_ANY] * nx),
        out_shape=[jax.ShapeDtypeStruct((m, n), out_dtype)] + (xchg.out_shape() if nx else []),
        scratch_shapes=([pltpu.VMEM((tm, tn), f32)] if nk > 1 else []) + (xchg.scratch() if nx else []),
        compiler_params=_cparams(("arbitrary",) * 3 if nx else ("parallel", "parallel", "arbitrary")),
    )(a, b, *(xchg.arrs if nx else []))
    return (out[0], out[1:]) if nx else out[0]


TOK_TILE = 512
SUB_COLS = 384


def _row_spec(width, tile=TOK_TILE):
    return pl.BlockSpec((tile, width), lambda i: (i, 0))


def _vec_spec(width, rows=1):
    return pl.BlockSpec((rows, width), lambda i: (0, 0))


def _ln_mod_fwd(x, gain, sc, sh, shard, name):
    s, d = x.shape
    nt = s // TOK_TILE
    ride = _ChipGather(shard)

    def body(x_ref, g_ref, sc_ref, sh_ref, sh_in, h_ref, sh_out, *sems):
        i = pl.program_id(0)
        pl.when(i == 0)(lambda: ride.start(sh_in, sh_out, sems))
        xv = x_ref[...]
        rstd = lax.rsqrt(jnp.mean(xv * xv, axis=-1, keepdims=True) + EPS)
        h = (xv * rstd) * g_ref[...] * (1.0 + sc_ref[...]) + sh_ref[...]
        h_ref[...] = h.astype(bf16)
        @pl.when(i == nt - 1)
        def _():
            ride.forward(sh_in, sh_out, sems)
            ride.finish(sh_in, sh_out, sems)

    return pl.pallas_call(
        body, name=name, grid=(nt,),
        in_specs=[_row_spec(d), _vec_spec(d), _vec_spec(d), _vec_spec(d), _ANY],
        out_specs=[_row_spec(d), _ANY],
        out_shape=[jax.ShapeDtypeStruct((s, d), bf16), ride.out_shape()],
        scratch_shapes=ride.scratch(),
        compiler_params=_cparams(("arbitrary",)),
    )(x, gain, sc, sh, shard)


def _proj_resid_ln_mod_fwd(pairs, x, gate, gain, sc, sh, name):
    s, d = x.shape
    npair = len(pairs)

    def body(*refs):
        aw = refs[:2 * npair]
        x_ref, gt_ref, g_ref, sc_ref, sh_ref, y_ref, x1_ref, h_ref = refs[2 * npair:]
        y = jnp.dot(aw[0][...].astype(bf16), aw[1][...], preferred_element_type=f32)
        for t in range(1, npair):
            y = y + jnp.dot(aw[2 * t][...].astype(bf16), aw[2 * t + 1][...], preferred_element_type=f32)
        y_ref[...] = y
        x1 = x_ref[...] + gt_ref[...] * y
        x1_ref[...] = x1
        rstd = lax.rsqrt(jnp.mean(x1 * x1, axis=-1, keepdims=True) + EPS)
        h = (x1 * rstd) * g_ref[...] * (1.0 + sc_ref[...]) + sh_ref[...]
        h_ref[...] = h.astype(bf16)

    aw_specs, aw = [], []
    for a, w in pairs:
        aw_specs += [_row_spec(a.shape[1]), pl.BlockSpec(w.shape, lambda i: (0, 0))]
        aw += [a, w]
    return pl.pallas_call(
        body, name=name, grid=(s // TOK_TILE,),
        in_specs=aw_specs + [_row_spec(d)] + [_vec_spec(d)] * 4,
        out_specs=[_row_spec(d)] * 3,
        out_shape=[jax.ShapeDtypeStruct((s, d), f32)] * 2 + [jax.ShapeDtypeStruct((s, d), bf16)],
        compiler_params=_cparams(("parallel",)),
    )(*aw, x, gate, gain, sc, sh)


FFN_TN = 1408


def _ffn_up(h2, w_gate, w_up, name):
    s, d = h2.shape
    tm = TOK_TILE

    def body(h_ref, wg_ref, wu_ref, a_ref, g_ref, u_ref):
        h = h_ref[...]
        g = jnp.dot(h, wg_ref[...], preferred_element_type=f32)
        u = jnp.dot(h, wu_ref[...], preferred_element_type=f32)
        a_ref[...] = (g * jax.nn.sigmoid(g) * u).astype(bf16)
        g_ref[...] = g.astype(bf16)
        u_ref[...] = u.astype(bf16)

    w_spec = pl.BlockSpec((d, FFN_TN), lambda j, i: (0, j))
    o_spec = pl.BlockSpec((tm, FFN_TN), lambda j, i: (i, j))
    return pl.pallas_call(
        body, name=name, grid=(D_FF // FFN_TN, s // tm),
        in_specs=[pl.BlockSpec((tm, d), lambda j, i: (i, 0)), w_spec, w_spec],
        out_specs=[o_spec] * 3,
        out_shape=[jax.ShapeDtypeStruct((s, D_FF), bf16)] * 3,
        compiler_params=_cparams(("parallel", "parallel")),
    )(h2, w_gate, w_up)


def _ffn_down_dx(dy2, w_down, gate, up, name):
    s, d = dy2.shape
    tm = TOK_TILE

    def body(dy_ref, w_ref, g_ref, u_ref, dg_ref, du_ref):
        dy = dy_ref[...]
        for c0 in range(0, FFN_TN, SUB_COLS):
            cols = slice(c0, min(c0 + SUB_COLS, FFN_TN))
            da = lax.dot_general(dy, w_ref[cols, :], _NT, preferred_element_type=f32)
            g = g_ref[:, cols].astype(f32)
            sg = jax.nn.sigmoid(g)
            du_ref[:, cols] = (da * g * sg).astype(bf16)
            dg_ref[:, cols] = (da * u_ref[:, cols].astype(f32) * sg * (1.0 + g * (1.0 - sg))).astype(bf16)

    t_spec = pl.BlockSpec((tm, FFN_TN), lambda j, i: (i, j))
    return pl.pallas_call(
        body, name=name, grid=(D_FF // FFN_TN, s // tm),
        in_specs=[pl.BlockSpec((tm, d), lambda j, i: (i, 0)), pl.BlockSpec((FFN_TN, d), lambda j, i: (j, 0)),
                  t_spec, t_spec],
        out_specs=[t_spec, t_spec],
        out_shape=[jax.ShapeDtypeStruct((s, D_FF), bf16)] * 2,
        compiler_params=_cparams(("parallel", "parallel")),
    )(dy2, w_down, gate, up)


def _acc_spec(width):
    return pl.BlockSpec((1, width), lambda i: (0, 0))


def _proj_final_loss_bwd(a, w, x1, gate2, final_g, target, name):
    s, d = x1.shape
    k = a.shape[1]

    def body(a_ref, w_ref, x1_ref, gt_ref, fg_ref, tg_ref, dx2_ref, dy2_ref, loss_ref, dfg_ref, dgt_ref):
        @pl.when(pl.program_id(0) == 0)
        def _():
            loss_ref[...] = jnp.zeros_like(loss_ref)
            dfg_ref[...] = jnp.zeros_like(dfg_ref)
            dgt_ref[...] = jnp.zeros_like(dgt_ref)

        y2 = jnp.dot(a_ref[...], w_ref[...], preferred_element_type=f32)
        gt = gt_ref[...]
        fg = fg_ref[...]
        x2 = x1_ref[...] + gt * y2
        rstd = lax.rsqrt(jnp.mean(x2 * x2, axis=-1, keepdims=True) + EPS)
        xn = x2 * rstd
        err = xn * fg - tg_ref[...]
        row = jnp.sum(err * err, axis=-1, keepdims=True) * (0.5 / d)
        loss_ref[...] += jnp.sum(row, axis=0, keepdims=True) + jnp.zeros_like(loss_ref)
        dout = err * (1.0 / d)
        dfg_ref[...] += jnp.sum(dout * xn, axis=0, keepdims=True)
        dxn = dout * fg
        dx2 = rstd * (dxn - xn * jnp.mean(dxn * xn, axis=-1, keepdims=True))
        dx2_ref[...] = dx2
        dgt_ref[...] += jnp.sum(dx2 * y2, axis=0, keepdims=True)
        dy2_ref[...] = (gt * dx2).astype(bf16)

    return pl.pallas_call(
        body, name=name, grid=(s // TOK_TILE,),
        in_specs=[_row_spec(k), pl.BlockSpec((k, d), lambda i: (0, 0)), _row_spec(d), _vec_spec(d), _vec_spec(d),
                  _row_spec(d)],
        out_specs=[_row_spec(d), _row_spec(d), _acc_spec(128), _acc_spec(d), _acc_spec(d)],
        out_shape=[jax.ShapeDtypeStruct((s, d), f32), jax.ShapeDtypeStruct((s, d), bf16),
                   jax.ShapeDtypeStruct((1, 128), f32), jax.ShapeDtypeStruct((1, d), f32),
                   jax.ShapeDtypeStruct((1, d), f32)],
        compiler_params=_cparams(("arbitrary",)),
    )(a, w, x1, gate2, final_g, target)


def _proj_ln_mod_bwd(pairs, xin, gain, sc, dres, tm, name, xchg, gate=None, y=None):
    s, d = xin.shape
    with_gate = gate is not None
    npair = len(pairs)
    n_in = 2 * npair + (7 if with_gate else 5) - 1
    n_out = 6 if with_gate else 4

    def body(*refs):
        ab = refs[:2 * npair]
        if with_gate:
            (x_ref, g_ref, sc_ref, dr_ref, gt_ref, y_ref,
             dx_ref, dsh_ref, dsc_ref, dg_ref, dy_ref, dgt_ref) = refs[2 * npair:]
        else:
            x_ref, g_ref, sc_ref, dr_ref, dx_ref, dsh_ref, dsc_ref, dg_ref = refs[2 * npair:]

        @pl.when(pl.program_id(0) == 0)
        def _():
            dsh_ref[...] = jnp.zeros_like(dsh_ref)
            dsc_ref[...] = jnp.zeros_like(dsc_ref)
            dg_ref[...] = jnp.zeros_like(dg_ref)
            if with_gate:
                dgt_ref[...] = jnp.zeros_like(dgt_ref)

        dh = lax.dot_general(ab[0][...].astype(bf16), ab[1][...], _NT, preferred_element_type=f32)
        for t in range(1, npair):
            dh = dh + lax.dot_general(ab[2 * t][...].astype(bf16), ab[2 * t + 1][...], _NT,
                                      preferred_element_type=f32)
        xv = x_ref[...]
        g = g_ref[...]
        sc1 = 1.0 + sc_ref[...]
        rstd = lax.rsqrt(jnp.mean(xv * xv, axis=-1, keepdims=True) + EPS)
        xn = xv * rstd
        dsh_ref[...] += jnp.sum(dh, axis=0, keepdims=True)
        dsc_ref[...] += jnp.sum(dh * (xn * g), axis=0, keepdims=True)
        dg_ref[...] += jnp.sum(dh * sc1 * xn, axis=0, keepdims=True)
        dxn = dh * sc1 * g
        dx = dr_ref[...] + rstd * (dxn - xn * jnp.mean(dxn * xn, axis=-1, keepdims=True))
        dx_ref[...] = dx
        if with_gate:
            dgt_ref[...] += jnp.sum(dx * y_ref[...], axis=0, keepdims=True)
            dy_ref[...] = (gt_ref[...] * dx).astype(bf16)

    row = lambda width: pl.BlockSpec((tm, width), lambda i: (i, 0))
    in_specs, args = [], []
    for a, b in pairs:
        in_specs += [row(a.shape[1]), pl.BlockSpec(b.shape, lambda i: (0, 0))]
        args += [a, b]
    in_specs += [row(d), _vec_spec(d), _vec_spec(d), row(d)]
    args += [xin, gain, sc, dres]
    out_specs = [row(d), _acc_spec(d), _acc_spec(d), _acc_spec(d)]
    out_shape = [jax.ShapeDtypeStruct((s, d), f32)] + [jax.ShapeDtypeStruct((1, d), f32)] * 3
    if with_gate:
        in_specs += [_vec_spec(d), row(d)]
        out_specs += [row(d), _acc_spec(d)]
        out_shape += [jax.ShapeDtypeStruct((s, d), bf16), jax.ShapeDtypeStruct((1, d), f32)]
        args += [gate, y]
    grid = (s // tm,)
    out = pl.pallas_call(
        _ride(body, n_in, n_out, xchg, grid), name=name, grid=grid,
        in_specs=in_specs + [_ANY] * xchg.n, out_specs=out_specs + [_ANY] * xchg.n,
        out_shape=out_shape + xchg.out_shape(), scratch_shapes=xchg.scratch(),
        compiler_params=_cparams(("arbitrary",)),
    )(*args, *xchg.arrs)
    return out[:n_out], out[n_out:]


def _bucket_tables():
    import numpy as np
    qi = np.arange(BAND)[:, None]
    kj = np.arange(2 * BAND)[None, :]
    steps = qi + BAND - kj
    max_exact = N_BUCKETS // 2
    out = []
    for d in DILATIONS:
        dist = np.maximum(steps, 0) * d
        dist_f = np.maximum(dist, 1).astype(np.float32)
        large = max_exact + (np.log(dist_f / np.float32(max_exact)) / np.float32(math.log(MAX_DISTANCE / max_exact))
                             * np.float32(N_BUCKETS - max_exact)).astype(np.int32)
        out.append(np.where(dist < max_exact, dist, np.minimum(large, N_BUCKETS - 1)))
    return jnp.asarray(np.stack(out).astype(np.int32))


def _bias_tables(rel_bias, idx):
    def body(idx_ref, rb_ref, o_ref):
        h = pl.program_id(1)
        idxv = idx_ref[0]
        acc = jnp.zeros((BAND, 2 * BAND), f32)
        for b in range(N_BUCKETS):
            acc = jnp.where(idxv == b, rb_ref[b, h], acc)
        o_ref[0, 0] = jnp.where(_attn_masks()[1], acc, NEG_INF)

    return pl.pallas_call(
        body, name="attn_bias_tables", grid=(3, N_HEADS),
        in_specs=[pl.BlockSpec((1, BAND, 2 * BAND), lambda br, h: (br, 0, 0)),
                  pl.BlockSpec(memory_space=pltpu.SMEM)],
        out_specs=pl.BlockSpec((1, 1, BAND, 2 * BAND), lambda br, h: (br, h, 0, 0)),
        out_shape=jax.ShapeDtypeStruct((3, N_HEADS, BAND, 2 * BAND), f32),
        compiler_params=_cparams(("parallel", "parallel")),
    )(idx, rel_bias)


def _bias_grad(dbias, idx):
    def body(idx_ref, db_ref, o_ref):
        br = pl.program_id(1)

        @pl.when(br == 0)
        def _():
            o_ref[...] = jnp.zeros_like(o_ref)

        idxv = idx_ref[0]
        dbv = db_ref[0, 0]
        row = lax.broadcasted_iota(jnp.int32, (N_BUCKETS, 128), 0)
        acc = jnp.zeros((N_BUCKETS, 128), f32)
        for b in range(N_BUCKETS):
            sb = jnp.sum(jnp.sum(jnp.where(idxv == b, dbv, 0.0), axis=1, keepdims=True), axis=0, keepdims=True)
            acc = acc + jnp.where(row == b, sb, 0.0)
        o_ref[0] += acc

    return pl.pallas_call(
        body, name="attn_bias_grad", grid=(N_HEADS, 3),
        in_specs=[pl.BlockSpec((1, BAND, 2 * BAND), lambda h, br: (br, 0, 0)),
                  pl.BlockSpec((1, 1, BAND, 2 * BAND), lambda h, br: (br, h, 0, 0))],
        out_specs=pl.BlockSpec((1, N_BUCKETS, 128), lambda h, br: (h, 0, 0)),
        out_shape=jax.ShapeDtypeStruct((N_HEADS, N_BUCKETS, 128), f32),
        compiler_params=_cparams(("parallel", "arbitrary")),
    )(idx, dbias)


def _attn_masks():
    lane = lax.broadcasted_iota(jnp.int32, (BAND, 128), 1)
    m0 = lane < HEAD_DIM
    qi = lax.broadcasted_iota(jnp.int32, (BAND, 2 * BAND), 0)
    kj = lax.broadcasted_iota(jnp.int32, (BAND, 2 * BAND), 1)
    steps = qi + BAND - kj
    in_window = (steps >= 0) & (steps <= BAND)
    return m0, in_window, kj >= BAND


_NT = (((1,), (1,)), ((), ()))
_TN = (((0,), (0,)), ((), ()))
_BNN = (((2,), (1,)), ((0,), (0,)))
_BNT = (((2,), (2,)), ((0,), (0,)))
_BTN = (((1,), (1,)), ((0,), (0,)))
ATTN_GROUP = 4
ATTN_ITEMS = PAD_UNIT // BAND
Q_COL, K_COL, V_COL = 0, 4, 8


def _attn_item_rows(j, d, c, cbase):
    r = lax.rem(j, d)
    b = lax.div(j, d)
    loc = b * (d * BAND) + r
    first = jnp.logical_and(c == 0, b == 0)
    start = cbase + loc
    pstart = jnp.where(first, start, start - d * BAND)
    return loc, start, pstart, first


def _attn_fwd(proj, bias, xchg):
    s = proj.shape[0]

    def body(q_ref, k_ref, v_ref, b_ref, y_ref, lse_ref, o_s, l_s):
        c = pl.program_id(1)
        cbase = pl.multiple_of(c * PAD_UNIT, PAD_UNIT)
        m0, in_window, cur_half = _attn_masks()
        for bi, d in enumerate(DILATIONS):
            def group(jg, carry, bi=bi, d=d):
                locs, qs, ks, vs, pens = [], [], [], [], []
                for t in range(ATTN_GROUP):
                    loc, start, pstart, first = _attn_item_rows(jg * ATTN_GROUP + t, d, c, cbase)
                    locs.append(loc)
                    qs.append(q_ref[pl.ds(loc, BAND, stride=d), :])
                    ks.append(jnp.concatenate([k_ref[pl.ds(pstart, BAND, stride=d), :],
                                               k_ref[pl.ds(start, BAND, stride=d), :]], axis=0))
                    vs.append(jnp.concatenate([v_ref[pl.ds(pstart, BAND, stride=d), :],
                                               v_ref[pl.ds(start, BAND, stride=d), :]], axis=0))
                    pens.append(jnp.where(cur_half, 0.0, jnp.where(first, NEG_INF, 0.0)))
                q = jnp.stack(qs)
                kk = jnp.stack(ks + ks).astype(bf16)
                vv = jnp.stack(vs + vs).astype(bf16)
                pen = jnp.stack(pens + pens)
                qh = (jnp.concatenate([jnp.where(m0, q, 0.0), jnp.where(m0, 0.0, q)], axis=0) * 0.125).astype(bf16)
                sc = lax.dot_general(qh, kk, _BNT, preferred_element_type=f32)
                sc = (sc.reshape(2, ATTN_GROUP, BAND, 2 * BAND) + b_ref[bi][:, None]).reshape(sc.shape) + pen
                mx = jnp.max(sc, axis=-1, keepdims=True)
                e = jnp.exp(sc - mx)
                l = jnp.sum(e, axis=-1, keepdims=True)
                o = lax.dot_general(e.astype(bf16), vv, _BNN, preferred_element_type=f32) * (1.0 / l)
                ls = mx + jnp.log(l)
                for t in range(ATTN_GROUP):
                    rows = pl.ds(locs[t], BAND, stride=d)
                    o_s[bi, rows, :] = jnp.where(m0, o[t], o[ATTN_GROUP + t])
                    l_s[bi, rows, :] = jnp.where(m0, ls[t], ls[ATTN_GROUP + t])
                return carry

            lax.fori_loop(0, ATTN_ITEMS // ATTN_GROUP, group, 0)

        def merge(t, carry):
            rows = pl.ds(pl.multiple_of(t * 256, 256), 256)
            ls = [l_s[i, rows, :] for i in range(3)]
            mx = jnp.maximum(jnp.maximum(ls[0], ls[1]), ls[2])
            ws = [jnp.exp(l - mx) for l in ls]
            tot = ws[0] + ws[1] + ws[2]
            y = (ws[0] * o_s[0, rows, :] + ws[1] * o_s[1, rows, :] + ws[2] * o_s[2, rows, :]) / tot
            y_ref[rows, :] = y
            lse_ref[rows, :] = mx + jnp.log(tot)
            return carry

        lax.fori_loop(0, PAD_UNIT // 256, merge, 0)

    chunk = lambda col: pl.BlockSpec((PAD_UNIT, 128), lambda p, c: (c, col + p))
    full = lambda col: pl.BlockSpec((s, 128), lambda p, c: (0, col + p))
    grid = (N_HEADS // 2, s // PAD_UNIT)
    out = pl.pallas_call(
        _ride(body, 4, 2, xchg, grid), name="attn_fwd", grid=grid,
        in_specs=[chunk(Q_COL), full(K_COL), full(V_COL),
                  pl.BlockSpec((3, 2, BAND, 2 * BAND), lambda p, c: (0, p, 0, 0))] + [_ANY] * xchg.n,
        out_specs=[chunk(0), chunk(0)] + [_ANY] * xchg.n,
        out_shape=[jax.ShapeDtypeStruct((s, GROUP_W), f32)] * 2 + xchg.out_shape(),
        scratch_shapes=[pltpu.VMEM((3, PAD_UNIT, 128), f32)] * 2 + xchg.scratch(),
        compiler_params=_cparams(("arbitrary", "arbitrary")),
    )(proj, proj, proj, bias, *xchg.arrs)
    return out[:2], out[2:]


def _attn_bwd(proj, bias, y, lse, dycat):
    s = proj.shape[0]

    def body(q_ref, k_ref, v_ref, b_ref, y_ref, lse_ref, dy_ref, dq_ref, dk_ref, dv_ref, db_ref, dd_s):
        c = pl.program_id(1)
        cbase = pl.multiple_of(c * PAD_UNIT, PAD_UNIT)
        m0, in_window, cur_half = _attn_masks()

        @pl.when(c == 0)
        def _():
            dk_ref[...] = jnp.zeros_like(dk_ref)
            dv_ref[...] = jnp.zeros_like(dv_ref)
            db_ref[...] = jnp.zeros_like(db_ref)

        dq_ref[...] = jnp.zeros_like(dq_ref)

        def rowdot(t, carry):
            rows = pl.ds(pl.multiple_of(t * 256, 256), 256)
            prod = dy_ref[rows, :] * y_ref[rows, :]
            lane = lax.broadcasted_iota(jnp.int32, prod.shape, 1)
            h0 = lane < HEAD_DIM
            d0 = jnp.sum(jnp.where(h0, prod, 0.0), axis=-1, keepdims=True)
            d1 = jnp.sum(jnp.where(h0, 0.0, prod), axis=-1, keepdims=True)
            dd_s[rows, :] = jnp.where(h0, d0, d1)
            return carry

        lax.fori_loop(0, PAD_UNIT // 256, rowdot, 0)

        for bi, d in enumerate(DILATIONS):
            def group(jg, carry, bi=bi, d=d):
                ng = ATTN_GROUP
                meta, qs, dos, lqs, dds, ks, vs, pens = [], [], [], [], [], [], [], []
                for t in range(ng):
                    loc, start, pstart, first = _attn_item_rows(jg * ng + t, d, c, cbase)
                    qrows = pl.ds(loc, BAND, stride=d)
                    rows = pl.ds(start, BAND, stride=d)
                    prows = pl.ds(pstart, BAND, stride=d)
                    meta.append((qrows, rows, prows))
                    qs.append(q_ref[qrows, :])
                    dos.append(dy_ref[qrows, :])
                    lqs.append(lse_ref[qrows, :])
                    dds.append(dd_s[qrows, :])
                    ks.append(jnp.concatenate([k_ref[prows, :], k_ref[rows, :]], axis=0))
                    vs.append(jnp.concatenate([v_ref[prows, :], v_ref[rows, :]], axis=0))
                    pens.append(jnp.where(cur_half, 0.0, jnp.where(first, NEG_INF, 0.0)))

                def heads(t):
                    return jnp.concatenate([jnp.where(m0, t, 0.0), jnp.where(m0, 0.0, t)], axis=0)

                def head_col(t):
                    return jnp.concatenate([t[:, :, 0:1], t[:, :, HEAD_DIM:HEAD_DIM + 1]], axis=0)

                qh = (heads(jnp.stack(qs)) * 0.125).astype(bf16)
                doh = heads(jnp.stack(dos)).astype(bf16)
                kk = jnp.stack(ks + ks).astype(bf16)
                vv = jnp.stack(vs + vs).astype(bf16)
                sc = lax.dot_general(qh, kk, _BNT, preferred_element_type=f32)
                sc = (sc.reshape(2, ng, BAND, 2 * BAND) + b_ref[bi][:, None]).reshape(sc.shape) + jnp.stack(pens + pens)
                p = jnp.exp(sc - head_col(jnp.stack(lqs)))
                dp = lax.dot_general(doh, vv, _BNT, preferred_element_type=f32)
                ds = p * (dp - head_col(jnp.stack(dds)))
                db_ref[bi] += jnp.sum(ds.reshape(2, ng, BAND, 2 * BAND), axis=1)
                dsb = ds.astype(bf16)
                dq = lax.dot_general(dsb, kk, _BNN, preferred_element_type=f32) * 0.125
                dk = lax.dot_general(dsb, qh, _BTN, preferred_element_type=f32)
                dv = lax.dot_general(p.astype(bf16), doh, _BTN, preferred_element_type=f32)
                for t in range(ng):
                    qrows, rows, prows = meta[t]
                    dq_ref[qrows, :] += jnp.where(m0, dq[t], dq[ng + t])
                    dkt = dk[t] + dk[ng + t]
                    dvt = dv[t] + dv[ng + t]
                    dk_ref[prows, :] += dkt[:BAND]
                    dk_ref[rows, :] += dkt[BAND:]
                    dv_ref[prows, :] += dvt[:BAND]
                    dv_ref[rows, :] += dvt[BAND:]
                return carry

            lax.fori_loop(0, ATTN_ITEMS // ATTN_GROUP, group, 0)

    chunk = lambda col: pl.BlockSpec((PAD_UNIT, 128), lambda p, c: (c, col + p))
    full = lambda col: pl.BlockSpec((s, 128), lambda p, c: (0, col + p))
    bias_spec = pl.BlockSpec((3, 2, BAND, 2 * BAND), lambda p, c: (0, p, 0, 0))
    return pl.pallas_call(
        body, name="attn_bwd", grid=(N_HEADS // 2, s // PAD_UNIT),
        in_specs=[chunk(Q_COL), full(K_COL), full(V_COL), bias_spec, chunk(0), chunk(0), chunk(0)],
        out_specs=[chunk(0), full(0), full(0), bias_spec],
        out_shape=[jax.ShapeDtypeStruct((s, GROUP_W), f32)] * 3
        + [jax.ShapeDtypeStruct((3, N_HEADS, BAND, 2 * BAND), f32)],
        scratch_shapes=[pltpu.VMEM((PAD_UNIT, 128), f32)],
        compiler_params=_cparams(("parallel", "arbitrary")),
    )(proj, proj, proj, bias, y, lse, dycat)


_HI = lax.Precision.HIGHEST
DELTA_COL = 1536
Z_COL = 3072
BA_BLOCK = 28
DELTA_ROWS = 1024


def _hdot(a, b):
    return jnp.dot(a, b, precision=_HI, preferred_element_type=f32)


_DIMS = dict(nn=(((2,), (1,)), ((0,), (0,))), nt=(((2,), (2,)), ((0,), (0,))), tn=(((1,), (1,)), ((0,), (0,))))


@functools.partial(jax.custom_vjp, nondiff_argnums=(2,))
def _mmx(a, b, mode):
    return lax.dot_general(a.astype(bf16), b.astype(bf16), _DIMS[mode], preferred_element_type=f32)


def _mmx_fwd(a, b, mode):
    return _mmx(a, b, mode), (a, b)


def _mmx_bwd(mode, res, g):
    a, b = res
    if mode == "nn":
        return _mmx(g, b, "nt"), _mmx(a, g, "tn")
    if mode == "nt":
        return _mmx(g, b, "nn"), _mmx(g, a, "tn")
    return _mmx(b, g, "nt"), _mmx(a, g, "nn")


_mmx.defvjp(_mmx_fwd, _mmx_bwd)


def _pair_iota():
    row = lax.broadcasted_iota(jnp.int32, (CHUNK, 128), 0)
    lane = lax.broadcasted_iota(jnp.int32, (CHUNK, 128), 1)
    return row, lane, lane & (CHUNK - 1)


def _bd(x):
    _, lane, _ = _pair_iota()
    m0 = lane < CHUNK
    return jnp.concatenate([jnp.where(m0, x, 0.0), jnp.where(m0, 0.0, x)], axis=1)


def _pmm(a, b):
    return _mmx(a, _bd(b), "nn")


def _ntp(x, y):
    return _mmx(x, _bd(y), "nt")


def _tnp(x, y):
    full = _mmx(x, y, "tn")
    _, lane, _ = _pair_iota()
    return jnp.where(lane < CHUNK, full[:, :CHUNK], full[:, CHUNK:])


def _tri_inv(a):
    row, lane, jj = _pair_iota()
    eye = jnp.where(row == jj, 1.0, 0.0).astype(f32)

    def same_block(log2b):
        return (row >> log2b) == (jj >> log2b)

    dgl = jnp.where(same_block(3), a, 0.0)
    d2 = _pmm(dgl, dgl)
    d4 = _pmm(d2, d2)
    t = _pmm(_pmm(eye - dgl, eye + d2), eye + d4)
    for lb in (3, 4, 5):
        off = jnp.where(same_block(lb + 1) & jnp.logical_not(same_block(lb)), a, 0.0)
        t = t - _pmm(_pmm(t, off), t)
    return t


@jax.custom_vjp
def _solve2(a, xv, xk, t):
    return _pmm(t, xv), _pmm(t, xk)


def _solve2_fwd(a, xv, xk, t):
    u, w = _pmm(t, xv), _pmm(t, xk)
    return (u, w), (t, u, w)


def _solve2_bwd(res, cts):
    t, u, w = res
    du, dw = cts
    dxv = _tnp(t, du)
    dxk = _tnp(t, dw)
    return -(_ntp(dxv, u) + _ntp(dxk, w)), dxv, dxk, jnp.zeros_like(t)


_solve2.defvjp(_solve2_fwd, _solve2_bwd)


def _chunk_pre(qp, kp, vp, bp, gcum, t=None):
    row, lane, jj = _pair_iota()
    causal = row >= jj
    strict = row > jj
    rsel = jnp.sum(jnp.where(row == jj, gcum, 0.0), axis=1, keepdims=True)
    decay = jnp.where(causal, jnp.exp(jnp.where(causal, gcum - rsel, 0.0)), 0.0)
    kb = kp * bp
    kd = _bd(kp)
    a = jnp.where(strict, _mmx(kb, kd, "nt") * decay, 0.0)
    eg = jnp.exp(gcum)
    if t is None:
        t = _tri_inv(a)
    u, w = _solve2(a, vp * bp, kb * eg, t)
    qk = jnp.where(causal, _mmx(qp, kd, "nt") * decay, 0.0)
    glast = jnp.sum(jnp.where(row == CHUNK - 1, gcum, 0.0), axis=1, keepdims=True)
    return u, w, qp * eg, kp * jnp.exp(glast - gcum), qk, jnp.exp(glast), t


def _chunk_post(u, w, qt, kh, qk, gam, sp):
    sd = _bd(sp)
    vnew = u - _mmx(w, sd, "nn")
    o = _mmx(qt, sd, "nn") + _pmm(qk, vnew)
    return o, gam * sp + _tnp(kh, vnew)


def _pair_spec(rows=DELTA_ROWS):
    return pl.BlockSpec((rows, 128), lambda i, p: (i, p))


DELTA_NB = DELTA_ROWS // CHUNK


def _chunks(ref):
    return ref[...].reshape(DELTA_NB, CHUNK, 128)


def _pairs(ref, rows):
    return jnp.stack([ref[rows, p * 128:(p + 1) * 128] for p in range(4)], axis=0)


def _delta_chunk_pre(qn, kn, sv, beta, g):
    s = qn.shape[0]

    def body(q_ref, k_ref, v_ref, b_ref, g_ref, u_ref, w_ref, qt_ref, kh_ref, qk_ref, t_ref, gm_ref):
        outs = _chunk_pre(_chunks(q_ref), _chunks(k_ref), _chunks(v_ref), _chunks(b_ref), _chunks(g_ref))
        for ref, val in zip((u_ref, w_ref, qt_ref, kh_ref, qk_ref, t_ref), outs[:5] + outs[6:]):
            ref[...] = val.reshape(DELTA_ROWS, 128).astype(ref.dtype)
        gm_ref[...] = jnp.broadcast_to(outs[5], (DELTA_NB, 8, 128)).reshape(DELTA_NB * 8, 128)

    v_spec = pl.BlockSpec((DELTA_ROWS, 128), lambda i, p: (i, 8 + p))
    return pl.pallas_call(
        body, name="delta_chunk_pre", grid=(s // DELTA_ROWS, 4),
        in_specs=[_pair_spec(), _pair_spec(), v_spec, _pair_spec(), _pair_spec()],
        out_specs=[_pair_spec()] * 6 + [_pair_spec(DELTA_NB * 8)],
        out_shape=[jax.ShapeDtypeStruct((s, GROUP_W), f32)] + [jax.ShapeDtypeStruct((s, GROUP_W), bf16)] * 5
        + [jax.ShapeDtypeStruct((s // 8, GROUP_W), f32)],
        compiler_params=_cparams(("parallel", "parallel")),
    )(qn, kn, sv, beta, g)


def _delta_scan_fwd(u, w, qt, kh, qk, gm):
    s = u.shape[0]

    def body(u_ref, w_ref, qt_ref, kh_ref, qk_ref, gm_ref, o_ref, ss_ref, st):
        @pl.when(pl.program_id(0) == 0)
        def _():
            st[...] = jnp.zeros_like(st)

        def chunk(ci, carry):
            rows = pl.ds(pl.multiple_of(ci * CHUNK, CHUNK), CHUNK)
            grow = pl.ds(pl.multiple_of(ci * 8, 8), 1)
            sp = st[...]
            o, s2 = _chunk_post(_pairs(u_ref, rows), _pairs(w_ref, rows), _pairs(qt_ref, rows),
                                _pairs(kh_ref, rows), _pairs(qk_ref, rows), _pairs(gm_ref, grow), sp)
            for p in range(4):
                ss_ref[rows, p * 128:(p + 1) * 128] = sp[p]
                o_ref[rows, p * 128:(p + 1) * 128] = o[p]
            st[...] = s2
            return carry

        lax.fori_loop(0, DELTA_NB, chunk, 0)

    spec = pl.BlockSpec((DELTA_ROWS, GROUP_W), lambda i: (i, 0))
    gspec = pl.BlockSpec((DELTA_NB * 8, GROUP_W), lambda i: (i, 0))
    return pl.pallas_call(
        body, name="delta_scan_fwd", grid=(s // DELTA_ROWS,),
        in_specs=[spec] * 5 + [gspec],
        out_specs=[spec, spec],
        out_shape=[jax.ShapeDtypeStruct((s, GROUP_W), f32)] * 2,
        scratch_shapes=[pltpu.VMEM((4, CHUNK, 128), f32)],
        compiler_params=_cparams(("arbitrary",)),
    )(u, w, qt, kh, qk, gm)


def _delta_scan_bwd(w, qt, kh, qk, gm, do):
    s = w.shape[0]
    nb = s // DELTA_ROWS

    def body(w_ref, qt_ref, kh_ref, qk_ref, gm_ref, do_ref, dso_ref, dst):
        @pl.when(pl.program_id(0) == 0)
        def _():
            dst[...] = jnp.zeros_like(dst)

        def chunk(t, carry):
            ci = DELTA_NB - 1 - t
            rows = pl.ds(pl.multiple_of(ci * CHUNK, CHUNK), CHUNK)
            grow = pl.ds(pl.multiple_of(ci * 8, 8), 1)
            ds = dst[...]
            for p in range(4):
                dso_ref[rows, p * 128:(p + 1) * 128] = ds[p]
            do = _pairs(do_ref, rows)
            dvn = _tnp(_pairs(qk_ref, rows), do) + _pmm(_pairs(kh_ref, rows), ds)
            dst[...] = _tnp(_pairs(qt_ref, rows), do) + _pairs(gm_ref, grow) * ds - _tnp(_pairs(w_ref, rows), dvn)
            return carry

        lax.fori_loop(0, DELTA_NB, chunk, 0)

    spec = pl.BlockSpec((DELTA_ROWS, GROUP_W), lambda i: (nb - 1 - i, 0))
    gspec = pl.BlockSpec((DELTA_NB * 8, GROUP_W), lambda i: (nb - 1 - i, 0))
    return pl.pallas_call(
        body, name="delta_scan_bwd", grid=(nb,),
        in_specs=[spec] * 4 + [gspec, spec],
        out_specs=spec,
        out_shape=jax.ShapeDtypeStruct((s, GROUP_W), f32),
        scratch_shapes=[pltpu.VMEM((4, CHUNK, 128), f32)],
        compiler_params=_cparams(("arbitrary",)),
    )(w, qt, kh, qk, gm, do)


def _delta_chunk_bwd(qn, kn, sv, beta, g, tinv, ss, dso, do):
    s = qn.shape[0]

    def body(q_ref, k_ref, v_ref, b_ref, g_ref, t_ref, ss_ref, dso_ref, do_ref,
             dq_ref, dk_ref, dv_ref, db_ref, dg_ref):
        sp = _chunks(ss_ref)
        t = _chunks(t_ref)

        def fn(q, k, v, b, gg):
            return _chunk_post(*_chunk_pre(q, k, v, b, gg, t)[:6], sp)

        _, vjp = jax.vjp(fn, _chunks(q_ref), _chunks(k_ref), _chunks(v_ref), _chunks(b_ref), _chunks(g_ref))
        grads = vjp((_chunks(do_ref), _chunks(dso_ref)))
        for ref, val in zip((dq_ref, dk_ref, dv_ref, db_ref, dg_ref), grads):
            ref[...] = val.reshape(DELTA_ROWS, 128)

    v_spec = pl.BlockSpec((DELTA_ROWS, 128), lambda i, p: (i, 8 + p))
    return pl.pallas_call(
        body, name="delta_chunk_bwd", grid=(s // DELTA_ROWS, 4),
        in_specs=[_pair_spec(), _pair_spec(), v_spec] + [_pair_spec()] * 6,
        out_specs=[_pair_spec()] * 5,
        out_shape=[jax.ShapeDtypeStruct((s, GROUP_W), f32)] * 5,
        compiler_params=_cparams(("parallel", "parallel")),
    )(qn, kn, sv, beta, g, tinv, ss, dso, do)


def _head_sums(x):
    r = lax.broadcasted_iota(jnp.int32, (128, 128), 0)
    c = lax.broadcasted_iota(jnp.int32, (128, 128), 1)
    pair = jnp.where((r >> 6) == (c >> 6), 1.0, 0.0).astype(f32)
    npair = x.shape[1] // 128
    xb = jnp.concatenate([x[None, :, p * 128:(p + 1) * 128] for p in range(npair)], axis=0)
    sums = _mmx(xb, jnp.broadcast_to(pair, (npair, 128, 128)), "nn")
    return jnp.concatenate([sums[p] for p in range(npair)], axis=1)


def _sel_dot(a, b):
    return jnp.dot(a, b, precision=lax.Precision.HIGH, preferred_element_type=f32)


def _expand_matrix(first):
    r = lax.broadcasted_iota(jnp.int32, (128, GROUP_W), 0)
    c = lax.broadcasted_iota(jnp.int32, (128, GROUP_W), 1) >> 6
    return jnp.where(r == c + first, 1.0, 0.0).astype(f32)


@functools.partial(jax.custom_vjp, nondiff_argnums=(1,))
def _expand_heads(ba, first):
    return _sel_dot(ba, _expand_matrix(first))


def _expand_heads_fwd(ba, first):
    return _expand_heads(ba, first), None


def _expand_heads_bwd(first, _, g):
    return (_mmx(g[None], _expand_matrix(first)[None], "nt")[0],)


_expand_heads.defvjp(_expand_heads_fwd, _expand_heads_bwd)


def _softplus(x):
    return jnp.maximum(x, 0.0) + jnp.log(1.0 + jnp.exp(-jnp.abs(x)))


def _prep_fn(sq, sk, ba, alog_e, dt_e):
    qn = sq * lax.rsqrt(_head_sums(sq * sq) + EPS) * (HEAD_DIM ** -0.5)
    kn = sk * lax.rsqrt(_head_sums(sk * sk) + EPS)
    bl = _expand_heads(ba, 0)
    al = _expand_heads(ba, N_HEADS)
    beta = jax.nn.sigmoid(bl)
    g = -jnp.exp(alog_e) * _softplus(al + dt_e)
    nchunk = g.shape[0] // CHUNK
    ri = lax.broadcasted_iota(jnp.int32, (nchunk, CHUNK, CHUNK), 1)
    ci = lax.broadcasted_iota(jnp.int32, (nchunk, CHUNK, CHUNK), 2)
    tril = jnp.where(ri >= ci, 1.0, 0.0).astype(f32)
    gcum = lax.dot_general(tril, g.reshape(nchunk, CHUNK, g.shape[1]), _BNN, precision=lax.Precision.HIGH,
                           preferred_element_type=f32)
    return qn, kn, beta, gcum.reshape(g.shape)


def _gnorm_fn(o, z, ng_e):
    ms = _head_sums(o * o) * (1.0 / HEAD_DIM)
    return o * lax.rsqrt(ms + EPS) * ng_e * (z * jax.nn.sigmoid(z))


def _tok_spec(width, col):
    return pl.BlockSpec((TOK_TILE, width), lambda i: (i, col))


def _conv_taps(xs_ref, w_ref, base, n, cols):
    acc = w_ref[CONV_WIDTH - 1:CONV_WIDTH, cols] * xs_ref[pl.ds(base, n), cols]
    for j in range(CONV_WIDTH - 1):
        acc = acc + w_ref[j:j + 1, cols] * xs_ref[pl.ds(base - (CONV_WIDTH - 1) + j, n), cols]
    return acc


def _conv_silu_fwd(proj, conv_w):
    s = proj.shape[0]
    wd = 3 * GROUP_W
    hb = TOK_TILE // 8

    def body(x_ref, halo_ref, w_ref, o_ref, xs):
        inner = pl.program_id(0) > 0

        def lane_block(cb, carry):
            cols = pl.ds(pl.multiple_of(cb * 128, 128), 128)
            xs[0:8, cols] = jnp.where(inner, halo_ref[:, cols], 0.0)
            xs[8:, cols] = x_ref[:, cols]
            y = _conv_taps(xs, w_ref, 8, TOK_TILE, cols)
            o_ref[:, cols] = y * jax.nn.sigmoid(y)
            return carry

        lax.fori_loop(0, wd // 128, lane_block, 0)

    return pl.pallas_call(
        body, name="delta_conv_fwd", grid=(s // TOK_TILE,),
        in_specs=[_tok_spec(wd, 1), pl.BlockSpec((8, wd), lambda i: (jnp.maximum(i * hb - 1, 0), 1)),
                  pl.BlockSpec((CONV_WIDTH, wd), lambda i: (0, 0))],
        out_specs=_tok_spec(wd, 0),
        out_shape=jax.ShapeDtypeStruct((s, wd), f32),
        scratch_shapes=[pltpu.VMEM((TOK_TILE + 8, wd), f32)],
        compiler_params=_cparams(("parallel",)),
    )(proj, proj, conv_w)


def _conv_silu_bwd(proj, conv_w, ds3, xchg):
    s = proj.shape[0]
    wd = 3 * GROUP_W
    hb = TOK_TILE // 8
    nt = s // TOK_TILE

    def body(x_ref, hp_ref, hn_ref, dq_ref, dk_ref, dv_ref, dqn_ref, dkn_ref, dvn_ref, w_ref, dx_ref, dw_ref, xs, dys):
        i = pl.program_id(0)

        @pl.when(i == 0)
        def _():
            dw_ref[...] = jnp.zeros_like(dw_ref)

        last = i == nt - 1
        def lane_block(lb, carry, third, cur, nxt):
            tcols = pl.ds(pl.multiple_of(lb * 128, 128), 128)
            cols = pl.ds(pl.multiple_of(third * GROUP_W + lb * 128, 128), 128)
            xs[0:8, cols] = jnp.where(i > 0, hp_ref[:, cols], 0.0)
            xs[8:8 + TOK_TILE, cols] = x_ref[:, cols]
            xs[8 + TOK_TILE:, cols] = jnp.where(last, 0.0, hn_ref[:, cols])
            y = _conv_taps(xs, w_ref, 8, TOK_TILE, cols)
            sg = jax.nn.sigmoid(y)
            dy0 = cur[:, tcols] * (sg * (1.0 + y * (1.0 - sg)))
            dys[0:TOK_TILE, cols] = dy0
            yn = _conv_taps(xs, w_ref, 8 + TOK_TILE, 8, cols)
            sgn = jax.nn.sigmoid(yn)
            dys[TOK_TILE:, cols] = jnp.where(last, 0.0, nxt[:, tcols]) * (sgn * (1.0 + yn * (1.0 - sgn)))
            dx = w_ref[CONV_WIDTH - 1:CONV_WIDTH, cols] * dy0
            for j in range(CONV_WIDTH - 1):
                dx = dx + w_ref[j:j + 1, cols] * dys[pl.ds(CONV_WIDTH - 1 - j, TOK_TILE), cols]
            dx_ref[:, cols] = dx.astype(dx_ref.dtype)
            for j in range(CONV_WIDTH):
                dw_ref[j:j + 1, cols] += jnp.sum(dy0 * xs[pl.ds(8 - (CONV_WIDTH - 1) + j, TOK_TILE), cols],
                                                 axis=0, keepdims=True)
            return carry

        for third, (cur, nxt) in enumerate(((dq_ref, dqn_ref), (dk_ref, dkn_ref), (dv_ref, dvn_ref))):
            lax.fori_loop(0, GROUP_W // 128, functools.partial(lane_block, third=third, cur=cur, nxt=nxt), 0)

    prev8 = lambda col: pl.BlockSpec((8, wd), lambda i: (jnp.maximum(i * hb - 1, 0), col))
    next8 = lambda col: pl.BlockSpec((8, wd), lambda i: (jnp.minimum((i + 1) * hb, s // 8 - 1), col))
    next8_third = pl.BlockSpec((8, GROUP_W), lambda i: (jnp.minimum((i + 1) * hb, s // 8 - 1), 0))
    out = pl.pallas_call(
        _ride(body, 10, 2, xchg, (nt,)), name="delta_conv_bwd", grid=(nt,),
        in_specs=[_tok_spec(wd, 1), prev8(1), next8(1)] + [_tok_spec(GROUP_W, 0)] * 3 + [next8_third] * 3
        + [pl.BlockSpec((CONV_WIDTH, wd), lambda i: (0, 0))] + [_ANY] * xchg.n,
        out_specs=[_tok_spec(wd, 0), pl.BlockSpec((CONV_WIDTH, wd), lambda i: (0, 0))] + [_ANY] * xchg.n,
        out_shape=[jax.ShapeDtypeStruct((s, wd), bf16), jax.ShapeDtypeStruct((CONV_WIDTH, wd), f32)] + xchg.out_shape(),
        scratch_shapes=[pltpu.VMEM((TOK_TILE + 16, wd), f32), pltpu.VMEM((TOK_TILE + 8, wd), f32)] + xchg.scratch(),
        compiler_params=_cparams(("arbitrary",)),
    )(proj, proj, proj, *ds3, *ds3, conv_w, *xchg.arrs)
    return out[:2], out[2:]


def _delta_prep_fwd(sconv, proj, alog_e, dt_e):
    s = sconv.shape[0]

    def body(sq_ref, sk_ref, ba_ref, al_ref, dt_ref, q_ref, k_ref, b_ref, g_ref):
        qn, kn, beta, g = _prep_fn(sq_ref[...], sk_ref[...], ba_ref[...], al_ref[...], dt_ref[...])
        q_ref[...] = qn
        k_ref[...] = kn
        b_ref[...] = beta
        g_ref[...] = g

    return pl.pallas_call(
        body, name="delta_prep_fwd", grid=(s // TOK_TILE,),
        in_specs=[_tok_spec(GROUP_W, 0), _tok_spec(GROUP_W, 1), _tok_spec(128, BA_BLOCK),
                  _vec_spec(GROUP_W), _vec_spec(GROUP_W)],
        out_specs=[_tok_spec(GROUP_W, 0)] * 4,
        out_shape=[jax.ShapeDtypeStruct((s, GROUP_W), f32)] * 4,
        compiler_params=_cparams(("parallel",)),
    )(sconv, sconv, proj, alog_e, dt_e)


def _delta_prep_bwd(sconv, proj, alog_e, dt_e, dqn, dkn, dbeta, dg, xchg):
    s = sconv.shape[0]
    grid = (s // TOK_TILE,)

    def body(sq_ref, sk_ref, ba_ref, al_ref, dt_ref, dq_ref, dk_ref, db_ref, dg_ref,
             dsq_ref, dsk_ref, dba_ref, dal_ref, ddt_ref):
        @pl.when(pl.program_id(0) == 0)
        def _():
            dal_ref[...] = jnp.zeros_like(dal_ref)
            ddt_ref[...] = jnp.zeros_like(ddt_ref)

        _, vjp = jax.vjp(_prep_fn, sq_ref[...], sk_ref[...], ba_ref[...], al_ref[...], dt_ref[...])
        dsq, dsk, dba, dal, ddt = vjp((dq_ref[...], dk_ref[...], db_ref[...], dg_ref[...]))
        dsq_ref[...] = dsq
        dsk_ref[...] = dsk
        dba_ref[...] = dba.astype(bf16)
        dal_ref[...] += dal
        ddt_ref[...] += ddt

    out = pl.pallas_call(
        _ride(body, 9, 5, xchg, grid), name="delta_prep_bwd", grid=grid,
        in_specs=[_tok_spec(GROUP_W, 0), _tok_spec(GROUP_W, 1), _tok_spec(128, BA_BLOCK),
                  _vec_spec(GROUP_W), _vec_spec(GROUP_W)] + [_tok_spec(GROUP_W, 0)] * 4 + [_ANY] * xchg.n,
        out_specs=[_tok_spec(GROUP_W, 0), _tok_spec(GROUP_W, 0), _tok_spec(128, 0),
                   _acc_spec(GROUP_W), _acc_spec(GROUP_W)] + [_ANY] * xchg.n,
        out_shape=[jax.ShapeDtypeStruct((s, GROUP_W), f32)] * 2 + [jax.ShapeDtypeStruct((s, 128), bf16)]
        + [jax.ShapeDtypeStruct((1, GROUP_W), f32)] * 2 + xchg.out_shape(),
        scratch_shapes=xchg.scratch(),
        compiler_params=_cparams(("arbitrary",)),
    )(sconv, sconv, proj, alog_e, dt_e, dqn, dkn, dbeta, dg, *xchg.arrs)
    return out[:5], out[5:]


def _gnorm_fwd(o, proj, ng_e):
    s = o.shape[0]

    def body(o_ref, z_ref, g_ref, y_ref):
        y_ref[...] = _gnorm_fn(o_ref[...], z_ref[...], g_ref[...])

    return pl.pallas_call(
        body, name="delta_gnorm_fwd", grid=(s // TOK_TILE,),
        in_specs=[_tok_spec(GROUP_W, 0), _tok_spec(GROUP_W, Z_COL // GROUP_W), _vec_spec(GROUP_W)],
        out_specs=_tok_spec(GROUP_W, 0),
        out_shape=jax.ShapeDtypeStruct((s, GROUP_W), f32),
        compiler_params=_cparams(("parallel",)),
    )(o, proj, ng_e)


def _gnorm_bwd(o, proj, ng_e, dycat):
    s = o.shape[0]

    def body(o_ref, z_ref, g_ref, dy_ref, do_ref, dz_ref, dg_ref):
        @pl.when(pl.program_id(0) == 0)
        def _():
            dg_ref[...] = jnp.zeros_like(dg_ref)

        _, vjp = jax.vjp(_gnorm_fn, o_ref[...], z_ref[...], g_ref[...])
        do, dz, dg = vjp(dy_ref[...])
        do_ref[...] = do
        dz_ref[...] = dz.astype(bf16)
        dg_ref[...] += dg

    return pl.pallas_call(
        body, name="delta_gnorm_bwd", grid=(s // TOK_TILE,),
        in_specs=[_tok_spec(GROUP_W, 0), _tok_spec(GROUP_W, Z_COL // GROUP_W), _vec_spec(GROUP_W),
                  _tok_spec(GROUP_W, 1)],
        out_specs=[_tok_spec(GROUP_W, 0), _tok_spec(GROUP_W, 0), _acc_spec(GROUP_W)],
        out_shape=[jax.ShapeDtypeStruct((s, GROUP_W), f32), jax.ShapeDtypeStruct((s, GROUP_W), bf16),
                   jax.ShapeDtypeStruct((1, GROUP_W), f32)],
        compiler_params=_cparams(("arbitrary",)),
    )(o, proj, ng_e, dycat)


_MESH = pl.DeviceIdType.MESH
_ANY = pl.BlockSpec(memory_space=pl.ANY)
_VMEM = pl.BlockSpec(memory_space=pltpu.VMEM)


def _my_place():
    x, y, c = lax.axis_index("x"), lax.axis_index("y"), lax.axis_index("c")
    return x, y, c, 4 * x + 2 * y + c


def _peer(k, x, y, c):
    px = 1 - x if k & 4 else x
    py = 1 - y if k & 2 else y
    pc = 1 - c if k & 1 else c
    return (px, py, pc), 4 * px + 2 * py + pc


def _exchange_all(src_of_peer, dst_ref, send_sems, recv_sems, x, y, c, me):
    sent = []
    for k in range(1, N_DEV):
        dev, pidx = _peer(k, x, y, c)
        cp = pltpu.make_async_remote_copy(src_ref=src_of_peer(pidx), dst_ref=dst_ref.at[me],
                                          send_sem=send_sems.at[k - 1], recv_sem=recv_sems.at[k - 1],
                                          device_id=dev, device_id_type=_MESH)
        cp.start()
        sent.append(cp)
    for k in range(1, N_DEV):
        dev, pidx = _peer(k, x, y, c)
        pltpu.make_async_remote_copy(src_ref=src_of_peer(pidx), dst_ref=dst_ref.at[pidx],
                                     send_sem=send_sems.at[k - 1], recv_sem=recv_sems.at[k - 1],
                                     device_id=dev, device_id_type=_MESH).wait_recv()
    for cp in sent:
        cp.wait_send()


def _ada_exchange(cv8, w_ada, b_ada8):
    def body(cv_ref, w_ref, b_ref, call_ref, modp_ref, part_s, s1, r1, s2, r2):
        x, y, c, me = _my_place()
        call_ref[me] = cv_ref[...]
        _exchange_all(lambda pidx: cv_ref, call_ref, s1, r1, x, y, c, me)
        bias = b_ref[me]
        for j in range(N_DEV):
            cj = call_ref[j][:, :D_MODEL]
            part_s[j] = _hdot(cj * jax.nn.sigmoid(cj), w_ref[...]) + bias
        modp_ref[me] = part_s[me]
        _exchange_all(lambda pidx: part_s.at[pidx], modp_ref, s2, r2, x, y, c, me)

    nsh = w_ada.shape[1]
    return pl.pallas_call(
        body, name="ada_exchange",
        in_specs=[_VMEM, _VMEM, _VMEM], out_specs=[_VMEM, _VMEM],
        out_shape=[jax.ShapeDtypeStruct((N_DEV, 8, cv8.shape[1]), f32), jax.ShapeDtypeStruct((N_DEV, 8, nsh), f32)],
        scratch_shapes=[pltpu.VMEM((N_DEV, 8, nsh), f32)] + [pltpu.SemaphoreType.DMA((N_DEV - 1,))] * 4,
        compiler_params=pltpu.CompilerParams(vmem_limit_bytes=VMEM_LIMIT),
    )(cv8, w_ada, b_ada8)


def _all_to_all(arrs, name):
    ex = _Exchange(arrs, gather=False)

    def body(*refs):
        srcs, dsts, sems = refs[:ex.n], refs[ex.n:2 * ex.n], refs[2 * ex.n:]
        ex.start(srcs, dsts, sems)
        ex.wait(srcs, dsts, sems)

    return pl.pallas_call(
        body, name=name,
        in_specs=[_ANY] * ex.n, out_specs=[_ANY] * ex.n,
        out_shape=ex.out_shape(), scratch_shapes=ex.scratch(),
    )(*arrs)


class _Exchange:
    def __init__(self, arrs, gather):
        self.arrs, self.gather, self.n = list(arrs), gather, len(arrs)

    def out_shape(self):
        return [jax.ShapeDtypeStruct(((N_DEV,) + a.shape) if self.gather else a.shape, a.dtype) for a in self.arrs]

    def scratch(self):
        if self.n == 0:
            return []
        return [pltpu.SemaphoreType.DMA((self.n, N_DEV - 1)), pltpu.SemaphoreType.DMA((self.n, N_DEV - 1)),
                pltpu.SemaphoreType.DMA((self.n,))]

    def _src(self, srcs, a, idx):
        return srcs[a] if self.gather else srcs[a].at[idx]

    def _copies(self, srcs, dsts, sems, incoming):
        send_sems, recv_sems, _ = sems
        x, y, c, me = _my_place()
        out = []
        for a in range(self.n):
            for k in range(1, N_DEV):
                dev, pidx = _peer(k, x, y, c)
                out.append(pltpu.make_async_remote_copy(
                    src_ref=self._src(srcs, a, pidx), dst_ref=dsts[a].at[pidx if incoming else me],
                    send_sem=send_sems.at[a, k - 1], recv_sem=recv_sems.at[a, k - 1],
                    device_id=dev, device_id_type=_MESH))
        return out

    def _local(self, srcs, dsts, sems):
        me = _my_place()[3]
        return [pltpu.make_async_copy(self._src(srcs, a, me), dsts[a].at[me], sems[2].at[a]) for a in range(self.n)]

    def start(self, srcs, dsts, sems):
        for cp in self._local(srcs, dsts, sems) + self._copies(srcs, dsts, sems, incoming=False):
            cp.start()

    def wait(self, srcs, dsts, sems):
        for cp in self._copies(srcs, dsts, sems, incoming=True):
            cp.wait_recv()
        for cp in self._copies(srcs, dsts, sems, incoming=False):
            cp.wait_send()
        for cp in self._local(srcs, dsts, sems):
            cp.wait()

    def start_at_first_step(self, grid, srcs, dsts, sems):
        first = functools.reduce(jnp.logical_and, [pl.program_id(i) == 0 for i in range(len(grid))])
        pl.when(first)(lambda: self.start(srcs, dsts, sems))

    def wait_at_last_step(self, grid, srcs, dsts, sems):
        last = functools.reduce(jnp.logical_and, [pl.program_id(i) == g - 1 for i, g in enumerate(grid)])
        pl.when(last)(lambda: self.wait(srcs, dsts, sems))


class _ChipGather:
    def __init__(self, shard):
        self.shard = shard

    def out_shape(self):
        return jax.ShapeDtypeStruct((N_DEV,) + self.shard.shape, self.shard.dtype)

    def scratch(self):
        return [pltpu.SemaphoreType.DMA((N_DEV - 1,)), pltpu.SemaphoreType.DMA((N_DEV - 1,)),
                pltpu.SemaphoreType.DMA(())]

    def _place(self):
        x, y, c, me = _my_place()
        return x, y, c, me, (x, y, 1 - c), [(1 - x, y), (x, 1 - y), (1 - x, 1 - y)]

    def _copy(self, out, sems, k, block, to, src=None):
        rows = out.at[4 * block[0] + 2 * block[1] + block[2]]
        return pltpu.make_async_remote_copy(src_ref=rows if src is None else src, dst_ref=rows,
                                            send_sem=sems[0].at[k], recv_sem=sems[1].at[k],
                                            device_id=to, device_id_type=_MESH)

    def start(self, src, out, sems):
        x, y, c, me, sib, chips = self._place()
        pltpu.make_async_copy(src, out.at[me], sems[2]).start()
        self._copy(out, sems, 0, (x, y, c), sib, src=src).start()
        for j, chip in enumerate(chips):
            self._copy(out, sems, 1 + j, (x, y, c), (*chip, c), src=src).start()

    def forward(self, src, out, sems):
        x, y, c, me, sib, chips = self._place()
        for j, chip in enumerate(chips):
            self._copy(out, sems, 1 + j, (*chip, c), (x, y, c)).wait_recv()
            self._copy(out, sems, 4 + j, (*chip, c), sib).start()

    def finish(self, src, out, sems):
        x, y, c, me, sib, chips = self._place()
        self._copy(out, sems, 0, (x, y, 1 - c), (x, y, c)).wait_recv()
        for j, chip in enumerate(chips):
            self._copy(out, sems, 4 + j, (*chip, 1 - c), (x, y, c)).wait_recv()
        self._copy(out, sems, 0, (x, y, c), sib, src=src).wait_send()
        for j, chip in enumerate(chips):
            self._copy(out, sems, 1 + j, (x, y, c), (*chip, c), src=src).wait_send()
            self._copy(out, sems, 4 + j, (*chip, c), sib).wait_send()
        pltpu.make_async_copy(src, out.at[me], sems[2]).wait()


def _ride(body, n_in, n_out, xchg, grid):
    nx = xchg.n
    if nx == 0:
        return body

    def wrapped(*refs):
        ins, xs = refs[:n_in], refs[n_in:n_in + nx]
        outs, xd = refs[n_in + nx:n_in + nx + n_out], refs[n_in + nx + n_out:n_in + 2 * nx + n_out]
        scratch = refs[n_in + 2 * nx + n_out:]
        xchg.start_at_first_step(grid, xs, xd, scratch[-3:])
        body(*ins, *outs, *scratch[:-3])
        xchg.wait_at_last_step(grid, xs, xd, scratch[-3:])

    return wrapped


def _adamw_math(w, g, m, v):
    m2 = ADAM_B1 * m + (1.0 - ADAM_B1) * g
    v2 = ADAM_B2 * v + (1.0 - ADAM_B2) * (g * g)
    m_hat = m2 / (1.0 - ADAM_B1 ** ADAM_STEP)
    v_hat = v2 / (1.0 - ADAM_B2 ** ADAM_STEP)
    delta = -ADAM_LR * (m_hat / (jnp.sqrt(v_hat) + ADAM_EPS) + ADAM_WD * w)
    return delta, m2, v2


def _row_tile(rows):
    for t in (256, 128, 64, 32, 16, 8):
        if rows % t == 0:
            return t
    return rows


def _reduce_adamw(parts, w, m, v, name):
    _, r, cdim = parts.shape
    tr = _row_tile(r)

    def body(p_ref, w_ref, m_ref, v_ref, g_ref, d_ref, m2_ref, v2_ref):
        g = p_ref[0].astype(f32)
        for j in range(1, N_DEV):
            g = g + p_ref[j].astype(f32)
        delta, m2, v2 = _adamw_math(w_ref[...], g, m_ref[...], v_ref[...])
        g_ref[...] = g
        d_ref[...] = delta
        m2_ref[...] = m2
        v2_ref[...] = v2

    spec = pl.BlockSpec((tr, cdim), lambda i: (i, 0))
    return pl.pallas_call(
        body, name=name, grid=(r // tr,),
        in_specs=[pl.BlockSpec((N_DEV, tr, cdim), lambda i: (0, i, 0)), spec, spec, spec],
        out_specs=[spec] * 4,
        out_shape=[jax.ShapeDtypeStruct((r, cdim), f32)] * 4,
        compiler_params=_cparams(("parallel",)),
    )(parts, w, m, v)


def _adamw(w, g, m, v, name):
    r, cdim = w.shape
    tr = _row_tile(r)

    def body(w_ref, g_ref, m_ref, v_ref, d_ref, m2_ref, v2_ref):
        delta, m2, v2 = _adamw_math(w_ref[...], g_ref[...], m_ref[...], v_ref[...])
        d_ref[...] = delta
        m2_ref[...] = m2
        v2_ref[...] = v2

    spec = pl.BlockSpec((tr, cdim), lambda i: (i, 0))
    return pl.pallas_call(
        body, name=name, grid=(r // tr,),
        in_specs=[spec] * 4, out_specs=[spec] * 3,
        out_shape=[jax.ShapeDtypeStruct((r, cdim), f32)] * 3,
        compiler_params=_cparams(("parallel",)),
    )(w, g, m, v)


def _sum_devices(parts, name):
    _, r, cdim = parts.shape

    def body(p_ref, o_ref):
        g = p_ref[0]
        for j in range(1, N_DEV):
            g = g + p_ref[j]
        o_ref[...] = g

    return pl.pallas_call(
        body, name=name, out_shape=jax.ShapeDtypeStruct((r, cdim), f32),
        in_specs=[_VMEM], out_specs=_VMEM,
    )(parts)


def _ada_wgrad(c_all8, dmod_cols):
    nsh = dmod_cols.shape[1]

    def body(c_ref, d_ref, o_ref):
        cv = c_ref[...]
        o_ref[...] = lax.dot_general(cv * jax.nn.sigmoid(cv), d_ref[...], _TN, precision=_HI,
                                     preferred_element_type=f32)

    return pl.pallas_call(
        body, name="ada_wgrad", out_shape=jax.ShapeDtypeStruct((D_MODEL, nsh), f32),
        in_specs=[_VMEM, _VMEM], out_specs=_VMEM,
        compiler_params=pltpu.CompilerParams(vmem_limit_bytes=VMEM_LIMIT),
    )(c_all8, dmod_cols)


def _cols(t):
    return t.transpose(1, 0, 2).reshape(t.shape[1], N_DEV * t.shape[2])


def _col_blocks(t, n):
    return t.reshape(t.shape[0], N_DEV, n).transpose(1, 0, 2).astype(bf16)


def _row_blocks(t):
    return t.reshape(N_DEV, t.shape[0] // N_DEV, t.shape[1]).astype(bf16)


def _local_step(x, tgt, mod, norm_attn_g, w_in_sh, rel_bias, conv_full, a_log, dt_bias, delta_norm_g,
                norm_ffn_g, final_norm_g, w_out_sh, w_gate_sh, w_up_sh, w_down_sh):
    s = x.shape[0]
    sh1, sc1, g1, sh2, sc2, g2 = [mod[:, i * D_MODEL:(i + 1) * D_MODEL] for i in range(6)]
    nag = norm_attn_g.reshape(1, D_MODEL)
    nfg = norm_ffn_g.reshape(1, D_MODEL)
    fg = final_norm_g.reshape(1, D_MODEL)
    idx = _bucket_tables()
    bias = _bias_tables(rel_bias, idx)
    alog_e = jnp.repeat(a_log.reshape(N_HEADS), HEAD_DIM)[None]
    dt_e = jnp.repeat(dt_bias.reshape(N_HEADS), HEAD_DIM)[None]
    ng_e = jnp.tile(delta_norm_g.reshape(HEAD_DIM), N_HEADS)[None]

    h1, w_in_g = _ln_mod_fwd(x, nag, sc1, sh1, w_in_sh, "ln1_fwd")
    w_in_p = jnp.pad(_cols(w_in_g), ((0, 0), (0, IN_PAD - IN_WIDTH)))
    proj, (w_out_g,) = _mm(h1, w_in_p, "nn", f32, 512, IN_PAD, 1024, "in_proj",
                           xchg=_Exchange([w_out_sh], gather=True))
    (y_attn, lse), (w_gate_g, w_up_g, w_down_g) = _attn_fwd(
        proj, bias, _Exchange([w_gate_sh, w_up_sh, w_down_sh], gather=True))
    w_out_b = w_out_g.reshape(2 * GROUP_W, D_MODEL)
    w_gate_b, w_up_b = _cols(w_gate_g), _cols(w_up_g)
    w_down_b = w_down_g.reshape(D_FF, D_MODEL)
    n_ff = w_gate_sh.shape[1]
    sconv = _conv_silu_fwd(proj, conv_full)
    qn, kn, beta, g = _delta_prep_fwd(sconv, proj, alog_e, dt_e)
    u, w, qt, kh, qk, tinv, gm = _delta_chunk_pre(qn, kn, sconv, beta, g)
    o, ss = _delta_scan_fwd(u, w, qt, kh, qk, gm)
    y_delta = _gnorm_fwd(o, proj, ng_e)
    y, x1, h2 = _proj_resid_ln_mod_fwd([(y_attn, w_out_b[:GROUP_W]), (y_delta, w_out_b[GROUP_W:])],
                                       x, g1, nfg, sc2, sh2, "out_proj_ln2")
    act, gate, up = _ffn_up(h2, w_gate_b, w_up_b, "ffn_up")
    dx2, dy2, loss, dfg, dg2 = _proj_final_loss_bwd(act, w_down_b, x1, g2, fg, tgt, "ffn_down_loss")

    dgate, dup = _ffn_down_dx(dy2, w_down_b, gate, up, "ffn_down_dx")
    g_down = _mm(act, dy2, "tn", f32, 1408, 1024, 1024, "ffn_down_dw")
    (dx1, dsh2, dsc2, dnfg, dy, dg1), (r_down,) = _proj_ln_mod_bwd(
        [(dgate, w_gate_b), (dup, w_up_b)], x1, nfg, sc2, dx2, 256, "ffn_up_dx_ln2",
        _Exchange([_row_blocks(g_down)], gather=False), gate=g1, y=y)
    g_gate = _mm(h2, dgate, "tn", f32, 1024, 1408, 1024, "ffn_gate_dw")
    g_up = _mm(h2, dup, "tn", f32, 1024, 1408, 1024, "ffn_up_dw")
    dycat = _mm(dy, w_out_b, "nt", f32, 512, 1024, 1024, "out_proj_dx")
    g_out = jnp.concatenate([_mm(y_attn, dy, "tn", f32, GROUP_W, 1024, 1024, "out_proj_dw_attn"),
                             _mm(y_delta, dy, "tn", f32, GROUP_W, 1024, 1024, "out_proj_dw_delta")], axis=0)
    dq, dk, dv, dbias = _attn_bwd(proj, bias, y_attn, lse, dycat)
    g_rb = _bias_grad(dbias, idx)[:, :, 0].T
    do, dz, dng = _gnorm_bwd(o, proj, ng_e, dycat)
    dso = _delta_scan_bwd(w, qt, kh, qk, gm, do)
    dqn, dkn, dvd, dbeta, dgd = _delta_chunk_bwd(qn, kn, sconv, beta, g, tinv, ss, dso, do)
    (dsq, dsk, dba, dal, ddt), (r_out,) = _delta_prep_bwd(
        sconv, proj, alog_e, dt_e, dqn, dkn, dbeta, dgd, _Exchange([_row_blocks(g_out)], gather=False))
    (dxc, g_conv), (r_gate, r_up) = _conv_silu_bwd(
        proj, conv_full, (dsq, dsk, dvd),
        _Exchange([_col_blocks(g_gate, n_ff), _col_blocks(g_up, n_ff)], gather=False))
    pieces = ((dq, 0), (dk, GROUP_W), (dv, 2 * GROUP_W), (dxc, DELTA_COL), (dz, Z_COL), (dba, BA_BLOCK * 128))
    g_in = jnp.concatenate(
        [_mm(h1, p, "tn", f32, 1024, min(p.shape[1], 768), 1024, "in_proj_dw_%d" % c) for p, c in pieces], axis=1)
    (gx, dsh1, dsc1, dnag), (r_in,) = _proj_ln_mod_bwd(
        [(p, w_in_p[:, c:c + p.shape[1]]) for p, c in pieces], x, nag, sc1, dx1, TOK_TILE, "in_proj_dx_ln1",
        _Exchange([_col_blocks(g_in[:, :IN_WIDTH], IN_WIDTH // N_DEV)], gather=False))
    grads = dict(
        x=gx, mod=jnp.concatenate([dsh1, dsc1, dg1, dsh2, dsc2, dg2], axis=1),
        norm_attn_g=dnag, norm_ffn_g=dnfg, final_norm_g=dfg, rel_bias=g_rb, conv_w=g_conv,
        a_log=dal.reshape(N_HEADS, HEAD_DIM).sum(-1), dt_bias=ddt.reshape(N_HEADS, HEAD_DIM).sum(-1),
        delta_norm_g=dng.reshape(N_HEADS, HEAD_DIM).sum(0),
        w_in=r_in, w_out=r_out, w_gate=r_gate, w_up=r_up, w_down=r_down)
    return loss[0, 0], grads


def _misc_row(rel_bias, a_log, dt_bias, delta_norm_g):
    flat = jnp.concatenate([rel_bias.reshape(-1), a_log.reshape(-1), dt_bias.reshape(-1), delta_norm_g.reshape(-1)])
    return jnp.pad(flat, (0, D_MODEL - flat.shape[0]))[None]


def _pack_small(b_ada, nag, nfg, fng, rel_bias, a_log, dt_bias, dng, conv_shard):
    rows = [b_ada.reshape(6, D_MODEL), nag.reshape(1, D_MODEL), nfg.reshape(1, D_MODEL), fng.reshape(1, D_MODEL),
            _misc_row(rel_bias, a_log, dt_bias, dng),
            jnp.pad(conv_shard.reshape(-1), (0, D_MODEL - conv_shard.size))[None],
            jnp.zeros((5, D_MODEL), f32)]
    return jnp.concatenate(rows, axis=0)


def _unpack_small(p, conv_shape):
    misc = p[9]
    return dict(
        b_ada=p[0:6].reshape(1, 6 * D_MODEL), norm_attn_g=p[6:7], norm_ffn_g=p[7:8], final_norm_g=p[8],
        rel_bias=misc[0:256].reshape(N_BUCKETS, N_HEADS), a_log=misc[256:264].reshape(1, N_HEADS),
        dt_bias=misc[264:272].reshape(1, N_HEADS), delta_norm_g=misc[272:336].reshape(1, HEAD_DIM),
        conv_w=p[10, :conv_shape[1] * conv_shape[2]].reshape(conv_shape))


def kernel(x, c, w_ada, b_ada, norm_attn_g, w_in, rel_bias, conv_w, a_log, dt_bias, delta_norm_g, w_out, norm_ffn_g, w_gate, w_up, w_down, final_norm_g, loss_target, m_w_ada, m_b_ada, m_norm_attn_g, m_w_in, m_rel_bias, m_conv_w, m_a_log, m_dt_bias, m_delta_norm_g, m_w_out, m_norm_ffn_g, m_w_gate, m_w_up, m_w_down, m_final_norm_g, v_w_ada, v_b_ada, v_norm_attn_g, v_w_in, v_rel_bias, v_conv_w, v_a_log, v_dt_bias, v_delta_norm_g, v_w_out, v_norm_ffn_g, v_w_gate, v_w_up, v_w_down, v_final_norm_g):
    me = 4 * lax.axis_index("x") + 2 * lax.axis_index("y") + lax.axis_index("c")
    ada_sh = w_ada.shape[2]
    conv_sh = conv_w.shape[2]

    cv = jnp.concatenate([c[0], conv_w[0].reshape(-1)])
    cv8 = jnp.zeros((8, 2 * D_MODEL), f32).at[0, :cv.shape[0]].set(cv)
    b8 = jnp.broadcast_to(b_ada.reshape(N_DEV, 1, ada_sh), (N_DEV, 8, ada_sh))
    call, modp = _ada_exchange(cv8, w_ada[0], b8)
    mod = modp[:, 0, :].reshape(1, 6 * D_MODEL)
    c_all = call[:, 0, :D_MODEL]
    conv_full = call[:, 0, D_MODEL:D_MODEL + CONV_WIDTH * conv_sh].reshape(N_DEV, CONV_WIDTH, conv_sh)
    conv_full = conv_full.transpose(1, 0, 2).reshape(CONV_WIDTH, N_DEV * conv_sh)

    loss_local, gr = _local_step(x[0], loss_target[0], mod, norm_attn_g, w_in[0].astype(bf16), rel_bias, conv_full, a_log,
                                 dt_bias, delta_norm_g, norm_ffn_g, final_norm_g, w_out[0].astype(bf16),
                                 w_gate[0].astype(bf16), w_up[0].astype(bf16), w_down[0].astype(bf16))
    loss = lax.psum(loss_local, ("x", "y", "c"))

    small = jnp.concatenate([
        gr["mod"].reshape(6, D_MODEL), gr["norm_attn_g"], gr["norm_ffn_g"], gr["final_norm_g"],
        gr["conv_w"].reshape(6, D_MODEL),
        _misc_row(gr["rel_bias"], gr["a_log"], gr["dt_bias"], gr["delta_norm_g"])], axis=0)
    parts = _all_to_all([jnp.broadcast_to(small[None], (N_DEV,) + small.shape)], "small_gather")[0]
    tot = _sum_devices(parts, "small_sum")
    g_conv_full = tot[9:15].reshape(CONV_WIDTH, N_DEV * conv_sh)
    g_conv = lax.dynamic_slice(g_conv_full, (0, me * conv_sh), (CONV_WIDTH, conv_sh))
    misc = tot[15]
    g_small = _pack_small(tot[0:6], tot[6], tot[7], tot[8], misc[0:256], misc[256:264], misc[264:272],
                          misc[272:336], g_conv)
    pk = lambda pre: _pack_small(pre[0], pre[1], pre[2], pre[3], pre[4], pre[5], pre[6], pre[7], pre[8])
    w_small = pk((b_ada, norm_attn_g, norm_ffn_g, final_norm_g, rel_bias, a_log, dt_bias, delta_norm_g, conv_w))
    m_small = pk((m_b_ada, m_norm_attn_g, m_norm_ffn_g, m_final_norm_g, m_rel_bias, m_a_log, m_dt_bias,
                  m_delta_norm_g, m_conv_w))
    v_small = pk((v_b_ada, v_norm_attn_g, v_norm_ffn_g, v_final_norm_g, v_rel_bias, v_a_log, v_dt_bias,
                  v_delta_norm_g, v_conv_w))
    d_small, m2_small, v2_small = _adamw(w_small, g_small, m_small, v_small, "adamw_small")
    cshape = conv_w.shape
    G, Dl, M2, V2 = (_unpack_small(t, cshape) for t in (g_small, d_small, m2_small, v2_small))

    dmod_all = parts[:, 0:6, :].reshape(N_DEV, 6 * D_MODEL)
    dmod_cols = lax.dynamic_slice(dmod_all, (0, me * ada_sh), (N_DEV, ada_sh))
    g_ada = _ada_wgrad(c_all, dmod_cols)
    d_ada, m2_ada, v2_ada = _adamw(w_ada[0], g_ada, m_w_ada[0], v_w_ada[0], "adamw_w_ada")

    big = {}
    for name, w_, m_, v_ in (("w_in", w_in, m_w_in, v_w_in), ("w_out", w_out, m_w_out, v_w_out),
                             ("w_gate", w_gate, m_w_gate, v_w_gate), ("w_up", w_up, m_w_up, v_w_up),
                             ("w_down", w_down, m_w_down, v_w_down)):
        big[name] = [t[None] for t in _reduce_adamw(gr[name], w_[0], m_[0], v_[0], "reduce_adamw_" + name)]

    def leaf(i, name):
        if name == "w_ada":
            return (g_ada, d_ada, m2_ada, v2_ada)[i][None]
        if name in big:
            return big[name][i]
        return (G, Dl, M2, V2)[i][name]

    order = ["w_ada", "b_ada", "norm_attn_g", "w_in", "rel_bias", "conv_w", "a_log", "dt_bias", "delta_norm_g",
             "w_out", "norm_ffn_g", "w_gate", "w_up", "w_down", "final_norm_g"]
    outs = [loss, gr["x"][None]]
    for i in range(4):
        outs += [leaf(i, n) for n in order]
    return tuple(outs)
```

```python
import functools
import math

import jax
import jax.numpy as jnp
from jax import lax
from jax.experimental import pallas as pl
from jax.experimental.pallas import tpu as pltpu

f32 = jnp.float32
bf16 = jnp.bfloat16

D_MODEL = 1024
HEAD_DIM = 64
N_HEADS = 8
GROUP_W = 512
IN_WIDTH = 3600
IN_PAD = 3840
D_FF = 2816
EPS = 1e-6
NEG_INF = -1e30
BAND = 128
PAD_UNIT = 2048
DILATIONS = (1, 4, 16)
N_BUCKETS = 32
MAX_DISTANCE = 2048
CONV_WIDTH = 4
CHUNK = 64
N_DEV = 8
VMEM_LIMIT = 56 * 1024 * 1024

ADAM_LR, ADAM_B1, ADAM_B2, ADAM_EPS, ADAM_WD, ADAM_STEP = 0.001, 0.9, 0.999, 1e-08, 0.01, 10


def _cparams(sem):
    return pltpu.CompilerParams(dimension_semantics=sem, vmem_limit_bytes=VMEM_LIMIT)


def _mm(a, b, mode, out_dtype, tm, tn, tk, name, xchg=None):
    if mode == "nn":
        (m, k), (_, n) = a.shape, b.shape
        a_spec = pl.BlockSpec((tm, tk), lambda j, i, kk: (i, kk))
        b_spec = pl.BlockSpec((tk, tn), lambda j, i, kk: (kk, j))
        dims = (((1,), (0,)), ((), ()))
    elif mode == "nt":
        (m, k), (n, _) = a.shape, b.shape
        a_spec = pl.BlockSpec((tm, tk), lambda j, i, kk: (i, kk))
        b_spec = pl.BlockSpec((tn, tk), lambda j, i, kk: (j, kk))
        dims = (((1,), (1,)), ((), ()))
    else:
        (k, m), (_, n) = a.shape, b.shape
        a_spec = pl.BlockSpec((tk, tm), lambda j, i, kk: (kk, i))
        b_spec = pl.BlockSpec((tk, tn), lambda j, i, kk: (kk, j))
        dims = (((0,), (0,)), ((), ()))
    assert m % tm == 0 and n % tn == 0 and k % tk == 0, (name, m, n, k, tm, tn, tk)
    nk = k // tk
    grid = (n // tn, m // tm, nk)
    nx = xchg.n if xchg is not None else 0

    def body(*refs):
        a_ref, b_ref = refs[:2]
        o_ref = refs[2 + nx]
        scratch = refs[3 + 2 * nx:]
        if nx:
            xrefs = (refs[2:2 + nx], refs[3 + nx:3 + 2 * nx], scratch[-3:])
            xchg.start_at_first_step(grid, *xrefs)
        if nk == 1:
            o_ref[...] = lax.dot_general(a_ref[...].astype(bf16), b_ref[...].astype(bf16), dims,
                                         preferred_element_type=f32).astype(o_ref.dtype)
        else:
            acc_ref = scratch[0]
            kk = pl.program_id(2)

            @pl.when(kk == 0)
            def _():
                acc_ref[...] = jnp.zeros_like(acc_ref)

            acc_ref[...] += lax.dot_general(a_ref[...].astype(bf16), b_ref[...].astype(bf16), dims,
                                            preferred_element_type=f32)

            @pl.when(kk == nk - 1)
            def _():
                o_ref[...] = acc_ref[...].astype(o_ref.dtype)
        if nx:
            xchg.wait_at_last_step(grid, *xrefs)

    out = pl.pallas_call(
        body, name=name, grid=grid,
        in_specs=[a_spec, b_spec] + ([_ANY] * nx),
        out_specs=[pl.BlockSpec((tm, tn), lambda j, i, kk: (i, j))] + ([_ANY] * nx),
        out_shape=[jax.ShapeDtypeStruct((m, n), out_dtype)] + (xchg.out_shape() if nx else []),
        scratch_shapes=([pltpu.VMEM((tm, tn), f32)] if nk > 1 else []) + (xchg.scratch() if nx else []),
        compiler_params=_cparams(("arbitrary",) * 3 if nx else ("parallel", "parallel", "arbitrary")),
    )(a, b, *(xchg.arrs if nx else []))
    return (out[0], out[1:]) if nx else out[0]


TOK_TILE = 512
SUB_COLS = 384


def _row_spec(width, tile=TOK_TILE):
    return pl.BlockSpec((tile, width), lambda i: (i, 0))


def _vec_spec(width, rows=1):
    return pl.BlockSpec((rows, width), lambda i: (0, 0))


def _ln_mod_fwd(x, gain, sc, sh, shard, name):
    s, d = x.shape
    nt = s // TOK_TILE
    ride = _ChipGather(shard)

    def body(x_ref, g_ref, sc_ref, sh_ref, sh_in, h_ref, sh_out, *sems):
        i = pl.program_id(0)
        pl.when(i == 0)(lambda: ride.start(sh_in, sh_out, sems))
        xv = x_ref[...]
        rstd = lax.rsqrt(jnp.mean(xv * xv, axis=-1, keepdims=True) + EPS)
        h = (xv * rstd) * g_ref[...] * (1.0 + sc_ref[...]) + sh_ref[...]
        h_ref[...] = h.astype(bf16)
        @pl.when(i == nt - 1)
        def _():
            ride.forward(sh_in, sh_out, sems)
            ride.finish(sh_in, sh_out, sems)

    return pl.pallas_call(
        body, name=name, grid=(nt,),
        in_specs=[_row_spec(d), _vec_spec(d), _vec_spec(d), _vec_spec(d), _ANY],
        out_specs=[_row_spec(d), _ANY],
        out_shape=[jax.ShapeDtypeStruct((s, d), bf16), ride.out_shape()],
        scratch_shapes=ride.scratch(),
        compiler_params=_cparams(("arbitrary",)),
    )(x, gain, sc, sh, shard)


def _proj_resid_ln_mod_fwd(pairs, x, gate, gain, sc, sh, name):
    s, d = x.shape
    npair = len(pairs)

    def body(*refs):
        aw = refs[:2 * npair]
        x_ref, gt_ref, g_ref, sc_ref, sh_ref, y_ref, x1_ref, h_ref = refs[2 * npair:]
        y = jnp.dot(aw[0][...].astype(bf16), aw[1][...], preferred_element_type=f32)
        for t in range(1, npair):
            y = y + jnp.dot(aw[2 * t][...].astype(bf16), aw[2 * t + 1][...], preferred_element_type=f32)
        y_ref[...] = y
        x1 = x_ref[...] + gt_ref[...] * y
        x1_ref[...] = x1
        rstd = lax.rsqrt(jnp.mean(x1 * x1, axis=-1, keepdims=True) + EPS)
        h = (x1 * rstd) * g_ref[...] * (1.0 + sc_ref[...]) + sh_ref[...]
        h_ref[...] = h.astype(bf16)

    aw_specs, aw = [], []
    for a, w in pairs:
        aw_specs += [_row_spec(a.shape[1]), pl.BlockSpec(w.shape, lambda i: (0, 0))]
        aw += [a, w]
    return pl.pallas_call(
        body, name=name, grid=(s // TOK_TILE,),
        in_specs=aw_specs + [_row_spec(d)] + [_vec_spec(d)] * 4,
        out_specs=[_row_spec(d)] * 3,
        out_shape=[jax.ShapeDtypeStruct((s, d), f32)] * 2 + [jax.ShapeDtypeStruct((s, d), bf16)],
        compiler_params=_cparams(("parallel",)),
    )(*aw, x, gate, gain, sc, sh)


FFN_TN = 1408


def _ffn_up(h2, w_gate, w_up, name):
    s, d = h2.shape
    tm = TOK_TILE

    def body(h_ref, wg_ref, wu_ref, a_ref, g_ref, u_ref):
        h = h_ref[...]
        g = jnp.dot(h, wg_ref[...], preferred_element_type=f32)
        u = jnp.dot(h, wu_ref[...], preferred_element_type=f32)
        a_ref[...] = (g * jax.nn.sigmoid(g) * u).astype(bf16)
        g_ref[...] = g.astype(bf16)
        u_ref[...] = u.astype(bf16)

    w_spec = pl.BlockSpec((d, FFN_TN), lambda j, i: (0, j))
    o_spec = pl.BlockSpec((tm, FFN_TN), lambda j, i: (i, j))
    return pl.pallas_call(
        body, name=name, grid=(D_FF // FFN_TN, s // tm),
        in_specs=[pl.BlockSpec((tm, d), lambda j, i: (i, 0)), w_spec, w_spec],
        out_specs=[o_spec] * 3,
        out_shape=[jax.ShapeDtypeStruct((s, D_FF), bf16)] * 3,
        compiler_params=_cparams(("parallel", "parallel")),
    )(h2, w_gate, w_up)


def _ffn_down_dx(dy2, w_down, gate, up, name):
    s, d = dy2.shape
    tm = TOK_TILE

    def body(dy_ref, w_ref, g_ref, u_ref, dg_ref, du_ref):
        dy = dy_ref[...]
        for c0 in range(0, FFN_TN, SUB_COLS):
            cols = slice(c0, min(c0 + SUB_COLS, FFN_TN))
            da = lax.dot_general(dy, w_ref[cols, :], _NT, preferred_element_type=f32)
            g = g_ref[:, cols].astype(f32)
            sg = jax.nn.sigmoid(g)
            du_ref[:, cols] = (da * g * sg).astype(bf16)
            dg_ref[:, cols] = (da * u_ref[:, cols].astype(f32) * sg * (1.0 + g * (1.0 - sg))).astype(bf16)

    t_spec = pl.BlockSpec((tm, FFN_TN), lambda j, i: (i, j))
    return pl.pallas_call(
        body, name=name, grid=(D_FF // FFN_TN, s // tm),
        in_specs=[pl.BlockSpec((tm, d), lambda j, i: (i, 0)), pl.BlockSpec((FFN_TN, d), lambda j, i: (j, 0)),
                  t_spec, t_spec],
        out_specs=[t_spec, t_spec],
        out_shape=[jax.ShapeDtypeStruct((s, D_FF), bf16)] * 2,
        compiler_params=_cparams(("parallel", "parallel")),
    )(dy2, w_down, gate, up)


def _acc_spec(width):
    return pl.BlockSpec((1, width), lambda i: (0, 0))


def _proj_final_loss_bwd(a, w, x1, gate2, final_g, target, name):
    s, d = x1.shape
    k = a.shape[1]

    def body(a_ref, w_ref, x1_ref, gt_ref, fg_ref, tg_ref, dx2_ref, dy2_ref, loss_ref, dfg_ref, dgt_ref):
        @pl.when(pl.program_id(0) == 0)
        def _():
            loss_ref[...] = jnp.zeros_like(loss_ref)
            dfg_ref[...] = jnp.zeros_like(dfg_ref)
            dgt_ref[...] = jnp.zeros_like(dgt_ref)

        y2 = jnp.dot(a_ref[...], w_ref[...], preferred_element_type=f32)
        gt = gt_ref[...]
        fg = fg_ref[...]
        x2 = x1_ref[...] + gt * y2
        rstd = lax.rsqrt(jnp.mean(x2 * x2, axis=-1, keepdims=True) + EPS)
        xn = x2 * rstd
        err = xn * fg - tg_ref[...]
        row = jnp.sum(err * err, axis=-1, keepdims=True) * (0.5 / d)
        loss_ref[...] += jnp.sum(row, axis=0, keepdims=True) + jnp.zeros_like(loss_ref)
        dout = err * (1.0 / d)
        dfg_ref[...] += jnp.sum(dout * xn, axis=0, keepdims=True)
        dxn = dout * fg
        dx2 = rstd * (dxn - xn * jnp.mean(dxn * xn, axis=-1, keepdims=True))
        dx2_ref[...] = dx2
        dgt_ref[...] += jnp.sum(dx2 * y2, axis=0, keepdims=True)
        dy2_ref[...] = (gt * dx2).astype(bf16)

    return pl.pallas_call(
        body, name=name, grid=(s // TOK_TILE,),
        in_specs=[_row_spec(k), pl.BlockSpec((k, d), lambda i: (0, 0)), _row_spec(d), _vec_spec(d), _vec_spec(d),
                  _row_spec(d)],
        out_specs=[_row_spec(d), _row_spec(d), _acc_spec(128), _acc_spec(d), _acc_spec(d)],
        out_shape=[jax.ShapeDtypeStruct((s, d), f32), jax.ShapeDtypeStruct((s, d), bf16),
                   jax.ShapeDtypeStruct((1, 128), f32), jax.ShapeDtypeStruct((1, d), f32),
                   jax.ShapeDtypeStruct((1, d), f32)],
        compiler_params=_cparams(("arbitrary",)),
    )(a, w, x1, gate2, final_g, target)


def _proj_ln_mod_bwd(pairs, xin, gain, sc, dres, tm, name, xchg, gate=None, y=None):
    s, d = xin.shape
    with_gate = gate is not None
    npair = len(pairs)
    n_in = 2 * npair + (7 if with_gate else 5) - 1
    n_out = 6 if with_gate else 4

    def body(*refs):
        ab = refs[:2 * npair]
        if with_gate:
            (x_ref, g_ref, sc_ref, dr_ref, gt_ref, y_ref,
             dx_ref, dsh_ref, dsc_ref, dg_ref, dy_ref, dgt_ref) = refs[2 * npair:]
        else:
            x_ref, g_ref, sc_ref, dr_ref, dx_ref, dsh_ref, dsc_ref, dg_ref = refs[2 * npair:]

        @pl.when(pl.program_id(0) == 0)
        def _():
            dsh_ref[...] = jnp.zeros_like(dsh_ref)
            dsc_ref[...] = jnp.zeros_like(dsc_ref)
            dg_ref[...] = jnp.zeros_like(dg_ref)
            if with_gate:
                dgt_ref[...] = jnp.zeros_like(dgt_ref)

        dh = lax.dot_general(ab[0][...].astype(bf16), ab[1][...], _NT, preferred_element_type=f32)
        for t in range(1, npair):
            dh = dh + lax.dot_general(ab[2 * t][...].astype(bf16), ab[2 * t + 1][...], _NT,
                                      preferred_element_type=f32)
        xv = x_ref[...]
        g = g_ref[...]
        sc1 = 1.0 + sc_ref[...]
        rstd = lax.rsqrt(jnp.mean(xv * xv, axis=-1, keepdims=True) + EPS)
        xn = xv * rstd
        dsh_ref[...] += jnp.sum(dh, axis=0, keepdims=True)
        dsc_ref[...] += jnp.sum(dh * (xn * g), axis=0, keepdims=True)
        dg_ref[...] += jnp.sum(dh * sc1 * xn, axis=0, keepdims=True)
        dxn = dh * sc1 * g
        dx = dr_ref[...] + rstd * (dxn - xn * jnp.mean(dxn * xn, axis=-1, keepdims=True))
        dx_ref[...] = dx
        if with_gate:
            dgt_ref[...] += jnp.sum(dx * y_ref[...], axis=0, keepdims=True)
            dy_ref[...] = (gt_ref[...] * dx).astype(bf16)

    row = lambda width: pl.BlockSpec((tm, width), lambda i: (i, 0))
    in_specs, args = [], []
    for a, b in pairs:
        in_specs += [row(a.shape[1]), pl.BlockSpec(b.shape, lambda i: (0, 0))]
        args += [a, b]
    in_specs += [row(d), _vec_spec(d), _vec_spec(d), row(d)]
    args += [xin, gain, sc, dres]
    out_specs = [row(d), _acc_spec(d), _acc_spec(d), _acc_spec(d)]
    out_shape = [jax.ShapeDtypeStruct((s, d), f32)] + [jax.ShapeDtypeStruct((1, d), f32)] * 3
    if with_gate:
        in_specs += [_vec_spec(d), row(d)]
        out_specs += [row(d), _acc_spec(d)]
        out_shape += [jax.ShapeDtypeStruct((s, d), bf16), jax.ShapeDtypeStruct((1, d), f32)]
        args += [gate, y]
    grid = (s // tm,)
    out = pl.pallas_call(
        _ride(body, n_in, n_out, xchg, grid), name=name, grid=grid,
        in_specs=in_specs + [_ANY] * xchg.n, out_specs=out_specs + [_ANY] * xchg.n,
        out_shape=out_shape + xchg.out_shape(), scratch_shapes=xchg.scratch(),
        compiler_params=_cparams(("arbitrary",)),
    )(*args, *xchg.arrs)
    return out[:n_out], out[n_out:]


def _bucket_tables():
    import numpy as np
    qi = np.arange(BAND)[:, None]
    kj = np.arange(2 * BAND)[None, :]
    steps = qi + BAND - kj
    max_exact = N_BUCKETS // 2
    out = []
    for d in DILATIONS:
        dist = np.maximum(steps, 0) * d
        dist_f = np.maximum(dist, 1).astype(np.float32)
        large = max_exact + (np.log(dist_f / np.float32(max_exact)) / np.float32(math.log(MAX_DISTANCE / max_exact))
                             * np.float32(N_BUCKETS - max_exact)).astype(np.int32)
        out.append(np.where(dist < max_exact, dist, np.minimum(large, N_BUCKETS - 1)))
    return jnp.asarray(np.stack(out).astype(np.int32))


def _bias_tables(rel_bias, idx):
    def body(idx_ref, rb_ref, o_ref):
        h = pl.program_id(1)
        idxv = idx_ref[0]
        acc = jnp.zeros((BAND, 2 * BAND), f32)
        for b in range(N_BUCKETS):
            acc = jnp.where(idxv == b, rb_ref[b, h], acc)
        o_ref[0, 0] = jnp.where(_attn_masks()[1], acc, NEG_INF)

    return pl.pallas_call(
        body, name="attn_bias_tables", grid=(3, N_HEADS),
        in_specs=[pl.BlockSpec((1, BAND, 2 * BAND), lambda br, h: (br, 0, 0)),
                  pl.BlockSpec(memory_space=pltpu.SMEM)],
        out_specs=pl.BlockSpec((1, 1, BAND, 2 * BAND), lambda br, h: (br, h, 0, 0)),
        out_shape=jax.ShapeDtypeStruct((3, N_HEADS, BAND, 2 * BAND), f32),
        compiler_params=_cparams(("parallel", "parallel")),
    )(idx, rel_bias)


def _bias_grad(dbias, idx):
    def body(idx_ref, db_ref, o_ref):
        br = pl.program_id(1)

        @pl.when(br == 0)
        def _():
            o_ref[...] = jnp.zeros_like(o_ref)

        idxv = idx_ref[0]
        dbv = db_ref[0, 0]
        row = lax.broadcasted_iota(jnp.int32, (N_BUCKETS, 128), 0)
        acc = jnp.zeros((N_BUCKETS, 128), f32)
        for b in range(N_BUCKETS):
            sb = jnp.sum(jnp.sum(jnp.where(idxv == b, dbv, 0.0), axis=1, keepdims=True), axis=0, keepdims=True)
            acc = acc + jnp.where(row == b, sb, 0.0)
        o_ref[0] += acc

    return pl.pallas_call(
        body, name="attn_bias_grad", grid=(N_HEADS, 3),
        in_specs=[pl.BlockSpec((1, BAND, 2 * BAND), lambda h, br: (br, 0, 0)),
                  pl.BlockSpec((1, 1, BAND, 2 * BAND), lambda h, br: (br, h, 0, 0))],
        out_specs=pl.BlockSpec((1, N_BUCKETS, 128), lambda h, br: (h, 0, 0)),
        out_shape=jax.ShapeDtypeStruct((N_HEADS, N_BUCKETS, 128), f32),
        compiler_params=_cparams(("parallel", "arbitrary")),
    )(idx, dbias)


def _attn_masks():
    lane = lax.broadcasted_iota(jnp.int32, (BAND, 128), 1)
    m0 = lane < HEAD_DIM
    qi = lax.broadcasted_iota(jnp.int32, (BAND, 2 * BAND), 0)
    kj = lax.broadcasted_iota(jnp.int32, (BAND, 2 * BAND), 1)
    steps = qi + BAND - kj
    in_window = (steps >= 0) & (steps <= BAND)
    return m0, in_window, kj >= BAND


_NT = (((1,), (1,)), ((), ()))
_TN = (((0,), (0,)), ((), ()))
_BNN = (((2,), (1,)), ((0,), (0,)))
_BNT = (((2,), (2,)), ((0,), (0,)))
_BTN = (((1,), (1,)), ((0,), (0,)))
ATTN_GROUP = 4
ATTN_ITEMS = PAD_UNIT // BAND
Q_COL, K_COL, V_COL = 0, 4, 8


def _attn_item_rows(j, d, c, cbase):
    r = lax.rem(j, d)
    b = lax.div(j, d)
    loc = b * (d * BAND) + r
    first = jnp.logical_and(c == 0, b == 0)
    start = cbase + loc
    pstart = jnp.where(first, start, start - d * BAND)
    return loc, start, pstart, first


def _attn_fwd(proj, bias, xchg):
    s = proj.shape[0]

    def body(q_ref, k_ref, v_ref, b_ref, y_ref, lse_ref, o_s, l_s):
        c = pl.program_id(1)
        cbase = pl.multiple_of(c * PAD_UNIT, PAD_UNIT)
        m0, in_window, cur_half = _attn_masks()
        for bi, d in enumerate(DILATIONS):
            def group(jg, carry, bi=bi, d=d):
                locs, qs, ks, vs, pens = [], [], [], [], []
                for t in range(ATTN_GROUP):
                    loc, start, pstart, first = _attn_item_rows(jg * ATTN_GROUP + t, d, c, cbase)
                    locs.append(loc)
                    qs.append(q_ref[pl.ds(loc, BAND, stride=d), :])
                    ks.append(jnp.concatenate([k_ref[pl.ds(pstart, BAND, stride=d), :],
                                               k_ref[pl.ds(start, BAND, stride=d), :]], axis=0))
                    vs.append(jnp.concatenate([v_ref[pl.ds(pstart, BAND, stride=d), :],
                                               v_ref[pl.ds(start, BAND, stride=d), :]], axis=0))
                    pens.append(jnp.where(cur_half, 0.0, jnp.where(first, NEG_INF, 0.0)))
                q = jnp.stack(qs)
                kk = jnp.stack(ks + ks).astype(bf16)
                vv = jnp.stack(vs + vs).astype(bf16)
                pen = jnp.stack(pens + pens)
                qh = (jnp.concatenate([jnp.where(m0, q, 0.0), jnp.where(m0, 0.0, q)], axis=0) * 0.125).astype(bf16)
                sc = lax.dot_general(qh, kk, _BNT, preferred_element_type=f32)
                sc = (sc.reshape(2, ATTN_GROUP, BAND, 2 * BAND) + b_ref[bi][:, None]).reshape(sc.shape) + pen
                mx = jnp.max(sc, axis=-1, keepdims=True)
                e = jnp.exp(sc - mx)
                l = jnp.sum(e, axis=-1, keepdims=True)
                o = lax.dot_general(e.astype(bf16), vv, _BNN, preferred_element_type=f32) * (1.0 / l)
                ls = mx + jnp.log(l)
                for t in range(ATTN_GROUP):
                    rows = pl.ds(locs[t], BAND, stride=d)
                    o_s[bi, rows, :] = jnp.where(m0, o[t], o[ATTN_GROUP + t])
                    l_s[bi, rows, :] = jnp.where(m0, ls[t], ls[ATTN_GROUP + t])
                return carry

            lax.fori_loop(0, ATTN_ITEMS // ATTN_GROUP, group, 0)

        def merge(t, carry):
            rows = pl.ds(pl.multiple_of(t * 256, 256), 256)
            ls = [l_s[i, rows, :] for i in range(3)]
            mx = jnp.maximum(jnp.maximum(ls[0], ls[1]), ls[2])
            ws = [jnp.exp(l - mx) for l in ls]
            tot = ws[0] + ws[1] + ws[2]
            y = (ws[0] * o_s[0, rows, :] + ws[1] * o_s[1, rows, :] + ws[2] * o_s[2, rows, :]) / tot
            y_ref[rows, :] = y
            lse_ref[rows, :] = mx + jnp.log(tot)
            return carry

        lax.fori_loop(0, PAD_UNIT // 256, merge, 0)

    chunk = lambda col: pl.BlockSpec((PAD_UNIT, 128), lambda p, c: (c, col + p))
    full = lambda col: pl.BlockSpec((s, 128), lambda p, c: (0, col + p))
    grid = (N_HEADS // 2, s // PAD_UNIT)
    out = pl.pallas_call(
        _ride(body, 4, 2, xchg, grid), name="attn_fwd", grid=grid,
        in_specs=[chunk(Q_COL), full(K_COL), full(V_COL),
                  pl.BlockSpec((3, 2, BAND, 2 * BAND), lambda p, c: (0, p, 0, 0))] + [_ANY] * xchg.n,
        out_specs=[chunk(0), chunk(0)] + [_ANY] * xchg.n,
        out_shape=[jax.ShapeDtypeStruct((s, GROUP_W), f32)] * 2 + xchg.out_shape(),
        scratch_shapes=[pltpu.VMEM((3, PAD_UNIT, 128), f32)] * 2 + xchg.scratch(),
        compiler_params=_cparams(("arbitrary", "arbitrary")),
    )(proj, proj, proj, bias, *xchg.arrs)
    return out[:2], out[2:]


def _attn_bwd(proj, bias, y, lse, dycat):
    s = proj.shape[0]

    def body(q_ref, k_ref, v_ref, b_ref, y_ref, lse_ref, dy_ref, dq_ref, dk_ref, dv_ref, db_ref, dd_s):
        c = pl.program_id(1)
        cbase = pl.multiple_of(c * PAD_UNIT, PAD_UNIT)
        m0, in_window, cur_half = _attn_masks()

        @pl.when(c == 0)
        def _():
            dk_ref[...] = jnp.zeros_like(dk_ref)
            dv_ref[...] = jnp.zeros_like(dv_ref)
            db_ref[...] = jnp.zeros_like(db_ref)

        dq_ref[...] = jnp.zeros_like(dq_ref)

        def rowdot(t, carry):
            rows = pl.ds(pl.multiple_of(t * 256, 256), 256)
            prod = dy_ref[rows, :] * y_ref[rows, :]
            lane = lax.broadcasted_iota(jnp.int32, prod.shape, 1)
            h0 = lane < HEAD_DIM
            d0 = jnp.sum(jnp.where(h0, prod, 0.0), axis=-1, keepdims=True)
            d1 = jnp.sum(jnp.where(h0, 0.0, prod), axis=-1, keepdims=True)
            dd_s[rows, :] = jnp.where(h0, d0, d1)
            return carry

        lax.fori_loop(0, PAD_UNIT // 256, rowdot, 0)

        for bi, d in enumerate(DILATIONS):
            def group(jg, carry, bi=bi, d=d):
                ng = ATTN_GROUP
                meta, qs, dos, lqs, dds, ks, vs, pens = [], [], [], [], [], [], [], []
                for t in range(ng):
                    loc, start, pstart, first = _attn_item_rows(jg * ng + t, d, c, cbase)
                    qrows = pl.ds(loc, BAND, stride=d)
                    rows = pl.ds(start, BAND, stride=d)
                    prows = pl.ds(pstart, BAND, stride=d)
                    meta.append((qrows, rows, prows))
                    qs.append(q_ref[qrows, :])
                    dos.append(dy_ref[qrows, :])
                    lqs.append(lse_ref[qrows, :])
                    dds.append(dd_s[qrows, :])
                    ks.append(jnp.concatenate([k_ref[prows, :], k_ref[rows, :]], axis=0))
                    vs.append(jnp.concatenate([v_ref[prows, :], v_ref[rows, :]], axis=0))
                    pens.append(jnp.where(cur_half, 0.0, jnp.where(first, NEG_INF, 0.0)))

                def heads(t):
                    return jnp.concatenate([jnp.where(m0, t, 0.0), jnp.where(m0, 0.0, t)], axis=0)

                def head_col(t):
                    return jnp.concatenate([t[:, :, 0:1], t[:, :, HEAD_DIM:HEAD_DIM + 1]], axis=0)

                qh = (heads(jnp.stack(qs)) * 0.125).astype(bf16)
                doh = heads(jnp.stack(dos)).astype(bf16)
                kk = jnp.stack(ks + ks).astype(bf16)
                vv = jnp.stack(vs + vs).astype(bf16)
                sc = lax.dot_general(qh, kk, _BNT, preferred_element_type=f32)
                sc = (sc.reshape(2, ng, BAND, 2 * BAND) + b_ref[bi][:, None]).reshape(sc.shape) + jnp.stack(pens + pens)
                p = jnp.exp(sc - head_col(jnp.stack(lqs)))
                dp = lax.dot_general(doh, vv, _BNT, preferred_element_type=f32)
                ds = p * (dp - head_col(jnp.stack(dds)))
                db_ref[bi] += jnp.sum(ds.reshape(2, ng, BAND, 2 * BAND), axis=1)
                dsb = ds.astype(bf16)
                dq = lax.dot_general(dsb, kk, _BNN, preferred_element_type=f32) * 0.125
                dk = lax.dot_general(dsb, qh, _BTN, preferred_element_type=f32)
                dv = lax.dot_general(p.astype(bf16), doh, _BTN, preferred_element_type=f32)
                for t in range(ng):
                    qrows, rows, prows = meta[t]
                    dq_ref[qrows, :] += jnp.where(m0, dq[t], dq[ng + t])
                    dkt = dk[t] + dk[ng + t]
                    dvt = dv[t] + dv[ng + t]
                    dk_ref[prows, :] += dkt[:BAND]
                    dk_ref[rows, :] += dkt[BAND:]
                    dv_ref[prows, :] += dvt[:BAND]
                    dv_ref[rows, :] += dvt[BAND:]
                return carry

            lax.fori_loop(0, ATTN_ITEMS // ATTN_GROUP, group, 0)

    chunk = lambda col: pl.BlockSpec((PAD_UNIT, 128), lambda p, c: (c, col + p))
    full = lambda col: pl.BlockSpec((s, 128), lambda p, c: (0, col + p))
    bias_spec = pl.BlockSpec((3, 2, BAND, 2 * BAND), lambda p, c: (0, p, 0, 0))
    return pl.pallas_call(
        body, name="attn_bwd", grid=(N_HEADS // 2, s // PAD_UNIT),
        in_specs=[chunk(Q_COL), full(K_COL), full(V_COL), bias_spec, chunk(0), chunk(0), chunk(0)],
        out_specs=[chunk(0), full(0), full(0), bias_spec],
        out_shape=[jax.ShapeDtypeStruct((s, GROUP_W), f32)] * 3
        + [jax.ShapeDtypeStruct((3, N_HEADS, BAND, 2 * BAND), f32)],
        scratch_shapes=[pltpu.VMEM((PAD_UNIT, 128), f32)],
        compiler_params=_cparams(("parallel", "arbitrary")),
    )(proj, proj, proj, bias, y, lse, dycat)


_HI = lax.Precision.HIGHEST
DELTA_COL = 1536
Z_COL = 3072
BA_BLOCK = 28
DELTA_ROWS = 1024


def _hdot(a, b):
    return jnp.dot(a, b, precision=_HI, preferred_element_type=f32)


_DIMS = dict(nn=(((2,), (1,)), ((0,), (0,))), nt=(((2,), (2,)), ((0,), (0,))), tn=(((1,), (1,)), ((0,), (0,))))


@functools.partial(jax.custom_vjp, nondiff_argnums=(2,))
def _mmx(a, b, mode):
    return lax.dot_general(a.astype(bf16), b.astype(bf16), _DIMS[mode], preferred_element_type=f32)


def _mmx_fwd(a, b, mode):
    return _mmx(a, b, mode), (a, b)


def _mmx_bwd(mode, res, g):
    a, b = res
    if mode == "nn":
        return _mmx(g, b, "nt"), _mmx(a, g, "tn")
    if mode == "nt":
        return _mmx(g, b, "nn"), _mmx(g, a, "tn")
    return _mmx(b, g, "nt"), _mmx(a, g, "nn")


_mmx.defvjp(_mmx_fwd, _mmx_bwd)


def _pair_iota():
    row = lax.broadcasted_iota(jnp.int32, (CHUNK, 128), 0)
    lane = lax.broadcasted_iota(jnp.int32, (CHUNK, 128), 1)
    return row, lane, lane & (CHUNK - 1)


def _bd(x):
    _, lane, _ = _pair_iota()
    m0 = lane < CHUNK
    return jnp.concatenate([jnp.where(m0, x, 0.0), jnp.where(m0, 0.0, x)], axis=1)


def _pmm(a, b):
    return _mmx(a, _bd(b), "nn")


def _ntp(x, y):
    return _mmx(x, _bd(y), "nt")


def _tnp(x, y):
    full = _mmx(x, y, "tn")
    _, lane, _ = _pair_iota()
    return jnp.where(lane < CHUNK, full[:, :CHUNK], full[:, CHUNK:])


def _tri_inv(a):
    row, lane, jj = _pair_iota()
    eye = jnp.where(row == jj, 1.0, 0.0).astype(f32)

    def same_block(log2b):
        return (row >> log2b) == (jj >> log2b)

    dgl = jnp.where(same_block(3), a, 0.0)
    d2 = _pmm(dgl, dgl)
    d4 = _pmm(d2, d2)
    t = _pmm(_pmm(eye - dgl, eye + d2), eye + d4)
    for lb in (3, 4, 5):
        off = jnp.where(same_block(lb + 1) & jnp.logical_not(same_block(lb)), a, 0.0)
        t = t - _pmm(_pmm(t, off), t)
    return t


@jax.custom_vjp
def _solve2(a, xv, xk, t):
    return _pmm(t, xv), _pmm(t, xk)


def _solve2_fwd(a, xv, xk, t):
    u, w = _pmm(t, xv), _pmm(t, xk)
    return (u, w), (t, u, w)


def _solve2_bwd(res, cts):
    t, u, w = res
    du, dw = cts
    dxv = _tnp(t, du)
    dxk = _tnp(t, dw)
    return -(_ntp(dxv, u) + _ntp(dxk, w)), dxv, dxk, jnp.zeros_like(t)


_solve2.defvjp(_solve2_fwd, _solve2_bwd)


def _chunk_pre(qp, kp, vp, bp, gcum, t=None):
    row, lane, jj = _pair_iota()
    causal = row >= jj
    strict = row > jj
    rsel = jnp.sum(jnp.where(row == jj, gcum, 0.0), axis=1, keepdims=True)
    decay = jnp.where(causal, jnp.exp(jnp.where(causal, gcum - rsel, 0.0)), 0.0)
    kb = kp * bp
    kd = _bd(kp)
    a = jnp.where(strict, _mmx(kb, kd, "nt") * decay, 0.0)
    eg = jnp.exp(gcum)
    if t is None:
        t = _tri_inv(a)
    u, w = _solve2(a, vp * bp, kb * eg, t)
    qk = jnp.where(causal, _mmx(qp, kd, "nt") * decay, 0.0)
    glast = jnp.sum(jnp.where(row == CHUNK - 1, gcum, 0.0), axis=1, keepdims=True)
    return u, w, qp * eg, kp * jnp.exp(glast - gcum), qk, jnp.exp(glast), t


def _chunk_post(u, w, qt, kh, qk, gam, sp):
    sd = _bd(sp)
    vnew = u - _mmx(w, sd, "nn")
    o = _mmx(qt, sd, "nn") + _pmm(qk, vnew)
    return o, gam * sp + _tnp(kh, vnew)


def _pair_spec(rows=DELTA_ROWS):
    return pl.BlockSpec((rows, 128), lambda i, p: (i, p))


DELTA_NB = DELTA_ROWS // CHUNK


def _chunks(ref):
    return ref[...].reshape(DELTA_NB, CHUNK, 128)


def _pairs(ref, rows):
    return jnp.stack([ref[rows, p * 128:(p + 1) * 128] for p in range(4)], axis=0)


def _delta_chunk_pre(qn, kn, sv, beta, g, xchg):
    s = qn.shape[0]

    def body(q_ref, k_ref, v_ref, b_ref, g_ref, u_ref, w_ref, qt_ref, kh_ref, qk_ref, t_ref, gm_ref):
        outs = _chunk_pre(_chunks(q_ref), _chunks(k_ref), _chunks(v_ref), _chunks(b_ref), _chunks(g_ref))
        for ref, val in zip((u_ref, w_ref, qt_ref, kh_ref, qk_ref, t_ref), outs[:5] + outs[6:]):
            ref[...] = val.reshape(DELTA_ROWS, 128).astype(ref.dtype)
        gm_ref[...] = jnp.broadcast_to(outs[5], (DELTA_NB, 8, 128)).reshape(DELTA_NB * 8, 128)

    v_spec = pl.BlockSpec((DELTA_ROWS, 128), lambda i, p: (i, 8 + p))
    grid = (s // DELTA_ROWS, 4)
    out = pl.pallas_call(
        _ride(body, 5, 7, xchg, grid), name="delta_chunk_pre", grid=grid,
        in_specs=[_pair_spec(), _pair_spec(), v_spec, _pair_spec(), _pair_spec()] + [_ANY] * xchg.n,
        out_specs=[_pair_spec()] * 6 + [_pair_spec(DELTA_NB * 8)] + [_ANY] * xchg.n,
        out_shape=[jax.ShapeDtypeStruct((s, GROUP_W), f32)] + [jax.ShapeDtypeStruct((s, GROUP_W), bf16)] * 5
        + [jax.ShapeDtypeStruct((s // 8, GROUP_W), f32)] + xchg.out_shape(),
        scratch_shapes=xchg.scratch(),
        compiler_params=_cparams(("arbitrary", "arbitrary")),
    )(qn, kn, sv, beta, g, *xchg.arrs)
    return out[:7], out[7:]


def _delta_scan_fwd(u, w, qt, kh, qk, gm):
    s = u.shape[0]

    def body(u_ref, w_ref, qt_ref, kh_ref, qk_ref, gm_ref, o_ref, ss_ref, st):
        @pl.when(pl.program_id(0) == 0)
        def _():
            st[...] = jnp.zeros_like(st)

        def chunk(ci, carry):
            rows = pl.ds(pl.multiple_of(ci * CHUNK, CHUNK), CHUNK)
            grow = pl.ds(pl.multiple_of(ci * 8, 8), 1)
            sp = st[...]
            o, s2 = _chunk_post(_pairs(u_ref, rows), _pairs(w_ref, rows), _pairs(qt_ref, rows),
                                _pairs(kh_ref, rows), _pairs(qk_ref, rows), _pairs(gm_ref, grow), sp)
            for p in range(4):
                ss_ref[rows, p * 128:(p + 1) * 128] = sp[p]
                o_ref[rows, p * 128:(p + 1) * 128] = o[p]
            st[...] = s2
            return carry

        lax.fori_loop(0, DELTA_NB, chunk, 0)

    spec = pl.BlockSpec((DELTA_ROWS, GROUP_W), lambda i: (i, 0))
    gspec = pl.BlockSpec((DELTA_NB * 8, GROUP_W), lambda i: (i, 0))
    return pl.pallas_call(
        body, name="delta_scan_fwd", grid=(s // DELTA_ROWS,),
        in_specs=[spec] * 5 + [gspec],
        out_specs=[spec, spec],
        out_shape=[jax.ShapeDtypeStruct((s, GROUP_W), f32)] * 2,
        scratch_shapes=[pltpu.VMEM((4, CHUNK, 128), f32)],
        compiler_params=_cparams(("arbitrary",)),
    )(u, w, qt, kh, qk, gm)


def _delta_scan_bwd(w, qt, kh, qk, gm, do, xchg):
    s = w.shape[0]
    nb = s // DELTA_ROWS

    def body(w_ref, qt_ref, kh_ref, qk_ref, gm_ref, do_ref, dso_ref, dst):
        @pl.when(pl.program_id(0) == 0)
        def _():
            dst[...] = jnp.zeros_like(dst)

        def chunk(t, carry):
            ci = DELTA_NB - 1 - t
            rows = pl.ds(pl.multiple_of(ci * CHUNK, CHUNK), CHUNK)
            grow = pl.ds(pl.multiple_of(ci * 8, 8), 1)
            ds = dst[...]
            for p in range(4):
                dso_ref[rows, p * 128:(p + 1) * 128] = ds[p]
            do = _pairs(do_ref, rows)
            dvn = _tnp(_pairs(qk_ref, rows), do) + _pmm(_pairs(kh_ref, rows), ds)
            dst[...] = _tnp(_pairs(qt_ref, rows), do) + _pairs(gm_ref, grow) * ds - _tnp(_pairs(w_ref, rows), dvn)
            return carry

        lax.fori_loop(0, DELTA_NB, chunk, 0)

    spec = pl.BlockSpec((DELTA_ROWS, GROUP_W), lambda i: (nb - 1 - i, 0))
    gspec = pl.BlockSpec((DELTA_NB * 8, GROUP_W), lambda i: (nb - 1 - i, 0))
    out = pl.pallas_call(
        _ride(body, 6, 1, xchg, (nb,)), name="delta_scan_bwd", grid=(nb,),
        in_specs=[spec] * 4 + [gspec, spec] + [_ANY] * xchg.n,
        out_specs=[spec] + [_ANY] * xchg.n,
        out_shape=[jax.ShapeDtypeStruct((s, GROUP_W), f32)] + xchg.out_shape(),
        scratch_shapes=[pltpu.VMEM((4, CHUNK, 128), f32)] + xchg.scratch(),
        compiler_params=_cparams(("arbitrary",)),
    )(w, qt, kh, qk, gm, do, *xchg.arrs)
    return out[0], out[1:]


def _delta_chunk_bwd(qn, kn, sv, beta, g, tinv, ss, dso, do):
    s = qn.shape[0]

    def body(q_ref, k_ref, v_ref, b_ref, g_ref, t_ref, ss_ref, dso_ref, do_ref,
             dq_ref, dk_ref, dv_ref, db_ref, dg_ref):
        sp = _chunks(ss_ref)
        t = _chunks(t_ref)

        def fn(q, k, v, b, gg):
            return _chunk_post(*_chunk_pre(q, k, v, b, gg, t)[:6], sp)

        _, vjp = jax.vjp(fn, _chunks(q_ref), _chunks(k_ref), _chunks(v_ref), _chunks(b_ref), _chunks(g_ref))
        grads = vjp((_chunks(do_ref), _chunks(dso_ref)))
        for ref, val in zip((dq_ref, dk_ref, dv_ref, db_ref, dg_ref), grads):
            ref[...] = val.reshape(DELTA_ROWS, 128)

    v_spec = pl.BlockSpec((DELTA_ROWS, 128), lambda i, p: (i, 8 + p))
    return pl.pallas_call(
        body, name="delta_chunk_bwd", grid=(s // DELTA_ROWS, 4),
        in_specs=[_pair_spec(), _pair_spec(), v_spec] + [_pair_spec()] * 6,
        out_specs=[_pair_spec()] * 5,
        out_shape=[jax.ShapeDtypeStruct((s, GROUP_W), f32)] * 5,
        compiler_params=_cparams(("parallel", "parallel")),
    )(qn, kn, sv, beta, g, tinv, ss, dso, do)


def _head_sums(x):
    r = lax.broadcasted_iota(jnp.int32, (128, 128), 0)
    c = lax.broadcasted_iota(jnp.int32, (128, 128), 1)
    pair = jnp.where((r >> 6) == (c >> 6), 1.0, 0.0).astype(f32)
    npair = x.shape[1] // 128
    xb = jnp.concatenate([x[None, :, p * 128:(p + 1) * 128] for p in range(npair)], axis=0)
    sums = _mmx(xb, jnp.broadcast_to(pair, (npair, 128, 128)), "nn")
    return jnp.concatenate([sums[p] for p in range(npair)], axis=1)


def _sel_dot(a, b):
    return jnp.dot(a, b, precision=lax.Precision.HIGH, preferred_element_type=f32)


def _expand_matrix(first):
    r = lax.broadcasted_iota(jnp.int32, (128, GROUP_W), 0)
    c = lax.broadcasted_iota(jnp.int32, (128, GROUP_W), 1) >> 6
    return jnp.where(r == c + first, 1.0, 0.0).astype(f32)


@functools.partial(jax.custom_vjp, nondiff_argnums=(1,))
def _expand_heads(ba, first):
    return _sel_dot(ba, _expand_matrix(first))


def _expand_heads_fwd(ba, first):
    return _expand_heads(ba, first), None


def _expand_heads_bwd(first, _, g):
    return (_mmx(g[None], _expand_matrix(first)[None], "nt")[0],)


_expand_heads.defvjp(_expand_heads_fwd, _expand_heads_bwd)


def _softplus(x):
    return jnp.maximum(x, 0.0) + jnp.log(1.0 + jnp.exp(-jnp.abs(x)))


def _prep_fn(sq, sk, ba, alog_e, dt_e):
    qn = sq * lax.rsqrt(_head_sums(sq * sq) + EPS) * (HEAD_DIM ** -0.5)
    kn = sk * lax.rsqrt(_head_sums(sk * sk) + EPS)
    bl = _expand_heads(ba, 0)
    al = _expand_heads(ba, N_HEADS)
    beta = jax.nn.sigmoid(bl)
    g = -jnp.exp(alog_e) * _softplus(al + dt_e)
    nchunk = g.shape[0] // CHUNK
    ri = lax.broadcasted_iota(jnp.int32, (nchunk, CHUNK, CHUNK), 1)
    ci = lax.broadcasted_iota(jnp.int32, (nchunk, CHUNK, CHUNK), 2)
    tril = jnp.where(ri >= ci, 1.0, 0.0).astype(f32)
    gcum = lax.dot_general(tril, g.reshape(nchunk, CHUNK, g.shape[1]), _BNN, precision=lax.Precision.HIGH,
                           preferred_element_type=f32)
    return qn, kn, beta, gcum.reshape(g.shape)


def _gnorm_fn(o, z, ng_e):
    ms = _head_sums(o * o) * (1.0 / HEAD_DIM)
    return o * lax.rsqrt(ms + EPS) * ng_e * (z * jax.nn.sigmoid(z))


def _tok_spec(width, col):
    return pl.BlockSpec((TOK_TILE, width), lambda i: (i, col))


def _conv_taps(xs_ref, w_ref, base, n, cols):
    acc = w_ref[CONV_WIDTH - 1:CONV_WIDTH, cols] * xs_ref[pl.ds(base, n), cols]
    for j in range(CONV_WIDTH - 1):
        acc = acc + w_ref[j:j + 1, cols] * xs_ref[pl.ds(base - (CONV_WIDTH - 1) + j, n), cols]
    return acc


def _conv_silu_fwd(proj, conv_w):
    s = proj.shape[0]
    wd = 3 * GROUP_W
    hb = TOK_TILE // 8

    def body(x_ref, halo_ref, w_ref, o_ref, xs):
        inner = pl.program_id(0) > 0

        def lane_block(cb, carry):
            cols = pl.ds(pl.multiple_of(cb * 128, 128), 128)
            xs[0:8, cols] = jnp.where(inner, halo_ref[:, cols], 0.0)
            xs[8:, cols] = x_ref[:, cols]
            y = _conv_taps(xs, w_ref, 8, TOK_TILE, cols)
            o_ref[:, cols] = y * jax.nn.sigmoid(y)
            return carry

        lax.fori_loop(0, wd // 128, lane_block, 0)

    return pl.pallas_call(
        body, name="delta_conv_fwd", grid=(s // TOK_TILE,),
        in_specs=[_tok_spec(wd, 1), pl.BlockSpec((8, wd), lambda i: (jnp.maximum(i * hb - 1, 0), 1)),
                  pl.BlockSpec((CONV_WIDTH, wd), lambda i: (0, 0))],
        out_specs=_tok_spec(wd, 0),
        out_shape=jax.ShapeDtypeStruct((s, wd), f32),
        scratch_shapes=[pltpu.VMEM((TOK_TILE + 8, wd), f32)],
        compiler_params=_cparams(("parallel",)),
    )(proj, proj, conv_w)


def _conv_silu_bwd(proj, conv_w, ds3, xchg):
    s = proj.shape[0]
    wd = 3 * GROUP_W
    hb = TOK_TILE // 8
    nt = s // TOK_TILE

    def body(x_ref, hp_ref, hn_ref, dq_ref, dk_ref, dv_ref, dqn_ref, dkn_ref, dvn_ref, w_ref, dx_ref, dw_ref, xs, dys):
        i = pl.program_id(0)

        @pl.when(i == 0)
        def _():
            dw_ref[...] = jnp.zeros_like(dw_ref)

        last = i == nt - 1
        def lane_block(lb, carry, third, cur, nxt):
            tcols = pl.ds(pl.multiple_of(lb * 128, 128), 128)
            cols = pl.ds(pl.multiple_of(third * GROUP_W + lb * 128, 128), 128)
            xs[0:8, cols] = jnp.where(i > 0, hp_ref[:, cols], 0.0)
            xs[8:8 + TOK_TILE, cols] = x_ref[:, cols]
            xs[8 + TOK_TILE:, cols] = jnp.where(last, 0.0, hn_ref[:, cols])
            y = _conv_taps(xs, w_ref, 8, TOK_TILE, cols)
            sg = jax.nn.sigmoid(y)
            dy0 = cur[:, tcols] * (sg * (1.0 + y * (1.0 - sg)))
            dys[0:TOK_TILE, cols] = dy0
            yn = _conv_taps(xs, w_ref, 8 + TOK_TILE, 8, cols)
            sgn = jax.nn.sigmoid(yn)
            dys[TOK_TILE:, cols] = jnp.where(last, 0.0, nxt[:, tcols]) * (sgn * (1.0 + yn * (1.0 - sgn)))
            dx = w_ref[CONV_WIDTH - 1:CONV_WIDTH, cols] * dy0
            for j in range(CONV_WIDTH - 1):
                dx = dx + w_ref[j:j + 1, cols] * dys[pl.ds(CONV_WIDTH - 1 - j, TOK_TILE), cols]
            dx_ref[:, cols] = dx.astype(dx_ref.dtype)
            for j in range(CONV_WIDTH):
                dw_ref[j:j + 1, cols] += jnp.sum(dy0 * xs[pl.ds(8 - (CONV_WIDTH - 1) + j, TOK_TILE), cols],
                                                 axis=0, keepdims=True)
            return carry

        for third, (cur, nxt) in enumerate(((dq_ref, dqn_ref), (dk_ref, dkn_ref), (dv_ref, dvn_ref))):
            lax.fori_loop(0, GROUP_W // 128, functools.partial(lane_block, third=third, cur=cur, nxt=nxt), 0)

    prev8 = lambda col: pl.BlockSpec((8, wd), lambda i: (jnp.maximum(i * hb - 1, 0), col))
    next8 = lambda col: pl.BlockSpec((8, wd), lambda i: (jnp.minimum((i + 1) * hb, s // 8 - 1), col))
    next8_third = pl.BlockSpec((8, GROUP_W), lambda i: (jnp.minimum((i + 1) * hb, s // 8 - 1), 0))
    out = pl.pallas_call(
        _ride(body, 10, 2, xchg, (nt,)), name="delta_conv_bwd", grid=(nt,),
        in_specs=[_tok_spec(wd, 1), prev8(1), next8(1)] + [_tok_spec(GROUP_W, 0)] * 3 + [next8_third] * 3
        + [pl.BlockSpec((CONV_WIDTH, wd), lambda i: (0, 0))] + [_ANY] * xchg.n,
        out_specs=[_tok_spec(wd, 0), pl.BlockSpec((CONV_WIDTH, wd), lambda i: (0, 0))] + [_ANY] * xchg.n,
        out_shape=[jax.ShapeDtypeStruct((s, wd), bf16), jax.ShapeDtypeStruct((CONV_WIDTH, wd), f32)] + xchg.out_shape(),
        scratch_shapes=[pltpu.VMEM((TOK_TILE + 16, wd), f32), pltpu.VMEM((TOK_TILE + 8, wd), f32)] + xchg.scratch(),
        compiler_params=_cparams(("arbitrary",)),
    )(proj, proj, proj, *ds3, *ds3, conv_w, *xchg.arrs)
    return out[:2], out[2:]


def _delta_prep_fwd(sconv, proj, alog_e, dt_e):
    s = sconv.shape[0]

    def body(sq_ref, sk_ref, ba_ref, al_ref, dt_ref, q_ref, k_ref, b_ref, g_ref):
        qn, kn, beta, g = _prep_fn(sq_ref[...], sk_ref[...], ba_ref[...], al_ref[...], dt_ref[...])
        q_ref[...] = qn
        k_ref[...] = kn
        b_ref[...] = beta
        g_ref[...] = g

    return pl.pallas_call(
        body, name="delta_prep_fwd", grid=(s // TOK_TILE,),
        in_specs=[_tok_spec(GROUP_W, 0), _tok_spec(GROUP_W, 1), _tok_spec(128, BA_BLOCK),
                  _vec_spec(GROUP_W), _vec_spec(GROUP_W)],
        out_specs=[_tok_spec(GROUP_W, 0)] * 4,
        out_shape=[jax.ShapeDtypeStruct((s, GROUP_W), f32)] * 4,
        compiler_params=_cparams(("parallel",)),
    )(sconv, sconv, proj, alog_e, dt_e)


def _delta_prep_bwd(sconv, proj, alog_e, dt_e, dqn, dkn, dbeta, dg, xchg):
    s = sconv.shape[0]
    grid = (s // TOK_TILE,)

    def body(sq_ref, sk_ref, ba_ref, al_ref, dt_ref, dq_ref, dk_ref, db_ref, dg_ref,
             dsq_ref, dsk_ref, dba_ref, dal_ref, ddt_ref):
        @pl.when(pl.program_id(0) == 0)
        def _():
            dal_ref[...] = jnp.zeros_like(dal_ref)
            ddt_ref[...] = jnp.zeros_like(ddt_ref)

        _, vjp = jax.vjp(_prep_fn, sq_ref[...], sk_ref[...], ba_ref[...], al_ref[...], dt_ref[...])
        dsq, dsk, dba, dal, ddt = vjp((dq_ref[...], dk_ref[...], db_ref[...], dg_ref[...]))
        dsq_ref[...] = dsq
        dsk_ref[...] = dsk
        dba_ref[...] = dba.astype(bf16)
        dal_ref[...] += dal
        ddt_ref[...] += ddt

    out = pl.pallas_call(
        _ride(body, 9, 5, xchg, grid), name="delta_prep_bwd", grid=grid,
        in_specs=[_tok_spec(GROUP_W, 0), _tok_spec(GROUP_W, 1), _tok_spec(128, BA_BLOCK),
                  _vec_spec(GROUP_W), _vec_spec(GROUP_W)] + [_tok_spec(GROUP_W, 0)] * 4 + [_ANY] * xchg.n,
        out_specs=[_tok_spec(GROUP_W, 0), _tok_spec(GROUP_W, 0), _tok_spec(128, 0),
                   _acc_spec(GROUP_W), _acc_spec(GROUP_W)] + [_ANY] * xchg.n,
        out_shape=[jax.ShapeDtypeStruct((s, GROUP_W), f32)] * 2 + [jax.ShapeDtypeStruct((s, 128), bf16)]
        + [jax.ShapeDtypeStruct((1, GROUP_W), f32)] * 2 + xchg.out_shape(),
        scratch_shapes=xchg.scratch(),
        compiler_params=_cparams(("arbitrary",)),
    )(sconv, sconv, proj, alog_e, dt_e, dqn, dkn, dbeta, dg, *xchg.arrs)
    return out[:5], out[5:]


def _gnorm_fwd(o, proj, ng_e):
    s = o.shape[0]

    def body(o_ref, z_ref, g_ref, y_ref):
        y_ref[...] = _gnorm_fn(o_ref[...], z_ref[...], g_ref[...])

    return pl.pallas_call(
        body, name="delta_gnorm_fwd", grid=(s // TOK_TILE,),
        in_specs=[_tok_spec(GROUP_W, 0), _tok_spec(GROUP_W, Z_COL // GROUP_W), _vec_spec(GROUP_W)],
        out_specs=_tok_spec(GROUP_W, 0),
        out_shape=jax.ShapeDtypeStruct((s, GROUP_W), f32),
        compiler_params=_cparams(("parallel",)),
    )(o, proj, ng_e)


def _gnorm_bwd(o, proj, ng_e, dycat):
    s = o.shape[0]

    def body(o_ref, z_ref, g_ref, dy_ref, do_ref, dz_ref, dg_ref):
        @pl.when(pl.program_id(0) == 0)
        def _():
            dg_ref[...] = jnp.zeros_like(dg_ref)

        _, vjp = jax.vjp(_gnorm_fn, o_ref[...], z_ref[...], g_ref[...])
        do, dz, dg = vjp(dy_ref[...])
        do_ref[...] = do
        dz_ref[...] = dz.astype(bf16)
        dg_ref[...] += dg

    return pl.pallas_call(
        body, name="delta_gnorm_bwd", grid=(s // TOK_TILE,),
        in_specs=[_tok_spec(GROUP_W, 0), _tok_spec(GROUP_W, Z_COL // GROUP_W), _vec_spec(GROUP_W),
                  _tok_spec(GROUP_W, 1)],
        out_specs=[_tok_spec(GROUP_W, 0), _tok_spec(GROUP_W, 0), _acc_spec(GROUP_W)],
        out_shape=[jax.ShapeDtypeStruct((s, GROUP_W), f32), jax.ShapeDtypeStruct((s, GROUP_W), bf16),
                   jax.ShapeDtypeStruct((1, GROUP_W), f32)],
        compiler_params=_cparams(("arbitrary",)),
    )(o, proj, ng_e, dycat)


_MESH = pl.DeviceIdType.MESH
_ANY = pl.BlockSpec(memory_space=pl.ANY)
_VMEM = pl.BlockSpec(memory_space=pltpu.VMEM)


def _my_place():
    x, y, c = lax.axis_index("x"), lax.axis_index("y"), lax.axis_index("c")
    return x, y, c, 4 * x + 2 * y + c


def _peer(k, x, y, c):
    px = 1 - x if k & 4 else x
    py = 1 - y if k & 2 else y
    pc = 1 - c if k & 1 else c
    return (px, py, pc), 4 * px + 2 * py + pc


def _exchange_all(src_of_peer, dst_ref, send_sems, recv_sems, x, y, c, me):
    sent = []
    for k in range(1, N_DEV):
        dev, pidx = _peer(k, x, y, c)
        cp = pltpu.make_async_remote_copy(src_ref=src_of_peer(pidx), dst_ref=dst_ref.at[me],
                                          send_sem=send_sems.at[k - 1], recv_sem=recv_sems.at[k - 1],
                                          device_id=dev, device_id_type=_MESH)
        cp.start()
        sent.append(cp)
    for k in range(1, N_DEV):
        dev, pidx = _peer(k, x, y, c)
        pltpu.make_async_remote_copy(src_ref=src_of_peer(pidx), dst_ref=dst_ref.at[pidx],
                                     send_sem=send_sems.at[k - 1], recv_sem=recv_sems.at[k - 1],
                                     device_id=dev, device_id_type=_MESH).wait_recv()
    for cp in sent:
        cp.wait_send()


def _ada_exchange(cv8, w_ada, b_ada8):
    def body(cv_ref, w_ref, b_ref, call_ref, modp_ref, part_s, s1, r1, s2, r2):
        x, y, c, me = _my_place()
        call_ref[me] = cv_ref[...]
        _exchange_all(lambda pidx: cv_ref, call_ref, s1, r1, x, y, c, me)
        bias = b_ref[me]
        for j in range(N_DEV):
            cj = call_ref[j][:, :D_MODEL]
            part_s[j] = _hdot(cj * jax.nn.sigmoid(cj), w_ref[...]) + bias
        modp_ref[me] = part_s[me]
        _exchange_all(lambda pidx: part_s.at[pidx], modp_ref, s2, r2, x, y, c, me)

    nsh = w_ada.shape[1]
    return pl.pallas_call(
        body, name="ada_exchange",
        in_specs=[_VMEM, _VMEM, _VMEM], out_specs=[_VMEM, _VMEM],
        out_shape=[jax.ShapeDtypeStruct((N_DEV, 8, cv8.shape[1]), f32), jax.ShapeDtypeStruct((N_DEV, 8, nsh), f32)],
        scratch_shapes=[pltpu.VMEM((N_DEV, 8, nsh), f32)] + [pltpu.SemaphoreType.DMA((N_DEV - 1,))] * 4,
        compiler_params=pltpu.CompilerParams(vmem_limit_bytes=VMEM_LIMIT),
    )(cv8, w_ada, b_ada8)


def _all_to_all(arrs, name):
    ex = _Exchange(arrs, gather=False)

    def body(*refs):
        srcs, dsts, sems = refs[:ex.n], refs[ex.n:2 * ex.n], refs[2 * ex.n:]
        ex.start(srcs, dsts, sems)
        ex.wait(srcs, dsts, sems)

    return pl.pallas_call(
        body, name=name,
        in_specs=[_ANY] * ex.n, out_specs=[_ANY] * ex.n,
        out_shape=ex.out_shape(), scratch_shapes=ex.scratch(),
    )(*arrs)


class _Exchange:
    def __init__(self, arrs, gather):
        self.arrs, self.gather, self.n = list(arrs), gather, len(arrs)

    def out_shape(self):
        return [jax.ShapeDtypeStruct(((N_DEV,) + a.shape) if self.gather else a.shape, a.dtype) for a in self.arrs]

    def scratch(self):
        if self.n == 0:
            return []
        return [pltpu.SemaphoreType.DMA((self.n, N_DEV - 1)), pltpu.SemaphoreType.DMA((self.n, N_DEV - 1)),
                pltpu.SemaphoreType.DMA((self.n,))]

    def _src(self, srcs, a, idx):
        return srcs[a] if self.gather else srcs[a].at[idx]

    def _copies(self, srcs, dsts, sems, incoming):
        send_sems, recv_sems, _ = sems
        x, y, c, me = _my_place()
        out = []
        for a in range(self.n):
            for k in range(1, N_DEV):
                dev, pidx = _peer(k, x, y, c)
                out.append(pltpu.make_async_remote_copy(
                    src_ref=self._src(srcs, a, pidx), dst_ref=dsts[a].at[pidx if incoming else me],
                    send_sem=send_sems.at[a, k - 1], recv_sem=recv_sems.at[a, k - 1],
                    device_id=dev, device_id_type=_MESH))
        return out

    def _local(self, srcs, dsts, sems):
        me = _my_place()[3]
        return [pltpu.make_async_copy(self._src(srcs, a, me), dsts[a].at[me], sems[2].at[a]) for a in range(self.n)]

    def start(self, srcs, dsts, sems):
        for cp in self._local(srcs, dsts, sems) + self._copies(srcs, dsts, sems, incoming=False):
            cp.start()

    def wait(self, srcs, dsts, sems):
        for cp in self._copies(srcs, dsts, sems, incoming=True):
            cp.wait_recv()
        for cp in self._copies(srcs, dsts, sems, incoming=False):
            cp.wait_send()
        for cp in self._local(srcs, dsts, sems):
            cp.wait()

    def start_at_first_step(self, grid, srcs, dsts, sems):
        first = functools.reduce(jnp.logical_and, [pl.program_id(i) == 0 for i in range(len(grid))])
        pl.when(first)(lambda: self.start(srcs, dsts, sems))

    def wait_at_last_step(self, grid, srcs, dsts, sems):
        last = functools.reduce(jnp.logical_and, [pl.program_id(i) == g - 1 for i, g in enumerate(grid)])
        pl.when(last)(lambda: self.wait(srcs, dsts, sems))


class _ChipGather:
    def __init__(self, shard):
        self.shard = shard

    def out_shape(self):
        return jax.ShapeDtypeStruct((N_DEV,) + self.shard.shape, self.shard.dtype)

    def scratch(self):
        return [pltpu.SemaphoreType.DMA((N_DEV - 1,)), pltpu.SemaphoreType.DMA((N_DEV - 1,)),
                pltpu.SemaphoreType.DMA(())]

    def _place(self):
        x, y, c, me = _my_place()
        return x, y, c, me, (x, y, 1 - c), [(1 - x, y), (x, 1 - y), (1 - x, 1 - y)]

    def _copy(self, out, sems, k, block, to, src=None):
        rows = out.at[4 * block[0] + 2 * block[1] + block[2]]
        return pltpu.make_async_remote_copy(src_ref=rows if src is None else src, dst_ref=rows,
                                            send_sem=sems[0].at[k], recv_sem=sems[1].at[k],
                                            device_id=to, device_id_type=_MESH)

    def start(self, src, out, sems):
        x, y, c, me, sib, chips = self._place()
        pltpu.make_async_copy(src, out.at[me], sems[2]).start()
        self._copy(out, sems, 0, (x, y, c), sib, src=src).start()
        for j, chip in enumerate(chips):
            self._copy(out, sems, 1 + j, (x, y, c), (*chip, c), src=src).start()

    def forward(self, src, out, sems):
        x, y, c, me, sib, chips = self._place()
        for j, chip in enumerate(chips):
            self._copy(out, sems, 1 + j, (*chip, c), (x, y, c)).wait_recv()
            self._copy(out, sems, 4 + j, (*chip, c), sib).start()

    def finish(self, src, out, sems):
        x, y, c, me, sib, chips = self._place()
        self._copy(out, sems, 0, (x, y, 1 - c), (x, y, c)).wait_recv()
        for j, chip in enumerate(chips):
            self._copy(out, sems, 4 + j, (*chip, 1 - c), (x, y, c)).wait_recv()
        self._copy(out, sems, 0, (x, y, c), sib, src=src).wait_send()
        for j, chip in enumerate(chips):
            self._copy(out, sems, 1 + j, (x, y, c), (*chip, c), src=src).wait_send()
            self._copy(out, sems, 4 + j, (*chip, c), sib).wait_send()
        pltpu.make_async_copy(src, out.at[me], sems[2]).wait()


def _ride(body, n_in, n_out, xchg, grid):
    nx = xchg.n
    if nx == 0:
        return body

    def wrapped(*refs):
        ins, xs = refs[:n_in], refs[n_in:n_in + nx]
        outs, xd = refs[n_in + nx:n_in + nx + n_out], refs[n_in + nx + n_out:n_in + 2 * nx + n_out]
        scratch = refs[n_in + 2 * nx + n_out:]
        xchg.start_at_first_step(grid, xs, xd, scratch[-3:])
        body(*ins, *outs, *scratch[:-3])
        xchg.wait_at_last_step(grid, xs, xd, scratch[-3:])

    return wrapped


def _adamw_math(w, g, m, v):
    m2 = ADAM_B1 * m + (1.0 - ADAM_B1) * g
    v2 = ADAM_B2 * v + (1.0 - ADAM_B2) * (g * g)
    m_hat = m2 / (1.0 - ADAM_B1 ** ADAM_STEP)
    v_hat = v2 / (1.0 - ADAM_B2 ** ADAM_STEP)
    delta = -ADAM_LR * (m_hat / (jnp.sqrt(v_hat) + ADAM_EPS) + ADAM_WD * w)
    return delta, m2, v2


def _row_tile(rows):
    for t in (256, 128, 64, 32, 16, 8):
        if rows % t == 0:
            return t
    return rows


def _reduce_adamw(parts, w, m, v, name):
    _, r, cdim = parts.shape
    tr = _row_tile(r)

    def body(p_ref, w_ref, m_ref, v_ref, g_ref, d_ref, m2_ref, v2_ref):
        g = p_ref[0].astype(f32)
        for j in range(1, N_DEV):
            g = g + p_ref[j].astype(f32)
        delta, m2, v2 = _adamw_math(w_ref[...], g, m_ref[...], v_ref[...])
        g_ref[...] = g
        d_ref[...] = delta
        m2_ref[...] = m2
        v2_ref[...] = v2

    spec = pl.BlockSpec((tr, cdim), lambda i: (i, 0))
    return pl.pallas_call(
        body, name=name, grid=(r // tr,),
        in_specs=[pl.BlockSpec((N_DEV, tr, cdim), lambda i: (0, i, 0)), spec, spec, spec],
        out_specs=[spec] * 4,
        out_shape=[jax.ShapeDtypeStruct((r, cdim), f32)] * 4,
        compiler_params=_cparams(("parallel",)),
    )(parts, w, m, v)


def _adamw(w, g, m, v, name):
    r, cdim = w.shape
    tr = _row_tile(r)

    def body(w_ref, g_ref, m_ref, v_ref, d_ref, m2_ref, v2_ref):
        delta, m2, v2 = _adamw_math(w_ref[...], g_ref[...], m_ref[...], v_ref[...])
        d_ref[...] = delta
        m2_ref[...] = m2
        v2_ref[...] = v2

    spec = pl.BlockSpec((tr, cdim), lambda i: (i, 0))
    return pl.pallas_call(
        body, name=name, grid=(r // tr,),
        in_specs=[spec] * 4, out_specs=[spec] * 3,
        out_shape=[jax.ShapeDtypeStruct((r, cdim), f32)] * 3,
        compiler_params=_cparams(("parallel",)),
    )(w, g, m, v)


def _sum_devices(parts, name):
    _, r, cdim = parts.shape

    def body(p_ref, o_ref):
        g = p_ref[0]
        for j in range(1, N_DEV):
            g = g + p_ref[j]
        o_ref[...] = g

    return pl.pallas_call(
        body, name=name, out_shape=jax.ShapeDtypeStruct((r, cdim), f32),
        in_specs=[_VMEM], out_specs=_VMEM,
    )(parts)


def _ada_wgrad(c_all8, dmod_cols):
    nsh = dmod_cols.shape[1]

    def body(c_ref, d_ref, o_ref):
        cv = c_ref[...]
        o_ref[...] = lax.dot_general(cv * jax.nn.sigmoid(cv), d_ref[...], _TN, precision=_HI,
                                     preferred_element_type=f32)

    return pl.pallas_call(
        body, name="ada_wgrad", out_shape=jax.ShapeDtypeStruct((D_MODEL, nsh), f32),
        in_specs=[_VMEM, _VMEM], out_specs=_VMEM,
        compiler_params=pltpu.CompilerParams(vmem_limit_bytes=VMEM_LIMIT),
    )(c_all8, dmod_cols)


def _cols(t):
    return t.transpose(1, 0, 2).reshape(t.shape[1], N_DEV * t.shape[2])


def _col_blocks(t, n):
    return t.reshape(t.shape[0], N_DEV, n).transpose(1, 0, 2).astype(bf16)


def _row_blocks(t):
    return t.reshape(N_DEV, t.shape[0] // N_DEV, t.shape[1]).astype(bf16)


def _local_step(x, tgt, mod, norm_attn_g, w_in_sh, rel_bias, conv_full, a_log, dt_bias, delta_norm_g,
                norm_ffn_g, final_norm_g, w_out_sh, w_gate_sh, w_up_sh, w_down_sh):
    s = x.shape[0]
    sh1, sc1, g1, sh2, sc2, g2 = [mod[:, i * D_MODEL:(i + 1) * D_MODEL] for i in range(6)]
    nag = norm_attn_g.reshape(1, D_MODEL)
    nfg = norm_ffn_g.reshape(1, D_MODEL)
    fg = final_norm_g.reshape(1, D_MODEL)
    idx = _bucket_tables()
    bias = _bias_tables(rel_bias, idx)
    alog_e = jnp.repeat(a_log.reshape(N_HEADS), HEAD_DIM)[None]
    dt_e = jnp.repeat(dt_bias.reshape(N_HEADS), HEAD_DIM)[None]
    ng_e = jnp.tile(delta_norm_g.reshape(HEAD_DIM), N_HEADS)[None]

    h1, w_in_g = _ln_mod_fwd(x, nag, sc1, sh1, w_in_sh, "ln1_fwd")
    w_in_p = jnp.pad(_cols(w_in_g), ((0, 0), (0, IN_PAD - IN_WIDTH)))
    proj, (w_out_g,) = _mm(h1, w_in_p, "nn", f32, 512, IN_PAD, 1024, "in_proj",
                           xchg=_Exchange([w_out_sh], gather=True))
    (y_attn, lse), (w_gate_g, w_up_g) = _attn_fwd(proj, bias, _Exchange([w_gate_sh, w_up_sh], gather=True))
    w_out_b = w_out_g.reshape(2 * GROUP_W, D_MODEL)
    w_gate_b, w_up_b = _cols(w_gate_g), _cols(w_up_g)
    n_ff = w_gate_sh.shape[1]
    sconv = _conv_silu_fwd(proj, conv_full)
    qn, kn, beta, g = _delta_prep_fwd(sconv, proj, alog_e, dt_e)
    (u, w, qt, kh, qk, tinv, gm), (w_down_g,) = _delta_chunk_pre(
        qn, kn, sconv, beta, g, _Exchange([w_down_sh], gather=True))
    w_down_b = w_down_g.reshape(D_FF, D_MODEL)
    o, ss = _delta_scan_fwd(u, w, qt, kh, qk, gm)
    y_delta = _gnorm_fwd(o, proj, ng_e)
    y, x1, h2 = _proj_resid_ln_mod_fwd([(y_attn, w_out_b[:GROUP_W]), (y_delta, w_out_b[GROUP_W:])],
                                       x, g1, nfg, sc2, sh2, "out_proj_ln2")
    act, gate, up = _ffn_up(h2, w_gate_b, w_up_b, "ffn_up")
    dx2, dy2, loss, dfg, dg2 = _proj_final_loss_bwd(act, w_down_b, x1, g2, fg, tgt, "ffn_down_loss")

    dgate, dup = _ffn_down_dx(dy2, w_down_b, gate, up, "ffn_down_dx")
    g_down = _mm(act, dy2, "tn", f32, 1408, 1024, 1024, "ffn_down_dw")
    (dx1, dsh2, dsc2, dnfg, dy, dg1), (r_down,) = _proj_ln_mod_bwd(
        [(dgate, w_gate_b), (dup, w_up_b)], x1, nfg, sc2, dx2, 256, "ffn_up_dx_ln2",
        _Exchange([_row_blocks(g_down)], gather=False), gate=g1, y=y)
    g_gate = _mm(h2, dgate, "tn", f32, 1024, 1408, 1024, "ffn_gate_dw")
    g_up = _mm(h2, dup, "tn", f32, 1024, 1408, 1024, "ffn_up_dw")
    dycat = _mm(dy, w_out_b, "nt", f32, 512, 1024, 1024, "out_proj_dx")
    g_out = jnp.concatenate([_mm(y_attn, dy, "tn", f32, GROUP_W, 1024, 1024, "out_proj_dw_attn"),
                             _mm(y_delta, dy, "tn", f32, GROUP_W, 1024, 1024, "out_proj_dw_delta")], axis=0)
    dq, dk, dv, dbias = _attn_bwd(proj, bias, y_attn, lse, dycat)
    g_rb = _bias_grad(dbias, idx)[:, :, 0].T
    do, dz, dng = _gnorm_bwd(o, proj, ng_e, dycat)
    dso, (r_out,) = _delta_scan_bwd(w, qt, kh, qk, gm, do, _Exchange([_row_blocks(g_out)], gather=False))
    dqn, dkn, dvd, dbeta, dgd = _delta_chunk_bwd(qn, kn, sconv, beta, g, tinv, ss, dso, do)
    (dsq, dsk, dba, dal, ddt), _ = _delta_prep_bwd(
        sconv, proj, alog_e, dt_e, dqn, dkn, dbeta, dgd, _Exchange([], gather=False))
    (dxc, g_conv), (r_gate, r_up) = _conv_silu_bwd(
        proj, conv_full, (dsq, dsk, dvd),
        _Exchange([_col_blocks(g_gate, n_ff), _col_blocks(g_up, n_ff)], gather=False))
    pieces = ((dq, 0), (dk, GROUP_W), (dv, 2 * GROUP_W), (dxc, DELTA_COL), (dz, Z_COL), (dba, BA_BLOCK * 128))
    g_in = jnp.concatenate(
        [_mm(h1, p, "tn", f32, 1024, min(p.shape[1], 768), 1024, "in_proj_dw_%d" % c) for p, c in pieces], axis=1)
    (gx, dsh1, dsc1, dnag), (r_in,) = _proj_ln_mod_bwd(
        [(p, w_in_p[:, c:c + p.shape[1]]) for p, c in pieces], x, nag, sc1, dx1, TOK_TILE, "in_proj_dx_ln1",
        _Exchange([_col_blocks(g_in[:, :IN_WIDTH], IN_WIDTH // N_DEV)], gather=False))
    grads = dict(
        x=gx, mod=jnp.concatenate([dsh1, dsc1, dg1, dsh2, dsc2, dg2], axis=1),
        norm_attn_g=dnag, norm_ffn_g=dnfg, final_norm_g=dfg, rel_bias=g_rb, conv_w=g_conv,
        a_log=dal.reshape(N_HEADS, HEAD_DIM).sum(-1), dt_bias=ddt.reshape(N_HEADS, HEAD_DIM).sum(-1),
        delta_norm_g=dng.reshape(N_HEADS, HEAD_DIM).sum(0),
        w_in=r_in, w_out=r_out, w_gate=r_gate, w_up=r_up, w_down=r_down)
    return loss[0, 0], grads


def _misc_row(rel_bias, a_log, dt_bias, delta_norm_g):
    flat = jnp.concatenate([rel_bias.reshape(-1), a_log.reshape(-1), dt_bias.reshape(-1), delta_norm_g.reshape(-1)])
    return jnp.pad(flat, (0, D_MODEL - flat.shape[0]))[None]


def _pack_small(b_ada, nag, nfg, fng, rel_bias, a_log, dt_bias, dng, conv_shard):
    rows = [b_ada.reshape(6, D_MODEL), nag.reshape(1, D_MODEL), nfg.reshape(1, D_MODEL), fng.reshape(1, D_MODEL),
            _misc_row(rel_bias, a_log, dt_bias, dng),
            jnp.pad(conv_shard.reshape(-1), (0, D_MODEL - conv_shard.size))[None],
            jnp.zeros((5, D_MODEL), f32)]
    return jnp.concatenate(rows, axis=0)


def _unpack_small(p, conv_shape):
    misc = p[9]
    return dict(
        b_ada=p[0:6].reshape(1, 6 * D_MODEL), norm_attn_g=p[6:7], norm_ffn_g=p[7:8], final_norm_g=p[8],
        rel_bias=misc[0:256].reshape(N_BUCKETS, N_HEADS), a_log=misc[256:264].reshape(1, N_HEADS),
        dt_bias=misc[264:272].reshape(1, N_HEADS), delta_norm_g=misc[272:336].reshape(1, HEAD_DIM),
        conv_w=p[10, :conv_shape[1] * conv_shape[2]].reshape(conv_shape))


def kernel(x, c, w_ada, b_ada, norm_attn_g, w_in, rel_bias, conv_w, a_log, dt_bias, delta_norm_g, w_out, norm_ffn_g, w_gate, w_up, w_down, final_norm_g, loss_target, m_w_ada, m_b_ada, m_norm_attn_g, m_w_in, m_rel_bias, m_conv_w, m_a_log, m_dt_bias, m_delta_norm_g, m_w_out, m_norm_ffn_g, m_w_gate, m_w_up, m_w_down, m_final_norm_g, v_w_ada, v_b_ada, v_norm_attn_g, v_w_in, v_rel_bias, v_conv_w, v_a_log, v_dt_bias, v_delta_norm_g, v_w_out, v_norm_ffn_g, v_w_gate, v_w_up, v_w_down, v_final_norm_g):
    me = 4 * lax.axis_index("x") + 2 * lax.axis_index("y") + lax.axis_index("c")
    ada_sh = w_ada.shape[2]
    conv_sh = conv_w.shape[2]

    cv = jnp.concatenate([c[0], conv_w[0].reshape(-1)])
    cv8 = jnp.zeros((8, 2 * D_MODEL), f32).at[0, :cv.shape[0]].set(cv)
    b8 = jnp.broadcast_to(b_ada.reshape(N_DEV, 1, ada_sh), (N_DEV, 8, ada_sh))
    call, modp = _ada_exchange(cv8, w_ada[0], b8)
    mod = modp[:, 0, :].reshape(1, 6 * D_MODEL)
    c_all = call[:, 0, :D_MODEL]
    conv_full = call[:, 0, D_MODEL:D_MODEL + CONV_WIDTH * conv_sh].reshape(N_DEV, CONV_WIDTH, conv_sh)
    conv_full = conv_full.transpose(1, 0, 2).reshape(CONV_WIDTH, N_DEV * conv_sh)

    loss_local, gr = _local_step(x[0], loss_target[0], mod, norm_attn_g, w_in[0].astype(bf16), rel_bias, conv_full, a_log,
                                 dt_bias, delta_norm_g, norm_ffn_g, final_norm_g, w_out[0].astype(bf16),
                                 w_gate[0].astype(bf16), w_up[0].astype(bf16), w_down[0].astype(bf16))
    loss = lax.psum(loss_local, ("x", "y", "c"))

    small = jnp.concatenate([
        gr["mod"].reshape(6, D_MODEL), gr["norm_attn_g"], gr["norm_ffn_g"], gr["final_norm_g"],
        gr["conv_w"].reshape(6, D_MODEL),
        _misc_row(gr["rel_bias"], gr["a_log"], gr["dt_bias"], gr["delta_norm_g"])], axis=0)
    parts = _all_to_all([jnp.broadcast_to(small[None], (N_DEV,) + small.shape)], "small_gather")[0]
    tot = _sum_devices(parts, "small_sum")
    g_conv_full = tot[9:15].reshape(CONV_WIDTH, N_DEV * conv_sh)
    g_conv = lax.dynamic_slice(g_conv_full, (0, me * conv_sh), (CONV_WIDTH, conv_sh))
    misc = tot[15]
    g_small = _pack_small(tot[0:6], tot[6], tot[7], tot[8], misc[0:256], misc[256:264], misc[264:272],
                          misc[272:336], g_conv)
    pk = lambda pre: _pack_small(pre[0], pre[1], pre[2], pre[3], pre[4], pre[5], pre[6], pre[7], pre[8])
    w_small = pk((b_ada, norm_attn_g, norm_ffn_g, final_norm_g, rel_bias, a_log, dt_bias, delta_norm_g, conv_w))
    m_small = pk((m_b_ada, m_norm_attn_g, m_norm_ffn_g, m_final_norm_g, m_rel_bias, m_a_log, m_dt_bias,
                  m_delta_norm_g, m_conv_w))
    v_small = pk((v_b_ada, v_norm_attn_g, v_norm_ffn_g, v_final_norm_g, v_rel_bias, v_a_log, v_dt_bias,
                  v_delta_norm_g, v_conv_w))
    d_small, m2_small, v2_small = _adamw(w_small, g_small, m_small, v_small, "adamw_small")
    cshape = conv_w.shape
    G, Dl, M2, V2 = (_unpack_small(t, cshape) for t in (g_small, d_small, m2_small, v2_small))

    dmod_all = parts[:, 0:6, :].reshape(N_DEV, 6 * D_MODEL)
    dmod_cols = lax.dynamic_slice(dmod_all, (0, me * ada_sh), (N_DEV, ada_sh))
    g_ada = _ada_wgrad(c_all, dmod_cols)
    d_ada, m2_ada, v2_ada = _adamw(w_ada[0], g_ada, m_w_ada[0], v_w_ada[0], "adamw_w_ada")

    big = {}
    for name, w_, m_, v_ in (("w_in", w_in, m_w_in, v_w_in), ("w_out", w_out, m_w_out, v_w_out),
                             ("w_gate", w_gate, m_w_gate, v_w_gate), ("w_up", w_up, m_w_up, v_w_up),
                             ("w_down", w_down, m_w_down, v_w_down)):
        big[name] = [t[None] for t in _reduce_adamw(gr[name], w_[0], m_[0], v_[0], "reduce_adamw_" + name)]

    def leaf(i, name):
        if name == "w_ada":
            return (g_ada, d_ada, m2_ada, v2_ada)[i][None]
        if name in big:
            return big[name][i]
        return (G, Dl, M2, V2)[i][name]

    order = ["w_ada", "b_ada", "norm_attn_g", "w_in", "rel_bias", "conv_w", "a_log", "dt_bias", "delta_norm_g",
             "w_out", "norm_ffn_g", "w_gate", "w_up", "w_down", "final_norm_g"]
    outs = [loss, gr["x"][None]]
    for i in range(4):
        outs += [leaf(i, n) for n in order]
    return tuple(outs)
```

```python
import functools
import math

import jax
import jax.numpy as jnp
from jax import lax
from jax.experimental import pallas as pl
from jax.experimental.pallas import tpu as pltpu

f32 = jnp.float32
bf16 = jnp.bfloat16

D_MODEL = 1024
HEAD_DIM = 64
N_HEADS = 8
GROUP_W = 512
IN_WIDTH = 3600
IN_PAD = 3840
D_FF = 2816
EPS = 1e-6
NEG_INF = -1e30
BAND = 128
PAD_UNIT = 2048
DILATIONS = (1, 4, 16)
N_BUCKETS = 32
MAX_DISTANCE = 2048
CONV_WIDTH = 4
CHUNK = 64
N_DEV = 8
VMEM_LIMIT = 56 * 1024 * 1024

ADAM_LR, ADAM_B1, ADAM_B2, ADAM_EPS, ADAM_WD, ADAM_STEP = 0.001, 0.9, 0.999, 1e-08, 0.01, 10


def _cparams(sem):
    return pltpu.CompilerParams(dimension_semantics=sem, vmem_limit_bytes=VMEM_LIMIT)


def _mm(a, b, mode, out_dtype, tm, tn, tk, name, xchg=None):
    if mode == "nn":
        (m, k), (_, n) = a.shape, b.shape
        a_spec = pl.BlockSpec((tm, tk), lambda j, i, kk: (i, kk))
        b_spec = pl.BlockSpec((tk, tn), lambda j, i, kk: (kk, j))
        dims = (((1,), (0,)), ((), ()))
    elif mode == "nt":
        (m, k), (n, _) = a.shape, b.shape
        a_spec = pl.BlockSpec((tm, tk), lambda j, i, kk: (i, kk))
        b_spec = pl.BlockSpec((tn, tk), lambda j, i, kk: (j, kk))
        dims = (((1,), (1,)), ((), ()))
    else:
        (k, m), (_, n) = a.shape, b.shape
        a_spec = pl.BlockSpec((tk, tm), lambda j, i, kk: (kk, i))
        b_spec = pl.BlockSpec((tk, tn), lambda j, i, kk: (kk, j))
        dims = (((0,), (0,)), ((), ()))
    assert m % tm == 0 and n % tn == 0 and k % tk == 0, (name, m, n, k, tm, tn, tk)
    nk = k // tk
    grid = (n // tn, m // tm, nk)
    nx = xchg.n if xchg is not None else 0

    def body(*refs):
        a_ref, b_ref = refs[:2]
        o_ref = refs[2 + nx]
        scratch = refs[3 + 2 * nx:]
        if nx:
            xrefs = (refs[2:2 + nx], refs[3 + nx:3 + 2 * nx], scratch[-3:])
            xchg.start_at_first_step(grid, *xrefs)
        if nk == 1:
            o_ref[...] = lax.dot_general(a_ref[...].astype(bf16), b_ref[...].astype(bf16), dims,
                                         preferred_element_type=f32).astype(o_ref.dtype)
        else:
            acc_ref = scratch[0]
            kk = pl.program_id(2)

            @pl.when(kk == 0)
            def _():
                acc_ref[...] = jnp.zeros_like(acc_ref)

            acc_ref[...] += lax.dot_general(a_ref[...].astype(bf16), b_ref[...].astype(bf16), dims,
                                            preferred_element_type=f32)

            @pl.when(kk == nk - 1)
            def _():
                o_ref[...] = acc_ref[...].astype(o_ref.dtype)
        if nx:
            xchg.wait_at_last_step(grid, *xrefs)

    out = pl.pallas_call(
        body, name=name, grid=grid,
        in_specs=[a_spec, b_spec] + ([_ANY] * nx),
        out_specs=[pl.BlockSpec((tm, tn), lambda j, i, kk: (i, j))] + ([_ANY] * nx),
        out_shape=[jax.ShapeDtypeStruct((m, n), out_dtype)] + (xchg.out_shape() if nx else []),
        scratch_shapes=([pltpu.VMEM((tm, tn), f32)] if nk > 1 else []) + (xchg.scratch() if nx else []),
        compiler_params=_cparams(("arbitrary",) * 3 if nx else ("parallel", "parallel", "arbitrary")),
    )(a, b, *(xchg.arrs if nx else []))
    return (out[0], out[1:]) if nx else out[0]


TOK_TILE = 512
SUB_COLS = 384


def _row_spec(width, tile=TOK_TILE):
    return pl.BlockSpec((tile, width), lambda i: (i, 0))


def _vec_spec(width, rows=1):
    return pl.BlockSpec((rows, width), lambda i: (0, 0))


def _ln_mod_fwd(x, gain, sc, sh, shard, name):
    s, d = x.shape
    nt = s // TOK_TILE
    ride = _ChipGather(shard)

    def body(x_ref, g_ref, sc_ref, sh_ref, sh_in, h_ref, sh_out, *sems):
        i = pl.program_id(0)
        pl.when(i == 0)(lambda: ride.start(sh_in, sh_out, sems))
        xv = x_ref[...]
        rstd = lax.rsqrt(jnp.mean(xv * xv, axis=-1, keepdims=True) + EPS)
        h = (xv * rstd) * g_ref[...] * (1.0 + sc_ref[...]) + sh_ref[...]
        h_ref[...] = h.astype(bf16)
        @pl.when(i == nt - 1)
        def _():
            ride.forward(sh_in, sh_out, sems)
            ride.finish(sh_in, sh_out, sems)

    return pl.pallas_call(
        body, name=name, grid=(nt,),
        in_specs=[_row_spec(d), _vec_spec(d), _vec_spec(d), _vec_spec(d), _ANY],
        out_specs=[_row_spec(d), _ANY],
        out_shape=[jax.ShapeDtypeStruct((s, d), bf16), ride.out_shape()],
        scratch_shapes=ride.scratch(),
        compiler_params=_cparams(("arbitrary",)),
    )(x, gain, sc, sh, shard)


def _proj_resid_ln_mod_fwd(pairs, x, gate, gain, sc, sh, name):
    s, d = x.shape
    npair = len(pairs)

    def body(*refs):
        aw = refs[:2 * npair]
        x_ref, gt_ref, g_ref, sc_ref, sh_ref, y_ref, x1_ref, h_ref = refs[2 * npair:]
        y = jnp.dot(aw[0][...].astype(bf16), aw[1][...], preferred_element_type=f32)
        for t in range(1, npair):
            y = y + jnp.dot(aw[2 * t][...].astype(bf16), aw[2 * t + 1][...], preferred_element_type=f32)
        y_ref[...] = y
        x1 = x_ref[...] + gt_ref[...] * y
        x1_ref[...] = x1
        rstd = lax.rsqrt(jnp.mean(x1 * x1, axis=-1, keepdims=True) + EPS)
        h = (x1 * rstd) * g_ref[...] * (1.0 + sc_ref[...]) + sh_ref[...]
        h_ref[...] = h.astype(bf16)

    aw_specs, aw = [], []
    for a, w in pairs:
        aw_specs += [_row_spec(a.shape[1]), pl.BlockSpec(w.shape, lambda i: (0, 0))]
        aw += [a, w]
    return pl.pallas_call(
        body, name=name, grid=(s // TOK_TILE,),
        in_specs=aw_specs + [_row_spec(d)] + [_vec_spec(d)] * 4,
        out_specs=[_row_spec(d)] * 3,
        out_shape=[jax.ShapeDtypeStruct((s, d), f32)] * 2 + [jax.ShapeDtypeStruct((s, d), bf16)],
        compiler_params=_cparams(("parallel",)),
    )(*aw, x, gate, gain, sc, sh)


FFN_TN = 1408


def _ffn_up(h2, w_gate, w_up, name):
    s, d = h2.shape
    tm = TOK_TILE

    def body(h_ref, wg_ref, wu_ref, a_ref, g_ref, u_ref):
        h = h_ref[...]
        g = jnp.dot(h, wg_ref[...], preferred_element_type=f32)
        u = jnp.dot(h, wu_ref[...], preferred_element_type=f32)
        a_ref[...] = (g * jax.nn.sigmoid(g) * u).astype(bf16)
        g_ref[...] = g.astype(bf16)
        u_ref[...] = u.astype(bf16)

    w_spec = pl.BlockSpec((d, FFN_TN), lambda j, i: (0, j))
    o_spec = pl.BlockSpec((tm, FFN_TN), lambda j, i: (i, j))
    return pl.pallas_call(
        body, name=name, grid=(D_FF // FFN_TN, s // tm),
        in_specs=[pl.BlockSpec((tm, d), lambda j, i: (i, 0)), w_spec, w_spec],
        out_specs=[o_spec] * 3,
        out_shape=[jax.ShapeDtypeStruct((s, D_FF), bf16)] * 3,
        compiler_params=_cparams(("parallel", "parallel")),
    )(h2, w_gate, w_up)


def _ffn_down_dx(dy2, w_down, gate, up, name):
    s, d = dy2.shape
    tm = TOK_TILE

    def body(dy_ref, w_ref, g_ref, u_ref, dg_ref, du_ref):
        dy = dy_ref[...]
        for c0 in range(0, FFN_TN, SUB_COLS):
            cols = slice(c0, min(c0 + SUB_COLS, FFN_TN))
            da = lax.dot_general(dy, w_ref[cols, :], _NT, preferred_element_type=f32)
            g = g_ref[:, cols].astype(f32)
            sg = jax.nn.sigmoid(g)
            du_ref[:, cols] = (da * g * sg).astype(bf16)
            dg_ref[:, cols] = (da * u_ref[:, cols].astype(f32) * sg * (1.0 + g * (1.0 - sg))).astype(bf16)

    t_spec = pl.BlockSpec((tm, FFN_TN), lambda j, i: (i, j))
    return pl.pallas_call(
        body, name=name, grid=(D_FF // FFN_TN, s // tm),
        in_specs=[pl.BlockSpec((tm, d), lambda j, i: (i, 0)), pl.BlockSpec((FFN_TN, d), lambda j, i: (j, 0)),
                  t_spec, t_spec],
        out_specs=[t_spec, t_spec],
        out_shape=[jax.ShapeDtypeStruct((s, D_FF), bf16)] * 2,
        compiler_params=_cparams(("parallel", "parallel")),
    )(dy2, w_down, gate, up)


def _acc_spec(width):
    return pl.BlockSpec((1, width), lambda i: (0, 0))


def _proj_final_loss_bwd(a, w, x1, gate2, final_g, target, name):
    s, d = x1.shape
    k = a.shape[1]

    def body(a_ref, w_ref, x1_ref, gt_ref, fg_ref, tg_ref, dx2_ref, dy2_ref, loss_ref, dfg_ref, dgt_ref):
        @pl.when(pl.program_id(0) == 0)
        def _():
            loss_ref[...] = jnp.zeros_like(loss_ref)
            dfg_ref[...] = jnp.zeros_like(dfg_ref)
            dgt_ref[...] = jnp.zeros_like(dgt_ref)

        y2 = jnp.dot(a_ref[...], w_ref[...], preferred_element_type=f32)
        gt = gt_ref[...]
        fg = fg_ref[...]
        x2 = x1_ref[...] + gt * y2
        rstd = lax.rsqrt(jnp.mean(x2 * x2, axis=-1, keepdims=True) + EPS)
        xn = x2 * rstd
        err = xn * fg - tg_ref[...]
        row = jnp.sum(err * err, axis=-1, keepdims=True) * (0.5 / d)
        loss_ref[...] += jnp.sum(row, axis=0, keepdims=True) + jnp.zeros_like(loss_ref)
        dout = err * (1.0 / d)
        dfg_ref[...] += jnp.sum(dout * xn, axis=0, keepdims=True)
        dxn = dout * fg
        dx2 = rstd * (dxn - xn * jnp.mean(dxn * xn, axis=-1, keepdims=True))
        dx2_ref[...] = dx2
        dgt_ref[...] += jnp.sum(dx2 * y2, axis=0, keepdims=True)
        dy2_ref[...] = (gt * dx2).astype(bf16)

    return pl.pallas_call(
        body, name=name, grid=(s // TOK_TILE,),
        in_specs=[_row_spec(k), pl.BlockSpec((k, d), lambda i: (0, 0)), _row_spec(d), _vec_spec(d), _vec_spec(d),
                  _row_spec(d)],
        out_specs=[_row_spec(d), _row_spec(d), _acc_spec(128), _acc_spec(d), _acc_spec(d)],
        out_shape=[jax.ShapeDtypeStruct((s, d), f32), jax.ShapeDtypeStruct((s, d), bf16),
                   jax.ShapeDtypeStruct((1, 128), f32), jax.ShapeDtypeStruct((1, d), f32),
                   jax.ShapeDtypeStruct((1, d), f32)],
        compiler_params=_cparams(("arbitrary",)),
    )(a, w, x1, gate2, final_g, target)


def _proj_ln_mod_bwd(pairs, xin, gain, sc, dres, tm, name, xchg, gate=None, y=None):
    s, d = xin.shape
    with_gate = gate is not None
    npair = len(pairs)
    n_in = 2 * npair + (7 if with_gate else 5) - 1
    n_out = 6 if with_gate else 4

    def body(*refs):
        ab = refs[:2 * npair]
        if with_gate:
            (x_ref, g_ref, sc_ref, dr_ref, gt_ref, y_ref,
             dx_ref, dsh_ref, dsc_ref, dg_ref, dy_ref, dgt_ref) = refs[2 * npair:]
        else:
            x_ref, g_ref, sc_ref, dr_ref, dx_ref, dsh_ref, dsc_ref, dg_ref = refs[2 * npair:]

        @pl.when(pl.program_id(0) == 0)
        def _():
            dsh_ref[...] = jnp.zeros_like(dsh_ref)
            dsc_ref[...] = jnp.zeros_like(dsc_ref)
            dg_ref[...] = jnp.zeros_like(dg_ref)
            if with_gate:
                dgt_ref[...] = jnp.zeros_like(dgt_ref)

        dh = lax.dot_general(ab[0][...].astype(bf16), ab[1][...], _NT, preferred_element_type=f32)
        for t in range(1, npair):
            dh = dh + lax.dot_general(ab[2 * t][...].astype(bf16), ab[2 * t + 1][...], _NT,
                                      preferred_element_type=f32)
        xv = x_ref[...]
        g = g_ref[...]
        sc1 = 1.0 + sc_ref[...]
        rstd = lax.rsqrt(jnp.mean(xv * xv, axis=-1, keepdims=True) + EPS)
        xn = xv * rstd
        dsh_ref[...] += jnp.sum(dh, axis=0, keepdims=True)
        dsc_ref[...] += jnp.sum(dh * (xn * g), axis=0, keepdims=True)
        dg_ref[...] += jnp.sum(dh * sc1 * xn, axis=0, keepdims=True)
        dxn = dh * sc1 * g
        dx = dr_ref[...] + rstd * (dxn - xn * jnp.mean(dxn * xn, axis=-1, keepdims=True))
        dx_ref[...] = dx
        if with_gate:
            dgt_ref[...] += jnp.sum(dx * y_ref[...], axis=0, keepdims=True)
            dy_ref[...] = (gt_ref[...] * dx).astype(bf16)

    row = lambda width: pl.BlockSpec((tm, width), lambda i: (i, 0))
    in_specs, args = [], []
    for a, b in pairs:
        in_specs += [row(a.shape[1]), pl.BlockSpec(b.shape, lambda i: (0, 0))]
        args += [a, b]
    in_specs += [row(d), _vec_spec(d), _vec_spec(d), row(d)]
    args += [xin, gain, sc, dres]
    out_specs = [row(d), _acc_spec(d), _acc_spec(d), _acc_spec(d)]
    out_shape = [jax.ShapeDtypeStruct((s, d), f32)] + [jax.ShapeDtypeStruct((1, d), f32)] * 3
    if with_gate:
        in_specs += [_vec_spec(d), row(d)]
        out_specs += [row(d), _acc_spec(d)]
        out_shape += [jax.ShapeDtypeStruct((s, d), bf16), jax.ShapeDtypeStruct((1, d), f32)]
        args += [gate, y]
    grid = (s // tm,)
    out = pl.pallas_call(
        _ride(body, n_in, n_out, xchg, grid), name=name, grid=grid,
        in_specs=in_specs + [_ANY] * xchg.n, out_specs=out_specs + [_ANY] * xchg.n,
        out_shape=out_shape + xchg.out_shape(), scratch_shapes=xchg.scratch(),
        compiler_params=_cparams(("arbitrary",)),
    )(*args, *xchg.arrs)
    return out[:n_out], out[n_out:]


def _bucket_tables():
    import numpy as np
    qi = np.arange(BAND)[:, None]
    kj = np.arange(2 * BAND)[None, :]
    steps = qi + BAND - kj
    max_exact = N_BUCKETS // 2
    out = []
    for d in DILATIONS:
        dist = np.maximum(steps, 0) * d
        dist_f = np.maximum(dist, 1).astype(np.float32)
        large = max_exact + (np.log(dist_f / np.float32(max_exact)) / np.float32(math.log(MAX_DISTANCE / max_exact))
                             * np.float32(N_BUCKETS - max_exact)).astype(np.int32)
        out.append(np.where(dist < max_exact, dist, np.minimum(large, N_BUCKETS - 1)))
    return jnp.asarray(np.stack(out).astype(np.int32))


def _bias_tables(rel_bias, idx):
    def body(idx_ref, rb_ref, o_ref):
        h = pl.program_id(1)
        idxv = idx_ref[0]
        acc = jnp.zeros((BAND, 2 * BAND), f32)
        for b in range(N_BUCKETS):
            acc = jnp.where(idxv == b, rb_ref[b, h], acc)
        o_ref[0, 0] = jnp.where(_attn_masks()[1], acc, NEG_INF)

    return pl.pallas_call(
        body, name="attn_bias_tables", grid=(3, N_HEADS),
        in_specs=[pl.BlockSpec((1, BAND, 2 * BAND), lambda br, h: (br, 0, 0)),
                  pl.BlockSpec(memory_space=pltpu.SMEM)],
        out_specs=pl.BlockSpec((1, 1, BAND, 2 * BAND), lambda br, h: (br, h, 0, 0)),
        out_shape=jax.ShapeDtypeStruct((3, N_HEADS, BAND, 2 * BAND), f32),
        compiler_params=_cparams(("parallel", "parallel")),
    )(idx, rel_bias)


def _bias_grad(dbias, idx):
    def body(idx_ref, db_ref, o_ref):
        br = pl.program_id(1)

        @pl.when(br == 0)
        def _():
            o_ref[...] = jnp.zeros_like(o_ref)

        idxv = idx_ref[0]
        dbv = db_ref[0, 0]
        row = lax.broadcasted_iota(jnp.int32, (N_BUCKETS, 128), 0)
        acc = jnp.zeros((N_BUCKETS, 128), f32)
        for b in range(N_BUCKETS):
            sb = jnp.sum(jnp.sum(jnp.where(idxv == b, dbv, 0.0), axis=1, keepdims=True), axis=0, keepdims=True)
            acc = acc + jnp.where(row == b, sb, 0.0)
        o_ref[0] += acc

    return pl.pallas_call(
        body, name="attn_bias_grad", grid=(N_HEADS, 3),
        in_specs=[pl.BlockSpec((1, BAND, 2 * BAND), lambda h, br: (br, 0, 0)),
                  pl.BlockSpec((1, 1, BAND, 2 * BAND), lambda h, br: (br, h, 0, 0))],
        out_specs=pl.BlockSpec((1, N_BUCKETS, 128), lambda h, br: (h, 0, 0)),
        out_shape=jax.ShapeDtypeStruct((N_HEADS, N_BUCKETS, 128), f32),
        compiler_params=_cparams(("parallel", "arbitrary")),
    )(idx, dbias)


def _attn_masks():
    lane = lax.broadcasted_iota(jnp.int32, (BAND, 128), 1)
    m0 = lane < HEAD_DIM
    qi = lax.broadcasted_iota(jnp.int32, (BAND, 2 * BAND), 0)
    kj = lax.broadcasted_iota(jnp.int32, (BAND, 2 * BAND), 1)
    steps = qi + BAND - kj
    in_window = (steps >= 0) & (steps <= BAND)
    return m0, in_window, kj >= BAND


_NT = (((1,), (1,)), ((), ()))
_TN = (((0,), (0,)), ((), ()))
_BNN = (((2,), (1,)), ((0,), (0,)))
_BNT = (((2,), (2,)), ((0,), (0,)))
_BTN = (((1,), (1,)), ((0,), (0,)))
ATTN_GROUP = 4
ATTN_ITEMS = PAD_UNIT // BAND
Q_COL, K_COL, V_COL = 0, 4, 8


def _attn_item_rows(j, d, c, cbase):
    r = lax.rem(j, d)
    b = lax.div(j, d)
    loc = b * (d * BAND) + r
    first = jnp.logical_and(c == 0, b == 0)
    start = cbase + loc
    pstart = jnp.where(first, start, start - d * BAND)
    return loc, start, pstart, first


def _attn_fwd(proj, bias, xchg):
    s = proj.shape[0]

    def body(q_ref, k_ref, v_ref, b_ref, y_ref, lse_ref, o_s, l_s):
        c = pl.program_id(1)
        cbase = pl.multiple_of(c * PAD_UNIT, PAD_UNIT)
        m0, in_window, cur_half = _attn_masks()
        for bi, d in enumerate(DILATIONS):
            def group(jg, carry, bi=bi, d=d):
                locs, qs, ks, vs, pens = [], [], [], [], []
                for t in range(ATTN_GROUP):
                    loc, start, pstart, first = _attn_item_rows(jg * ATTN_GROUP + t, d, c, cbase)
                    locs.append(loc)
                    qs.append(q_ref[pl.ds(loc, BAND, stride=d), :])
                    ks.append(jnp.concatenate([k_ref[pl.ds(pstart, BAND, stride=d), :],
                                               k_ref[pl.ds(start, BAND, stride=d), :]], axis=0))
                    vs.append(jnp.concatenate([v_ref[pl.ds(pstart, BAND, stride=d), :],
                                               v_ref[pl.ds(start, BAND, stride=d), :]], axis=0))
                    pens.append(jnp.where(cur_half, 0.0, jnp.where(first, NEG_INF, 0.0)))
                q = jnp.stack(qs)
                kk = jnp.stack(ks + ks).astype(bf16)
                vv = jnp.stack(vs + vs).astype(bf16)
                pen = jnp.stack(pens + pens)
                qh = (jnp.concatenate([jnp.where(m0, q, 0.0), jnp.where(m0, 0.0, q)], axis=0) * 0.125).astype(bf16)
                sc = lax.dot_general(qh, kk, _BNT, preferred_element_type=f32)
                sc = (sc.reshape(2, ATTN_GROUP, BAND, 2 * BAND) + b_ref[bi][:, None]).reshape(sc.shape) + pen
                mx = jnp.max(sc, axis=-1, keepdims=True)
                e = jnp.exp(sc - mx)
                l = jnp.sum(e, axis=-1, keepdims=True)
                o = lax.dot_general(e.astype(bf16), vv, _BNN, preferred_element_type=f32) * (1.0 / l)
                ls = mx + jnp.log(l)
                for t in range(ATTN_GROUP):
                    rows = pl.ds(locs[t], BAND, stride=d)
                    o_s[bi, rows, :] = jnp.where(m0, o[t], o[ATTN_GROUP + t])
                    l_s[bi, rows, :] = jnp.where(m0, ls[t], ls[ATTN_GROUP + t])
                return carry

            lax.fori_loop(0, ATTN_ITEMS // ATTN_GROUP, group, 0)

        def merge(t, carry):
            rows = pl.ds(pl.multiple_of(t * 256, 256), 256)
            ls = [l_s[i, rows, :] for i in range(3)]
            mx = jnp.maximum(jnp.maximum(ls[0], ls[1]), ls[2])
            ws = [jnp.exp(l - mx) for l in ls]
            tot = ws[0] + ws[1] + ws[2]
            y = (ws[0] * o_s[0, rows, :] + ws[1] * o_s[1, rows, :] + ws[2] * o_s[2, rows, :]) / tot
            y_ref[rows, :] = y
            lse_ref[rows, :] = mx + jnp.log(tot)
            return carry

        lax.fori_loop(0, PAD_UNIT // 256, merge, 0)

    chunk = lambda col: pl.BlockSpec((PAD_UNIT, 128), lambda p, c: (c, col + p))
    full = lambda col: pl.BlockSpec((s, 128), lambda p, c: (0, col + p))
    grid = (N_HEADS // 2, s // PAD_UNIT)
    out = pl.pallas_call(
        _ride(body, 4, 2, xchg, grid), name="attn_fwd", grid=grid,
        in_specs=[chunk(Q_COL), full(K_COL), full(V_COL),
                  pl.BlockSpec((3, 2, BAND, 2 * BAND), lambda p, c: (0, p, 0, 0))] + [_ANY] * xchg.n,
        out_specs=[chunk(0), chunk(0)] + [_ANY] * xchg.n,
        out_shape=[jax.ShapeDtypeStruct((s, GROUP_W), f32)] * 2 + xchg.out_shape(),
        scratch_shapes=[pltpu.VMEM((3, PAD_UNIT, 128), f32)] * 2 + xchg.scratch(),
        compiler_params=_cparams(("arbitrary", "arbitrary")),
    )(proj, proj, proj, bias, *xchg.arrs)
    return out[:2], out[2:]


def _attn_bwd(proj, bias, y, lse, dycat):
    s = proj.shape[0]

    def body(q_ref, k_ref, v_ref, b_ref, y_ref, lse_ref, dy_ref, dq_ref, dk_ref, dv_ref, db_ref, dd_s):
        c = pl.program_id(1)
        cbase = pl.multiple_of(c * PAD_UNIT, PAD_UNIT)
        m0, in_window, cur_half = _attn_masks()

        @pl.when(c == 0)
        def _():
            dk_ref[...] = jnp.zeros_like(dk_ref)
            dv_ref[...] = jnp.zeros_like(dv_ref)
            db_ref[...] = jnp.zeros_like(db_ref)

        dq_ref[...] = jnp.zeros_like(dq_ref)

        def rowdot(t, carry):
            rows = pl.ds(pl.multiple_of(t * 256, 256), 256)
            prod = dy_ref[rows, :] * y_ref[rows, :]
            lane = lax.broadcasted_iota(jnp.int32, prod.shape, 1)
            h0 = lane < HEAD_DIM
            d0 = jnp.sum(jnp.where(h0, prod, 0.0), axis=-1, keepdims=True)
            d1 = jnp.sum(jnp.where(h0, 0.0, prod), axis=-1, keepdims=True)
            dd_s[rows, :] = jnp.where(h0, d0, d1)
            return carry

        lax.fori_loop(0, PAD_UNIT // 256, rowdot, 0)

        for bi, d in enumerate(DILATIONS):
            def group(jg, carry, bi=bi, d=d):
                ng = ATTN_GROUP
                meta, qs, dos, lqs, dds, ks, vs, pens = [], [], [], [], [], [], [], []
                for t in range(ng):
                    loc, start, pstart, first = _attn_item_rows(jg * ng + t, d, c, cbase)
                    qrows = pl.ds(loc, BAND, stride=d)
                    rows = pl.ds(start, BAND, stride=d)
                    prows = pl.ds(pstart, BAND, stride=d)
                    meta.append((qrows, rows, prows))
                    qs.append(q_ref[qrows, :])
                    dos.append(dy_ref[qrows, :])
                    lqs.append(lse_ref[qrows, :])
                    dds.append(dd_s[qrows, :])
                    ks.append(jnp.concatenate([k_ref[prows, :], k_ref[rows, :]], axis=0))
                    vs.append(jnp.concatenate([v_ref[prows, :], v_ref[rows, :]], axis=0))
                    pens.append(jnp.where(cur_half, 0.0, jnp.where(first, NEG_INF, 0.0)))

                def heads(t):
                    return jnp.concatenate([jnp.where(m0, t, 0.0), jnp.where(m0, 0.0, t)], axis=0)

                def head_col(t):
                    return jnp.concatenate([t[:, :, 0:1], t[:, :, HEAD_DIM:HEAD_DIM + 1]], axis=0)

                qh = (heads(jnp.stack(qs)) * 0.125).astype(bf16)
                doh = heads(jnp.stack(dos)).astype(bf16)
                kk = jnp.stack(ks + ks).astype(bf16)
                vv = jnp.stack(vs + vs).astype(bf16)
                sc = lax.dot_general(qh, kk, _BNT, preferred_element_type=f32)
                sc = (sc.reshape(2, ng, BAND, 2 * BAND) + b_ref[bi][:, None]).reshape(sc.shape) + jnp.stack(pens + pens)
                p = jnp.exp(sc - head_col(jnp.stack(lqs)))
                dp = lax.dot_general(doh, vv, _BNT, preferred_element_type=f32)
                ds = p * (dp - head_col(jnp.stack(dds)))
                db_ref[bi] += jnp.sum(ds.reshape(2, ng, BAND, 2 * BAND), axis=1)
                dsb = ds.astype(bf16)
                dq = lax.dot_general(dsb, kk, _BNN, preferred_element_type=f32) * 0.125
                dk = lax.dot_general(dsb, qh, _BTN, preferred_element_type=f32)
                dv = lax.dot_general(p.astype(bf16), doh, _BTN, preferred_element_type=f32)
                for t in range(ng):
                    qrows, rows, prows = meta[t]
                    dq_ref[qrows, :] += jnp.where(m0, dq[t], dq[ng + t])
                    dkt = dk[t] + dk[ng + t]
                    dvt = dv[t] + dv[ng + t]
                    dk_ref[prows, :] += dkt[:BAND]
                    dk_ref[rows, :] += dkt[BAND:]
                    dv_ref[prows, :] += dvt[:BAND]
                    dv_ref[rows, :] += dvt[BAND:]
                return carry

            lax.fori_loop(0, ATTN_ITEMS // ATTN_GROUP, group, 0)

    chunk = lambda col: pl.BlockSpec((PAD_UNIT, 128), lambda p, c: (c, col + p))
    full = lambda col: pl.BlockSpec((s, 128), lambda p, c: (0, col + p))
    bias_spec = pl.BlockSpec((3, 2, BAND, 2 * BAND), lambda p, c: (0, p, 0, 0))
    return pl.pallas_call(
        body, name="attn_bwd", grid=(N_HEADS // 2, s // PAD_UNIT),
        in_specs=[chunk(Q_COL), full(K_COL), full(V_COL), bias_spec, chunk(0), chunk(0), chunk(0)],
        out_specs=[chunk(0), full(0), full(0), bias_spec],
        out_shape=[jax.ShapeDtypeStruct((s, GROUP_W), f32)] * 3
        + [jax.ShapeDtypeStruct((3, N_HEADS, BAND, 2 * BAND), f32)],
        scratch_shapes=[pltpu.VMEM((PAD_UNIT, 128), f32)],
        compiler_params=_cparams(("parallel", "arbitrary")),
    )(proj, proj, proj, bias, y, lse, dycat)


_HI = lax.Precision.HIGHEST
DELTA_COL = 1536
Z_COL = 3072
BA_BLOCK = 28
DELTA_ROWS = 1024


def _hdot(a, b):
    return jnp.dot(a, b, precision=_HI, preferred_element_type=f32)


_DIMS = dict(nn=(((2,), (1,)), ((0,), (0,))), nt=(((2,), (2,)), ((0,), (0,))), tn=(((1,), (1,)), ((0,), (0,))))


@functools.partial(jax.custom_vjp, nondiff_argnums=(2,))
def _mmx(a, b, mode):
    return lax.dot_general(a.astype(bf16), b.astype(bf16), _DIMS[mode], preferred_element_type=f32)


def _mmx_fwd(a, b, mode):
    return _mmx(a, b, mode), (a, b)


def _mmx_bwd(mode, res, g):
    a, b = res
    if mode == "nn":
        return _mmx(g, b, "nt"), _mmx(a, g, "tn")
    if mode == "nt":
        return _mmx(g, b, "nn"), _mmx(g, a, "tn")
    return _mmx(b, g, "nt"), _mmx(a, g, "nn")


_mmx.defvjp(_mmx_fwd, _mmx_bwd)


def _pair_iota():
    row = lax.broadcasted_iota(jnp.int32, (CHUNK, 128), 0)
    lane = lax.broadcasted_iota(jnp.int32, (CHUNK, 128), 1)
    return row, lane, lane & (CHUNK - 1)


def _bd(x):
    _, lane, _ = _pair_iota()
    m0 = lane < CHUNK
    return jnp.concatenate([jnp.where(m0, x, 0.0), jnp.where(m0, 0.0, x)], axis=1)


def _pmm(a, b):
    return _mmx(a, _bd(b), "nn")


def _ntp(x, y):
    return _mmx(x, _bd(y), "nt")


def _tnp(x, y):
    full = _mmx(x, y, "tn")
    _, lane, _ = _pair_iota()
    return jnp.where(lane < CHUNK, full[:, :CHUNK], full[:, CHUNK:])


def _tri_inv(a):
    row, lane, jj = _pair_iota()
    eye = jnp.where(row == jj, 1.0, 0.0).astype(f32)

    def same_block(log2b):
        return (row >> log2b) == (jj >> log2b)

    dgl = jnp.where(same_block(3), a, 0.0)
    d2 = _pmm(dgl, dgl)
    d4 = _pmm(d2, d2)
    t = _pmm(_pmm(eye - dgl, eye + d2), eye + d4)
    for lb in (3, 4, 5):
        off = jnp.where(same_block(lb + 1) & jnp.logical_not(same_block(lb)), a, 0.0)
        t = t - _pmm(_pmm(t, off), t)
    return t


@jax.custom_vjp
def _solve2(a, xv, xk, t):
    return _pmm(t, xv), _pmm(t, xk)


def _solve2_fwd(a, xv, xk, t):
    u, w = _pmm(t, xv), _pmm(t, xk)
    return (u, w), (t, u, w)


def _solve2_bwd(res, cts):
    t, u, w = res
    du, dw = cts
    dxv = _tnp(t, du)
    dxk = _tnp(t, dw)
    return -(_ntp(dxv, u) + _ntp(dxk, w)), dxv, dxk, jnp.zeros_like(t)


_solve2.defvjp(_solve2_fwd, _solve2_bwd)


def _chunk_pre(qp, kp, vp, bp, gcum, t=None):
    row, lane, jj = _pair_iota()
    causal = row >= jj
    strict = row > jj
    rsel = jnp.sum(jnp.where(row == jj, gcum, 0.0), axis=1, keepdims=True)
    decay = jnp.where(causal, jnp.exp(jnp.where(causal, gcum - rsel, 0.0)), 0.0)
    kb = kp * bp
    kd = _bd(kp)
    a = jnp.where(strict, _mmx(kb, kd, "nt") * decay, 0.0)
    eg = jnp.exp(gcum)
    if t is None:
        t = _tri_inv(a)
    u, w = _solve2(a, vp * bp, kb * eg, t)
    qk = jnp.where(causal, _mmx(qp, kd, "nt") * decay, 0.0)
    glast = jnp.sum(jnp.where(row == CHUNK - 1, gcum, 0.0), axis=1, keepdims=True)
    return u, w, qp * eg, kp * jnp.exp(glast - gcum), qk, jnp.exp(glast), t


def _chunk_post(u, w, qt, kh, qk, gam, sp):
    sd = _bd(sp)
    vnew = u - _mmx(w, sd, "nn")
    o = _mmx(qt, sd, "nn") + _pmm(qk, vnew)
    return o, gam * sp + _tnp(kh, vnew)


def _pair_spec(rows=DELTA_ROWS):
    return pl.BlockSpec((rows, 128), lambda i, p: (i, p))


DELTA_NB = DELTA_ROWS // CHUNK


def _chunks(ref):
    return ref[...].reshape(DELTA_NB, CHUNK, 128)


def _pairs(ref, rows):
    return jnp.stack([ref[rows, p * 128:(p + 1) * 128] for p in range(4)], axis=0)


def _delta_chunk_pre(qn, kn, sv, beta, g, xchg):
    s = qn.shape[0]

    def body(q_ref, k_ref, v_ref, b_ref, g_ref, u_ref, w_ref, qt_ref, kh_ref, qk_ref, t_ref, gm_ref):
        outs = _chunk_pre(_chunks(q_ref), _chunks(k_ref), _chunks(v_ref), _chunks(b_ref), _chunks(g_ref))
        for ref, val in zip((u_ref, w_ref, qt_ref, kh_ref, qk_ref, t_ref), outs[:5] + outs[6:]):
            ref[...] = val.reshape(DELTA_ROWS, 128).astype(ref.dtype)
        gm_ref[...] = jnp.broadcast_to(outs[5], (DELTA_NB, 8, 128)).reshape(DELTA_NB * 8, 128)

    v_spec = pl.BlockSpec((DELTA_ROWS, 128), lambda i, p: (i, 8 + p))
    grid = (s // DELTA_ROWS, 4)
    out = pl.pallas_call(
        _ride(body, 5, 7, xchg, grid), name="delta_chunk_pre", grid=grid,
        in_specs=[_pair_spec(), _pair_spec(), v_spec, _pair_spec(), _pair_spec()] + [_ANY] * xchg.n,
        out_specs=[_pair_spec()] * 6 + [_pair_spec(DELTA_NB * 8)] + [_ANY] * xchg.n,
        out_shape=[jax.ShapeDtypeStruct((s, GROUP_W), f32)] + [jax.ShapeDtypeStruct((s, GROUP_W), bf16)] * 5
        + [jax.ShapeDtypeStruct((s // 8, GROUP_W), f32)] + xchg.out_shape(),
        scratch_shapes=xchg.scratch(),
        compiler_params=_cparams(("arbitrary", "arbitrary")),
    )(qn, kn, sv, beta, g, *xchg.arrs)
    return out[:7], out[7:]


def _delta_scan_fwd(u, w, qt, kh, qk, gm):
    s = u.shape[0]

    def body(u_ref, w_ref, qt_ref, kh_ref, qk_ref, gm_ref, o_ref, ss_ref, st):
        @pl.when(pl.program_id(0) == 0)
        def _():
            st[...] = jnp.zeros_like(st)

        def chunk(ci, carry):
            rows = pl.ds(pl.multiple_of(ci * CHUNK, CHUNK), CHUNK)
            grow = pl.ds(pl.multiple_of(ci * 8, 8), 1)
            sp = st[...]
            o, s2 = _chunk_post(_pairs(u_ref, rows), _pairs(w_ref, rows), _pairs(qt_ref, rows),
                                _pairs(kh_ref, rows), _pairs(qk_ref, rows), _pairs(gm_ref, grow), sp)
            for p in range(4):
                ss_ref[rows, p * 128:(p + 1) * 128] = sp[p]
                o_ref[rows, p * 128:(p + 1) * 128] = o[p]
            st[...] = s2
            return carry

        lax.fori_loop(0, DELTA_NB, chunk, 0)

    spec = pl.BlockSpec((DELTA_ROWS, GROUP_W), lambda i: (i, 0))
    gspec = pl.BlockSpec((DELTA_NB * 8, GROUP_W), lambda i: (i, 0))
    return pl.pallas_call(
        body, name="delta_scan_fwd", grid=(s // DELTA_ROWS,),
        in_specs=[spec] * 5 + [gspec],
        out_specs=[spec, spec],
        out_shape=[jax.ShapeDtypeStruct((s, GROUP_W), f32)] * 2,
        scratch_shapes=[pltpu.VMEM((4, CHUNK, 128), f32)],
        compiler_params=_cparams(("arbitrary",)),
    )(u, w, qt, kh, qk, gm)


def _delta_scan_bwd(w, qt, kh, qk, gm, do, xchg):
    s = w.shape[0]
    nb = s // DELTA_ROWS

    def body(w_ref, qt_ref, kh_ref, qk_ref, gm_ref, do_ref, dso_ref, dst):
        @pl.when(pl.program_id(0) == 0)
        def _():
            dst[...] = jnp.zeros_like(dst)

        def chunk(t, carry):
            ci = DELTA_NB - 1 - t
            rows = pl.ds(pl.multiple_of(ci * CHUNK, CHUNK), CHUNK)
            grow = pl.ds(pl.multiple_of(ci * 8, 8), 1)
            ds = dst[...]
            for p in range(4):
                dso_ref[rows, p * 128:(p + 1) * 128] = ds[p]
            do = _pairs(do_ref, rows)
            dvn = _tnp(_pairs(qk_ref, rows), do) + _pmm(_pairs(kh_ref, rows), ds)
            dst[...] = _tnp(_pairs(qt_ref, rows), do) + _pairs(gm_ref, grow) * ds - _tnp(_pairs(w_ref, rows), dvn)
            return carry

        lax.fori_loop(0, DELTA_NB, chunk, 0)

    spec = pl.BlockSpec((DELTA_ROWS, GROUP_W), lambda i: (nb - 1 - i, 0))
    gspec = pl.BlockSpec((DELTA_NB * 8, GROUP_W), lambda i: (nb - 1 - i, 0))
    out = pl.pallas_call(
        _ride(body, 6, 1, xchg, (nb,)), name="delta_scan_bwd", grid=(nb,),
        in_specs=[spec] * 4 + [gspec, spec] + [_ANY] * xchg.n,
        out_specs=[spec] + [_ANY] * xchg.n,
        out_shape=[jax.ShapeDtypeStruct((s, GROUP_W), f32)] + xchg.out_shape(),
        scratch_shapes=[pltpu.VMEM((4, CHUNK, 128), f32)] + xchg.scratch(),
        compiler_params=_cparams(("arbitrary",)),
    )(w, qt, kh, qk, gm, do, *xchg.arrs)
    return out[0], out[1:]


def _delta_chunk_bwd(qn, kn, sv, beta, g, tinv, ss, dso, do):
    s = qn.shape[0]

    def body(q_ref, k_ref, v_ref, b_ref, g_ref, t_ref, ss_ref, dso_ref, do_ref,
             dq_ref, dk_ref, dv_ref, db_ref, dg_ref):
        sp = _chunks(ss_ref)
        t = _chunks(t_ref)

        def fn(q, k, v, b, gg):
            return _chunk_post(*_chunk_pre(q, k, v, b, gg, t)[:6], sp)

        _, vjp = jax.vjp(fn, _chunks(q_ref), _chunks(k_ref), _chunks(v_ref), _chunks(b_ref), _chunks(g_ref))
        grads = vjp((_chunks(do_ref), _chunks(dso_ref)))
        for ref, val in zip((dq_ref, dk_ref, dv_ref, db_ref, dg_ref), grads):
            ref[...] = val.reshape(DELTA_ROWS, 128)

    v_spec = pl.BlockSpec((DELTA_ROWS, 128), lambda i, p: (i, 8 + p))
    return pl.pallas_call(
        body, name="delta_chunk_bwd", grid=(s // DELTA_ROWS, 4),
        in_specs=[_pair_spec(), _pair_spec(), v_spec] + [_pair_spec()] * 6,
        out_specs=[_pair_spec()] * 5,
        out_shape=[jax.ShapeDtypeStruct((s, GROUP_W), f32)] * 5,
        compiler_params=_cparams(("parallel", "parallel")),
    )(qn, kn, sv, beta, g, tinv, ss, dso, do)


def _head_sums(x):
    r = lax.broadcasted_iota(jnp.int32, (128, 128), 0)
    c = lax.broadcasted_iota(jnp.int32, (128, 128), 1)
    pair = jnp.where((r >> 6) == (c >> 6), 1.0, 0.0).astype(f32)
    npair = x.shape[1] // 128
    xb = jnp.concatenate([x[None, :, p * 128:(p + 1) * 128] for p in range(npair)], axis=0)
    sums = _mmx(xb, jnp.broadcast_to(pair, (npair, 128, 128)), "nn")
    return jnp.concatenate([sums[p] for p in range(npair)], axis=1)


def _sel_dot(a, b):
    return jnp.dot(a, b, precision=lax.Precision.HIGH, preferred_element_type=f32)


def _expand_matrix(first):
    r = lax.broadcasted_iota(jnp.int32, (128, GROUP_W), 0)
    c = lax.broadcasted_iota(jnp.int32, (128, GROUP_W), 1) >> 6
    return jnp.where(r == c + first, 1.0, 0.0).astype(f32)


@functools.partial(jax.custom_vjp, nondiff_argnums=(1,))
def _expand_heads(ba, first):
    return _sel_dot(ba, _expand_matrix(first))


def _expand_heads_fwd(ba, first):
    return _expand_heads(ba, first), None


def _expand_heads_bwd(first, _, g):
    return (_mmx(g[None], _expand_matrix(first)[None], "nt")[0],)


_expand_heads.defvjp(_expand_heads_fwd, _expand_heads_bwd)


def _softplus(x):
    return jnp.maximum(x, 0.0) + jnp.log(1.0 + jnp.exp(-jnp.abs(x)))


def _prep_fn(sq, sk, ba, alog_e, dt_e):
    qn = sq * lax.rsqrt(_head_sums(sq * sq) + EPS) * (HEAD_DIM ** -0.5)
    kn = sk * lax.rsqrt(_head_sums(sk * sk) + EPS)
    bl = _expand_heads(ba, 0)
    al = _expand_heads(ba, N_HEADS)
    beta = jax.nn.sigmoid(bl)
    g = -jnp.exp(alog_e) * _softplus(al + dt_e)
    nchunk = g.shape[0] // CHUNK
    ri = lax.broadcasted_iota(jnp.int32, (nchunk, CHUNK, CHUNK), 1)
    ci = lax.broadcasted_iota(jnp.int32, (nchunk, CHUNK, CHUNK), 2)
    tril = jnp.where(ri >= ci, 1.0, 0.0).astype(f32)
    gcum = lax.dot_general(tril, g.reshape(nchunk, CHUNK, g.shape[1]), _BNN, precision=lax.Precision.HIGH,
                           preferred_element_type=f32)
    return qn, kn, beta, gcum.reshape(g.shape)


def _gnorm_fn(o, z, ng_e):
    ms = _head_sums(o * o) * (1.0 / HEAD_DIM)
    return o * lax.rsqrt(ms + EPS) * ng_e * (z * jax.nn.sigmoid(z))


def _tok_spec(width, col):
    return pl.BlockSpec((TOK_TILE, width), lambda i: (i, col))


def _conv_taps(xs_ref, w_ref, base, n, cols):
    acc = w_ref[CONV_WIDTH - 1:CONV_WIDTH, cols] * xs_ref[pl.ds(base, n), cols]
    for j in range(CONV_WIDTH - 1):
        acc = acc + w_ref[j:j + 1, cols] * xs_ref[pl.ds(base - (CONV_WIDTH - 1) + j, n), cols]
    return acc


def _conv_silu_fwd(proj, conv_w):
    s = proj.shape[0]
    wd = 3 * GROUP_W
    hb = TOK_TILE // 8

    def body(x_ref, halo_ref, w_ref, o_ref, y_ref, xs):
        inner = pl.program_id(0) > 0

        def lane_block(cb, carry):
            cols = pl.ds(pl.multiple_of(cb * 128, 128), 128)
            xs[0:8, cols] = jnp.where(inner, halo_ref[:, cols], 0.0)
            xs[8:, cols] = x_ref[:, cols]
            y = _conv_taps(xs, w_ref, 8, TOK_TILE, cols)
            y_ref[:, cols] = y
            o_ref[:, cols] = y * jax.nn.sigmoid(y)
            return carry

        lax.fori_loop(0, wd // 128, lane_block, 0)

    return pl.pallas_call(
        body, name="delta_conv_fwd", grid=(s // TOK_TILE,),
        in_specs=[_tok_spec(wd, 1), pl.BlockSpec((8, wd), lambda i: (jnp.maximum(i * hb - 1, 0), 1)),
                  pl.BlockSpec((CONV_WIDTH, wd), lambda i: (0, 0))],
        out_specs=[_tok_spec(wd, 0)] * 2,
        out_shape=[jax.ShapeDtypeStruct((s, wd), f32)] * 2,
        scratch_shapes=[pltpu.VMEM((TOK_TILE + 8, wd), f32)],
        compiler_params=_cparams(("parallel",)),
    )(proj, proj, conv_w)


def _conv_silu_bwd(proj, conv_w, yc, ds3, xchg):
    s = proj.shape[0]
    wd = 3 * GROUP_W
    hb = TOK_TILE // 8
    nt = s // TOK_TILE

    def body(x_ref, hp_ref, y_ref, yn_ref, dq_ref, dk_ref, dv_ref, dqn_ref, dkn_ref, dvn_ref, w_ref,
             dx_ref, dw_ref, xs, dys):
        i = pl.program_id(0)

        @pl.when(i == 0)
        def _():
            dw_ref[...] = jnp.zeros_like(dw_ref)

        last = i == nt - 1
        def lane_block(lb, carry, third, cur, nxt):
            tcols = pl.ds(pl.multiple_of(lb * 128, 128), 128)
            cols = pl.ds(pl.multiple_of(third * GROUP_W + lb * 128, 128), 128)
            xs[0:8, cols] = jnp.where(i > 0, hp_ref[:, cols], 0.0)
            xs[8:, cols] = x_ref[:, cols]
            y = y_ref[:, cols]
            sg = jax.nn.sigmoid(y)
            dy0 = cur[:, tcols] * (sg * (1.0 + y * (1.0 - sg)))
            dys[0:TOK_TILE, cols] = dy0
            yn = yn_ref[:, cols]
            sgn = jax.nn.sigmoid(yn)
            dys[TOK_TILE:, cols] = jnp.where(last, 0.0, nxt[:, tcols]) * (sgn * (1.0 + yn * (1.0 - sgn)))
            dx = w_ref[CONV_WIDTH - 1:CONV_WIDTH, cols] * dy0
            for j in range(CONV_WIDTH - 1):
                dx = dx + w_ref[j:j + 1, cols] * dys[pl.ds(CONV_WIDTH - 1 - j, TOK_TILE), cols]
            dx_ref[:, cols] = dx.astype(dx_ref.dtype)
            for j in range(CONV_WIDTH):
                dw_ref[j:j + 1, cols] += jnp.sum(dy0 * xs[pl.ds(8 - (CONV_WIDTH - 1) + j, TOK_TILE), cols],
                                                 axis=0, keepdims=True)
            return carry

        for third, (cur, nxt) in enumerate(((dq_ref, dqn_ref), (dk_ref, dkn_ref), (dv_ref, dvn_ref))):
            lax.fori_loop(0, GROUP_W // 128, functools.partial(lane_block, third=third, cur=cur, nxt=nxt), 0)

    prev8 = lambda col: pl.BlockSpec((8, wd), lambda i: (jnp.maximum(i * hb - 1, 0), col))
    next8 = lambda col: pl.BlockSpec((8, wd), lambda i: (jnp.minimum((i + 1) * hb, s // 8 - 1), col))
    next8_third = pl.BlockSpec((8, GROUP_W), lambda i: (jnp.minimum((i + 1) * hb, s // 8 - 1), 0))
    out = pl.pallas_call(
        _ride(body, 11, 2, xchg, (nt,)), name="delta_conv_bwd", grid=(nt,),
        in_specs=[_tok_spec(wd, 1), prev8(1), _tok_spec(wd, 0), next8(0)] + [_tok_spec(GROUP_W, 0)] * 3
        + [next8_third] * 3
        + [pl.BlockSpec((CONV_WIDTH, wd), lambda i: (0, 0))] + [_ANY] * xchg.n,
        out_specs=[_tok_spec(wd, 0), pl.BlockSpec((CONV_WIDTH, wd), lambda i: (0, 0))] + [_ANY] * xchg.n,
        out_shape=[jax.ShapeDtypeStruct((s, wd), bf16), jax.ShapeDtypeStruct((CONV_WIDTH, wd), f32)] + xchg.out_shape(),
        scratch_shapes=[pltpu.VMEM((TOK_TILE + 8, wd), f32), pltpu.VMEM((TOK_TILE + 8, wd), f32)] + xchg.scratch(),
        compiler_params=_cparams(("arbitrary",)),
    )(proj, proj, yc, yc, *ds3, *ds3, conv_w, *xchg.arrs)
    return out[:2], out[2:]


def _delta_prep_fwd(sconv, proj, alog_e, dt_e):
    s = sconv.shape[0]

    def body(sq_ref, sk_ref, ba_ref, al_ref, dt_ref, q_ref, k_ref, b_ref, g_ref):
        qn, kn, beta, g = _prep_fn(sq_ref[...], sk_ref[...], ba_ref[...], al_ref[...], dt_ref[...])
        q_ref[...] = qn
        k_ref[...] = kn
        b_ref[...] = beta
        g_ref[...] = g

    return pl.pallas_call(
        body, name="delta_prep_fwd", grid=(s // TOK_TILE,),
        in_specs=[_tok_spec(GROUP_W, 0), _tok_spec(GROUP_W, 1), _tok_spec(128, BA_BLOCK),
                  _vec_spec(GROUP_W), _vec_spec(GROUP_W)],
        out_specs=[_tok_spec(GROUP_W, 0)] * 4,
        out_shape=[jax.ShapeDtypeStruct((s, GROUP_W), f32)] * 4,
        compiler_params=_cparams(("parallel",)),
    )(sconv, sconv, proj, alog_e, dt_e)


def _delta_prep_bwd(sconv, proj, alog_e, dt_e, dqn, dkn, dbeta, dg, xchg):
    s = sconv.shape[0]
    grid = (s // TOK_TILE,)

    def body(sq_ref, sk_ref, ba_ref, al_ref, dt_ref, dq_ref, dk_ref, db_ref, dg_ref,
             dsq_ref, dsk_ref, dba_ref, dal_ref, ddt_ref):
        @pl.when(pl.program_id(0) == 0)
        def _():
            dal_ref[...] = jnp.zeros_like(dal_ref)
            ddt_ref[...] = jnp.zeros_like(ddt_ref)

        _, vjp = jax.vjp(_prep_fn, sq_ref[...], sk_ref[...], ba_ref[...], al_ref[...], dt_ref[...])
        dsq, dsk, dba, dal, ddt = vjp((dq_ref[...], dk_ref[...], db_ref[...], dg_ref[...]))
        dsq_ref[...] = dsq
        dsk_ref[...] = dsk
        dba_ref[...] = dba.astype(bf16)
        dal_ref[...] += dal
        ddt_ref[...] += ddt

    out = pl.pallas_call(
        _ride(body, 9, 5, xchg, grid), name="delta_prep_bwd", grid=grid,
        in_specs=[_tok_spec(GROUP_W, 0), _tok_spec(GROUP_W, 1), _tok_spec(128, BA_BLOCK),
                  _vec_spec(GROUP_W), _vec_spec(GROUP_W)] + [_tok_spec(GROUP_W, 0)] * 4 + [_ANY] * xchg.n,
        out_specs=[_tok_spec(GROUP_W, 0), _tok_spec(GROUP_W, 0), _tok_spec(128, 0),
                   _acc_spec(GROUP_W), _acc_spec(GROUP_W)] + [_ANY] * xchg.n,
        out_shape=[jax.ShapeDtypeStruct((s, GROUP_W), f32)] * 2 + [jax.ShapeDtypeStruct((s, 128), bf16)]
        + [jax.ShapeDtypeStruct((1, GROUP_W), f32)] * 2 + xchg.out_shape(),
        scratch_shapes=xchg.scratch(),
        compiler_params=_cparams(("arbitrary",)),
    )(sconv, sconv, proj, alog_e, dt_e, dqn, dkn, dbeta, dg, *xchg.arrs)
    return out[:5], out[5:]


def _gnorm_fwd(o, proj, ng_e):
    s = o.shape[0]

    def body(o_ref, z_ref, g_ref, y_ref):
        y_ref[...] = _gnorm_fn(o_ref[...], z_ref[...], g_ref[...])

    return pl.pallas_call(
        body, name="delta_gnorm_fwd", grid=(s // TOK_TILE,),
        in_specs=[_tok_spec(GROUP_W, 0), _tok_spec(GROUP_W, Z_COL // GROUP_W), _vec_spec(GROUP_W)],
        out_specs=_tok_spec(GROUP_W, 0),
        out_shape=jax.ShapeDtypeStruct((s, GROUP_W), f32),
        compiler_params=_cparams(("parallel",)),
    )(o, proj, ng_e)


def _gnorm_bwd(o, proj, ng_e, dycat):
    s = o.shape[0]

    def body(o_ref, z_ref, g_ref, dy_ref, do_ref, dz_ref, dg_ref):
        @pl.when(pl.program_id(0) == 0)
        def _():
            dg_ref[...] = jnp.zeros_like(dg_ref)

        _, vjp = jax.vjp(_gnorm_fn, o_ref[...], z_ref[...], g_ref[...])
        do, dz, dg = vjp(dy_ref[...])
        do_ref[...] = do
        dz_ref[...] = dz.astype(bf16)
        dg_ref[...] += dg

    return pl.pallas_call(
        body, name="delta_gnorm_bwd", grid=(s // TOK_TILE,),
        in_specs=[_tok_spec(GROUP_W, 0), _tok_spec(GROUP_W, Z_COL // GROUP_W), _vec_spec(GROUP_W),
                  _tok_spec(GROUP_W, 1)],
        out_specs=[_tok_spec(GROUP_W, 0), _tok_spec(GROUP_W, 0), _acc_spec(GROUP_W)],
        out_shape=[jax.ShapeDtypeStruct((s, GROUP_W), f32), jax.ShapeDtypeStruct((s, GROUP_W), bf16),
                   jax.ShapeDtypeStruct((1, GROUP_W), f32)],
        compiler_params=_cparams(("arbitrary",)),
    )(o, proj, ng_e, dycat)


_MESH = pl.DeviceIdType.MESH
_ANY = pl.BlockSpec(memory_space=pl.ANY)
_VMEM = pl.BlockSpec(memory_space=pltpu.VMEM)


def _my_place():
    x, y, c = lax.axis_index("x"), lax.axis_index("y"), lax.axis_index("c")
    return x, y, c, 4 * x + 2 * y + c


def _peer(k, x, y, c):
    px = 1 - x if k & 4 else x
    py = 1 - y if k & 2 else y
    pc = 1 - c if k & 1 else c
    return (px, py, pc), 4 * px + 2 * py + pc


def _exchange_all(src_of_peer, dst_ref, send_sems, recv_sems, x, y, c, me):
    sent = []
    for k in range(1, N_DEV):
        dev, pidx = _peer(k, x, y, c)
        cp = pltpu.make_async_remote_copy(src_ref=src_of_peer(pidx), dst_ref=dst_ref.at[me],
                                          send_sem=send_sems.at[k - 1], recv_sem=recv_sems.at[k - 1],
                                          device_id=dev, device_id_type=_MESH)
        cp.start()
        sent.append(cp)
    for k in range(1, N_DEV):
        dev, pidx = _peer(k, x, y, c)
        pltpu.make_async_remote_copy(src_ref=src_of_peer(pidx), dst_ref=dst_ref.at[pidx],
                                     send_sem=send_sems.at[k - 1], recv_sem=recv_sems.at[k - 1],
                                     device_id=dev, device_id_type=_MESH).wait_recv()
    for cp in sent:
        cp.wait_send()


def _ada_exchange(cv8, w_ada, b_ada8):
    def body(cv_ref, w_ref, b_ref, call_ref, modp_ref, part_s, s1, r1, s2, r2):
        x, y, c, me = _my_place()
        call_ref[me] = cv_ref[...]
        _exchange_all(lambda pidx: cv_ref, call_ref, s1, r1, x, y, c, me)
        bias = b_ref[me]
        for j in range(N_DEV):
            cj = call_ref[j][:, :D_MODEL]
            part_s[j] = _hdot(cj * jax.nn.sigmoid(cj), w_ref[...]) + bias
        modp_ref[me] = part_s[me]
        _exchange_all(lambda pidx: part_s.at[pidx], modp_ref, s2, r2, x, y, c, me)

    nsh = w_ada.shape[1]
    return pl.pallas_call(
        body, name="ada_exchange",
        in_specs=[_VMEM, _VMEM, _VMEM], out_specs=[_VMEM, _VMEM],
        out_shape=[jax.ShapeDtypeStruct((N_DEV, 8, cv8.shape[1]), f32), jax.ShapeDtypeStruct((N_DEV, 8, nsh), f32)],
        scratch_shapes=[pltpu.VMEM((N_DEV, 8, nsh), f32)] + [pltpu.SemaphoreType.DMA((N_DEV - 1,))] * 4,
        compiler_params=pltpu.CompilerParams(vmem_limit_bytes=VMEM_LIMIT),
    )(cv8, w_ada, b_ada8)


def _all_to_all(arrs, name):
    ex = _Exchange(arrs, gather=False)

    def body(*refs):
        srcs, dsts, sems = refs[:ex.n], refs[ex.n:2 * ex.n], refs[2 * ex.n:]
        ex.start(srcs, dsts, sems)
        ex.wait(srcs, dsts, sems)

    return pl.pallas_call(
        body, name=name,
        in_specs=[_ANY] * ex.n, out_specs=[_ANY] * ex.n,
        out_shape=ex.out_shape(), scratch_shapes=ex.scratch(),
    )(*arrs)


class _Exchange:
    def __init__(self, arrs, gather):
        self.arrs, self.gather, self.n = list(arrs), gather, len(arrs)

    def out_shape(self):
        return [jax.ShapeDtypeStruct(((N_DEV,) + a.shape) if self.gather else a.shape, a.dtype) for a in self.arrs]

    def scratch(self):
        if self.n == 0:
            return []
        return [pltpu.SemaphoreType.DMA((self.n, N_DEV - 1)), pltpu.SemaphoreType.DMA((self.n, N_DEV - 1)),
                pltpu.SemaphoreType.DMA((self.n,))]

    def _src(self, srcs, a, idx):
        return srcs[a] if self.gather else srcs[a].at[idx]

    def _copies(self, srcs, dsts, sems, incoming):
        send_sems, recv_sems, _ = sems
        x, y, c, me = _my_place()
        out = []
        for a in range(self.n):
            for k in range(1, N_DEV):
                dev, pidx = _peer(k, x, y, c)
                out.append(pltpu.make_async_remote_copy(
                    src_ref=self._src(srcs, a, pidx), dst_ref=dsts[a].at[pidx if incoming else me],
                    send_sem=send_sems.at[a, k - 1], recv_sem=recv_sems.at[a, k - 1],
                    device_id=dev, device_id_type=_MESH))
        return out

    def _local(self, srcs, dsts, sems):
        me = _my_place()[3]
        return [pltpu.make_async_copy(self._src(srcs, a, me), dsts[a].at[me], sems[2].at[a]) for a in range(self.n)]

    def start(self, srcs, dsts, sems):
        for cp in self._local(srcs, dsts, sems) + self._copies(srcs, dsts, sems, incoming=False):
            cp.start()

    def wait(self, srcs, dsts, sems):
        for cp in self._copies(srcs, dsts, sems, incoming=True):
            cp.wait_recv()
        for cp in self._copies(srcs, dsts, sems, incoming=False):
            cp.wait_send()
        for cp in self._local(srcs, dsts, sems):
            cp.wait()

    def start_at_first_step(self, grid, srcs, dsts, sems):
        first = functools.reduce(jnp.logical_and, [pl.program_id(i) == 0 for i in range(len(grid))])
        pl.when(first)(lambda: self.start(srcs, dsts, sems))

    def wait_at_last_step(self, grid, srcs, dsts, sems):
        last = functools.reduce(jnp.logical_and, [pl.program_id(i) == g - 1 for i, g in enumerate(grid)])
        pl.when(last)(lambda: self.wait(srcs, dsts, sems))


class _ChipGather:
    def __init__(self, shard):
        self.shard = shard

    def out_shape(self):
        return jax.ShapeDtypeStruct((N_DEV,) + self.shard.shape, self.shard.dtype)

    def scratch(self):
        return [pltpu.SemaphoreType.DMA((N_DEV - 1,)), pltpu.SemaphoreType.DMA((N_DEV - 1,)),
                pltpu.SemaphoreType.DMA(())]

    def _place(self):
        x, y, c, me = _my_place()
        return x, y, c, me, (x, y, 1 - c), [(1 - x, y), (x, 1 - y), (1 - x, 1 - y)]

    def _copy(self, out, sems, k, block, to, src=None):
        rows = out.at[4 * block[0] + 2 * block[1] + block[2]]
        return pltpu.make_async_remote_copy(src_ref=rows if src is None else src, dst_ref=rows,
                                            send_sem=sems[0].at[k], recv_sem=sems[1].at[k],
                                            device_id=to, device_id_type=_MESH)

    def start(self, src, out, sems):
        x, y, c, me, sib, chips = self._place()
        pltpu.make_async_copy(src, out.at[me], sems[2]).start()
        self._copy(out, sems, 0, (x, y, c), sib, src=src).start()
        for j, chip in enumerate(chips):
            self._copy(out, sems, 1 + j, (x, y, c), (*chip, c), src=src).start()

    def forward(self, src, out, sems):
        x, y, c, me, sib, chips = self._place()
        for j, chip in enumerate(chips):
            self._copy(out, sems, 1 + j, (*chip, c), (x, y, c)).wait_recv()
            self._copy(out, sems, 4 + j, (*chip, c), sib).start()

    def finish(self, src, out, sems):
        x, y, c, me, sib, chips = self._place()
        self._copy(out, sems, 0, (x, y, 1 - c), (x, y, c)).wait_recv()
        for j, chip in enumerate(chips):
            self._copy(out, sems, 4 + j, (*chip, 1 - c), (x, y, c)).wait_recv()
        self._copy(out, sems, 0, (x, y, c), sib, src=src).wait_send()
        for j, chip in enumerate(chips):
            self._copy(out, sems, 1 + j, (x, y, c), (*chip, c), src=src).wait_send()
            self._copy(out, sems, 4 + j, (*chip, c), sib).wait_send()
        pltpu.make_async_copy(src, out.at[me], sems[2]).wait()


def _ride(body, n_in, n_out, xchg, grid):
    nx = xchg.n
    if nx == 0:
        return body

    def wrapped(*refs):
        ins, xs = refs[:n_in], refs[n_in:n_in + nx]
        outs, xd = refs[n_in + nx:n_in + nx + n_out], refs[n_in + nx + n_out:n_in + 2 * nx + n_out]
        scratch = refs[n_in + 2 * nx + n_out:]
        xchg.start_at_first_step(grid, xs, xd, scratch[-3:])
        body(*ins, *outs, *scratch[:-3])
        xchg.wait_at_last_step(grid, xs, xd, scratch[-3:])

    return wrapped


def _adamw_math(w, g, m, v):
    m2 = ADAM_B1 * m + (1.0 - ADAM_B1) * g
    v2 = ADAM_B2 * v + (1.0 - ADAM_B2) * (g * g)
    m_hat = m2 / (1.0 - ADAM_B1 ** ADAM_STEP)
    v_hat = v2 / (1.0 - ADAM_B2 ** ADAM_STEP)
    delta = -ADAM_LR * (m_hat / (jnp.sqrt(v_hat) + ADAM_EPS) + ADAM_WD * w)
    return delta, m2, v2


def _row_tile(rows):
    for t in (256, 128, 64, 32, 16, 8):
        if rows % t == 0:
            return t
    return rows


def _reduce_adamw(parts, w, m, v, name):
    _, r, cdim = parts.shape
    tr = _row_tile(r)

    def body(p_ref, w_ref, m_ref, v_ref, g_ref, d_ref, m2_ref, v2_ref):
        g = p_ref[0].astype(f32)
        for j in range(1, N_DEV):
            g = g + p_ref[j].astype(f32)
        delta, m2, v2 = _adamw_math(w_ref[...], g, m_ref[...], v_ref[...])
        g_ref[...] = g
        d_ref[...] = delta
        m2_ref[...] = m2
        v2_ref[...] = v2

    spec = pl.BlockSpec((tr, cdim), lambda i: (i, 0))
    return pl.pallas_call(
        body, name=name, grid=(r // tr,),
        in_specs=[pl.BlockSpec((N_DEV, tr, cdim), lambda i: (0, i, 0)), spec, spec, spec],
        out_specs=[spec] * 4,
        out_shape=[jax.ShapeDtypeStruct((r, cdim), f32)] * 4,
        compiler_params=_cparams(("parallel",)),
    )(parts, w, m, v)


def _adamw(w, g, m, v, name):
    r, cdim = w.shape
    tr = _row_tile(r)

    def body(w_ref, g_ref, m_ref, v_ref, d_ref, m2_ref, v2_ref):
        delta, m2, v2 = _adamw_math(w_ref[...], g_ref[...], m_ref[...], v_ref[...])
        d_ref[...] = delta
        m2_ref[...] = m2
        v2_ref[...] = v2

    spec = pl.BlockSpec((tr, cdim), lambda i: (i, 0))
    return pl.pallas_call(
        body, name=name, grid=(r // tr,),
        in_specs=[spec] * 4, out_specs=[spec] * 3,
        out_shape=[jax.ShapeDtypeStruct((r, cdim), f32)] * 3,
        compiler_params=_cparams(("parallel",)),
    )(w, g, m, v)


def _sum_devices(parts, name):
    _, r, cdim = parts.shape

    def body(p_ref, o_ref):
        g = p_ref[0]
        for j in range(1, N_DEV):
            g = g + p_ref[j]
        o_ref[...] = g

    return pl.pallas_call(
        body, name=name, out_shape=jax.ShapeDtypeStruct((r, cdim), f32),
        in_specs=[_VMEM], out_specs=_VMEM,
    )(parts)


def _ada_wgrad(c_all8, dmod_cols):
    nsh = dmod_cols.shape[1]

    def body(c_ref, d_ref, o_ref):
        cv = c_ref[...]
        o_ref[...] = lax.dot_general(cv * jax.nn.sigmoid(cv), d_ref[...], _TN, precision=_HI,
                                     preferred_element_type=f32)

    return pl.pallas_call(
        body, name="ada_wgrad", out_shape=jax.ShapeDtypeStruct((D_MODEL, nsh), f32),
        in_specs=[_VMEM, _VMEM], out_specs=_VMEM,
        compiler_params=pltpu.CompilerParams(vmem_limit_bytes=VMEM_LIMIT),
    )(c_all8, dmod_cols)


def _cols(t):
    return t.transpose(1, 0, 2).reshape(t.shape[1], N_DEV * t.shape[2])


def _col_blocks(t, n):
    return t.reshape(t.shape[0], N_DEV, n).transpose(1, 0, 2).astype(bf16)


def _row_blocks(t):
    return t.reshape(N_DEV, t.shape[0] // N_DEV, t.shape[1]).astype(bf16)


def _local_step(x, tgt, mod, norm_attn_g, w_in_sh, rel_bias, conv_full, a_log, dt_bias, delta_norm_g,
                norm_ffn_g, final_norm_g, w_out_sh, w_gate_sh, w_up_sh, w_down_sh):
    s = x.shape[0]
    sh1, sc1, g1, sh2, sc2, g2 = [mod[:, i * D_MODEL:(i + 1) * D_MODEL] for i in range(6)]
    nag = norm_attn_g.reshape(1, D_MODEL)
    nfg = norm_ffn_g.reshape(1, D_MODEL)
    fg = final_norm_g.reshape(1, D_MODEL)
    idx = _bucket_tables()
    bias = _bias_tables(rel_bias, idx)
    alog_e = jnp.repeat(a_log.reshape(N_HEADS), HEAD_DIM)[None]
    dt_e = jnp.repeat(dt_bias.reshape(N_HEADS), HEAD_DIM)[None]
    ng_e = jnp.tile(delta_norm_g.reshape(HEAD_DIM), N_HEADS)[None]

    h1, w_in_g = _ln_mod_fwd(x, nag, sc1, sh1, w_in_sh, "ln1_fwd")
    w_in_p = jnp.pad(_cols(w_in_g), ((0, 0), (0, IN_PAD - IN_WIDTH)))
    proj, (w_out_g, w_down_g) = _mm(h1, w_in_p, "nn", f32, 512, IN_PAD, 1024, "in_proj",
                                    xchg=_Exchange([w_out_sh, w_down_sh], gather=True))
    (y_attn, lse), (w_gate_g, w_up_g) = _attn_fwd(proj, bias, _Exchange([w_gate_sh, w_up_sh], gather=True))
    w_out_b = w_out_g.reshape(2 * GROUP_W, D_MODEL)
    w_down_b = w_down_g.reshape(D_FF, D_MODEL)
    w_gate_b, w_up_b = _cols(w_gate_g), _cols(w_up_g)
    n_ff = w_gate_sh.shape[1]
    sconv, yconv = _conv_silu_fwd(proj, conv_full)
    qn, kn, beta, g = _delta_prep_fwd(sconv, proj, alog_e, dt_e)
    (u, w, qt, kh, qk, tinv, gm), _ = _delta_chunk_pre(qn, kn, sconv, beta, g, _Exchange([], gather=False))
    o, ss = _delta_scan_fwd(u, w, qt, kh, qk, gm)
    y_delta = _gnorm_fwd(o, proj, ng_e)
    y, x1, h2 = _proj_resid_ln_mod_fwd([(y_attn, w_out_b[:GROUP_W]), (y_delta, w_out_b[GROUP_W:])],
                                       x, g1, nfg, sc2, sh2, "out_proj_ln2")
    act, gate, up = _ffn_up(h2, w_gate_b, w_up_b, "ffn_up")
    dx2, dy2, loss, dfg, dg2 = _proj_final_loss_bwd(act, w_down_b, x1, g2, fg, tgt, "ffn_down_loss")

    dgate, dup = _ffn_down_dx(dy2, w_down_b, gate, up, "ffn_down_dx")
    g_down = _mm(act, dy2, "tn", f32, 1408, 1024, 1024, "ffn_down_dw")
    (dx1, dsh2, dsc2, dnfg, dy, dg1), (r_down,) = _proj_ln_mod_bwd(
        [(dgate, w_gate_b), (dup, w_up_b)], x1, nfg, sc2, dx2, 256, "ffn_up_dx_ln2",
        _Exchange([_row_blocks(g_down)], gather=False), gate=g1, y=y)
    g_gate = _mm(h2, dgate, "tn", f32, 1024, 1408, 1024, "ffn_gate_dw")
    g_up = _mm(h2, dup, "tn", f32, 1024, 1408, 1024, "ffn_up_dw")
    dycat = _mm(dy, w_out_b, "nt", f32, 512, 1024, 1024, "out_proj_dx")
    g_out = jnp.concatenate([_mm(y_attn, dy, "tn", f32, GROUP_W, 1024, 1024, "out_proj_dw_attn"),
                             _mm(y_delta, dy, "tn", f32, GROUP_W, 1024, 1024, "out_proj_dw_delta")], axis=0)
    dq, dk, dv, dbias = _attn_bwd(proj, bias, y_attn, lse, dycat)
    g_rb = _bias_grad(dbias, idx)[:, :, 0].T
    do, dz, dng = _gnorm_bwd(o, proj, ng_e, dycat)
    dso, _ = _delta_scan_bwd(w, qt, kh, qk, gm, do, _Exchange([], gather=False))
    dqn, dkn, dvd, dbeta, dgd = _delta_chunk_bwd(qn, kn, sconv, beta, g, tinv, ss, dso, do)
    (dsq, dsk, dba, dal, ddt), _ = _delta_prep_bwd(
        sconv, proj, alog_e, dt_e, dqn, dkn, dbeta, dgd, _Exchange([], gather=False))
    (dxc, g_conv), (r_gate, r_up, r_out) = _conv_silu_bwd(
        proj, conv_full, yconv, (dsq, dsk, dvd),
        _Exchange([_col_blocks(g_gate, n_ff), _col_blocks(g_up, n_ff), _row_blocks(g_out)], gather=False))
    pieces = ((dq, 0), (dk, GROUP_W), (dv, 2 * GROUP_W), (dxc, DELTA_COL), (dz, Z_COL), (dba, BA_BLOCK * 128))
    g_in = jnp.concatenate(
        [_mm(h1, p, "tn", f32, 1024, min(p.shape[1], 768), 1024, "in_proj_dw_%d" % c) for p, c in pieces], axis=1)
    (gx, dsh1, dsc1, dnag), (r_in,) = _proj_ln_mod_bwd(
        [(p, w_in_p[:, c:c + p.shape[1]]) for p, c in pieces], x, nag, sc1, dx1, TOK_TILE, "in_proj_dx_ln1",
        _Exchange([_col_blocks(g_in[:, :IN_WIDTH], IN_WIDTH // N_DEV)], gather=False))
    grads = dict(
        x=gx, mod=jnp.concatenate([dsh1, dsc1, dg1, dsh2, dsc2, dg2], axis=1),
        norm_attn_g=dnag, norm_ffn_g=dnfg, final_norm_g=dfg, rel_bias=g_rb, conv_w=g_conv,
        a_log=dal.reshape(N_HEADS, HEAD_DIM).sum(-1), dt_bias=ddt.reshape(N_HEADS, HEAD_DIM).sum(-1),
        delta_norm_g=dng.reshape(N_HEADS, HEAD_DIM).sum(0),
        w_in=r_in, w_out=r_out, w_gate=r_gate, w_up=r_up, w_down=r_down)
    return loss[0, 0], grads


def _misc_row(rel_bias, a_log, dt_bias, delta_norm_g):
    flat = jnp.concatenate([rel_bias.reshape(-1), a_log.reshape(-1), dt_bias.reshape(-1), delta_norm_g.reshape(-1)])
    return jnp.pad(flat, (0, D_MODEL - flat.shape[0]))[None]


def _pack_small(b_ada, nag, nfg, fng, rel_bias, a_log, dt_bias, dng, conv_shard):
    rows = [b_ada.reshape(6, D_MODEL), nag.reshape(1, D_MODEL), nfg.reshape(1, D_MODEL), fng.reshape(1, D_MODEL),
            _misc_row(rel_bias, a_log, dt_bias, dng),
            jnp.pad(conv_shard.reshape(-1), (0, D_MODEL - conv_shard.size))[None],
            jnp.zeros((5, D_MODEL), f32)]
    return jnp.concatenate(rows, axis=0)


def _unpack_small(p, conv_shape):
    misc = p[9]
    return dict(
        b_ada=p[0:6].reshape(1, 6 * D_MODEL), norm_attn_g=p[6:7], norm_ffn_g=p[7:8], final_norm_g=p[8],
        rel_bias=misc[0:256].reshape(N_BUCKETS, N_HEADS), a_log=misc[256:264].reshape(1, N_HEADS),
        dt_bias=misc[264:272].reshape(1, N_HEADS), delta_norm_g=misc[272:336].reshape(1, HEAD_DIM),
        conv_w=p[10, :conv_shape[1] * conv_shape[2]].reshape(conv_shape))


def kernel(x, c, w_ada, b_ada, norm_attn_g, w_in, rel_bias, conv_w, a_log, dt_bias, delta_norm_g, w_out, norm_ffn_g, w_gate, w_up, w_down, final_norm_g, loss_target, m_w_ada, m_b_ada, m_norm_attn_g, m_w_in, m_rel_bias, m_conv_w, m_a_log, m_dt_bias, m_delta_norm_g, m_w_out, m_norm_ffn_g, m_w_gate, m_w_up, m_w_down, m_final_norm_g, v_w_ada, v_b_ada, v_norm_attn_g, v_w_in, v_rel_bias, v_conv_w, v_a_log, v_dt_bias, v_delta_norm_g, v_w_out, v_norm_ffn_g, v_w_gate, v_w_up, v_w_down, v_final_norm_g):
    me = 4 * lax.axis_index("x") + 2 * lax.axis_index("y") + lax.axis_index("c")
    ada_sh = w_ada.shape[2]
    conv_sh = conv_w.shape[2]

    cv = jnp.concatenate([c[0], conv_w[0].reshape(-1)])
    cv8 = jnp.zeros((8, 2 * D_MODEL), f32).at[0, :cv.shape[0]].set(cv)
    b8 = jnp.broadcast_to(b_ada.reshape(N_DEV, 1, ada_sh), (N_DEV, 8, ada_sh))
    call, modp = _ada_exchange(cv8, w_ada[0], b8)
    mod = modp[:, 0, :].reshape(1, 6 * D_MODEL)
    c_all = call[:, 0, :D_MODEL]
    conv_full = call[:, 0, D_MODEL:D_MODEL + CONV_WIDTH * conv_sh].reshape(N_DEV, CONV_WIDTH, conv_sh)
    conv_full = conv_full.transpose(1, 0, 2).reshape(CONV_WIDTH, N_DEV * conv_sh)

    loss_local, gr = _local_step(x[0], loss_target[0], mod, norm_attn_g, w_in[0].astype(bf16), rel_bias, conv_full, a_log,
                                 dt_bias, delta_norm_g, norm_ffn_g, final_norm_g, w_out[0].astype(bf16),
                                 w_gate[0].astype(bf16), w_up[0].astype(bf16), w_down[0].astype(bf16))
    loss = lax.psum(loss_local, ("x", "y", "c"))

    small = jnp.concatenate([
        gr["mod"].reshape(6, D_MODEL), gr["norm_attn_g"], gr["norm_ffn_g"], gr["final_norm_g"],
        gr["conv_w"].reshape(6, D_MODEL),
        _misc_row(gr["rel_bias"], gr["a_log"], gr["dt_bias"], gr["delta_norm_g"])], axis=0)
    parts = _all_to_all([jnp.broadcast_to(small[None], (N_DEV,) + small.shape)], "small_gather")[0]
    tot = _sum_devices(parts, "small_sum")
    g_conv_full = tot[9:15].reshape(CONV_WIDTH, N_DEV * conv_sh)
    g_conv = lax.dynamic_slice(g_conv_full, (0, me * conv_sh), (CONV_WIDTH, conv_sh))
    misc = tot[15]
    g_small = _pack_small(tot[0:6], tot[6], tot[7], tot[8], misc[0:256], misc[256:264], misc[264:272],
                          misc[272:336], g_conv)
    pk = lambda pre: _pack_small(pre[0], pre[1], pre[2], pre[3], pre[4], pre[5], pre[6], pre[7], pre[8])
    w_small = pk((b_ada, norm_attn_g, norm_ffn_g, final_norm_g, rel_bias, a_log, dt_bias, delta_norm_g, conv_w))
    m_small = pk((m_b_ada, m_norm_attn_g, m_norm_ffn_g, m_final_norm_g, m_rel_bias, m_a_log, m_dt_bias,
                  m_delta_norm_g, m_conv_w))
    v_small = pk((v_b_ada, v_norm_attn_g, v_norm_ffn_g, v_final_norm_g, v_rel_bias, v_a_log, v_dt_bias,
                  v_delta_norm_g, v_conv_w))
    d_small, m2_small, v2_small = _adamw(w_small, g_small, m_small, v_small, "adamw_small")
    cshape = conv_w.shape
    G, Dl, M2, V2 = (_unpack_small(t, cshape) for t in (g_small, d_small, m2_small, v2_small))

    dmod_all = parts[:, 0:6, :].reshape(N_DEV, 6 * D_MODEL)
    dmod_cols = lax.dynamic_slice(dmod_all, (0, me * ada_sh), (N_DEV, ada_sh))
    g_ada = _ada_wgrad(c_all, dmod_cols)
    d_ada, m2_ada, v2_ada = _adamw(w_ada[0], g_ada, m_w_ada[0], v_w_ada[0], "adamw_w_ada")

    big = {}
    for name, w_, m_, v_ in (("w_in", w_in, m_w_in, v_w_in), ("w_out", w_out, m_w_out, v_w_out),
                             ("w_gate", w_gate, m_w_gate, v_w_gate), ("w_up", w_up, m_w_up, v_w_up),
                             ("w_down", w_down, m_w_down, v_w_down)):
        big[name] = [t[None] for t in _reduce_adamw(gr[name], w_[0], m_[0], v_[0], "reduce_adamw_" + name)]

    def leaf(i, name):
        if name == "w_ada":
            return (g_ada, d_ada, m2_ada, v2_ada)[i][None]
        if name in big:
            return big[name][i]
        return (G, Dl, M2, V2)[i][name]

    order = ["w_ada", "b_ada", "norm_attn_g", "w_in", "rel_bias", "conv_w", "a_log", "dt_bias", "delta_norm_g",
             "w_out", "norm_ffn_g", "w_gate", "w_up", "w_down", "final_norm_g"]
    outs = [loss, gr["x"][None]]
    for i in range(4):
        outs += [leaf(i, n) for n in order]
    return tuple(outs)
```

```python
import functools
import math

import jax
import jax.numpy as jnp
from jax import lax
from jax.experimental import pallas as pl
from jax.experimental.pallas import tpu as pltpu

f32 = jnp.float32
bf16 = jnp.bfloat16

D_MODEL = 1024
HEAD_DIM = 64
N_HEADS = 8
GROUP_W = 512
IN_WIDTH = 3600
IN_PAD = 3840
D_FF = 2816
EPS = 1e-6
NEG_INF = -1e30
BAND = 128
PAD_UNIT = 2048
DILATIONS = (1, 4, 16)
N_BUCKETS = 32
MAX_DISTANCE = 2048
CONV_WIDTH = 4
CHUNK = 64
N_DEV = 8
VMEM_LIMIT = 56 * 1024 * 1024

ADAM_LR, ADAM_B1, ADAM_B2, ADAM_EPS, ADAM_WD, ADAM_STEP = 0.001, 0.9, 0.999, 1e-08, 0.01, 10


def _cparams(sem):
    return pltpu.CompilerParams(dimension_semantics=sem, vmem_limit_bytes=VMEM_LIMIT)


def _mm(a, b, mode, out_dtype, tm, tn, tk, name, xchg=None):
    if mode == "nn":
        (m, k), (_, n) = a.shape, b.shape
        a_spec = pl.BlockSpec((tm, tk), lambda j, i, kk: (i, kk))
        b_spec = pl.BlockSpec((tk, tn), lambda j, i, kk: (kk, j))
        dims = (((1,), (0,)), ((), ()))
    elif mode == "nt":
        (m, k), (n, _) = a.shape, b.shape
        a_spec = pl.BlockSpec((tm, tk), lambda j, i, kk: (i, kk))
        b_spec = pl.BlockSpec((tn, tk), lambda j, i, kk: (j, kk))
        dims = (((1,), (1,)), ((), ()))
    else:
        (k, m), (_, n) = a.shape, b.shape
        a_spec = pl.BlockSpec((tk, tm), lambda j, i, kk: (kk, i))
        b_spec = pl.BlockSpec((tk, tn), lambda j, i, kk: (kk, j))
        dims = (((0,), (0,)), ((), ()))
    assert m % tm == 0 and n % tn == 0 and k % tk == 0, (name, m, n, k, tm, tn, tk)
    nk = k // tk
    grid = (n // tn, m // tm, nk)
    nx = xchg.n if xchg is not None else 0

    def body(*refs):
        a_ref, b_ref = refs[:2]
        o_ref = refs[2 + nx]
        scratch = refs[3 + 2 * nx:]
        if nx:
            xrefs = (refs[2:2 + nx], refs[3 + nx:3 + 2 * nx], scratch[-3:])
            xchg.start_at_first_step(grid, *xrefs)
        if nk == 1:
            o_ref[...] = lax.dot_general(a_ref[...].astype(bf16), b_ref[...].astype(bf16), dims,
                                         preferred_element_type=f32).astype(o_ref.dtype)
        else:
            acc_ref = scratch[0]
            kk = pl.program_id(2)

            @pl.when(kk == 0)
            def _():
                acc_ref[...] = jnp.zeros_like(acc_ref)

            acc_ref[...] += lax.dot_general(a_ref[...].astype(bf16), b_ref[...].astype(bf16), dims,
                                            preferred_element_type=f32)

            @pl.when(kk == nk - 1)
            def _():
                o_ref[...] = acc_ref[...].astype(o_ref.dtype)
        if nx:
            xchg.wait_at_last_step(grid, *xrefs)

    out = pl.pallas_call(
        body, name=name, grid=grid,
        in_specs=[a_spec, b_spec] + ([_ANY] * nx),
        out_specs=[pl.BlockSpec((tm, tn), lambda j, i, kk: (i, j))] + ([_ANY] * nx),
        out_shape=[jax.ShapeDtypeStruct((m, n), out_dtype)] + (xchg.out_shape() if nx else []),
        scratch_shapes=([pltpu.VMEM((tm, tn), f32)] if nk > 1 else []) + (xchg.scratch() if nx else []),
        compiler_params=_cparams(("arbitrary",) * 3 if nx else ("parallel", "parallel", "arbitrary")),
    )(a, b, *(xchg.arrs if nx else []))
    return (out[0], out[1:]) if nx else out[0]


TOK_TILE = 512
SUB_COLS = 384


def _row_spec(width, tile=TOK_TILE):
    return pl.BlockSpec((tile, width), lambda i: (i, 0))


def _vec_spec(width, rows=1):
    return pl.BlockSpec((rows, width), lambda i: (0, 0))


def _ln_mod_fwd(x, gain, sc, sh, shard, name):
    s, d = x.shape
    nt = s // TOK_TILE
    ride = _ChipGather(shard)

    def body(x_ref, g_ref, sc_ref, sh_ref, sh_in, h_ref, sh_out, *sems):
        i = pl.program_id(0)
        pl.when(i == 0)(lambda: ride.start(sh_in, sh_out, sems))
        xv = x_ref[...]
        rstd = lax.rsqrt(jnp.mean(xv * xv, axis=-1, keepdims=True) + EPS)
        h = (xv * rstd) * g_ref[...] * (1.0 + sc_ref[...]) + sh_ref[...]
        h_ref[...] = h.astype(bf16)
        @pl.when(i == nt - 1)
        def _():
            ride.forward(sh_in, sh_out, sems)
            ride.finish(sh_in, sh_out, sems)

    return pl.pallas_call(
        body, name=name, grid=(nt,),
        in_specs=[_row_spec(d), _vec_spec(d), _vec_spec(d), _vec_spec(d), _ANY],
        out_specs=[_row_spec(d), _ANY],
        out_shape=[jax.ShapeDtypeStruct((s, d), bf16), ride.out_shape()],
        scratch_shapes=ride.scratch(),
        compiler_params=_cparams(("arbitrary",)),
    )(x, gain, sc, sh, shard)


def _proj_resid_ln_mod_fwd(pairs, x, gate, gain, sc, sh, name):
    s, d = x.shape
    npair = len(pairs)

    def body(*refs):
        aw = refs[:2 * npair]
        x_ref, gt_ref, g_ref, sc_ref, sh_ref, y_ref, x1_ref, h_ref = refs[2 * npair:]
        y = jnp.dot(aw[0][...].astype(bf16), aw[1][...], preferred_element_type=f32)
        for t in range(1, npair):
            y = y + jnp.dot(aw[2 * t][...].astype(bf16), aw[2 * t + 1][...], preferred_element_type=f32)
        y_ref[...] = y
        x1 = x_ref[...] + gt_ref[...] * y
        x1_ref[...] = x1
        rstd = lax.rsqrt(jnp.mean(x1 * x1, axis=-1, keepdims=True) + EPS)
        h = (x1 * rstd) * g_ref[...] * (1.0 + sc_ref[...]) + sh_ref[...]
        h_ref[...] = h.astype(bf16)

    aw_specs, aw = [], []
    for a, w in pairs:
        aw_specs += [_row_spec(a.shape[1]), pl.BlockSpec(w.shape, lambda i: (0, 0))]
        aw += [a, w]
    return pl.pallas_call(
        body, name=name, grid=(s // TOK_TILE,),
        in_specs=aw_specs + [_row_spec(d)] + [_vec_spec(d)] * 4,
        out_specs=[_row_spec(d)] * 3,
        out_shape=[jax.ShapeDtypeStruct((s, d), f32)] * 2 + [jax.ShapeDtypeStruct((s, d), bf16)],
        compiler_params=_cparams(("parallel",)),
    )(*aw, x, gate, gain, sc, sh)


FFN_TN = 1408


def _ffn_up(h2, w_gate, w_up, name):
    s, d = h2.shape
    tm = TOK_TILE

    def body(h_ref, wg_ref, wu_ref, a_ref, g_ref, u_ref):
        h = h_ref[...]
        g = jnp.dot(h, wg_ref[...], preferred_element_type=f32)
        u = jnp.dot(h, wu_ref[...], preferred_element_type=f32)
        a_ref[...] = (g * jax.nn.sigmoid(g) * u).astype(bf16)
        g_ref[...] = g.astype(bf16)
        u_ref[...] = u.astype(bf16)

    w_spec = pl.BlockSpec((d, FFN_TN), lambda j, i: (0, j))
    o_spec = pl.BlockSpec((tm, FFN_TN), lambda j, i: (i, j))
    return pl.pallas_call(
        body, name=name, grid=(D_FF // FFN_TN, s // tm),
        in_specs=[pl.BlockSpec((tm, d), lambda j, i: (i, 0)), w_spec, w_spec],
        out_specs=[o_spec] * 3,
        out_shape=[jax.ShapeDtypeStruct((s, D_FF), bf16)] * 3,
        compiler_params=_cparams(("parallel", "parallel")),
    )(h2, w_gate, w_up)


def _ffn_down_dx(dy2, w_down, gate, up, name):
    s, d = dy2.shape
    tm = TOK_TILE

    def body(dy_ref, w_ref, g_ref, u_ref, dg_ref, du_ref):
        dy = dy_ref[...]
        for c0 in range(0, FFN_TN, SUB_COLS):
            cols = slice(c0, min(c0 + SUB_COLS, FFN_TN))
            da = lax.dot_general(dy, w_ref[cols, :], _NT, preferred_element_type=f32)
            g = g_ref[:, cols].astype(f32)
            sg = jax.nn.sigmoid(g)
            du_ref[:, cols] = (da * g * sg).astype(bf16)
            dg_ref[:, cols] = (da * u_ref[:, cols].astype(f32) * sg * (1.0 + g * (1.0 - sg))).astype(bf16)

    t_spec = pl.BlockSpec((tm, FFN_TN), lambda j, i: (i, j))
    return pl.pallas_call(
        body, name=name, grid=(D_FF // FFN_TN, s // tm),
        in_specs=[pl.BlockSpec((tm, d), lambda j, i: (i, 0)), pl.BlockSpec((FFN_TN, d), lambda j, i: (j, 0)),
                  t_spec, t_spec],
        out_specs=[t_spec, t_spec],
        out_shape=[jax.ShapeDtypeStruct((s, D_FF), bf16)] * 2,
        compiler_params=_cparams(("parallel", "parallel")),
    )(dy2, w_down, gate, up)


def _acc_spec(width):
    return pl.BlockSpec((1, width), lambda i: (0, 0))


def _proj_final_loss_bwd(a, w, x1, gate2, final_g, target, name):
    s, d = x1.shape
    k = a.shape[1]

    def body(a_ref, w_ref, x1_ref, gt_ref, fg_ref, tg_ref, dx2_ref, dy2_ref, loss_ref, dfg_ref, dgt_ref):
        @pl.when(pl.program_id(0) == 0)
        def _():
            loss_ref[...] = jnp.zeros_like(loss_ref)
            dfg_ref[...] = jnp.zeros_like(dfg_ref)
            dgt_ref[...] = jnp.zeros_like(dgt_ref)

        y2 = jnp.dot(a_ref[...], w_ref[...], preferred_element_type=f32)
        gt = gt_ref[...]
        fg = fg_ref[...]
        x2 = x1_ref[...] + gt * y2
        rstd = lax.rsqrt(jnp.mean(x2 * x2, axis=-1, keepdims=True) + EPS)
        xn = x2 * rstd
        err = xn * fg - tg_ref[...]
        row = jnp.sum(err * err, axis=-1, keepdims=True) * (0.5 / d)
        loss_ref[...] += jnp.sum(row, axis=0, keepdims=True) + jnp.zeros_like(loss_ref)
        dout = err * (1.0 / d)
        dfg_ref[...] += jnp.sum(dout * xn, axis=0, keepdims=True)
        dxn = dout * fg
        dx2 = rstd * (dxn - xn * jnp.mean(dxn * xn, axis=-1, keepdims=True))
        dx2_ref[...] = dx2
        dgt_ref[...] += jnp.sum(dx2 * y2, axis=0, keepdims=True)
        dy2_ref[...] = (gt * dx2).astype(bf16)

    return pl.pallas_call(
        body, name=name, grid=(s // TOK_TILE,),
        in_specs=[_row_spec(k), pl.BlockSpec((k, d), lambda i: (0, 0)), _row_spec(d), _vec_spec(d), _vec_spec(d),
                  _row_spec(d)],
        out_specs=[_row_spec(d), _row_spec(d), _acc_spec(128), _acc_spec(d), _acc_spec(d)],
        out_shape=[jax.ShapeDtypeStruct((s, d), f32), jax.ShapeDtypeStruct((s, d), bf16),
                   jax.ShapeDtypeStruct((1, 128), f32), jax.ShapeDtypeStruct((1, d), f32),
                   jax.ShapeDtypeStruct((1, d), f32)],
        compiler_params=_cparams(("arbitrary",)),
    )(a, w, x1, gate2, final_g, target)


def _proj_ln_mod_bwd(pairs, xin, gain, sc, dres, tm, name, xchg, gate=None, y=None):
    s, d = xin.shape
    with_gate = gate is not None
    npair = len(pairs)
    n_in = 2 * npair + (7 if with_gate else 5) - 1
    n_out = 6 if with_gate else 4

    def body(*refs):
        ab = refs[:2 * npair]
        if with_gate:
            (x_ref, g_ref, sc_ref, dr_ref, gt_ref, y_ref,
             dx_ref, dsh_ref, dsc_ref, dg_ref, dy_ref, dgt_ref) = refs[2 * npair:]
        else:
            x_ref, g_ref, sc_ref, dr_ref, dx_ref, dsh_ref, dsc_ref, dg_ref = refs[2 * npair:]

        @pl.when(pl.program_id(0) == 0)
        def _():
            dsh_ref[...] = jnp.zeros_like(dsh_ref)
            dsc_ref[...] = jnp.zeros_like(dsc_ref)
            dg_ref[...] = jnp.zeros_like(dg_ref)
            if with_gate:
                dgt_ref[...] = jnp.zeros_like(dgt_ref)

        dh = lax.dot_general(ab[0][...].astype(bf16), ab[1][...], _NT, preferred_element_type=f32)
        for t in range(1, npair):
            dh = dh + lax.dot_general(ab[2 * t][...].astype(bf16), ab[2 * t + 1][...], _NT,
                                      preferred_element_type=f32)
        xv = x_ref[...]
        g = g_ref[...]
        sc1 = 1.0 + sc_ref[...]
        rstd = lax.rsqrt(jnp.mean(xv * xv, axis=-1, keepdims=True) + EPS)
        xn = xv * rstd
        dsh_ref[...] += jnp.sum(dh, axis=0, keepdims=True)
        dsc_ref[...] += jnp.sum(dh * (xn * g), axis=0, keepdims=True)
        dg_ref[...] += jnp.sum(dh * sc1 * xn, axis=0, keepdims=True)
        dxn = dh * sc1 * g
        dx = dr_ref[...] + rstd * (dxn - xn * jnp.mean(dxn * xn, axis=-1, keepdims=True))
        dx_ref[...] = dx
        if with_gate:
            dgt_ref[...] += jnp.sum(dx * y_ref[...], axis=0, keepdims=True)
            dy_ref[...] = (gt_ref[...] * dx).astype(bf16)

    row = lambda width: pl.BlockSpec((tm, width), lambda i: (i, 0))
    in_specs, args = [], []
    for a, b in pairs:
        in_specs += [row(a.shape[1]), pl.BlockSpec(b.shape, lambda i: (0, 0))]
        args += [a, b]
    in_specs += [row(d), _vec_spec(d), _vec_spec(d), row(d)]
    args += [xin, gain, sc, dres]
    out_specs = [row(d), _acc_spec(d), _acc_spec(d), _acc_spec(d)]
    out_shape = [jax.ShapeDtypeStruct((s, d), f32)] + [jax.ShapeDtypeStruct((1, d), f32)] * 3
    if with_gate:
        in_specs += [_vec_spec(d), row(d)]
        out_specs += [row(d), _acc_spec(d)]
        out_shape += [jax.ShapeDtypeStruct((s, d), bf16), jax.ShapeDtypeStruct((1, d), f32)]
        args += [gate, y]
    grid = (s // tm,)
    out = pl.pallas_call(
        _ride(body, n_in, n_out, xchg, grid), name=name, grid=grid,
        in_specs=in_specs + [_ANY] * xchg.n, out_specs=out_specs + [_ANY] * xchg.n,
        out_shape=out_shape + xchg.out_shape(), scratch_shapes=xchg.scratch(),
        compiler_params=_cparams(("arbitrary",)),
    )(*args, *xchg.arrs)
    return out[:n_out], out[n_out:]


def _bucket_tables():
    import numpy as np
    qi = np.arange(BAND)[:, None]
    kj = np.arange(2 * BAND)[None, :]
    steps = qi + BAND - kj
    max_exact = N_BUCKETS // 2
    out = []
    for d in DILATIONS:
        dist = np.maximum(steps, 0) * d
        dist_f = np.maximum(dist, 1).astype(np.float32)
        large = max_exact + (np.log(dist_f / np.float32(max_exact)) / np.float32(math.log(MAX_DISTANCE / max_exact))
                             * np.float32(N_BUCKETS - max_exact)).astype(np.int32)
        out.append(np.where(dist < max_exact, dist, np.minimum(large, N_BUCKETS - 1)))
    return jnp.asarray(np.stack(out).astype(np.int32))


def _bias_tables(rel_bias, idx):
    def body(idx_ref, rb_ref, o_ref):
        h = pl.program_id(1)
        idxv = idx_ref[0]
        acc = jnp.zeros((BAND, 2 * BAND), f32)
        for b in range(N_BUCKETS):
            acc = jnp.where(idxv == b, rb_ref[b, h], acc)
        o_ref[0, 0] = jnp.where(_attn_masks()[1], acc, NEG_INF)

    return pl.pallas_call(
        body, name="attn_bias_tables", grid=(3, N_HEADS),
        in_specs=[pl.BlockSpec((1, BAND, 2 * BAND), lambda br, h: (br, 0, 0)),
                  pl.BlockSpec(memory_space=pltpu.SMEM)],
        out_specs=pl.BlockSpec((1, 1, BAND, 2 * BAND), lambda br, h: (br, h, 0, 0)),
        out_shape=jax.ShapeDtypeStruct((3, N_HEADS, BAND, 2 * BAND), f32),
        compiler_params=_cparams(("parallel", "parallel")),
    )(idx, rel_bias)


def _bias_grad(dbias, idx):
    def body(idx_ref, db_ref, o_ref):
        br = pl.program_id(1)

        @pl.when(br == 0)
        def _():
            o_ref[...] = jnp.zeros_like(o_ref)

        idxv = idx_ref[0]
        dbv = db_ref[0, 0]
        row = lax.broadcasted_iota(jnp.int32, (N_BUCKETS, 128), 0)
        acc = jnp.zeros((N_BUCKETS, 128), f32)
        for b in range(N_BUCKETS):
            sb = jnp.sum(jnp.sum(jnp.where(idxv == b, dbv, 0.0), axis=1, keepdims=True), axis=0, keepdims=True)
            acc = acc + jnp.where(row == b, sb, 0.0)
        o_ref[0] += acc

    return pl.pallas_call(
        body, name="attn_bias_grad", grid=(N_HEADS, 3),
        in_specs=[pl.BlockSpec((1, BAND, 2 * BAND), lambda h, br: (br, 0, 0)),
                  pl.BlockSpec((1, 1, BAND, 2 * BAND), lambda h, br: (br, h, 0, 0))],
        out_specs=pl.BlockSpec((1, N_BUCKETS, 128), lambda h, br: (h, 0, 0)),
        out_shape=jax.ShapeDtypeStruct((N_HEADS, N_BUCKETS, 128), f32),
        compiler_params=_cparams(("parallel", "arbitrary")),
    )(idx, dbias)


def _attn_masks():
    lane = lax.broadcasted_iota(jnp.int32, (BAND, 128), 1)
    m0 = lane < HEAD_DIM
    qi = lax.broadcasted_iota(jnp.int32, (BAND, 2 * BAND), 0)
    kj = lax.broadcasted_iota(jnp.int32, (BAND, 2 * BAND), 1)
    steps = qi + BAND - kj
    in_window = (steps >= 0) & (steps <= BAND)
    return m0, in_window, kj >= BAND


_NT = (((1,), (1,)), ((), ()))
_TN = (((0,), (0,)), ((), ()))
_BNN = (((2,), (1,)), ((0,), (0,)))
_BNT = (((2,), (2,)), ((0,), (0,)))
_BTN = (((1,), (1,)), ((0,), (0,)))
ATTN_GROUP = 4
ATTN_ITEMS = PAD_UNIT // BAND
Q_COL, K_COL, V_COL = 0, 4, 8


def _attn_item_rows(j, d, c, cbase):
    r = lax.rem(j, d)
    b = lax.div(j, d)
    loc = b * (d * BAND) + r
    first = jnp.logical_and(c == 0, b == 0)
    start = cbase + loc
    pstart = jnp.where(first, start, start - d * BAND)
    return loc, start, pstart, first


def _attn_fwd(proj, bias, xchg):
    s = proj.shape[0]

    def body(q_ref, k_ref, v_ref, b_ref, y_ref, lse_ref, o_s, l_s):
        c = pl.program_id(1)
        cbase = pl.multiple_of(c * PAD_UNIT, PAD_UNIT)
        m0, in_window, cur_half = _attn_masks()
        for bi, d in enumerate(DILATIONS):
            def group(jg, carry, bi=bi, d=d):
                locs, qs, ks, vs, pens = [], [], [], [], []
                for t in range(ATTN_GROUP):
                    loc, start, pstart, first = _attn_item_rows(jg * ATTN_GROUP + t, d, c, cbase)
                    locs.append(loc)
                    qs.append(q_ref[pl.ds(loc, BAND, stride=d), :])
                    ks.append(jnp.concatenate([k_ref[pl.ds(pstart, BAND, stride=d), :],
                                               k_ref[pl.ds(start, BAND, stride=d), :]], axis=0))
                    vs.append(jnp.concatenate([v_ref[pl.ds(pstart, BAND, stride=d), :],
                                               v_ref[pl.ds(start, BAND, stride=d), :]], axis=0))
                    pens.append(jnp.where(cur_half, 0.0, jnp.where(first, NEG_INF, 0.0)))
                q = jnp.stack(qs)
                kk = jnp.stack(ks + ks).astype(bf16)
                vv = jnp.stack(vs + vs).astype(bf16)
                pen = jnp.stack(pens + pens)
                qh = (jnp.concatenate([jnp.where(m0, q, 0.0), jnp.where(m0, 0.0, q)], axis=0) * 0.125).astype(bf16)
                sc = lax.dot_general(qh, kk, _BNT, preferred_element_type=f32)
                sc = (sc.reshape(2, ATTN_GROUP, BAND, 2 * BAND) + b_ref[bi][:, None]).reshape(sc.shape) + pen
                mx = jnp.max(sc, axis=-1, keepdims=True)
                e = jnp.exp(sc - mx)
                l = jnp.sum(e, axis=-1, keepdims=True)
                o = lax.dot_general(e.astype(bf16), vv, _BNN, preferred_element_type=f32) * (1.0 / l)
                ls = mx + jnp.log(l)
                for t in range(ATTN_GROUP):
                    rows = pl.ds(locs[t], BAND, stride=d)
                    o_s[bi, rows, :] = jnp.where(m0, o[t], o[ATTN_GROUP + t])
                    l_s[bi, rows, :] = jnp.where(m0, ls[t], ls[ATTN_GROUP + t])
                return carry

            lax.fori_loop(0, ATTN_ITEMS // ATTN_GROUP, group, 0)

        def merge(t, carry):
            rows = pl.ds(pl.multiple_of(t * 256, 256), 256)
            ls = [l_s[i, rows, :] for i in range(3)]
            mx = jnp.maximum(jnp.maximum(ls[0], ls[1]), ls[2])
            ws = [jnp.exp(l - mx) for l in ls]
            tot = ws[0] + ws[1] + ws[2]
            y = (ws[0] * o_s[0, rows, :] + ws[1] * o_s[1, rows, :] + ws[2] * o_s[2, rows, :]) / tot
            y_ref[rows, :] = y
            lse_ref[rows, :] = mx + jnp.log(tot)
            return carry

        lax.fori_loop(0, PAD_UNIT // 256, merge, 0)

    chunk = lambda col: pl.BlockSpec((PAD_UNIT, 128), lambda p, c: (c, col + p))
    full = lambda col: pl.BlockSpec((s, 128), lambda p, c: (0, col + p))
    grid = (N_HEADS // 2, s // PAD_UNIT)
    out = pl.pallas_call(
        _ride(body, 4, 2, xchg, grid), name="attn_fwd", grid=grid,
        in_specs=[chunk(Q_COL), full(K_COL), full(V_COL),
                  pl.BlockSpec((3, 2, BAND, 2 * BAND), lambda p, c: (0, p, 0, 0))] + [_ANY] * xchg.n,
        out_specs=[chunk(0), chunk(0)] + [_ANY] * xchg.n,
        out_shape=[jax.ShapeDtypeStruct((s, GROUP_W), f32)] * 2 + xchg.out_shape(),
        scratch_shapes=[pltpu.VMEM((3, PAD_UNIT, 128), f32)] * 2 + xchg.scratch(),
        compiler_params=_cparams(("arbitrary", "arbitrary")),
    )(proj, proj, proj, bias, *xchg.arrs)
    return out[:2], out[2:]


def _attn_bwd(proj, bias, y, lse, dycat):
    s = proj.shape[0]

    def body(q_ref, k_ref, v_ref, b_ref, y_ref, lse_ref, dy_ref, dq_ref, dk_ref, dv_ref, db_ref, dd_s):
        c = pl.program_id(1)
        cbase = pl.multiple_of(c * PAD_UNIT, PAD_UNIT)
        m0, in_window, cur_half = _attn_masks()

        @pl.when(c == 0)
        def _():
            dk_ref[...] = jnp.zeros_like(dk_ref)
            dv_ref[...] = jnp.zeros_like(dv_ref)
            db_ref[...] = jnp.zeros_like(db_ref)

        dq_ref[...] = jnp.zeros_like(dq_ref)

        def rowdot(t, carry):
            rows = pl.ds(pl.multiple_of(t * 256, 256), 256)
            prod = dy_ref[rows, :] * y_ref[rows, :]
            lane = lax.broadcasted_iota(jnp.int32, prod.shape, 1)
            h0 = lane < HEAD_DIM
            d0 = jnp.sum(jnp.where(h0, prod, 0.0), axis=-1, keepdims=True)
            d1 = jnp.sum(jnp.where(h0, 0.0, prod), axis=-1, keepdims=True)
            dd_s[rows, :] = jnp.where(h0, d0, d1)
            return carry

        lax.fori_loop(0, PAD_UNIT // 256, rowdot, 0)

        for bi, d in enumerate(DILATIONS):
            def group(jg, carry, bi=bi, d=d):
                ng = ATTN_GROUP
                meta, qs, dos, lqs, dds, ks, vs, pens = [], [], [], [], [], [], [], []
                for t in range(ng):
                    loc, start, pstart, first = _attn_item_rows(jg * ng + t, d, c, cbase)
                    qrows = pl.ds(loc, BAND, stride=d)
                    rows = pl.ds(start, BAND, stride=d)
                    prows = pl.ds(pstart, BAND, stride=d)
                    meta.append((qrows, rows, prows))
                    qs.append(q_ref[qrows, :])
                    dos.append(dy_ref[qrows, :])
                    lqs.append(lse_ref[qrows, :])
                    dds.append(dd_s[qrows, :])
                    ks.append(jnp.concatenate([k_ref[prows, :], k_ref[rows, :]], axis=0))
                    vs.append(jnp.concatenate([v_ref[prows, :], v_ref[rows, :]], axis=0))
                    pens.append(jnp.where(cur_half, 0.0, jnp.where(first, NEG_INF, 0.0)))

                def heads(t):
                    return jnp.concatenate([jnp.where(m0, t, 0.0), jnp.where(m0, 0.0, t)], axis=0)

                def head_col(t):
                    return jnp.concatenate([t[:, :, 0:1], t[:, :, HEAD_DIM:HEAD_DIM + 1]], axis=0)

                qh = (heads(jnp.stack(qs)) * 0.125).astype(bf16)
                doh = heads(jnp.stack(dos)).astype(bf16)
                kk = jnp.stack(ks + ks).astype(bf16)
                vv = jnp.stack(vs + vs).astype(bf16)
                sc = lax.dot_general(qh, kk, _BNT, preferred_element_type=f32)
                sc = (sc.reshape(2, ng, BAND, 2 * BAND) + b_ref[bi][:, None]).reshape(sc.shape) + jnp.stack(pens + pens)
                p = jnp.exp(sc - head_col(jnp.stack(lqs)))
                dp = lax.dot_general(doh, vv, _BNT, preferred_element_type=f32)
                ds = p * (dp - head_col(jnp.stack(dds)))
                db_ref[bi] += jnp.sum(ds.reshape(2, ng, BAND, 2 * BAND), axis=1)
                dsb = ds.astype(bf16)
                dq = lax.dot_general(dsb, kk, _BNN, preferred_element_type=f32) * 0.125
                dk = lax.dot_general(dsb, qh, _BTN, preferred_element_type=f32)
                dv = lax.dot_general(p.astype(bf16), doh, _BTN, preferred_element_type=f32)
                for t in range(ng):
                    qrows, rows, prows = meta[t]
                    dq_ref[qrows, :] += jnp.where(m0, dq[t], dq[ng + t])
                    dkt = dk[t] + dk[ng + t]
                    dvt = dv[t] + dv[ng + t]
                    dk_ref[prows, :] += dkt[:BAND]
                    dk_ref[rows, :] += dkt[BAND:]
                    dv_ref[prows, :] += dvt[:BAND]
                    dv_ref[rows, :] += dvt[BAND:]
                return carry

            lax.fori_loop(0, ATTN_ITEMS // ATTN_GROUP, group, 0)

    chunk = lambda col: pl.BlockSpec((PAD_UNIT, 128), lambda p, c: (c, col + p))
    full = lambda col: pl.BlockSpec((s, 128), lambda p, c: (0, col + p))
    bias_spec = pl.BlockSpec((3, 2, BAND, 2 * BAND), lambda p, c: (0, p, 0, 0))
    return pl.pallas_call(
        body, name="attn_bwd", grid=(N_HEADS // 2, s // PAD_UNIT),
        in_specs=[chunk(Q_COL), full(K_COL), full(V_COL), bias_spec, chunk(0), chunk(0), chunk(0)],
        out_specs=[chunk(0), full(0), full(0), bias_spec],
        out_shape=[jax.ShapeDtypeStruct((s, GROUP_W), f32)] * 3
        + [jax.ShapeDtypeStruct((3, N_HEADS, BAND, 2 * BAND), f32)],
        scratch_shapes=[pltpu.VMEM((PAD_UNIT, 128), f32)],
        compiler_params=_cparams(("parallel", "arbitrary")),
    )(proj, proj, proj, bias, y, lse, dycat)


_HI = lax.Precision.HIGHEST
DELTA_COL = 1536
Z_COL = 3072
BA_BLOCK = 28
DELTA_ROWS = 1024


def _hdot(a, b):
    return jnp.dot(a, b, precision=_HI, preferred_element_type=f32)


_DIMS = dict(nn=(((2,), (1,)), ((0,), (0,))), nt=(((2,), (2,)), ((0,), (0,))), tn=(((1,), (1,)), ((0,), (0,))))


@functools.partial(jax.custom_vjp, nondiff_argnums=(2,))
def _mmx(a, b, mode):
    return lax.dot_general(a.astype(bf16), b.astype(bf16), _DIMS[mode], preferred_element_type=f32)


def _mmx_fwd(a, b, mode):
    return _mmx(a, b, mode), (a, b)


def _mmx_bwd(mode, res, g):
    a, b = res
    if mode == "nn":
        return _mmx(g, b, "nt"), _mmx(a, g, "tn")
    if mode == "nt":
        return _mmx(g, b, "nn"), _mmx(g, a, "tn")
    return _mmx(b, g, "nt"), _mmx(a, g, "nn")


_mmx.defvjp(_mmx_fwd, _mmx_bwd)


def _pair_iota():
    row = lax.broadcasted_iota(jnp.int32, (CHUNK, 128), 0)
    lane = lax.broadcasted_iota(jnp.int32, (CHUNK, 128), 1)
    return row, lane, lane & (CHUNK - 1)


def _bd(x):
    _, lane, _ = _pair_iota()
    m0 = lane < CHUNK
    return jnp.concatenate([jnp.where(m0, x, 0.0), jnp.where(m0, 0.0, x)], axis=1)


def _pmm(a, b):
    return _mmx(a, _bd(b), "nn")


def _ntp(x, y):
    return _mmx(x, _bd(y), "nt")


def _tnp(x, y):
    full = _mmx(x, y, "tn")
    _, lane, _ = _pair_iota()
    return jnp.where(lane < CHUNK, full[:, :CHUNK], full[:, CHUNK:])


def _tri_inv(a):
    row, lane, jj = _pair_iota()
    eye = jnp.where(row == jj, 1.0, 0.0).astype(f32)

    def same_block(log2b):
        return (row >> log2b) == (jj >> log2b)

    dgl = jnp.where(same_block(3), a, 0.0)
    d2 = _pmm(dgl, dgl)
    d4 = _pmm(d2, d2)
    t = _pmm(_pmm(eye - dgl, eye + d2), eye + d4)
    for lb in (3, 4, 5):
        off = jnp.where(same_block(lb + 1) & jnp.logical_not(same_block(lb)), a, 0.0)
        t = t - _pmm(_pmm(t, off), t)
    return t


@jax.custom_vjp
def _solve2(a, xv, xk, t):
    return _pmm(t, xv), _pmm(t, xk)


def _solve2_fwd(a, xv, xk, t):
    u, w = _pmm(t, xv), _pmm(t, xk)
    return (u, w), (t, u, w)


def _solve2_bwd(res, cts):
    t, u, w = res
    du, dw = cts
    dxv = _tnp(t, du)
    dxk = _tnp(t, dw)
    return -(_ntp(dxv, u) + _ntp(dxk, w)), dxv, dxk, jnp.zeros_like(t)


_solve2.defvjp(_solve2_fwd, _solve2_bwd)


def _chunk_pre(qp, kp, vp, bp, gcum, t=None):
    row, lane, jj = _pair_iota()
    causal = row >= jj
    strict = row > jj
    rsel = jnp.sum(jnp.where(row == jj, gcum, 0.0), axis=1, keepdims=True)
    decay = jnp.where(causal, jnp.exp(jnp.where(causal, gcum - rsel, 0.0)), 0.0)
    kb = kp * bp
    kd = _bd(kp)
    a = jnp.where(strict, _mmx(kb, kd, "nt") * decay, 0.0)
    eg = jnp.exp(gcum)
    if t is None:
        t = _tri_inv(a)
    u, w = _solve2(a, vp * bp, kb * eg, t)
    qk = jnp.where(causal, _mmx(qp, kd, "nt") * decay, 0.0)
    glast = jnp.sum(jnp.where(row == CHUNK - 1, gcum, 0.0), axis=1, keepdims=True)
    return u, w, qp * eg, kp * jnp.exp(glast - gcum), qk, jnp.exp(glast), t


def _chunk_post(u, w, qt, kh, qk, gam, sp):
    sd = _bd(sp)
    vnew = u - _mmx(w, sd, "nn")
    o = _mmx(qt, sd, "nn") + _pmm(qk, vnew)
    return o, gam * sp + _tnp(kh, vnew)


def _pair_spec(rows=DELTA_ROWS):
    return pl.BlockSpec((rows, 128), lambda i, p: (i, p))


DELTA_NB = DELTA_ROWS // CHUNK


def _chunks(ref):
    return ref[...].reshape(DELTA_NB, CHUNK, 128)


def _pairs(ref, rows):
    return jnp.stack([ref[rows, p * 128:(p + 1) * 128] for p in range(4)], axis=0)


def _delta_chunk_pre(qn, kn, sv, beta, g, xchg):
    s = qn.shape[0]

    def body(q_ref, k_ref, v_ref, b_ref, g_ref, u_ref, w_ref, qt_ref, kh_ref, qk_ref, t_ref, gm_ref):
        outs = _chunk_pre(_chunks(q_ref), _chunks(k_ref), _chunks(v_ref), _chunks(b_ref), _chunks(g_ref))
        for ref, val in zip((u_ref, w_ref, qt_ref, kh_ref, qk_ref, t_ref), outs[:5] + outs[6:]):
            ref[...] = val.reshape(DELTA_ROWS, 128).astype(ref.dtype)
        gm_ref[...] = jnp.broadcast_to(outs[5], (DELTA_NB, 8, 128)).reshape(DELTA_NB * 8, 128)

    v_spec = pl.BlockSpec((DELTA_ROWS, 128), lambda i, p: (i, 8 + p))
    grid = (s // DELTA_ROWS, 4)
    out = pl.pallas_call(
        _ride(body, 5, 7, xchg, grid), name="delta_chunk_pre", grid=grid,
        in_specs=[_pair_spec(), _pair_spec(), v_spec, _pair_spec(), _pair_spec()] + [_ANY] * xchg.n,
        out_specs=[_pair_spec()] * 6 + [_pair_spec(DELTA_NB * 8)] + [_ANY] * xchg.n,
        out_shape=[jax.ShapeDtypeStruct((s, GROUP_W), f32)] + [jax.ShapeDtypeStruct((s, GROUP_W), bf16)] * 5
        + [jax.ShapeDtypeStruct((s // 8, GROUP_W), f32)] + xchg.out_shape(),
        scratch_shapes=xchg.scratch(),
        compiler_params=_cparams(("arbitrary", "arbitrary")),
    )(qn, kn, sv, beta, g, *xchg.arrs)
    return out[:7], out[7:]


def _delta_scan_fwd(u, w, qt, kh, qk, gm):
    s = u.shape[0]

    def body(u_ref, w_ref, qt_ref, kh_ref, qk_ref, gm_ref, o_ref, ss_ref, st):
        @pl.when(pl.program_id(0) == 0)
        def _():
            st[...] = jnp.zeros_like(st)

        def chunk(ci, carry):
            rows = pl.ds(pl.multiple_of(ci * CHUNK, CHUNK), CHUNK)
            grow = pl.ds(pl.multiple_of(ci * 8, 8), 1)
            sp = st[...]
            o, s2 = _chunk_post(_pairs(u_ref, rows), _pairs(w_ref, rows), _pairs(qt_ref, rows),
                                _pairs(kh_ref, rows), _pairs(qk_ref, rows), _pairs(gm_ref, grow), sp)
            for p in range(4):
                ss_ref[rows, p * 128:(p + 1) * 128] = sp[p]
                o_ref[rows, p * 128:(p + 1) * 128] = o[p]
            st[...] = s2
            return carry

        lax.fori_loop(0, DELTA_NB, chunk, 0)

    spec = pl.BlockSpec((DELTA_ROWS, GROUP_W), lambda i: (i, 0))
    gspec = pl.BlockSpec((DELTA_NB * 8, GROUP_W), lambda i: (i, 0))
    return pl.pallas_call(
        body, name="delta_scan_fwd", grid=(s // DELTA_ROWS,),
        in_specs=[spec] * 5 + [gspec],
        out_specs=[spec, spec],
        out_shape=[jax.ShapeDtypeStruct((s, GROUP_W), f32)] * 2,
        scratch_shapes=[pltpu.VMEM((4, CHUNK, 128), f32)],
        compiler_params=_cparams(("arbitrary",)),
    )(u, w, qt, kh, qk, gm)


def _delta_scan_bwd(w, qt, kh, qk, gm, do, xchg):
    s = w.shape[0]
    nb = s // DELTA_ROWS

    def body(w_ref, qt_ref, kh_ref, qk_ref, gm_ref, do_ref, dso_ref, dst):
        @pl.when(pl.program_id(0) == 0)
        def _():
            dst[...] = jnp.zeros_like(dst)

        def chunk(t, carry):
            ci = DELTA_NB - 1 - t
            rows = pl.ds(pl.multiple_of(ci * CHUNK, CHUNK), CHUNK)
            grow = pl.ds(pl.multiple_of(ci * 8, 8), 1)
            ds = dst[...]
            for p in range(4):
                dso_ref[rows, p * 128:(p + 1) * 128] = ds[p]
            do = _pairs(do_ref, rows)
            dvn = _tnp(_pairs(qk_ref, rows), do) + _pmm(_pairs(kh_ref, rows), ds)
            dst[...] = _tnp(_pairs(qt_ref, rows), do) + _pairs(gm_ref, grow) * ds - _tnp(_pairs(w_ref, rows), dvn)
            return carry

        lax.fori_loop(0, DELTA_NB, chunk, 0)

    spec = pl.BlockSpec((DELTA_ROWS, GROUP_W), lambda i: (nb - 1 - i, 0))
    gspec = pl.BlockSpec((DELTA_NB * 8, GROUP_W), lambda i: (nb - 1 - i, 0))
    out = pl.pallas_call(
        _ride(body, 6, 1, xchg, (nb,)), name="delta_scan_bwd", grid=(nb,),
        in_specs=[spec] * 4 + [gspec, spec] + [_ANY] * xchg.n,
        out_specs=[spec] + [_ANY] * xchg.n,
        out_shape=[jax.ShapeDtypeStruct((s, GROUP_W), f32)] + xchg.out_shape(),
        scratch_shapes=[pltpu.VMEM((4, CHUNK, 128), f32)] + xchg.scratch(),
        compiler_params=_cparams(("arbitrary",)),
    )(w, qt, kh, qk, gm, do, *xchg.arrs)
    return out[0], out[1:]


def _delta_chunk_bwd(qn, kn, sv, beta, g, tinv, ss, dso, do, xchg):
    s = qn.shape[0]

    def body(q_ref, k_ref, v_ref, b_ref, g_ref, t_ref, ss_ref, dso_ref, do_ref,
             dq_ref, dk_ref, dv_ref, db_ref, dg_ref):
        sp = _chunks(ss_ref)
        t = _chunks(t_ref)

        def fn(q, k, v, b, gg):
            return _chunk_post(*_chunk_pre(q, k, v, b, gg, t)[:6], sp)

        _, vjp = jax.vjp(fn, _chunks(q_ref), _chunks(k_ref), _chunks(v_ref), _chunks(b_ref), _chunks(g_ref))
        grads = vjp((_chunks(do_ref), _chunks(dso_ref)))
        for ref, val in zip((dq_ref, dk_ref, dv_ref, db_ref, dg_ref), grads):
            ref[...] = val.reshape(DELTA_ROWS, 128)

    v_spec = pl.BlockSpec((DELTA_ROWS, 128), lambda i, p: (i, 8 + p))
    grid = (s // DELTA_ROWS, 4)
    out = pl.pallas_call(
        _ride(body, 9, 5, xchg, grid), name="delta_chunk_bwd", grid=grid,
        in_specs=[_pair_spec(), _pair_spec(), v_spec] + [_pair_spec()] * 6 + [_ANY] * xchg.n,
        out_specs=[_pair_spec()] * 5 + [_ANY] * xchg.n,
        out_shape=[jax.ShapeDtypeStruct((s, GROUP_W), f32)] * 5 + xchg.out_shape(),
        scratch_shapes=xchg.scratch(),
        compiler_params=_cparams(("arbitrary", "arbitrary")),
    )(qn, kn, sv, beta, g, tinv, ss, dso, do, *xchg.arrs)
    return out[:5], out[5:]


def _head_sums(x):
    r = lax.broadcasted_iota(jnp.int32, (128, 128), 0)
    c = lax.broadcasted_iota(jnp.int32, (128, 128), 1)
    pair = jnp.where((r >> 6) == (c >> 6), 1.0, 0.0).astype(f32)
    npair = x.shape[1] // 128
    xb = jnp.concatenate([x[None, :, p * 128:(p + 1) * 128] for p in range(npair)], axis=0)
    sums = _mmx(xb, jnp.broadcast_to(pair, (npair, 128, 128)), "nn")
    return jnp.concatenate([sums[p] for p in range(npair)], axis=1)


def _sel_dot(a, b):
    return jnp.dot(a, b, precision=lax.Precision.HIGH, preferred_element_type=f32)


def _expand_matrix(first):
    r = lax.broadcasted_iota(jnp.int32, (128, GROUP_W), 0)
    c = lax.broadcasted_iota(jnp.int32, (128, GROUP_W), 1) >> 6
    return jnp.where(r == c + first, 1.0, 0.0).astype(f32)


@functools.partial(jax.custom_vjp, nondiff_argnums=(1,))
def _expand_heads(ba, first):
    return _sel_dot(ba, _expand_matrix(first))


def _expand_heads_fwd(ba, first):
    return _expand_heads(ba, first), None


def _expand_heads_bwd(first, _, g):
    return (_mmx(g[None], _expand_matrix(first)[None], "nt")[0],)


_expand_heads.defvjp(_expand_heads_fwd, _expand_heads_bwd)


def _softplus(x):
    return jnp.maximum(x, 0.0) + jnp.log(1.0 + jnp.exp(-jnp.abs(x)))


def _prep_fn(sq, sk, ba, alog_e, dt_e):
    qn = sq * lax.rsqrt(_head_sums(sq * sq) + EPS) * (HEAD_DIM ** -0.5)
    kn = sk * lax.rsqrt(_head_sums(sk * sk) + EPS)
    bl = _expand_heads(ba, 0)
    al = _expand_heads(ba, N_HEADS)
    beta = jax.nn.sigmoid(bl)
    g = -jnp.exp(alog_e) * _softplus(al + dt_e)
    nchunk = g.shape[0] // CHUNK
    ri = lax.broadcasted_iota(jnp.int32, (nchunk, CHUNK, CHUNK), 1)
    ci = lax.broadcasted_iota(jnp.int32, (nchunk, CHUNK, CHUNK), 2)
    tril = jnp.where(ri >= ci, 1.0, 0.0).astype(f32)
    gcum = lax.dot_general(tril, g.reshape(nchunk, CHUNK, g.shape[1]), _BNN, precision=lax.Precision.HIGH,
                           preferred_element_type=f32)
    return qn, kn, beta, gcum.reshape(g.shape)


def _gnorm_fn(o, z, ng_e):
    ms = _head_sums(o * o) * (1.0 / HEAD_DIM)
    return o * lax.rsqrt(ms + EPS) * ng_e * (z * jax.nn.sigmoid(z))


def _tok_spec(width, col):
    return pl.BlockSpec((TOK_TILE, width), lambda i: (i, col))


def _conv_taps(xs_ref, w_ref, base, n, cols):
    acc = w_ref[CONV_WIDTH - 1:CONV_WIDTH, cols] * xs_ref[pl.ds(base, n), cols]
    for j in range(CONV_WIDTH - 1):
        acc = acc + w_ref[j:j + 1, cols] * xs_ref[pl.ds(base - (CONV_WIDTH - 1) + j, n), cols]
    return acc


def _conv_silu_fwd(proj, conv_w):
    s = proj.shape[0]
    wd = 3 * GROUP_W
    hb = TOK_TILE // 8

    def body(x_ref, halo_ref, w_ref, o_ref, y_ref, xs):
        inner = pl.program_id(0) > 0

        def lane_block(cb, carry):
            cols = pl.ds(pl.multiple_of(cb * 128, 128), 128)
            xs[0:8, cols] = jnp.where(inner, halo_ref[:, cols], 0.0)
            xs[8:, cols] = x_ref[:, cols]
            y = _conv_taps(xs, w_ref, 8, TOK_TILE, cols)
            y_ref[:, cols] = y
            o_ref[:, cols] = y * jax.nn.sigmoid(y)
            return carry

        lax.fori_loop(0, wd // 128, lane_block, 0)

    return pl.pallas_call(
        body, name="delta_conv_fwd", grid=(s // TOK_TILE,),
        in_specs=[_tok_spec(wd, 1), pl.BlockSpec((8, wd), lambda i: (jnp.maximum(i * hb - 1, 0), 1)),
                  pl.BlockSpec((CONV_WIDTH, wd), lambda i: (0, 0))],
        out_specs=[_tok_spec(wd, 0)] * 2,
        out_shape=[jax.ShapeDtypeStruct((s, wd), f32)] * 2,
        scratch_shapes=[pltpu.VMEM((TOK_TILE + 8, wd), f32)],
        compiler_params=_cparams(("parallel",)),
    )(proj, proj, conv_w)


def _conv_silu_bwd(proj, conv_w, yc, ds3, xchg):
    s = proj.shape[0]
    wd = 3 * GROUP_W
    hb = TOK_TILE // 8
    nt = s // TOK_TILE

    def body(x_ref, hp_ref, y_ref, yn_ref, dq_ref, dk_ref, dv_ref, dqn_ref, dkn_ref, dvn_ref, w_ref,
             dx_ref, dw_ref, xs, dys):
        i = pl.program_id(0)

        @pl.when(i == 0)
        def _():
            dw_ref[...] = jnp.zeros_like(dw_ref)

        last = i == nt - 1
        def lane_block(lb, carry, third, cur, nxt):
            tcols = pl.ds(pl.multiple_of(lb * 128, 128), 128)
            cols = pl.ds(pl.multiple_of(third * GROUP_W + lb * 128, 128), 128)
            xs[0:8, cols] = jnp.where(i > 0, hp_ref[:, cols], 0.0)
            xs[8:, cols] = x_ref[:, cols]
            y = y_ref[:, cols]
            sg = jax.nn.sigmoid(y)
            dy0 = cur[:, tcols] * (sg * (1.0 + y * (1.0 - sg)))
            dys[0:TOK_TILE, cols] = dy0
            yn = yn_ref[:, cols]
            sgn = jax.nn.sigmoid(yn)
            dys[TOK_TILE:, cols] = jnp.where(last, 0.0, nxt[:, tcols]) * (sgn * (1.0 + yn * (1.0 - sgn)))
            dx = w_ref[CONV_WIDTH - 1:CONV_WIDTH, cols] * dy0
            for j in range(CONV_WIDTH - 1):
                dx = dx + w_ref[j:j + 1, cols] * dys[pl.ds(CONV_WIDTH - 1 - j, TOK_TILE), cols]
            dx_ref[:, cols] = dx.astype(dx_ref.dtype)
            for j in range(CONV_WIDTH):
                dw_ref[j:j + 1, cols] += jnp.sum(dy0 * xs[pl.ds(8 - (CONV_WIDTH - 1) + j, TOK_TILE), cols],
                                                 axis=0, keepdims=True)
            return carry

        for third, (cur, nxt) in enumerate(((dq_ref, dqn_ref), (dk_ref, dkn_ref), (dv_ref, dvn_ref))):
            lax.fori_loop(0, GROUP_W // 128, functools.partial(lane_block, third=third, cur=cur, nxt=nxt), 0)

    prev8 = lambda col: pl.BlockSpec((8, wd), lambda i: (jnp.maximum(i * hb - 1, 0), col))
    next8 = lambda col: pl.BlockSpec((8, wd), lambda i: (jnp.minimum((i + 1) * hb, s // 8 - 1), col))
    next8_third = pl.BlockSpec((8, GROUP_W), lambda i: (jnp.minimum((i + 1) * hb, s // 8 - 1), 0))
    out = pl.pallas_call(
        _ride(body, 11, 2, xchg, (nt,)), name="delta_conv_bwd", grid=(nt,),
        in_specs=[_tok_spec(wd, 1), prev8(1), _tok_spec(wd, 0), next8(0)] + [_tok_spec(GROUP_W, 0)] * 3
        + [next8_third] * 3
        + [pl.BlockSpec((CONV_WIDTH, wd), lambda i: (0, 0))] + [_ANY] * xchg.n,
        out_specs=[_tok_spec(wd, 0), pl.BlockSpec((CONV_WIDTH, wd), lambda i: (0, 0))] + [_ANY] * xchg.n,
        out_shape=[jax.ShapeDtypeStruct((s, wd), bf16), jax.ShapeDtypeStruct((CONV_WIDTH, wd), f32)] + xchg.out_shape(),
        scratch_shapes=[pltpu.VMEM((TOK_TILE + 8, wd), f32), pltpu.VMEM((TOK_TILE + 8, wd), f32)] + xchg.scratch(),
        compiler_params=_cparams(("arbitrary",)),
    )(proj, proj, yc, yc, *ds3, *ds3, conv_w, *xchg.arrs)
    return out[:2], out[2:]


def _delta_prep_fwd(sconv, proj, alog_e, dt_e):
    s = sconv.shape[0]

    def body(sq_ref, sk_ref, ba_ref, al_ref, dt_ref, q_ref, k_ref, b_ref, g_ref):
        qn, kn, beta, g = _prep_fn(sq_ref[...], sk_ref[...], ba_ref[...], al_ref[...], dt_ref[...])
        q_ref[...] = qn
        k_ref[...] = kn
        b_ref[...] = beta
        g_ref[...] = g

    return pl.pallas_call(
        body, name="delta_prep_fwd", grid=(s // TOK_TILE,),
        in_specs=[_tok_spec(GROUP_W, 0), _tok_spec(GROUP_W, 1), _tok_spec(128, BA_BLOCK),
                  _vec_spec(GROUP_W), _vec_spec(GROUP_W)],
        out_specs=[_tok_spec(GROUP_W, 0)] * 4,
        out_shape=[jax.ShapeDtypeStruct((s, GROUP_W), f32)] * 4,
        compiler_params=_cparams(("parallel",)),
    )(sconv, sconv, proj, alog_e, dt_e)


def _delta_prep_bwd(sconv, proj, alog_e, dt_e, dqn, dkn, dbeta, dg, xchg):
    s = sconv.shape[0]
    grid = (s // TOK_TILE,)

    def body(sq_ref, sk_ref, ba_ref, al_ref, dt_ref, dq_ref, dk_ref, db_ref, dg_ref,
             dsq_ref, dsk_ref, dba_ref, dal_ref, ddt_ref):
        @pl.when(pl.program_id(0) == 0)
        def _():
            dal_ref[...] = jnp.zeros_like(dal_ref)
            ddt_ref[...] = jnp.zeros_like(ddt_ref)

        _, vjp = jax.vjp(_prep_fn, sq_ref[...], sk_ref[...], ba_ref[...], al_ref[...], dt_ref[...])
        dsq, dsk, dba, dal, ddt = vjp((dq_ref[...], dk_ref[...], db_ref[...], dg_ref[...]))
        dsq_ref[...] = dsq
        dsk_ref[...] = dsk
        dba_ref[...] = dba.astype(bf16)
        dal_ref[...] += dal
        ddt_ref[...] += ddt

    out = pl.pallas_call(
        _ride(body, 9, 5, xchg, grid), name="delta_prep_bwd", grid=grid,
        in_specs=[_tok_spec(GROUP_W, 0), _tok_spec(GROUP_W, 1), _tok_spec(128, BA_BLOCK),
                  _vec_spec(GROUP_W), _vec_spec(GROUP_W)] + [_tok_spec(GROUP_W, 0)] * 4 + [_ANY] * xchg.n,
        out_specs=[_tok_spec(GROUP_W, 0), _tok_spec(GROUP_W, 0), _tok_spec(128, 0),
                   _acc_spec(GROUP_W), _acc_spec(GROUP_W)] + [_ANY] * xchg.n,
        out_shape=[jax.ShapeDtypeStruct((s, GROUP_W), f32)] * 2 + [jax.ShapeDtypeStruct((s, 128), bf16)]
        + [jax.ShapeDtypeStruct((1, GROUP_W), f32)] * 2 + xchg.out_shape(),
        scratch_shapes=xchg.scratch(),
        compiler_params=_cparams(("arbitrary",)),
    )(sconv, sconv, proj, alog_e, dt_e, dqn, dkn, dbeta, dg, *xchg.arrs)
    return out[:5], out[5:]


def _gnorm_fwd(o, proj, ng_e):
    s = o.shape[0]

    def body(o_ref, z_ref, g_ref, y_ref):
        y_ref[...] = _gnorm_fn(o_ref[...], z_ref[...], g_ref[...])

    return pl.pallas_call(
        body, name="delta_gnorm_fwd", grid=(s // TOK_TILE,),
        in_specs=[_tok_spec(GROUP_W, 0), _tok_spec(GROUP_W, Z_COL // GROUP_W), _vec_spec(GROUP_W)],
        out_specs=_tok_spec(GROUP_W, 0),
        out_shape=jax.ShapeDtypeStruct((s, GROUP_W), f32),
        compiler_params=_cparams(("parallel",)),
    )(o, proj, ng_e)


def _gnorm_bwd(o, proj, ng_e, dycat):
    s = o.shape[0]

    def body(o_ref, z_ref, g_ref, dy_ref, do_ref, dz_ref, dg_ref):
        @pl.when(pl.program_id(0) == 0)
        def _():
            dg_ref[...] = jnp.zeros_like(dg_ref)

        _, vjp = jax.vjp(_gnorm_fn, o_ref[...], z_ref[...], g_ref[...])
        do, dz, dg = vjp(dy_ref[...])
        do_ref[...] = do
        dz_ref[...] = dz.astype(bf16)
        dg_ref[...] += dg

    return pl.pallas_call(
        body, name="delta_gnorm_bwd", grid=(s // TOK_TILE,),
        in_specs=[_tok_spec(GROUP_W, 0), _tok_spec(GROUP_W, Z_COL // GROUP_W), _vec_spec(GROUP_W),
                  _tok_spec(GROUP_W, 1)],
        out_specs=[_tok_spec(GROUP_W, 0), _tok_spec(GROUP_W, 0), _acc_spec(GROUP_W)],
        out_shape=[jax.ShapeDtypeStruct((s, GROUP_W), f32), jax.ShapeDtypeStruct((s, GROUP_W), bf16),
                   jax.ShapeDtypeStruct((1, GROUP_W), f32)],
        compiler_params=_cparams(("arbitrary",)),
    )(o, proj, ng_e, dycat)


_MESH = pl.DeviceIdType.MESH
_ANY = pl.BlockSpec(memory_space=pl.ANY)
_VMEM = pl.BlockSpec(memory_space=pltpu.VMEM)


def _my_place():
    x, y, c = lax.axis_index("x"), lax.axis_index("y"), lax.axis_index("c")
    return x, y, c, 4 * x + 2 * y + c


def _peer(k, x, y, c):
    px = 1 - x if k & 4 else x
    py = 1 - y if k & 2 else y
    pc = 1 - c if k & 1 else c
    return (px, py, pc), 4 * px + 2 * py + pc


def _exchange_all(src_of_peer, dst_ref, send_sems, recv_sems, x, y, c, me):
    sent = []
    for k in range(1, N_DEV):
        dev, pidx = _peer(k, x, y, c)
        cp = pltpu.make_async_remote_copy(src_ref=src_of_peer(pidx), dst_ref=dst_ref.at[me],
                                          send_sem=send_sems.at[k - 1], recv_sem=recv_sems.at[k - 1],
                                          device_id=dev, device_id_type=_MESH)
        cp.start()
        sent.append(cp)
    for k in range(1, N_DEV):
        dev, pidx = _peer(k, x, y, c)
        pltpu.make_async_remote_copy(src_ref=src_of_peer(pidx), dst_ref=dst_ref.at[pidx],
                                     send_sem=send_sems.at[k - 1], recv_sem=recv_sems.at[k - 1],
                                     device_id=dev, device_id_type=_MESH).wait_recv()
    for cp in sent:
        cp.wait_send()


def _ada_exchange(cv8, w_ada, b_ada8):
    def body(cv_ref, w_ref, b_ref, call_ref, modp_ref, part_s, s1, r1, s2, r2):
        x, y, c, me = _my_place()
        call_ref[me] = cv_ref[...]
        _exchange_all(lambda pidx: cv_ref, call_ref, s1, r1, x, y, c, me)
        bias = b_ref[me]
        for j in range(N_DEV):
            cj = call_ref[j][:, :D_MODEL]
            part_s[j] = _hdot(cj * jax.nn.sigmoid(cj), w_ref[...]) + bias
        modp_ref[me] = part_s[me]
        _exchange_all(lambda pidx: part_s.at[pidx], modp_ref, s2, r2, x, y, c, me)

    nsh = w_ada.shape[1]
    return pl.pallas_call(
        body, name="ada_exchange",
        in_specs=[_VMEM, _VMEM, _VMEM], out_specs=[_VMEM, _VMEM],
        out_shape=[jax.ShapeDtypeStruct((N_DEV, 8, cv8.shape[1]), f32), jax.ShapeDtypeStruct((N_DEV, 8, nsh), f32)],
        scratch_shapes=[pltpu.VMEM((N_DEV, 8, nsh), f32)] + [pltpu.SemaphoreType.DMA((N_DEV - 1,))] * 4,
        compiler_params=pltpu.CompilerParams(vmem_limit_bytes=VMEM_LIMIT),
    )(cv8, w_ada, b_ada8)


def _all_to_all(arrs, name):
    ex = _Exchange(arrs, gather=False)

    def body(*refs):
        srcs, dsts, sems = refs[:ex.n], refs[ex.n:2 * ex.n], refs[2 * ex.n:]
        ex.start(srcs, dsts, sems)
        ex.wait(srcs, dsts, sems)

    return pl.pallas_call(
        body, name=name,
        in_specs=[_ANY] * ex.n, out_specs=[_ANY] * ex.n,
        out_shape=ex.out_shape(), scratch_shapes=ex.scratch(),
    )(*arrs)


class _Exchange:
    def __init__(self, arrs, gather):
        self.arrs, self.gather, self.n = list(arrs), gather, len(arrs)

    def out_shape(self):
        return [jax.ShapeDtypeStruct(((N_DEV,) + a.shape) if self.gather else a.shape, a.dtype) for a in self.arrs]

    def scratch(self):
        if self.n == 0:
            return []
        return [pltpu.SemaphoreType.DMA((self.n, N_DEV - 1)), pltpu.SemaphoreType.DMA((self.n, N_DEV - 1)),
                pltpu.SemaphoreType.DMA((self.n,))]

    def _src(self, srcs, a, idx):
        return srcs[a] if self.gather else srcs[a].at[idx]

    def _copies(self, srcs, dsts, sems, incoming):
        send_sems, recv_sems, _ = sems
        x, y, c, me = _my_place()
        out = []
        for a in range(self.n):
            for k in range(1, N_DEV):
                dev, pidx = _peer(k, x, y, c)
                out.append(pltpu.make_async_remote_copy(
                    src_ref=self._src(srcs, a, pidx), dst_ref=dsts[a].at[pidx if incoming else me],
                    send_sem=send_sems.at[a, k - 1], recv_sem=recv_sems.at[a, k - 1],
                    device_id=dev, device_id_type=_MESH))
        return out

    def _local(self, srcs, dsts, sems):
        me = _my_place()[3]
        return [pltpu.make_async_copy(self._src(srcs, a, me), dsts[a].at[me], sems[2].at[a]) for a in range(self.n)]

    def start(self, srcs, dsts, sems):
        for cp in self._local(srcs, dsts, sems) + self._copies(srcs, dsts, sems, incoming=False):
            cp.start()

    def wait(self, srcs, dsts, sems):
        for cp in self._copies(srcs, dsts, sems, incoming=True):
            cp.wait_recv()
        for cp in self._copies(srcs, dsts, sems, incoming=False):
            cp.wait_send()
        for cp in self._local(srcs, dsts, sems):
            cp.wait()

    def start_at_first_step(self, grid, srcs, dsts, sems):
        first = functools.reduce(jnp.logical_and, [pl.program_id(i) == 0 for i in range(len(grid))])
        pl.when(first)(lambda: self.start(srcs, dsts, sems))

    def wait_at_last_step(self, grid, srcs, dsts, sems):
        last = functools.reduce(jnp.logical_and, [pl.program_id(i) == g - 1 for i, g in enumerate(grid)])
        pl.when(last)(lambda: self.wait(srcs, dsts, sems))


class _ChipGather:
    def __init__(self, shard):
        self.shard = shard

    def out_shape(self):
        return jax.ShapeDtypeStruct((N_DEV,) + self.shard.shape, self.shard.dtype)

    def scratch(self):
        return [pltpu.SemaphoreType.DMA((N_DEV - 1,)), pltpu.SemaphoreType.DMA((N_DEV - 1,)),
                pltpu.SemaphoreType.DMA(())]

    def _place(self):
        x, y, c, me = _my_place()
        return x, y, c, me, (x, y, 1 - c), [(1 - x, y), (x, 1 - y), (1 - x, 1 - y)]

    def _copy(self, out, sems, k, block, to, src=None):
        rows = out.at[4 * block[0] + 2 * block[1] + block[2]]
        return pltpu.make_async_remote_copy(src_ref=rows if src is None else src, dst_ref=rows,
                                            send_sem=sems[0].at[k], recv_sem=sems[1].at[k],
                                            device_id=to, device_id_type=_MESH)

    def start(self, src, out, sems):
        x, y, c, me, sib, chips = self._place()
        pltpu.make_async_copy(src, out.at[me], sems[2]).start()
        self._copy(out, sems, 0, (x, y, c), sib, src=src).start()
        for j, chip in enumerate(chips):
            self._copy(out, sems, 1 + j, (x, y, c), (*chip, c), src=src).start()

    def forward(self, src, out, sems):
        x, y, c, me, sib, chips = self._place()
        for j, chip in enumerate(chips):
            self._copy(out, sems, 1 + j, (*chip, c), (x, y, c)).wait_recv()
            self._copy(out, sems, 4 + j, (*chip, c), sib).start()

    def finish(self, src, out, sems):
        x, y, c, me, sib, chips = self._place()
        self._copy(out, sems, 0, (x, y, 1 - c), (x, y, c)).wait_recv()
        for j, chip in enumerate(chips):
            self._copy(out, sems, 4 + j, (*chip, 1 - c), (x, y, c)).wait_recv()
        self._copy(out, sems, 0, (x, y, c), sib, src=src).wait_send()
        for j, chip in enumerate(chips):
            self._copy(out, sems, 1 + j, (x, y, c), (*chip, c), src=src).wait_send()
            self._copy(out, sems, 4 + j, (*chip, c), sib).wait_send()
        pltpu.make_async_copy(src, out.at[me], sems[2]).wait()


def _ride(body, n_in, n_out, xchg, grid):
    nx = xchg.n
    if nx == 0:
        return body

    def wrapped(*refs):
        ins, xs = refs[:n_in], refs[n_in:n_in + nx]
        outs, xd = refs[n_in + nx:n_in + nx + n_out], refs[n_in + nx + n_out:n_in + 2 * nx + n_out]
        scratch = refs[n_in + 2 * nx + n_out:]
        xchg.start_at_first_step(grid, xs, xd, scratch[-3:])
        body(*ins, *outs, *scratch[:-3])
        xchg.wait_at_last_step(grid, xs, xd, scratch[-3:])

    return wrapped


def _adamw_math(w, g, m, v):
    m2 = ADAM_B1 * m + (1.0 - ADAM_B1) * g
    v2 = ADAM_B2 * v + (1.0 - ADAM_B2) * (g * g)
    m_hat = m2 / (1.0 - ADAM_B1 ** ADAM_STEP)
    v_hat = v2 / (1.0 - ADAM_B2 ** ADAM_STEP)
    delta = -ADAM_LR * (m_hat / (jnp.sqrt(v_hat) + ADAM_EPS) + ADAM_WD * w)
    return delta, m2, v2


def _row_tile(rows):
    for t in (256, 128, 64, 32, 16, 8):
        if rows % t == 0:
            return t
    return rows


def _reduce_adamw(parts, w, m, v, name):
    _, r, cdim = parts.shape
    tr = _row_tile(r)

    def body(p_ref, w_ref, m_ref, v_ref, g_ref, d_ref, m2_ref, v2_ref):
        g = p_ref[0].astype(f32)
        for j in range(1, N_DEV):
            g = g + p_ref[j].astype(f32)
        delta, m2, v2 = _adamw_math(w_ref[...], g, m_ref[...], v_ref[...])
        g_ref[...] = g
        d_ref[...] = delta
        m2_ref[...] = m2
        v2_ref[...] = v2

    spec = pl.BlockSpec((tr, cdim), lambda i: (i, 0))
    return pl.pallas_call(
        body, name=name, grid=(r // tr,),
        in_specs=[pl.BlockSpec((N_DEV, tr, cdim), lambda i: (0, i, 0)), spec, spec, spec],
        out_specs=[spec] * 4,
        out_shape=[jax.ShapeDtypeStruct((r, cdim), f32)] * 4,
        compiler_params=_cparams(("parallel",)),
    )(parts, w, m, v)


def _adamw(w, g, m, v, name):
    r, cdim = w.shape
    tr = _row_tile(r)

    def body(w_ref, g_ref, m_ref, v_ref, d_ref, m2_ref, v2_ref):
        delta, m2, v2 = _adamw_math(w_ref[...], g_ref[...], m_ref[...], v_ref[...])
        d_ref[...] = delta
        m2_ref[...] = m2
        v2_ref[...] = v2

    spec = pl.BlockSpec((tr, cdim), lambda i: (i, 0))
    return pl.pallas_call(
        body, name=name, grid=(r // tr,),
        in_specs=[spec] * 4, out_specs=[spec] * 3,
        out_shape=[jax.ShapeDtypeStruct((r, cdim), f32)] * 3,
        compiler_params=_cparams(("parallel",)),
    )(w, g, m, v)


def _sum_devices(parts, name):
    _, r, cdim = parts.shape

    def body(p_ref, o_ref):
        g = p_ref[0]
        for j in range(1, N_DEV):
            g = g + p_ref[j]
        o_ref[...] = g

    return pl.pallas_call(
        body, name=name, out_shape=jax.ShapeDtypeStruct((r, cdim), f32),
        in_specs=[_VMEM], out_specs=_VMEM,
    )(parts)


def _ada_wgrad(c_all8, dmod_cols):
    nsh = dmod_cols.shape[1]

    def body(c_ref, d_ref, o_ref):
        cv = c_ref[...]
        o_ref[...] = lax.dot_general(cv * jax.nn.sigmoid(cv), d_ref[...], _TN, precision=_HI,
                                     preferred_element_type=f32)

    return pl.pallas_call(
        body, name="ada_wgrad", out_shape=jax.ShapeDtypeStruct((D_MODEL, nsh), f32),
        in_specs=[_VMEM, _VMEM], out_specs=_VMEM,
        compiler_params=pltpu.CompilerParams(vmem_limit_bytes=VMEM_LIMIT),
    )(c_all8, dmod_cols)


def _cols(t):
    return t.transpose(1, 0, 2).reshape(t.shape[1], N_DEV * t.shape[2])


def _col_blocks(t, n):
    return t.reshape(t.shape[0], N_DEV, n).transpose(1, 0, 2).astype(bf16)


def _row_blocks(t):
    return t.reshape(N_DEV, t.shape[0] // N_DEV, t.shape[1]).astype(bf16)


def _local_step(x, tgt, mod, norm_attn_g, w_in_sh, rel_bias, conv_full, a_log, dt_bias, delta_norm_g,
                norm_ffn_g, final_norm_g, w_out_sh, w_gate_sh, w_up_sh, w_down_sh):
    s = x.shape[0]
    sh1, sc1, g1, sh2, sc2, g2 = [mod[:, i * D_MODEL:(i + 1) * D_MODEL] for i in range(6)]
    nag = norm_attn_g.reshape(1, D_MODEL)
    nfg = norm_ffn_g.reshape(1, D_MODEL)
    fg = final_norm_g.reshape(1, D_MODEL)
    idx = _bucket_tables()
    bias = _bias_tables(rel_bias, idx)
    alog_e = jnp.repeat(a_log.reshape(N_HEADS), HEAD_DIM)[None]
    dt_e = jnp.repeat(dt_bias.reshape(N_HEADS), HEAD_DIM)[None]
    ng_e = jnp.tile(delta_norm_g.reshape(HEAD_DIM), N_HEADS)[None]

    h1, w_in_g = _ln_mod_fwd(x, nag, sc1, sh1, w_in_sh, "ln1_fwd")
    w_in_p = jnp.pad(_cols(w_in_g), ((0, 0), (0, IN_PAD - IN_WIDTH)))
    proj, (w_out_g,) = _mm(h1, w_in_p, "nn", f32, 512, IN_PAD, 1024, "in_proj",
                           xchg=_Exchange([w_out_sh], gather=True))
    (y_attn, lse), (w_gate_g, w_up_g, w_down_g) = _attn_fwd(
        proj, bias, _Exchange([w_gate_sh, w_up_sh, w_down_sh], gather=True))
    w_out_b = w_out_g.reshape(2 * GROUP_W, D_MODEL)
    w_down_b = w_down_g.reshape(D_FF, D_MODEL)
    w_gate_b, w_up_b = _cols(w_gate_g), _cols(w_up_g)
    n_ff = w_gate_sh.shape[1]
    sconv, yconv = _conv_silu_fwd(proj, conv_full)
    qn, kn, beta, g = _delta_prep_fwd(sconv, proj, alog_e, dt_e)
    (u, w, qt, kh, qk, tinv, gm), _ = _delta_chunk_pre(qn, kn, sconv, beta, g, _Exchange([], gather=False))
    o, ss = _delta_scan_fwd(u, w, qt, kh, qk, gm)
    y_delta = _gnorm_fwd(o, proj, ng_e)
    y, x1, h2 = _proj_resid_ln_mod_fwd([(y_attn, w_out_b[:GROUP_W]), (y_delta, w_out_b[GROUP_W:])],
                                       x, g1, nfg, sc2, sh2, "out_proj_ln2")
    act, gate, up = _ffn_up(h2, w_gate_b, w_up_b, "ffn_up")
    dx2, dy2, loss, dfg, dg2 = _proj_final_loss_bwd(act, w_down_b, x1, g2, fg, tgt, "ffn_down_loss")

    dgate, dup = _ffn_down_dx(dy2, w_down_b, gate, up, "ffn_down_dx")
    g_down = _mm(act, dy2, "tn", f32, 1408, 1024, 1024, "ffn_down_dw")
    (dx1, dsh2, dsc2, dnfg, dy, dg1), (r_down,) = _proj_ln_mod_bwd(
        [(dgate, w_gate_b), (dup, w_up_b)], x1, nfg, sc2, dx2, 256, "ffn_up_dx_ln2",
        _Exchange([_row_blocks(g_down)], gather=False), gate=g1, y=y)
    g_gate = _mm(h2, dgate, "tn", f32, 1024, 1408, 1024, "ffn_gate_dw")
    g_up = _mm(h2, dup, "tn", f32, 1024, 1408, 1024, "ffn_up_dw")
    dycat = _mm(dy, w_out_b, "nt", f32, 512, 1024, 1024, "out_proj_dx")
    g_out = jnp.concatenate([_mm(y_attn, dy, "tn", f32, GROUP_W, 1024, 1024, "out_proj_dw_attn"),
                             _mm(y_delta, dy, "tn", f32, GROUP_W, 1024, 1024, "out_proj_dw_delta")], axis=0)
    dq, dk, dv, dbias = _attn_bwd(proj, bias, y_attn, lse, dycat)
    g_rb = _bias_grad(dbias, idx)[:, :, 0].T
    do, dz, dng = _gnorm_bwd(o, proj, ng_e, dycat)
    dso, _ = _delta_scan_bwd(w, qt, kh, qk, gm, do, _Exchange([], gather=False))
    (dqn, dkn, dvd, dbeta, dgd), (r_up,) = _delta_chunk_bwd(
        qn, kn, sconv, beta, g, tinv, ss, dso, do, _Exchange([_col_blocks(g_up, n_ff)], gather=False))
    (dsq, dsk, dba, dal, ddt), _ = _delta_prep_bwd(
        sconv, proj, alog_e, dt_e, dqn, dkn, dbeta, dgd, _Exchange([], gather=False))
    (dxc, g_conv), (r_gate, r_out) = _conv_silu_bwd(
        proj, conv_full, yconv, (dsq, dsk, dvd),
        _Exchange([_col_blocks(g_gate, n_ff), _row_blocks(g_out)], gather=False))
    pieces = ((dq, 0), (dk, GROUP_W), (dv, 2 * GROUP_W), (dxc, DELTA_COL), (dz, Z_COL), (dba, BA_BLOCK * 128))
    g_in = jnp.concatenate(
        [_mm(h1, p, "tn", f32, 1024, min(p.shape[1], 768), 1024, "in_proj_dw_%d" % c) for p, c in pieces], axis=1)
    (gx, dsh1, dsc1, dnag), (r_in,) = _proj_ln_mod_bwd(
        [(p, w_in_p[:, c:c + p.shape[1]]) for p, c in pieces], x, nag, sc1, dx1, TOK_TILE, "in_proj_dx_ln1",
        _Exchange([_col_blocks(g_in[:, :IN_WIDTH], IN_WIDTH // N_DEV)], gather=False))
    grads = dict(
        x=gx, mod=jnp.concatenate([dsh1, dsc1, dg1, dsh2, dsc2, dg2], axis=1),
        norm_attn_g=dnag, norm_ffn_g=dnfg, final_norm_g=dfg, rel_bias=g_rb, conv_w=g_conv,
        a_log=dal.reshape(N_HEADS, HEAD_DIM).sum(-1), dt_bias=ddt.reshape(N_HEADS, HEAD_DIM).sum(-1),
        delta_norm_g=dng.reshape(N_HEADS, HEAD_DIM).sum(0),
        w_in=r_in, w_out=r_out, w_gate=r_gate, w_up=r_up, w_down=r_down)
    return loss[0, 0], grads


def _misc_row(rel_bias, a_log, dt_bias, delta_norm_g):
    flat = jnp.concatenate([rel_bias.reshape(-1), a_log.reshape(-1), dt_bias.reshape(-1), delta_norm_g.reshape(-1)])
    return jnp.pad(flat, (0, D_MODEL - flat.shape[0]))[None]


def _pack_small(b_ada, nag, nfg, fng, rel_bias, a_log, dt_bias, dng, conv_shard):
    rows = [b_ada.reshape(6, D_MODEL), nag.reshape(1, D_MODEL), nfg.reshape(1, D_MODEL), fng.reshape(1, D_MODEL),
            _misc_row(rel_bias, a_log, dt_bias, dng),
            jnp.pad(conv_shard.reshape(-1), (0, D_MODEL - conv_shard.size))[None],
            jnp.zeros((5, D_MODEL), f32)]
    return jnp.concatenate(rows, axis=0)


def _unpack_small(p, conv_shape):
    misc = p[9]
    return dict(
        b_ada=p[0:6].reshape(1, 6 * D_MODEL), norm_attn_g=p[6:7], norm_ffn_g=p[7:8], final_norm_g=p[8],
        rel_bias=misc[0:256].reshape(N_BUCKETS, N_HEADS), a_log=misc[256:264].reshape(1, N_HEADS),
        dt_bias=misc[264:272].reshape(1, N_HEADS), delta_norm_g=misc[272:336].reshape(1, HEAD_DIM),
        conv_w=p[10, :conv_shape[1] * conv_shape[2]].reshape(conv_shape))


def kernel(x, c, w_ada, b_ada, norm_attn_g, w_in, rel_bias, conv_w, a_log, dt_bias, delta_norm_g, w_out, norm_ffn_g, w_gate, w_up, w_down, final_norm_g, loss_target, m_w_ada, m_b_ada, m_norm_attn_g, m_w_in, m_rel_bias, m_conv_w, m_a_log, m_dt_bias, m_delta_norm_g, m_w_out, m_norm_ffn_g, m_w_gate, m_w_up, m_w_down, m_final_norm_g, v_w_ada, v_b_ada, v_norm_attn_g, v_w_in, v_rel_bias, v_conv_w, v_a_log, v_dt_bias, v_delta_norm_g, v_w_out, v_norm_ffn_g, v_w_gate, v_w_up, v_w_down, v_final_norm_g):
    me = 4 * lax.axis_index("x") + 2 * lax.axis_index("y") + lax.axis_index("c")
    ada_sh = w_ada.shape[2]
    conv_sh = conv_w.shape[2]

    cv = jnp.concatenate([c[0], conv_w[0].reshape(-1)])
    cv8 = jnp.zeros((8, 2 * D_MODEL), f32).at[0, :cv.shape[0]].set(cv)
    b8 = jnp.broadcast_to(b_ada.reshape(N_DEV, 1, ada_sh), (N_DEV, 8, ada_sh))
    call, modp = _ada_exchange(cv8, w_ada[0], b8)
    mod = modp[:, 0, :].reshape(1, 6 * D_MODEL)
    c_all = call[:, 0, :D_MODEL]
    conv_full = call[:, 0, D_MODEL:D_MODEL + CONV_WIDTH * conv_sh].reshape(N_DEV, CONV_WIDTH, conv_sh)
    conv_full = conv_full.transpose(1, 0, 2).reshape(CONV_WIDTH, N_DEV * conv_sh)

    loss_local, gr = _local_step(x[0], loss_target[0], mod, norm_attn_g, w_in[0].astype(bf16), rel_bias, conv_full, a_log,
                                 dt_bias, delta_norm_g, norm_ffn_g, final_norm_g, w_out[0].astype(bf16),
                                 w_gate[0].astype(bf16), w_up[0].astype(bf16), w_down[0].astype(bf16))
    loss = lax.psum(loss_local, ("x", "y", "c"))

    small = jnp.concatenate([
        gr["mod"].reshape(6, D_MODEL), gr["norm_attn_g"], gr["norm_ffn_g"], gr["final_norm_g"],
        gr["conv_w"].reshape(6, D_MODEL),
        _misc_row(gr["rel_bias"], gr["a_log"], gr["dt_bias"], gr["delta_norm_g"])], axis=0)
    parts = _all_to_all([jnp.broadcast_to(small[None], (N_DEV,) + small.shape)], "small_gather")[0]
    tot = _sum_devices(parts, "small_sum")
    g_conv_full = tot[9:15].reshape(CONV_WIDTH, N_DEV * conv_sh)
    g_conv = lax.dynamic_slice(g_conv_full, (0, me * conv_sh), (CONV_WIDTH, conv_sh))
    misc = tot[15]
    g_small = _pack_small(tot[0:6], tot[6], tot[7], tot[8], misc[0:256], misc[256:264], misc[264:272],
                          misc[272:336], g_conv)
    pk = lambda pre: _pack_small(pre[0], pre[1], pre[2], pre[3], pre[4], pre[5], pre[6], pre[7], pre[8])
    w_small = pk((b_ada, norm_attn_g, norm_ffn_g, final_norm_g, rel_bias, a_log, dt_bias, delta_norm_g, conv_w))
    m_small = pk((m_b_ada, m_norm_attn_g, m_norm_ffn_g, m_final_norm_g, m_rel_bias, m_a_log, m_dt_bias,
                  m_delta_norm_g, m_conv_w))
    v_small = pk((v_b_ada, v_norm_attn_g, v_norm_ffn_g, v_final_norm_g, v_rel_bias, v_a_log, v_dt_bias,
                  v_delta_norm_g, v_conv_w))
    d_small, m2_small, v2_small = _adamw(w_small, g_small, m_small, v_small, "adamw_small")
    cshape = conv_w.shape
    G, Dl, M2, V2 = (_unpack_small(t, cshape) for t in (g_small, d_small, m2_small, v2_small))

    dmod_all = parts[:, 0:6, :].reshape(N_DEV, 6 * D_MODEL)
    dmod_cols = lax.dynamic_slice(dmod_all, (0, me * ada_sh), (N_DEV, ada_sh))
    g_ada = _ada_wgrad(c_all, dmod_cols)
    d_ada, m2_ada, v2_ada = _adamw(w_ada[0], g_ada, m_w_ada[0], v_w_ada[0], "adamw_w_ada")

    big = {}
    for name, w_, m_, v_ in (("w_in", w_in, m_w_in, v_w_in), ("w_out", w_out, m_w_out, v_w_out),
                             ("w_gate", w_gate, m_w_gate, v_w_gate), ("w_up", w_up, m_w_up, v_w_up),
                             ("w_down", w_down, m_w_down, v_w_down)):
        big[name] = [t[None] for t in _reduce_adamw(gr[name], w_[0], m_[0], v_[0], "reduce_adamw_" + name)]

    def leaf(i, name):
        if name == "w_ada":
            return (g_ada, d_ada, m2_ada, v2_ada)[i][None]
        if name in big:
            return big[name][i]
        return (G, Dl, M2, V2)[i][name]

    order = ["w_ada", "b_ada", "norm_attn_g", "w_in", "rel_bias", "conv_w", "a_log", "dt_bias", "delta_norm_g",
             "w_out", "norm_ffn_g", "w_gate", "w_up", "w_down", "final_norm_g"]
    outs = [loss, gr["x"][None]]
    for i in range(4):
        outs += [leaf(i, n) for n in order]
    return tuple(outs)
```

```python
import functools
import math

import jax
import jax.numpy as jnp
from jax import lax
from jax.experimental import pallas as pl
from jax.experimental.pallas import tpu as pltpu

f32 = jnp.float32
bf16 = jnp.bfloat16

D_MODEL = 1024
HEAD_DIM = 64
N_HEADS = 8
GROUP_W = 512
IN_WIDTH = 3600
IN_PAD = 3840
D_FF = 2816
EPS = 1e-6
NEG_INF = -1e30
BAND = 128
PAD_UNIT = 2048
DILATIONS = (1, 4, 16)
N_BUCKETS = 32
MAX_DISTANCE = 2048
CONV_WIDTH = 4
CHUNK = 64
N_DEV = 8
VMEM_LIMIT = 56 * 1024 * 1024

ADAM_LR, ADAM_B1, ADAM_B2, ADAM_EPS, ADAM_WD, ADAM_STEP = 0.001, 0.9, 0.999, 1e-08, 0.01, 10


def _cparams(sem):
    return pltpu.CompilerParams(dimension_semantics=sem, vmem_limit_bytes=VMEM_LIMIT)


def _mm(a, b, mode, out_dtype, tm, tn, tk, name, xchg=None):
    if mode == "nn":
        (m, k), (_, n) = a.shape, b.shape
        a_spec = pl.BlockSpec((tm, tk), lambda j, i, kk: (i, kk))
        b_spec = pl.BlockSpec((tk, tn), lambda j, i, kk: (kk, j))
        dims = (((1,), (0,)), ((), ()))
    elif mode == "nt":
        (m, k), (n, _) = a.shape, b.shape
        a_spec = pl.BlockSpec((tm, tk), lambda j, i, kk: (i, kk))
        b_spec = pl.BlockSpec((tn, tk), lambda j, i, kk: (j, kk))
        dims = (((1,), (1,)), ((), ()))
    else:
        (k, m), (_, n) = a.shape, b.shape
        a_spec = pl.BlockSpec((tk, tm), lambda j, i, kk: (kk, i))
        b_spec = pl.BlockSpec((tk, tn), lambda j, i, kk: (kk, j))
        dims = (((0,), (0,)), ((), ()))
    assert m % tm == 0 and n % tn == 0 and k % tk == 0, (name, m, n, k, tm, tn, tk)
    nk = k // tk
    grid = (n // tn, m // tm, nk)
    nx = xchg.n if xchg is not None else 0

    def body(*refs):
        a_ref, b_ref = refs[:2]
        o_ref = refs[2 + nx]
        scratch = refs[3 + 2 * nx:]
        if nx:
            xrefs = (refs[2:2 + nx], refs[3 + nx:3 + 2 * nx], scratch[-3:])
            xchg.start_at_first_step(grid, *xrefs)
        if nk == 1:
            o_ref[...] = lax.dot_general(a_ref[...].astype(bf16), b_ref[...].astype(bf16), dims,
                                         preferred_element_type=f32).astype(o_ref.dtype)
        else:
            acc_ref = scratch[0]
            kk = pl.program_id(2)

            @pl.when(kk == 0)
            def _():
                acc_ref[...] = jnp.zeros_like(acc_ref)

            acc_ref[...] += lax.dot_general(a_ref[...].astype(bf16), b_ref[...].astype(bf16), dims,
                                            preferred_element_type=f32)

            @pl.when(kk == nk - 1)
            def _():
                o_ref[...] = acc_ref[...].astype(o_ref.dtype)
        if nx:
            xchg.wait_at_last_step(grid, *xrefs)

    out = pl.pallas_call(
        body, name=name, grid=grid,
        in_specs=[a_spec, b_spec] + ([_ANY] * nx),
        out_specs=[pl.BlockSpec((tm, tn), lambda j, i, kk: (i, j))] + ([_ANY] * nx),
        out_shape=[jax.ShapeDtypeStruct((m, n), out_dtype)] + (xchg.out_shape() if nx else []),
        scratch_shapes=([pltpu.VMEM((tm, tn), f32)] if nk > 1 else []) + (xchg.scratch() if nx else []),
        compiler_params=_cparams(("arbitrary",) * 3 if nx else ("parallel", "parallel", "arbitrary")),
    )(a, b, *(xchg.arrs if nx else []))
    return (out[0], out[1:]) if nx else out[0]


TOK_TILE = 512
SUB_COLS = 384


def _row_spec(width, tile=TOK_TILE):
    return pl.BlockSpec((tile, width), lambda i: (i, 0))


def _vec_spec(width, rows=1):
    return pl.BlockSpec((rows, width), lambda i: (0, 0))


def _ln_mod_fwd(x, gain, sc, sh, shard, name):
    s, d = x.shape
    nt = s // TOK_TILE
    ride = _ChipGather(shard)

    def body(x_ref, g_ref, sc_ref, sh_ref, sh_in, h_ref, sh_out, *sems):
        i = pl.program_id(0)
        pl.when(i == 0)(lambda: ride.start(sh_in, sh_out, sems))
        xv = x_ref[...]
        rstd = lax.rsqrt(jnp.mean(xv * xv, axis=-1, keepdims=True) + EPS)
        h = (xv * rstd) * g_ref[...] * (1.0 + sc_ref[...]) + sh_ref[...]
        h_ref[...] = h.astype(bf16)
        @pl.when(i == nt - 1)
        def _():
            ride.forward(sh_in, sh_out, sems)
            ride.finish(sh_in, sh_out, sems)

    return pl.pallas_call(
        body, name=name, grid=(nt,),
        in_specs=[_row_spec(d), _vec_spec(d), _vec_spec(d), _vec_spec(d), _ANY],
        out_specs=[_row_spec(d), _ANY],
        out_shape=[jax.ShapeDtypeStruct((s, d), bf16), ride.out_shape()],
        scratch_shapes=ride.scratch(),
        compiler_params=_cparams(("arbitrary",)),
    )(x, gain, sc, sh, shard)


def _proj_resid_ln_mod_fwd(pairs, x, gate, gain, sc, sh, name):
    s, d = x.shape
    npair = len(pairs)

    def body(*refs):
        aw = refs[:2 * npair]
        x_ref, gt_ref, g_ref, sc_ref, sh_ref, y_ref, x1_ref, h_ref = refs[2 * npair:]
        y = jnp.dot(aw[0][...].astype(bf16), aw[1][...], preferred_element_type=f32)
        for t in range(1, npair):
            y = y + jnp.dot(aw[2 * t][...].astype(bf16), aw[2 * t + 1][...], preferred_element_type=f32)
        y_ref[...] = y
        x1 = x_ref[...] + gt_ref[...] * y
        x1_ref[...] = x1
        rstd = lax.rsqrt(jnp.mean(x1 * x1, axis=-1, keepdims=True) + EPS)
        h = (x1 * rstd) * g_ref[...] * (1.0 + sc_ref[...]) + sh_ref[...]
        h_ref[...] = h.astype(bf16)

    aw_specs, aw = [], []
    for a, w in pairs:
        aw_specs += [_row_spec(a.shape[1]), pl.BlockSpec(w.shape, lambda i: (0, 0))]
        aw += [a, w]
    return pl.pallas_call(
        body, name=name, grid=(s // TOK_TILE,),
        in_specs=aw_specs + [_row_spec(d)] + [_vec_spec(d)] * 4,
        out_specs=[_row_spec(d)] * 3,
        out_shape=[jax.ShapeDtypeStruct((s, d), f32)] * 2 + [jax.ShapeDtypeStruct((s, d), bf16)],
        compiler_params=_cparams(("parallel",)),
    )(*aw, x, gate, gain, sc, sh)


FFN_TN = 1408


def _ffn_up(h2, w_gate, w_up, name):
    s, d = h2.shape
    tm = TOK_TILE

    def body(h_ref, wg_ref, wu_ref, a_ref, g_ref, u_ref):
        h = h_ref[...]
        g = jnp.dot(h, wg_ref[...], preferred_element_type=f32)
        u = jnp.dot(h, wu_ref[...], preferred_element_type=f32)
        a_ref[...] = (g * jax.nn.sigmoid(g) * u).astype(bf16)
        g_ref[...] = g.astype(bf16)
        u_ref[...] = u.astype(bf16)

    w_spec = pl.BlockSpec((d, FFN_TN), lambda j, i: (0, j))
    o_spec = pl.BlockSpec((tm, FFN_TN), lambda j, i: (i, j))
    return pl.pallas_call(
        body, name=name, grid=(D_FF // FFN_TN, s // tm),
        in_specs=[pl.BlockSpec((tm, d), lambda j, i: (i, 0)), w_spec, w_spec],
        out_specs=[o_spec] * 3,
        out_shape=[jax.ShapeDtypeStruct((s, D_FF), bf16)] * 3,
        compiler_params=_cparams(("parallel", "parallel")),
    )(h2, w_gate, w_up)


def _ffn_down_dx(dy2, w_down, gate, up, name):
    s, d = dy2.shape
    tm = TOK_TILE

    def body(dy_ref, w_ref, g_ref, u_ref, dg_ref, du_ref):
        dy = dy_ref[...]
        for c0 in range(0, FFN_TN, SUB_COLS):
            cols = slice(c0, min(c0 + SUB_COLS, FFN_TN))
            da = lax.dot_general(dy, w_ref[cols, :], _NT, preferred_element_type=f32)
            g = g_ref[:, cols].astype(f32)
            sg = jax.nn.sigmoid(g)
            du_ref[:, cols] = (da * g * sg).astype(bf16)
            dg_ref[:, cols] = (da * u_ref[:, cols].astype(f32) * sg * (1.0 + g * (1.0 - sg))).astype(bf16)

    t_spec = pl.BlockSpec((tm, FFN_TN), lambda j, i: (i, j))
    return pl.pallas_call(
        body, name=name, grid=(D_FF // FFN_TN, s // tm),
        in_specs=[pl.BlockSpec((tm, d), lambda j, i: (i, 0)), pl.BlockSpec((FFN_TN, d), lambda j, i: (j, 0)),
                  t_spec, t_spec],
        out_specs=[t_spec, t_spec],
        out_shape=[jax.ShapeDtypeStruct((s, D_FF), bf16)] * 2,
        compiler_params=_cparams(("parallel", "parallel")),
    )(dy2, w_down, gate, up)


def _acc_spec(width):
    return pl.BlockSpec((1, width), lambda i: (0, 0))


def _proj_final_loss_bwd(a, w, x1, gate2, final_g, target, name):
    s, d = x1.shape
    k = a.shape[1]

    def body(a_ref, w_ref, x1_ref, gt_ref, fg_ref, tg_ref, dx2_ref, dy2_ref, loss_ref, dfg_ref, dgt_ref):
        @pl.when(pl.program_id(0) == 0)
        def _():
            loss_ref[...] = jnp.zeros_like(loss_ref)
            dfg_ref[...] = jnp.zeros_like(dfg_ref)
            dgt_ref[...] = jnp.zeros_like(dgt_ref)

        y2 = jnp.dot(a_ref[...], w_ref[...], preferred_element_type=f32)
        gt = gt_ref[...]
        fg = fg_ref[...]
        x2 = x1_ref[...] + gt * y2
        rstd = lax.rsqrt(jnp.mean(x2 * x2, axis=-1, keepdims=True) + EPS)
        xn = x2 * rstd
        err = xn * fg - tg_ref[...]
        row = jnp.sum(err * err, axis=-1, keepdims=True) * (0.5 / d)
        loss_ref[...] += jnp.sum(row, axis=0, keepdims=True) + jnp.zeros_like(loss_ref)
        dout = err * (1.0 / d)
        dfg_ref[...] += jnp.sum(dout * xn, axis=0, keepdims=True)
        dxn = dout * fg
        dx2 = rstd * (dxn - xn * jnp.mean(dxn * xn, axis=-1, keepdims=True))
        dx2_ref[...] = dx2
        dgt_ref[...] += jnp.sum(dx2 * y2, axis=0, keepdims=True)
        dy2_ref[...] = (gt * dx2).astype(bf16)

    return pl.pallas_call(
        body, name=name, grid=(s // TOK_TILE,),
        in_specs=[_row_spec(k), pl.BlockSpec((k, d), lambda i: (0, 0)), _row_spec(d), _vec_spec(d), _vec_spec(d),
                  _row_spec(d)],
        out_specs=[_row_spec(d), _row_spec(d), _acc_spec(128), _acc_spec(d), _acc_spec(d)],
        out_shape=[jax.ShapeDtypeStruct((s, d), f32), jax.ShapeDtypeStruct((s, d), bf16),
                   jax.ShapeDtypeStruct((1, 128), f32), jax.ShapeDtypeStruct((1, d), f32),
                   jax.ShapeDtypeStruct((1, d), f32)],
        compiler_params=_cparams(("arbitrary",)),
    )(a, w, x1, gate2, final_g, target)


def _proj_ln_mod_bwd(pairs, xin, gain, sc, dres, tm, name, xchg, gate=None, y=None):
    s, d = xin.shape
    with_gate = gate is not None
    npair = len(pairs)
    n_in = 2 * npair + (7 if with_gate else 5) - 1
    n_out = 6 if with_gate else 4

    def body(*refs):
        ab = refs[:2 * npair]
        if with_gate:
            (x_ref, g_ref, sc_ref, dr_ref, gt_ref, y_ref,
             dx_ref, dsh_ref, dsc_ref, dg_ref, dy_ref, dgt_ref) = refs[2 * npair:]
        else:
            x_ref, g_ref, sc_ref, dr_ref, dx_ref, dsh_ref, dsc_ref, dg_ref = refs[2 * npair:]

        @pl.when(pl.program_id(0) == 0)
        def _():
            dsh_ref[...] = jnp.zeros_like(dsh_ref)
            dsc_ref[...] = jnp.zeros_like(dsc_ref)
            dg_ref[...] = jnp.zeros_like(dg_ref)
            if with_gate:
                dgt_ref[...] = jnp.zeros_like(dgt_ref)

        dh = lax.dot_general(ab[0][...].astype(bf16), ab[1][...], _NT, preferred_element_type=f32)
        for t in range(1, npair):
            dh = dh + lax.dot_general(ab[2 * t][...].astype(bf16), ab[2 * t + 1][...], _NT,
                                      preferred_element_type=f32)
        xv = x_ref[...]
        g = g_ref[...]
        sc1 = 1.0 + sc_ref[...]
        rstd = lax.rsqrt(jnp.mean(xv * xv, axis=-1, keepdims=True) + EPS)
        xn = xv * rstd
        dsh_ref[...] += jnp.sum(dh, axis=0, keepdims=True)
        dsc_ref[...] += jnp.sum(dh * (xn * g), axis=0, keepdims=True)
        dg_ref[...] += jnp.sum(dh * sc1 * xn, axis=0, keepdims=True)
        dxn = dh * sc1 * g
        dx = dr_ref[...] + rstd * (dxn - xn * jnp.mean(dxn * xn, axis=-1, keepdims=True))
        dx_ref[...] = dx
        if with_gate:
            dgt_ref[...] += jnp.sum(dx * y_ref[...], axis=0, keepdims=True)
            dy_ref[...] = (gt_ref[...] * dx).astype(bf16)

    row = lambda width: pl.BlockSpec((tm, width), lambda i: (i, 0))
    in_specs, args = [], []
    for a, b in pairs:
        in_specs += [row(a.shape[1]), pl.BlockSpec(b.shape, lambda i: (0, 0))]
        args += [a, b]
    in_specs += [row(d), _vec_spec(d), _vec_spec(d), row(d)]
    args += [xin, gain, sc, dres]
    out_specs = [row(d), _acc_spec(d), _acc_spec(d), _acc_spec(d)]
    out_shape = [jax.ShapeDtypeStruct((s, d), f32)] + [jax.ShapeDtypeStruct((1, d), f32)] * 3
    if with_gate:
        in_specs += [_vec_spec(d), row(d)]
        out_specs += [row(d), _acc_spec(d)]
        out_shape += [jax.ShapeDtypeStruct((s, d), bf16), jax.ShapeDtypeStruct((1, d), f32)]
        args += [gate, y]
    grid = (s // tm,)
    out = pl.pallas_call(
        _ride(body, n_in, n_out, xchg, grid), name=name, grid=grid,
        in_specs=in_specs + [_ANY] * xchg.n, out_specs=out_specs + [_ANY] * xchg.n,
        out_shape=out_shape + xchg.out_shape(), scratch_shapes=xchg.scratch(),
        compiler_params=_cparams(("arbitrary",)),
    )(*args, *xchg.arrs)
    return out[:n_out], out[n_out:]


def _bucket_tables():
    import numpy as np
    qi = np.arange(BAND)[:, None]
    kj = np.arange(2 * BAND)[None, :]
    steps = qi + BAND - kj
    max_exact = N_BUCKETS // 2
    out = []
    for d in DILATIONS:
        dist = np.maximum(steps, 0) * d
        dist_f = np.maximum(dist, 1).astype(np.float32)
        large = max_exact + (np.log(dist_f / np.float32(max_exact)) / np.float32(math.log(MAX_DISTANCE / max_exact))
                             * np.float32(N_BUCKETS - max_exact)).astype(np.int32)
        out.append(np.where(dist < max_exact, dist, np.minimum(large, N_BUCKETS - 1)))
    return jnp.asarray(np.stack(out).astype(np.int32))


def _bias_tables(rel_bias, idx):
    def body(idx_ref, rb_ref, o_ref):
        h = pl.program_id(1)
        idxv = idx_ref[0]
        acc = jnp.zeros((BAND, 2 * BAND), f32)
        for b in range(N_BUCKETS):
            acc = jnp.where(idxv == b, rb_ref[b, h], acc)
        o_ref[0, 0] = jnp.where(_attn_masks()[1], acc, NEG_INF)

    return pl.pallas_call(
        body, name="attn_bias_tables", grid=(3, N_HEADS),
        in_specs=[pl.BlockSpec((1, BAND, 2 * BAND), lambda br, h: (br, 0, 0)),
                  pl.BlockSpec(memory_space=pltpu.SMEM)],
        out_specs=pl.BlockSpec((1, 1, BAND, 2 * BAND), lambda br, h: (br, h, 0, 0)),
        out_shape=jax.ShapeDtypeStruct((3, N_HEADS, BAND, 2 * BAND), f32),
        compiler_params=_cparams(("parallel", "parallel")),
    )(idx, rel_bias)


def _bias_grad(dbias, idx):
    def body(idx_ref, db_ref, o_ref):
        br = pl.program_id(1)

        @pl.when(br == 0)
        def _():
            o_ref[...] = jnp.zeros_like(o_ref)

        idxv = idx_ref[0]
        dbv = db_ref[0, 0]
        row = lax.broadcasted_iota(jnp.int32, (N_BUCKETS, 128), 0)
        acc = jnp.zeros((N_BUCKETS, 128), f32)
        for b in range(N_BUCKETS):
            sb = jnp.sum(jnp.sum(jnp.where(idxv == b, dbv, 0.0), axis=1, keepdims=True), axis=0, keepdims=True)
            acc = acc + jnp.where(row == b, sb, 0.0)
        o_ref[0] += acc

    return pl.pallas_call(
        body, name="attn_bias_grad", grid=(N_HEADS, 3),
        in_specs=[pl.BlockSpec((1, BAND, 2 * BAND), lambda h, br: (br, 0, 0)),
                  pl.BlockSpec((1, 1, BAND, 2 * BAND), lambda h, br: (br, h, 0, 0))],
        out_specs=pl.BlockSpec((1, N_BUCKETS, 128), lambda h, br: (h, 0, 0)),
        out_shape=jax.ShapeDtypeStruct((N_HEADS, N_BUCKETS, 128), f32),
        compiler_params=_cparams(("parallel", "arbitrary")),
    )(idx, dbias)


def _attn_masks():
    lane = lax.broadcasted_iota(jnp.int32, (BAND, 128), 1)
    m0 = lane < HEAD_DIM
    qi = lax.broadcasted_iota(jnp.int32, (BAND, 2 * BAND), 0)
    kj = lax.broadcasted_iota(jnp.int32, (BAND, 2 * BAND), 1)
    steps = qi + BAND - kj
    in_window = (steps >= 0) & (steps <= BAND)
    return m0, in_window, kj >= BAND


_NT = (((1,), (1,)), ((), ()))
_TN = (((0,), (0,)), ((), ()))
_BNN = (((2,), (1,)), ((0,), (0,)))
_BNT = (((2,), (2,)), ((0,), (0,)))
_BTN = (((1,), (1,)), ((0,), (0,)))
ATTN_GROUP = 4
ATTN_ITEMS = PAD_UNIT // BAND
Q_COL, K_COL, V_COL = 0, 4, 8


def _attn_item_rows(j, d, c, cbase):
    r = lax.rem(j, d)
    b = lax.div(j, d)
    loc = b * (d * BAND) + r
    first = jnp.logical_and(c == 0, b == 0)
    start = cbase + loc
    pstart = jnp.where(first, start, start - d * BAND)
    return loc, start, pstart, first


def _attn_fwd(proj, bias, shards):
    s = proj.shape[0]
    rides = [_ChipGather(sh) for sh in shards]

    def body(q_ref, k_ref, v_ref, b_ref, y_ref, lse_ref, o_s, l_s):
        c = pl.program_id(1)
        cbase = pl.multiple_of(c * PAD_UNIT, PAD_UNIT)
        m0, in_window, cur_half = _attn_masks()
        for bi, d in enumerate(DILATIONS):
            def group(jg, carry, bi=bi, d=d):
                locs, qs, ks, vs, pens = [], [], [], [], []
                for t in range(ATTN_GROUP):
                    loc, start, pstart, first = _attn_item_rows(jg * ATTN_GROUP + t, d, c, cbase)
                    locs.append(loc)
                    qs.append(q_ref[pl.ds(loc, BAND, stride=d), :])
                    ks.append(jnp.concatenate([k_ref[pl.ds(pstart, BAND, stride=d), :],
                                               k_ref[pl.ds(start, BAND, stride=d), :]], axis=0))
                    vs.append(jnp.concatenate([v_ref[pl.ds(pstart, BAND, stride=d), :],
                                               v_ref[pl.ds(start, BAND, stride=d), :]], axis=0))
                    pens.append(jnp.where(cur_half, 0.0, jnp.where(first, NEG_INF, 0.0)))
                q = jnp.stack(qs)
                kk = jnp.stack(ks + ks).astype(bf16)
                vv = jnp.stack(vs + vs).astype(bf16)
                pen = jnp.stack(pens + pens)
                qh = (jnp.concatenate([jnp.where(m0, q, 0.0), jnp.where(m0, 0.0, q)], axis=0) * 0.125).astype(bf16)
                sc = lax.dot_general(qh, kk, _BNT, preferred_element_type=f32)
                sc = (sc.reshape(2, ATTN_GROUP, BAND, 2 * BAND) + b_ref[bi][:, None]).reshape(sc.shape) + pen
                mx = jnp.max(sc, axis=-1, keepdims=True)
                e = jnp.exp(sc - mx)
                l = jnp.sum(e, axis=-1, keepdims=True)
                o = lax.dot_general(e.astype(bf16), vv, _BNN, preferred_element_type=f32) * (1.0 / l)
                ls = mx + jnp.log(l)
                for t in range(ATTN_GROUP):
                    rows = pl.ds(locs[t], BAND, stride=d)
                    o_s[bi, rows, :] = jnp.where(m0, o[t], o[ATTN_GROUP + t])
                    l_s[bi, rows, :] = jnp.where(m0, ls[t], ls[ATTN_GROUP + t])
                return carry

            lax.fori_loop(0, ATTN_ITEMS // ATTN_GROUP, group, 0)

        def merge(t, carry):
            rows = pl.ds(pl.multiple_of(t * 256, 256), 256)
            ls = [l_s[i, rows, :] for i in range(3)]
            mx = jnp.maximum(jnp.maximum(ls[0], ls[1]), ls[2])
            ws = [jnp.exp(l - mx) for l in ls]
            tot = ws[0] + ws[1] + ws[2]
            y = (ws[0] * o_s[0, rows, :] + ws[1] * o_s[1, rows, :] + ws[2] * o_s[2, rows, :]) / tot
            y_ref[rows, :] = y
            lse_ref[rows, :] = mx + jnp.log(tot)
            return carry

        lax.fori_loop(0, PAD_UNIT // 256, merge, 0)

    chunk = lambda col: pl.BlockSpec((PAD_UNIT, 128), lambda p, c: (c, col + p))
    full = lambda col: pl.BlockSpec((s, 128), lambda p, c: (0, col + p))
    grid = (N_HEADS // 2, s // PAD_UNIT)
    nsteps = grid[0] * grid[1]
    out = pl.pallas_call(
        _ride_gathers(body, 4, 2, rides, grid, (3 * nsteps) // 4), name="attn_fwd", grid=grid,
        in_specs=[chunk(Q_COL), full(K_COL), full(V_COL),
                  pl.BlockSpec((3, 2, BAND, 2 * BAND), lambda p, c: (0, p, 0, 0))] + [_ANY] * len(rides),
        out_specs=[chunk(0), chunk(0)] + [_ANY] * len(rides),
        out_shape=[jax.ShapeDtypeStruct((s, GROUP_W), f32)] * 2 + [r.out_shape() for r in rides],
        scratch_shapes=[pltpu.VMEM((3, PAD_UNIT, 128), f32)] * 2 + [t for r in rides for t in r.scratch()],
        compiler_params=_cparams(("arbitrary", "arbitrary")),
    )(proj, proj, proj, bias, *shards)
    return out[:2], out[2:]


def _attn_bwd(proj, bias, y, lse, dycat):
    s = proj.shape[0]

    def body(q_ref, k_ref, v_ref, b_ref, y_ref, lse_ref, dy_ref, dq_ref, dk_ref, dv_ref, db_ref, dd_s):
        c = pl.program_id(1)
        cbase = pl.multiple_of(c * PAD_UNIT, PAD_UNIT)
        m0, in_window, cur_half = _attn_masks()

        @pl.when(c == 0)
        def _():
            dk_ref[...] = jnp.zeros_like(dk_ref)
            dv_ref[...] = jnp.zeros_like(dv_ref)
            db_ref[...] = jnp.zeros_like(db_ref)

        dq_ref[...] = jnp.zeros_like(dq_ref)

        def rowdot(t, carry):
            rows = pl.ds(pl.multiple_of(t * 256, 256), 256)
            prod = dy_ref[rows, :] * y_ref[rows, :]
            lane = lax.broadcasted_iota(jnp.int32, prod.shape, 1)
            h0 = lane < HEAD_DIM
            d0 = jnp.sum(jnp.where(h0, prod, 0.0), axis=-1, keepdims=True)
            d1 = jnp.sum(jnp.where(h0, 0.0, prod), axis=-1, keepdims=True)
            dd_s[rows, :] = jnp.where(h0, d0, d1)
            return carry

        lax.fori_loop(0, PAD_UNIT // 256, rowdot, 0)

        for bi, d in enumerate(DILATIONS):
            def group(jg, carry, bi=bi, d=d):
                ng = ATTN_GROUP
                meta, qs, dos, lqs, dds, ks, vs, pens = [], [], [], [], [], [], [], []
                for t in range(ng):
                    loc, start, pstart, first = _attn_item_rows(jg * ng + t, d, c, cbase)
                    qrows = pl.ds(loc, BAND, stride=d)
                    rows = pl.ds(start, BAND, stride=d)
                    prows = pl.ds(pstart, BAND, stride=d)
                    meta.append((qrows, rows, prows))
                    qs.append(q_ref[qrows, :])
                    dos.append(dy_ref[qrows, :])
                    lqs.append(lse_ref[qrows, :])
                    dds.append(dd_s[qrows, :])
                    ks.append(jnp.concatenate([k_ref[prows, :], k_ref[rows, :]], axis=0))
                    vs.append(jnp.concatenate([v_ref[prows, :], v_ref[rows, :]], axis=0))
                    pens.append(jnp.where(cur_half, 0.0, jnp.where(first, NEG_INF, 0.0)))

                def heads(t):
                    return jnp.concatenate([jnp.where(m0, t, 0.0), jnp.where(m0, 0.0, t)], axis=0)

                def head_col(t):
                    return jnp.concatenate([t[:, :, 0:1], t[:, :, HEAD_DIM:HEAD_DIM + 1]], axis=0)

                qh = (heads(jnp.stack(qs)) * 0.125).astype(bf16)
                doh = heads(jnp.stack(dos)).astype(bf16)
                kk = jnp.stack(ks + ks).astype(bf16)
                vv = jnp.stack(vs + vs).astype(bf16)
                sc = lax.dot_general(qh, kk, _BNT, preferred_element_type=f32)
                sc = (sc.reshape(2, ng, BAND, 2 * BAND) + b_ref[bi][:, None]).reshape(sc.shape) + jnp.stack(pens + pens)
                p = jnp.exp(sc - head_col(jnp.stack(lqs)))
                dp = lax.dot_general(doh, vv, _BNT, preferred_element_type=f32)
                ds = p * (dp - head_col(jnp.stack(dds)))
                db_ref[bi] += jnp.sum(ds.reshape(2, ng, BAND, 2 * BAND), axis=1)
                dsb = ds.astype(bf16)
                dq = lax.dot_general(dsb, kk, _BNN, preferred_element_type=f32) * 0.125
                dk = lax.dot_general(dsb, qh, _BTN, preferred_element_type=f32)
                dv = lax.dot_general(p.astype(bf16), doh, _BTN, preferred_element_type=f32)
                for t in range(ng):
                    qrows, rows, prows = meta[t]
                    dq_ref[qrows, :] += jnp.where(m0, dq[t], dq[ng + t])
                    dkt = dk[t] + dk[ng + t]
                    dvt = dv[t] + dv[ng + t]
                    dk_ref[prows, :] += dkt[:BAND]
                    dk_ref[rows, :] += dkt[BAND:]
                    dv_ref[prows, :] += dvt[:BAND]
                    dv_ref[rows, :] += dvt[BAND:]
                return carry

            lax.fori_loop(0, ATTN_ITEMS // ATTN_GROUP, group, 0)

    chunk = lambda col: pl.BlockSpec((PAD_UNIT, 128), lambda p, c: (c, col + p))
    full = lambda col: pl.BlockSpec((s, 128), lambda p, c: (0, col + p))
    bias_spec = pl.BlockSpec((3, 2, BAND, 2 * BAND), lambda p, c: (0, p, 0, 0))
    return pl.pallas_call(
        body, name="attn_bwd", grid=(N_HEADS // 2, s // PAD_UNIT),
        in_specs=[chunk(Q_COL), full(K_COL), full(V_COL), bias_spec, chunk(0), chunk(0), chunk(0)],
        out_specs=[chunk(0), full(0), full(0), bias_spec],
        out_shape=[jax.ShapeDtypeStruct((s, GROUP_W), f32)] * 3
        + [jax.ShapeDtypeStruct((3, N_HEADS, BAND, 2 * BAND), f32)],
        scratch_shapes=[pltpu.VMEM((PAD_UNIT, 128), f32)],
        compiler_params=_cparams(("parallel", "arbitrary")),
    )(proj, proj, proj, bias, y, lse, dycat)


_HI = lax.Precision.HIGHEST
DELTA_COL = 1536
Z_COL = 3072
BA_BLOCK = 28
DELTA_ROWS = 1024


def _hdot(a, b):
    return jnp.dot(a, b, precision=_HI, preferred_element_type=f32)


_DIMS = dict(nn=(((2,), (1,)), ((0,), (0,))), nt=(((2,), (2,)), ((0,), (0,))), tn=(((1,), (1,)), ((0,), (0,))))


@functools.partial(jax.custom_vjp, nondiff_argnums=(2,))
def _mmx(a, b, mode):
    return lax.dot_general(a.astype(bf16), b.astype(bf16), _DIMS[mode], preferred_element_type=f32)


def _mmx_fwd(a, b, mode):
    return _mmx(a, b, mode), (a, b)


def _mmx_bwd(mode, res, g):
    a, b = res
    if mode == "nn":
        return _mmx(g, b, "nt"), _mmx(a, g, "tn")
    if mode == "nt":
        return _mmx(g, b, "nn"), _mmx(g, a, "tn")
    return _mmx(b, g, "nt"), _mmx(a, g, "nn")


_mmx.defvjp(_mmx_fwd, _mmx_bwd)


def _pair_iota():
    row = lax.broadcasted_iota(jnp.int32, (CHUNK, 128), 0)
    lane = lax.broadcasted_iota(jnp.int32, (CHUNK, 128), 1)
    return row, lane, lane & (CHUNK - 1)


def _bd(x):
    _, lane, _ = _pair_iota()
    m0 = lane < CHUNK
    return jnp.concatenate([jnp.where(m0, x, 0.0), jnp.where(m0, 0.0, x)], axis=1)


def _pmm(a, b):
    return _mmx(a, _bd(b), "nn")


def _ntp(x, y):
    return _mmx(x, _bd(y), "nt")


def _tnp(x, y):
    full = _mmx(x, y, "tn")
    _, lane, _ = _pair_iota()
    return jnp.where(lane < CHUNK, full[:, :CHUNK], full[:, CHUNK:])


def _tri_inv(a):
    row, lane, jj = _pair_iota()
    eye = jnp.where(row == jj, 1.0, 0.0).astype(f32)

    def same_block(log2b):
        return (row >> log2b) == (jj >> log2b)

    dgl = jnp.where(same_block(3), a, 0.0)
    d2 = _pmm(dgl, dgl)
    d4 = _pmm(d2, d2)
    t = _pmm(_pmm(eye - dgl, eye + d2), eye + d4)
    for lb in (3, 4, 5):
        off = jnp.where(same_block(lb + 1) & jnp.logical_not(same_block(lb)), a, 0.0)
        t = t - _pmm(_pmm(t, off), t)
    return t


@jax.custom_vjp
def _solve2(a, xv, xk, t):
    return _pmm(t, xv), _pmm(t, xk)


def _solve2_fwd(a, xv, xk, t):
    u, w = _pmm(t, xv), _pmm(t, xk)
    return (u, w), (t, u, w)


def _solve2_bwd(res, cts):
    t, u, w = res
    du, dw = cts
    dxv = _tnp(t, du)
    dxk = _tnp(t, dw)
    return -(_ntp(dxv, u) + _ntp(dxk, w)), dxv, dxk, jnp.zeros_like(t)


_solve2.defvjp(_solve2_fwd, _solve2_bwd)


def _chunk_pre(qp, kp, vp, bp, gcum, t=None):
    row, lane, jj = _pair_iota()
    causal = row >= jj
    strict = row > jj
    rsel = jnp.sum(jnp.where(row == jj, gcum, 0.0), axis=1, keepdims=True)
    decay = jnp.where(causal, jnp.exp(jnp.where(causal, gcum - rsel, 0.0)), 0.0)
    kb = kp * bp
    kd = _bd(kp)
    a = jnp.where(strict, _mmx(kb, kd, "nt") * decay, 0.0)
    eg = jnp.exp(gcum)
    if t is None:
        t = _tri_inv(a)
    u, w = _solve2(a, vp * bp, kb * eg, t)
    qk = jnp.where(causal, _mmx(qp, kd, "nt") * decay, 0.0)
    glast = jnp.sum(jnp.where(row == CHUNK - 1, gcum, 0.0), axis=1, keepdims=True)
    return u, w, qp * eg, kp * jnp.exp(glast - gcum), qk, jnp.exp(glast), t


def _chunk_post(u, w, qt, kh, qk, gam, sp):
    sd = _bd(sp)
    vnew = u - _mmx(w, sd, "nn")
    o = _mmx(qt, sd, "nn") + _pmm(qk, vnew)
    return o, gam * sp + _tnp(kh, vnew)


def _pair_spec(rows=DELTA_ROWS):
    return pl.BlockSpec((rows, 128), lambda i, p: (i, p))


DELTA_NB = DELTA_ROWS // CHUNK


def _chunks(ref):
    return ref[...].reshape(DELTA_NB, CHUNK, 128)


def _pairs(ref, rows):
    return jnp.stack([ref[rows, p * 128:(p + 1) * 128] for p in range(4)], axis=0)


def _delta_chunk_pre(qn, kn, sv, beta, g, xchg):
    s = qn.shape[0]

    def body(q_ref, k_ref, v_ref, b_ref, g_ref, u_ref, w_ref, qt_ref, kh_ref, qk_ref, t_ref, gm_ref):
        outs = _chunk_pre(_chunks(q_ref), _chunks(k_ref), _chunks(v_ref), _chunks(b_ref), _chunks(g_ref))
        for ref, val in zip((u_ref, w_ref, qt_ref, kh_ref, qk_ref, t_ref), outs[:5] + outs[6:]):
            ref[...] = val.reshape(DELTA_ROWS, 128).astype(ref.dtype)
        gm_ref[...] = jnp.broadcast_to(outs[5], (DELTA_NB, 8, 128)).reshape(DELTA_NB * 8, 128)

    v_spec = pl.BlockSpec((DELTA_ROWS, 128), lambda i, p: (i, 8 + p))
    grid = (s // DELTA_ROWS, 4)
    out = pl.pallas_call(
        _ride(body, 5, 7, xchg, grid), name="delta_chunk_pre", grid=grid,
        in_specs=[_pair_spec(), _pair_spec(), v_spec, _pair_spec(), _pair_spec()] + [_ANY] * xchg.n,
        out_specs=[_pair_spec()] * 6 + [_pair_spec(DELTA_NB * 8)] + [_ANY] * xchg.n,
        out_shape=[jax.ShapeDtypeStruct((s, GROUP_W), f32)] + [jax.ShapeDtypeStruct((s, GROUP_W), bf16)] * 5
        + [jax.ShapeDtypeStruct((s // 8, GROUP_W), f32)] + xchg.out_shape(),
        scratch_shapes=xchg.scratch(),
        compiler_params=_cparams(("arbitrary", "arbitrary")),
    )(qn, kn, sv, beta, g, *xchg.arrs)
    return out[:7], out[7:]


def _delta_scan_fwd(u, w, qt, kh, qk, gm):
    s = u.shape[0]

    def body(u_ref, w_ref, qt_ref, kh_ref, qk_ref, gm_ref, o_ref, ss_ref, st):
        @pl.when(pl.program_id(0) == 0)
        def _():
            st[...] = jnp.zeros_like(st)

        def chunk(ci, carry):
            rows = pl.ds(pl.multiple_of(ci * CHUNK, CHUNK), CHUNK)
            grow = pl.ds(pl.multiple_of(ci * 8, 8), 1)
            sp = st[...]
            o, s2 = _chunk_post(_pairs(u_ref, rows), _pairs(w_ref, rows), _pairs(qt_ref, rows),
                                _pairs(kh_ref, rows), _pairs(qk_ref, rows), _pairs(gm_ref, grow), sp)
            for p in range(4):
                ss_ref[rows, p * 128:(p + 1) * 128] = sp[p]
                o_ref[rows, p * 128:(p + 1) * 128] = o[p]
            st[...] = s2
            return carry

        lax.fori_loop(0, DELTA_NB, chunk, 0)

    spec = pl.BlockSpec((DELTA_ROWS, GROUP_W), lambda i: (i, 0))
    gspec = pl.BlockSpec((DELTA_NB * 8, GROUP_W), lambda i: (i, 0))
    return pl.pallas_call(
        body, name="delta_scan_fwd", grid=(s // DELTA_ROWS,),
        in_specs=[spec] * 5 + [gspec],
        out_specs=[spec, spec],
        out_shape=[jax.ShapeDtypeStruct((s, GROUP_W), f32)] * 2,
        scratch_shapes=[pltpu.VMEM((4, CHUNK, 128), f32)],
        compiler_params=_cparams(("arbitrary",)),
    )(u, w, qt, kh, qk, gm)


def _delta_scan_bwd(w, qt, kh, qk, gm, do, xchg):
    s = w.shape[0]
    nb = s // DELTA_ROWS

    def body(w_ref, qt_ref, kh_ref, qk_ref, gm_ref, do_ref, dso_ref, dst):
        @pl.when(pl.program_id(0) == 0)
        def _():
            dst[...] = jnp.zeros_like(dst)

        def chunk(t, carry):
            ci = DELTA_NB - 1 - t
            rows = pl.ds(pl.multiple_of(ci * CHUNK, CHUNK), CHUNK)
            grow = pl.ds(pl.multiple_of(ci * 8, 8), 1)
            ds = dst[...]
            for p in range(4):
                dso_ref[rows, p * 128:(p + 1) * 128] = ds[p]
            do = _pairs(do_ref, rows)
            dvn = _tnp(_pairs(qk_ref, rows), do) + _pmm(_pairs(kh_ref, rows), ds)
            dst[...] = _tnp(_pairs(qt_ref, rows), do) + _pairs(gm_ref, grow) * ds - _tnp(_pairs(w_ref, rows), dvn)
            return carry

        lax.fori_loop(0, DELTA_NB, chunk, 0)

    spec = pl.BlockSpec((DELTA_ROWS, GROUP_W), lambda i: (nb - 1 - i, 0))
    gspec = pl.BlockSpec((DELTA_NB * 8, GROUP_W), lambda i: (nb - 1 - i, 0))
    out = pl.pallas_call(
        _ride(body, 6, 1, xchg, (nb,)), name="delta_scan_bwd", grid=(nb,),
        in_specs=[spec] * 4 + [gspec, spec] + [_ANY] * xchg.n,
        out_specs=[spec] + [_ANY] * xchg.n,
        out_shape=[jax.ShapeDtypeStruct((s, GROUP_W), f32)] + xchg.out_shape(),
        scratch_shapes=[pltpu.VMEM((4, CHUNK, 128), f32)] + xchg.scratch(),
        compiler_params=_cparams(("arbitrary",)),
    )(w, qt, kh, qk, gm, do, *xchg.arrs)
    return out[0], out[1:]


def _delta_chunk_bwd(qn, kn, sv, beta, g, tinv, ss, dso, do, xchg):
    s = qn.shape[0]

    def body(q_ref, k_ref, v_ref, b_ref, g_ref, t_ref, ss_ref, dso_ref, do_ref,
             dq_ref, dk_ref, dv_ref, db_ref, dg_ref):
        sp = _chunks(ss_ref)
        t = _chunks(t_ref)

        def fn(q, k, v, b, gg):
            return _chunk_post(*_chunk_pre(q, k, v, b, gg, t)[:6], sp)

        _, vjp = jax.vjp(fn, _chunks(q_ref), _chunks(k_ref), _chunks(v_ref), _chunks(b_ref), _chunks(g_ref))
        grads = vjp((_chunks(do_ref), _chunks(dso_ref)))
        for ref, val in zip((dq_ref, dk_ref, dv_ref, db_ref, dg_ref), grads):
            ref[...] = val.reshape(DELTA_ROWS, 128)

    v_spec = pl.BlockSpec((DELTA_ROWS, 128), lambda i, p: (i, 8 + p))
    grid = (s // DELTA_ROWS, 4)
    out = pl.pallas_call(
        _ride(body, 9, 5, xchg, grid), name="delta_chunk_bwd", grid=grid,
        in_specs=[_pair_spec(), _pair_spec(), v_spec] + [_pair_spec()] * 6 + [_ANY] * xchg.n,
        out_specs=[_pair_spec()] * 5 + [_ANY] * xchg.n,
        out_shape=[jax.ShapeDtypeStruct((s, GROUP_W), f32)] * 5 + xchg.out_shape(),
        scratch_shapes=xchg.scratch(),
        compiler_params=_cparams(("arbitrary", "arbitrary")),
    )(qn, kn, sv, beta, g, tinv, ss, dso, do, *xchg.arrs)
    return out[:5], out[5:]


def _head_sums(x):
    r = lax.broadcasted_iota(jnp.int32, (128, 128), 0)
    c = lax.broadcasted_iota(jnp.int32, (128, 128), 1)
    pair = jnp.where((r >> 6) == (c >> 6), 1.0, 0.0).astype(f32)
    npair = x.shape[1] // 128
    xb = jnp.concatenate([x[None, :, p * 128:(p + 1) * 128] for p in range(npair)], axis=0)
    sums = _mmx(xb, jnp.broadcast_to(pair, (npair, 128, 128)), "nn")
    return jnp.concatenate([sums[p] for p in range(npair)], axis=1)


def _sel_dot(a, b):
    return jnp.dot(a, b, precision=lax.Precision.HIGH, preferred_element_type=f32)


def _expand_matrix(first):
    r = lax.broadcasted_iota(jnp.int32, (128, GROUP_W), 0)
    c = lax.broadcasted_iota(jnp.int32, (128, GROUP_W), 1) >> 6
    return jnp.where(r == c + first, 1.0, 0.0).astype(f32)


@functools.partial(jax.custom_vjp, nondiff_argnums=(1,))
def _expand_heads(ba, first):
    return _sel_dot(ba, _expand_matrix(first))


def _expand_heads_fwd(ba, first):
    return _expand_heads(ba, first), None


def _expand_heads_bwd(first, _, g):
    return (_mmx(g[None], _expand_matrix(first)[None], "nt")[0],)


_expand_heads.defvjp(_expand_heads_fwd, _expand_heads_bwd)


def _softplus(x):
    return jnp.maximum(x, 0.0) + jnp.log(1.0 + jnp.exp(-jnp.abs(x)))


def _prep_fn(sq, sk, ba, alog_e, dt_e):
    qn = sq * lax.rsqrt(_head_sums(sq * sq) + EPS) * (HEAD_DIM ** -0.5)
    kn = sk * lax.rsqrt(_head_sums(sk * sk) + EPS)
    bl = _expand_heads(ba, 0)
    al = _expand_heads(ba, N_HEADS)
    beta = jax.nn.sigmoid(bl)
    g = -jnp.exp(alog_e) * _softplus(al + dt_e)
    nchunk = g.shape[0] // CHUNK
    ri = lax.broadcasted_iota(jnp.int32, (nchunk, CHUNK, CHUNK), 1)
    ci = lax.broadcasted_iota(jnp.int32, (nchunk, CHUNK, CHUNK), 2)
    tril = jnp.where(ri >= ci, 1.0, 0.0).astype(f32)
    gcum = lax.dot_general(tril, g.reshape(nchunk, CHUNK, g.shape[1]), _BNN, precision=lax.Precision.HIGH,
                           preferred_element_type=f32)
    return qn, kn, beta, gcum.reshape(g.shape)


def _gnorm_fn(o, z, ng_e):
    ms = _head_sums(o * o) * (1.0 / HEAD_DIM)
    return o * lax.rsqrt(ms + EPS) * ng_e * (z * jax.nn.sigmoid(z))


def _tok_spec(width, col):
    return pl.BlockSpec((TOK_TILE, width), lambda i: (i, col))


def _conv_taps(xs_ref, w_ref, base, n, cols):
    acc = w_ref[CONV_WIDTH - 1:CONV_WIDTH, cols] * xs_ref[pl.ds(base, n), cols]
    for j in range(CONV_WIDTH - 1):
        acc = acc + w_ref[j:j + 1, cols] * xs_ref[pl.ds(base - (CONV_WIDTH - 1) + j, n), cols]
    return acc


def _conv_silu_fwd(proj, conv_w):
    s = proj.shape[0]
    wd = 3 * GROUP_W
    hb = TOK_TILE // 8

    def body(x_ref, halo_ref, w_ref, o_ref, y_ref, xs):
        inner = pl.program_id(0) > 0

        def lane_block(cb, carry):
            cols = pl.ds(pl.multiple_of(cb * 128, 128), 128)
            xs[0:8, cols] = jnp.where(inner, halo_ref[:, cols], 0.0)
            xs[8:, cols] = x_ref[:, cols]
            y = _conv_taps(xs, w_ref, 8, TOK_TILE, cols)
            y_ref[:, cols] = y
            o_ref[:, cols] = y * jax.nn.sigmoid(y)
            return carry

        lax.fori_loop(0, wd // 128, lane_block, 0)

    return pl.pallas_call(
        body, name="delta_conv_fwd", grid=(s // TOK_TILE,),
        in_specs=[_tok_spec(wd, 1), pl.BlockSpec((8, wd), lambda i: (jnp.maximum(i * hb - 1, 0), 1)),
                  pl.BlockSpec((CONV_WIDTH, wd), lambda i: (0, 0))],
        out_specs=[_tok_spec(wd, 0)] * 2,
        out_shape=[jax.ShapeDtypeStruct((s, wd), f32)] * 2,
        scratch_shapes=[pltpu.VMEM((TOK_TILE + 8, wd), f32)],
        compiler_params=_cparams(("parallel",)),
    )(proj, proj, conv_w)


def _conv_silu_bwd(proj, conv_w, yc, ds3, xchg):
    s = proj.shape[0]
    wd = 3 * GROUP_W
    hb = TOK_TILE // 8
    nt = s // TOK_TILE

    def body(x_ref, hp_ref, y_ref, yn_ref, dq_ref, dk_ref, dv_ref, dqn_ref, dkn_ref, dvn_ref, w_ref,
             dx_ref, dw_ref, xs, dys):
        i = pl.program_id(0)

        @pl.when(i == 0)
        def _():
            dw_ref[...] = jnp.zeros_like(dw_ref)

        last = i == nt - 1
        def lane_block(lb, carry, third, cur, nxt):
            tcols = pl.ds(pl.multiple_of(lb * 128, 128), 128)
            cols = pl.ds(pl.multiple_of(third * GROUP_W + lb * 128, 128), 128)
            xs[0:8, cols] = jnp.where(i > 0, hp_ref[:, cols], 0.0)
            xs[8:, cols] = x_ref[:, cols]
            y = y_ref[:, cols]
            sg = jax.nn.sigmoid(y)
            dy0 = cur[:, tcols] * (sg * (1.0 + y * (1.0 - sg)))
            dys[0:TOK_TILE, cols] = dy0
            yn = yn_ref[:, cols]
            sgn = jax.nn.sigmoid(yn)
            dys[TOK_TILE:, cols] = jnp.where(last, 0.0, nxt[:, tcols]) * (sgn * (1.0 + yn * (1.0 - sgn)))
            dx = w_ref[CONV_WIDTH - 1:CONV_WIDTH, cols] * dy0
            for j in range(CONV_WIDTH - 1):
                dx = dx + w_ref[j:j + 1, cols] * dys[pl.ds(CONV_WIDTH - 1 - j, TOK_TILE), cols]
            dx_ref[:, cols] = dx.astype(dx_ref.dtype)
            for j in range(CONV_WIDTH):
                dw_ref[j:j + 1, cols] += jnp.sum(dy0 * xs[pl.ds(8 - (CONV_WIDTH - 1) + j, TOK_TILE), cols],
                                                 axis=0, keepdims=True)
            return carry

        for third, (cur, nxt) in enumerate(((dq_ref, dqn_ref), (dk_ref, dkn_ref), (dv_ref, dvn_ref))):
            lax.fori_loop(0, GROUP_W // 128, functools.partial(lane_block, third=third, cur=cur, nxt=nxt), 0)

    prev8 = lambda col: pl.BlockSpec((8, wd), lambda i: (jnp.maximum(i * hb - 1, 0), col))
    next8 = lambda col: pl.BlockSpec((8, wd), lambda i: (jnp.minimum((i + 1) * hb, s // 8 - 1), col))
    next8_third = pl.BlockSpec((8, GROUP_W), lambda i: (jnp.minimum((i + 1) * hb, s // 8 - 1), 0))
    out = pl.pallas_call(
        _ride(body, 11, 2, xchg, (nt,)), name="delta_conv_bwd", grid=(nt,),
        in_specs=[_tok_spec(wd, 1), prev8(1), _tok_spec(wd, 0), next8(0)] + [_tok_spec(GROUP_W, 0)] * 3
        + [next8_third] * 3
        + [pl.BlockSpec((CONV_WIDTH, wd), lambda i: (0, 0))] + [_ANY] * xchg.n,
        out_specs=[_tok_spec(wd, 0), pl.BlockSpec((CONV_WIDTH, wd), lambda i: (0, 0))] + [_ANY] * xchg.n,
        out_shape=[jax.ShapeDtypeStruct((s, wd), bf16), jax.ShapeDtypeStruct((CONV_WIDTH, wd), f32)] + xchg.out_shape(),
        scratch_shapes=[pltpu.VMEM((TOK_TILE + 8, wd), f32), pltpu.VMEM((TOK_TILE + 8, wd), f32)] + xchg.scratch(),
        compiler_params=_cparams(("arbitrary",)),
    )(proj, proj, yc, yc, *ds3, *ds3, conv_w, *xchg.arrs)
    return out[:2], out[2:]


def _delta_prep_fwd(sconv, proj, alog_e, dt_e):
    s = sconv.shape[0]

    def body(sq_ref, sk_ref, ba_ref, al_ref, dt_ref, q_ref, k_ref, b_ref, g_ref):
        qn, kn, beta, g = _prep_fn(sq_ref[...], sk_ref[...], ba_ref[...], al_ref[...], dt_ref[...])
        q_ref[...] = qn
        k_ref[...] = kn
        b_ref[...] = beta
        g_ref[...] = g

    return pl.pallas_call(
        body, name="delta_prep_fwd", grid=(s // TOK_TILE,),
        in_specs=[_tok_spec(GROUP_W, 0), _tok_spec(GROUP_W, 1), _tok_spec(128, BA_BLOCK),
                  _vec_spec(GROUP_W), _vec_spec(GROUP_W)],
        out_specs=[_tok_spec(GROUP_W, 0)] * 4,
        out_shape=[jax.ShapeDtypeStruct((s, GROUP_W), f32)] * 4,
        compiler_params=_cparams(("parallel",)),
    )(sconv, sconv, proj, alog_e, dt_e)


def _delta_prep_bwd(sconv, proj, alog_e, dt_e, dqn, dkn, dbeta, dg, xchg):
    s = sconv.shape[0]
    grid = (s // TOK_TILE,)

    def body(sq_ref, sk_ref, ba_ref, al_ref, dt_ref, dq_ref, dk_ref, db_ref, dg_ref,
             dsq_ref, dsk_ref, dba_ref, dal_ref, ddt_ref):
        @pl.when(pl.program_id(0) == 0)
        def _():
            dal_ref[...] = jnp.zeros_like(dal_ref)
            ddt_ref[...] = jnp.zeros_like(ddt_ref)

        _, vjp = jax.vjp(_prep_fn, sq_ref[...], sk_ref[...], ba_ref[...], al_ref[...], dt_ref[...])
        dsq, dsk, dba, dal, ddt = vjp((dq_ref[...], dk_ref[...], db_ref[...], dg_ref[...]))
        dsq_ref[...] = dsq
        dsk_ref[...] = dsk
        dba_ref[...] = dba.astype(bf16)
        dal_ref[...] += dal
        ddt_ref[...] += ddt

    out = pl.pallas_call(
        _ride(body, 9, 5, xchg, grid), name="delta_prep_bwd", grid=grid,
        in_specs=[_tok_spec(GROUP_W, 0), _tok_spec(GROUP_W, 1), _tok_spec(128, BA_BLOCK),
                  _vec_spec(GROUP_W), _vec_spec(GROUP_W)] + [_tok_spec(GROUP_W, 0)] * 4 + [_ANY] * xchg.n,
        out_specs=[_tok_spec(GROUP_W, 0), _tok_spec(GROUP_W, 0), _tok_spec(128, 0),
                   _acc_spec(GROUP_W), _acc_spec(GROUP_W)] + [_ANY] * xchg.n,
        out_shape=[jax.ShapeDtypeStruct((s, GROUP_W), f32)] * 2 + [jax.ShapeDtypeStruct((s, 128), bf16)]
        + [jax.ShapeDtypeStruct((1, GROUP_W), f32)] * 2 + xchg.out_shape(),
        scratch_shapes=xchg.scratch(),
        compiler_params=_cparams(("arbitrary",)),
    )(sconv, sconv, proj, alog_e, dt_e, dqn, dkn, dbeta, dg, *xchg.arrs)
    return out[:5], out[5:]


def _gnorm_fwd(o, proj, ng_e):
    s = o.shape[0]

    def body(o_ref, z_ref, g_ref, y_ref):
        y_ref[...] = _gnorm_fn(o_ref[...], z_ref[...], g_ref[...])

    return pl.pallas_call(
        body, name="delta_gnorm_fwd", grid=(s // TOK_TILE,),
        in_specs=[_tok_spec(GROUP_W, 0), _tok_spec(GROUP_W, Z_COL // GROUP_W), _vec_spec(GROUP_W)],
        out_specs=_tok_spec(GROUP_W, 0),
        out_shape=jax.ShapeDtypeStruct((s, GROUP_W), f32),
        compiler_params=_cparams(("parallel",)),
    )(o, proj, ng_e)


def _gnorm_bwd(o, proj, ng_e, dycat):
    s = o.shape[0]

    def body(o_ref, z_ref, g_ref, dy_ref, do_ref, dz_ref, dg_ref):
        @pl.when(pl.program_id(0) == 0)
        def _():
            dg_ref[...] = jnp.zeros_like(dg_ref)

        _, vjp = jax.vjp(_gnorm_fn, o_ref[...], z_ref[...], g_ref[...])
        do, dz, dg = vjp(dy_ref[...])
        do_ref[...] = do
        dz_ref[...] = dz.astype(bf16)
        dg_ref[...] += dg

    return pl.pallas_call(
        body, name="delta_gnorm_bwd", grid=(s // TOK_TILE,),
        in_specs=[_tok_spec(GROUP_W, 0), _tok_spec(GROUP_W, Z_COL // GROUP_W), _vec_spec(GROUP_W),
                  _tok_spec(GROUP_W, 1)],
        out_specs=[_tok_spec(GROUP_W, 0), _tok_spec(GROUP_W, 0), _acc_spec(GROUP_W)],
        out_shape=[jax.ShapeDtypeStruct((s, GROUP_W), f32), jax.ShapeDtypeStruct((s, GROUP_W), bf16),
                   jax.ShapeDtypeStruct((1, GROUP_W), f32)],
        compiler_params=_cparams(("arbitrary",)),
    )(o, proj, ng_e, dycat)


_MESH = pl.DeviceIdType.MESH
_ANY = pl.BlockSpec(memory_space=pl.ANY)
_VMEM = pl.BlockSpec(memory_space=pltpu.VMEM)


def _my_place():
    x, y, c = lax.axis_index("x"), lax.axis_index("y"), lax.axis_index("c")
    return x, y, c, 4 * x + 2 * y + c


def _peer(k, x, y, c):
    px = 1 - x if k & 4 else x
    py = 1 - y if k & 2 else y
    pc = 1 - c if k & 1 else c
    return (px, py, pc), 4 * px + 2 * py + pc


def _exchange_all(src_of_peer, dst_ref, send_sems, recv_sems, x, y, c, me):
    sent = []
    for k in range(1, N_DEV):
        dev, pidx = _peer(k, x, y, c)
        cp = pltpu.make_async_remote_copy(src_ref=src_of_peer(pidx), dst_ref=dst_ref.at[me],
                                          send_sem=send_sems.at[k - 1], recv_sem=recv_sems.at[k - 1],
                                          device_id=dev, device_id_type=_MESH)
        cp.start()
        sent.append(cp)
    for k in range(1, N_DEV):
        dev, pidx = _peer(k, x, y, c)
        pltpu.make_async_remote_copy(src_ref=src_of_peer(pidx), dst_ref=dst_ref.at[pidx],
                                     send_sem=send_sems.at[k - 1], recv_sem=recv_sems.at[k - 1],
                                     device_id=dev, device_id_type=_MESH).wait_recv()
    for cp in sent:
        cp.wait_send()


def _ada_exchange(cv8, w_ada, b_ada8):
    def body(cv_ref, w_ref, b_ref, call_ref, modp_ref, part_s, s1, r1, s2, r2):
        x, y, c, me = _my_place()
        call_ref[me] = cv_ref[...]
        _exchange_all(lambda pidx: cv_ref, call_ref, s1, r1, x, y, c, me)
        bias = b_ref[me]
        for j in range(N_DEV):
            cj = call_ref[j][:, :D_MODEL]
            part_s[j] = _hdot(cj * jax.nn.sigmoid(cj), w_ref[...]) + bias
        modp_ref[me] = part_s[me]
        _exchange_all(lambda pidx: part_s.at[pidx], modp_ref, s2, r2, x, y, c, me)

    nsh = w_ada.shape[1]
    return pl.pallas_call(
        body, name="ada_exchange",
        in_specs=[_VMEM, _VMEM, _VMEM], out_specs=[_VMEM, _VMEM],
        out_shape=[jax.ShapeDtypeStruct((N_DEV, 8, cv8.shape[1]), f32), jax.ShapeDtypeStruct((N_DEV, 8, nsh), f32)],
        scratch_shapes=[pltpu.VMEM((N_DEV, 8, nsh), f32)] + [pltpu.SemaphoreType.DMA((N_DEV - 1,))] * 4,
        compiler_params=pltpu.CompilerParams(vmem_limit_bytes=VMEM_LIMIT),
    )(cv8, w_ada, b_ada8)


def _all_to_all(arrs, name):
    ex = _Exchange(arrs, gather=False)

    def body(*refs):
        srcs, dsts, sems = refs[:ex.n], refs[ex.n:2 * ex.n], refs[2 * ex.n:]
        ex.start(srcs, dsts, sems)
        ex.wait(srcs, dsts, sems)

    return pl.pallas_call(
        body, name=name,
        in_specs=[_ANY] * ex.n, out_specs=[_ANY] * ex.n,
        out_shape=ex.out_shape(), scratch_shapes=ex.scratch(),
    )(*arrs)


class _Exchange:
    def __init__(self, arrs, gather):
        self.arrs, self.gather, self.n = list(arrs), gather, len(arrs)

    def out_shape(self):
        return [jax.ShapeDtypeStruct(((N_DEV,) + a.shape) if self.gather else a.shape, a.dtype) for a in self.arrs]

    def scratch(self):
        if self.n == 0:
            return []
        return [pltpu.SemaphoreType.DMA((self.n, N_DEV - 1)), pltpu.SemaphoreType.DMA((self.n, N_DEV - 1)),
                pltpu.SemaphoreType.DMA((self.n,))]

    def _src(self, srcs, a, idx):
        return srcs[a] if self.gather else srcs[a].at[idx]

    def _copies(self, srcs, dsts, sems, incoming):
        send_sems, recv_sems, _ = sems
        x, y, c, me = _my_place()
        out = []
        for a in range(self.n):
            for k in range(1, N_DEV):
                dev, pidx = _peer(k, x, y, c)
                out.append(pltpu.make_async_remote_copy(
                    src_ref=self._src(srcs, a, pidx), dst_ref=dsts[a].at[pidx if incoming else me],
                    send_sem=send_sems.at[a, k - 1], recv_sem=recv_sems.at[a, k - 1],
                    device_id=dev, device_id_type=_MESH))
        return out

    def _local(self, srcs, dsts, sems):
        me = _my_place()[3]
        return [pltpu.make_async_copy(self._src(srcs, a, me), dsts[a].at[me], sems[2].at[a]) for a in range(self.n)]

    def start(self, srcs, dsts, sems):
        for cp in self._local(srcs, dsts, sems) + self._copies(srcs, dsts, sems, incoming=False):
            cp.start()

    def wait(self, srcs, dsts, sems):
        for cp in self._copies(srcs, dsts, sems, incoming=True):
            cp.wait_recv()
        for cp in self._copies(srcs, dsts, sems, incoming=False):
            cp.wait_send()
        for cp in self._local(srcs, dsts, sems):
            cp.wait()

    def start_at_first_step(self, grid, srcs, dsts, sems):
        first = functools.reduce(jnp.logical_and, [pl.program_id(i) == 0 for i in range(len(grid))])
        pl.when(first)(lambda: self.start(srcs, dsts, sems))

    def wait_at_last_step(self, grid, srcs, dsts, sems):
        last = functools.reduce(jnp.logical_and, [pl.program_id(i) == g - 1 for i, g in enumerate(grid)])
        pl.when(last)(lambda: self.wait(srcs, dsts, sems))


class _ChipGather:
    def __init__(self, shard):
        self.shard = shard

    def out_shape(self):
        return jax.ShapeDtypeStruct((N_DEV,) + self.shard.shape, self.shard.dtype)

    def scratch(self):
        return [pltpu.SemaphoreType.DMA((N_DEV - 1,)), pltpu.SemaphoreType.DMA((N_DEV - 1,)),
                pltpu.SemaphoreType.DMA(())]

    def _place(self):
        x, y, c, me = _my_place()
        return x, y, c, me, (x, y, 1 - c), [(1 - x, y), (x, 1 - y), (1 - x, 1 - y)]

    def _copy(self, out, sems, k, block, to, src=None):
        rows = out.at[4 * block[0] + 2 * block[1] + block[2]]
        return pltpu.make_async_remote_copy(src_ref=rows if src is None else src, dst_ref=rows,
                                            send_sem=sems[0].at[k], recv_sem=sems[1].at[k],
                                            device_id=to, device_id_type=_MESH)

    def start(self, src, out, sems):
        x, y, c, me, sib, chips = self._place()
        pltpu.make_async_copy(src, out.at[me], sems[2]).start()
        self._copy(out, sems, 0, (x, y, c), sib, src=src).start()
        for j, chip in enumerate(chips):
            self._copy(out, sems, 1 + j, (x, y, c), (*chip, c), src=src).start()

    def forward(self, src, out, sems):
        x, y, c, me, sib, chips = self._place()
        for j, chip in enumerate(chips):
            self._copy(out, sems, 1 + j, (*chip, c), (x, y, c)).wait_recv()
            self._copy(out, sems, 4 + j, (*chip, c), sib).start()

    def finish(self, src, out, sems):
        x, y, c, me, sib, chips = self._place()
        self._copy(out, sems, 0, (x, y, 1 - c), (x, y, c)).wait_recv()
        for j, chip in enumerate(chips):
            self._copy(out, sems, 4 + j, (*chip, 1 - c), (x, y, c)).wait_recv()
        self._copy(out, sems, 0, (x, y, c), sib, src=src).wait_send()
        for j, chip in enumerate(chips):
            self._copy(out, sems, 1 + j, (x, y, c), (*chip, c), src=src).wait_send()
            self._copy(out, sems, 4 + j, (*chip, c), sib).wait_send()
        pltpu.make_async_copy(src, out.at[me], sems[2]).wait()


def _ride_gathers(body, n_in, n_out, rides, grid, forward_step):
    n = len(rides)
    sizes = list(grid)

    def wrapped(*refs):
        ins, xs = refs[:n_in], refs[n_in:n_in + n]
        outs, xd = refs[n_in + n:n_in + n + n_out], refs[n_in + n + n_out:n_in + 2 * n + n_out]
        scratch = refs[n_in + 2 * n + n_out:]
        own, sems = scratch[:len(scratch) - 3 * n], scratch[len(scratch) - 3 * n:]
        step = pl.program_id(0)
        for i in range(1, len(sizes)):
            step = step * sizes[i] + pl.program_id(i)

        def each(phase):
            for r in range(n):
                getattr(rides[r], phase)(xs[r], xd[r], sems[3 * r:3 * r + 3])

        pl.when(step == 0)(lambda: each("start"))
        body(*ins, *outs, *own)
        pl.when(step == forward_step)(lambda: each("forward"))
        pl.when(step == math.prod(sizes) - 1)(lambda: each("finish"))

    return wrapped


def _ride(body, n_in, n_out, xchg, grid):
    nx = xchg.n
    if nx == 0:
        return body

    def wrapped(*refs):
        ins, xs = refs[:n_in], refs[n_in:n_in + nx]
        outs, xd = refs[n_in + nx:n_in + nx + n_out], refs[n_in + nx + n_out:n_in + 2 * nx + n_out]
        scratch = refs[n_in + 2 * nx + n_out:]
        xchg.start_at_first_step(grid, xs, xd, scratch[-3:])
        body(*ins, *outs, *scratch[:-3])
        xchg.wait_at_last_step(grid, xs, xd, scratch[-3:])

    return wrapped


def _adamw_math(w, g, m, v):
    m2 = ADAM_B1 * m + (1.0 - ADAM_B1) * g
    v2 = ADAM_B2 * v + (1.0 - ADAM_B2) * (g * g)
    m_hat = m2 / (1.0 - ADAM_B1 ** ADAM_STEP)
    v_hat = v2 / (1.0 - ADAM_B2 ** ADAM_STEP)
    delta = -ADAM_LR * (m_hat / (jnp.sqrt(v_hat) + ADAM_EPS) + ADAM_WD * w)
    return delta, m2, v2


def _row_tile(rows):
    for t in (256, 128, 64, 32, 16, 8):
        if rows % t == 0:
            return t
    return rows


def _reduce_adamw(parts, w, m, v, name):
    _, r, cdim = parts.shape
    tr = _row_tile(r)

    def body(p_ref, w_ref, m_ref, v_ref, g_ref, d_ref, m2_ref, v2_ref):
        g = p_ref[0].astype(f32)
        for j in range(1, N_DEV):
            g = g + p_ref[j].astype(f32)
        delta, m2, v2 = _adamw_math(w_ref[...], g, m_ref[...], v_ref[...])
        g_ref[...] = g
        d_ref[...] = delta
        m2_ref[...] = m2
        v2_ref[...] = v2

    spec = pl.BlockSpec((tr, cdim), lambda i: (i, 0))
    return pl.pallas_call(
        body, name=name, grid=(r // tr,),
        in_specs=[pl.BlockSpec((N_DEV, tr, cdim), lambda i: (0, i, 0)), spec, spec, spec],
        out_specs=[spec] * 4,
        out_shape=[jax.ShapeDtypeStruct((r, cdim), f32)] * 4,
        compiler_params=_cparams(("parallel",)),
    )(parts, w, m, v)


def _adamw(w, g, m, v, name):
    r, cdim = w.shape
    tr = _row_tile(r)

    def body(w_ref, g_ref, m_ref, v_ref, d_ref, m2_ref, v2_ref):
        delta, m2, v2 = _adamw_math(w_ref[...], g_ref[...], m_ref[...], v_ref[...])
        d_ref[...] = delta
        m2_ref[...] = m2
        v2_ref[...] = v2

    spec = pl.BlockSpec((tr, cdim), lambda i: (i, 0))
    return pl.pallas_call(
        body, name=name, grid=(r // tr,),
        in_specs=[spec] * 4, out_specs=[spec] * 3,
        out_shape=[jax.ShapeDtypeStruct((r, cdim), f32)] * 3,
        compiler_params=_cparams(("parallel",)),
    )(w, g, m, v)


def _sum_devices(parts, name):
    _, r, cdim = parts.shape

    def body(p_ref, o_ref):
        g = p_ref[0]
        for j in range(1, N_DEV):
            g = g + p_ref[j]
        o_ref[...] = g

    return pl.pallas_call(
        body, name=name, out_shape=jax.ShapeDtypeStruct((r, cdim), f32),
        in_specs=[_VMEM], out_specs=_VMEM,
    )(parts)


def _ada_wgrad(c_all8, dmod_cols):
    nsh = dmod_cols.shape[1]

    def body(c_ref, d_ref, o_ref):
        cv = c_ref[...]
        o_ref[...] = lax.dot_general(cv * jax.nn.sigmoid(cv), d_ref[...], _TN, precision=_HI,
                                     preferred_element_type=f32)

    return pl.pallas_call(
        body, name="ada_wgrad", out_shape=jax.ShapeDtypeStruct((D_MODEL, nsh), f32),
        in_specs=[_VMEM, _VMEM], out_specs=_VMEM,
        compiler_params=pltpu.CompilerParams(vmem_limit_bytes=VMEM_LIMIT),
    )(c_all8, dmod_cols)


def _cols(t):
    return t.transpose(1, 0, 2).reshape(t.shape[1], N_DEV * t.shape[2])


def _col_blocks(t, n):
    return t.reshape(t.shape[0], N_DEV, n).transpose(1, 0, 2).astype(bf16)


def _row_blocks(t):
    return t.reshape(N_DEV, t.shape[0] // N_DEV, t.shape[1]).astype(bf16)


def _local_step(x, tgt, mod, norm_attn_g, w_in_sh, rel_bias, conv_full, a_log, dt_bias, delta_norm_g,
                norm_ffn_g, final_norm_g, w_out_sh, w_gate_sh, w_up_sh, w_down_sh):
    s = x.shape[0]
    sh1, sc1, g1, sh2, sc2, g2 = [mod[:, i * D_MODEL:(i + 1) * D_MODEL] for i in range(6)]
    nag = norm_attn_g.reshape(1, D_MODEL)
    nfg = norm_ffn_g.reshape(1, D_MODEL)
    fg = final_norm_g.reshape(1, D_MODEL)
    idx = _bucket_tables()
    bias = _bias_tables(rel_bias, idx)
    alog_e = jnp.repeat(a_log.reshape(N_HEADS), HEAD_DIM)[None]
    dt_e = jnp.repeat(dt_bias.reshape(N_HEADS), HEAD_DIM)[None]
    ng_e = jnp.tile(delta_norm_g.reshape(HEAD_DIM), N_HEADS)[None]

    h1, w_in_g = _ln_mod_fwd(x, nag, sc1, sh1, w_in_sh, "ln1_fwd")
    w_in_p = jnp.pad(_cols(w_in_g), ((0, 0), (0, IN_PAD - IN_WIDTH)))
    proj, (w_out_g,) = _mm(h1, w_in_p, "nn", f32, 512, IN_PAD, 1024, "in_proj",
                           xchg=_Exchange([w_out_sh], gather=True))
    (y_attn, lse), (w_gate_g, w_up_g, w_down_g) = _attn_fwd(proj, bias, [w_gate_sh, w_up_sh, w_down_sh])
    w_out_b = w_out_g.reshape(2 * GROUP_W, D_MODEL)
    w_down_b = w_down_g.reshape(D_FF, D_MODEL)
    w_gate_b, w_up_b = _cols(w_gate_g), _cols(w_up_g)
    n_ff = w_gate_sh.shape[1]
    sconv, yconv = _conv_silu_fwd(proj, conv_full)
    qn, kn, beta, g = _delta_prep_fwd(sconv, proj, alog_e, dt_e)
    (u, w, qt, kh, qk, tinv, gm), _ = _delta_chunk_pre(qn, kn, sconv, beta, g, _Exchange([], gather=False))
    o, ss = _delta_scan_fwd(u, w, qt, kh, qk, gm)
    y_delta = _gnorm_fwd(o, proj, ng_e)
    y, x1, h2 = _proj_resid_ln_mod_fwd([(y_attn, w_out_b[:GROUP_W]), (y_delta, w_out_b[GROUP_W:])],
                                       x, g1, nfg, sc2, sh2, "out_proj_ln2")
    act, gate, up = _ffn_up(h2, w_gate_b, w_up_b, "ffn_up")
    dx2, dy2, loss, dfg, dg2 = _proj_final_loss_bwd(act, w_down_b, x1, g2, fg, tgt, "ffn_down_loss")

    dgate, dup = _ffn_down_dx(dy2, w_down_b, gate, up, "ffn_down_dx")
    g_down = _mm(act, dy2, "tn", f32, 1408, 1024, 1024, "ffn_down_dw")
    (dx1, dsh2, dsc2, dnfg, dy, dg1), (r_down,) = _proj_ln_mod_bwd(
        [(dgate, w_gate_b), (dup, w_up_b)], x1, nfg, sc2, dx2, 256, "ffn_up_dx_ln2",
        _Exchange([_row_blocks(g_down)], gather=False), gate=g1, y=y)
    g_gate = _mm(h2, dgate, "tn", f32, 1024, 1408, 1024, "ffn_gate_dw")
    g_up = _mm(h2, dup, "tn", f32, 1024, 1408, 1024, "ffn_up_dw")
    dycat = _mm(dy, w_out_b, "nt", f32, 512, 1024, 1024, "out_proj_dx")
    g_out = jnp.concatenate([_mm(y_attn, dy, "tn", f32, GROUP_W, 1024, 1024, "out_proj_dw_attn"),
                             _mm(y_delta, dy, "tn", f32, GROUP_W, 1024, 1024, "out_proj_dw_delta")], axis=0)
    dq, dk, dv, dbias = _attn_bwd(proj, bias, y_attn, lse, dycat)
    g_rb = _bias_grad(dbias, idx)[:, :, 0].T
    do, dz, dng = _gnorm_bwd(o, proj, ng_e, dycat)
    dso, _ = _delta_scan_bwd(w, qt, kh, qk, gm, do, _Exchange([], gather=False))
    (dqn, dkn, dvd, dbeta, dgd), (r_up,) = _delta_chunk_bwd(
        qn, kn, sconv, beta, g, tinv, ss, dso, do, _Exchange([_col_blocks(g_up, n_ff)], gather=False))
    (dsq, dsk, dba, dal, ddt), _ = _delta_prep_bwd(
        sconv, proj, alog_e, dt_e, dqn, dkn, dbeta, dgd, _Exchange([], gather=False))
    (dxc, g_conv), (r_gate, r_out) = _conv_silu_bwd(
        proj, conv_full, yconv, (dsq, dsk, dvd),
        _Exchange([_col_blocks(g_gate, n_ff), _row_blocks(g_out)], gather=False))
    pieces = ((dq, 0), (dk, GROUP_W), (dv, 2 * GROUP_W), (dxc, DELTA_COL), (dz, Z_COL), (dba, BA_BLOCK * 128))
    g_in = jnp.concatenate(
        [_mm(h1, p, "tn", f32, 1024, min(p.shape[1], 768), 1024, "in_proj_dw_%d" % c) for p, c in pieces], axis=1)
    (gx, dsh1, dsc1, dnag), (r_in,) = _proj_ln_mod_bwd(
        [(p, w_in_p[:, c:c + p.shape[1]]) for p, c in pieces], x, nag, sc1, dx1, TOK_TILE, "in_proj_dx_ln1",
        _Exchange([_col_blocks(g_in[:, :IN_WIDTH], IN_WIDTH // N_DEV)], gather=False))
    grads = dict(
        x=gx, mod=jnp.concatenate([dsh1, dsc1, dg1, dsh2, dsc2, dg2], axis=1),
        norm_attn_g=dnag, norm_ffn_g=dnfg, final_norm_g=dfg, rel_bias=g_rb, conv_w=g_conv,
        a_log=dal.reshape(N_HEADS, HEAD_DIM).sum(-1), dt_bias=ddt.reshape(N_HEADS, HEAD_DIM).sum(-1),
        delta_norm_g=dng.reshape(N_HEADS, HEAD_DIM).sum(0),
        w_in=r_in, w_out=r_out, w_gate=r_gate, w_up=r_up, w_down=r_down)
    return loss[0, 0], grads


def _misc_row(rel_bias, a_log, dt_bias, delta_norm_g):
    flat = jnp.concatenate([rel_bias.reshape(-1), a_log.reshape(-1), dt_bias.reshape(-1), delta_norm_g.reshape(-1)])
    return jnp.pad(flat, (0, D_MODEL - flat.shape[0]))[None]


def _pack_small(b_ada, nag, nfg, fng, rel_bias, a_log, dt_bias, dng, conv_shard):
    rows = [b_ada.reshape(6, D_MODEL), nag.reshape(1, D_MODEL), nfg.reshape(1, D_MODEL), fng.reshape(1, D_MODEL),
            _misc_row(rel_bias, a_log, dt_bias, dng),
            jnp.pad(conv_shard.reshape(-1), (0, D_MODEL - conv_shard.size))[None],
            jnp.zeros((5, D_MODEL), f32)]
    return jnp.concatenate(rows, axis=0)


def _unpack_small(p, conv_shape):
    misc = p[9]
    return dict(
        b_ada=p[0:6].reshape(1, 6 * D_MODEL), norm_attn_g=p[6:7], norm_ffn_g=p[7:8], final_norm_g=p[8],
        rel_bias=misc[0:256].reshape(N_BUCKETS, N_HEADS), a_log=misc[256:264].reshape(1, N_HEADS),
        dt_bias=misc[264:272].reshape(1, N_HEADS), delta_norm_g=misc[272:336].reshape(1, HEAD_DIM),
        conv_w=p[10, :conv_shape[1] * conv_shape[2]].reshape(conv_shape))


def kernel(x, c, w_ada, b_ada, norm_attn_g, w_in, rel_bias, conv_w, a_log, dt_bias, delta_norm_g, w_out, norm_ffn_g, w_gate, w_up, w_down, final_norm_g, loss_target, m_w_ada, m_b_ada, m_norm_attn_g, m_w_in, m_rel_bias, m_conv_w, m_a_log, m_dt_bias, m_delta_norm_g, m_w_out, m_norm_ffn_g, m_w_gate, m_w_up, m_w_down, m_final_norm_g, v_w_ada, v_b_ada, v_norm_attn_g, v_w_in, v_rel_bias, v_conv_w, v_a_log, v_dt_bias, v_delta_norm_g, v_w_out, v_norm_ffn_g, v_w_gate, v_w_up, v_w_down, v_final_norm_g):
    me = 4 * lax.axis_index("x") + 2 * lax.axis_index("y") + lax.axis_index("c")
    ada_sh = w_ada.shape[2]
    conv_sh = conv_w.shape[2]

    cv = jnp.concatenate([c[0], conv_w[0].reshape(-1)])
    cv8 = jnp.zeros((8, 2 * D_MODEL), f32).at[0, :cv.shape[0]].set(cv)
    b8 = jnp.broadcast_to(b_ada.reshape(N_DEV, 1, ada_sh), (N_DEV, 8, ada_sh))
    call, modp = _ada_exchange(cv8, w_ada[0], b8)
    mod = modp[:, 0, :].reshape(1, 6 * D_MODEL)
    c_all = call[:, 0, :D_MODEL]
    conv_full = call[:, 0, D_MODEL:D_MODEL + CONV_WIDTH * conv_sh].reshape(N_DEV, CONV_WIDTH, conv_sh)
    conv_full = conv_full.transpose(1, 0, 2).reshape(CONV_WIDTH, N_DEV * conv_sh)

    loss_local, gr = _local_step(x[0], loss_target[0], mod, norm_attn_g, w_in[0].astype(bf16), rel_bias, conv_full, a_log,
                                 dt_bias, delta_norm_g, norm_ffn_g, final_norm_g, w_out[0].astype(bf16),
                                 w_gate[0].astype(bf16), w_up[0].astype(bf16), w_down[0].astype(bf16))
    loss = lax.psum(loss_local, ("x", "y", "c"))

    small = jnp.concatenate([
        gr["mod"].reshape(6, D_MODEL), gr["norm_attn_g"], gr["norm_ffn_g"], gr["final_norm_g"],
        gr["conv_w"].reshape(6, D_MODEL),
        _misc_row(gr["rel_bias"], gr["a_log"], gr["dt_bias"], gr["delta_norm_g"])], axis=0)
    parts = _all_to_all([jnp.broadcast_to(small[None], (N_DEV,) + small.shape)], "small_gather")[0]
    tot = _sum_devices(parts, "small_sum")
    g_conv_full = tot[9:15].reshape(CONV_WIDTH, N_DEV * conv_sh)
    g_conv = lax.dynamic_slice(g_conv_full, (0, me * conv_sh), (CONV_WIDTH, conv_sh))
    misc = tot[15]
    g_small = _pack_small(tot[0:6], tot[6], tot[7], tot[8], misc[0:256], misc[256:264], misc[264:272],
                          misc[272:336], g_conv)
    pk = lambda pre: _pack_small(pre[0], pre[1], pre[2], pre[3], pre[4], pre[5], pre[6], pre[7], pre[8])
    w_small = pk((b_ada, norm_attn_g, norm_ffn_g, final_norm_g, rel_bias, a_log, dt_bias, delta_norm_g, conv_w))
    m_small = pk((m_b_ada, m_norm_attn_g, m_norm_ffn_g, m_final_norm_g, m_rel_bias, m_a_log, m_dt_bias,
                  m_delta_norm_g, m_conv_w))
    v_small = pk((v_b_ada, v_norm_attn_g, v_norm_ffn_g, v_final_norm_g, v_rel_bias, v_a_log, v_dt_bias,
                  v_delta_norm_g, v_conv_w))
    d_small, m2_small, v2_small = _adamw(w_small, g_small, m_small, v_small, "adamw_small")
    cshape = conv_w.shape
    G, Dl, M2, V2 = (_unpack_small(t, cshape) for t in (g_small, d_small, m2_small, v2_small))

    dmod_all = parts[:, 0:6, :].reshape(N_DEV, 6 * D_MODEL)
    dmod_cols = lax.dynamic_slice(dmod_all, (0, me * ada_sh), (N_DEV, ada_sh))
    g_ada = _ada_wgrad(c_all, dmod_cols)
    d_ada, m2_ada, v2_ada = _adamw(w_ada[0], g_ada, m_w_ada[0], v_w_ada[0], "adamw_w_ada")

    big = {}
    for name, w_, m_, v_ in (("w_in", w_in, m_w_in, v_w_in), ("w_out", w_out, m_w_out, v_w_out),
                             ("w_gate", w_gate, m_w_gate, v_w_gate), ("w_up", w_up, m_w_up, v_w_up),
                             ("w_down", w_down, m_w_down, v_w_down)):
        big[name] = [t[None] for t in _reduce_adamw(gr[name], w_[0], m_[0], v_[0], "reduce_adamw_" + name)]

    def leaf(i, name):
        if name == "w_ada":
            return (g_ada, d_ada, m2_ada, v2_ada)[i][None]
        if name in big:
            return big[name][i]
        return (G, Dl, M2, V2)[i][name]

    order = ["w_ada", "b_ada", "norm_attn_g", "w_in", "rel_bias", "conv_w", "a_log", "dt_bias", "delta_norm_g",
             "w_out", "norm_ffn_g", "w_gate", "w_up", "w_down", "final_norm_g"]
    outs = [loss, gr["x"][None]]
    for i in range(4):
        outs += [leaf(i, n) for n in order]
    return tuple(outs)
```

```python
import functools
import math

import jax
import jax.numpy as jnp
from jax import lax
from jax.experimental import pallas as pl
from jax.experimental.pallas import tpu as pltpu

f32 = jnp.float32
bf16 = jnp.bfloat16

D_MODEL = 1024
HEAD_DIM = 64
N_HEADS = 8
GROUP_W = 512
IN_WIDTH = 3600
IN_PAD = 3840
D_FF = 2816
EPS = 1e-6
NEG_INF = -1e30
BAND = 128
PAD_UNIT = 2048
DILATIONS = (1, 4, 16)
N_BUCKETS = 32
MAX_DISTANCE = 2048
CONV_WIDTH = 4
CHUNK = 64
N_DEV = 8
VMEM_LIMIT = 56 * 1024 * 1024

ADAM_LR, ADAM_B1, ADAM_B2, ADAM_EPS, ADAM_WD, ADAM_STEP = 0.001, 0.9, 0.999, 1e-08, 0.01, 10


def _cparams(sem):
    return pltpu.CompilerParams(dimension_semantics=sem, vmem_limit_bytes=VMEM_LIMIT)


def _mm(a, b, mode, out_dtype, tm, tn, tk, name, xchg=None):
    if mode == "nn":
        (m, k), (_, n) = a.shape, b.shape
        a_spec = pl.BlockSpec((tm, tk), lambda j, i, kk: (i, kk))
        b_spec = pl.BlockSpec((tk, tn), lambda j, i, kk: (kk, j))
        dims = (((1,), (0,)), ((), ()))
    elif mode == "nt":
        (m, k), (n, _) = a.shape, b.shape
        a_spec = pl.BlockSpec((tm, tk), lambda j, i, kk: (i, kk))
        b_spec = pl.BlockSpec((tn, tk), lambda j, i, kk: (j, kk))
        dims = (((1,), (1,)), ((), ()))
    else:
        (k, m), (_, n) = a.shape, b.shape
        a_spec = pl.BlockSpec((tk, tm), lambda j, i, kk: (kk, i))
        b_spec = pl.BlockSpec((tk, tn), lambda j, i, kk: (kk, j))
        dims = (((0,), (0,)), ((), ()))
    assert m % tm == 0 and n % tn == 0 and k % tk == 0, (name, m, n, k, tm, tn, tk)
    nk = k // tk
    grid = (n // tn, m // tm, nk)
    nx = xchg.n if xchg is not None else 0

    def body(*refs):
        a_ref, b_ref = refs[:2]
        o_ref = refs[2 + nx]
        scratch = refs[3 + 2 * nx:]
        if nx:
            xrefs = (refs[2:2 + nx], refs[3 + nx:3 + 2 * nx], scratch[-3:])
            xchg.start_at_first_step(grid, *xrefs)
        if nk == 1:
            o_ref[...] = lax.dot_general(a_ref[...].astype(bf16), b_ref[...].astype(bf16), dims,
                                         preferred_element_type=f32).astype(o_ref.dtype)
        else:
            acc_ref = scratch[0]
            kk = pl.program_id(2)

            @pl.when(kk == 0)
            def _():
                acc_ref[...] = jnp.zeros_like(acc_ref)

            acc_ref[...] += lax.dot_general(a_ref[...].astype(bf16), b_ref[...].astype(bf16), dims,
                                            preferred_element_type=f32)

            @pl.when(kk == nk - 1)
            def _():
                o_ref[...] = acc_ref[...].astype(o_ref.dtype)
        if nx:
            xchg.wait_at_last_step(grid, *xrefs)

    out = pl.pallas_call(
        body, name=name, grid=grid,
        in_specs=[a_spec, b_spec] + ([_ANY] * nx),
        out_specs=[pl.BlockSpec((tm, tn), lambda j, i, kk: (i, j))] + ([_ANY] * nx),
        out_shape=[jax.ShapeDtypeStruct((m, n), out_dtype)] + (xchg.out_shape() if nx else []),
        scratch_shapes=([pltpu.VMEM((tm, tn), f32)] if nk > 1 else []) + (xchg.scratch() if nx else []),
        compiler_params=_cparams(("arbitrary",) * 3 if nx else ("parallel", "parallel", "arbitrary")),
    )(a, b, *(xchg.arrs if nx else []))
    return (out[0], out[1:]) if nx else out[0]


TOK_TILE = 512
SUB_COLS = 384


def _row_spec(width, tile=TOK_TILE):
    return pl.BlockSpec((tile, width), lambda i: (i, 0))


def _vec_spec(width, rows=1):
    return pl.BlockSpec((rows, width), lambda i: (0, 0))


def _ln_mod_fwd(x, gain, sc, sh, shard, name):
    s, d = x.shape
    nt = s // TOK_TILE
    ride = _ChipGather(shard)

    def body(x_ref, g_ref, sc_ref, sh_ref, sh_in, h_ref, sh_out, *sems):
        i = pl.program_id(0)
        pl.when(i == 0)(lambda: ride.start(sh_in, sh_out, sems))
        xv = x_ref[...]
        rstd = lax.rsqrt(jnp.mean(xv * xv, axis=-1, keepdims=True) + EPS)
        h = (xv * rstd) * g_ref[...] * (1.0 + sc_ref[...]) + sh_ref[...]
        h_ref[...] = h.astype(bf16)
        @pl.when(i == nt - 1)
        def _():
            ride.forward(sh_in, sh_out, sems)
            ride.finish(sh_in, sh_out, sems)

    return pl.pallas_call(
        body, name=name, grid=(nt,),
        in_specs=[_row_spec(d), _vec_spec(d), _vec_spec(d), _vec_spec(d), _ANY],
        out_specs=[_row_spec(d), _ANY],
        out_shape=[jax.ShapeDtypeStruct((s, d), bf16), ride.out_shape()],
        scratch_shapes=ride.scratch(),
        compiler_params=_cparams(("arbitrary",)),
    )(x, gain, sc, sh, shard)


def _proj_resid_ln_mod_fwd(pairs, x, gate, gain, sc, sh, name):
    s, d = x.shape
    npair = len(pairs)

    def body(*refs):
        aw = refs[:2 * npair]
        x_ref, gt_ref, g_ref, sc_ref, sh_ref, y_ref, x1_ref, h_ref = refs[2 * npair:]
        y = jnp.dot(aw[0][...].astype(bf16), aw[1][...], preferred_element_type=f32)
        for t in range(1, npair):
            y = y + jnp.dot(aw[2 * t][...].astype(bf16), aw[2 * t + 1][...], preferred_element_type=f32)
        y_ref[...] = y
        x1 = x_ref[...] + gt_ref[...] * y
        x1_ref[...] = x1
        rstd = lax.rsqrt(jnp.mean(x1 * x1, axis=-1, keepdims=True) + EPS)
        h = (x1 * rstd) * g_ref[...] * (1.0 + sc_ref[...]) + sh_ref[...]
        h_ref[...] = h.astype(bf16)

    aw_specs, aw = [], []
    for a, w in pairs:
        aw_specs += [_row_spec(a.shape[1]), pl.BlockSpec(w.shape, lambda i: (0, 0))]
        aw += [a, w]
    return pl.pallas_call(
        body, name=name, grid=(s // TOK_TILE,),
        in_specs=aw_specs + [_row_spec(d)] + [_vec_spec(d)] * 4,
        out_specs=[_row_spec(d)] * 3,
        out_shape=[jax.ShapeDtypeStruct((s, d), f32)] * 2 + [jax.ShapeDtypeStruct((s, d), bf16)],
        compiler_params=_cparams(("parallel",)),
    )(*aw, x, gate, gain, sc, sh)


FFN_TN = 1408


def _ffn_up(h2, w_gate, w_up, name):
    s, d = h2.shape
    tm = 2 * TOK_TILE

    def body(h_ref, wg_ref, wu_ref, a_ref, g_ref, u_ref):
        h = h_ref[...]
        g = jnp.dot(h, wg_ref[...], preferred_element_type=f32)
        u = jnp.dot(h, wu_ref[...], preferred_element_type=f32)
        a_ref[...] = (g * jax.nn.sigmoid(g) * u).astype(bf16)
        g_ref[...] = g.astype(bf16)
        u_ref[...] = u.astype(bf16)

    w_spec = pl.BlockSpec((d, FFN_TN), lambda j, i: (0, j))
    o_spec = pl.BlockSpec((tm, FFN_TN), lambda j, i: (i, j))
    return pl.pallas_call(
        body, name=name, grid=(D_FF // FFN_TN, s // tm),
        in_specs=[pl.BlockSpec((tm, d), lambda j, i: (i, 0)), w_spec, w_spec],
        out_specs=[o_spec] * 3,
        out_shape=[jax.ShapeDtypeStruct((s, D_FF), bf16)] * 3,
        compiler_params=_cparams(("parallel", "parallel")),
    )(h2, w_gate, w_up)


def _ffn_down_dx(dy2, w_down, gate, up, name):
    s, d = dy2.shape
    tm = 2 * TOK_TILE

    def body(dy_ref, w_ref, g_ref, u_ref, dg_ref, du_ref):
        dy = dy_ref[...]
        for c0 in range(0, FFN_TN, SUB_COLS):
            cols = slice(c0, min(c0 + SUB_COLS, FFN_TN))
            da = lax.dot_general(dy, w_ref[cols, :], _NT, preferred_element_type=f32)
            g = g_ref[:, cols].astype(f32)
            sg = jax.nn.sigmoid(g)
            du_ref[:, cols] = (da * g * sg).astype(bf16)
            dg_ref[:, cols] = (da * u_ref[:, cols].astype(f32) * sg * (1.0 + g * (1.0 - sg))).astype(bf16)

    t_spec = pl.BlockSpec((tm, FFN_TN), lambda j, i: (i, j))
    return pl.pallas_call(
        body, name=name, grid=(D_FF // FFN_TN, s // tm),
        in_specs=[pl.BlockSpec((tm, d), lambda j, i: (i, 0)), pl.BlockSpec((FFN_TN, d), lambda j, i: (j, 0)),
                  t_spec, t_spec],
        out_specs=[t_spec, t_spec],
        out_shape=[jax.ShapeDtypeStruct((s, D_FF), bf16)] * 2,
        compiler_params=_cparams(("parallel", "parallel")),
    )(dy2, w_down, gate, up)


def _acc_spec(width):
    return pl.BlockSpec((1, width), lambda i: (0, 0))


def _proj_final_loss_bwd(a, w, x1, gate2, final_g, target, name):
    s, d = x1.shape
    k = a.shape[1]

    def body(a_ref, w_ref, x1_ref, gt_ref, fg_ref, tg_ref, dx2_ref, dy2_ref, loss_ref, dfg_ref, dgt_ref):
        @pl.when(pl.program_id(0) == 0)
        def _():
            loss_ref[...] = jnp.zeros_like(loss_ref)
            dfg_ref[...] = jnp.zeros_like(dfg_ref)
            dgt_ref[...] = jnp.zeros_like(dgt_ref)

        y2 = jnp.dot(a_ref[...], w_ref[...], preferred_element_type=f32)
        gt = gt_ref[...]
        fg = fg_ref[...]
        x2 = x1_ref[...] + gt * y2
        rstd = lax.rsqrt(jnp.mean(x2 * x2, axis=-1, keepdims=True) + EPS)
        xn = x2 * rstd
        err = xn * fg - tg_ref[...]
        row = jnp.sum(err * err, axis=-1, keepdims=True) * (0.5 / d)
        loss_ref[...] += jnp.sum(row, axis=0, keepdims=True) + jnp.zeros_like(loss_ref)
        dout = err * (1.0 / d)
        dfg_ref[...] += jnp.sum(dout * xn, axis=0, keepdims=True)
        dxn = dout * fg
        dx2 = rstd * (dxn - xn * jnp.mean(dxn * xn, axis=-1, keepdims=True))
        dx2_ref[...] = dx2
        dgt_ref[...] += jnp.sum(dx2 * y2, axis=0, keepdims=True)
        dy2_ref[...] = (gt * dx2).astype(bf16)

    return pl.pallas_call(
        body, name=name, grid=(s // TOK_TILE,),
        in_specs=[_row_spec(k), pl.BlockSpec((k, d), lambda i: (0, 0)), _row_spec(d), _vec_spec(d), _vec_spec(d),
                  _row_spec(d)],
        out_specs=[_row_spec(d), _row_spec(d), _acc_spec(128), _acc_spec(d), _acc_spec(d)],
        out_shape=[jax.ShapeDtypeStruct((s, d), f32), jax.ShapeDtypeStruct((s, d), bf16),
                   jax.ShapeDtypeStruct((1, 128), f32), jax.ShapeDtypeStruct((1, d), f32),
                   jax.ShapeDtypeStruct((1, d), f32)],
        compiler_params=_cparams(("arbitrary",)),
    )(a, w, x1, gate2, final_g, target)


def _proj_ln_mod_bwd(pairs, xin, gain, sc, dres, tm, name, xchg, gate=None, y=None):
    s, d = xin.shape
    with_gate = gate is not None
    npair = len(pairs)
    n_in = 2 * npair + (7 if with_gate else 5) - 1
    n_out = 6 if with_gate else 4

    def body(*refs):
        ab = refs[:2 * npair]
        if with_gate:
            (x_ref, g_ref, sc_ref, dr_ref, gt_ref, y_ref,
             dx_ref, dsh_ref, dsc_ref, dg_ref, dy_ref, dgt_ref) = refs[2 * npair:]
        else:
            x_ref, g_ref, sc_ref, dr_ref, dx_ref, dsh_ref, dsc_ref, dg_ref = refs[2 * npair:]

        @pl.when(pl.program_id(0) == 0)
        def _():
            dsh_ref[...] = jnp.zeros_like(dsh_ref)
            dsc_ref[...] = jnp.zeros_like(dsc_ref)
            dg_ref[...] = jnp.zeros_like(dg_ref)
            if with_gate:
                dgt_ref[...] = jnp.zeros_like(dgt_ref)

        dh = lax.dot_general(ab[0][...].astype(bf16), ab[1][...], _NT, preferred_element_type=f32)
        for t in range(1, npair):
            dh = dh + lax.dot_general(ab[2 * t][...].astype(bf16), ab[2 * t + 1][...], _NT,
                                      preferred_element_type=f32)
        xv = x_ref[...]
        g = g_ref[...]
        sc1 = 1.0 + sc_ref[...]
        rstd = lax.rsqrt(jnp.mean(xv * xv, axis=-1, keepdims=True) + EPS)
        xn = xv * rstd
        dsh_ref[...] += jnp.sum(dh, axis=0, keepdims=True)
        dsc_ref[...] += jnp.sum(dh * (xn * g), axis=0, keepdims=True)
        dg_ref[...] += jnp.sum(dh * sc1 * xn, axis=0, keepdims=True)
        dxn = dh * sc1 * g
        dx = dr_ref[...] + rstd * (dxn - xn * jnp.mean(dxn * xn, axis=-1, keepdims=True))
        dx_ref[...] = dx
        if with_gate:
            dgt_ref[...] += jnp.sum(dx * y_ref[...], axis=0, keepdims=True)
            dy_ref[...] = (gt_ref[...] * dx).astype(bf16)

    row = lambda width: pl.BlockSpec((tm, width), lambda i: (i, 0))
    in_specs, args = [], []
    for a, b in pairs:
        in_specs += [row(a.shape[1]), pl.BlockSpec(b.shape, lambda i: (0, 0))]
        args += [a, b]
    in_specs += [row(d), _vec_spec(d), _vec_spec(d), row(d)]
    args += [xin, gain, sc, dres]
    out_specs = [row(d), _acc_spec(d), _acc_spec(d), _acc_spec(d)]
    out_shape = [jax.ShapeDtypeStruct((s, d), f32)] + [jax.ShapeDtypeStruct((1, d), f32)] * 3
    if with_gate:
        in_specs += [_vec_spec(d), row(d)]
        out_specs += [row(d), _acc_spec(d)]
        out_shape += [jax.ShapeDtypeStruct((s, d), bf16), jax.ShapeDtypeStruct((1, d), f32)]
        args += [gate, y]
    grid = (s // tm,)
    out = pl.pallas_call(
        _ride(body, n_in, n_out, xchg, grid), name=name, grid=grid,
        in_specs=in_specs + [_ANY] * xchg.n, out_specs=out_specs + [_ANY] * xchg.n,
        out_shape=out_shape + xchg.out_shape(), scratch_shapes=xchg.scratch(),
        compiler_params=_cparams(("arbitrary",)),
    )(*args, *xchg.arrs)
    return out[:n_out], out[n_out:]


def _bucket_tables():
    import numpy as np
    qi = np.arange(BAND)[:, None]
    kj = np.arange(2 * BAND)[None, :]
    steps = qi + BAND - kj
    max_exact = N_BUCKETS // 2
    out = []
    for d in DILATIONS:
        dist = np.maximum(steps, 0) * d
        dist_f = np.maximum(dist, 1).astype(np.float32)
        large = max_exact + (np.log(dist_f / np.float32(max_exact)) / np.float32(math.log(MAX_DISTANCE / max_exact))
                             * np.float32(N_BUCKETS - max_exact)).astype(np.int32)
        out.append(np.where(dist < max_exact, dist, np.minimum(large, N_BUCKETS - 1)))
    return jnp.asarray(np.stack(out).astype(np.int32))


def _bias_tables(rel_bias, idx):
    def body(idx_ref, rb_ref, o_ref):
        h = pl.program_id(1)
        idxv = idx_ref[0]
        acc = jnp.zeros((BAND, 2 * BAND), f32)
        for b in range(N_BUCKETS):
            acc = jnp.where(idxv == b, rb_ref[b, h], acc)
        o_ref[0, 0] = jnp.where(_attn_masks()[1], acc, NEG_INF)

    return pl.pallas_call(
        body, name="attn_bias_tables", grid=(3, N_HEADS),
        in_specs=[pl.BlockSpec((1, BAND, 2 * BAND), lambda br, h: (br, 0, 0)),
                  pl.BlockSpec(memory_space=pltpu.SMEM)],
        out_specs=pl.BlockSpec((1, 1, BAND, 2 * BAND), lambda br, h: (br, h, 0, 0)),
        out_shape=jax.ShapeDtypeStruct((3, N_HEADS, BAND, 2 * BAND), f32),
        compiler_params=_cparams(("parallel", "parallel")),
    )(idx, rel_bias)


def _bias_grad(dbias, idx):
    def body(idx_ref, db_ref, o_ref):
        br = pl.program_id(1)

        @pl.when(br == 0)
        def _():
            o_ref[...] = jnp.zeros_like(o_ref)

        idxv = idx_ref[0]
        dbv = db_ref[0, 0]
        row = lax.broadcasted_iota(jnp.int32, (N_BUCKETS, 128), 0)
        acc = jnp.zeros((N_BUCKETS, 128), f32)
        for b in range(N_BUCKETS):
            sb = jnp.sum(jnp.sum(jnp.where(idxv == b, dbv, 0.0), axis=1, keepdims=True), axis=0, keepdims=True)
            acc = acc + jnp.where(row == b, sb, 0.0)
        o_ref[0] += acc

    return pl.pallas_call(
        body, name="attn_bias_grad", grid=(N_HEADS, 3),
        in_specs=[pl.BlockSpec((1, BAND, 2 * BAND), lambda h, br: (br, 0, 0)),
                  pl.BlockSpec((1, 1, BAND, 2 * BAND), lambda h, br: (br, h, 0, 0))],
        out_specs=pl.BlockSpec((1, N_BUCKETS, 128), lambda h, br: (h, 0, 0)),
        out_shape=jax.ShapeDtypeStruct((N_HEADS, N_BUCKETS, 128), f32),
        compiler_params=_cparams(("parallel", "arbitrary")),
    )(idx, dbias)


def _attn_masks():
    lane = lax.broadcasted_iota(jnp.int32, (BAND, 128), 1)
    m0 = lane < HEAD_DIM
    qi = lax.broadcasted_iota(jnp.int32, (BAND, 2 * BAND), 0)
    kj = lax.broadcasted_iota(jnp.int32, (BAND, 2 * BAND), 1)
    steps = qi + BAND - kj
    in_window = (steps >= 0) & (steps <= BAND)
    return m0, in_window, kj >= BAND


_NT = (((1,), (1,)), ((), ()))
_TN = (((0,), (0,)), ((), ()))
_BNN = (((2,), (1,)), ((0,), (0,)))
_BNT = (((2,), (2,)), ((0,), (0,)))
_BTN = (((1,), (1,)), ((0,), (0,)))
ATTN_GROUP = 4
ATTN_ITEMS = PAD_UNIT // BAND
Q_COL, K_COL, V_COL = 0, 4, 8


def _attn_item_rows(j, d, c, cbase):
    r = lax.rem(j, d)
    b = lax.div(j, d)
    loc = b * (d * BAND) + r
    first = jnp.logical_and(c == 0, b == 0)
    start = cbase + loc
    pstart = jnp.where(first, start, start - d * BAND)
    return loc, start, pstart, first


def _attn_fwd(proj, bias, shards):
    s = proj.shape[0]
    rides = [_ChipGather(sh) for sh in shards]

    def body(q_ref, k_ref, v_ref, b_ref, y_ref, lse_ref, o_s, l_s):
        c = pl.program_id(1)
        cbase = pl.multiple_of(c * PAD_UNIT, PAD_UNIT)
        m0, in_window, cur_half = _attn_masks()
        for bi, d in enumerate(DILATIONS):
            def group(jg, carry, bi=bi, d=d):
                locs, qs, ks, vs, pens = [], [], [], [], []
                for t in range(ATTN_GROUP):
                    loc, start, pstart, first = _attn_item_rows(jg * ATTN_GROUP + t, d, c, cbase)
                    locs.append(loc)
                    qs.append(q_ref[pl.ds(loc, BAND, stride=d), :])
                    ks.append(jnp.concatenate([k_ref[pl.ds(pstart, BAND, stride=d), :],
                                               k_ref[pl.ds(start, BAND, stride=d), :]], axis=0))
                    vs.append(jnp.concatenate([v_ref[pl.ds(pstart, BAND, stride=d), :],
                                               v_ref[pl.ds(start, BAND, stride=d), :]], axis=0))
                    pens.append(jnp.where(cur_half, 0.0, jnp.where(first, NEG_INF, 0.0)))
                q = jnp.stack(qs)
                kk = jnp.stack(ks + ks).astype(bf16)
                vv = jnp.stack(vs + vs).astype(bf16)
                pen = jnp.stack(pens + pens)
                qh = (jnp.concatenate([jnp.where(m0, q, 0.0), jnp.where(m0, 0.0, q)], axis=0) * 0.125).astype(bf16)
                sc = lax.dot_general(qh, kk, _BNT, preferred_element_type=f32)
                sc = (sc.reshape(2, ATTN_GROUP, BAND, 2 * BAND) + b_ref[bi][:, None]).reshape(sc.shape) + pen
                mx = jnp.max(sc, axis=-1, keepdims=True)
                e = jnp.exp(sc - mx)
                l = jnp.sum(e, axis=-1, keepdims=True)
                o = lax.dot_general(e.astype(bf16), vv, _BNN, preferred_element_type=f32) * (1.0 / l)
                ls = mx + jnp.log(l)
                for t in range(ATTN_GROUP):
                    rows = pl.ds(locs[t], BAND, stride=d)
                    o_s[bi, rows, :] = jnp.where(m0, o[t], o[ATTN_GROUP + t])
                    l_s[bi, rows, :] = jnp.where(m0, ls[t], ls[ATTN_GROUP + t])
                return carry

            lax.fori_loop(0, ATTN_ITEMS // ATTN_GROUP, group, 0)

        def merge(t, carry):
            rows = pl.ds(pl.multiple_of(t * 256, 256), 256)
            ls = [l_s[i, rows, :] for i in range(3)]
            mx = jnp.maximum(jnp.maximum(ls[0], ls[1]), ls[2])
            ws = [jnp.exp(l - mx) for l in ls]
            tot = ws[0] + ws[1] + ws[2]
            y = (ws[0] * o_s[0, rows, :] + ws[1] * o_s[1, rows, :] + ws[2] * o_s[2, rows, :]) / tot
            y_ref[rows, :] = y
            lse_ref[rows, :] = mx + jnp.log(tot)
            return carry

        lax.fori_loop(0, PAD_UNIT // 256, merge, 0)

    chunk = lambda col: pl.BlockSpec((PAD_UNIT, 128), lambda p, c: (c, col + p))
    full = lambda col: pl.BlockSpec((s, 128), lambda p, c: (0, col + p))
    grid = (N_HEADS // 2, s // PAD_UNIT)
    nsteps = grid[0] * grid[1]
    out = pl.pallas_call(
        _ride_gathers(body, 4, 2, rides, grid, (3 * nsteps) // 4), name="attn_fwd", grid=grid,
        in_specs=[chunk(Q_COL), full(K_COL), full(V_COL),
                  pl.BlockSpec((3, 2, BAND, 2 * BAND), lambda p, c: (0, p, 0, 0))] + [_ANY] * len(rides),
        out_specs=[chunk(0), chunk(0)] + [_ANY] * len(rides),
        out_shape=[jax.ShapeDtypeStruct((s, GROUP_W), f32)] * 2 + [r.out_shape() for r in rides],
        scratch_shapes=[pltpu.VMEM((3, PAD_UNIT, 128), f32)] * 2 + [t for r in rides for t in r.scratch()],
        compiler_params=_cparams(("arbitrary", "arbitrary")),
    )(proj, proj, proj, bias, *shards)
    return out[:2], out[2:]


def _attn_bwd(proj, bias, y, lse, dycat):
    s = proj.shape[0]

    def body(q_ref, k_ref, v_ref, b_ref, y_ref, lse_ref, dy_ref, dq_ref, dk_ref, dv_ref, db_ref, dd_s):
        c = pl.program_id(1)
        cbase = pl.multiple_of(c * PAD_UNIT, PAD_UNIT)
        m0, in_window, cur_half = _attn_masks()

        @pl.when(c == 0)
        def _():
            dk_ref[...] = jnp.zeros_like(dk_ref)
            dv_ref[...] = jnp.zeros_like(dv_ref)
            db_ref[...] = jnp.zeros_like(db_ref)

        dq_ref[...] = jnp.zeros_like(dq_ref)

        def rowdot(t, carry):
            rows = pl.ds(pl.multiple_of(t * 256, 256), 256)
            prod = dy_ref[rows, :] * y_ref[rows, :]
            lane = lax.broadcasted_iota(jnp.int32, prod.shape, 1)
            h0 = lane < HEAD_DIM
            d0 = jnp.sum(jnp.where(h0, prod, 0.0), axis=-1, keepdims=True)
            d1 = jnp.sum(jnp.where(h0, 0.0, prod), axis=-1, keepdims=True)
            dd_s[rows, :] = jnp.where(h0, d0, d1)
            return carry

        lax.fori_loop(0, PAD_UNIT // 256, rowdot, 0)

        for bi, d in enumerate(DILATIONS):
            def group(jg, carry, bi=bi, d=d):
                ng = ATTN_GROUP
                meta, qs, dos, lqs, dds, ks, vs, pens = [], [], [], [], [], [], [], []
                for t in range(ng):
                    loc, start, pstart, first = _attn_item_rows(jg * ng + t, d, c, cbase)
                    qrows = pl.ds(loc, BAND, stride=d)
                    rows = pl.ds(start, BAND, stride=d)
                    prows = pl.ds(pstart, BAND, stride=d)
                    meta.append((qrows, rows, prows))
                    qs.append(q_ref[qrows, :])
                    dos.append(dy_ref[qrows, :])
                    lqs.append(lse_ref[qrows, :])
                    dds.append(dd_s[qrows, :])
                    ks.append(jnp.concatenate([k_ref[prows, :], k_ref[rows, :]], axis=0))
                    vs.append(jnp.concatenate([v_ref[prows, :], v_ref[rows, :]], axis=0))
                    pens.append(jnp.where(cur_half, 0.0, jnp.where(first, NEG_INF, 0.0)))

                def heads(t):
                    return jnp.concatenate([jnp.where(m0, t, 0.0), jnp.where(m0, 0.0, t)], axis=0)

                def head_col(t):
                    return jnp.concatenate([t[:, :, 0:1], t[:, :, HEAD_DIM:HEAD_DIM + 1]], axis=0)

                qh = (heads(jnp.stack(qs)) * 0.125).astype(bf16)
                doh = heads(jnp.stack(dos)).astype(bf16)
                kk = jnp.stack(ks + ks).astype(bf16)
                vv = jnp.stack(vs + vs).astype(bf16)
                sc = lax.dot_general(qh, kk, _BNT, preferred_element_type=f32)
                sc = (sc.reshape(2, ng, BAND, 2 * BAND) + b_ref[bi][:, None]).reshape(sc.shape) + jnp.stack(pens + pens)
                p = jnp.exp(sc - head_col(jnp.stack(lqs)))
                dp = lax.dot_general(doh, vv, _BNT, preferred_element_type=f32)
                ds = p * (dp - head_col(jnp.stack(dds)))
                db_ref[bi] += jnp.sum(ds.reshape(2, ng, BAND, 2 * BAND), axis=1)
                dsb = ds.astype(bf16)
                dq = lax.dot_general(dsb, kk, _BNN, preferred_element_type=f32) * 0.125
                dk = lax.dot_general(dsb, qh, _BTN, preferred_element_type=f32)
                dv = lax.dot_general(p.astype(bf16), doh, _BTN, preferred_element_type=f32)
                for t in range(ng):
                    qrows, rows, prows = meta[t]
                    dq_ref[qrows, :] += jnp.where(m0, dq[t], dq[ng + t])
                    dkt = dk[t] + dk[ng + t]
                    dvt = dv[t] + dv[ng + t]
                    dk_ref[prows, :] += dkt[:BAND]
                    dk_ref[rows, :] += dkt[BAND:]
                    dv_ref[prows, :] += dvt[:BAND]
                    dv_ref[rows, :] += dvt[BAND:]
                return carry

            lax.fori_loop(0, ATTN_ITEMS // ATTN_GROUP, group, 0)

    chunk = lambda col: pl.BlockSpec((PAD_UNIT, 128), lambda p, c: (c, col + p))
    full = lambda col: pl.BlockSpec((s, 128), lambda p, c: (0, col + p))
    bias_spec = pl.BlockSpec((3, 2, BAND, 2 * BAND), lambda p, c: (0, p, 0, 0))
    return pl.pallas_call(
        body, name="attn_bwd", grid=(N_HEADS // 2, s // PAD_UNIT),
        in_specs=[chunk(Q_COL), full(K_COL), full(V_COL), bias_spec, chunk(0), chunk(0), chunk(0)],
        out_specs=[chunk(0), full(0), full(0), bias_spec],
        out_shape=[jax.ShapeDtypeStruct((s, GROUP_W), f32)] * 3
        + [jax.ShapeDtypeStruct((3, N_HEADS, BAND, 2 * BAND), f32)],
        scratch_shapes=[pltpu.VMEM((PAD_UNIT, 128), f32)],
        compiler_params=_cparams(("parallel", "arbitrary")),
    )(proj, proj, proj, bias, y, lse, dycat)


_HI = lax.Precision.HIGHEST
DELTA_COL = 1536
Z_COL = 3072
BA_BLOCK = 28
DELTA_ROWS = 1024


def _hdot(a, b):
    return jnp.dot(a, b, precision=_HI, preferred_element_type=f32)


_DIMS = dict(nn=(((2,), (1,)), ((0,), (0,))), nt=(((2,), (2,)), ((0,), (0,))), tn=(((1,), (1,)), ((0,), (0,))))


@functools.partial(jax.custom_vjp, nondiff_argnums=(2,))
def _mmx(a, b, mode):
    return lax.dot_general(a.astype(bf16), b.astype(bf16), _DIMS[mode], preferred_element_type=f32)


def _mmx_fwd(a, b, mode):
    return _mmx(a, b, mode), (a, b)


def _mmx_bwd(mode, res, g):
    a, b = res
    if mode == "nn":
        return _mmx(g, b, "nt"), _mmx(a, g, "tn")
    if mode == "nt":
        return _mmx(g, b, "nn"), _mmx(g, a, "tn")
    return _mmx(b, g, "nt"), _mmx(a, g, "nn")


_mmx.defvjp(_mmx_fwd, _mmx_bwd)


def _pair_iota():
    row = lax.broadcasted_iota(jnp.int32, (CHUNK, 128), 0)
    lane = lax.broadcasted_iota(jnp.int32, (CHUNK, 128), 1)
    return row, lane, lane & (CHUNK - 1)


def _bd(x):
    _, lane, _ = _pair_iota()
    m0 = lane < CHUNK
    return jnp.concatenate([jnp.where(m0, x, 0.0), jnp.where(m0, 0.0, x)], axis=1)


def _pmm(a, b):
    return _mmx(a, _bd(b), "nn")


def _ntp(x, y):
    return _mmx(x, _bd(y), "nt")


def _tnp(x, y):
    full = _mmx(x, y, "tn")
    _, lane, _ = _pair_iota()
    return jnp.where(lane < CHUNK, full[:, :CHUNK], full[:, CHUNK:])


def _tri_inv(a):
    row, lane, jj = _pair_iota()
    eye = jnp.where(row == jj, 1.0, 0.0).astype(f32)

    def same_block(log2b):
        return (row >> log2b) == (jj >> log2b)

    dgl = jnp.where(same_block(3), a, 0.0)
    d2 = _pmm(dgl, dgl)
    d4 = _pmm(d2, d2)
    t = _pmm(_pmm(eye - dgl, eye + d2), eye + d4)
    for lb in (3, 4, 5):
        off = jnp.where(same_block(lb + 1) & jnp.logical_not(same_block(lb)), a, 0.0)
        t = t - _pmm(_pmm(t, off), t)
    return t


@jax.custom_vjp
def _solve2(a, xv, xk, t):
    return _pmm(t, xv), _pmm(t, xk)


def _solve2_fwd(a, xv, xk, t):
    u, w = _pmm(t, xv), _pmm(t, xk)
    return (u, w), (t, u, w)


def _solve2_bwd(res, cts):
    t, u, w = res
    du, dw = cts
    dxv = _tnp(t, du)
    dxk = _tnp(t, dw)
    return -(_ntp(dxv, u) + _ntp(dxk, w)), dxv, dxk, jnp.zeros_like(t)


_solve2.defvjp(_solve2_fwd, _solve2_bwd)


def _chunk_pre(qp, kp, vp, bp, gcum, t=None):
    row, lane, jj = _pair_iota()
    causal = row >= jj
    strict = row > jj
    rsel = jnp.sum(jnp.where(row == jj, gcum, 0.0), axis=1, keepdims=True)
    decay = jnp.where(causal, jnp.exp(jnp.where(causal, gcum - rsel, 0.0)), 0.0)
    kb = kp * bp
    kd = _bd(kp)
    a = jnp.where(strict, _mmx(kb, kd, "nt") * decay, 0.0)
    eg = jnp.exp(gcum)
    if t is None:
        t = _tri_inv(a)
    u, w = _solve2(a, vp * bp, kb * eg, t)
    qk = jnp.where(causal, _mmx(qp, kd, "nt") * decay, 0.0)
    glast = jnp.sum(jnp.where(row == CHUNK - 1, gcum, 0.0), axis=1, keepdims=True)
    return u, w, qp * eg, kp * jnp.exp(glast - gcum), qk, jnp.exp(glast), t


def _chunk_post(u, w, qt, kh, qk, gam, sp):
    sd = _bd(sp)
    vnew = u - _mmx(w, sd, "nn")
    o = _mmx(qt, sd, "nn") + _pmm(qk, vnew)
    return o, gam * sp + _tnp(kh, vnew)


def _pair_spec(rows=DELTA_ROWS):
    return pl.BlockSpec((rows, 128), lambda i, p: (i, p))


DELTA_NB = DELTA_ROWS // CHUNK


def _chunks(ref):
    return ref[...].reshape(DELTA_NB, CHUNK, 128)


def _pairs(ref, rows):
    return jnp.stack([ref[rows, p * 128:(p + 1) * 128] for p in range(4)], axis=0)


def _delta_chunk_pre(qn, kn, sv, beta, g, xchg):
    s = qn.shape[0]

    def body(q_ref, k_ref, v_ref, b_ref, g_ref, u_ref, w_ref, qt_ref, kh_ref, qk_ref, t_ref, gm_ref):
        outs = _chunk_pre(_chunks(q_ref), _chunks(k_ref), _chunks(v_ref), _chunks(b_ref), _chunks(g_ref))
        for ref, val in zip((u_ref, w_ref, qt_ref, kh_ref, qk_ref, t_ref), outs[:5] + outs[6:]):
            ref[...] = val.reshape(DELTA_ROWS, 128).astype(ref.dtype)
        gm_ref[...] = jnp.broadcast_to(outs[5], (DELTA_NB, 8, 128)).reshape(DELTA_NB * 8, 128)

    v_spec = pl.BlockSpec((DELTA_ROWS, 128), lambda i, p: (i, 8 + p))
    grid = (s // DELTA_ROWS, 4)
    out = pl.pallas_call(
        _ride(body, 5, 7, xchg, grid), name="delta_chunk_pre", grid=grid,
        in_specs=[_pair_spec(), _pair_spec(), v_spec, _pair_spec(), _pair_spec()] + [_ANY] * xchg.n,
        out_specs=[_pair_spec()] * 6 + [_pair_spec(DELTA_NB * 8)] + [_ANY] * xchg.n,
        out_shape=[jax.ShapeDtypeStruct((s, GROUP_W), f32)] + [jax.ShapeDtypeStruct((s, GROUP_W), bf16)] * 5
        + [jax.ShapeDtypeStruct((s // 8, GROUP_W), f32)] + xchg.out_shape(),
        scratch_shapes=xchg.scratch(),
        compiler_params=_cparams(("arbitrary", "arbitrary")),
    )(qn, kn, sv, beta, g, *xchg.arrs)
    return out[:7], out[7:]


def _delta_scan_fwd(u, w, qt, kh, qk, gm):
    s = u.shape[0]

    def body(u_ref, w_ref, qt_ref, kh_ref, qk_ref, gm_ref, o_ref, ss_ref, st):
        @pl.when(pl.program_id(0) == 0)
        def _():
            st[...] = jnp.zeros_like(st)

        def chunk(ci, carry):
            rows = pl.ds(pl.multiple_of(ci * CHUNK, CHUNK), CHUNK)
            grow = pl.ds(pl.multiple_of(ci * 8, 8), 1)
            sp = st[...]
            o, s2 = _chunk_post(_pairs(u_ref, rows), _pairs(w_ref, rows), _pairs(qt_ref, rows),
                                _pairs(kh_ref, rows), _pairs(qk_ref, rows), _pairs(gm_ref, grow), sp)
            for p in range(4):
                ss_ref[rows, p * 128:(p + 1) * 128] = sp[p]
                o_ref[rows, p * 128:(p + 1) * 128] = o[p]
            st[...] = s2
            return carry

        lax.fori_loop(0, DELTA_NB, chunk, 0)

    spec = pl.BlockSpec((DELTA_ROWS, GROUP_W), lambda i: (i, 0))
    gspec = pl.BlockSpec((DELTA_NB * 8, GROUP_W), lambda i: (i, 0))
    return pl.pallas_call(
        body, name="delta_scan_fwd", grid=(s // DELTA_ROWS,),
        in_specs=[spec] * 5 + [gspec],
        out_specs=[spec, spec],
        out_shape=[jax.ShapeDtypeStruct((s, GROUP_W), f32)] * 2,
        scratch_shapes=[pltpu.VMEM((4, CHUNK, 128), f32)],
        compiler_params=_cparams(("arbitrary",)),
    )(u, w, qt, kh, qk, gm)


def _delta_scan_bwd(w, qt, kh, qk, gm, do, xchg):
    s = w.shape[0]
    nb = s // DELTA_ROWS

    def body(w_ref, qt_ref, kh_ref, qk_ref, gm_ref, do_ref, dso_ref, dst):
        @pl.when(pl.program_id(0) == 0)
        def _():
            dst[...] = jnp.zeros_like(dst)

        def chunk(t, carry):
            ci = DELTA_NB - 1 - t
            rows = pl.ds(pl.multiple_of(ci * CHUNK, CHUNK), CHUNK)
            grow = pl.ds(pl.multiple_of(ci * 8, 8), 1)
            ds = dst[...]
            for p in range(4):
                dso_ref[rows, p * 128:(p + 1) * 128] = ds[p]
            do = _pairs(do_ref, rows)
            dvn = _tnp(_pairs(qk_ref, rows), do) + _pmm(_pairs(kh_ref, rows), ds)
            dst[...] = _tnp(_pairs(qt_ref, rows), do) + _pairs(gm_ref, grow) * ds - _tnp(_pairs(w_ref, rows), dvn)
            return carry

        lax.fori_loop(0, DELTA_NB, chunk, 0)

    spec = pl.BlockSpec((DELTA_ROWS, GROUP_W), lambda i: (nb - 1 - i, 0))
    gspec = pl.BlockSpec((DELTA_NB * 8, GROUP_W), lambda i: (nb - 1 - i, 0))
    out = pl.pallas_call(
        _ride(body, 6, 1, xchg, (nb,)), name="delta_scan_bwd", grid=(nb,),
        in_specs=[spec] * 4 + [gspec, spec] + [_ANY] * xchg.n,
        out_specs=[spec] + [_ANY] * xchg.n,
        out_shape=[jax.ShapeDtypeStruct((s, GROUP_W), f32)] + xchg.out_shape(),
        scratch_shapes=[pltpu.VMEM((4, CHUNK, 128), f32)] + xchg.scratch(),
        compiler_params=_cparams(("arbitrary",)),
    )(w, qt, kh, qk, gm, do, *xchg.arrs)
    return out[0], out[1:]


def _delta_chunk_bwd(qn, kn, sv, beta, g, tinv, ss, dso, do, xchg):
    s = qn.shape[0]

    def body(q_ref, k_ref, v_ref, b_ref, g_ref, t_ref, ss_ref, dso_ref, do_ref,
             dq_ref, dk_ref, dv_ref, db_ref, dg_ref):
        sp = _chunks(ss_ref)
        t = _chunks(t_ref)

        def fn(q, k, v, b, gg):
            return _chunk_post(*_chunk_pre(q, k, v, b, gg, t)[:6], sp)

        _, vjp = jax.vjp(fn, _chunks(q_ref), _chunks(k_ref), _chunks(v_ref), _chunks(b_ref), _chunks(g_ref))
        grads = vjp((_chunks(do_ref), _chunks(dso_ref)))
        for ref, val in zip((dq_ref, dk_ref, dv_ref, db_ref, dg_ref), grads):
            ref[...] = val.reshape(DELTA_ROWS, 128)

    v_spec = pl.BlockSpec((DELTA_ROWS, 128), lambda i, p: (i, 8 + p))
    grid = (s // DELTA_ROWS, 4)
    out = pl.pallas_call(
        _ride(body, 9, 5, xchg, grid), name="delta_chunk_bwd", grid=grid,
        in_specs=[_pair_spec(), _pair_spec(), v_spec] + [_pair_spec()] * 6 + [_ANY] * xchg.n,
        out_specs=[_pair_spec()] * 5 + [_ANY] * xchg.n,
        out_shape=[jax.ShapeDtypeStruct((s, GROUP_W), f32)] * 5 + xchg.out_shape(),
        scratch_shapes=xchg.scratch(),
        compiler_params=_cparams(("arbitrary", "arbitrary")),
    )(qn, kn, sv, beta, g, tinv, ss, dso, do, *xchg.arrs)
    return out[:5], out[5:]


def _head_sums(x):
    r = lax.broadcasted_iota(jnp.int32, (128, 128), 0)
    c = lax.broadcasted_iota(jnp.int32, (128, 128), 1)
    pair = jnp.where((r >> 6) == (c >> 6), 1.0, 0.0).astype(f32)
    npair = x.shape[1] // 128
    xb = jnp.concatenate([x[None, :, p * 128:(p + 1) * 128] for p in range(npair)], axis=0)
    sums = _mmx(xb, jnp.broadcast_to(pair, (npair, 128, 128)), "nn")
    return jnp.concatenate([sums[p] for p in range(npair)], axis=1)


def _sel_dot(a, b):
    return jnp.dot(a, b, precision=lax.Precision.HIGH, preferred_element_type=f32)


def _expand_matrix(first):
    r = lax.broadcasted_iota(jnp.int32, (128, GROUP_W), 0)
    c = lax.broadcasted_iota(jnp.int32, (128, GROUP_W), 1) >> 6
    return jnp.where(r == c + first, 1.0, 0.0).astype(f32)


@functools.partial(jax.custom_vjp, nondiff_argnums=(1,))
def _expand_heads(ba, first):
    return _sel_dot(ba, _expand_matrix(first))


def _expand_heads_fwd(ba, first):
    return _expand_heads(ba, first), None


def _expand_heads_bwd(first, _, g):
    return (_mmx(g[None], _expand_matrix(first)[None], "nt")[0],)


_expand_heads.defvjp(_expand_heads_fwd, _expand_heads_bwd)


def _softplus(x):
    return jnp.maximum(x, 0.0) + jnp.log(1.0 + jnp.exp(-jnp.abs(x)))


def _prep_fn(sq, sk, ba, alog_e, dt_e):
    qn = sq * lax.rsqrt(_head_sums(sq * sq) + EPS) * (HEAD_DIM ** -0.5)
    kn = sk * lax.rsqrt(_head_sums(sk * sk) + EPS)
    bl = _expand_heads(ba, 0)
    al = _expand_heads(ba, N_HEADS)
    beta = jax.nn.sigmoid(bl)
    g = -jnp.exp(alog_e) * _softplus(al + dt_e)
    nchunk = g.shape[0] // CHUNK
    ri = lax.broadcasted_iota(jnp.int32, (nchunk, CHUNK, CHUNK), 1)
    ci = lax.broadcasted_iota(jnp.int32, (nchunk, CHUNK, CHUNK), 2)
    tril = jnp.where(ri >= ci, 1.0, 0.0).astype(f32)
    gcum = lax.dot_general(tril, g.reshape(nchunk, CHUNK, g.shape[1]), _BNN, precision=lax.Precision.HIGH,
                           preferred_element_type=f32)
    return qn, kn, beta, gcum.reshape(g.shape)


def _gnorm_fn(o, z, ng_e):
    ms = _head_sums(o * o) * (1.0 / HEAD_DIM)
    return o * lax.rsqrt(ms + EPS) * ng_e * (z * jax.nn.sigmoid(z))


def _tok_spec(width, col):
    return pl.BlockSpec((TOK_TILE, width), lambda i: (i, col))


def _conv_taps(xs_ref, w_ref, base, n, cols):
    acc = w_ref[CONV_WIDTH - 1:CONV_WIDTH, cols] * xs_ref[pl.ds(base, n), cols]
    for j in range(CONV_WIDTH - 1):
        acc = acc + w_ref[j:j + 1, cols] * xs_ref[pl.ds(base - (CONV_WIDTH - 1) + j, n), cols]
    return acc


def _conv_silu_fwd(proj, conv_w):
    s = proj.shape[0]
    wd = 3 * GROUP_W
    hb = TOK_TILE // 8

    def body(x_ref, halo_ref, w_ref, o_ref, y_ref, xs):
        inner = pl.program_id(0) > 0

        def lane_block(cb, carry):
            cols = pl.ds(pl.multiple_of(cb * 128, 128), 128)
            xs[0:8, cols] = jnp.where(inner, halo_ref[:, cols], 0.0)
            xs[8:, cols] = x_ref[:, cols]
            y = _conv_taps(xs, w_ref, 8, TOK_TILE, cols)
            y_ref[:, cols] = y
            o_ref[:, cols] = y * jax.nn.sigmoid(y)
            return carry

        lax.fori_loop(0, wd // 128, lane_block, 0)

    return pl.pallas_call(
        body, name="delta_conv_fwd", grid=(s // TOK_TILE,),
        in_specs=[_tok_spec(wd, 1), pl.BlockSpec((8, wd), lambda i: (jnp.maximum(i * hb - 1, 0), 1)),
                  pl.BlockSpec((CONV_WIDTH, wd), lambda i: (0, 0))],
        out_specs=[_tok_spec(wd, 0)] * 2,
        out_shape=[jax.ShapeDtypeStruct((s, wd), f32)] * 2,
        scratch_shapes=[pltpu.VMEM((TOK_TILE + 8, wd), f32)],
        compiler_params=_cparams(("parallel",)),
    )(proj, proj, conv_w)


def _conv_silu_bwd(proj, conv_w, yc, ds3, xchg):
    s = proj.shape[0]
    wd = 3 * GROUP_W
    hb = TOK_TILE // 8
    nt = s // TOK_TILE

    def body(x_ref, hp_ref, y_ref, yn_ref, dq_ref, dk_ref, dv_ref, dqn_ref, dkn_ref, dvn_ref, w_ref,
             dx_ref, dw_ref, xs, dys):
        i = pl.program_id(0)

        @pl.when(i == 0)
        def _():
            dw_ref[...] = jnp.zeros_like(dw_ref)

        last = i == nt - 1
        def lane_block(lb, carry, third, cur, nxt):
            tcols = pl.ds(pl.multiple_of(lb * 128, 128), 128)
            cols = pl.ds(pl.multiple_of(third * GROUP_W + lb * 128, 128), 128)
            xs[0:8, cols] = jnp.where(i > 0, hp_ref[:, cols], 0.0)
            xs[8:, cols] = x_ref[:, cols]
            y = y_ref[:, cols]
            sg = jax.nn.sigmoid(y)
            dy0 = cur[:, tcols] * (sg * (1.0 + y * (1.0 - sg)))
            dys[0:TOK_TILE, cols] = dy0
            yn = yn_ref[:, cols]
            sgn = jax.nn.sigmoid(yn)
            dys[TOK_TILE:, cols] = jnp.where(last, 0.0, nxt[:, tcols]) * (sgn * (1.0 + yn * (1.0 - sgn)))
            dx = w_ref[CONV_WIDTH - 1:CONV_WIDTH, cols] * dy0
            for j in range(CONV_WIDTH - 1):
                dx = dx + w_ref[j:j + 1, cols] * dys[pl.ds(CONV_WIDTH - 1 - j, TOK_TILE), cols]
            dx_ref[:, cols] = dx.astype(dx_ref.dtype)
            for j in range(CONV_WIDTH):
                dw_ref[j:j + 1, cols] += jnp.sum(dy0 * xs[pl.ds(8 - (CONV_WIDTH - 1) + j, TOK_TILE), cols],
                                                 axis=0, keepdims=True)
            return carry

        for third, (cur, nxt) in enumerate(((dq_ref, dqn_ref), (dk_ref, dkn_ref), (dv_ref, dvn_ref))):
            lax.fori_loop(0, GROUP_W // 128, functools.partial(lane_block, third=third, cur=cur, nxt=nxt), 0)

    prev8 = lambda col: pl.BlockSpec((8, wd), lambda i: (jnp.maximum(i * hb - 1, 0), col))
    next8 = lambda col: pl.BlockSpec((8, wd), lambda i: (jnp.minimum((i + 1) * hb, s // 8 - 1), col))
    next8_third = pl.BlockSpec((8, GROUP_W), lambda i: (jnp.minimum((i + 1) * hb, s // 8 - 1), 0))
    out = pl.pallas_call(
        _ride(body, 11, 2, xchg, (nt,)), name="delta_conv_bwd", grid=(nt,),
        in_specs=[_tok_spec(wd, 1), prev8(1), _tok_spec(wd, 0), next8(0)] + [_tok_spec(GROUP_W, 0)] * 3
        + [next8_third] * 3
        + [pl.BlockSpec((CONV_WIDTH, wd), lambda i: (0, 0))] + [_ANY] * xchg.n,
        out_specs=[_tok_spec(wd, 0), pl.BlockSpec((CONV_WIDTH, wd), lambda i: (0, 0))] + [_ANY] * xchg.n,
        out_shape=[jax.ShapeDtypeStruct((s, wd), bf16), jax.ShapeDtypeStruct((CONV_WIDTH, wd), f32)] + xchg.out_shape(),
        scratch_shapes=[pltpu.VMEM((TOK_TILE + 8, wd), f32), pltpu.VMEM((TOK_TILE + 8, wd), f32)] + xchg.scratch(),
        compiler_params=_cparams(("arbitrary",)),
    )(proj, proj, yc, yc, *ds3, *ds3, conv_w, *xchg.arrs)
    return out[:2], out[2:]


def _delta_prep_fwd(sconv, proj, alog_e, dt_e):
    s = sconv.shape[0]

    def body(sq_ref, sk_ref, ba_ref, al_ref, dt_ref, q_ref, k_ref, b_ref, g_ref):
        qn, kn, beta, g = _prep_fn(sq_ref[...], sk_ref[...], ba_ref[...], al_ref[...], dt_ref[...])
        q_ref[...] = qn
        k_ref[...] = kn
        b_ref[...] = beta
        g_ref[...] = g

    return pl.pallas_call(
        body, name="delta_prep_fwd", grid=(s // TOK_TILE,),
        in_specs=[_tok_spec(GROUP_W, 0), _tok_spec(GROUP_W, 1), _tok_spec(128, BA_BLOCK),
                  _vec_spec(GROUP_W), _vec_spec(GROUP_W)],
        out_specs=[_tok_spec(GROUP_W, 0)] * 4,
        out_shape=[jax.ShapeDtypeStruct((s, GROUP_W), f32)] * 4,
        compiler_params=_cparams(("parallel",)),
    )(sconv, sconv, proj, alog_e, dt_e)


def _delta_prep_bwd(sconv, proj, alog_e, dt_e, dqn, dkn, dbeta, dg, xchg):
    s = sconv.shape[0]
    grid = (s // TOK_TILE,)

    def body(sq_ref, sk_ref, ba_ref, al_ref, dt_ref, dq_ref, dk_ref, db_ref, dg_ref,
             dsq_ref, dsk_ref, dba_ref, dal_ref, ddt_ref):
        @pl.when(pl.program_id(0) == 0)
        def _():
            dal_ref[...] = jnp.zeros_like(dal_ref)
            ddt_ref[...] = jnp.zeros_like(ddt_ref)

        _, vjp = jax.vjp(_prep_fn, sq_ref[...], sk_ref[...], ba_ref[...], al_ref[...], dt_ref[...])
        dsq, dsk, dba, dal, ddt = vjp((dq_ref[...], dk_ref[...], db_ref[...], dg_ref[...]))
        dsq_ref[...] = dsq
        dsk_ref[...] = dsk
        dba_ref[...] = dba.astype(bf16)
        dal_ref[...] += dal
        ddt_ref[...] += ddt

    out = pl.pallas_call(
        _ride(body, 9, 5, xchg, grid), name="delta_prep_bwd", grid=grid,
        in_specs=[_tok_spec(GROUP_W, 0), _tok_spec(GROUP_W, 1), _tok_spec(128, BA_BLOCK),
                  _vec_spec(GROUP_W), _vec_spec(GROUP_W)] + [_tok_spec(GROUP_W, 0)] * 4 + [_ANY] * xchg.n,
        out_specs=[_tok_spec(GROUP_W, 0), _tok_spec(GROUP_W, 0), _tok_spec(128, 0),
                   _acc_spec(GROUP_W), _acc_spec(GROUP_W)] + [_ANY] * xchg.n,
        out_shape=[jax.ShapeDtypeStruct((s, GROUP_W), f32)] * 2 + [jax.ShapeDtypeStruct((s, 128), bf16)]
        + [jax.ShapeDtypeStruct((1, GROUP_W), f32)] * 2 + xchg.out_shape(),
        scratch_shapes=xchg.scratch(),
        compiler_params=_cparams(("arbitrary",)),
    )(sconv, sconv, proj, alog_e, dt_e, dqn, dkn, dbeta, dg, *xchg.arrs)
    return out[:5], out[5:]


def _gnorm_fwd(o, proj, ng_e):
    s = o.shape[0]

    def body(o_ref, z_ref, g_ref, y_ref):
        y_ref[...] = _gnorm_fn(o_ref[...], z_ref[...], g_ref[...])

    return pl.pallas_call(
        body, name="delta_gnorm_fwd", grid=(s // TOK_TILE,),
        in_specs=[_tok_spec(GROUP_W, 0), _tok_spec(GROUP_W, Z_COL // GROUP_W), _vec_spec(GROUP_W)],
        out_specs=_tok_spec(GROUP_W, 0),
        out_shape=jax.ShapeDtypeStruct((s, GROUP_W), f32),
        compiler_params=_cparams(("parallel",)),
    )(o, proj, ng_e)


def _gnorm_bwd(o, proj, ng_e, dycat):
    s = o.shape[0]

    def body(o_ref, z_ref, g_ref, dy_ref, do_ref, dz_ref, dg_ref):
        @pl.when(pl.program_id(0) == 0)
        def _():
            dg_ref[...] = jnp.zeros_like(dg_ref)

        _, vjp = jax.vjp(_gnorm_fn, o_ref[...], z_ref[...], g_ref[...])
        do, dz, dg = vjp(dy_ref[...])
        do_ref[...] = do
        dz_ref[...] = dz.astype(bf16)
        dg_ref[...] += dg

    return pl.pallas_call(
        body, name="delta_gnorm_bwd", grid=(s // TOK_TILE,),
        in_specs=[_tok_spec(GROUP_W, 0), _tok_spec(GROUP_W, Z_COL // GROUP_W), _vec_spec(GROUP_W),
                  _tok_spec(GROUP_W, 1)],
        out_specs=[_tok_spec(GROUP_W, 0), _tok_spec(GROUP_W, 0), _acc_spec(GROUP_W)],
        out_shape=[jax.ShapeDtypeStruct((s, GROUP_W), f32), jax.ShapeDtypeStruct((s, GROUP_W), bf16),
                   jax.ShapeDtypeStruct((1, GROUP_W), f32)],
        compiler_params=_cparams(("arbitrary",)),
    )(o, proj, ng_e, dycat)


_MESH = pl.DeviceIdType.MESH
_ANY = pl.BlockSpec(memory_space=pl.ANY)
_VMEM = pl.BlockSpec(memory_space=pltpu.VMEM)


def _my_place():
    x, y, c = lax.axis_index("x"), lax.axis_index("y"), lax.axis_index("c")
    return x, y, c, 4 * x + 2 * y + c


def _peer(k, x, y, c):
    px = 1 - x if k & 4 else x
    py = 1 - y if k & 2 else y
    pc = 1 - c if k & 1 else c
    return (px, py, pc), 4 * px + 2 * py + pc


def _exchange_all(src_of_peer, dst_ref, send_sems, recv_sems, x, y, c, me):
    sent = []
    for k in range(1, N_DEV):
        dev, pidx = _peer(k, x, y, c)
        cp = pltpu.make_async_remote_copy(src_ref=src_of_peer(pidx), dst_ref=dst_ref.at[me],
                                          send_sem=send_sems.at[k - 1], recv_sem=recv_sems.at[k - 1],
                                          device_id=dev, device_id_type=_MESH)
        cp.start()
        sent.append(cp)
    for k in range(1, N_DEV):
        dev, pidx = _peer(k, x, y, c)
        pltpu.make_async_remote_copy(src_ref=src_of_peer(pidx), dst_ref=dst_ref.at[pidx],
                                     send_sem=send_sems.at[k - 1], recv_sem=recv_sems.at[k - 1],
                                     device_id=dev, device_id_type=_MESH).wait_recv()
    for cp in sent:
        cp.wait_send()


def _ada_exchange(cv8, w_ada, b_ada8):
    def body(cv_ref, w_ref, b_ref, call_ref, modp_ref, part_s, s1, r1, s2, r2):
        x, y, c, me = _my_place()
        call_ref[me] = cv_ref[...]
        _exchange_all(lambda pidx: cv_ref, call_ref, s1, r1, x, y, c, me)
        bias = b_ref[me]
        for j in range(N_DEV):
            cj = call_ref[j][:, :D_MODEL]
            part_s[j] = _hdot(cj * jax.nn.sigmoid(cj), w_ref[...]) + bias
        modp_ref[me] = part_s[me]
        _exchange_all(lambda pidx: part_s.at[pidx], modp_ref, s2, r2, x, y, c, me)

    nsh = w_ada.shape[1]
    return pl.pallas_call(
        body, name="ada_exchange",
        in_specs=[_VMEM, _VMEM, _VMEM], out_specs=[_VMEM, _VMEM],
        out_shape=[jax.ShapeDtypeStruct((N_DEV, 8, cv8.shape[1]), f32), jax.ShapeDtypeStruct((N_DEV, 8, nsh), f32)],
        scratch_shapes=[pltpu.VMEM((N_DEV, 8, nsh), f32)] + [pltpu.SemaphoreType.DMA((N_DEV - 1,))] * 4,
        compiler_params=pltpu.CompilerParams(vmem_limit_bytes=VMEM_LIMIT),
    )(cv8, w_ada, b_ada8)


def _all_to_all(arrs, name):
    ex = _Exchange(arrs, gather=False)

    def body(*refs):
        srcs, dsts, sems = refs[:ex.n], refs[ex.n:2 * ex.n], refs[2 * ex.n:]
        ex.start(srcs, dsts, sems)
        ex.wait(srcs, dsts, sems)

    return pl.pallas_call(
        body, name=name,
        in_specs=[_ANY] * ex.n, out_specs=[_ANY] * ex.n,
        out_shape=ex.out_shape(), scratch_shapes=ex.scratch(),
    )(*arrs)


class _Exchange:
    def __init__(self, arrs, gather):
        self.arrs, self.gather, self.n = list(arrs), gather, len(arrs)

    def out_shape(self):
        return [jax.ShapeDtypeStruct(((N_DEV,) + a.shape) if self.gather else a.shape, a.dtype) for a in self.arrs]

    def scratch(self):
        if self.n == 0:
            return []
        return [pltpu.SemaphoreType.DMA((self.n, N_DEV - 1)), pltpu.SemaphoreType.DMA((self.n, N_DEV - 1)),
                pltpu.SemaphoreType.DMA((self.n,))]

    def _src(self, srcs, a, idx):
        return srcs[a] if self.gather else srcs[a].at[idx]

    def _copies(self, srcs, dsts, sems, incoming):
        send_sems, recv_sems, _ = sems
        x, y, c, me = _my_place()
        out = []
        for a in range(self.n):
            for k in range(1, N_DEV):
                dev, pidx = _peer(k, x, y, c)
                out.append(pltpu.make_async_remote_copy(
                    src_ref=self._src(srcs, a, pidx), dst_ref=dsts[a].at[pidx if incoming else me],
                    send_sem=send_sems.at[a, k - 1], recv_sem=recv_sems.at[a, k - 1],
                    device_id=dev, device_id_type=_MESH))
        return out

    def _local(self, srcs, dsts, sems):
        me = _my_place()[3]
        return [pltpu.make_async_copy(self._src(srcs, a, me), dsts[a].at[me], sems[2].at[a]) for a in range(self.n)]

    def start(self, srcs, dsts, sems):
        for cp in self._local(srcs, dsts, sems) + self._copies(srcs, dsts, sems, incoming=False):
            cp.start()

    def wait(self, srcs, dsts, sems):
        for cp in self._copies(srcs, dsts, sems, incoming=True):
            cp.wait_recv()
        for cp in self._copies(srcs, dsts, sems, incoming=False):
            cp.wait_send()
        for cp in self._local(srcs, dsts, sems):
            cp.wait()

    def start_at_first_step(self, grid, srcs, dsts, sems):
        first = functools.reduce(jnp.logical_and, [pl.program_id(i) == 0 for i in range(len(grid))])
        pl.when(first)(lambda: self.start(srcs, dsts, sems))

    def wait_at_last_step(self, grid, srcs, dsts, sems):
        last = functools.reduce(jnp.logical_and, [pl.program_id(i) == g - 1 for i, g in enumerate(grid)])
        pl.when(last)(lambda: self.wait(srcs, dsts, sems))


class _ChipGather:
    def __init__(self, shard):
        self.shard = shard

    def out_shape(self):
        return jax.ShapeDtypeStruct((N_DEV,) + self.shard.shape, self.shard.dtype)

    def scratch(self):
        return [pltpu.SemaphoreType.DMA((N_DEV - 1,)), pltpu.SemaphoreType.DMA((N_DEV - 1,)),
                pltpu.SemaphoreType.DMA(())]

    def _place(self):
        x, y, c, me = _my_place()
        return x, y, c, me, (x, y, 1 - c), [(1 - x, y), (x, 1 - y), (1 - x, 1 - y)]

    def _copy(self, out, sems, k, block, to, src=None):
        rows = out.at[4 * block[0] + 2 * block[1] + block[2]]
        return pltpu.make_async_remote_copy(src_ref=rows if src is None else src, dst_ref=rows,
                                            send_sem=sems[0].at[k], recv_sem=sems[1].at[k],
                                            device_id=to, device_id_type=_MESH)

    def start(self, src, out, sems):
        x, y, c, me, sib, chips = self._place()
        pltpu.make_async_copy(src, out.at[me], sems[2]).start()
        self._copy(out, sems, 0, (x, y, c), sib, src=src).start()
        for j, chip in enumerate(chips):
            self._copy(out, sems, 1 + j, (x, y, c), (*chip, c), src=src).start()

    def forward(self, src, out, sems):
        x, y, c, me, sib, chips = self._place()
        for j, chip in enumerate(chips):
            self._copy(out, sems, 1 + j, (*chip, c), (x, y, c)).wait_recv()
            self._copy(out, sems, 4 + j, (*chip, c), sib).start()

    def finish(self, src, out, sems):
        x, y, c, me, sib, chips = self._place()
        self._copy(out, sems, 0, (x, y, 1 - c), (x, y, c)).wait_recv()
        for j, chip in enumerate(chips):
            self._copy(out, sems, 4 + j, (*chip, 1 - c), (x, y, c)).wait_recv()
        self._copy(out, sems, 0, (x, y, c), sib, src=src).wait_send()
        for j, chip in enumerate(chips):
            self._copy(out, sems, 1 + j, (x, y, c), (*chip, c), src=src).wait_send()
            self._copy(out, sems, 4 + j, (*chip, c), sib).wait_send()
        pltpu.make_async_copy(src, out.at[me], sems[2]).wait()


def _ride_gathers(body, n_in, n_out, rides, grid, forward_step):
    n = len(rides)
    sizes = list(grid)

    def wrapped(*refs):
        ins, xs = refs[:n_in], refs[n_in:n_in + n]
        outs, xd = refs[n_in + n:n_in + n + n_out], refs[n_in + n + n_out:n_in + 2 * n + n_out]
        scratch = refs[n_in + 2 * n + n_out:]
        own, sems = scratch[:len(scratch) - 3 * n], scratch[len(scratch) - 3 * n:]
        step = pl.program_id(0)
        for i in range(1, len(sizes)):
            step = step * sizes[i] + pl.program_id(i)

        def each(phase):
            for r in range(n):
                getattr(rides[r], phase)(xs[r], xd[r], sems[3 * r:3 * r + 3])

        pl.when(step == 0)(lambda: each("start"))
        body(*ins, *outs, *own)
        pl.when(step == forward_step)(lambda: each("forward"))
        pl.when(step == math.prod(sizes) - 1)(lambda: each("finish"))

    return wrapped


def _ride(body, n_in, n_out, xchg, grid):
    nx = xchg.n
    if nx == 0:
        return body

    def wrapped(*refs):
        ins, xs = refs[:n_in], refs[n_in:n_in + nx]
        outs, xd = refs[n_in + nx:n_in + nx + n_out], refs[n_in + nx + n_out:n_in + 2 * nx + n_out]
        scratch = refs[n_in + 2 * nx + n_out:]
        xchg.start_at_first_step(grid, xs, xd, scratch[-3:])
        body(*ins, *outs, *scratch[:-3])
        xchg.wait_at_last_step(grid, xs, xd, scratch[-3:])

    return wrapped


def _adamw_math(w, g, m, v):
    m2 = ADAM_B1 * m + (1.0 - ADAM_B1) * g
    v2 = ADAM_B2 * v + (1.0 - ADAM_B2) * (g * g)
    m_hat = m2 / (1.0 - ADAM_B1 ** ADAM_STEP)
    v_hat = v2 / (1.0 - ADAM_B2 ** ADAM_STEP)
    delta = -ADAM_LR * (m_hat / (jnp.sqrt(v_hat) + ADAM_EPS) + ADAM_WD * w)
    return delta, m2, v2


def _row_tile(rows):
    for t in (256, 128, 64, 32, 16, 8):
        if rows % t == 0:
            return t
    return rows


def _reduce_adamw(parts, w, m, v, name):
    _, r, cdim = parts.shape
    tr = _row_tile(r)

    def body(p_ref, w_ref, m_ref, v_ref, g_ref, d_ref, m2_ref, v2_ref):
        g = p_ref[0].astype(f32)
        for j in range(1, N_DEV):
            g = g + p_ref[j].astype(f32)
        delta, m2, v2 = _adamw_math(w_ref[...], g, m_ref[...], v_ref[...])
        g_ref[...] = g
        d_ref[...] = delta
        m2_ref[...] = m2
        v2_ref[...] = v2

    spec = pl.BlockSpec((tr, cdim), lambda i: (i, 0))
    return pl.pallas_call(
        body, name=name, grid=(r // tr,),
        in_specs=[pl.BlockSpec((N_DEV, tr, cdim), lambda i: (0, i, 0)), spec, spec, spec],
        out_specs=[spec] * 4,
        out_shape=[jax.ShapeDtypeStruct((r, cdim), f32)] * 4,
        compiler_params=_cparams(("parallel",)),
    )(parts, w, m, v)


def _adamw(w, g, m, v, name):
    r, cdim = w.shape
    tr = _row_tile(r)

    def body(w_ref, g_ref, m_ref, v_ref, d_ref, m2_ref, v2_ref):
        delta, m2, v2 = _adamw_math(w_ref[...], g_ref[...], m_ref[...], v_ref[...])
        d_ref[...] = delta
        m2_ref[...] = m2
        v2_ref[...] = v2

    spec = pl.BlockSpec((tr, cdim), lambda i: (i, 0))
    return pl.pallas_call(
        body, name=name, grid=(r // tr,),
        in_specs=[spec] * 4, out_specs=[spec] * 3,
        out_shape=[jax.ShapeDtypeStruct((r, cdim), f32)] * 3,
        compiler_params=_cparams(("parallel",)),
    )(w, g, m, v)


def _sum_devices(parts, name):
    _, r, cdim = parts.shape

    def body(p_ref, o_ref):
        g = p_ref[0]
        for j in range(1, N_DEV):
            g = g + p_ref[j]
        o_ref[...] = g

    return pl.pallas_call(
        body, name=name, out_shape=jax.ShapeDtypeStruct((r, cdim), f32),
        in_specs=[_VMEM], out_specs=_VMEM,
    )(parts)


def _ada_wgrad(c_all8, dmod_cols):
    nsh = dmod_cols.shape[1]

    def body(c_ref, d_ref, o_ref):
        cv = c_ref[...]
        o_ref[...] = lax.dot_general(cv * jax.nn.sigmoid(cv), d_ref[...], _TN, precision=_HI,
                                     preferred_element_type=f32)

    return pl.pallas_call(
        body, name="ada_wgrad", out_shape=jax.ShapeDtypeStruct((D_MODEL, nsh), f32),
        in_specs=[_VMEM, _VMEM], out_specs=_VMEM,
        compiler_params=pltpu.CompilerParams(vmem_limit_bytes=VMEM_LIMIT),
    )(c_all8, dmod_cols)


def _cols(t):
    return t.transpose(1, 0, 2).reshape(t.shape[1], N_DEV * t.shape[2])


def _col_blocks(t, n):
    return t.reshape(t.shape[0], N_DEV, n).transpose(1, 0, 2).astype(bf16)


def _row_blocks(t):
    return t.reshape(N_DEV, t.shape[0] // N_DEV, t.shape[1]).astype(bf16)


def _local_step(x, tgt, mod, norm_attn_g, w_in_sh, rel_bias, conv_full, a_log, dt_bias, delta_norm_g,
                norm_ffn_g, final_norm_g, w_out_sh, w_gate_sh, w_up_sh, w_down_sh):
    s = x.shape[0]
    sh1, sc1, g1, sh2, sc2, g2 = [mod[:, i * D_MODEL:(i + 1) * D_MODEL] for i in range(6)]
    nag = norm_attn_g.reshape(1, D_MODEL)
    nfg = norm_ffn_g.reshape(1, D_MODEL)
    fg = final_norm_g.reshape(1, D_MODEL)
    idx = _bucket_tables()
    bias = _bias_tables(rel_bias, idx)
    alog_e = jnp.repeat(a_log.reshape(N_HEADS), HEAD_DIM)[None]
    dt_e = jnp.repeat(dt_bias.reshape(N_HEADS), HEAD_DIM)[None]
    ng_e = jnp.tile(delta_norm_g.reshape(HEAD_DIM), N_HEADS)[None]

    h1, w_in_g = _ln_mod_fwd(x, nag, sc1, sh1, w_in_sh, "ln1_fwd")
    w_in_p = jnp.pad(_cols(w_in_g), ((0, 0), (0, IN_PAD - IN_WIDTH)))
    proj, (w_out_g,) = _mm(h1, w_in_p, "nn", f32, 512, IN_PAD, 1024, "in_proj",
                           xchg=_Exchange([w_out_sh], gather=True))
    (y_attn, lse), (w_gate_g, w_up_g, w_down_g) = _attn_fwd(proj, bias, [w_gate_sh, w_up_sh, w_down_sh])
    w_out_b = w_out_g.reshape(2 * GROUP_W, D_MODEL)
    w_down_b = w_down_g.reshape(D_FF, D_MODEL)
    w_gate_b, w_up_b = _cols(w_gate_g), _cols(w_up_g)
    n_ff = w_gate_sh.shape[1]
    sconv, yconv = _conv_silu_fwd(proj, conv_full)
    qn, kn, beta, g = _delta_prep_fwd(sconv, proj, alog_e, dt_e)
    (u, w, qt, kh, qk, tinv, gm), _ = _delta_chunk_pre(qn, kn, sconv, beta, g, _Exchange([], gather=False))
    o, ss = _delta_scan_fwd(u, w, qt, kh, qk, gm)
    y_delta = _gnorm_fwd(o, proj, ng_e)
    y, x1, h2 = _proj_resid_ln_mod_fwd([(y_attn, w_out_b[:GROUP_W]), (y_delta, w_out_b[GROUP_W:])],
                                       x, g1, nfg, sc2, sh2, "out_proj_ln2")
    act, gate, up = _ffn_up(h2, w_gate_b, w_up_b, "ffn_up")
    dx2, dy2, loss, dfg, dg2 = _proj_final_loss_bwd(act, w_down_b, x1, g2, fg, tgt, "ffn_down_loss")

    dgate, dup = _ffn_down_dx(dy2, w_down_b, gate, up, "ffn_down_dx")
    g_down = _mm(act, dy2, "tn", f32, 1408, 1024, 2048, "ffn_down_dw")
    (dx1, dsh2, dsc2, dnfg, dy, dg1), (r_down,) = _proj_ln_mod_bwd(
        [(dgate, w_gate_b), (dup, w_up_b)], x1, nfg, sc2, dx2, 256, "ffn_up_dx_ln2",
        _Exchange([_row_blocks(g_down)], gather=False), gate=g1, y=y)
    g_gate = _mm(h2, dgate, "tn", f32, 1024, 1408, 2048, "ffn_gate_dw")
    g_up = _mm(h2, dup, "tn", f32, 1024, 1408, 2048, "ffn_up_dw")
    dycat = _mm(dy, w_out_b, "nt", f32, 512, 1024, 1024, "out_proj_dx")
    g_out = jnp.concatenate([_mm(y_attn, dy, "tn", f32, GROUP_W, 1024, 2048, "out_proj_dw_attn"),
                             _mm(y_delta, dy, "tn", f32, GROUP_W, 1024, 2048, "out_proj_dw_delta")], axis=0)
    dq, dk, dv, dbias = _attn_bwd(proj, bias, y_attn, lse, dycat)
    g_rb = _bias_grad(dbias, idx)[:, :, 0].T
    do, dz, dng = _gnorm_bwd(o, proj, ng_e, dycat)
    dso, _ = _delta_scan_bwd(w, qt, kh, qk, gm, do, _Exchange([], gather=False))
    (dqn, dkn, dvd, dbeta, dgd), (r_up,) = _delta_chunk_bwd(
        qn, kn, sconv, beta, g, tinv, ss, dso, do, _Exchange([_col_blocks(g_up, n_ff)], gather=False))
    (dsq, dsk, dba, dal, ddt), _ = _delta_prep_bwd(
        sconv, proj, alog_e, dt_e, dqn, dkn, dbeta, dgd, _Exchange([], gather=False))
    (dxc, g_conv), (r_gate, r_out) = _conv_silu_bwd(
        proj, conv_full, yconv, (dsq, dsk, dvd),
        _Exchange([_col_blocks(g_gate, n_ff), _row_blocks(g_out)], gather=False))
    pieces = ((dq, 0), (dk, GROUP_W), (dv, 2 * GROUP_W), (dxc, DELTA_COL), (dz, Z_COL), (dba, BA_BLOCK * 128))
    g_in = jnp.concatenate(
        [_mm(h1, p, "tn", f32, 1024, min(p.shape[1], 768), 2048, "in_proj_dw_%d" % c) for p, c in pieces], axis=1)
    (gx, dsh1, dsc1, dnag), (r_in,) = _proj_ln_mod_bwd(
        [(p, w_in_p[:, c:c + p.shape[1]]) for p, c in pieces], x, nag, sc1, dx1, TOK_TILE, "in_proj_dx_ln1",
        _Exchange([_col_blocks(g_in[:, :IN_WIDTH], IN_WIDTH // N_DEV)], gather=False))
    grads = dict(
        x=gx, mod=jnp.concatenate([dsh1, dsc1, dg1, dsh2, dsc2, dg2], axis=1),
        norm_attn_g=dnag, norm_ffn_g=dnfg, final_norm_g=dfg, rel_bias=g_rb, conv_w=g_conv,
        a_log=dal.reshape(N_HEADS, HEAD_DIM).sum(-1), dt_bias=ddt.reshape(N_HEADS, HEAD_DIM).sum(-1),
        delta_norm_g=dng.reshape(N_HEADS, HEAD_DIM).sum(0),
        w_in=r_in, w_out=r_out, w_gate=r_gate, w_up=r_up, w_down=r_down)
    return loss[0, 0], grads


def _misc_row(rel_bias, a_log, dt_bias, delta_norm_g):
    flat = jnp.concatenate([rel_bias.reshape(-1), a_log.reshape(-1), dt_bias.reshape(-1), delta_norm_g.reshape(-1)])
    return jnp.pad(flat, (0, D_MODEL - flat.shape[0]))[None]


def _pack_small(b_ada, nag, nfg, fng, rel_bias, a_log, dt_bias, dng, conv_shard):
    rows = [b_ada.reshape(6, D_MODEL), nag.reshape(1, D_MODEL), nfg.reshape(1, D_MODEL), fng.reshape(1, D_MODEL),
            _misc_row(rel_bias, a_log, dt_bias, dng),
            jnp.pad(conv_shard.reshape(-1), (0, D_MODEL - conv_shard.size))[None],
            jnp.zeros((5, D_MODEL), f32)]
    return jnp.concatenate(rows, axis=0)


def _unpack_small(p, conv_shape):
    misc = p[9]
    return dict(
        b_ada=p[0:6].reshape(1, 6 * D_MODEL), norm_attn_g=p[6:7], norm_ffn_g=p[7:8], final_norm_g=p[8],
        rel_bias=misc[0:256].reshape(N_BUCKETS, N_HEADS), a_log=misc[256:264].reshape(1, N_HEADS),
        dt_bias=misc[264:272].reshape(1, N_HEADS), delta_norm_g=misc[272:336].reshape(1, HEAD_DIM),
        conv_w=p[10, :conv_shape[1] * conv_shape[2]].reshape(conv_shape))


def kernel(x, c, w_ada, b_ada, norm_attn_g, w_in, rel_bias, conv_w, a_log, dt_bias, delta_norm_g, w_out, norm_ffn_g, w_gate, w_up, w_down, final_norm_g, loss_target, m_w_ada, m_b_ada, m_norm_attn_g, m_w_in, m_rel_bias, m_conv_w, m_a_log, m_dt_bias, m_delta_norm_g, m_w_out, m_norm_ffn_g, m_w_gate, m_w_up, m_w_down, m_final_norm_g, v_w_ada, v_b_ada, v_norm_attn_g, v_w_in, v_rel_bias, v_conv_w, v_a_log, v_dt_bias, v_delta_norm_g, v_w_out, v_norm_ffn_g, v_w_gate, v_w_up, v_w_down, v_final_norm_g):
    me = 4 * lax.axis_index("x") + 2 * lax.axis_index("y") + lax.axis_index("c")
    ada_sh = w_ada.shape[2]
    conv_sh = conv_w.shape[2]

    cv = jnp.concatenate([c[0], conv_w[0].reshape(-1)])
    cv8 = jnp.zeros((8, 2 * D_MODEL), f32).at[0, :cv.shape[0]].set(cv)
    b8 = jnp.broadcast_to(b_ada.reshape(N_DEV, 1, ada_sh), (N_DEV, 8, ada_sh))
    call, modp = _ada_exchange(cv8, w_ada[0], b8)
    mod = modp[:, 0, :].reshape(1, 6 * D_MODEL)
    c_all = call[:, 0, :D_MODEL]
    conv_full = call[:, 0, D_MODEL:D_MODEL + CONV_WIDTH * conv_sh].reshape(N_DEV, CONV_WIDTH, conv_sh)
    conv_full = conv_full.transpose(1, 0, 2).reshape(CONV_WIDTH, N_DEV * conv_sh)

    loss_local, gr = _local_step(x[0], loss_target[0], mod, norm_attn_g, w_in[0].astype(bf16), rel_bias, conv_full, a_log,
                                 dt_bias, delta_norm_g, norm_ffn_g, final_norm_g, w_out[0].astype(bf16),
                                 w_gate[0].astype(bf16), w_up[0].astype(bf16), w_down[0].astype(bf16))
    loss = lax.psum(loss_local, ("x", "y", "c"))

    small = jnp.concatenate([
        gr["mod"].reshape(6, D_MODEL), gr["norm_attn_g"], gr["norm_ffn_g"], gr["final_norm_g"],
        gr["conv_w"].reshape(6, D_MODEL),
        _misc_row(gr["rel_bias"], gr["a_log"], gr["dt_bias"], gr["delta_norm_g"])], axis=0)
    parts = _all_to_all([jnp.broadcast_to(small[None], (N_DEV,) + small.shape)], "small_gather")[0]
    tot = _sum_devices(parts, "small_sum")
    g_conv_full = tot[9:15].reshape(CONV_WIDTH, N_DEV * conv_sh)
    g_conv = lax.dynamic_slice(g_conv_full, (0, me * conv_sh), (CONV_WIDTH, conv_sh))
    misc = tot[15]
    g_small = _pack_small(tot[0:6], tot[6], tot[7], tot[8], misc[0:256], misc[256:264], misc[264:272],
                          misc[272:336], g_conv)
    pk = lambda pre: _pack_small(pre[0], pre[1], pre[2], pre[3], pre[4], pre[5], pre[6], pre[7], pre[8])
    w_small = pk((b_ada, norm_attn_g, norm_ffn_g, final_norm_g, rel_bias, a_log, dt_bias, delta_norm_g, conv_w))
    m_small = pk((m_b_ada, m_norm_attn_g, m_norm_ffn_g, m_final_norm_g, m_rel_bias, m_a_log, m_dt_bias,
                  m_delta_norm_g, m_conv_w))
    v_small = pk((v_b_ada, v_norm_attn_g, v_norm_ffn_g, v_final_norm_g, v_rel_bias, v_a_log, v_dt_bias,
                  v_delta_norm_g, v_conv_w))
    d_small, m2_small, v2_small = _adamw(w_small, g_small, m_small, v_small, "adamw_small")
    cshape = conv_w.shape
    G, Dl, M2, V2 = (_unpack_small(t, cshape) for t in (g_small, d_small, m2_small, v2_small))

    dmod_all = parts[:, 0:6, :].reshape(N_DEV, 6 * D_MODEL)
    dmod_cols = lax.dynamic_slice(dmod_all, (0, me * ada_sh), (N_DEV, ada_sh))
    g_ada = _ada_wgrad(c_all, dmod_cols)
    d_ada, m2_ada, v2_ada = _adamw(w_ada[0], g_ada, m_w_ada[0], v_w_ada[0], "adamw_w_ada")

    big = {}
    for name, w_, m_, v_ in (("w_in", w_in, m_w_in, v_w_in), ("w_out", w_out, m_w_out, v_w_out),
                             ("w_gate", w_gate, m_w_gate, v_w_gate), ("w_up", w_up, m_w_up, v_w_up),
                             ("w_down", w_down, m_w_down, v_w_down)):
        big[name] = [t[None] for t in _reduce_adamw(gr[name], w_[0], m_[0], v_[0], "reduce_adamw_" + name)]

    def leaf(i, name):
        if name == "w_ada":
            return (g_ada, d_ada, m2_ada, v2_ada)[i][None]
        if name in big:
            return big[name][i]
        return (G, Dl, M2, V2)[i][name]

    order = ["w_ada", "b_ada", "norm_attn_g", "w_in", "rel_bias", "conv_w", "a_log", "dt_bias", "delta_norm_g",
             "w_out", "norm_ffn_g", "w_gate", "w_up", "w_down", "final_norm_g"]
    outs = [loss, gr["x"][None]]
    for i in range(4):
        outs += [leaf(i, n) for n in order]
    return tuple(outs)
```

```python
import functools
import math

import jax
import jax.numpy as jnp
from jax import lax
from jax.experimental import pallas as pl
from jax.experimental.pallas import tpu as pltpu

f32 = jnp.float32
bf16 = jnp.bfloat16

D_MODEL = 1024
HEAD_DIM = 64
N_HEADS = 8
GROUP_W = 512
IN_WIDTH = 3600
IN_PAD = 3840
D_FF = 2816
EPS = 1e-6
NEG_INF = -1e30
BAND = 128
PAD_UNIT = 2048
DILATIONS = (1, 4, 16)
N_BUCKETS = 32
MAX_DISTANCE = 2048
CONV_WIDTH = 4
CHUNK = 64
N_DEV = 8
VMEM_LIMIT = 56 * 1024 * 1024

ADAM_LR, ADAM_B1, ADAM_B2, ADAM_EPS, ADAM_WD, ADAM_STEP = 0.001, 0.9, 0.999, 1e-08, 0.01, 10


def _cparams(sem):
    return pltpu.CompilerParams(dimension_semantics=sem, vmem_limit_bytes=VMEM_LIMIT)


def _mm(a, b, mode, out_dtype, tm, tn, tk, name, xchg=None):
    if mode == "nn":
        (m, k), (_, n) = a.shape, b.shape
        a_spec = pl.BlockSpec((tm, tk), lambda j, i, kk: (i, kk))
        b_spec = pl.BlockSpec((tk, tn), lambda j, i, kk: (kk, j))
        dims = (((1,), (0,)), ((), ()))
    elif mode == "nt":
        (m, k), (n, _) = a.shape, b.shape
        a_spec = pl.BlockSpec((tm, tk), lambda j, i, kk: (i, kk))
        b_spec = pl.BlockSpec((tn, tk), lambda j, i, kk: (j, kk))
        dims = (((1,), (1,)), ((), ()))
    else:
        (k, m), (_, n) = a.shape, b.shape
        a_spec = pl.BlockSpec((tk, tm), lambda j, i, kk: (kk, i))
        b_spec = pl.BlockSpec((tk, tn), lambda j, i, kk: (kk, j))
        dims = (((0,), (0,)), ((), ()))
    assert m % tm == 0 and n % tn == 0 and k % tk == 0, (name, m, n, k, tm, tn, tk)
    nk = k // tk
    grid = (n // tn, m // tm, nk)
    nx = xchg.n if xchg is not None else 0

    def body(*refs):
        a_ref, b_ref = refs[:2]
        o_ref = refs[2 + nx]
        scratch = refs[3 + 2 * nx:]
        if nx:
            xrefs = (refs[2:2 + nx], refs[3 + nx:3 + 2 * nx], scratch[-3:])
            xchg.start_at_first_step(grid, *xrefs)
        if nk == 1:
            o_ref[...] = lax.dot_general(a_ref[...].astype(bf16), b_ref[...].astype(bf16), dims,
                                         preferred_element_type=f32).astype(o_ref.dtype)
        else:
            acc_ref = scratch[0]
            kk = pl.program_id(2)

            @pl.when(kk == 0)
            def _():
                acc_ref[...] = jnp.zeros_like(acc_ref)

            acc_ref[...] += lax.dot_general(a_ref[...].astype(bf16), b_ref[...].astype(bf16), dims,
                                            preferred_element_type=f32)

            @pl.when(kk == nk - 1)
            def _():
                o_ref[...] = acc_ref[...].astype(o_ref.dtype)
        if nx:
            xchg.wait_at_last_step(grid, *xrefs)

    out = pl.pallas_call(
        body, name=name, grid=grid,
        in_specs=[a_spec, b_spec] + ([_ANY] * nx),
        out_specs=[pl.BlockSpec((tm, tn), lambda j, i, kk: (i, j))] + ([_ANY] * nx),
        out_shape=[jax.ShapeDtypeStruct((m, n), out_dtype)] + (xchg.out_shape() if nx else []),
        scratch_shapes=([pltpu.VMEM((tm, tn), f32)] if nk > 1 else []) + (xchg.scratch() if nx else []),
        compiler_params=_cparams(("arbitrary",) * 3 if nx else ("parallel", "parallel", "arbitrary")),
    )(a, b, *(xchg.arrs if nx else []))
    return (out[0], out[1:]) if nx else out[0]


TOK_TILE = 512
SUB_COLS = 384


def _row_spec(width, tile=TOK_TILE):
    return pl.BlockSpec((tile, width), lambda i: (i, 0))


def _vec_spec(width, rows=1):
    return pl.BlockSpec((rows, width), lambda i: (0, 0))


def _ln_mod_fwd(x, gain, sc, sh, shard, name):
    s, d = x.shape
    nt = s // TOK_TILE
    ride = _ChipGather(shard)

    def body(x_ref, g_ref, sc_ref, sh_ref, sh_in, h_ref, sh_out, *sems):
        i = pl.program_id(0)
        pl.when(i == 0)(lambda: ride.start(sh_in, sh_out, sems))
        xv = x_ref[...]
        rstd = lax.rsqrt(jnp.mean(xv * xv, axis=-1, keepdims=True) + EPS)
        h = (xv * rstd) * g_ref[...] * (1.0 + sc_ref[...]) + sh_ref[...]
        h_ref[...] = h.astype(bf16)
        @pl.when(i == nt - 1)
        def _():
            ride.forward(sh_in, sh_out, sems)
            ride.finish(sh_in, sh_out, sems)

    return pl.pallas_call(
        body, name=name, grid=(nt,),
        in_specs=[_row_spec(d), _vec_spec(d), _vec_spec(d), _vec_spec(d), _ANY],
        out_specs=[_row_spec(d), _ANY],
        out_shape=[jax.ShapeDtypeStruct((s, d), bf16), ride.out_shape()],
        scratch_shapes=ride.scratch(),
        compiler_params=_cparams(("arbitrary",)),
    )(x, gain, sc, sh, shard)


def _proj_resid_ln_mod_fwd(pairs, x, gate, gain, sc, sh, name):
    s, d = x.shape
    npair = len(pairs)

    def body(*refs):
        aw = refs[:2 * npair]
        x_ref, gt_ref, g_ref, sc_ref, sh_ref, y_ref, x1_ref, h_ref = refs[2 * npair:]
        y = jnp.dot(aw[0][...].astype(bf16), aw[1][...], preferred_element_type=f32)
        for t in range(1, npair):
            y = y + jnp.dot(aw[2 * t][...].astype(bf16), aw[2 * t + 1][...], preferred_element_type=f32)
        y_ref[...] = y
        x1 = x_ref[...] + gt_ref[...] * y
        x1_ref[...] = x1
        rstd = lax.rsqrt(jnp.mean(x1 * x1, axis=-1, keepdims=True) + EPS)
        h = (x1 * rstd) * g_ref[...] * (1.0 + sc_ref[...]) + sh_ref[...]
        h_ref[...] = h.astype(bf16)

    aw_specs, aw = [], []
    for a, w in pairs:
        aw_specs += [_row_spec(a.shape[1]), pl.BlockSpec(w.shape, lambda i: (0, 0))]
        aw += [a, w]
    return pl.pallas_call(
        body, name=name, grid=(s // TOK_TILE,),
        in_specs=aw_specs + [_row_spec(d)] + [_vec_spec(d)] * 4,
        out_specs=[_row_spec(d)] * 3,
        out_shape=[jax.ShapeDtypeStruct((s, d), f32)] * 2 + [jax.ShapeDtypeStruct((s, d), bf16)],
        compiler_params=_cparams(("parallel",)),
    )(*aw, x, gate, gain, sc, sh)


FFN_TN = 1408


def _ffn_up(h2, w_gate, w_up, name):
    s, d = h2.shape
    tm = 2 * TOK_TILE

    def body(h_ref, wg_ref, wu_ref, a_ref, g_ref, u_ref):
        h = h_ref[...]
        g = jnp.dot(h, wg_ref[...], preferred_element_type=f32)
        u = jnp.dot(h, wu_ref[...], preferred_element_type=f32)
        a_ref[...] = (g * jax.nn.sigmoid(g) * u).astype(bf16)
        g_ref[...] = g.astype(bf16)
        u_ref[...] = u.astype(bf16)

    w_spec = pl.BlockSpec((d, FFN_TN), lambda j, i: (0, j))
    o_spec = pl.BlockSpec((tm, FFN_TN), lambda j, i: (i, j))
    return pl.pallas_call(
        body, name=name, grid=(D_FF // FFN_TN, s // tm),
        in_specs=[pl.BlockSpec((tm, d), lambda j, i: (i, 0)), w_spec, w_spec],
        out_specs=[o_spec] * 3,
        out_shape=[jax.ShapeDtypeStruct((s, D_FF), bf16)] * 3,
        compiler_params=_cparams(("parallel", "parallel")),
    )(h2, w_gate, w_up)


def _ffn_down_dx(dy2, w_down, gate, up, name):
    s, d = dy2.shape
    tm = 2 * TOK_TILE

    def body(dy_ref, w_ref, g_ref, u_ref, dg_ref, du_ref):
        dy = dy_ref[...]
        for c0 in range(0, FFN_TN, SUB_COLS):
            cols = slice(c0, min(c0 + SUB_COLS, FFN_TN))
            da = lax.dot_general(dy, w_ref[cols, :], _NT, preferred_element_type=f32)
            g = g_ref[:, cols].astype(f32)
            sg = jax.nn.sigmoid(g)
            du_ref[:, cols] = (da * g * sg).astype(bf16)
            dg_ref[:, cols] = (da * u_ref[:, cols].astype(f32) * sg * (1.0 + g * (1.0 - sg))).astype(bf16)

    t_spec = pl.BlockSpec((tm, FFN_TN), lambda j, i: (i, j))
    return pl.pallas_call(
        body, name=name, grid=(D_FF // FFN_TN, s // tm),
        in_specs=[pl.BlockSpec((tm, d), lambda j, i: (i, 0)), pl.BlockSpec((FFN_TN, d), lambda j, i: (j, 0)),
                  t_spec, t_spec],
        out_specs=[t_spec, t_spec],
        out_shape=[jax.ShapeDtypeStruct((s, D_FF), bf16)] * 2,
        compiler_params=_cparams(("parallel", "parallel")),
    )(dy2, w_down, gate, up)


def _acc_spec(width):
    return pl.BlockSpec((1, width), lambda i: (0, 0))


def _proj_final_loss_bwd(a, w, x1, gate2, final_g, target, name):
    s, d = x1.shape
    k = a.shape[1]

    def body(a_ref, w_ref, x1_ref, gt_ref, fg_ref, tg_ref, dx2_ref, dy2_ref, loss_ref, dfg_ref, dgt_ref):
        @pl.when(pl.program_id(0) == 0)
        def _():
            loss_ref[...] = jnp.zeros_like(loss_ref)
            dfg_ref[...] = jnp.zeros_like(dfg_ref)
            dgt_ref[...] = jnp.zeros_like(dgt_ref)

        y2 = jnp.dot(a_ref[...], w_ref[...], preferred_element_type=f32)
        gt = gt_ref[...]
        fg = fg_ref[...]
        x2 = x1_ref[...] + gt * y2
        rstd = lax.rsqrt(jnp.mean(x2 * x2, axis=-1, keepdims=True) + EPS)
        xn = x2 * rstd
        err = xn * fg - tg_ref[...]
        row = jnp.sum(err * err, axis=-1, keepdims=True) * (0.5 / d)
        loss_ref[...] += jnp.sum(row, axis=0, keepdims=True) + jnp.zeros_like(loss_ref)
        dout = err * (1.0 / d)
        dfg_ref[...] += jnp.sum(dout * xn, axis=0, keepdims=True)
        dxn = dout * fg
        dx2 = rstd * (dxn - xn * jnp.mean(dxn * xn, axis=-1, keepdims=True))
        dx2_ref[...] = dx2
        dgt_ref[...] += jnp.sum(dx2 * y2, axis=0, keepdims=True)
        dy2_ref[...] = (gt * dx2).astype(bf16)

    return pl.pallas_call(
        body, name=name, grid=(s // TOK_TILE,),
        in_specs=[_row_spec(k), pl.BlockSpec((k, d), lambda i: (0, 0)), _row_spec(d), _vec_spec(d), _vec_spec(d),
                  _row_spec(d)],
        out_specs=[_row_spec(d), _row_spec(d), _acc_spec(128), _acc_spec(d), _acc_spec(d)],
        out_shape=[jax.ShapeDtypeStruct((s, d), f32), jax.ShapeDtypeStruct((s, d), bf16),
                   jax.ShapeDtypeStruct((1, 128), f32), jax.ShapeDtypeStruct((1, d), f32),
                   jax.ShapeDtypeStruct((1, d), f32)],
        compiler_params=_cparams(("arbitrary",)),
    )(a, w, x1, gate2, final_g, target)


def _proj_ln_mod_bwd(pairs, xin, gain, sc, dres, tm, name, xchg, gate=None, y=None):
    s, d = xin.shape
    with_gate = gate is not None
    npair = len(pairs)
    n_in = 2 * npair + (7 if with_gate else 5) - 1
    n_out = 6 if with_gate else 4

    def body(*refs):
        ab = refs[:2 * npair]
        if with_gate:
            (x_ref, g_ref, sc_ref, dr_ref, gt_ref, y_ref,
             dx_ref, dsh_ref, dsc_ref, dg_ref, dy_ref, dgt_ref) = refs[2 * npair:]
        else:
            x_ref, g_ref, sc_ref, dr_ref, dx_ref, dsh_ref, dsc_ref, dg_ref = refs[2 * npair:]

        @pl.when(pl.program_id(0) == 0)
        def _():
            dsh_ref[...] = jnp.zeros_like(dsh_ref)
            dsc_ref[...] = jnp.zeros_like(dsc_ref)
            dg_ref[...] = jnp.zeros_like(dg_ref)
            if with_gate:
                dgt_ref[...] = jnp.zeros_like(dgt_ref)

        dh = lax.dot_general(ab[0][...].astype(bf16), ab[1][...], _NT, preferred_element_type=f32)
        for t in range(1, npair):
            dh = dh + lax.dot_general(ab[2 * t][...].astype(bf16), ab[2 * t + 1][...], _NT,
                                      preferred_element_type=f32)
        xv = x_ref[...]
        g = g_ref[...]
        sc1 = 1.0 + sc_ref[...]
        rstd = lax.rsqrt(jnp.mean(xv * xv, axis=-1, keepdims=True) + EPS)
        xn = xv * rstd
        dsh_ref[...] += jnp.sum(dh, axis=0, keepdims=True)
        dsc_ref[...] += jnp.sum(dh * (xn * g), axis=0, keepdims=True)
        dg_ref[...] += jnp.sum(dh * sc1 * xn, axis=0, keepdims=True)
        dxn = dh * sc1 * g
        dx = dr_ref[...] + rstd * (dxn - xn * jnp.mean(dxn * xn, axis=-1, keepdims=True))
        dx_ref[...] = dx
        if with_gate:
            dgt_ref[...] += jnp.sum(dx * y_ref[...], axis=0, keepdims=True)
            dy_ref[...] = (gt_ref[...] * dx).astype(bf16)

    row = lambda width: pl.BlockSpec((tm, width), lambda i: (i, 0))
    in_specs, args = [], []
    for a, b in pairs:
        in_specs += [row(a.shape[1]), pl.BlockSpec(b.shape, lambda i: (0, 0))]
        args += [a, b]
    in_specs += [row(d), _vec_spec(d), _vec_spec(d), row(d)]
    args += [xin, gain, sc, dres]
    out_specs = [row(d), _acc_spec(d), _acc_spec(d), _acc_spec(d)]
    out_shape = [jax.ShapeDtypeStruct((s, d), f32)] + [jax.ShapeDtypeStruct((1, d), f32)] * 3
    if with_gate:
        in_specs += [_vec_spec(d), row(d)]
        out_specs += [row(d), _acc_spec(d)]
        out_shape += [jax.ShapeDtypeStruct((s, d), bf16), jax.ShapeDtypeStruct((1, d), f32)]
        args += [gate, y]
    grid = (s // tm,)
    out = pl.pallas_call(
        _ride(body, n_in, n_out, xchg, grid), name=name, grid=grid,
        in_specs=in_specs + [_ANY] * xchg.n, out_specs=out_specs + [_ANY] * xchg.n,
        out_shape=out_shape + xchg.out_shape(), scratch_shapes=xchg.scratch(),
        compiler_params=_cparams(("arbitrary",)),
    )(*args, *xchg.arrs)
    return out[:n_out], out[n_out:]


def _bucket_tables():
    import numpy as np
    qi = np.arange(BAND)[:, None]
    kj = np.arange(2 * BAND)[None, :]
    steps = qi + BAND - kj
    max_exact = N_BUCKETS // 2
    out = []
    for d in DILATIONS:
        dist = np.maximum(steps, 0) * d
        dist_f = np.maximum(dist, 1).astype(np.float32)
        large = max_exact + (np.log(dist_f / np.float32(max_exact)) / np.float32(math.log(MAX_DISTANCE / max_exact))
                             * np.float32(N_BUCKETS - max_exact)).astype(np.int32)
        out.append(np.where(dist < max_exact, dist, np.minimum(large, N_BUCKETS - 1)))
    return jnp.asarray(np.stack(out).astype(np.int32))


def _bias_tables(rel_bias, idx):
    def body(idx_ref, rb_ref, o_ref):
        h = pl.program_id(1)
        idxv = idx_ref[0]
        acc = jnp.zeros((BAND, 2 * BAND), f32)
        for b in range(N_BUCKETS):
            acc = jnp.where(idxv == b, rb_ref[b, h], acc)
        o_ref[0, 0] = jnp.where(_attn_masks()[1], acc, NEG_INF)

    return pl.pallas_call(
        body, name="attn_bias_tables", grid=(3, N_HEADS),
        in_specs=[pl.BlockSpec((1, BAND, 2 * BAND), lambda br, h: (br, 0, 0)),
                  pl.BlockSpec(memory_space=pltpu.SMEM)],
        out_specs=pl.BlockSpec((1, 1, BAND, 2 * BAND), lambda br, h: (br, h, 0, 0)),
        out_shape=jax.ShapeDtypeStruct((3, N_HEADS, BAND, 2 * BAND), f32),
        compiler_params=_cparams(("parallel", "parallel")),
    )(idx, rel_bias)


def _bias_grad(dbias, idx):
    def body(idx_ref, db_ref, o_ref):
        br = pl.program_id(1)

        @pl.when(br == 0)
        def _():
            o_ref[...] = jnp.zeros_like(o_ref)

        idxv = idx_ref[0]
        dbv = db_ref[0, 0]
        row = lax.broadcasted_iota(jnp.int32, (N_BUCKETS, 128), 0)
        acc = jnp.zeros((N_BUCKETS, 128), f32)
        for b in range(N_BUCKETS):
            sb = jnp.sum(jnp.sum(jnp.where(idxv == b, dbv, 0.0), axis=1, keepdims=True), axis=0, keepdims=True)
            acc = acc + jnp.where(row == b, sb, 0.0)
        o_ref[0] += acc

    return pl.pallas_call(
        body, name="attn_bias_grad", grid=(N_HEADS, 3),
        in_specs=[pl.BlockSpec((1, BAND, 2 * BAND), lambda h, br: (br, 0, 0)),
                  pl.BlockSpec((1, 1, BAND, 2 * BAND), lambda h, br: (br, h, 0, 0))],
        out_specs=pl.BlockSpec((1, N_BUCKETS, 128), lambda h, br: (h, 0, 0)),
        out_shape=jax.ShapeDtypeStruct((N_HEADS, N_BUCKETS, 128), f32),
        compiler_params=_cparams(("parallel", "arbitrary")),
    )(idx, dbias)


def _attn_masks():
    lane = lax.broadcasted_iota(jnp.int32, (BAND, 128), 1)
    m0 = lane < HEAD_DIM
    qi = lax.broadcasted_iota(jnp.int32, (BAND, 2 * BAND), 0)
    kj = lax.broadcasted_iota(jnp.int32, (BAND, 2 * BAND), 1)
    steps = qi + BAND - kj
    in_window = (steps >= 0) & (steps <= BAND)
    return m0, in_window, kj >= BAND


_NT = (((1,), (1,)), ((), ()))
_TN = (((0,), (0,)), ((), ()))
_BNN = (((2,), (1,)), ((0,), (0,)))
_BNT = (((2,), (2,)), ((0,), (0,)))
_BTN = (((1,), (1,)), ((0,), (0,)))
ATTN_GROUP = 4
ATTN_ITEMS = PAD_UNIT // BAND
Q_COL, K_COL, V_COL = 0, 4, 8


def _attn_item_rows(j, d, c, cbase):
    r = lax.rem(j, d)
    b = lax.div(j, d)
    loc = b * (d * BAND) + r
    first = jnp.logical_and(c == 0, b == 0)
    start = cbase + loc
    pstart = jnp.where(first, start, start - d * BAND)
    return loc, start, pstart, first


def _attn_fwd(proj, bias, shards):
    s = proj.shape[0]
    rides = [_ChipGather(sh) for sh in shards]

    def body(q_ref, k_ref, v_ref, b_ref, y_ref, lse_ref, o_s, l_s):
        c = pl.program_id(1)
        cbase = pl.multiple_of(c * PAD_UNIT, PAD_UNIT)
        m0, in_window, cur_half = _attn_masks()
        for bi, d in enumerate(DILATIONS):
            def group(jg, carry, bi=bi, d=d):
                locs, qs, ks, vs, pens = [], [], [], [], []
                for t in range(ATTN_GROUP):
                    loc, start, pstart, first = _attn_item_rows(jg * ATTN_GROUP + t, d, c, cbase)
                    locs.append(loc)
                    qs.append(q_ref[pl.ds(loc, BAND, stride=d), :])
                    ks.append(jnp.concatenate([k_ref[pl.ds(pstart, BAND, stride=d), :],
                                               k_ref[pl.ds(start, BAND, stride=d), :]], axis=0))
                    vs.append(jnp.concatenate([v_ref[pl.ds(pstart, BAND, stride=d), :],
                                               v_ref[pl.ds(start, BAND, stride=d), :]], axis=0))
                    pens.append(jnp.where(cur_half, 0.0, jnp.where(first, NEG_INF, 0.0)))
                q = jnp.stack(qs)
                kk = jnp.stack(ks + ks).astype(bf16)
                vv = jnp.stack(vs + vs).astype(bf16)
                pen = jnp.stack(pens + pens)
                qh = (jnp.concatenate([jnp.where(m0, q, 0.0), jnp.where(m0, 0.0, q)], axis=0) * 0.125).astype(bf16)
                sc = lax.dot_general(qh, kk, _BNT, preferred_element_type=f32)
                sc = (sc.reshape(2, ATTN_GROUP, BAND, 2 * BAND) + b_ref[bi][:, None]).reshape(sc.shape) + pen
                mx = jnp.max(sc, axis=-1, keepdims=True)
                e = jnp.exp(sc - mx)
                l = jnp.sum(e, axis=-1, keepdims=True)
                o = lax.dot_general(e.astype(bf16), vv, _BNN, preferred_element_type=f32) * (1.0 / l)
                ls = mx + jnp.log(l)
                for t in range(ATTN_GROUP):
                    rows = pl.ds(locs[t], BAND, stride=d)
                    o_s[bi, rows, :] = jnp.where(m0, o[t], o[ATTN_GROUP + t])
                    l_s[bi, rows, :] = jnp.where(m0, ls[t], ls[ATTN_GROUP + t])
                return carry

            lax.fori_loop(0, ATTN_ITEMS // ATTN_GROUP, group, 0)

        def merge(t, carry):
            rows = pl.ds(pl.multiple_of(t * 256, 256), 256)
            ls = [l_s[i, rows, :] for i in range(3)]
            mx = jnp.maximum(jnp.maximum(ls[0], ls[1]), ls[2])
            ws = [jnp.exp(l - mx) for l in ls]
            tot = ws[0] + ws[1] + ws[2]
            y = (ws[0] * o_s[0, rows, :] + ws[1] * o_s[1, rows, :] + ws[2] * o_s[2, rows, :]) / tot
            y_ref[rows, :] = y
            lse_ref[rows, :] = mx + jnp.log(tot)
            return carry

        lax.fori_loop(0, PAD_UNIT // 256, merge, 0)

    chunk = lambda col: pl.BlockSpec((PAD_UNIT, 128), lambda p, c: (c, col + p))
    full = lambda col: pl.BlockSpec((s, 128), lambda p, c: (0, col + p))
    grid = (N_HEADS // 2, s // PAD_UNIT)
    nsteps = grid[0] * grid[1]
    out = pl.pallas_call(
        _ride_gathers(body, 4, 2, rides, grid, (3 * nsteps) // 4), name="attn_fwd", grid=grid,
        in_specs=[chunk(Q_COL), full(K_COL), full(V_COL),
                  pl.BlockSpec((3, 2, BAND, 2 * BAND), lambda p, c: (0, p, 0, 0))] + [_ANY] * len(rides),
        out_specs=[chunk(0), chunk(0)] + [_ANY] * len(rides),
        out_shape=[jax.ShapeDtypeStruct((s, GROUP_W), f32)] * 2 + [r.out_shape() for r in rides],
        scratch_shapes=[pltpu.VMEM((3, PAD_UNIT, 128), f32)] * 2 + [t for r in rides for t in r.scratch()],
        compiler_params=_cparams(("arbitrary", "arbitrary")),
    )(proj, proj, proj, bias, *shards)
    return out[:2], out[2:]


def _attn_bwd(proj, bias, y, lse, dycat):
    s = proj.shape[0]

    def body(q_ref, k_ref, v_ref, b_ref, y_ref, lse_ref, dy_ref, dq_ref, dk_ref, dv_ref, db_ref, dd_s):
        c = pl.program_id(1)
        cbase = pl.multiple_of(c * PAD_UNIT, PAD_UNIT)
        m0, in_window, cur_half = _attn_masks()

        @pl.when(c == 0)
        def _():
            dk_ref[...] = jnp.zeros_like(dk_ref)
            dv_ref[...] = jnp.zeros_like(dv_ref)
            db_ref[...] = jnp.zeros_like(db_ref)

        dq_ref[...] = jnp.zeros_like(dq_ref)

        def rowdot(t, carry):
            rows = pl.ds(pl.multiple_of(t * 256, 256), 256)
            prod = dy_ref[rows, :] * y_ref[rows, :]
            lane = lax.broadcasted_iota(jnp.int32, prod.shape, 1)
            h0 = lane < HEAD_DIM
            d0 = jnp.sum(jnp.where(h0, prod, 0.0), axis=-1, keepdims=True)
            d1 = jnp.sum(jnp.where(h0, 0.0, prod), axis=-1, keepdims=True)
            dd_s[rows, :] = jnp.where(h0, d0, d1)
            return carry

        lax.fori_loop(0, PAD_UNIT // 256, rowdot, 0)

        for bi, d in enumerate(DILATIONS):
            def group(jg, carry, bi=bi, d=d):
                ng = ATTN_GROUP
                meta, qs, dos, lqs, dds, ks, vs, pens = [], [], [], [], [], [], [], []
                for t in range(ng):
                    loc, start, pstart, first = _attn_item_rows(jg * ng + t, d, c, cbase)
                    qrows = pl.ds(loc, BAND, stride=d)
                    rows = pl.ds(start, BAND, stride=d)
                    prows = pl.ds(pstart, BAND, stride=d)
                    meta.append((qrows, rows, prows))
                    qs.append(q_ref[qrows, :])
                    dos.append(dy_ref[qrows, :])
                    lqs.append(lse_ref[qrows, :])
                    dds.append(dd_s[qrows, :])
                    ks.append(jnp.concatenate([k_ref[prows, :], k_ref[rows, :]], axis=0))
                    vs.append(jnp.concatenate([v_ref[prows, :], v_ref[rows, :]], axis=0))
                    pens.append(jnp.where(cur_half, 0.0, jnp.where(first, NEG_INF, 0.0)))

                def heads(t):
                    return jnp.concatenate([jnp.where(m0, t, 0.0), jnp.where(m0, 0.0, t)], axis=0)

                def head_col(t):
                    return jnp.concatenate([t[:, :, 0:1], t[:, :, HEAD_DIM:HEAD_DIM + 1]], axis=0)

                qh = (heads(jnp.stack(qs)) * 0.125).astype(bf16)
                doh = heads(jnp.stack(dos)).astype(bf16)
                kk = jnp.stack(ks + ks).astype(bf16)
                vv = jnp.stack(vs + vs).astype(bf16)
                sc = lax.dot_general(qh, kk, _BNT, preferred_element_type=f32)
                sc = (sc.reshape(2, ng, BAND, 2 * BAND) + b_ref[bi][:, None]).reshape(sc.shape) + jnp.stack(pens + pens)
                p = jnp.exp(sc - head_col(jnp.stack(lqs)))
                dp = lax.dot_general(doh, vv, _BNT, preferred_element_type=f32)
                ds = p * (dp - head_col(jnp.stack(dds)))
                db_ref[bi] += jnp.sum(ds.reshape(2, ng, BAND, 2 * BAND), axis=1)
                dsb = ds.astype(bf16)
                dq = lax.dot_general(dsb, kk, _BNN, preferred_element_type=f32) * 0.125
                dk = lax.dot_general(dsb, qh, _BTN, preferred_element_type=f32)
                dv = lax.dot_general(p.astype(bf16), doh, _BTN, preferred_element_type=f32)
                for t in range(ng):
                    qrows, rows, prows = meta[t]
                    dq_ref[qrows, :] += jnp.where(m0, dq[t], dq[ng + t])
                    dkt = dk[t] + dk[ng + t]
                    dvt = dv[t] + dv[ng + t]
                    dk_ref[prows, :] += dkt[:BAND]
                    dk_ref[rows, :] += dkt[BAND:]
                    dv_ref[prows, :] += dvt[:BAND]
                    dv_ref[rows, :] += dvt[BAND:]
                return carry

            lax.fori_loop(0, ATTN_ITEMS // ATTN_GROUP, group, 0)

    chunk = lambda col: pl.BlockSpec((PAD_UNIT, 128), lambda p, c: (c, col + p))
    full = lambda col: pl.BlockSpec((s, 128), lambda p, c: (0, col + p))
    bias_spec = pl.BlockSpec((3, 2, BAND, 2 * BAND), lambda p, c: (0, p, 0, 0))
    return pl.pallas_call(
        body, name="attn_bwd", grid=(N_HEADS // 2, s // PAD_UNIT),
        in_specs=[chunk(Q_COL), full(K_COL), full(V_COL), bias_spec, chunk(0), chunk(0), chunk(0)],
        out_specs=[chunk(0), full(0), full(0), bias_spec],
        out_shape=[jax.ShapeDtypeStruct((s, GROUP_W), f32)] * 3
        + [jax.ShapeDtypeStruct((3, N_HEADS, BAND, 2 * BAND), f32)],
        scratch_shapes=[pltpu.VMEM((PAD_UNIT, 128), f32)],
        compiler_params=_cparams(("parallel", "arbitrary")),
    )(proj, proj, proj, bias, y, lse, dycat)


_HI = lax.Precision.HIGHEST
DELTA_COL = 1536
Z_COL = 3072
BA_BLOCK = 28
DELTA_ROWS = 1024


def _hdot(a, b):
    return jnp.dot(a, b, precision=_HI, preferred_element_type=f32)


_DIMS = dict(nn=(((2,), (1,)), ((0,), (0,))), nt=(((2,), (2,)), ((0,), (0,))), tn=(((1,), (1,)), ((0,), (0,))))


@functools.partial(jax.custom_vjp, nondiff_argnums=(2,))
def _mmx(a, b, mode):
    return lax.dot_general(a.astype(bf16), b.astype(bf16), _DIMS[mode], preferred_element_type=f32)


def _mmx_fwd(a, b, mode):
    return _mmx(a, b, mode), (a, b)


def _mmx_bwd(mode, res, g):
    a, b = res
    if mode == "nn":
        return _mmx(g, b, "nt"), _mmx(a, g, "tn")
    if mode == "nt":
        return _mmx(g, b, "nn"), _mmx(g, a, "tn")
    return _mmx(b, g, "nt"), _mmx(a, g, "nn")


_mmx.defvjp(_mmx_fwd, _mmx_bwd)


def _pair_iota():
    row = lax.broadcasted_iota(jnp.int32, (CHUNK, 128), 0)
    lane = lax.broadcasted_iota(jnp.int32, (CHUNK, 128), 1)
    return row, lane, lane & (CHUNK - 1)


def _bd(x):
    _, lane, _ = _pair_iota()
    m0 = lane < CHUNK
    return jnp.concatenate([jnp.where(m0, x, 0.0), jnp.where(m0, 0.0, x)], axis=1)


def _pmm(a, b):
    return _mmx(a, _bd(b), "nn")


def _ntp(x, y):
    return _mmx(x, _bd(y), "nt")


def _tnp(x, y):
    full = _mmx(x, y, "tn")
    _, lane, _ = _pair_iota()
    return jnp.where(lane < CHUNK, full[:, :CHUNK], full[:, CHUNK:])


def _tri_inv(a):
    row, lane, jj = _pair_iota()
    eye = jnp.where(row == jj, 1.0, 0.0).astype(f32)

    def same_block(log2b):
        return (row >> log2b) == (jj >> log2b)

    dgl = jnp.where(same_block(3), a, 0.0)
    d2 = _pmm(dgl, dgl)
    d4 = _pmm(d2, d2)
    t = _pmm(_pmm(eye - dgl, eye + d2), eye + d4)
    for lb in (3, 4, 5):
        off = jnp.where(same_block(lb + 1) & jnp.logical_not(same_block(lb)), a, 0.0)
        t = t - _pmm(_pmm(t, off), t)
    return t


@jax.custom_vjp
def _solve2(a, xv, xk, t):
    return _pmm(t, xv), _pmm(t, xk)


def _solve2_fwd(a, xv, xk, t):
    u, w = _pmm(t, xv), _pmm(t, xk)
    return (u, w), (t, u, w)


def _solve2_bwd(res, cts):
    t, u, w = res
    du, dw = cts
    dxv = _tnp(t, du)
    dxk = _tnp(t, dw)
    return -(_ntp(dxv, u) + _ntp(dxk, w)), dxv, dxk, jnp.zeros_like(t)


_solve2.defvjp(_solve2_fwd, _solve2_bwd)


def _chunk_pre(qp, kp, vp, bp, gcum, t=None):
    row, lane, jj = _pair_iota()
    causal = row >= jj
    strict = row > jj
    rsel = jnp.sum(jnp.where(row == jj, gcum, 0.0), axis=1, keepdims=True)
    decay = jnp.where(causal, jnp.exp(jnp.where(causal, gcum - rsel, 0.0)), 0.0)
    kb = kp * bp
    kd = _bd(kp)
    a = jnp.where(strict, _mmx(kb, kd, "nt") * decay, 0.0)
    eg = jnp.exp(gcum)
    if t is None:
        t = _tri_inv(a)
    u, w = _solve2(a, vp * bp, kb * eg, t)
    qk = jnp.where(causal, _mmx(qp, kd, "nt") * decay, 0.0)
    glast = jnp.sum(jnp.where(row == CHUNK - 1, gcum, 0.0), axis=1, keepdims=True)
    return u, w, qp * eg, kp * jnp.exp(glast - gcum), qk, jnp.exp(glast), t


def _chunk_post(u, w, qt, kh, qk, gam, sp):
    sd = _bd(sp)
    vnew = u - _mmx(w, sd, "nn")
    o = _mmx(qt, sd, "nn") + _pmm(qk, vnew)
    return o, gam * sp + _tnp(kh, vnew)


def _pair_spec(rows=DELTA_ROWS):
    return pl.BlockSpec((rows, 128), lambda i, p: (i, p))


DELTA_NB = DELTA_ROWS // CHUNK


def _chunks(ref):
    return ref[...].reshape(DELTA_NB, CHUNK, 128)


def _pairs(ref, rows):
    return jnp.stack([ref[rows, p * 128:(p + 1) * 128] for p in range(4)], axis=0)


def _delta_chunk_pre(qn, kn, sv, beta, g, xchg):
    s = qn.shape[0]

    def body(q_ref, k_ref, v_ref, b_ref, g_ref, u_ref, w_ref, qt_ref, kh_ref, qk_ref, t_ref, gm_ref):
        outs = _chunk_pre(_chunks(q_ref), _chunks(k_ref), _chunks(v_ref), _chunks(b_ref), _chunks(g_ref))
        for ref, val in zip((u_ref, w_ref, qt_ref, kh_ref, qk_ref, t_ref), outs[:5] + outs[6:]):
            ref[...] = val.reshape(DELTA_ROWS, 128).astype(ref.dtype)
        gm_ref[...] = jnp.broadcast_to(outs[5], (DELTA_NB, 8, 128)).reshape(DELTA_NB * 8, 128)

    v_spec = pl.BlockSpec((DELTA_ROWS, 128), lambda i, p: (i, 8 + p))
    grid = (s // DELTA_ROWS, 4)
    out = pl.pallas_call(
        _ride(body, 5, 7, xchg, grid), name="delta_chunk_pre", grid=grid,
        in_specs=[_pair_spec(), _pair_spec(), v_spec, _pair_spec(), _pair_spec()] + [_ANY] * xchg.n,
        out_specs=[_pair_spec()] * 6 + [_pair_spec(DELTA_NB * 8)] + [_ANY] * xchg.n,
        out_shape=[jax.ShapeDtypeStruct((s, GROUP_W), f32)] + [jax.ShapeDtypeStruct((s, GROUP_W), bf16)] * 5
        + [jax.ShapeDtypeStruct((s // 8, GROUP_W), f32)] + xchg.out_shape(),
        scratch_shapes=xchg.scratch(),
        compiler_params=_cparams(("arbitrary", "arbitrary")),
    )(qn, kn, sv, beta, g, *xchg.arrs)
    return out[:7], out[7:]


def _delta_scan_fwd(u, w, qt, kh, qk, gm):
    s = u.shape[0]

    def body(u_ref, w_ref, qt_ref, kh_ref, qk_ref, gm_ref, o_ref, ss_ref, st):
        @pl.when(pl.program_id(0) == 0)
        def _():
            st[...] = jnp.zeros_like(st)

        def chunk(ci, carry):
            rows = pl.ds(pl.multiple_of(ci * CHUNK, CHUNK), CHUNK)
            grow = pl.ds(pl.multiple_of(ci * 8, 8), 1)
            sp = st[...]
            o, s2 = _chunk_post(_pairs(u_ref, rows), _pairs(w_ref, rows), _pairs(qt_ref, rows),
                                _pairs(kh_ref, rows), _pairs(qk_ref, rows), _pairs(gm_ref, grow), sp)
            for p in range(4):
                ss_ref[rows, p * 128:(p + 1) * 128] = sp[p]
                o_ref[rows, p * 128:(p + 1) * 128] = o[p]
            st[...] = s2
            return carry

        lax.fori_loop(0, DELTA_NB, chunk, 0)

    spec = pl.BlockSpec((DELTA_ROWS, GROUP_W), lambda i: (i, 0))
    gspec = pl.BlockSpec((DELTA_NB * 8, GROUP_W), lambda i: (i, 0))
    return pl.pallas_call(
        body, name="delta_scan_fwd", grid=(s // DELTA_ROWS,),
        in_specs=[spec] * 5 + [gspec],
        out_specs=[spec, spec],
        out_shape=[jax.ShapeDtypeStruct((s, GROUP_W), f32)] * 2,
        scratch_shapes=[pltpu.VMEM((4, CHUNK, 128), f32)],
        compiler_params=_cparams(("arbitrary",)),
    )(u, w, qt, kh, qk, gm)


def _delta_scan_bwd(w, qt, kh, qk, gm, do, xchg):
    s = w.shape[0]
    nb = s // DELTA_ROWS

    def body(w_ref, qt_ref, kh_ref, qk_ref, gm_ref, do_ref, dso_ref, dst):
        @pl.when(pl.program_id(0) == 0)
        def _():
            dst[...] = jnp.zeros_like(dst)

        def chunk(t, carry):
            ci = DELTA_NB - 1 - t
            rows = pl.ds(pl.multiple_of(ci * CHUNK, CHUNK), CHUNK)
            grow = pl.ds(pl.multiple_of(ci * 8, 8), 1)
            ds = dst[...]
            for p in range(4):
                dso_ref[rows, p * 128:(p + 1) * 128] = ds[p]
            do = _pairs(do_ref, rows)
            dvn = _tnp(_pairs(qk_ref, rows), do) + _pmm(_pairs(kh_ref, rows), ds)
            dst[...] = _tnp(_pairs(qt_ref, rows), do) + _pairs(gm_ref, grow) * ds - _tnp(_pairs(w_ref, rows), dvn)
            return carry

        lax.fori_loop(0, DELTA_NB, chunk, 0)

    spec = pl.BlockSpec((DELTA_ROWS, GROUP_W), lambda i: (nb - 1 - i, 0))
    gspec = pl.BlockSpec((DELTA_NB * 8, GROUP_W), lambda i: (nb - 1 - i, 0))
    out = pl.pallas_call(
        _ride(body, 6, 1, xchg, (nb,)), name="delta_scan_bwd", grid=(nb,),
        in_specs=[spec] * 4 + [gspec, spec] + [_ANY] * xchg.n,
        out_specs=[spec] + [_ANY] * xchg.n,
        out_shape=[jax.ShapeDtypeStruct((s, GROUP_W), f32)] + xchg.out_shape(),
        scratch_shapes=[pltpu.VMEM((4, CHUNK, 128), f32)] + xchg.scratch(),
        compiler_params=_cparams(("arbitrary",)),
    )(w, qt, kh, qk, gm, do, *xchg.arrs)
    return out[0], out[1:]


def _delta_chunk_bwd(qn, kn, sv, beta, g, tinv, ss, dso, do, xchg):
    s = qn.shape[0]

    def body(q_ref, k_ref, v_ref, b_ref, g_ref, t_ref, ss_ref, dso_ref, do_ref,
             dq_ref, dk_ref, dv_ref, db_ref, dg_ref):
        sp = _chunks(ss_ref)
        t = _chunks(t_ref)

        def fn(q, k, v, b, gg):
            return _chunk_post(*_chunk_pre(q, k, v, b, gg, t)[:6], sp)

        _, vjp = jax.vjp(fn, _chunks(q_ref), _chunks(k_ref), _chunks(v_ref), _chunks(b_ref), _chunks(g_ref))
        grads = vjp((_chunks(do_ref), _chunks(dso_ref)))
        for ref, val in zip((dq_ref, dk_ref, dv_ref, db_ref, dg_ref), grads):
            ref[...] = val.reshape(DELTA_ROWS, 128)

    v_spec = pl.BlockSpec((DELTA_ROWS, 128), lambda i, p: (i, 8 + p))
    grid = (s // DELTA_ROWS, 4)
    out = pl.pallas_call(
        _ride(body, 9, 5, xchg, grid), name="delta_chunk_bwd", grid=grid,
        in_specs=[_pair_spec(), _pair_spec(), v_spec] + [_pair_spec()] * 6 + [_ANY] * xchg.n,
        out_specs=[_pair_spec()] * 5 + [_ANY] * xchg.n,
        out_shape=[jax.ShapeDtypeStruct((s, GROUP_W), f32)] * 5 + xchg.out_shape(),
        scratch_shapes=xchg.scratch(),
        compiler_params=_cparams(("arbitrary", "arbitrary")),
    )(qn, kn, sv, beta, g, tinv, ss, dso, do, *xchg.arrs)
    return out[:5], out[5:]


def _head_sums(x):
    r = lax.broadcasted_iota(jnp.int32, (128, 128), 0)
    c = lax.broadcasted_iota(jnp.int32, (128, 128), 1)
    pair = jnp.where((r >> 6) == (c >> 6), 1.0, 0.0).astype(f32)
    npair = x.shape[1] // 128
    xb = jnp.concatenate([x[None, :, p * 128:(p + 1) * 128] for p in range(npair)], axis=0)
    sums = _mmx(xb, jnp.broadcast_to(pair, (npair, 128, 128)), "nn")
    return jnp.concatenate([sums[p] for p in range(npair)], axis=1)


def _sel_dot(a, b):
    return jnp.dot(a, b, precision=lax.Precision.HIGH, preferred_element_type=f32)


def _expand_matrix(first):
    r = lax.broadcasted_iota(jnp.int32, (128, GROUP_W), 0)
    c = lax.broadcasted_iota(jnp.int32, (128, GROUP_W), 1) >> 6
    return jnp.where(r == c + first, 1.0, 0.0).astype(f32)


@functools.partial(jax.custom_vjp, nondiff_argnums=(1,))
def _expand_heads(ba, first):
    return _sel_dot(ba, _expand_matrix(first))


def _expand_heads_fwd(ba, first):
    return _expand_heads(ba, first), None


def _expand_heads_bwd(first, _, g):
    return (_mmx(g[None], _expand_matrix(first)[None], "nt")[0],)


_expand_heads.defvjp(_expand_heads_fwd, _expand_heads_bwd)


def _softplus(x):
    return jnp.maximum(x, 0.0) + jnp.log(1.0 + jnp.exp(-jnp.abs(x)))


def _prep_fn(sq, sk, ba, alog_e, dt_e):
    qn = sq * lax.rsqrt(_head_sums(sq * sq) + EPS) * (HEAD_DIM ** -0.5)
    kn = sk * lax.rsqrt(_head_sums(sk * sk) + EPS)
    bl = _expand_heads(ba, 0)
    al = _expand_heads(ba, N_HEADS)
    beta = jax.nn.sigmoid(bl)
    g = -jnp.exp(alog_e) * _softplus(al + dt_e)
    nchunk = g.shape[0] // CHUNK
    ri = lax.broadcasted_iota(jnp.int32, (nchunk, CHUNK, CHUNK), 1)
    ci = lax.broadcasted_iota(jnp.int32, (nchunk, CHUNK, CHUNK), 2)
    tril = jnp.where(ri >= ci, 1.0, 0.0).astype(f32)
    gcum = lax.dot_general(tril, g.reshape(nchunk, CHUNK, g.shape[1]), _BNN, precision=lax.Precision.HIGH,
                           preferred_element_type=f32)
    return qn, kn, beta, gcum.reshape(g.shape)


def _gnorm_fn(o, z, ng_e):
    ms = _head_sums(o * o) * (1.0 / HEAD_DIM)
    return o * lax.rsqrt(ms + EPS) * ng_e * (z * jax.nn.sigmoid(z))


def _tok_spec(width, col):
    return pl.BlockSpec((TOK_TILE, width), lambda i: (i, col))


def _conv_taps(xs_ref, w_ref, base, n, cols):
    acc = w_ref[CONV_WIDTH - 1:CONV_WIDTH, cols] * xs_ref[pl.ds(base, n), cols]
    for j in range(CONV_WIDTH - 1):
        acc = acc + w_ref[j:j + 1, cols] * xs_ref[pl.ds(base - (CONV_WIDTH - 1) + j, n), cols]
    return acc


def _conv_silu_fwd(proj, conv_w):
    s = proj.shape[0]
    wd = 3 * GROUP_W
    hb = TOK_TILE // 8

    def body(x_ref, halo_ref, w_ref, o_ref, y_ref, xs):
        inner = pl.program_id(0) > 0

        def lane_block(cb, carry):
            cols = pl.ds(pl.multiple_of(cb * 128, 128), 128)
            xs[0:8, cols] = jnp.where(inner, halo_ref[:, cols], 0.0)
            xs[8:, cols] = x_ref[:, cols]
            y = _conv_taps(xs, w_ref, 8, TOK_TILE, cols)
            y_ref[:, cols] = y
            o_ref[:, cols] = y * jax.nn.sigmoid(y)
            return carry

        lax.fori_loop(0, wd // 128, lane_block, 0)

    return pl.pallas_call(
        body, name="delta_conv_fwd", grid=(s // TOK_TILE,),
        in_specs=[_tok_spec(wd, 1), pl.BlockSpec((8, wd), lambda i: (jnp.maximum(i * hb - 1, 0), 1)),
                  pl.BlockSpec((CONV_WIDTH, wd), lambda i: (0, 0))],
        out_specs=[_tok_spec(wd, 0)] * 2,
        out_shape=[jax.ShapeDtypeStruct((s, wd), f32)] * 2,
        scratch_shapes=[pltpu.VMEM((TOK_TILE + 8, wd), f32)],
        compiler_params=_cparams(("parallel",)),
    )(proj, proj, conv_w)


def _conv_silu_bwd(proj, conv_w, yc, ds3, xchg):
    s = proj.shape[0]
    wd = 3 * GROUP_W
    hb = TOK_TILE // 8
    nt = s // TOK_TILE

    def body(x_ref, hp_ref, y_ref, yn_ref, dq_ref, dk_ref, dv_ref, dqn_ref, dkn_ref, dvn_ref, w_ref,
             dx_ref, dw_ref, xs, dys):
        i = pl.program_id(0)

        @pl.when(i == 0)
        def _():
            dw_ref[...] = jnp.zeros_like(dw_ref)

        last = i == nt - 1
        def lane_block(lb, carry, third, cur, nxt):
            tcols = pl.ds(pl.multiple_of(lb * 128, 128), 128)
            cols = pl.ds(pl.multiple_of(third * GROUP_W + lb * 128, 128), 128)
            xs[0:8, cols] = jnp.where(i > 0, hp_ref[:, cols], 0.0)
            xs[8:, cols] = x_ref[:, cols]
            y = y_ref[:, cols]
            sg = jax.nn.sigmoid(y)
            dy0 = cur[:, tcols] * (sg * (1.0 + y * (1.0 - sg)))
            dys[0:TOK_TILE, cols] = dy0
            yn = yn_ref[:, cols]
            sgn = jax.nn.sigmoid(yn)
            dys[TOK_TILE:, cols] = jnp.where(last, 0.0, nxt[:, tcols]) * (sgn * (1.0 + yn * (1.0 - sgn)))
            dx = w_ref[CONV_WIDTH - 1:CONV_WIDTH, cols] * dy0
            for j in range(CONV_WIDTH - 1):
                dx = dx + w_ref[j:j + 1, cols] * dys[pl.ds(CONV_WIDTH - 1 - j, TOK_TILE), cols]
            dx_ref[:, cols] = dx.astype(dx_ref.dtype)
            for j in range(CONV_WIDTH):
                dw_ref[j:j + 1, cols] += jnp.sum(dy0 * xs[pl.ds(8 - (CONV_WIDTH - 1) + j, TOK_TILE), cols],
                                                 axis=0, keepdims=True)
            return carry

        for third, (cur, nxt) in enumerate(((dq_ref, dqn_ref), (dk_ref, dkn_ref), (dv_ref, dvn_ref))):
            lax.fori_loop(0, GROUP_W // 128, functools.partial(lane_block, third=third, cur=cur, nxt=nxt), 0)

    prev8 = lambda col: pl.BlockSpec((8, wd), lambda i: (jnp.maximum(i * hb - 1, 0), col))
    next8 = lambda col: pl.BlockSpec((8, wd), lambda i: (jnp.minimum((i + 1) * hb, s // 8 - 1), col))
    next8_third = pl.BlockSpec((8, GROUP_W), lambda i: (jnp.minimum((i + 1) * hb, s // 8 - 1), 0))
    out = pl.pallas_call(
        _ride(body, 11, 2, xchg, (nt,)), name="delta_conv_bwd", grid=(nt,),
        in_specs=[_tok_spec(wd, 1), prev8(1), _tok_spec(wd, 0), next8(0)] + [_tok_spec(GROUP_W, 0)] * 3
        + [next8_third] * 3
        + [pl.BlockSpec((CONV_WIDTH, wd), lambda i: (0, 0))] + [_ANY] * xchg.n,
        out_specs=[_tok_spec(wd, 0), pl.BlockSpec((CONV_WIDTH, wd), lambda i: (0, 0))] + [_ANY] * xchg.n,
        out_shape=[jax.ShapeDtypeStruct((s, wd), bf16), jax.ShapeDtypeStruct((CONV_WIDTH, wd), f32)] + xchg.out_shape(),
        scratch_shapes=[pltpu.VMEM((TOK_TILE + 8, wd), f32), pltpu.VMEM((TOK_TILE + 8, wd), f32)] + xchg.scratch(),
        compiler_params=_cparams(("arbitrary",)),
    )(proj, proj, yc, yc, *ds3, *ds3, conv_w, *xchg.arrs)
    return out[:2], out[2:]


def _delta_prep_fwd(sconv, proj, alog_e, dt_e):
    s = sconv.shape[0]

    def body(sq_ref, sk_ref, ba_ref, al_ref, dt_ref, q_ref, k_ref, b_ref, g_ref):
        qn, kn, beta, g = _prep_fn(sq_ref[...], sk_ref[...], ba_ref[...], al_ref[...], dt_ref[...])
        q_ref[...] = qn
        k_ref[...] = kn
        b_ref[...] = beta
        g_ref[...] = g

    return pl.pallas_call(
        body, name="delta_prep_fwd", grid=(s // TOK_TILE,),
        in_specs=[_tok_spec(GROUP_W, 0), _tok_spec(GROUP_W, 1), _tok_spec(128, BA_BLOCK),
                  _vec_spec(GROUP_W), _vec_spec(GROUP_W)],
        out_specs=[_tok_spec(GROUP_W, 0)] * 4,
        out_shape=[jax.ShapeDtypeStruct((s, GROUP_W), f32)] * 4,
        compiler_params=_cparams(("parallel",)),
    )(sconv, sconv, proj, alog_e, dt_e)


def _delta_prep_bwd(sconv, proj, alog_e, dt_e, dqn, dkn, dbeta, dg, xchg):
    s = sconv.shape[0]
    grid = (s // TOK_TILE,)

    def body(sq_ref, sk_ref, ba_ref, al_ref, dt_ref, dq_ref, dk_ref, db_ref, dg_ref,
             dsq_ref, dsk_ref, dba_ref, dal_ref, ddt_ref):
        @pl.when(pl.program_id(0) == 0)
        def _():
            dal_ref[...] = jnp.zeros_like(dal_ref)
            ddt_ref[...] = jnp.zeros_like(ddt_ref)

        _, vjp = jax.vjp(_prep_fn, sq_ref[...], sk_ref[...], ba_ref[...], al_ref[...], dt_ref[...])
        dsq, dsk, dba, dal, ddt = vjp((dq_ref[...], dk_ref[...], db_ref[...], dg_ref[...]))
        dsq_ref[...] = dsq
        dsk_ref[...] = dsk
        dba_ref[...] = dba.astype(bf16)
        dal_ref[...] += dal
        ddt_ref[...] += ddt

    out = pl.pallas_call(
        _ride(body, 9, 5, xchg, grid), name="delta_prep_bwd", grid=grid,
        in_specs=[_tok_spec(GROUP_W, 0), _tok_spec(GROUP_W, 1), _tok_spec(128, BA_BLOCK),
                  _vec_spec(GROUP_W), _vec_spec(GROUP_W)] + [_tok_spec(GROUP_W, 0)] * 4 + [_ANY] * xchg.n,
        out_specs=[_tok_spec(GROUP_W, 0), _tok_spec(GROUP_W, 0), _tok_spec(128, 0),
                   _acc_spec(GROUP_W), _acc_spec(GROUP_W)] + [_ANY] * xchg.n,
        out_shape=[jax.ShapeDtypeStruct((s, GROUP_W), f32)] * 2 + [jax.ShapeDtypeStruct((s, 128), bf16)]
        + [jax.ShapeDtypeStruct((1, GROUP_W), f32)] * 2 + xchg.out_shape(),
        scratch_shapes=xchg.scratch(),
        compiler_params=_cparams(("arbitrary",)),
    )(sconv, sconv, proj, alog_e, dt_e, dqn, dkn, dbeta, dg, *xchg.arrs)
    return out[:5], out[5:]


def _gnorm_fwd(o, proj, ng_e):
    s = o.shape[0]

    def body(o_ref, z_ref, g_ref, y_ref):
        y_ref[...] = _gnorm_fn(o_ref[...], z_ref[...], g_ref[...])

    return pl.pallas_call(
        body, name="delta_gnorm_fwd", grid=(s // TOK_TILE,),
        in_specs=[_tok_spec(GROUP_W, 0), _tok_spec(GROUP_W, Z_COL // GROUP_W), _vec_spec(GROUP_W)],
        out_specs=_tok_spec(GROUP_W, 0),
        out_shape=jax.ShapeDtypeStruct((s, GROUP_W), f32),
        compiler_params=_cparams(("parallel",)),
    )(o, proj, ng_e)


def _gnorm_bwd(o, proj, ng_e, dycat):
    s = o.shape[0]

    def body(o_ref, z_ref, g_ref, dy_ref, do_ref, dz_ref, dg_ref):
        @pl.when(pl.program_id(0) == 0)
        def _():
            dg_ref[...] = jnp.zeros_like(dg_ref)

        _, vjp = jax.vjp(_gnorm_fn, o_ref[...], z_ref[...], g_ref[...])
        do, dz, dg = vjp(dy_ref[...])
        do_ref[...] = do
        dz_ref[...] = dz.astype(bf16)
        dg_ref[...] += dg

    return pl.pallas_call(
        body, name="delta_gnorm_bwd", grid=(s // TOK_TILE,),
        in_specs=[_tok_spec(GROUP_W, 0), _tok_spec(GROUP_W, Z_COL // GROUP_W), _vec_spec(GROUP_W),
                  _tok_spec(GROUP_W, 1)],
        out_specs=[_tok_spec(GROUP_W, 0), _tok_spec(GROUP_W, 0), _acc_spec(GROUP_W)],
        out_shape=[jax.ShapeDtypeStruct((s, GROUP_W), f32), jax.ShapeDtypeStruct((s, GROUP_W), bf16),
                   jax.ShapeDtypeStruct((1, GROUP_W), f32)],
        compiler_params=_cparams(("arbitrary",)),
    )(o, proj, ng_e, dycat)


_MESH = pl.DeviceIdType.MESH
_ANY = pl.BlockSpec(memory_space=pl.ANY)
_VMEM = pl.BlockSpec(memory_space=pltpu.VMEM)


def _my_place():
    x, y, c = lax.axis_index("x"), lax.axis_index("y"), lax.axis_index("c")
    return x, y, c, 4 * x + 2 * y + c


def _peer(k, x, y, c):
    px = 1 - x if k & 4 else x
    py = 1 - y if k & 2 else y
    pc = 1 - c if k & 1 else c
    return (px, py, pc), 4 * px + 2 * py + pc


def _exchange_all(src_of_peer, dst_ref, send_sems, recv_sems, x, y, c, me):
    sent = []
    for k in range(1, N_DEV):
        dev, pidx = _peer(k, x, y, c)
        cp = pltpu.make_async_remote_copy(src_ref=src_of_peer(pidx), dst_ref=dst_ref.at[me],
                                          send_sem=send_sems.at[k - 1], recv_sem=recv_sems.at[k - 1],
                                          device_id=dev, device_id_type=_MESH)
        cp.start()
        sent.append(cp)
    for k in range(1, N_DEV):
        dev, pidx = _peer(k, x, y, c)
        pltpu.make_async_remote_copy(src_ref=src_of_peer(pidx), dst_ref=dst_ref.at[pidx],
                                     send_sem=send_sems.at[k - 1], recv_sem=recv_sems.at[k - 1],
                                     device_id=dev, device_id_type=_MESH).wait_recv()
    for cp in sent:
        cp.wait_send()


def _ada_exchange(cv8, w_ada, b_ada8):
    def body(cv_ref, w_ref, b_ref, call_ref, modp_ref, part_s, s1, r1, s2, r2):
        x, y, c, me = _my_place()
        call_ref[me] = cv_ref[...]
        _exchange_all(lambda pidx: cv_ref, call_ref, s1, r1, x, y, c, me)
        bias = b_ref[me]
        for j in range(N_DEV):
            cj = call_ref[j][:, :D_MODEL]
            part_s[j] = _hdot(cj * jax.nn.sigmoid(cj), w_ref[...]) + bias
        modp_ref[me] = part_s[me]
        _exchange_all(lambda pidx: part_s.at[pidx], modp_ref, s2, r2, x, y, c, me)

    nsh = w_ada.shape[1]
    return pl.pallas_call(
        body, name="ada_exchange",
        in_specs=[_VMEM, _VMEM, _VMEM], out_specs=[_VMEM, _VMEM],
        out_shape=[jax.ShapeDtypeStruct((N_DEV, 8, cv8.shape[1]), f32), jax.ShapeDtypeStruct((N_DEV, 8, nsh), f32)],
        scratch_shapes=[pltpu.VMEM((N_DEV, 8, nsh), f32)] + [pltpu.SemaphoreType.DMA((N_DEV - 1,))] * 4,
        compiler_params=pltpu.CompilerParams(vmem_limit_bytes=VMEM_LIMIT),
    )(cv8, w_ada, b_ada8)


def _all_to_all(arrs, name):
    ex = _Exchange(arrs, gather=False)

    def body(*refs):
        srcs, dsts, sems = refs[:ex.n], refs[ex.n:2 * ex.n], refs[2 * ex.n:]
        ex.start(srcs, dsts, sems)
        ex.wait(srcs, dsts, sems)

    return pl.pallas_call(
        body, name=name,
        in_specs=[_ANY] * ex.n, out_specs=[_ANY] * ex.n,
        out_shape=ex.out_shape(), scratch_shapes=ex.scratch(),
    )(*arrs)


class _Exchange:
    def __init__(self, arrs, gather):
        self.arrs, self.gather, self.n = list(arrs), gather, len(arrs)

    def out_shape(self):
        return [jax.ShapeDtypeStruct(((N_DEV,) + a.shape) if self.gather else a.shape, a.dtype) for a in self.arrs]

    def scratch(self):
        if self.n == 0:
            return []
        return [pltpu.SemaphoreType.DMA((self.n, N_DEV - 1)), pltpu.SemaphoreType.DMA((self.n, N_DEV - 1)),
                pltpu.SemaphoreType.DMA((self.n,))]

    def _src(self, srcs, a, idx):
        return srcs[a] if self.gather else srcs[a].at[idx]

    def _copies(self, srcs, dsts, sems, incoming):
        send_sems, recv_sems, _ = sems
        x, y, c, me = _my_place()
        out = []
        for a in range(self.n):
            for k in range(1, N_DEV):
                dev, pidx = _peer(k, x, y, c)
                out.append(pltpu.make_async_remote_copy(
                    src_ref=self._src(srcs, a, pidx), dst_ref=dsts[a].at[pidx if incoming else me],
                    send_sem=send_sems.at[a, k - 1], recv_sem=recv_sems.at[a, k - 1],
                    device_id=dev, device_id_type=_MESH))
        return out

    def _local(self, srcs, dsts, sems):
        me = _my_place()[3]
        return [pltpu.make_async_copy(self._src(srcs, a, me), dsts[a].at[me], sems[2].at[a]) for a in range(self.n)]

    def start(self, srcs, dsts, sems):
        for cp in self._local(srcs, dsts, sems) + self._copies(srcs, dsts, sems, incoming=False):
            cp.start()

    def wait(self, srcs, dsts, sems):
        for cp in self._copies(srcs, dsts, sems, incoming=True):
            cp.wait_recv()
        for cp in self._copies(srcs, dsts, sems, incoming=False):
            cp.wait_send()
        for cp in self._local(srcs, dsts, sems):
            cp.wait()

    def start_at_first_step(self, grid, srcs, dsts, sems):
        first = functools.reduce(jnp.logical_and, [pl.program_id(i) == 0 for i in range(len(grid))])
        pl.when(first)(lambda: self.start(srcs, dsts, sems))

    def wait_at_last_step(self, grid, srcs, dsts, sems):
        last = functools.reduce(jnp.logical_and, [pl.program_id(i) == g - 1 for i, g in enumerate(grid)])
        pl.when(last)(lambda: self.wait(srcs, dsts, sems))


class _ChipGather:
    def __init__(self, shard):
        self.shard = shard

    def out_shape(self):
        return jax.ShapeDtypeStruct((N_DEV,) + self.shard.shape, self.shard.dtype)

    def scratch(self):
        return [pltpu.SemaphoreType.DMA((N_DEV - 1,)), pltpu.SemaphoreType.DMA((N_DEV - 1,)),
                pltpu.SemaphoreType.DMA(())]

    def _place(self):
        x, y, c, me = _my_place()
        return x, y, c, me, (x, y, 1 - c), [(1 - x, y), (x, 1 - y), (1 - x, 1 - y)]

    def _copy(self, out, sems, k, block, to, src=None):
        rows = out.at[4 * block[0] + 2 * block[1] + block[2]]
        return pltpu.make_async_remote_copy(src_ref=rows if src is None else src, dst_ref=rows,
                                            send_sem=sems[0].at[k], recv_sem=sems[1].at[k],
                                            device_id=to, device_id_type=_MESH)

    def start(self, src, out, sems):
        x, y, c, me, sib, chips = self._place()
        pltpu.make_async_copy(src, out.at[me], sems[2]).start()
        self._copy(out, sems, 0, (x, y, c), sib, src=src).start()
        for j, chip in enumerate(chips):
            self._copy(out, sems, 1 + j, (x, y, c), (*chip, c), src=src).start()

    def forward(self, src, out, sems):
        x, y, c, me, sib, chips = self._place()
        for j, chip in enumerate(chips):
            self._copy(out, sems, 1 + j, (*chip, c), (x, y, c)).wait_recv()
            self._copy(out, sems, 4 + j, (*chip, c), sib).start()

    def finish(self, src, out, sems):
        x, y, c, me, sib, chips = self._place()
        self._copy(out, sems, 0, (x, y, 1 - c), (x, y, c)).wait_recv()
        for j, chip in enumerate(chips):
            self._copy(out, sems, 4 + j, (*chip, 1 - c), (x, y, c)).wait_recv()
        self._copy(out, sems, 0, (x, y, c), sib, src=src).wait_send()
        for j, chip in enumerate(chips):
            self._copy(out, sems, 1 + j, (x, y, c), (*chip, c), src=src).wait_send()
            self._copy(out, sems, 4 + j, (*chip, c), sib).wait_send()
        pltpu.make_async_copy(src, out.at[me], sems[2]).wait()


def _ride_gathers(body, n_in, n_out, rides, grid, forward_step):
    n = len(rides)
    sizes = list(grid)

    def wrapped(*refs):
        ins, xs = refs[:n_in], refs[n_in:n_in + n]
        outs, xd = refs[n_in + n:n_in + n + n_out], refs[n_in + n + n_out:n_in + 2 * n + n_out]
        scratch = refs[n_in + 2 * n + n_out:]
        own, sems = scratch[:len(scratch) - 3 * n], scratch[len(scratch) - 3 * n:]
        step = pl.program_id(0)
        for i in range(1, len(sizes)):
            step = step * sizes[i] + pl.program_id(i)

        def each(phase):
            for r in range(n):
                getattr(rides[r], phase)(xs[r], xd[r], sems[3 * r:3 * r + 3])

        pl.when(step == 0)(lambda: each("start"))
        body(*ins, *outs, *own)
        pl.when(step == forward_step)(lambda: each("forward"))
        pl.when(step == math.prod(sizes) - 1)(lambda: each("finish"))

    return wrapped


def _ride(body, n_in, n_out, xchg, grid):
    nx = xchg.n
    if nx == 0:
        return body

    def wrapped(*refs):
        ins, xs = refs[:n_in], refs[n_in:n_in + nx]
        outs, xd = refs[n_in + nx:n_in + nx + n_out], refs[n_in + nx + n_out:n_in + 2 * nx + n_out]
        scratch = refs[n_in + 2 * nx + n_out:]
        xchg.start_at_first_step(grid, xs, xd, scratch[-3:])
        body(*ins, *outs, *scratch[:-3])
        xchg.wait_at_last_step(grid, xs, xd, scratch[-3:])

    return wrapped


def _adamw_math(w, g, m, v):
    m2 = ADAM_B1 * m + (1.0 - ADAM_B1) * g
    v2 = ADAM_B2 * v + (1.0 - ADAM_B2) * (g * g)
    m_hat = m2 / (1.0 - ADAM_B1 ** ADAM_STEP)
    v_hat = v2 / (1.0 - ADAM_B2 ** ADAM_STEP)
    delta = -ADAM_LR * (m_hat / (jnp.sqrt(v_hat) + ADAM_EPS) + ADAM_WD * w)
    return delta, m2, v2


def _row_tile(rows):
    for t in (256, 128, 64, 32, 16, 8):
        if rows % t == 0:
            return t
    return rows


def _reduce_adamw(parts, w, m, v, name):
    _, r, cdim = parts.shape
    tr = _row_tile(r)

    def body(p_ref, w_ref, m_ref, v_ref, g_ref, d_ref, m2_ref, v2_ref):
        g = p_ref[0].astype(f32)
        for j in range(1, N_DEV):
            g = g + p_ref[j].astype(f32)
        delta, m2, v2 = _adamw_math(w_ref[...], g, m_ref[...], v_ref[...])
        g_ref[...] = g
        d_ref[...] = delta
        m2_ref[...] = m2
        v2_ref[...] = v2

    spec = pl.BlockSpec((tr, cdim), lambda i: (i, 0))
    return pl.pallas_call(
        body, name=name, grid=(r // tr,),
        in_specs=[pl.BlockSpec((N_DEV, tr, cdim), lambda i: (0, i, 0)), spec, spec, spec],
        out_specs=[spec] * 4,
        out_shape=[jax.ShapeDtypeStruct((r, cdim), f32)] * 4,
        compiler_params=_cparams(("parallel",)),
    )(parts, w, m, v)


def _adamw(w, g, m, v, name):
    r, cdim = w.shape
    tr = _row_tile(r)

    def body(w_ref, g_ref, m_ref, v_ref, d_ref, m2_ref, v2_ref):
        delta, m2, v2 = _adamw_math(w_ref[...], g_ref[...], m_ref[...], v_ref[...])
        d_ref[...] = delta
        m2_ref[...] = m2
        v2_ref[...] = v2

    spec = pl.BlockSpec((tr, cdim), lambda i: (i, 0))
    return pl.pallas_call(
        body, name=name, grid=(r // tr,),
        in_specs=[spec] * 4, out_specs=[spec] * 3,
        out_shape=[jax.ShapeDtypeStruct((r, cdim), f32)] * 3,
        compiler_params=_cparams(("parallel",)),
    )(w, g, m, v)


def _sum_devices(parts, name):
    _, r, cdim = parts.shape

    def body(p_ref, o_ref):
        g = p_ref[0]
        for j in range(1, N_DEV):
            g = g + p_ref[j]
        o_ref[...] = g

    return pl.pallas_call(
        body, name=name, out_shape=jax.ShapeDtypeStruct((r, cdim), f32),
        in_specs=[_VMEM], out_specs=_VMEM,
    )(parts)


def _ada_wgrad(c_all8, dmod_cols):
    nsh = dmod_cols.shape[1]

    def body(c_ref, d_ref, o_ref):
        cv = c_ref[...]
        o_ref[...] = lax.dot_general(cv * jax.nn.sigmoid(cv), d_ref[...], _TN, precision=_HI,
                                     preferred_element_type=f32)

    return pl.pallas_call(
        body, name="ada_wgrad", out_shape=jax.ShapeDtypeStruct((D_MODEL, nsh), f32),
        in_specs=[_VMEM, _VMEM], out_specs=_VMEM,
        compiler_params=pltpu.CompilerParams(vmem_limit_bytes=VMEM_LIMIT),
    )(c_all8, dmod_cols)


def _cols(t):
    return t.transpose(1, 0, 2).reshape(t.shape[1], N_DEV * t.shape[2])


def _col_blocks(t, n):
    return t.reshape(t.shape[0], N_DEV, n).transpose(1, 0, 2).astype(bf16)


def _row_blocks(t):
    return t.reshape(N_DEV, t.shape[0] // N_DEV, t.shape[1]).astype(bf16)


def _local_step(x, tgt, mod, norm_attn_g, w_in_sh, rel_bias, conv_full, a_log, dt_bias, delta_norm_g,
                norm_ffn_g, final_norm_g, w_out_sh, w_gate_sh, w_up_sh, w_down_sh):
    s = x.shape[0]
    sh1, sc1, g1, sh2, sc2, g2 = [mod[:, i * D_MODEL:(i + 1) * D_MODEL] for i in range(6)]
    nag = norm_attn_g.reshape(1, D_MODEL)
    nfg = norm_ffn_g.reshape(1, D_MODEL)
    fg = final_norm_g.reshape(1, D_MODEL)
    idx = _bucket_tables()
    bias = _bias_tables(rel_bias, idx)
    alog_e = jnp.repeat(a_log.reshape(N_HEADS), HEAD_DIM)[None]
    dt_e = jnp.repeat(dt_bias.reshape(N_HEADS), HEAD_DIM)[None]
    ng_e = jnp.tile(delta_norm_g.reshape(HEAD_DIM), N_HEADS)[None]

    h1, w_in_g = _ln_mod_fwd(x, nag, sc1, sh1, w_in_sh, "ln1_fwd")
    w_in_p = jnp.pad(_cols(w_in_g), ((0, 0), (0, IN_PAD - IN_WIDTH)))
    proj, (w_out_g,) = _mm(h1, w_in_p, "nn", f32, 512, IN_PAD, 1024, "in_proj",
                           xchg=_Exchange([w_out_sh], gather=True))
    (y_attn, lse), (w_gate_g, w_up_g, w_down_g) = _attn_fwd(proj, bias, [w_gate_sh, w_up_sh, w_down_sh])
    w_out_b = w_out_g.reshape(2 * GROUP_W, D_MODEL)
    w_down_b = w_down_g.reshape(D_FF, D_MODEL)
    w_gate_b, w_up_b = _cols(w_gate_g), _cols(w_up_g)
    n_ff = w_gate_sh.shape[1]
    sconv, yconv = _conv_silu_fwd(proj, conv_full)
    qn, kn, beta, g = _delta_prep_fwd(sconv, proj, alog_e, dt_e)
    (u, w, qt, kh, qk, tinv, gm), _ = _delta_chunk_pre(qn, kn, sconv, beta, g, _Exchange([], gather=False))
    o, ss = _delta_scan_fwd(u, w, qt, kh, qk, gm)
    y_delta = _gnorm_fwd(o, proj, ng_e)
    y, x1, h2 = _proj_resid_ln_mod_fwd([(y_attn, w_out_b[:GROUP_W]), (y_delta, w_out_b[GROUP_W:])],
                                       x, g1, nfg, sc2, sh2, "out_proj_ln2")
    act, gate, up = _ffn_up(h2, w_gate_b, w_up_b, "ffn_up")
    dx2, dy2, loss, dfg, dg2 = _proj_final_loss_bwd(act, w_down_b, x1, g2, fg, tgt, "ffn_down_loss")

    dgate, dup = _ffn_down_dx(dy2, w_down_b, gate, up, "ffn_down_dx")
    g_down = _mm(act, dy2, "tn", f32, 1408, 1024, 2048, "ffn_down_dw")
    (dx1, dsh2, dsc2, dnfg, dy, dg1), (r_down,) = _proj_ln_mod_bwd(
        [(dgate, w_gate_b), (dup, w_up_b)], x1, nfg, sc2, dx2, 256, "ffn_up_dx_ln2",
        _Exchange([_row_blocks(g_down)], gather=False), gate=g1, y=y)
    g_gate = _mm(h2, dgate, "tn", f32, 1024, 1408, 2048, "ffn_gate_dw")
    g_up = _mm(h2, dup, "tn", f32, 1024, 1408, 2048, "ffn_up_dw")
    dycat = _mm(dy, w_out_b, "nt", f32, 512, 1024, 1024, "out_proj_dx")
    g_out = jnp.concatenate([_mm(y_attn, dy, "tn", f32, GROUP_W, 1024, 4096, "out_proj_dw_attn"),
                             _mm(y_delta, dy, "tn", f32, GROUP_W, 1024, 4096, "out_proj_dw_delta")], axis=0)
    dq, dk, dv, dbias = _attn_bwd(proj, bias, y_attn, lse, dycat)
    g_rb = _bias_grad(dbias, idx)[:, :, 0].T
    do, dz, dng = _gnorm_bwd(o, proj, ng_e, dycat)
    dso, _ = _delta_scan_bwd(w, qt, kh, qk, gm, do, _Exchange([], gather=False))
    (dqn, dkn, dvd, dbeta, dgd), (r_up,) = _delta_chunk_bwd(
        qn, kn, sconv, beta, g, tinv, ss, dso, do, _Exchange([_col_blocks(g_up, n_ff)], gather=False))
    (dsq, dsk, dba, dal, ddt), _ = _delta_prep_bwd(
        sconv, proj, alog_e, dt_e, dqn, dkn, dbeta, dgd, _Exchange([], gather=False))
    (dxc, g_conv), (r_gate, r_out) = _conv_silu_bwd(
        proj, conv_full, yconv, (dsq, dsk, dvd),
        _Exchange([_col_blocks(g_gate, n_ff), _row_blocks(g_out)], gather=False))
    pieces = ((dq, 0), (dk, GROUP_W), (dv, 2 * GROUP_W), (dxc, DELTA_COL), (dz, Z_COL), (dba, BA_BLOCK * 128))
    g_in = jnp.concatenate(
        [_mm(h1, p, "tn", f32, 1024, min(p.shape[1], 768), 4096, "in_proj_dw_%d" % c) for p, c in pieces], axis=1)
    (gx, dsh1, dsc1, dnag), (r_in,) = _proj_ln_mod_bwd(
        [(p, w_in_p[:, c:c + p.shape[1]]) for p, c in pieces], x, nag, sc1, dx1, TOK_TILE, "in_proj_dx_ln1",
        _Exchange([_col_blocks(g_in[:, :IN_WIDTH], IN_WIDTH // N_DEV)], gather=False))
    grads = dict(
        x=gx, mod=jnp.concatenate([dsh1, dsc1, dg1, dsh2, dsc2, dg2], axis=1),
        norm_attn_g=dnag, norm_ffn_g=dnfg, final_norm_g=dfg, rel_bias=g_rb, conv_w=g_conv,
        a_log=dal.reshape(N_HEADS, HEAD_DIM).sum(-1), dt_bias=ddt.reshape(N_HEADS, HEAD_DIM).sum(-1),
        delta_norm_g=dng.reshape(N_HEADS, HEAD_DIM).sum(0),
        w_in=r_in, w_out=r_out, w_gate=r_gate, w_up=r_up, w_down=r_down)
    return loss[0, 0], grads


def _misc_row(rel_bias, a_log, dt_bias, delta_norm_g):
    flat = jnp.concatenate([rel_bias.reshape(-1), a_log.reshape(-1), dt_bias.reshape(-1), delta_norm_g.reshape(-1)])
    return jnp.pad(flat, (0, D_MODEL - flat.shape[0]))[None]


def _pack_small(b_ada, nag, nfg, fng, rel_bias, a_log, dt_bias, dng, conv_shard):
    rows = [b_ada.reshape(6, D_MODEL), nag.reshape(1, D_MODEL), nfg.reshape(1, D_MODEL), fng.reshape(1, D_MODEL),
            _misc_row(rel_bias, a_log, dt_bias, dng),
            jnp.pad(conv_shard.reshape(-1), (0, D_MODEL - conv_shard.size))[None],
            jnp.zeros((5, D_MODEL), f32)]
    return jnp.concatenate(rows, axis=0)


def _unpack_small(p, conv_shape):
    misc = p[9]
    return dict(
        b_ada=p[0:6].reshape(1, 6 * D_MODEL), norm_attn_g=p[6:7], norm_ffn_g=p[7:8], final_norm_g=p[8],
        rel_bias=misc[0:256].reshape(N_BUCKETS, N_HEADS), a_log=misc[256:264].reshape(1, N_HEADS),
        dt_bias=misc[264:272].reshape(1, N_HEADS), delta_norm_g=misc[272:336].reshape(1, HEAD_DIM),
        conv_w=p[10, :conv_shape[1] * conv_shape[2]].reshape(conv_shape))


def kernel(x, c, w_ada, b_ada, norm_attn_g, w_in, rel_bias, conv_w, a_log, dt_bias, delta_norm_g, w_out, norm_ffn_g, w_gate, w_up, w_down, final_norm_g, loss_target, m_w_ada, m_b_ada, m_norm_attn_g, m_w_in, m_rel_bias, m_conv_w, m_a_log, m_dt_bias, m_delta_norm_g, m_w_out, m_norm_ffn_g, m_w_gate, m_w_up, m_w_down, m_final_norm_g, v_w_ada, v_b_ada, v_norm_attn_g, v_w_in, v_rel_bias, v_conv_w, v_a_log, v_dt_bias, v_delta_norm_g, v_w_out, v_norm_ffn_g, v_w_gate, v_w_up, v_w_down, v_final_norm_g):
    me = 4 * lax.axis_index("x") + 2 * lax.axis_index("y") + lax.axis_index("c")
    ada_sh = w_ada.shape[2]
    conv_sh = conv_w.shape[2]

    cv = jnp.concatenate([c[0], conv_w[0].reshape(-1)])
    cv8 = jnp.zeros((8, 2 * D_MODEL), f32).at[0, :cv.shape[0]].set(cv)
    b8 = jnp.broadcast_to(b_ada.reshape(N_DEV, 1, ada_sh), (N_DEV, 8, ada_sh))
    call, modp = _ada_exchange(cv8, w_ada[0], b8)
    mod = modp[:, 0, :].reshape(1, 6 * D_MODEL)
    c_all = call[:, 0, :D_MODEL]
    conv_full = call[:, 0, D_MODEL:D_MODEL + CONV_WIDTH * conv_sh].reshape(N_DEV, CONV_WIDTH, conv_sh)
    conv_full = conv_full.transpose(1, 0, 2).reshape(CONV_WIDTH, N_DEV * conv_sh)

    loss_local, gr = _local_step(x[0], loss_target[0], mod, norm_attn_g, w_in[0].astype(bf16), rel_bias, conv_full, a_log,
                                 dt_bias, delta_norm_g, norm_ffn_g, final_norm_g, w_out[0].astype(bf16),
                                 w_gate[0].astype(bf16), w_up[0].astype(bf16), w_down[0].astype(bf16))
    loss = lax.psum(loss_local, ("x", "y", "c"))

    small = jnp.concatenate([
        gr["mod"].reshape(6, D_MODEL), gr["norm_attn_g"], gr["norm_ffn_g"], gr["final_norm_g"],
        gr["conv_w"].reshape(6, D_MODEL),
        _misc_row(gr["rel_bias"], gr["a_log"], gr["dt_bias"], gr["delta_norm_g"])], axis=0)
    parts = _all_to_all([jnp.broadcast_to(small[None], (N_DEV,) + small.shape)], "small_gather")[0]
    tot = _sum_devices(parts, "small_sum")
    g_conv_full = tot[9:15].reshape(CONV_WIDTH, N_DEV * conv_sh)
    g_conv = lax.dynamic_slice(g_conv_full, (0, me * conv_sh), (CONV_WIDTH, conv_sh))
    misc = tot[15]
    g_small = _pack_small(tot[0:6], tot[6], tot[7], tot[8], misc[0:256], misc[256:264], misc[264:272],
                          misc[272:336], g_conv)
    pk = lambda pre: _pack_small(pre[0], pre[1], pre[2], pre[3], pre[4], pre[5], pre[6], pre[7], pre[8])
    w_small = pk((b_ada, norm_attn_g, norm_ffn_g, final_norm_g, rel_bias, a_log, dt_bias, delta_norm_g, conv_w))
    m_small = pk((m_b_ada, m_norm_attn_g, m_norm_ffn_g, m_final_norm_g, m_rel_bias, m_a_log, m_dt_bias,
                  m_delta_norm_g, m_conv_w))
    v_small = pk((v_b_ada, v_norm_attn_g, v_norm_ffn_g, v_final_norm_g, v_rel_bias, v_a_log, v_dt_bias,
                  v_delta_norm_g, v_conv_w))
    d_small, m2_small, v2_small = _adamw(w_small, g_small, m_small, v_small, "adamw_small")
    cshape = conv_w.shape
    G, Dl, M2, V2 = (_unpack_small(t, cshape) for t in (g_small, d_small, m2_small, v2_small))

    dmod_all = parts[:, 0:6, :].reshape(N_DEV, 6 * D_MODEL)
    dmod_cols = lax.dynamic_slice(dmod_all, (0, me * ada_sh), (N_DEV, ada_sh))
    g_ada = _ada_wgrad(c_all, dmod_cols)
    d_ada, m2_ada, v2_ada = _adamw(w_ada[0], g_ada, m_w_ada[0], v_w_ada[0], "adamw_w_ada")

    big = {}
    for name, w_, m_, v_ in (("w_in", w_in, m_w_in, v_w_in), ("w_out", w_out, m_w_out, v_w_out),
                             ("w_gate", w_gate, m_w_gate, v_w_gate), ("w_up", w_up, m_w_up, v_w_up),
                             ("w_down", w_down, m_w_down, v_w_down)):
        big[name] = [t[None] for t in _reduce_adamw(gr[name], w_[0], m_[0], v_[0], "reduce_adamw_" + name)]

    def leaf(i, name):
        if name == "w_ada":
            return (g_ada, d_ada, m2_ada, v2_ada)[i][None]
        if name in big:
            return big[name][i]
        return (G, Dl, M2, V2)[i][name]

    order = ["w_ada", "b_ada", "norm_attn_g", "w_in", "rel_bias", "conv_w", "a_log", "dt_bias", "delta_norm_g",
             "w_out", "norm_ffn_g", "w_gate", "w_up", "w_down", "final_norm_g"]
    outs = [loss, gr["x"][None]]
    for i in range(4):
        outs += [leaf(i, n) for n in order]
    return tuple(outs)
```

```python
import functools
import math

import jax
import jax.numpy as jnp
from jax import lax
from jax.experimental import pallas as pl
from jax.experimental.pallas import tpu as pltpu

f32 = jnp.float32
bf16 = jnp.bfloat16

D_MODEL = 1024
HEAD_DIM = 64
N_HEADS = 8
GROUP_W = 512
IN_WIDTH = 3600
IN_PAD = 3840
D_FF = 2816
EPS = 1e-6
NEG_INF = -1e30
BAND = 128
PAD_UNIT = 2048
DILATIONS = (1, 4, 16)
N_BUCKETS = 32
MAX_DISTANCE = 2048
CONV_WIDTH = 4
CHUNK = 64
N_DEV = 8
VMEM_LIMIT = 56 * 1024 * 1024

ADAM_LR, ADAM_B1, ADAM_B2, ADAM_EPS, ADAM_WD, ADAM_STEP = 0.001, 0.9, 0.999, 1e-08, 0.01, 10


def _cparams(sem):
    return pltpu.CompilerParams(dimension_semantics=sem, vmem_limit_bytes=VMEM_LIMIT)


def _mm(a, b, mode, out_dtype, tm, tn, tk, name, xchg=None):
    if mode == "nn":
        (m, k), (_, n) = a.shape, b.shape
        a_spec = pl.BlockSpec((tm, tk), lambda j, i, kk: (i, kk))
        b_spec = pl.BlockSpec((tk, tn), lambda j, i, kk: (kk, j))
        dims = (((1,), (0,)), ((), ()))
    elif mode == "nt":
        (m, k), (n, _) = a.shape, b.shape
        a_spec = pl.BlockSpec((tm, tk), lambda j, i, kk: (i, kk))
        b_spec = pl.BlockSpec((tn, tk), lambda j, i, kk: (j, kk))
        dims = (((1,), (1,)), ((), ()))
    else:
        (k, m), (_, n) = a.shape, b.shape
        a_spec = pl.BlockSpec((tk, tm), lambda j, i, kk: (kk, i))
        b_spec = pl.BlockSpec((tk, tn), lambda j, i, kk: (kk, j))
        dims = (((0,), (0,)), ((), ()))
    assert m % tm == 0 and n % tn == 0 and k % tk == 0, (name, m, n, k, tm, tn, tk)
    nk = k // tk
    grid = (n // tn, m // tm, nk)
    nx = xchg.n if xchg is not None else 0

    def body(*refs):
        a_ref, b_ref = refs[:2]
        o_ref = refs[2 + nx]
        scratch = refs[3 + 2 * nx:]
        if nx:
            xrefs = (refs[2:2 + nx], refs[3 + nx:3 + 2 * nx], scratch[-3:])
            xchg.start_at_first_step(grid, *xrefs)
        if nk == 1:
            o_ref[...] = lax.dot_general(a_ref[...].astype(bf16), b_ref[...].astype(bf16), dims,
                                         preferred_element_type=f32).astype(o_ref.dtype)
        else:
            acc_ref = scratch[0]
            kk = pl.program_id(2)

            @pl.when(kk == 0)
            def _():
                acc_ref[...] = jnp.zeros_like(acc_ref)

            acc_ref[...] += lax.dot_general(a_ref[...].astype(bf16), b_ref[...].astype(bf16), dims,
                                            preferred_element_type=f32)

            @pl.when(kk == nk - 1)
            def _():
                o_ref[...] = acc_ref[...].astype(o_ref.dtype)
        if nx:
            xchg.wait_at_last_step(grid, *xrefs)

    out = pl.pallas_call(
        body, name=name, grid=grid,
        in_specs=[a_spec, b_spec] + ([_ANY] * nx),
        out_specs=[pl.BlockSpec((tm, tn), lambda j, i, kk: (i, j))] + ([_ANY] * nx),
        out_shape=[jax.ShapeDtypeStruct((m, n), out_dtype)] + (xchg.out_shape() if nx else []),
        scratch_shapes=([pltpu.VMEM((tm, tn), f32)] if nk > 1 else []) + (xchg.scratch() if nx else []),
        compiler_params=_cparams(("arbitrary",) * 3 if nx else ("parallel", "parallel", "arbitrary")),
    )(a, b, *(xchg.arrs if nx else []))
    return (out[0], out[1:]) if nx else out[0]


TOK_TILE = 512
SUB_COLS = 384


def _row_spec(width, tile=TOK_TILE):
    return pl.BlockSpec((tile, width), lambda i: (i, 0))


def _vec_spec(width, rows=1):
    return pl.BlockSpec((rows, width), lambda i: (0, 0))


def _ln_mod_fwd(x, gain, sc, sh, shard, name):
    s, d = x.shape
    nt = s // TOK_TILE
    ride = _ChipGather(shard)

    def body(x_ref, g_ref, sc_ref, sh_ref, sh_in, h_ref, sh_out, *sems):
        i = pl.program_id(0)
        pl.when(i == 0)(lambda: ride.start(sh_in, sh_out, sems))
        xv = x_ref[...]
        rstd = lax.rsqrt(jnp.mean(xv * xv, axis=-1, keepdims=True) + EPS)
        h = (xv * rstd) * g_ref[...] * (1.0 + sc_ref[...]) + sh_ref[...]
        h_ref[...] = h.astype(bf16)
        @pl.when(i == nt - 1)
        def _():
            ride.forward(sh_in, sh_out, sems)
            ride.finish(sh_in, sh_out, sems)

    return pl.pallas_call(
        body, name=name, grid=(nt,),
        in_specs=[_row_spec(d), _vec_spec(d), _vec_spec(d), _vec_spec(d), _ANY],
        out_specs=[_row_spec(d), _ANY],
        out_shape=[jax.ShapeDtypeStruct((s, d), bf16), ride.out_shape()],
        scratch_shapes=ride.scratch(),
        compiler_params=_cparams(("arbitrary",)),
    )(x, gain, sc, sh, shard)


def _proj_resid_ln_mod_fwd(pairs, x, gate, gain, sc, sh, name):
    s, d = x.shape
    npair = len(pairs)

    def body(*refs):
        aw = refs[:2 * npair]
        x_ref, gt_ref, g_ref, sc_ref, sh_ref, y_ref, x1_ref, h_ref = refs[2 * npair:]
        halves = [slice(r * TOK_TILE, (r + 1) * TOK_TILE) for r in range(2)]
        ys = []
        for rows in halves:
            y = jnp.dot(aw[0][rows, :].astype(bf16), aw[1][...], preferred_element_type=f32)
            for t in range(1, npair):
                y = y + jnp.dot(aw[2 * t][rows, :].astype(bf16), aw[2 * t + 1][...], preferred_element_type=f32)
            ys.append(y)
        for rows, y in zip(halves, ys):
            y_ref[rows, :] = y
            x1 = x_ref[rows, :] + gt_ref[...] * y
            x1_ref[rows, :] = x1
            rstd = lax.rsqrt(jnp.mean(x1 * x1, axis=-1, keepdims=True) + EPS)
            h = (x1 * rstd) * g_ref[...] * (1.0 + sc_ref[...]) + sh_ref[...]
            h_ref[rows, :] = h.astype(bf16)

    tile = 2 * TOK_TILE
    aw_specs, aw = [], []
    for a, w in pairs:
        aw_specs += [_row_spec(a.shape[1], tile), pl.BlockSpec(w.shape, lambda i: (0, 0))]
        aw += [a, w]
    return pl.pallas_call(
        body, name=name, grid=(s // tile,),
        in_specs=aw_specs + [_row_spec(d, tile)] + [_vec_spec(d)] * 4,
        out_specs=[_row_spec(d, tile)] * 3,
        out_shape=[jax.ShapeDtypeStruct((s, d), f32)] * 2 + [jax.ShapeDtypeStruct((s, d), bf16)],
        compiler_params=_cparams(("parallel",)),
    )(*aw, x, gate, gain, sc, sh)


FFN_TN = 1408


def _ffn_up(h2, w_gate, w_up, name):
    s, d = h2.shape
    tm = 2 * TOK_TILE

    def body(h_ref, wg_ref, wu_ref, a_ref, g_ref, u_ref):
        h = h_ref[...]
        g = jnp.dot(h, wg_ref[...], preferred_element_type=f32)
        u = jnp.dot(h, wu_ref[...], preferred_element_type=f32)
        a_ref[...] = (g * jax.nn.sigmoid(g) * u).astype(bf16)
        g_ref[...] = g.astype(bf16)
        u_ref[...] = u.astype(bf16)

    w_spec = pl.BlockSpec((d, FFN_TN), lambda j, i: (0, j))
    o_spec = pl.BlockSpec((tm, FFN_TN), lambda j, i: (i, j))
    return pl.pallas_call(
        body, name=name, grid=(D_FF // FFN_TN, s // tm),
        in_specs=[pl.BlockSpec((tm, d), lambda j, i: (i, 0)), w_spec, w_spec],
        out_specs=[o_spec] * 3,
        out_shape=[jax.ShapeDtypeStruct((s, D_FF), bf16)] * 3,
        compiler_params=_cparams(("parallel", "parallel")),
    )(h2, w_gate, w_up)


def _ffn_down_dx(dy2, w_down, gate, up, name):
    s, d = dy2.shape
    tm = 2 * TOK_TILE

    def body(dy_ref, w_ref, g_ref, u_ref, dg_ref, du_ref):
        dy = dy_ref[...]
        for c0 in range(0, FFN_TN, SUB_COLS):
            cols = slice(c0, min(c0 + SUB_COLS, FFN_TN))
            da = lax.dot_general(dy, w_ref[cols, :], _NT, preferred_element_type=f32)
            g = g_ref[:, cols].astype(f32)
            sg = jax.nn.sigmoid(g)
            du_ref[:, cols] = (da * g * sg).astype(bf16)
            dg_ref[:, cols] = (da * u_ref[:, cols].astype(f32) * sg * (1.0 + g * (1.0 - sg))).astype(bf16)

    t_spec = pl.BlockSpec((tm, FFN_TN), lambda j, i: (i, j))
    return pl.pallas_call(
        body, name=name, grid=(D_FF // FFN_TN, s // tm),
        in_specs=[pl.BlockSpec((tm, d), lambda j, i: (i, 0)), pl.BlockSpec((FFN_TN, d), lambda j, i: (j, 0)),
                  t_spec, t_spec],
        out_specs=[t_spec, t_spec],
        out_shape=[jax.ShapeDtypeStruct((s, D_FF), bf16)] * 2,
        compiler_params=_cparams(("parallel", "parallel")),
    )(dy2, w_down, gate, up)


def _acc_spec(width):
    return pl.BlockSpec((1, width), lambda i: (0, 0))


def _proj_final_loss_bwd(a, w, x1, gate2, final_g, target, name):
    s, d = x1.shape
    k = a.shape[1]

    def body(a_ref, w_ref, x1_ref, gt_ref, fg_ref, tg_ref, dx2_ref, dy2_ref, loss_ref, dfg_ref, dgt_ref):
        @pl.when(pl.program_id(0) == 0)
        def _():
            loss_ref[...] = jnp.zeros_like(loss_ref)
            dfg_ref[...] = jnp.zeros_like(dfg_ref)
            dgt_ref[...] = jnp.zeros_like(dgt_ref)

        y2 = jnp.dot(a_ref[...], w_ref[...], preferred_element_type=f32)
        gt = gt_ref[...]
        fg = fg_ref[...]
        x2 = x1_ref[...] + gt * y2
        rstd = lax.rsqrt(jnp.mean(x2 * x2, axis=-1, keepdims=True) + EPS)
        xn = x2 * rstd
        err = xn * fg - tg_ref[...]
        row = jnp.sum(err * err, axis=-1, keepdims=True) * (0.5 / d)
        loss_ref[...] += jnp.sum(row, axis=0, keepdims=True) + jnp.zeros_like(loss_ref)
        dout = err * (1.0 / d)
        dfg_ref[...] += jnp.sum(dout * xn, axis=0, keepdims=True)
        dxn = dout * fg
        dx2 = rstd * (dxn - xn * jnp.mean(dxn * xn, axis=-1, keepdims=True))
        dx2_ref[...] = dx2
        dgt_ref[...] += jnp.sum(dx2 * y2, axis=0, keepdims=True)
        dy2_ref[...] = (gt * dx2).astype(bf16)

    return pl.pallas_call(
        body, name=name, grid=(s // TOK_TILE,),
        in_specs=[_row_spec(k), pl.BlockSpec((k, d), lambda i: (0, 0)), _row_spec(d), _vec_spec(d), _vec_spec(d),
                  _row_spec(d)],
        out_specs=[_row_spec(d), _row_spec(d), _acc_spec(128), _acc_spec(d), _acc_spec(d)],
        out_shape=[jax.ShapeDtypeStruct((s, d), f32), jax.ShapeDtypeStruct((s, d), bf16),
                   jax.ShapeDtypeStruct((1, 128), f32), jax.ShapeDtypeStruct((1, d), f32),
                   jax.ShapeDtypeStruct((1, d), f32)],
        compiler_params=_cparams(("arbitrary",)),
    )(a, w, x1, gate2, final_g, target)


def _proj_ln_mod_bwd(pairs, xin, gain, sc, dres, tm, name, xchg, gate=None, y=None):
    s, d = xin.shape
    with_gate = gate is not None
    npair = len(pairs)
    n_in = 2 * npair + (7 if with_gate else 5) - 1
    n_out = 6 if with_gate else 4

    def body(*refs):
        ab = refs[:2 * npair]
        if with_gate:
            (x_ref, g_ref, sc_ref, dr_ref, gt_ref, y_ref,
             dx_ref, dsh_ref, dsc_ref, dg_ref, dy_ref, dgt_ref) = refs[2 * npair:]
        else:
            x_ref, g_ref, sc_ref, dr_ref, dx_ref, dsh_ref, dsc_ref, dg_ref = refs[2 * npair:]

        @pl.when(pl.program_id(0) == 0)
        def _():
            dsh_ref[...] = jnp.zeros_like(dsh_ref)
            dsc_ref[...] = jnp.zeros_like(dsc_ref)
            dg_ref[...] = jnp.zeros_like(dg_ref)
            if with_gate:
                dgt_ref[...] = jnp.zeros_like(dgt_ref)

        dh = lax.dot_general(ab[0][...].astype(bf16), ab[1][...], _NT, preferred_element_type=f32)
        for t in range(1, npair):
            dh = dh + lax.dot_general(ab[2 * t][...].astype(bf16), ab[2 * t + 1][...], _NT,
                                      preferred_element_type=f32)
        xv = x_ref[...]
        g = g_ref[...]
        sc1 = 1.0 + sc_ref[...]
        rstd = lax.rsqrt(jnp.mean(xv * xv, axis=-1, keepdims=True) + EPS)
        xn = xv * rstd
        dsh_ref[...] += jnp.sum(dh, axis=0, keepdims=True)
        dsc_ref[...] += jnp.sum(dh * (xn * g), axis=0, keepdims=True)
        dg_ref[...] += jnp.sum(dh * sc1 * xn, axis=0, keepdims=True)
        dxn = dh * sc1 * g
        dx = dr_ref[...] + rstd * (dxn - xn * jnp.mean(dxn * xn, axis=-1, keepdims=True))
        dx_ref[...] = dx
        if with_gate:
            dgt_ref[...] += jnp.sum(dx * y_ref[...], axis=0, keepdims=True)
            dy_ref[...] = (gt_ref[...] * dx).astype(bf16)

    row = lambda width: pl.BlockSpec((tm, width), lambda i: (i, 0))
    in_specs, args = [], []
    for a, b in pairs:
        in_specs += [row(a.shape[1]), pl.BlockSpec(b.shape, lambda i: (0, 0))]
        args += [a, b]
    in_specs += [row(d), _vec_spec(d), _vec_spec(d), row(d)]
    args += [xin, gain, sc, dres]
    out_specs = [row(d), _acc_spec(d), _acc_spec(d), _acc_spec(d)]
    out_shape = [jax.ShapeDtypeStruct((s, d), f32)] + [jax.ShapeDtypeStruct((1, d), f32)] * 3
    if with_gate:
        in_specs += [_vec_spec(d), row(d)]
        out_specs += [row(d), _acc_spec(d)]
        out_shape += [jax.ShapeDtypeStruct((s, d), bf16), jax.ShapeDtypeStruct((1, d), f32)]
        args += [gate, y]
    grid = (s // tm,)
    out = pl.pallas_call(
        _ride(body, n_in, n_out, xchg, grid), name=name, grid=grid,
        in_specs=in_specs + [_ANY] * xchg.n, out_specs=out_specs + [_ANY] * xchg.n,
        out_shape=out_shape + xchg.out_shape(), scratch_shapes=xchg.scratch(),
        compiler_params=_cparams(("arbitrary",)),
    )(*args, *xchg.arrs)
    return out[:n_out], out[n_out:]


def _bucket_tables():
    import numpy as np
    qi = np.arange(BAND)[:, None]
    kj = np.arange(2 * BAND)[None, :]
    steps = qi + BAND - kj
    max_exact = N_BUCKETS // 2
    out = []
    for d in DILATIONS:
        dist = np.maximum(steps, 0) * d
        dist_f = np.maximum(dist, 1).astype(np.float32)
        large = max_exact + (np.log(dist_f / np.float32(max_exact)) / np.float32(math.log(MAX_DISTANCE / max_exact))
                             * np.float32(N_BUCKETS - max_exact)).astype(np.int32)
        out.append(np.where(dist < max_exact, dist, np.minimum(large, N_BUCKETS - 1)))
    return jnp.asarray(np.stack(out).astype(np.int32))


def _bias_tables(rel_bias, idx):
    def body(idx_ref, rb_ref, o_ref):
        h = pl.program_id(1)
        idxv = idx_ref[0]
        acc = jnp.zeros((BAND, 2 * BAND), f32)
        for b in range(N_BUCKETS):
            acc = jnp.where(idxv == b, rb_ref[b, h], acc)
        o_ref[0, 0] = jnp.where(_attn_masks()[1], acc, NEG_INF)

    return pl.pallas_call(
        body, name="attn_bias_tables", grid=(3, N_HEADS),
        in_specs=[pl.BlockSpec((1, BAND, 2 * BAND), lambda br, h: (br, 0, 0)),
                  pl.BlockSpec(memory_space=pltpu.SMEM)],
        out_specs=pl.BlockSpec((1, 1, BAND, 2 * BAND), lambda br, h: (br, h, 0, 0)),
        out_shape=jax.ShapeDtypeStruct((3, N_HEADS, BAND, 2 * BAND), f32),
        compiler_params=_cparams(("parallel", "parallel")),
    )(idx, rel_bias)


def _bias_grad(dbias, idx):
    def body(idx_ref, db_ref, o_ref):
        br = pl.program_id(1)

        @pl.when(br == 0)
        def _():
            o_ref[...] = jnp.zeros_like(o_ref)

        idxv = idx_ref[0]
        dbv = db_ref[0, 0]
        row = lax.broadcasted_iota(jnp.int32, (N_BUCKETS, 128), 0)
        acc = jnp.zeros((N_BUCKETS, 128), f32)
        for b in range(N_BUCKETS):
            sb = jnp.sum(jnp.sum(jnp.where(idxv == b, dbv, 0.0), axis=1, keepdims=True), axis=0, keepdims=True)
            acc = acc + jnp.where(row == b, sb, 0.0)
        o_ref[0] += acc

    return pl.pallas_call(
        body, name="attn_bias_grad", grid=(N_HEADS, 3),
        in_specs=[pl.BlockSpec((1, BAND, 2 * BAND), lambda h, br: (br, 0, 0)),
                  pl.BlockSpec((1, 1, BAND, 2 * BAND), lambda h, br: (br, h, 0, 0))],
        out_specs=pl.BlockSpec((1, N_BUCKETS, 128), lambda h, br: (h, 0, 0)),
        out_shape=jax.ShapeDtypeStruct((N_HEADS, N_BUCKETS, 128), f32),
        compiler_params=_cparams(("parallel", "arbitrary")),
    )(idx, dbias)


def _attn_masks():
    lane = lax.broadcasted_iota(jnp.int32, (BAND, 128), 1)
    m0 = lane < HEAD_DIM
    qi = lax.broadcasted_iota(jnp.int32, (BAND, 2 * BAND), 0)
    kj = lax.broadcasted_iota(jnp.int32, (BAND, 2 * BAND), 1)
    steps = qi + BAND - kj
    in_window = (steps >= 0) & (steps <= BAND)
    return m0, in_window, kj >= BAND


_NT = (((1,), (1,)), ((), ()))
_TN = (((0,), (0,)), ((), ()))
_BNN = (((2,), (1,)), ((0,), (0,)))
_BNT = (((2,), (2,)), ((0,), (0,)))
_BTN = (((1,), (1,)), ((0,), (0,)))
ATTN_GROUP = 4
ATTN_ITEMS = PAD_UNIT // BAND
Q_COL, K_COL, V_COL = 0, 4, 8


def _attn_item_rows(j, d, c, cbase):
    r = lax.rem(j, d)
    b = lax.div(j, d)
    loc = b * (d * BAND) + r
    first = jnp.logical_and(c == 0, b == 0)
    start = cbase + loc
    pstart = jnp.where(first, start, start - d * BAND)
    return loc, start, pstart, first


def _attn_fwd(proj, bias, shards):
    s = proj.shape[0]
    rides = [_ChipGather(sh) for sh in shards]

    def body(q_ref, k_ref, v_ref, b_ref, y_ref, lse_ref, o_s, l_s):
        c = pl.program_id(1)
        cbase = pl.multiple_of(c * PAD_UNIT, PAD_UNIT)
        m0, in_window, cur_half = _attn_masks()
        for bi, d in enumerate(DILATIONS):
            def group(jg, carry, bi=bi, d=d):
                locs, qs, ks, vs, pens = [], [], [], [], []
                for t in range(ATTN_GROUP):
                    loc, start, pstart, first = _attn_item_rows(jg * ATTN_GROUP + t, d, c, cbase)
                    locs.append(loc)
                    qs.append(q_ref[pl.ds(loc, BAND, stride=d), :])
                    ks.append(jnp.concatenate([k_ref[pl.ds(pstart, BAND, stride=d), :],
                                               k_ref[pl.ds(start, BAND, stride=d), :]], axis=0))
                    vs.append(jnp.concatenate([v_ref[pl.ds(pstart, BAND, stride=d), :],
                                               v_ref[pl.ds(start, BAND, stride=d), :]], axis=0))
                    pens.append(jnp.where(cur_half, 0.0, jnp.where(first, NEG_INF, 0.0)))
                q = jnp.stack(qs)
                kk = jnp.stack(ks + ks).astype(bf16)
                vv = jnp.stack(vs + vs).astype(bf16)
                pen = jnp.stack(pens + pens)
                qh = (jnp.concatenate([jnp.where(m0, q, 0.0), jnp.where(m0, 0.0, q)], axis=0) * 0.125).astype(bf16)
                sc = lax.dot_general(qh, kk, _BNT, preferred_element_type=f32)
                sc = (sc.reshape(2, ATTN_GROUP, BAND, 2 * BAND) + b_ref[bi][:, None]).reshape(sc.shape) + pen
                mx = jnp.max(sc, axis=-1, keepdims=True)
                e = jnp.exp(sc - mx)
                l = jnp.sum(e, axis=-1, keepdims=True)
                o = lax.dot_general(e.astype(bf16), vv, _BNN, preferred_element_type=f32) * (1.0 / l)
                ls = mx + jnp.log(l)
                for t in range(ATTN_GROUP):
                    rows = pl.ds(locs[t], BAND, stride=d)
                    o_s[bi, rows, :] = jnp.where(m0, o[t], o[ATTN_GROUP + t])
                    l_s[bi, rows, :] = jnp.where(m0, ls[t], ls[ATTN_GROUP + t])
                return carry

            lax.fori_loop(0, ATTN_ITEMS // ATTN_GROUP, group, 0)

        def merge(t, carry):
            rows = pl.ds(pl.multiple_of(t * 256, 256), 256)
            ls = [l_s[i, rows, :] for i in range(3)]
            mx = jnp.maximum(jnp.maximum(ls[0], ls[1]), ls[2])
            ws = [jnp.exp(l - mx) for l in ls]
            tot = ws[0] + ws[1] + ws[2]
            y = (ws[0] * o_s[0, rows, :] + ws[1] * o_s[1, rows, :] + ws[2] * o_s[2, rows, :]) / tot
            y_ref[rows, :] = y
            lse_ref[rows, :] = mx + jnp.log(tot)
            return carry

        lax.fori_loop(0, PAD_UNIT // 256, merge, 0)

    chunk = lambda col: pl.BlockSpec((PAD_UNIT, 128), lambda p, c: (c, col + p))
    full = lambda col: pl.BlockSpec((s, 128), lambda p, c: (0, col + p))
    grid = (N_HEADS // 2, s // PAD_UNIT)
    nsteps = grid[0] * grid[1]
    out = pl.pallas_call(
        _ride_gathers(body, 4, 2, rides, grid, (3 * nsteps) // 4), name="attn_fwd", grid=grid,
        in_specs=[chunk(Q_COL), full(K_COL), full(V_COL),
                  pl.BlockSpec((3, 2, BAND, 2 * BAND), lambda p, c: (0, p, 0, 0))] + [_ANY] * len(rides),
        out_specs=[chunk(0), chunk(0)] + [_ANY] * len(rides),
        out_shape=[jax.ShapeDtypeStruct((s, GROUP_W), f32)] * 2 + [r.out_shape() for r in rides],
        scratch_shapes=[pltpu.VMEM((3, PAD_UNIT, 128), f32)] * 2 + [t for r in rides for t in r.scratch()],
        compiler_params=_cparams(("arbitrary", "arbitrary")),
    )(proj, proj, proj, bias, *shards)
    return out[:2], out[2:]


def _attn_bwd(proj, bias, y, lse, dycat):
    s = proj.shape[0]

    def body(q_ref, k_ref, v_ref, b_ref, y_ref, lse_ref, dy_ref, dq_ref, dk_ref, dv_ref, db_ref, dd_s):
        c = pl.program_id(1)
        cbase = pl.multiple_of(c * PAD_UNIT, PAD_UNIT)
        m0, in_window, cur_half = _attn_masks()

        @pl.when(c == 0)
        def _():
            dk_ref[...] = jnp.zeros_like(dk_ref)
            dv_ref[...] = jnp.zeros_like(dv_ref)
            db_ref[...] = jnp.zeros_like(db_ref)

        dq_ref[...] = jnp.zeros_like(dq_ref)

        def rowdot(t, carry):
            rows = pl.ds(pl.multiple_of(t * 256, 256), 256)
            prod = dy_ref[rows, :] * y_ref[rows, :]
            lane = lax.broadcasted_iota(jnp.int32, prod.shape, 1)
            h0 = lane < HEAD_DIM
            d0 = jnp.sum(jnp.where(h0, prod, 0.0), axis=-1, keepdims=True)
            d1 = jnp.sum(jnp.where(h0, 0.0, prod), axis=-1, keepdims=True)
            dd_s[rows, :] = jnp.where(h0, d0, d1)
            return carry

        lax.fori_loop(0, PAD_UNIT // 256, rowdot, 0)

        for bi, d in enumerate(DILATIONS):
            def group(jg, carry, bi=bi, d=d):
                ng = ATTN_GROUP
                meta, qs, dos, lqs, dds, ks, vs, pens = [], [], [], [], [], [], [], []
                for t in range(ng):
                    loc, start, pstart, first = _attn_item_rows(jg * ng + t, d, c, cbase)
                    qrows = pl.ds(loc, BAND, stride=d)
                    rows = pl.ds(start, BAND, stride=d)
                    prows = pl.ds(pstart, BAND, stride=d)
                    meta.append((qrows, rows, prows))
                    qs.append(q_ref[qrows, :])
                    dos.append(dy_ref[qrows, :])
                    lqs.append(lse_ref[qrows, :])
                    dds.append(dd_s[qrows, :])
                    ks.append(jnp.concatenate([k_ref[prows, :], k_ref[rows, :]], axis=0))
                    vs.append(jnp.concatenate([v_ref[prows, :], v_ref[rows, :]], axis=0))
                    pens.append(jnp.where(cur_half, 0.0, jnp.where(first, NEG_INF, 0.0)))

                def heads(t):
                    return jnp.concatenate([jnp.where(m0, t, 0.0), jnp.where(m0, 0.0, t)], axis=0)

                def head_col(t):
                    return jnp.concatenate([t[:, :, 0:1], t[:, :, HEAD_DIM:HEAD_DIM + 1]], axis=0)

                qh = (heads(jnp.stack(qs)) * 0.125).astype(bf16)
                doh = heads(jnp.stack(dos)).astype(bf16)
                kk = jnp.stack(ks + ks).astype(bf16)
                vv = jnp.stack(vs + vs).astype(bf16)
                sc = lax.dot_general(qh, kk, _BNT, preferred_element_type=f32)
                sc = (sc.reshape(2, ng, BAND, 2 * BAND) + b_ref[bi][:, None]).reshape(sc.shape) + jnp.stack(pens + pens)
                p = jnp.exp(sc - head_col(jnp.stack(lqs)))
                dp = lax.dot_general(doh, vv, _BNT, preferred_element_type=f32)
                ds = p * (dp - head_col(jnp.stack(dds)))
                db_ref[bi] += jnp.sum(ds.reshape(2, ng, BAND, 2 * BAND), axis=1)
                dsb = ds.astype(bf16)
                dq = lax.dot_general(dsb, kk, _BNN, preferred_element_type=f32) * 0.125
                dk = lax.dot_general(dsb, qh, _BTN, preferred_element_type=f32)
                dv = lax.dot_general(p.astype(bf16), doh, _BTN, preferred_element_type=f32)
                for t in range(ng):
                    qrows, rows, prows = meta[t]
                    dq_ref[qrows, :] += jnp.where(m0, dq[t], dq[ng + t])
                    dkt = dk[t] + dk[ng + t]
                    dvt = dv[t] + dv[ng + t]
                    dk_ref[prows, :] += dkt[:BAND]
                    dk_ref[rows, :] += dkt[BAND:]
                    dv_ref[prows, :] += dvt[:BAND]
                    dv_ref[rows, :] += dvt[BAND:]
                return carry

            lax.fori_loop(0, ATTN_ITEMS // ATTN_GROUP, group, 0)

    chunk = lambda col: pl.BlockSpec((PAD_UNIT, 128), lambda p, c: (c, col + p))
    full = lambda col: pl.BlockSpec((s, 128), lambda p, c: (0, col + p))
    bias_spec = pl.BlockSpec((3, 2, BAND, 2 * BAND), lambda p, c: (0, p, 0, 0))
    return pl.pallas_call(
        body, name="attn_bwd", grid=(N_HEADS // 2, s // PAD_UNIT),
        in_specs=[chunk(Q_COL), full(K_COL), full(V_COL), bias_spec, chunk(0), chunk(0), chunk(0)],
        out_specs=[chunk(0), full(0), full(0), bias_spec],
        out_shape=[jax.ShapeDtypeStruct((s, GROUP_W), f32)] * 3
        + [jax.ShapeDtypeStruct((3, N_HEADS, BAND, 2 * BAND), f32)],
        scratch_shapes=[pltpu.VMEM((PAD_UNIT, 128), f32)],
        compiler_params=_cparams(("parallel", "arbitrary")),
    )(proj, proj, proj, bias, y, lse, dycat)


_HI = lax.Precision.HIGHEST
DELTA_COL = 1536
Z_COL = 3072
BA_BLOCK = 28
DELTA_ROWS = 1024


def _hdot(a, b):
    return jnp.dot(a, b, precision=_HI, preferred_element_type=f32)


_DIMS = dict(nn=(((2,), (1,)), ((0,), (0,))), nt=(((2,), (2,)), ((0,), (0,))), tn=(((1,), (1,)), ((0,), (0,))))


@functools.partial(jax.custom_vjp, nondiff_argnums=(2,))
def _mmx(a, b, mode):
    return lax.dot_general(a.astype(bf16), b.astype(bf16), _DIMS[mode], preferred_element_type=f32)


def _mmx_fwd(a, b, mode):
    return _mmx(a, b, mode), (a, b)


def _mmx_bwd(mode, res, g):
    a, b = res
    if mode == "nn":
        return _mmx(g, b, "nt"), _mmx(a, g, "tn")
    if mode == "nt":
        return _mmx(g, b, "nn"), _mmx(g, a, "tn")
    return _mmx(b, g, "nt"), _mmx(a, g, "nn")


_mmx.defvjp(_mmx_fwd, _mmx_bwd)


def _pair_iota():
    row = lax.broadcasted_iota(jnp.int32, (CHUNK, 128), 0)
    lane = lax.broadcasted_iota(jnp.int32, (CHUNK, 128), 1)
    return row, lane, lane & (CHUNK - 1)


def _bd(x):
    _, lane, _ = _pair_iota()
    m0 = lane < CHUNK
    return jnp.concatenate([jnp.where(m0, x, 0.0), jnp.where(m0, 0.0, x)], axis=1)


def _pmm(a, b):
    return _mmx(a, _bd(b), "nn")


def _ntp(x, y):
    return _mmx(x, _bd(y), "nt")


def _tnp(x, y):
    full = _mmx(x, y, "tn")
    _, lane, _ = _pair_iota()
    return jnp.where(lane < CHUNK, full[:, :CHUNK], full[:, CHUNK:])


def _tri_inv(a):
    row, lane, jj = _pair_iota()
    eye = jnp.where(row == jj, 1.0, 0.0).astype(f32)

    def same_block(log2b):
        return (row >> log2b) == (jj >> log2b)

    dgl = jnp.where(same_block(3), a, 0.0)
    d2 = _pmm(dgl, dgl)
    d4 = _pmm(d2, d2)
    t = _pmm(_pmm(eye - dgl, eye + d2), eye + d4)
    for lb in (3, 4, 5):
        off = jnp.where(same_block(lb + 1) & jnp.logical_not(same_block(lb)), a, 0.0)
        t = t - _pmm(_pmm(t, off), t)
    return t


@jax.custom_vjp
def _solve2(a, xv, xk, t):
    return _pmm(t, xv), _pmm(t, xk)


def _solve2_fwd(a, xv, xk, t):
    u, w = _pmm(t, xv), _pmm(t, xk)
    return (u, w), (t, u, w)


def _solve2_bwd(res, cts):
    t, u, w = res
    du, dw = cts
    dxv = _tnp(t, du)
    dxk = _tnp(t, dw)
    return -(_ntp(dxv, u) + _ntp(dxk, w)), dxv, dxk, jnp.zeros_like(t)


_solve2.defvjp(_solve2_fwd, _solve2_bwd)


def _chunk_pre(qp, kp, vp, bp, gcum, t=None):
    row, lane, jj = _pair_iota()
    causal = row >= jj
    strict = row > jj
    rsel = jnp.sum(jnp.where(row == jj, gcum, 0.0), axis=1, keepdims=True)
    decay = jnp.where(causal, jnp.exp(jnp.where(causal, gcum - rsel, 0.0)), 0.0)
    kb = kp * bp
    kd = _bd(kp)
    a = jnp.where(strict, _mmx(kb, kd, "nt") * decay, 0.0)
    eg = jnp.exp(gcum)
    if t is None:
        t = _tri_inv(a)
    u, w = _solve2(a, vp * bp, kb * eg, t)
    qk = jnp.where(causal, _mmx(qp, kd, "nt") * decay, 0.0)
    glast = jnp.sum(jnp.where(row == CHUNK - 1, gcum, 0.0), axis=1, keepdims=True)
    return u, w, qp * eg, kp * jnp.exp(glast - gcum), qk, jnp.exp(glast), t


def _chunk_post(u, w, qt, kh, qk, gam, sp):
    sd = _bd(sp)
    vnew = u - _mmx(w, sd, "nn")
    o = _mmx(qt, sd, "nn") + _pmm(qk, vnew)
    return o, gam * sp + _tnp(kh, vnew)


def _pair_spec(rows=DELTA_ROWS):
    return pl.BlockSpec((rows, 128), lambda i, p: (i, p))


DELTA_NB = DELTA_ROWS // CHUNK


def _chunks(ref):
    return ref[...].reshape(DELTA_NB, CHUNK, 128)


def _pairs(ref, rows):
    return jnp.stack([ref[rows, p * 128:(p + 1) * 128] for p in range(4)], axis=0)


def _delta_chunk_pre(qn, kn, sv, beta, g, xchg):
    s = qn.shape[0]

    def body(q_ref, k_ref, v_ref, b_ref, g_ref, u_ref, w_ref, qt_ref, kh_ref, qk_ref, t_ref, gm_ref):
        outs = _chunk_pre(_chunks(q_ref), _chunks(k_ref), _chunks(v_ref), _chunks(b_ref), _chunks(g_ref))
        for ref, val in zip((u_ref, w_ref, qt_ref, kh_ref, qk_ref, t_ref), outs[:5] + outs[6:]):
            ref[...] = val.reshape(DELTA_ROWS, 128).astype(ref.dtype)
        gm_ref[...] = jnp.broadcast_to(outs[5], (DELTA_NB, 8, 128)).reshape(DELTA_NB * 8, 128)

    v_spec = pl.BlockSpec((DELTA_ROWS, 128), lambda i, p: (i, 8 + p))
    grid = (s // DELTA_ROWS, 4)
    out = pl.pallas_call(
        _ride(body, 5, 7, xchg, grid), name="delta_chunk_pre", grid=grid,
        in_specs=[_pair_spec(), _pair_spec(), v_spec, _pair_spec(), _pair_spec()] + [_ANY] * xchg.n,
        out_specs=[_pair_spec()] * 6 + [_pair_spec(DELTA_NB * 8)] + [_ANY] * xchg.n,
        out_shape=[jax.ShapeDtypeStruct((s, GROUP_W), f32)] + [jax.ShapeDtypeStruct((s, GROUP_W), bf16)] * 5
        + [jax.ShapeDtypeStruct((s // 8, GROUP_W), f32)] + xchg.out_shape(),
        scratch_shapes=xchg.scratch(),
        compiler_params=_cparams(("arbitrary", "arbitrary")),
    )(qn, kn, sv, beta, g, *xchg.arrs)
    return out[:7], out[7:]


def _delta_scan_fwd(u, w, qt, kh, qk, gm):
    s = u.shape[0]

    def body(u_ref, w_ref, qt_ref, kh_ref, qk_ref, gm_ref, o_ref, ss_ref, st):
        @pl.when(pl.program_id(0) == 0)
        def _():
            st[...] = jnp.zeros_like(st)

        def chunk(ci, carry):
            rows = pl.ds(pl.multiple_of(ci * CHUNK, CHUNK), CHUNK)
            grow = pl.ds(pl.multiple_of(ci * 8, 8), 1)
            sp = st[...]
            o, s2 = _chunk_post(_pairs(u_ref, rows), _pairs(w_ref, rows), _pairs(qt_ref, rows),
                                _pairs(kh_ref, rows), _pairs(qk_ref, rows), _pairs(gm_ref, grow), sp)
            for p in range(4):
                ss_ref[rows, p * 128:(p + 1) * 128] = sp[p]
                o_ref[rows, p * 128:(p + 1) * 128] = o[p]
            st[...] = s2
            return carry

        lax.fori_loop(0, DELTA_NB, chunk, 0)

    spec = pl.BlockSpec((DELTA_ROWS, GROUP_W), lambda i: (i, 0))
    gspec = pl.BlockSpec((DELTA_NB * 8, GROUP_W), lambda i: (i, 0))
    return pl.pallas_call(
        body, name="delta_scan_fwd", grid=(s // DELTA_ROWS,),
        in_specs=[spec] * 5 + [gspec],
        out_specs=[spec, spec],
        out_shape=[jax.ShapeDtypeStruct((s, GROUP_W), f32)] * 2,
        scratch_shapes=[pltpu.VMEM((4, CHUNK, 128), f32)],
        compiler_params=_cparams(("arbitrary",)),
    )(u, w, qt, kh, qk, gm)


def _delta_scan_bwd(w, qt, kh, qk, gm, do, xchg):
    s = w.shape[0]
    nb = s // DELTA_ROWS

    def body(w_ref, qt_ref, kh_ref, qk_ref, gm_ref, do_ref, dso_ref, dst):
        @pl.when(pl.program_id(0) == 0)
        def _():
            dst[...] = jnp.zeros_like(dst)

        def chunk(t, carry):
            ci = DELTA_NB - 1 - t
            rows = pl.ds(pl.multiple_of(ci * CHUNK, CHUNK), CHUNK)
            grow = pl.ds(pl.multiple_of(ci * 8, 8), 1)
            ds = dst[...]
            for p in range(4):
                dso_ref[rows, p * 128:(p + 1) * 128] = ds[p]
            do = _pairs(do_ref, rows)
            dvn = _tnp(_pairs(qk_ref, rows), do) + _pmm(_pairs(kh_ref, rows), ds)
            dst[...] = _tnp(_pairs(qt_ref, rows), do) + _pairs(gm_ref, grow) * ds - _tnp(_pairs(w_ref, rows), dvn)
            return carry

        lax.fori_loop(0, DELTA_NB, chunk, 0)

    spec = pl.BlockSpec((DELTA_ROWS, GROUP_W), lambda i: (nb - 1 - i, 0))
    gspec = pl.BlockSpec((DELTA_NB * 8, GROUP_W), lambda i: (nb - 1 - i, 0))
    out = pl.pallas_call(
        _ride(body, 6, 1, xchg, (nb,)), name="delta_scan_bwd", grid=(nb,),
        in_specs=[spec] * 4 + [gspec, spec] + [_ANY] * xchg.n,
        out_specs=[spec] + [_ANY] * xchg.n,
        out_shape=[jax.ShapeDtypeStruct((s, GROUP_W), f32)] + xchg.out_shape(),
        scratch_shapes=[pltpu.VMEM((4, CHUNK, 128), f32)] + xchg.scratch(),
        compiler_params=_cparams(("arbitrary",)),
    )(w, qt, kh, qk, gm, do, *xchg.arrs)
    return out[0], out[1:]


def _delta_chunk_bwd(qn, kn, sv, beta, g, tinv, ss, dso, do, xchg):
    s = qn.shape[0]

    def body(q_ref, k_ref, v_ref, b_ref, g_ref, t_ref, ss_ref, dso_ref, do_ref,
             dq_ref, dk_ref, dv_ref, db_ref, dg_ref):
        sp = _chunks(ss_ref)
        t = _chunks(t_ref)

        def fn(q, k, v, b, gg):
            return _chunk_post(*_chunk_pre(q, k, v, b, gg, t)[:6], sp)

        _, vjp = jax.vjp(fn, _chunks(q_ref), _chunks(k_ref), _chunks(v_ref), _chunks(b_ref), _chunks(g_ref))
        grads = vjp((_chunks(do_ref), _chunks(dso_ref)))
        for ref, val in zip((dq_ref, dk_ref, dv_ref, db_ref, dg_ref), grads):
            ref[...] = val.reshape(DELTA_ROWS, 128)

    v_spec = pl.BlockSpec((DELTA_ROWS, 128), lambda i, p: (i, 8 + p))
    grid = (s // DELTA_ROWS, 4)
    out = pl.pallas_call(
        _ride(body, 9, 5, xchg, grid), name="delta_chunk_bwd", grid=grid,
        in_specs=[_pair_spec(), _pair_spec(), v_spec] + [_pair_spec()] * 6 + [_ANY] * xchg.n,
        out_specs=[_pair_spec()] * 5 + [_ANY] * xchg.n,
        out_shape=[jax.ShapeDtypeStruct((s, GROUP_W), f32)] * 5 + xchg.out_shape(),
        scratch_shapes=xchg.scratch(),
        compiler_params=_cparams(("arbitrary", "arbitrary")),
    )(qn, kn, sv, beta, g, tinv, ss, dso, do, *xchg.arrs)
    return out[:5], out[5:]


def _head_sums(x):
    r = lax.broadcasted_iota(jnp.int32, (128, 128), 0)
    c = lax.broadcasted_iota(jnp.int32, (128, 128), 1)
    pair = jnp.where((r >> 6) == (c >> 6), 1.0, 0.0).astype(f32)
    npair = x.shape[1] // 128
    xb = jnp.concatenate([x[None, :, p * 128:(p + 1) * 128] for p in range(npair)], axis=0)
    sums = _mmx(xb, jnp.broadcast_to(pair, (npair, 128, 128)), "nn")
    return jnp.concatenate([sums[p] for p in range(npair)], axis=1)


def _sel_dot(a, b):
    return jnp.dot(a, b, precision=lax.Precision.HIGH, preferred_element_type=f32)


def _expand_matrix(first):
    r = lax.broadcasted_iota(jnp.int32, (128, GROUP_W), 0)
    c = lax.broadcasted_iota(jnp.int32, (128, GROUP_W), 1) >> 6
    return jnp.where(r == c + first, 1.0, 0.0).astype(f32)


@functools.partial(jax.custom_vjp, nondiff_argnums=(1,))
def _expand_heads(ba, first):
    return _sel_dot(ba, _expand_matrix(first))


def _expand_heads_fwd(ba, first):
    return _expand_heads(ba, first), None


def _expand_heads_bwd(first, _, g):
    return (_mmx(g[None], _expand_matrix(first)[None], "nt")[0],)


_expand_heads.defvjp(_expand_heads_fwd, _expand_heads_bwd)


def _softplus(x):
    return jnp.maximum(x, 0.0) + jnp.log(1.0 + jnp.exp(-jnp.abs(x)))


def _prep_fn(sq, sk, ba, alog_e, dt_e):
    qn = sq * lax.rsqrt(_head_sums(sq * sq) + EPS) * (HEAD_DIM ** -0.5)
    kn = sk * lax.rsqrt(_head_sums(sk * sk) + EPS)
    bl = _expand_heads(ba, 0)
    al = _expand_heads(ba, N_HEADS)
    beta = jax.nn.sigmoid(bl)
    g = -jnp.exp(alog_e) * _softplus(al + dt_e)
    nchunk = g.shape[0] // CHUNK
    ri = lax.broadcasted_iota(jnp.int32, (nchunk, CHUNK, CHUNK), 1)
    ci = lax.broadcasted_iota(jnp.int32, (nchunk, CHUNK, CHUNK), 2)
    tril = jnp.where(ri >= ci, 1.0, 0.0).astype(f32)
    gcum = lax.dot_general(tril, g.reshape(nchunk, CHUNK, g.shape[1]), _BNN, precision=lax.Precision.HIGH,
                           preferred_element_type=f32)
    return qn, kn, beta, gcum.reshape(g.shape)


def _gnorm_fn(o, z, ng_e):
    ms = _head_sums(o * o) * (1.0 / HEAD_DIM)
    return o * lax.rsqrt(ms + EPS) * ng_e * (z * jax.nn.sigmoid(z))


def _tok_spec(width, col):
    return pl.BlockSpec((TOK_TILE, width), lambda i: (i, col))


def _conv_taps(xs_ref, w_ref, base, n, cols):
    acc = w_ref[CONV_WIDTH - 1:CONV_WIDTH, cols] * xs_ref[pl.ds(base, n), cols]
    for j in range(CONV_WIDTH - 1):
        acc = acc + w_ref[j:j + 1, cols] * xs_ref[pl.ds(base - (CONV_WIDTH - 1) + j, n), cols]
    return acc


def _conv_silu_fwd(proj, conv_w):
    s = proj.shape[0]
    wd = 3 * GROUP_W
    hb = TOK_TILE // 8

    def body(x_ref, halo_ref, w_ref, o_ref, y_ref, xs):
        inner = pl.program_id(0) > 0

        def lane_block(cb, carry):
            cols = pl.ds(pl.multiple_of(cb * 128, 128), 128)
            xs[0:8, cols] = jnp.where(inner, halo_ref[:, cols], 0.0)
            xs[8:, cols] = x_ref[:, cols]
            y = _conv_taps(xs, w_ref, 8, TOK_TILE, cols)
            y_ref[:, cols] = y
            o_ref[:, cols] = y * jax.nn.sigmoid(y)
            return carry

        lax.fori_loop(0, wd // 128, lane_block, 0)

    return pl.pallas_call(
        body, name="delta_conv_fwd", grid=(s // TOK_TILE,),
        in_specs=[_tok_spec(wd, 1), pl.BlockSpec((8, wd), lambda i: (jnp.maximum(i * hb - 1, 0), 1)),
                  pl.BlockSpec((CONV_WIDTH, wd), lambda i: (0, 0))],
        out_specs=[_tok_spec(wd, 0)] * 2,
        out_shape=[jax.ShapeDtypeStruct((s, wd), f32)] * 2,
        scratch_shapes=[pltpu.VMEM((TOK_TILE + 8, wd), f32)],
        compiler_params=_cparams(("parallel",)),
    )(proj, proj, conv_w)


def _conv_silu_bwd(proj, conv_w, yc, ds3, xchg):
    s = proj.shape[0]
    wd = 3 * GROUP_W
    hb = TOK_TILE // 8
    nt = s // TOK_TILE

    def body(x_ref, hp_ref, y_ref, yn_ref, dq_ref, dk_ref, dv_ref, dqn_ref, dkn_ref, dvn_ref, w_ref,
             dx_ref, dw_ref, xs, dys):
        i = pl.program_id(0)

        @pl.when(i == 0)
        def _():
            dw_ref[...] = jnp.zeros_like(dw_ref)

        last = i == nt - 1
        def lane_block(lb, carry, third, cur, nxt):
            tcols = pl.ds(pl.multiple_of(lb * 128, 128), 128)
            cols = pl.ds(pl.multiple_of(third * GROUP_W + lb * 128, 128), 128)
            xs[0:8, cols] = jnp.where(i > 0, hp_ref[:, cols], 0.0)
            xs[8:, cols] = x_ref[:, cols]
            y = y_ref[:, cols]
            sg = jax.nn.sigmoid(y)
            dy0 = cur[:, tcols] * (sg * (1.0 + y * (1.0 - sg)))
            dys[0:TOK_TILE, cols] = dy0
            yn = yn_ref[:, cols]
            sgn = jax.nn.sigmoid(yn)
            dys[TOK_TILE:, cols] = jnp.where(last, 0.0, nxt[:, tcols]) * (sgn * (1.0 + yn * (1.0 - sgn)))
            dx = w_ref[CONV_WIDTH - 1:CONV_WIDTH, cols] * dy0
            for j in range(CONV_WIDTH - 1):
                dx = dx + w_ref[j:j + 1, cols] * dys[pl.ds(CONV_WIDTH - 1 - j, TOK_TILE), cols]
            dx_ref[:, cols] = dx.astype(dx_ref.dtype)
            for j in range(CONV_WIDTH):
                dw_ref[j:j + 1, cols] += jnp.sum(dy0 * xs[pl.ds(8 - (CONV_WIDTH - 1) + j, TOK_TILE), cols],
                                                 axis=0, keepdims=True)
            return carry

        for third, (cur, nxt) in enumerate(((dq_ref, dqn_ref), (dk_ref, dkn_ref), (dv_ref, dvn_ref))):
            lax.fori_loop(0, GROUP_W // 128, functools.partial(lane_block, third=third, cur=cur, nxt=nxt), 0)

    prev8 = lambda col: pl.BlockSpec((8, wd), lambda i: (jnp.maximum(i * hb - 1, 0), col))
    next8 = lambda col: pl.BlockSpec((8, wd), lambda i: (jnp.minimum((i + 1) * hb, s // 8 - 1), col))
    next8_third = pl.BlockSpec((8, GROUP_W), lambda i: (jnp.minimum((i + 1) * hb, s // 8 - 1), 0))
    out = pl.pallas_call(
        _ride(body, 11, 2, xchg, (nt,)), name="delta_conv_bwd", grid=(nt,),
        in_specs=[_tok_spec(wd, 1), prev8(1), _tok_spec(wd, 0), next8(0)] + [_tok_spec(GROUP_W, 0)] * 3
        + [next8_third] * 3
        + [pl.BlockSpec((CONV_WIDTH, wd), lambda i: (0, 0))] + [_ANY] * xchg.n,
        out_specs=[_tok_spec(wd, 0), pl.BlockSpec((CONV_WIDTH, wd), lambda i: (0, 0))] + [_ANY] * xchg.n,
        out_shape=[jax.ShapeDtypeStruct((s, wd), bf16), jax.ShapeDtypeStruct((CONV_WIDTH, wd), f32)] + xchg.out_shape(),
        scratch_shapes=[pltpu.VMEM((TOK_TILE + 8, wd), f32), pltpu.VMEM((TOK_TILE + 8, wd), f32)] + xchg.scratch(),
        compiler_params=_cparams(("arbitrary",)),
    )(proj, proj, yc, yc, *ds3, *ds3, conv_w, *xchg.arrs)
    return out[:2], out[2:]


def _delta_prep_fwd(sconv, proj, alog_e, dt_e):
    s = sconv.shape[0]

    def body(sq_ref, sk_ref, ba_ref, al_ref, dt_ref, q_ref, k_ref, b_ref, g_ref):
        qn, kn, beta, g = _prep_fn(sq_ref[...], sk_ref[...], ba_ref[...], al_ref[...], dt_ref[...])
        q_ref[...] = qn
        k_ref[...] = kn
        b_ref[...] = beta
        g_ref[...] = g

    return pl.pallas_call(
        body, name="delta_prep_fwd", grid=(s // TOK_TILE,),
        in_specs=[_tok_spec(GROUP_W, 0), _tok_spec(GROUP_W, 1), _tok_spec(128, BA_BLOCK),
                  _vec_spec(GROUP_W), _vec_spec(GROUP_W)],
        out_specs=[_tok_spec(GROUP_W, 0)] * 4,
        out_shape=[jax.ShapeDtypeStruct((s, GROUP_W), f32)] * 4,
        compiler_params=_cparams(("parallel",)),
    )(sconv, sconv, proj, alog_e, dt_e)


def _delta_prep_bwd(sconv, proj, alog_e, dt_e, dqn, dkn, dbeta, dg, xchg):
    s = sconv.shape[0]
    grid = (s // TOK_TILE,)

    def body(sq_ref, sk_ref, ba_ref, al_ref, dt_ref, dq_ref, dk_ref, db_ref, dg_ref,
             dsq_ref, dsk_ref, dba_ref, dal_ref, ddt_ref):
        @pl.when(pl.program_id(0) == 0)
        def _():
            dal_ref[...] = jnp.zeros_like(dal_ref)
            ddt_ref[...] = jnp.zeros_like(ddt_ref)

        _, vjp = jax.vjp(_prep_fn, sq_ref[...], sk_ref[...], ba_ref[...], al_ref[...], dt_ref[...])
        dsq, dsk, dba, dal, ddt = vjp((dq_ref[...], dk_ref[...], db_ref[...], dg_ref[...]))
        dsq_ref[...] = dsq
        dsk_ref[...] = dsk
        dba_ref[...] = dba.astype(bf16)
        dal_ref[...] += dal
        ddt_ref[...] += ddt

    out = pl.pallas_call(
        _ride(body, 9, 5, xchg, grid), name="delta_prep_bwd", grid=grid,
        in_specs=[_tok_spec(GROUP_W, 0), _tok_spec(GROUP_W, 1), _tok_spec(128, BA_BLOCK),
                  _vec_spec(GROUP_W), _vec_spec(GROUP_W)] + [_tok_spec(GROUP_W, 0)] * 4 + [_ANY] * xchg.n,
        out_specs=[_tok_spec(GROUP_W, 0), _tok_spec(GROUP_W, 0), _tok_spec(128, 0),
                   _acc_spec(GROUP_W), _acc_spec(GROUP_W)] + [_ANY] * xchg.n,
        out_shape=[jax.ShapeDtypeStruct((s, GROUP_W), f32)] * 2 + [jax.ShapeDtypeStruct((s, 128), bf16)]
        + [jax.ShapeDtypeStruct((1, GROUP_W), f32)] * 2 + xchg.out_shape(),
        scratch_shapes=xchg.scratch(),
        compiler_params=_cparams(("arbitrary",)),
    )(sconv, sconv, proj, alog_e, dt_e, dqn, dkn, dbeta, dg, *xchg.arrs)
    return out[:5], out[5:]


def _gnorm_fwd(o, proj, ng_e):
    s = o.shape[0]

    def body(o_ref, z_ref, g_ref, y_ref):
        y_ref[...] = _gnorm_fn(o_ref[...], z_ref[...], g_ref[...])

    return pl.pallas_call(
        body, name="delta_gnorm_fwd", grid=(s // TOK_TILE,),
        in_specs=[_tok_spec(GROUP_W, 0), _tok_spec(GROUP_W, Z_COL // GROUP_W), _vec_spec(GROUP_W)],
        out_specs=_tok_spec(GROUP_W, 0),
        out_shape=jax.ShapeDtypeStruct((s, GROUP_W), f32),
        compiler_params=_cparams(("parallel",)),
    )(o, proj, ng_e)


def _gnorm_bwd(o, proj, ng_e, dycat):
    s = o.shape[0]

    def body(o_ref, z_ref, g_ref, dy_ref, do_ref, dz_ref, dg_ref):
        @pl.when(pl.program_id(0) == 0)
        def _():
            dg_ref[...] = jnp.zeros_like(dg_ref)

        _, vjp = jax.vjp(_gnorm_fn, o_ref[...], z_ref[...], g_ref[...])
        do, dz, dg = vjp(dy_ref[...])
        do_ref[...] = do
        dz_ref[...] = dz.astype(bf16)
        dg_ref[...] += dg

    return pl.pallas_call(
        body, name="delta_gnorm_bwd", grid=(s // TOK_TILE,),
        in_specs=[_tok_spec(GROUP_W, 0), _tok_spec(GROUP_W, Z_COL // GROUP_W), _vec_spec(GROUP_W),
                  _tok_spec(GROUP_W, 1)],
        out_specs=[_tok_spec(GROUP_W, 0), _tok_spec(GROUP_W, 0), _acc_spec(GROUP_W)],
        out_shape=[jax.ShapeDtypeStruct((s, GROUP_W), f32), jax.ShapeDtypeStruct((s, GROUP_W), bf16),
                   jax.ShapeDtypeStruct((1, GROUP_W), f32)],
        compiler_params=_cparams(("arbitrary",)),
    )(o, proj, ng_e, dycat)


_MESH = pl.DeviceIdType.MESH
_ANY = pl.BlockSpec(memory_space=pl.ANY)
_VMEM = pl.BlockSpec(memory_space=pltpu.VMEM)


def _my_place():
    x, y, c = lax.axis_index("x"), lax.axis_index("y"), lax.axis_index("c")
    return x, y, c, 4 * x + 2 * y + c


def _peer(k, x, y, c):
    px = 1 - x if k & 4 else x
    py = 1 - y if k & 2 else y
    pc = 1 - c if k & 1 else c
    return (px, py, pc), 4 * px + 2 * py + pc


def _exchange_all(src_of_peer, dst_ref, send_sems, recv_sems, x, y, c, me):
    sent = []
    for k in range(1, N_DEV):
        dev, pidx = _peer(k, x, y, c)
        cp = pltpu.make_async_remote_copy(src_ref=src_of_peer(pidx), dst_ref=dst_ref.at[me],
                                          send_sem=send_sems.at[k - 1], recv_sem=recv_sems.at[k - 1],
                                          device_id=dev, device_id_type=_MESH)
        cp.start()
        sent.append(cp)
    for k in range(1, N_DEV):
        dev, pidx = _peer(k, x, y, c)
        pltpu.make_async_remote_copy(src_ref=src_of_peer(pidx), dst_ref=dst_ref.at[pidx],
                                     send_sem=send_sems.at[k - 1], recv_sem=recv_sems.at[k - 1],
                                     device_id=dev, device_id_type=_MESH).wait_recv()
    for cp in sent:
        cp.wait_send()


def _ada_exchange(cv8, w_ada, b_ada8):
    def body(cv_ref, w_ref, b_ref, call_ref, modp_ref, part_s, s1, r1, s2, r2):
        x, y, c, me = _my_place()
        call_ref[me] = cv_ref[...]
        _exchange_all(lambda pidx: cv_ref, call_ref, s1, r1, x, y, c, me)
        bias = b_ref[me]
        for j in range(N_DEV):
            cj = call_ref[j][:, :D_MODEL]
            part_s[j] = _hdot(cj * jax.nn.sigmoid(cj), w_ref[...]) + bias
        modp_ref[me] = part_s[me]
        _exchange_all(lambda pidx: part_s.at[pidx], modp_ref, s2, r2, x, y, c, me)

    nsh = w_ada.shape[1]
    return pl.pallas_call(
        body, name="ada_exchange",
        in_specs=[_VMEM, _VMEM, _VMEM], out_specs=[_VMEM, _VMEM],
        out_shape=[jax.ShapeDtypeStruct((N_DEV, 8, cv8.shape[1]), f32), jax.ShapeDtypeStruct((N_DEV, 8, nsh), f32)],
        scratch_shapes=[pltpu.VMEM((N_DEV, 8, nsh), f32)] + [pltpu.SemaphoreType.DMA((N_DEV - 1,))] * 4,
        compiler_params=pltpu.CompilerParams(vmem_limit_bytes=VMEM_LIMIT),
    )(cv8, w_ada, b_ada8)


def _all_to_all(arrs, name):
    ex = _Exchange(arrs, gather=False)

    def body(*refs):
        srcs, dsts, sems = refs[:ex.n], refs[ex.n:2 * ex.n], refs[2 * ex.n:]
        ex.start(srcs, dsts, sems)
        ex.wait(srcs, dsts, sems)

    return pl.pallas_call(
        body, name=name,
        in_specs=[_ANY] * ex.n, out_specs=[_ANY] * ex.n,
        out_shape=ex.out_shape(), scratch_shapes=ex.scratch(),
    )(*arrs)


class _Exchange:
    def __init__(self, arrs, gather):
        self.arrs, self.gather, self.n = list(arrs), gather, len(arrs)

    def out_shape(self):
        return [jax.ShapeDtypeStruct(((N_DEV,) + a.shape) if self.gather else a.shape, a.dtype) for a in self.arrs]

    def scratch(self):
        if self.n == 0:
            return []
        return [pltpu.SemaphoreType.DMA((self.n, N_DEV - 1)), pltpu.SemaphoreType.DMA((self.n, N_DEV - 1)),
                pltpu.SemaphoreType.DMA((self.n,))]

    def _src(self, srcs, a, idx):
        return srcs[a] if self.gather else srcs[a].at[idx]

    def _copies(self, srcs, dsts, sems, incoming):
        send_sems, recv_sems, _ = sems
        x, y, c, me = _my_place()
        out = []
        for a in range(self.n):
            for k in range(1, N_DEV):
                dev, pidx = _peer(k, x, y, c)
                out.append(pltpu.make_async_remote_copy(
                    src_ref=self._src(srcs, a, pidx), dst_ref=dsts[a].at[pidx if incoming else me],
                    send_sem=send_sems.at[a, k - 1], recv_sem=recv_sems.at[a, k - 1],
                    device_id=dev, device_id_type=_MESH))
        return out

    def _local(self, srcs, dsts, sems):
        me = _my_place()[3]
        return [pltpu.make_async_copy(self._src(srcs, a, me), dsts[a].at[me], sems[2].at[a]) for a in range(self.n)]

    def start(self, srcs, dsts, sems):
        for cp in self._local(srcs, dsts, sems) + self._copies(srcs, dsts, sems, incoming=False):
            cp.start()

    def wait(self, srcs, dsts, sems):
        for cp in self._copies(srcs, dsts, sems, incoming=True):
            cp.wait_recv()
        for cp in self._copies(srcs, dsts, sems, incoming=False):
            cp.wait_send()
        for cp in self._local(srcs, dsts, sems):
            cp.wait()

    def start_at_first_step(self, grid, srcs, dsts, sems):
        first = functools.reduce(jnp.logical_and, [pl.program_id(i) == 0 for i in range(len(grid))])
        pl.when(first)(lambda: self.start(srcs, dsts, sems))

    def wait_at_last_step(self, grid, srcs, dsts, sems):
        last = functools.reduce(jnp.logical_and, [pl.program_id(i) == g - 1 for i, g in enumerate(grid)])
        pl.when(last)(lambda: self.wait(srcs, dsts, sems))


class _ChipGather:
    def __init__(self, shard):
        self.shard = shard

    def out_shape(self):
        return jax.ShapeDtypeStruct((N_DEV,) + self.shard.shape, self.shard.dtype)

    def scratch(self):
        return [pltpu.SemaphoreType.DMA((N_DEV - 1,)), pltpu.SemaphoreType.DMA((N_DEV - 1,)),
                pltpu.SemaphoreType.DMA(())]

    def _place(self):
        x, y, c, me = _my_place()
        return x, y, c, me, (x, y, 1 - c), [(1 - x, y), (x, 1 - y), (1 - x, 1 - y)]

    def _copy(self, out, sems, k, block, to, src=None):
        rows = out.at[4 * block[0] + 2 * block[1] + block[2]]
        return pltpu.make_async_remote_copy(src_ref=rows if src is None else src, dst_ref=rows,
                                            send_sem=sems[0].at[k], recv_sem=sems[1].at[k],
                                            device_id=to, device_id_type=_MESH)

    def start(self, src, out, sems):
        x, y, c, me, sib, chips = self._place()
        pltpu.make_async_copy(src, out.at[me], sems[2]).start()
        self._copy(out, sems, 0, (x, y, c), sib, src=src).start()
        for j, chip in enumerate(chips):
            self._copy(out, sems, 1 + j, (x, y, c), (*chip, c), src=src).start()

    def forward(self, src, out, sems):
        x, y, c, me, sib, chips = self._place()
        for j, chip in enumerate(chips):
            self._copy(out, sems, 1 + j, (*chip, c), (x, y, c)).wait_recv()
            self._copy(out, sems, 4 + j, (*chip, c), sib).start()

    def finish(self, src, out, sems):
        x, y, c, me, sib, chips = self._place()
        self._copy(out, sems, 0, (x, y, 1 - c), (x, y, c)).wait_recv()
        for j, chip in enumerate(chips):
            self._copy(out, sems, 4 + j, (*chip, 1 - c), (x, y, c)).wait_recv()
        self._copy(out, sems, 0, (x, y, c), sib, src=src).wait_send()
        for j, chip in enumerate(chips):
            self._copy(out, sems, 1 + j, (x, y, c), (*chip, c), src=src).wait_send()
            self._copy(out, sems, 4 + j, (*chip, c), sib).wait_send()
        pltpu.make_async_copy(src, out.at[me], sems[2]).wait()


def _ride_gathers(body, n_in, n_out, rides, grid, forward_step):
    n = len(rides)
    sizes = list(grid)

    def wrapped(*refs):
        ins, xs = refs[:n_in], refs[n_in:n_in + n]
        outs, xd = refs[n_in + n:n_in + n + n_out], refs[n_in + n + n_out:n_in + 2 * n + n_out]
        scratch = refs[n_in + 2 * n + n_out:]
        own, sems = scratch[:len(scratch) - 3 * n], scratch[len(scratch) - 3 * n:]
        step = pl.program_id(0)
        for i in range(1, len(sizes)):
            step = step * sizes[i] + pl.program_id(i)

        def each(phase):
            for r in range(n):
                getattr(rides[r], phase)(xs[r], xd[r], sems[3 * r:3 * r + 3])

        pl.when(step == 0)(lambda: each("start"))
        body(*ins, *outs, *own)
        pl.when(step == forward_step)(lambda: each("forward"))
        pl.when(step == math.prod(sizes) - 1)(lambda: each("finish"))

    return wrapped


def _ride(body, n_in, n_out, xchg, grid):
    nx = xchg.n
    if nx == 0:
        return body

    def wrapped(*refs):
        ins, xs = refs[:n_in], refs[n_in:n_in + nx]
        outs, xd = refs[n_in + nx:n_in + nx + n_out], refs[n_in + nx + n_out:n_in + 2 * nx + n_out]
        scratch = refs[n_in + 2 * nx + n_out:]
        xchg.start_at_first_step(grid, xs, xd, scratch[-3:])
        body(*ins, *outs, *scratch[:-3])
        xchg.wait_at_last_step(grid, xs, xd, scratch[-3:])

    return wrapped


def _adamw_math(w, g, m, v):
    m2 = ADAM_B1 * m + (1.0 - ADAM_B1) * g
    v2 = ADAM_B2 * v + (1.0 - ADAM_B2) * (g * g)
    m_hat = m2 / (1.0 - ADAM_B1 ** ADAM_STEP)
    v_hat = v2 / (1.0 - ADAM_B2 ** ADAM_STEP)
    delta = -ADAM_LR * (m_hat / (jnp.sqrt(v_hat) + ADAM_EPS) + ADAM_WD * w)
    return delta, m2, v2


def _row_tile(rows):
    for t in (256, 128, 64, 32, 16, 8):
        if rows % t == 0:
            return t
    return rows


def _reduce_adamw(parts, w, m, v, name):
    _, r, cdim = parts.shape
    tr = _row_tile(r)

    def body(p_ref, w_ref, m_ref, v_ref, g_ref, d_ref, m2_ref, v2_ref):
        g = p_ref[0].astype(f32)
        for j in range(1, N_DEV):
            g = g + p_ref[j].astype(f32)
        delta, m2, v2 = _adamw_math(w_ref[...], g, m_ref[...], v_ref[...])
        g_ref[...] = g
        d_ref[...] = delta
        m2_ref[...] = m2
        v2_ref[...] = v2

    spec = pl.BlockSpec((tr, cdim), lambda i: (i, 0))
    return pl.pallas_call(
        body, name=name, grid=(r // tr,),
        in_specs=[pl.BlockSpec((N_DEV, tr, cdim), lambda i: (0, i, 0)), spec, spec, spec],
        out_specs=[spec] * 4,
        out_shape=[jax.ShapeDtypeStruct((r, cdim), f32)] * 4,
        compiler_params=_cparams(("parallel",)),
    )(parts, w, m, v)


def _adamw(w, g, m, v, name):
    r, cdim = w.shape
    tr = _row_tile(r)

    def body(w_ref, g_ref, m_ref, v_ref, d_ref, m2_ref, v2_ref):
        delta, m2, v2 = _adamw_math(w_ref[...], g_ref[...], m_ref[...], v_ref[...])
        d_ref[...] = delta
        m2_ref[...] = m2
        v2_ref[...] = v2

    spec = pl.BlockSpec((tr, cdim), lambda i: (i, 0))
    return pl.pallas_call(
        body, name=name, grid=(r // tr,),
        in_specs=[spec] * 4, out_specs=[spec] * 3,
        out_shape=[jax.ShapeDtypeStruct((r, cdim), f32)] * 3,
        compiler_params=_cparams(("parallel",)),
    )(w, g, m, v)


def _sum_devices(parts, name):
    _, r, cdim = parts.shape

    def body(p_ref, o_ref):
        g = p_ref[0]
        for j in range(1, N_DEV):
            g = g + p_ref[j]
        o_ref[...] = g

    return pl.pallas_call(
        body, name=name, out_shape=jax.ShapeDtypeStruct((r, cdim), f32),
        in_specs=[_VMEM], out_specs=_VMEM,
    )(parts)


def _ada_wgrad(c_all8, dmod_cols):
    nsh = dmod_cols.shape[1]

    def body(c_ref, d_ref, o_ref):
        cv = c_ref[...]
        o_ref[...] = lax.dot_general(cv * jax.nn.sigmoid(cv), d_ref[...], _TN, precision=_HI,
                                     preferred_element_type=f32)

    return pl.pallas_call(
        body, name="ada_wgrad", out_shape=jax.ShapeDtypeStruct((D_MODEL, nsh), f32),
        in_specs=[_VMEM, _VMEM], out_specs=_VMEM,
        compiler_params=pltpu.CompilerParams(vmem_limit_bytes=VMEM_LIMIT),
    )(c_all8, dmod_cols)


def _cols(t):
    return t.transpose(1, 0, 2).reshape(t.shape[1], N_DEV * t.shape[2])


def _col_blocks(t, n):
    return t.reshape(t.shape[0], N_DEV, n).transpose(1, 0, 2).astype(bf16)


def _row_blocks(t):
    return t.reshape(N_DEV, t.shape[0] // N_DEV, t.shape[1]).astype(bf16)


def _local_step(x, tgt, mod, norm_attn_g, w_in_sh, rel_bias, conv_full, a_log, dt_bias, delta_norm_g,
                norm_ffn_g, final_norm_g, w_out_sh, w_gate_sh, w_up_sh, w_down_sh):
    s = x.shape[0]
    sh1, sc1, g1, sh2, sc2, g2 = [mod[:, i * D_MODEL:(i + 1) * D_MODEL] for i in range(6)]
    nag = norm_attn_g.reshape(1, D_MODEL)
    nfg = norm_ffn_g.reshape(1, D_MODEL)
    fg = final_norm_g.reshape(1, D_MODEL)
    idx = _bucket_tables()
    bias = _bias_tables(rel_bias, idx)
    alog_e = jnp.repeat(a_log.reshape(N_HEADS), HEAD_DIM)[None]
    dt_e = jnp.repeat(dt_bias.reshape(N_HEADS), HEAD_DIM)[None]
    ng_e = jnp.tile(delta_norm_g.reshape(HEAD_DIM), N_HEADS)[None]

    h1, w_in_g = _ln_mod_fwd(x, nag, sc1, sh1, w_in_sh, "ln1_fwd")
    w_in_p = jnp.pad(_cols(w_in_g), ((0, 0), (0, IN_PAD - IN_WIDTH)))
    proj, (w_out_g,) = _mm(h1, w_in_p, "nn", f32, 512, IN_PAD, 1024, "in_proj",
                           xchg=_Exchange([w_out_sh], gather=True))
    (y_attn, lse), (w_gate_g, w_up_g, w_down_g) = _attn_fwd(proj, bias, [w_gate_sh, w_up_sh, w_down_sh])
    w_out_b = w_out_g.reshape(2 * GROUP_W, D_MODEL)
    w_down_b = w_down_g.reshape(D_FF, D_MODEL)
    w_gate_b, w_up_b = _cols(w_gate_g), _cols(w_up_g)
    n_ff = w_gate_sh.shape[1]
    sconv, yconv = _conv_silu_fwd(proj, conv_full)
    qn, kn, beta, g = _delta_prep_fwd(sconv, proj, alog_e, dt_e)
    (u, w, qt, kh, qk, tinv, gm), _ = _delta_chunk_pre(qn, kn, sconv, beta, g, _Exchange([], gather=False))
    o, ss = _delta_scan_fwd(u, w, qt, kh, qk, gm)
    y_delta = _gnorm_fwd(o, proj, ng_e)
    y, x1, h2 = _proj_resid_ln_mod_fwd([(y_attn, w_out_b[:GROUP_W]), (y_delta, w_out_b[GROUP_W:])],
                                       x, g1, nfg, sc2, sh2, "out_proj_ln2")
    act, gate, up = _ffn_up(h2, w_gate_b, w_up_b, "ffn_up")
    dx2, dy2, loss, dfg, dg2 = _proj_final_loss_bwd(act, w_down_b, x1, g2, fg, tgt, "ffn_down_loss")

    dgate, dup = _ffn_down_dx(dy2, w_down_b, gate, up, "ffn_down_dx")
    g_down = _mm(act, dy2, "tn", f32, 1408, 1024, 2048, "ffn_down_dw")
    (dx1, dsh2, dsc2, dnfg, dy, dg1), (r_down,) = _proj_ln_mod_bwd(
        [(dgate, w_gate_b), (dup, w_up_b)], x1, nfg, sc2, dx2, 256, "ffn_up_dx_ln2",
        _Exchange([_row_blocks(g_down)], gather=False), gate=g1, y=y)
    g_gate = _mm(h2, dgate, "tn", f32, 1024, 1408, 2048, "ffn_gate_dw")
    g_up = _mm(h2, dup, "tn", f32, 1024, 1408, 2048, "ffn_up_dw")
    dycat = _mm(dy, w_out_b, "nt", f32, 512, 1024, 1024, "out_proj_dx")
    g_out = jnp.concatenate([_mm(y_attn, dy, "tn", f32, GROUP_W, 1024, 2048, "out_proj_dw_attn"),
                             _mm(y_delta, dy, "tn", f32, GROUP_W, 1024, 2048, "out_proj_dw_delta")], axis=0)
    dq, dk, dv, dbias = _attn_bwd(proj, bias, y_attn, lse, dycat)
    g_rb = _bias_grad(dbias, idx)[:, :, 0].T
    do, dz, dng = _gnorm_bwd(o, proj, ng_e, dycat)
    dso, _ = _delta_scan_bwd(w, qt, kh, qk, gm, do, _Exchange([], gather=False))
    (dqn, dkn, dvd, dbeta, dgd), (r_up,) = _delta_chunk_bwd(
        qn, kn, sconv, beta, g, tinv, ss, dso, do, _Exchange([_col_blocks(g_up, n_ff)], gather=False))
    (dsq, dsk, dba, dal, ddt), _ = _delta_prep_bwd(
        sconv, proj, alog_e, dt_e, dqn, dkn, dbeta, dgd, _Exchange([], gather=False))
    (dxc, g_conv), (r_gate, r_out) = _conv_silu_bwd(
        proj, conv_full, yconv, (dsq, dsk, dvd),
        _Exchange([_col_blocks(g_gate, n_ff), _row_blocks(g_out)], gather=False))
    pieces = ((dq, 0), (dk, GROUP_W), (dv, 2 * GROUP_W), (dxc, DELTA_COL), (dz, Z_COL), (dba, BA_BLOCK * 128))
    g_in = jnp.concatenate(
        [_mm(h1, p, "tn", f32, 1024, min(p.shape[1], 768), 2048, "in_proj_dw_%d" % c) for p, c in pieces], axis=1)
    (gx, dsh1, dsc1, dnag), (r_in,) = _proj_ln_mod_bwd(
        [(p, w_in_p[:, c:c + p.shape[1]]) for p, c in pieces], x, nag, sc1, dx1, TOK_TILE, "in_proj_dx_ln1",
        _Exchange([_col_blocks(g_in[:, :IN_WIDTH], IN_WIDTH // N_DEV)], gather=False))
    grads = dict(
        x=gx, mod=jnp.concatenate([dsh1, dsc1, dg1, dsh2, dsc2, dg2], axis=1),
        norm_attn_g=dnag, norm_ffn_g=dnfg, final_norm_g=dfg, rel_bias=g_rb, conv_w=g_conv,
        a_log=dal.reshape(N_HEADS, HEAD_DIM).sum(-1), dt_bias=ddt.reshape(N_HEADS, HEAD_DIM).sum(-1),
        delta_norm_g=dng.reshape(N_HEADS, HEAD_DIM).sum(0),
        w_in=r_in, w_out=r_out, w_gate=r_gate, w_up=r_up, w_down=r_down)
    return loss[0, 0], grads


def _misc_row(rel_bias, a_log, dt_bias, delta_norm_g):
    flat = jnp.concatenate([rel_bias.reshape(-1), a_log.reshape(-1), dt_bias.reshape(-1), delta_norm_g.reshape(-1)])
    return jnp.pad(flat, (0, D_MODEL - flat.shape[0]))[None]


def _pack_small(b_ada, nag, nfg, fng, rel_bias, a_log, dt_bias, dng, conv_shard):
    rows = [b_ada.reshape(6, D_MODEL), nag.reshape(1, D_MODEL), nfg.reshape(1, D_MODEL), fng.reshape(1, D_MODEL),
            _misc_row(rel_bias, a_log, dt_bias, dng),
            jnp.pad(conv_shard.reshape(-1), (0, D_MODEL - conv_shard.size))[None],
            jnp.zeros((5, D_MODEL), f32)]
    return jnp.concatenate(rows, axis=0)


def _unpack_small(p, conv_shape):
    misc = p[9]
    return dict(
        b_ada=p[0:6].reshape(1, 6 * D_MODEL), norm_attn_g=p[6:7], norm_ffn_g=p[7:8], final_norm_g=p[8],
        rel_bias=misc[0:256].reshape(N_BUCKETS, N_HEADS), a_log=misc[256:264].reshape(1, N_HEADS),
        dt_bias=misc[264:272].reshape(1, N_HEADS), delta_norm_g=misc[272:336].reshape(1, HEAD_DIM),
        conv_w=p[10, :conv_shape[1] * conv_shape[2]].reshape(conv_shape))


def kernel(x, c, w_ada, b_ada, norm_attn_g, w_in, rel_bias, conv_w, a_log, dt_bias, delta_norm_g, w_out, norm_ffn_g, w_gate, w_up, w_down, final_norm_g, loss_target, m_w_ada, m_b_ada, m_norm_attn_g, m_w_in, m_rel_bias, m_conv_w, m_a_log, m_dt_bias, m_delta_norm_g, m_w_out, m_norm_ffn_g, m_w_gate, m_w_up, m_w_down, m_final_norm_g, v_w_ada, v_b_ada, v_norm_attn_g, v_w_in, v_rel_bias, v_conv_w, v_a_log, v_dt_bias, v_delta_norm_g, v_w_out, v_norm_ffn_g, v_w_gate, v_w_up, v_w_down, v_final_norm_g):
    me = 4 * lax.axis_index("x") + 2 * lax.axis_index("y") + lax.axis_index("c")
    ada_sh = w_ada.shape[2]
    conv_sh = conv_w.shape[2]

    cv = jnp.concatenate([c[0], conv_w[0].reshape(-1)])
    cv8 = jnp.zeros((8, 2 * D_MODEL), f32).at[0, :cv.shape[0]].set(cv)
    b8 = jnp.broadcast_to(b_ada.reshape(N_DEV, 1, ada_sh), (N_DEV, 8, ada_sh))
    call, modp = _ada_exchange(cv8, w_ada[0], b8)
    mod = modp[:, 0, :].reshape(1, 6 * D_MODEL)
    c_all = call[:, 0, :D_MODEL]
    conv_full = call[:, 0, D_MODEL:D_MODEL + CONV_WIDTH * conv_sh].reshape(N_DEV, CONV_WIDTH, conv_sh)
    conv_full = conv_full.transpose(1, 0, 2).reshape(CONV_WIDTH, N_DEV * conv_sh)

    loss_local, gr = _local_step(x[0], loss_target[0], mod, norm_attn_g, w_in[0].astype(bf16), rel_bias, conv_full, a_log,
                                 dt_bias, delta_norm_g, norm_ffn_g, final_norm_g, w_out[0].astype(bf16),
                                 w_gate[0].astype(bf16), w_up[0].astype(bf16), w_down[0].astype(bf16))
    loss = lax.psum(loss_local, ("x", "y", "c"))

    small = jnp.concatenate([
        gr["mod"].reshape(6, D_MODEL), gr["norm_attn_g"], gr["norm_ffn_g"], gr["final_norm_g"],
        gr["conv_w"].reshape(6, D_MODEL),
        _misc_row(gr["rel_bias"], gr["a_log"], gr["dt_bias"], gr["delta_norm_g"])], axis=0)
    parts = _all_to_all([jnp.broadcast_to(small[None], (N_DEV,) + small.shape)], "small_gather")[0]
    tot = _sum_devices(parts, "small_sum")
    g_conv_full = tot[9:15].reshape(CONV_WIDTH, N_DEV * conv_sh)
    g_conv = lax.dynamic_slice(g_conv_full, (0, me * conv_sh), (CONV_WIDTH, conv_sh))
    misc = tot[15]
    g_small = _pack_small(tot[0:6], tot[6], tot[7], tot[8], misc[0:256], misc[256:264], misc[264:272],
                          misc[272:336], g_conv)
    pk = lambda pre: _pack_small(pre[0], pre[1], pre[2], pre[3], pre[4], pre[5], pre[6], pre[7], pre[8])
    w_small = pk((b_ada, norm_attn_g, norm_ffn_g, final_norm_g, rel_bias, a_log, dt_bias, delta_norm_g, conv_w))
    m_small = pk((m_b_ada, m_norm_attn_g, m_norm_ffn_g, m_final_norm_g, m_rel_bias, m_a_log, m_dt_bias,
                  m_delta_norm_g, m_conv_w))
    v_small = pk((v_b_ada, v_norm_attn_g, v_norm_ffn_g, v_final_norm_g, v_rel_bias, v_a_log, v_dt_bias,
                  v_delta_norm_g, v_conv_w))
    d_small, m2_small, v2_small = _adamw(w_small, g_small, m_small, v_small, "adamw_small")
    cshape = conv_w.shape
    G, Dl, M2, V2 = (_unpack_small(t, cshape) for t in (g_small, d_small, m2_small, v2_small))

    dmod_all = parts[:, 0:6, :].reshape(N_DEV, 6 * D_MODEL)
    dmod_cols = lax.dynamic_slice(dmod_all, (0, me * ada_sh), (N_DEV, ada_sh))
    g_ada = _ada_wgrad(c_all, dmod_cols)
    d_ada, m2_ada, v2_ada = _adamw(w_ada[0], g_ada, m_w_ada[0], v_w_ada[0], "adamw_w_ada")

    big = {}
    for name, w_, m_, v_ in (("w_in", w_in, m_w_in, v_w_in), ("w_out", w_out, m_w_out, v_w_out),
                             ("w_gate", w_gate, m_w_gate, v_w_gate), ("w_up", w_up, m_w_up, v_w_up),
                             ("w_down", w_down, m_w_down, v_w_down)):
        big[name] = [t[None] for t in _reduce_adamw(gr[name], w_[0], m_[0], v_[0], "reduce_adamw_" + name)]

    def leaf(i, name):
        if name == "w_ada":
            return (g_ada, d_ada, m2_ada, v2_ada)[i][None]
        if name in big:
            return big[name][i]
        return (G, Dl, M2, V2)[i][name]

    order = ["w_ada", "b_ada", "norm_attn_g", "w_in", "rel_bias", "conv_w", "a_log", "dt_bias", "delta_norm_g",
             "w_out", "norm_ffn_g", "w_gate", "w_up", "w_down", "final_norm_g"]
    outs = [loss, gr["x"][None]]
    for i in range(4):
        outs += [leaf(i, n) for n in order]
    return tuple(outs)
```

```python
import functools
import math

import jax
import jax.numpy as jnp
from jax import lax
from jax.experimental import pallas as pl
from jax.experimental.pallas import tpu as pltpu

f32 = jnp.float32
bf16 = jnp.bfloat16

D_MODEL = 1024
HEAD_DIM = 64
N_HEADS = 8
GROUP_W = 512
IN_WIDTH = 3600
IN_PAD = 3840
D_FF = 2816
EPS = 1e-6
NEG_INF = -1e30
BAND = 128
PAD_UNIT = 2048
DILATIONS = (1, 4, 16)
N_BUCKETS = 32
MAX_DISTANCE = 2048
CONV_WIDTH = 4
CHUNK = 64
N_DEV = 8
VMEM_LIMIT = 56 * 1024 * 1024

ADAM_LR, ADAM_B1, ADAM_B2, ADAM_EPS, ADAM_WD, ADAM_STEP = 0.001, 0.9, 0.999, 1e-08, 0.01, 10


def _cparams(sem):
    return pltpu.CompilerParams(dimension_semantics=sem, vmem_limit_bytes=VMEM_LIMIT)


def _mm(a, b, mode, out_dtype, tm, tn, tk, name, xchg=None):
    if mode == "nn":
        (m, k), (_, n) = a.shape, b.shape
        a_spec = pl.BlockSpec((tm, tk), lambda j, i, kk: (i, kk))
        b_spec = pl.BlockSpec((tk, tn), lambda j, i, kk: (kk, j))
        dims = (((1,), (0,)), ((), ()))
    elif mode == "nt":
        (m, k), (n, _) = a.shape, b.shape
        a_spec = pl.BlockSpec((tm, tk), lambda j, i, kk: (i, kk))
        b_spec = pl.BlockSpec((tn, tk), lambda j, i, kk: (j, kk))
        dims = (((1,), (1,)), ((), ()))
    else:
        (k, m), (_, n) = a.shape, b.shape
        a_spec = pl.BlockSpec((tk, tm), lambda j, i, kk: (kk, i))
        b_spec = pl.BlockSpec((tk, tn), lambda j, i, kk: (kk, j))
        dims = (((0,), (0,)), ((), ()))
    assert m % tm == 0 and n % tn == 0 and k % tk == 0, (name, m, n, k, tm, tn, tk)
    nk = k // tk
    grid = (n // tn, m // tm, nk)
    nx = xchg.n if xchg is not None else 0

    def body(*refs):
        a_ref, b_ref = refs[:2]
        o_ref = refs[2 + nx]
        scratch = refs[3 + 2 * nx:]
        if nx:
            xrefs = (refs[2:2 + nx], refs[3 + nx:3 + 2 * nx], scratch[-3:])
            xchg.start_at_first_step(grid, *xrefs)
        if nk == 1:
            o_ref[...] = lax.dot_general(a_ref[...].astype(bf16), b_ref[...].astype(bf16), dims,
                                         preferred_element_type=f32).astype(o_ref.dtype)
        else:
            acc_ref = scratch[0]
            kk = pl.program_id(2)

            @pl.when(kk == 0)
            def _():
                acc_ref[...] = jnp.zeros_like(acc_ref)

            acc_ref[...] += lax.dot_general(a_ref[...].astype(bf16), b_ref[...].astype(bf16), dims,
                                            preferred_element_type=f32)

            @pl.when(kk == nk - 1)
            def _():
                o_ref[...] = acc_ref[...].astype(o_ref.dtype)
        if nx:
            xchg.wait_at_last_step(grid, *xrefs)

    out = pl.pallas_call(
        body, name=name, grid=grid,
        in_specs=[a_spec, b_spec] + ([_ANY] * nx),
        out_specs=[pl.BlockSpec((tm, tn), lambda j, i, kk: (i, j))] + ([_ANY] * nx),
        out_shape=[jax.ShapeDtypeStruct((m, n), out_dtype)] + (xchg.out_shape() if nx else []),
        scratch_shapes=([pltpu.VMEM((tm, tn), f32)] if nk > 1 else []) + (xchg.scratch() if nx else []),
        compiler_params=_cparams(("arbitrary",) * 3 if nx else ("parallel", "parallel", "arbitrary")),
    )(a, b, *(xchg.arrs if nx else []))
    return (out[0], out[1:]) if nx else out[0]


TOK_TILE = 512
SUB_COLS = 384


def _row_spec(width, tile=TOK_TILE):
    return pl.BlockSpec((tile, width), lambda i: (i, 0))


def _vec_spec(width, rows=1):
    return pl.BlockSpec((rows, width), lambda i: (0, 0))


def _ln_mod_fwd(x, gain, sc, sh, shard, name):
    s, d = x.shape
    nt = s // TOK_TILE
    ride = _ChipGather(shard)

    def body(x_ref, g_ref, sc_ref, sh_ref, sh_in, h_ref, sh_out, *sems):
        i = pl.program_id(0)
        pl.when(i == 0)(lambda: ride.start(sh_in, sh_out, sems))
        xv = x_ref[...]
        rstd = lax.rsqrt(jnp.mean(xv * xv, axis=-1, keepdims=True) + EPS)
        h = (xv * rstd) * g_ref[...] * (1.0 + sc_ref[...]) + sh_ref[...]
        h_ref[...] = h.astype(bf16)
        @pl.when(i == nt - 1)
        def _():
            ride.forward(sh_in, sh_out, sems)
            ride.finish(sh_in, sh_out, sems)

    return pl.pallas_call(
        body, name=name, grid=(nt,),
        in_specs=[_row_spec(d), _vec_spec(d), _vec_spec(d), _vec_spec(d), _ANY],
        out_specs=[_row_spec(d), _ANY],
        out_shape=[jax.ShapeDtypeStruct((s, d), bf16), ride.out_shape()],
        scratch_shapes=ride.scratch(),
        compiler_params=_cparams(("arbitrary",)),
    )(x, gain, sc, sh, shard)


def _proj_resid_ln_mod_fwd(pairs, x, gate, gain, sc, sh, name):
    s, d = x.shape
    npair = len(pairs)

    def body(*refs):
        aw = refs[:2 * npair]
        x_ref, gt_ref, g_ref, sc_ref, sh_ref, y_ref, x1_ref, h_ref = refs[2 * npair:]
        halves = [slice(r * TOK_TILE, (r + 1) * TOK_TILE) for r in range(2)]
        ys = []
        for rows in halves:
            y = jnp.dot(aw[0][rows, :].astype(bf16), aw[1][...], preferred_element_type=f32)
            for t in range(1, npair):
                y = y + jnp.dot(aw[2 * t][rows, :].astype(bf16), aw[2 * t + 1][...], preferred_element_type=f32)
            ys.append(y)
        for rows, y in zip(halves, ys):
            y_ref[rows, :] = y
            x1 = x_ref[rows, :] + gt_ref[...] * y
            x1_ref[rows, :] = x1
            rstd = lax.rsqrt(jnp.mean(x1 * x1, axis=-1, keepdims=True) + EPS)
            h = (x1 * rstd) * g_ref[...] * (1.0 + sc_ref[...]) + sh_ref[...]
            h_ref[rows, :] = h.astype(bf16)

    tile = 2 * TOK_TILE
    aw_specs, aw = [], []
    for a, w in pairs:
        aw_specs += [_row_spec(a.shape[1], tile), pl.BlockSpec(w.shape, lambda i: (0, 0))]
        aw += [a, w]
    return pl.pallas_call(
        body, name=name, grid=(s // tile,),
        in_specs=aw_specs + [_row_spec(d, tile)] + [_vec_spec(d)] * 4,
        out_specs=[_row_spec(d, tile)] * 3,
        out_shape=[jax.ShapeDtypeStruct((s, d), f32)] * 2 + [jax.ShapeDtypeStruct((s, d), bf16)],
        compiler_params=_cparams(("parallel",)),
    )(*aw, x, gate, gain, sc, sh)


FFN_TN = 1408


def _ffn_up(h2, w_gate, w_up, name):
    s, d = h2.shape
    tm = 2 * TOK_TILE

    def body(h_ref, wg_ref, wu_ref, a_ref, g_ref, u_ref):
        h = h_ref[...]
        g = jnp.dot(h, wg_ref[...], preferred_element_type=f32)
        u = jnp.dot(h, wu_ref[...], preferred_element_type=f32)
        a_ref[...] = (g * jax.nn.sigmoid(g) * u).astype(bf16)
        g_ref[...] = g.astype(bf16)
        u_ref[...] = u.astype(bf16)

    w_spec = pl.BlockSpec((d, FFN_TN), lambda j, i: (0, j))
    o_spec = pl.BlockSpec((tm, FFN_TN), lambda j, i: (i, j))
    return pl.pallas_call(
        body, name=name, grid=(D_FF // FFN_TN, s // tm),
        in_specs=[pl.BlockSpec((tm, d), lambda j, i: (i, 0)), w_spec, w_spec],
        out_specs=[o_spec] * 3,
        out_shape=[jax.ShapeDtypeStruct((s, D_FF), bf16)] * 3,
        compiler_params=_cparams(("parallel", "parallel")),
    )(h2, w_gate, w_up)


def _ffn_down_dx(dy2, w_down, gate, up, name):
    s, d = dy2.shape
    tm = 2 * TOK_TILE

    def body(dy_ref, w_ref, g_ref, u_ref, dg_ref, du_ref):
        dy = dy_ref[...]
        for c0 in range(0, FFN_TN, SUB_COLS):
            cols = slice(c0, min(c0 + SUB_COLS, FFN_TN))
            da = lax.dot_general(dy, w_ref[cols, :], _NT, preferred_element_type=f32)
            g = g_ref[:, cols].astype(f32)
            sg = jax.nn.sigmoid(g)
            du_ref[:, cols] = (da * g * sg).astype(bf16)
            dg_ref[:, cols] = (da * u_ref[:, cols].astype(f32) * sg * (1.0 + g * (1.0 - sg))).astype(bf16)

    t_spec = pl.BlockSpec((tm, FFN_TN), lambda j, i: (i, j))
    return pl.pallas_call(
        body, name=name, grid=(D_FF // FFN_TN, s // tm),
        in_specs=[pl.BlockSpec((tm, d), lambda j, i: (i, 0)), pl.BlockSpec((FFN_TN, d), lambda j, i: (j, 0)),
                  t_spec, t_spec],
        out_specs=[t_spec, t_spec],
        out_shape=[jax.ShapeDtypeStruct((s, D_FF), bf16)] * 2,
        compiler_params=_cparams(("parallel", "parallel")),
    )(dy2, w_down, gate, up)


def _acc_spec(width):
    return pl.BlockSpec((1, width), lambda i: (0, 0))


def _proj_final_loss_bwd(a, w, x1, gate2, final_g, target, name):
    s, d = x1.shape
    k = a.shape[1]

    def body(a_ref, w_ref, x1_ref, gt_ref, fg_ref, tg_ref, dx2_ref, dy2_ref, loss_ref, dfg_ref, dgt_ref):
        @pl.when(pl.program_id(0) == 0)
        def _():
            loss_ref[...] = jnp.zeros_like(loss_ref)
            dfg_ref[...] = jnp.zeros_like(dfg_ref)
            dgt_ref[...] = jnp.zeros_like(dgt_ref)

        y2 = jnp.dot(a_ref[...], w_ref[...], preferred_element_type=f32)
        gt = gt_ref[...]
        fg = fg_ref[...]
        x2 = x1_ref[...] + gt * y2
        rstd = lax.rsqrt(jnp.mean(x2 * x2, axis=-1, keepdims=True) + EPS)
        xn = x2 * rstd
        err = xn * fg - tg_ref[...]
        row = jnp.sum(err * err, axis=-1, keepdims=True) * (0.5 / d)
        loss_ref[...] += jnp.sum(row, axis=0, keepdims=True) + jnp.zeros_like(loss_ref)
        dout = err * (1.0 / d)
        dfg_ref[...] += jnp.sum(dout * xn, axis=0, keepdims=True)
        dxn = dout * fg
        dx2 = rstd * (dxn - xn * jnp.mean(dxn * xn, axis=-1, keepdims=True))
        dx2_ref[...] = dx2
        dgt_ref[...] += jnp.sum(dx2 * y2, axis=0, keepdims=True)
        dy2_ref[...] = (gt * dx2).astype(bf16)

    return pl.pallas_call(
        body, name=name, grid=(s // TOK_TILE,),
        in_specs=[_row_spec(k), pl.BlockSpec((k, d), lambda i: (0, 0)), _row_spec(d), _vec_spec(d), _vec_spec(d),
                  _row_spec(d)],
        out_specs=[_row_spec(d), _row_spec(d), _acc_spec(128), _acc_spec(d), _acc_spec(d)],
        out_shape=[jax.ShapeDtypeStruct((s, d), f32), jax.ShapeDtypeStruct((s, d), bf16),
                   jax.ShapeDtypeStruct((1, 128), f32), jax.ShapeDtypeStruct((1, d), f32),
                   jax.ShapeDtypeStruct((1, d), f32)],
        compiler_params=_cparams(("arbitrary",)),
    )(a, w, x1, gate2, final_g, target)


def _proj_ln_mod_bwd(pairs, xin, gain, sc, dres, tm, name, xchg, gate=None, y=None):
    s, d = xin.shape
    with_gate = gate is not None
    npair = len(pairs)
    n_in = 2 * npair + (7 if with_gate else 5) - 1
    n_out = 6 if with_gate else 4

    def body(*refs):
        ab = refs[:2 * npair]
        if with_gate:
            (x_ref, g_ref, sc_ref, dr_ref, gt_ref, y_ref,
             dx_ref, dsh_ref, dsc_ref, dg_ref, dy_ref, dgt_ref) = refs[2 * npair:]
        else:
            x_ref, g_ref, sc_ref, dr_ref, dx_ref, dsh_ref, dsc_ref, dg_ref = refs[2 * npair:]

        @pl.when(pl.program_id(0) == 0)
        def _():
            dsh_ref[...] = jnp.zeros_like(dsh_ref)
            dsc_ref[...] = jnp.zeros_like(dsc_ref)
            dg_ref[...] = jnp.zeros_like(dg_ref)
            if with_gate:
                dgt_ref[...] = jnp.zeros_like(dgt_ref)

        dh = lax.dot_general(ab[0][...].astype(bf16), ab[1][...], _NT, preferred_element_type=f32)
        for t in range(1, npair):
            dh = dh + lax.dot_general(ab[2 * t][...].astype(bf16), ab[2 * t + 1][...], _NT,
                                      preferred_element_type=f32)
        xv = x_ref[...]
        g = g_ref[...]
        sc1 = 1.0 + sc_ref[...]
        rstd = lax.rsqrt(jnp.mean(xv * xv, axis=-1, keepdims=True) + EPS)
        xn = xv * rstd
        dsh_ref[...] += jnp.sum(dh, axis=0, keepdims=True)
        dsc_ref[...] += jnp.sum(dh * (xn * g), axis=0, keepdims=True)
        dg_ref[...] += jnp.sum(dh * sc1 * xn, axis=0, keepdims=True)
        dxn = dh * sc1 * g
        dx = dr_ref[...] + rstd * (dxn - xn * jnp.mean(dxn * xn, axis=-1, keepdims=True))
        dx_ref[...] = dx
        if with_gate:
            dgt_ref[...] += jnp.sum(dx * y_ref[...], axis=0, keepdims=True)
            dy_ref[...] = (gt_ref[...] * dx).astype(bf16)

    row = lambda width: pl.BlockSpec((tm, width), lambda i: (i, 0))
    in_specs, args = [], []
    for a, b in pairs:
        in_specs += [row(a.shape[1]), pl.BlockSpec(b.shape, lambda i: (0, 0))]
        args += [a, b]
    in_specs += [row(d), _vec_spec(d), _vec_spec(d), row(d)]
    args += [xin, gain, sc, dres]
    out_specs = [row(d), _acc_spec(d), _acc_spec(d), _acc_spec(d)]
    out_shape = [jax.ShapeDtypeStruct((s, d), f32)] + [jax.ShapeDtypeStruct((1, d), f32)] * 3
    if with_gate:
        in_specs += [_vec_spec(d), row(d)]
        out_specs += [row(d), _acc_spec(d)]
        out_shape += [jax.ShapeDtypeStruct((s, d), bf16), jax.ShapeDtypeStruct((1, d), f32)]
        args += [gate, y]
    grid = (s // tm,)
    out = pl.pallas_call(
        _ride(body, n_in, n_out, xchg, grid), name=name, grid=grid,
        in_specs=in_specs + [_ANY] * xchg.n, out_specs=out_specs + [_ANY] * xchg.n,
        out_shape=out_shape + xchg.out_shape(), scratch_shapes=xchg.scratch(),
        compiler_params=_cparams(("arbitrary",)),
    )(*args, *xchg.arrs)
    return out[:n_out], out[n_out:]


def _bucket_tables():
    import numpy as np
    qi = np.arange(BAND)[:, None]
    kj = np.arange(2 * BAND)[None, :]
    steps = qi + BAND - kj
    max_exact = N_BUCKETS // 2
    out = []
    for d in DILATIONS:
        dist = np.maximum(steps, 0) * d
        dist_f = np.maximum(dist, 1).astype(np.float32)
        large = max_exact + (np.log(dist_f / np.float32(max_exact)) / np.float32(math.log(MAX_DISTANCE / max_exact))
                             * np.float32(N_BUCKETS - max_exact)).astype(np.int32)
        out.append(np.where(dist < max_exact, dist, np.minimum(large, N_BUCKETS - 1)))
    return jnp.asarray(np.stack(out).astype(np.int32))


def _bias_tables(rel_bias, idx):
    def body(idx_ref, rb_ref, o_ref):
        h = pl.program_id(1)
        idxv = idx_ref[0]
        acc = jnp.zeros((BAND, 2 * BAND), f32)
        for b in range(N_BUCKETS):
            acc = jnp.where(idxv == b, rb_ref[b, h], acc)
        o_ref[0, 0] = jnp.where(_attn_masks()[1], acc, NEG_INF)

    return pl.pallas_call(
        body, name="attn_bias_tables", grid=(3, N_HEADS),
        in_specs=[pl.BlockSpec((1, BAND, 2 * BAND), lambda br, h: (br, 0, 0)),
                  pl.BlockSpec(memory_space=pltpu.SMEM)],
        out_specs=pl.BlockSpec((1, 1, BAND, 2 * BAND), lambda br, h: (br, h, 0, 0)),
        out_shape=jax.ShapeDtypeStruct((3, N_HEADS, BAND, 2 * BAND), f32),
        compiler_params=_cparams(("parallel", "parallel")),
    )(idx, rel_bias)


def _bias_grad(dbias, idx):
    def body(idx_ref, db_ref, o_ref):
        br = pl.program_id(1)

        @pl.when(br == 0)
        def _():
            o_ref[...] = jnp.zeros_like(o_ref)

        idxv = idx_ref[0]
        dbv = db_ref[0, 0]
        row = lax.broadcasted_iota(jnp.int32, (N_BUCKETS, 128), 0)
        acc = jnp.zeros((N_BUCKETS, 128), f32)
        for b in range(N_BUCKETS):
            sb = jnp.sum(jnp.sum(jnp.where(idxv == b, dbv, 0.0), axis=1, keepdims=True), axis=0, keepdims=True)
            acc = acc + jnp.where(row == b, sb, 0.0)
        o_ref[0] += acc

    return pl.pallas_call(
        body, name="attn_bias_grad", grid=(N_HEADS, 3),
        in_specs=[pl.BlockSpec((1, BAND, 2 * BAND), lambda h, br: (br, 0, 0)),
                  pl.BlockSpec((1, 1, BAND, 2 * BAND), lambda h, br: (br, h, 0, 0))],
        out_specs=pl.BlockSpec((1, N_BUCKETS, 128), lambda h, br: (h, 0, 0)),
        out_shape=jax.ShapeDtypeStruct((N_HEADS, N_BUCKETS, 128), f32),
        compiler_params=_cparams(("parallel", "arbitrary")),
    )(idx, dbias)


def _attn_masks():
    lane = lax.broadcasted_iota(jnp.int32, (BAND, 128), 1)
    m0 = lane < HEAD_DIM
    qi = lax.broadcasted_iota(jnp.int32, (BAND, 2 * BAND), 0)
    kj = lax.broadcasted_iota(jnp.int32, (BAND, 2 * BAND), 1)
    steps = qi + BAND - kj
    in_window = (steps >= 0) & (steps <= BAND)
    return m0, in_window, kj >= BAND


_NT = (((1,), (1,)), ((), ()))
_TN = (((0,), (0,)), ((), ()))
_BNN = (((2,), (1,)), ((0,), (0,)))
_BNT = (((2,), (2,)), ((0,), (0,)))
_BTN = (((1,), (1,)), ((0,), (0,)))
ATTN_GROUP = 4
ATTN_ITEMS = PAD_UNIT // BAND
Q_COL, K_COL, V_COL = 0, 4, 8


def _attn_item_rows(j, d, c, cbase):
    r = lax.rem(j, d)
    b = lax.div(j, d)
    loc = b * (d * BAND) + r
    first = jnp.logical_and(c == 0, b == 0)
    start = cbase + loc
    pstart = jnp.where(first, start, start - d * BAND)
    return loc, start, pstart, first


def _attn_fwd(proj, bias, shards):
    s = proj.shape[0]
    rides = [_ChipGather(sh) for sh in shards]

    def body(q_ref, k_ref, v_ref, b_ref, y_ref, lse_ref, o_s, l_s):
        c = pl.program_id(1)
        cbase = pl.multiple_of(c * PAD_UNIT, PAD_UNIT)
        m0, in_window, cur_half = _attn_masks()
        for bi, d in enumerate(DILATIONS):
            def group(jg, carry, bi=bi, d=d):
                locs, qs, ks, vs, pens = [], [], [], [], []
                for t in range(ATTN_GROUP):
                    loc, start, pstart, first = _attn_item_rows(jg * ATTN_GROUP + t, d, c, cbase)
                    locs.append(loc)
                    qs.append(q_ref[pl.ds(loc, BAND, stride=d), :])
                    ks.append(jnp.concatenate([k_ref[pl.ds(pstart, BAND, stride=d), :],
                                               k_ref[pl.ds(start, BAND, stride=d), :]], axis=0))
                    vs.append(jnp.concatenate([v_ref[pl.ds(pstart, BAND, stride=d), :],
                                               v_ref[pl.ds(start, BAND, stride=d), :]], axis=0))
                    pens.append(jnp.where(cur_half, 0.0, jnp.where(first, NEG_INF, 0.0)))
                q = jnp.stack(qs)
                kk = jnp.stack(ks + ks).astype(bf16)
                vv = jnp.stack(vs + vs).astype(bf16)
                pen = jnp.stack(pens + pens)
                qh = (jnp.concatenate([jnp.where(m0, q, 0.0), jnp.where(m0, 0.0, q)], axis=0) * 0.125).astype(bf16)
                sc = lax.dot_general(qh, kk, _BNT, preferred_element_type=f32)
                sc = (sc.reshape(2, ATTN_GROUP, BAND, 2 * BAND) + b_ref[bi][:, None]).reshape(sc.shape) + pen
                mx = jnp.max(sc, axis=-1, keepdims=True)
                e = jnp.exp(sc - mx)
                l = jnp.sum(e, axis=-1, keepdims=True)
                o = lax.dot_general(e.astype(bf16), vv, _BNN, preferred_element_type=f32) * (1.0 / l)
                ls = mx + jnp.log(l)
                for t in range(ATTN_GROUP):
                    rows = pl.ds(locs[t], BAND, stride=d)
                    o_s[bi, rows, :] = jnp.where(m0, o[t], o[ATTN_GROUP + t])
                    l_s[bi, rows, :] = jnp.where(m0, ls[t], ls[ATTN_GROUP + t])
                return carry

            lax.fori_loop(0, ATTN_ITEMS // ATTN_GROUP, group, 0)

        def merge(t, carry):
            rows = pl.ds(pl.multiple_of(t * 256, 256), 256)
            ls = [l_s[i, rows, :] for i in range(3)]
            mx = jnp.maximum(jnp.maximum(ls[0], ls[1]), ls[2])
            ws = [jnp.exp(l - mx) for l in ls]
            tot = ws[0] + ws[1] + ws[2]
            y = (ws[0] * o_s[0, rows, :] + ws[1] * o_s[1, rows, :] + ws[2] * o_s[2, rows, :]) / tot
            y_ref[rows, :] = y
            lse_ref[rows, :] = mx + jnp.log(tot)
            return carry

        lax.fori_loop(0, PAD_UNIT // 256, merge, 0)

    chunk = lambda col: pl.BlockSpec((PAD_UNIT, 128), lambda p, c: (c, col + p))
    full = lambda col: pl.BlockSpec((s, 128), lambda p, c: (0, col + p))
    grid = (N_HEADS // 2, s // PAD_UNIT)
    nsteps = grid[0] * grid[1]
    out = pl.pallas_call(
        _ride_gathers(body, 4, 2, rides, grid, (3 * nsteps) // 4), name="attn_fwd", grid=grid,
        in_specs=[chunk(Q_COL), full(K_COL), full(V_COL),
                  pl.BlockSpec((3, 2, BAND, 2 * BAND), lambda p, c: (0, p, 0, 0))] + [_ANY] * len(rides),
        out_specs=[chunk(0), chunk(0)] + [_ANY] * len(rides),
        out_shape=[jax.ShapeDtypeStruct((s, GROUP_W), f32)] * 2 + [r.out_shape() for r in rides],
        scratch_shapes=[pltpu.VMEM((3, PAD_UNIT, 128), f32)] * 2 + [t for r in rides for t in r.scratch()],
        compiler_params=_cparams(("arbitrary", "arbitrary")),
    )(proj, proj, proj, bias, *shards)
    return out[:2], out[2:]


def _attn_bwd(proj, bias, y, lse, dycat):
    s = proj.shape[0]

    def body(q_ref, k_ref, v_ref, b_ref, y_ref, lse_ref, dy_ref, dq_ref, dk_ref, dv_ref, db_ref, dd_s):
        c = pl.program_id(1)
        cbase = pl.multiple_of(c * PAD_UNIT, PAD_UNIT)
        m0, in_window, cur_half = _attn_masks()

        @pl.when(c == 0)
        def _():
            dk_ref[...] = jnp.zeros_like(dk_ref)
            dv_ref[...] = jnp.zeros_like(dv_ref)
            db_ref[...] = jnp.zeros_like(db_ref)

        dq_ref[...] = jnp.zeros_like(dq_ref)

        def rowdot(t, carry):
            rows = pl.ds(pl.multiple_of(t * 256, 256), 256)
            prod = dy_ref[rows, :] * y_ref[rows, :]
            lane = lax.broadcasted_iota(jnp.int32, prod.shape, 1)
            h0 = lane < HEAD_DIM
            d0 = jnp.sum(jnp.where(h0, prod, 0.0), axis=-1, keepdims=True)
            d1 = jnp.sum(jnp.where(h0, 0.0, prod), axis=-1, keepdims=True)
            dd_s[rows, :] = jnp.where(h0, d0, d1)
            return carry

        lax.fori_loop(0, PAD_UNIT // 256, rowdot, 0)

        for bi, d in enumerate(DILATIONS):
            def group(jg, carry, bi=bi, d=d):
                ng = ATTN_GROUP
                meta, qs, dos, lqs, dds, ks, vs, pens = [], [], [], [], [], [], [], []
                for t in range(ng):
                    loc, start, pstart, first = _attn_item_rows(jg * ng + t, d, c, cbase)
                    qrows = pl.ds(loc, BAND, stride=d)
                    rows = pl.ds(start, BAND, stride=d)
                    prows = pl.ds(pstart, BAND, stride=d)
                    meta.append((qrows, rows, prows))
                    qs.append(q_ref[qrows, :])
                    dos.append(dy_ref[qrows, :])
                    lqs.append(lse_ref[qrows, :])
                    dds.append(dd_s[qrows, :])
                    ks.append(jnp.concatenate([k_ref[prows, :], k_ref[rows, :]], axis=0))
                    vs.append(jnp.concatenate([v_ref[prows, :], v_ref[rows, :]], axis=0))
                    pens.append(jnp.where(cur_half, 0.0, jnp.where(first, NEG_INF, 0.0)))

                def heads(t):
                    return jnp.concatenate([jnp.where(m0, t, 0.0), jnp.where(m0, 0.0, t)], axis=0)

                def head_col(t):
                    return jnp.concatenate([t[:, :, 0:1], t[:, :, HEAD_DIM:HEAD_DIM + 1]], axis=0)

                qh = (heads(jnp.stack(qs)) * 0.125).astype(bf16)
                doh = heads(jnp.stack(dos)).astype(bf16)
                kk = jnp.stack(ks + ks).astype(bf16)
                vv = jnp.stack(vs + vs).astype(bf16)
                sc = lax.dot_general(qh, kk, _BNT, preferred_element_type=f32)
                sc = (sc.reshape(2, ng, BAND, 2 * BAND) + b_ref[bi][:, None]).reshape(sc.shape) + jnp.stack(pens + pens)
                p = jnp.exp(sc - head_col(jnp.stack(lqs)))
                dp = lax.dot_general(doh, vv, _BNT, preferred_element_type=f32)
                ds = p * (dp - head_col(jnp.stack(dds)))
                db_ref[bi] += jnp.sum(ds.reshape(2, ng, BAND, 2 * BAND), axis=1)
                dsb = ds.astype(bf16)
                dq = lax.dot_general(dsb, kk, _BNN, preferred_element_type=f32) * 0.125
                dk = lax.dot_general(dsb, qh, _BTN, preferred_element_type=f32)
                dv = lax.dot_general(p.astype(bf16), doh, _BTN, preferred_element_type=f32)
                for t in range(ng):
                    qrows, rows, prows = meta[t]
                    dq_ref[qrows, :] += jnp.where(m0, dq[t], dq[ng + t])
                    dkt = dk[t] + dk[ng + t]
                    dvt = dv[t] + dv[ng + t]
                    dk_ref[prows, :] += dkt[:BAND]
                    dk_ref[rows, :] += dkt[BAND:]
                    dv_ref[prows, :] += dvt[:BAND]
                    dv_ref[rows, :] += dvt[BAND:]
                return carry

            lax.fori_loop(0, ATTN_ITEMS // ATTN_GROUP, group, 0)

    chunk = lambda col: pl.BlockSpec((PAD_UNIT, 128), lambda p, c: (c, col + p))
    full = lambda col: pl.BlockSpec((s, 128), lambda p, c: (0, col + p))
    bias_spec = pl.BlockSpec((3, 2, BAND, 2 * BAND), lambda p, c: (0, p, 0, 0))
    return pl.pallas_call(
        body, name="attn_bwd", grid=(N_HEADS // 2, s // PAD_UNIT),
        in_specs=[chunk(Q_COL), full(K_COL), full(V_COL), bias_spec, chunk(0), chunk(0), chunk(0)],
        out_specs=[chunk(0), full(0), full(0), bias_spec],
        out_shape=[jax.ShapeDtypeStruct((s, GROUP_W), f32)] * 3
        + [jax.ShapeDtypeStruct((3, N_HEADS, BAND, 2 * BAND), f32)],
        scratch_shapes=[pltpu.VMEM((PAD_UNIT, 128), f32)],
        compiler_params=_cparams(("parallel", "arbitrary")),
    )(proj, proj, proj, bias, y, lse, dycat)


_HI = lax.Precision.HIGHEST
DELTA_COL = 1536
Z_COL = 3072
BA_BLOCK = 28
DELTA_ROWS = 2048


def _hdot(a, b):
    return jnp.dot(a, b, precision=_HI, preferred_element_type=f32)


_DIMS = dict(nn=(((2,), (1,)), ((0,), (0,))), nt=(((2,), (2,)), ((0,), (0,))), tn=(((1,), (1,)), ((0,), (0,))))


@functools.partial(jax.custom_vjp, nondiff_argnums=(2,))
def _mmx(a, b, mode):
    return lax.dot_general(a.astype(bf16), b.astype(bf16), _DIMS[mode], preferred_element_type=f32)


def _mmx_fwd(a, b, mode):
    return _mmx(a, b, mode), (a, b)


def _mmx_bwd(mode, res, g):
    a, b = res
    if mode == "nn":
        return _mmx(g, b, "nt"), _mmx(a, g, "tn")
    if mode == "nt":
        return _mmx(g, b, "nn"), _mmx(g, a, "tn")
    return _mmx(b, g, "nt"), _mmx(a, g, "nn")


_mmx.defvjp(_mmx_fwd, _mmx_bwd)


def _pair_iota():
    row = lax.broadcasted_iota(jnp.int32, (CHUNK, 128), 0)
    lane = lax.broadcasted_iota(jnp.int32, (CHUNK, 128), 1)
    return row, lane, lane & (CHUNK - 1)


def _bd(x):
    _, lane, _ = _pair_iota()
    m0 = lane < CHUNK
    return jnp.concatenate([jnp.where(m0, x, 0.0), jnp.where(m0, 0.0, x)], axis=1)


def _pmm(a, b):
    return _mmx(a, _bd(b), "nn")


def _ntp(x, y):
    return _mmx(x, _bd(y), "nt")


def _tnp(x, y):
    full = _mmx(x, y, "tn")
    _, lane, _ = _pair_iota()
    return jnp.where(lane < CHUNK, full[:, :CHUNK], full[:, CHUNK:])


def _tri_inv(a):
    row, lane, jj = _pair_iota()
    eye = jnp.where(row == jj, 1.0, 0.0).astype(f32)

    def same_block(log2b):
        return (row >> log2b) == (jj >> log2b)

    dgl = jnp.where(same_block(3), a, 0.0)
    d2 = _pmm(dgl, dgl)
    d4 = _pmm(d2, d2)
    t = _pmm(_pmm(eye - dgl, eye + d2), eye + d4)
    for lb in (3, 4, 5):
        off = jnp.where(same_block(lb + 1) & jnp.logical_not(same_block(lb)), a, 0.0)
        t = t - _pmm(_pmm(t, off), t)
    return t


@jax.custom_vjp
def _solve2(a, xv, xk, t):
    return _pmm(t, xv), _pmm(t, xk)


def _solve2_fwd(a, xv, xk, t):
    u, w = _pmm(t, xv), _pmm(t, xk)
    return (u, w), (t, u, w)


def _solve2_bwd(res, cts):
    t, u, w = res
    du, dw = cts
    dxv = _tnp(t, du)
    dxk = _tnp(t, dw)
    return -(_ntp(dxv, u) + _ntp(dxk, w)), dxv, dxk, jnp.zeros_like(t)


_solve2.defvjp(_solve2_fwd, _solve2_bwd)


def _chunk_pre(qp, kp, vp, bp, gcum, t=None):
    row, lane, jj = _pair_iota()
    causal = row >= jj
    strict = row > jj
    rsel = jnp.sum(jnp.where(row == jj, gcum, 0.0), axis=1, keepdims=True)
    decay = jnp.where(causal, jnp.exp(jnp.where(causal, gcum - rsel, 0.0)), 0.0)
    kb = kp * bp
    kd = _bd(kp)
    a = jnp.where(strict, _mmx(kb, kd, "nt") * decay, 0.0)
    eg = jnp.exp(gcum)
    if t is None:
        t = _tri_inv(a)
    u, w = _solve2(a, vp * bp, kb * eg, t)
    qk = jnp.where(causal, _mmx(qp, kd, "nt") * decay, 0.0)
    glast = jnp.sum(jnp.where(row == CHUNK - 1, gcum, 0.0), axis=1, keepdims=True)
    return u, w, qp * eg, kp * jnp.exp(glast - gcum), qk, jnp.exp(glast), t


def _chunk_post(u, w, qt, kh, qk, gam, sp):
    sd = _bd(sp)
    vnew = u - _mmx(w, sd, "nn")
    o = _mmx(qt, sd, "nn") + _pmm(qk, vnew)
    return o, gam * sp + _tnp(kh, vnew)


def _pair_spec(rows=DELTA_ROWS):
    return pl.BlockSpec((rows, 128), lambda i, p: (i, p))


DELTA_NB = DELTA_ROWS // CHUNK


def _chunks(ref):
    return ref[...].reshape(DELTA_NB, CHUNK, 128)


def _pairs(ref, rows):
    return jnp.stack([ref[rows, p * 128:(p + 1) * 128] for p in range(4)], axis=0)


def _delta_chunk_pre(qn, kn, sv, beta, g, xchg):
    s = qn.shape[0]

    def body(q_ref, k_ref, v_ref, b_ref, g_ref, u_ref, w_ref, qt_ref, kh_ref, qk_ref, t_ref, gm_ref):
        outs = _chunk_pre(_chunks(q_ref), _chunks(k_ref), _chunks(v_ref), _chunks(b_ref), _chunks(g_ref))
        for ref, val in zip((u_ref, w_ref, qt_ref, kh_ref, qk_ref, t_ref), outs[:5] + outs[6:]):
            ref[...] = val.reshape(DELTA_ROWS, 128).astype(ref.dtype)
        gm_ref[...] = jnp.broadcast_to(outs[5], (DELTA_NB, 8, 128)).reshape(DELTA_NB * 8, 128)

    v_spec = pl.BlockSpec((DELTA_ROWS, 128), lambda i, p: (i, 8 + p))
    grid = (s // DELTA_ROWS, 4)
    out = pl.pallas_call(
        _ride(body, 5, 7, xchg, grid), name="delta_chunk_pre", grid=grid,
        in_specs=[_pair_spec(), _pair_spec(), v_spec, _pair_spec(), _pair_spec()] + [_ANY] * xchg.n,
        out_specs=[_pair_spec()] * 6 + [_pair_spec(DELTA_NB * 8)] + [_ANY] * xchg.n,
        out_shape=[jax.ShapeDtypeStruct((s, GROUP_W), f32)] + [jax.ShapeDtypeStruct((s, GROUP_W), bf16)] * 5
        + [jax.ShapeDtypeStruct((s // 8, GROUP_W), f32)] + xchg.out_shape(),
        scratch_shapes=xchg.scratch(),
        compiler_params=_cparams(("arbitrary", "arbitrary")),
    )(qn, kn, sv, beta, g, *xchg.arrs)
    return out[:7], out[7:]


def _delta_scan_fwd(u, w, qt, kh, qk, gm):
    s = u.shape[0]

    def body(u_ref, w_ref, qt_ref, kh_ref, qk_ref, gm_ref, o_ref, ss_ref, st):
        @pl.when(pl.program_id(0) == 0)
        def _():
            st[...] = jnp.zeros_like(st)

        def chunk(ci, carry):
            rows = pl.ds(pl.multiple_of(ci * CHUNK, CHUNK), CHUNK)
            grow = pl.ds(pl.multiple_of(ci * 8, 8), 1)
            sp = st[...]
            o, s2 = _chunk_post(_pairs(u_ref, rows), _pairs(w_ref, rows), _pairs(qt_ref, rows),
                                _pairs(kh_ref, rows), _pairs(qk_ref, rows), _pairs(gm_ref, grow), sp)
            for p in range(4):
                ss_ref[rows, p * 128:(p + 1) * 128] = sp[p]
                o_ref[rows, p * 128:(p + 1) * 128] = o[p]
            st[...] = s2
            return carry

        lax.fori_loop(0, DELTA_NB, chunk, 0)

    spec = pl.BlockSpec((DELTA_ROWS, GROUP_W), lambda i: (i, 0))
    gspec = pl.BlockSpec((DELTA_NB * 8, GROUP_W), lambda i: (i, 0))
    return pl.pallas_call(
        body, name="delta_scan_fwd", grid=(s // DELTA_ROWS,),
        in_specs=[spec] * 5 + [gspec],
        out_specs=[spec, spec],
        out_shape=[jax.ShapeDtypeStruct((s, GROUP_W), f32)] * 2,
        scratch_shapes=[pltpu.VMEM((4, CHUNK, 128), f32)],
        compiler_params=_cparams(("arbitrary",)),
    )(u, w, qt, kh, qk, gm)


def _delta_scan_bwd(w, qt, kh, qk, gm, do, xchg):
    s = w.shape[0]
    nb = s // DELTA_ROWS

    def body(w_ref, qt_ref, kh_ref, qk_ref, gm_ref, do_ref, dso_ref, dst):
        @pl.when(pl.program_id(0) == 0)
        def _():
            dst[...] = jnp.zeros_like(dst)

        def chunk(t, carry):
            ci = DELTA_NB - 1 - t
            rows = pl.ds(pl.multiple_of(ci * CHUNK, CHUNK), CHUNK)
            grow = pl.ds(pl.multiple_of(ci * 8, 8), 1)
            ds = dst[...]
            for p in range(4):
                dso_ref[rows, p * 128:(p + 1) * 128] = ds[p]
            do = _pairs(do_ref, rows)
            dvn = _tnp(_pairs(qk_ref, rows), do) + _pmm(_pairs(kh_ref, rows), ds)
            dst[...] = _tnp(_pairs(qt_ref, rows), do) + _pairs(gm_ref, grow) * ds - _tnp(_pairs(w_ref, rows), dvn)
            return carry

        lax.fori_loop(0, DELTA_NB, chunk, 0)

    spec = pl.BlockSpec((DELTA_ROWS, GROUP_W), lambda i: (nb - 1 - i, 0))
    gspec = pl.BlockSpec((DELTA_NB * 8, GROUP_W), lambda i: (nb - 1 - i, 0))
    out = pl.pallas_call(
        _ride(body, 6, 1, xchg, (nb,)), name="delta_scan_bwd", grid=(nb,),
        in_specs=[spec] * 4 + [gspec, spec] + [_ANY] * xchg.n,
        out_specs=[spec] + [_ANY] * xchg.n,
        out_shape=[jax.ShapeDtypeStruct((s, GROUP_W), f32)] + xchg.out_shape(),
        scratch_shapes=[pltpu.VMEM((4, CHUNK, 128), f32)] + xchg.scratch(),
        compiler_params=_cparams(("arbitrary",)),
    )(w, qt, kh, qk, gm, do, *xchg.arrs)
    return out[0], out[1:]


def _delta_chunk_bwd(qn, kn, sv, beta, g, tinv, ss, dso, do, xchg):
    s = qn.shape[0]

    def body(q_ref, k_ref, v_ref, b_ref, g_ref, t_ref, ss_ref, dso_ref, do_ref,
             dq_ref, dk_ref, dv_ref, db_ref, dg_ref):
        sp = _chunks(ss_ref)
        t = _chunks(t_ref)

        def fn(q, k, v, b, gg):
            return _chunk_post(*_chunk_pre(q, k, v, b, gg, t)[:6], sp)

        _, vjp = jax.vjp(fn, _chunks(q_ref), _chunks(k_ref), _chunks(v_ref), _chunks(b_ref), _chunks(g_ref))
        grads = vjp((_chunks(do_ref), _chunks(dso_ref)))
        for ref, val in zip((dq_ref, dk_ref, dv_ref, db_ref, dg_ref), grads):
            ref[...] = val.reshape(DELTA_ROWS, 128)

    v_spec = pl.BlockSpec((DELTA_ROWS, 128), lambda i, p: (i, 8 + p))
    grid = (s // DELTA_ROWS, 4)
    out = pl.pallas_call(
        _ride(body, 9, 5, xchg, grid), name="delta_chunk_bwd", grid=grid,
        in_specs=[_pair_spec(), _pair_spec(), v_spec] + [_pair_spec()] * 6 + [_ANY] * xchg.n,
        out_specs=[_pair_spec()] * 5 + [_ANY] * xchg.n,
        out_shape=[jax.ShapeDtypeStruct((s, GROUP_W), f32)] * 5 + xchg.out_shape(),
        scratch_shapes=xchg.scratch(),
        compiler_params=_cparams(("arbitrary", "arbitrary")),
    )(qn, kn, sv, beta, g, tinv, ss, dso, do, *xchg.arrs)
    return out[:5], out[5:]


def _head_sums(x):
    r = lax.broadcasted_iota(jnp.int32, (128, 128), 0)
    c = lax.broadcasted_iota(jnp.int32, (128, 128), 1)
    pair = jnp.where((r >> 6) == (c >> 6), 1.0, 0.0).astype(f32)
    npair = x.shape[1] // 128
    xb = jnp.concatenate([x[None, :, p * 128:(p + 1) * 128] for p in range(npair)], axis=0)
    sums = _mmx(xb, jnp.broadcast_to(pair, (npair, 128, 128)), "nn")
    return jnp.concatenate([sums[p] for p in range(npair)], axis=1)


def _sel_dot(a, b):
    return jnp.dot(a, b, precision=lax.Precision.HIGH, preferred_element_type=f32)


def _expand_matrix(first):
    r = lax.broadcasted_iota(jnp.int32, (128, GROUP_W), 0)
    c = lax.broadcasted_iota(jnp.int32, (128, GROUP_W), 1) >> 6
    return jnp.where(r == c + first, 1.0, 0.0).astype(f32)


@functools.partial(jax.custom_vjp, nondiff_argnums=(1,))
def _expand_heads(ba, first):
    return _sel_dot(ba, _expand_matrix(first))


def _expand_heads_fwd(ba, first):
    return _expand_heads(ba, first), None


def _expand_heads_bwd(first, _, g):
    return (_mmx(g[None], _expand_matrix(first)[None], "nt")[0],)


_expand_heads.defvjp(_expand_heads_fwd, _expand_heads_bwd)


def _softplus(x):
    return jnp.maximum(x, 0.0) + jnp.log(1.0 + jnp.exp(-jnp.abs(x)))


def _prep_fn(sq, sk, ba, alog_e, dt_e):
    qn = sq * lax.rsqrt(_head_sums(sq * sq) + EPS) * (HEAD_DIM ** -0.5)
    kn = sk * lax.rsqrt(_head_sums(sk * sk) + EPS)
    bl = _expand_heads(ba, 0)
    al = _expand_heads(ba, N_HEADS)
    beta = jax.nn.sigmoid(bl)
    g = -jnp.exp(alog_e) * _softplus(al + dt_e)
    nchunk = g.shape[0] // CHUNK
    ri = lax.broadcasted_iota(jnp.int32, (nchunk, CHUNK, CHUNK), 1)
    ci = lax.broadcasted_iota(jnp.int32, (nchunk, CHUNK, CHUNK), 2)
    tril = jnp.where(ri >= ci, 1.0, 0.0).astype(f32)
    gcum = lax.dot_general(tril, g.reshape(nchunk, CHUNK, g.shape[1]), _BNN, precision=lax.Precision.HIGH,
                           preferred_element_type=f32)
    return qn, kn, beta, gcum.reshape(g.shape)


def _gnorm_fn(o, z, ng_e):
    ms = _head_sums(o * o) * (1.0 / HEAD_DIM)
    return o * lax.rsqrt(ms + EPS) * ng_e * (z * jax.nn.sigmoid(z))


def _tok_spec(width, col):
    return pl.BlockSpec((TOK_TILE, width), lambda i: (i, col))


def _conv_taps(xs_ref, w_ref, base, n, cols):
    acc = w_ref[CONV_WIDTH - 1:CONV_WIDTH, cols] * xs_ref[pl.ds(base, n), cols]
    for j in range(CONV_WIDTH - 1):
        acc = acc + w_ref[j:j + 1, cols] * xs_ref[pl.ds(base - (CONV_WIDTH - 1) + j, n), cols]
    return acc


def _conv_silu_fwd(proj, conv_w):
    s = proj.shape[0]
    wd = 3 * GROUP_W
    hb = TOK_TILE // 8

    def body(x_ref, halo_ref, w_ref, o_ref, y_ref, xs):
        inner = pl.program_id(0) > 0

        def lane_block(cb, carry):
            cols = pl.ds(pl.multiple_of(cb * 128, 128), 128)
            xs[0:8, cols] = jnp.where(inner, halo_ref[:, cols], 0.0)
            xs[8:, cols] = x_ref[:, cols]
            y = _conv_taps(xs, w_ref, 8, TOK_TILE, cols)
            y_ref[:, cols] = y
            o_ref[:, cols] = y * jax.nn.sigmoid(y)
            return carry

        lax.fori_loop(0, wd // 128, lane_block, 0)

    return pl.pallas_call(
        body, name="delta_conv_fwd", grid=(s // TOK_TILE,),
        in_specs=[_tok_spec(wd, 1), pl.BlockSpec((8, wd), lambda i: (jnp.maximum(i * hb - 1, 0), 1)),
                  pl.BlockSpec((CONV_WIDTH, wd), lambda i: (0, 0))],
        out_specs=[_tok_spec(wd, 0)] * 2,
        out_shape=[jax.ShapeDtypeStruct((s, wd), f32)] * 2,
        scratch_shapes=[pltpu.VMEM((TOK_TILE + 8, wd), f32)],
        compiler_params=_cparams(("parallel",)),
    )(proj, proj, conv_w)


def _conv_silu_bwd(proj, conv_w, yc, ds3, xchg):
    s = proj.shape[0]
    wd = 3 * GROUP_W
    hb = TOK_TILE // 8
    nt = s // TOK_TILE

    def body(x_ref, hp_ref, y_ref, yn_ref, dq_ref, dk_ref, dv_ref, dqn_ref, dkn_ref, dvn_ref, w_ref,
             dx_ref, dw_ref, xs, dys):
        i = pl.program_id(0)

        @pl.when(i == 0)
        def _():
            dw_ref[...] = jnp.zeros_like(dw_ref)

        last = i == nt - 1
        def lane_block(lb, carry, third, cur, nxt):
            tcols = pl.ds(pl.multiple_of(lb * 128, 128), 128)
            cols = pl.ds(pl.multiple_of(third * GROUP_W + lb * 128, 128), 128)
            xs[0:8, cols] = jnp.where(i > 0, hp_ref[:, cols], 0.0)
            xs[8:, cols] = x_ref[:, cols]
            y = y_ref[:, cols]
            sg = jax.nn.sigmoid(y)
            dy0 = cur[:, tcols] * (sg * (1.0 + y * (1.0 - sg)))
            dys[0:TOK_TILE, cols] = dy0
            yn = yn_ref[:, cols]
            sgn = jax.nn.sigmoid(yn)
            dys[TOK_TILE:, cols] = jnp.where(last, 0.0, nxt[:, tcols]) * (sgn * (1.0 + yn * (1.0 - sgn)))
            dx = w_ref[CONV_WIDTH - 1:CONV_WIDTH, cols] * dy0
            for j in range(CONV_WIDTH - 1):
                dx = dx + w_ref[j:j + 1, cols] * dys[pl.ds(CONV_WIDTH - 1 - j, TOK_TILE), cols]
            dx_ref[:, cols] = dx.astype(dx_ref.dtype)
            for j in range(CONV_WIDTH):
                dw_ref[j:j + 1, cols] += jnp.sum(dy0 * xs[pl.ds(8 - (CONV_WIDTH - 1) + j, TOK_TILE), cols],
                                                 axis=0, keepdims=True)
            return carry

        for third, (cur, nxt) in enumerate(((dq_ref, dqn_ref), (dk_ref, dkn_ref), (dv_ref, dvn_ref))):
            lax.fori_loop(0, GROUP_W // 128, functools.partial(lane_block, third=third, cur=cur, nxt=nxt), 0)

    prev8 = lambda col: pl.BlockSpec((8, wd), lambda i: (jnp.maximum(i * hb - 1, 0), col))
    next8 = lambda col: pl.BlockSpec((8, wd), lambda i: (jnp.minimum((i + 1) * hb, s // 8 - 1), col))
    next8_third = pl.BlockSpec((8, GROUP_W), lambda i: (jnp.minimum((i + 1) * hb, s // 8 - 1), 0))
    out = pl.pallas_call(
        _ride(body, 11, 2, xchg, (nt,)), name="delta_conv_bwd", grid=(nt,),
        in_specs=[_tok_spec(wd, 1), prev8(1), _tok_spec(wd, 0), next8(0)] + [_tok_spec(GROUP_W, 0)] * 3
        + [next8_third] * 3
        + [pl.BlockSpec((CONV_WIDTH, wd), lambda i: (0, 0))] + [_ANY] * xchg.n,
        out_specs=[_tok_spec(wd, 0), pl.BlockSpec((CONV_WIDTH, wd), lambda i: (0, 0))] + [_ANY] * xchg.n,
        out_shape=[jax.ShapeDtypeStruct((s, wd), bf16), jax.ShapeDtypeStruct((CONV_WIDTH, wd), f32)] + xchg.out_shape(),
        scratch_shapes=[pltpu.VMEM((TOK_TILE + 8, wd), f32), pltpu.VMEM((TOK_TILE + 8, wd), f32)] + xchg.scratch(),
        compiler_params=_cparams(("arbitrary",)),
    )(proj, proj, yc, yc, *ds3, *ds3, conv_w, *xchg.arrs)
    return out[:2], out[2:]


def _delta_prep_fwd(sconv, proj, alog_e, dt_e):
    s = sconv.shape[0]

    def body(sq_ref, sk_ref, ba_ref, al_ref, dt_ref, q_ref, k_ref, b_ref, g_ref):
        qn, kn, beta, g = _prep_fn(sq_ref[...], sk_ref[...], ba_ref[...], al_ref[...], dt_ref[...])
        q_ref[...] = qn
        k_ref[...] = kn
        b_ref[...] = beta
        g_ref[...] = g

    return pl.pallas_call(
        body, name="delta_prep_fwd", grid=(s // TOK_TILE,),
        in_specs=[_tok_spec(GROUP_W, 0), _tok_spec(GROUP_W, 1), _tok_spec(128, BA_BLOCK),
                  _vec_spec(GROUP_W), _vec_spec(GROUP_W)],
        out_specs=[_tok_spec(GROUP_W, 0)] * 4,
        out_shape=[jax.ShapeDtypeStruct((s, GROUP_W), f32)] * 4,
        compiler_params=_cparams(("parallel",)),
    )(sconv, sconv, proj, alog_e, dt_e)


def _delta_prep_bwd(sconv, proj, alog_e, dt_e, dqn, dkn, dbeta, dg, xchg):
    s = sconv.shape[0]
    grid = (s // TOK_TILE,)

    def body(sq_ref, sk_ref, ba_ref, al_ref, dt_ref, dq_ref, dk_ref, db_ref, dg_ref,
             dsq_ref, dsk_ref, dba_ref, dal_ref, ddt_ref):
        @pl.when(pl.program_id(0) == 0)
        def _():
            dal_ref[...] = jnp.zeros_like(dal_ref)
            ddt_ref[...] = jnp.zeros_like(ddt_ref)

        _, vjp = jax.vjp(_prep_fn, sq_ref[...], sk_ref[...], ba_ref[...], al_ref[...], dt_ref[...])
        dsq, dsk, dba, dal, ddt = vjp((dq_ref[...], dk_ref[...], db_ref[...], dg_ref[...]))
        dsq_ref[...] = dsq
        dsk_ref[...] = dsk
        dba_ref[...] = dba.astype(bf16)
        dal_ref[...] += dal
        ddt_ref[...] += ddt

    out = pl.pallas_call(
        _ride(body, 9, 5, xchg, grid), name="delta_prep_bwd", grid=grid,
        in_specs=[_tok_spec(GROUP_W, 0), _tok_spec(GROUP_W, 1), _tok_spec(128, BA_BLOCK),
                  _vec_spec(GROUP_W), _vec_spec(GROUP_W)] + [_tok_spec(GROUP_W, 0)] * 4 + [_ANY] * xchg.n,
        out_specs=[_tok_spec(GROUP_W, 0), _tok_spec(GROUP_W, 0), _tok_spec(128, 0),
                   _acc_spec(GROUP_W), _acc_spec(GROUP_W)] + [_ANY] * xchg.n,
        out_shape=[jax.ShapeDtypeStruct((s, GROUP_W), f32)] * 2 + [jax.ShapeDtypeStruct((s, 128), bf16)]
        + [jax.ShapeDtypeStruct((1, GROUP_W), f32)] * 2 + xchg.out_shape(),
        scratch_shapes=xchg.scratch(),
        compiler_params=_cparams(("arbitrary",)),
    )(sconv, sconv, proj, alog_e, dt_e, dqn, dkn, dbeta, dg, *xchg.arrs)
    return out[:5], out[5:]


def _gnorm_fwd(o, proj, ng_e):
    s = o.shape[0]

    def body(o_ref, z_ref, g_ref, y_ref):
        y_ref[...] = _gnorm_fn(o_ref[...], z_ref[...], g_ref[...])

    return pl.pallas_call(
        body, name="delta_gnorm_fwd", grid=(s // TOK_TILE,),
        in_specs=[_tok_spec(GROUP_W, 0), _tok_spec(GROUP_W, Z_COL // GROUP_W), _vec_spec(GROUP_W)],
        out_specs=_tok_spec(GROUP_W, 0),
        out_shape=jax.ShapeDtypeStruct((s, GROUP_W), f32),
        compiler_params=_cparams(("parallel",)),
    )(o, proj, ng_e)


def _gnorm_bwd(o, proj, ng_e, dycat):
    s = o.shape[0]

    def body(o_ref, z_ref, g_ref, dy_ref, do_ref, dz_ref, dg_ref):
        @pl.when(pl.program_id(0) == 0)
        def _():
            dg_ref[...] = jnp.zeros_like(dg_ref)

        _, vjp = jax.vjp(_gnorm_fn, o_ref[...], z_ref[...], g_ref[...])
        do, dz, dg = vjp(dy_ref[...])
        do_ref[...] = do
        dz_ref[...] = dz.astype(bf16)
        dg_ref[...] += dg

    return pl.pallas_call(
        body, name="delta_gnorm_bwd", grid=(s // TOK_TILE,),
        in_specs=[_tok_spec(GROUP_W, 0), _tok_spec(GROUP_W, Z_COL // GROUP_W), _vec_spec(GROUP_W),
                  _tok_spec(GROUP_W, 1)],
        out_specs=[_tok_spec(GROUP_W, 0), _tok_spec(GROUP_W, 0), _acc_spec(GROUP_W)],
        out_shape=[jax.ShapeDtypeStruct((s, GROUP_W), f32), jax.ShapeDtypeStruct((s, GROUP_W), bf16),
                   jax.ShapeDtypeStruct((1, GROUP_W), f32)],
        compiler_params=_cparams(("arbitrary",)),
    )(o, proj, ng_e, dycat)


_MESH = pl.DeviceIdType.MESH
_ANY = pl.BlockSpec(memory_space=pl.ANY)
_VMEM = pl.BlockSpec(memory_space=pltpu.VMEM)


def _my_place():
    x, y, c = lax.axis_index("x"), lax.axis_index("y"), lax.axis_index("c")
    return x, y, c, 4 * x + 2 * y + c


def _peer(k, x, y, c):
    px = 1 - x if k & 4 else x
    py = 1 - y if k & 2 else y
    pc = 1 - c if k & 1 else c
    return (px, py, pc), 4 * px + 2 * py + pc


def _exchange_all(src_of_peer, dst_ref, send_sems, recv_sems, x, y, c, me):
    sent = []
    for k in range(1, N_DEV):
        dev, pidx = _peer(k, x, y, c)
        cp = pltpu.make_async_remote_copy(src_ref=src_of_peer(pidx), dst_ref=dst_ref.at[me],
                                          send_sem=send_sems.at[k - 1], recv_sem=recv_sems.at[k - 1],
                                          device_id=dev, device_id_type=_MESH)
        cp.start()
        sent.append(cp)
    for k in range(1, N_DEV):
        dev, pidx = _peer(k, x, y, c)
        pltpu.make_async_remote_copy(src_ref=src_of_peer(pidx), dst_ref=dst_ref.at[pidx],
                                     send_sem=send_sems.at[k - 1], recv_sem=recv_sems.at[k - 1],
                                     device_id=dev, device_id_type=_MESH).wait_recv()
    for cp in sent:
        cp.wait_send()


def _ada_exchange(cv8, w_ada, b_ada8):
    def body(cv_ref, w_ref, b_ref, call_ref, modp_ref, part_s, s1, r1, s2, r2):
        x, y, c, me = _my_place()
        call_ref[me] = cv_ref[...]
        _exchange_all(lambda pidx: cv_ref, call_ref, s1, r1, x, y, c, me)
        bias = b_ref[me]
        for j in range(N_DEV):
            cj = call_ref[j][:, :D_MODEL]
            part_s[j] = _hdot(cj * jax.nn.sigmoid(cj), w_ref[...]) + bias
        modp_ref[me] = part_s[me]
        _exchange_all(lambda pidx: part_s.at[pidx], modp_ref, s2, r2, x, y, c, me)

    nsh = w_ada.shape[1]
    return pl.pallas_call(
        body, name="ada_exchange",
        in_specs=[_VMEM, _VMEM, _VMEM], out_specs=[_VMEM, _VMEM],
        out_shape=[jax.ShapeDtypeStruct((N_DEV, 8, cv8.shape[1]), f32), jax.ShapeDtypeStruct((N_DEV, 8, nsh), f32)],
        scratch_shapes=[pltpu.VMEM((N_DEV, 8, nsh), f32)] + [pltpu.SemaphoreType.DMA((N_DEV - 1,))] * 4,
        compiler_params=pltpu.CompilerParams(vmem_limit_bytes=VMEM_LIMIT),
    )(cv8, w_ada, b_ada8)


def _all_to_all(arrs, name):
    ex = _Exchange(arrs, gather=False)

    def body(*refs):
        srcs, dsts, sems = refs[:ex.n], refs[ex.n:2 * ex.n], refs[2 * ex.n:]
        ex.start(srcs, dsts, sems)
        ex.wait(srcs, dsts, sems)

    return pl.pallas_call(
        body, name=name,
        in_specs=[_ANY] * ex.n, out_specs=[_ANY] * ex.n,
        out_shape=ex.out_shape(), scratch_shapes=ex.scratch(),
    )(*arrs)


class _Exchange:
    def __init__(self, arrs, gather):
        self.arrs, self.gather, self.n = list(arrs), gather, len(arrs)

    def out_shape(self):
        return [jax.ShapeDtypeStruct(((N_DEV,) + a.shape) if self.gather else a.shape, a.dtype) for a in self.arrs]

    def scratch(self):
        if self.n == 0:
            return []
        return [pltpu.SemaphoreType.DMA((self.n, N_DEV - 1)), pltpu.SemaphoreType.DMA((self.n, N_DEV - 1)),
                pltpu.SemaphoreType.DMA((self.n,))]

    def _src(self, srcs, a, idx):
        return srcs[a] if self.gather else srcs[a].at[idx]

    def _copies(self, srcs, dsts, sems, incoming):
        send_sems, recv_sems, _ = sems
        x, y, c, me = _my_place()
        out = []
        for a in range(self.n):
            for k in range(1, N_DEV):
                dev, pidx = _peer(k, x, y, c)
                out.append(pltpu.make_async_remote_copy(
                    src_ref=self._src(srcs, a, pidx), dst_ref=dsts[a].at[pidx if incoming else me],
                    send_sem=send_sems.at[a, k - 1], recv_sem=recv_sems.at[a, k - 1],
                    device_id=dev, device_id_type=_MESH))
        return out

    def _local(self, srcs, dsts, sems):
        me = _my_place()[3]
        return [pltpu.make_async_copy(self._src(srcs, a, me), dsts[a].at[me], sems[2].at[a]) for a in range(self.n)]

    def start(self, srcs, dsts, sems):
        for cp in self._local(srcs, dsts, sems) + self._copies(srcs, dsts, sems, incoming=False):
            cp.start()

    def wait(self, srcs, dsts, sems):
        for cp in self._copies(srcs, dsts, sems, incoming=True):
            cp.wait_recv()
        for cp in self._copies(srcs, dsts, sems, incoming=False):
            cp.wait_send()
        for cp in self._local(srcs, dsts, sems):
            cp.wait()

    def start_at_first_step(self, grid, srcs, dsts, sems):
        first = functools.reduce(jnp.logical_and, [pl.program_id(i) == 0 for i in range(len(grid))])
        pl.when(first)(lambda: self.start(srcs, dsts, sems))

    def wait_at_last_step(self, grid, srcs, dsts, sems):
        last = functools.reduce(jnp.logical_and, [pl.program_id(i) == g - 1 for i, g in enumerate(grid)])
        pl.when(last)(lambda: self.wait(srcs, dsts, sems))


class _ChipGather:
    def __init__(self, shard):
        self.shard = shard

    def out_shape(self):
        return jax.ShapeDtypeStruct((N_DEV,) + self.shard.shape, self.shard.dtype)

    def scratch(self):
        return [pltpu.SemaphoreType.DMA((N_DEV - 1,)), pltpu.SemaphoreType.DMA((N_DEV - 1,)),
                pltpu.SemaphoreType.DMA(())]

    def _place(self):
        x, y, c, me = _my_place()
        return x, y, c, me, (x, y, 1 - c), [(1 - x, y), (x, 1 - y), (1 - x, 1 - y)]

    def _copy(self, out, sems, k, block, to, src=None):
        rows = out.at[4 * block[0] + 2 * block[1] + block[2]]
        return pltpu.make_async_remote_copy(src_ref=rows if src is None else src, dst_ref=rows,
                                            send_sem=sems[0].at[k], recv_sem=sems[1].at[k],
                                            device_id=to, device_id_type=_MESH)

    def start(self, src, out, sems):
        x, y, c, me, sib, chips = self._place()
        pltpu.make_async_copy(src, out.at[me], sems[2]).start()
        self._copy(out, sems, 0, (x, y, c), sib, src=src).start()
        for j, chip in enumerate(chips):
            self._copy(out, sems, 1 + j, (x, y, c), (*chip, c), src=src).start()

    def forward(self, src, out, sems):
        x, y, c, me, sib, chips = self._place()
        for j, chip in enumerate(chips):
            self._copy(out, sems, 1 + j, (*chip, c), (x, y, c)).wait_recv()
            self._copy(out, sems, 4 + j, (*chip, c), sib).start()

    def finish(self, src, out, sems):
        x, y, c, me, sib, chips = self._place()
        self._copy(out, sems, 0, (x, y, 1 - c), (x, y, c)).wait_recv()
        for j, chip in enumerate(chips):
            self._copy(out, sems, 4 + j, (*chip, 1 - c), (x, y, c)).wait_recv()
        self._copy(out, sems, 0, (x, y, c), sib, src=src).wait_send()
        for j, chip in enumerate(chips):
            self._copy(out, sems, 1 + j, (x, y, c), (*chip, c), src=src).wait_send()
            self._copy(out, sems, 4 + j, (*chip, c), sib).wait_send()
        pltpu.make_async_copy(src, out.at[me], sems[2]).wait()


def _ride_gathers(body, n_in, n_out, rides, grid, forward_step):
    n = len(rides)
    sizes = list(grid)

    def wrapped(*refs):
        ins, xs = refs[:n_in], refs[n_in:n_in + n]
        outs, xd = refs[n_in + n:n_in + n + n_out], refs[n_in + n + n_out:n_in + 2 * n + n_out]
        scratch = refs[n_in + 2 * n + n_out:]
        own, sems = scratch[:len(scratch) - 3 * n], scratch[len(scratch) - 3 * n:]
        step = pl.program_id(0)
        for i in range(1, len(sizes)):
            step = step * sizes[i] + pl.program_id(i)

        def each(phase):
            for r in range(n):
                getattr(rides[r], phase)(xs[r], xd[r], sems[3 * r:3 * r + 3])

        pl.when(step == 0)(lambda: each("start"))
        body(*ins, *outs, *own)
        pl.when(step == forward_step)(lambda: each("forward"))
        pl.when(step == math.prod(sizes) - 1)(lambda: each("finish"))

    return wrapped


def _ride(body, n_in, n_out, xchg, grid):
    nx = xchg.n
    if nx == 0:
        return body

    def wrapped(*refs):
        ins, xs = refs[:n_in], refs[n_in:n_in + nx]
        outs, xd = refs[n_in + nx:n_in + nx + n_out], refs[n_in + nx + n_out:n_in + 2 * nx + n_out]
        scratch = refs[n_in + 2 * nx + n_out:]
        xchg.start_at_first_step(grid, xs, xd, scratch[-3:])
        body(*ins, *outs, *scratch[:-3])
        xchg.wait_at_last_step(grid, xs, xd, scratch[-3:])

    return wrapped


def _adamw_math(w, g, m, v):
    m2 = ADAM_B1 * m + (1.0 - ADAM_B1) * g
    v2 = ADAM_B2 * v + (1.0 - ADAM_B2) * (g * g)
    m_hat = m2 / (1.0 - ADAM_B1 ** ADAM_STEP)
    v_hat = v2 / (1.0 - ADAM_B2 ** ADAM_STEP)
    delta = -ADAM_LR * (m_hat / (jnp.sqrt(v_hat) + ADAM_EPS) + ADAM_WD * w)
    return delta, m2, v2


def _row_tile(rows):
    for t in (256, 128, 64, 32, 16, 8):
        if rows % t == 0:
            return t
    return rows


def _reduce_adamw(parts, w, m, v, name):
    _, r, cdim = parts.shape
    tr = _row_tile(r)

    def body(p_ref, w_ref, m_ref, v_ref, g_ref, d_ref, m2_ref, v2_ref):
        g = p_ref[0].astype(f32)
        for j in range(1, N_DEV):
            g = g + p_ref[j].astype(f32)
        delta, m2, v2 = _adamw_math(w_ref[...], g, m_ref[...], v_ref[...])
        g_ref[...] = g
        d_ref[...] = delta
        m2_ref[...] = m2
        v2_ref[...] = v2

    spec = pl.BlockSpec((tr, cdim), lambda i: (i, 0))
    return pl.pallas_call(
        body, name=name, grid=(r // tr,),
        in_specs=[pl.BlockSpec((N_DEV, tr, cdim), lambda i: (0, i, 0)), spec, spec, spec],
        out_specs=[spec] * 4,
        out_shape=[jax.ShapeDtypeStruct((r, cdim), f32)] * 4,
        compiler_params=_cparams(("parallel",)),
    )(parts, w, m, v)


def _adamw(w, g, m, v, name):
    r, cdim = w.shape
    tr = _row_tile(r)

    def body(w_ref, g_ref, m_ref, v_ref, d_ref, m2_ref, v2_ref):
        delta, m2, v2 = _adamw_math(w_ref[...], g_ref[...], m_ref[...], v_ref[...])
        d_ref[...] = delta
        m2_ref[...] = m2
        v2_ref[...] = v2

    spec = pl.BlockSpec((tr, cdim), lambda i: (i, 0))
    return pl.pallas_call(
        body, name=name, grid=(r // tr,),
        in_specs=[spec] * 4, out_specs=[spec] * 3,
        out_shape=[jax.ShapeDtypeStruct((r, cdim), f32)] * 3,
        compiler_params=_cparams(("parallel",)),
    )(w, g, m, v)


def _sum_devices(parts, name):
    _, r, cdim = parts.shape

    def body(p_ref, o_ref):
        g = p_ref[0]
        for j in range(1, N_DEV):
            g = g + p_ref[j]
        o_ref[...] = g

    return pl.pallas_call(
        body, name=name, out_shape=jax.ShapeDtypeStruct((r, cdim), f32),
        in_specs=[_VMEM], out_specs=_VMEM,
    )(parts)


def _ada_wgrad(c_all8, dmod_cols):
    nsh = dmod_cols.shape[1]

    def body(c_ref, d_ref, o_ref):
        cv = c_ref[...]
        o_ref[...] = lax.dot_general(cv * jax.nn.sigmoid(cv), d_ref[...], _TN, precision=_HI,
                                     preferred_element_type=f32)

    return pl.pallas_call(
        body, name="ada_wgrad", out_shape=jax.ShapeDtypeStruct((D_MODEL, nsh), f32),
        in_specs=[_VMEM, _VMEM], out_specs=_VMEM,
        compiler_params=pltpu.CompilerParams(vmem_limit_bytes=VMEM_LIMIT),
    )(c_all8, dmod_cols)


def _cols(t):
    return t.transpose(1, 0, 2).reshape(t.shape[1], N_DEV * t.shape[2])


def _col_blocks(t, n):
    return t.reshape(t.shape[0], N_DEV, n).transpose(1, 0, 2).astype(bf16)


def _row_blocks(t):
    return t.reshape(N_DEV, t.shape[0] // N_DEV, t.shape[1]).astype(bf16)


def _local_step(x, tgt, mod, norm_attn_g, w_in_sh, rel_bias, conv_full, a_log, dt_bias, delta_norm_g,
                norm_ffn_g, final_norm_g, w_out_sh, w_gate_sh, w_up_sh, w_down_sh):
    s = x.shape[0]
    sh1, sc1, g1, sh2, sc2, g2 = [mod[:, i * D_MODEL:(i + 1) * D_MODEL] for i in range(6)]
    nag = norm_attn_g.reshape(1, D_MODEL)
    nfg = norm_ffn_g.reshape(1, D_MODEL)
    fg = final_norm_g.reshape(1, D_MODEL)
    idx = _bucket_tables()
    bias = _bias_tables(rel_bias, idx)
    alog_e = jnp.repeat(a_log.reshape(N_HEADS), HEAD_DIM)[None]
    dt_e = jnp.repeat(dt_bias.reshape(N_HEADS), HEAD_DIM)[None]
    ng_e = jnp.tile(delta_norm_g.reshape(HEAD_DIM), N_HEADS)[None]

    h1, w_in_g = _ln_mod_fwd(x, nag, sc1, sh1, w_in_sh, "ln1_fwd")
    w_in_p = jnp.pad(_cols(w_in_g), ((0, 0), (0, IN_PAD - IN_WIDTH)))
    proj, (w_out_g,) = _mm(h1, w_in_p, "nn", f32, 512, IN_PAD, 1024, "in_proj",
                           xchg=_Exchange([w_out_sh], gather=True))
    (y_attn, lse), (w_gate_g, w_up_g, w_down_g) = _attn_fwd(proj, bias, [w_gate_sh, w_up_sh, w_down_sh])
    w_out_b = w_out_g.reshape(2 * GROUP_W, D_MODEL)
    w_down_b = w_down_g.reshape(D_FF, D_MODEL)
    w_gate_b, w_up_b = _cols(w_gate_g), _cols(w_up_g)
    n_ff = w_gate_sh.shape[1]
    sconv, yconv = _conv_silu_fwd(proj, conv_full)
    qn, kn, beta, g = _delta_prep_fwd(sconv, proj, alog_e, dt_e)
    (u, w, qt, kh, qk, tinv, gm), _ = _delta_chunk_pre(qn, kn, sconv, beta, g, _Exchange([], gather=False))
    o, ss = _delta_scan_fwd(u, w, qt, kh, qk, gm)
    y_delta = _gnorm_fwd(o, proj, ng_e)
    y, x1, h2 = _proj_resid_ln_mod_fwd([(y_attn, w_out_b[:GROUP_W]), (y_delta, w_out_b[GROUP_W:])],
                                       x, g1, nfg, sc2, sh2, "out_proj_ln2")
    act, gate, up = _ffn_up(h2, w_gate_b, w_up_b, "ffn_up")
    dx2, dy2, loss, dfg, dg2 = _proj_final_loss_bwd(act, w_down_b, x1, g2, fg, tgt, "ffn_down_loss")

    dgate, dup = _ffn_down_dx(dy2, w_down_b, gate, up, "ffn_down_dx")
    g_down = _mm(act, dy2, "tn", f32, 1408, 1024, 2048, "ffn_down_dw")
    (dx1, dsh2, dsc2, dnfg, dy, dg1), (r_down,) = _proj_ln_mod_bwd(
        [(dgate, w_gate_b), (dup, w_up_b)], x1, nfg, sc2, dx2, 256, "ffn_up_dx_ln2",
        _Exchange([_row_blocks(g_down)], gather=False), gate=g1, y=y)
    g_gate = _mm(h2, dgate, "tn", f32, 1024, 1408, 2048, "ffn_gate_dw")
    g_up = _mm(h2, dup, "tn", f32, 1024, 1408, 2048, "ffn_up_dw")
    dycat = _mm(dy, w_out_b, "nt", f32, 512, 1024, 1024, "out_proj_dx")
    g_out = jnp.concatenate([_mm(y_attn, dy, "tn", f32, GROUP_W, 1024, 2048, "out_proj_dw_attn"),
                             _mm(y_delta, dy, "tn", f32, GROUP_W, 1024, 2048, "out_proj_dw_delta")], axis=0)
    dq, dk, dv, dbias = _attn_bwd(proj, bias, y_attn, lse, dycat)
    g_rb = _bias_grad(dbias, idx)[:, :, 0].T
    do, dz, dng = _gnorm_bwd(o, proj, ng_e, dycat)
    dso, _ = _delta_scan_bwd(w, qt, kh, qk, gm, do, _Exchange([], gather=False))
    (dqn, dkn, dvd, dbeta, dgd), (r_up,) = _delta_chunk_bwd(
        qn, kn, sconv, beta, g, tinv, ss, dso, do, _Exchange([_col_blocks(g_up, n_ff)], gather=False))
    (dsq, dsk, dba, dal, ddt), _ = _delta_prep_bwd(
        sconv, proj, alog_e, dt_e, dqn, dkn, dbeta, dgd, _Exchange([], gather=False))
    (dxc, g_conv), (r_gate, r_out) = _conv_silu_bwd(
        proj, conv_full, yconv, (dsq, dsk, dvd),
        _Exchange([_col_blocks(g_gate, n_ff), _row_blocks(g_out)], gather=False))
    pieces = ((dq, 0), (dk, GROUP_W), (dv, 2 * GROUP_W), (dxc, DELTA_COL), (dz, Z_COL), (dba, BA_BLOCK * 128))
    g_in = jnp.concatenate(
        [_mm(h1, p, "tn", f32, 1024, min(p.shape[1], 768), 2048, "in_proj_dw_%d" % c) for p, c in pieces], axis=1)
    (gx, dsh1, dsc1, dnag), (r_in,) = _proj_ln_mod_bwd(
        [(p, w_in_p[:, c:c + p.shape[1]]) for p, c in pieces], x, nag, sc1, dx1, TOK_TILE, "in_proj_dx_ln1",
        _Exchange([_col_blocks(g_in[:, :IN_WIDTH], IN_WIDTH // N_DEV)], gather=False))
    grads = dict(
        x=gx, mod=jnp.concatenate([dsh1, dsc1, dg1, dsh2, dsc2, dg2], axis=1),
        norm_attn_g=dnag, norm_ffn_g=dnfg, final_norm_g=dfg, rel_bias=g_rb, conv_w=g_conv,
        a_log=dal.reshape(N_HEADS, HEAD_DIM).sum(-1), dt_bias=ddt.reshape(N_HEADS, HEAD_DIM).sum(-1),
        delta_norm_g=dng.reshape(N_HEADS, HEAD_DIM).sum(0),
        w_in=r_in, w_out=r_out, w_gate=r_gate, w_up=r_up, w_down=r_down)
    return loss[0, 0], grads


def _misc_row(rel_bias, a_log, dt_bias, delta_norm_g):
    flat = jnp.concatenate([rel_bias.reshape(-1), a_log.reshape(-1), dt_bias.reshape(-1), delta_norm_g.reshape(-1)])
    return jnp.pad(flat, (0, D_MODEL - flat.shape[0]))[None]


def _pack_small(b_ada, nag, nfg, fng, rel_bias, a_log, dt_bias, dng, conv_shard):
    rows = [b_ada.reshape(6, D_MODEL), nag.reshape(1, D_MODEL), nfg.reshape(1, D_MODEL), fng.reshape(1, D_MODEL),
            _misc_row(rel_bias, a_log, dt_bias, dng),
            jnp.pad(conv_shard.reshape(-1), (0, D_MODEL - conv_shard.size))[None],
            jnp.zeros((5, D_MODEL), f32)]
    return jnp.concatenate(rows, axis=0)


def _unpack_small(p, conv_shape):
    misc = p[9]
    return dict(
        b_ada=p[0:6].reshape(1, 6 * D_MODEL), norm_attn_g=p[6:7], norm_ffn_g=p[7:8], final_norm_g=p[8],
        rel_bias=misc[0:256].reshape(N_BUCKETS, N_HEADS), a_log=misc[256:264].reshape(1, N_HEADS),
        dt_bias=misc[264:272].reshape(1, N_HEADS), delta_norm_g=misc[272:336].reshape(1, HEAD_DIM),
        conv_w=p[10, :conv_shape[1] * conv_shape[2]].reshape(conv_shape))


def kernel(x, c, w_ada, b_ada, norm_attn_g, w_in, rel_bias, conv_w, a_log, dt_bias, delta_norm_g, w_out, norm_ffn_g, w_gate, w_up, w_down, final_norm_g, loss_target, m_w_ada, m_b_ada, m_norm_attn_g, m_w_in, m_rel_bias, m_conv_w, m_a_log, m_dt_bias, m_delta_norm_g, m_w_out, m_norm_ffn_g, m_w_gate, m_w_up, m_w_down, m_final_norm_g, v_w_ada, v_b_ada, v_norm_attn_g, v_w_in, v_rel_bias, v_conv_w, v_a_log, v_dt_bias, v_delta_norm_g, v_w_out, v_norm_ffn_g, v_w_gate, v_w_up, v_w_down, v_final_norm_g):
    me = 4 * lax.axis_index("x") + 2 * lax.axis_index("y") + lax.axis_index("c")
    ada_sh = w_ada.shape[2]
    conv_sh = conv_w.shape[2]

    cv = jnp.concatenate([c[0], conv_w[0].reshape(-1)])
    cv8 = jnp.zeros((8, 2 * D_MODEL), f32).at[0, :cv.shape[0]].set(cv)
    b8 = jnp.broadcast_to(b_ada.reshape(N_DEV, 1, ada_sh), (N_DEV, 8, ada_sh))
    call, modp = _ada_exchange(cv8, w_ada[0], b8)
    mod = modp[:, 0, :].reshape(1, 6 * D_MODEL)
    c_all = call[:, 0, :D_MODEL]
    conv_full = call[:, 0, D_MODEL:D_MODEL + CONV_WIDTH * conv_sh].reshape(N_DEV, CONV_WIDTH, conv_sh)
    conv_full = conv_full.transpose(1, 0, 2).reshape(CONV_WIDTH, N_DEV * conv_sh)

    loss_local, gr = _local_step(x[0], loss_target[0], mod, norm_attn_g, w_in[0].astype(bf16), rel_bias, conv_full, a_log,
                                 dt_bias, delta_norm_g, norm_ffn_g, final_norm_g, w_out[0].astype(bf16),
                                 w_gate[0].astype(bf16), w_up[0].astype(bf16), w_down[0].astype(bf16))
    loss = lax.psum(loss_local, ("x", "y", "c"))

    small = jnp.concatenate([
        gr["mod"].reshape(6, D_MODEL), gr["norm_attn_g"], gr["norm_ffn_g"], gr["final_norm_g"],
        gr["conv_w"].reshape(6, D_MODEL),
        _misc_row(gr["rel_bias"], gr["a_log"], gr["dt_bias"], gr["delta_norm_g"])], axis=0)
    parts = _all_to_all([jnp.broadcast_to(small[None], (N_DEV,) + small.shape)], "small_gather")[0]
    tot = _sum_devices(parts, "small_sum")
    g_conv_full = tot[9:15].reshape(CONV_WIDTH, N_DEV * conv_sh)
    g_conv = lax.dynamic_slice(g_conv_full, (0, me * conv_sh), (CONV_WIDTH, conv_sh))
    misc = tot[15]
    g_small = _pack_small(tot[0:6], tot[6], tot[7], tot[8], misc[0:256], misc[256:264], misc[264:272],
                          misc[272:336], g_conv)
    pk = lambda pre: _pack_small(pre[0], pre[1], pre[2], pre[3], pre[4], pre[5], pre[6], pre[7], pre[8])
    w_small = pk((b_ada, norm_attn_g, norm_ffn_g, final_norm_g, rel_bias, a_log, dt_bias, delta_norm_g, conv_w))
    m_small = pk((m_b_ada, m_norm_attn_g, m_norm_ffn_g, m_final_norm_g, m_rel_bias, m_a_log, m_dt_bias,
                  m_delta_norm_g, m_conv_w))
    v_small = pk((v_b_ada, v_norm_attn_g, v_norm_ffn_g, v_final_norm_g, v_rel_bias, v_a_log, v_dt_bias,
                  v_delta_norm_g, v_conv_w))
    d_small, m2_small, v2_small = _adamw(w_small, g_small, m_small, v_small, "adamw_small")
    cshape = conv_w.shape
    G, Dl, M2, V2 = (_unpack_small(t, cshape) for t in (g_small, d_small, m2_small, v2_small))

    dmod_all = parts[:, 0:6, :].reshape(N_DEV, 6 * D_MODEL)
    dmod_cols = lax.dynamic_slice(dmod_all, (0, me * ada_sh), (N_DEV, ada_sh))
    g_ada = _ada_wgrad(c_all, dmod_cols)
    d_ada, m2_ada, v2_ada = _adamw(w_ada[0], g_ada, m_w_ada[0], v_w_ada[0], "adamw_w_ada")

    big = {}
    for name, w_, m_, v_ in (("w_in", w_in, m_w_in, v_w_in), ("w_out", w_out, m_w_out, v_w_out),
                             ("w_gate", w_gate, m_w_gate, v_w_gate), ("w_up", w_up, m_w_up, v_w_up),
                             ("w_down", w_down, m_w_down, v_w_down)):
        big[name] = [t[None] for t in _reduce_adamw(gr[name], w_[0], m_[0], v_[0], "reduce_adamw_" + name)]

    def leaf(i, name):
        if name == "w_ada":
            return (g_ada, d_ada, m2_ada, v2_ada)[i][None]
        if name in big:
            return big[name][i]
        return (G, Dl, M2, V2)[i][name]

    order = ["w_ada", "b_ada", "norm_attn_g", "w_in", "rel_bias", "conv_w", "a_log", "dt_bias", "delta_norm_g",
             "w_out", "norm_ffn_g", "w_gate", "w_up", "w_down", "final_norm_g"]
    outs = [loss, gr["x"][None]]
    for i in range(4):
        outs += [leaf(i, n) for n in order]
    return tuple(outs)
```

```python
import functools
import math

import jax
import jax.numpy as jnp
from jax import lax
from jax.experimental import pallas as pl
from jax.experimental.pallas import tpu as pltpu

f32 = jnp.float32
bf16 = jnp.bfloat16

D_MODEL = 1024
HEAD_DIM = 64
N_HEADS = 8
GROUP_W = 512
IN_WIDTH = 3600
IN_PAD = 3840
D_FF = 2816
EPS = 1e-6
NEG_INF = -1e30
BAND = 128
PAD_UNIT = 2048
DILATIONS = (1, 4, 16)
N_BUCKETS = 32
MAX_DISTANCE = 2048
CONV_WIDTH = 4
CHUNK = 64
N_DEV = 8
VMEM_LIMIT = 56 * 1024 * 1024

ADAM_LR, ADAM_B1, ADAM_B2, ADAM_EPS, ADAM_WD, ADAM_STEP = 0.001, 0.9, 0.999, 1e-08, 0.01, 10


def _cparams(sem):
    return pltpu.CompilerParams(dimension_semantics=sem, vmem_limit_bytes=VMEM_LIMIT)


def _mm(a, b, mode, out_dtype, tm, tn, tk, name, xchg=None):
    if mode == "nn":
        (m, k), (_, n) = a.shape, b.shape
        a_spec = pl.BlockSpec((tm, tk), lambda j, i, kk: (i, kk))
        b_spec = pl.BlockSpec((tk, tn), lambda j, i, kk: (kk, j))
        dims = (((1,), (0,)), ((), ()))
    elif mode == "nt":
        (m, k), (n, _) = a.shape, b.shape
        a_spec = pl.BlockSpec((tm, tk), lambda j, i, kk: (i, kk))
        b_spec = pl.BlockSpec((tn, tk), lambda j, i, kk: (j, kk))
        dims = (((1,), (1,)), ((), ()))
    else:
        (k, m), (_, n) = a.shape, b.shape
        a_spec = pl.BlockSpec((tk, tm), lambda j, i, kk: (kk, i))
        b_spec = pl.BlockSpec((tk, tn), lambda j, i, kk: (kk, j))
        dims = (((0,), (0,)), ((), ()))
    assert m % tm == 0 and n % tn == 0 and k % tk == 0, (name, m, n, k, tm, tn, tk)
    nk = k // tk
    grid = (n // tn, m // tm, nk)
    nx = xchg.n if xchg is not None else 0

    def body(*refs):
        a_ref, b_ref = refs[:2]
        o_ref = refs[2 + nx]
        scratch = refs[3 + 2 * nx:]
        if nx:
            xrefs = (refs[2:2 + nx], refs[3 + nx:3 + 2 * nx], scratch[-3:])
            xchg.start_at_first_step(grid, *xrefs)
        if nk == 1:
            o_ref[...] = lax.dot_general(a_ref[...].astype(bf16), b_ref[...].astype(bf16), dims,
                                         preferred_element_type=f32).astype(o_ref.dtype)
        else:
            acc_ref = scratch[0]
            kk = pl.program_id(2)

            @pl.when(kk == 0)
            def _():
                acc_ref[...] = jnp.zeros_like(acc_ref)

            acc_ref[...] += lax.dot_general(a_ref[...].astype(bf16), b_ref[...].astype(bf16), dims,
                                            preferred_element_type=f32)

            @pl.when(kk == nk - 1)
            def _():
                o_ref[...] = acc_ref[...].astype(o_ref.dtype)
        if nx:
            xchg.wait_at_last_step(grid, *xrefs)

    out = pl.pallas_call(
        body, name=name, grid=grid,
        in_specs=[a_spec, b_spec] + ([_ANY] * nx),
        out_specs=[pl.BlockSpec((tm, tn), lambda j, i, kk: (i, j))] + ([_ANY] * nx),
        out_shape=[jax.ShapeDtypeStruct((m, n), out_dtype)] + (xchg.out_shape() if nx else []),
        scratch_shapes=([pltpu.VMEM((tm, tn), f32)] if nk > 1 else []) + (xchg.scratch() if nx else []),
        compiler_params=_cparams(("arbitrary",) * 3 if nx else ("parallel", "parallel", "arbitrary")),
    )(a, b, *(xchg.arrs if nx else []))
    return (out[0], out[1:]) if nx else out[0]


TOK_TILE = 512
SUB_COLS = 384


def _row_spec(width, tile=TOK_TILE):
    return pl.BlockSpec((tile, width), lambda i: (i, 0))


def _vec_spec(width, rows=1):
    return pl.BlockSpec((rows, width), lambda i: (0, 0))


def _ln_mod_fwd(x, gain, sc, sh, shard, name):
    s, d = x.shape
    nt = s // TOK_TILE
    ride = _ChipGather(shard)

    def body(x_ref, g_ref, sc_ref, sh_ref, sh_in, h_ref, sh_out, *sems):
        i = pl.program_id(0)
        pl.when(i == 0)(lambda: ride.start(sh_in, sh_out, sems))
        xv = x_ref[...]
        rstd = lax.rsqrt(jnp.mean(xv * xv, axis=-1, keepdims=True) + EPS)
        h = (xv * rstd) * g_ref[...] * (1.0 + sc_ref[...]) + sh_ref[...]
        h_ref[...] = h.astype(bf16)
        @pl.when(i == nt - 1)
        def _():
            ride.forward(sh_in, sh_out, sems)
            ride.finish(sh_in, sh_out, sems)

    return pl.pallas_call(
        body, name=name, grid=(nt,),
        in_specs=[_row_spec(d), _vec_spec(d), _vec_spec(d), _vec_spec(d), _ANY],
        out_specs=[_row_spec(d), _ANY],
        out_shape=[jax.ShapeDtypeStruct((s, d), bf16), ride.out_shape()],
        scratch_shapes=ride.scratch(),
        compiler_params=_cparams(("arbitrary",)),
    )(x, gain, sc, sh, shard)


def _proj_resid_ln_mod_fwd(pairs, x, gate, gain, sc, sh, name):
    s, d = x.shape
    npair = len(pairs)

    def body(*refs):
        aw = refs[:2 * npair]
        x_ref, gt_ref, g_ref, sc_ref, sh_ref, y_ref, x1_ref, h_ref = refs[2 * npair:]
        halves = [slice(r * TOK_TILE, (r + 1) * TOK_TILE) for r in range(2)]
        ys = []
        for rows in halves:
            y = jnp.dot(aw[0][rows, :].astype(bf16), aw[1][...], preferred_element_type=f32)
            for t in range(1, npair):
                y = y + jnp.dot(aw[2 * t][rows, :].astype(bf16), aw[2 * t + 1][...], preferred_element_type=f32)
            ys.append(y)
        for rows, y in zip(halves, ys):
            y_ref[rows, :] = y
            x1 = x_ref[rows, :] + gt_ref[...] * y
            x1_ref[rows, :] = x1
            rstd = lax.rsqrt(jnp.mean(x1 * x1, axis=-1, keepdims=True) + EPS)
            h = (x1 * rstd) * g_ref[...] * (1.0 + sc_ref[...]) + sh_ref[...]
            h_ref[rows, :] = h.astype(bf16)

    tile = 2 * TOK_TILE
    aw_specs, aw = [], []
    for a, w in pairs:
        aw_specs += [_row_spec(a.shape[1], tile), pl.BlockSpec(w.shape, lambda i: (0, 0))]
        aw += [a, w]
    return pl.pallas_call(
        body, name=name, grid=(s // tile,),
        in_specs=aw_specs + [_row_spec(d, tile)] + [_vec_spec(d)] * 4,
        out_specs=[_row_spec(d, tile)] * 3,
        out_shape=[jax.ShapeDtypeStruct((s, d), f32)] * 2 + [jax.ShapeDtypeStruct((s, d), bf16)],
        compiler_params=_cparams(("parallel",)),
    )(*aw, x, gate, gain, sc, sh)


FFN_TN = 1408


def _ffn_up(h2, w_gate, w_up, name):
    s, d = h2.shape
    tm = 2 * TOK_TILE

    def body(h_ref, wg_ref, wu_ref, a_ref, g_ref, u_ref):
        h = h_ref[...]
        g = jnp.dot(h, wg_ref[...], preferred_element_type=f32)
        u = jnp.dot(h, wu_ref[...], preferred_element_type=f32)
        a_ref[...] = (g * jax.nn.sigmoid(g) * u).astype(bf16)
        g_ref[...] = g.astype(bf16)
        u_ref[...] = u.astype(bf16)

    w_spec = pl.BlockSpec((d, FFN_TN), lambda j, i: (0, j))
    o_spec = pl.BlockSpec((tm, FFN_TN), lambda j, i: (i, j))
    return pl.pallas_call(
        body, name=name, grid=(D_FF // FFN_TN, s // tm),
        in_specs=[pl.BlockSpec((tm, d), lambda j, i: (i, 0)), w_spec, w_spec],
        out_specs=[o_spec] * 3,
        out_shape=[jax.ShapeDtypeStruct((s, D_FF), bf16)] * 3,
        compiler_params=_cparams(("parallel", "parallel")),
    )(h2, w_gate, w_up)


def _ffn_down_dx(dy2, w_down, gate, up, name):
    s, d = dy2.shape
    tm = 2 * TOK_TILE

    def body(dy_ref, w_ref, g_ref, u_ref, dg_ref, du_ref):
        dy = dy_ref[...]
        for c0 in range(0, FFN_TN, SUB_COLS):
            cols = slice(c0, min(c0 + SUB_COLS, FFN_TN))
            da = lax.dot_general(dy, w_ref[cols, :], _NT, preferred_element_type=f32)
            g = g_ref[:, cols].astype(f32)
            sg = jax.nn.sigmoid(g)
            du_ref[:, cols] = (da * g * sg).astype(bf16)
            dg_ref[:, cols] = (da * u_ref[:, cols].astype(f32) * sg * (1.0 + g * (1.0 - sg))).astype(bf16)

    t_spec = pl.BlockSpec((tm, FFN_TN), lambda j, i: (i, j))
    return pl.pallas_call(
        body, name=name, grid=(D_FF // FFN_TN, s // tm),
        in_specs=[pl.BlockSpec((tm, d), lambda j, i: (i, 0)), pl.BlockSpec((FFN_TN, d), lambda j, i: (j, 0)),
                  t_spec, t_spec],
        out_specs=[t_spec, t_spec],
        out_shape=[jax.ShapeDtypeStruct((s, D_FF), bf16)] * 2,
        compiler_params=_cparams(("parallel", "parallel")),
    )(dy2, w_down, gate, up)


def _acc_spec(width):
    return pl.BlockSpec((1, width), lambda i: (0, 0))


def _proj_final_loss_bwd(a, w, x1, gate2, final_g, target, name):
    s, d = x1.shape
    k = a.shape[1]

    def body(a_ref, w_ref, x1_ref, gt_ref, fg_ref, tg_ref, dx2_ref, dy2_ref, loss_ref, dfg_ref, dgt_ref):
        @pl.when(pl.program_id(0) == 0)
        def _():
            loss_ref[...] = jnp.zeros_like(loss_ref)
            dfg_ref[...] = jnp.zeros_like(dfg_ref)
            dgt_ref[...] = jnp.zeros_like(dgt_ref)

        y2 = jnp.dot(a_ref[...], w_ref[...], preferred_element_type=f32)
        gt = gt_ref[...]
        fg = fg_ref[...]
        x2 = x1_ref[...] + gt * y2
        rstd = lax.rsqrt(jnp.mean(x2 * x2, axis=-1, keepdims=True) + EPS)
        xn = x2 * rstd
        err = xn * fg - tg_ref[...]
        row = jnp.sum(err * err, axis=-1, keepdims=True) * (0.5 / d)
        loss_ref[...] += jnp.sum(row, axis=0, keepdims=True) + jnp.zeros_like(loss_ref)
        dout = err * (1.0 / d)
        dfg_ref[...] += jnp.sum(dout * xn, axis=0, keepdims=True)
        dxn = dout * fg
        dx2 = rstd * (dxn - xn * jnp.mean(dxn * xn, axis=-1, keepdims=True))
        dx2_ref[...] = dx2
        dgt_ref[...] += jnp.sum(dx2 * y2, axis=0, keepdims=True)
        dy2_ref[...] = (gt * dx2).astype(bf16)

    return pl.pallas_call(
        body, name=name, grid=(s // TOK_TILE,),
        in_specs=[_row_spec(k), pl.BlockSpec((k, d), lambda i: (0, 0)), _row_spec(d), _vec_spec(d), _vec_spec(d),
                  _row_spec(d)],
        out_specs=[_row_spec(d), _row_spec(d), _acc_spec(128), _acc_spec(d), _acc_spec(d)],
        out_shape=[jax.ShapeDtypeStruct((s, d), f32), jax.ShapeDtypeStruct((s, d), bf16),
                   jax.ShapeDtypeStruct((1, 128), f32), jax.ShapeDtypeStruct((1, d), f32),
                   jax.ShapeDtypeStruct((1, d), f32)],
        compiler_params=_cparams(("arbitrary",)),
    )(a, w, x1, gate2, final_g, target)


def _proj_ln_mod_bwd(pairs, xin, gain, sc, dres, tm, name, xchg, gate=None, y=None):
    s, d = xin.shape
    with_gate = gate is not None
    npair = len(pairs)
    n_in = 2 * npair + (7 if with_gate else 5) - 1
    n_out = 6 if with_gate else 4

    def body(*refs):
        ab = refs[:2 * npair]
        if with_gate:
            (x_ref, g_ref, sc_ref, dr_ref, gt_ref, y_ref,
             dx_ref, dsh_ref, dsc_ref, dg_ref, dy_ref, dgt_ref) = refs[2 * npair:]
        else:
            x_ref, g_ref, sc_ref, dr_ref, dx_ref, dsh_ref, dsc_ref, dg_ref = refs[2 * npair:]

        @pl.when(pl.program_id(0) == 0)
        def _():
            dsh_ref[...] = jnp.zeros_like(dsh_ref)
            dsc_ref[...] = jnp.zeros_like(dsc_ref)
            dg_ref[...] = jnp.zeros_like(dg_ref)
            if with_gate:
                dgt_ref[...] = jnp.zeros_like(dgt_ref)

        dh = lax.dot_general(ab[0][...].astype(bf16), ab[1][...], _NT, preferred_element_type=f32)
        for t in range(1, npair):
            dh = dh + lax.dot_general(ab[2 * t][...].astype(bf16), ab[2 * t + 1][...], _NT,
                                      preferred_element_type=f32)
        xv = x_ref[...]
        g = g_ref[...]
        sc1 = 1.0 + sc_ref[...]
        rstd = lax.rsqrt(jnp.mean(xv * xv, axis=-1, keepdims=True) + EPS)
        xn = xv * rstd
        dsh_ref[...] += jnp.sum(dh, axis=0, keepdims=True)
        dsc_ref[...] += jnp.sum(dh * (xn * g), axis=0, keepdims=True)
        dg_ref[...] += jnp.sum(dh * sc1 * xn, axis=0, keepdims=True)
        dxn = dh * sc1 * g
        dx = dr_ref[...] + rstd * (dxn - xn * jnp.mean(dxn * xn, axis=-1, keepdims=True))
        dx_ref[...] = dx
        if with_gate:
            dgt_ref[...] += jnp.sum(dx * y_ref[...], axis=0, keepdims=True)
            dy_ref[...] = (gt_ref[...] * dx).astype(bf16)

    row = lambda width: pl.BlockSpec((tm, width), lambda i: (i, 0))
    in_specs, args = [], []
    for a, b in pairs:
        in_specs += [row(a.shape[1]), pl.BlockSpec(b.shape, lambda i: (0, 0))]
        args += [a, b]
    in_specs += [row(d), _vec_spec(d), _vec_spec(d), row(d)]
    args += [xin, gain, sc, dres]
    out_specs = [row(d), _acc_spec(d), _acc_spec(d), _acc_spec(d)]
    out_shape = [jax.ShapeDtypeStruct((s, d), f32)] + [jax.ShapeDtypeStruct((1, d), f32)] * 3
    if with_gate:
        in_specs += [_vec_spec(d), row(d)]
        out_specs += [row(d), _acc_spec(d)]
        out_shape += [jax.ShapeDtypeStruct((s, d), bf16), jax.ShapeDtypeStruct((1, d), f32)]
        args += [gate, y]
    grid = (s // tm,)
    out = pl.pallas_call(
        _ride(body, n_in, n_out, xchg, grid), name=name, grid=grid,
        in_specs=in_specs + [_ANY] * xchg.n, out_specs=out_specs + [_ANY] * xchg.n,
        out_shape=out_shape + xchg.out_shape(), scratch_shapes=xchg.scratch(),
        compiler_params=_cparams(("arbitrary",)),
    )(*args, *xchg.arrs)
    return out[:n_out], out[n_out:]


def _bucket_tables():
    import numpy as np
    qi = np.arange(BAND)[:, None]
    kj = np.arange(2 * BAND)[None, :]
    steps = qi + BAND - kj
    max_exact = N_BUCKETS // 2
    out = []
    for d in DILATIONS:
        dist = np.maximum(steps, 0) * d
        dist_f = np.maximum(dist, 1).astype(np.float32)
        large = max_exact + (np.log(dist_f / np.float32(max_exact)) / np.float32(math.log(MAX_DISTANCE / max_exact))
                             * np.float32(N_BUCKETS - max_exact)).astype(np.int32)
        out.append(np.where(dist < max_exact, dist, np.minimum(large, N_BUCKETS - 1)))
    return jnp.asarray(np.stack(out).astype(np.int32))


def _bias_tables(rel_bias, idx):
    def body(idx_ref, rb_ref, o_ref):
        h = pl.program_id(1)
        idxv = idx_ref[0]
        acc = jnp.zeros((BAND, 2 * BAND), f32)
        for b in range(N_BUCKETS):
            acc = jnp.where(idxv == b, rb_ref[b, h], acc)
        o_ref[0, 0] = jnp.where(_attn_masks()[1], acc, NEG_INF)

    return pl.pallas_call(
        body, name="attn_bias_tables", grid=(3, N_HEADS),
        in_specs=[pl.BlockSpec((1, BAND, 2 * BAND), lambda br, h: (br, 0, 0)),
                  pl.BlockSpec(memory_space=pltpu.SMEM)],
        out_specs=pl.BlockSpec((1, 1, BAND, 2 * BAND), lambda br, h: (br, h, 0, 0)),
        out_shape=jax.ShapeDtypeStruct((3, N_HEADS, BAND, 2 * BAND), f32),
        compiler_params=_cparams(("parallel", "parallel")),
    )(idx, rel_bias)


def _bias_grad(dbias, idx):
    def body(idx_ref, db_ref, o_ref):
        br = pl.program_id(1)

        @pl.when(br == 0)
        def _():
            o_ref[...] = jnp.zeros_like(o_ref)

        idxv = idx_ref[0]
        dbv = db_ref[0, 0]
        row = lax.broadcasted_iota(jnp.int32, (N_BUCKETS, 128), 0)
        acc = jnp.zeros((N_BUCKETS, 128), f32)
        for b in range(N_BUCKETS):
            sb = jnp.sum(jnp.sum(jnp.where(idxv == b, dbv, 0.0), axis=1, keepdims=True), axis=0, keepdims=True)
            acc = acc + jnp.where(row == b, sb, 0.0)
        o_ref[0] += acc

    return pl.pallas_call(
        body, name="attn_bias_grad", grid=(N_HEADS, 3),
        in_specs=[pl.BlockSpec((1, BAND, 2 * BAND), lambda h, br: (br, 0, 0)),
                  pl.BlockSpec((1, 1, BAND, 2 * BAND), lambda h, br: (br, h, 0, 0))],
        out_specs=pl.BlockSpec((1, N_BUCKETS, 128), lambda h, br: (h, 0, 0)),
        out_shape=jax.ShapeDtypeStruct((N_HEADS, N_BUCKETS, 128), f32),
        compiler_params=_cparams(("parallel", "arbitrary")),
    )(idx, dbias)


def _attn_masks():
    lane = lax.broadcasted_iota(jnp.int32, (BAND, 128), 1)
    m0 = lane < HEAD_DIM
    qi = lax.broadcasted_iota(jnp.int32, (BAND, 2 * BAND), 0)
    kj = lax.broadcasted_iota(jnp.int32, (BAND, 2 * BAND), 1)
    steps = qi + BAND - kj
    in_window = (steps >= 0) & (steps <= BAND)
    return m0, in_window, kj >= BAND


_NT = (((1,), (1,)), ((), ()))
_TN = (((0,), (0,)), ((), ()))
_BNN = (((2,), (1,)), ((0,), (0,)))
_BNT = (((2,), (2,)), ((0,), (0,)))
_BTN = (((1,), (1,)), ((0,), (0,)))
ATTN_GROUP = 4
ATTN_ITEMS = PAD_UNIT // BAND
Q_COL, K_COL, V_COL = 0, 4, 8


def _attn_item_rows(j, d, c, cbase):
    r = lax.rem(j, d)
    b = lax.div(j, d)
    loc = b * (d * BAND) + r
    first = jnp.logical_and(c == 0, b == 0)
    start = cbase + loc
    pstart = jnp.where(first, start, start - d * BAND)
    return loc, start, pstart, first


def _attn_fwd(proj, bias, shards):
    s = proj.shape[0]
    rides = [_ChipGather(sh) for sh in shards]

    def body(q_ref, k_ref, v_ref, b_ref, y_ref, lse_ref, o_s, l_s):
        c = pl.program_id(1)
        cbase = pl.multiple_of(c * PAD_UNIT, PAD_UNIT)
        m0, in_window, cur_half = _attn_masks()
        for bi, d in enumerate(DILATIONS):
            def group(jg, carry, bi=bi, d=d):
                locs, qs, ks, vs, pens = [], [], [], [], []
                for t in range(ATTN_GROUP):
                    loc, start, pstart, first = _attn_item_rows(jg * ATTN_GROUP + t, d, c, cbase)
                    locs.append(loc)
                    qs.append(q_ref[pl.ds(loc, BAND, stride=d), :])
                    ks.append(jnp.concatenate([k_ref[pl.ds(pstart, BAND, stride=d), :],
                                               k_ref[pl.ds(start, BAND, stride=d), :]], axis=0))
                    vs.append(jnp.concatenate([v_ref[pl.ds(pstart, BAND, stride=d), :],
                                               v_ref[pl.ds(start, BAND, stride=d), :]], axis=0))
                    pens.append(jnp.where(cur_half, 0.0, jnp.where(first, NEG_INF, 0.0)))
                q = jnp.stack(qs)
                kk = jnp.stack(ks + ks).astype(bf16)
                vv = jnp.stack(vs + vs).astype(bf16)
                pen = jnp.stack(pens + pens)
                qh = (jnp.concatenate([jnp.where(m0, q, 0.0), jnp.where(m0, 0.0, q)], axis=0) * 0.125).astype(bf16)
                sc = lax.dot_general(qh, kk, _BNT, preferred_element_type=f32)
                sc = (sc.reshape(2, ATTN_GROUP, BAND, 2 * BAND) + b_ref[bi][:, None]).reshape(sc.shape) + pen
                mx = jnp.max(sc, axis=-1, keepdims=True)
                e = jnp.exp(sc - mx)
                l = jnp.sum(e, axis=-1, keepdims=True)
                o = lax.dot_general(e.astype(bf16), vv, _BNN, preferred_element_type=f32) * (1.0 / l)
                ls = mx + jnp.log(l)
                for t in range(ATTN_GROUP):
                    rows = pl.ds(locs[t], BAND, stride=d)
                    o_s[bi, rows, :] = jnp.where(m0, o[t], o[ATTN_GROUP + t])
                    l_s[bi, rows, :] = jnp.where(m0, ls[t], ls[ATTN_GROUP + t])
                return carry

            lax.fori_loop(0, ATTN_ITEMS // ATTN_GROUP, group, 0)

        def merge(t, carry):
            rows = pl.ds(pl.multiple_of(t * 256, 256), 256)
            ls = [l_s[i, rows, :] for i in range(3)]
            mx = jnp.maximum(jnp.maximum(ls[0], ls[1]), ls[2])
            ws = [jnp.exp(l - mx) for l in ls]
            tot = ws[0] + ws[1] + ws[2]
            y = (ws[0] * o_s[0, rows, :] + ws[1] * o_s[1, rows, :] + ws[2] * o_s[2, rows, :]) / tot
            y_ref[rows, :] = y
            lse_ref[rows, :] = mx + jnp.log(tot)
            return carry

        lax.fori_loop(0, PAD_UNIT // 256, merge, 0)

    chunk = lambda col: pl.BlockSpec((PAD_UNIT, 128), lambda p, c: (c, col + p))
    full = lambda col: pl.BlockSpec((s, 128), lambda p, c: (0, col + p))
    grid = (N_HEADS // 2, s // PAD_UNIT)
    nsteps = grid[0] * grid[1]
    out = pl.pallas_call(
        _ride_gathers(body, 4, 2, rides, grid, (3 * nsteps) // 4), name="attn_fwd", grid=grid,
        in_specs=[chunk(Q_COL), full(K_COL), full(V_COL),
                  pl.BlockSpec((3, 2, BAND, 2 * BAND), lambda p, c: (0, p, 0, 0))] + [_ANY] * len(rides),
        out_specs=[chunk(0), chunk(0)] + [_ANY] * len(rides),
        out_shape=[jax.ShapeDtypeStruct((s, GROUP_W), f32)] * 2 + [r.out_shape() for r in rides],
        scratch_shapes=[pltpu.VMEM((3, PAD_UNIT, 128), f32)] * 2 + [t for r in rides for t in r.scratch()],
        compiler_params=_cparams(("arbitrary", "arbitrary")),
    )(proj, proj, proj, bias, *shards)
    return out[:2], out[2:]


def _attn_bwd(proj, bias, y, lse, dycat):
    s = proj.shape[0]

    def body(q_ref, k_ref, v_ref, b_ref, y_ref, lse_ref, dy_ref, dq_ref, dk_ref, dv_ref, db_ref, dd_s):
        c = pl.program_id(1)
        cbase = pl.multiple_of(c * PAD_UNIT, PAD_UNIT)
        m0, in_window, cur_half = _attn_masks()

        @pl.when(c == 0)
        def _():
            dk_ref[...] = jnp.zeros_like(dk_ref)
            dv_ref[...] = jnp.zeros_like(dv_ref)
            db_ref[...] = jnp.zeros_like(db_ref)

        dq_ref[...] = jnp.zeros_like(dq_ref)

        def rowdot(t, carry):
            rows = pl.ds(pl.multiple_of(t * 256, 256), 256)
            prod = dy_ref[rows, :] * y_ref[rows, :]
            lane = lax.broadcasted_iota(jnp.int32, prod.shape, 1)
            h0 = lane < HEAD_DIM
            d0 = jnp.sum(jnp.where(h0, prod, 0.0), axis=-1, keepdims=True)
            d1 = jnp.sum(jnp.where(h0, 0.0, prod), axis=-1, keepdims=True)
            dd_s[rows, :] = jnp.where(h0, d0, d1)
            return carry

        lax.fori_loop(0, PAD_UNIT // 256, rowdot, 0)

        for bi, d in enumerate(DILATIONS):
            def group(jg, carry, bi=bi, d=d):
                ng = ATTN_GROUP
                meta, qs, dos, lqs, dds, ks, vs, pens = [], [], [], [], [], [], [], []
                for t in range(ng):
                    loc, start, pstart, first = _attn_item_rows(jg * ng + t, d, c, cbase)
                    qrows = pl.ds(loc, BAND, stride=d)
                    rows = pl.ds(start, BAND, stride=d)
                    prows = pl.ds(pstart, BAND, stride=d)
                    meta.append((qrows, rows, prows))
                    qs.append(q_ref[qrows, :])
                    dos.append(dy_ref[qrows, :])
                    lqs.append(lse_ref[qrows, :])
                    dds.append(dd_s[qrows, :])
                    ks.append(jnp.concatenate([k_ref[prows, :], k_ref[rows, :]], axis=0))
                    vs.append(jnp.concatenate([v_ref[prows, :], v_ref[rows, :]], axis=0))
                    pens.append(jnp.where(cur_half, 0.0, jnp.where(first, NEG_INF, 0.0)))

                def heads(t):
                    return jnp.concatenate([jnp.where(m0, t, 0.0), jnp.where(m0, 0.0, t)], axis=0)

                def head_col(t):
                    return jnp.concatenate([t[:, :, 0:1], t[:, :, HEAD_DIM:HEAD_DIM + 1]], axis=0)

                qh = (heads(jnp.stack(qs)) * 0.125).astype(bf16)
                doh = heads(jnp.stack(dos)).astype(bf16)
                kk = jnp.stack(ks + ks).astype(bf16)
                vv = jnp.stack(vs + vs).astype(bf16)
                sc = lax.dot_general(qh, kk, _BNT, preferred_element_type=f32)
                sc = (sc.reshape(2, ng, BAND, 2 * BAND) + b_ref[bi][:, None]).reshape(sc.shape) + jnp.stack(pens + pens)
                p = jnp.exp(sc - head_col(jnp.stack(lqs)))
                dp = lax.dot_general(doh, vv, _BNT, preferred_element_type=f32)
                ds = p * (dp - head_col(jnp.stack(dds)))
                db_ref[bi] += jnp.sum(ds.reshape(2, ng, BAND, 2 * BAND), axis=1)
                dsb = ds.astype(bf16)
                dq = lax.dot_general(dsb, kk, _BNN, preferred_element_type=f32) * 0.125
                dk = lax.dot_general(dsb, qh, _BTN, preferred_element_type=f32)
                dv = lax.dot_general(p.astype(bf16), doh, _BTN, preferred_element_type=f32)
                for t in range(ng):
                    qrows, rows, prows = meta[t]
                    dq_ref[qrows, :] += jnp.where(m0, dq[t], dq[ng + t])
                    dkt = dk[t] + dk[ng + t]
                    dvt = dv[t] + dv[ng + t]
                    dk_ref[prows, :] += dkt[:BAND]
                    dk_ref[rows, :] += dkt[BAND:]
                    dv_ref[prows, :] += dvt[:BAND]
                    dv_ref[rows, :] += dvt[BAND:]
                return carry

            lax.fori_loop(0, ATTN_ITEMS // ATTN_GROUP, group, 0)

    chunk = lambda col: pl.BlockSpec((PAD_UNIT, 128), lambda p, c: (c, col + p))
    full = lambda col: pl.BlockSpec((s, 128), lambda p, c: (0, col + p))
    bias_spec = pl.BlockSpec((3, 2, BAND, 2 * BAND), lambda p, c: (0, p, 0, 0))
    return pl.pallas_call(
        body, name="attn_bwd", grid=(N_HEADS // 2, s // PAD_UNIT),
        in_specs=[chunk(Q_COL), full(K_COL), full(V_COL), bias_spec, chunk(0), chunk(0), chunk(0)],
        out_specs=[chunk(0), full(0), full(0), bias_spec],
        out_shape=[jax.ShapeDtypeStruct((s, GROUP_W), f32)] * 3
        + [jax.ShapeDtypeStruct((3, N_HEADS, BAND, 2 * BAND), f32)],
        scratch_shapes=[pltpu.VMEM((PAD_UNIT, 128), f32)],
        compiler_params=_cparams(("parallel", "arbitrary")),
    )(proj, proj, proj, bias, y, lse, dycat)


_HI = lax.Precision.HIGHEST
DELTA_COL = 1536
Z_COL = 3072
BA_BLOCK = 28
DELTA_ROWS = 2048


def _hdot(a, b):
    return jnp.dot(a, b, precision=_HI, preferred_element_type=f32)


_DIMS = dict(nn=(((2,), (1,)), ((0,), (0,))), nt=(((2,), (2,)), ((0,), (0,))), tn=(((1,), (1,)), ((0,), (0,))))


@functools.partial(jax.custom_vjp, nondiff_argnums=(2,))
def _mmx(a, b, mode):
    return lax.dot_general(a.astype(bf16), b.astype(bf16), _DIMS[mode], preferred_element_type=f32)


def _mmx_fwd(a, b, mode):
    return _mmx(a, b, mode), (a, b)


def _mmx_bwd(mode, res, g):
    a, b = res
    if mode == "nn":
        return _mmx(g, b, "nt"), _mmx(a, g, "tn")
    if mode == "nt":
        return _mmx(g, b, "nn"), _mmx(g, a, "tn")
    return _mmx(b, g, "nt"), _mmx(a, g, "nn")


_mmx.defvjp(_mmx_fwd, _mmx_bwd)


def _pair_iota():
    row = lax.broadcasted_iota(jnp.int32, (CHUNK, 128), 0)
    lane = lax.broadcasted_iota(jnp.int32, (CHUNK, 128), 1)
    return row, lane, lane & (CHUNK - 1)


def _bd(x):
    _, lane, _ = _pair_iota()
    m0 = lane < CHUNK
    return jnp.concatenate([jnp.where(m0, x, 0.0), jnp.where(m0, 0.0, x)], axis=1)


def _pmm(a, b):
    return _mmx(a, _bd(b), "nn")


def _ntp(x, y):
    return _mmx(x, _bd(y), "nt")


def _tnp(x, y):
    full = _mmx(x, y, "tn")
    _, lane, _ = _pair_iota()
    return jnp.where(lane < CHUNK, full[:, :CHUNK], full[:, CHUNK:])


def _tri_inv(a):
    row, lane, jj = _pair_iota()
    eye = jnp.where(row == jj, 1.0, 0.0).astype(f32)

    def same_block(log2b):
        return (row >> log2b) == (jj >> log2b)

    dgl = jnp.where(same_block(3), a, 0.0)
    d2 = _pmm(dgl, dgl)
    d4 = _pmm(d2, d2)
    t = _pmm(_pmm(eye - dgl, eye + d2), eye + d4)
    for lb in (3, 4, 5):
        off = jnp.where(same_block(lb + 1) & jnp.logical_not(same_block(lb)), a, 0.0)
        t = t - _pmm(_pmm(t, off), t)
    return t


@jax.custom_vjp
def _solve2(a, xv, xk, t):
    return _pmm(t, xv), _pmm(t, xk)


def _solve2_fwd(a, xv, xk, t):
    u, w = _pmm(t, xv), _pmm(t, xk)
    return (u, w), (t, u, w)


def _solve2_bwd(res, cts):
    t, u, w = res
    du, dw = cts
    dxv = _tnp(t, du)
    dxk = _tnp(t, dw)
    return -(_ntp(dxv, u) + _ntp(dxk, w)), dxv, dxk, jnp.zeros_like(t)


_solve2.defvjp(_solve2_fwd, _solve2_bwd)


def _chunk_pre(qp, kp, vp, bp, gcum, t=None):
    row, lane, jj = _pair_iota()
    causal = row >= jj
    strict = row > jj
    rsel = jnp.sum(jnp.where(row == jj, gcum, 0.0), axis=1, keepdims=True)
    decay = jnp.where(causal, jnp.exp(jnp.where(causal, gcum - rsel, 0.0)), 0.0)
    kb = kp * bp
    kd = _bd(kp)
    a = jnp.where(strict, _mmx(kb, kd, "nt") * decay, 0.0)
    eg = jnp.exp(gcum)
    if t is None:
        t = _tri_inv(a)
    u, w = _solve2(a, vp * bp, kb * eg, t)
    qk = jnp.where(causal, _mmx(qp, kd, "nt") * decay, 0.0)
    glast = jnp.sum(jnp.where(row == CHUNK - 1, gcum, 0.0), axis=1, keepdims=True)
    return u, w, qp * eg, kp * jnp.exp(glast - gcum), qk, jnp.exp(glast), t


def _chunk_post(u, w, qt, kh, qk, gam, sp):
    sd = _bd(sp)
    vnew = u - _mmx(w, sd, "nn")
    o = _mmx(qt, sd, "nn") + _pmm(qk, vnew)
    return o, gam * sp + _tnp(kh, vnew)


def _pair_spec(rows=DELTA_ROWS):
    return pl.BlockSpec((rows, 128), lambda i, p: (i, p))


DELTA_NB = DELTA_ROWS // CHUNK
SCAN_ROWS = 1024
SCAN_NB = SCAN_ROWS // CHUNK


def _chunks(ref):
    return ref[...].reshape(DELTA_NB, CHUNK, 128)


def _pairs(ref, rows):
    return jnp.stack([ref[rows, p * 128:(p + 1) * 128] for p in range(4)], axis=0)


def _delta_chunk_pre(qn, kn, sv, beta, g, xchg):
    s = qn.shape[0]

    def body(q_ref, k_ref, v_ref, b_ref, g_ref, u_ref, w_ref, qt_ref, kh_ref, qk_ref, t_ref, gm_ref):
        outs = _chunk_pre(_chunks(q_ref), _chunks(k_ref), _chunks(v_ref), _chunks(b_ref), _chunks(g_ref))
        for ref, val in zip((u_ref, w_ref, qt_ref, kh_ref, qk_ref, t_ref), outs[:5] + outs[6:]):
            ref[...] = val.reshape(DELTA_ROWS, 128).astype(ref.dtype)
        gm_ref[...] = jnp.broadcast_to(outs[5], (DELTA_NB, 8, 128)).reshape(DELTA_NB * 8, 128)

    v_spec = pl.BlockSpec((DELTA_ROWS, 128), lambda i, p: (i, 8 + p))
    grid = (s // DELTA_ROWS, 4)
    out = pl.pallas_call(
        _ride(body, 5, 7, xchg, grid), name="delta_chunk_pre", grid=grid,
        in_specs=[_pair_spec(), _pair_spec(), v_spec, _pair_spec(), _pair_spec()] + [_ANY] * xchg.n,
        out_specs=[_pair_spec()] * 6 + [_pair_spec(DELTA_NB * 8)] + [_ANY] * xchg.n,
        out_shape=[jax.ShapeDtypeStruct((s, GROUP_W), f32)] + [jax.ShapeDtypeStruct((s, GROUP_W), bf16)] * 5
        + [jax.ShapeDtypeStruct((s // 8, GROUP_W), f32)] + xchg.out_shape(),
        scratch_shapes=xchg.scratch(),
        compiler_params=_cparams(("arbitrary", "arbitrary")),
    )(qn, kn, sv, beta, g, *xchg.arrs)
    return out[:7], out[7:]


def _delta_scan_fwd(u, w, qt, kh, qk, gm):
    s = u.shape[0]

    def body(u_ref, w_ref, qt_ref, kh_ref, qk_ref, gm_ref, o_ref, ss_ref, st):
        @pl.when(pl.program_id(0) == 0)
        def _():
            st[...] = jnp.zeros_like(st)

        def chunk(ci, carry):
            rows = pl.ds(pl.multiple_of(ci * CHUNK, CHUNK), CHUNK)
            grow = pl.ds(pl.multiple_of(ci * 8, 8), 1)
            sp = st[...]
            o, s2 = _chunk_post(_pairs(u_ref, rows), _pairs(w_ref, rows), _pairs(qt_ref, rows),
                                _pairs(kh_ref, rows), _pairs(qk_ref, rows), _pairs(gm_ref, grow), sp)
            for p in range(4):
                ss_ref[rows, p * 128:(p + 1) * 128] = sp[p]
                o_ref[rows, p * 128:(p + 1) * 128] = o[p]
            st[...] = s2
            return carry

        lax.fori_loop(0, SCAN_NB, chunk, 0)

    spec = pl.BlockSpec((SCAN_ROWS, GROUP_W), lambda i: (i, 0))
    gspec = pl.BlockSpec((SCAN_NB * 8, GROUP_W), lambda i: (i, 0))
    return pl.pallas_call(
        body, name="delta_scan_fwd", grid=(s // SCAN_ROWS,),
        in_specs=[spec] * 5 + [gspec],
        out_specs=[spec, spec],
        out_shape=[jax.ShapeDtypeStruct((s, GROUP_W), f32)] * 2,
        scratch_shapes=[pltpu.VMEM((4, CHUNK, 128), f32)],
        compiler_params=_cparams(("arbitrary",)),
    )(u, w, qt, kh, qk, gm)


def _delta_scan_bwd(w, qt, kh, qk, gm, do, xchg):
    s = w.shape[0]
    nb = s // SCAN_ROWS

    def body(w_ref, qt_ref, kh_ref, qk_ref, gm_ref, do_ref, dso_ref, dst):
        @pl.when(pl.program_id(0) == 0)
        def _():
            dst[...] = jnp.zeros_like(dst)

        def chunk(t, carry):
            ci = SCAN_NB - 1 - t
            rows = pl.ds(pl.multiple_of(ci * CHUNK, CHUNK), CHUNK)
            grow = pl.ds(pl.multiple_of(ci * 8, 8), 1)
            ds = dst[...]
            for p in range(4):
                dso_ref[rows, p * 128:(p + 1) * 128] = ds[p]
            do = _pairs(do_ref, rows)
            dvn = _tnp(_pairs(qk_ref, rows), do) + _pmm(_pairs(kh_ref, rows), ds)
            dst[...] = _tnp(_pairs(qt_ref, rows), do) + _pairs(gm_ref, grow) * ds - _tnp(_pairs(w_ref, rows), dvn)
            return carry

        lax.fori_loop(0, SCAN_NB, chunk, 0)

    spec = pl.BlockSpec((SCAN_ROWS, GROUP_W), lambda i: (nb - 1 - i, 0))
    gspec = pl.BlockSpec((SCAN_NB * 8, GROUP_W), lambda i: (nb - 1 - i, 0))
    out = pl.pallas_call(
        _ride(body, 6, 1, xchg, (nb,)), name="delta_scan_bwd", grid=(nb,),
        in_specs=[spec] * 4 + [gspec, spec] + [_ANY] * xchg.n,
        out_specs=[spec] + [_ANY] * xchg.n,
        out_shape=[jax.ShapeDtypeStruct((s, GROUP_W), f32)] + xchg.out_shape(),
        scratch_shapes=[pltpu.VMEM((4, CHUNK, 128), f32)] + xchg.scratch(),
        compiler_params=_cparams(("arbitrary",)),
    )(w, qt, kh, qk, gm, do, *xchg.arrs)
    return out[0], out[1:]


def _delta_chunk_bwd(qn, kn, sv, beta, g, tinv, ss, dso, do, xchg):
    s = qn.shape[0]

    def body(q_ref, k_ref, v_ref, b_ref, g_ref, t_ref, ss_ref, dso_ref, do_ref,
             dq_ref, dk_ref, dv_ref, db_ref, dg_ref):
        sp = _chunks(ss_ref)
        t = _chunks(t_ref)

        def fn(q, k, v, b, gg):
            return _chunk_post(*_chunk_pre(q, k, v, b, gg, t)[:6], sp)

        _, vjp = jax.vjp(fn, _chunks(q_ref), _chunks(k_ref), _chunks(v_ref), _chunks(b_ref), _chunks(g_ref))
        grads = vjp((_chunks(do_ref), _chunks(dso_ref)))
        for ref, val in zip((dq_ref, dk_ref, dv_ref, db_ref, dg_ref), grads):
            ref[...] = val.reshape(DELTA_ROWS, 128)

    v_spec = pl.BlockSpec((DELTA_ROWS, 128), lambda i, p: (i, 8 + p))
    grid = (s // DELTA_ROWS, 4)
    out = pl.pallas_call(
        _ride(body, 9, 5, xchg, grid), name="delta_chunk_bwd", grid=grid,
        in_specs=[_pair_spec(), _pair_spec(), v_spec] + [_pair_spec()] * 6 + [_ANY] * xchg.n,
        out_specs=[_pair_spec()] * 5 + [_ANY] * xchg.n,
        out_shape=[jax.ShapeDtypeStruct((s, GROUP_W), f32)] * 5 + xchg.out_shape(),
        scratch_shapes=xchg.scratch(),
        compiler_params=_cparams(("arbitrary", "arbitrary")),
    )(qn, kn, sv, beta, g, tinv, ss, dso, do, *xchg.arrs)
    return out[:5], out[5:]


def _head_sums(x):
    r = lax.broadcasted_iota(jnp.int32, (128, 128), 0)
    c = lax.broadcasted_iota(jnp.int32, (128, 128), 1)
    pair = jnp.where((r >> 6) == (c >> 6), 1.0, 0.0).astype(f32)
    npair = x.shape[1] // 128
    xb = jnp.concatenate([x[None, :, p * 128:(p + 1) * 128] for p in range(npair)], axis=0)
    sums = _mmx(xb, jnp.broadcast_to(pair, (npair, 128, 128)), "nn")
    return jnp.concatenate([sums[p] for p in range(npair)], axis=1)


def _sel_dot(a, b):
    return jnp.dot(a, b, precision=lax.Precision.HIGH, preferred_element_type=f32)


def _expand_matrix(first):
    r = lax.broadcasted_iota(jnp.int32, (128, GROUP_W), 0)
    c = lax.broadcasted_iota(jnp.int32, (128, GROUP_W), 1) >> 6
    return jnp.where(r == c + first, 1.0, 0.0).astype(f32)


@functools.partial(jax.custom_vjp, nondiff_argnums=(1,))
def _expand_heads(ba, first):
    return _sel_dot(ba, _expand_matrix(first))


def _expand_heads_fwd(ba, first):
    return _expand_heads(ba, first), None


def _expand_heads_bwd(first, _, g):
    return (_mmx(g[None], _expand_matrix(first)[None], "nt")[0],)


_expand_heads.defvjp(_expand_heads_fwd, _expand_heads_bwd)


def _softplus(x):
    return jnp.maximum(x, 0.0) + jnp.log(1.0 + jnp.exp(-jnp.abs(x)))


def _prep_fn(sq, sk, ba, alog_e, dt_e):
    qn = sq * lax.rsqrt(_head_sums(sq * sq) + EPS) * (HEAD_DIM ** -0.5)
    kn = sk * lax.rsqrt(_head_sums(sk * sk) + EPS)
    bl = _expand_heads(ba, 0)
    al = _expand_heads(ba, N_HEADS)
    beta = jax.nn.sigmoid(bl)
    g = -jnp.exp(alog_e) * _softplus(al + dt_e)
    nchunk = g.shape[0] // CHUNK
    ri = lax.broadcasted_iota(jnp.int32, (nchunk, CHUNK, CHUNK), 1)
    ci = lax.broadcasted_iota(jnp.int32, (nchunk, CHUNK, CHUNK), 2)
    tril = jnp.where(ri >= ci, 1.0, 0.0).astype(f32)
    gcum = lax.dot_general(tril, g.reshape(nchunk, CHUNK, g.shape[1]), _BNN, precision=lax.Precision.HIGH,
                           preferred_element_type=f32)
    return qn, kn, beta, gcum.reshape(g.shape)


def _gnorm_fn(o, z, ng_e):
    ms = _head_sums(o * o) * (1.0 / HEAD_DIM)
    return o * lax.rsqrt(ms + EPS) * ng_e * (z * jax.nn.sigmoid(z))


def _tok_spec(width, col):
    return pl.BlockSpec((TOK_TILE, width), lambda i: (i, col))


def _conv_taps(xs_ref, w_ref, base, n, cols):
    acc = w_ref[CONV_WIDTH - 1:CONV_WIDTH, cols] * xs_ref[pl.ds(base, n), cols]
    for j in range(CONV_WIDTH - 1):
        acc = acc + w_ref[j:j + 1, cols] * xs_ref[pl.ds(base - (CONV_WIDTH - 1) + j, n), cols]
    return acc


def _conv_silu_fwd(proj, conv_w):
    s = proj.shape[0]
    wd = 3 * GROUP_W
    hb = TOK_TILE // 8

    def body(x_ref, halo_ref, w_ref, o_ref, y_ref, xs):
        inner = pl.program_id(0) > 0

        def lane_block(cb, carry):
            cols = pl.ds(pl.multiple_of(cb * 128, 128), 128)
            xs[0:8, cols] = jnp.where(inner, halo_ref[:, cols], 0.0)
            xs[8:, cols] = x_ref[:, cols]
            y = _conv_taps(xs, w_ref, 8, TOK_TILE, cols)
            y_ref[:, cols] = y
            o_ref[:, cols] = y * jax.nn.sigmoid(y)
            return carry

        lax.fori_loop(0, wd // 128, lane_block, 0)

    return pl.pallas_call(
        body, name="delta_conv_fwd", grid=(s // TOK_TILE,),
        in_specs=[_tok_spec(wd, 1), pl.BlockSpec((8, wd), lambda i: (jnp.maximum(i * hb - 1, 0), 1)),
                  pl.BlockSpec((CONV_WIDTH, wd), lambda i: (0, 0))],
        out_specs=[_tok_spec(wd, 0)] * 2,
        out_shape=[jax.ShapeDtypeStruct((s, wd), f32)] * 2,
        scratch_shapes=[pltpu.VMEM((TOK_TILE + 8, wd), f32)],
        compiler_params=_cparams(("parallel",)),
    )(proj, proj, conv_w)


def _conv_silu_bwd(proj, conv_w, yc, ds3, xchg):
    s = proj.shape[0]
    wd = 3 * GROUP_W
    hb = TOK_TILE // 8
    nt = s // TOK_TILE

    def body(x_ref, hp_ref, y_ref, yn_ref, dq_ref, dk_ref, dv_ref, dqn_ref, dkn_ref, dvn_ref, w_ref,
             dx_ref, dw_ref, xs, dys):
        i = pl.program_id(0)

        @pl.when(i == 0)
        def _():
            dw_ref[...] = jnp.zeros_like(dw_ref)

        last = i == nt - 1
        def lane_block(lb, carry, third, cur, nxt):
            tcols = pl.ds(pl.multiple_of(lb * 128, 128), 128)
            cols = pl.ds(pl.multiple_of(third * GROUP_W + lb * 128, 128), 128)
            xs[0:8, cols] = jnp.where(i > 0, hp_ref[:, cols], 0.0)
            xs[8:, cols] = x_ref[:, cols]
            y = y_ref[:, cols]
            sg = jax.nn.sigmoid(y)
            dy0 = cur[:, tcols] * (sg * (1.0 + y * (1.0 - sg)))
            dys[0:TOK_TILE, cols] = dy0
            yn = yn_ref[:, cols]
            sgn = jax.nn.sigmoid(yn)
            dys[TOK_TILE:, cols] = jnp.where(last, 0.0, nxt[:, tcols]) * (sgn * (1.0 + yn * (1.0 - sgn)))
            dx = w_ref[CONV_WIDTH - 1:CONV_WIDTH, cols] * dy0
            for j in range(CONV_WIDTH - 1):
                dx = dx + w_ref[j:j + 1, cols] * dys[pl.ds(CONV_WIDTH - 1 - j, TOK_TILE), cols]
            dx_ref[:, cols] = dx.astype(dx_ref.dtype)
            for j in range(CONV_WIDTH):
                dw_ref[j:j + 1, cols] += jnp.sum(dy0 * xs[pl.ds(8 - (CONV_WIDTH - 1) + j, TOK_TILE), cols],
                                                 axis=0, keepdims=True)
            return carry

        for third, (cur, nxt) in enumerate(((dq_ref, dqn_ref), (dk_ref, dkn_ref), (dv_ref, dvn_ref))):
            lax.fori_loop(0, GROUP_W // 128, functools.partial(lane_block, third=third, cur=cur, nxt=nxt), 0)

    prev8 = lambda col: pl.BlockSpec((8, wd), lambda i: (jnp.maximum(i * hb - 1, 0), col))
    next8 = lambda col: pl.BlockSpec((8, wd), lambda i: (jnp.minimum((i + 1) * hb, s // 8 - 1), col))
    next8_third = pl.BlockSpec((8, GROUP_W), lambda i: (jnp.minimum((i + 1) * hb, s // 8 - 1), 0))
    out = pl.pallas_call(
        _ride(body, 11, 2, xchg, (nt,)), name="delta_conv_bwd", grid=(nt,),
        in_specs=[_tok_spec(wd, 1), prev8(1), _tok_spec(wd, 0), next8(0)] + [_tok_spec(GROUP_W, 0)] * 3
        + [next8_third] * 3
        + [pl.BlockSpec((CONV_WIDTH, wd), lambda i: (0, 0))] + [_ANY] * xchg.n,
        out_specs=[_tok_spec(wd, 0), pl.BlockSpec((CONV_WIDTH, wd), lambda i: (0, 0))] + [_ANY] * xchg.n,
        out_shape=[jax.ShapeDtypeStruct((s, wd), bf16), jax.ShapeDtypeStruct((CONV_WIDTH, wd), f32)] + xchg.out_shape(),
        scratch_shapes=[pltpu.VMEM((TOK_TILE + 8, wd), f32), pltpu.VMEM((TOK_TILE + 8, wd), f32)] + xchg.scratch(),
        compiler_params=_cparams(("arbitrary",)),
    )(proj, proj, yc, yc, *ds3, *ds3, conv_w, *xchg.arrs)
    return out[:2], out[2:]


def _delta_prep_fwd(sconv, proj, alog_e, dt_e):
    s = sconv.shape[0]

    def body(sq_ref, sk_ref, ba_ref, al_ref, dt_ref, q_ref, k_ref, b_ref, g_ref):
        qn, kn, beta, g = _prep_fn(sq_ref[...], sk_ref[...], ba_ref[...], al_ref[...], dt_ref[...])
        q_ref[...] = qn
        k_ref[...] = kn
        b_ref[...] = beta
        g_ref[...] = g

    return pl.pallas_call(
        body, name="delta_prep_fwd", grid=(s // TOK_TILE,),
        in_specs=[_tok_spec(GROUP_W, 0), _tok_spec(GROUP_W, 1), _tok_spec(128, BA_BLOCK),
                  _vec_spec(GROUP_W), _vec_spec(GROUP_W)],
        out_specs=[_tok_spec(GROUP_W, 0)] * 4,
        out_shape=[jax.ShapeDtypeStruct((s, GROUP_W), f32)] * 4,
        compiler_params=_cparams(("parallel",)),
    )(sconv, sconv, proj, alog_e, dt_e)


def _delta_prep_bwd(sconv, proj, alog_e, dt_e, dqn, dkn, dbeta, dg, xchg):
    s = sconv.shape[0]
    grid = (s // TOK_TILE,)

    def body(sq_ref, sk_ref, ba_ref, al_ref, dt_ref, dq_ref, dk_ref, db_ref, dg_ref,
             dsq_ref, dsk_ref, dba_ref, dal_ref, ddt_ref):
        @pl.when(pl.program_id(0) == 0)
        def _():
            dal_ref[...] = jnp.zeros_like(dal_ref)
            ddt_ref[...] = jnp.zeros_like(ddt_ref)

        _, vjp = jax.vjp(_prep_fn, sq_ref[...], sk_ref[...], ba_ref[...], al_ref[...], dt_ref[...])
        dsq, dsk, dba, dal, ddt = vjp((dq_ref[...], dk_ref[...], db_ref[...], dg_ref[...]))
        dsq_ref[...] = dsq
        dsk_ref[...] = dsk
        dba_ref[...] = dba.astype(bf16)
        dal_ref[...] += dal
        ddt_ref[...] += ddt

    out = pl.pallas_call(
        _ride(body, 9, 5, xchg, grid), name="delta_prep_bwd", grid=grid,
        in_specs=[_tok_spec(GROUP_W, 0), _tok_spec(GROUP_W, 1), _tok_spec(128, BA_BLOCK),
                  _vec_spec(GROUP_W), _vec_spec(GROUP_W)] + [_tok_spec(GROUP_W, 0)] * 4 + [_ANY] * xchg.n,
        out_specs=[_tok_spec(GROUP_W, 0), _tok_spec(GROUP_W, 0), _tok_spec(128, 0),
                   _acc_spec(GROUP_W), _acc_spec(GROUP_W)] + [_ANY] * xchg.n,
        out_shape=[jax.ShapeDtypeStruct((s, GROUP_W), f32)] * 2 + [jax.ShapeDtypeStruct((s, 128), bf16)]
        + [jax.ShapeDtypeStruct((1, GROUP_W), f32)] * 2 + xchg.out_shape(),
        scratch_shapes=xchg.scratch(),
        compiler_params=_cparams(("arbitrary",)),
    )(sconv, sconv, proj, alog_e, dt_e, dqn, dkn, dbeta, dg, *xchg.arrs)
    return out[:5], out[5:]


def _gnorm_fwd(o, proj, ng_e):
    s = o.shape[0]

    def body(o_ref, z_ref, g_ref, y_ref):
        y_ref[...] = _gnorm_fn(o_ref[...], z_ref[...], g_ref[...])

    return pl.pallas_call(
        body, name="delta_gnorm_fwd", grid=(s // TOK_TILE,),
        in_specs=[_tok_spec(GROUP_W, 0), _tok_spec(GROUP_W, Z_COL // GROUP_W), _vec_spec(GROUP_W)],
        out_specs=_tok_spec(GROUP_W, 0),
        out_shape=jax.ShapeDtypeStruct((s, GROUP_W), f32),
        compiler_params=_cparams(("parallel",)),
    )(o, proj, ng_e)


def _gnorm_bwd(o, proj, ng_e, dycat):
    s = o.shape[0]

    def body(o_ref, z_ref, g_ref, dy_ref, do_ref, dz_ref, dg_ref):
        @pl.when(pl.program_id(0) == 0)
        def _():
            dg_ref[...] = jnp.zeros_like(dg_ref)

        _, vjp = jax.vjp(_gnorm_fn, o_ref[...], z_ref[...], g_ref[...])
        do, dz, dg = vjp(dy_ref[...])
        do_ref[...] = do
        dz_ref[...] = dz.astype(bf16)
        dg_ref[...] += dg

    return pl.pallas_call(
        body, name="delta_gnorm_bwd", grid=(s // TOK_TILE,),
        in_specs=[_tok_spec(GROUP_W, 0), _tok_spec(GROUP_W, Z_COL // GROUP_W), _vec_spec(GROUP_W),
                  _tok_spec(GROUP_W, 1)],
        out_specs=[_tok_spec(GROUP_W, 0), _tok_spec(GROUP_W, 0), _acc_spec(GROUP_W)],
        out_shape=[jax.ShapeDtypeStruct((s, GROUP_W), f32), jax.ShapeDtypeStruct((s, GROUP_W), bf16),
                   jax.ShapeDtypeStruct((1, GROUP_W), f32)],
        compiler_params=_cparams(("arbitrary",)),
    )(o, proj, ng_e, dycat)


_MESH = pl.DeviceIdType.MESH
_ANY = pl.BlockSpec(memory_space=pl.ANY)
_VMEM = pl.BlockSpec(memory_space=pltpu.VMEM)


def _my_place():
    x, y, c = lax.axis_index("x"), lax.axis_index("y"), lax.axis_index("c")
    return x, y, c, 4 * x + 2 * y + c


def _peer(k, x, y, c):
    px = 1 - x if k & 4 else x
    py = 1 - y if k & 2 else y
    pc = 1 - c if k & 1 else c
    return (px, py, pc), 4 * px + 2 * py + pc


def _exchange_all(src_of_peer, dst_ref, send_sems, recv_sems, x, y, c, me):
    sent = []
    for k in range(1, N_DEV):
        dev, pidx = _peer(k, x, y, c)
        cp = pltpu.make_async_remote_copy(src_ref=src_of_peer(pidx), dst_ref=dst_ref.at[me],
                                          send_sem=send_sems.at[k - 1], recv_sem=recv_sems.at[k - 1],
                                          device_id=dev, device_id_type=_MESH)
        cp.start()
        sent.append(cp)
    for k in range(1, N_DEV):
        dev, pidx = _peer(k, x, y, c)
        pltpu.make_async_remote_copy(src_ref=src_of_peer(pidx), dst_ref=dst_ref.at[pidx],
                                     send_sem=send_sems.at[k - 1], recv_sem=recv_sems.at[k - 1],
                                     device_id=dev, device_id_type=_MESH).wait_recv()
    for cp in sent:
        cp.wait_send()


def _ada_exchange(cv8, w_ada, b_ada8):
    def body(cv_ref, w_ref, b_ref, call_ref, modp_ref, part_s, s1, r1, s2, r2):
        x, y, c, me = _my_place()
        call_ref[me] = cv_ref[...]
        _exchange_all(lambda pidx: cv_ref, call_ref, s1, r1, x, y, c, me)
        bias = b_ref[me]
        for j in range(N_DEV):
            cj = call_ref[j][:, :D_MODEL]
            part_s[j] = _hdot(cj * jax.nn.sigmoid(cj), w_ref[...]) + bias
        modp_ref[me] = part_s[me]
        _exchange_all(lambda pidx: part_s.at[pidx], modp_ref, s2, r2, x, y, c, me)

    nsh = w_ada.shape[1]
    return pl.pallas_call(
        body, name="ada_exchange",
        in_specs=[_VMEM, _VMEM, _VMEM], out_specs=[_VMEM, _VMEM],
        out_shape=[jax.ShapeDtypeStruct((N_DEV, 8, cv8.shape[1]), f32), jax.ShapeDtypeStruct((N_DEV, 8, nsh), f32)],
        scratch_shapes=[pltpu.VMEM((N_DEV, 8, nsh), f32)] + [pltpu.SemaphoreType.DMA((N_DEV - 1,))] * 4,
        compiler_params=pltpu.CompilerParams(vmem_limit_bytes=VMEM_LIMIT),
    )(cv8, w_ada, b_ada8)


def _all_to_all(arrs, name):
    ex = _Exchange(arrs, gather=False)

    def body(*refs):
        srcs, dsts, sems = refs[:ex.n], refs[ex.n:2 * ex.n], refs[2 * ex.n:]
        ex.start(srcs, dsts, sems)
        ex.wait(srcs, dsts, sems)

    return pl.pallas_call(
        body, name=name,
        in_specs=[_ANY] * ex.n, out_specs=[_ANY] * ex.n,
        out_shape=ex.out_shape(), scratch_shapes=ex.scratch(),
    )(*arrs)


class _Exchange:
    def __init__(self, arrs, gather):
        self.arrs, self.gather, self.n = list(arrs), gather, len(arrs)

    def out_shape(self):
        return [jax.ShapeDtypeStruct(((N_DEV,) + a.shape) if self.gather else a.shape, a.dtype) for a in self.arrs]

    def scratch(self):
        if self.n == 0:
            return []
        return [pltpu.SemaphoreType.DMA((self.n, N_DEV - 1)), pltpu.SemaphoreType.DMA((self.n, N_DEV - 1)),
                pltpu.SemaphoreType.DMA((self.n,))]

    def _src(self, srcs, a, idx):
        return srcs[a] if self.gather else srcs[a].at[idx]

    def _copies(self, srcs, dsts, sems, incoming):
        send_sems, recv_sems, _ = sems
        x, y, c, me = _my_place()
        out = []
        for a in range(self.n):
            for k in range(1, N_DEV):
                dev, pidx = _peer(k, x, y, c)
                out.append(pltpu.make_async_remote_copy(
                    src_ref=self._src(srcs, a, pidx), dst_ref=dsts[a].at[pidx if incoming else me],
                    send_sem=send_sems.at[a, k - 1], recv_sem=recv_sems.at[a, k - 1],
                    device_id=dev, device_id_type=_MESH))
        return out

    def _local(self, srcs, dsts, sems):
        me = _my_place()[3]
        return [pltpu.make_async_copy(self._src(srcs, a, me), dsts[a].at[me], sems[2].at[a]) for a in range(self.n)]

    def start(self, srcs, dsts, sems):
        for cp in self._local(srcs, dsts, sems) + self._copies(srcs, dsts, sems, incoming=False):
            cp.start()

    def wait(self, srcs, dsts, sems):
        for cp in self._copies(srcs, dsts, sems, incoming=True):
            cp.wait_recv()
        for cp in self._copies(srcs, dsts, sems, incoming=False):
            cp.wait_send()
        for cp in self._local(srcs, dsts, sems):
            cp.wait()

    def start_at_first_step(self, grid, srcs, dsts, sems):
        first = functools.reduce(jnp.logical_and, [pl.program_id(i) == 0 for i in range(len(grid))])
        pl.when(first)(lambda: self.start(srcs, dsts, sems))

    def wait_at_last_step(self, grid, srcs, dsts, sems):
        last = functools.reduce(jnp.logical_and, [pl.program_id(i) == g - 1 for i, g in enumerate(grid)])
        pl.when(last)(lambda: self.wait(srcs, dsts, sems))


class _ChipGather:
    def __init__(self, shard):
        self.shard = shard

    def out_shape(self):
        return jax.ShapeDtypeStruct((N_DEV,) + self.shard.shape, self.shard.dtype)

    def scratch(self):
        return [pltpu.SemaphoreType.DMA((N_DEV - 1,)), pltpu.SemaphoreType.DMA((N_DEV - 1,)),
                pltpu.SemaphoreType.DMA(())]

    def _place(self):
        x, y, c, me = _my_place()
        return x, y, c, me, (x, y, 1 - c), [(1 - x, y), (x, 1 - y), (1 - x, 1 - y)]

    def _copy(self, out, sems, k, block, to, src=None):
        rows = out.at[4 * block[0] + 2 * block[1] + block[2]]
        return pltpu.make_async_remote_copy(src_ref=rows if src is None else src, dst_ref=rows,
                                            send_sem=sems[0].at[k], recv_sem=sems[1].at[k],
                                            device_id=to, device_id_type=_MESH)

    def start(self, src, out, sems):
        x, y, c, me, sib, chips = self._place()
        pltpu.make_async_copy(src, out.at[me], sems[2]).start()
        self._copy(out, sems, 0, (x, y, c), sib, src=src).start()
        for j, chip in enumerate(chips):
            self._copy(out, sems, 1 + j, (x, y, c), (*chip, c), src=src).start()

    def forward(self, src, out, sems):
        x, y, c, me, sib, chips = self._place()
        for j, chip in enumerate(chips):
            self._copy(out, sems, 1 + j, (*chip, c), (x, y, c)).wait_recv()
            self._copy(out, sems, 4 + j, (*chip, c), sib).start()

    def finish(self, src, out, sems):
        x, y, c, me, sib, chips = self._place()
        self._copy(out, sems, 0, (x, y, 1 - c), (x, y, c)).wait_recv()
        for j, chip in enumerate(chips):
            self._copy(out, sems, 4 + j, (*chip, 1 - c), (x, y, c)).wait_recv()
        self._copy(out, sems, 0, (x, y, c), sib, src=src).wait_send()
        for j, chip in enumerate(chips):
            self._copy(out, sems, 1 + j, (x, y, c), (*chip, c), src=src).wait_send()
            self._copy(out, sems, 4 + j, (*chip, c), sib).wait_send()
        pltpu.make_async_copy(src, out.at[me], sems[2]).wait()


def _ride_gathers(body, n_in, n_out, rides, grid, forward_step):
    n = len(rides)
    sizes = list(grid)

    def wrapped(*refs):
        ins, xs = refs[:n_in], refs[n_in:n_in + n]
        outs, xd = refs[n_in + n:n_in + n + n_out], refs[n_in + n + n_out:n_in + 2 * n + n_out]
        scratch = refs[n_in + 2 * n + n_out:]
        own, sems = scratch[:len(scratch) - 3 * n], scratch[len(scratch) - 3 * n:]
        step = pl.program_id(0)
        for i in range(1, len(sizes)):
            step = step * sizes[i] + pl.program_id(i)

        def each(phase):
            for r in range(n):
                getattr(rides[r], phase)(xs[r], xd[r], sems[3 * r:3 * r + 3])

        pl.when(step == 0)(lambda: each("start"))
        body(*ins, *outs, *own)
        pl.when(step == forward_step)(lambda: each("forward"))
        pl.when(step == math.prod(sizes) - 1)(lambda: each("finish"))

    return wrapped


def _ride(body, n_in, n_out, xchg, grid):
    nx = xchg.n
    if nx == 0:
        return body

    def wrapped(*refs):
        ins, xs = refs[:n_in], refs[n_in:n_in + nx]
        outs, xd = refs[n_in + nx:n_in + nx + n_out], refs[n_in + nx + n_out:n_in + 2 * nx + n_out]
        scratch = refs[n_in + 2 * nx + n_out:]
        xchg.start_at_first_step(grid, xs, xd, scratch[-3:])
        body(*ins, *outs, *scratch[:-3])
        xchg.wait_at_last_step(grid, xs, xd, scratch[-3:])

    return wrapped


def _adamw_math(w, g, m, v):
    m2 = ADAM_B1 * m + (1.0 - ADAM_B1) * g
    v2 = ADAM_B2 * v + (1.0 - ADAM_B2) * (g * g)
    m_hat = m2 / (1.0 - ADAM_B1 ** ADAM_STEP)
    v_hat = v2 / (1.0 - ADAM_B2 ** ADAM_STEP)
    delta = -ADAM_LR * (m_hat / (jnp.sqrt(v_hat) + ADAM_EPS) + ADAM_WD * w)
    return delta, m2, v2


def _row_tile(rows):
    for t in (256, 128, 64, 32, 16, 8):
        if rows % t == 0:
            return t
    return rows


def _reduce_adamw(parts, w, m, v, name):
    _, r, cdim = parts.shape
    tr = _row_tile(r)

    def body(p_ref, w_ref, m_ref, v_ref, g_ref, d_ref, m2_ref, v2_ref):
        g = p_ref[0].astype(f32)
        for j in range(1, N_DEV):
            g = g + p_ref[j].astype(f32)
        delta, m2, v2 = _adamw_math(w_ref[...], g, m_ref[...], v_ref[...])
        g_ref[...] = g
        d_ref[...] = delta
        m2_ref[...] = m2
        v2_ref[...] = v2

    spec = pl.BlockSpec((tr, cdim), lambda i: (i, 0))
    return pl.pallas_call(
        body, name=name, grid=(r // tr,),
        in_specs=[pl.BlockSpec((N_DEV, tr, cdim), lambda i: (0, i, 0)), spec, spec, spec],
        out_specs=[spec] * 4,
        out_shape=[jax.ShapeDtypeStruct((r, cdim), f32)] * 4,
        compiler_params=_cparams(("parallel",)),
    )(parts, w, m, v)


def _adamw(w, g, m, v, name):
    r, cdim = w.shape
    tr = _row_tile(r)

    def body(w_ref, g_ref, m_ref, v_ref, d_ref, m2_ref, v2_ref):
        delta, m2, v2 = _adamw_math(w_ref[...], g_ref[...], m_ref[...], v_ref[...])
        d_ref[...] = delta
        m2_ref[...] = m2
        v2_ref[...] = v2

    spec = pl.BlockSpec((tr, cdim), lambda i: (i, 0))
    return pl.pallas_call(
        body, name=name, grid=(r // tr,),
        in_specs=[spec] * 4, out_specs=[spec] * 3,
        out_shape=[jax.ShapeDtypeStruct((r, cdim), f32)] * 3,
        compiler_params=_cparams(("parallel",)),
    )(w, g, m, v)


def _sum_devices(parts, name):
    _, r, cdim = parts.shape

    def body(p_ref, o_ref):
        g = p_ref[0]
        for j in range(1, N_DEV):
            g = g + p_ref[j]
        o_ref[...] = g

    return pl.pallas_call(
        body, name=name, out_shape=jax.ShapeDtypeStruct((r, cdim), f32),
        in_specs=[_VMEM], out_specs=_VMEM,
    )(parts)


def _ada_wgrad(c_all8, dmod_cols):
    nsh = dmod_cols.shape[1]

    def body(c_ref, d_ref, o_ref):
        cv = c_ref[...]
        o_ref[...] = lax.dot_general(cv * jax.nn.sigmoid(cv), d_ref[...], _TN, precision=_HI,
                                     preferred_element_type=f32)

    return pl.pallas_call(
        body, name="ada_wgrad", out_shape=jax.ShapeDtypeStruct((D_MODEL, nsh), f32),
        in_specs=[_VMEM, _VMEM], out_specs=_VMEM,
        compiler_params=pltpu.CompilerParams(vmem_limit_bytes=VMEM_LIMIT),
    )(c_all8, dmod_cols)


def _cols(t):
    return t.transpose(1, 0, 2).reshape(t.shape[1], N_DEV * t.shape[2])


def _col_blocks(t, n):
    return t.reshape(t.shape[0], N_DEV, n).transpose(1, 0, 2).astype(bf16)


def _row_blocks(t):
    return t.reshape(N_DEV, t.shape[0] // N_DEV, t.shape[1]).astype(bf16)


def _local_step(x, tgt, mod, norm_attn_g, w_in_sh, rel_bias, conv_full, a_log, dt_bias, delta_norm_g,
                norm_ffn_g, final_norm_g, w_out_sh, w_gate_sh, w_up_sh, w_down_sh):
    s = x.shape[0]
    sh1, sc1, g1, sh2, sc2, g2 = [mod[:, i * D_MODEL:(i + 1) * D_MODEL] for i in range(6)]
    nag = norm_attn_g.reshape(1, D_MODEL)
    nfg = norm_ffn_g.reshape(1, D_MODEL)
    fg = final_norm_g.reshape(1, D_MODEL)
    idx = _bucket_tables()
    bias = _bias_tables(rel_bias, idx)
    alog_e = jnp.repeat(a_log.reshape(N_HEADS), HEAD_DIM)[None]
    dt_e = jnp.repeat(dt_bias.reshape(N_HEADS), HEAD_DIM)[None]
    ng_e = jnp.tile(delta_norm_g.reshape(HEAD_DIM), N_HEADS)[None]

    h1, w_in_g = _ln_mod_fwd(x, nag, sc1, sh1, w_in_sh, "ln1_fwd")
    w_in_p = jnp.pad(_cols(w_in_g), ((0, 0), (0, IN_PAD - IN_WIDTH)))
    proj, (w_out_g,) = _mm(h1, w_in_p, "nn", f32, 512, IN_PAD, 1024, "in_proj",
                           xchg=_Exchange([w_out_sh], gather=True))
    (y_attn, lse), (w_gate_g, w_up_g, w_down_g) = _attn_fwd(proj, bias, [w_gate_sh, w_up_sh, w_down_sh])
    w_out_b = w_out_g.reshape(2 * GROUP_W, D_MODEL)
    w_down_b = w_down_g.reshape(D_FF, D_MODEL)
    w_gate_b, w_up_b = _cols(w_gate_g), _cols(w_up_g)
    n_ff = w_gate_sh.shape[1]
    sconv, yconv = _conv_silu_fwd(proj, conv_full)
    qn, kn, beta, g = _delta_prep_fwd(sconv, proj, alog_e, dt_e)
    (u, w, qt, kh, qk, tinv, gm), _ = _delta_chunk_pre(qn, kn, sconv, beta, g, _Exchange([], gather=False))
    o, ss = _delta_scan_fwd(u, w, qt, kh, qk, gm)
    y_delta = _gnorm_fwd(o, proj, ng_e)
    y, x1, h2 = _proj_resid_ln_mod_fwd([(y_attn, w_out_b[:GROUP_W]), (y_delta, w_out_b[GROUP_W:])],
                                       x, g1, nfg, sc2, sh2, "out_proj_ln2")
    act, gate, up = _ffn_up(h2, w_gate_b, w_up_b, "ffn_up")
    dx2, dy2, loss, dfg, dg2 = _proj_final_loss_bwd(act, w_down_b, x1, g2, fg, tgt, "ffn_down_loss")

    dgate, dup = _ffn_down_dx(dy2, w_down_b, gate, up, "ffn_down_dx")
    g_down = _mm(act, dy2, "tn", f32, 1408, 1024, 2048, "ffn_down_dw")
    (dx1, dsh2, dsc2, dnfg, dy, dg1), (r_down,) = _proj_ln_mod_bwd(
        [(dgate, w_gate_b), (dup, w_up_b)], x1, nfg, sc2, dx2, 256, "ffn_up_dx_ln2",
        _Exchange([_row_blocks(g_down)], gather=False), gate=g1, y=y)
    g_gate = _mm(h2, dgate, "tn", f32, 1024, 1408, 2048, "ffn_gate_dw")
    g_up = _mm(h2, dup, "tn", f32, 1024, 1408, 2048, "ffn_up_dw")
    dycat = _mm(dy, w_out_b, "nt", f32, 512, 1024, 1024, "out_proj_dx")
    g_out = jnp.concatenate([_mm(y_attn, dy, "tn", f32, GROUP_W, 1024, 2048, "out_proj_dw_attn"),
                             _mm(y_delta, dy, "tn", f32, GROUP_W, 1024, 2048, "out_proj_dw_delta")], axis=0)
    dq, dk, dv, dbias = _attn_bwd(proj, bias, y_attn, lse, dycat)
    g_rb = _bias_grad(dbias, idx)[:, :, 0].T
    do, dz, dng = _gnorm_bwd(o, proj, ng_e, dycat)
    dso, _ = _delta_scan_bwd(w, qt, kh, qk, gm, do, _Exchange([], gather=False))
    (dqn, dkn, dvd, dbeta, dgd), (r_up,) = _delta_chunk_bwd(
        qn, kn, sconv, beta, g, tinv, ss, dso, do, _Exchange([_col_blocks(g_up, n_ff)], gather=False))
    (dsq, dsk, dba, dal, ddt), _ = _delta_prep_bwd(
        sconv, proj, alog_e, dt_e, dqn, dkn, dbeta, dgd, _Exchange([], gather=False))
    (dxc, g_conv), (r_gate, r_out) = _conv_silu_bwd(
        proj, conv_full, yconv, (dsq, dsk, dvd),
        _Exchange([_col_blocks(g_gate, n_ff), _row_blocks(g_out)], gather=False))
    pieces = ((dq, 0), (dk, GROUP_W), (dv, 2 * GROUP_W), (dxc, DELTA_COL), (dz, Z_COL), (dba, BA_BLOCK * 128))
    g_in = jnp.concatenate(
        [_mm(h1, p, "tn", f32, 1024, min(p.shape[1], 768), 2048, "in_proj_dw_%d" % c) for p, c in pieces], axis=1)
    (gx, dsh1, dsc1, dnag), (r_in,) = _proj_ln_mod_bwd(
        [(p, w_in_p[:, c:c + p.shape[1]]) for p, c in pieces], x, nag, sc1, dx1, TOK_TILE, "in_proj_dx_ln1",
        _Exchange([_col_blocks(g_in[:, :IN_WIDTH], IN_WIDTH // N_DEV)], gather=False))
    grads = dict(
        x=gx, mod=jnp.concatenate([dsh1, dsc1, dg1, dsh2, dsc2, dg2], axis=1),
        norm_attn_g=dnag, norm_ffn_g=dnfg, final_norm_g=dfg, rel_bias=g_rb, conv_w=g_conv,
        a_log=dal.reshape(N_HEADS, HEAD_DIM).sum(-1), dt_bias=ddt.reshape(N_HEADS, HEAD_DIM).sum(-1),
        delta_norm_g=dng.reshape(N_HEADS, HEAD_DIM).sum(0),
        w_in=r_in, w_out=r_out, w_gate=r_gate, w_up=r_up, w_down=r_down)
    return loss[0, 0], grads


def _misc_row(rel_bias, a_log, dt_bias, delta_norm_g):
    flat = jnp.concatenate([rel_bias.reshape(-1), a_log.reshape(-1), dt_bias.reshape(-1), delta_norm_g.reshape(-1)])
    return jnp.pad(flat, (0, D_MODEL - flat.shape[0]))[None]


def _pack_small(b_ada, nag, nfg, fng, rel_bias, a_log, dt_bias, dng, conv_shard):
    rows = [b_ada.reshape(6, D_MODEL), nag.reshape(1, D_MODEL), nfg.reshape(1, D_MODEL), fng.reshape(1, D_MODEL),
            _misc_row(rel_bias, a_log, dt_bias, dng),
            jnp.pad(conv_shard.reshape(-1), (0, D_MODEL - conv_shard.size))[None],
            jnp.zeros((5, D_MODEL), f32)]
    return jnp.concatenate(rows, axis=0)


def _unpack_small(p, conv_shape):
    misc = p[9]
    return dict(
        b_ada=p[0:6].reshape(1, 6 * D_MODEL), norm_attn_g=p[6:7], norm_ffn_g=p[7:8], final_norm_g=p[8],
        rel_bias=misc[0:256].reshape(N_BUCKETS, N_HEADS), a_log=misc[256:264].reshape(1, N_HEADS),
        dt_bias=misc[264:272].reshape(1, N_HEADS), delta_norm_g=misc[272:336].reshape(1, HEAD_DIM),
        conv_w=p[10, :conv_shape[1] * conv_shape[2]].reshape(conv_shape))


def kernel(x, c, w_ada, b_ada, norm_attn_g, w_in, rel_bias, conv_w, a_log, dt_bias, delta_norm_g, w_out, norm_ffn_g, w_gate, w_up, w_down, final_norm_g, loss_target, m_w_ada, m_b_ada, m_norm_attn_g, m_w_in, m_rel_bias, m_conv_w, m_a_log, m_dt_bias, m_delta_norm_g, m_w_out, m_norm_ffn_g, m_w_gate, m_w_up, m_w_down, m_final_norm_g, v_w_ada, v_b_ada, v_norm_attn_g, v_w_in, v_rel_bias, v_conv_w, v_a_log, v_dt_bias, v_delta_norm_g, v_w_out, v_norm_ffn_g, v_w_gate, v_w_up, v_w_down, v_final_norm_g):
    me = 4 * lax.axis_index("x") + 2 * lax.axis_index("y") + lax.axis_index("c")
    ada_sh = w_ada.shape[2]
    conv_sh = conv_w.shape[2]

    cv = jnp.concatenate([c[0], conv_w[0].reshape(-1)])
    cv8 = jnp.zeros((8, 2 * D_MODEL), f32).at[0, :cv.shape[0]].set(cv)
    b8 = jnp.broadcast_to(b_ada.reshape(N_DEV, 1, ada_sh), (N_DEV, 8, ada_sh))
    call, modp = _ada_exchange(cv8, w_ada[0], b8)
    mod = modp[:, 0, :].reshape(1, 6 * D_MODEL)
    c_all = call[:, 0, :D_MODEL]
    conv_full = call[:, 0, D_MODEL:D_MODEL + CONV_WIDTH * conv_sh].reshape(N_DEV, CONV_WIDTH, conv_sh)
    conv_full = conv_full.transpose(1, 0, 2).reshape(CONV_WIDTH, N_DEV * conv_sh)

    loss_local, gr = _local_step(x[0], loss_target[0], mod, norm_attn_g, w_in[0].astype(bf16), rel_bias, conv_full, a_log,
                                 dt_bias, delta_norm_g, norm_ffn_g, final_norm_g, w_out[0].astype(bf16),
                                 w_gate[0].astype(bf16), w_up[0].astype(bf16), w_down[0].astype(bf16))
    loss = lax.psum(loss_local, ("x", "y", "c"))

    small = jnp.concatenate([
        gr["mod"].reshape(6, D_MODEL), gr["norm_attn_g"], gr["norm_ffn_g"], gr["final_norm_g"],
        gr["conv_w"].reshape(6, D_MODEL),
        _misc_row(gr["rel_bias"], gr["a_log"], gr["dt_bias"], gr["delta_norm_g"])], axis=0)
    parts = _all_to_all([jnp.broadcast_to(small[None], (N_DEV,) + small.shape)], "small_gather")[0]
    tot = _sum_devices(parts, "small_sum")
    g_conv_full = tot[9:15].reshape(CONV_WIDTH, N_DEV * conv_sh)
    g_conv = lax.dynamic_slice(g_conv_full, (0, me * conv_sh), (CONV_WIDTH, conv_sh))
    misc = tot[15]
    g_small = _pack_small(tot[0:6], tot[6], tot[7], tot[8], misc[0:256], misc[256:264], misc[264:272],
                          misc[272:336], g_conv)
    pk = lambda pre: _pack_small(pre[0], pre[1], pre[2], pre[3], pre[4], pre[5], pre[6], pre[7], pre[8])
    w_small = pk((b_ada, norm_attn_g, norm_ffn_g, final_norm_g, rel_bias, a_log, dt_bias, delta_norm_g, conv_w))
    m_small = pk((m_b_ada, m_norm_attn_g, m_norm_ffn_g, m_final_norm_g, m_rel_bias, m_a_log, m_dt_bias,
                  m_delta_norm_g, m_conv_w))
    v_small = pk((v_b_ada, v_norm_attn_g, v_norm_ffn_g, v_final_norm_g, v_rel_bias, v_a_log, v_dt_bias,
                  v_delta_norm_g, v_conv_w))
    d_small, m2_small, v2_small = _adamw(w_small, g_small, m_small, v_small, "adamw_small")
    cshape = conv_w.shape
    G, Dl, M2, V2 = (_unpack_small(t, cshape) for t in (g_small, d_small, m2_small, v2_small))

    dmod_all = parts[:, 0:6, :].reshape(N_DEV, 6 * D_MODEL)
    dmod_cols = lax.dynamic_slice(dmod_all, (0, me * ada_sh), (N_DEV, ada_sh))
    g_ada = _ada_wgrad(c_all, dmod_cols)
    d_ada, m2_ada, v2_ada = _adamw(w_ada[0], g_ada, m_w_ada[0], v_w_ada[0], "adamw_w_ada")

    big = {}
    for name, w_, m_, v_ in (("w_in", w_in, m_w_in, v_w_in), ("w_out", w_out, m_w_out, v_w_out),
                             ("w_gate", w_gate, m_w_gate, v_w_gate), ("w_up", w_up, m_w_up, v_w_up),
                             ("w_down", w_down, m_w_down, v_w_down)):
        big[name] = [t[None] for t in _reduce_adamw(gr[name], w_[0], m_[0], v_[0], "reduce_adamw_" + name)]

    def leaf(i, name):
        if name == "w_ada":
            return (g_ada, d_ada, m2_ada, v2_ada)[i][None]
        if name in big:
            return big[name][i]
        return (G, Dl, M2, V2)[i][name]

    order = ["w_ada", "b_ada", "norm_attn_g", "w_in", "rel_bias", "conv_w", "a_log", "dt_bias", "delta_norm_g",
             "w_out", "norm_ffn_g", "w_gate", "w_up", "w_down", "final_norm_g"]
    outs = [loss, gr["x"][None]]
    for i in range(4):
        outs += [leaf(i, n) for n in order]
    return tuple(outs)
```
